```python
import jax, jax.numpy as jnp
from jax import lax
import numpy as np

D_MODEL = 1024
BATCH = 8
SEQ = 4096
DEPTH = 2

N_A_LAYERS = DEPTH // 2
N_B_LAYERS = DEPTH - N_A_LAYERS

HEAD_DIM = 128
A_HEADS = 6
A_WIDTH = A_HEADS * HEAD_DIM
CHUNK = 64

B_HEADS = 6
B_WIDTH = B_HEADS * HEAD_DIM
DILATED_GROUPS = ((128, 1), (512, 4), (2048, 16))
N_GROUPS = len(DILATED_GROUPS)
ROPE_THETA = 10000.0

MEM_TOKENS = 256
MEM_HEADS = 4
MEM_HEAD_DIM = 64
MEM_WIDTH = MEM_HEADS * MEM_HEAD_DIM

MIX_WIDTH = A_WIDTH + MEM_WIDTH
A_COLS = 4 * A_WIDTH + MEM_WIDTH
B_COLS = N_GROUPS * B_WIDTH + MEM_WIDTH
FFN_HIDDEN = ((-(-8 * D_MODEL // 3)) + 255) // 256 * 256
EPS = 1e-6

kernel_name = "yoco_hgrn2_dilated_attn_memory_hybrid"


def rms_norm(x, g):
    xf = x.astype(jnp.float32)
    y = xf * lax.rsqrt(jnp.mean(xf * xf, axis=-1, keepdims=True) + EPS)
    return (y * g.astype(jnp.float32)).astype(x.dtype)


def rope(x, pos):
    dh = x.shape[-1]
    half = dh // 2
    inv = ROPE_THETA ** (-jnp.arange(half, dtype=jnp.float32) / half)
    ang = pos.astype(jnp.float32)[:, None] * inv[None, :]
    cos = jnp.cos(ang)[None, :, None, :]
    sin = jnp.sin(ang)[None, :, None, :]
    xf = x.astype(jnp.float32)
    x1, x2 = xf[..., :half], xf[..., half:]
    out = jnp.concatenate([x1 * cos - x2 * sin, x2 * cos + x1 * sin], axis=-1)
    return out.astype(x.dtype)


def hgrn2_chunkwise(q, f_logit, i, lb):
    Bn, S, H, dk = q.shape
    nC = S // CHUNK
    f = lb + (1.0 - lb) * jax.nn.sigmoid(f_logit.astype(jnp.float32))
    k = 1.0 - f
    logf = jnp.log(f)
    qf = jax.nn.silu(q.astype(jnp.float32))
    vf = i.astype(jnp.float32)

    def chunks(t):
        return t.reshape(Bn, nC, CHUNK, H, t.shape[-1]).transpose(1, 0, 3, 2, 4)

    qc, kc, vc, gc = chunks(qf), chunks(k), chunks(vf), chunks(logf)
    b = jnp.cumsum(gc, axis=3)
    b_end = b[:, :, :, -1:, :]
    q_in = qc * jnp.exp(b)
    k_in = kc * jnp.exp(-b)
    k_out = kc * jnp.exp(b_end - b)
    causal = jnp.tril(jnp.ones((CHUNK, CHUNK), dtype=bool))
    att = jnp.where(causal, jnp.einsum('nbhqd,nbhkd->nbhqk', q_in, k_in), 0.0)
    o_intra = jnp.einsum('nbhqk,nbhke->nbhqe', att, vc)
    decay = jnp.exp(b_end[:, :, :, 0, :])

    def step(state, xs):
        q_n, k_n, v_n, dec = xs
        o_n = jnp.einsum('bhqd,bhde->bhqe', q_n, state)
        state = dec[..., None] * state + jnp.einsum('bhkd,bhke->bhde', k_n, v_n)
        return state, o_n

    s0 = jnp.zeros((Bn, H, dk, vf.shape[-1]), jnp.float32)
    _, o_inter = lax.scan(step, s0, (q_in, k_out, vc, decay))
    o = o_intra + o_inter
    return o.transpose(1, 0, 3, 2, 4).reshape(Bn, S, H, vf.shape[-1])


def dilated_branch(q, k, v, window, dilation):
    Bn, S, H, dh = q.shape
    span = window // dilation
    blk = span
    L = S // dilation
    nb = -(-L // blk)
    Lp = nb * blk

    def by_residue(t):
        t = t.reshape(Bn, L, dilation, H, dh).transpose(0, 2, 3, 1, 4)
        return jnp.pad(t, ((0, 0), (0, 0), (0, 0), (0, Lp - L), (0, 0)))

    def band(t):
        tp = jnp.pad(t, ((0, 0), (0, 0), (0, 0), (blk, 0), (0, 0)))
        prev = tp[:, :, :, :Lp].reshape(Bn, dilation, H, nb, blk, dh)
        cur = tp[:, :, :, blk:].reshape(Bn, dilation, H, nb, blk, dh)
        return jnp.concatenate([prev, cur], axis=4)

    qb = by_residue(q).reshape(Bn, dilation, H, nb, blk, dh)
    kb = band(by_residue(k))
    vb = band(by_residue(v))
    s = jnp.einsum('brhnqd,brhnkd->brhnqk', qb, kb,
                   preferred_element_type=jnp.float32) * (dh ** -0.5)
    qpos = jnp.arange(nb)[:, None, None] * blk + jnp.arange(blk)[None, :, None]
    kpos = (jnp.arange(nb)[:, None, None] - 1) * blk + jnp.arange(2 * blk)[None, None, :]
    dist = qpos - kpos
    mask = (dist >= 0) & (dist <= span) & (kpos >= 0)
    s = jnp.where(mask, s, -jnp.inf)
    lse = jax.nn.logsumexp(s, axis=-1)
    p = jnp.exp(s - lse[..., None])
    o = jnp.einsum('brhnqk,brhnkd->brhnqd', p.astype(v.dtype), vb)
    o = o.reshape(Bn, dilation, H, Lp, dh)[:, :, :, :L].transpose(0, 3, 1, 2, 4).reshape(Bn, S, H, dh)
    lse = lse.reshape(Bn, dilation, H, Lp)[..., :L].transpose(0, 3, 1, 2).reshape(Bn, S, H)
    return o, lse


def memory_attention(q, mk, mv):
    s = jnp.einsum('bshd,bmhd->bhsm', q, mk,
                   preferred_element_type=jnp.float32) * (q.shape[-1] ** -0.5)
    p = jax.nn.softmax(s, axis=-1)
    return jnp.einsum('bhsm,bmhd->bshd', p.astype(mv.dtype), mv)


def _fwd_setup_inputs(seed: int = 0) -> dict:
    key = jax.random.key(seed)
    ks = jax.random.split(key, 24)

    def w(k, shape, fan_in):
        return jax.random.normal(k, shape, jnp.float32) * (fan_in ** -0.5)

    def gain(k, shape):
        return 1.0 + 0.02 * jax.random.normal(k, shape, jnp.float32)

    return {
        "x": jax.random.normal(ks[0], (BATCH, SEQ, D_MODEL), jnp.float32),
        "mem": jax.random.normal(ks[1], (BATCH, MEM_TOKENS, D_MODEL), jnp.float32),
        "norm_mix": gain(ks[2], (DEPTH, D_MODEL)),
        "norm_ffn": gain(ks[3], (DEPTH, D_MODEL)),
        "a_w_in": w(ks[4], (N_A_LAYERS, D_MODEL, A_COLS), D_MODEL),
        "a_lb_logits": 0.1 * jax.random.normal(ks[5], (N_A_LAYERS + 1, A_WIDTH), jnp.float32),
        "a_onorm": gain(ks[6], (N_A_LAYERS, A_WIDTH)),
        "b_w_in": w(ks[7], (N_B_LAYERS, D_MODEL, B_COLS), D_MODEL),
        "b_qnorm": gain(ks[8], (N_B_LAYERS, N_GROUPS, HEAD_DIM)),
        "kv_norm": gain(ks[9], (D_MODEL,)),
        "w_kv": w(ks[10], (D_MODEL, 2 * B_WIDTH), D_MODEL),
        "b_knorm": gain(ks[11], (HEAD_DIM,)),
        "mem_norm": gain(ks[12], (DEPTH, D_MODEL)),
        "w_mem_kv": w(ks[13], (DEPTH, D_MODEL, 2 * MEM_WIDTH), D_MODEL),
        "mem_qnorm": gain(ks[14], (DEPTH, MEM_HEAD_DIM)),
        "mem_knorm": gain(ks[15], (DEPTH, MEM_HEAD_DIM)),
        "w_out": w(ks[16], (DEPTH, MIX_WIDTH, D_MODEL), MIX_WIDTH),
        "w_gate_up": w(ks[17], (DEPTH, D_MODEL, 2 * FFN_HIDDEN), D_MODEL),
        "w_down": w(ks[18], (DEPTH, FFN_HIDDEN, D_MODEL), FFN_HIDDEN),
    }


def _fwd_reference(x, mem, norm_mix, norm_ffn, a_w_in, a_lb_logits, a_onorm, b_w_in, b_qnorm,
              kv_norm, w_kv, b_knorm, mem_norm, w_mem_kv, mem_qnorm, mem_knorm,
              w_out, w_gate_up, w_down):
    Bn, S, _ = x.shape
    pos = jnp.arange(S)
    lb_all = jnp.cumsum(jax.nn.softmax(a_lb_logits.astype(jnp.float32), axis=0), axis=0)
    h = x
    k_sh = None
    v_sh = None
    for l in range(DEPTH):
        xn = rms_norm(h, norm_mix[l])
        mn = rms_norm(mem, mem_norm[l])
        mk, mv = jnp.split(mn @ w_mem_kv[l], 2, axis=-1)
        mk = rms_norm(mk.reshape(Bn, MEM_TOKENS, MEM_HEADS, MEM_HEAD_DIM), mem_knorm[l])
        mv = mv.reshape(Bn, MEM_TOKENS, MEM_HEADS, MEM_HEAD_DIM)
        if l < N_A_LAYERS:
            proj = xn @ a_w_in[l]
            q, f, i, g, mq = jnp.split(proj, [A_WIDTH, 2 * A_WIDTH, 3 * A_WIDTH, 4 * A_WIDTH], axis=-1)
            shp = (Bn, S, A_HEADS, HEAD_DIM)
            o = hgrn2_chunkwise(q.reshape(shp), f.reshape(shp), i.reshape(shp),
                                lb_all[l].reshape(A_HEADS, HEAD_DIM))
            o = rms_norm(o, a_onorm[l].reshape(A_HEADS, HEAD_DIM)) * jax.nn.silu(g.reshape(shp).astype(jnp.float32))
            mix_main = o.reshape(Bn, S, A_WIDTH).astype(h.dtype)
        else:
            j = l - N_A_LAYERS
            proj = xn @ b_w_in[j]
            qs = proj[..., :N_GROUPS * B_WIDTH].reshape(Bn, S, N_GROUPS, B_HEADS, HEAD_DIM)
            mq = proj[..., N_GROUPS * B_WIDTH:]
            outs = []
            lses = []
            for gi, (win, dil) in enumerate(DILATED_GROUPS):
                qg = rope(rms_norm(qs[:, :, gi], b_qnorm[j, gi]), pos)
                o_g, lse_g = dilated_branch(qg, k_sh, v_sh, win, dil)
                outs.append(o_g)
                lses.append(lse_g)
            alpha = jax.nn.softmax(jnp.stack(lses, axis=0), axis=0)
            o = jnp.sum(alpha[..., None] * jnp.stack(outs, axis=0).astype(jnp.float32), axis=0)
            mix_main = o.reshape(Bn, S, B_WIDTH).astype(h.dtype)
        mq = rms_norm(mq.reshape(Bn, S, MEM_HEADS, MEM_HEAD_DIM), mem_qnorm[l])
        mo = memory_attention(mq, mk, mv).reshape(Bn, S, MEM_WIDTH)
        h = h + jnp.concatenate([mix_main, mo.astype(h.dtype)], axis=-1) @ w_out[l]
        gt, up = jnp.split(rms_norm(h, norm_ffn[l]) @ w_gate_up[l], 2, axis=-1)
        h = h + (jax.nn.silu(gt) * up) @ w_down[l]
        if l == N_A_LAYERS - 1:
            k_sh, v_sh = jnp.split(rms_norm(h, kv_norm) @ w_kv, 2, axis=-1)
            k_sh = rope(rms_norm(k_sh.reshape(Bn, S, B_HEADS, HEAD_DIM), b_knorm), pos)
            v_sh = v_sh.reshape(Bn, S, B_HEADS, HEAD_DIM)
    return h


import jax as _jax
import jax.numpy as _jnp

TWIN_FORMAT = 'train_step'
FWD_PARAMS = ['x', 'mem', 'norm_mix', 'norm_ffn', 'a_w_in', 'a_lb_logits', 'a_onorm', 'b_w_in', 'b_qnorm', 'kv_norm', 'w_kv', 'b_knorm', 'mem_norm', 'w_mem_kv', 'mem_qnorm', 'mem_knorm', 'w_out', 'w_gate_up', 'w_down']
TWIN_WEIGHTS = ['norm_mix', 'norm_ffn', 'a_w_in', 'a_lb_logits', 'a_onorm', 'b_w_in', 'b_qnorm', 'kv_norm', 'w_kv', 'b_knorm', 'mem_norm', 'w_mem_kv', 'mem_qnorm', 'mem_knorm', 'w_out', 'w_gate_up', 'w_down']
TWIN_DIFF_INPUT = 'x'
TWIN_INPUTS = ['x', 'mem', 'norm_mix', 'norm_ffn', 'a_w_in', 'a_lb_logits', 'a_onorm', 'b_w_in', 'b_qnorm', 'kv_norm', 'w_kv', 'b_knorm', 'mem_norm', 'w_mem_kv', 'mem_qnorm', 'mem_knorm', 'w_out', 'w_gate_up', 'w_down', 'loss_target', 'm_norm_mix', 'm_norm_ffn', 'm_a_w_in', 'm_a_lb_logits', 'm_a_onorm', 'm_b_w_in', 'm_b_qnorm', 'm_kv_norm', 'm_w_kv', 'm_b_knorm', 'm_mem_norm', 'm_w_mem_kv', 'm_mem_qnorm', 'm_mem_knorm', 'm_w_out', 'm_w_gate_up', 'm_w_down', 'v_norm_mix', 'v_norm_ffn', 'v_a_w_in', 'v_a_lb_logits', 'v_a_onorm', 'v_b_w_in', 'v_b_qnorm', 'v_kv_norm', 'v_w_kv', 'v_b_knorm', 'v_mem_norm', 'v_w_mem_kv', 'v_mem_qnorm', 'v_mem_knorm', 'v_w_out', 'v_w_gate_up', 'v_w_down']
TWIN_OUTPUTS = ['loss', 'grad_x', 'grad_norm_mix', 'grad_norm_ffn', 'grad_a_w_in', 'grad_a_lb_logits', 'grad_a_onorm', 'grad_b_w_in', 'grad_b_qnorm', 'grad_kv_norm', 'grad_w_kv', 'grad_b_knorm', 'grad_mem_norm', 'grad_w_mem_kv', 'grad_mem_qnorm', 'grad_mem_knorm', 'grad_w_out', 'grad_w_gate_up', 'grad_w_down', 'delta_norm_mix', 'delta_norm_ffn', 'delta_a_w_in', 'delta_a_lb_logits', 'delta_a_onorm', 'delta_b_w_in', 'delta_b_qnorm', 'delta_kv_norm', 'delta_w_kv', 'delta_b_knorm', 'delta_mem_norm', 'delta_w_mem_kv', 'delta_mem_qnorm', 'delta_mem_knorm', 'delta_w_out', 'delta_w_gate_up', 'delta_w_down', 'new_m_norm_mix', 'new_m_norm_ffn', 'new_m_a_w_in', 'new_m_a_lb_logits', 'new_m_a_onorm', 'new_m_b_w_in', 'new_m_b_qnorm', 'new_m_kv_norm', 'new_m_w_kv', 'new_m_b_knorm', 'new_m_mem_norm', 'new_m_w_mem_kv', 'new_m_mem_qnorm', 'new_m_mem_knorm', 'new_m_w_out', 'new_m_w_gate_up', 'new_m_w_down', 'new_v_norm_mix', 'new_v_norm_ffn', 'new_v_a_w_in', 'new_v_a_lb_logits', 'new_v_a_onorm', 'new_v_b_w_in', 'new_v_b_qnorm', 'new_v_kv_norm', 'new_v_w_kv', 'new_v_b_knorm', 'new_v_mem_norm', 'new_v_w_mem_kv', 'new_v_mem_qnorm', 'new_v_mem_knorm', 'new_v_w_out', 'new_v_w_gate_up', 'new_v_w_down']
TWIN_LEAF_KINDS = {'loss': 'loss', 'grad_x': 'grad_x', 'grad_norm_mix': 'grad_w', 'grad_norm_ffn': 'grad_w', 'grad_a_w_in': 'grad_w', 'grad_a_lb_logits': 'grad_w', 'grad_a_onorm': 'grad_w', 'grad_b_w_in': 'grad_w', 'grad_b_qnorm': 'grad_w', 'grad_kv_norm': 'grad_w', 'grad_w_kv': 'grad_w', 'grad_b_knorm': 'grad_w', 'grad_mem_norm': 'grad_w', 'grad_w_mem_kv': 'grad_w', 'grad_mem_qnorm': 'grad_w', 'grad_mem_knorm': 'grad_w', 'grad_w_out': 'grad_w', 'grad_w_gate_up': 'grad_w', 'grad_w_down': 'grad_w', 'delta_norm_mix': 'delta_w', 'delta_norm_ffn': 'delta_w', 'delta_a_w_in': 'delta_w', 'delta_a_lb_logits': 'delta_w', 'delta_a_onorm': 'delta_w', 'delta_b_w_in': 'delta_w', 'delta_b_qnorm': 'delta_w', 'delta_kv_norm': 'delta_w', 'delta_w_kv': 'delta_w', 'delta_b_knorm': 'delta_w', 'delta_mem_norm': 'delta_w', 'delta_w_mem_kv': 'delta_w', 'delta_mem_qnorm': 'delta_w', 'delta_mem_knorm': 'delta_w', 'delta_w_out': 'delta_w', 'delta_w_gate_up': 'delta_w', 'delta_w_down': 'delta_w', 'new_m_norm_mix': 'new_m', 'new_m_norm_ffn': 'new_m', 'new_m_a_w_in': 'new_m', 'new_m_a_lb_logits': 'new_m', 'new_m_a_onorm': 'new_m', 'new_m_b_w_in': 'new_m', 'new_m_b_qnorm': 'new_m', 'new_m_kv_norm': 'new_m', 'new_m_w_kv': 'new_m', 'new_m_b_knorm': 'new_m', 'new_m_mem_norm': 'new_m', 'new_m_w_mem_kv': 'new_m', 'new_m_mem_qnorm': 'new_m', 'new_m_mem_knorm': 'new_m', 'new_m_w_out': 'new_m', 'new_m_w_gate_up': 'new_m', 'new_m_w_down': 'new_m', 'new_v_norm_mix': 'new_v', 'new_v_norm_ffn': 'new_v', 'new_v_a_w_in': 'new_v', 'new_v_a_lb_logits': 'new_v', 'new_v_a_onorm': 'new_v', 'new_v_b_w_in': 'new_v', 'new_v_b_qnorm': 'new_v', 'new_v_kv_norm': 'new_v', 'new_v_w_kv': 'new_v', 'new_v_b_knorm': 'new_v', 'new_v_mem_norm': 'new_v', 'new_v_w_mem_kv': 'new_v', 'new_v_mem_qnorm': 'new_v', 'new_v_mem_knorm': 'new_v', 'new_v_w_out': 'new_v', 'new_v_w_gate_up': 'new_v', 'new_v_w_down': 'new_v'}


def _forward(args):
    return _fwd_reference(*[args[k] for k in FWD_PARAMS])


def _output_shape():
    out = _jax.eval_shape(lambda: _forward(_fwd_setup_inputs(0)))
    return out.shape, out.dtype

N_MICROBATCH = 1
ADAM_LR = 0.001
ADAM_B1 = 0.9
ADAM_B2 = 0.999
ADAM_EPS = 1e-08
ADAM_WD = 0.01
ADAM_STEP = 10
PER_EXAMPLE_BATCH_AXIS = {'x': 0, 'mem': 0, 'loss_target': 0}
SHARED_INPUTS = []
_WEIGHT_DTYPES = {'norm_mix': _jnp.float32, 'norm_ffn': _jnp.float32, 'a_w_in': _jnp.float32, 'a_lb_logits': _jnp.float32, 'a_onorm': _jnp.float32, 'b_w_in': _jnp.float32, 'b_qnorm': _jnp.float32, 'kv_norm': _jnp.float32, 'w_kv': _jnp.float32, 'b_knorm': _jnp.float32, 'mem_norm': _jnp.float32, 'w_mem_kv': _jnp.float32, 'mem_qnorm': _jnp.float32, 'mem_knorm': _jnp.float32, 'w_out': _jnp.float32, 'w_gate_up': _jnp.float32, 'w_down': _jnp.float32}
MOMENT_SCALE = {'norm_mix': 6.760676e+00, 'norm_ffn': 2.457627e+01, 'a_w_in': 2.619145e-01, 'a_lb_logits': 3.376495e-02, 'a_onorm': 1.093372e+01, 'b_w_in': 4.053061e-02, 'b_qnorm': 3.141087e-01, 'kv_norm': 2.390886e-01, 'w_kv': 9.239339e-02, 'b_knorm': 8.755364e-01, 'mem_norm': 1.017332e-01, 'w_mem_kv': 1.084973e-01, 'mem_qnorm': 1.187375e+00, 'mem_knorm': 1.193044e+00, 'w_out': 2.345111e-01, 'w_gate_up': 1.772769e-01, 'w_down': 2.835088e-01}


def _to_microbatches(a, axis):
    t = _jnp.moveaxis(a, axis, 0)
    t = t.reshape((N_MICROBATCH, t.shape[0] // N_MICROBATCH) + t.shape[1:])
    return _jnp.moveaxis(t, 1, axis + 1)


def setup_inputs(seed: int = 0) -> dict:
    inp = _fwd_setup_inputs(seed)
    key = _jax.random.fold_in(_jax.random.key(seed), 7919)
    shape, _ = _output_shape()
    out = dict(inp)
    out["loss_target"] = _jax.random.normal(_jax.random.fold_in(key, 0), shape, _jnp.float32)
    for i, name in enumerate(TWIN_WEIGHTS):
        w = inp[name].astype(_jnp.float32)
        if MOMENT_SCALE is None:
            s = _jnp.sqrt(_jnp.mean(_jnp.square(w)) + 1e-30)
        else:
            s = MOMENT_SCALE[name]
        km, kv = _jax.random.split(_jax.random.fold_in(key, i + 1))
        out[name] = w
        out["m_" + name] = s * _jax.random.normal(km, w.shape, _jnp.float32)
        out["v_" + name] = (s * s) * _jax.random.uniform(kv, w.shape, _jnp.float32, 0.5, 1.5)
    if N_MICROBATCH > 1:
        for name, axis in PER_EXAMPLE_BATCH_AXIS.items():
            out[name] = _to_microbatches(out[name], axis)
    return {'x': out['x'], 'mem': out['mem'], 'norm_mix': out['norm_mix'], 'norm_ffn': out['norm_ffn'], 'a_w_in': out['a_w_in'], 'a_lb_logits': out['a_lb_logits'], 'a_onorm': out['a_onorm'], 'b_w_in': out['b_w_in'], 'b_qnorm': out['b_qnorm'], 'kv_norm': out['kv_norm'], 'w_kv': out['w_kv'], 'b_knorm': out['b_knorm'], 'mem_norm': out['mem_norm'], 'w_mem_kv': out['w_mem_kv'], 'mem_qnorm': out['mem_qnorm'], 'mem_knorm': out['mem_knorm'], 'w_out': out['w_out'], 'w_gate_up': out['w_gate_up'], 'w_down': out['w_down'], 'loss_target': out['loss_target'], 'm_norm_mix': out['m_norm_mix'], 'm_norm_ffn': out['m_norm_ffn'], 'm_a_w_in': out['m_a_w_in'], 'm_a_lb_logits': out['m_a_lb_logits'], 'm_a_onorm': out['m_a_onorm'], 'm_b_w_in': out['m_b_w_in'], 'm_b_qnorm': out['m_b_qnorm'], 'm_kv_norm': out['m_kv_norm'], 'm_w_kv': out['m_w_kv'], 'm_b_knorm': out['m_b_knorm'], 'm_mem_norm': out['m_mem_norm'], 'm_w_mem_kv': out['m_w_mem_kv'], 'm_mem_qnorm': out['m_mem_qnorm'], 'm_mem_knorm': out['m_mem_knorm'], 'm_w_out': out['m_w_out'], 'm_w_gate_up': out['m_w_gate_up'], 'm_w_down': out['m_w_down'], 'v_norm_mix': out['v_norm_mix'], 'v_norm_ffn': out['v_norm_ffn'], 'v_a_w_in': out['v_a_w_in'], 'v_a_lb_logits': out['v_a_lb_logits'], 'v_a_onorm': out['v_a_onorm'], 'v_b_w_in': out['v_b_w_in'], 'v_b_qnorm': out['v_b_qnorm'], 'v_kv_norm': out['v_kv_norm'], 'v_w_kv': out['v_w_kv'], 'v_b_knorm': out['v_b_knorm'], 'v_mem_norm': out['v_mem_norm'], 'v_w_mem_kv': out['v_w_mem_kv'], 'v_mem_qnorm': out['v_mem_qnorm'], 'v_mem_knorm': out['v_mem_knorm'], 'v_w_out': out['v_w_out'], 'v_w_gate_up': out['v_w_gate_up'], 'v_w_down': out['v_w_down']}


def _loss(weights, diff, rest, loss_target):
    with _jax.named_scope("forward"):
        args = {**rest, TWIN_DIFF_INPUT: diff, **{k: w.astype(_WEIGHT_DTYPES[k]) for k, w in weights.items()}}
        y = _forward(args)
    with _jax.named_scope("loss_head"):
        err = _jnp.square(y.astype(_jnp.float32) - loss_target)
        return 0.5 * _jnp.sum(_jnp.mean(err, axis=-1)) if err.ndim else 0.5 * err


def _adamw(w, g, m, v):
    m = ADAM_B1 * m + (1.0 - ADAM_B1) * g
    v = ADAM_B2 * v + (1.0 - ADAM_B2) * _jnp.square(g)
    m_hat = m / (1.0 - ADAM_B1 ** ADAM_STEP)
    v_hat = v / (1.0 - ADAM_B2 ** ADAM_STEP)
    delta = -ADAM_LR * (m_hat / (_jnp.sqrt(v_hat) + ADAM_EPS) + ADAM_WD * w)
    return delta, m, v


def reference(x, mem, norm_mix, norm_ffn, a_w_in, a_lb_logits, a_onorm, b_w_in, b_qnorm, kv_norm, w_kv, b_knorm, mem_norm, w_mem_kv, mem_qnorm, mem_knorm, w_out, w_gate_up, w_down, loss_target, m_norm_mix, m_norm_ffn, m_a_w_in, m_a_lb_logits, m_a_onorm, m_b_w_in, m_b_qnorm, m_kv_norm, m_w_kv, m_b_knorm, m_mem_norm, m_w_mem_kv, m_mem_qnorm, m_mem_knorm, m_w_out, m_w_gate_up, m_w_down, v_norm_mix, v_norm_ffn, v_a_w_in, v_a_lb_logits, v_a_onorm, v_b_w_in, v_b_qnorm, v_kv_norm, v_w_kv, v_b_knorm, v_mem_norm, v_w_mem_kv, v_mem_qnorm, v_mem_knorm, v_w_out, v_w_gate_up, v_w_down):
    given = dict(x=x, mem=mem, norm_mix=norm_mix, norm_ffn=norm_ffn, a_w_in=a_w_in, a_lb_logits=a_lb_logits, a_onorm=a_onorm, b_w_in=b_w_in, b_qnorm=b_qnorm, kv_norm=kv_norm, w_kv=w_kv, b_knorm=b_knorm, mem_norm=mem_norm, w_mem_kv=w_mem_kv, mem_qnorm=mem_qnorm, mem_knorm=mem_knorm, w_out=w_out, w_gate_up=w_gate_up, w_down=w_down, loss_target=loss_target, m_norm_mix=m_norm_mix, m_norm_ffn=m_norm_ffn, m_a_w_in=m_a_w_in, m_a_lb_logits=m_a_lb_logits, m_a_onorm=m_a_onorm, m_b_w_in=m_b_w_in, m_b_qnorm=m_b_qnorm, m_kv_norm=m_kv_norm, m_w_kv=m_w_kv, m_b_knorm=m_b_knorm, m_mem_norm=m_mem_norm, m_w_mem_kv=m_w_mem_kv, m_mem_qnorm=m_mem_qnorm, m_mem_knorm=m_mem_knorm, m_w_out=m_w_out, m_w_gate_up=m_w_gate_up, m_w_down=m_w_down, v_norm_mix=v_norm_mix, v_norm_ffn=v_norm_ffn, v_a_w_in=v_a_w_in, v_a_lb_logits=v_a_lb_logits, v_a_onorm=v_a_onorm, v_b_w_in=v_b_w_in, v_b_qnorm=v_b_qnorm, v_kv_norm=v_kv_norm, v_w_kv=v_w_kv, v_b_knorm=v_b_knorm, v_mem_norm=v_mem_norm, v_w_mem_kv=v_w_mem_kv, v_mem_qnorm=v_mem_qnorm, v_mem_knorm=v_mem_knorm, v_w_out=v_w_out, v_w_gate_up=v_w_gate_up, v_w_down=v_w_down)
    weights = {n: given[n] for n in TWIN_WEIGHTS}
    shared = {n: given[n] for n in SHARED_INPUTS}
    per_example = {n: given[n] for n in ['x', 'mem']}
    grad_fn = _jax.value_and_grad(_loss, argnums=(0, 1))

    def one_microbatch(ex, loss_target):
        ex = dict(ex)
        diff = ex.pop(TWIN_DIFF_INPUT)
        return grad_fn(weights, diff, {**shared, **ex}, loss_target)

    if N_MICROBATCH == 1:
        loss, (grad_w, grad_x) = one_microbatch(per_example, given["loss_target"])
    else:
        def body(carry, xs):
            loss_sum, grad_sum = carry
            l_k, (gw_k, gx_k) = one_microbatch(xs[0], xs[1])
            with _jax.named_scope("update"):
                return (loss_sum + l_k, _jax.tree.map(_jnp.add, grad_sum, gw_k)), gx_k

        init = (_jnp.zeros((), _jnp.float32), _jax.tree.map(_jnp.zeros_like, weights))
        (loss, grad_w), grad_x = _jax.lax.scan(body, init, (per_example, given["loss_target"]))
    with _jax.named_scope("update"):
        delta_w, new_m, new_v = {}, {}, {}
        for n in TWIN_WEIGHTS:
            delta_w[n], new_m[n], new_v[n] = _adamw(weights[n], grad_w[n], given["m_" + n], given["v_" + n])
    return (loss, grad_x, *[grad_w[n] for n in TWIN_WEIGHTS], *[delta_w[n] for n in TWIN_WEIGHTS],
            *[new_m[n] for n in TWIN_WEIGHTS], *[new_v[n] for n in TWIN_WEIGHTS])
```

```python
import functools

import jax
import jax.numpy as jnp
from jax import lax
from jax.experimental import pallas as pl
from jax.experimental.pallas import tpu as pltpu

F32 = jnp.float32
BF16 = jnp.bfloat16

N_DEV = 8
D_MODEL = 1024
HEAD_DIM = 128
A_HEADS = 6
A_WIDTH = A_HEADS * HEAD_DIM
CHUNK = 64
B_HEADS = 6
B_WIDTH = B_HEADS * HEAD_DIM
DILATIONS = (1, 4, 16)
SPAN = 128
N_GROUPS = 3
ROPE_THETA = 10000.0
MEM_TOKENS = 256
MEM_HEADS = 4
MEM_HEAD_DIM = 64
MEM_WIDTH = MEM_HEADS * MEM_HEAD_DIM
FFN_HIDDEN = 2816
EPS = 1e-6

ADAM_LR = 0.001
ADAM_B1 = 0.9
ADAM_B2 = 0.999
ADAM_EPS = 1e-08
ADAM_WD = 0.01
ADAM_STEP = 10

V7X_VMEM_LIMIT_BYTES = 56 * 1024 * 1024

NT_DIMS = (((1,), (1,)), ((), ()))
TN_DIMS = (((0,), (0,)), ((), ()))


def _cp(*sem):
    return pltpu.CompilerParams(dimension_semantics=sem, vmem_limit_bytes=V7X_VMEM_LIMIT_BYTES)


def _dot(a, b):
    return jnp.dot(a.astype(BF16), b.astype(BF16), preferred_element_type=F32)


def _dot_nt(a, b):
    return lax.dot_general(a.astype(BF16), b.astype(BF16), NT_DIMS, preferred_element_type=F32)


def _dot_tn(a, b):
    return lax.dot_general(a.astype(BF16), b.astype(BF16), TN_DIMS, preferred_element_type=F32)


def _dot3(m01, x):
    hi = x.astype(BF16)
    r1 = x - hi.astype(F32)
    mid = r1.astype(BF16)
    lo = (r1 - mid.astype(F32)).astype(BF16)
    d = functools.partial(jnp.dot, preferred_element_type=F32)
    return d(m01, hi) + d(m01, mid) + d(m01, lo)


def _sigmoid(x):
    return 1.0 / (1.0 + jnp.exp(-x))


def _full(shape):
    return pl.BlockSpec(shape, lambda *_: (0,) * len(shape))


def _rms_matmul(x, g, w, *, tt, tn, name):
    T, K = x.shape
    N = w.shape[1]

    def body(x_ref, g_ref, w_ref, y_ref, xn_ref):
        @pl.when(pl.program_id(1) == 0)
        def _():
            xf = x_ref[...]
            r = lax.rsqrt(jnp.mean(xf * xf, axis=-1, keepdims=True) + EPS)
            xn_ref[...] = (xf * r * g_ref[...]).astype(BF16)

        y_ref[...] = jnp.dot(xn_ref[...], w_ref[...], preferred_element_type=F32)

    return pl.pallas_call(
        body, grid=(T // tt, N // tn),
        in_specs=[pl.BlockSpec((tt, K), lambda i, j: (i, 0)), _full((1, K)), pl.BlockSpec((K, tn), lambda i, j: (0, j))],
        out_specs=[pl.BlockSpec((tt, tn), lambda i, j: (i, j)), pl.BlockSpec((tt, K), lambda i, j: (i, 0))],
        out_shape=[jax.ShapeDtypeStruct((T, N), F32), jax.ShapeDtypeStruct((T, K), BF16)],
        compiler_params=_cp("parallel", "arbitrary"), name=name)(x, g, w)


def _mm_res(res, a, w, *, tt, name):
    T, K = a.shape
    N = w.shape[1]

    def body(r_ref, a_ref, w_ref, o_ref):
        o_ref[...] = r_ref[...] + _dot(a_ref[...], w_ref[...])

    return pl.pallas_call(
        body, grid=(T // tt,),
        in_specs=[pl.BlockSpec((tt, N), lambda i: (i, 0)), pl.BlockSpec((tt, K), lambda i: (i, 0)), _full((K, N))],
        out_specs=pl.BlockSpec((tt, N), lambda i: (i, 0)),
        out_shape=jax.ShapeDtypeStruct((T, N), F32),
        compiler_params=_cp("parallel"), name=name)(res, a, w)


def _swiglu_down(h, gu, wd, *, tt, name):
    T, D = h.shape
    Fh = wd.shape[0]

    def body(h_ref, gt_ref, up_ref, w_ref, o_ref):
        gt = gt_ref[...]
        act = gt * _sigmoid(gt) * up_ref[...]
        o_ref[...] = h_ref[...] + _dot(act, w_ref[...])

    return pl.pallas_call(
        body, grid=(T // tt,),
        in_specs=[pl.BlockSpec((tt, D), lambda i: (i, 0)), pl.BlockSpec((tt, Fh), lambda i: (i, 0)),
                  pl.BlockSpec((tt, Fh), lambda i: (i, 1)), _full((Fh, D))],
        out_specs=pl.BlockSpec((tt, D), lambda i: (i, 0)),
        out_shape=jax.ShapeDtypeStruct((T, D), F32),
        compiler_params=_cp("parallel"), name=name)(h, gu, gu, wd)


def _swiglu_bwd(dh, gu, wd, *, tt, name):
    T, D = dh.shape
    Fh = wd.shape[0]

    def body(dh_ref, gt_ref, up_ref, w_ref, dgu_ref, act_ref):
        gt = gt_ref[...]
        up = up_ref[...]
        s = _sigmoid(gt)
        silu = gt * s
        dact = _dot_nt(dh_ref[...], w_ref[...])
        act_ref[...] = (silu * up).astype(BF16)
        dgu_ref[:, :Fh] = (dact * up * (s * (1.0 + gt * (1.0 - s)))).astype(BF16)
        dgu_ref[:, Fh:] = (dact * silu).astype(BF16)

    return pl.pallas_call(
        body, grid=(T // tt,),
        in_specs=[pl.BlockSpec((tt, D), lambda i: (i, 0)), pl.BlockSpec((tt, Fh), lambda i: (i, 0)),
                  pl.BlockSpec((tt, Fh), lambda i: (i, 1)), _full((Fh, D))],
        out_specs=[pl.BlockSpec((tt, 2 * Fh), lambda i: (i, 0)), pl.BlockSpec((tt, Fh), lambda i: (i, 0))],
        out_shape=[jax.ShapeDtypeStruct((T, 2 * Fh), BF16), jax.ShapeDtypeStruct((T, Fh), BF16)],
        compiler_params=_cp("parallel"), name=name)(dh, gu, gu, wd)


def _mm_nt(a, w, *, tt, name):
    T, N = a.shape
    K = w.shape[0]

    def body(a_ref, w_ref, o_ref):
        o_ref[...] = _dot_nt(a_ref[...], w_ref[...])

    return pl.pallas_call(
        body, grid=(T // tt,),
        in_specs=[pl.BlockSpec((tt, N), lambda i: (i, 0)), _full((K, N))],
        out_specs=pl.BlockSpec((tt, K), lambda i: (i, 0)),
        out_shape=jax.ShapeDtypeStruct((T, K), F32),
        compiler_params=_cp("parallel"), name=name)(a, w)


def _mm_tn(a, b, *, tt, tn, name):
    T, K = a.shape
    N = b.shape[1]

    def body(a_ref, b_ref, o_ref):
        @pl.when(pl.program_id(1) == 0)
        def _():
            o_ref[...] = jnp.zeros_like(o_ref)

        o_ref[...] += _dot_tn(a_ref[...], b_ref[...])

    return pl.pallas_call(
        body, grid=(N // tn, T // tt),
        in_specs=[pl.BlockSpec((tt, K), lambda j, t: (t, 0)), pl.BlockSpec((tt, tn), lambda j, t: (t, j))],
        out_specs=pl.BlockSpec((K, tn), lambda j, t: (0, j)),
        out_shape=jax.ShapeDtypeStruct((K, N), F32),
        compiler_params=_cp("parallel", "arbitrary"), name=name)(a, b)


def _rms_bwd_dx(x, g, w, dy, dres, *, tt, name):
    T, K = x.shape
    N = w.shape[1]

    def body(x_ref, g_ref, w_ref, dy_ref, dres_ref, dx_ref, dg_ref):
        @pl.when(pl.program_id(0) == 0)
        def _():
            dg_ref[...] = jnp.zeros_like(dg_ref)

        dxn = _dot_nt(dy_ref[...], w_ref[...])
        xf = x_ref[...]
        r = lax.rsqrt(jnp.mean(xf * xf, axis=-1, keepdims=True) + EPS)
        xhat = xf * r
        dg_ref[...] += jnp.sum(dxn * xhat, axis=0, keepdims=True)
        dxhat = dxn * g_ref[...]
        dx_ref[...] = dres_ref[...] + r * (dxhat - xhat * jnp.mean(dxhat * xhat, axis=-1, keepdims=True))

    return pl.pallas_call(
        body, grid=(T // tt,),
        in_specs=[pl.BlockSpec((tt, K), lambda i: (i, 0)), _full((1, K)), _full((K, N)),
                  pl.BlockSpec((tt, N), lambda i: (i, 0)), pl.BlockSpec((tt, K), lambda i: (i, 0))],
        out_specs=[pl.BlockSpec((tt, K), lambda i: (i, 0)), _full((1, K))],
        out_shape=[jax.ShapeDtypeStruct((T, K), F32), jax.ShapeDtypeStruct((1, K), F32)],
        compiler_params=_cp("arbitrary"), name=name)(x, g, w, dy, dres)


def _loss_kernel(y, tgt, *, tt, name):
    T, D = y.shape

    def body(y_ref, t_ref, dy_ref, acc_ref):
        @pl.when(pl.program_id(0) == 0)
        def _():
            acc_ref[...] = jnp.zeros_like(acc_ref)

        e = y_ref[...] - t_ref[...]
        dy_ref[...] = e * (1.0 / D)
        acc_ref[...] += jnp.sum(e * e, axis=0, keepdims=True)

    return pl.pallas_call(
        body, grid=(T // tt,),
        in_specs=[pl.BlockSpec((tt, D), lambda i: (i, 0)), pl.BlockSpec((tt, D), lambda i: (i, 0))],
        out_specs=[pl.BlockSpec((tt, D), lambda i: (i, 0)), _full((1, D))],
        out_shape=[jax.ShapeDtypeStruct((T, D), F32), jax.ShapeDtypeStruct((1, D), F32)],
        compiler_params=_cp("arbitrary"), name=name)(y, tgt)


HGRN_TB = 512
HGRN_NCH = HGRN_TB // CHUNK


def _hgrn_chunk_fwd(q, z, lbv, tril01):
    sig = _sigmoid(z)
    f = lbv + (1.0 - lbv) * sig
    kk = 1.0 - f
    b = _dot3(tril01, jnp.log(f))
    bend = b[CHUNK - 1:CHUNK, :]
    sq = _sigmoid(q)
    eb = jnp.exp(b)
    emb = jnp.exp(-b)
    eo = jnp.exp(bend - b)
    dec = jnp.exp(bend)
    return sig, f, kk, sq, eb, emb, eo, dec


def _hgrn2_fwd(proj, lb, *, name):
    T = proj.shape[0]
    nT = T // HGRN_TB
    nC = T // CHUNK

    def body(q_ref, z_ref, v_ref, lb_ref, o_ref, st_ref, state):
        @pl.when(pl.program_id(1) == 0)
        def _():
            state[...] = jnp.zeros_like(state)

        row = lax.broadcasted_iota(jnp.int32, (CHUNK, CHUNK), 0)
        col = lax.broadcasted_iota(jnp.int32, (CHUNK, CHUNK), 1)
        causal = row >= col
        tril01 = causal.astype(BF16)
        lbv = lb_ref[...]

        def chunk(c, carry):
            rows = pl.ds(pl.multiple_of(c * CHUNK, CHUNK), CHUNK)
            q = q_ref[rows, :]
            v = v_ref[rows, :].astype(BF16)
            sig, f, kk, sq, eb, emb, eo, dec = _hgrn_chunk_fwd(q, z_ref[rows, :], lbv, tril01)
            qi = (q * sq * eb).astype(BF16)
            ki = (kk * emb).astype(BF16)
            ko = (kk * eo).astype(BF16)
            st = state[...]
            att = jnp.where(causal, _dot_nt(qi, ki), 0.0)
            o_ref[rows, :] = _dot(att, v) + _dot_nt(qi, st)
            st_ref[c, 0] = st
            state[...] = st * dec + _dot_tn(v, ko)
            return carry

        lax.fori_loop(0, HGRN_NCH, chunk, 0)

    hb = lambda off: pl.BlockSpec((HGRN_TB, HEAD_DIM), lambda h, i: (i, off + h))
    return pl.pallas_call(
        body, grid=(A_HEADS, nT),
        in_specs=[hb(0), hb(A_HEADS), hb(2 * A_HEADS), pl.BlockSpec((1, HEAD_DIM), lambda h, i: (0, h))],
        out_specs=[hb(0), pl.BlockSpec((HGRN_NCH, 1, HEAD_DIM, HEAD_DIM), lambda h, i: (i, h, 0, 0))],
        out_shape=[jax.ShapeDtypeStruct((T, A_WIDTH), F32), jax.ShapeDtypeStruct((nC, A_HEADS, HEAD_DIM, HEAD_DIM), F32)],
        scratch_shapes=[pltpu.VMEM((HEAD_DIM, HEAD_DIM), F32)],
        compiler_params=_cp("parallel", "arbitrary"), name=name)(proj, proj, proj, lb)


def _hgrn2_bwd(proj, lb, st_all, do, *, name):
    T = proj.shape[0]
    nT = T // HGRN_TB

    def body(q_ref, z_ref, v_ref, lb_ref, st_ref, do_ref, dq_ref, dz_ref, dv_ref, dlb_ref, dstate):
        @pl.when(pl.program_id(1) == 0)
        def _():
            dstate[...] = jnp.zeros_like(dstate)
            dlb_ref[...] = jnp.zeros_like(dlb_ref)

        row = lax.broadcasted_iota(jnp.int32, (CHUNK, CHUNK), 0)
        col = lax.broadcasted_iota(jnp.int32, (CHUNK, CHUNK), 1)
        causal = row >= col
        tril01 = causal.astype(BF16)
        triu01 = (row <= col).astype(BF16)
        lbv = lb_ref[...]

        def chunk(cc, carry):
            c = HGRN_NCH - 1 - cc
            rows = pl.ds(pl.multiple_of(c * CHUNK, CHUNK), CHUNK)
            q = q_ref[rows, :]
            v = v_ref[rows, :].astype(BF16)
            sig, f, kk, sq, eb, emb, eo, dec = _hgrn_chunk_fwd(q, z_ref[rows, :], lbv, tril01)
            qi32 = q * sq * eb
            ki32 = kk * emb
            ko32 = kk * eo
            qi, ki, ko = qi32.astype(BF16), ki32.astype(BF16), ko32.astype(BF16)
            att = jnp.where(causal, _dot_nt(qi, ki), 0.0).astype(BF16)
            dout = do_ref[rows, :].astype(BF16)
            st = st_ref[c, 0]
            dst = dstate[...]
            dst16 = dst.astype(BF16)
            datt = jnp.where(causal, _dot_nt(dout, v), 0.0).astype(BF16)
            dqi = _dot(datt, ki) + _dot(dout, st)
            dki = _dot_tn(datt, qi)
            dv_ref[rows, :] = (_dot_tn(att, dout) + _dot_nt(ko, dst16)).astype(BF16)
            dko = _dot(v, dst16)
            ddec = jnp.sum(dst * st, axis=0, keepdims=True)
            dstate[...] = dst * dec + _dot_tn(dout, qi)
            dkk = dki * emb + dko * eo
            db = dqi * qi32 - dki * ki32 - dko * ko32
            dbend = jnp.sum(dko * ko32, axis=0, keepdims=True) + ddec * dec
            dlogf = _dot3(triu01, db) + dbend
            df = dlogf / f - dkk
            dz_ref[rows, :] = (df * (1.0 - lbv) * sig * (1.0 - sig)).astype(BF16)
            dlb_ref[...] += jnp.sum(df * (1.0 - sig), axis=0, keepdims=True)
            dq_ref[rows, :] = (dqi * eb * (sq * (1.0 + q * (1.0 - sq)))).astype(BF16)
            return carry

        lax.fori_loop(0, HGRN_NCH, chunk, 0)

    hb = lambda off: pl.BlockSpec((HGRN_TB, HEAD_DIM), lambda h, i: (nT - 1 - i, off + h))
    hlb = pl.BlockSpec((1, HEAD_DIM), lambda h, i: (0, h))
    o16 = jax.ShapeDtypeStruct((T, A_WIDTH), BF16)
    return pl.pallas_call(
        body, grid=(A_HEADS, nT),
        in_specs=[hb(0), hb(A_HEADS), hb(2 * A_HEADS), hlb,
                  pl.BlockSpec((HGRN_NCH, 1, HEAD_DIM, HEAD_DIM), lambda h, i: (nT - 1 - i, h, 0, 0)), hb(0)],
        out_specs=[hb(0), hb(0), hb(0), hlb],
        out_shape=[o16, o16, o16, jax.ShapeDtypeStruct((1, A_WIDTH), F32)],
        scratch_shapes=[pltpu.VMEM((HEAD_DIM, HEAD_DIM), F32)],
        compiler_params=_cp("parallel", "arbitrary"), name=name)(proj, proj, proj, lb, st_all, do)


def _head_rms(x):
    r = lax.rsqrt(jnp.mean(x * x, axis=-1, keepdims=True) + EPS)
    return x * r, r


def _head_rms_bwd(dxhat, xhat, r):
    return r * (dxhat - xhat * jnp.mean(dxhat * xhat, axis=-1, keepdims=True))


def _a_post_fwd(o, proj, onorm, *, tt, name):
    T = o.shape[0]

    def body(o_ref, g_ref, w_ref, y_ref):
        for h in range(A_HEADS):
            sl = slice(h * HEAD_DIM, (h + 1) * HEAD_DIM)
            xhat, _ = _head_rms(o_ref[:, sl])
            g = g_ref[:, sl]
            y_ref[:, sl] = xhat * w_ref[:, sl] * (g * _sigmoid(g))

    blk = lambda c: pl.BlockSpec((tt, A_WIDTH), lambda i: (i, c))
    return pl.pallas_call(
        body, grid=(T // tt,), in_specs=[blk(0), blk(3), _full((1, A_WIDTH))], out_specs=blk(0),
        out_shape=jax.ShapeDtypeStruct((T, A_WIDTH), F32),
        compiler_params=_cp("parallel"), name=name)(o, proj, onorm)


def _a_post_bwd(o, proj, onorm, dmix, *, tt, name):
    T = o.shape[0]

    def body(o_ref, g_ref, w_ref, dy_ref, do_ref, dg_ref, dw_ref):
        @pl.when(pl.program_id(0) == 0)
        def _():
            dw_ref[...] = jnp.zeros_like(dw_ref)

        for h in range(A_HEADS):
            sl = slice(h * HEAD_DIM, (h + 1) * HEAD_DIM)
            xhat, r = _head_rms(o_ref[:, sl])
            g = g_ref[:, sl]
            s = _sigmoid(g)
            dy = dy_ref[:, sl]
            w = w_ref[:, sl]
            dg_ref[:, sl] = (dy * xhat * w * (s * (1.0 + g * (1.0 - s)))).astype(BF16)
            dyn = dy * (g * s)
            dw_ref[:, sl] += jnp.sum(dyn * xhat, axis=0, keepdims=True)
            do_ref[:, sl] = _head_rms_bwd(dyn * w, xhat, r)

    blk = lambda c: pl.BlockSpec((tt, A_WIDTH), lambda i: (i, c))
    return pl.pallas_call(
        body, grid=(T // tt,), in_specs=[blk(0), blk(3), _full((1, A_WIDTH)), blk(0)],
        out_specs=[blk(0), blk(0), _full((1, A_WIDTH))],
        out_shape=[jax.ShapeDtypeStruct((T, A_WIDTH), F32), jax.ShapeDtypeStruct((T, A_WIDTH), BF16),
                   jax.ShapeDtypeStruct((1, A_WIDTH), F32)],
        compiler_params=_cp("arbitrary"), name=name)(o, proj, onorm, dmix)


def _mem_head_masks(n):
    lane = lax.broadcasted_iota(jnp.int32, (n, MEM_WIDTH), 1)
    return [(lane >= m * MEM_HEAD_DIM) & (lane < (m + 1) * MEM_HEAD_DIM) for m in range(MEM_HEADS)]


def _mem_head_rms(x, masks):
    x2 = x * x
    r = jnp.zeros_like(x)
    for mk in masks:
        ms = jnp.sum(jnp.where(mk, x2, 0.0), axis=-1, keepdims=True) * (1.0 / MEM_HEAD_DIM)
        r = jnp.where(mk, lax.rsqrt(ms + EPS), r)
    return x * r, r


def _mem_head_rms_bwd(dxhat, xhat, r, masks):
    t = dxhat * xhat
    m = jnp.zeros_like(t)
    for mk in masks:
        m = jnp.where(mk, jnp.sum(jnp.where(mk, t, 0.0), axis=-1, keepdims=True) * (1.0 / MEM_HEAD_DIM), m)
    return r * (dxhat - xhat * m)


MEM_SCALE = MEM_HEAD_DIM ** -0.5


def _mem_attn_fwd(proj, qcol, mkv, qn_w, kn_w, *, tt, name):
    T = proj.shape[0]

    def body(q_ref, k_ref, v_ref, qw_ref, kw_ref, o_ref):
        qmasks = _mem_head_masks(tt)
        kmasks = _mem_head_masks(MEM_TOKENS)
        qhat, _ = _mem_head_rms(q_ref[...], qmasks)
        qn = qhat * qw_ref[...]
        khat, _ = _mem_head_rms(k_ref[...], kmasks)
        kn = (khat * kw_ref[...]).astype(BF16)
        v = v_ref[...].astype(BF16)
        out = jnp.zeros((tt, MEM_WIDTH), F32)
        for m in range(MEM_HEADS):
            s = _dot_nt(jnp.where(qmasks[m], qn, 0.0), kn) * MEM_SCALE
            s = s - jnp.max(s, axis=-1, keepdims=True)
            p = jnp.exp(s)
            p = p / jnp.sum(p, axis=-1, keepdims=True)
            out = jnp.where(qmasks[m], _dot(p, v), out)
        o_ref[...] = out

    return pl.pallas_call(
        body, grid=(T // tt,),
        in_specs=[pl.BlockSpec((tt, MEM_WIDTH), lambda i: (i, qcol)), pl.BlockSpec((MEM_TOKENS, MEM_WIDTH), lambda i: (0, 0)),
                  pl.BlockSpec((MEM_TOKENS, MEM_WIDTH), lambda i: (0, 1)), _full((1, MEM_WIDTH)), _full((1, MEM_WIDTH))],
        out_specs=pl.BlockSpec((tt, MEM_WIDTH), lambda i: (i, 0)),
        out_shape=jax.ShapeDtypeStruct((T, MEM_WIDTH), F32),
        compiler_params=_cp("parallel"), name=name)(proj, mkv, mkv, qn_w, kn_w)


def _mem_attn_bwd(proj, qcol, mkv, qn_w, kn_w, dmix, *, tt, name):
    T = proj.shape[0]
    nsteps = T // tt
    ocol = (dmix.shape[1] - MEM_WIDTH) // MEM_WIDTH

    def body(q_ref, k_ref, v_ref, qw_ref, kw_ref, do_ref, dq_ref, dkv_ref, dqw_ref, dkw_ref, dk_acc, dv_acc):
        step = pl.program_id(0)

        @pl.when(step == 0)
        def _():
            dk_acc[...] = jnp.zeros_like(dk_acc)
            dv_acc[...] = jnp.zeros_like(dv_acc)
            dqw_ref[...] = jnp.zeros_like(dqw_ref)

        qmasks = _mem_head_masks(tt)
        kmasks = _mem_head_masks(MEM_TOKENS)
        qhat, qr = _mem_head_rms(q_ref[...], qmasks)
        qn = qhat * qw_ref[...]
        khat, kr = _mem_head_rms(k_ref[...], kmasks)
        kn = (khat * kw_ref[...]).astype(BF16)
        v = v_ref[...].astype(BF16)
        dout = do_ref[...]
        dqn = jnp.zeros((tt, MEM_WIDTH), F32)
        dkn = jnp.zeros((MEM_TOKENS, MEM_WIDTH), F32)
        dvv = jnp.zeros((MEM_TOKENS, MEM_WIDTH), F32)
        for m in range(MEM_HEADS):
            qm = jnp.where(qmasks[m], qn, 0.0).astype(BF16)
            s = _dot_nt(qm, kn) * MEM_SCALE
            s = s - jnp.max(s, axis=-1, keepdims=True)
            p = jnp.exp(s)
            p = p / jnp.sum(p, axis=-1, keepdims=True)
            dom = jnp.where(qmasks[m], dout, 0.0).astype(BF16)
            dp = _dot_nt(dom, v)
            ds = (p * (dp - jnp.sum(p * dp, axis=-1, keepdims=True)) * MEM_SCALE).astype(BF16)
            dqn = jnp.where(qmasks[m], _dot(ds, kn), dqn)
            dkn = jnp.where(kmasks[m], _dot_tn(ds, qm), dkn)
            dvv = jnp.where(kmasks[m], _dot_tn(p, dom), dvv)
        dqw_ref[...] += jnp.sum(dqn * qhat, axis=0, keepdims=True)
        dq_ref[...] = _mem_head_rms_bwd(dqn * qw_ref[...], qhat, qr, qmasks).astype(BF16)
        dk_acc[...] += dkn
        dv_acc[...] += dvv

        @pl.when(step == nsteps - 1)
        def _():
            dk = dk_acc[...]
            dkw_ref[...] = jnp.sum(dk * khat, axis=0, keepdims=True)
            dkv_ref[:, :MEM_WIDTH] = _mem_head_rms_bwd(dk * kw_ref[...], khat, kr, kmasks)
            dkv_ref[:, MEM_WIDTH:] = dv_acc[...]

    return pl.pallas_call(
        body, grid=(nsteps,),
        in_specs=[pl.BlockSpec((tt, MEM_WIDTH), lambda i: (i, qcol)), pl.BlockSpec((MEM_TOKENS, MEM_WIDTH), lambda i: (0, 0)),
                  pl.BlockSpec((MEM_TOKENS, MEM_WIDTH), lambda i: (0, 1)), _full((1, MEM_WIDTH)), _full((1, MEM_WIDTH)),
                  pl.BlockSpec((tt, MEM_WIDTH), lambda i: (i, ocol))],
        out_specs=[pl.BlockSpec((tt, MEM_WIDTH), lambda i: (i, 0)), _full((MEM_TOKENS, 2 * MEM_WIDTH)),
                   _full((1, MEM_WIDTH)), _full((1, MEM_WIDTH))],
        out_shape=[jax.ShapeDtypeStruct((T, MEM_WIDTH), BF16), jax.ShapeDtypeStruct((MEM_TOKENS, 2 * MEM_WIDTH), F32),
                   jax.ShapeDtypeStruct((1, MEM_WIDTH), F32), jax.ShapeDtypeStruct((1, MEM_WIDTH), F32)],
        scratch_shapes=[pltpu.VMEM((MEM_TOKENS, MEM_WIDTH), F32), pltpu.VMEM((MEM_TOKENS, MEM_WIDTH), F32)],
        compiler_params=_cp("arbitrary"), name=name)(proj, mkv, mkv, qn_w, kn_w, dmix)


HALF = HEAD_DIM // 2
ATT_SCALE = HEAD_DIM ** -0.5
NEG = -1e30


def _rope_tables(T):
    inv = ROPE_THETA ** (-jnp.arange(HALF, dtype=F32) / HALF)
    ang = jnp.arange(T, dtype=F32)[:, None] * inv[None, :]
    cos, sin = jnp.cos(ang), jnp.sin(ang)
    return jnp.concatenate([cos, cos], axis=-1), jnp.concatenate([-sin, sin], axis=-1)


def _rope(x, cosf, sinsg):
    return x * cosf + pltpu.roll(x, HALF, 1) * sinsg


def _rope_bwd(dy, cosf, sinsg):
    return dy * cosf + pltpu.roll(dy * sinsg, HALF, 1)


def _headnorm_rope_fwd(x, w_heads, cosf, sinsg, *, col0, n_heads, tt, name):
    T = x.shape[0]
    W = n_heads * HEAD_DIM

    def body(x_ref, w_ref, c_ref, s_ref, y_ref):
        c, s = c_ref[...], s_ref[...]
        for h in range(n_heads):
            sl = slice(h * HEAD_DIM, (h + 1) * HEAD_DIM)
            xhat, _ = _head_rms(x_ref[:, sl])
            y_ref[:, sl] = _rope(xhat * w_ref[:, sl], c, s).astype(BF16)

    tbl = pl.BlockSpec((tt, HEAD_DIM), lambda i: (i, 0))
    return pl.pallas_call(
        body, grid=(T // tt,),
        in_specs=[pl.BlockSpec((tt, W), lambda i: (i, col0)), _full((1, W)), tbl, tbl],
        out_specs=pl.BlockSpec((tt, W), lambda i: (i, 0)),
        out_shape=jax.ShapeDtypeStruct((T, W), BF16),
        compiler_params=_cp("parallel"), name=name)(x, w_heads, cosf, sinsg)


def _q_prep_bwd(proj, w_heads, cosf, sinsg, dqs, *, tt, name):
    T = proj.shape[0]
    W = N_GROUPS * B_WIDTH

    def body(x_ref, w_ref, c_ref, s_ref, d0, d1, d2, dx_ref, dw_ref):
        @pl.when(pl.program_id(0) == 0)
        def _():
            dw_ref[...] = jnp.zeros_like(dw_ref)

        c, s = c_ref[...], s_ref[...]
        for gi, d_ref in enumerate((d0, d1, d2)):
            for h in range(B_HEADS):
                sl = slice((gi * B_HEADS + h) * HEAD_DIM, (gi * B_HEADS + h + 1) * HEAD_DIM)
                xhat, r = _head_rms(x_ref[:, sl])
                dyn = _rope_bwd(d_ref[:, h * HEAD_DIM:(h + 1) * HEAD_DIM], c, s)
                dw_ref[:, sl] += jnp.sum(dyn * xhat, axis=0, keepdims=True)
                dx_ref[:, sl] = _head_rms_bwd(dyn * w_ref[:, sl], xhat, r).astype(BF16)

    tbl = pl.BlockSpec((tt, HEAD_DIM), lambda i: (i, 0))
    dyb = pl.BlockSpec((tt, B_WIDTH), lambda i: (i, 0))
    return pl.pallas_call(
        body, grid=(T // tt,),
        in_specs=[pl.BlockSpec((tt, W), lambda i: (i, 0)), _full((1, W)), tbl, tbl, dyb, dyb, dyb],
        out_specs=[pl.BlockSpec((tt, W), lambda i: (i, 0)), _full((1, W))],
        out_shape=[jax.ShapeDtypeStruct((T, W), BF16), jax.ShapeDtypeStruct((1, W), F32)],
        compiler_params=_cp("arbitrary"), name=name)(proj, w_heads, cosf, sinsg, *dqs)


def _kv_prep_bwd(kv, w_heads, cosf, sinsg, dks, dvs, *, tt, name):
    T = kv.shape[0]

    def body(x_ref, w_ref, c_ref, s_ref, k0, k1, k2, v0, v1, v2, dx_ref, dw_ref):
        @pl.when(pl.program_id(0) == 0)
        def _():
            dw_ref[...] = jnp.zeros_like(dw_ref)

        c, s = c_ref[...], s_ref[...]
        for h in range(B_HEADS):
            sl = slice(h * HEAD_DIM, (h + 1) * HEAD_DIM)
            vs = slice(B_WIDTH + h * HEAD_DIM, B_WIDTH + (h + 1) * HEAD_DIM)
            xhat, r = _head_rms(x_ref[:, sl])
            dyn = _rope_bwd(k0[:, sl] + k1[:, sl] + k2[:, sl], c, s)
            dw_ref[:, sl] += jnp.sum(dyn * xhat, axis=0, keepdims=True)
            dx_ref[:, sl] = _head_rms_bwd(dyn * w_ref[:, sl], xhat, r).astype(BF16)
            dx_ref[:, vs] = (v0[:, sl] + v1[:, sl] + v2[:, sl]).astype(BF16)

    tbl = pl.BlockSpec((tt, HEAD_DIM), lambda i: (i, 0))
    dyb = pl.BlockSpec((tt, B_WIDTH), lambda i: (i, 0))
    return pl.pallas_call(
        body, grid=(T // tt,),
        in_specs=[dyb, _full((1, B_WIDTH)), tbl, tbl] + [dyb] * 6,
        out_specs=[pl.BlockSpec((tt, 2 * B_WIDTH), lambda i: (i, 0)), _full((1, B_WIDTH))],
        out_shape=[jax.ShapeDtypeStruct((T, 2 * B_WIDTH), BF16), jax.ShapeDtypeStruct((1, B_WIDTH), F32)],
        compiler_params=_cp("arbitrary"), name=name)(kv, w_heads, cosf, sinsg, *dks, *dvs)


def _band_masks(n_is_first=None):
    row = lax.broadcasted_iota(jnp.int32, (SPAN, SPAN), 0)
    col = lax.broadcasted_iota(jnp.int32, (SPAN, SPAN), 1)
    return row >= col, col >= row


def _dil_views(T, d):
    L = T // d
    return L, L // SPAN


def _dil_fwd(qr, kr, kv, gi, d, *, name):
    T = qr.shape[0]
    L, nb = _dil_views(T, d)

    def body(q_ref, kc_ref, kp_ref, vc_ref, vp_ref, o_ref, lse_ref):
        cur_ok, prev_band = _band_masks()
        prev_ok = prev_band & (pl.program_id(1) > 0)
        for h in range(B_HEADS):
            sl = slice(h * HEAD_DIM, (h + 1) * HEAD_DIM)
            q = q_ref[:, sl]
            sc = jnp.where(cur_ok, _dot_nt(q, kc_ref[:, sl]) * ATT_SCALE, NEG)
            sp = jnp.where(prev_ok, _dot_nt(q, kp_ref[:, sl]) * ATT_SCALE, NEG)
            m = jnp.maximum(jnp.max(sc, axis=-1, keepdims=True), jnp.max(sp, axis=-1, keepdims=True))
            pc = jnp.exp(sc - m)
            pp = jnp.exp(sp - m)
            l = jnp.sum(pc, axis=-1, keepdims=True) + jnp.sum(pp, axis=-1, keepdims=True)
            o_ref[:, sl] = (_dot(pc, vc_ref[:, sl]) + _dot(pp, vp_ref[:, sl])) / l
            lse_ref[:, sl] = jnp.broadcast_to(m + jnp.log(l), (SPAN, HEAD_DIM))

    blk = lambda f: pl.BlockSpec((SPAN, B_WIDTH), f)
    cur = lambda r, n: (n, r)
    prev = lambda r, n: (jnp.maximum(n - 1, 0), r)
    ov = jax.ShapeDtypeStruct((L, d * B_WIDTH), F32)
    o, lse = pl.pallas_call(
        body, grid=(d, nb),
        in_specs=[blk(lambda r, n: (n, r * N_GROUPS + gi)), blk(cur), blk(prev),
                  blk(lambda r, n: (n, 2 * r + 1)), blk(lambda r, n: (jnp.maximum(n - 1, 0), 2 * r + 1))],
        out_specs=[blk(cur), blk(cur)], out_shape=[ov, ov],
        compiler_params=_cp("parallel", "arbitrary"), name=name,
    )(qr.reshape(L, d * N_GROUPS * B_WIDTH), kr.reshape(L, d * B_WIDTH), kr.reshape(L, d * B_WIDTH),
      kv.reshape(L, d * 2 * B_WIDTH), kv.reshape(L, d * 2 * B_WIDTH))
    return o.reshape(T, B_WIDTH), lse.reshape(T, B_WIDTH)


def _dil_combine_fwd(os_, lses, *, tt, name):
    T = os_[0].shape[0]

    def body(o0, o1, o2, l0, l1, l2, y_ref, lse_ref):
        a, b, c = l0[...], l1[...], l2[...]
        m = jnp.maximum(jnp.maximum(a, b), c)
        wa, wb, wc = jnp.exp(a - m), jnp.exp(b - m), jnp.exp(c - m)
        den = wa + wb + wc
        y_ref[...] = (wa * o0[...] + wb * o1[...] + wc * o2[...]) / den
        lse_ref[...] = m + jnp.log(den)

    blk = pl.BlockSpec((tt, B_WIDTH), lambda i: (i, 0))
    sh = jax.ShapeDtypeStruct((T, B_WIDTH), F32)
    return pl.pallas_call(
        body, grid=(T // tt,), in_specs=[blk] * 6, out_specs=[blk, blk], out_shape=[sh, sh],
        compiler_params=_cp("parallel"), name=name)(*os_, *lses)


def _dil_bwd_prep(dmix, mix_main, *, tt, name):
    T = mix_main.shape[0]

    def body(dy_ref, y_ref, dmm_ref, dd_ref):
        for h in range(B_HEADS):
            sl = slice(h * HEAD_DIM, (h + 1) * HEAD_DIM)
            dy = dy_ref[:, sl]
            dmm_ref[:, sl] = dy.astype(BF16)
            dd_ref[:, sl] = jnp.broadcast_to(jnp.sum(dy * y_ref[:, sl], axis=-1, keepdims=True), (tt, HEAD_DIM))

    blk = pl.BlockSpec((tt, B_WIDTH), lambda i: (i, 0))
    return pl.pallas_call(
        body, grid=(T // tt,), in_specs=[blk, blk], out_specs=[blk, blk],
        out_shape=[jax.ShapeDtypeStruct((T, B_WIDTH), BF16), jax.ShapeDtypeStruct((T, B_WIDTH), F32)],
        compiler_params=_cp("parallel"), name=name)(dmix, mix_main)


def _dil_bwd_dq(qr, kr, kv, dmm, lse, dd, gi, d, *, name):
    T = qr.shape[0]
    L, nb = _dil_views(T, d)

    def body(q_ref, kc_ref, kp_ref, vc_ref, vp_ref, dy_ref, lse_ref, dd_ref, dq_ref):
        cur_ok, prev_band = _band_masks()
        prev_ok = prev_band & (pl.program_id(1) > 0)
        for h in range(B_HEADS):
            sl = slice(h * HEAD_DIM, (h + 1) * HEAD_DIM)
            q, dy = q_ref[:, sl], dy_ref[:, sl]
            kc, kp = kc_ref[:, sl], kp_ref[:, sl]
            lse_h = jnp.max(lse_ref[:, sl], axis=-1, keepdims=True)
            dd_h = jnp.max(dd_ref[:, sl], axis=-1, keepdims=True)
            pc = jnp.exp(jnp.where(cur_ok, _dot_nt(q, kc) * ATT_SCALE, NEG) - lse_h)
            pp = jnp.exp(jnp.where(prev_ok, _dot_nt(q, kp) * ATT_SCALE, NEG) - lse_h)
            dsc = pc * (_dot_nt(dy, vc_ref[:, sl]) - dd_h) * ATT_SCALE
            dsp = pp * (_dot_nt(dy, vp_ref[:, sl]) - dd_h) * ATT_SCALE
            dq_ref[:, sl] = _dot(dsc, kc) + _dot(dsp, kp)

    blk = lambda f: pl.BlockSpec((SPAN, B_WIDTH), f)
    cur = lambda r, n: (n, r)
    prev = lambda r, n: (jnp.maximum(n - 1, 0), r)
    v2 = lambda a: a.reshape(L, d * a.shape[1])
    dq = pl.pallas_call(
        body, grid=(d, nb),
        in_specs=[blk(lambda r, n: (n, r * N_GROUPS + gi)), blk(cur), blk(prev),
                  blk(lambda r, n: (n, 2 * r + 1)), blk(lambda r, n: (jnp.maximum(n - 1, 0), 2 * r + 1)),
                  blk(cur), blk(cur), blk(cur)],
        out_specs=blk(cur), out_shape=jax.ShapeDtypeStruct((L, d * B_WIDTH), F32),
        compiler_params=_cp("parallel", "arbitrary"), name=name,
    )(v2(qr), v2(kr), v2(kr), v2(kv), v2(kv), v2(dmm), v2(lse), v2(dd))
    return dq.reshape(T, B_WIDTH)


def _dil_bwd_dkv(qr, kr, kv, dmm, lse, dd, gi, d, *, name):
    T = qr.shape[0]
    L, nb = _dil_views(T, d)

    def body(k_ref, v_ref, q0_ref, q1_ref, dy0_ref, dy1_ref, lse0_ref, lse1_ref, dd0_ref, dd1_ref, dk_ref, dv_ref):
        cur_ok, prev_band = _band_masks()
        next_ok = prev_band & (pl.program_id(1) < nb - 1)
        for h in range(B_HEADS):
            sl = slice(h * HEAD_DIM, (h + 1) * HEAD_DIM)
            k, v = k_ref[:, sl], v_ref[:, sl]
            dk = jnp.zeros((SPAN, HEAD_DIM), F32)
            dv = jnp.zeros((SPAN, HEAD_DIM), F32)
            for ok, q_ref, dy_ref, lse_ref, dd_ref in ((cur_ok, q0_ref, dy0_ref, lse0_ref, dd0_ref),
                                                         (next_ok, q1_ref, dy1_ref, lse1_ref, dd1_ref)):
                q, dy = q_ref[:, sl], dy_ref[:, sl]
                lse_h = jnp.max(lse_ref[:, sl], axis=-1, keepdims=True)
                dd_h = jnp.max(dd_ref[:, sl], axis=-1, keepdims=True)
                p = jnp.exp(jnp.where(ok, _dot_nt(q, k) * ATT_SCALE, NEG) - lse_h)
                ds = p * (_dot_nt(dy, v) - dd_h) * ATT_SCALE
                dk = dk + _dot_tn(ds, q)
                dv = dv + _dot_tn(p, dy)
            dk_ref[:, sl] = dk
            dv_ref[:, sl] = dv

    blk = lambda f: pl.BlockSpec((SPAN, B_WIDTH), f)
    cur = lambda r, n: (n, r)
    nxt = lambda r, n: (jnp.minimum(n + 1, nb - 1), r)
    qcur = lambda r, n: (n, r * N_GROUPS + gi)
    qnxt = lambda r, n: (jnp.minimum(n + 1, nb - 1), r * N_GROUPS + gi)
    v2 = lambda a: a.reshape(L, d * a.shape[1])
    ov = jax.ShapeDtypeStruct((L, d * B_WIDTH), F32)
    dk, dv = pl.pallas_call(
        body, grid=(d, nb),
        in_specs=[blk(cur), blk(lambda r, n: (n, 2 * r + 1)), blk(qcur), blk(qnxt),
                  blk(cur), blk(nxt), blk(cur), blk(nxt), blk(cur), blk(nxt)],
        out_specs=[blk(cur), blk(cur)], out_shape=[ov, ov],
        compiler_params=_cp("parallel", "arbitrary"), name=name,
    )(v2(kr), v2(kv), v2(qr), v2(qr), v2(dmm), v2(dmm), v2(lse), v2(lse), v2(dd), v2(dd))
    return dk.reshape(T, B_WIDTH), dv.reshape(T, B_WIDTH)


A_MQ_COL = 4 * A_WIDTH // MEM_WIDTH
B_MQ_COL = N_GROUPS * B_WIDTH // MEM_WIDTH


def _row(v):
    return v.reshape(1, -1).astype(F32)


def _local_step(x, mem, tgt, W, P):
    T = x.shape[0]
    cosf, sinsg = _rope_tables(T)
    lb_soft = jax.nn.softmax(P["a_lb_logits"].astype(F32), axis=0)
    lb = lb_soft[0:1]
    qw_heads = jnp.repeat(P["b_qnorm"][0], B_HEADS, axis=0).reshape(1, -1)
    kw_heads = jnp.tile(_row(P["b_knorm"]), (1, B_HEADS))
    mqw = [jnp.tile(_row(P["mem_qnorm"][l]), (1, MEM_HEADS)) for l in range(2)]
    mkw = [jnp.tile(_row(P["mem_knorm"][l]), (1, MEM_HEADS)) for l in range(2)]
    nmix = [_row(P["norm_mix"][l]) for l in range(2)]
    nffn = [_row(P["norm_ffn"][l]) for l in range(2)]
    mnorm = [_row(P["mem_norm"][l]) for l in range(2)]
    kvn = _row(P["kv_norm"])
    onorm = _row(P["a_onorm"])
    Wa, Wb, Wkv = W["a_w_in"], W["b_w_in"], W["w_kv"]
    Wmkv, Wout, Wgu, Wd = W["w_mem_kv"], W["w_out"], W["w_gate_up"], W["w_down"]

    proj_a, xn0 = _rms_matmul(x, nmix[0], Wa, tt=512, tn=1664, name="proj_a")
    mkv0, mn0 = _rms_matmul(mem, mnorm[0], Wmkv[0], tt=MEM_TOKENS, tn=2 * MEM_WIDTH, name="mem_kv0")
    o_raw, st = _hgrn2_fwd(proj_a, lb, name="hgrn2_fwd")
    mm0 = _a_post_fwd(o_raw, proj_a, onorm, tt=512, name="a_post_fwd")
    mo0 = _mem_attn_fwd(proj_a, A_MQ_COL, mkv0, mqw[0], mkw[0], tt=512, name="mem_attn_fwd0")
    mix0 = jnp.concatenate([mm0, mo0], axis=1)
    hm0 = _mm_res(x, mix0, Wout[0], tt=512, name="out_proj0")
    gu0, hn0 = _rms_matmul(hm0, nffn[0], Wgu[0], tt=512, tn=1408, name="gate_up0")
    h1 = _swiglu_down(hm0, gu0, Wd[0], tt=256, name="down0")
    kv, hkn = _rms_matmul(h1, kvn, Wkv, tt=512, tn=768, name="kv_proj")
    kr = _headnorm_rope_fwd(kv, kw_heads, cosf, sinsg, col0=0, n_heads=B_HEADS, tt=512, name="k_prep")

    proj_b, xn1 = _rms_matmul(h1, nmix[1], Wb, tt=512, tn=1280, name="proj_b")
    mkv1, mn1 = _rms_matmul(mem, mnorm[1], Wmkv[1], tt=MEM_TOKENS, tn=2 * MEM_WIDTH, name="mem_kv1")
    qr = _headnorm_rope_fwd(proj_b, qw_heads, cosf, sinsg, col0=0, n_heads=N_GROUPS * B_HEADS, tt=512, name="q_prep")
    outs = [_dil_fwd(qr, kr, kv, gi, d, name=f"dil_fwd{gi}") for gi, d in enumerate(DILATIONS)]
    mm1, lse_tot = _dil_combine_fwd([o for o, _ in outs], [s for _, s in outs], tt=512, name="dil_combine")
    mo1 = _mem_attn_fwd(proj_b, B_MQ_COL, mkv1, mqw[1], mkw[1], tt=512, name="mem_attn_fwd1")
    mix1 = jnp.concatenate([mm1, mo1], axis=1)
    hm1 = _mm_res(h1, mix1, Wout[1], tt=512, name="out_proj1")
    gu1, hn1 = _rms_matmul(hm1, nffn[1], Wgu[1], tt=512, tn=1408, name="gate_up1")
    y = _swiglu_down(hm1, gu1, Wd[1], tt=256, name="down1")
    dy, sq = _loss_kernel(y, tgt, tt=512, name="loss")

    gW = {}
    gP = {}
    zeros_mem = jnp.zeros((MEM_TOKENS, D_MODEL), F32)

    def ffn_bwd(l, dh, hm, gu, hn):
        dgu, act = _swiglu_bwd(dh, gu, Wd[l], tt=256, name=f"swiglu_bwd{l}")
        g_wd = _mm_tn(act, dh, tt=512, tn=512, name=f"g_w_down{l}")
        g_wgu = _mm_tn(hn, dgu, tt=512, tn=1408, name=f"g_w_gate_up{l}")
        dhm, g_nf = _rms_bwd_dx(hm, nffn[l], Wgu[l], dgu, dh, tt=256, name=f"gate_up_bwd{l}")
        return dhm, g_wd, g_wgu, g_nf

    def mix_bwd(l, dhm, mix, proj, qcol, mkv, mn):
        dmix = _mm_nt(dhm, Wout[l], tt=512, name=f"out_proj_bwd{l}")
        g_wout = _mm_tn(mix, dhm, tt=512, tn=512, name=f"g_w_out{l}")
        dmq, dmkv, dqw, dkw = _mem_attn_bwd(proj, qcol, mkv, mqw[l], mkw[l], dmix, tt=512, name=f"mem_attn_bwd{l}")
        g_wmkv = _mm_tn(mn, dmkv, tt=MEM_TOKENS, tn=2 * MEM_WIDTH, name=f"g_w_mem_kv{l}")
        _, g_mn = _rms_bwd_dx(mem, mnorm[l], Wmkv[l], dmkv, zeros_mem, tt=MEM_TOKENS, name=f"mem_kv_bwd{l}")
        fold = lambda v: v.reshape(MEM_HEADS, MEM_HEAD_DIM).sum(axis=0)
        return dmix, dmq, g_wout, g_wmkv, g_mn, fold(dqw), fold(dkw)

    dhm1, g_wd1, g_wgu1, g_nf1 = ffn_bwd(1, dy, hm1, gu1, hn1)
    dmix1, dmq1, g_wout1, g_wmkv1, g_mn1, g_mq1, g_mk1 = mix_bwd(1, dhm1, mix1, proj_b, B_MQ_COL, mkv1, mn1)
    dmm, dd = _dil_bwd_prep(dmix1, mm1, tt=512, name="dil_bwd_prep")
    dqs, dks, dvs = [], [], []
    for gi, d in enumerate(DILATIONS):
        dqs.append(_dil_bwd_dq(qr, kr, kv, dmm, lse_tot, dd, gi, d, name=f"dil_bwd_dq{gi}"))
        dk_g, dv_g = _dil_bwd_dkv(qr, kr, kv, dmm, lse_tot, dd, gi, d, name=f"dil_bwd_dkv{gi}")
        dks.append(dk_g)
        dvs.append(dv_g)
    dq_raw, dqw = _q_prep_bwd(proj_b, qw_heads, cosf, sinsg, dqs, tt=512, name="q_prep_bwd")
    dkv, dkw = _kv_prep_bwd(kv, kw_heads, cosf, sinsg, dks, dvs, tt=512, name="kv_prep_bwd")
    dproj_b = jnp.concatenate([dq_raw, dmq1], axis=1)
    gW["b_w_in"] = _mm_tn(xn1, dproj_b, tt=512, tn=1280, name="g_b_w_in")[None]
    dh1, g_nm1 = _rms_bwd_dx(h1, nmix[1], Wb, dproj_b, dhm1, tt=256, name="proj_b_bwd")
    gW["w_kv"] = _mm_tn(hkn, dkv, tt=512, tn=768, name="g_w_kv")
    dh1, g_kvn = _rms_bwd_dx(h1, kvn, Wkv, dkv, dh1, tt=256, name="kv_proj_bwd")

    dhm0, g_wd0, g_wgu0, g_nf0 = ffn_bwd(0, dh1, hm0, gu0, hn0)
    dmix0, dmq0, g_wout0, g_wmkv0, g_mn0, g_mq0, g_mk0 = mix_bwd(0, dhm0, mix0, proj_a, A_MQ_COL, mkv0, mn0)
    do_raw, dg, g_onorm = _a_post_bwd(o_raw, proj_a, onorm, dmix0, tt=512, name="a_post_bwd")
    dq, dz, dv, dlb = _hgrn2_bwd(proj_a, lb, st, do_raw, name="hgrn2_bwd")
    dproj_a = jnp.concatenate([dq, dz, dv, dg, dmq0], axis=1)
    gW["a_w_in"] = _mm_tn(xn0, dproj_a, tt=512, tn=1664, name="g_a_w_in")[None]
    gx, g_nm0 = _rms_bwd_dx(x, nmix[0], Wa, dproj_a, dhm0, tt=256, name="proj_a_bwd")

    gW["w_mem_kv"] = jnp.stack([g_wmkv0, g_wmkv1])
    gW["w_out"] = jnp.stack([g_wout0, g_wout1])
    gW["w_gate_up"] = jnp.stack([g_wgu0, g_wgu1])
    gW["w_down"] = jnp.stack([g_wd0, g_wd1])
    dl0 = lb_soft[0:1] * lb_soft[1:2] * dlb
    gP["a_lb_logits"] = jnp.concatenate([dl0, -dl0], axis=0)
    gP["a_onorm"] = g_onorm
    gP["norm_mix"] = jnp.concatenate([g_nm0, g_nm1], axis=0)
    gP["norm_ffn"] = jnp.concatenate([g_nf0, g_nf1], axis=0)
    gP["b_qnorm"] = dqw.reshape(N_GROUPS, B_HEADS, HEAD_DIM).sum(axis=1)[None]
    gP["kv_norm"] = g_kvn.reshape(-1)
    gP["b_knorm"] = dkw.reshape(B_HEADS, HEAD_DIM).sum(axis=0)
    gP["mem_norm"] = jnp.concatenate([g_mn0, g_mn1], axis=0)
    gP["mem_qnorm"] = jnp.stack([g_mq0, g_mq1])
    gP["mem_knorm"] = jnp.stack([g_mk0, g_mk1])
    return sq, gx, gW, gP


MESH_ID = pl.DeviceIdType.MESH
HBM_SPEC = pl.BlockSpec(memory_space=pltpu.HBM)


def _position():
    return lax.axis_index("x"), lax.axis_index("y"), lax.axis_index("c")


def _all_gather(block, *, name):
    def body(x_ref, out_ref, send_sems, recv_sems, local_sem):
        x, y, c = _position()
        me, sibling = (x, y, c), (x, y, 1 - c)
        chips = [(1 - x, y), (x, 1 - y), (1 - x, 1 - y)]

        def slot(px, py, pc):
            return out_ref.at[4 * px + 2 * py + pc]

        def copy(k, blk, to, src=None):
            return pltpu.make_async_remote_copy(
                src_ref=slot(*blk) if src is None else src, dst_ref=slot(*blk),
                send_sem=send_sems.at[k], recv_sem=recv_sems.at[k], device_id=to, device_id_type=MESH_ID)

        mine = pltpu.make_async_copy(x_ref, slot(*me), local_sem)
        mine.start()
        first = [copy(0, me, sibling, src=x_ref)]
        first += [copy(1 + j, me, (*chip, c), src=x_ref) for j, chip in enumerate(chips)]
        for cp in first:
            cp.start()
        passed = [copy(4 + j, (*chip, c), sibling) for j, chip in enumerate(chips)]
        for j, chip in enumerate(chips):
            copy(1 + j, (*chip, c), me).wait_recv()
            passed[j].start()
        copy(0, sibling, me).wait_recv()
        for j, chip in enumerate(chips):
            copy(4 + j, (*chip, 1 - c), me).wait_recv()
        for cp in first + passed:
            cp.wait_send()
        mine.wait()

    return pl.pallas_call(
        body, out_shape=jax.ShapeDtypeStruct((N_DEV,) + block.shape, block.dtype),
        in_specs=[HBM_SPEC], out_specs=HBM_SPEC,
        scratch_shapes=[pltpu.SemaphoreType.DMA((7,)), pltpu.SemaphoreType.DMA((7,)), pltpu.SemaphoreType.DMA],
        name=name)(block)


def _all_to_all(blocks, *, name):
    def body(x_ref, out_ref, send_sems, recv_sems, local_sem):
        x, y, c = _position()
        me = 4 * x + 2 * y + c
        mine = pltpu.make_async_copy(x_ref.at[me], out_ref.at[me], local_sem)
        mine.start()
        copies = []
        for k in range(1, N_DEV):
            tx = x if not (k >> 2) & 1 else 1 - x
            ty = y if not (k >> 1) & 1 else 1 - y
            tc = c if not k & 1 else 1 - c
            cp = pltpu.make_async_remote_copy(
                src_ref=x_ref.at[4 * tx + 2 * ty + tc], dst_ref=out_ref.at[me],
                send_sem=send_sems.at[k - 1], recv_sem=recv_sems.at[k - 1],
                device_id=(tx, ty, tc), device_id_type=MESH_ID)
            cp.start()
            copies.append(cp)
        for cp in copies:
            cp.wait()
        mine.wait()

    return pl.pallas_call(
        body, out_shape=jax.ShapeDtypeStruct(blocks.shape, blocks.dtype),
        in_specs=[HBM_SPEC], out_specs=HBM_SPEC,
        scratch_shapes=[pltpu.SemaphoreType.DMA((7,)), pltpu.SemaphoreType.DMA((7,)), pltpu.SemaphoreType.DMA],
        name=name)(blocks)


def _sum_sources(parts, *, name):
    n, R, C = parts.shape

    def body(p_ref, o_ref):
        acc = p_ref[0].astype(F32)
        for s in range(1, n):
            acc = acc + p_ref[s].astype(F32)
        o_ref[...] = acc

    return pl.pallas_call(
        body, grid=(1,), in_specs=[_full((n, R, C))], out_specs=_full((R, C)),
        out_shape=jax.ShapeDtypeStruct((R, C), F32), compiler_params=_cp("arbitrary"), name=name)(parts)


def _adamw(parts, w, m, v, *, tr, name):
    n, R, C = parts.shape
    c1 = 1.0 - ADAM_B1 ** ADAM_STEP
    c2 = 1.0 - ADAM_B2 ** ADAM_STEP

    def body(p_ref, w_ref, m_ref, v_ref, g_ref, d_ref, nm_ref, nv_ref):
        g = p_ref[0].astype(F32)
        for s in range(1, n):
            g = g + p_ref[s].astype(F32)
        nm = ADAM_B1 * m_ref[...] + (1.0 - ADAM_B1) * g
        nv = ADAM_B2 * v_ref[...] + (1.0 - ADAM_B2) * (g * g)
        g_ref[...] = g
        nm_ref[...] = nm
        nv_ref[...] = nv
        d_ref[...] = -ADAM_LR * ((nm / c1) / (jnp.sqrt(nv / c2) + ADAM_EPS) + ADAM_WD * w_ref[...])

    blk = pl.BlockSpec((tr, C), lambda i: (i, 0))
    sh = jax.ShapeDtypeStruct((R, C), F32)
    return pl.pallas_call(
        body, grid=(R // tr,), in_specs=[pl.BlockSpec((n, tr, C), lambda i: (0, i, 0)), blk, blk, blk],
        out_specs=[blk] * 4, out_shape=[sh] * 4, compiler_params=_cp("parallel"), name=name)(parts, w, m, v)


BIG = ("a_w_in", "b_w_in", "w_kv", "w_mem_kv", "w_out", "w_gate_up", "w_down")
BIG_LAYOUT = {
    "a_w_in": ((1, 1024, 416), (1, 2, 0, 3), (1, 1024, 3328)),
    "b_w_in": ((1, 1024, 320), (1, 2, 0, 3), (1, 1024, 2560)),
    "w_kv": ((1024, 192), (1, 0, 2), (1024, 1536)),
    "w_mem_kv": ((2, 128, 512), (1, 0, 2, 3), (2, 1024, 512)),
    "w_out": ((2, 128, 1024), (1, 0, 2, 3), (2, 1024, 1024)),
    "w_gate_up": ((2, 1024, 704), (1, 2, 0, 3), (2, 1024, 5632)),
    "w_down": ((2, 352, 1024), (1, 0, 2, 3), (2, 2816, 1024)),
}
PACK_COLS = 1024
PACK_ROWS = 3456
ADAMW_ROWS = 384

SMALL_REPLICATED = ("norm_mix", "norm_ffn", "b_qnorm", "kv_norm", "b_knorm", "mem_norm", "mem_qnorm", "mem_knorm")
SMALL_SHARDED = ("a_lb_logits", "a_onorm")
SMALL_ORDER = SMALL_REPLICATED + SMALL_SHARDED
LANES = 128


def _prod(shape):
    n = 1
    for s in shape:
        n *= s
    return n


def _pack_flat(arrays, rows, cols, dtype):
    flat = jnp.concatenate([a.reshape(-1).astype(dtype) for a in arrays])
    return jnp.pad(flat, (0, rows * cols - flat.shape[0])).reshape(rows, cols)


def _unpack_flat(packed, shapes):
    flat = packed.reshape(-1)
    out, off = [], 0
    for s in shapes:
        out.append(flat[off:off + _prod(s)].reshape(s))
        off += _prod(s)
    return out


def _pack_big(shards, dtype):
    return _pack_flat([shards[n] for n in BIG], PACK_ROWS, PACK_COLS, dtype)


def _unpack_big(packed):
    return dict(zip(BIG, _unpack_flat(packed, [BIG_LAYOUT[n][0] for n in BIG])))


def _assemble_full(gathered):
    flat = gathered.reshape(N_DEV, -1)
    out, off = {}, 0
    for n in BIG:
        shard, perm, full = BIG_LAYOUT[n]
        k = _prod(shard)
        out[n] = flat[:, off:off + k].reshape((N_DEV,) + shard).transpose(perm).reshape(full)
        off += k
    return out


def _split_full(full_grads, dtype):
    parts = []
    for n in BIG:
        shard, perm, full = BIG_LAYOUT[n]
        inv = tuple(perm.index(i) for i in range(len(perm)))
        permuted = tuple(((N_DEV,) + shard)[p] for p in perm)
        parts.append(full_grads[n].reshape(permuted).transpose(inv).reshape(N_DEV, -1).astype(dtype))
    flat = jnp.concatenate(parts, axis=1)
    return jnp.pad(flat, ((0, 0), (0, PACK_ROWS * PACK_COLS - flat.shape[1]))).reshape(N_DEV, PACK_ROWS, PACK_COLS)


def kernel(x, mem, norm_mix, norm_ffn, a_w_in, a_lb_logits, a_onorm, b_w_in, b_qnorm, kv_norm, w_kv, b_knorm, mem_norm, w_mem_kv, mem_qnorm, mem_knorm, w_out, w_gate_up, w_down, loss_target, m_norm_mix, m_norm_ffn, m_a_w_in, m_a_lb_logits, m_a_onorm, m_b_w_in, m_b_qnorm, m_kv_norm, m_w_kv, m_b_knorm, m_mem_norm, m_w_mem_kv, m_mem_qnorm, m_mem_knorm, m_w_out, m_w_gate_up, m_w_down, v_norm_mix, v_norm_ffn, v_a_w_in, v_a_lb_logits, v_a_onorm, v_b_w_in, v_b_qnorm, v_kv_norm, v_w_kv, v_b_knorm, v_mem_norm, v_w_mem_kv, v_mem_qnorm, v_mem_knorm, v_w_out, v_w_gate_up, v_w_down):
    names = ("norm_mix", "norm_ffn", "a_w_in", "a_lb_logits", "a_onorm", "b_w_in", "b_qnorm", "kv_norm", "w_kv", "b_knorm",
             "mem_norm", "w_mem_kv", "mem_qnorm", "mem_knorm", "w_out", "w_gate_up", "w_down")
    w = dict(zip(names, (norm_mix, norm_ffn, a_w_in, a_lb_logits, a_onorm, b_w_in, b_qnorm, kv_norm, w_kv, b_knorm,
                         mem_norm, w_mem_kv, mem_qnorm, mem_knorm, w_out, w_gate_up, w_down)))
    m = dict(zip(names, (m_norm_mix, m_norm_ffn, m_a_w_in, m_a_lb_logits, m_a_onorm, m_b_w_in, m_b_qnorm, m_kv_norm, m_w_kv,
                         m_b_knorm, m_mem_norm, m_w_mem_kv, m_mem_qnorm, m_mem_knorm, m_w_out, m_w_gate_up, m_w_down)))
    v = dict(zip(names, (v_norm_mix, v_norm_ffn, v_a_w_in, v_a_lb_logits, v_a_onorm, v_b_w_in, v_b_qnorm, v_kv_norm, v_w_kv,
                         v_b_knorm, v_mem_norm, v_w_mem_kv, v_mem_qnorm, v_mem_knorm, v_w_out, v_w_gate_up, v_w_down)))

    full = _assemble_full(_all_gather(_pack_big(w, BF16), name="gather_weights"))
    W = {n: (full[n][0] if n in ("a_w_in", "b_w_in") else full[n]) for n in BIG}
    small_in = _all_gather(_pack_flat([a_lb_logits, a_onorm], 8, LANES, F32), name="gather_small").reshape(N_DEV, -1)
    P = {n: w[n] for n in SMALL_REPLICATED}
    P["a_lb_logits"] = small_in[:, :192].reshape(N_DEV, 2, 96).transpose(1, 0, 2).reshape(2, A_WIDTH)
    P["a_onorm"] = small_in[:, 192:288].reshape(1, A_WIDTH)

    sq, gx, gW, gP = _local_step(x[0], mem[0], loss_target[0], W, P)
    loss = lax.psum(0.5 * jnp.sum(sq) / D_MODEL, ("x", "y", "c"))

    received = _all_to_all(_split_full(gW, BF16), name="scatter_grads")
    g_p, d_p, nm_p, nv_p = _adamw(received, _pack_big(w, F32), _pack_big(m, F32), _pack_big(v, F32),
                                  tr=ADAMW_ROWS, name="adamw_big")
    out = {"grad": _unpack_big(g_p), "delta": _unpack_big(d_p), "new_m": _unpack_big(nm_p), "new_v": _unpack_big(nv_p)}

    full_shapes = [(2, A_WIDTH) if n == "a_lb_logits" else (1, A_WIDTH) if n == "a_onorm" else w[n].shape for n in SMALL_ORDER]
    n_small = sum(_prod(s) for s in full_shapes)
    rows_small = -(-n_small // (8 * LANES)) * 8
    g_all = _all_gather(_pack_flat([gP[n] for n in SMALL_ORDER], rows_small, LANES, F32), name="gather_small_grads")
    g_small = dict(zip(SMALL_ORDER, _unpack_flat(_sum_sources(g_all, name="sum_small_grads"), full_shapes)))
    me = 4 * lax.axis_index("x") + 2 * lax.axis_index("y") + lax.axis_index("c")
    for n in SMALL_SHARDED:
        g_small[n] = lax.dynamic_slice_in_dim(g_small[n], me * 96, 96, axis=1)
    shapes = [w[n].shape for n in SMALL_ORDER]
    rows_upd = -(-sum(_prod(s) for s in shapes) // (8 * LANES)) * 8
    pk = lambda d: _pack_flat([d[n] for n in SMALL_ORDER], rows_upd, LANES, F32)
    res = _adamw(pk(g_small)[None], pk(w), pk(m), pk(v), tr=rows_upd, name="adamw_small")
    for kind, packed in zip(("grad", "delta", "new_m", "new_v"), res):
        out[kind].update(zip(SMALL_ORDER, _unpack_flat(packed, shapes)))

    return (loss, gx[None], *[out["grad"][n] for n in names], *[out["delta"][n] for n in names],
            *[out["new_m"][n] for n in names], *[out["new_v"][n] for n in names])
```

```python
import functools

import jax
import jax.numpy as jnp
from jax import lax
from jax.experimental import pallas as pl
from jax.experimental.pallas import tpu as pltpu

F32 = jnp.float32
BF16 = jnp.bfloat16

N_DEV = 8
D_MODEL = 1024
HEAD_DIM = 128
A_HEADS = 6
A_WIDTH = A_HEADS * HEAD_DIM
CHUNK = 64
B_HEADS = 6
B_WIDTH = B_HEADS * HEAD_DIM
DILATIONS = (1, 4, 16)
SPAN = 128
N_GROUPS = 3
ROPE_THETA = 10000.0
MEM_TOKENS = 256
MEM_HEADS = 4
MEM_HEAD_DIM = 64
MEM_WIDTH = MEM_HEADS * MEM_HEAD_DIM
FFN_HIDDEN = 2816
EPS = 1e-6

ADAM_LR = 0.001
ADAM_B1 = 0.9
ADAM_B2 = 0.999
ADAM_EPS = 1e-08
ADAM_WD = 0.01
ADAM_STEP = 10

V7X_VMEM_LIMIT_BYTES = 56 * 1024 * 1024

NT_DIMS = (((1,), (1,)), ((), ()))
TN_DIMS = (((0,), (0,)), ((), ()))


def _cp(*sem):
    return pltpu.CompilerParams(dimension_semantics=sem, vmem_limit_bytes=V7X_VMEM_LIMIT_BYTES)


def _dot(a, b):
    return jnp.dot(a.astype(BF16), b.astype(BF16), preferred_element_type=F32)


def _dot_nt(a, b):
    return lax.dot_general(a.astype(BF16), b.astype(BF16), NT_DIMS, preferred_element_type=F32)


def _dot_tn(a, b):
    return lax.dot_general(a.astype(BF16), b.astype(BF16), TN_DIMS, preferred_element_type=F32)


def _dot3(m01, x):
    hi = x.astype(BF16)
    r1 = x - hi.astype(F32)
    mid = r1.astype(BF16)
    lo = (r1 - mid.astype(F32)).astype(BF16)
    d = functools.partial(jnp.dot, preferred_element_type=F32)
    return d(m01, hi) + d(m01, mid) + d(m01, lo)


def _sigmoid(x):
    return 1.0 / (1.0 + jnp.exp(-x))


def _full(shape):
    return pl.BlockSpec(shape, lambda *_: (0,) * len(shape))


def _rms_matmul(x, g, w, *, tt, tn, wt, name):
    T, K = x.shape
    N = w.shape[0] if wt else w.shape[1]

    def body(x_ref, g_ref, w_ref, y_ref, xn_ref):
        @pl.when(pl.program_id(1) == 0)
        def _():
            xf = x_ref[...]
            r = lax.rsqrt(jnp.mean(xf * xf, axis=-1, keepdims=True) + EPS)
            xn_ref[...] = (xf * r * g_ref[...]).astype(BF16)

        y_ref[...] = (_dot_nt if wt else _dot)(xn_ref[...], w_ref[...])

    w_spec = pl.BlockSpec((tn, K), lambda i, j: (j, 0)) if wt else pl.BlockSpec((K, tn), lambda i, j: (0, j))
    return pl.pallas_call(
        body, grid=(T // tt, N // tn),
        in_specs=[pl.BlockSpec((tt, K), lambda i, j: (i, 0)), _full((1, K)), w_spec],
        out_specs=[pl.BlockSpec((tt, tn), lambda i, j: (i, j)), pl.BlockSpec((tt, K), lambda i, j: (i, 0))],
        out_shape=[jax.ShapeDtypeStruct((T, N), F32), jax.ShapeDtypeStruct((T, K), BF16)],
        compiler_params=_cp("parallel", "arbitrary"), name=name)(x, g, w)


def _mm_res(res, a, w, *, tt, name):
    T, K = a.shape
    N = w.shape[1]

    def body(r_ref, a_ref, w_ref, o_ref):
        o_ref[...] = r_ref[...] + _dot(a_ref[...], w_ref[...])

    return pl.pallas_call(
        body, grid=(T // tt,),
        in_specs=[pl.BlockSpec((tt, N), lambda i: (i, 0)), pl.BlockSpec((tt, K), lambda i: (i, 0)), _full((K, N))],
        out_specs=pl.BlockSpec((tt, N), lambda i: (i, 0)),
        out_shape=jax.ShapeDtypeStruct((T, N), F32),
        compiler_params=_cp("parallel"), name=name)(res, a, w)


def _swiglu_down(h, gu, wd, *, tt, name):
    T, D = h.shape
    Fh = wd.shape[0]

    def body(h_ref, gt_ref, up_ref, w_ref, o_ref):
        gt = gt_ref[...]
        act = gt * _sigmoid(gt) * up_ref[...]
        o_ref[...] = h_ref[...] + _dot(act, w_ref[...])

    return pl.pallas_call(
        body, grid=(T // tt,),
        in_specs=[pl.BlockSpec((tt, D), lambda i: (i, 0)), pl.BlockSpec((tt, Fh), lambda i: (i, 0)),
                  pl.BlockSpec((tt, Fh), lambda i: (i, 1)), _full((Fh, D))],
        out_specs=pl.BlockSpec((tt, D), lambda i: (i, 0)),
        out_shape=jax.ShapeDtypeStruct((T, D), F32),
        compiler_params=_cp("parallel"), name=name)(h, gu, gu, wd)


def _swiglu_bwd(dh, gu, wd, *, tt, name):
    T, D = dh.shape
    Fh = wd.shape[0]

    def body(dh_ref, gt_ref, up_ref, w_ref, dgu_ref, act_ref):
        gt = gt_ref[...]
        up = up_ref[...]
        s = _sigmoid(gt)
        silu = gt * s
        dact = _dot_nt(dh_ref[...], w_ref[...])
        act_ref[...] = (silu * up).astype(BF16)
        dgu_ref[:, :Fh] = (dact * up * (s * (1.0 + gt * (1.0 - s)))).astype(BF16)
        dgu_ref[:, Fh:] = (dact * silu).astype(BF16)

    return pl.pallas_call(
        body, grid=(T // tt,),
        in_specs=[pl.BlockSpec((tt, D), lambda i: (i, 0)), pl.BlockSpec((tt, Fh), lambda i: (i, 0)),
                  pl.BlockSpec((tt, Fh), lambda i: (i, 1)), _full((Fh, D))],
        out_specs=[pl.BlockSpec((tt, 2 * Fh), lambda i: (i, 0)), pl.BlockSpec((tt, Fh), lambda i: (i, 0))],
        out_shape=[jax.ShapeDtypeStruct((T, 2 * Fh), BF16), jax.ShapeDtypeStruct((T, Fh), BF16)],
        compiler_params=_cp("parallel"), name=name)(dh, gu, gu, wd)


def _mm_nt(a, w, *, tt, name):
    T, N = a.shape
    K = w.shape[0]

    def body(a_ref, w_ref, o_ref):
        o_ref[...] = _dot_nt(a_ref[...], w_ref[...])

    return pl.pallas_call(
        body, grid=(T // tt,),
        in_specs=[pl.BlockSpec((tt, N), lambda i: (i, 0)), _full((K, N))],
        out_specs=pl.BlockSpec((tt, K), lambda i: (i, 0)),
        out_shape=jax.ShapeDtypeStruct((T, K), F32),
        compiler_params=_cp("parallel"), name=name)(a, w)


def _mm_tn(a, b, *, tt, tka, name):
    T, Ka = a.shape
    N = b.shape[1]
    last = T // tt - 1

    def body(a_ref, b_ref, o_ref, acc):
        @pl.when(pl.program_id(1) == 0)
        def _():
            acc[...] = jnp.zeros_like(acc)

        acc[...] += _dot_tn(a_ref[...], b_ref[...])

        @pl.when(pl.program_id(1) == last)
        def _():
            o_ref[...] = acc[...].astype(BF16)

    return pl.pallas_call(
        body, grid=(Ka // tka, T // tt),
        in_specs=[pl.BlockSpec((tt, tka), lambda j, t: (t, j)), pl.BlockSpec((tt, N), lambda j, t: (t, 0))],
        out_specs=pl.BlockSpec((tka, N), lambda j, t: (j, 0)),
        out_shape=jax.ShapeDtypeStruct((Ka, N), BF16),
        scratch_shapes=[pltpu.VMEM((tka, N), F32)],
        compiler_params=_cp("parallel", "arbitrary"), name=name)(a, b)


def _rms_bwd_dx(x, g, w, dy, dres, *, tt, wt, name):
    T, K = x.shape
    N = w.shape[0] if wt else w.shape[1]

    def body(x_ref, g_ref, w_ref, dy_ref, dres_ref, dx_ref, dg_ref):
        @pl.when(pl.program_id(0) == 0)
        def _():
            dg_ref[...] = jnp.zeros_like(dg_ref)

        dxn = (_dot if wt else _dot_nt)(dy_ref[...], w_ref[...])
        xf = x_ref[...]
        r = lax.rsqrt(jnp.mean(xf * xf, axis=-1, keepdims=True) + EPS)
        xhat = xf * r
        dg_ref[...] += jnp.sum(dxn * xhat, axis=0, keepdims=True)
        dxhat = dxn * g_ref[...]
        dx_ref[...] = dres_ref[...] + r * (dxhat - xhat * jnp.mean(dxhat * xhat, axis=-1, keepdims=True))

    return pl.pallas_call(
        body, grid=(T // tt,),
        in_specs=[pl.BlockSpec((tt, K), lambda i: (i, 0)), _full((1, K)), _full(w.shape),
                  pl.BlockSpec((tt, N), lambda i: (i, 0)), pl.BlockSpec((tt, K), lambda i: (i, 0))],
        out_specs=[pl.BlockSpec((tt, K), lambda i: (i, 0)), _full((1, K))],
        out_shape=[jax.ShapeDtypeStruct((T, K), F32), jax.ShapeDtypeStruct((1, K), F32)],
        compiler_params=_cp("arbitrary"), name=name)(x, g, w, dy, dres)


def _loss_kernel(y, tgt, *, tt, name):
    T, D = y.shape

    def body(y_ref, t_ref, dy_ref, acc_ref):
        @pl.when(pl.program_id(0) == 0)
        def _():
            acc_ref[...] = jnp.zeros_like(acc_ref)

        e = y_ref[...] - t_ref[...]
        dy_ref[...] = e * (1.0 / D)
        acc_ref[...] += jnp.sum(e * e, axis=0, keepdims=True)

    return pl.pallas_call(
        body, grid=(T // tt,),
        in_specs=[pl.BlockSpec((tt, D), lambda i: (i, 0)), pl.BlockSpec((tt, D), lambda i: (i, 0))],
        out_specs=[pl.BlockSpec((tt, D), lambda i: (i, 0)), _full((1, D))],
        out_shape=[jax.ShapeDtypeStruct((T, D), F32), jax.ShapeDtypeStruct((1, D), F32)],
        compiler_params=_cp("arbitrary"), name=name)(y, tgt)


HGRN_TB = 512
HGRN_NCH = HGRN_TB // CHUNK


def _hgrn_chunk_fwd(q, z, lbv, tril01):
    sig = _sigmoid(z)
    f = lbv + (1.0 - lbv) * sig
    kk = 1.0 - f
    b = _dot3(tril01, jnp.log(f))
    bend = b[CHUNK - 1:CHUNK, :]
    sq = _sigmoid(q)
    eb = jnp.exp(b)
    emb = jnp.exp(-b)
    eo = jnp.exp(bend - b)
    dec = jnp.exp(bend)
    return sig, f, kk, sq, eb, emb, eo, dec


def _hgrn2_fwd(proj, lb, *, name):
    T = proj.shape[0]
    nT = T // HGRN_TB
    nC = T // CHUNK

    def body(q_ref, z_ref, v_ref, lb_ref, o_ref, st_ref, state):
        @pl.when(pl.program_id(1) == 0)
        def _():
            state[...] = jnp.zeros_like(state)

        row = lax.broadcasted_iota(jnp.int32, (CHUNK, CHUNK), 0)
        col = lax.broadcasted_iota(jnp.int32, (CHUNK, CHUNK), 1)
        causal = row >= col
        tril01 = causal.astype(BF16)
        lbv = lb_ref[...]

        def chunk(c, carry):
            rows = pl.ds(pl.multiple_of(c * CHUNK, CHUNK), CHUNK)
            q = q_ref[rows, :]
            v = v_ref[rows, :].astype(BF16)
            sig, f, kk, sq, eb, emb, eo, dec = _hgrn_chunk_fwd(q, z_ref[rows, :], lbv, tril01)
            qi = (q * sq * eb).astype(BF16)
            ki = (kk * emb).astype(BF16)
            ko = (kk * eo).astype(BF16)
            st = state[...]
            att = jnp.where(causal, _dot_nt(qi, ki), 0.0)
            o_ref[rows, :] = _dot(att, v) + _dot_nt(qi, st)
            st_ref[c, 0] = st
            state[...] = st * dec + _dot_tn(v, ko)
            return carry

        lax.fori_loop(0, HGRN_NCH, chunk, 0)

    hb = lambda off: pl.BlockSpec((HGRN_TB, HEAD_DIM), lambda h, i: (i, off + h))
    return pl.pallas_call(
        body, grid=(A_HEADS, nT),
        in_specs=[hb(0), hb(A_HEADS), hb(2 * A_HEADS), pl.BlockSpec((1, HEAD_DIM), lambda h, i: (0, h))],
        out_specs=[hb(0), pl.BlockSpec((HGRN_NCH, 1, HEAD_DIM, HEAD_DIM), lambda h, i: (i, h, 0, 0))],
        out_shape=[jax.ShapeDtypeStruct((T, A_WIDTH), F32), jax.ShapeDtypeStruct((nC, A_HEADS, HEAD_DIM, HEAD_DIM), F32)],
        scratch_shapes=[pltpu.VMEM((HEAD_DIM, HEAD_DIM), F32)],
        compiler_params=_cp("parallel", "arbitrary"), name=name)(proj, proj, proj, lb)


def _hgrn2_bwd(proj, lb, st_all, do, *, name):
    T = proj.shape[0]
    nT = T // HGRN_TB

    def body(q_ref, z_ref, v_ref, lb_ref, st_ref, do_ref, dq_ref, dz_ref, dv_ref, dlb_ref, dstate):
        @pl.when(pl.program_id(1) == 0)
        def _():
            dstate[...] = jnp.zeros_like(dstate)
            dlb_ref[...] = jnp.zeros_like(dlb_ref)

        row = lax.broadcasted_iota(jnp.int32, (CHUNK, CHUNK), 0)
        col = lax.broadcasted_iota(jnp.int32, (CHUNK, CHUNK), 1)
        causal = row >= col
        tril01 = causal.astype(BF16)
        triu01 = (row <= col).astype(BF16)
        lbv = lb_ref[...]

        def chunk(cc, carry):
            c = HGRN_NCH - 1 - cc
            rows = pl.ds(pl.multiple_of(c * CHUNK, CHUNK), CHUNK)
            q = q_ref[rows, :]
            v = v_ref[rows, :].astype(BF16)
            sig, f, kk, sq, eb, emb, eo, dec = _hgrn_chunk_fwd(q, z_ref[rows, :], lbv, tril01)
            qi32 = q * sq * eb
            ki32 = kk * emb
            ko32 = kk * eo
            qi, ki, ko = qi32.astype(BF16), ki32.astype(BF16), ko32.astype(BF16)
            att = jnp.where(causal, _dot_nt(qi, ki), 0.0).astype(BF16)
            dout = do_ref[rows, :].astype(BF16)
            st = st_ref[c, 0]
            dst = dstate[...]
            dst16 = dst.astype(BF16)
            datt = jnp.where(causal, _dot_nt(dout, v), 0.0).astype(BF16)
            dqi = _dot(datt, ki) + _dot(dout, st)
            dki = _dot_tn(datt, qi)
            dv_ref[rows, :] = (_dot_tn(att, dout) + _dot_nt(ko, dst16)).astype(BF16)
            dko = _dot(v, dst16)
            ddec = jnp.sum(dst * st, axis=0, keepdims=True)
            dstate[...] = dst * dec + _dot_tn(dout, qi)
            dkk = dki * emb + dko * eo
            db = dqi * qi32 - dki * ki32 - dko * ko32
            dbend = jnp.sum(dko * ko32, axis=0, keepdims=True) + ddec * dec
            dlogf = _dot3(triu01, db) + dbend
            df = dlogf / f - dkk
            dz_ref[rows, :] = (df * (1.0 - lbv) * sig * (1.0 - sig)).astype(BF16)
            dlb_ref[...] += jnp.sum(df * (1.0 - sig), axis=0, keepdims=True)
            dq_ref[rows, :] = (dqi * eb * (sq * (1.0 + q * (1.0 - sq)))).astype(BF16)
            return carry

        lax.fori_loop(0, HGRN_NCH, chunk, 0)

    hb = lambda off: pl.BlockSpec((HGRN_TB, HEAD_DIM), lambda h, i: (nT - 1 - i, off + h))
    hlb = pl.BlockSpec((1, HEAD_DIM), lambda h, i: (0, h))
    o16 = jax.ShapeDtypeStruct((T, A_WIDTH), BF16)
    return pl.pallas_call(
        body, grid=(A_HEADS, nT),
        in_specs=[hb(0), hb(A_HEADS), hb(2 * A_HEADS), hlb,
                  pl.BlockSpec((HGRN_NCH, 1, HEAD_DIM, HEAD_DIM), lambda h, i: (nT - 1 - i, h, 0, 0)), hb(0)],
        out_specs=[hb(0), hb(0), hb(0), hlb],
        out_shape=[o16, o16, o16, jax.ShapeDtypeStruct((1, A_WIDTH), F32)],
        scratch_shapes=[pltpu.VMEM((HEAD_DIM, HEAD_DIM), F32)],
        compiler_params=_cp("parallel", "arbitrary"), name=name)(proj, proj, proj, lb, st_all, do)


def _head_rms(x):
    r = lax.rsqrt(jnp.mean(x * x, axis=-1, keepdims=True) + EPS)
    return x * r, r


def _head_rms_bwd(dxhat, xhat, r):
    return r * (dxhat - xhat * jnp.mean(dxhat * xhat, axis=-1, keepdims=True))


def _a_post_fwd(o, proj, onorm, *, tt, name):
    T = o.shape[0]

    def body(o_ref, g_ref, w_ref, y_ref):
        for h in range(A_HEADS):
            sl = slice(h * HEAD_DIM, (h + 1) * HEAD_DIM)
            xhat, _ = _head_rms(o_ref[:, sl])
            g = g_ref[:, sl]
            y_ref[:, sl] = xhat * w_ref[:, sl] * (g * _sigmoid(g))

    blk = lambda c: pl.BlockSpec((tt, A_WIDTH), lambda i: (i, c))
    return pl.pallas_call(
        body, grid=(T // tt,), in_specs=[blk(0), blk(3), _full((1, A_WIDTH))], out_specs=blk(0),
        out_shape=jax.ShapeDtypeStruct((T, A_WIDTH), F32),
        compiler_params=_cp("parallel"), name=name)(o, proj, onorm)


def _a_post_bwd(o, proj, onorm, dmix, *, tt, name):
    T = o.shape[0]

    def body(o_ref, g_ref, w_ref, dy_ref, do_ref, dg_ref, dw_ref):
        @pl.when(pl.program_id(0) == 0)
        def _():
            dw_ref[...] = jnp.zeros_like(dw_ref)

        for h in range(A_HEADS):
            sl = slice(h * HEAD_DIM, (h + 1) * HEAD_DIM)
            xhat, r = _head_rms(o_ref[:, sl])
            g = g_ref[:, sl]
            s = _sigmoid(g)
            dy = dy_ref[:, sl]
            w = w_ref[:, sl]
            dg_ref[:, sl] = (dy * xhat * w * (s * (1.0 + g * (1.0 - s)))).astype(BF16)
            dyn = dy * (g * s)
            dw_ref[:, sl] += jnp.sum(dyn * xhat, axis=0, keepdims=True)
            do_ref[:, sl] = _head_rms_bwd(dyn * w, xhat, r)

    blk = lambda c: pl.BlockSpec((tt, A_WIDTH), lambda i: (i, c))
    return pl.pallas_call(
        body, grid=(T // tt,), in_specs=[blk(0), blk(3), _full((1, A_WIDTH)), blk(0)],
        out_specs=[blk(0), blk(0), _full((1, A_WIDTH))],
        out_shape=[jax.ShapeDtypeStruct((T, A_WIDTH), F32), jax.ShapeDtypeStruct((T, A_WIDTH), BF16),
                   jax.ShapeDtypeStruct((1, A_WIDTH), F32)],
        compiler_params=_cp("arbitrary"), name=name)(o, proj, onorm, dmix)


def _mem_head_masks(n):
    lane = lax.broadcasted_iota(jnp.int32, (n, MEM_WIDTH), 1)
    return [(lane >= m * MEM_HEAD_DIM) & (lane < (m + 1) * MEM_HEAD_DIM) for m in range(MEM_HEADS)]


def _mem_head_rms(x, masks):
    x2 = x * x
    r = jnp.zeros_like(x)
    for mk in masks:
        ms = jnp.sum(jnp.where(mk, x2, 0.0), axis=-1, keepdims=True) * (1.0 / MEM_HEAD_DIM)
        r = jnp.where(mk, lax.rsqrt(ms + EPS), r)
    return x * r, r


def _mem_head_rms_bwd(dxhat, xhat, r, masks):
    t = dxhat * xhat
    m = jnp.zeros_like(t)
    for mk in masks:
        m = jnp.where(mk, jnp.sum(jnp.where(mk, t, 0.0), axis=-1, keepdims=True) * (1.0 / MEM_HEAD_DIM), m)
    return r * (dxhat - xhat * m)


MEM_SCALE = MEM_HEAD_DIM ** -0.5


def _mem_attn_fwd(proj, qcol, mkv, qn_w, kn_w, *, tt, name):
    T = proj.shape[0]

    def body(q_ref, k_ref, v_ref, qw_ref, kw_ref, o_ref):
        qmasks = _mem_head_masks(tt)
        kmasks = _mem_head_masks(MEM_TOKENS)
        qhat, _ = _mem_head_rms(q_ref[...], qmasks)
        qn = qhat * qw_ref[...]
        khat, _ = _mem_head_rms(k_ref[...], kmasks)
        kn = (khat * kw_ref[...]).astype(BF16)
        v = v_ref[...].astype(BF16)
        out = jnp.zeros((tt, MEM_WIDTH), F32)
        for m in range(MEM_HEADS):
            s = _dot_nt(jnp.where(qmasks[m], qn, 0.0), kn) * MEM_SCALE
            s = s - jnp.max(s, axis=-1, keepdims=True)
            p = jnp.exp(s)
            p = p / jnp.sum(p, axis=-1, keepdims=True)
            out = jnp.where(qmasks[m], _dot(p, v), out)
        o_ref[...] = out

    return pl.pallas_call(
        body, grid=(T // tt,),
        in_specs=[pl.BlockSpec((tt, MEM_WIDTH), lambda i: (i, qcol)), pl.BlockSpec((MEM_TOKENS, MEM_WIDTH), lambda i: (0, 0)),
                  pl.BlockSpec((MEM_TOKENS, MEM_WIDTH), lambda i: (0, 1)), _full((1, MEM_WIDTH)), _full((1, MEM_WIDTH))],
        out_specs=pl.BlockSpec((tt, MEM_WIDTH), lambda i: (i, 0)),
        out_shape=jax.ShapeDtypeStruct((T, MEM_WIDTH), F32),
        compiler_params=_cp("parallel"), name=name)(proj, mkv, mkv, qn_w, kn_w)


def _mem_attn_bwd(proj, qcol, mkv, qn_w, kn_w, dmix, *, tt, name):
    T = proj.shape[0]
    nsteps = T // tt
    ocol = (dmix.shape[1] - MEM_WIDTH) // MEM_WIDTH

    def body(q_ref, k_ref, v_ref, qw_ref, kw_ref, do_ref, dq_ref, dkv_ref, dqw_ref, dkw_ref, dk_acc, dv_acc):
        step = pl.program_id(0)

        @pl.when(step == 0)
        def _():
            dk_acc[...] = jnp.zeros_like(dk_acc)
            dv_acc[...] = jnp.zeros_like(dv_acc)
            dqw_ref[...] = jnp.zeros_like(dqw_ref)

        qmasks = _mem_head_masks(tt)
        kmasks = _mem_head_masks(MEM_TOKENS)
        qhat, qr = _mem_head_rms(q_ref[...], qmasks)
        qn = qhat * qw_ref[...]
        khat, kr = _mem_head_rms(k_ref[...], kmasks)
        kn = (khat * kw_ref[...]).astype(BF16)
        v = v_ref[...].astype(BF16)
        dout = do_ref[...]
        dqn = jnp.zeros((tt, MEM_WIDTH), F32)
        dkn = jnp.zeros((MEM_TOKENS, MEM_WIDTH), F32)
        dvv = jnp.zeros((MEM_TOKENS, MEM_WIDTH), F32)
        for m in range(MEM_HEADS):
            qm = jnp.where(qmasks[m], qn, 0.0).astype(BF16)
            s = _dot_nt(qm, kn) * MEM_SCALE
            s = s - jnp.max(s, axis=-1, keepdims=True)
            p = jnp.exp(s)
            p = p / jnp.sum(p, axis=-1, keepdims=True)
            dom = jnp.where(qmasks[m], dout, 0.0).astype(BF16)
            dp = _dot_nt(dom, v)
            ds = (p * (dp - jnp.sum(p * dp, axis=-1, keepdims=True)) * MEM_SCALE).astype(BF16)
            dqn = jnp.where(qmasks[m], _dot(ds, kn), dqn)
            dkn = jnp.where(kmasks[m], _dot_tn(ds, qm), dkn)
            dvv = jnp.where(kmasks[m], _dot_tn(p, dom), dvv)
        dqw_ref[...] += jnp.sum(dqn * qhat, axis=0, keepdims=True)
        dq_ref[...] = _mem_head_rms_bwd(dqn * qw_ref[...], qhat, qr, qmasks).astype(BF16)
        dk_acc[...] += dkn
        dv_acc[...] += dvv

        @pl.when(step == nsteps - 1)
        def _():
            dk = dk_acc[...]
            dkw_ref[...] = jnp.sum(dk * khat, axis=0, keepdims=True)
            dkv_ref[:, :MEM_WIDTH] = _mem_head_rms_bwd(dk * kw_ref[...], khat, kr, kmasks)
            dkv_ref[:, MEM_WIDTH:] = dv_acc[...]

    return pl.pallas_call(
        body, grid=(nsteps,),
        in_specs=[pl.BlockSpec((tt, MEM_WIDTH), lambda i: (i, qcol)), pl.BlockSpec((MEM_TOKENS, MEM_WIDTH), lambda i: (0, 0)),
                  pl.BlockSpec((MEM_TOKENS, MEM_WIDTH), lambda i: (0, 1)), _full((1, MEM_WIDTH)), _full((1, MEM_WIDTH)),
                  pl.BlockSpec((tt, MEM_WIDTH), lambda i: (i, ocol))],
        out_specs=[pl.BlockSpec((tt, MEM_WIDTH), lambda i: (i, 0)), _full((MEM_TOKENS, 2 * MEM_WIDTH)),
                   _full((1, MEM_WIDTH)), _full((1, MEM_WIDTH))],
        out_shape=[jax.ShapeDtypeStruct((T, MEM_WIDTH), BF16), jax.ShapeDtypeStruct((MEM_TOKENS, 2 * MEM_WIDTH), F32),
                   jax.ShapeDtypeStruct((1, MEM_WIDTH), F32), jax.ShapeDtypeStruct((1, MEM_WIDTH), F32)],
        scratch_shapes=[pltpu.VMEM((MEM_TOKENS, MEM_WIDTH), F32), pltpu.VMEM((MEM_TOKENS, MEM_WIDTH), F32)],
        compiler_params=_cp("arbitrary"), name=name)(proj, mkv, mkv, qn_w, kn_w, dmix)


HALF = HEAD_DIM // 2
ATT_SCALE = HEAD_DIM ** -0.5
NEG = -1e30


def _rope_tables(T):
    inv = ROPE_THETA ** (-jnp.arange(HALF, dtype=F32) / HALF)
    ang = jnp.arange(T, dtype=F32)[:, None] * inv[None, :]
    cos, sin = jnp.cos(ang), jnp.sin(ang)
    return jnp.concatenate([cos, cos], axis=-1), jnp.concatenate([-sin, sin], axis=-1)


def _rope(x, cosf, sinsg):
    return x * cosf + pltpu.roll(x, HALF, 1) * sinsg


def _rope_bwd(dy, cosf, sinsg):
    return dy * cosf + pltpu.roll(dy * sinsg, HALF, 1)


def _headnorm_rope_fwd(x, w_heads, cosf, sinsg, *, col0, n_heads, tt, name):
    T = x.shape[0]
    W = n_heads * HEAD_DIM

    def body(x_ref, w_ref, c_ref, s_ref, y_ref):
        c, s = c_ref[...], s_ref[...]
        for h in range(n_heads):
            sl = slice(h * HEAD_DIM, (h + 1) * HEAD_DIM)
            xhat, _ = _head_rms(x_ref[:, sl])
            y_ref[:, sl] = _rope(xhat * w_ref[:, sl], c, s).astype(BF16)

    tbl = pl.BlockSpec((tt, HEAD_DIM), lambda i: (i, 0))
    return pl.pallas_call(
        body, grid=(T // tt,),
        in_specs=[pl.BlockSpec((tt, W), lambda i: (i, col0)), _full((1, W)), tbl, tbl],
        out_specs=pl.BlockSpec((tt, W), lambda i: (i, 0)),
        out_shape=jax.ShapeDtypeStruct((T, W), BF16),
        compiler_params=_cp("parallel"), name=name)(x, w_heads, cosf, sinsg)


def _q_prep_bwd(proj, w_heads, cosf, sinsg, dqs, *, tt, name):
    T = proj.shape[0]
    W = N_GROUPS * B_WIDTH

    def body(x_ref, w_ref, c_ref, s_ref, d0, d1, d2, dx_ref, dw_ref):
        @pl.when(pl.program_id(0) == 0)
        def _():
            dw_ref[...] = jnp.zeros_like(dw_ref)

        c, s = c_ref[...], s_ref[...]
        for gi, d_ref in enumerate((d0, d1, d2)):
            for h in range(B_HEADS):
                sl = slice((gi * B_HEADS + h) * HEAD_DIM, (gi * B_HEADS + h + 1) * HEAD_DIM)
                xhat, r = _head_rms(x_ref[:, sl])
                dyn = _rope_bwd(d_ref[:, h * HEAD_DIM:(h + 1) * HEAD_DIM], c, s)
                dw_ref[:, sl] += jnp.sum(dyn * xhat, axis=0, keepdims=True)
                dx_ref[:, sl] = _head_rms_bwd(dyn * w_ref[:, sl], xhat, r).astype(BF16)

    tbl = pl.BlockSpec((tt, HEAD_DIM), lambda i: (i, 0))
    dyb = pl.BlockSpec((tt, B_WIDTH), lambda i: (i, 0))
    return pl.pallas_call(
        body, grid=(T // tt,),
        in_specs=[pl.BlockSpec((tt, W), lambda i: (i, 0)), _full((1, W)), tbl, tbl, dyb, dyb, dyb],
        out_specs=[pl.BlockSpec((tt, W), lambda i: (i, 0)), _full((1, W))],
        out_shape=[jax.ShapeDtypeStruct((T, W), BF16), jax.ShapeDtypeStruct((1, W), F32)],
        compiler_params=_cp("arbitrary"), name=name)(proj, w_heads, cosf, sinsg, *dqs)


def _kv_prep_bwd(kv, w_heads, cosf, sinsg, dks, dvs, *, tt, name):
    T = kv.shape[0]

    def body(x_ref, w_ref, c_ref, s_ref, k0, k1, k2, v0, v1, v2, dx_ref, dw_ref):
        @pl.when(pl.program_id(0) == 0)
        def _():
            dw_ref[...] = jnp.zeros_like(dw_ref)

        c, s = c_ref[...], s_ref[...]
        for h in range(B_HEADS):
            sl = slice(h * HEAD_DIM, (h + 1) * HEAD_DIM)
            vs = slice(B_WIDTH + h * HEAD_DIM, B_WIDTH + (h + 1) * HEAD_DIM)
            xhat, r = _head_rms(x_ref[:, sl])
            dyn = _rope_bwd(k0[:, sl] + k1[:, sl] + k2[:, sl], c, s)
            dw_ref[:, sl] += jnp.sum(dyn * xhat, axis=0, keepdims=True)
            dx_ref[:, sl] = _head_rms_bwd(dyn * w_ref[:, sl], xhat, r).astype(BF16)
            dx_ref[:, vs] = (v0[:, sl] + v1[:, sl] + v2[:, sl]).astype(BF16)

    tbl = pl.BlockSpec((tt, HEAD_DIM), lambda i: (i, 0))
    dyb = pl.BlockSpec((tt, B_WIDTH), lambda i: (i, 0))
    return pl.pallas_call(
        body, grid=(T // tt,),
        in_specs=[dyb, _full((1, B_WIDTH)), tbl, tbl] + [dyb] * 6,
        out_specs=[pl.BlockSpec((tt, 2 * B_WIDTH), lambda i: (i, 0)), _full((1, B_WIDTH))],
        out_shape=[jax.ShapeDtypeStruct((T, 2 * B_WIDTH), BF16), jax.ShapeDtypeStruct((1, B_WIDTH), F32)],
        compiler_params=_cp("arbitrary"), name=name)(kv, w_heads, cosf, sinsg, *dks, *dvs)


def _band_masks(n_is_first=None):
    row = lax.broadcasted_iota(jnp.int32, (SPAN, SPAN), 0)
    col = lax.broadcasted_iota(jnp.int32, (SPAN, SPAN), 1)
    return row >= col, col >= row


def _dil_views(T, d):
    L = T // d
    return L, L // SPAN


def _dil_fwd(qr, kr, kv, gi, d, *, name):
    T = qr.shape[0]
    L, nb = _dil_views(T, d)

    def body(q_ref, kc_ref, kp_ref, vc_ref, vp_ref, o_ref, lse_ref):
        cur_ok, prev_band = _band_masks()
        prev_ok = prev_band & (pl.program_id(1) > 0)
        for h in range(B_HEADS):
            sl = slice(h * HEAD_DIM, (h + 1) * HEAD_DIM)
            q = q_ref[:, sl]
            sc = jnp.where(cur_ok, _dot_nt(q, kc_ref[:, sl]) * ATT_SCALE, NEG)
            sp = jnp.where(prev_ok, _dot_nt(q, kp_ref[:, sl]) * ATT_SCALE, NEG)
            m = jnp.maximum(jnp.max(sc, axis=-1, keepdims=True), jnp.max(sp, axis=-1, keepdims=True))
            pc = jnp.exp(sc - m)
            pp = jnp.exp(sp - m)
            l = jnp.sum(pc, axis=-1, keepdims=True) + jnp.sum(pp, axis=-1, keepdims=True)
            o_ref[:, sl] = (_dot(pc, vc_ref[:, sl]) + _dot(pp, vp_ref[:, sl])) / l
            lse_ref[:, sl] = jnp.broadcast_to(m + jnp.log(l), (SPAN, HEAD_DIM))

    blk = lambda f: pl.BlockSpec((SPAN, B_WIDTH), f)
    cur = lambda r, n: (n, r)
    prev = lambda r, n: (jnp.maximum(n - 1, 0), r)
    ov = jax.ShapeDtypeStruct((L, d * B_WIDTH), F32)
    o, lse = pl.pallas_call(
        body, grid=(d, nb),
        in_specs=[blk(lambda r, n: (n, r * N_GROUPS + gi)), blk(cur), blk(prev),
                  blk(lambda r, n: (n, 2 * r + 1)), blk(lambda r, n: (jnp.maximum(n - 1, 0), 2 * r + 1))],
        out_specs=[blk(cur), blk(cur)], out_shape=[ov, ov],
        compiler_params=_cp("parallel", "arbitrary"), name=name,
    )(qr.reshape(L, d * N_GROUPS * B_WIDTH), kr.reshape(L, d * B_WIDTH), kr.reshape(L, d * B_WIDTH),
      kv.reshape(L, d * 2 * B_WIDTH), kv.reshape(L, d * 2 * B_WIDTH))
    return o.reshape(T, B_WIDTH), lse.reshape(T, B_WIDTH)


def _dil_combine_fwd(os_, lses, *, tt, name):
    T = os_[0].shape[0]

    def body(o0, o1, o2, l0, l1, l2, y_ref, lse_ref):
        a, b, c = l0[...], l1[...], l2[...]
        m = jnp.maximum(jnp.maximum(a, b), c)
        wa, wb, wc = jnp.exp(a - m), jnp.exp(b - m), jnp.exp(c - m)
        den = wa + wb + wc
        y_ref[...] = (wa * o0[...] + wb * o1[...] + wc * o2[...]) / den
        lse_ref[...] = m + jnp.log(den)

    blk = pl.BlockSpec((tt, B_WIDTH), lambda i: (i, 0))
    sh = jax.ShapeDtypeStruct((T, B_WIDTH), F32)
    return pl.pallas_call(
        body, grid=(T // tt,), in_specs=[blk] * 6, out_specs=[blk, blk], out_shape=[sh, sh],
        compiler_params=_cp("parallel"), name=name)(*os_, *lses)


def _dil_bwd_prep(dmix, mix_main, *, tt, name):
    T = mix_main.shape[0]

    def body(dy_ref, y_ref, dmm_ref, dd_ref):
        for h in range(B_HEADS):
            sl = slice(h * HEAD_DIM, (h + 1) * HEAD_DIM)
            dy = dy_ref[:, sl]
            dmm_ref[:, sl] = dy.astype(BF16)
            dd_ref[:, sl] = jnp.broadcast_to(jnp.sum(dy * y_ref[:, sl], axis=-1, keepdims=True), (tt, HEAD_DIM))

    blk = pl.BlockSpec((tt, B_WIDTH), lambda i: (i, 0))
    return pl.pallas_call(
        body, grid=(T // tt,), in_specs=[blk, blk], out_specs=[blk, blk],
        out_shape=[jax.ShapeDtypeStruct((T, B_WIDTH), BF16), jax.ShapeDtypeStruct((T, B_WIDTH), F32)],
        compiler_params=_cp("parallel"), name=name)(dmix, mix_main)


def _dil_bwd_dq(qr, kr, kv, dmm, lse, dd, gi, d, *, name):
    T = qr.shape[0]
    L, nb = _dil_views(T, d)

    def body(q_ref, kc_ref, kp_ref, vc_ref, vp_ref, dy_ref, lse_ref, dd_ref, dq_ref):
        cur_ok, prev_band = _band_masks()
        prev_ok = prev_band & (pl.program_id(1) > 0)
        for h in range(B_HEADS):
            sl = slice(h * HEAD_DIM, (h + 1) * HEAD_DIM)
            q, dy = q_ref[:, sl], dy_ref[:, sl]
            kc, kp = kc_ref[:, sl], kp_ref[:, sl]
            lse_h = jnp.max(lse_ref[:, sl], axis=-1, keepdims=True)
            dd_h = jnp.max(dd_ref[:, sl], axis=-1, keepdims=True)
            pc = jnp.exp(jnp.where(cur_ok, _dot_nt(q, kc) * ATT_SCALE, NEG) - lse_h)
            pp = jnp.exp(jnp.where(prev_ok, _dot_nt(q, kp) * ATT_SCALE, NEG) - lse_h)
            dsc = pc * (_dot_nt(dy, vc_ref[:, sl]) - dd_h) * ATT_SCALE
            dsp = pp * (_dot_nt(dy, vp_ref[:, sl]) - dd_h) * ATT_SCALE
            dq_ref[:, sl] = _dot(dsc, kc) + _dot(dsp, kp)

    blk = lambda f: pl.BlockSpec((SPAN, B_WIDTH), f)
    cur = lambda r, n: (n, r)
    prev = lambda r, n: (jnp.maximum(n - 1, 0), r)
    v2 = lambda a: a.reshape(L, d * a.shape[1])
    dq = pl.pallas_call(
        body, grid=(d, nb),
        in_specs=[blk(lambda r, n: (n, r * N_GROUPS + gi)), blk(cur), blk(prev),
                  blk(lambda r, n: (n, 2 * r + 1)), blk(lambda r, n: (jnp.maximum(n - 1, 0), 2 * r + 1)),
                  blk(cur), blk(cur), blk(cur)],
        out_specs=blk(cur), out_shape=jax.ShapeDtypeStruct((L, d * B_WIDTH), F32),
        compiler_params=_cp("parallel", "arbitrary"), name=name,
    )(v2(qr), v2(kr), v2(kr), v2(kv), v2(kv), v2(dmm), v2(lse), v2(dd))
    return dq.reshape(T, B_WIDTH)


def _dil_bwd_dkv(qr, kr, kv, dmm, lse, dd, gi, d, *, name):
    T = qr.shape[0]
    L, nb = _dil_views(T, d)

    def body(k_ref, v_ref, q0_ref, q1_ref, dy0_ref, dy1_ref, lse0_ref, lse1_ref, dd0_ref, dd1_ref, dk_ref, dv_ref):
        cur_ok, prev_band = _band_masks()
        next_ok = prev_band & (pl.program_id(1) < nb - 1)
        for h in range(B_HEADS):
            sl = slice(h * HEAD_DIM, (h + 1) * HEAD_DIM)
            k, v = k_ref[:, sl], v_ref[:, sl]
            dk = jnp.zeros((SPAN, HEAD_DIM), F32)
            dv = jnp.zeros((SPAN, HEAD_DIM), F32)
            for ok, q_ref, dy_ref, lse_ref, dd_ref in ((cur_ok, q0_ref, dy0_ref, lse0_ref, dd0_ref),
                                                         (next_ok, q1_ref, dy1_ref, lse1_ref, dd1_ref)):
                q, dy = q_ref[:, sl], dy_ref[:, sl]
                lse_h = jnp.max(lse_ref[:, sl], axis=-1, keepdims=True)
                dd_h = jnp.max(dd_ref[:, sl], axis=-1, keepdims=True)
                p = jnp.exp(jnp.where(ok, _dot_nt(q, k) * ATT_SCALE, NEG) - lse_h)
                ds = p * (_dot_nt(dy, v) - dd_h) * ATT_SCALE
                dk = dk + _dot_tn(ds, q)
                dv = dv + _dot_tn(p, dy)
            dk_ref[:, sl] = dk
            dv_ref[:, sl] = dv

    blk = lambda f: pl.BlockSpec((SPAN, B_WIDTH), f)
    cur = lambda r, n: (n, r)
    nxt = lambda r, n: (jnp.minimum(n + 1, nb - 1), r)
    qcur = lambda r, n: (n, r * N_GROUPS + gi)
    qnxt = lambda r, n: (jnp.minimum(n + 1, nb - 1), r * N_GROUPS + gi)
    v2 = lambda a: a.reshape(L, d * a.shape[1])
    ov = jax.ShapeDtypeStruct((L, d * B_WIDTH), F32)
    dk, dv = pl.pallas_call(
        body, grid=(d, nb),
        in_specs=[blk(cur), blk(lambda r, n: (n, 2 * r + 1)), blk(qcur), blk(qnxt),
                  blk(cur), blk(nxt), blk(cur), blk(nxt), blk(cur), blk(nxt)],
        out_specs=[blk(cur), blk(cur)], out_shape=[ov, ov],
        compiler_params=_cp("parallel", "arbitrary"), name=name,
    )(v2(kr), v2(kv), v2(qr), v2(qr), v2(dmm), v2(dmm), v2(lse), v2(lse), v2(dd), v2(dd))
    return dk.reshape(T, B_WIDTH), dv.reshape(T, B_WIDTH)


A_MQ_COL = 4 * A_WIDTH // MEM_WIDTH
B_MQ_COL = N_GROUPS * B_WIDTH // MEM_WIDTH


def _row(v):
    return v.reshape(1, -1).astype(F32)


def _local_step(x, mem, tgt, get_w, P, put_g):
    T = x.shape[0]
    cosf, sinsg = _rope_tables(T)
    lb_soft = jax.nn.softmax(P["a_lb_logits"].astype(F32), axis=0)
    lb = lb_soft[0:1]
    qw_heads = jnp.repeat(P["b_qnorm"][0], B_HEADS, axis=0).reshape(1, -1)
    kw_heads = jnp.tile(_row(P["b_knorm"]), (1, B_HEADS))
    mqw = [jnp.tile(_row(P["mem_qnorm"][l]), (1, MEM_HEADS)) for l in range(2)]
    mkw = [jnp.tile(_row(P["mem_knorm"][l]), (1, MEM_HEADS)) for l in range(2)]
    nmix = [_row(P["norm_mix"][l]) for l in range(2)]
    nffn = [_row(P["norm_ffn"][l]) for l in range(2)]
    mnorm = [_row(P["mem_norm"][l]) for l in range(2)]
    kvn = _row(P["kv_norm"])
    onorm = _row(P["a_onorm"])
    W = {}

    def w_of(name):
        if name not in W:
            W[name] = get_w(name)
        return W[name]

    proj_a, xn0 = _rms_matmul(x, nmix[0], w_of("a_w_in"), tt=512, tn=1664, wt=True, name="proj_a")
    mkv0, mn0 = _rms_matmul(mem, mnorm[0], w_of("w_mem_kv0"), tt=MEM_TOKENS, tn=2 * MEM_WIDTH, wt=False, name="mem_kv0")
    o_raw, st = _hgrn2_fwd(proj_a, lb, name="hgrn2_fwd")
    mm0 = _a_post_fwd(o_raw, proj_a, onorm, tt=512, name="a_post_fwd")
    mo0 = _mem_attn_fwd(proj_a, A_MQ_COL, mkv0, mqw[0], mkw[0], tt=512, name="mem_attn_fwd0")
    mix0 = jnp.concatenate([mm0, mo0], axis=1)
    hm0 = _mm_res(x, mix0, w_of("w_out0"), tt=512, name="out_proj0")
    gu0, hn0 = _rms_matmul(hm0, nffn[0], w_of("w_gate_up0"), tt=512, tn=1408, wt=True, name="gate_up0")
    h1 = _swiglu_down(hm0, gu0, w_of("w_down0"), tt=256, name="down0")
    kv, hkn = _rms_matmul(h1, kvn, w_of("w_kv"), tt=512, tn=768, wt=True, name="kv_proj")
    kr = _headnorm_rope_fwd(kv, kw_heads, cosf, sinsg, col0=0, n_heads=B_HEADS, tt=512, name="k_prep")

    proj_b, xn1 = _rms_matmul(h1, nmix[1], w_of("b_w_in"), tt=512, tn=1280, wt=True, name="proj_b")
    mkv1, mn1 = _rms_matmul(mem, mnorm[1], w_of("w_mem_kv1"), tt=MEM_TOKENS, tn=2 * MEM_WIDTH, wt=False, name="mem_kv1")
    qr = _headnorm_rope_fwd(proj_b, qw_heads, cosf, sinsg, col0=0, n_heads=N_GROUPS * B_HEADS, tt=512, name="q_prep")
    outs = [_dil_fwd(qr, kr, kv, gi, d, name=f"dil_fwd{gi}") for gi, d in enumerate(DILATIONS)]
    mm1, lse_tot = _dil_combine_fwd([o for o, _ in outs], [s for _, s in outs], tt=512, name="dil_combine")
    mo1 = _mem_attn_fwd(proj_b, B_MQ_COL, mkv1, mqw[1], mkw[1], tt=512, name="mem_attn_fwd1")
    mix1 = jnp.concatenate([mm1, mo1], axis=1)
    hm1 = _mm_res(h1, mix1, w_of("w_out1"), tt=512, name="out_proj1")
    gu1, hn1 = _rms_matmul(hm1, nffn[1], w_of("w_gate_up1"), tt=512, tn=1408, wt=True, name="gate_up1")
    y = _swiglu_down(hm1, gu1, w_of("w_down1"), tt=256, name="down1")
    dy, sq = _loss_kernel(y, tgt, tt=512, name="loss")

    gP = {}
    zeros_mem = jnp.zeros((MEM_TOKENS, D_MODEL), F32)

    def ffn_bwd(l, dh, hm, gu, hn):
        dgu, act = _swiglu_bwd(dh, gu, w_of(f"w_down{l}"), tt=256, name=f"swiglu_bwd{l}")
        put_g(f"w_down{l}", _mm_tn(act, dh, tt=512, tka=1408, name=f"g_w_down{l}"))
        put_g(f"w_gate_up{l}", _mm_tn(dgu, hn, tt=512, tka=1408, name=f"g_w_gate_up{l}"))
        dhm, g_nf = _rms_bwd_dx(hm, nffn[l], w_of(f"w_gate_up{l}"), dgu, dh, tt=256, wt=True, name=f"gate_up_bwd{l}")
        return dhm, g_nf

    def mix_bwd(l, dhm, mix, proj, qcol, mkv, mn):
        dmix = _mm_nt(dhm, w_of(f"w_out{l}"), tt=512, name=f"out_proj_bwd{l}")
        put_g(f"w_out{l}", _mm_tn(mix, dhm, tt=512, tka=512, name=f"g_w_out{l}"))
        dmq, dmkv, dqw, dkw = _mem_attn_bwd(proj, qcol, mkv, mqw[l], mkw[l], dmix, tt=512, name=f"mem_attn_bwd{l}")
        put_g(f"w_mem_kv{l}", _mm_tn(mn, dmkv, tt=MEM_TOKENS, tka=512, name=f"g_w_mem_kv{l}"))
        _, g_mn = _rms_bwd_dx(mem, mnorm[l], w_of(f"w_mem_kv{l}"), dmkv, zeros_mem, tt=MEM_TOKENS, wt=False, name=f"mem_kv_bwd{l}")
        fold = lambda v: v.reshape(MEM_HEADS, MEM_HEAD_DIM).sum(axis=0)
        return dmix, dmq, g_mn, fold(dqw), fold(dkw)

    dhm1, g_nf1 = ffn_bwd(1, dy, hm1, gu1, hn1)
    dmix1, dmq1, g_mn1, g_mq1, g_mk1 = mix_bwd(1, dhm1, mix1, proj_b, B_MQ_COL, mkv1, mn1)
    dmm, dd = _dil_bwd_prep(dmix1, mm1, tt=512, name="dil_bwd_prep")
    dqs, dks, dvs = [], [], []
    for gi, d in enumerate(DILATIONS):
        dqs.append(_dil_bwd_dq(qr, kr, kv, dmm, lse_tot, dd, gi, d, name=f"dil_bwd_dq{gi}"))
        dk_g, dv_g = _dil_bwd_dkv(qr, kr, kv, dmm, lse_tot, dd, gi, d, name=f"dil_bwd_dkv{gi}")
        dks.append(dk_g)
        dvs.append(dv_g)
    dq_raw, dqw = _q_prep_bwd(proj_b, qw_heads, cosf, sinsg, dqs, tt=512, name="q_prep_bwd")
    dkv, dkw = _kv_prep_bwd(kv, kw_heads, cosf, sinsg, dks, dvs, tt=512, name="kv_prep_bwd")
    dproj_b = jnp.concatenate([dq_raw, dmq1], axis=1)
    put_g("b_w_in", _mm_tn(dproj_b, xn1, tt=512, tka=1280, name="g_b_w_in"))
    dh1, g_nm1 = _rms_bwd_dx(h1, nmix[1], w_of("b_w_in"), dproj_b, dhm1, tt=256, wt=True, name="proj_b_bwd")
    put_g("w_kv", _mm_tn(dkv, hkn, tt=512, tka=768, name="g_w_kv"))
    dh1, g_kvn = _rms_bwd_dx(h1, kvn, w_of("w_kv"), dkv, dh1, tt=256, wt=True, name="kv_proj_bwd")

    dhm0, g_nf0 = ffn_bwd(0, dh1, hm0, gu0, hn0)
    dmix0, dmq0, g_mn0, g_mq0, g_mk0 = mix_bwd(0, dhm0, mix0, proj_a, A_MQ_COL, mkv0, mn0)
    do_raw, dg, g_onorm = _a_post_bwd(o_raw, proj_a, onorm, dmix0, tt=512, name="a_post_bwd")
    dq, dz, dv, dlb = _hgrn2_bwd(proj_a, lb, st, do_raw, name="hgrn2_bwd")
    dproj_a = jnp.concatenate([dq, dz, dv, dg, dmq0], axis=1)
    put_g("a_w_in", _mm_tn(dproj_a, xn0, tt=512, tka=1664, name="g_a_w_in"))
    gx, g_nm0 = _rms_bwd_dx(x, nmix[0], w_of("a_w_in"), dproj_a, dhm0, tt=256, wt=True, name="proj_a_bwd")

    dl0 = lb_soft[0:1] * lb_soft[1:2] * dlb
    gP["a_lb_logits"] = jnp.concatenate([dl0, -dl0], axis=0)
    gP["a_onorm"] = g_onorm
    gP["norm_mix"] = jnp.concatenate([g_nm0, g_nm1], axis=0)
    gP["norm_ffn"] = jnp.concatenate([g_nf0, g_nf1], axis=0)
    gP["b_qnorm"] = dqw.reshape(N_GROUPS, B_HEADS, HEAD_DIM).sum(axis=1)[None]
    gP["kv_norm"] = g_kvn.reshape(-1)
    gP["b_knorm"] = dkw.reshape(B_HEADS, HEAD_DIM).sum(axis=0)
    gP["mem_norm"] = jnp.concatenate([g_mn0, g_mn1], axis=0)
    gP["mem_qnorm"] = jnp.stack([g_mq0, g_mq1])
    gP["mem_knorm"] = jnp.stack([g_mk0, g_mk1])
    return sq, gx, gP


MESH_ID = pl.DeviceIdType.MESH
HBM_SPEC = pl.BlockSpec(memory_space=pltpu.HBM)


def _position():
    return lax.axis_index("x"), lax.axis_index("y"), lax.axis_index("c")


def _all_gather(blocks, *, name):
    n = len(blocks)

    def body(*refs):
        x_refs, out_refs = refs[:n], refs[n:2 * n]
        send_sems, recv_sems, local_sems = refs[2 * n:]
        x, y, c = _position()
        me, sibling = (x, y, c), (x, y, 1 - c)
        chips = [(1 - x, y), (x, 1 - y), (1 - x, 1 - y)]

        def slot(a, px, py, pc):
            return out_refs[a].at[4 * px + 2 * py + pc]

        def copy(a, k, blk, to, src=None):
            return pltpu.make_async_remote_copy(
                src_ref=slot(a, *blk) if src is None else src, dst_ref=slot(a, *blk),
                send_sem=send_sems.at[7 * a + k], recv_sem=recv_sems.at[7 * a + k], device_id=to, device_id_type=MESH_ID)

        mine = [pltpu.make_async_copy(x_refs[a], slot(a, *me), local_sems.at[a]) for a in range(n)]
        for cp in mine:
            cp.start()
        first = []
        for a in range(n):
            first.append(copy(a, 0, me, sibling, src=x_refs[a]))
            first += [copy(a, 1 + j, me, (*chip, c), src=x_refs[a]) for j, chip in enumerate(chips)]
        for cp in first:
            cp.start()
        passed = []
        for j, chip in enumerate(chips):
            for a in range(n):
                copy(a, 1 + j, (*chip, c), me).wait_recv()
                cp = copy(a, 4 + j, (*chip, c), sibling)
                cp.start()
                passed.append(cp)
        for a in range(n):
            copy(a, 0, sibling, me).wait_recv()
            for j, chip in enumerate(chips):
                copy(a, 4 + j, (*chip, 1 - c), me).wait_recv()
        for cp in first + passed:
            cp.wait_send()
        for cp in mine:
            cp.wait()

    return pl.pallas_call(
        body, out_shape=[jax.ShapeDtypeStruct((N_DEV,) + b.shape, b.dtype) for b in blocks],
        in_specs=[HBM_SPEC] * n, out_specs=[HBM_SPEC] * n,
        scratch_shapes=[pltpu.SemaphoreType.DMA((7 * n,)), pltpu.SemaphoreType.DMA((7 * n,)), pltpu.SemaphoreType.DMA((n,))],
        name=name)(*blocks)


def _all_to_all(blocks, *, name):
    n = len(blocks)

    def body(*refs):
        x_refs, out_refs = refs[:n], refs[n:2 * n]
        send_sems, recv_sems, local_sems = refs[2 * n:]
        x, y, c = _position()
        me = 4 * x + 2 * y + c
        mine = [pltpu.make_async_copy(x_refs[a].at[me], out_refs[a].at[me], local_sems.at[a]) for a in range(n)]
        for cp in mine:
            cp.start()
        copies = []
        for a in range(n):
            for k in range(1, N_DEV):
                tx = x if not (k >> 2) & 1 else 1 - x
                ty = y if not (k >> 1) & 1 else 1 - y
                tc = c if not k & 1 else 1 - c
                cp = pltpu.make_async_remote_copy(
                    src_ref=x_refs[a].at[4 * tx + 2 * ty + tc], dst_ref=out_refs[a].at[me],
                    send_sem=send_sems.at[7 * a + k - 1], recv_sem=recv_sems.at[7 * a + k - 1],
                    device_id=(tx, ty, tc), device_id_type=MESH_ID)
                cp.start()
                copies.append(cp)
        for cp in copies:
            cp.wait()
        for cp in mine:
            cp.wait()

    return pl.pallas_call(
        body, out_shape=[jax.ShapeDtypeStruct(b.shape, b.dtype) for b in blocks],
        in_specs=[HBM_SPEC] * n, out_specs=[HBM_SPEC] * n,
        scratch_shapes=[pltpu.SemaphoreType.DMA((7 * n,)), pltpu.SemaphoreType.DMA((7 * n,)), pltpu.SemaphoreType.DMA((n,))],
        name=name)(*blocks)


def _sum_sources(parts, *, tr, name):
    n, R, C = parts.shape

    def body(p_ref, o_ref):
        acc = p_ref[0].astype(F32)
        for s in range(1, n):
            acc = acc + p_ref[s].astype(F32)
        o_ref[...] = acc

    return pl.pallas_call(
        body, grid=(R // tr,), in_specs=[pl.BlockSpec((n, tr, C), lambda i: (0, i, 0))],
        out_specs=pl.BlockSpec((tr, C), lambda i: (i, 0)),
        out_shape=jax.ShapeDtypeStruct((R, C), F32), compiler_params=_cp("parallel"), name=name)(parts)


def _adamw(g, w, m, v, *, tr, name):
    L, R, C = w.shape
    c1 = 1.0 - ADAM_B1 ** ADAM_STEP
    c2 = 1.0 - ADAM_B2 ** ADAM_STEP

    def body(g_ref, w_ref, m_ref, v_ref, d_ref, nm_ref, nv_ref):
        gv = g_ref[...]
        nm = ADAM_B1 * m_ref[...] + (1.0 - ADAM_B1) * gv
        nv = ADAM_B2 * v_ref[...] + (1.0 - ADAM_B2) * (gv * gv)
        nm_ref[...] = nm
        nv_ref[...] = nv
        d_ref[...] = -ADAM_LR * ((nm / c1) / (jnp.sqrt(nv / c2) + ADAM_EPS) + ADAM_WD * w_ref[...])

    blk = pl.BlockSpec((None, tr, C), lambda l, i: (l, i, 0))
    sh = jax.ShapeDtypeStruct((L, R, C), F32)
    return pl.pallas_call(
        body, grid=(L, R // tr), in_specs=[blk] * 4, out_specs=[blk] * 3, out_shape=[sh] * 3,
        compiler_params=_cp("parallel", "parallel"), name=name)(g, w, m, v)


UNITS = {
    "a_w_in": ("a_w_in", 0, True), "w_mem_kv0": ("w_mem_kv", 0, False), "w_out0": ("w_out", 0, False),
    "w_gate_up0": ("w_gate_up", 0, True), "w_down0": ("w_down", 0, False), "w_kv": ("w_kv", None, True),
    "b_w_in": ("b_w_in", 0, True), "w_mem_kv1": ("w_mem_kv", 1, False), "w_out1": ("w_out", 1, False),
    "w_gate_up1": ("w_gate_up", 1, True), "w_down1": ("w_down", 1, False),
}
BIG = ("a_w_in", "b_w_in", "w_kv", "w_mem_kv", "w_out", "w_gate_up", "w_down")
ADAMW_ROW_TILE = {"a_w_in": 256, "b_w_in": 256, "w_kv": 256, "w_mem_kv": 128, "w_out": 128, "w_gate_up": 256, "w_down": 352}


def _wire_block(weights, unit):
    name, layer, col = UNITS[unit]
    a = weights[name] if layer is None else weights[name][layer]
    return (a.T if col else a).astype(BF16)


def _natural_grads(sums):
    out = {}
    for name in BIG:
        parts = [sums[u].T if col else sums[u] for u, (wn, _, col) in UNITS.items() if wn == name]
        out[name] = parts[0] if name == "w_kv" else jnp.stack(parts)
    return out


SMALL_REPLICATED = ("norm_mix", "norm_ffn", "b_qnorm", "kv_norm", "b_knorm", "mem_norm", "mem_qnorm", "mem_knorm")
SMALL_SHARDED = ("a_lb_logits", "a_onorm")
SMALL_ORDER = SMALL_REPLICATED + SMALL_SHARDED
LANES = 128


def _prod(shape):
    n = 1
    for s in shape:
        n *= s
    return n


def _pack_flat(arrays, rows, cols, dtype):
    flat = jnp.concatenate([a.reshape(-1).astype(dtype) for a in arrays])
    return jnp.pad(flat, (0, rows * cols - flat.shape[0])).reshape(rows, cols)


def _unpack_flat(packed, shapes):
    flat = packed.reshape(-1)
    out, off = [], 0
    for s in shapes:
        out.append(flat[off:off + _prod(s)].reshape(s))
        off += _prod(s)
    return out


def kernel(x, mem, norm_mix, norm_ffn, a_w_in, a_lb_logits, a_onorm, b_w_in, b_qnorm, kv_norm, w_kv, b_knorm, mem_norm, w_mem_kv, mem_qnorm, mem_knorm, w_out, w_gate_up, w_down, loss_target, m_norm_mix, m_norm_ffn, m_a_w_in, m_a_lb_logits, m_a_onorm, m_b_w_in, m_b_qnorm, m_kv_norm, m_w_kv, m_b_knorm, m_mem_norm, m_w_mem_kv, m_mem_qnorm, m_mem_knorm, m_w_out, m_w_gate_up, m_w_down, v_norm_mix, v_norm_ffn, v_a_w_in, v_a_lb_logits, v_a_onorm, v_b_w_in, v_b_qnorm, v_kv_norm, v_w_kv, v_b_knorm, v_mem_norm, v_w_mem_kv, v_mem_qnorm, v_mem_knorm, v_w_out, v_w_gate_up, v_w_down):
    names = ("norm_mix", "norm_ffn", "a_w_in", "a_lb_logits", "a_onorm", "b_w_in", "b_qnorm", "kv_norm", "w_kv", "b_knorm",
             "mem_norm", "w_mem_kv", "mem_qnorm", "mem_knorm", "w_out", "w_gate_up", "w_down")
    w = dict(zip(names, (norm_mix, norm_ffn, a_w_in, a_lb_logits, a_onorm, b_w_in, b_qnorm, kv_norm, w_kv, b_knorm,
                         mem_norm, w_mem_kv, mem_qnorm, mem_knorm, w_out, w_gate_up, w_down)))
    m = dict(zip(names, (m_norm_mix, m_norm_ffn, m_a_w_in, m_a_lb_logits, m_a_onorm, m_b_w_in, m_b_qnorm, m_kv_norm, m_w_kv,
                         m_b_knorm, m_mem_norm, m_w_mem_kv, m_mem_qnorm, m_mem_knorm, m_w_out, m_w_gate_up, m_w_down)))
    v = dict(zip(names, (v_norm_mix, v_norm_ffn, v_a_w_in, v_a_lb_logits, v_a_onorm, v_b_w_in, v_b_qnorm, v_kv_norm, v_w_kv,
                         v_b_knorm, v_mem_norm, v_w_mem_kv, v_mem_qnorm, v_mem_knorm, v_w_out, v_w_gate_up, v_w_down)))

    units = list(UNITS)
    gathered = _all_gather([_wire_block(w, u) for u in units] + [_pack_flat([a_lb_logits, a_onorm], 8, LANES, F32)],
                           name="gather_weights")
    full = {u: g.reshape(-1, g.shape[-1]) for u, g in zip(units, gathered)}
    small_in = gathered[-1].reshape(N_DEV, -1)
    P = {n: w[n] for n in SMALL_REPLICATED}
    P["a_lb_logits"] = small_in[:, :192].reshape(N_DEV, 2, 96).transpose(1, 0, 2).reshape(2, A_WIDTH)
    P["a_onorm"] = small_in[:, 192:288].reshape(1, A_WIDTH)

    grads = {}
    sq, gx, gP = _local_step(x[0], mem[0], loss_target[0], full.__getitem__, P, grads.__setitem__)
    loss = lax.psum(0.5 * jnp.sum(sq) / D_MODEL, ("x", "y", "c"))

    received = _all_to_all([grads[u].reshape(N_DEV, -1, grads[u].shape[-1]) for u in units], name="scatter_grads")
    sums = {u: _sum_sources(r, tr=r.shape[1] if r.shape[1] <= 416 else r.shape[1] // 2, name=f"sum_{u}")
            for u, r in zip(units, received)}
    out = {"grad": _natural_grads(sums), "delta": {}, "new_m": {}, "new_v": {}}
    for n in BIG:
        shape = w[n].shape
        as3 = lambda a: a.reshape((-1,) + shape[-2:])
        res = _adamw(as3(out["grad"][n]), as3(w[n]), as3(m[n]), as3(v[n]), tr=ADAMW_ROW_TILE[n], name=f"adamw_{n}")
        out["grad"][n] = out["grad"][n].reshape(shape)
        for kind, r in zip(("delta", "new_m", "new_v"), res):
            out[kind][n] = r.reshape(shape)

    full_shapes = [(2, A_WIDTH) if n == "a_lb_logits" else (1, A_WIDTH) if n == "a_onorm" else w[n].shape for n in SMALL_ORDER]
    n_small = sum(_prod(s) for s in full_shapes)
    rows_small = -(-n_small // (8 * LANES)) * 8
    g_all, = _all_gather([_pack_flat([gP[n] for n in SMALL_ORDER], rows_small, LANES, F32)], name="gather_small_grads")
    g_small = dict(zip(SMALL_ORDER, _unpack_flat(_sum_sources(g_all, tr=rows_small, name="sum_small_grads"), full_shapes)))
    me = 4 * lax.axis_index("x") + 2 * lax.axis_index("y") + lax.axis_index("c")
    for n in SMALL_SHARDED:
        g_small[n] = lax.dynamic_slice_in_dim(g_small[n], me * 96, 96, axis=1)
    shapes = [w[n].shape for n in SMALL_ORDER]
    rows_upd = -(-sum(_prod(s) for s in shapes) // (8 * LANES)) * 8
    pk = lambda d: _pack_flat([d[n] for n in SMALL_ORDER], rows_upd, LANES, F32)
    res = _adamw(pk(g_small)[None], pk(w)[None], pk(m)[None], pk(v)[None], tr=rows_upd, name="adamw_small")
    out["grad"].update(g_small)
    for kind, packed in zip(("delta", "new_m", "new_v"), res):
        out[kind].update(zip(SMALL_ORDER, _unpack_flat(packed[0], shapes)))

    return (loss, gx[None], *[out["grad"][n] for n in names], *[out["delta"][n] for n in names],
            *[out["new_m"][n] for n in names], *[out["new_v"][n] for n in names])
```

```python
import functools

import jax
import jax.numpy as jnp
from jax import lax
from jax.experimental import pallas as pl
from jax.experimental.pallas import tpu as pltpu

F32 = jnp.float32
BF16 = jnp.bfloat16

N_DEV = 8
D_MODEL = 1024
HEAD_DIM = 128
A_HEADS = 6
A_WIDTH = A_HEADS * HEAD_DIM
CHUNK = 64
B_HEADS = 6
B_WIDTH = B_HEADS * HEAD_DIM
DILATIONS = (1, 4, 16)
SPAN = 128
N_GROUPS = 3
ROPE_THETA = 10000.0
MEM_TOKENS = 256
MEM_HEADS = 4
MEM_HEAD_DIM = 64
MEM_WIDTH = MEM_HEADS * MEM_HEAD_DIM
FFN_HIDDEN = 2816
EPS = 1e-6

ADAM_LR = 0.001
ADAM_B1 = 0.9
ADAM_B2 = 0.999
ADAM_EPS = 1e-08
ADAM_WD = 0.01
ADAM_STEP = 10

V7X_VMEM_LIMIT_BYTES = 56 * 1024 * 1024

NT_DIMS = (((1,), (1,)), ((), ()))
TN_DIMS = (((0,), (0,)), ((), ()))


def _cp(*sem):
    return pltpu.CompilerParams(dimension_semantics=sem, vmem_limit_bytes=V7X_VMEM_LIMIT_BYTES)


def _dot(a, b):
    return jnp.dot(a.astype(BF16), b.astype(BF16), preferred_element_type=F32)


def _dot_nt(a, b):
    return lax.dot_general(a.astype(BF16), b.astype(BF16), NT_DIMS, preferred_element_type=F32)


def _dot_tn(a, b):
    return lax.dot_general(a.astype(BF16), b.astype(BF16), TN_DIMS, preferred_element_type=F32)


def _dot3(m01, x):
    hi = x.astype(BF16)
    r1 = x - hi.astype(F32)
    mid = r1.astype(BF16)
    lo = (r1 - mid.astype(F32)).astype(BF16)
    d = functools.partial(jnp.dot, preferred_element_type=F32)
    return d(m01, hi) + d(m01, mid) + d(m01, lo)


def _sigmoid(x):
    return 1.0 / (1.0 + jnp.exp(-x))


def _full(shape):
    return pl.BlockSpec(shape, lambda *_: (0,) * len(shape))


def _rms_matmul(x, g, w, *, tt, tn, wt, name):
    T, K = x.shape
    N = w.shape[0] if wt else w.shape[1]

    def body(x_ref, g_ref, w_ref, y_ref, xn_ref):
        @pl.when(pl.program_id(1) == 0)
        def _():
            xf = x_ref[...]
            r = lax.rsqrt(jnp.mean(xf * xf, axis=-1, keepdims=True) + EPS)
            xn_ref[...] = (xf * r * g_ref[...]).astype(BF16)

        y_ref[...] = (_dot_nt if wt else _dot)(xn_ref[...], w_ref[...])

    w_spec = pl.BlockSpec((tn, K), lambda i, j: (j, 0)) if wt else pl.BlockSpec((K, tn), lambda i, j: (0, j))
    return pl.pallas_call(
        body, grid=(T // tt, N // tn),
        in_specs=[pl.BlockSpec((tt, K), lambda i, j: (i, 0)), _full((1, K)), w_spec],
        out_specs=[pl.BlockSpec((tt, tn), lambda i, j: (i, j)), pl.BlockSpec((tt, K), lambda i, j: (i, 0))],
        out_shape=[jax.ShapeDtypeStruct((T, N), F32), jax.ShapeDtypeStruct((T, K), BF16)],
        compiler_params=_cp("parallel", "arbitrary"), name=name)(x, g, w)


def _mm_res(res, a, w, *, tt, name):
    T, K = a.shape
    N = w.shape[1]

    def body(r_ref, a_ref, w_ref, o_ref):
        o_ref[...] = r_ref[...] + _dot(a_ref[...], w_ref[...])

    return pl.pallas_call(
        body, grid=(T // tt,),
        in_specs=[pl.BlockSpec((tt, N), lambda i: (i, 0)), pl.BlockSpec((tt, K), lambda i: (i, 0)), _full((K, N))],
        out_specs=pl.BlockSpec((tt, N), lambda i: (i, 0)),
        out_shape=jax.ShapeDtypeStruct((T, N), F32),
        compiler_params=_cp("parallel"), name=name)(res, a, w)


def _swiglu_down(h, gu, wd, *, tt, name):
    T, D = h.shape
    Fh = wd.shape[0]

    def body(h_ref, gt_ref, up_ref, w_ref, o_ref):
        gt = gt_ref[...]
        act = gt * _sigmoid(gt) * up_ref[...]
        o_ref[...] = h_ref[...] + _dot(act, w_ref[...])

    return pl.pallas_call(
        body, grid=(T // tt,),
        in_specs=[pl.BlockSpec((tt, D), lambda i: (i, 0)), pl.BlockSpec((tt, Fh), lambda i: (i, 0)),
                  pl.BlockSpec((tt, Fh), lambda i: (i, 1)), _full((Fh, D))],
        out_specs=pl.BlockSpec((tt, D), lambda i: (i, 0)),
        out_shape=jax.ShapeDtypeStruct((T, D), F32),
        compiler_params=_cp("parallel"), name=name)(h, gu, gu, wd)


def _swiglu_bwd(dh, gu, wd, *, tt, name):
    T, D = dh.shape
    Fh = wd.shape[0]

    def body(dh_ref, gt_ref, up_ref, w_ref, dgu_ref, act_ref):
        gt = gt_ref[...]
        up = up_ref[...]
        s = _sigmoid(gt)
        silu = gt * s
        dact = _dot_nt(dh_ref[...], w_ref[...])
        act_ref[...] = (silu * up).astype(BF16)
        dgu_ref[:, :Fh] = (dact * up * (s * (1.0 + gt * (1.0 - s)))).astype(BF16)
        dgu_ref[:, Fh:] = (dact * silu).astype(BF16)

    return pl.pallas_call(
        body, grid=(T // tt,),
        in_specs=[pl.BlockSpec((tt, D), lambda i: (i, 0)), pl.BlockSpec((tt, Fh), lambda i: (i, 0)),
                  pl.BlockSpec((tt, Fh), lambda i: (i, 1)), _full((Fh, D))],
        out_specs=[pl.BlockSpec((tt, 2 * Fh), lambda i: (i, 0)), pl.BlockSpec((tt, Fh), lambda i: (i, 0))],
        out_shape=[jax.ShapeDtypeStruct((T, 2 * Fh), BF16), jax.ShapeDtypeStruct((T, Fh), BF16)],
        compiler_params=_cp("parallel"), name=name)(dh, gu, gu, wd)


def _mm_nt(a, w, *, tt, name):
    T, N = a.shape
    K = w.shape[0]

    def body(a_ref, w_ref, o_ref):
        o_ref[...] = _dot_nt(a_ref[...], w_ref[...])

    return pl.pallas_call(
        body, grid=(T // tt,),
        in_specs=[pl.BlockSpec((tt, N), lambda i: (i, 0)), _full((K, N))],
        out_specs=pl.BlockSpec((tt, K), lambda i: (i, 0)),
        out_shape=jax.ShapeDtypeStruct((T, K), F32),
        compiler_params=_cp("parallel"), name=name)(a, w)


def _mm_tn(a, b, *, tt, tka, name):
    T, Ka = a.shape
    N = b.shape[1]
    last = T // tt - 1

    def body(a_ref, b_ref, o_ref, acc):
        @pl.when(pl.program_id(1) == 0)
        def _():
            acc[...] = jnp.zeros_like(acc)

        acc[...] += _dot_tn(a_ref[...], b_ref[...])

        @pl.when(pl.program_id(1) == last)
        def _():
            o_ref[...] = acc[...].astype(BF16)

    return pl.pallas_call(
        body, grid=(Ka // tka, T // tt),
        in_specs=[pl.BlockSpec((tt, tka), lambda j, t: (t, j)), pl.BlockSpec((tt, N), lambda j, t: (t, 0))],
        out_specs=pl.BlockSpec((tka, N), lambda j, t: (j, 0)),
        out_shape=jax.ShapeDtypeStruct((Ka, N), BF16),
        scratch_shapes=[pltpu.VMEM((tka, N), F32)],
        compiler_params=_cp("parallel", "arbitrary"), name=name)(a, b)


def _rms_bwd_dx(x, g, w, dy, dres, *, tt, wt, name):
    T, K = x.shape
    N = w.shape[0] if wt else w.shape[1]

    def body(x_ref, g_ref, w_ref, dy_ref, dres_ref, dx_ref, dg_ref):
        @pl.when(pl.program_id(0) == 0)
        def _():
            dg_ref[...] = jnp.zeros_like(dg_ref)

        dxn = (_dot if wt else _dot_nt)(dy_ref[...], w_ref[...])
        xf = x_ref[...]
        r = lax.rsqrt(jnp.mean(xf * xf, axis=-1, keepdims=True) + EPS)
        xhat = xf * r
        dg_ref[...] += jnp.sum(dxn * xhat, axis=0, keepdims=True)
        dxhat = dxn * g_ref[...]
        dx_ref[...] = dres_ref[...] + r * (dxhat - xhat * jnp.mean(dxhat * xhat, axis=-1, keepdims=True))

    return pl.pallas_call(
        body, grid=(T // tt,),
        in_specs=[pl.BlockSpec((tt, K), lambda i: (i, 0)), _full((1, K)), _full(w.shape),
                  pl.BlockSpec((tt, N), lambda i: (i, 0)), pl.BlockSpec((tt, K), lambda i: (i, 0))],
        out_specs=[pl.BlockSpec((tt, K), lambda i: (i, 0)), _full((1, K))],
        out_shape=[jax.ShapeDtypeStruct((T, K), F32), jax.ShapeDtypeStruct((1, K), F32)],
        compiler_params=_cp("arbitrary"), name=name)(x, g, w, dy, dres)


def _loss_kernel(y, tgt, *, tt, name):
    T, D = y.shape

    def body(y_ref, t_ref, dy_ref, acc_ref):
        @pl.when(pl.program_id(0) == 0)
        def _():
            acc_ref[...] = jnp.zeros_like(acc_ref)

        e = y_ref[...] - t_ref[...]
        dy_ref[...] = e * (1.0 / D)
        acc_ref[...] += jnp.sum(e * e, axis=0, keepdims=True)

    return pl.pallas_call(
        body, grid=(T // tt,),
        in_specs=[pl.BlockSpec((tt, D), lambda i: (i, 0)), pl.BlockSpec((tt, D), lambda i: (i, 0))],
        out_specs=[pl.BlockSpec((tt, D), lambda i: (i, 0)), _full((1, D))],
        out_shape=[jax.ShapeDtypeStruct((T, D), F32), jax.ShapeDtypeStruct((1, D), F32)],
        compiler_params=_cp("arbitrary"), name=name)(y, tgt)


HGRN_TB = 512
HGRN_NCH = HGRN_TB // CHUNK


def _hgrn_chunk_fwd(q, z, lbv, tril01):
    sig = _sigmoid(z)
    f = lbv + (1.0 - lbv) * sig
    kk = 1.0 - f
    b = _dot3(tril01, jnp.log(f))
    bend = b[CHUNK - 1:CHUNK, :]
    sq = _sigmoid(q)
    eb = jnp.exp(b)
    emb = jnp.exp(-b)
    eo = jnp.exp(bend - b)
    dec = jnp.exp(bend)
    return sig, f, kk, sq, eb, emb, eo, dec


def _hgrn2_fwd(proj, lb, *, name):
    T = proj.shape[0]
    nT = T // HGRN_TB
    nC = T // CHUNK

    def body(q_ref, z_ref, v_ref, lb_ref, o_ref, st_ref, state):
        @pl.when(pl.program_id(1) == 0)
        def _():
            state[...] = jnp.zeros_like(state)

        row = lax.broadcasted_iota(jnp.int32, (CHUNK, CHUNK), 0)
        col = lax.broadcasted_iota(jnp.int32, (CHUNK, CHUNK), 1)
        causal = row >= col
        tril01 = causal.astype(BF16)
        lbv = lb_ref[...]

        def chunk(c, carry):
            rows = pl.ds(pl.multiple_of(c * CHUNK, CHUNK), CHUNK)
            q = q_ref[rows, :]
            v = v_ref[rows, :].astype(BF16)
            sig, f, kk, sq, eb, emb, eo, dec = _hgrn_chunk_fwd(q, z_ref[rows, :], lbv, tril01)
            qi = (q * sq * eb).astype(BF16)
            ki = (kk * emb).astype(BF16)
            ko = (kk * eo).astype(BF16)
            st = state[...]
            att = jnp.where(causal, _dot_nt(qi, ki), 0.0)
            o_ref[rows, :] = _dot(att, v) + _dot_nt(qi, st)
            st_ref[c, 0] = st
            state[...] = st * dec + _dot_tn(v, ko)
            return carry

        lax.fori_loop(0, HGRN_NCH, chunk, 0)

    hb = lambda off: pl.BlockSpec((HGRN_TB, HEAD_DIM), lambda h, i: (i, off + h))
    return pl.pallas_call(
        body, grid=(A_HEADS, nT),
        in_specs=[hb(0), hb(A_HEADS), hb(2 * A_HEADS), pl.BlockSpec((1, HEAD_DIM), lambda h, i: (0, h))],
        out_specs=[hb(0), pl.BlockSpec((HGRN_NCH, 1, HEAD_DIM, HEAD_DIM), lambda h, i: (i, h, 0, 0))],
        out_shape=[jax.ShapeDtypeStruct((T, A_WIDTH), F32), jax.ShapeDtypeStruct((nC, A_HEADS, HEAD_DIM, HEAD_DIM), F32)],
        scratch_shapes=[pltpu.VMEM((HEAD_DIM, HEAD_DIM), F32)],
        compiler_params=_cp("parallel", "arbitrary"), name=name)(proj, proj, proj, lb)


def _hgrn2_bwd(proj, lb, st_all, do, *, name):
    T = proj.shape[0]
    nT = T // HGRN_TB

    def body(q_ref, z_ref, v_ref, lb_ref, st_ref, do_ref, dq_ref, dz_ref, dv_ref, dlb_ref, dstate):
        @pl.when(pl.program_id(1) == 0)
        def _():
            dstate[...] = jnp.zeros_like(dstate)
            dlb_ref[...] = jnp.zeros_like(dlb_ref)

        row = lax.broadcasted_iota(jnp.int32, (CHUNK, CHUNK), 0)
        col = lax.broadcasted_iota(jnp.int32, (CHUNK, CHUNK), 1)
        causal = row >= col
        tril01 = causal.astype(BF16)
        triu01 = (row <= col).astype(BF16)
        lbv = lb_ref[...]

        def chunk(cc, carry):
            c = HGRN_NCH - 1 - cc
            rows = pl.ds(pl.multiple_of(c * CHUNK, CHUNK), CHUNK)
            q = q_ref[rows, :]
            v = v_ref[rows, :].astype(BF16)
            sig, f, kk, sq, eb, emb, eo, dec = _hgrn_chunk_fwd(q, z_ref[rows, :], lbv, tril01)
            qi32 = q * sq * eb
            ki32 = kk * emb
            ko32 = kk * eo
            qi, ki, ko = qi32.astype(BF16), ki32.astype(BF16), ko32.astype(BF16)
            att = jnp.where(causal, _dot_nt(qi, ki), 0.0).astype(BF16)
            dout = do_ref[rows, :].astype(BF16)
            st = st_ref[c, 0]
            dst = dstate[...]
            dst16 = dst.astype(BF16)
            datt = jnp.where(causal, _dot_nt(dout, v), 0.0).astype(BF16)
            dqi = _dot(datt, ki) + _dot(dout, st)
            dki = _dot_tn(datt, qi)
            dv_ref[rows, :] = (_dot_tn(att, dout) + _dot_nt(ko, dst16)).astype(BF16)
            dko = _dot(v, dst16)
            ddec = jnp.sum(dst * st, axis=0, keepdims=True)
            dstate[...] = dst * dec + _dot_tn(dout, qi)
            dkk = dki * emb + dko * eo
            db = dqi * qi32 - dki * ki32 - dko * ko32
            dbend = jnp.sum(dko * ko32, axis=0, keepdims=True) + ddec * dec
            dlogf = _dot3(triu01, db) + dbend
            df = dlogf / f - dkk
            dz_ref[rows, :] = (df * (1.0 - lbv) * sig * (1.0 - sig)).astype(BF16)
            dlb_ref[...] += jnp.sum(df * (1.0 - sig), axis=0, keepdims=True)
            dq_ref[rows, :] = (dqi * eb * (sq * (1.0 + q * (1.0 - sq)))).astype(BF16)
            return carry

        lax.fori_loop(0, HGRN_NCH, chunk, 0)

    hb = lambda off: pl.BlockSpec((HGRN_TB, HEAD_DIM), lambda h, i: (nT - 1 - i, off + h))
    hlb = pl.BlockSpec((1, HEAD_DIM), lambda h, i: (0, h))
    o16 = jax.ShapeDtypeStruct((T, A_WIDTH), BF16)
    return pl.pallas_call(
        body, grid=(A_HEADS, nT),
        in_specs=[hb(0), hb(A_HEADS), hb(2 * A_HEADS), hlb,
                  pl.BlockSpec((HGRN_NCH, 1, HEAD_DIM, HEAD_DIM), lambda h, i: (nT - 1 - i, h, 0, 0)), hb(0)],
        out_specs=[hb(0), hb(0), hb(0), hlb],
        out_shape=[o16, o16, o16, jax.ShapeDtypeStruct((1, A_WIDTH), F32)],
        scratch_shapes=[pltpu.VMEM((HEAD_DIM, HEAD_DIM), F32)],
        compiler_params=_cp("parallel", "arbitrary"), name=name)(proj, proj, proj, lb, st_all, do)


def _head_rms(x):
    r = lax.rsqrt(jnp.mean(x * x, axis=-1, keepdims=True) + EPS)
    return x * r, r


def _head_rms_bwd(dxhat, xhat, r):
    return r * (dxhat - xhat * jnp.mean(dxhat * xhat, axis=-1, keepdims=True))


def _a_post_fwd(o, proj, onorm, *, tt, name):
    T = o.shape[0]

    def body(o_ref, g_ref, w_ref, y_ref):
        for h in range(A_HEADS):
            sl = slice(h * HEAD_DIM, (h + 1) * HEAD_DIM)
            xhat, _ = _head_rms(o_ref[:, sl])
            g = g_ref[:, sl]
            y_ref[:, sl] = xhat * w_ref[:, sl] * (g * _sigmoid(g))

    blk = lambda c: pl.BlockSpec((tt, A_WIDTH), lambda i: (i, c))
    return pl.pallas_call(
        body, grid=(T // tt,), in_specs=[blk(0), blk(3), _full((1, A_WIDTH))], out_specs=blk(0),
        out_shape=jax.ShapeDtypeStruct((T, A_WIDTH), F32),
        compiler_params=_cp("parallel"), name=name)(o, proj, onorm)


def _a_post_bwd(o, proj, onorm, dmix, *, tt, name):
    T = o.shape[0]

    def body(o_ref, g_ref, w_ref, dy_ref, do_ref, dg_ref, dw_ref):
        @pl.when(pl.program_id(0) == 0)
        def _():
            dw_ref[...] = jnp.zeros_like(dw_ref)

        for h in range(A_HEADS):
            sl = slice(h * HEAD_DIM, (h + 1) * HEAD_DIM)
            xhat, r = _head_rms(o_ref[:, sl])
            g = g_ref[:, sl]
            s = _sigmoid(g)
            dy = dy_ref[:, sl]
            w = w_ref[:, sl]
            dg_ref[:, sl] = (dy * xhat * w * (s * (1.0 + g * (1.0 - s)))).astype(BF16)
            dyn = dy * (g * s)
            dw_ref[:, sl] += jnp.sum(dyn * xhat, axis=0, keepdims=True)
            do_ref[:, sl] = _head_rms_bwd(dyn * w, xhat, r)

    blk = lambda c: pl.BlockSpec((tt, A_WIDTH), lambda i: (i, c))
    return pl.pallas_call(
        body, grid=(T // tt,), in_specs=[blk(0), blk(3), _full((1, A_WIDTH)), blk(0)],
        out_specs=[blk(0), blk(0), _full((1, A_WIDTH))],
        out_shape=[jax.ShapeDtypeStruct((T, A_WIDTH), F32), jax.ShapeDtypeStruct((T, A_WIDTH), BF16),
                   jax.ShapeDtypeStruct((1, A_WIDTH), F32)],
        compiler_params=_cp("arbitrary"), name=name)(o, proj, onorm, dmix)


def _mem_head_masks(n):
    lane = lax.broadcasted_iota(jnp.int32, (n, MEM_WIDTH), 1)
    return [(lane >= m * MEM_HEAD_DIM) & (lane < (m + 1) * MEM_HEAD_DIM) for m in range(MEM_HEADS)]


def _mem_head_rms(x, masks):
    x2 = x * x
    r = jnp.zeros_like(x)
    for mk in masks:
        ms = jnp.sum(jnp.where(mk, x2, 0.0), axis=-1, keepdims=True) * (1.0 / MEM_HEAD_DIM)
        r = jnp.where(mk, lax.rsqrt(ms + EPS), r)
    return x * r, r


def _mem_head_rms_bwd(dxhat, xhat, r, masks):
    t = dxhat * xhat
    m = jnp.zeros_like(t)
    for mk in masks:
        m = jnp.where(mk, jnp.sum(jnp.where(mk, t, 0.0), axis=-1, keepdims=True) * (1.0 / MEM_HEAD_DIM), m)
    return r * (dxhat - xhat * m)


MEM_SCALE = MEM_HEAD_DIM ** -0.5


def _mem_attn_fwd(proj, qcol, mkv, qn_w, kn_w, *, tt, name):
    T = proj.shape[0]

    def body(q_ref, k_ref, v_ref, qw_ref, kw_ref, o_ref):
        qmasks = _mem_head_masks(tt)
        kmasks = _mem_head_masks(MEM_TOKENS)
        qhat, _ = _mem_head_rms(q_ref[...], qmasks)
        qn = qhat * qw_ref[...]
        khat, _ = _mem_head_rms(k_ref[...], kmasks)
        kn = (khat * kw_ref[...]).astype(BF16)
        v = v_ref[...].astype(BF16)
        out = jnp.zeros((tt, MEM_WIDTH), F32)
        for m in range(MEM_HEADS):
            s = _dot_nt(jnp.where(qmasks[m], qn, 0.0), kn) * MEM_SCALE
            s = s - jnp.max(s, axis=-1, keepdims=True)
            p = jnp.exp(s)
            p = p / jnp.sum(p, axis=-1, keepdims=True)
            out = jnp.where(qmasks[m], _dot(p, v), out)
        o_ref[...] = out

    return pl.pallas_call(
        body, grid=(T // tt,),
        in_specs=[pl.BlockSpec((tt, MEM_WIDTH), lambda i: (i, qcol)), pl.BlockSpec((MEM_TOKENS, MEM_WIDTH), lambda i: (0, 0)),
                  pl.BlockSpec((MEM_TOKENS, MEM_WIDTH), lambda i: (0, 1)), _full((1, MEM_WIDTH)), _full((1, MEM_WIDTH))],
        out_specs=pl.BlockSpec((tt, MEM_WIDTH), lambda i: (i, 0)),
        out_shape=jax.ShapeDtypeStruct((T, MEM_WIDTH), F32),
        compiler_params=_cp("parallel"), name=name)(proj, mkv, mkv, qn_w, kn_w)


def _mem_attn_bwd(proj, qcol, mkv, qn_w, kn_w, dmix, *, tt, name):
    T = proj.shape[0]
    nsteps = T // tt
    ocol = (dmix.shape[1] - MEM_WIDTH) // MEM_WIDTH

    def body(q_ref, k_ref, v_ref, qw_ref, kw_ref, do_ref, dq_ref, dkv_ref, dqw_ref, dkw_ref, dk_acc, dv_acc):
        step = pl.program_id(0)

        @pl.when(step == 0)
        def _():
            dk_acc[...] = jnp.zeros_like(dk_acc)
            dv_acc[...] = jnp.zeros_like(dv_acc)
            dqw_ref[...] = jnp.zeros_like(dqw_ref)

        qmasks = _mem_head_masks(tt)
        kmasks = _mem_head_masks(MEM_TOKENS)
        qhat, qr = _mem_head_rms(q_ref[...], qmasks)
        qn = qhat * qw_ref[...]
        khat, kr = _mem_head_rms(k_ref[...], kmasks)
        kn = (khat * kw_ref[...]).astype(BF16)
        v = v_ref[...].astype(BF16)
        dout = do_ref[...]
        dqn = jnp.zeros((tt, MEM_WIDTH), F32)
        dkn = jnp.zeros((MEM_TOKENS, MEM_WIDTH), F32)
        dvv = jnp.zeros((MEM_TOKENS, MEM_WIDTH), F32)
        for m in range(MEM_HEADS):
            qm = jnp.where(qmasks[m], qn, 0.0).astype(BF16)
            s = _dot_nt(qm, kn) * MEM_SCALE
            s = s - jnp.max(s, axis=-1, keepdims=True)
            p = jnp.exp(s)
            p = p / jnp.sum(p, axis=-1, keepdims=True)
            dom = jnp.where(qmasks[m], dout, 0.0).astype(BF16)
            dp = _dot_nt(dom, v)
            ds = (p * (dp - jnp.sum(p * dp, axis=-1, keepdims=True)) * MEM_SCALE).astype(BF16)
            dqn = jnp.where(qmasks[m], _dot(ds, kn), dqn)
            dkn = jnp.where(kmasks[m], _dot_tn(ds, qm), dkn)
            dvv = jnp.where(kmasks[m], _dot_tn(p, dom), dvv)
        dqw_ref[...] += jnp.sum(dqn * qhat, axis=0, keepdims=True)
        dq_ref[...] = _mem_head_rms_bwd(dqn * qw_ref[...], qhat, qr, qmasks).astype(BF16)
        dk_acc[...] += dkn
        dv_acc[...] += dvv

        @pl.when(step == nsteps - 1)
        def _():
            dk = dk_acc[...]
            dkw_ref[...] = jnp.sum(dk * khat, axis=0, keepdims=True)
            dkv_ref[:, :MEM_WIDTH] = _mem_head_rms_bwd(dk * kw_ref[...], khat, kr, kmasks)
            dkv_ref[:, MEM_WIDTH:] = dv_acc[...]

    return pl.pallas_call(
        body, grid=(nsteps,),
        in_specs=[pl.BlockSpec((tt, MEM_WIDTH), lambda i: (i, qcol)), pl.BlockSpec((MEM_TOKENS, MEM_WIDTH), lambda i: (0, 0)),
                  pl.BlockSpec((MEM_TOKENS, MEM_WIDTH), lambda i: (0, 1)), _full((1, MEM_WIDTH)), _full((1, MEM_WIDTH)),
                  pl.BlockSpec((tt, MEM_WIDTH), lambda i: (i, ocol))],
        out_specs=[pl.BlockSpec((tt, MEM_WIDTH), lambda i: (i, 0)), _full((MEM_TOKENS, 2 * MEM_WIDTH)),
                   _full((1, MEM_WIDTH)), _full((1, MEM_WIDTH))],
        out_shape=[jax.ShapeDtypeStruct((T, MEM_WIDTH), BF16), jax.ShapeDtypeStruct((MEM_TOKENS, 2 * MEM_WIDTH), F32),
                   jax.ShapeDtypeStruct((1, MEM_WIDTH), F32), jax.ShapeDtypeStruct((1, MEM_WIDTH), F32)],
        scratch_shapes=[pltpu.VMEM((MEM_TOKENS, MEM_WIDTH), F32), pltpu.VMEM((MEM_TOKENS, MEM_WIDTH), F32)],
        compiler_params=_cp("arbitrary"), name=name)(proj, mkv, mkv, qn_w, kn_w, dmix)


HALF = HEAD_DIM // 2
ATT_SCALE = HEAD_DIM ** -0.5
NEG = -1e30


def _rope_tables(T):
    inv = ROPE_THETA ** (-jnp.arange(HALF, dtype=F32) / HALF)
    ang = jnp.arange(T, dtype=F32)[:, None] * inv[None, :]
    cos, sin = jnp.cos(ang), jnp.sin(ang)
    return jnp.concatenate([cos, cos], axis=-1), jnp.concatenate([-sin, sin], axis=-1)


def _rope(x, cosf, sinsg):
    return x * cosf + pltpu.roll(x, HALF, 1) * sinsg


def _rope_bwd(dy, cosf, sinsg):
    return dy * cosf + pltpu.roll(dy * sinsg, HALF, 1)


def _headnorm_rope_fwd(x, w_heads, cosf, sinsg, *, col0, n_heads, tt, name):
    T = x.shape[0]
    W = n_heads * HEAD_DIM

    def body(x_ref, w_ref, c_ref, s_ref, y_ref):
        c, s = c_ref[...], s_ref[...]
        for h in range(n_heads):
            sl = slice(h * HEAD_DIM, (h + 1) * HEAD_DIM)
            xhat, _ = _head_rms(x_ref[:, sl])
            y_ref[:, sl] = _rope(xhat * w_ref[:, sl], c, s).astype(BF16)

    tbl = pl.BlockSpec((tt, HEAD_DIM), lambda i: (i, 0))
    return pl.pallas_call(
        body, grid=(T // tt,),
        in_specs=[pl.BlockSpec((tt, W), lambda i: (i, col0)), _full((1, W)), tbl, tbl],
        out_specs=pl.BlockSpec((tt, W), lambda i: (i, 0)),
        out_shape=jax.ShapeDtypeStruct((T, W), BF16),
        compiler_params=_cp("parallel"), name=name)(x, w_heads, cosf, sinsg)


def _q_prep_bwd(proj, w_heads, cosf, sinsg, dqs, *, tt, name):
    T = proj.shape[0]
    W = N_GROUPS * B_WIDTH

    def body(x_ref, w_ref, c_ref, s_ref, d0, d1, d2, dx_ref, dw_ref):
        @pl.when(pl.program_id(0) == 0)
        def _():
            dw_ref[...] = jnp.zeros_like(dw_ref)

        c, s = c_ref[...], s_ref[...]
        for gi, d_ref in enumerate((d0, d1, d2)):
            for h in range(B_HEADS):
                sl = slice((gi * B_HEADS + h) * HEAD_DIM, (gi * B_HEADS + h + 1) * HEAD_DIM)
                xhat, r = _head_rms(x_ref[:, sl])
                dyn = _rope_bwd(d_ref[:, h * HEAD_DIM:(h + 1) * HEAD_DIM], c, s)
                dw_ref[:, sl] += jnp.sum(dyn * xhat, axis=0, keepdims=True)
                dx_ref[:, sl] = _head_rms_bwd(dyn * w_ref[:, sl], xhat, r).astype(BF16)

    tbl = pl.BlockSpec((tt, HEAD_DIM), lambda i: (i, 0))
    dyb = pl.BlockSpec((tt, B_WIDTH), lambda i: (i, 0))
    return pl.pallas_call(
        body, grid=(T // tt,),
        in_specs=[pl.BlockSpec((tt, W), lambda i: (i, 0)), _full((1, W)), tbl, tbl, dyb, dyb, dyb],
        out_specs=[pl.BlockSpec((tt, W), lambda i: (i, 0)), _full((1, W))],
        out_shape=[jax.ShapeDtypeStruct((T, W), BF16), jax.ShapeDtypeStruct((1, W), F32)],
        compiler_params=_cp("arbitrary"), name=name)(proj, w_heads, cosf, sinsg, *dqs)


def _kv_prep_bwd(kv, w_heads, cosf, sinsg, dks, dvs, *, tt, name):
    T = kv.shape[0]

    def body(x_ref, w_ref, c_ref, s_ref, k0, k1, k2, v0, v1, v2, dx_ref, dw_ref):
        @pl.when(pl.program_id(0) == 0)
        def _():
            dw_ref[...] = jnp.zeros_like(dw_ref)

        c, s = c_ref[...], s_ref[...]
        for h in range(B_HEADS):
            sl = slice(h * HEAD_DIM, (h + 1) * HEAD_DIM)
            vs = slice(B_WIDTH + h * HEAD_DIM, B_WIDTH + (h + 1) * HEAD_DIM)
            xhat, r = _head_rms(x_ref[:, sl])
            dyn = _rope_bwd(k0[:, sl] + k1[:, sl] + k2[:, sl], c, s)
            dw_ref[:, sl] += jnp.sum(dyn * xhat, axis=0, keepdims=True)
            dx_ref[:, sl] = _head_rms_bwd(dyn * w_ref[:, sl], xhat, r).astype(BF16)
            dx_ref[:, vs] = (v0[:, sl] + v1[:, sl] + v2[:, sl]).astype(BF16)

    tbl = pl.BlockSpec((tt, HEAD_DIM), lambda i: (i, 0))
    dyb = pl.BlockSpec((tt, B_WIDTH), lambda i: (i, 0))
    return pl.pallas_call(
        body, grid=(T // tt,),
        in_specs=[dyb, _full((1, B_WIDTH)), tbl, tbl] + [dyb] * 6,
        out_specs=[pl.BlockSpec((tt, 2 * B_WIDTH), lambda i: (i, 0)), _full((1, B_WIDTH))],
        out_shape=[jax.ShapeDtypeStruct((T, 2 * B_WIDTH), BF16), jax.ShapeDtypeStruct((1, B_WIDTH), F32)],
        compiler_params=_cp("arbitrary"), name=name)(kv, w_heads, cosf, sinsg, *dks, *dvs)


def _band_masks(n_is_first=None):
    row = lax.broadcasted_iota(jnp.int32, (SPAN, SPAN), 0)
    col = lax.broadcasted_iota(jnp.int32, (SPAN, SPAN), 1)
    return row >= col, col >= row


def _dil_views(T, d):
    L = T // d
    return L, L // SPAN


def _dil_fwd(qr, kr, kv, gi, d, *, name):
    T = qr.shape[0]
    L, nb = _dil_views(T, d)

    def body(q_ref, kc_ref, kp_ref, vc_ref, vp_ref, o_ref, lse_ref):
        cur_ok, prev_band = _band_masks()
        prev_ok = prev_band & (pl.program_id(1) > 0)
        for h in range(B_HEADS):
            sl = slice(h * HEAD_DIM, (h + 1) * HEAD_DIM)
            q = q_ref[:, sl]
            sc = jnp.where(cur_ok, _dot_nt(q, kc_ref[:, sl]) * ATT_SCALE, NEG)
            sp = jnp.where(prev_ok, _dot_nt(q, kp_ref[:, sl]) * ATT_SCALE, NEG)
            m = jnp.maximum(jnp.max(sc, axis=-1, keepdims=True), jnp.max(sp, axis=-1, keepdims=True))
            pc = jnp.exp(sc - m)
            pp = jnp.exp(sp - m)
            l = jnp.sum(pc, axis=-1, keepdims=True) + jnp.sum(pp, axis=-1, keepdims=True)
            o_ref[:, sl] = (_dot(pc, vc_ref[:, sl]) + _dot(pp, vp_ref[:, sl])) / l
            lse_ref[:, sl] = jnp.broadcast_to(m + jnp.log(l), (SPAN, HEAD_DIM))

    blk = lambda f: pl.BlockSpec((SPAN, B_WIDTH), f)
    cur = lambda r, n: (n, r)
    prev = lambda r, n: (jnp.maximum(n - 1, 0), r)
    ov = jax.ShapeDtypeStruct((L, d * B_WIDTH), F32)
    o, lse = pl.pallas_call(
        body, grid=(d, nb),
        in_specs=[blk(lambda r, n: (n, r * N_GROUPS + gi)), blk(cur), blk(prev),
                  blk(lambda r, n: (n, 2 * r + 1)), blk(lambda r, n: (jnp.maximum(n - 1, 0), 2 * r + 1))],
        out_specs=[blk(cur), blk(cur)], out_shape=[ov, ov],
        compiler_params=_cp("parallel", "arbitrary"), name=name,
    )(qr.reshape(L, d * N_GROUPS * B_WIDTH), kr.reshape(L, d * B_WIDTH), kr.reshape(L, d * B_WIDTH),
      kv.reshape(L, d * 2 * B_WIDTH), kv.reshape(L, d * 2 * B_WIDTH))
    return o.reshape(T, B_WIDTH), lse.reshape(T, B_WIDTH)


def _dil_combine_fwd(os_, lses, *, tt, name):
    T = os_[0].shape[0]

    def body(o0, o1, o2, l0, l1, l2, y_ref, lse_ref):
        a, b, c = l0[...], l1[...], l2[...]
        m = jnp.maximum(jnp.maximum(a, b), c)
        wa, wb, wc = jnp.exp(a - m), jnp.exp(b - m), jnp.exp(c - m)
        den = wa + wb + wc
        y_ref[...] = (wa * o0[...] + wb * o1[...] + wc * o2[...]) / den
        lse_ref[...] = m + jnp.log(den)

    blk = pl.BlockSpec((tt, B_WIDTH), lambda i: (i, 0))
    sh = jax.ShapeDtypeStruct((T, B_WIDTH), F32)
    return pl.pallas_call(
        body, grid=(T // tt,), in_specs=[blk] * 6, out_specs=[blk, blk], out_shape=[sh, sh],
        compiler_params=_cp("parallel"), name=name)(*os_, *lses)


def _dil_bwd_prep(dmix, mix_main, *, tt, name):
    T = mix_main.shape[0]

    def body(dy_ref, y_ref, dmm_ref, dd_ref):
        for h in range(B_HEADS):
            sl = slice(h * HEAD_DIM, (h + 1) * HEAD_DIM)
            dy = dy_ref[:, sl]
            dmm_ref[:, sl] = dy.astype(BF16)
            dd_ref[:, sl] = jnp.broadcast_to(jnp.sum(dy * y_ref[:, sl], axis=-1, keepdims=True), (tt, HEAD_DIM))

    blk = pl.BlockSpec((tt, B_WIDTH), lambda i: (i, 0))
    return pl.pallas_call(
        body, grid=(T // tt,), in_specs=[blk, blk], out_specs=[blk, blk],
        out_shape=[jax.ShapeDtypeStruct((T, B_WIDTH), BF16), jax.ShapeDtypeStruct((T, B_WIDTH), F32)],
        compiler_params=_cp("parallel"), name=name)(dmix, mix_main)


def _dil_bwd_dq(qr, kr, kv, dmm, lse, dd, gi, d, *, name):
    T = qr.shape[0]
    L, nb = _dil_views(T, d)

    def body(q_ref, kc_ref, kp_ref, vc_ref, vp_ref, dy_ref, lse_ref, dd_ref, dq_ref):
        cur_ok, prev_band = _band_masks()
        prev_ok = prev_band & (pl.program_id(1) > 0)
        for h in range(B_HEADS):
            sl = slice(h * HEAD_DIM, (h + 1) * HEAD_DIM)
            q, dy = q_ref[:, sl], dy_ref[:, sl]
            kc, kp = kc_ref[:, sl], kp_ref[:, sl]
            lse_h = jnp.max(lse_ref[:, sl], axis=-1, keepdims=True)
            dd_h = jnp.max(dd_ref[:, sl], axis=-1, keepdims=True)
            pc = jnp.exp(jnp.where(cur_ok, _dot_nt(q, kc) * ATT_SCALE, NEG) - lse_h)
            pp = jnp.exp(jnp.where(prev_ok, _dot_nt(q, kp) * ATT_SCALE, NEG) - lse_h)
            dsc = pc * (_dot_nt(dy, vc_ref[:, sl]) - dd_h) * ATT_SCALE
            dsp = pp * (_dot_nt(dy, vp_ref[:, sl]) - dd_h) * ATT_SCALE
            dq_ref[:, sl] = _dot(dsc, kc) + _dot(dsp, kp)

    blk = lambda f: pl.BlockSpec((SPAN, B_WIDTH), f)
    cur = lambda r, n: (n, r)
    prev = lambda r, n: (jnp.maximum(n - 1, 0), r)
    v2 = lambda a: a.reshape(L, d * a.shape[1])
    dq = pl.pallas_call(
        body, grid=(d, nb),
        in_specs=[blk(lambda r, n: (n, r * N_GROUPS + gi)), blk(cur), blk(prev),
                  blk(lambda r, n: (n, 2 * r + 1)), blk(lambda r, n: (jnp.maximum(n - 1, 0), 2 * r + 1)),
                  blk(cur), blk(cur), blk(cur)],
        out_specs=blk(cur), out_shape=jax.ShapeDtypeStruct((L, d * B_WIDTH), F32),
        compiler_params=_cp("parallel", "arbitrary"), name=name,
    )(v2(qr), v2(kr), v2(kr), v2(kv), v2(kv), v2(dmm), v2(lse), v2(dd))
    return dq.reshape(T, B_WIDTH)


def _dil_bwd_dkv(qr, kr, kv, dmm, lse, dd, gi, d, *, name):
    T = qr.shape[0]
    L, nb = _dil_views(T, d)

    def body(k_ref, v_ref, q0_ref, q1_ref, dy0_ref, dy1_ref, lse0_ref, lse1_ref, dd0_ref, dd1_ref, dk_ref, dv_ref):
        cur_ok, prev_band = _band_masks()
        next_ok = prev_band & (pl.program_id(1) < nb - 1)
        for h in range(B_HEADS):
            sl = slice(h * HEAD_DIM, (h + 1) * HEAD_DIM)
            k, v = k_ref[:, sl], v_ref[:, sl]
            dk = jnp.zeros((SPAN, HEAD_DIM), F32)
            dv = jnp.zeros((SPAN, HEAD_DIM), F32)
            for ok, q_ref, dy_ref, lse_ref, dd_ref in ((cur_ok, q0_ref, dy0_ref, lse0_ref, dd0_ref),
                                                         (next_ok, q1_ref, dy1_ref, lse1_ref, dd1_ref)):
                q, dy = q_ref[:, sl], dy_ref[:, sl]
                lse_h = jnp.max(lse_ref[:, sl], axis=-1, keepdims=True)
                dd_h = jnp.max(dd_ref[:, sl], axis=-1, keepdims=True)
                p = jnp.exp(jnp.where(ok, _dot_nt(q, k) * ATT_SCALE, NEG) - lse_h)
                ds = p * (_dot_nt(dy, v) - dd_h) * ATT_SCALE
                dk = dk + _dot_tn(ds, q)
                dv = dv + _dot_tn(p, dy)
            dk_ref[:, sl] = dk
            dv_ref[:, sl] = dv

    blk = lambda f: pl.BlockSpec((SPAN, B_WIDTH), f)
    cur = lambda r, n: (n, r)
    nxt = lambda r, n: (jnp.minimum(n + 1, nb - 1), r)
    qcur = lambda r, n: (n, r * N_GROUPS + gi)
    qnxt = lambda r, n: (jnp.minimum(n + 1, nb - 1), r * N_GROUPS + gi)
    v2 = lambda a: a.reshape(L, d * a.shape[1])
    ov = jax.ShapeDtypeStruct((L, d * B_WIDTH), F32)
    dk, dv = pl.pallas_call(
        body, grid=(d, nb),
        in_specs=[blk(cur), blk(lambda r, n: (n, 2 * r + 1)), blk(qcur), blk(qnxt),
                  blk(cur), blk(nxt), blk(cur), blk(nxt), blk(cur), blk(nxt)],
        out_specs=[blk(cur), blk(cur)], out_shape=[ov, ov],
        compiler_params=_cp("parallel", "arbitrary"), name=name,
    )(v2(kr), v2(kv), v2(qr), v2(qr), v2(dmm), v2(dmm), v2(lse), v2(lse), v2(dd), v2(dd))
    return dk.reshape(T, B_WIDTH), dv.reshape(T, B_WIDTH)


A_MQ_COL = 4 * A_WIDTH // MEM_WIDTH
B_MQ_COL = N_GROUPS * B_WIDTH // MEM_WIDTH


def _row(v):
    return v.reshape(1, -1).astype(F32)


def _local_step(x, mem, tgt, get_w, P, put_g):
    T = x.shape[0]
    cosf, sinsg = _rope_tables(T)
    lb_soft = jax.nn.softmax(P["a_lb_logits"].astype(F32), axis=0)
    lb = lb_soft[0:1]
    qw_heads = jnp.repeat(P["b_qnorm"][0], B_HEADS, axis=0).reshape(1, -1)
    kw_heads = jnp.tile(_row(P["b_knorm"]), (1, B_HEADS))
    mqw = [jnp.tile(_row(P["mem_qnorm"][l]), (1, MEM_HEADS)) for l in range(2)]
    mkw = [jnp.tile(_row(P["mem_knorm"][l]), (1, MEM_HEADS)) for l in range(2)]
    nmix = [_row(P["norm_mix"][l]) for l in range(2)]
    nffn = [_row(P["norm_ffn"][l]) for l in range(2)]
    mnorm = [_row(P["mem_norm"][l]) for l in range(2)]
    kvn = _row(P["kv_norm"])
    onorm = _row(P["a_onorm"])
    W = {}

    def w_of(name, after=None):
        if name not in W:
            W[name] = get_w(name, after)
        return W[name]

    proj_a, xn0 = _rms_matmul(x, nmix[0], w_of("a_w_in"), tt=512, tn=1664, wt=True, name="proj_a")
    mkv0, mn0 = _rms_matmul(mem, mnorm[0], w_of("w_mem_kv0"), tt=MEM_TOKENS, tn=2 * MEM_WIDTH, wt=False, name="mem_kv0")
    o_raw, st = _hgrn2_fwd(proj_a, lb, name="hgrn2_fwd")
    mm0 = _a_post_fwd(o_raw, proj_a, onorm, tt=512, name="a_post_fwd")
    mo0 = _mem_attn_fwd(proj_a, A_MQ_COL, mkv0, mqw[0], mkw[0], tt=512, name="mem_attn_fwd0")
    mix0 = jnp.concatenate([mm0, mo0], axis=1)
    hm0 = _mm_res(x, mix0, w_of("w_out0", mix0), tt=512, name="out_proj0")
    gu0, hn0 = _rms_matmul(hm0, nffn[0], w_of("w_gate_up0", hm0), tt=512, tn=1408, wt=True, name="gate_up0")
    h1 = _swiglu_down(hm0, gu0, w_of("w_down0", gu0), tt=256, name="down0")
    kv, hkn = _rms_matmul(h1, kvn, w_of("w_kv", h1), tt=512, tn=768, wt=True, name="kv_proj")
    kr = _headnorm_rope_fwd(kv, kw_heads, cosf, sinsg, col0=0, n_heads=B_HEADS, tt=512, name="k_prep")

    proj_b, xn1 = _rms_matmul(h1, nmix[1], w_of("b_w_in", kr), tt=512, tn=1280, wt=True, name="proj_b")
    mkv1, mn1 = _rms_matmul(mem, mnorm[1], w_of("w_mem_kv1", kr), tt=MEM_TOKENS, tn=2 * MEM_WIDTH, wt=False, name="mem_kv1")
    qr = _headnorm_rope_fwd(proj_b, qw_heads, cosf, sinsg, col0=0, n_heads=N_GROUPS * B_HEADS, tt=512, name="q_prep")
    outs = [_dil_fwd(qr, kr, kv, gi, d, name=f"dil_fwd{gi}") for gi, d in enumerate(DILATIONS)]
    mm1, lse_tot = _dil_combine_fwd([o for o, _ in outs], [s for _, s in outs], tt=512, name="dil_combine")
    mo1 = _mem_attn_fwd(proj_b, B_MQ_COL, mkv1, mqw[1], mkw[1], tt=512, name="mem_attn_fwd1")
    mix1 = jnp.concatenate([mm1, mo1], axis=1)
    hm1 = _mm_res(h1, mix1, w_of("w_out1", mix1), tt=512, name="out_proj1")
    gu1, hn1 = _rms_matmul(hm1, nffn[1], w_of("w_gate_up1", hm1), tt=512, tn=1408, wt=True, name="gate_up1")
    y = _swiglu_down(hm1, gu1, w_of("w_down1", gu1), tt=256, name="down1")
    dy, sq = _loss_kernel(y, tgt, tt=512, name="loss")

    gP = {}
    zeros_mem = jnp.zeros((MEM_TOKENS, D_MODEL), F32)

    def ffn_bwd(l, dh, hm, gu, hn):
        dgu, act = _swiglu_bwd(dh, gu, w_of(f"w_down{l}"), tt=256, name=f"swiglu_bwd{l}")
        g_wd = _mm_tn(act, dh, tt=512, tka=1408, name=f"g_w_down{l}")
        g_wgu = _mm_tn(dgu, hn, tt=512, tka=1408, name=f"g_w_gate_up{l}")
        dgu = put_g({f"w_down{l}": g_wd, f"w_gate_up{l}": g_wgu}, dgu)
        dhm, g_nf = _rms_bwd_dx(hm, nffn[l], w_of(f"w_gate_up{l}"), dgu, dh, tt=256, wt=True, name=f"gate_up_bwd{l}")
        return dhm, g_nf

    def mix_bwd(l, dhm, mix, proj, qcol, mkv, mn):
        dmix = _mm_nt(dhm, w_of(f"w_out{l}"), tt=512, name=f"out_proj_bwd{l}")
        g_wout = _mm_tn(mix, dhm, tt=512, tka=512, name=f"g_w_out{l}")
        dmq, dmkv, dqw, dkw = _mem_attn_bwd(proj, qcol, mkv, mqw[l], mkw[l], dmix, tt=512, name=f"mem_attn_bwd{l}")
        g_wmkv = _mm_tn(mn, dmkv, tt=MEM_TOKENS, tka=512, name=f"g_w_mem_kv{l}")
        dmix = put_g({f"w_out{l}": g_wout, f"w_mem_kv{l}": g_wmkv}, dmix)
        _, g_mn = _rms_bwd_dx(mem, mnorm[l], w_of(f"w_mem_kv{l}"), dmkv, zeros_mem, tt=MEM_TOKENS, wt=False, name=f"mem_kv_bwd{l}")
        fold = lambda v: v.reshape(MEM_HEADS, MEM_HEAD_DIM).sum(axis=0)
        return dmix, dmq, g_mn, fold(dqw), fold(dkw)

    dhm1, g_nf1 = ffn_bwd(1, dy, hm1, gu1, hn1)
    dmix1, dmq1, g_mn1, g_mq1, g_mk1 = mix_bwd(1, dhm1, mix1, proj_b, B_MQ_COL, mkv1, mn1)
    dmm, dd = _dil_bwd_prep(dmix1, mm1, tt=512, name="dil_bwd_prep")
    dqs, dks, dvs = [], [], []
    for gi, d in enumerate(DILATIONS):
        dqs.append(_dil_bwd_dq(qr, kr, kv, dmm, lse_tot, dd, gi, d, name=f"dil_bwd_dq{gi}"))
        dk_g, dv_g = _dil_bwd_dkv(qr, kr, kv, dmm, lse_tot, dd, gi, d, name=f"dil_bwd_dkv{gi}")
        dks.append(dk_g)
        dvs.append(dv_g)
    dq_raw, dqw = _q_prep_bwd(proj_b, qw_heads, cosf, sinsg, dqs, tt=512, name="q_prep_bwd")
    dkv, dkw = _kv_prep_bwd(kv, kw_heads, cosf, sinsg, dks, dvs, tt=512, name="kv_prep_bwd")
    dproj_b = jnp.concatenate([dq_raw, dmq1], axis=1)
    g_wb = _mm_tn(dproj_b, xn1, tt=512, tka=1280, name="g_b_w_in")
    g_wkv = _mm_tn(dkv, hkn, tt=512, tka=768, name="g_w_kv")
    dproj_b = put_g({"b_w_in": g_wb, "w_kv": g_wkv}, dproj_b)
    dh1, g_nm1 = _rms_bwd_dx(h1, nmix[1], w_of("b_w_in"), dproj_b, dhm1, tt=256, wt=True, name="proj_b_bwd")
    dh1, g_kvn = _rms_bwd_dx(h1, kvn, w_of("w_kv"), dkv, dh1, tt=256, wt=True, name="kv_proj_bwd")

    dhm0, g_nf0 = ffn_bwd(0, dh1, hm0, gu0, hn0)
    dmix0, dmq0, g_mn0, g_mq0, g_mk0 = mix_bwd(0, dhm0, mix0, proj_a, A_MQ_COL, mkv0, mn0)
    do_raw, dg, g_onorm = _a_post_bwd(o_raw, proj_a, onorm, dmix0, tt=512, name="a_post_bwd")
    dq, dz, dv, dlb = _hgrn2_bwd(proj_a, lb, st, do_raw, name="hgrn2_bwd")
    dproj_a = jnp.concatenate([dq, dz, dv, dg, dmq0], axis=1)
    dproj_a = put_g({"a_w_in": _mm_tn(dproj_a, xn0, tt=512, tka=1664, name="g_a_w_in")}, dproj_a)
    gx, g_nm0 = _rms_bwd_dx(x, nmix[0], w_of("a_w_in"), dproj_a, dhm0, tt=256, wt=True, name="proj_a_bwd")

    dl0 = lb_soft[0:1] * lb_soft[1:2] * dlb
    gP["a_lb_logits"] = jnp.concatenate([dl0, -dl0], axis=0)
    gP["a_onorm"] = g_onorm
    gP["norm_mix"] = jnp.concatenate([g_nm0, g_nm1], axis=0)
    gP["norm_ffn"] = jnp.concatenate([g_nf0, g_nf1], axis=0)
    gP["b_qnorm"] = dqw.reshape(N_GROUPS, B_HEADS, HEAD_DIM).sum(axis=1)[None]
    gP["kv_norm"] = g_kvn.reshape(-1)
    gP["b_knorm"] = dkw.reshape(B_HEADS, HEAD_DIM).sum(axis=0)
    gP["mem_norm"] = jnp.concatenate([g_mn0, g_mn1], axis=0)
    gP["mem_qnorm"] = jnp.stack([g_mq0, g_mq1])
    gP["mem_knorm"] = jnp.stack([g_mk0, g_mk1])
    return sq, gx, gP


MESH_ID = pl.DeviceIdType.MESH
HBM_SPEC = pl.BlockSpec(memory_space=pltpu.HBM)


def _position():
    return lax.axis_index("x"), lax.axis_index("y"), lax.axis_index("c")


def _all_gather(blocks, *, name):
    n = len(blocks)

    def body(*refs):
        x_refs, out_refs = refs[:n], refs[n:2 * n]
        send_sems, recv_sems, local_sems = refs[2 * n:]
        x, y, c = _position()
        me, sibling = (x, y, c), (x, y, 1 - c)
        chips = [(1 - x, y), (x, 1 - y), (1 - x, 1 - y)]

        def slot(a, px, py, pc):
            return out_refs[a].at[4 * px + 2 * py + pc]

        def copy(a, k, blk, to, src=None):
            return pltpu.make_async_remote_copy(
                src_ref=slot(a, *blk) if src is None else src, dst_ref=slot(a, *blk),
                send_sem=send_sems.at[7 * a + k], recv_sem=recv_sems.at[7 * a + k], device_id=to, device_id_type=MESH_ID)

        mine = [pltpu.make_async_copy(x_refs[a], slot(a, *me), local_sems.at[a]) for a in range(n)]
        for cp in mine:
            cp.start()
        first = []
        for a in range(n):
            first.append(copy(a, 0, me, sibling, src=x_refs[a]))
            first += [copy(a, 1 + j, me, (*chip, c), src=x_refs[a]) for j, chip in enumerate(chips)]
        for cp in first:
            cp.start()
        passed = []
        for j, chip in enumerate(chips):
            for a in range(n):
                copy(a, 1 + j, (*chip, c), me).wait_recv()
                cp = copy(a, 4 + j, (*chip, c), sibling)
                cp.start()
                passed.append(cp)
        for a in range(n):
            copy(a, 0, sibling, me).wait_recv()
            for j, chip in enumerate(chips):
                copy(a, 4 + j, (*chip, 1 - c), me).wait_recv()
        for cp in first + passed:
            cp.wait_send()
        for cp in mine:
            cp.wait()

    return pl.pallas_call(
        body, out_shape=[jax.ShapeDtypeStruct((N_DEV,) + b.shape, b.dtype) for b in blocks],
        in_specs=[HBM_SPEC] * n, out_specs=[HBM_SPEC] * n,
        scratch_shapes=[pltpu.SemaphoreType.DMA((7 * n,)), pltpu.SemaphoreType.DMA((7 * n,)), pltpu.SemaphoreType.DMA((n,))],
        name=name)(*blocks)


def _all_to_all(blocks, *, name):
    n = len(blocks)

    def body(*refs):
        x_refs, out_refs = refs[:n], refs[n:2 * n]
        send_sems, recv_sems, local_sems = refs[2 * n:]
        x, y, c = _position()
        me = 4 * x + 2 * y + c
        mine = [pltpu.make_async_copy(x_refs[a].at[me], out_refs[a].at[me], local_sems.at[a]) for a in range(n)]
        for cp in mine:
            cp.start()
        copies = []
        for a in range(n):
            for k in range(1, N_DEV):
                tx = x if not (k >> 2) & 1 else 1 - x
                ty = y if not (k >> 1) & 1 else 1 - y
                tc = c if not k & 1 else 1 - c
                cp = pltpu.make_async_remote_copy(
                    src_ref=x_refs[a].at[4 * tx + 2 * ty + tc], dst_ref=out_refs[a].at[me],
                    send_sem=send_sems.at[7 * a + k - 1], recv_sem=recv_sems.at[7 * a + k - 1],
                    device_id=(tx, ty, tc), device_id_type=MESH_ID)
                cp.start()
                copies.append(cp)
        for cp in copies:
            cp.wait()
        for cp in mine:
            cp.wait()

    return pl.pallas_call(
        body, out_shape=[jax.ShapeDtypeStruct(b.shape, b.dtype) for b in blocks],
        in_specs=[HBM_SPEC] * n, out_specs=[HBM_SPEC] * n,
        scratch_shapes=[pltpu.SemaphoreType.DMA((7 * n,)), pltpu.SemaphoreType.DMA((7 * n,)), pltpu.SemaphoreType.DMA((n,))],
        name=name)(*blocks)


SEM_SPEC = pl.BlockSpec(memory_space=pltpu.SEMAPHORE)
ANY_SPEC = pl.BlockSpec(memory_space=pl.ANY)
DATAFLOW = pltpu.SideEffectType.DATAFLOW_SIDE_EFFECTING


def _peer(k, x, y, c):
    return (1 - x if (k >> 2) & 1 else x, 1 - y if (k >> 1) & 1 else y, 1 - c if k & 1 else c)


def _own_slot_filled(own_block):
    x, y, c = _position()
    zone = lax.empty((N_DEV,) + own_block.shape, own_block.dtype)
    return lax.dynamic_update_slice_in_dim(zone, own_block[None], 4 * x + 2 * y + c, axis=0)


def _split_start(srcs, scatter, *, name):
    n = len(srcs)
    x, y, c = _position()
    me = 4 * x + 2 * y + c
    lands = [_own_slot_filled(lax.dynamic_index_in_dim(s, me, 0, keepdims=False) if scatter else s) for s in srcs]

    def body(*refs):
        src_refs, land_refs = refs[:n], refs[n:2 * n]
        send_sems, recv_sems = refs[2 * n], refs[2 * n + 1]
        token = refs[-1]
        bx, by, bc = _position()
        bme = 4 * bx + 2 * by + bc
        for a in range(n):
            for k in range(1, N_DEV):
                tx, ty, tc = _peer(k, bx, by, bc)
                src = src_refs[a].at[4 * tx + 2 * ty + tc] if scatter else src_refs[a]
                pltpu.make_async_remote_copy(
                    src_ref=src, dst_ref=land_refs[a].at[bme],
                    send_sem=send_sems.at[7 * a + k - 1], recv_sem=recv_sems.at[7 * a + k - 1],
                    device_id=(tx, ty, tc), device_id_type=MESH_ID).start()
        token[...] = jnp.zeros_like(token)

    hbm = lambda a: pltpu.HBM(a.shape, a.dtype)
    outs = pl.pallas_call(
        body, name=name,
        out_shape=(pltpu.SemaphoreType.DMA((7 * n,)), pltpu.SemaphoreType.DMA((7 * n,)),
                   *[hbm(s) for s in srcs], *[hbm(l) for l in lands], jax.ShapeDtypeStruct((8, 128), F32)),
        in_specs=[HBM_SPEC] * (2 * n),
        out_specs=(SEM_SPEC, SEM_SPEC, *[HBM_SPEC] * (2 * n), pl.BlockSpec(memory_space=pltpu.VMEM)),
        input_output_aliases={i: 2 + i for i in range(2 * n)},
        compiler_params=pltpu.CompilerParams(has_side_effects=DATAFLOW),
    )(*[pltpu.with_memory_space_constraint(s, pltpu.HBM) for s in srcs],
      *[pltpu.with_memory_space_constraint(l, pltpu.HBM) for l in lands])
    return {"n": n, "scatter": scatter, "send": outs[0], "recv": outs[1], "srcs": outs[2:2 + n],
            "lands": outs[2 + n:2 + 2 * n], "token": outs[-1]}


def _split_wait(handle, after, *, name):
    n, scatter = handle["n"], handle["scatter"]

    def body(*refs):
        src_refs, land_refs = refs[:n], refs[n:2 * n]
        send_sems, recv_sems = refs[2 * n], refs[2 * n + 1]
        bx, by, bc = _position()
        for a in range(n):
            for k in range(1, N_DEV):
                src = src_refs[a].at[0] if scatter else src_refs[a]
                cp = pltpu.make_async_remote_copy(
                    src_ref=src, dst_ref=land_refs[a].at[0],
                    send_sem=send_sems.at[7 * a + k - 1], recv_sem=recv_sems.at[7 * a + k - 1],
                    device_id=_peer(k, bx, by, bc), device_id_type=MESH_ID)
                cp.wait_send()
                cp.wait_recv()

    hbm = lambda a: pltpu.HBM(a.shape, a.dtype)
    outs = pl.pallas_call(
        body, name=name,
        out_shape=(*[hbm(s) for s in handle["srcs"]], *[hbm(l) for l in handle["lands"]]),
        in_specs=[HBM_SPEC] * (2 * n) + [SEM_SPEC, SEM_SPEC, ANY_SPEC],
        out_specs=tuple([HBM_SPEC] * (2 * n)),
        input_output_aliases={i: i for i in range(2 * n)},
        compiler_params=pltpu.CompilerParams(has_side_effects=DATAFLOW),
    )(*handle["srcs"], *handle["lands"], handle["send"], handle["recv"], after)
    return list(outs[n:])


def _after(token, value):
    return lax.optimization_barrier((token, value))[1]


def _sum_sources(parts, *, tr, name):
    n, R, C = parts.shape

    def body(p_ref, o_ref):
        acc = p_ref[0].astype(F32)
        for s in range(1, n):
            acc = acc + p_ref[s].astype(F32)
        o_ref[...] = acc

    return pl.pallas_call(
        body, grid=(R // tr,), in_specs=[pl.BlockSpec((n, tr, C), lambda i: (0, i, 0))],
        out_specs=pl.BlockSpec((tr, C), lambda i: (i, 0)),
        out_shape=jax.ShapeDtypeStruct((R, C), F32), compiler_params=_cp("parallel"), name=name)(parts)


def _adamw(g, w, m, v, *, tr, name):
    L, R, C = w.shape
    c1 = 1.0 - ADAM_B1 ** ADAM_STEP
    c2 = 1.0 - ADAM_B2 ** ADAM_STEP

    def body(g_ref, w_ref, m_ref, v_ref, d_ref, nm_ref, nv_ref):
        gv = g_ref[...]
        nm = ADAM_B1 * m_ref[...] + (1.0 - ADAM_B1) * gv
        nv = ADAM_B2 * v_ref[...] + (1.0 - ADAM_B2) * (gv * gv)
        nm_ref[...] = nm
        nv_ref[...] = nv
        d_ref[...] = -ADAM_LR * ((nm / c1) / (jnp.sqrt(nv / c2) + ADAM_EPS) + ADAM_WD * w_ref[...])

    blk = pl.BlockSpec((None, tr, C), lambda l, i: (l, i, 0))
    sh = jax.ShapeDtypeStruct((L, R, C), F32)
    return pl.pallas_call(
        body, grid=(L, R // tr), in_specs=[blk] * 4, out_specs=[blk] * 3, out_shape=[sh] * 3,
        compiler_params=_cp("parallel", "parallel"), name=name)(g, w, m, v)


UNITS = {
    "a_w_in": ("a_w_in", 0, True), "w_mem_kv0": ("w_mem_kv", 0, False), "w_out0": ("w_out", 0, False),
    "w_gate_up0": ("w_gate_up", 0, True), "w_down0": ("w_down", 0, False), "w_kv": ("w_kv", None, True),
    "b_w_in": ("b_w_in", 0, True), "w_mem_kv1": ("w_mem_kv", 1, False), "w_out1": ("w_out", 1, False),
    "w_gate_up1": ("w_gate_up", 1, True), "w_down1": ("w_down", 1, False),
}
BIG = ("a_w_in", "b_w_in", "w_kv", "w_mem_kv", "w_out", "w_gate_up", "w_down")
ADAMW_ROW_TILE = {"a_w_in": 256, "b_w_in": 256, "w_kv": 256, "w_mem_kv": 128, "w_out": 128, "w_gate_up": 256, "w_down": 352}


def _wire_block(weights, unit):
    name, layer, col = UNITS[unit]
    a = weights[name] if layer is None else weights[name][layer]
    return (a.T if col else a).astype(BF16)


def _natural_grads(sums):
    out = {}
    for name in BIG:
        parts = [sums[u].T if col else sums[u] for u, (wn, _, col) in UNITS.items() if wn == name]
        out[name] = parts[0] if name == "w_kv" else jnp.stack(parts)
    return out


SMALL_REPLICATED = ("norm_mix", "norm_ffn", "b_qnorm", "kv_norm", "b_knorm", "mem_norm", "mem_qnorm", "mem_knorm")
SMALL_SHARDED = ("a_lb_logits", "a_onorm")
SMALL_ORDER = SMALL_REPLICATED + SMALL_SHARDED
LANES = 128


def _prod(shape):
    n = 1
    for s in shape:
        n *= s
    return n


def _pack_flat(arrays, rows, cols, dtype):
    flat = jnp.concatenate([a.reshape(-1).astype(dtype) for a in arrays])
    return jnp.pad(flat, (0, rows * cols - flat.shape[0])).reshape(rows, cols)


def _unpack_flat(packed, shapes):
    flat = packed.reshape(-1)
    out, off = [], 0
    for s in shapes:
        out.append(flat[off:off + _prod(s)].reshape(s))
        off += _prod(s)
    return out


def kernel(x, mem, norm_mix, norm_ffn, a_w_in, a_lb_logits, a_onorm, b_w_in, b_qnorm, kv_norm, w_kv, b_knorm, mem_norm, w_mem_kv, mem_qnorm, mem_knorm, w_out, w_gate_up, w_down, loss_target, m_norm_mix, m_norm_ffn, m_a_w_in, m_a_lb_logits, m_a_onorm, m_b_w_in, m_b_qnorm, m_kv_norm, m_w_kv, m_b_knorm, m_mem_norm, m_w_mem_kv, m_mem_qnorm, m_mem_knorm, m_w_out, m_w_gate_up, m_w_down, v_norm_mix, v_norm_ffn, v_a_w_in, v_a_lb_logits, v_a_onorm, v_b_w_in, v_b_qnorm, v_kv_norm, v_w_kv, v_b_knorm, v_mem_norm, v_w_mem_kv, v_mem_qnorm, v_mem_knorm, v_w_out, v_w_gate_up, v_w_down):
    names = ("norm_mix", "norm_ffn", "a_w_in", "a_lb_logits", "a_onorm", "b_w_in", "b_qnorm", "kv_norm", "w_kv", "b_knorm",
             "mem_norm", "w_mem_kv", "mem_qnorm", "mem_knorm", "w_out", "w_gate_up", "w_down")
    w = dict(zip(names, (norm_mix, norm_ffn, a_w_in, a_lb_logits, a_onorm, b_w_in, b_qnorm, kv_norm, w_kv, b_knorm,
                         mem_norm, w_mem_kv, mem_qnorm, mem_knorm, w_out, w_gate_up, w_down)))
    m = dict(zip(names, (m_norm_mix, m_norm_ffn, m_a_w_in, m_a_lb_logits, m_a_onorm, m_b_w_in, m_b_qnorm, m_kv_norm, m_w_kv,
                         m_b_knorm, m_mem_norm, m_w_mem_kv, m_mem_qnorm, m_mem_knorm, m_w_out, m_w_gate_up, m_w_down)))
    v = dict(zip(names, (v_norm_mix, v_norm_ffn, v_a_w_in, v_a_lb_logits, v_a_onorm, v_b_w_in, v_b_qnorm, v_kv_norm, v_w_kv,
                         v_b_knorm, v_mem_norm, v_w_mem_kv, v_mem_qnorm, v_mem_knorm, v_w_out, v_w_gate_up, v_w_down)))

    first = ["a_w_in", "w_mem_kv0"]
    gathered = _all_gather([_wire_block(w, u) for u in first] + [_pack_flat([a_lb_logits, a_onorm], 8, LANES, F32)],
                           name="gather_first")
    full = {u: g.reshape(-1, g.shape[-1]) for u, g in zip(first, gathered)}
    small_in = gathered[-1].reshape(N_DEV, -1)
    P = {n: w[n] for n in SMALL_REPLICATED}
    P["a_lb_logits"] = small_in[:, :192].reshape(N_DEV, 2, 96).transpose(1, 0, 2).reshape(2, A_WIDTH)
    P["a_onorm"] = small_in[:, 192:288].reshape(1, A_WIDTH)
    later = [["w_out0", "w_gate_up0"], ["w_down0", "w_kv"], ["b_w_in", "w_mem_kv1"], ["w_out1", "w_gate_up1", "w_down1"]]
    pending = {}
    token = gathered[-1][0, 0, 0]
    for i, group in enumerate(later):
        handle = _split_start([_after(token, _wire_block(w, u)) for u in group], False, name=f"gather{i}_start")
        token = handle["token"][0, 0]
        for u in group:
            pending[u] = (i, group, handle)
    x0 = _after(token, x[0])

    def get_w(unit, after):
        if unit not in full:
            i, group, handle = pending[unit]
            for u, land in zip(group, _split_wait(handle, after, name=f"gather{i}_wait")):
                full[u] = land.reshape(-1, land.shape[-1])
        return full[unit]

    sent = []

    def put_g(group, value):
        units = list(group)
        handle = _split_start([group[u].reshape(N_DEV, -1, group[u].shape[-1]) for u in units], True,
                              name=f"scatter{len(sent)}_start")
        sent.append((units, handle))
        return _after(handle["token"][0, 0], value)

    sq, gx, gP = _local_step(x0, mem[0], loss_target[0], get_w, P, put_g)
    loss = lax.psum(0.5 * jnp.sum(sq) / D_MODEL, ("x", "y", "c"))

    sums = {}
    for i, (units, handle) in enumerate(sent):
        for u, r in zip(units, _split_wait(handle, gx, name=f"scatter{i}_wait")):
            sums[u] = _sum_sources(r, tr=r.shape[1] if r.shape[1] <= 416 else r.shape[1] // 2, name=f"sum_{u}")
    out = {"grad": _natural_grads(sums), "delta": {}, "new_m": {}, "new_v": {}}
    for n in BIG:
        shape = w[n].shape
        as3 = lambda a: a.reshape((-1,) + shape[-2:])
        res = _adamw(as3(out["grad"][n]), as3(w[n]), as3(m[n]), as3(v[n]), tr=ADAMW_ROW_TILE[n], name=f"adamw_{n}")
        out["grad"][n] = out["grad"][n].reshape(shape)
        for kind, r in zip(("delta", "new_m", "new_v"), res):
            out[kind][n] = r.reshape(shape)

    full_shapes = [(2, A_WIDTH) if n == "a_lb_logits" else (1, A_WIDTH) if n == "a_onorm" else w[n].shape for n in SMALL_ORDER]
    n_small = sum(_prod(s) for s in full_shapes)
    rows_small = -(-n_small // (8 * LANES)) * 8
    g_all, = _all_gather([_pack_flat([gP[n] for n in SMALL_ORDER], rows_small, LANES, F32)], name="gather_small_grads")
    g_small = dict(zip(SMALL_ORDER, _unpack_flat(_sum_sources(g_all, tr=rows_small, name="sum_small_grads"), full_shapes)))
    me = 4 * lax.axis_index("x") + 2 * lax.axis_index("y") + lax.axis_index("c")
    for n in SMALL_SHARDED:
        g_small[n] = lax.dynamic_slice_in_dim(g_small[n], me * 96, 96, axis=1)
    shapes = [w[n].shape for n in SMALL_ORDER]
    rows_upd = -(-sum(_prod(s) for s in shapes) // (8 * LANES)) * 8
    pk = lambda d: _pack_flat([d[n] for n in SMALL_ORDER], rows_upd, LANES, F32)
    res = _adamw(pk(g_small)[None], pk(w)[None], pk(m)[None], pk(v)[None], tr=rows_upd, name="adamw_small")
    out["grad"].update(g_small)
    for kind, packed in zip(("delta", "new_m", "new_v"), res):
        out[kind].update(zip(SMALL_ORDER, _unpack_flat(packed[0], shapes)))

    return (loss, gx[None], *[out["grad"][n] for n in names], *[out["delta"][n] for n in names],
            *[out["new_m"][n] for n in names], *[out["new_v"][n] for n in names])
```

```python
import functools

import jax
import jax.numpy as jnp
from jax import lax
from jax.experimental import pallas as pl
from jax.experimental.pallas import tpu as pltpu

F32 = jnp.float32
BF16 = jnp.bfloat16

N_DEV = 8
D_MODEL = 1024
HEAD_DIM = 128
A_HEADS = 6
A_WIDTH = A_HEADS * HEAD_DIM
CHUNK = 64
B_HEADS = 6
B_WIDTH = B_HEADS * HEAD_DIM
DILATIONS = (1, 4, 16)
SPAN = 128
N_GROUPS = 3
ROPE_THETA = 10000.0
MEM_TOKENS = 256
MEM_HEADS = 4
MEM_HEAD_DIM = 64
MEM_WIDTH = MEM_HEADS * MEM_HEAD_DIM
FFN_HIDDEN = 2816
EPS = 1e-6

ADAM_LR = 0.001
ADAM_B1 = 0.9
ADAM_B2 = 0.999
ADAM_EPS = 1e-08
ADAM_WD = 0.01
ADAM_STEP = 10

V7X_VMEM_LIMIT_BYTES = 56 * 1024 * 1024

NT_DIMS = (((1,), (1,)), ((), ()))
TN_DIMS = (((0,), (0,)), ((), ()))


def _cp(*sem):
    return pltpu.CompilerParams(dimension_semantics=sem, vmem_limit_bytes=V7X_VMEM_LIMIT_BYTES)


def _dot(a, b):
    return jnp.dot(a.astype(BF16), b.astype(BF16), preferred_element_type=F32)


def _dot_nt(a, b):
    return lax.dot_general(a.astype(BF16), b.astype(BF16), NT_DIMS, preferred_element_type=F32)


def _dot_tn(a, b):
    return lax.dot_general(a.astype(BF16), b.astype(BF16), TN_DIMS, preferred_element_type=F32)


def _dot3(m01, x):
    hi = x.astype(BF16)
    r1 = x - hi.astype(F32)
    mid = r1.astype(BF16)
    lo = (r1 - mid.astype(F32)).astype(BF16)
    d = functools.partial(jnp.dot, preferred_element_type=F32)
    return d(m01, hi) + d(m01, mid) + d(m01, lo)


def _sigmoid(x):
    return 1.0 / (1.0 + jnp.exp(-x))


def _full(shape):
    return pl.BlockSpec(shape, lambda *_: (0,) * len(shape))


def _dep(body, n_in, dep):
    if dep is None:
        return body, [], []

    def with_dep(*refs):
        return body(*refs[:n_in], *refs[n_in + 1:])

    return with_dep, [pl.BlockSpec(memory_space=pl.ANY)], [dep]


def _rms_matmul(x, g, w, *, tt, tn, wt, name, dep=None):
    T, K = x.shape
    N = w.shape[0] if wt else w.shape[1]

    def kernel_body(x_ref, g_ref, w_ref, y_ref, xn_ref):
        @pl.when(pl.program_id(1) == 0)
        def _():
            xf = x_ref[...]
            r = lax.rsqrt(jnp.mean(xf * xf, axis=-1, keepdims=True) + EPS)
            xn_ref[...] = (xf * r * g_ref[...]).astype(BF16)

        y_ref[...] = (_dot_nt if wt else _dot)(xn_ref[...], w_ref[...])

    w_spec = pl.BlockSpec((tn, K), lambda i, j: (j, 0)) if wt else pl.BlockSpec((K, tn), lambda i, j: (0, j))
    body, dep_specs, dep_args = _dep(kernel_body, 3, dep)
    return pl.pallas_call(
        body, grid=(T // tt, N // tn),
        in_specs=[pl.BlockSpec((tt, K), lambda i, j: (i, 0)), _full((1, K)), w_spec] + dep_specs,
        out_specs=[pl.BlockSpec((tt, tn), lambda i, j: (i, j)), pl.BlockSpec((tt, K), lambda i, j: (i, 0))],
        out_shape=[jax.ShapeDtypeStruct((T, N), F32), jax.ShapeDtypeStruct((T, K), BF16)],
        compiler_params=_cp("parallel", "arbitrary"), name=name)(x, g, w, *dep_args)


def _mm_res(res, a, w, *, tt, name):
    T, K = a.shape
    N = w.shape[1]

    def body(r_ref, a_ref, w_ref, o_ref):
        o_ref[...] = r_ref[...] + _dot(a_ref[...], w_ref[...])

    return pl.pallas_call(
        body, grid=(T // tt,),
        in_specs=[pl.BlockSpec((tt, N), lambda i: (i, 0)), pl.BlockSpec((tt, K), lambda i: (i, 0)), _full((K, N))],
        out_specs=pl.BlockSpec((tt, N), lambda i: (i, 0)),
        out_shape=jax.ShapeDtypeStruct((T, N), F32),
        compiler_params=_cp("parallel"), name=name)(res, a, w)


def _swiglu_down(h, gu, wd, *, tt, name):
    T, D = h.shape
    Fh = wd.shape[0]

    def body(h_ref, gt_ref, up_ref, w_ref, o_ref):
        gt = gt_ref[...]
        act = gt * _sigmoid(gt) * up_ref[...]
        o_ref[...] = h_ref[...] + _dot(act, w_ref[...])

    return pl.pallas_call(
        body, grid=(T // tt,),
        in_specs=[pl.BlockSpec((tt, D), lambda i: (i, 0)), pl.BlockSpec((tt, Fh), lambda i: (i, 0)),
                  pl.BlockSpec((tt, Fh), lambda i: (i, 1)), _full((Fh, D))],
        out_specs=pl.BlockSpec((tt, D), lambda i: (i, 0)),
        out_shape=jax.ShapeDtypeStruct((T, D), F32),
        compiler_params=_cp("parallel"), name=name)(h, gu, gu, wd)


def _swiglu_bwd(dh, gu, wd, *, tt, name):
    T, D = dh.shape
    Fh = wd.shape[0]

    def body(dh_ref, gt_ref, up_ref, w_ref, dgu_ref, act_ref):
        gt = gt_ref[...]
        up = up_ref[...]
        s = _sigmoid(gt)
        silu = gt * s
        dact = _dot_nt(dh_ref[...], w_ref[...])
        act_ref[...] = (silu * up).astype(BF16)
        dgu_ref[:, :Fh] = (dact * up * (s * (1.0 + gt * (1.0 - s)))).astype(BF16)
        dgu_ref[:, Fh:] = (dact * silu).astype(BF16)

    return pl.pallas_call(
        body, grid=(T // tt,),
        in_specs=[pl.BlockSpec((tt, D), lambda i: (i, 0)), pl.BlockSpec((tt, Fh), lambda i: (i, 0)),
                  pl.BlockSpec((tt, Fh), lambda i: (i, 1)), _full((Fh, D))],
        out_specs=[pl.BlockSpec((tt, 2 * Fh), lambda i: (i, 0)), pl.BlockSpec((tt, Fh), lambda i: (i, 0))],
        out_shape=[jax.ShapeDtypeStruct((T, 2 * Fh), BF16), jax.ShapeDtypeStruct((T, Fh), BF16)],
        compiler_params=_cp("parallel"), name=name)(dh, gu, gu, wd)


def _mm_nt(a, w, *, tt, name):
    T, N = a.shape
    K = w.shape[0]

    def body(a_ref, w_ref, o_ref):
        o_ref[...] = _dot_nt(a_ref[...], w_ref[...])

    return pl.pallas_call(
        body, grid=(T // tt,),
        in_specs=[pl.BlockSpec((tt, N), lambda i: (i, 0)), _full((K, N))],
        out_specs=pl.BlockSpec((tt, K), lambda i: (i, 0)),
        out_shape=jax.ShapeDtypeStruct((T, K), F32),
        compiler_params=_cp("parallel"), name=name)(a, w)


def _mm_tn(a, b, *, tt, tka, name):
    T, Ka = a.shape
    N = b.shape[1]
    last = T // tt - 1

    def body(a_ref, b_ref, o_ref, acc):
        @pl.when(pl.program_id(1) == 0)
        def _():
            acc[...] = jnp.zeros_like(acc)

        acc[...] += _dot_tn(a_ref[...], b_ref[...])

        @pl.when(pl.program_id(1) == last)
        def _():
            o_ref[...] = acc[...].astype(BF16)

    return pl.pallas_call(
        body, grid=(Ka // tka, T // tt),
        in_specs=[pl.BlockSpec((tt, tka), lambda j, t: (t, j)), pl.BlockSpec((tt, N), lambda j, t: (t, 0))],
        out_specs=pl.BlockSpec((tka, N), lambda j, t: (j, 0)),
        out_shape=jax.ShapeDtypeStruct((Ka, N), BF16),
        scratch_shapes=[pltpu.VMEM((tka, N), F32)],
        compiler_params=_cp("parallel", "arbitrary"), name=name)(a, b)


def _rms_bwd_dx(x, g, w, dy, dres, *, tt, wt, name, dep=None):
    T, K = x.shape
    N = w.shape[0] if wt else w.shape[1]

    def kernel_body(x_ref, g_ref, w_ref, dy_ref, dres_ref, dx_ref, dg_ref):
        @pl.when(pl.program_id(0) == 0)
        def _():
            dg_ref[...] = jnp.zeros_like(dg_ref)

        dxn = (_dot if wt else _dot_nt)(dy_ref[...], w_ref[...])
        xf = x_ref[...]
        r = lax.rsqrt(jnp.mean(xf * xf, axis=-1, keepdims=True) + EPS)
        xhat = xf * r
        dg_ref[...] += jnp.sum(dxn * xhat, axis=0, keepdims=True)
        dxhat = dxn * g_ref[...]
        dx_ref[...] = dres_ref[...] + r * (dxhat - xhat * jnp.mean(dxhat * xhat, axis=-1, keepdims=True))

    body, dep_specs, dep_args = _dep(kernel_body, 5, dep)
    return pl.pallas_call(
        body, grid=(T // tt,),
        in_specs=[pl.BlockSpec((tt, K), lambda i: (i, 0)), _full((1, K)), _full(w.shape),
                  pl.BlockSpec((tt, N), lambda i: (i, 0)), pl.BlockSpec((tt, K), lambda i: (i, 0))] + dep_specs,
        out_specs=[pl.BlockSpec((tt, K), lambda i: (i, 0)), _full((1, K))],
        out_shape=[jax.ShapeDtypeStruct((T, K), F32), jax.ShapeDtypeStruct((1, K), F32)],
        compiler_params=_cp("arbitrary"), name=name)(x, g, w, dy, dres, *dep_args)


def _loss_kernel(y, tgt, *, tt, name):
    T, D = y.shape

    def body(y_ref, t_ref, dy_ref, acc_ref):
        @pl.when(pl.program_id(0) == 0)
        def _():
            acc_ref[...] = jnp.zeros_like(acc_ref)

        e = y_ref[...] - t_ref[...]
        dy_ref[...] = e * (1.0 / D)
        acc_ref[...] += jnp.sum(e * e, axis=0, keepdims=True)

    return pl.pallas_call(
        body, grid=(T // tt,),
        in_specs=[pl.BlockSpec((tt, D), lambda i: (i, 0)), pl.BlockSpec((tt, D), lambda i: (i, 0))],
        out_specs=[pl.BlockSpec((tt, D), lambda i: (i, 0)), _full((1, D))],
        out_shape=[jax.ShapeDtypeStruct((T, D), F32), jax.ShapeDtypeStruct((1, D), F32)],
        compiler_params=_cp("arbitrary"), name=name)(y, tgt)


HGRN_TB = 512
HGRN_NCH = HGRN_TB // CHUNK


def _hgrn_chunk_fwd(q, z, lbv, tril01):
    sig = _sigmoid(z)
    f = lbv + (1.0 - lbv) * sig
    kk = 1.0 - f
    b = _dot3(tril01, jnp.log(f))
    bend = b[CHUNK - 1:CHUNK, :]
    sq = _sigmoid(q)
    eb = jnp.exp(b)
    emb = jnp.exp(-b)
    eo = jnp.exp(bend - b)
    dec = jnp.exp(bend)
    return sig, f, kk, sq, eb, emb, eo, dec


def _hgrn2_fwd(proj, lb, *, name):
    T = proj.shape[0]
    nT = T // HGRN_TB
    nC = T // CHUNK

    def body(q_ref, z_ref, v_ref, lb_ref, o_ref, st_ref, state):
        @pl.when(pl.program_id(1) == 0)
        def _():
            state[...] = jnp.zeros_like(state)

        row = lax.broadcasted_iota(jnp.int32, (CHUNK, CHUNK), 0)
        col = lax.broadcasted_iota(jnp.int32, (CHUNK, CHUNK), 1)
        causal = row >= col
        tril01 = causal.astype(BF16)
        lbv = lb_ref[...]

        def chunk(c, carry):
            rows = pl.ds(pl.multiple_of(c * CHUNK, CHUNK), CHUNK)
            q = q_ref[rows, :]
            v = v_ref[rows, :].astype(BF16)
            sig, f, kk, sq, eb, emb, eo, dec = _hgrn_chunk_fwd(q, z_ref[rows, :], lbv, tril01)
            qi = (q * sq * eb).astype(BF16)
            ki = (kk * emb).astype(BF16)
            ko = (kk * eo).astype(BF16)
            st = state[...]
            att = jnp.where(causal, _dot_nt(qi, ki), 0.0)
            o_ref[rows, :] = _dot(att, v) + _dot_nt(qi, st)
            st_ref[c, 0] = st
            state[...] = st * dec + _dot_tn(v, ko)
            return carry

        lax.fori_loop(0, HGRN_NCH, chunk, 0)

    hb = lambda off: pl.BlockSpec((HGRN_TB, HEAD_DIM), lambda h, i: (i, off + h))
    return pl.pallas_call(
        body, grid=(A_HEADS, nT),
        in_specs=[hb(0), hb(A_HEADS), hb(2 * A_HEADS), pl.BlockSpec((1, HEAD_DIM), lambda h, i: (0, h))],
        out_specs=[hb(0), pl.BlockSpec((HGRN_NCH, 1, HEAD_DIM, HEAD_DIM), lambda h, i: (i, h, 0, 0))],
        out_shape=[jax.ShapeDtypeStruct((T, A_WIDTH), F32), jax.ShapeDtypeStruct((nC, A_HEADS, HEAD_DIM, HEAD_DIM), F32)],
        scratch_shapes=[pltpu.VMEM((HEAD_DIM, HEAD_DIM), F32)],
        compiler_params=_cp("parallel", "arbitrary"), name=name)(proj, proj, proj, lb)


def _hgrn2_bwd(proj, lb, st_all, do, *, name):
    T = proj.shape[0]
    nT = T // HGRN_TB

    def body(q_ref, z_ref, v_ref, lb_ref, st_ref, do_ref, dq_ref, dz_ref, dv_ref, dlb_ref, dstate):
        @pl.when(pl.program_id(1) == 0)
        def _():
            dstate[...] = jnp.zeros_like(dstate)
            dlb_ref[...] = jnp.zeros_like(dlb_ref)

        row = lax.broadcasted_iota(jnp.int32, (CHUNK, CHUNK), 0)
        col = lax.broadcasted_iota(jnp.int32, (CHUNK, CHUNK), 1)
        causal = row >= col
        tril01 = causal.astype(BF16)
        triu01 = (row <= col).astype(BF16)
        lbv = lb_ref[...]

        def chunk(cc, carry):
            c = HGRN_NCH - 1 - cc
            rows = pl.ds(pl.multiple_of(c * CHUNK, CHUNK), CHUNK)
            q = q_ref[rows, :]
            v = v_ref[rows, :].astype(BF16)
            sig, f, kk, sq, eb, emb, eo, dec = _hgrn_chunk_fwd(q, z_ref[rows, :], lbv, tril01)
            qi32 = q * sq * eb
            ki32 = kk * emb
            ko32 = kk * eo
            qi, ki, ko = qi32.astype(BF16), ki32.astype(BF16), ko32.astype(BF16)
            att = jnp.where(causal, _dot_nt(qi, ki), 0.0).astype(BF16)
            dout = do_ref[rows, :].astype(BF16)
            st = st_ref[c, 0]
            dst = dstate[...]
            dst16 = dst.astype(BF16)
            datt = jnp.where(causal, _dot_nt(dout, v), 0.0).astype(BF16)
            dqi = _dot(datt, ki) + _dot(dout, st)
            dki = _dot_tn(datt, qi)
            dv_ref[rows, :] = (_dot_tn(att, dout) + _dot_nt(ko, dst16)).astype(BF16)
            dko = _dot(v, dst16)
            ddec = jnp.sum(dst * st, axis=0, keepdims=True)
            dstate[...] = dst * dec + _dot_tn(dout, qi)
            dkk = dki * emb + dko * eo
            db = dqi * qi32 - dki * ki32 - dko * ko32
            dbend = jnp.sum(dko * ko32, axis=0, keepdims=True) + ddec * dec
            dlogf = _dot3(triu01, db) + dbend
            df = dlogf / f - dkk
            dz_ref[rows, :] = (df * (1.0 - lbv) * sig * (1.0 - sig)).astype(BF16)
            dlb_ref[...] += jnp.sum(df * (1.0 - sig), axis=0, keepdims=True)
            dq_ref[rows, :] = (dqi * eb * (sq * (1.0 + q * (1.0 - sq)))).astype(BF16)
            return carry

        lax.fori_loop(0, HGRN_NCH, chunk, 0)

    hb = lambda off: pl.BlockSpec((HGRN_TB, HEAD_DIM), lambda h, i: (nT - 1 - i, off + h))
    hlb = pl.BlockSpec((1, HEAD_DIM), lambda h, i: (0, h))
    o16 = jax.ShapeDtypeStruct((T, A_WIDTH), BF16)
    return pl.pallas_call(
        body, grid=(A_HEADS, nT),
        in_specs=[hb(0), hb(A_HEADS), hb(2 * A_HEADS), hlb,
                  pl.BlockSpec((HGRN_NCH, 1, HEAD_DIM, HEAD_DIM), lambda h, i: (nT - 1 - i, h, 0, 0)), hb(0)],
        out_specs=[hb(0), hb(0), hb(0), hlb],
        out_shape=[o16, o16, o16, jax.ShapeDtypeStruct((1, A_WIDTH), F32)],
        scratch_shapes=[pltpu.VMEM((HEAD_DIM, HEAD_DIM), F32)],
        compiler_params=_cp("parallel", "arbitrary"), name=name)(proj, proj, proj, lb, st_all, do)


def _head_rms(x):
    r = lax.rsqrt(jnp.mean(x * x, axis=-1, keepdims=True) + EPS)
    return x * r, r


def _head_rms_bwd(dxhat, xhat, r):
    return r * (dxhat - xhat * jnp.mean(dxhat * xhat, axis=-1, keepdims=True))


def _a_post_fwd(o, proj, onorm, *, tt, name):
    T = o.shape[0]

    def body(o_ref, g_ref, w_ref, y_ref):
        for h in range(A_HEADS):
            sl = slice(h * HEAD_DIM, (h + 1) * HEAD_DIM)
            xhat, _ = _head_rms(o_ref[:, sl])
            g = g_ref[:, sl]
            y_ref[:, sl] = xhat * w_ref[:, sl] * (g * _sigmoid(g))

    blk = lambda c: pl.BlockSpec((tt, A_WIDTH), lambda i: (i, c))
    return pl.pallas_call(
        body, grid=(T // tt,), in_specs=[blk(0), blk(3), _full((1, A_WIDTH))], out_specs=blk(0),
        out_shape=jax.ShapeDtypeStruct((T, A_WIDTH), F32),
        compiler_params=_cp("parallel"), name=name)(o, proj, onorm)


def _a_post_bwd(o, proj, onorm, dmix, *, tt, name, dep=None):
    T = o.shape[0]

    def kernel_body(o_ref, g_ref, w_ref, dy_ref, do_ref, dg_ref, dw_ref):
        @pl.when(pl.program_id(0) == 0)
        def _():
            dw_ref[...] = jnp.zeros_like(dw_ref)

        for h in range(A_HEADS):
            sl = slice(h * HEAD_DIM, (h + 1) * HEAD_DIM)
            xhat, r = _head_rms(o_ref[:, sl])
            g = g_ref[:, sl]
            s = _sigmoid(g)
            dy = dy_ref[:, sl]
            w = w_ref[:, sl]
            dg_ref[:, sl] = (dy * xhat * w * (s * (1.0 + g * (1.0 - s)))).astype(BF16)
            dyn = dy * (g * s)
            dw_ref[:, sl] += jnp.sum(dyn * xhat, axis=0, keepdims=True)
            do_ref[:, sl] = _head_rms_bwd(dyn * w, xhat, r)

    blk = lambda c: pl.BlockSpec((tt, A_WIDTH), lambda i: (i, c))
    body, dep_specs, dep_args = _dep(kernel_body, 4, dep)
    return pl.pallas_call(
        body, grid=(T // tt,), in_specs=[blk(0), blk(3), _full((1, A_WIDTH)), blk(0)] + dep_specs,
        out_specs=[blk(0), blk(0), _full((1, A_WIDTH))],
        out_shape=[jax.ShapeDtypeStruct((T, A_WIDTH), F32), jax.ShapeDtypeStruct((T, A_WIDTH), BF16),
                   jax.ShapeDtypeStruct((1, A_WIDTH), F32)],
        compiler_params=_cp("arbitrary"), name=name)(o, proj, onorm, dmix, *dep_args)


def _mem_head_masks(n):
    lane = lax.broadcasted_iota(jnp.int32, (n, MEM_WIDTH), 1)
    return [(lane >= m * MEM_HEAD_DIM) & (lane < (m + 1) * MEM_HEAD_DIM) for m in range(MEM_HEADS)]


def _mem_head_rms(x, masks):
    x2 = x * x
    r = jnp.zeros_like(x)
    for mk in masks:
        ms = jnp.sum(jnp.where(mk, x2, 0.0), axis=-1, keepdims=True) * (1.0 / MEM_HEAD_DIM)
        r = jnp.where(mk, lax.rsqrt(ms + EPS), r)
    return x * r, r


def _mem_head_rms_bwd(dxhat, xhat, r, masks):
    t = dxhat * xhat
    m = jnp.zeros_like(t)
    for mk in masks:
        m = jnp.where(mk, jnp.sum(jnp.where(mk, t, 0.0), axis=-1, keepdims=True) * (1.0 / MEM_HEAD_DIM), m)
    return r * (dxhat - xhat * m)


MEM_SCALE = MEM_HEAD_DIM ** -0.5


def _mem_attn_fwd(proj, qcol, mkv, qn_w, kn_w, *, tt, name):
    T = proj.shape[0]

    def body(q_ref, k_ref, v_ref, qw_ref, kw_ref, o_ref):
        qmasks = _mem_head_masks(tt)
        kmasks = _mem_head_masks(MEM_TOKENS)
        qhat, _ = _mem_head_rms(q_ref[...], qmasks)
        qn = qhat * qw_ref[...]
        khat, _ = _mem_head_rms(k_ref[...], kmasks)
        kn = (khat * kw_ref[...]).astype(BF16)
        v = v_ref[...].astype(BF16)
        out = jnp.zeros((tt, MEM_WIDTH), F32)
        for m in range(MEM_HEADS):
            s = _dot_nt(jnp.where(qmasks[m], qn, 0.0), kn) * MEM_SCALE
            s = s - jnp.max(s, axis=-1, keepdims=True)
            p = jnp.exp(s)
            p = p / jnp.sum(p, axis=-1, keepdims=True)
            out = jnp.where(qmasks[m], _dot(p, v), out)
        o_ref[...] = out

    return pl.pallas_call(
        body, grid=(T // tt,),
        in_specs=[pl.BlockSpec((tt, MEM_WIDTH), lambda i: (i, qcol)), pl.BlockSpec((MEM_TOKENS, MEM_WIDTH), lambda i: (0, 0)),
                  pl.BlockSpec((MEM_TOKENS, MEM_WIDTH), lambda i: (0, 1)), _full((1, MEM_WIDTH)), _full((1, MEM_WIDTH))],
        out_specs=pl.BlockSpec((tt, MEM_WIDTH), lambda i: (i, 0)),
        out_shape=jax.ShapeDtypeStruct((T, MEM_WIDTH), F32),
        compiler_params=_cp("parallel"), name=name)(proj, mkv, mkv, qn_w, kn_w)


def _mem_attn_bwd(proj, qcol, mkv, qn_w, kn_w, dmix, *, tt, name):
    T = proj.shape[0]
    nsteps = T // tt
    ocol = (dmix.shape[1] - MEM_WIDTH) // MEM_WIDTH

    def body(q_ref, k_ref, v_ref, qw_ref, kw_ref, do_ref, dq_ref, dkv_ref, dqw_ref, dkw_ref, dk_acc, dv_acc):
        step = pl.program_id(0)

        @pl.when(step == 0)
        def _():
            dk_acc[...] = jnp.zeros_like(dk_acc)
            dv_acc[...] = jnp.zeros_like(dv_acc)
            dqw_ref[...] = jnp.zeros_like(dqw_ref)

        qmasks = _mem_head_masks(tt)
        kmasks = _mem_head_masks(MEM_TOKENS)
        qhat, qr = _mem_head_rms(q_ref[...], qmasks)
        qn = qhat * qw_ref[...]
        khat, kr = _mem_head_rms(k_ref[...], kmasks)
        kn = (khat * kw_ref[...]).astype(BF16)
        v = v_ref[...].astype(BF16)
        dout = do_ref[...]
        dqn = jnp.zeros((tt, MEM_WIDTH), F32)
        dkn = jnp.zeros((MEM_TOKENS, MEM_WIDTH), F32)
        dvv = jnp.zeros((MEM_TOKENS, MEM_WIDTH), F32)
        for m in range(MEM_HEADS):
            qm = jnp.where(qmasks[m], qn, 0.0).astype(BF16)
            s = _dot_nt(qm, kn) * MEM_SCALE
            s = s - jnp.max(s, axis=-1, keepdims=True)
            p = jnp.exp(s)
            p = p / jnp.sum(p, axis=-1, keepdims=True)
            dom = jnp.where(qmasks[m], dout, 0.0).astype(BF16)
            dp = _dot_nt(dom, v)
            ds = (p * (dp - jnp.sum(p * dp, axis=-1, keepdims=True)) * MEM_SCALE).astype(BF16)
            dqn = jnp.where(qmasks[m], _dot(ds, kn), dqn)
            dkn = jnp.where(kmasks[m], _dot_tn(ds, qm), dkn)
            dvv = jnp.where(kmasks[m], _dot_tn(p, dom), dvv)
        dqw_ref[...] += jnp.sum(dqn * qhat, axis=0, keepdims=True)
        dq_ref[...] = _mem_head_rms_bwd(dqn * qw_ref[...], qhat, qr, qmasks).astype(BF16)
        dk_acc[...] += dkn
        dv_acc[...] += dvv

        @pl.when(step == nsteps - 1)
        def _():
            dk = dk_acc[...]
            dkw_ref[...] = jnp.sum(dk * khat, axis=0, keepdims=True)
            dkv_ref[:, :MEM_WIDTH] = _mem_head_rms_bwd(dk * kw_ref[...], khat, kr, kmasks)
            dkv_ref[:, MEM_WIDTH:] = dv_acc[...]

    return pl.pallas_call(
        body, grid=(nsteps,),
        in_specs=[pl.BlockSpec((tt, MEM_WIDTH), lambda i: (i, qcol)), pl.BlockSpec((MEM_TOKENS, MEM_WIDTH), lambda i: (0, 0)),
                  pl.BlockSpec((MEM_TOKENS, MEM_WIDTH), lambda i: (0, 1)), _full((1, MEM_WIDTH)), _full((1, MEM_WIDTH)),
                  pl.BlockSpec((tt, MEM_WIDTH), lambda i: (i, ocol))],
        out_specs=[pl.BlockSpec((tt, MEM_WIDTH), lambda i: (i, 0)), _full((MEM_TOKENS, 2 * MEM_WIDTH)),
                   _full((1, MEM_WIDTH)), _full((1, MEM_WIDTH))],
        out_shape=[jax.ShapeDtypeStruct((T, MEM_WIDTH), BF16), jax.ShapeDtypeStruct((MEM_TOKENS, 2 * MEM_WIDTH), F32),
                   jax.ShapeDtypeStruct((1, MEM_WIDTH), F32), jax.ShapeDtypeStruct((1, MEM_WIDTH), F32)],
        scratch_shapes=[pltpu.VMEM((MEM_TOKENS, MEM_WIDTH), F32), pltpu.VMEM((MEM_TOKENS, MEM_WIDTH), F32)],
        compiler_params=_cp("arbitrary"), name=name)(proj, mkv, mkv, qn_w, kn_w, dmix)


HALF = HEAD_DIM // 2
ATT_SCALE = HEAD_DIM ** -0.5
NEG = -1e30


def _rope_tables(T):
    inv = ROPE_THETA ** (-jnp.arange(HALF, dtype=F32) / HALF)
    ang = jnp.arange(T, dtype=F32)[:, None] * inv[None, :]
    cos, sin = jnp.cos(ang), jnp.sin(ang)
    return jnp.concatenate([cos, cos], axis=-1), jnp.concatenate([-sin, sin], axis=-1)


def _rope(x, cosf, sinsg):
    return x * cosf + pltpu.roll(x, HALF, 1) * sinsg


def _rope_bwd(dy, cosf, sinsg):
    return dy * cosf + pltpu.roll(dy * sinsg, HALF, 1)


def _headnorm_rope_fwd(x, w_heads, cosf, sinsg, *, col0, n_heads, tt, name):
    T = x.shape[0]
    W = n_heads * HEAD_DIM

    def body(x_ref, w_ref, c_ref, s_ref, y_ref):
        c, s = c_ref[...], s_ref[...]
        for h in range(n_heads):
            sl = slice(h * HEAD_DIM, (h + 1) * HEAD_DIM)
            xhat, _ = _head_rms(x_ref[:, sl])
            y_ref[:, sl] = _rope(xhat * w_ref[:, sl], c, s).astype(BF16)

    tbl = pl.BlockSpec((tt, HEAD_DIM), lambda i: (i, 0))
    return pl.pallas_call(
        body, grid=(T // tt,),
        in_specs=[pl.BlockSpec((tt, W), lambda i: (i, col0)), _full((1, W)), tbl, tbl],
        out_specs=pl.BlockSpec((tt, W), lambda i: (i, 0)),
        out_shape=jax.ShapeDtypeStruct((T, W), BF16),
        compiler_params=_cp("parallel"), name=name)(x, w_heads, cosf, sinsg)


def _q_prep_bwd(proj, w_heads, cosf, sinsg, dqs, *, tt, name):
    T = proj.shape[0]
    W = N_GROUPS * B_WIDTH

    def body(x_ref, w_ref, c_ref, s_ref, d0, d1, d2, dx_ref, dw_ref):
        @pl.when(pl.program_id(0) == 0)
        def _():
            dw_ref[...] = jnp.zeros_like(dw_ref)

        c, s = c_ref[...], s_ref[...]
        for gi, d_ref in enumerate((d0, d1, d2)):
            for h in range(B_HEADS):
                sl = slice((gi * B_HEADS + h) * HEAD_DIM, (gi * B_HEADS + h + 1) * HEAD_DIM)
                xhat, r = _head_rms(x_ref[:, sl])
                dyn = _rope_bwd(d_ref[:, h * HEAD_DIM:(h + 1) * HEAD_DIM], c, s)
                dw_ref[:, sl] += jnp.sum(dyn * xhat, axis=0, keepdims=True)
                dx_ref[:, sl] = _head_rms_bwd(dyn * w_ref[:, sl], xhat, r).astype(BF16)

    tbl = pl.BlockSpec((tt, HEAD_DIM), lambda i: (i, 0))
    dyb = pl.BlockSpec((tt, B_WIDTH), lambda i: (i, 0))
    return pl.pallas_call(
        body, grid=(T // tt,),
        in_specs=[pl.BlockSpec((tt, W), lambda i: (i, 0)), _full((1, W)), tbl, tbl, dyb, dyb, dyb],
        out_specs=[pl.BlockSpec((tt, W), lambda i: (i, 0)), _full((1, W))],
        out_shape=[jax.ShapeDtypeStruct((T, W), BF16), jax.ShapeDtypeStruct((1, W), F32)],
        compiler_params=_cp("arbitrary"), name=name)(proj, w_heads, cosf, sinsg, *dqs)


def _kv_prep_bwd(kv, w_heads, cosf, sinsg, dks, dvs, *, tt, name):
    T = kv.shape[0]

    def body(x_ref, w_ref, c_ref, s_ref, k0, k1, k2, v0, v1, v2, dx_ref, dw_ref):
        @pl.when(pl.program_id(0) == 0)
        def _():
            dw_ref[...] = jnp.zeros_like(dw_ref)

        c, s = c_ref[...], s_ref[...]
        for h in range(B_HEADS):
            sl = slice(h * HEAD_DIM, (h + 1) * HEAD_DIM)
            vs = slice(B_WIDTH + h * HEAD_DIM, B_WIDTH + (h + 1) * HEAD_DIM)
            xhat, r = _head_rms(x_ref[:, sl])
            dyn = _rope_bwd(k0[:, sl] + k1[:, sl] + k2[:, sl], c, s)
            dw_ref[:, sl] += jnp.sum(dyn * xhat, axis=0, keepdims=True)
            dx_ref[:, sl] = _head_rms_bwd(dyn * w_ref[:, sl], xhat, r).astype(BF16)
            dx_ref[:, vs] = (v0[:, sl] + v1[:, sl] + v2[:, sl]).astype(BF16)

    tbl = pl.BlockSpec((tt, HEAD_DIM), lambda i: (i, 0))
    dyb = pl.BlockSpec((tt, B_WIDTH), lambda i: (i, 0))
    return pl.pallas_call(
        body, grid=(T // tt,),
        in_specs=[dyb, _full((1, B_WIDTH)), tbl, tbl] + [dyb] * 6,
        out_specs=[pl.BlockSpec((tt, 2 * B_WIDTH), lambda i: (i, 0)), _full((1, B_WIDTH))],
        out_shape=[jax.ShapeDtypeStruct((T, 2 * B_WIDTH), BF16), jax.ShapeDtypeStruct((1, B_WIDTH), F32)],
        compiler_params=_cp("arbitrary"), name=name)(kv, w_heads, cosf, sinsg, *dks, *dvs)


def _band_masks(n_is_first=None):
    row = lax.broadcasted_iota(jnp.int32, (SPAN, SPAN), 0)
    col = lax.broadcasted_iota(jnp.int32, (SPAN, SPAN), 1)
    return row >= col, col >= row


def _dil_views(T, d):
    L = T // d
    return L, L // SPAN


def _dil_fwd(qr, kr, kv, gi, d, *, name):
    T = qr.shape[0]
    L, nb = _dil_views(T, d)

    def body(q_ref, kc_ref, kp_ref, vc_ref, vp_ref, o_ref, lse_ref):
        cur_ok, prev_band = _band_masks()
        prev_ok = prev_band & (pl.program_id(1) > 0)
        for h in range(B_HEADS):
            sl = slice(h * HEAD_DIM, (h + 1) * HEAD_DIM)
            q = q_ref[:, sl]
            sc = jnp.where(cur_ok, _dot_nt(q, kc_ref[:, sl]) * ATT_SCALE, NEG)
            sp = jnp.where(prev_ok, _dot_nt(q, kp_ref[:, sl]) * ATT_SCALE, NEG)
            m = jnp.maximum(jnp.max(sc, axis=-1, keepdims=True), jnp.max(sp, axis=-1, keepdims=True))
            pc = jnp.exp(sc - m)
            pp = jnp.exp(sp - m)
            l = jnp.sum(pc, axis=-1, keepdims=True) + jnp.sum(pp, axis=-1, keepdims=True)
            o_ref[:, sl] = (_dot(pc, vc_ref[:, sl]) + _dot(pp, vp_ref[:, sl])) / l
            lse_ref[:, sl] = jnp.broadcast_to(m + jnp.log(l), (SPAN, HEAD_DIM))

    blk = lambda f: pl.BlockSpec((SPAN, B_WIDTH), f)
    cur = lambda r, n: (n, r)
    prev = lambda r, n: (jnp.maximum(n - 1, 0), r)
    ov = jax.ShapeDtypeStruct((L, d * B_WIDTH), F32)
    o, lse = pl.pallas_call(
        body, grid=(d, nb),
        in_specs=[blk(lambda r, n: (n, r * N_GROUPS + gi)), blk(cur), blk(prev),
                  blk(lambda r, n: (n, 2 * r + 1)), blk(lambda r, n: (jnp.maximum(n - 1, 0), 2 * r + 1))],
        out_specs=[blk(cur), blk(cur)], out_shape=[ov, ov],
        compiler_params=_cp("parallel", "arbitrary"), name=name,
    )(qr.reshape(L, d * N_GROUPS * B_WIDTH), kr.reshape(L, d * B_WIDTH), kr.reshape(L, d * B_WIDTH),
      kv.reshape(L, d * 2 * B_WIDTH), kv.reshape(L, d * 2 * B_WIDTH))
    return o.reshape(T, B_WIDTH), lse.reshape(T, B_WIDTH)


def _dil_combine_fwd(os_, lses, *, tt, name):
    T = os_[0].shape[0]

    def body(o0, o1, o2, l0, l1, l2, y_ref, lse_ref):
        a, b, c = l0[...], l1[...], l2[...]
        m = jnp.maximum(jnp.maximum(a, b), c)
        wa, wb, wc = jnp.exp(a - m), jnp.exp(b - m), jnp.exp(c - m)
        den = wa + wb + wc
        y_ref[...] = (wa * o0[...] + wb * o1[...] + wc * o2[...]) / den
        lse_ref[...] = m + jnp.log(den)

    blk = pl.BlockSpec((tt, B_WIDTH), lambda i: (i, 0))
    sh = jax.ShapeDtypeStruct((T, B_WIDTH), F32)
    return pl.pallas_call(
        body, grid=(T // tt,), in_specs=[blk] * 6, out_specs=[blk, blk], out_shape=[sh, sh],
        compiler_params=_cp("parallel"), name=name)(*os_, *lses)


def _dil_bwd_prep(dmix, mix_main, *, tt, name, dep=None):
    T = mix_main.shape[0]

    def kernel_body(dy_ref, y_ref, dmm_ref, dd_ref):
        for h in range(B_HEADS):
            sl = slice(h * HEAD_DIM, (h + 1) * HEAD_DIM)
            dy = dy_ref[:, sl]
            dmm_ref[:, sl] = dy.astype(BF16)
            dd_ref[:, sl] = jnp.broadcast_to(jnp.sum(dy * y_ref[:, sl], axis=-1, keepdims=True), (tt, HEAD_DIM))

    blk = pl.BlockSpec((tt, B_WIDTH), lambda i: (i, 0))
    body, dep_specs, dep_args = _dep(kernel_body, 2, dep)
    return pl.pallas_call(
        body, grid=(T // tt,), in_specs=[blk, blk] + dep_specs, out_specs=[blk, blk],
        out_shape=[jax.ShapeDtypeStruct((T, B_WIDTH), BF16), jax.ShapeDtypeStruct((T, B_WIDTH), F32)],
        compiler_params=_cp("parallel"), name=name)(dmix, mix_main, *dep_args)


def _dil_bwd_dq(qr, kr, kv, dmm, lse, dd, gi, d, *, name):
    T = qr.shape[0]
    L, nb = _dil_views(T, d)

    def body(q_ref, kc_ref, kp_ref, vc_ref, vp_ref, dy_ref, lse_ref, dd_ref, dq_ref):
        cur_ok, prev_band = _band_masks()
        prev_ok = prev_band & (pl.program_id(1) > 0)
        for h in range(B_HEADS):
            sl = slice(h * HEAD_DIM, (h + 1) * HEAD_DIM)
            q, dy = q_ref[:, sl], dy_ref[:, sl]
            kc, kp = kc_ref[:, sl], kp_ref[:, sl]
            lse_h = jnp.max(lse_ref[:, sl], axis=-1, keepdims=True)
            dd_h = jnp.max(dd_ref[:, sl], axis=-1, keepdims=True)
            pc = jnp.exp(jnp.where(cur_ok, _dot_nt(q, kc) * ATT_SCALE, NEG) - lse_h)
            pp = jnp.exp(jnp.where(prev_ok, _dot_nt(q, kp) * ATT_SCALE, NEG) - lse_h)
            dsc = pc * (_dot_nt(dy, vc_ref[:, sl]) - dd_h) * ATT_SCALE
            dsp = pp * (_dot_nt(dy, vp_ref[:, sl]) - dd_h) * ATT_SCALE
            dq_ref[:, sl] = _dot(dsc, kc) + _dot(dsp, kp)

    blk = lambda f: pl.BlockSpec((SPAN, B_WIDTH), f)
    cur = lambda r, n: (n, r)
    prev = lambda r, n: (jnp.maximum(n - 1, 0), r)
    v2 = lambda a: a.reshape(L, d * a.shape[1])
    dq = pl.pallas_call(
        body, grid=(d, nb),
        in_specs=[blk(lambda r, n: (n, r * N_GROUPS + gi)), blk(cur), blk(prev),
                  blk(lambda r, n: (n, 2 * r + 1)), blk(lambda r, n: (jnp.maximum(n - 1, 0), 2 * r + 1)),
                  blk(cur), blk(cur), blk(cur)],
        out_specs=blk(cur), out_shape=jax.ShapeDtypeStruct((L, d * B_WIDTH), F32),
        compiler_params=_cp("parallel", "arbitrary"), name=name,
    )(v2(qr), v2(kr), v2(kr), v2(kv), v2(kv), v2(dmm), v2(lse), v2(dd))
    return dq.reshape(T, B_WIDTH)


def _dil_bwd_dkv(qr, kr, kv, dmm, lse, dd, gi, d, *, name):
    T = qr.shape[0]
    L, nb = _dil_views(T, d)

    def body(k_ref, v_ref, q0_ref, q1_ref, dy0_ref, dy1_ref, lse0_ref, lse1_ref, dd0_ref, dd1_ref, dk_ref, dv_ref):
        cur_ok, prev_band = _band_masks()
        next_ok = prev_band & (pl.program_id(1) < nb - 1)
        for h in range(B_HEADS):
            sl = slice(h * HEAD_DIM, (h + 1) * HEAD_DIM)
            k, v = k_ref[:, sl], v_ref[:, sl]
            dk = jnp.zeros((SPAN, HEAD_DIM), F32)
            dv = jnp.zeros((SPAN, HEAD_DIM), F32)
            for ok, q_ref, dy_ref, lse_ref, dd_ref in ((cur_ok, q0_ref, dy0_ref, lse0_ref, dd0_ref),
                                                         (next_ok, q1_ref, dy1_ref, lse1_ref, dd1_ref)):
                q, dy = q_ref[:, sl], dy_ref[:, sl]
                lse_h = jnp.max(lse_ref[:, sl], axis=-1, keepdims=True)
                dd_h = jnp.max(dd_ref[:, sl], axis=-1, keepdims=True)
                p = jnp.exp(jnp.where(ok, _dot_nt(q, k) * ATT_SCALE, NEG) - lse_h)
                ds = p * (_dot_nt(dy, v) - dd_h) * ATT_SCALE
                dk = dk + _dot_tn(ds, q)
                dv = dv + _dot_tn(p, dy)
            dk_ref[:, sl] = dk
            dv_ref[:, sl] = dv

    blk = lambda f: pl.BlockSpec((SPAN, B_WIDTH), f)
    cur = lambda r, n: (n, r)
    nxt = lambda r, n: (jnp.minimum(n + 1, nb - 1), r)
    qcur = lambda r, n: (n, r * N_GROUPS + gi)
    qnxt = lambda r, n: (jnp.minimum(n + 1, nb - 1), r * N_GROUPS + gi)
    v2 = lambda a: a.reshape(L, d * a.shape[1])
    ov = jax.ShapeDtypeStruct((L, d * B_WIDTH), F32)
    dk, dv = pl.pallas_call(
        body, grid=(d, nb),
        in_specs=[blk(cur), blk(lambda r, n: (n, 2 * r + 1)), blk(qcur), blk(qnxt),
                  blk(cur), blk(nxt), blk(cur), blk(nxt), blk(cur), blk(nxt)],
        out_specs=[blk(cur), blk(cur)], out_shape=[ov, ov],
        compiler_params=_cp("parallel", "arbitrary"), name=name,
    )(v2(kr), v2(kv), v2(qr), v2(qr), v2(dmm), v2(dmm), v2(lse), v2(lse), v2(dd), v2(dd))
    return dk.reshape(T, B_WIDTH), dv.reshape(T, B_WIDTH)


A_MQ_COL = 4 * A_WIDTH // MEM_WIDTH
B_MQ_COL = N_GROUPS * B_WIDTH // MEM_WIDTH


def _row(v):
    return v.reshape(1, -1).astype(F32)


def _local_step(x, mem, tgt, get_w, P, put_g, first_dep=None):
    T = x.shape[0]
    cosf, sinsg = _rope_tables(T)
    lb_soft = jax.nn.softmax(P["a_lb_logits"].astype(F32), axis=0)
    lb = lb_soft[0:1]
    qw_heads = jnp.repeat(P["b_qnorm"][0], B_HEADS, axis=0).reshape(1, -1)
    kw_heads = jnp.tile(_row(P["b_knorm"]), (1, B_HEADS))
    mqw = [jnp.tile(_row(P["mem_qnorm"][l]), (1, MEM_HEADS)) for l in range(2)]
    mkw = [jnp.tile(_row(P["mem_knorm"][l]), (1, MEM_HEADS)) for l in range(2)]
    nmix = [_row(P["norm_mix"][l]) for l in range(2)]
    nffn = [_row(P["norm_ffn"][l]) for l in range(2)]
    mnorm = [_row(P["mem_norm"][l]) for l in range(2)]
    kvn = _row(P["kv_norm"])
    onorm = _row(P["a_onorm"])
    W = {}

    def w_of(name, after=None):
        if name not in W:
            W[name] = get_w(name, after)
        return W[name]

    proj_a, xn0 = _rms_matmul(x, nmix[0], w_of("a_w_in"), tt=512, tn=1664, wt=True, name="proj_a", dep=first_dep)
    mkv0, mn0 = _rms_matmul(mem, mnorm[0], w_of("w_mem_kv0"), tt=MEM_TOKENS, tn=2 * MEM_WIDTH, wt=False, name="mem_kv0")
    o_raw, st = _hgrn2_fwd(proj_a, lb, name="hgrn2_fwd")
    mm0 = _a_post_fwd(o_raw, proj_a, onorm, tt=512, name="a_post_fwd")
    mo0 = _mem_attn_fwd(proj_a, A_MQ_COL, mkv0, mqw[0], mkw[0], tt=512, name="mem_attn_fwd0")
    mix0 = jnp.concatenate([mm0, mo0], axis=1)
    hm0 = _mm_res(x, mix0, w_of("w_out0", mix0), tt=512, name="out_proj0")
    gu0, hn0 = _rms_matmul(hm0, nffn[0], w_of("w_gate_up0", hm0), tt=512, tn=1408, wt=True, name="gate_up0")
    h1 = _swiglu_down(hm0, gu0, w_of("w_down0", gu0), tt=256, name="down0")
    kv, hkn = _rms_matmul(h1, kvn, w_of("w_kv", h1), tt=512, tn=768, wt=True, name="kv_proj")
    kr = _headnorm_rope_fwd(kv, kw_heads, cosf, sinsg, col0=0, n_heads=B_HEADS, tt=512, name="k_prep")

    proj_b, xn1 = _rms_matmul(h1, nmix[1], w_of("b_w_in", kr), tt=512, tn=1280, wt=True, name="proj_b")
    mkv1, mn1 = _rms_matmul(mem, mnorm[1], w_of("w_mem_kv1", kr), tt=MEM_TOKENS, tn=2 * MEM_WIDTH, wt=False, name="mem_kv1")
    qr = _headnorm_rope_fwd(proj_b, qw_heads, cosf, sinsg, col0=0, n_heads=N_GROUPS * B_HEADS, tt=512, name="q_prep")
    outs = [_dil_fwd(qr, kr, kv, gi, d, name=f"dil_fwd{gi}") for gi, d in enumerate(DILATIONS)]
    mm1, lse_tot = _dil_combine_fwd([o for o, _ in outs], [s for _, s in outs], tt=512, name="dil_combine")
    mo1 = _mem_attn_fwd(proj_b, B_MQ_COL, mkv1, mqw[1], mkw[1], tt=512, name="mem_attn_fwd1")
    mix1 = jnp.concatenate([mm1, mo1], axis=1)
    hm1 = _mm_res(h1, mix1, w_of("w_out1", mix1), tt=512, name="out_proj1")
    gu1, hn1 = _rms_matmul(hm1, nffn[1], w_of("w_gate_up1", hm1), tt=512, tn=1408, wt=True, name="gate_up1")
    y = _swiglu_down(hm1, gu1, w_of("w_down1", gu1), tt=256, name="down1")
    dy, sq = _loss_kernel(y, tgt, tt=512, name="loss")

    gP = {}
    zeros_mem = jnp.zeros((MEM_TOKENS, D_MODEL), F32)

    def ffn_bwd(l, dh, hm, gu, hn):
        dgu, act = _swiglu_bwd(dh, gu, w_of(f"w_down{l}"), tt=256, name=f"swiglu_bwd{l}")
        g_wd = _mm_tn(act, dh, tt=512, tka=1408, name=f"g_w_down{l}")
        g_wgu = _mm_tn(dgu, hn, tt=512, tka=1408, name=f"g_w_gate_up{l}")
        sent = put_g({f"w_down{l}": g_wd, f"w_gate_up{l}": g_wgu})
        dhm, g_nf = _rms_bwd_dx(hm, nffn[l], w_of(f"w_gate_up{l}"), dgu, dh, tt=256, wt=True, name=f"gate_up_bwd{l}", dep=sent)
        return dhm, g_nf

    def mix_bwd(l, dhm, mix, proj, qcol, mkv, mn):
        dmix = _mm_nt(dhm, w_of(f"w_out{l}"), tt=512, name=f"out_proj_bwd{l}")
        g_wout = _mm_tn(mix, dhm, tt=512, tka=512, name=f"g_w_out{l}")
        dmq, dmkv, dqw, dkw = _mem_attn_bwd(proj, qcol, mkv, mqw[l], mkw[l], dmix, tt=512, name=f"mem_attn_bwd{l}")
        g_wmkv = _mm_tn(mn, dmkv, tt=MEM_TOKENS, tka=512, name=f"g_w_mem_kv{l}")
        sent = put_g({f"w_out{l}": g_wout, f"w_mem_kv{l}": g_wmkv})
        _, g_mn = _rms_bwd_dx(mem, mnorm[l], w_of(f"w_mem_kv{l}"), dmkv, zeros_mem, tt=MEM_TOKENS, wt=False, name=f"mem_kv_bwd{l}")
        fold = lambda v: v.reshape(MEM_HEADS, MEM_HEAD_DIM).sum(axis=0)
        return dmix, dmq, g_mn, fold(dqw), fold(dkw), sent

    dhm1, g_nf1 = ffn_bwd(1, dy, hm1, gu1, hn1)
    dmix1, dmq1, g_mn1, g_mq1, g_mk1, sent = mix_bwd(1, dhm1, mix1, proj_b, B_MQ_COL, mkv1, mn1)
    dmm, dd = _dil_bwd_prep(dmix1, mm1, tt=512, name="dil_bwd_prep", dep=sent)
    dqs, dks, dvs = [], [], []
    for gi, d in enumerate(DILATIONS):
        dqs.append(_dil_bwd_dq(qr, kr, kv, dmm, lse_tot, dd, gi, d, name=f"dil_bwd_dq{gi}"))
        dk_g, dv_g = _dil_bwd_dkv(qr, kr, kv, dmm, lse_tot, dd, gi, d, name=f"dil_bwd_dkv{gi}")
        dks.append(dk_g)
        dvs.append(dv_g)
    dq_raw, dqw = _q_prep_bwd(proj_b, qw_heads, cosf, sinsg, dqs, tt=512, name="q_prep_bwd")
    dkv, dkw = _kv_prep_bwd(kv, kw_heads, cosf, sinsg, dks, dvs, tt=512, name="kv_prep_bwd")
    dproj_b = jnp.concatenate([dq_raw, dmq1], axis=1)
    g_wb = _mm_tn(dproj_b, xn1, tt=512, tka=1280, name="g_b_w_in")
    g_wkv = _mm_tn(dkv, hkn, tt=512, tka=768, name="g_w_kv")
    sent = put_g({"b_w_in": g_wb, "w_kv": g_wkv})
    dh1, g_nm1 = _rms_bwd_dx(h1, nmix[1], w_of("b_w_in"), dproj_b, dhm1, tt=256, wt=True, name="proj_b_bwd", dep=sent)
    dh1, g_kvn = _rms_bwd_dx(h1, kvn, w_of("w_kv"), dkv, dh1, tt=256, wt=True, name="kv_proj_bwd")

    dhm0, g_nf0 = ffn_bwd(0, dh1, hm0, gu0, hn0)
    dmix0, dmq0, g_mn0, g_mq0, g_mk0, sent = mix_bwd(0, dhm0, mix0, proj_a, A_MQ_COL, mkv0, mn0)
    do_raw, dg, g_onorm = _a_post_bwd(o_raw, proj_a, onorm, dmix0, tt=512, name="a_post_bwd", dep=sent)
    dq, dz, dv, dlb = _hgrn2_bwd(proj_a, lb, st, do_raw, name="hgrn2_bwd")
    dproj_a = jnp.concatenate([dq, dz, dv, dg, dmq0], axis=1)
    sent = put_g({"a_w_in": _mm_tn(dproj_a, xn0, tt=512, tka=1664, name="g_a_w_in")})
    gx, g_nm0 = _rms_bwd_dx(x, nmix[0], w_of("a_w_in"), dproj_a, dhm0, tt=256, wt=True, name="proj_a_bwd", dep=sent)

    dl0 = lb_soft[0:1] * lb_soft[1:2] * dlb
    gP["a_lb_logits"] = jnp.concatenate([dl0, -dl0], axis=0)
    gP["a_onorm"] = g_onorm
    gP["norm_mix"] = jnp.concatenate([g_nm0, g_nm1], axis=0)
    gP["norm_ffn"] = jnp.concatenate([g_nf0, g_nf1], axis=0)
    gP["b_qnorm"] = dqw.reshape(N_GROUPS, B_HEADS, HEAD_DIM).sum(axis=1)[None]
    gP["kv_norm"] = g_kvn.reshape(-1)
    gP["b_knorm"] = dkw.reshape(B_HEADS, HEAD_DIM).sum(axis=0)
    gP["mem_norm"] = jnp.concatenate([g_mn0, g_mn1], axis=0)
    gP["mem_qnorm"] = jnp.stack([g_mq0, g_mq1])
    gP["mem_knorm"] = jnp.stack([g_mk0, g_mk1])
    return sq, gx, gP


MESH_ID = pl.DeviceIdType.MESH
HBM_SPEC = pl.BlockSpec(memory_space=pltpu.HBM)


def _position():
    return lax.axis_index("x"), lax.axis_index("y"), lax.axis_index("c")


def _all_gather(blocks, *, name):
    n = len(blocks)

    def body(*refs):
        x_refs, out_refs = refs[:n], refs[n:2 * n]
        send_sems, recv_sems, local_sems = refs[2 * n:]
        x, y, c = _position()
        me, sibling = (x, y, c), (x, y, 1 - c)
        chips = [(1 - x, y), (x, 1 - y), (1 - x, 1 - y)]

        def slot(a, px, py, pc):
            return out_refs[a].at[4 * px + 2 * py + pc]

        def copy(a, k, blk, to, src=None):
            return pltpu.make_async_remote_copy(
                src_ref=slot(a, *blk) if src is None else src, dst_ref=slot(a, *blk),
                send_sem=send_sems.at[7 * a + k], recv_sem=recv_sems.at[7 * a + k], device_id=to, device_id_type=MESH_ID)

        mine = [pltpu.make_async_copy(x_refs[a], slot(a, *me), local_sems.at[a]) for a in range(n)]
        for cp in mine:
            cp.start()
        first = []
        for a in range(n):
            first.append(copy(a, 0, me, sibling, src=x_refs[a]))
            first += [copy(a, 1 + j, me, (*chip, c), src=x_refs[a]) for j, chip in enumerate(chips)]
        for cp in first:
            cp.start()
        passed = []
        for j, chip in enumerate(chips):
            for a in range(n):
                copy(a, 1 + j, (*chip, c), me).wait_recv()
                cp = copy(a, 4 + j, (*chip, c), sibling)
                cp.start()
                passed.append(cp)
        for a in range(n):
            copy(a, 0, sibling, me).wait_recv()
            for j, chip in enumerate(chips):
                copy(a, 4 + j, (*chip, 1 - c), me).wait_recv()
        for cp in first + passed:
            cp.wait_send()
        for cp in mine:
            cp.wait()

    return pl.pallas_call(
        body, out_shape=[jax.ShapeDtypeStruct((N_DEV,) + b.shape, b.dtype) for b in blocks],
        in_specs=[HBM_SPEC] * n, out_specs=[HBM_SPEC] * n,
        scratch_shapes=[pltpu.SemaphoreType.DMA((7 * n,)), pltpu.SemaphoreType.DMA((7 * n,)), pltpu.SemaphoreType.DMA((n,))],
        name=name)(*blocks)


SEM_SPEC = pl.BlockSpec(memory_space=pltpu.SEMAPHORE)
ANY_SPEC = pl.BlockSpec(memory_space=pl.ANY)
DATAFLOW = pltpu.SideEffectType.DATAFLOW_SIDE_EFFECTING


def _peer(k, x, y, c):
    return (1 - x if (k >> 2) & 1 else x, 1 - y if (k >> 1) & 1 else y, 1 - c if k & 1 else c)


def _own_slot_filled(own_block):
    x, y, c = _position()
    zone = lax.empty((N_DEV,) + own_block.shape, own_block.dtype)
    return lax.dynamic_update_slice_in_dim(zone, own_block[None], 4 * x + 2 * y + c, axis=0)


def _split_start(srcs, scatter, after, *, name):
    n = len(srcs)
    extra = [] if after is None else [after]
    x, y, c = _position()
    me = 4 * x + 2 * y + c
    lands = [_own_slot_filled(lax.dynamic_index_in_dim(s, me, 0, keepdims=False) if scatter else s) for s in srcs]

    def body(*refs):
        src_refs, land_refs = refs[:n], refs[n:2 * n]
        send_sems, recv_sems = refs[2 * n + len(extra)], refs[2 * n + len(extra) + 1]
        token = refs[-1]
        bx, by, bc = _position()
        bme = 4 * bx + 2 * by + bc
        for a in range(n):
            for k in range(1, N_DEV):
                tx, ty, tc = _peer(k, bx, by, bc)
                src = src_refs[a].at[4 * tx + 2 * ty + tc] if scatter else src_refs[a]
                pltpu.make_async_remote_copy(
                    src_ref=src, dst_ref=land_refs[a].at[bme],
                    send_sem=send_sems.at[7 * a + k - 1], recv_sem=recv_sems.at[7 * a + k - 1],
                    device_id=(tx, ty, tc), device_id_type=MESH_ID).start()
        token[...] = jnp.zeros_like(token)

    hbm = lambda a: pltpu.HBM(a.shape, a.dtype)
    outs = pl.pallas_call(
        body, name=name,
        out_shape=(pltpu.SemaphoreType.DMA((7 * n,)), pltpu.SemaphoreType.DMA((7 * n,)),
                   *[hbm(s) for s in srcs], *[hbm(l) for l in lands], jax.ShapeDtypeStruct((8, 128), F32)),
        in_specs=[HBM_SPEC] * (2 * n) + [ANY_SPEC] * len(extra),
        out_specs=(SEM_SPEC, SEM_SPEC, *[HBM_SPEC] * (2 * n), pl.BlockSpec(memory_space=pltpu.VMEM)),
        input_output_aliases={i: 2 + i for i in range(2 * n)},
        compiler_params=pltpu.CompilerParams(has_side_effects=DATAFLOW),
    )(*[pltpu.with_memory_space_constraint(s, pltpu.HBM) for s in srcs],
      *[pltpu.with_memory_space_constraint(l, pltpu.HBM) for l in lands], *extra)
    return {"n": n, "scatter": scatter, "send": outs[0], "recv": outs[1], "srcs": outs[2:2 + n],
            "lands": outs[2 + n:2 + 2 * n], "token": outs[-1]}


def _split_wait(handle, after, *, name):
    n, scatter = handle["n"], handle["scatter"]

    def body(*refs):
        src_refs, land_refs = refs[:n], refs[n:2 * n]
        send_sems, recv_sems = refs[2 * n], refs[2 * n + 1]
        bx, by, bc = _position()
        for a in range(n):
            for k in range(1, N_DEV):
                src = src_refs[a].at[0] if scatter else src_refs[a]
                cp = pltpu.make_async_remote_copy(
                    src_ref=src, dst_ref=land_refs[a].at[0],
                    send_sem=send_sems.at[7 * a + k - 1], recv_sem=recv_sems.at[7 * a + k - 1],
                    device_id=_peer(k, bx, by, bc), device_id_type=MESH_ID)
                cp.wait_send()
                cp.wait_recv()

    hbm = lambda a: pltpu.HBM(a.shape, a.dtype)
    outs = pl.pallas_call(
        body, name=name,
        out_shape=(*[hbm(s) for s in handle["srcs"]], *[hbm(l) for l in handle["lands"]]),
        in_specs=[HBM_SPEC] * (2 * n) + [SEM_SPEC, SEM_SPEC, ANY_SPEC],
        out_specs=tuple([HBM_SPEC] * (2 * n)),
        input_output_aliases={i: i for i in range(2 * n)},
        compiler_params=pltpu.CompilerParams(has_side_effects=DATAFLOW),
    )(*handle["srcs"], *handle["lands"], handle["send"], handle["recv"], after)
    return list(outs[n:])


def _sum_sources(parts, *, tr, name):
    n, R, C = parts.shape

    def body(p_ref, o_ref):
        acc = p_ref[0].astype(F32)
        for s in range(1, n):
            acc = acc + p_ref[s].astype(F32)
        o_ref[...] = acc

    return pl.pallas_call(
        body, grid=(R // tr,), in_specs=[pl.BlockSpec((n, tr, C), lambda i: (0, i, 0))],
        out_specs=pl.BlockSpec((tr, C), lambda i: (i, 0)),
        out_shape=jax.ShapeDtypeStruct((R, C), F32), compiler_params=_cp("parallel"), name=name)(parts)


def _adamw(g, w, m, v, *, tr, name):
    L, R, C = w.shape
    c1 = 1.0 - ADAM_B1 ** ADAM_STEP
    c2 = 1.0 - ADAM_B2 ** ADAM_STEP

    def body(g_ref, w_ref, m_ref, v_ref, d_ref, nm_ref, nv_ref):
        gv = g_ref[...]
        nm = ADAM_B1 * m_ref[...] + (1.0 - ADAM_B1) * gv
        nv = ADAM_B2 * v_ref[...] + (1.0 - ADAM_B2) * (gv * gv)
        nm_ref[...] = nm
        nv_ref[...] = nv
        d_ref[...] = -ADAM_LR * ((nm / c1) / (jnp.sqrt(nv / c2) + ADAM_EPS) + ADAM_WD * w_ref[...])

    blk = pl.BlockSpec((None, tr, C), lambda l, i: (l, i, 0))
    sh = jax.ShapeDtypeStruct((L, R, C), F32)
    return pl.pallas_call(
        body, grid=(L, R // tr), in_specs=[blk] * 4, out_specs=[blk] * 3, out_shape=[sh] * 3,
        compiler_params=_cp("parallel", "parallel"), name=name)(g, w, m, v)


UNITS = {
    "a_w_in": ("a_w_in", 0, True), "w_mem_kv0": ("w_mem_kv", 0, False), "w_out0": ("w_out", 0, False),
    "w_gate_up0": ("w_gate_up", 0, True), "w_down0": ("w_down", 0, False), "w_kv": ("w_kv", None, True),
    "b_w_in": ("b_w_in", 0, True), "w_mem_kv1": ("w_mem_kv", 1, False), "w_out1": ("w_out", 1, False),
    "w_gate_up1": ("w_gate_up", 1, True), "w_down1": ("w_down", 1, False),
}
BIG = ("a_w_in", "b_w_in", "w_kv", "w_mem_kv", "w_out", "w_gate_up", "w_down")
ADAMW_ROW_TILE = {"a_w_in": 256, "b_w_in": 256, "w_kv": 256, "w_mem_kv": 128, "w_out": 128, "w_gate_up": 256, "w_down": 352}


def _wire_block(weights, unit):
    name, layer, col = UNITS[unit]
    a = weights[name] if layer is None else weights[name][layer]
    return (a.T if col else a).astype(BF16)


def _natural_grads(sums):
    out = {}
    for name in BIG:
        parts = [sums[u].T if col else sums[u] for u, (wn, _, col) in UNITS.items() if wn == name]
        out[name] = parts[0] if name == "w_kv" else jnp.stack(parts)
    return out


SMALL_REPLICATED = ("norm_mix", "norm_ffn", "b_qnorm", "kv_norm", "b_knorm", "mem_norm", "mem_qnorm", "mem_knorm")
SMALL_SHARDED = ("a_lb_logits", "a_onorm")
SMALL_ORDER = SMALL_REPLICATED + SMALL_SHARDED
LANES = 128


def _prod(shape):
    n = 1
    for s in shape:
        n *= s
    return n


def _pack_flat(arrays, rows, cols, dtype):
    flat = jnp.concatenate([a.reshape(-1).astype(dtype) for a in arrays])
    return jnp.pad(flat, (0, rows * cols - flat.shape[0])).reshape(rows, cols)


def _unpack_flat(packed, shapes):
    flat = packed.reshape(-1)
    out, off = [], 0
    for s in shapes:
        out.append(flat[off:off + _prod(s)].reshape(s))
        off += _prod(s)
    return out


def kernel(x, mem, norm_mix, norm_ffn, a_w_in, a_lb_logits, a_onorm, b_w_in, b_qnorm, kv_norm, w_kv, b_knorm, mem_norm, w_mem_kv, mem_qnorm, mem_knorm, w_out, w_gate_up, w_down, loss_target, m_norm_mix, m_norm_ffn, m_a_w_in, m_a_lb_logits, m_a_onorm, m_b_w_in, m_b_qnorm, m_kv_norm, m_w_kv, m_b_knorm, m_mem_norm, m_w_mem_kv, m_mem_qnorm, m_mem_knorm, m_w_out, m_w_gate_up, m_w_down, v_norm_mix, v_norm_ffn, v_a_w_in, v_a_lb_logits, v_a_onorm, v_b_w_in, v_b_qnorm, v_kv_norm, v_w_kv, v_b_knorm, v_mem_norm, v_w_mem_kv, v_mem_qnorm, v_mem_knorm, v_w_out, v_w_gate_up, v_w_down):
    names = ("norm_mix", "norm_ffn", "a_w_in", "a_lb_logits", "a_onorm", "b_w_in", "b_qnorm", "kv_norm", "w_kv", "b_knorm",
             "mem_norm", "w_mem_kv", "mem_qnorm", "mem_knorm", "w_out", "w_gate_up", "w_down")
    w = dict(zip(names, (norm_mix, norm_ffn, a_w_in, a_lb_logits, a_onorm, b_w_in, b_qnorm, kv_norm, w_kv, b_knorm,
                         mem_norm, w_mem_kv, mem_qnorm, mem_knorm, w_out, w_gate_up, w_down)))
    m = dict(zip(names, (m_norm_mix, m_norm_ffn, m_a_w_in, m_a_lb_logits, m_a_onorm, m_b_w_in, m_b_qnorm, m_kv_norm, m_w_kv,
                         m_b_knorm, m_mem_norm, m_w_mem_kv, m_mem_qnorm, m_mem_knorm, m_w_out, m_w_gate_up, m_w_down)))
    v = dict(zip(names, (v_norm_mix, v_norm_ffn, v_a_w_in, v_a_lb_logits, v_a_onorm, v_b_w_in, v_b_qnorm, v_kv_norm, v_w_kv,
                         v_b_knorm, v_mem_norm, v_w_mem_kv, v_mem_qnorm, v_mem_knorm, v_w_out, v_w_gate_up, v_w_down)))

    first = ["a_w_in", "w_mem_kv0"]
    gathered = _all_gather([_wire_block(w, u) for u in first] + [_pack_flat([a_lb_logits, a_onorm], 8, LANES, F32)],
                           name="gather_first")
    full = {u: g.reshape(-1, g.shape[-1]) for u, g in zip(first, gathered)}
    small_in = gathered[-1].reshape(N_DEV, -1)
    P = {n: w[n] for n in SMALL_REPLICATED}
    P["a_lb_logits"] = small_in[:, :192].reshape(N_DEV, 2, 96).transpose(1, 0, 2).reshape(2, A_WIDTH)
    P["a_onorm"] = small_in[:, 192:288].reshape(1, A_WIDTH)
    later = [["w_out0", "w_gate_up0"], ["w_down0", "w_kv"], ["b_w_in", "w_mem_kv1"], ["w_out1", "w_gate_up1", "w_down1"]]
    pending = {}
    token = gathered[-1]
    for i, group in enumerate(later):
        handle = _split_start([_wire_block(w, u) for u in group], False, token, name=f"gather{i}_start")
        token = handle["token"]
        for u in group:
            pending[u] = (i, group, handle)

    def get_w(unit, after):
        if unit not in full:
            i, group, handle = pending[unit]
            for u, land in zip(group, _split_wait(handle, after, name=f"gather{i}_wait")):
                full[u] = land.reshape(-1, land.shape[-1])
        return full[unit]

    sent = []

    def put_g(group):
        units = list(group)
        handle = _split_start([group[u].reshape(N_DEV, -1, group[u].shape[-1]) for u in units], True, None,
                              name=f"scatter{len(sent)}_start")
        sent.append((units, handle))
        return handle["token"]

    sq, gx, gP = _local_step(x[0], mem[0], loss_target[0], get_w, P, put_g, first_dep=token)
    loss = lax.psum(0.5 * jnp.sum(sq) / D_MODEL, ("x", "y", "c"))

    sums = {}
    for i, (units, handle) in enumerate(sent):
        for u, r in zip(units, _split_wait(handle, gx, name=f"scatter{i}_wait")):
            sums[u] = _sum_sources(r, tr=r.shape[1] if r.shape[1] <= 416 else r.shape[1] // 2, name=f"sum_{u}")
    out = {"grad": _natural_grads(sums), "delta": {}, "new_m": {}, "new_v": {}}
    for n in BIG:
        shape = w[n].shape
        as3 = lambda a: a.reshape((-1,) + shape[-2:])
        res = _adamw(as3(out["grad"][n]), as3(w[n]), as3(m[n]), as3(v[n]), tr=ADAMW_ROW_TILE[n], name=f"adamw_{n}")
        out["grad"][n] = out["grad"][n].reshape(shape)
        for kind, r in zip(("delta", "new_m", "new_v"), res):
            out[kind][n] = r.reshape(shape)

    full_shapes = [(2, A_WIDTH) if n == "a_lb_logits" else (1, A_WIDTH) if n == "a_onorm" else w[n].shape for n in SMALL_ORDER]
    n_small = sum(_prod(s) for s in full_shapes)
    rows_small = -(-n_small // (8 * LANES)) * 8
    g_all, = _all_gather([_pack_flat([gP[n] for n in SMALL_ORDER], rows_small, LANES, F32)], name="gather_small_grads")
    g_small = dict(zip(SMALL_ORDER, _unpack_flat(_sum_sources(g_all, tr=rows_small, name="sum_small_grads"), full_shapes)))
    me = 4 * lax.axis_index("x") + 2 * lax.axis_index("y") + lax.axis_index("c")
    for n in SMALL_SHARDED:
        g_small[n] = lax.dynamic_slice_in_dim(g_small[n], me * 96, 96, axis=1)
    shapes = [w[n].shape for n in SMALL_ORDER]
    rows_upd = -(-sum(_prod(s) for s in shapes) // (8 * LANES)) * 8
    pk = lambda d: _pack_flat([d[n] for n in SMALL_ORDER], rows_upd, LANES, F32)
    res = _adamw(pk(g_small)[None], pk(w)[None], pk(m)[None], pk(v)[None], tr=rows_upd, name="adamw_small")
    out["grad"].update(g_small)
    for kind, packed in zip(("delta", "new_m", "new_v"), res):
        out[kind].update(zip(SMALL_ORDER, _unpack_flat(packed[0], shapes)))

    return (loss, gx[None], *[out["grad"][n] for n in names], *[out["delta"][n] for n in names],
            *[out["new_m"][n] for n in names], *[out["new_v"][n] for n in names])
```

```python
import functools

import jax
import jax.numpy as jnp
from jax import lax
from jax.experimental import pallas as pl
from jax.experimental.pallas import tpu as pltpu

F32 = jnp.float32
BF16 = jnp.bfloat16

N_DEV = 8
D_MODEL = 1024
HEAD_DIM = 128
A_HEADS = 6
A_WIDTH = A_HEADS * HEAD_DIM
CHUNK = 64
B_HEADS = 6
B_WIDTH = B_HEADS * HEAD_DIM
DILATIONS = (1, 4, 16)
SPAN = 128
N_GROUPS = 3
ROPE_THETA = 10000.0
MEM_TOKENS = 256
MEM_HEADS = 4
MEM_HEAD_DIM = 64
MEM_WIDTH = MEM_HEADS * MEM_HEAD_DIM
FFN_HIDDEN = 2816
EPS = 1e-6

ADAM_LR = 0.001
ADAM_B1 = 0.9
ADAM_B2 = 0.999
ADAM_EPS = 1e-08
ADAM_WD = 0.01
ADAM_STEP = 10

V7X_VMEM_LIMIT_BYTES = 56 * 1024 * 1024

NT_DIMS = (((1,), (1,)), ((), ()))
TN_DIMS = (((0,), (0,)), ((), ()))


def _cp(*sem):
    return pltpu.CompilerParams(dimension_semantics=sem, vmem_limit_bytes=V7X_VMEM_LIMIT_BYTES)


def _dot(a, b):
    return jnp.dot(a.astype(BF16), b.astype(BF16), preferred_element_type=F32)


def _dot_nt(a, b):
    return lax.dot_general(a.astype(BF16), b.astype(BF16), NT_DIMS, preferred_element_type=F32)


def _dot_tn(a, b):
    return lax.dot_general(a.astype(BF16), b.astype(BF16), TN_DIMS, preferred_element_type=F32)


def _dot3(m01, x):
    hi = x.astype(BF16)
    r1 = x - hi.astype(F32)
    mid = r1.astype(BF16)
    lo = (r1 - mid.astype(F32)).astype(BF16)
    d = functools.partial(jnp.dot, preferred_element_type=F32)
    return d(m01, hi) + d(m01, mid) + d(m01, lo)


def _sigmoid(x):
    return 1.0 / (1.0 + jnp.exp(-x))


def _full(shape):
    return pl.BlockSpec(shape, lambda *_: (0,) * len(shape))


def _dep(body, n_in, dep):
    if dep is None:
        return body, [], []

    def with_dep(*refs):
        return body(*refs[:n_in], *refs[n_in + 1:])

    return with_dep, [pl.BlockSpec(memory_space=pl.ANY)], [dep]


def _rms_matmul(x, g, w, *, tt, tn, wt, name, dep=None):
    T, K = x.shape
    N = w.shape[0] if wt else w.shape[1]

    def kernel_body(x_ref, g_ref, w_ref, y_ref, xn_ref):
        @pl.when(pl.program_id(1) == 0)
        def _():
            xf = x_ref[...]
            r = lax.rsqrt(jnp.mean(xf * xf, axis=-1, keepdims=True) + EPS)
            xn_ref[...] = (xf * r * g_ref[...]).astype(BF16)

        y_ref[...] = (_dot_nt if wt else _dot)(xn_ref[...], w_ref[...])

    w_spec = pl.BlockSpec((tn, K), lambda i, j: (j, 0)) if wt else pl.BlockSpec((K, tn), lambda i, j: (0, j))
    body, dep_specs, dep_args = _dep(kernel_body, 3, dep)
    return pl.pallas_call(
        body, grid=(T // tt, N // tn),
        in_specs=[pl.BlockSpec((tt, K), lambda i, j: (i, 0)), _full((1, K)), w_spec] + dep_specs,
        out_specs=[pl.BlockSpec((tt, tn), lambda i, j: (i, j)), pl.BlockSpec((tt, K), lambda i, j: (i, 0))],
        out_shape=[jax.ShapeDtypeStruct((T, N), F32), jax.ShapeDtypeStruct((T, K), BF16)],
        compiler_params=_cp("parallel", "arbitrary"), name=name)(x, g, w, *dep_args)


def _mm_res(res, a, w, *, tt, name):
    T, K = a.shape
    N = w.shape[1]

    def body(r_ref, a_ref, w_ref, o_ref):
        o_ref[...] = r_ref[...] + _dot(a_ref[...], w_ref[...])

    return pl.pallas_call(
        body, grid=(T // tt,),
        in_specs=[pl.BlockSpec((tt, N), lambda i: (i, 0)), pl.BlockSpec((tt, K), lambda i: (i, 0)), _full((K, N))],
        out_specs=pl.BlockSpec((tt, N), lambda i: (i, 0)),
        out_shape=jax.ShapeDtypeStruct((T, N), F32),
        compiler_params=_cp("parallel"), name=name)(res, a, w)


def _swiglu_down(h, gu, wd, *, tt, name):
    T, D = h.shape
    Fh = wd.shape[0]

    def body(h_ref, gt_ref, up_ref, w_ref, o_ref):
        gt = gt_ref[...]
        act = gt * _sigmoid(gt) * up_ref[...]
        o_ref[...] = h_ref[...] + _dot(act, w_ref[...])

    return pl.pallas_call(
        body, grid=(T // tt,),
        in_specs=[pl.BlockSpec((tt, D), lambda i: (i, 0)), pl.BlockSpec((tt, Fh), lambda i: (i, 0)),
                  pl.BlockSpec((tt, Fh), lambda i: (i, 1)), _full((Fh, D))],
        out_specs=pl.BlockSpec((tt, D), lambda i: (i, 0)),
        out_shape=jax.ShapeDtypeStruct((T, D), F32),
        compiler_params=_cp("parallel"), name=name)(h, gu, gu, wd)


def _swiglu_bwd(dh, gu, wd, *, tt, name):
    T, D = dh.shape
    Fh = wd.shape[0]

    def body(dh_ref, gt_ref, up_ref, w_ref, dgu_ref, act_ref):
        gt = gt_ref[...]
        up = up_ref[...]
        s = _sigmoid(gt)
        silu = gt * s
        dact = _dot_nt(dh_ref[...], w_ref[...])
        act_ref[...] = (silu * up).astype(BF16)
        dgu_ref[:, :Fh] = (dact * up * (s * (1.0 + gt * (1.0 - s)))).astype(BF16)
        dgu_ref[:, Fh:] = (dact * silu).astype(BF16)

    return pl.pallas_call(
        body, grid=(T // tt,),
        in_specs=[pl.BlockSpec((tt, D), lambda i: (i, 0)), pl.BlockSpec((tt, Fh), lambda i: (i, 0)),
                  pl.BlockSpec((tt, Fh), lambda i: (i, 1)), _full((Fh, D))],
        out_specs=[pl.BlockSpec((tt, 2 * Fh), lambda i: (i, 0)), pl.BlockSpec((tt, Fh), lambda i: (i, 0))],
        out_shape=[jax.ShapeDtypeStruct((T, 2 * Fh), BF16), jax.ShapeDtypeStruct((T, Fh), BF16)],
        compiler_params=_cp("parallel"), name=name)(dh, gu, gu, wd)


def _mm_nt(a, w, *, tt, name):
    T, N = a.shape
    K = w.shape[0]

    def body(a_ref, w_ref, o_ref):
        o_ref[...] = _dot_nt(a_ref[...], w_ref[...])

    return pl.pallas_call(
        body, grid=(T // tt,),
        in_specs=[pl.BlockSpec((tt, N), lambda i: (i, 0)), _full((K, N))],
        out_specs=pl.BlockSpec((tt, K), lambda i: (i, 0)),
        out_shape=jax.ShapeDtypeStruct((T, K), F32),
        compiler_params=_cp("parallel"), name=name)(a, w)


def _mm_tn(a, b, *, tt, tka, name):
    T, Ka = a.shape
    N = b.shape[1]
    last = T // tt - 1

    def body(a_ref, b_ref, o_ref, acc):
        @pl.when(pl.program_id(1) == 0)
        def _():
            acc[...] = jnp.zeros_like(acc)

        acc[...] += _dot_tn(a_ref[...], b_ref[...])

        @pl.when(pl.program_id(1) == last)
        def _():
            o_ref[...] = acc[...].astype(BF16)

    return pl.pallas_call(
        body, grid=(Ka // tka, T // tt),
        in_specs=[pl.BlockSpec((tt, tka), lambda j, t: (t, j)), pl.BlockSpec((tt, N), lambda j, t: (t, 0))],
        out_specs=pl.BlockSpec((tka, N), lambda j, t: (j, 0)),
        out_shape=jax.ShapeDtypeStruct((Ka, N), BF16),
        scratch_shapes=[pltpu.VMEM((tka, N), F32)],
        compiler_params=_cp("parallel", "arbitrary"), name=name)(a, b)


def _rms_bwd_dx(x, g, w, dy, dres, *, tt, wt, name, dep=None):
    T, K = x.shape
    N = w.shape[0] if wt else w.shape[1]

    def kernel_body(x_ref, g_ref, w_ref, dy_ref, dres_ref, dx_ref, dg_ref):
        @pl.when(pl.program_id(0) == 0)
        def _():
            dg_ref[...] = jnp.zeros_like(dg_ref)

        dxn = (_dot if wt else _dot_nt)(dy_ref[...], w_ref[...])
        xf = x_ref[...]
        r = lax.rsqrt(jnp.mean(xf * xf, axis=-1, keepdims=True) + EPS)
        xhat = xf * r
        dg_ref[...] += jnp.sum(dxn * xhat, axis=0, keepdims=True)
        dxhat = dxn * g_ref[...]
        dx_ref[...] = dres_ref[...] + r * (dxhat - xhat * jnp.mean(dxhat * xhat, axis=-1, keepdims=True))

    body, dep_specs, dep_args = _dep(kernel_body, 5, dep)
    return pl.pallas_call(
        body, grid=(T // tt,),
        in_specs=[pl.BlockSpec((tt, K), lambda i: (i, 0)), _full((1, K)), _full(w.shape),
                  pl.BlockSpec((tt, N), lambda i: (i, 0)), pl.BlockSpec((tt, K), lambda i: (i, 0))] + dep_specs,
        out_specs=[pl.BlockSpec((tt, K), lambda i: (i, 0)), _full((1, K))],
        out_shape=[jax.ShapeDtypeStruct((T, K), F32), jax.ShapeDtypeStruct((1, K), F32)],
        compiler_params=_cp("arbitrary"), name=name)(x, g, w, dy, dres, *dep_args)


def _loss_kernel(y, tgt, *, tt, name):
    T, D = y.shape

    def body(y_ref, t_ref, dy_ref, acc_ref):
        @pl.when(pl.program_id(0) == 0)
        def _():
            acc_ref[...] = jnp.zeros_like(acc_ref)

        e = y_ref[...] - t_ref[...]
        dy_ref[...] = e * (1.0 / D)
        acc_ref[...] += jnp.sum(e * e, axis=0, keepdims=True)

    return pl.pallas_call(
        body, grid=(T // tt,),
        in_specs=[pl.BlockSpec((tt, D), lambda i: (i, 0)), pl.BlockSpec((tt, D), lambda i: (i, 0))],
        out_specs=[pl.BlockSpec((tt, D), lambda i: (i, 0)), _full((1, D))],
        out_shape=[jax.ShapeDtypeStruct((T, D), F32), jax.ShapeDtypeStruct((1, D), F32)],
        compiler_params=_cp("arbitrary"), name=name)(y, tgt)


HGRN_TB = 512
HGRN_NCH = HGRN_TB // CHUNK


def _hgrn_chunk_fwd(q, z, lbv, tril01):
    sig = _sigmoid(z)
    f = lbv + (1.0 - lbv) * sig
    kk = 1.0 - f
    b = _dot3(tril01, jnp.log(f))
    bend = b[CHUNK - 1:CHUNK, :]
    sq = _sigmoid(q)
    eb = jnp.exp(b)
    emb = jnp.exp(-b)
    eo = jnp.exp(bend - b)
    dec = jnp.exp(bend)
    return sig, f, kk, sq, eb, emb, eo, dec


def _hgrn2_fwd(proj, lb, *, name):
    T = proj.shape[0]
    nT = T // HGRN_TB
    nC = T // CHUNK

    def body(q_ref, z_ref, v_ref, lb_ref, o_ref, st_ref, state):
        @pl.when(pl.program_id(1) == 0)
        def _():
            state[...] = jnp.zeros_like(state)

        row = lax.broadcasted_iota(jnp.int32, (CHUNK, CHUNK), 0)
        col = lax.broadcasted_iota(jnp.int32, (CHUNK, CHUNK), 1)
        causal = row >= col
        tril01 = causal.astype(BF16)
        lbv = lb_ref[...]

        def chunk(c, carry):
            rows = pl.ds(pl.multiple_of(c * CHUNK, CHUNK), CHUNK)
            q = q_ref[rows, :]
            v = v_ref[rows, :].astype(BF16)
            sig, f, kk, sq, eb, emb, eo, dec = _hgrn_chunk_fwd(q, z_ref[rows, :], lbv, tril01)
            qi = (q * sq * eb).astype(BF16)
            ki = (kk * emb).astype(BF16)
            ko = (kk * eo).astype(BF16)
            st = state[...]
            att = jnp.where(causal, _dot_nt(qi, ki), 0.0)
            o_ref[rows, :] = _dot(att, v) + _dot_nt(qi, st)
            st_ref[c, 0] = st
            state[...] = st * dec + _dot_tn(v, ko)
            return carry

        lax.fori_loop(0, HGRN_NCH, chunk, 0)

    hb = lambda off: pl.BlockSpec((HGRN_TB, HEAD_DIM), lambda h, i: (i, off + h))
    return pl.pallas_call(
        body, grid=(A_HEADS, nT),
        in_specs=[hb(0), hb(A_HEADS), hb(2 * A_HEADS), pl.BlockSpec((1, HEAD_DIM), lambda h, i: (0, h))],
        out_specs=[hb(0), pl.BlockSpec((HGRN_NCH, 1, HEAD_DIM, HEAD_DIM), lambda h, i: (i, h, 0, 0))],
        out_shape=[jax.ShapeDtypeStruct((T, A_WIDTH), F32), jax.ShapeDtypeStruct((nC, A_HEADS, HEAD_DIM, HEAD_DIM), F32)],
        scratch_shapes=[pltpu.VMEM((HEAD_DIM, HEAD_DIM), F32)],
        compiler_params=_cp("parallel", "arbitrary"), name=name)(proj, proj, proj, lb)


def _hgrn2_bwd(proj, lb, st_all, do, *, name):
    T = proj.shape[0]
    nT = T // HGRN_TB

    def body(q_ref, z_ref, v_ref, lb_ref, st_ref, do_ref, dq_ref, dz_ref, dv_ref, dlb_ref, dstate):
        @pl.when(pl.program_id(1) == 0)
        def _():
            dstate[...] = jnp.zeros_like(dstate)
            dlb_ref[...] = jnp.zeros_like(dlb_ref)

        row = lax.broadcasted_iota(jnp.int32, (CHUNK, CHUNK), 0)
        col = lax.broadcasted_iota(jnp.int32, (CHUNK, CHUNK), 1)
        causal = row >= col
        tril01 = causal.astype(BF16)
        triu01 = (row <= col).astype(BF16)
        lbv = lb_ref[...]

        def chunk(cc, carry):
            c = HGRN_NCH - 1 - cc
            rows = pl.ds(pl.multiple_of(c * CHUNK, CHUNK), CHUNK)
            q = q_ref[rows, :]
            v = v_ref[rows, :].astype(BF16)
            sig, f, kk, sq, eb, emb, eo, dec = _hgrn_chunk_fwd(q, z_ref[rows, :], lbv, tril01)
            qi32 = q * sq * eb
            ki32 = kk * emb
            ko32 = kk * eo
            qi, ki, ko = qi32.astype(BF16), ki32.astype(BF16), ko32.astype(BF16)
            att = jnp.where(causal, _dot_nt(qi, ki), 0.0).astype(BF16)
            dout = do_ref[rows, :].astype(BF16)
            st = st_ref[c, 0]
            dst = dstate[...]
            dst16 = dst.astype(BF16)
            datt = jnp.where(causal, _dot_nt(dout, v), 0.0).astype(BF16)
            dqi = _dot(datt, ki) + _dot(dout, st)
            dki = _dot_tn(datt, qi)
            dv_ref[rows, :] = (_dot_tn(att, dout) + _dot_nt(ko, dst16)).astype(BF16)
            dko = _dot(v, dst16)
            ddec = jnp.sum(dst * st, axis=0, keepdims=True)
            dstate[...] = dst * dec + _dot_tn(dout, qi)
            dkk = dki * emb + dko * eo
            db = dqi * qi32 - dki * ki32 - dko * ko32
            dbend = jnp.sum(dko * ko32, axis=0, keepdims=True) + ddec * dec
            dlogf = _dot3(triu01, db) + dbend
            df = dlogf / f - dkk
            dz_ref[rows, :] = (df * (1.0 - lbv) * sig * (1.0 - sig)).astype(BF16)
            dlb_ref[...] += jnp.sum(df * (1.0 - sig), axis=0, keepdims=True)
            dq_ref[rows, :] = (dqi * eb * (sq * (1.0 + q * (1.0 - sq)))).astype(BF16)
            return carry

        lax.fori_loop(0, HGRN_NCH, chunk, 0)

    hb = lambda off: pl.BlockSpec((HGRN_TB, HEAD_DIM), lambda h, i: (nT - 1 - i, off + h))
    hlb = pl.BlockSpec((1, HEAD_DIM), lambda h, i: (0, h))
    o16 = jax.ShapeDtypeStruct((T, A_WIDTH), BF16)
    return pl.pallas_call(
        body, grid=(A_HEADS, nT),
        in_specs=[hb(0), hb(A_HEADS), hb(2 * A_HEADS), hlb,
                  pl.BlockSpec((HGRN_NCH, 1, HEAD_DIM, HEAD_DIM), lambda h, i: (nT - 1 - i, h, 0, 0)), hb(0)],
        out_specs=[hb(0), hb(0), hb(0), hlb],
        out_shape=[o16, o16, o16, jax.ShapeDtypeStruct((1, A_WIDTH), F32)],
        scratch_shapes=[pltpu.VMEM((HEAD_DIM, HEAD_DIM), F32)],
        compiler_params=_cp("parallel", "arbitrary"), name=name)(proj, proj, proj, lb, st_all, do)


def _head_rms(x):
    r = lax.rsqrt(jnp.mean(x * x, axis=-1, keepdims=True) + EPS)
    return x * r, r


def _head_rms_bwd(dxhat, xhat, r):
    return r * (dxhat - xhat * jnp.mean(dxhat * xhat, axis=-1, keepdims=True))


def _a_post_fwd(o, proj, onorm, *, tt, name):
    T = o.shape[0]

    def body(o_ref, g_ref, w_ref, y_ref):
        for h in range(A_HEADS):
            sl = slice(h * HEAD_DIM, (h + 1) * HEAD_DIM)
            xhat, _ = _head_rms(o_ref[:, sl])
            g = g_ref[:, sl]
            y_ref[:, sl] = xhat * w_ref[:, sl] * (g * _sigmoid(g))

    blk = lambda c: pl.BlockSpec((tt, A_WIDTH), lambda i: (i, c))
    return pl.pallas_call(
        body, grid=(T // tt,), in_specs=[blk(0), blk(3), _full((1, A_WIDTH))], out_specs=blk(0),
        out_shape=jax.ShapeDtypeStruct((T, A_WIDTH), F32),
        compiler_params=_cp("parallel"), name=name)(o, proj, onorm)


def _a_post_bwd(o, proj, onorm, dmix, *, tt, name, dep=None):
    T = o.shape[0]

    def kernel_body(o_ref, g_ref, w_ref, dy_ref, do_ref, dg_ref, dw_ref):
        @pl.when(pl.program_id(0) == 0)
        def _():
            dw_ref[...] = jnp.zeros_like(dw_ref)

        for h in range(A_HEADS):
            sl = slice(h * HEAD_DIM, (h + 1) * HEAD_DIM)
            xhat, r = _head_rms(o_ref[:, sl])
            g = g_ref[:, sl]
            s = _sigmoid(g)
            dy = dy_ref[:, sl]
            w = w_ref[:, sl]
            dg_ref[:, sl] = (dy * xhat * w * (s * (1.0 + g * (1.0 - s)))).astype(BF16)
            dyn = dy * (g * s)
            dw_ref[:, sl] += jnp.sum(dyn * xhat, axis=0, keepdims=True)
            do_ref[:, sl] = _head_rms_bwd(dyn * w, xhat, r)

    blk = lambda c: pl.BlockSpec((tt, A_WIDTH), lambda i: (i, c))
    body, dep_specs, dep_args = _dep(kernel_body, 4, dep)
    return pl.pallas_call(
        body, grid=(T // tt,), in_specs=[blk(0), blk(3), _full((1, A_WIDTH)), blk(0)] + dep_specs,
        out_specs=[blk(0), blk(0), _full((1, A_WIDTH))],
        out_shape=[jax.ShapeDtypeStruct((T, A_WIDTH), F32), jax.ShapeDtypeStruct((T, A_WIDTH), BF16),
                   jax.ShapeDtypeStruct((1, A_WIDTH), F32)],
        compiler_params=_cp("arbitrary"), name=name)(o, proj, onorm, dmix, *dep_args)


def _mem_head_masks(n):
    lane = lax.broadcasted_iota(jnp.int32, (n, MEM_WIDTH), 1)
    return [(lane >= m * MEM_HEAD_DIM) & (lane < (m + 1) * MEM_HEAD_DIM) for m in range(MEM_HEADS)]


def _mem_head_rms(x, masks):
    x2 = x * x
    r = jnp.zeros_like(x)
    for mk in masks:
        ms = jnp.sum(jnp.where(mk, x2, 0.0), axis=-1, keepdims=True) * (1.0 / MEM_HEAD_DIM)
        r = jnp.where(mk, lax.rsqrt(ms + EPS), r)
    return x * r, r


def _mem_head_rms_bwd(dxhat, xhat, r, masks):
    t = dxhat * xhat
    m = jnp.zeros_like(t)
    for mk in masks:
        m = jnp.where(mk, jnp.sum(jnp.where(mk, t, 0.0), axis=-1, keepdims=True) * (1.0 / MEM_HEAD_DIM), m)
    return r * (dxhat - xhat * m)


MEM_SCALE = MEM_HEAD_DIM ** -0.5


def _mem_attn_fwd(proj, qcol, mkv, qn_w, kn_w, *, tt, name):
    T = proj.shape[0]

    def body(q_ref, k_ref, v_ref, qw_ref, kw_ref, o_ref):
        qmasks = _mem_head_masks(tt)
        kmasks = _mem_head_masks(MEM_TOKENS)
        qhat, _ = _mem_head_rms(q_ref[...], qmasks)
        qn = qhat * qw_ref[...]
        khat, _ = _mem_head_rms(k_ref[...], kmasks)
        kn = (khat * kw_ref[...]).astype(BF16)
        v = v_ref[...].astype(BF16)
        out = jnp.zeros((tt, MEM_WIDTH), F32)
        for m in range(MEM_HEADS):
            s = _dot_nt(jnp.where(qmasks[m], qn, 0.0), kn) * MEM_SCALE
            s = s - jnp.max(s, axis=-1, keepdims=True)
            p = jnp.exp(s)
            p = p / jnp.sum(p, axis=-1, keepdims=True)
            out = jnp.where(qmasks[m], _dot(p, v), out)
        o_ref[...] = out

    return pl.pallas_call(
        body, grid=(T // tt,),
        in_specs=[pl.BlockSpec((tt, MEM_WIDTH), lambda i: (i, qcol)), pl.BlockSpec((MEM_TOKENS, MEM_WIDTH), lambda i: (0, 0)),
                  pl.BlockSpec((MEM_TOKENS, MEM_WIDTH), lambda i: (0, 1)), _full((1, MEM_WIDTH)), _full((1, MEM_WIDTH))],
        out_specs=pl.BlockSpec((tt, MEM_WIDTH), lambda i: (i, 0)),
        out_shape=jax.ShapeDtypeStruct((T, MEM_WIDTH), F32),
        compiler_params=_cp("parallel"), name=name)(proj, mkv, mkv, qn_w, kn_w)


def _mem_attn_bwd(proj, qcol, mkv, qn_w, kn_w, dmix, *, tt, name):
    T = proj.shape[0]
    nsteps = T // tt
    ocol = (dmix.shape[1] - MEM_WIDTH) // MEM_WIDTH

    def body(q_ref, k_ref, v_ref, qw_ref, kw_ref, do_ref, dq_ref, dkv_ref, dqw_ref, dkw_ref, dk_acc, dv_acc):
        step = pl.program_id(0)

        @pl.when(step == 0)
        def _():
            dk_acc[...] = jnp.zeros_like(dk_acc)
            dv_acc[...] = jnp.zeros_like(dv_acc)
            dqw_ref[...] = jnp.zeros_like(dqw_ref)

        qmasks = _mem_head_masks(tt)
        kmasks = _mem_head_masks(MEM_TOKENS)
        qhat, qr = _mem_head_rms(q_ref[...], qmasks)
        qn = qhat * qw_ref[...]
        khat, kr = _mem_head_rms(k_ref[...], kmasks)
        kn = (khat * kw_ref[...]).astype(BF16)
        v = v_ref[...].astype(BF16)
        dout = do_ref[...]
        dqn = jnp.zeros((tt, MEM_WIDTH), F32)
        dkn = jnp.zeros((MEM_TOKENS, MEM_WIDTH), F32)
        dvv = jnp.zeros((MEM_TOKENS, MEM_WIDTH), F32)
        for m in range(MEM_HEADS):
            qm = jnp.where(qmasks[m], qn, 0.0).astype(BF16)
            s = _dot_nt(qm, kn) * MEM_SCALE
            s = s - jnp.max(s, axis=-1, keepdims=True)
            p = jnp.exp(s)
            p = p / jnp.sum(p, axis=-1, keepdims=True)
            dom = jnp.where(qmasks[m], dout, 0.0).astype(BF16)
            dp = _dot_nt(dom, v)
            ds = (p * (dp - jnp.sum(p * dp, axis=-1, keepdims=True)) * MEM_SCALE).astype(BF16)
            dqn = jnp.where(qmasks[m], _dot(ds, kn), dqn)
            dkn = jnp.where(kmasks[m], _dot_tn(ds, qm), dkn)
            dvv = jnp.where(kmasks[m], _dot_tn(p, dom), dvv)
        dqw_ref[...] += jnp.sum(dqn * qhat, axis=0, keepdims=True)
        dq_ref[...] = _mem_head_rms_bwd(dqn * qw_ref[...], qhat, qr, qmasks).astype(BF16)
        dk_acc[...] += dkn
        dv_acc[...] += dvv

        @pl.when(step == nsteps - 1)
        def _():
            dk = dk_acc[...]
            dkw_ref[...] = jnp.sum(dk * khat, axis=0, keepdims=True)
            dkv_ref[:, :MEM_WIDTH] = _mem_head_rms_bwd(dk * kw_ref[...], khat, kr, kmasks)
            dkv_ref[:, MEM_WIDTH:] = dv_acc[...]

    return pl.pallas_call(
        body, grid=(nsteps,),
        in_specs=[pl.BlockSpec((tt, MEM_WIDTH), lambda i: (i, qcol)), pl.BlockSpec((MEM_TOKENS, MEM_WIDTH), lambda i: (0, 0)),
                  pl.BlockSpec((MEM_TOKENS, MEM_WIDTH), lambda i: (0, 1)), _full((1, MEM_WIDTH)), _full((1, MEM_WIDTH)),
                  pl.BlockSpec((tt, MEM_WIDTH), lambda i: (i, ocol))],
        out_specs=[pl.BlockSpec((tt, MEM_WIDTH), lambda i: (i, 0)), _full((MEM_TOKENS, 2 * MEM_WIDTH)),
                   _full((1, MEM_WIDTH)), _full((1, MEM_WIDTH))],
        out_shape=[jax.ShapeDtypeStruct((T, MEM_WIDTH), BF16), jax.ShapeDtypeStruct((MEM_TOKENS, 2 * MEM_WIDTH), F32),
                   jax.ShapeDtypeStruct((1, MEM_WIDTH), F32), jax.ShapeDtypeStruct((1, MEM_WIDTH), F32)],
        scratch_shapes=[pltpu.VMEM((MEM_TOKENS, MEM_WIDTH), F32), pltpu.VMEM((MEM_TOKENS, MEM_WIDTH), F32)],
        compiler_params=_cp("arbitrary"), name=name)(proj, mkv, mkv, qn_w, kn_w, dmix)


HALF = HEAD_DIM // 2
ATT_SCALE = HEAD_DIM ** -0.5
NEG = -1e30


def _rope_tables(T):
    inv = ROPE_THETA ** (-jnp.arange(HALF, dtype=F32) / HALF)
    ang = jnp.arange(T, dtype=F32)[:, None] * inv[None, :]
    cos, sin = jnp.cos(ang), jnp.sin(ang)
    return jnp.concatenate([cos, cos], axis=-1), jnp.concatenate([-sin, sin], axis=-1)


def _rope(x, cosf, sinsg):
    return x * cosf + pltpu.roll(x, HALF, 1) * sinsg


def _rope_bwd(dy, cosf, sinsg):
    return dy * cosf + pltpu.roll(dy * sinsg, HALF, 1)


def _headnorm_rope_fwd(x, w_heads, cosf, sinsg, *, col0, n_heads, tt, name):
    T = x.shape[0]
    W = n_heads * HEAD_DIM

    def body(x_ref, w_ref, c_ref, s_ref, y_ref):
        c, s = c_ref[...], s_ref[...]
        for h in range(n_heads):
            sl = slice(h * HEAD_DIM, (h + 1) * HEAD_DIM)
            xhat, _ = _head_rms(x_ref[:, sl])
            y_ref[:, sl] = _rope(xhat * w_ref[:, sl], c, s)

    tbl = pl.BlockSpec((tt, HEAD_DIM), lambda i: (i, 0))
    return pl.pallas_call(
        body, grid=(T // tt,),
        in_specs=[pl.BlockSpec((tt, W), lambda i: (i, col0)), _full((1, W)), tbl, tbl],
        out_specs=pl.BlockSpec((tt, W), lambda i: (i, 0)),
        out_shape=jax.ShapeDtypeStruct((T, W), F32),
        compiler_params=_cp("parallel"), name=name)(x, w_heads, cosf, sinsg)


def _q_prep_bwd(proj, w_heads, cosf, sinsg, dqs, *, tt, name):
    T = proj.shape[0]
    W = N_GROUPS * B_WIDTH

    def body(x_ref, w_ref, c_ref, s_ref, d0, d1, d2, dx_ref, dw_ref):
        @pl.when(pl.program_id(0) == 0)
        def _():
            dw_ref[...] = jnp.zeros_like(dw_ref)

        c, s = c_ref[...], s_ref[...]
        for gi, d_ref in enumerate((d0, d1, d2)):
            for h in range(B_HEADS):
                sl = slice((gi * B_HEADS + h) * HEAD_DIM, (gi * B_HEADS + h + 1) * HEAD_DIM)
                xhat, r = _head_rms(x_ref[:, sl])
                dyn = _rope_bwd(d_ref[:, h * HEAD_DIM:(h + 1) * HEAD_DIM], c, s)
                dw_ref[:, sl] += jnp.sum(dyn * xhat, axis=0, keepdims=True)
                dx_ref[:, sl] = _head_rms_bwd(dyn * w_ref[:, sl], xhat, r).astype(BF16)

    tbl = pl.BlockSpec((tt, HEAD_DIM), lambda i: (i, 0))
    dyb = pl.BlockSpec((tt, B_WIDTH), lambda i: (i, 0))
    return pl.pallas_call(
        body, grid=(T // tt,),
        in_specs=[pl.BlockSpec((tt, W), lambda i: (i, 0)), _full((1, W)), tbl, tbl, dyb, dyb, dyb],
        out_specs=[pl.BlockSpec((tt, W), lambda i: (i, 0)), _full((1, W))],
        out_shape=[jax.ShapeDtypeStruct((T, W), BF16), jax.ShapeDtypeStruct((1, W), F32)],
        compiler_params=_cp("arbitrary"), name=name)(proj, w_heads, cosf, sinsg, *dqs)


def _kv_prep_bwd(kv, w_heads, cosf, sinsg, dks, dvs, *, tt, name):
    T = kv.shape[0]

    def body(x_ref, w_ref, c_ref, s_ref, k0, k1, k2, v0, v1, v2, dx_ref, dw_ref):
        @pl.when(pl.program_id(0) == 0)
        def _():
            dw_ref[...] = jnp.zeros_like(dw_ref)

        c, s = c_ref[...], s_ref[...]
        for h in range(B_HEADS):
            sl = slice(h * HEAD_DIM, (h + 1) * HEAD_DIM)
            vs = slice(B_WIDTH + h * HEAD_DIM, B_WIDTH + (h + 1) * HEAD_DIM)
            xhat, r = _head_rms(x_ref[:, sl])
            dyn = _rope_bwd(k0[:, sl] + k1[:, sl] + k2[:, sl], c, s)
            dw_ref[:, sl] += jnp.sum(dyn * xhat, axis=0, keepdims=True)
            dx_ref[:, sl] = _head_rms_bwd(dyn * w_ref[:, sl], xhat, r).astype(BF16)
            dx_ref[:, vs] = (v0[:, sl] + v1[:, sl] + v2[:, sl]).astype(BF16)

    tbl = pl.BlockSpec((tt, HEAD_DIM), lambda i: (i, 0))
    dyb = pl.BlockSpec((tt, B_WIDTH), lambda i: (i, 0))
    return pl.pallas_call(
        body, grid=(T // tt,),
        in_specs=[dyb, _full((1, B_WIDTH)), tbl, tbl] + [dyb] * 6,
        out_specs=[pl.BlockSpec((tt, 2 * B_WIDTH), lambda i: (i, 0)), _full((1, B_WIDTH))],
        out_shape=[jax.ShapeDtypeStruct((T, 2 * B_WIDTH), BF16), jax.ShapeDtypeStruct((1, B_WIDTH), F32)],
        compiler_params=_cp("arbitrary"), name=name)(kv, w_heads, cosf, sinsg, *dks, *dvs)


def _band_masks(n_is_first=None):
    row = lax.broadcasted_iota(jnp.int32, (SPAN, SPAN), 0)
    col = lax.broadcasted_iota(jnp.int32, (SPAN, SPAN), 1)
    return row >= col, col >= row


def _dil_views(T, d):
    L = T // d
    return L, L // SPAN


def _dil_fwd(qr, kr, kv, gi, d, *, name):
    T = qr.shape[0]
    L, nb = _dil_views(T, d)

    def body(q_ref, kc_ref, kp_ref, vc_ref, vp_ref, o_ref, lse_ref):
        cur_ok, prev_band = _band_masks()
        prev_ok = prev_band & (pl.program_id(1) > 0)
        for h in range(B_HEADS):
            sl = slice(h * HEAD_DIM, (h + 1) * HEAD_DIM)
            q = q_ref[:, sl]
            sc = jnp.where(cur_ok, _dot_nt(q, kc_ref[:, sl]) * ATT_SCALE, NEG)
            sp = jnp.where(prev_ok, _dot_nt(q, kp_ref[:, sl]) * ATT_SCALE, NEG)
            m = jnp.maximum(jnp.max(sc, axis=-1, keepdims=True), jnp.max(sp, axis=-1, keepdims=True))
            pc = jnp.exp(sc - m)
            pp = jnp.exp(sp - m)
            l = jnp.sum(pc, axis=-1, keepdims=True) + jnp.sum(pp, axis=-1, keepdims=True)
            o_ref[:, sl] = (_dot(pc, vc_ref[:, sl]) + _dot(pp, vp_ref[:, sl])) / l
            lse_ref[:, sl] = jnp.broadcast_to(m + jnp.log(l), (SPAN, HEAD_DIM))

    blk = lambda f: pl.BlockSpec((SPAN, B_WIDTH), f)
    cur = lambda r, n: (n, r)
    prev = lambda r, n: (jnp.maximum(n - 1, 0), r)
    ov = jax.ShapeDtypeStruct((L, d * B_WIDTH), F32)
    o, lse = pl.pallas_call(
        body, grid=(d, nb),
        in_specs=[blk(lambda r, n: (n, r * N_GROUPS + gi)), blk(cur), blk(prev),
                  blk(lambda r, n: (n, 2 * r + 1)), blk(lambda r, n: (jnp.maximum(n - 1, 0), 2 * r + 1))],
        out_specs=[blk(cur), blk(cur)], out_shape=[ov, ov],
        compiler_params=_cp("parallel", "arbitrary"), name=name,
    )(qr.reshape(L, d * N_GROUPS * B_WIDTH), kr.reshape(L, d * B_WIDTH), kr.reshape(L, d * B_WIDTH),
      kv.reshape(L, d * 2 * B_WIDTH), kv.reshape(L, d * 2 * B_WIDTH))
    return o.reshape(T, B_WIDTH), lse.reshape(T, B_WIDTH)


def _dil_combine_fwd(os_, lses, *, tt, name):
    T = os_[0].shape[0]

    def body(o0, o1, o2, l0, l1, l2, y_ref, lse_ref):
        a, b, c = l0[...], l1[...], l2[...]
        m = jnp.maximum(jnp.maximum(a, b), c)
        wa, wb, wc = jnp.exp(a - m), jnp.exp(b - m), jnp.exp(c - m)
        den = wa + wb + wc
        y_ref[...] = (wa * o0[...] + wb * o1[...] + wc * o2[...]) / den
        lse_ref[...] = m + jnp.log(den)

    blk = pl.BlockSpec((tt, B_WIDTH), lambda i: (i, 0))
    sh = jax.ShapeDtypeStruct((T, B_WIDTH), F32)
    return pl.pallas_call(
        body, grid=(T // tt,), in_specs=[blk] * 6, out_specs=[blk, blk], out_shape=[sh, sh],
        compiler_params=_cp("parallel"), name=name)(*os_, *lses)


def _dil_bwd_prep(dmix, mix_main, *, tt, name, dep=None):
    T = mix_main.shape[0]

    def kernel_body(dy_ref, y_ref, dmm_ref, dd_ref):
        for h in range(B_HEADS):
            sl = slice(h * HEAD_DIM, (h + 1) * HEAD_DIM)
            dy = dy_ref[:, sl]
            dmm_ref[:, sl] = dy.astype(BF16)
            dd_ref[:, sl] = jnp.broadcast_to(jnp.sum(dy * y_ref[:, sl], axis=-1, keepdims=True), (tt, HEAD_DIM))

    blk = pl.BlockSpec((tt, B_WIDTH), lambda i: (i, 0))
    body, dep_specs, dep_args = _dep(kernel_body, 2, dep)
    return pl.pallas_call(
        body, grid=(T // tt,), in_specs=[blk, blk] + dep_specs, out_specs=[blk, blk],
        out_shape=[jax.ShapeDtypeStruct((T, B_WIDTH), BF16), jax.ShapeDtypeStruct((T, B_WIDTH), F32)],
        compiler_params=_cp("parallel"), name=name)(dmix, mix_main, *dep_args)


def _dil_bwd_dq(qr, kr, kv, dmm, lse, dd, gi, d, *, name):
    T = qr.shape[0]
    L, nb = _dil_views(T, d)

    def body(q_ref, kc_ref, kp_ref, vc_ref, vp_ref, dy_ref, lse_ref, dd_ref, dq_ref):
        cur_ok, prev_band = _band_masks()
        prev_ok = prev_band & (pl.program_id(1) > 0)
        for h in range(B_HEADS):
            sl = slice(h * HEAD_DIM, (h + 1) * HEAD_DIM)
            q, dy = q_ref[:, sl], dy_ref[:, sl]
            kc, kp = kc_ref[:, sl], kp_ref[:, sl]
            lse_h = jnp.max(lse_ref[:, sl], axis=-1, keepdims=True)
            dd_h = jnp.max(dd_ref[:, sl], axis=-1, keepdims=True)
            pc = jnp.exp(jnp.where(cur_ok, _dot_nt(q, kc) * ATT_SCALE, NEG) - lse_h)
            pp = jnp.exp(jnp.where(prev_ok, _dot_nt(q, kp) * ATT_SCALE, NEG) - lse_h)
            dsc = pc * (_dot_nt(dy, vc_ref[:, sl]) - dd_h) * ATT_SCALE
            dsp = pp * (_dot_nt(dy, vp_ref[:, sl]) - dd_h) * ATT_SCALE
            dq_ref[:, sl] = _dot(dsc, kc) + _dot(dsp, kp)

    blk = lambda f: pl.BlockSpec((SPAN, B_WIDTH), f)
    cur = lambda r, n: (n, r)
    prev = lambda r, n: (jnp.maximum(n - 1, 0), r)
    v2 = lambda a: a.reshape(L, d * a.shape[1])
    dq = pl.pallas_call(
        body, grid=(d, nb),
        in_specs=[blk(lambda r, n: (n, r * N_GROUPS + gi)), blk(cur), blk(prev),
                  blk(lambda r, n: (n, 2 * r + 1)), blk(lambda r, n: (jnp.maximum(n - 1, 0), 2 * r + 1)),
                  blk(cur), blk(cur), blk(cur)],
        out_specs=blk(cur), out_shape=jax.ShapeDtypeStruct((L, d * B_WIDTH), F32),
        compiler_params=_cp("parallel", "arbitrary"), name=name,
    )(v2(qr), v2(kr), v2(kr), v2(kv), v2(kv), v2(dmm), v2(lse), v2(dd))
    return dq.reshape(T, B_WIDTH)


def _dil_bwd_dkv(qr, kr, kv, dmm, lse, dd, gi, d, *, name):
    T = qr.shape[0]
    L, nb = _dil_views(T, d)

    def body(k_ref, v_ref, q0_ref, q1_ref, dy0_ref, dy1_ref, lse0_ref, lse1_ref, dd0_ref, dd1_ref, dk_ref, dv_ref):
        cur_ok, prev_band = _band_masks()
        next_ok = prev_band & (pl.program_id(1) < nb - 1)
        for h in range(B_HEADS):
            sl = slice(h * HEAD_DIM, (h + 1) * HEAD_DIM)
            k, v = k_ref[:, sl], v_ref[:, sl]
            dk = jnp.zeros((SPAN, HEAD_DIM), F32)
            dv = jnp.zeros((SPAN, HEAD_DIM), F32)
            for ok, q_ref, dy_ref, lse_ref, dd_ref in ((cur_ok, q0_ref, dy0_ref, lse0_ref, dd0_ref),
                                                         (next_ok, q1_ref, dy1_ref, lse1_ref, dd1_ref)):
                q, dy = q_ref[:, sl], dy_ref[:, sl]
                lse_h = jnp.max(lse_ref[:, sl], axis=-1, keepdims=True)
                dd_h = jnp.max(dd_ref[:, sl], axis=-1, keepdims=True)
                p = jnp.exp(jnp.where(ok, _dot_nt(q, k) * ATT_SCALE, NEG) - lse_h)
                ds = p * (_dot_nt(dy, v) - dd_h) * ATT_SCALE
                dk = dk + _dot_tn(ds, q)
                dv = dv + _dot_tn(p, dy)
            dk_ref[:, sl] = dk
            dv_ref[:, sl] = dv

    blk = lambda f: pl.BlockSpec((SPAN, B_WIDTH), f)
    cur = lambda r, n: (n, r)
    nxt = lambda r, n: (jnp.minimum(n + 1, nb - 1), r)
    qcur = lambda r, n: (n, r * N_GROUPS + gi)
    qnxt = lambda r, n: (jnp.minimum(n + 1, nb - 1), r * N_GROUPS + gi)
    v2 = lambda a: a.reshape(L, d * a.shape[1])
    ov = jax.ShapeDtypeStruct((L, d * B_WIDTH), F32)
    dk, dv = pl.pallas_call(
        body, grid=(d, nb),
        in_specs=[blk(cur), blk(lambda r, n: (n, 2 * r + 1)), blk(qcur), blk(qnxt),
                  blk(cur), blk(nxt), blk(cur), blk(nxt), blk(cur), blk(nxt)],
        out_specs=[blk(cur), blk(cur)], out_shape=[ov, ov],
        compiler_params=_cp("parallel", "arbitrary"), name=name,
    )(v2(kr), v2(kv), v2(qr), v2(qr), v2(dmm), v2(dmm), v2(lse), v2(lse), v2(dd), v2(dd))
    return dk.reshape(T, B_WIDTH), dv.reshape(T, B_WIDTH)


def _dils_specs(gi, d, nblk):
    blk = lambda f: pl.BlockSpec((SPAN * d, HEAD_DIM), f)
    return {
        "q": blk(lambda h, n: (n, gi * B_HEADS + h)), "q_next": blk(lambda h, n: (jnp.minimum(n + 1, nblk - 1), gi * B_HEADS + h)),
        "cur": blk(lambda h, n: (n, h)), "prev": blk(lambda h, n: (jnp.maximum(n - 1, 0), h)),
        "next": blk(lambda h, n: (jnp.minimum(n + 1, nblk - 1), h)),
        "v": blk(lambda h, n: (n, B_HEADS + h)), "v_prev": blk(lambda h, n: (jnp.maximum(n - 1, 0), B_HEADS + h)),
    }


def _dils_fwd(qr, kr, kv, gi, d, *, name):
    T = qr.shape[0]
    nblk = T // (SPAN * d)
    sp = _dils_specs(gi, d, nblk)

    def body(q_ref, kc_ref, kp_ref, vc_ref, vp_ref, o_ref, lse_ref):
        cur_ok, prev_band = _band_masks()
        prev_ok = prev_band & (pl.program_id(1) > 0)

        def residue(r, carry):
            rows = pl.ds(r, SPAN, stride=d)
            q = q_ref[rows, :]
            sc = jnp.where(cur_ok, _dot_nt(q, kc_ref[rows, :]) * ATT_SCALE, NEG)
            sp_ = jnp.where(prev_ok, _dot_nt(q, kp_ref[rows, :]) * ATT_SCALE, NEG)
            m = jnp.maximum(jnp.max(sc, axis=-1, keepdims=True), jnp.max(sp_, axis=-1, keepdims=True))
            pc = jnp.exp(sc - m)
            pp = jnp.exp(sp_ - m)
            l = jnp.sum(pc, axis=-1, keepdims=True) + jnp.sum(pp, axis=-1, keepdims=True)
            o_ref[rows, :] = (_dot(pc, vc_ref[rows, :]) + _dot(pp, vp_ref[rows, :])) / l
            lse_ref[rows, :] = jnp.broadcast_to(m + jnp.log(l), (SPAN, HEAD_DIM))
            return carry

        lax.fori_loop(0, d, residue, 0)

    sh = jax.ShapeDtypeStruct((T, B_WIDTH), F32)
    return pl.pallas_call(
        body, grid=(B_HEADS, nblk), in_specs=[sp["q"], sp["cur"], sp["prev"], sp["v"], sp["v_prev"]],
        out_specs=[sp["cur"], sp["cur"]], out_shape=[sh, sh],
        compiler_params=_cp("parallel", "arbitrary"), name=name)(qr, kr, kr, kv, kv)


def _dils_bwd_dq(qr, kr, kv, dmix, lse, dd, gi, d, *, name):
    T = qr.shape[0]
    nblk = T // (SPAN * d)
    sp = _dils_specs(gi, d, nblk)

    def body(q_ref, kc_ref, kp_ref, vc_ref, vp_ref, dy_ref, lse_ref, dd_ref, dq_ref):
        cur_ok, prev_band = _band_masks()
        prev_ok = prev_band & (pl.program_id(1) > 0)

        def residue(r, carry):
            rows = pl.ds(r, SPAN, stride=d)
            q, dy = q_ref[rows, :], dy_ref[rows, :]
            kc, kp = kc_ref[rows, :], kp_ref[rows, :]
            lse_h = jnp.max(lse_ref[rows, :], axis=-1, keepdims=True)
            dd_h = jnp.max(dd_ref[rows, :], axis=-1, keepdims=True)
            pc = jnp.exp(jnp.where(cur_ok, _dot_nt(q, kc) * ATT_SCALE, NEG) - lse_h)
            pp = jnp.exp(jnp.where(prev_ok, _dot_nt(q, kp) * ATT_SCALE, NEG) - lse_h)
            dsc = pc * (_dot_nt(dy, vc_ref[rows, :]) - dd_h) * ATT_SCALE
            dsp = pp * (_dot_nt(dy, vp_ref[rows, :]) - dd_h) * ATT_SCALE
            dq_ref[rows, :] = _dot(dsc, kc) + _dot(dsp, kp)
            return carry

        lax.fori_loop(0, d, residue, 0)

    return pl.pallas_call(
        body, grid=(B_HEADS, nblk),
        in_specs=[sp["q"], sp["cur"], sp["prev"], sp["v"], sp["v_prev"], sp["cur"], sp["cur"], sp["cur"]],
        out_specs=sp["cur"], out_shape=jax.ShapeDtypeStruct((T, B_WIDTH), F32),
        compiler_params=_cp("parallel", "arbitrary"), name=name)(qr, kr, kr, kv, kv, dmix, lse, dd)


def _dils_bwd_dkv(qr, kr, kv, dmix, lse, dd, gi, d, *, name):
    T = qr.shape[0]
    nblk = T // (SPAN * d)
    sp = _dils_specs(gi, d, nblk)

    def body(k_ref, v_ref, q0_ref, q1_ref, dy0_ref, dy1_ref, lse0_ref, lse1_ref, dd0_ref, dd1_ref, dk_ref, dv_ref):
        cur_ok, prev_band = _band_masks()
        next_ok = prev_band & (pl.program_id(1) < nblk - 1)

        def residue(r, carry):
            rows = pl.ds(r, SPAN, stride=d)
            k, v = k_ref[rows, :], v_ref[rows, :]
            dk = jnp.zeros((SPAN, HEAD_DIM), F32)
            dv = jnp.zeros((SPAN, HEAD_DIM), F32)
            for ok, q_ref, dy_ref, lse_ref, dd_ref in ((cur_ok, q0_ref, dy0_ref, lse0_ref, dd0_ref),
                                                         (next_ok, q1_ref, dy1_ref, lse1_ref, dd1_ref)):
                q, dy = q_ref[rows, :], dy_ref[rows, :]
                lse_h = jnp.max(lse_ref[rows, :], axis=-1, keepdims=True)
                dd_h = jnp.max(dd_ref[rows, :], axis=-1, keepdims=True)
                p = jnp.exp(jnp.where(ok, _dot_nt(q, k) * ATT_SCALE, NEG) - lse_h)
                ds = p * (_dot_nt(dy, v) - dd_h) * ATT_SCALE
                dk = dk + _dot_tn(ds, q)
                dv = dv + _dot_tn(p, dy)
            dk_ref[rows, :] = dk
            dv_ref[rows, :] = dv
            return carry

        lax.fori_loop(0, d, residue, 0)

    sh = jax.ShapeDtypeStruct((T, B_WIDTH), F32)
    return pl.pallas_call(
        body, grid=(B_HEADS, nblk),
        in_specs=[sp["cur"], sp["v"], sp["q"], sp["q_next"], sp["cur"], sp["next"], sp["cur"], sp["next"], sp["cur"], sp["next"]],
        out_specs=[sp["cur"], sp["cur"]], out_shape=[sh, sh],
        compiler_params=_cp("parallel", "arbitrary"), name=name)(kr, kv, qr, qr, dmix, dmix, lse, lse, dd, dd)


A_MQ_COL = 4 * A_WIDTH // MEM_WIDTH
B_MQ_COL = N_GROUPS * B_WIDTH // MEM_WIDTH


def _row(v):
    return v.reshape(1, -1).astype(F32)


def _local_step(x, mem, tgt, get_w, P, put_g, first_dep=None):
    T = x.shape[0]
    cosf, sinsg = _rope_tables(T)
    lb_soft = jax.nn.softmax(P["a_lb_logits"].astype(F32), axis=0)
    lb = lb_soft[0:1]
    qw_heads = jnp.repeat(P["b_qnorm"][0], B_HEADS, axis=0).reshape(1, -1)
    kw_heads = jnp.tile(_row(P["b_knorm"]), (1, B_HEADS))
    mqw = [jnp.tile(_row(P["mem_qnorm"][l]), (1, MEM_HEADS)) for l in range(2)]
    mkw = [jnp.tile(_row(P["mem_knorm"][l]), (1, MEM_HEADS)) for l in range(2)]
    nmix = [_row(P["norm_mix"][l]) for l in range(2)]
    nffn = [_row(P["norm_ffn"][l]) for l in range(2)]
    mnorm = [_row(P["mem_norm"][l]) for l in range(2)]
    kvn = _row(P["kv_norm"])
    onorm = _row(P["a_onorm"])
    W = {}

    def w_of(name, after=None):
        if name not in W:
            W[name] = get_w(name, after)
        return W[name]

    proj_a, xn0 = _rms_matmul(x, nmix[0], w_of("a_w_in"), tt=512, tn=1664, wt=True, name="proj_a", dep=first_dep)
    mkv0, mn0 = _rms_matmul(mem, mnorm[0], w_of("w_mem_kv0"), tt=MEM_TOKENS, tn=2 * MEM_WIDTH, wt=False, name="mem_kv0")
    o_raw, st = _hgrn2_fwd(proj_a, lb, name="hgrn2_fwd")
    mm0 = _a_post_fwd(o_raw, proj_a, onorm, tt=512, name="a_post_fwd")
    mo0 = _mem_attn_fwd(proj_a, A_MQ_COL, mkv0, mqw[0], mkw[0], tt=512, name="mem_attn_fwd0")
    mix0 = jnp.concatenate([mm0, mo0], axis=1)
    hm0 = _mm_res(x, mix0, w_of("w_out0", mix0), tt=512, name="out_proj0")
    gu0, hn0 = _rms_matmul(hm0, nffn[0], w_of("w_gate_up0", hm0), tt=512, tn=1408, wt=True, name="gate_up0")
    h1 = _swiglu_down(hm0, gu0, w_of("w_down0", gu0), tt=256, name="down0")
    kv, hkn = _rms_matmul(h1, kvn, w_of("w_kv", h1), tt=512, tn=768, wt=True, name="kv_proj")
    kr = _headnorm_rope_fwd(kv, kw_heads, cosf, sinsg, col0=0, n_heads=B_HEADS, tt=512, name="k_prep")

    proj_b, xn1 = _rms_matmul(h1, nmix[1], w_of("b_w_in", kr), tt=512, tn=1280, wt=True, name="proj_b")
    mkv1, mn1 = _rms_matmul(mem, mnorm[1], w_of("w_mem_kv1", kr), tt=MEM_TOKENS, tn=2 * MEM_WIDTH, wt=False, name="mem_kv1")
    qr = _headnorm_rope_fwd(proj_b, qw_heads, cosf, sinsg, col0=0, n_heads=N_GROUPS * B_HEADS, tt=512, name="q_prep")
    outs = [(_dil_fwd if d == 1 else _dils_fwd)(qr, kr, kv, gi, d, name=f"dil_fwd{gi}") for gi, d in enumerate(DILATIONS)]
    mm1, lse_tot = _dil_combine_fwd([o for o, _ in outs], [s for _, s in outs], tt=512, name="dil_combine")
    mo1 = _mem_attn_fwd(proj_b, B_MQ_COL, mkv1, mqw[1], mkw[1], tt=512, name="mem_attn_fwd1")
    mix1 = jnp.concatenate([mm1, mo1], axis=1)
    hm1 = _mm_res(h1, mix1, w_of("w_out1", mix1), tt=512, name="out_proj1")
    gu1, hn1 = _rms_matmul(hm1, nffn[1], w_of("w_gate_up1", hm1), tt=512, tn=1408, wt=True, name="gate_up1")
    y = _swiglu_down(hm1, gu1, w_of("w_down1", gu1), tt=256, name="down1")
    dy, sq = _loss_kernel(y, tgt, tt=512, name="loss")

    gP = {}
    zeros_mem = jnp.zeros((MEM_TOKENS, D_MODEL), F32)

    def ffn_bwd(l, dh, hm, gu, hn):
        dgu, act = _swiglu_bwd(dh, gu, w_of(f"w_down{l}"), tt=256, name=f"swiglu_bwd{l}")
        g_wd = _mm_tn(act, dh, tt=512, tka=1408, name=f"g_w_down{l}")
        g_wgu = _mm_tn(dgu, hn, tt=512, tka=1408, name=f"g_w_gate_up{l}")
        sent = put_g({f"w_down{l}": g_wd, f"w_gate_up{l}": g_wgu})
        dhm, g_nf = _rms_bwd_dx(hm, nffn[l], w_of(f"w_gate_up{l}"), dgu, dh, tt=256, wt=True, name=f"gate_up_bwd{l}", dep=sent)
        return dhm, g_nf

    def mix_bwd(l, dhm, mix, proj, qcol, mkv, mn):
        dmix = _mm_nt(dhm, w_of(f"w_out{l}"), tt=512, name=f"out_proj_bwd{l}")
        g_wout = _mm_tn(mix, dhm, tt=512, tka=512, name=f"g_w_out{l}")
        dmq, dmkv, dqw, dkw = _mem_attn_bwd(proj, qcol, mkv, mqw[l], mkw[l], dmix, tt=512, name=f"mem_attn_bwd{l}")
        g_wmkv = _mm_tn(mn, dmkv, tt=MEM_TOKENS, tka=512, name=f"g_w_mem_kv{l}")
        sent = put_g({f"w_out{l}": g_wout, f"w_mem_kv{l}": g_wmkv})
        _, g_mn = _rms_bwd_dx(mem, mnorm[l], w_of(f"w_mem_kv{l}"), dmkv, zeros_mem, tt=MEM_TOKENS, wt=False, name=f"mem_kv_bwd{l}")
        fold = lambda v: v.reshape(MEM_HEADS, MEM_HEAD_DIM).sum(axis=0)
        return dmix, dmq, g_mn, fold(dqw), fold(dkw), sent

    dhm1, g_nf1 = ffn_bwd(1, dy, hm1, gu1, hn1)
    dmix1, dmq1, g_mn1, g_mq1, g_mk1, sent = mix_bwd(1, dhm1, mix1, proj_b, B_MQ_COL, mkv1, mn1)
    dmm, dd = _dil_bwd_prep(dmix1, mm1, tt=512, name="dil_bwd_prep", dep=sent)
    dqs, dks, dvs = [], [], []
    for gi, d in enumerate(DILATIONS):
        if d == 1:
            dqs.append(_dil_bwd_dq(qr, kr, kv, dmm, lse_tot, dd, gi, d, name=f"dil_bwd_dq{gi}"))
            dk_g, dv_g = _dil_bwd_dkv(qr, kr, kv, dmm, lse_tot, dd, gi, d, name=f"dil_bwd_dkv{gi}")
        else:
            dqs.append(_dils_bwd_dq(qr, kr, kv, dmix1, lse_tot, dd, gi, d, name=f"dil_bwd_dq{gi}"))
            dk_g, dv_g = _dils_bwd_dkv(qr, kr, kv, dmix1, lse_tot, dd, gi, d, name=f"dil_bwd_dkv{gi}")
        dks.append(dk_g)
        dvs.append(dv_g)
    dq_raw, dqw = _q_prep_bwd(proj_b, qw_heads, cosf, sinsg, dqs, tt=512, name="q_prep_bwd")
    dkv, dkw = _kv_prep_bwd(kv, kw_heads, cosf, sinsg, dks, dvs, tt=512, name="kv_prep_bwd")
    dproj_b = jnp.concatenate([dq_raw, dmq1], axis=1)
    g_wb = _mm_tn(dproj_b, xn1, tt=512, tka=1280, name="g_b_w_in")
    g_wkv = _mm_tn(dkv, hkn, tt=512, tka=768, name="g_w_kv")
    sent = put_g({"b_w_in": g_wb, "w_kv": g_wkv})
    dh1, g_nm1 = _rms_bwd_dx(h1, nmix[1], w_of("b_w_in"), dproj_b, dhm1, tt=256, wt=True, name="proj_b_bwd", dep=sent)
    dh1, g_kvn = _rms_bwd_dx(h1, kvn, w_of("w_kv"), dkv, dh1, tt=256, wt=True, name="kv_proj_bwd")

    dhm0, g_nf0 = ffn_bwd(0, dh1, hm0, gu0, hn0)
    dmix0, dmq0, g_mn0, g_mq0, g_mk0, sent = mix_bwd(0, dhm0, mix0, proj_a, A_MQ_COL, mkv0, mn0)
    do_raw, dg, g_onorm = _a_post_bwd(o_raw, proj_a, onorm, dmix0, tt=512, name="a_post_bwd", dep=sent)
    dq, dz, dv, dlb = _hgrn2_bwd(proj_a, lb, st, do_raw, name="hgrn2_bwd")
    dproj_a = jnp.concatenate([dq, dz, dv, dg, dmq0], axis=1)
    sent = put_g({"a_w_in": _mm_tn(dproj_a, xn0, tt=512, tka=1664, name="g_a_w_in")})
    gx, g_nm0 = _rms_bwd_dx(x, nmix[0], w_of("a_w_in"), dproj_a, dhm0, tt=256, wt=True, name="proj_a_bwd", dep=sent)

    dl0 = lb_soft[0:1] * lb_soft[1:2] * dlb
    gP["a_lb_logits"] = jnp.concatenate([dl0, -dl0], axis=0)
    gP["a_onorm"] = g_onorm
    gP["norm_mix"] = jnp.concatenate([g_nm0, g_nm1], axis=0)
    gP["norm_ffn"] = jnp.concatenate([g_nf0, g_nf1], axis=0)
    gP["b_qnorm"] = dqw.reshape(N_GROUPS, B_HEADS, HEAD_DIM).sum(axis=1)[None]
    gP["kv_norm"] = g_kvn.reshape(-1)
    gP["b_knorm"] = dkw.reshape(B_HEADS, HEAD_DIM).sum(axis=0)
    gP["mem_norm"] = jnp.concatenate([g_mn0, g_mn1], axis=0)
    gP["mem_qnorm"] = jnp.stack([g_mq0, g_mq1])
    gP["mem_knorm"] = jnp.stack([g_mk0, g_mk1])
    return sq, gx, gP


MESH_ID = pl.DeviceIdType.MESH
HBM_SPEC = pl.BlockSpec(memory_space=pltpu.HBM)


def _position():
    return lax.axis_index("x"), lax.axis_index("y"), lax.axis_index("c")


def _all_gather(blocks, *, name):
    n = len(blocks)

    def body(*refs):
        x_refs, out_refs = refs[:n], refs[n:2 * n]
        send_sems, recv_sems, local_sems = refs[2 * n:]
        x, y, c = _position()
        me, sibling = (x, y, c), (x, y, 1 - c)
        chips = [(1 - x, y), (x, 1 - y), (1 - x, 1 - y)]

        def slot(a, px, py, pc):
            return out_refs[a].at[4 * px + 2 * py + pc]

        def copy(a, k, blk, to, src=None):
            return pltpu.make_async_remote_copy(
                src_ref=slot(a, *blk) if src is None else src, dst_ref=slot(a, *blk),
                send_sem=send_sems.at[7 * a + k], recv_sem=recv_sems.at[7 * a + k], device_id=to, device_id_type=MESH_ID)

        mine = [pltpu.make_async_copy(x_refs[a], slot(a, *me), local_sems.at[a]) for a in range(n)]
        for cp in mine:
            cp.start()
        first = []
        for a in range(n):
            first.append(copy(a, 0, me, sibling, src=x_refs[a]))
            first += [copy(a, 1 + j, me, (*chip, c), src=x_refs[a]) for j, chip in enumerate(chips)]
        for cp in first:
            cp.start()
        passed = []
        for j, chip in enumerate(chips):
            for a in range(n):
                copy(a, 1 + j, (*chip, c), me).wait_recv()
                cp = copy(a, 4 + j, (*chip, c), sibling)
                cp.start()
                passed.append(cp)
        for a in range(n):
            copy(a, 0, sibling, me).wait_recv()
            for j, chip in enumerate(chips):
                copy(a, 4 + j, (*chip, 1 - c), me).wait_recv()
        for cp in first + passed:
            cp.wait_send()
        for cp in mine:
            cp.wait()

    return pl.pallas_call(
        body, out_shape=[jax.ShapeDtypeStruct((N_DEV,) + b.shape, b.dtype) for b in blocks],
        in_specs=[HBM_SPEC] * n, out_specs=[HBM_SPEC] * n,
        scratch_shapes=[pltpu.SemaphoreType.DMA((7 * n,)), pltpu.SemaphoreType.DMA((7 * n,)), pltpu.SemaphoreType.DMA((n,))],
        name=name)(*blocks)


SEM_SPEC = pl.BlockSpec(memory_space=pltpu.SEMAPHORE)
ANY_SPEC = pl.BlockSpec(memory_space=pl.ANY)
DATAFLOW = pltpu.SideEffectType.DATAFLOW_SIDE_EFFECTING


def _peer(k, x, y, c):
    return (1 - x if (k >> 2) & 1 else x, 1 - y if (k >> 1) & 1 else y, 1 - c if k & 1 else c)


def _own_slot_filled(own_block):
    x, y, c = _position()
    zone = lax.empty((N_DEV,) + own_block.shape, own_block.dtype)
    return lax.dynamic_update_slice_in_dim(zone, own_block[None], 4 * x + 2 * y + c, axis=0)


def _split_start(srcs, scatter, after, *, name):
    n = len(srcs)
    extra = [] if after is None else [after]
    x, y, c = _position()
    me = 4 * x + 2 * y + c
    lands = [_own_slot_filled(lax.dynamic_index_in_dim(s, me, 0, keepdims=False) if scatter else s) for s in srcs]

    def body(*refs):
        src_refs, land_refs = refs[:n], refs[n:2 * n]
        send_sems, recv_sems = refs[2 * n + len(extra)], refs[2 * n + len(extra) + 1]
        token = refs[-1]
        bx, by, bc = _position()
        bme = 4 * bx + 2 * by + bc
        for a in range(n):
            for k in range(1, N_DEV):
                tx, ty, tc = _peer(k, bx, by, bc)
                src = src_refs[a].at[4 * tx + 2 * ty + tc] if scatter else src_refs[a]
                pltpu.make_async_remote_copy(
                    src_ref=src, dst_ref=land_refs[a].at[bme],
                    send_sem=send_sems.at[7 * a + k - 1], recv_sem=recv_sems.at[7 * a + k - 1],
                    device_id=(tx, ty, tc), device_id_type=MESH_ID).start()
        token[...] = jnp.zeros_like(token)

    hbm = lambda a: pltpu.HBM(a.shape, a.dtype)
    outs = pl.pallas_call(
        body, name=name,
        out_shape=(pltpu.SemaphoreType.DMA((7 * n,)), pltpu.SemaphoreType.DMA((7 * n,)),
                   *[hbm(s) for s in srcs], *[hbm(l) for l in lands], jax.ShapeDtypeStruct((8, 128), F32)),
        in_specs=[HBM_SPEC] * (2 * n) + [ANY_SPEC] * len(extra),
        out_specs=(SEM_SPEC, SEM_SPEC, *[HBM_SPEC] * (2 * n), pl.BlockSpec(memory_space=pltpu.VMEM)),
        input_output_aliases={i: 2 + i for i in range(2 * n)},
        compiler_params=pltpu.CompilerParams(has_side_effects=DATAFLOW),
    )(*[pltpu.with_memory_space_constraint(s, pltpu.HBM) for s in srcs],
      *[pltpu.with_memory_space_constraint(l, pltpu.HBM) for l in lands], *extra)
    return {"n": n, "scatter": scatter, "send": outs[0], "recv": outs[1], "srcs": outs[2:2 + n],
            "lands": outs[2 + n:2 + 2 * n], "token": outs[-1]}


def _split_wait(handle, after, *, name):
    n, scatter = handle["n"], handle["scatter"]

    def body(*refs):
        src_refs, land_refs = refs[:n], refs[n:2 * n]
        send_sems, recv_sems = refs[2 * n], refs[2 * n + 1]
        bx, by, bc = _position()
        for a in range(n):
            for k in range(1, N_DEV):
                src = src_refs[a].at[0] if scatter else src_refs[a]
                cp = pltpu.make_async_remote_copy(
                    src_ref=src, dst_ref=land_refs[a].at[0],
                    send_sem=send_sems.at[7 * a + k - 1], recv_sem=recv_sems.at[7 * a + k - 1],
                    device_id=_peer(k, bx, by, bc), device_id_type=MESH_ID)
                cp.wait_send()
                cp.wait_recv()

    hbm = lambda a: pltpu.HBM(a.shape, a.dtype)
    outs = pl.pallas_call(
        body, name=name,
        out_shape=(*[hbm(s) for s in handle["srcs"]], *[hbm(l) for l in handle["lands"]]),
        in_specs=[HBM_SPEC] * (2 * n) + [SEM_SPEC, SEM_SPEC, ANY_SPEC],
        out_specs=tuple([HBM_SPEC] * (2 * n)),
        input_output_aliases={i: i for i in range(2 * n)},
        compiler_params=pltpu.CompilerParams(has_side_effects=DATAFLOW),
    )(*handle["srcs"], *handle["lands"], handle["send"], handle["recv"], after)
    return list(outs[n:])


def _sum_sources(parts, *, tr, name):
    n, R, C = parts.shape

    def body(p_ref, o_ref):
        acc = p_ref[0].astype(F32)
        for s in range(1, n):
            acc = acc + p_ref[s].astype(F32)
        o_ref[...] = acc

    return pl.pallas_call(
        body, grid=(R // tr,), in_specs=[pl.BlockSpec((n, tr, C), lambda i: (0, i, 0))],
        out_specs=pl.BlockSpec((tr, C), lambda i: (i, 0)),
        out_shape=jax.ShapeDtypeStruct((R, C), F32), compiler_params=_cp("parallel"), name=name)(parts)


def _adamw_math(g, w, m, v):
    c1 = 1.0 - ADAM_B1 ** ADAM_STEP
    c2 = 1.0 - ADAM_B2 ** ADAM_STEP
    nm = ADAM_B1 * m + (1.0 - ADAM_B1) * g
    nv = ADAM_B2 * v + (1.0 - ADAM_B2) * (g * g)
    return -ADAM_LR * ((nm / c1) / (jnp.sqrt(nv / c2) + ADAM_EPS) + ADAM_WD * w), nm, nv


def _reduce_adamw(received, w, m, v, *, col, tr, name):
    L, R, C = w.shape

    def body(*refs):
        p_refs = refs[:L]
        w_ref, m_ref, v_ref, g_ref, d_ref, nm_ref, nv_ref = refs[L:]
        for l in range(L):
            @pl.when(pl.program_id(0) == l)
            def _(p_ref=p_refs[l]):
                acc = p_ref[0].astype(F32)
                for s in range(1, N_DEV):
                    acc = acc + p_ref[s].astype(F32)
                g = acc.T if col else acc
                g_ref[...] = g
                d_ref[...], nm_ref[...], nv_ref[...] = _adamw_math(g, w_ref[...], m_ref[...], v_ref[...])

    p_spec = (pl.BlockSpec((N_DEV, C, tr), lambda l, i: (0, 0, i)) if col
              else pl.BlockSpec((N_DEV, tr, C), lambda l, i: (0, i, 0)))
    blk = pl.BlockSpec((None, tr, C), lambda l, i: (l, i, 0))
    sh = jax.ShapeDtypeStruct((L, R, C), F32)
    return pl.pallas_call(
        body, grid=(L, R // tr), in_specs=[p_spec] * L + [blk] * 3, out_specs=[blk] * 4, out_shape=[sh] * 4,
        compiler_params=_cp("parallel", "parallel"), name=name)(*received, w, m, v)


def _adamw(g, w, m, v, *, tr, name):
    L, R, C = w.shape

    def body(g_ref, w_ref, m_ref, v_ref, d_ref, nm_ref, nv_ref):
        d_ref[...], nm_ref[...], nv_ref[...] = _adamw_math(g_ref[...], w_ref[...], m_ref[...], v_ref[...])

    blk = pl.BlockSpec((None, tr, C), lambda l, i: (l, i, 0))
    sh = jax.ShapeDtypeStruct((L, R, C), F32)
    return pl.pallas_call(
        body, grid=(L, R // tr), in_specs=[blk] * 4, out_specs=[blk] * 3, out_shape=[sh] * 3,
        compiler_params=_cp("parallel", "parallel"), name=name)(g, w, m, v)


UNITS = {
    "a_w_in": ("a_w_in", 0, True), "w_mem_kv0": ("w_mem_kv", 0, False), "w_out0": ("w_out", 0, False),
    "w_gate_up0": ("w_gate_up", 0, True), "w_down0": ("w_down", 0, False), "w_kv": ("w_kv", None, True),
    "b_w_in": ("b_w_in", 0, True), "w_mem_kv1": ("w_mem_kv", 1, False), "w_out1": ("w_out", 1, False),
    "w_gate_up1": ("w_gate_up", 1, True), "w_down1": ("w_down", 1, False),
}
BIG = ("a_w_in", "b_w_in", "w_kv", "w_mem_kv", "w_out", "w_gate_up", "w_down")
ADAMW_ROW_TILE = {"a_w_in": 256, "b_w_in": 256, "w_kv": 256, "w_mem_kv": 128, "w_out": 128, "w_gate_up": 256, "w_down": 176}


def _wire_block(weights, unit):
    name, layer, col = UNITS[unit]
    a = weights[name] if layer is None else weights[name][layer]
    return (a.T if col else a).astype(BF16)


SMALL_REPLICATED = ("norm_mix", "norm_ffn", "b_qnorm", "kv_norm", "b_knorm", "mem_norm", "mem_qnorm", "mem_knorm")
SMALL_SHARDED = ("a_lb_logits", "a_onorm")
SMALL_ORDER = SMALL_REPLICATED + SMALL_SHARDED
LANES = 128


def _prod(shape):
    n = 1
    for s in shape:
        n *= s
    return n


def _pack_flat(arrays, rows, cols, dtype):
    flat = jnp.concatenate([a.reshape(-1).astype(dtype) for a in arrays])
    return jnp.pad(flat, (0, rows * cols - flat.shape[0])).reshape(rows, cols)


def _unpack_flat(packed, shapes):
    flat = packed.reshape(-1)
    out, off = [], 0
    for s in shapes:
        out.append(flat[off:off + _prod(s)].reshape(s))
        off += _prod(s)
    return out


def kernel(x, mem, norm_mix, norm_ffn, a_w_in, a_lb_logits, a_onorm, b_w_in, b_qnorm, kv_norm, w_kv, b_knorm, mem_norm, w_mem_kv, mem_qnorm, mem_knorm, w_out, w_gate_up, w_down, loss_target, m_norm_mix, m_norm_ffn, m_a_w_in, m_a_lb_logits, m_a_onorm, m_b_w_in, m_b_qnorm, m_kv_norm, m_w_kv, m_b_knorm, m_mem_norm, m_w_mem_kv, m_mem_qnorm, m_mem_knorm, m_w_out, m_w_gate_up, m_w_down, v_norm_mix, v_norm_ffn, v_a_w_in, v_a_lb_logits, v_a_onorm, v_b_w_in, v_b_qnorm, v_kv_norm, v_w_kv, v_b_knorm, v_mem_norm, v_w_mem_kv, v_mem_qnorm, v_mem_knorm, v_w_out, v_w_gate_up, v_w_down):
    names = ("norm_mix", "norm_ffn", "a_w_in", "a_lb_logits", "a_onorm", "b_w_in", "b_qnorm", "kv_norm", "w_kv", "b_knorm",
             "mem_norm", "w_mem_kv", "mem_qnorm", "mem_knorm", "w_out", "w_gate_up", "w_down")
    w = dict(zip(names, (norm_mix, norm_ffn, a_w_in, a_lb_logits, a_onorm, b_w_in, b_qnorm, kv_norm, w_kv, b_knorm,
                         mem_norm, w_mem_kv, mem_qnorm, mem_knorm, w_out, w_gate_up, w_down)))
    m = dict(zip(names, (m_norm_mix, m_norm_ffn, m_a_w_in, m_a_lb_logits, m_a_onorm, m_b_w_in, m_b_qnorm, m_kv_norm, m_w_kv,
                         m_b_knorm, m_mem_norm, m_w_mem_kv, m_mem_qnorm, m_mem_knorm, m_w_out, m_w_gate_up, m_w_down)))
    v = dict(zip(names, (v_norm_mix, v_norm_ffn, v_a_w_in, v_a_lb_logits, v_a_onorm, v_b_w_in, v_b_qnorm, v_kv_norm, v_w_kv,
                         v_b_knorm, v_mem_norm, v_w_mem_kv, v_mem_qnorm, v_mem_knorm, v_w_out, v_w_gate_up, v_w_down)))

    first = ["a_w_in", "w_mem_kv0"]
    gathered = _all_gather([_wire_block(w, u) for u in first] + [_pack_flat([a_lb_logits, a_onorm], 8, LANES, F32)],
                           name="gather_first")
    full = {u: g.reshape(-1, g.shape[-1]) for u, g in zip(first, gathered)}
    small_in = gathered[-1].reshape(N_DEV, -1)
    P = {n: w[n] for n in SMALL_REPLICATED}
    P["a_lb_logits"] = small_in[:, :192].reshape(N_DEV, 2, 96).transpose(1, 0, 2).reshape(2, A_WIDTH)
    P["a_onorm"] = small_in[:, 192:288].reshape(1, A_WIDTH)
    later = [["w_out0", "w_gate_up0"], ["w_down0", "w_kv"], ["b_w_in", "w_mem_kv1"], ["w_out1", "w_gate_up1", "w_down1"]]
    pending = {}
    token = gathered[-1]
    for i, group in enumerate(later):
        handle = _split_start([_wire_block(w, u) for u in group], False, token, name=f"gather{i}_start")
        token = handle["token"]
        for u in group:
            pending[u] = (i, group, handle)

    def get_w(unit, after):
        if unit not in full:
            i, group, handle = pending[unit]
            for u, land in zip(group, _split_wait(handle, after, name=f"gather{i}_wait")):
                full[u] = land.reshape(-1, land.shape[-1])
        return full[unit]

    sent = []

    def put_g(group):
        units = list(group)
        handle = _split_start([group[u].reshape(N_DEV, -1, group[u].shape[-1]) for u in units], True, None,
                              name=f"scatter{len(sent)}_start")
        sent.append((units, handle))
        return handle["token"]

    sq, gx, gP = _local_step(x[0], mem[0], loss_target[0], get_w, P, put_g, first_dep=token)
    loss = lax.psum(0.5 * jnp.sum(sq) / D_MODEL, ("x", "y", "c"))

    received = {}
    for i, (units, handle) in enumerate(sent):
        received.update(zip(units, _split_wait(handle, gx, name=f"scatter{i}_wait")))
    out = {"grad": {}, "delta": {}, "new_m": {}, "new_v": {}}
    for n in BIG:
        shape = w[n].shape
        as3 = lambda a: a.reshape((-1,) + shape[-2:])
        mine = [u for u, (wn, _, _) in UNITS.items() if wn == n]
        res = _reduce_adamw([received[u] for u in mine], as3(w[n]), as3(m[n]), as3(v[n]), col=UNITS[mine[0]][2],
                            tr=ADAMW_ROW_TILE[n], name=f"adamw_{n}")
        for kind, r in zip(("grad", "delta", "new_m", "new_v"), res):
            out[kind][n] = r.reshape(shape)

    full_shapes = [(2, A_WIDTH) if n == "a_lb_logits" else (1, A_WIDTH) if n == "a_onorm" else w[n].shape for n in SMALL_ORDER]
    n_small = sum(_prod(s) for s in full_shapes)
    rows_small = -(-n_small // (8 * LANES)) * 8
    g_all, = _all_gather([_pack_flat([gP[n] for n in SMALL_ORDER], rows_small, LANES, F32)], name="gather_small_grads")
    g_small = dict(zip(SMALL_ORDER, _unpack_flat(_sum_sources(g_all, tr=rows_small, name="sum_small_grads"), full_shapes)))
    me = 4 * lax.axis_index("x") + 2 * lax.axis_index("y") + lax.axis_index("c")
    for n in SMALL_SHARDED:
        g_small[n] = lax.dynamic_slice_in_dim(g_small[n], me * 96, 96, axis=1)
    shapes = [w[n].shape for n in SMALL_ORDER]
    rows_upd = -(-sum(_prod(s) for s in shapes) // (8 * LANES)) * 8
    pk = lambda d: _pack_flat([d[n] for n in SMALL_ORDER], rows_upd, LANES, F32)
    res = _adamw(pk(g_small)[None], pk(w)[None], pk(m)[None], pk(v)[None], tr=rows_upd, name="adamw_small")
    out["grad"].update(g_small)
    for kind, packed in zip(("delta", "new_m", "new_v"), res):
        out[kind].update(zip(SMALL_ORDER, _unpack_flat(packed[0], shapes)))

    return (loss, gx[None], *[out["grad"][n] for n in names], *[out["delta"][n] for n in names],
            *[out["new_m"][n] for n in names], *[out["new_v"][n] for n in names])
```

```python
import functools

import jax
import jax.numpy as jnp
from jax import lax
from jax.experimental import pallas as pl
from jax.experimental.pallas import tpu as pltpu

F32 = jnp.float32
BF16 = jnp.bfloat16

N_DEV = 8
D_MODEL = 1024
HEAD_DIM = 128
A_HEADS = 6
A_WIDTH = A_HEADS * HEAD_DIM
CHUNK = 64
B_HEADS = 6
B_WIDTH = B_HEADS * HEAD_DIM
DILATIONS = (1, 4, 16)
SPAN = 128
N_GROUPS = 3
ROPE_THETA = 10000.0
MEM_TOKENS = 256
MEM_HEADS = 4
MEM_HEAD_DIM = 64
MEM_WIDTH = MEM_HEADS * MEM_HEAD_DIM
FFN_HIDDEN = 2816
EPS = 1e-6

ADAM_LR = 0.001
ADAM_B1 = 0.9
ADAM_B2 = 0.999
ADAM_EPS = 1e-08
ADAM_WD = 0.01
ADAM_STEP = 10

V7X_VMEM_LIMIT_BYTES = 56 * 1024 * 1024

NT_DIMS = (((1,), (1,)), ((), ()))
TN_DIMS = (((0,), (0,)), ((), ()))


def _cp(*sem):
    return pltpu.CompilerParams(dimension_semantics=sem, vmem_limit_bytes=V7X_VMEM_LIMIT_BYTES)


def _dot(a, b):
    return jnp.dot(a.astype(BF16), b.astype(BF16), preferred_element_type=F32)


def _dot_nt(a, b):
    return lax.dot_general(a.astype(BF16), b.astype(BF16), NT_DIMS, preferred_element_type=F32)


def _dot_tn(a, b):
    return lax.dot_general(a.astype(BF16), b.astype(BF16), TN_DIMS, preferred_element_type=F32)


def _dot3(m01, x):
    hi = x.astype(BF16)
    r1 = x - hi.astype(F32)
    mid = r1.astype(BF16)
    lo = (r1 - mid.astype(F32)).astype(BF16)
    d = functools.partial(jnp.dot, preferred_element_type=F32)
    return d(m01, hi) + d(m01, mid) + d(m01, lo)


def _sigmoid(x):
    return 1.0 / (1.0 + jnp.exp(-x))


def _full(shape):
    return pl.BlockSpec(shape, lambda *_: (0,) * len(shape))


def _dep(body, n_in, dep):
    if dep is None:
        return body, [], []

    def with_dep(*refs):
        return body(*refs[:n_in], *refs[n_in + 1:])

    return with_dep, [pl.BlockSpec(memory_space=pl.ANY)], [dep]


def _rms_matmul(x, g, w, *, tt, tn, wt, name, dep=None):
    T, K = x.shape
    N = w.shape[0] if wt else w.shape[1]

    def kernel_body(x_ref, g_ref, w_ref, y_ref, xn_ref):
        @pl.when(pl.program_id(1) == 0)
        def _():
            xf = x_ref[...]
            r = lax.rsqrt(jnp.mean(xf * xf, axis=-1, keepdims=True) + EPS)
            xn_ref[...] = (xf * r * g_ref[...]).astype(BF16)

        y_ref[...] = (_dot_nt if wt else _dot)(xn_ref[...], w_ref[...])

    w_spec = pl.BlockSpec((tn, K), lambda i, j: (j, 0)) if wt else pl.BlockSpec((K, tn), lambda i, j: (0, j))
    body, dep_specs, dep_args = _dep(kernel_body, 3, dep)
    return pl.pallas_call(
        body, grid=(T // tt, N // tn),
        in_specs=[pl.BlockSpec((tt, K), lambda i, j: (i, 0)), _full((1, K)), w_spec] + dep_specs,
        out_specs=[pl.BlockSpec((tt, tn), lambda i, j: (i, j)), pl.BlockSpec((tt, K), lambda i, j: (i, 0))],
        out_shape=[jax.ShapeDtypeStruct((T, N), F32), jax.ShapeDtypeStruct((T, K), BF16)],
        compiler_params=_cp("parallel", "arbitrary"), name=name)(x, g, w, *dep_args)


def _mm_res(res, a, w, *, tt, name):
    T, K = a.shape
    N = w.shape[1]

    def body(r_ref, a_ref, w_ref, o_ref):
        o_ref[...] = r_ref[...] + _dot(a_ref[...], w_ref[...])

    return pl.pallas_call(
        body, grid=(T // tt,),
        in_specs=[pl.BlockSpec((tt, N), lambda i: (i, 0)), pl.BlockSpec((tt, K), lambda i: (i, 0)), _full((K, N))],
        out_specs=pl.BlockSpec((tt, N), lambda i: (i, 0)),
        out_shape=jax.ShapeDtypeStruct((T, N), F32),
        compiler_params=_cp("parallel"), name=name)(res, a, w)


def _swiglu_down(h, gu, wd, *, tt, name):
    T, D = h.shape
    Fh = wd.shape[0]

    def body(h_ref, gt_ref, up_ref, w_ref, o_ref):
        gt = gt_ref[...]
        act = gt * _sigmoid(gt) * up_ref[...]
        o_ref[...] = h_ref[...] + _dot(act, w_ref[...])

    return pl.pallas_call(
        body, grid=(T // tt,),
        in_specs=[pl.BlockSpec((tt, D), lambda i: (i, 0)), pl.BlockSpec((tt, Fh), lambda i: (i, 0)),
                  pl.BlockSpec((tt, Fh), lambda i: (i, 1)), _full((Fh, D))],
        out_specs=pl.BlockSpec((tt, D), lambda i: (i, 0)),
        out_shape=jax.ShapeDtypeStruct((T, D), F32),
        compiler_params=_cp("parallel"), name=name)(h, gu, gu, wd)


def _swiglu_bwd(dh, gu, wd, *, tt, name):
    T, D = dh.shape
    Fh = wd.shape[0]

    def body(dh_ref, gt_ref, up_ref, w_ref, dgu_ref, act_ref):
        gt = gt_ref[...]
        up = up_ref[...]
        s = _sigmoid(gt)
        silu = gt * s
        dact = _dot_nt(dh_ref[...], w_ref[...])
        act_ref[...] = (silu * up).astype(BF16)
        dgu_ref[:, :Fh] = (dact * up * (s * (1.0 + gt * (1.0 - s)))).astype(BF16)
        dgu_ref[:, Fh:] = (dact * silu).astype(BF16)

    return pl.pallas_call(
        body, grid=(T // tt,),
        in_specs=[pl.BlockSpec((tt, D), lambda i: (i, 0)), pl.BlockSpec((tt, Fh), lambda i: (i, 0)),
                  pl.BlockSpec((tt, Fh), lambda i: (i, 1)), _full((Fh, D))],
        out_specs=[pl.BlockSpec((tt, 2 * Fh), lambda i: (i, 0)), pl.BlockSpec((tt, Fh), lambda i: (i, 0))],
        out_shape=[jax.ShapeDtypeStruct((T, 2 * Fh), BF16), jax.ShapeDtypeStruct((T, Fh), BF16)],
        compiler_params=_cp("parallel"), name=name)(dh, gu, gu, wd)


def _mm_nt(a, w, *, tt, name):
    T, N = a.shape
    K = w.shape[0]

    def body(a_ref, w_ref, o_ref):
        o_ref[...] = _dot_nt(a_ref[...], w_ref[...])

    return pl.pallas_call(
        body, grid=(T // tt,),
        in_specs=[pl.BlockSpec((tt, N), lambda i: (i, 0)), _full((K, N))],
        out_specs=pl.BlockSpec((tt, K), lambda i: (i, 0)),
        out_shape=jax.ShapeDtypeStruct((T, K), F32),
        compiler_params=_cp("parallel"), name=name)(a, w)


def _mm_tn(a, b, *, tt, tka, name):
    T, Ka = a.shape
    N = b.shape[1]
    last = T // tt - 1

    def body(a_ref, b_ref, o_ref, acc):
        @pl.when(pl.program_id(1) == 0)
        def _():
            acc[...] = jnp.zeros_like(acc)

        acc[...] += _dot_tn(a_ref[...], b_ref[...])

        @pl.when(pl.program_id(1) == last)
        def _():
            o_ref[...] = acc[...].astype(BF16)

    return pl.pallas_call(
        body, grid=(Ka // tka, T // tt),
        in_specs=[pl.BlockSpec((tt, tka), lambda j, t: (t, j)), pl.BlockSpec((tt, N), lambda j, t: (t, 0))],
        out_specs=pl.BlockSpec((tka, N), lambda j, t: (j, 0)),
        out_shape=jax.ShapeDtypeStruct((Ka, N), BF16),
        scratch_shapes=[pltpu.VMEM((tka, N), F32)],
        compiler_params=_cp("parallel", "arbitrary"), name=name)(a, b)


def _rms_bwd_dx(x, g, w, dy, dres, *, tt, wt, name, dep=None):
    T, K = x.shape
    N = w.shape[0] if wt else w.shape[1]

    def kernel_body(x_ref, g_ref, w_ref, dy_ref, dres_ref, dx_ref, dg_ref):
        @pl.when(pl.program_id(0) == 0)
        def _():
            dg_ref[...] = jnp.zeros_like(dg_ref)

        dxn = (_dot if wt else _dot_nt)(dy_ref[...], w_ref[...])
        xf = x_ref[...]
        r = lax.rsqrt(jnp.mean(xf * xf, axis=-1, keepdims=True) + EPS)
        xhat = xf * r
        dg_ref[...] += jnp.sum(dxn * xhat, axis=0, keepdims=True)
        dxhat = dxn * g_ref[...]
        dx_ref[...] = dres_ref[...] + r * (dxhat - xhat * jnp.mean(dxhat * xhat, axis=-1, keepdims=True))

    body, dep_specs, dep_args = _dep(kernel_body, 5, dep)
    return pl.pallas_call(
        body, grid=(T // tt,),
        in_specs=[pl.BlockSpec((tt, K), lambda i: (i, 0)), _full((1, K)), _full(w.shape),
                  pl.BlockSpec((tt, N), lambda i: (i, 0)), pl.BlockSpec((tt, K), lambda i: (i, 0))] + dep_specs,
        out_specs=[pl.BlockSpec((tt, K), lambda i: (i, 0)), _full((1, K))],
        out_shape=[jax.ShapeDtypeStruct((T, K), F32), jax.ShapeDtypeStruct((1, K), F32)],
        compiler_params=_cp("arbitrary"), name=name)(x, g, w, dy, dres, *dep_args)


def _loss_kernel(y, tgt, *, tt, name):
    T, D = y.shape

    def body(y_ref, t_ref, dy_ref, acc_ref):
        @pl.when(pl.program_id(0) == 0)
        def _():
            acc_ref[...] = jnp.zeros_like(acc_ref)

        e = y_ref[...] - t_ref[...]
        dy_ref[...] = e * (1.0 / D)
        acc_ref[...] += jnp.sum(e * e, axis=0, keepdims=True)

    return pl.pallas_call(
        body, grid=(T // tt,),
        in_specs=[pl.BlockSpec((tt, D), lambda i: (i, 0)), pl.BlockSpec((tt, D), lambda i: (i, 0))],
        out_specs=[pl.BlockSpec((tt, D), lambda i: (i, 0)), _full((1, D))],
        out_shape=[jax.ShapeDtypeStruct((T, D), F32), jax.ShapeDtypeStruct((1, D), F32)],
        compiler_params=_cp("arbitrary"), name=name)(y, tgt)


HGRN_TB = 512
HGRN_NCH = HGRN_TB // CHUNK
HGRN_HPB = 3


def _hgrn_chunk_fwd(q, z, lbv, tril01):
    sig = _sigmoid(z)
    f = lbv + (1.0 - lbv) * sig
    kk = 1.0 - f
    b = _dot3(tril01, jnp.log(f))
    bend = b[CHUNK - 1:CHUNK, :]
    sq = _sigmoid(q)
    eb = jnp.exp(b)
    emb = jnp.exp(-b)
    eo = jnp.exp(bend - b)
    dec = jnp.exp(bend)
    return sig, f, kk, sq, eb, emb, eo, dec


def _hgrn2_fwd(proj, lb, *, name):
    T = proj.shape[0]
    nT = T // HGRN_TB
    nC = T // CHUNK

    def body(q_ref, z_ref, v_ref, lb_ref, o_ref, st_ref, state):
        @pl.when(pl.program_id(1) == 0)
        def _():
            state[...] = jnp.zeros_like(state)

        row = lax.broadcasted_iota(jnp.int32, (CHUNK, CHUNK), 0)
        col = lax.broadcasted_iota(jnp.int32, (CHUNK, CHUNK), 1)
        causal = row >= col
        tril01 = causal.astype(BF16)

        def chunk(c, carry):
            rows = pl.ds(pl.multiple_of(c * CHUNK, CHUNK), CHUNK)
            for hh in range(HGRN_HPB):
                sl = slice(hh * HEAD_DIM, (hh + 1) * HEAD_DIM)
                q = q_ref[rows, sl]
                v = v_ref[rows, sl].astype(BF16)
                sig, f, kk, sq, eb, emb, eo, dec = _hgrn_chunk_fwd(q, z_ref[rows, sl], lb_ref[:, sl], tril01)
                qi = (q * sq * eb).astype(BF16)
                ki = (kk * emb).astype(BF16)
                ko = (kk * eo).astype(BF16)
                st = state[hh]
                att = jnp.where(causal, _dot_nt(qi, ki), 0.0)
                o_ref[rows, sl] = _dot(att, v) + _dot_nt(qi, st)
                st_ref[c, hh] = st
                state[hh] = st * dec + _dot_tn(v, ko)
            return carry

        lax.fori_loop(0, HGRN_NCH, chunk, 0)

    W = HGRN_HPB * HEAD_DIM
    nG = A_HEADS // HGRN_HPB
    hb = lambda off: pl.BlockSpec((HGRN_TB, W), lambda h, i: (i, off + h))
    return pl.pallas_call(
        body, grid=(nG, nT),
        in_specs=[hb(0), hb(nG), hb(2 * nG), pl.BlockSpec((1, W), lambda h, i: (0, h))],
        out_specs=[hb(0), pl.BlockSpec((HGRN_NCH, HGRN_HPB, HEAD_DIM, HEAD_DIM), lambda h, i: (i, h, 0, 0))],
        out_shape=[jax.ShapeDtypeStruct((T, A_WIDTH), F32), jax.ShapeDtypeStruct((nC, A_HEADS, HEAD_DIM, HEAD_DIM), F32)],
        scratch_shapes=[pltpu.VMEM((HGRN_HPB, HEAD_DIM, HEAD_DIM), F32)],
        compiler_params=_cp("parallel", "arbitrary"), name=name)(proj, proj, proj, lb)


def _hgrn2_bwd(proj, lb, st_all, do, *, name):
    T = proj.shape[0]
    nT = T // HGRN_TB

    def body(q_ref, z_ref, v_ref, lb_ref, st_ref, do_ref, dq_ref, dz_ref, dv_ref, dlb_ref, dstate):
        @pl.when(pl.program_id(1) == 0)
        def _():
            dstate[...] = jnp.zeros_like(dstate)
            dlb_ref[...] = jnp.zeros_like(dlb_ref)

        row = lax.broadcasted_iota(jnp.int32, (CHUNK, CHUNK), 0)
        col = lax.broadcasted_iota(jnp.int32, (CHUNK, CHUNK), 1)
        causal = row >= col
        tril01 = causal.astype(BF16)
        triu01 = (row <= col).astype(BF16)

        def chunk(cc, carry):
            c = HGRN_NCH - 1 - cc
            rows = pl.ds(pl.multiple_of(c * CHUNK, CHUNK), CHUNK)
            for hh in range(HGRN_HPB):
                sl = slice(hh * HEAD_DIM, (hh + 1) * HEAD_DIM)
                lbv = lb_ref[:, sl]
                q = q_ref[rows, sl]
                v = v_ref[rows, sl].astype(BF16)
                sig, f, kk, sq, eb, emb, eo, dec = _hgrn_chunk_fwd(q, z_ref[rows, sl], lbv, tril01)
                qi32 = q * sq * eb
                ki32 = kk * emb
                ko32 = kk * eo
                qi, ki, ko = qi32.astype(BF16), ki32.astype(BF16), ko32.astype(BF16)
                att = jnp.where(causal, _dot_nt(qi, ki), 0.0).astype(BF16)
                dout = do_ref[rows, sl].astype(BF16)
                st = st_ref[c, hh]
                dst = dstate[hh]
                dst16 = dst.astype(BF16)
                datt = jnp.where(causal, _dot_nt(dout, v), 0.0).astype(BF16)
                dqi = _dot(datt, ki) + _dot(dout, st)
                dki = _dot_tn(datt, qi)
                dv_ref[rows, sl] = (_dot_tn(att, dout) + _dot_nt(ko, dst16)).astype(BF16)
                dko = _dot(v, dst16)
                ddec = jnp.sum(dst * st, axis=0, keepdims=True)
                dstate[hh] = dst * dec + _dot_tn(dout, qi)
                dkk = dki * emb + dko * eo
                db = dqi * qi32 - dki * ki32 - dko * ko32
                dbend = jnp.sum(dko * ko32, axis=0, keepdims=True) + ddec * dec
                dlogf = _dot3(triu01, db) + dbend
                df = dlogf / f - dkk
                dz_ref[rows, sl] = (df * (1.0 - lbv) * sig * (1.0 - sig)).astype(BF16)
                dlb_ref[:, sl] += jnp.sum(df * (1.0 - sig), axis=0, keepdims=True)
                dq_ref[rows, sl] = (dqi * eb * (sq * (1.0 + q * (1.0 - sq)))).astype(BF16)
            return carry

        lax.fori_loop(0, HGRN_NCH, chunk, 0)

    W = HGRN_HPB * HEAD_DIM
    nG = A_HEADS // HGRN_HPB
    hb = lambda off: pl.BlockSpec((HGRN_TB, W), lambda h, i: (nT - 1 - i, off + h))
    hlb = pl.BlockSpec((1, W), lambda h, i: (0, h))
    o16 = jax.ShapeDtypeStruct((T, A_WIDTH), BF16)
    return pl.pallas_call(
        body, grid=(nG, nT),
        in_specs=[hb(0), hb(nG), hb(2 * nG), hlb,
                  pl.BlockSpec((HGRN_NCH, HGRN_HPB, HEAD_DIM, HEAD_DIM), lambda h, i: (nT - 1 - i, h, 0, 0)), hb(0)],
        out_specs=[hb(0), hb(0), hb(0), hlb],
        out_shape=[o16, o16, o16, jax.ShapeDtypeStruct((1, A_WIDTH), F32)],
        scratch_shapes=[pltpu.VMEM((HGRN_HPB, HEAD_DIM, HEAD_DIM), F32)],
        compiler_params=_cp("parallel", "arbitrary"), name=name)(proj, proj, proj, lb, st_all, do)


def _head_rms(x):
    r = lax.rsqrt(jnp.mean(x * x, axis=-1, keepdims=True) + EPS)
    return x * r, r


def _head_rms_bwd(dxhat, xhat, r):
    return r * (dxhat - xhat * jnp.mean(dxhat * xhat, axis=-1, keepdims=True))


def _a_post_fwd(o, proj, onorm, *, tt, name):
    T = o.shape[0]

    def body(o_ref, g_ref, w_ref, y_ref):
        for h in range(A_HEADS):
            sl = slice(h * HEAD_DIM, (h + 1) * HEAD_DIM)
            xhat, _ = _head_rms(o_ref[:, sl])
            g = g_ref[:, sl]
            y_ref[:, sl] = xhat * w_ref[:, sl] * (g * _sigmoid(g))

    blk = lambda c: pl.BlockSpec((tt, A_WIDTH), lambda i: (i, c))
    return pl.pallas_call(
        body, grid=(T // tt,), in_specs=[blk(0), blk(3), _full((1, A_WIDTH))], out_specs=blk(0),
        out_shape=jax.ShapeDtypeStruct((T, A_WIDTH), F32),
        compiler_params=_cp("parallel"), name=name)(o, proj, onorm)


def _a_post_bwd(o, proj, onorm, dmix, *, tt, name, dep=None):
    T = o.shape[0]

    def kernel_body(o_ref, g_ref, w_ref, dy_ref, do_ref, dg_ref, dw_ref):
        @pl.when(pl.program_id(0) == 0)
        def _():
            dw_ref[...] = jnp.zeros_like(dw_ref)

        for h in range(A_HEADS):
            sl = slice(h * HEAD_DIM, (h + 1) * HEAD_DIM)
            xhat, r = _head_rms(o_ref[:, sl])
            g = g_ref[:, sl]
            s = _sigmoid(g)
            dy = dy_ref[:, sl]
            w = w_ref[:, sl]
            dg_ref[:, sl] = (dy * xhat * w * (s * (1.0 + g * (1.0 - s)))).astype(BF16)
            dyn = dy * (g * s)
            dw_ref[:, sl] += jnp.sum(dyn * xhat, axis=0, keepdims=True)
            do_ref[:, sl] = _head_rms_bwd(dyn * w, xhat, r)

    blk = lambda c: pl.BlockSpec((tt, A_WIDTH), lambda i: (i, c))
    body, dep_specs, dep_args = _dep(kernel_body, 4, dep)
    return pl.pallas_call(
        body, grid=(T // tt,), in_specs=[blk(0), blk(3), _full((1, A_WIDTH)), blk(0)] + dep_specs,
        out_specs=[blk(0), blk(0), _full((1, A_WIDTH))],
        out_shape=[jax.ShapeDtypeStruct((T, A_WIDTH), F32), jax.ShapeDtypeStruct((T, A_WIDTH), BF16),
                   jax.ShapeDtypeStruct((1, A_WIDTH), F32)],
        compiler_params=_cp("arbitrary"), name=name)(o, proj, onorm, dmix, *dep_args)


def _mem_head_masks(n):
    lane = lax.broadcasted_iota(jnp.int32, (n, MEM_WIDTH), 1)
    return [(lane >= m * MEM_HEAD_DIM) & (lane < (m + 1) * MEM_HEAD_DIM) for m in range(MEM_HEADS)]


def _mem_head_rms(x, masks):
    x2 = x * x
    r = jnp.zeros_like(x)
    for mk in masks:
        ms = jnp.sum(jnp.where(mk, x2, 0.0), axis=-1, keepdims=True) * (1.0 / MEM_HEAD_DIM)
        r = jnp.where(mk, lax.rsqrt(ms + EPS), r)
    return x * r, r


def _mem_head_rms_bwd(dxhat, xhat, r, masks):
    t = dxhat * xhat
    m = jnp.zeros_like(t)
    for mk in masks:
        m = jnp.where(mk, jnp.sum(jnp.where(mk, t, 0.0), axis=-1, keepdims=True) * (1.0 / MEM_HEAD_DIM), m)
    return r * (dxhat - xhat * m)


MEM_SCALE = MEM_HEAD_DIM ** -0.5


def _mem_attn_fwd(proj, qcol, mkv, qn_w, kn_w, *, tt, name):
    T = proj.shape[0]

    def body(q_ref, k_ref, v_ref, qw_ref, kw_ref, o_ref):
        qmasks = _mem_head_masks(tt)
        kmasks = _mem_head_masks(MEM_TOKENS)
        qhat, _ = _mem_head_rms(q_ref[...], qmasks)
        qn = qhat * qw_ref[...]
        khat, _ = _mem_head_rms(k_ref[...], kmasks)
        kn = (khat * kw_ref[...]).astype(BF16)
        v = v_ref[...].astype(BF16)
        out = jnp.zeros((tt, MEM_WIDTH), F32)
        for m in range(MEM_HEADS):
            s = _dot_nt(jnp.where(qmasks[m], qn, 0.0), kn) * MEM_SCALE
            s = s - jnp.max(s, axis=-1, keepdims=True)
            p = jnp.exp(s)
            p = p / jnp.sum(p, axis=-1, keepdims=True)
            out = jnp.where(qmasks[m], _dot(p, v), out)
        o_ref[...] = out

    return pl.pallas_call(
        body, grid=(T // tt,),
        in_specs=[pl.BlockSpec((tt, MEM_WIDTH), lambda i: (i, qcol)), pl.BlockSpec((MEM_TOKENS, MEM_WIDTH), lambda i: (0, 0)),
                  pl.BlockSpec((MEM_TOKENS, MEM_WIDTH), lambda i: (0, 1)), _full((1, MEM_WIDTH)), _full((1, MEM_WIDTH))],
        out_specs=pl.BlockSpec((tt, MEM_WIDTH), lambda i: (i, 0)),
        out_shape=jax.ShapeDtypeStruct((T, MEM_WIDTH), F32),
        compiler_params=_cp("parallel"), name=name)(proj, mkv, mkv, qn_w, kn_w)


def _mem_attn_bwd(proj, qcol, mkv, qn_w, kn_w, dmix, *, tt, name):
    T = proj.shape[0]
    nsteps = T // tt
    ocol = (dmix.shape[1] - MEM_WIDTH) // MEM_WIDTH

    def body(q_ref, k_ref, v_ref, qw_ref, kw_ref, do_ref, dq_ref, dkv_ref, dqw_ref, dkw_ref, dk_acc, dv_acc):
        step = pl.program_id(0)

        @pl.when(step == 0)
        def _():
            dk_acc[...] = jnp.zeros_like(dk_acc)
            dv_acc[...] = jnp.zeros_like(dv_acc)
            dqw_ref[...] = jnp.zeros_like(dqw_ref)

        qmasks = _mem_head_masks(tt)
        kmasks = _mem_head_masks(MEM_TOKENS)
        qhat, qr = _mem_head_rms(q_ref[...], qmasks)
        qn = qhat * qw_ref[...]
        khat, kr = _mem_head_rms(k_ref[...], kmasks)
        kn = (khat * kw_ref[...]).astype(BF16)
        v = v_ref[...].astype(BF16)
        dout = do_ref[...]
        dqn = jnp.zeros((tt, MEM_WIDTH), F32)
        dkn = jnp.zeros((MEM_TOKENS, MEM_WIDTH), F32)
        dvv = jnp.zeros((MEM_TOKENS, MEM_WIDTH), F32)
        for m in range(MEM_HEADS):
            qm = jnp.where(qmasks[m], qn, 0.0).astype(BF16)
            s = _dot_nt(qm, kn) * MEM_SCALE
            s = s - jnp.max(s, axis=-1, keepdims=True)
            p = jnp.exp(s)
            p = p / jnp.sum(p, axis=-1, keepdims=True)
            dom = jnp.where(qmasks[m], dout, 0.0).astype(BF16)
            dp = _dot_nt(dom, v)
            ds = (p * (dp - jnp.sum(p * dp, axis=-1, keepdims=True)) * MEM_SCALE).astype(BF16)
            dqn = jnp.where(qmasks[m], _dot(ds, kn), dqn)
            dkn = jnp.where(kmasks[m], _dot_tn(ds, qm), dkn)
            dvv = jnp.where(kmasks[m], _dot_tn(p, dom), dvv)
        dqw_ref[...] += jnp.sum(dqn * qhat, axis=0, keepdims=True)
        dq_ref[...] = _mem_head_rms_bwd(dqn * qw_ref[...], qhat, qr, qmasks).astype(BF16)
        dk_acc[...] += dkn
        dv_acc[...] += dvv

        @pl.when(step == nsteps - 1)
        def _():
            dk = dk_acc[...]
            dkw_ref[...] = jnp.sum(dk * khat, axis=0, keepdims=True)
            dkv_ref[:, :MEM_WIDTH] = _mem_head_rms_bwd(dk * kw_ref[...], khat, kr, kmasks)
            dkv_ref[:, MEM_WIDTH:] = dv_acc[...]

    return pl.pallas_call(
        body, grid=(nsteps,),
        in_specs=[pl.BlockSpec((tt, MEM_WIDTH), lambda i: (i, qcol)), pl.BlockSpec((MEM_TOKENS, MEM_WIDTH), lambda i: (0, 0)),
                  pl.BlockSpec((MEM_TOKENS, MEM_WIDTH), lambda i: (0, 1)), _full((1, MEM_WIDTH)), _full((1, MEM_WIDTH)),
                  pl.BlockSpec((tt, MEM_WIDTH), lambda i: (i, ocol))],
        out_specs=[pl.BlockSpec((tt, MEM_WIDTH), lambda i: (i, 0)), _full((MEM_TOKENS, 2 * MEM_WIDTH)),
                   _full((1, MEM_WIDTH)), _full((1, MEM_WIDTH))],
        out_shape=[jax.ShapeDtypeStruct((T, MEM_WIDTH), BF16), jax.ShapeDtypeStruct((MEM_TOKENS, 2 * MEM_WIDTH), F32),
                   jax.ShapeDtypeStruct((1, MEM_WIDTH), F32), jax.ShapeDtypeStruct((1, MEM_WIDTH), F32)],
        scratch_shapes=[pltpu.VMEM((MEM_TOKENS, MEM_WIDTH), F32), pltpu.VMEM((MEM_TOKENS, MEM_WIDTH), F32)],
        compiler_params=_cp("arbitrary"), name=name)(proj, mkv, mkv, qn_w, kn_w, dmix)


HALF = HEAD_DIM // 2
ATT_SCALE = HEAD_DIM ** -0.5
NEG = -1e30


def _rope_tables(T):
    inv = ROPE_THETA ** (-jnp.arange(HALF, dtype=F32) / HALF)
    ang = jnp.arange(T, dtype=F32)[:, None] * inv[None, :]
    cos, sin = jnp.cos(ang), jnp.sin(ang)
    return jnp.concatenate([cos, cos], axis=-1), jnp.concatenate([-sin, sin], axis=-1)


def _rope(x, cosf, sinsg):
    return x * cosf + pltpu.roll(x, HALF, 1) * sinsg


def _rope_bwd(dy, cosf, sinsg):
    return dy * cosf + pltpu.roll(dy * sinsg, HALF, 1)


def _headnorm_rope_fwd(x, w_heads, cosf, sinsg, *, col0, n_heads, tt, name):
    T = x.shape[0]
    W = n_heads * HEAD_DIM

    def body(x_ref, w_ref, c_ref, s_ref, y_ref):
        c, s = c_ref[...], s_ref[...]
        for h in range(n_heads):
            sl = slice(h * HEAD_DIM, (h + 1) * HEAD_DIM)
            xhat, _ = _head_rms(x_ref[:, sl])
            y_ref[:, sl] = _rope(xhat * w_ref[:, sl], c, s)

    tbl = pl.BlockSpec((tt, HEAD_DIM), lambda i: (i, 0))
    return pl.pallas_call(
        body, grid=(T // tt,),
        in_specs=[pl.BlockSpec((tt, W), lambda i: (i, col0)), _full((1, W)), tbl, tbl],
        out_specs=pl.BlockSpec((tt, W), lambda i: (i, 0)),
        out_shape=jax.ShapeDtypeStruct((T, W), F32),
        compiler_params=_cp("parallel"), name=name)(x, w_heads, cosf, sinsg)


def _q_prep_bwd(proj, w_heads, cosf, sinsg, dqs, *, tt, name):
    T = proj.shape[0]
    W = N_GROUPS * B_WIDTH

    def body(x_ref, w_ref, c_ref, s_ref, d0, d1, d2, dx_ref, dw_ref):
        @pl.when(pl.program_id(0) == 0)
        def _():
            dw_ref[...] = jnp.zeros_like(dw_ref)

        c, s = c_ref[...], s_ref[...]
        for gi, d_ref in enumerate((d0, d1, d2)):
            for h in range(B_HEADS):
                sl = slice((gi * B_HEADS + h) * HEAD_DIM, (gi * B_HEADS + h + 1) * HEAD_DIM)
                xhat, r = _head_rms(x_ref[:, sl])
                dyn = _rope_bwd(d_ref[:, h * HEAD_DIM:(h + 1) * HEAD_DIM], c, s)
                dw_ref[:, sl] += jnp.sum(dyn * xhat, axis=0, keepdims=True)
                dx_ref[:, sl] = _head_rms_bwd(dyn * w_ref[:, sl], xhat, r).astype(BF16)

    tbl = pl.BlockSpec((tt, HEAD_DIM), lambda i: (i, 0))
    dyb = pl.BlockSpec((tt, B_WIDTH), lambda i: (i, 0))
    return pl.pallas_call(
        body, grid=(T // tt,),
        in_specs=[pl.BlockSpec((tt, W), lambda i: (i, 0)), _full((1, W)), tbl, tbl, dyb, dyb, dyb],
        out_specs=[pl.BlockSpec((tt, W), lambda i: (i, 0)), _full((1, W))],
        out_shape=[jax.ShapeDtypeStruct((T, W), BF16), jax.ShapeDtypeStruct((1, W), F32)],
        compiler_params=_cp("arbitrary"), name=name)(proj, w_heads, cosf, sinsg, *dqs)


def _kv_prep_bwd(kv, w_heads, cosf, sinsg, dks, dvs, *, tt, name):
    T = kv.shape[0]

    def body(x_ref, w_ref, c_ref, s_ref, k0, k1, k2, v0, v1, v2, dx_ref, dw_ref):
        @pl.when(pl.program_id(0) == 0)
        def _():
            dw_ref[...] = jnp.zeros_like(dw_ref)

        c, s = c_ref[...], s_ref[...]
        for h in range(B_HEADS):
            sl = slice(h * HEAD_DIM, (h + 1) * HEAD_DIM)
            vs = slice(B_WIDTH + h * HEAD_DIM, B_WIDTH + (h + 1) * HEAD_DIM)
            xhat, r = _head_rms(x_ref[:, sl])
            dyn = _rope_bwd(k0[:, sl] + k1[:, sl] + k2[:, sl], c, s)
            dw_ref[:, sl] += jnp.sum(dyn * xhat, axis=0, keepdims=True)
            dx_ref[:, sl] = _head_rms_bwd(dyn * w_ref[:, sl], xhat, r).astype(BF16)
            dx_ref[:, vs] = (v0[:, sl] + v1[:, sl] + v2[:, sl]).astype(BF16)

    tbl = pl.BlockSpec((tt, HEAD_DIM), lambda i: (i, 0))
    dyb = pl.BlockSpec((tt, B_WIDTH), lambda i: (i, 0))
    return pl.pallas_call(
        body, grid=(T // tt,),
        in_specs=[dyb, _full((1, B_WIDTH)), tbl, tbl] + [dyb] * 6,
        out_specs=[pl.BlockSpec((tt, 2 * B_WIDTH), lambda i: (i, 0)), _full((1, B_WIDTH))],
        out_shape=[jax.ShapeDtypeStruct((T, 2 * B_WIDTH), BF16), jax.ShapeDtypeStruct((1, B_WIDTH), F32)],
        compiler_params=_cp("arbitrary"), name=name)(kv, w_heads, cosf, sinsg, *dks, *dvs)


def _band_masks(n_is_first=None):
    row = lax.broadcasted_iota(jnp.int32, (SPAN, SPAN), 0)
    col = lax.broadcasted_iota(jnp.int32, (SPAN, SPAN), 1)
    return row >= col, col >= row


def _dil_views(T, d):
    L = T // d
    return L, L // SPAN


def _dil_fwd(qr, kr, kv, gi, d, *, name):
    T = qr.shape[0]
    L, nb = _dil_views(T, d)

    def body(q_ref, kc_ref, kp_ref, vc_ref, vp_ref, o_ref, lse_ref):
        cur_ok, prev_band = _band_masks()
        prev_ok = prev_band & (pl.program_id(1) > 0)
        for h in range(B_HEADS):
            sl = slice(h * HEAD_DIM, (h + 1) * HEAD_DIM)
            q = q_ref[:, sl]
            sc = jnp.where(cur_ok, _dot_nt(q, kc_ref[:, sl]) * ATT_SCALE, NEG)
            sp = jnp.where(prev_ok, _dot_nt(q, kp_ref[:, sl]) * ATT_SCALE, NEG)
            m = jnp.maximum(jnp.max(sc, axis=-1, keepdims=True), jnp.max(sp, axis=-1, keepdims=True))
            pc = jnp.exp(sc - m)
            pp = jnp.exp(sp - m)
            l = jnp.sum(pc, axis=-1, keepdims=True) + jnp.sum(pp, axis=-1, keepdims=True)
            o_ref[:, sl] = (_dot(pc, vc_ref[:, sl]) + _dot(pp, vp_ref[:, sl])) / l
            lse_ref[:, sl] = jnp.broadcast_to(m + jnp.log(l), (SPAN, HEAD_DIM))

    blk = lambda f: pl.BlockSpec((SPAN, B_WIDTH), f)
    cur = lambda r, n: (n, r)
    prev = lambda r, n: (jnp.maximum(n - 1, 0), r)
    ov = jax.ShapeDtypeStruct((L, d * B_WIDTH), F32)
    o, lse = pl.pallas_call(
        body, grid=(d, nb),
        in_specs=[blk(lambda r, n: (n, r * N_GROUPS + gi)), blk(cur), blk(prev),
                  blk(lambda r, n: (n, 2 * r + 1)), blk(lambda r, n: (jnp.maximum(n - 1, 0), 2 * r + 1))],
        out_specs=[blk(cur), blk(cur)], out_shape=[ov, ov],
        compiler_params=_cp("parallel", "arbitrary"), name=name,
    )(qr.reshape(L, d * N_GROUPS * B_WIDTH), kr.reshape(L, d * B_WIDTH), kr.reshape(L, d * B_WIDTH),
      kv.reshape(L, d * 2 * B_WIDTH), kv.reshape(L, d * 2 * B_WIDTH))
    return o.reshape(T, B_WIDTH), lse.reshape(T, B_WIDTH)


def _dil_combine_fwd(os_, lses, *, tt, name):
    T = os_[0].shape[0]

    def body(o0, o1, o2, l0, l1, l2, y_ref, lse_ref):
        a, b, c = l0[...], l1[...], l2[...]
        m = jnp.maximum(jnp.maximum(a, b), c)
        wa, wb, wc = jnp.exp(a - m), jnp.exp(b - m), jnp.exp(c - m)
        den = wa + wb + wc
        y_ref[...] = (wa * o0[...] + wb * o1[...] + wc * o2[...]) / den
        lse_ref[...] = m + jnp.log(den)

    blk = pl.BlockSpec((tt, B_WIDTH), lambda i: (i, 0))
    sh = jax.ShapeDtypeStruct((T, B_WIDTH), F32)
    return pl.pallas_call(
        body, grid=(T // tt,), in_specs=[blk] * 6, out_specs=[blk, blk], out_shape=[sh, sh],
        compiler_params=_cp("parallel"), name=name)(*os_, *lses)


def _dil_bwd_prep(dmix, mix_main, *, tt, name, dep=None):
    T = mix_main.shape[0]

    def kernel_body(dy_ref, y_ref, dmm_ref, dd_ref):
        for h in range(B_HEADS):
            sl = slice(h * HEAD_DIM, (h + 1) * HEAD_DIM)
            dy = dy_ref[:, sl]
            dmm_ref[:, sl] = dy.astype(BF16)
            dd_ref[:, sl] = jnp.broadcast_to(jnp.sum(dy * y_ref[:, sl], axis=-1, keepdims=True), (tt, HEAD_DIM))

    blk = pl.BlockSpec((tt, B_WIDTH), lambda i: (i, 0))
    body, dep_specs, dep_args = _dep(kernel_body, 2, dep)
    return pl.pallas_call(
        body, grid=(T // tt,), in_specs=[blk, blk] + dep_specs, out_specs=[blk, blk],
        out_shape=[jax.ShapeDtypeStruct((T, B_WIDTH), BF16), jax.ShapeDtypeStruct((T, B_WIDTH), F32)],
        compiler_params=_cp("parallel"), name=name)(dmix, mix_main, *dep_args)


def _dil_bwd_dq(qr, kr, kv, dmm, lse, dd, gi, d, *, name):
    T = qr.shape[0]
    L, nb = _dil_views(T, d)

    def body(q_ref, kc_ref, kp_ref, vc_ref, vp_ref, dy_ref, lse_ref, dd_ref, dq_ref):
        cur_ok, prev_band = _band_masks()
        prev_ok = prev_band & (pl.program_id(1) > 0)
        for h in range(B_HEADS):
            sl = slice(h * HEAD_DIM, (h + 1) * HEAD_DIM)
            q, dy = q_ref[:, sl], dy_ref[:, sl]
            kc, kp = kc_ref[:, sl], kp_ref[:, sl]
            lse_h = jnp.max(lse_ref[:, sl], axis=-1, keepdims=True)
            dd_h = jnp.max(dd_ref[:, sl], axis=-1, keepdims=True)
            pc = jnp.exp(jnp.where(cur_ok, _dot_nt(q, kc) * ATT_SCALE, NEG) - lse_h)
            pp = jnp.exp(jnp.where(prev_ok, _dot_nt(q, kp) * ATT_SCALE, NEG) - lse_h)
            dsc = pc * (_dot_nt(dy, vc_ref[:, sl]) - dd_h) * ATT_SCALE
            dsp = pp * (_dot_nt(dy, vp_ref[:, sl]) - dd_h) * ATT_SCALE
            dq_ref[:, sl] = _dot(dsc, kc) + _dot(dsp, kp)

    blk = lambda f: pl.BlockSpec((SPAN, B_WIDTH), f)
    cur = lambda r, n: (n, r)
    prev = lambda r, n: (jnp.maximum(n - 1, 0), r)
    v2 = lambda a: a.reshape(L, d * a.shape[1])
    dq = pl.pallas_call(
        body, grid=(d, nb),
        in_specs=[blk(lambda r, n: (n, r * N_GROUPS + gi)), blk(cur), blk(prev),
                  blk(lambda r, n: (n, 2 * r + 1)), blk(lambda r, n: (jnp.maximum(n - 1, 0), 2 * r + 1)),
                  blk(cur), blk(cur), blk(cur)],
        out_specs=blk(cur), out_shape=jax.ShapeDtypeStruct((L, d * B_WIDTH), F32),
        compiler_params=_cp("parallel", "arbitrary"), name=name,
    )(v2(qr), v2(kr), v2(kr), v2(kv), v2(kv), v2(dmm), v2(lse), v2(dd))
    return dq.reshape(T, B_WIDTH)


def _dil_bwd_dkv(qr, kr, kv, dmm, lse, dd, gi, d, *, name):
    T = qr.shape[0]
    L, nb = _dil_views(T, d)

    def body(k_ref, v_ref, q0_ref, q1_ref, dy0_ref, dy1_ref, lse0_ref, lse1_ref, dd0_ref, dd1_ref, dk_ref, dv_ref):
        cur_ok, prev_band = _band_masks()
        next_ok = prev_band & (pl.program_id(1) < nb - 1)
        for h in range(B_HEADS):
            sl = slice(h * HEAD_DIM, (h + 1) * HEAD_DIM)
            k, v = k_ref[:, sl], v_ref[:, sl]
            dk = jnp.zeros((SPAN, HEAD_DIM), F32)
            dv = jnp.zeros((SPAN, HEAD_DIM), F32)
            for ok, q_ref, dy_ref, lse_ref, dd_ref in ((cur_ok, q0_ref, dy0_ref, lse0_ref, dd0_ref),
                                                         (next_ok, q1_ref, dy1_ref, lse1_ref, dd1_ref)):
                q, dy = q_ref[:, sl], dy_ref[:, sl]
                lse_h = jnp.max(lse_ref[:, sl], axis=-1, keepdims=True)
                dd_h = jnp.max(dd_ref[:, sl], axis=-1, keepdims=True)
                p = jnp.exp(jnp.where(ok, _dot_nt(q, k) * ATT_SCALE, NEG) - lse_h)
                ds = p * (_dot_nt(dy, v) - dd_h) * ATT_SCALE
                dk = dk + _dot_tn(ds, q)
                dv = dv + _dot_tn(p, dy)
            dk_ref[:, sl] = dk
            dv_ref[:, sl] = dv

    blk = lambda f: pl.BlockSpec((SPAN, B_WIDTH), f)
    cur = lambda r, n: (n, r)
    nxt = lambda r, n: (jnp.minimum(n + 1, nb - 1), r)
    qcur = lambda r, n: (n, r * N_GROUPS + gi)
    qnxt = lambda r, n: (jnp.minimum(n + 1, nb - 1), r * N_GROUPS + gi)
    v2 = lambda a: a.reshape(L, d * a.shape[1])
    ov = jax.ShapeDtypeStruct((L, d * B_WIDTH), F32)
    dk, dv = pl.pallas_call(
        body, grid=(d, nb),
        in_specs=[blk(cur), blk(lambda r, n: (n, 2 * r + 1)), blk(qcur), blk(qnxt),
                  blk(cur), blk(nxt), blk(cur), blk(nxt), blk(cur), blk(nxt)],
        out_specs=[blk(cur), blk(cur)], out_shape=[ov, ov],
        compiler_params=_cp("parallel", "arbitrary"), name=name,
    )(v2(kr), v2(kv), v2(qr), v2(qr), v2(dmm), v2(dmm), v2(lse), v2(lse), v2(dd), v2(dd))
    return dk.reshape(T, B_WIDTH), dv.reshape(T, B_WIDTH)


DILS_UNROLL = 2


def _dils_specs(gi, d, nblk):
    blk = lambda f: pl.BlockSpec((SPAN * d, HEAD_DIM), f)
    return {
        "q": blk(lambda h, n: (n, gi * B_HEADS + h)), "q_next": blk(lambda h, n: (jnp.minimum(n + 1, nblk - 1), gi * B_HEADS + h)),
        "cur": blk(lambda h, n: (n, h)), "prev": blk(lambda h, n: (jnp.maximum(n - 1, 0), h)),
        "next": blk(lambda h, n: (jnp.minimum(n + 1, nblk - 1), h)),
        "v": blk(lambda h, n: (n, B_HEADS + h)), "v_prev": blk(lambda h, n: (jnp.maximum(n - 1, 0), B_HEADS + h)),
    }


def _dils_fwd(qr, kr, kv, gi, d, *, name):
    T = qr.shape[0]
    nblk = T // (SPAN * d)
    sp = _dils_specs(gi, d, nblk)

    def body(q_ref, kc_ref, kp_ref, vc_ref, vp_ref, o_ref, lse_ref):
        cur_ok, prev_band = _band_masks()
        prev_ok = prev_band & (pl.program_id(1) > 0)

        def residue(r, carry):
            rows = pl.ds(r, SPAN, stride=d)
            q = q_ref[rows, :]
            sc = jnp.where(cur_ok, _dot_nt(q, kc_ref[rows, :]) * ATT_SCALE, NEG)
            sp_ = jnp.where(prev_ok, _dot_nt(q, kp_ref[rows, :]) * ATT_SCALE, NEG)
            m = jnp.maximum(jnp.max(sc, axis=-1, keepdims=True), jnp.max(sp_, axis=-1, keepdims=True))
            pc = jnp.exp(sc - m)
            pp = jnp.exp(sp_ - m)
            l = jnp.sum(pc, axis=-1, keepdims=True) + jnp.sum(pp, axis=-1, keepdims=True)
            o_ref[rows, :] = (_dot(pc, vc_ref[rows, :]) + _dot(pp, vp_ref[rows, :])) / l
            lse_ref[rows, :] = jnp.broadcast_to(m + jnp.log(l), (SPAN, HEAD_DIM))
            return carry

        lax.fori_loop(0, d, residue, 0, unroll=DILS_UNROLL)

    sh = jax.ShapeDtypeStruct((T, B_WIDTH), F32)
    return pl.pallas_call(
        body, grid=(B_HEADS, nblk), in_specs=[sp["q"], sp["cur"], sp["prev"], sp["v"], sp["v_prev"]],
        out_specs=[sp["cur"], sp["cur"]], out_shape=[sh, sh],
        compiler_params=_cp("parallel", "arbitrary"), name=name)(qr, kr, kr, kv, kv)


def _dils_bwd_dq(qr, kr, kv, dmix, lse, dd, gi, d, *, name):
    T = qr.shape[0]
    nblk = T // (SPAN * d)
    sp = _dils_specs(gi, d, nblk)

    def body(q_ref, kc_ref, kp_ref, vc_ref, vp_ref, dy_ref, lse_ref, dd_ref, dq_ref):
        cur_ok, prev_band = _band_masks()
        prev_ok = prev_band & (pl.program_id(1) > 0)

        def residue(r, carry):
            rows = pl.ds(r, SPAN, stride=d)
            q, dy = q_ref[rows, :], dy_ref[rows, :]
            kc, kp = kc_ref[rows, :], kp_ref[rows, :]
            lse_h = jnp.max(lse_ref[rows, :], axis=-1, keepdims=True)
            dd_h = jnp.max(dd_ref[rows, :], axis=-1, keepdims=True)
            pc = jnp.exp(jnp.where(cur_ok, _dot_nt(q, kc) * ATT_SCALE, NEG) - lse_h)
            pp = jnp.exp(jnp.where(prev_ok, _dot_nt(q, kp) * ATT_SCALE, NEG) - lse_h)
            dsc = pc * (_dot_nt(dy, vc_ref[rows, :]) - dd_h) * ATT_SCALE
            dsp = pp * (_dot_nt(dy, vp_ref[rows, :]) - dd_h) * ATT_SCALE
            dq_ref[rows, :] = _dot(dsc, kc) + _dot(dsp, kp)
            return carry

        lax.fori_loop(0, d, residue, 0, unroll=DILS_UNROLL)

    return pl.pallas_call(
        body, grid=(B_HEADS, nblk),
        in_specs=[sp["q"], sp["cur"], sp["prev"], sp["v"], sp["v_prev"], sp["cur"], sp["cur"], sp["cur"]],
        out_specs=sp["cur"], out_shape=jax.ShapeDtypeStruct((T, B_WIDTH), F32),
        compiler_params=_cp("parallel", "arbitrary"), name=name)(qr, kr, kr, kv, kv, dmix, lse, dd)


def _dils_bwd_dkv(qr, kr, kv, dmix, lse, dd, gi, d, *, name):
    T = qr.shape[0]
    nblk = T // (SPAN * d)
    sp = _dils_specs(gi, d, nblk)

    def body(k_ref, v_ref, q0_ref, q1_ref, dy0_ref, dy1_ref, lse0_ref, lse1_ref, dd0_ref, dd1_ref, dk_ref, dv_ref):
        cur_ok, prev_band = _band_masks()
        next_ok = prev_band & (pl.program_id(1) < nblk - 1)

        def residue(r, carry):
            rows = pl.ds(r, SPAN, stride=d)
            k, v = k_ref[rows, :], v_ref[rows, :]
            dk = jnp.zeros((SPAN, HEAD_DIM), F32)
            dv = jnp.zeros((SPAN, HEAD_DIM), F32)
            for ok, q_ref, dy_ref, lse_ref, dd_ref in ((cur_ok, q0_ref, dy0_ref, lse0_ref, dd0_ref),
                                                         (next_ok, q1_ref, dy1_ref, lse1_ref, dd1_ref)):
                q, dy = q_ref[rows, :], dy_ref[rows, :]
                lse_h = jnp.max(lse_ref[rows, :], axis=-1, keepdims=True)
                dd_h = jnp.max(dd_ref[rows, :], axis=-1, keepdims=True)
                p = jnp.exp(jnp.where(ok, _dot_nt(q, k) * ATT_SCALE, NEG) - lse_h)
                ds = p * (_dot_nt(dy, v) - dd_h) * ATT_SCALE
                dk = dk + _dot_tn(ds, q)
                dv = dv + _dot_tn(p, dy)
            dk_ref[rows, :] = dk
            dv_ref[rows, :] = dv
            return carry

        lax.fori_loop(0, d, residue, 0, unroll=DILS_UNROLL)

    sh = jax.ShapeDtypeStruct((T, B_WIDTH), F32)
    return pl.pallas_call(
        body, grid=(B_HEADS, nblk),
        in_specs=[sp["cur"], sp["v"], sp["q"], sp["q_next"], sp["cur"], sp["next"], sp["cur"], sp["next"], sp["cur"], sp["next"]],
        out_specs=[sp["cur"], sp["cur"]], out_shape=[sh, sh],
        compiler_params=_cp("parallel", "arbitrary"), name=name)(kr, kv, qr, qr, dmix, dmix, lse, lse, dd, dd)


A_MQ_COL = 4 * A_WIDTH // MEM_WIDTH
B_MQ_COL = N_GROUPS * B_WIDTH // MEM_WIDTH


def _row(v):
    return v.reshape(1, -1).astype(F32)


def _local_step(x, mem, tgt, get_w, P, put_g, first_dep=None):
    T = x.shape[0]
    cosf, sinsg = _rope_tables(T)
    lb_soft = jax.nn.softmax(P["a_lb_logits"].astype(F32), axis=0)
    lb = lb_soft[0:1]
    qw_heads = jnp.repeat(P["b_qnorm"][0], B_HEADS, axis=0).reshape(1, -1)
    kw_heads = jnp.tile(_row(P["b_knorm"]), (1, B_HEADS))
    mqw = [jnp.tile(_row(P["mem_qnorm"][l]), (1, MEM_HEADS)) for l in range(2)]
    mkw = [jnp.tile(_row(P["mem_knorm"][l]), (1, MEM_HEADS)) for l in range(2)]
    nmix = [_row(P["norm_mix"][l]) for l in range(2)]
    nffn = [_row(P["norm_ffn"][l]) for l in range(2)]
    mnorm = [_row(P["mem_norm"][l]) for l in range(2)]
    kvn = _row(P["kv_norm"])
    onorm = _row(P["a_onorm"])
    W = {}

    def w_of(name, after=None):
        if name not in W:
            W[name] = get_w(name, after)
        return W[name]

    proj_a, xn0 = _rms_matmul(x, nmix[0], w_of("a_w_in"), tt=512, tn=1664, wt=True, name="proj_a", dep=first_dep)
    mkv0, mn0 = _rms_matmul(mem, mnorm[0], w_of("w_mem_kv0"), tt=MEM_TOKENS, tn=2 * MEM_WIDTH, wt=False, name="mem_kv0")
    o_raw, st = _hgrn2_fwd(proj_a, lb, name="hgrn2_fwd")
    mm0 = _a_post_fwd(o_raw, proj_a, onorm, tt=512, name="a_post_fwd")
    mo0 = _mem_attn_fwd(proj_a, A_MQ_COL, mkv0, mqw[0], mkw[0], tt=512, name="mem_attn_fwd0")
    mix0 = jnp.concatenate([mm0, mo0], axis=1)
    hm0 = _mm_res(x, mix0, w_of("w_out0", mix0), tt=512, name="out_proj0")
    gu0, hn0 = _rms_matmul(hm0, nffn[0], w_of("w_gate_up0", hm0), tt=512, tn=1408, wt=True, name="gate_up0")
    h1 = _swiglu_down(hm0, gu0, w_of("w_down0", gu0), tt=256, name="down0")
    kv, hkn = _rms_matmul(h1, kvn, w_of("w_kv", h1), tt=512, tn=768, wt=True, name="kv_proj")
    kr = _headnorm_rope_fwd(kv, kw_heads, cosf, sinsg, col0=0, n_heads=B_HEADS, tt=512, name="k_prep")

    proj_b, xn1 = _rms_matmul(h1, nmix[1], w_of("b_w_in", kr), tt=512, tn=1280, wt=True, name="proj_b")
    mkv1, mn1 = _rms_matmul(mem, mnorm[1], w_of("w_mem_kv1", kr), tt=MEM_TOKENS, tn=2 * MEM_WIDTH, wt=False, name="mem_kv1")
    qr = _headnorm_rope_fwd(proj_b, qw_heads, cosf, sinsg, col0=0, n_heads=N_GROUPS * B_HEADS, tt=512, name="q_prep")
    outs = [(_dil_fwd if d == 1 else _dils_fwd)(qr, kr, kv, gi, d, name=f"dil_fwd{gi}") for gi, d in enumerate(DILATIONS)]
    mm1, lse_tot = _dil_combine_fwd([o for o, _ in outs], [s for _, s in outs], tt=512, name="dil_combine")
    mo1 = _mem_attn_fwd(proj_b, B_MQ_COL, mkv1, mqw[1], mkw[1], tt=512, name="mem_attn_fwd1")
    mix1 = jnp.concatenate([mm1, mo1], axis=1)
    hm1 = _mm_res(h1, mix1, w_of("w_out1", mix1), tt=512, name="out_proj1")
    gu1, hn1 = _rms_matmul(hm1, nffn[1], w_of("w_gate_up1", hm1), tt=512, tn=1408, wt=True, name="gate_up1")
    y = _swiglu_down(hm1, gu1, w_of("w_down1", gu1), tt=256, name="down1")
    dy, sq = _loss_kernel(y, tgt, tt=512, name="loss")

    gP = {}
    zeros_mem = jnp.zeros((MEM_TOKENS, D_MODEL), F32)

    def ffn_bwd(l, dh, hm, gu, hn):
        dgu, act = _swiglu_bwd(dh, gu, w_of(f"w_down{l}"), tt=256, name=f"swiglu_bwd{l}")
        g_wd = _mm_tn(act, dh, tt=512, tka=1408, name=f"g_w_down{l}")
        g_wgu = _mm_tn(dgu, hn, tt=512, tka=1408, name=f"g_w_gate_up{l}")
        sent = put_g({f"w_down{l}": g_wd, f"w_gate_up{l}": g_wgu})
        dhm, g_nf = _rms_bwd_dx(hm, nffn[l], w_of(f"w_gate_up{l}"), dgu, dh, tt=256, wt=True, name=f"gate_up_bwd{l}", dep=sent)
        return dhm, g_nf

    def mix_bwd(l, dhm, mix, proj, qcol, mkv, mn):
        dmix = _mm_nt(dhm, w_of(f"w_out{l}"), tt=512, name=f"out_proj_bwd{l}")
        g_wout = _mm_tn(mix, dhm, tt=512, tka=512, name=f"g_w_out{l}")
        dmq, dmkv, dqw, dkw = _mem_attn_bwd(proj, qcol, mkv, mqw[l], mkw[l], dmix, tt=512, name=f"mem_attn_bwd{l}")
        g_wmkv = _mm_tn(mn, dmkv, tt=MEM_TOKENS, tka=512, name=f"g_w_mem_kv{l}")
        sent = put_g({f"w_out{l}": g_wout, f"w_mem_kv{l}": g_wmkv})
        _, g_mn = _rms_bwd_dx(mem, mnorm[l], w_of(f"w_mem_kv{l}"), dmkv, zeros_mem, tt=MEM_TOKENS, wt=False, name=f"mem_kv_bwd{l}")
        fold = lambda v: v.reshape(MEM_HEADS, MEM_HEAD_DIM).sum(axis=0)
        return dmix, dmq, g_mn, fold(dqw), fold(dkw), sent

    dhm1, g_nf1 = ffn_bwd(1, dy, hm1, gu1, hn1)
    dmix1, dmq1, g_mn1, g_mq1, g_mk1, sent = mix_bwd(1, dhm1, mix1, proj_b, B_MQ_COL, mkv1, mn1)
    dmm, dd = _dil_bwd_prep(dmix1, mm1, tt=512, name="dil_bwd_prep", dep=sent)
    dqs, dks, dvs = [], [], []
    for gi, d in enumerate(DILATIONS):
        if d == 1:
            dqs.append(_dil_bwd_dq(qr, kr, kv, dmm, lse_tot, dd, gi, d, name=f"dil_bwd_dq{gi}"))
            dk_g, dv_g = _dil_bwd_dkv(qr, kr, kv, dmm, lse_tot, dd, gi, d, name=f"dil_bwd_dkv{gi}")
        else:
            dqs.append(_dils_bwd_dq(qr, kr, kv, dmix1, lse_tot, dd, gi, d, name=f"dil_bwd_dq{gi}"))
            dk_g, dv_g = _dils_bwd_dkv(qr, kr, kv, dmix1, lse_tot, dd, gi, d, name=f"dil_bwd_dkv{gi}")
        dks.append(dk_g)
        dvs.append(dv_g)
    dq_raw, dqw = _q_prep_bwd(proj_b, qw_heads, cosf, sinsg, dqs, tt=512, name="q_prep_bwd")
    dkv, dkw = _kv_prep_bwd(kv, kw_heads, cosf, sinsg, dks, dvs, tt=512, name="kv_prep_bwd")
    dproj_b = jnp.concatenate([dq_raw, dmq1], axis=1)
    g_wb = _mm_tn(dproj_b, xn1, tt=512, tka=1280, name="g_b_w_in")
    g_wkv = _mm_tn(dkv, hkn, tt=512, tka=768, name="g_w_kv")
    sent = put_g({"b_w_in": g_wb, "w_kv": g_wkv})
    dh1, g_nm1 = _rms_bwd_dx(h1, nmix[1], w_of("b_w_in"), dproj_b, dhm1, tt=256, wt=True, name="proj_b_bwd", dep=sent)
    dh1, g_kvn = _rms_bwd_dx(h1, kvn, w_of("w_kv"), dkv, dh1, tt=256, wt=True, name="kv_proj_bwd")

    dhm0, g_nf0 = ffn_bwd(0, dh1, hm0, gu0, hn0)
    dmix0, dmq0, g_mn0, g_mq0, g_mk0, sent = mix_bwd(0, dhm0, mix0, proj_a, A_MQ_COL, mkv0, mn0)
    do_raw, dg, g_onorm = _a_post_bwd(o_raw, proj_a, onorm, dmix0, tt=512, name="a_post_bwd", dep=sent)
    dq, dz, dv, dlb = _hgrn2_bwd(proj_a, lb, st, do_raw, name="hgrn2_bwd")
    dproj_a = jnp.concatenate([dq, dz, dv, dg, dmq0], axis=1)
    sent = put_g({"a_w_in": _mm_tn(dproj_a, xn0, tt=512, tka=1664, name="g_a_w_in")})
    gx, g_nm0 = _rms_bwd_dx(x, nmix[0], w_of("a_w_in"), dproj_a, dhm0, tt=256, wt=True, name="proj_a_bwd", dep=sent)

    dl0 = lb_soft[0:1] * lb_soft[1:2] * dlb
    gP["a_lb_logits"] = jnp.concatenate([dl0, -dl0], axis=0)
    gP["a_onorm"] = g_onorm
    gP["norm_mix"] = jnp.concatenate([g_nm0, g_nm1], axis=0)
    gP["norm_ffn"] = jnp.concatenate([g_nf0, g_nf1], axis=0)
    gP["b_qnorm"] = dqw.reshape(N_GROUPS, B_HEADS, HEAD_DIM).sum(axis=1)[None]
    gP["kv_norm"] = g_kvn.reshape(-1)
    gP["b_knorm"] = dkw.reshape(B_HEADS, HEAD_DIM).sum(axis=0)
    gP["mem_norm"] = jnp.concatenate([g_mn0, g_mn1], axis=0)
    gP["mem_qnorm"] = jnp.stack([g_mq0, g_mq1])
    gP["mem_knorm"] = jnp.stack([g_mk0, g_mk1])
    return sq, gx, gP


MESH_ID = pl.DeviceIdType.MESH
HBM_SPEC = pl.BlockSpec(memory_space=pltpu.HBM)


def _position():
    return lax.axis_index("x"), lax.axis_index("y"), lax.axis_index("c")


def _all_gather(blocks, *, name):
    n = len(blocks)

    def body(*refs):
        x_refs, out_refs = refs[:n], refs[n:2 * n]
        send_sems, recv_sems, local_sems = refs[2 * n:]
        x, y, c = _position()
        me, sibling = (x, y, c), (x, y, 1 - c)
        chips = [(1 - x, y), (x, 1 - y), (1 - x, 1 - y)]

        def slot(a, px, py, pc):
            return out_refs[a].at[4 * px + 2 * py + pc]

        def copy(a, k, blk, to, src=None):
            return pltpu.make_async_remote_copy(
                src_ref=slot(a, *blk) if src is None else src, dst_ref=slot(a, *blk),
                send_sem=send_sems.at[7 * a + k], recv_sem=recv_sems.at[7 * a + k], device_id=to, device_id_type=MESH_ID)

        mine = [pltpu.make_async_copy(x_refs[a], slot(a, *me), local_sems.at[a]) for a in range(n)]
        for cp in mine:
            cp.start()
        first = []
        for a in range(n):
            first.append(copy(a, 0, me, sibling, src=x_refs[a]))
            first += [copy(a, 1 + j, me, (*chip, c), src=x_refs[a]) for j, chip in enumerate(chips)]
        for cp in first:
            cp.start()
        passed = []
        for j, chip in enumerate(chips):
            for a in range(n):
                copy(a, 1 + j, (*chip, c), me).wait_recv()
                cp = copy(a, 4 + j, (*chip, c), sibling)
                cp.start()
                passed.append(cp)
        for a in range(n):
            copy(a, 0, sibling, me).wait_recv()
            for j, chip in enumerate(chips):
                copy(a, 4 + j, (*chip, 1 - c), me).wait_recv()
        for cp in first + passed:
            cp.wait_send()
        for cp in mine:
            cp.wait()

    return pl.pallas_call(
        body, out_shape=[jax.ShapeDtypeStruct((N_DEV,) + b.shape, b.dtype) for b in blocks],
        in_specs=[HBM_SPEC] * n, out_specs=[HBM_SPEC] * n,
        scratch_shapes=[pltpu.SemaphoreType.DMA((7 * n,)), pltpu.SemaphoreType.DMA((7 * n,)), pltpu.SemaphoreType.DMA((n,))],
        name=name)(*blocks)


SEM_SPEC = pl.BlockSpec(memory_space=pltpu.SEMAPHORE)
ANY_SPEC = pl.BlockSpec(memory_space=pl.ANY)
DATAFLOW = pltpu.SideEffectType.DATAFLOW_SIDE_EFFECTING


def _peer(k, x, y, c):
    return (1 - x if (k >> 2) & 1 else x, 1 - y if (k >> 1) & 1 else y, 1 - c if k & 1 else c)


def _own_slot_filled(own_block):
    x, y, c = _position()
    zone = lax.empty((N_DEV,) + own_block.shape, own_block.dtype)
    return lax.dynamic_update_slice_in_dim(zone, own_block[None], 4 * x + 2 * y + c, axis=0)


def _split_start(srcs, scatter, after, *, name):
    n = len(srcs)
    extra = [] if after is None else [after]
    x, y, c = _position()
    me = 4 * x + 2 * y + c
    lands = [_own_slot_filled(lax.dynamic_index_in_dim(s, me, 0, keepdims=False) if scatter else s) for s in srcs]

    def body(*refs):
        src_refs, land_refs = refs[:n], refs[n:2 * n]
        send_sems, recv_sems = refs[2 * n + len(extra)], refs[2 * n + len(extra) + 1]
        token = refs[-1]
        bx, by, bc = _position()
        bme = 4 * bx + 2 * by + bc
        for a in range(n):
            for k in range(1, N_DEV):
                tx, ty, tc = _peer(k, bx, by, bc)
                src = src_refs[a].at[4 * tx + 2 * ty + tc] if scatter else src_refs[a]
                pltpu.make_async_remote_copy(
                    src_ref=src, dst_ref=land_refs[a].at[bme],
                    send_sem=send_sems.at[7 * a + k - 1], recv_sem=recv_sems.at[7 * a + k - 1],
                    device_id=(tx, ty, tc), device_id_type=MESH_ID).start()
        token[...] = jnp.zeros_like(token)

    hbm = lambda a: pltpu.HBM(a.shape, a.dtype)
    outs = pl.pallas_call(
        body, name=name,
        out_shape=(pltpu.SemaphoreType.DMA((7 * n,)), pltpu.SemaphoreType.DMA((7 * n,)),
                   *[hbm(s) for s in srcs], *[hbm(l) for l in lands], jax.ShapeDtypeStruct((8, 128), F32)),
        in_specs=[HBM_SPEC] * (2 * n) + [ANY_SPEC] * len(extra),
        out_specs=(SEM_SPEC, SEM_SPEC, *[HBM_SPEC] * (2 * n), pl.BlockSpec(memory_space=pltpu.VMEM)),
        input_output_aliases={i: 2 + i for i in range(2 * n)},
        compiler_params=pltpu.CompilerParams(has_side_effects=DATAFLOW),
    )(*[pltpu.with_memory_space_constraint(s, pltpu.HBM) for s in srcs],
      *[pltpu.with_memory_space_constraint(l, pltpu.HBM) for l in lands], *extra)
    return {"n": n, "scatter": scatter, "send": outs[0], "recv": outs[1], "srcs": outs[2:2 + n],
            "lands": outs[2 + n:2 + 2 * n], "token": outs[-1]}


def _split_wait(handle, after, *, name):
    n, scatter = handle["n"], handle["scatter"]

    def body(*refs):
        src_refs, land_refs = refs[:n], refs[n:2 * n]
        send_sems, recv_sems = refs[2 * n], refs[2 * n + 1]
        bx, by, bc = _position()
        for a in range(n):
            for k in range(1, N_DEV):
                src = src_refs[a].at[0] if scatter else src_refs[a]
                cp = pltpu.make_async_remote_copy(
                    src_ref=src, dst_ref=land_refs[a].at[0],
                    send_sem=send_sems.at[7 * a + k - 1], recv_sem=recv_sems.at[7 * a + k - 1],
                    device_id=_peer(k, bx, by, bc), device_id_type=MESH_ID)
                cp.wait_send()
                cp.wait_recv()

    hbm = lambda a: pltpu.HBM(a.shape, a.dtype)
    outs = pl.pallas_call(
        body, name=name,
        out_shape=(*[hbm(s) for s in handle["srcs"]], *[hbm(l) for l in handle["lands"]]),
        in_specs=[HBM_SPEC] * (2 * n) + [SEM_SPEC, SEM_SPEC, ANY_SPEC],
        out_specs=tuple([HBM_SPEC] * (2 * n)),
        input_output_aliases={i: i for i in range(2 * n)},
        compiler_params=pltpu.CompilerParams(has_side_effects=DATAFLOW),
    )(*handle["srcs"], *handle["lands"], handle["send"], handle["recv"], after)
    return list(outs[n:])


def _sum_sources(parts, *, tr, name):
    n, R, C = parts.shape

    def body(p_ref, o_ref):
        acc = p_ref[0].astype(F32)
        for s in range(1, n):
            acc = acc + p_ref[s].astype(F32)
        o_ref[...] = acc

    return pl.pallas_call(
        body, grid=(R // tr,), in_specs=[pl.BlockSpec((n, tr, C), lambda i: (0, i, 0))],
        out_specs=pl.BlockSpec((tr, C), lambda i: (i, 0)),
        out_shape=jax.ShapeDtypeStruct((R, C), F32), compiler_params=_cp("parallel"), name=name)(parts)


def _adamw_math(g, w, m, v):
    c1 = 1.0 - ADAM_B1 ** ADAM_STEP
    c2 = 1.0 - ADAM_B2 ** ADAM_STEP
    nm = ADAM_B1 * m + (1.0 - ADAM_B1) * g
    nv = ADAM_B2 * v + (1.0 - ADAM_B2) * (g * g)
    return -ADAM_LR * ((nm / c1) / (jnp.sqrt(nv / c2) + ADAM_EPS) + ADAM_WD * w), nm, nv


def _reduce_adamw(received, w, m, v, *, col, tr, name):
    L, R, C = w.shape

    def body(*refs):
        p_refs = refs[:L]
        w_ref, m_ref, v_ref, g_ref, d_ref, nm_ref, nv_ref = refs[L:]
        for l in range(L):
            @pl.when(pl.program_id(0) == l)
            def _(p_ref=p_refs[l]):
                acc = p_ref[0].astype(F32)
                for s in range(1, N_DEV):
                    acc = acc + p_ref[s].astype(F32)
                g = acc.T if col else acc
                g_ref[...] = g
                d_ref[...], nm_ref[...], nv_ref[...] = _adamw_math(g, w_ref[...], m_ref[...], v_ref[...])

    p_spec = (pl.BlockSpec((N_DEV, C, tr), lambda l, i: (0, 0, i)) if col
              else pl.BlockSpec((N_DEV, tr, C), lambda l, i: (0, i, 0)))
    blk = pl.BlockSpec((None, tr, C), lambda l, i: (l, i, 0))
    sh = jax.ShapeDtypeStruct((L, R, C), F32)
    return pl.pallas_call(
        body, grid=(L, R // tr), in_specs=[p_spec] * L + [blk] * 3, out_specs=[blk] * 4, out_shape=[sh] * 4,
        compiler_params=_cp("parallel", "parallel"), name=name)(*received, w, m, v)


def _adamw(g, w, m, v, *, tr, name):
    L, R, C = w.shape

    def body(g_ref, w_ref, m_ref, v_ref, d_ref, nm_ref, nv_ref):
        d_ref[...], nm_ref[...], nv_ref[...] = _adamw_math(g_ref[...], w_ref[...], m_ref[...], v_ref[...])

    blk = pl.BlockSpec((None, tr, C), lambda l, i: (l, i, 0))
    sh = jax.ShapeDtypeStruct((L, R, C), F32)
    return pl.pallas_call(
        body, grid=(L, R // tr), in_specs=[blk] * 4, out_specs=[blk] * 3, out_shape=[sh] * 3,
        compiler_params=_cp("parallel", "parallel"), name=name)(g, w, m, v)


UNITS = {
    "a_w_in": ("a_w_in", 0, True), "w_mem_kv0": ("w_mem_kv", 0, False), "w_out0": ("w_out", 0, False),
    "w_gate_up0": ("w_gate_up", 0, True), "w_down0": ("w_down", 0, False), "w_kv": ("w_kv", None, True),
    "b_w_in": ("b_w_in", 0, True), "w_mem_kv1": ("w_mem_kv", 1, False), "w_out1": ("w_out", 1, False),
    "w_gate_up1": ("w_gate_up", 1, True), "w_down1": ("w_down", 1, False),
}
BIG = ("a_w_in", "b_w_in", "w_kv", "w_mem_kv", "w_out", "w_gate_up", "w_down")
ADAMW_ROW_TILE = {"a_w_in": 256, "b_w_in": 256, "w_kv": 256, "w_mem_kv": 128, "w_out": 128, "w_gate_up": 176, "w_down": 176}
TRANSPOSED_UPDATE = ("w_gate_up",)


def _wire_block(weights, unit):
    name, layer, col = UNITS[unit]
    a = weights[name] if layer is None else weights[name][layer]
    return (a.T if col else a).astype(BF16)


SMALL_REPLICATED = ("norm_mix", "norm_ffn", "b_qnorm", "kv_norm", "b_knorm", "mem_norm", "mem_qnorm", "mem_knorm")
SMALL_SHARDED = ("a_lb_logits", "a_onorm")
SMALL_ORDER = SMALL_REPLICATED + SMALL_SHARDED
LANES = 128


def _prod(shape):
    n = 1
    for s in shape:
        n *= s
    return n


def _pack_flat(arrays, rows, cols, dtype):
    flat = jnp.concatenate([a.reshape(-1).astype(dtype) for a in arrays])
    return jnp.pad(flat, (0, rows * cols - flat.shape[0])).reshape(rows, cols)


def _unpack_flat(packed, shapes):
    flat = packed.reshape(-1)
    out, off = [], 0
    for s in shapes:
        out.append(flat[off:off + _prod(s)].reshape(s))
        off += _prod(s)
    return out


def kernel(x, mem, norm_mix, norm_ffn, a_w_in, a_lb_logits, a_onorm, b_w_in, b_qnorm, kv_norm, w_kv, b_knorm, mem_norm, w_mem_kv, mem_qnorm, mem_knorm, w_out, w_gate_up, w_down, loss_target, m_norm_mix, m_norm_ffn, m_a_w_in, m_a_lb_logits, m_a_onorm, m_b_w_in, m_b_qnorm, m_kv_norm, m_w_kv, m_b_knorm, m_mem_norm, m_w_mem_kv, m_mem_qnorm, m_mem_knorm, m_w_out, m_w_gate_up, m_w_down, v_norm_mix, v_norm_ffn, v_a_w_in, v_a_lb_logits, v_a_onorm, v_b_w_in, v_b_qnorm, v_kv_norm, v_w_kv, v_b_knorm, v_mem_norm, v_w_mem_kv, v_mem_qnorm, v_mem_knorm, v_w_out, v_w_gate_up, v_w_down):
    names = ("norm_mix", "norm_ffn", "a_w_in", "a_lb_logits", "a_onorm", "b_w_in", "b_qnorm", "kv_norm", "w_kv", "b_knorm",
             "mem_norm", "w_mem_kv", "mem_qnorm", "mem_knorm", "w_out", "w_gate_up", "w_down")
    w = dict(zip(names, (norm_mix, norm_ffn, a_w_in, a_lb_logits, a_onorm, b_w_in, b_qnorm, kv_norm, w_kv, b_knorm,
                         mem_norm, w_mem_kv, mem_qnorm, mem_knorm, w_out, w_gate_up, w_down)))
    m = dict(zip(names, (m_norm_mix, m_norm_ffn, m_a_w_in, m_a_lb_logits, m_a_onorm, m_b_w_in, m_b_qnorm, m_kv_norm, m_w_kv,
                         m_b_knorm, m_mem_norm, m_w_mem_kv, m_mem_qnorm, m_mem_knorm, m_w_out, m_w_gate_up, m_w_down)))
    v = dict(zip(names, (v_norm_mix, v_norm_ffn, v_a_w_in, v_a_lb_logits, v_a_onorm, v_b_w_in, v_b_qnorm, v_kv_norm, v_w_kv,
                         v_b_knorm, v_mem_norm, v_w_mem_kv, v_mem_qnorm, v_mem_knorm, v_w_out, v_w_gate_up, v_w_down)))

    first = ["a_w_in", "w_mem_kv0"]
    gathered = _all_gather([_wire_block(w, u) for u in first] + [_pack_flat([a_lb_logits, a_onorm], 8, LANES, F32)],
                           name="gather_first")
    full = {u: g.reshape(-1, g.shape[-1]) for u, g in zip(first, gathered)}
    small_in = gathered[-1].reshape(N_DEV, -1)
    P = {n: w[n] for n in SMALL_REPLICATED}
    P["a_lb_logits"] = small_in[:, :192].reshape(N_DEV, 2, 96).transpose(1, 0, 2).reshape(2, A_WIDTH)
    P["a_onorm"] = small_in[:, 192:288].reshape(1, A_WIDTH)
    later = [["w_out0", "w_gate_up0"], ["w_down0", "w_kv"], ["b_w_in", "w_mem_kv1"], ["w_out1", "w_gate_up1", "w_down1"]]
    pending = {}
    token = gathered[-1]
    for i, group in enumerate(later):
        handle = _split_start([_wire_block(w, u) for u in group], False, token, name=f"gather{i}_start")
        token = handle["token"]
        for u in group:
            pending[u] = (i, group, handle)

    def get_w(unit, after):
        if unit not in full:
            i, group, handle = pending[unit]
            for u, land in zip(group, _split_wait(handle, after, name=f"gather{i}_wait")):
                full[u] = land.reshape(-1, land.shape[-1])
        return full[unit]

    sent = []

    def put_g(group):
        units = list(group)
        handle = _split_start([group[u].reshape(N_DEV, -1, group[u].shape[-1]) for u in units], True, None,
                              name=f"scatter{len(sent)}_start")
        sent.append((units, handle))
        return handle["token"]

    sq, gx, gP = _local_step(x[0], mem[0], loss_target[0], get_w, P, put_g, first_dep=token)
    loss = lax.psum(0.5 * jnp.sum(sq) / D_MODEL, ("x", "y", "c"))

    received = {}
    for i, (units, handle) in enumerate(sent):
        received.update(zip(units, _split_wait(handle, gx, name=f"scatter{i}_wait")))
    out = {"grad": {}, "delta": {}, "new_m": {}, "new_v": {}}
    for n in BIG:
        shape = w[n].shape
        as3 = lambda a: a.reshape((-1,) + shape[-2:])
        mine = [u for u, (wn, _, _) in UNITS.items() if wn == n]
        col = UNITS[mine[0]][2]
        flip = (lambda a: jnp.swapaxes(a, 1, 2)) if n in TRANSPOSED_UPDATE else (lambda a: a)
        res = _reduce_adamw([received[u] for u in mine], flip(as3(w[n])), flip(as3(m[n])), flip(as3(v[n])),
                            col=col and n not in TRANSPOSED_UPDATE, tr=ADAMW_ROW_TILE[n], name=f"adamw_{n}")
        for kind, r in zip(("grad", "delta", "new_m", "new_v"), res):
            out[kind][n] = flip(r).reshape(shape)

    full_shapes = [(2, A_WIDTH) if n == "a_lb_logits" else (1, A_WIDTH) if n == "a_onorm" else w[n].shape for n in SMALL_ORDER]
    n_small = sum(_prod(s) for s in full_shapes)
    rows_small = -(-n_small // (8 * LANES)) * 8
    g_all, = _all_gather([_pack_flat([gP[n] for n in SMALL_ORDER], rows_small, LANES, F32)], name="gather_small_grads")
    g_small = dict(zip(SMALL_ORDER, _unpack_flat(_sum_sources(g_all, tr=rows_small, name="sum_small_grads"), full_shapes)))
    me = 4 * lax.axis_index("x") + 2 * lax.axis_index("y") + lax.axis_index("c")
    for n in SMALL_SHARDED:
        g_small[n] = lax.dynamic_slice_in_dim(g_small[n], me * 96, 96, axis=1)
    shapes = [w[n].shape for n in SMALL_ORDER]
    rows_upd = -(-sum(_prod(s) for s in shapes) // (8 * LANES)) * 8
    pk = lambda d: _pack_flat([d[n] for n in SMALL_ORDER], rows_upd, LANES, F32)
    res = _adamw(pk(g_small)[None], pk(w)[None], pk(m)[None], pk(v)[None], tr=rows_upd, name="adamw_small")
    out["grad"].update(g_small)
    for kind, packed in zip(("delta", "new_m", "new_v"), res):
        out[kind].update(zip(SMALL_ORDER, _unpack_flat(packed[0], shapes)))

    return (loss, gx[None], *[out["grad"][n] for n in names], *[out["delta"][n] for n in names],
            *[out["new_m"][n] for n in names], *[out["new_v"][n] for n in names])
```

```python
import functools

import jax
import jax.numpy as jnp
from jax import lax
from jax.experimental import pallas as pl
from jax.experimental.pallas import tpu as pltpu

F32 = jnp.float32
BF16 = jnp.bfloat16

N_DEV = 8
D_MODEL = 1024
HEAD_DIM = 128
A_HEADS = 6
A_WIDTH = A_HEADS * HEAD_DIM
CHUNK = 64
B_HEADS = 6
B_WIDTH = B_HEADS * HEAD_DIM
DILATIONS = (1, 4, 16)
SPAN = 128
N_GROUPS = 3
ROPE_THETA = 10000.0
MEM_TOKENS = 256
MEM_HEADS = 4
MEM_HEAD_DIM = 64
MEM_WIDTH = MEM_HEADS * MEM_HEAD_DIM
FFN_HIDDEN = 2816
EPS = 1e-6

ADAM_LR = 0.001
ADAM_B1 = 0.9
ADAM_B2 = 0.999
ADAM_EPS = 1e-08
ADAM_WD = 0.01
ADAM_STEP = 10

V7X_VMEM_LIMIT_BYTES = 56 * 1024 * 1024

NT_DIMS = (((1,), (1,)), ((), ()))
TN_DIMS = (((0,), (0,)), ((), ()))


def _cp(*sem):
    return pltpu.CompilerParams(dimension_semantics=sem, vmem_limit_bytes=V7X_VMEM_LIMIT_BYTES)


def _dot(a, b):
    return jnp.dot(a.astype(BF16), b.astype(BF16), preferred_element_type=F32)


def _dot_nt(a, b):
    return lax.dot_general(a.astype(BF16), b.astype(BF16), NT_DIMS, preferred_element_type=F32)


def _dot_tn(a, b):
    return lax.dot_general(a.astype(BF16), b.astype(BF16), TN_DIMS, preferred_element_type=F32)


def _dot3(m01, x):
    hi = x.astype(BF16)
    r1 = x - hi.astype(F32)
    mid = r1.astype(BF16)
    lo = (r1 - mid.astype(F32)).astype(BF16)
    d = functools.partial(jnp.dot, preferred_element_type=F32)
    return d(m01, hi) + d(m01, mid) + d(m01, lo)


def _sigmoid(x):
    return 1.0 / (1.0 + jnp.exp(-x))


def _full(shape):
    return pl.BlockSpec(shape, lambda *_: (0,) * len(shape))


def _dep(body, n_in, dep):
    if dep is None:
        return body, [], []

    def with_dep(*refs):
        return body(*refs[:n_in], *refs[n_in + 1:])

    return with_dep, [pl.BlockSpec(memory_space=pl.ANY)], [dep]


def _rms_matmul(x, g, w, *, tt, tn, wt, name, dep=None):
    T, K = x.shape
    N = w.shape[0] if wt else w.shape[1]

    def kernel_body(x_ref, g_ref, w_ref, y_ref, xn_ref):
        @pl.when(pl.program_id(1) == 0)
        def _():
            xf = x_ref[...]
            r = lax.rsqrt(jnp.mean(xf * xf, axis=-1, keepdims=True) + EPS)
            xn_ref[...] = (xf * r * g_ref[...]).astype(BF16)

        y_ref[...] = (_dot_nt if wt else _dot)(xn_ref[...], w_ref[...])

    w_spec = pl.BlockSpec((tn, K), lambda i, j: (j, 0)) if wt else pl.BlockSpec((K, tn), lambda i, j: (0, j))
    body, dep_specs, dep_args = _dep(kernel_body, 3, dep)
    return pl.pallas_call(
        body, grid=(T // tt, N // tn),
        in_specs=[pl.BlockSpec((tt, K), lambda i, j: (i, 0)), _full((1, K)), w_spec] + dep_specs,
        out_specs=[pl.BlockSpec((tt, tn), lambda i, j: (i, j)), pl.BlockSpec((tt, K), lambda i, j: (i, 0))],
        out_shape=[jax.ShapeDtypeStruct((T, N), F32), jax.ShapeDtypeStruct((T, K), BF16)],
        compiler_params=_cp("parallel", "arbitrary"), name=name)(x, g, w, *dep_args)


def _mm_res(res, a, w, *, tt, name):
    T, K = a.shape
    N = w.shape[1]

    def body(r_ref, a_ref, w_ref, o_ref):
        o_ref[...] = r_ref[...] + _dot(a_ref[...], w_ref[...])

    return pl.pallas_call(
        body, grid=(T // tt,),
        in_specs=[pl.BlockSpec((tt, N), lambda i: (i, 0)), pl.BlockSpec((tt, K), lambda i: (i, 0)), _full((K, N))],
        out_specs=pl.BlockSpec((tt, N), lambda i: (i, 0)),
        out_shape=jax.ShapeDtypeStruct((T, N), F32),
        compiler_params=_cp("parallel"), name=name)(res, a, w)


def _swiglu_down(h, gu, wd, *, tt, name):
    T, D = h.shape
    Fh = wd.shape[0]

    def body(h_ref, gt_ref, up_ref, w_ref, o_ref):
        gt = gt_ref[...]
        act = gt * _sigmoid(gt) * up_ref[...]
        o_ref[...] = h_ref[...] + _dot(act, w_ref[...])

    return pl.pallas_call(
        body, grid=(T // tt,),
        in_specs=[pl.BlockSpec((tt, D), lambda i: (i, 0)), pl.BlockSpec((tt, Fh), lambda i: (i, 0)),
                  pl.BlockSpec((tt, Fh), lambda i: (i, 1)), _full((Fh, D))],
        out_specs=pl.BlockSpec((tt, D), lambda i: (i, 0)),
        out_shape=jax.ShapeDtypeStruct((T, D), F32),
        compiler_params=_cp("parallel"), name=name)(h, gu, gu, wd)


def _swiglu_bwd(dh, gu, wd, *, tt, name):
    T, D = dh.shape
    Fh = wd.shape[0]

    def body(dh_ref, gt_ref, up_ref, w_ref, dgu_ref, act_ref):
        gt = gt_ref[...]
        up = up_ref[...]
        s = _sigmoid(gt)
        silu = gt * s
        dact = _dot_nt(dh_ref[...], w_ref[...])
        act_ref[...] = (silu * up).astype(BF16)
        dgu_ref[:, :Fh] = (dact * up * (s * (1.0 + gt * (1.0 - s)))).astype(BF16)
        dgu_ref[:, Fh:] = (dact * silu).astype(BF16)

    return pl.pallas_call(
        body, grid=(T // tt,),
        in_specs=[pl.BlockSpec((tt, D), lambda i: (i, 0)), pl.BlockSpec((tt, Fh), lambda i: (i, 0)),
                  pl.BlockSpec((tt, Fh), lambda i: (i, 1)), _full((Fh, D))],
        out_specs=[pl.BlockSpec((tt, 2 * Fh), lambda i: (i, 0)), pl.BlockSpec((tt, Fh), lambda i: (i, 0))],
        out_shape=[jax.ShapeDtypeStruct((T, 2 * Fh), BF16), jax.ShapeDtypeStruct((T, Fh), BF16)],
        compiler_params=_cp("parallel"), name=name)(dh, gu, gu, wd)


def _mm_nt(a, w, *, tt, name):
    T, N = a.shape
    K = w.shape[0]

    def body(a_ref, w_ref, o_ref):
        o_ref[...] = _dot_nt(a_ref[...], w_ref[...])

    return pl.pallas_call(
        body, grid=(T // tt,),
        in_specs=[pl.BlockSpec((tt, N), lambda i: (i, 0)), _full((K, N))],
        out_specs=pl.BlockSpec((tt, K), lambda i: (i, 0)),
        out_shape=jax.ShapeDtypeStruct((T, K), F32),
        compiler_params=_cp("parallel"), name=name)(a, w)


def _mm_tn(a, b, *, tt, tka, name):
    T, Ka = a.shape
    N = b.shape[1]
    last = T // tt - 1

    def body(a_ref, b_ref, o_ref, acc):
        @pl.when(pl.program_id(1) == 0)
        def _():
            acc[...] = jnp.zeros_like(acc)

        acc[...] += _dot_tn(a_ref[...], b_ref[...])

        @pl.when(pl.program_id(1) == last)
        def _():
            o_ref[...] = acc[...].astype(BF16)

    return pl.pallas_call(
        body, grid=(Ka // tka, T // tt),
        in_specs=[pl.BlockSpec((tt, tka), lambda j, t: (t, j)), pl.BlockSpec((tt, N), lambda j, t: (t, 0))],
        out_specs=pl.BlockSpec((tka, N), lambda j, t: (j, 0)),
        out_shape=jax.ShapeDtypeStruct((Ka, N), BF16),
        scratch_shapes=[pltpu.VMEM((tka, N), F32)],
        compiler_params=_cp("parallel", "arbitrary"), name=name)(a, b)


def _rms_bwd_dx(x, g, w, dy, dres, *, tt, wt, name, dep=None):
    T, K = x.shape
    N = w.shape[0] if wt else w.shape[1]

    def kernel_body(x_ref, g_ref, w_ref, dy_ref, dres_ref, dx_ref, dg_ref):
        @pl.when(pl.program_id(0) == 0)
        def _():
            dg_ref[...] = jnp.zeros_like(dg_ref)

        dxn = (_dot if wt else _dot_nt)(dy_ref[...], w_ref[...])
        xf = x_ref[...]
        r = lax.rsqrt(jnp.mean(xf * xf, axis=-1, keepdims=True) + EPS)
        xhat = xf * r
        dg_ref[...] += jnp.sum(dxn * xhat, axis=0, keepdims=True)
        dxhat = dxn * g_ref[...]
        dx_ref[...] = dres_ref[...] + r * (dxhat - xhat * jnp.mean(dxhat * xhat, axis=-1, keepdims=True))

    body, dep_specs, dep_args = _dep(kernel_body, 5, dep)
    return pl.pallas_call(
        body, grid=(T // tt,),
        in_specs=[pl.BlockSpec((tt, K), lambda i: (i, 0)), _full((1, K)), _full(w.shape),
                  pl.BlockSpec((tt, N), lambda i: (i, 0)), pl.BlockSpec((tt, K), lambda i: (i, 0))] + dep_specs,
        out_specs=[pl.BlockSpec((tt, K), lambda i: (i, 0)), _full((1, K))],
        out_shape=[jax.ShapeDtypeStruct((T, K), F32), jax.ShapeDtypeStruct((1, K), F32)],
        compiler_params=_cp("arbitrary"), name=name)(x, g, w, dy, dres, *dep_args)


def _loss_kernel(y, tgt, *, tt, name):
    T, D = y.shape

    def body(y_ref, t_ref, dy_ref, acc_ref):
        @pl.when(pl.program_id(0) == 0)
        def _():
            acc_ref[...] = jnp.zeros_like(acc_ref)

        e = y_ref[...] - t_ref[...]
        dy_ref[...] = e * (1.0 / D)
        acc_ref[...] += jnp.sum(e * e, axis=0, keepdims=True)

    return pl.pallas_call(
        body, grid=(T // tt,),
        in_specs=[pl.BlockSpec((tt, D), lambda i: (i, 0)), pl.BlockSpec((tt, D), lambda i: (i, 0))],
        out_specs=[pl.BlockSpec((tt, D), lambda i: (i, 0)), _full((1, D))],
        out_shape=[jax.ShapeDtypeStruct((T, D), F32), jax.ShapeDtypeStruct((1, D), F32)],
        compiler_params=_cp("arbitrary"), name=name)(y, tgt)


HGRN_TB = 512
HGRN_NCH = HGRN_TB // CHUNK
HGRN_HPB = 6


def _hgrn_chunk_fwd(q, z, lbv, tril01):
    sig = _sigmoid(z)
    f = lbv + (1.0 - lbv) * sig
    kk = 1.0 - f
    b = _dot3(tril01, jnp.log(f))
    bend = b[CHUNK - 1:CHUNK, :]
    sq = _sigmoid(q)
    eb = jnp.exp(b)
    emb = jnp.exp(-b)
    eo = jnp.exp(bend - b)
    dec = jnp.exp(bend)
    return sig, f, kk, sq, eb, emb, eo, dec


def _hgrn2_fwd(proj, lb, *, name):
    T = proj.shape[0]
    nT = T // HGRN_TB
    nC = T // CHUNK

    def body(q_ref, z_ref, v_ref, lb_ref, o_ref, st_ref, state):
        @pl.when(pl.program_id(1) == 0)
        def _():
            state[...] = jnp.zeros_like(state)

        row = lax.broadcasted_iota(jnp.int32, (CHUNK, CHUNK), 0)
        col = lax.broadcasted_iota(jnp.int32, (CHUNK, CHUNK), 1)
        causal = row >= col
        tril01 = causal.astype(BF16)

        def chunk(c, carry):
            rows = pl.ds(pl.multiple_of(c * CHUNK, CHUNK), CHUNK)
            for hh in range(HGRN_HPB):
                sl = slice(hh * HEAD_DIM, (hh + 1) * HEAD_DIM)
                q = q_ref[rows, sl]
                v = v_ref[rows, sl].astype(BF16)
                sig, f, kk, sq, eb, emb, eo, dec = _hgrn_chunk_fwd(q, z_ref[rows, sl], lb_ref[:, sl], tril01)
                qi = (q * sq * eb).astype(BF16)
                ki = (kk * emb).astype(BF16)
                ko = (kk * eo).astype(BF16)
                st = state[hh]
                att = jnp.where(causal, _dot_nt(qi, ki), 0.0)
                o_ref[rows, sl] = _dot(att, v) + _dot_nt(qi, st)
                st_ref[c, hh] = st
                state[hh] = st * dec + _dot_tn(v, ko)
            return carry

        lax.fori_loop(0, HGRN_NCH, chunk, 0)

    W = HGRN_HPB * HEAD_DIM
    nG = A_HEADS // HGRN_HPB
    hb = lambda off: pl.BlockSpec((HGRN_TB, W), lambda h, i: (i, off + h))
    return pl.pallas_call(
        body, grid=(nG, nT),
        in_specs=[hb(0), hb(nG), hb(2 * nG), pl.BlockSpec((1, W), lambda h, i: (0, h))],
        out_specs=[hb(0), pl.BlockSpec((HGRN_NCH, HGRN_HPB, HEAD_DIM, HEAD_DIM), lambda h, i: (i, h, 0, 0))],
        out_shape=[jax.ShapeDtypeStruct((T, A_WIDTH), F32), jax.ShapeDtypeStruct((nC, A_HEADS, HEAD_DIM, HEAD_DIM), F32)],
        scratch_shapes=[pltpu.VMEM((HGRN_HPB, HEAD_DIM, HEAD_DIM), F32)],
        compiler_params=_cp("parallel", "arbitrary"), name=name)(proj, proj, proj, lb)


def _hgrn2_bwd(proj, lb, st_all, do, *, name):
    T = proj.shape[0]
    nT = T // HGRN_TB

    def body(q_ref, z_ref, v_ref, lb_ref, st_ref, do_ref, dq_ref, dz_ref, dv_ref, dlb_ref, dstate):
        @pl.when(pl.program_id(1) == 0)
        def _():
            dstate[...] = jnp.zeros_like(dstate)
            dlb_ref[...] = jnp.zeros_like(dlb_ref)

        row = lax.broadcasted_iota(jnp.int32, (CHUNK, CHUNK), 0)
        col = lax.broadcasted_iota(jnp.int32, (CHUNK, CHUNK), 1)
        causal = row >= col
        tril01 = causal.astype(BF16)
        triu01 = (row <= col).astype(BF16)

        def chunk(cc, carry):
            c = HGRN_NCH - 1 - cc
            rows = pl.ds(pl.multiple_of(c * CHUNK, CHUNK), CHUNK)
            for hh in range(HGRN_HPB):
                sl = slice(hh * HEAD_DIM, (hh + 1) * HEAD_DIM)
                lbv = lb_ref[:, sl]
                q = q_ref[rows, sl]
                v = v_ref[rows, sl].astype(BF16)
                sig, f, kk, sq, eb, emb, eo, dec = _hgrn_chunk_fwd(q, z_ref[rows, sl], lbv, tril01)
                qi32 = q * sq * eb
                ki32 = kk * emb
                ko32 = kk * eo
                qi, ki, ko = qi32.astype(BF16), ki32.astype(BF16), ko32.astype(BF16)
                att = jnp.where(causal, _dot_nt(qi, ki), 0.0).astype(BF16)
                dout = do_ref[rows, sl].astype(BF16)
                st = st_ref[c, hh]
                dst = dstate[hh]
                dst16 = dst.astype(BF16)
                datt = jnp.where(causal, _dot_nt(dout, v), 0.0).astype(BF16)
                dqi = _dot(datt, ki) + _dot(dout, st)
                dki = _dot_tn(datt, qi)
                dv_ref[rows, sl] = (_dot_tn(att, dout) + _dot_nt(ko, dst16)).astype(BF16)
                dko = _dot(v, dst16)
                ddec = jnp.sum(dst * st, axis=0, keepdims=True)
                dstate[hh] = dst * dec + _dot_tn(dout, qi)
                dkk = dki * emb + dko * eo
                db = dqi * qi32 - dki * ki32 - dko * ko32
                dbend = jnp.sum(dko * ko32, axis=0, keepdims=True) + ddec * dec
                dlogf = _dot3(triu01, db) + dbend
                df = dlogf / f - dkk
                dz_ref[rows, sl] = (df * (1.0 - lbv) * sig * (1.0 - sig)).astype(BF16)
                dlb_ref[:, sl] += jnp.sum(df * (1.0 - sig), axis=0, keepdims=True)
                dq_ref[rows, sl] = (dqi * eb * (sq * (1.0 + q * (1.0 - sq)))).astype(BF16)
            return carry

        lax.fori_loop(0, HGRN_NCH, chunk, 0)

    W = HGRN_HPB * HEAD_DIM
    nG = A_HEADS // HGRN_HPB
    hb = lambda off: pl.BlockSpec((HGRN_TB, W), lambda h, i: (nT - 1 - i, off + h))
    hlb = pl.BlockSpec((1, W), lambda h, i: (0, h))
    o16 = jax.ShapeDtypeStruct((T, A_WIDTH), BF16)
    return pl.pallas_call(
        body, grid=(nG, nT),
        in_specs=[hb(0), hb(nG), hb(2 * nG), hlb,
                  pl.BlockSpec((HGRN_NCH, HGRN_HPB, HEAD_DIM, HEAD_DIM), lambda h, i: (nT - 1 - i, h, 0, 0)), hb(0)],
        out_specs=[hb(0), hb(0), hb(0), hlb],
        out_shape=[o16, o16, o16, jax.ShapeDtypeStruct((1, A_WIDTH), F32)],
        scratch_shapes=[pltpu.VMEM((HGRN_HPB, HEAD_DIM, HEAD_DIM), F32)],
        compiler_params=_cp("parallel", "arbitrary"), name=name)(proj, proj, proj, lb, st_all, do)


def _head_rms(x):
    r = lax.rsqrt(jnp.mean(x * x, axis=-1, keepdims=True) + EPS)
    return x * r, r


def _head_rms_bwd(dxhat, xhat, r):
    return r * (dxhat - xhat * jnp.mean(dxhat * xhat, axis=-1, keepdims=True))


def _a_post_fwd(o, proj, onorm, *, tt, name):
    T = o.shape[0]

    def body(o_ref, g_ref, w_ref, y_ref):
        for h in range(A_HEADS):
            sl = slice(h * HEAD_DIM, (h + 1) * HEAD_DIM)
            xhat, _ = _head_rms(o_ref[:, sl])
            g = g_ref[:, sl]
            y_ref[:, sl] = xhat * w_ref[:, sl] * (g * _sigmoid(g))

    blk = lambda c: pl.BlockSpec((tt, A_WIDTH), lambda i: (i, c))
    return pl.pallas_call(
        body, grid=(T // tt,), in_specs=[blk(0), blk(3), _full((1, A_WIDTH))], out_specs=blk(0),
        out_shape=jax.ShapeDtypeStruct((T, A_WIDTH), F32),
        compiler_params=_cp("parallel"), name=name)(o, proj, onorm)


def _a_post_bwd(o, proj, onorm, dmix, *, tt, name, dep=None):
    T = o.shape[0]

    def kernel_body(o_ref, g_ref, w_ref, dy_ref, do_ref, dg_ref, dw_ref):
        @pl.when(pl.program_id(0) == 0)
        def _():
            dw_ref[...] = jnp.zeros_like(dw_ref)

        for h in range(A_HEADS):
            sl = slice(h * HEAD_DIM, (h + 1) * HEAD_DIM)
            xhat, r = _head_rms(o_ref[:, sl])
            g = g_ref[:, sl]
            s = _sigmoid(g)
            dy = dy_ref[:, sl]
            w = w_ref[:, sl]
            dg_ref[:, sl] = (dy * xhat * w * (s * (1.0 + g * (1.0 - s)))).astype(BF16)
            dyn = dy * (g * s)
            dw_ref[:, sl] += jnp.sum(dyn * xhat, axis=0, keepdims=True)
            do_ref[:, sl] = _head_rms_bwd(dyn * w, xhat, r)

    blk = lambda c: pl.BlockSpec((tt, A_WIDTH), lambda i: (i, c))
    body, dep_specs, dep_args = _dep(kernel_body, 4, dep)
    return pl.pallas_call(
        body, grid=(T // tt,), in_specs=[blk(0), blk(3), _full((1, A_WIDTH)), blk(0)] + dep_specs,
        out_specs=[blk(0), blk(0), _full((1, A_WIDTH))],
        out_shape=[jax.ShapeDtypeStruct((T, A_WIDTH), F32), jax.ShapeDtypeStruct((T, A_WIDTH), BF16),
                   jax.ShapeDtypeStruct((1, A_WIDTH), F32)],
        compiler_params=_cp("arbitrary"), name=name)(o, proj, onorm, dmix, *dep_args)


def _mem_head_masks(n):
    lane = lax.broadcasted_iota(jnp.int32, (n, MEM_WIDTH), 1)
    return [(lane >= m * MEM_HEAD_DIM) & (lane < (m + 1) * MEM_HEAD_DIM) for m in range(MEM_HEADS)]


def _mem_head_rms(x, masks):
    x2 = x * x
    r = jnp.zeros_like(x)
    for mk in masks:
        ms = jnp.sum(jnp.where(mk, x2, 0.0), axis=-1, keepdims=True) * (1.0 / MEM_HEAD_DIM)
        r = jnp.where(mk, lax.rsqrt(ms + EPS), r)
    return x * r, r


def _mem_head_rms_bwd(dxhat, xhat, r, masks):
    t = dxhat * xhat
    m = jnp.zeros_like(t)
    for mk in masks:
        m = jnp.where(mk, jnp.sum(jnp.where(mk, t, 0.0), axis=-1, keepdims=True) * (1.0 / MEM_HEAD_DIM), m)
    return r * (dxhat - xhat * m)


MEM_SCALE = MEM_HEAD_DIM ** -0.5


def _mem_attn_fwd(proj, qcol, mkv, qn_w, kn_w, *, tt, name):
    T = proj.shape[0]

    def body(q_ref, k_ref, v_ref, qw_ref, kw_ref, o_ref):
        qmasks = _mem_head_masks(tt)
        kmasks = _mem_head_masks(MEM_TOKENS)
        qhat, _ = _mem_head_rms(q_ref[...], qmasks)
        qn = qhat * qw_ref[...]
        khat, _ = _mem_head_rms(k_ref[...], kmasks)
        kn = (khat * kw_ref[...]).astype(BF16)
        v = v_ref[...].astype(BF16)
        out = jnp.zeros((tt, MEM_WIDTH), F32)
        for m in range(MEM_HEADS):
            s = _dot_nt(jnp.where(qmasks[m], qn, 0.0), kn) * MEM_SCALE
            s = s - jnp.max(s, axis=-1, keepdims=True)
            p = jnp.exp(s)
            p = p / jnp.sum(p, axis=-1, keepdims=True)
            out = jnp.where(qmasks[m], _dot(p, v), out)
        o_ref[...] = out

    return pl.pallas_call(
        body, grid=(T // tt,),
        in_specs=[pl.BlockSpec((tt, MEM_WIDTH), lambda i: (i, qcol)), pl.BlockSpec((MEM_TOKENS, MEM_WIDTH), lambda i: (0, 0)),
                  pl.BlockSpec((MEM_TOKENS, MEM_WIDTH), lambda i: (0, 1)), _full((1, MEM_WIDTH)), _full((1, MEM_WIDTH))],
        out_specs=pl.BlockSpec((tt, MEM_WIDTH), lambda i: (i, 0)),
        out_shape=jax.ShapeDtypeStruct((T, MEM_WIDTH), F32),
        compiler_params=_cp("parallel"), name=name)(proj, mkv, mkv, qn_w, kn_w)


def _mem_attn_bwd(proj, qcol, mkv, qn_w, kn_w, dmix, *, tt, name):
    T = proj.shape[0]
    nsteps = T // tt
    ocol = (dmix.shape[1] - MEM_WIDTH) // MEM_WIDTH

    def body(q_ref, k_ref, v_ref, qw_ref, kw_ref, do_ref, dq_ref, dkv_ref, dqw_ref, dkw_ref, dk_acc, dv_acc):
        step = pl.program_id(0)

        @pl.when(step == 0)
        def _():
            dk_acc[...] = jnp.zeros_like(dk_acc)
            dv_acc[...] = jnp.zeros_like(dv_acc)
            dqw_ref[...] = jnp.zeros_like(dqw_ref)

        qmasks = _mem_head_masks(tt)
        kmasks = _mem_head_masks(MEM_TOKENS)
        qhat, qr = _mem_head_rms(q_ref[...], qmasks)
        qn = qhat * qw_ref[...]
        khat, kr = _mem_head_rms(k_ref[...], kmasks)
        kn = (khat * kw_ref[...]).astype(BF16)
        v = v_ref[...].astype(BF16)
        dout = do_ref[...]
        dqn = jnp.zeros((tt, MEM_WIDTH), F32)
        dkn = jnp.zeros((MEM_TOKENS, MEM_WIDTH), F32)
        dvv = jnp.zeros((MEM_TOKENS, MEM_WIDTH), F32)
        for m in range(MEM_HEADS):
            qm = jnp.where(qmasks[m], qn, 0.0).astype(BF16)
            s = _dot_nt(qm, kn) * MEM_SCALE
            s = s - jnp.max(s, axis=-1, keepdims=True)
            p = jnp.exp(s)
            p = p / jnp.sum(p, axis=-1, keepdims=True)
            dom = jnp.where(qmasks[m], dout, 0.0).astype(BF16)
            dp = _dot_nt(dom, v)
            ds = (p * (dp - jnp.sum(p * dp, axis=-1, keepdims=True)) * MEM_SCALE).astype(BF16)
            dqn = jnp.where(qmasks[m], _dot(ds, kn), dqn)
            dkn = jnp.where(kmasks[m], _dot_tn(ds, qm), dkn)
            dvv = jnp.where(kmasks[m], _dot_tn(p, dom), dvv)
        dqw_ref[...] += jnp.sum(dqn * qhat, axis=0, keepdims=True)
        dq_ref[...] = _mem_head_rms_bwd(dqn * qw_ref[...], qhat, qr, qmasks).astype(BF16)
        dk_acc[...] += dkn
        dv_acc[...] += dvv

        @pl.when(step == nsteps - 1)
        def _():
            dk = dk_acc[...]
            dkw_ref[...] = jnp.sum(dk * khat, axis=0, keepdims=True)
            dkv_ref[:, :MEM_WIDTH] = _mem_head_rms_bwd(dk * kw_ref[...], khat, kr, kmasks)
            dkv_ref[:, MEM_WIDTH:] = dv_acc[...]

    return pl.pallas_call(
        body, grid=(nsteps,),
        in_specs=[pl.BlockSpec((tt, MEM_WIDTH), lambda i: (i, qcol)), pl.BlockSpec((MEM_TOKENS, MEM_WIDTH), lambda i: (0, 0)),
                  pl.BlockSpec((MEM_TOKENS, MEM_WIDTH), lambda i: (0, 1)), _full((1, MEM_WIDTH)), _full((1, MEM_WIDTH)),
                  pl.BlockSpec((tt, MEM_WIDTH), lambda i: (i, ocol))],
        out_specs=[pl.BlockSpec((tt, MEM_WIDTH), lambda i: (i, 0)), _full((MEM_TOKENS, 2 * MEM_WIDTH)),
                   _full((1, MEM_WIDTH)), _full((1, MEM_WIDTH))],
        out_shape=[jax.ShapeDtypeStruct((T, MEM_WIDTH), BF16), jax.ShapeDtypeStruct((MEM_TOKENS, 2 * MEM_WIDTH), F32),
                   jax.ShapeDtypeStruct((1, MEM_WIDTH), F32), jax.ShapeDtypeStruct((1, MEM_WIDTH), F32)],
        scratch_shapes=[pltpu.VMEM((MEM_TOKENS, MEM_WIDTH), F32), pltpu.VMEM((MEM_TOKENS, MEM_WIDTH), F32)],
        compiler_params=_cp("arbitrary"), name=name)(proj, mkv, mkv, qn_w, kn_w, dmix)


HALF = HEAD_DIM // 2
ATT_SCALE = HEAD_DIM ** -0.5
NEG = -1e30


def _rope_tables(T):
    inv = ROPE_THETA ** (-jnp.arange(HALF, dtype=F32) / HALF)
    ang = jnp.arange(T, dtype=F32)[:, None] * inv[None, :]
    cos, sin = jnp.cos(ang), jnp.sin(ang)
    return jnp.concatenate([cos, cos], axis=-1), jnp.concatenate([-sin, sin], axis=-1)


def _rope(x, cosf, sinsg):
    return x * cosf + pltpu.roll(x, HALF, 1) * sinsg


def _rope_bwd(dy, cosf, sinsg):
    return dy * cosf + pltpu.roll(dy * sinsg, HALF, 1)


def _headnorm_rope_fwd(x, w_heads, cosf, sinsg, *, col0, n_heads, tt, name):
    T = x.shape[0]
    W = n_heads * HEAD_DIM

    def body(x_ref, w_ref, c_ref, s_ref, y_ref):
        c, s = c_ref[...], s_ref[...]
        for h in range(n_heads):
            sl = slice(h * HEAD_DIM, (h + 1) * HEAD_DIM)
            xhat, _ = _head_rms(x_ref[:, sl])
            y_ref[:, sl] = _rope(xhat * w_ref[:, sl], c, s)

    tbl = pl.BlockSpec((tt, HEAD_DIM), lambda i: (i, 0))
    return pl.pallas_call(
        body, grid=(T // tt,),
        in_specs=[pl.BlockSpec((tt, W), lambda i: (i, col0)), _full((1, W)), tbl, tbl],
        out_specs=pl.BlockSpec((tt, W), lambda i: (i, 0)),
        out_shape=jax.ShapeDtypeStruct((T, W), F32),
        compiler_params=_cp("parallel"), name=name)(x, w_heads, cosf, sinsg)


def _q_prep_bwd(proj, w_heads, cosf, sinsg, dqs, *, tt, name):
    T = proj.shape[0]
    W = N_GROUPS * B_WIDTH

    def body(x_ref, w_ref, c_ref, s_ref, d0, d1, d2, dx_ref, dw_ref):
        @pl.when(pl.program_id(0) == 0)
        def _():
            dw_ref[...] = jnp.zeros_like(dw_ref)

        c, s = c_ref[...], s_ref[...]
        for gi, d_ref in enumerate((d0, d1, d2)):
            for h in range(B_HEADS):
                sl = slice((gi * B_HEADS + h) * HEAD_DIM, (gi * B_HEADS + h + 1) * HEAD_DIM)
                xhat, r = _head_rms(x_ref[:, sl])
                dyn = _rope_bwd(d_ref[:, h * HEAD_DIM:(h + 1) * HEAD_DIM], c, s)
                dw_ref[:, sl] += jnp.sum(dyn * xhat, axis=0, keepdims=True)
                dx_ref[:, sl] = _head_rms_bwd(dyn * w_ref[:, sl], xhat, r).astype(BF16)

    tbl = pl.BlockSpec((tt, HEAD_DIM), lambda i: (i, 0))
    dyb = pl.BlockSpec((tt, B_WIDTH), lambda i: (i, 0))
    return pl.pallas_call(
        body, grid=(T // tt,),
        in_specs=[pl.BlockSpec((tt, W), lambda i: (i, 0)), _full((1, W)), tbl, tbl, dyb, dyb, dyb],
        out_specs=[pl.BlockSpec((tt, W), lambda i: (i, 0)), _full((1, W))],
        out_shape=[jax.ShapeDtypeStruct((T, W), BF16), jax.ShapeDtypeStruct((1, W), F32)],
        compiler_params=_cp("arbitrary"), name=name)(proj, w_heads, cosf, sinsg, *dqs)


def _kv_prep_bwd(kv, w_heads, cosf, sinsg, dks, dvs, *, tt, name):
    T = kv.shape[0]

    def body(x_ref, w_ref, c_ref, s_ref, k0, k1, k2, v0, v1, v2, dx_ref, dw_ref):
        @pl.when(pl.program_id(0) == 0)
        def _():
            dw_ref[...] = jnp.zeros_like(dw_ref)

        c, s = c_ref[...], s_ref[...]
        for h in range(B_HEADS):
            sl = slice(h * HEAD_DIM, (h + 1) * HEAD_DIM)
            vs = slice(B_WIDTH + h * HEAD_DIM, B_WIDTH + (h + 1) * HEAD_DIM)
            xhat, r = _head_rms(x_ref[:, sl])
            dyn = _rope_bwd(k0[:, sl] + k1[:, sl] + k2[:, sl], c, s)
            dw_ref[:, sl] += jnp.sum(dyn * xhat, axis=0, keepdims=True)
            dx_ref[:, sl] = _head_rms_bwd(dyn * w_ref[:, sl], xhat, r).astype(BF16)
            dx_ref[:, vs] = (v0[:, sl] + v1[:, sl] + v2[:, sl]).astype(BF16)

    tbl = pl.BlockSpec((tt, HEAD_DIM), lambda i: (i, 0))
    dyb = pl.BlockSpec((tt, B_WIDTH), lambda i: (i, 0))
    return pl.pallas_call(
        body, grid=(T // tt,),
        in_specs=[dyb, _full((1, B_WIDTH)), tbl, tbl] + [dyb] * 6,
        out_specs=[pl.BlockSpec((tt, 2 * B_WIDTH), lambda i: (i, 0)), _full((1, B_WIDTH))],
        out_shape=[jax.ShapeDtypeStruct((T, 2 * B_WIDTH), BF16), jax.ShapeDtypeStruct((1, B_WIDTH), F32)],
        compiler_params=_cp("arbitrary"), name=name)(kv, w_heads, cosf, sinsg, *dks, *dvs)


def _band_masks(n_is_first=None):
    row = lax.broadcasted_iota(jnp.int32, (SPAN, SPAN), 0)
    col = lax.broadcasted_iota(jnp.int32, (SPAN, SPAN), 1)
    return row >= col, col >= row


def _dil_views(T, d):
    L = T // d
    return L, L // SPAN


def _dil_fwd(qr, kr, kv, gi, d, *, name):
    T = qr.shape[0]
    L, nb = _dil_views(T, d)

    def body(q_ref, kc_ref, kp_ref, vc_ref, vp_ref, o_ref, lse_ref):
        cur_ok, prev_band = _band_masks()
        prev_ok = prev_band & (pl.program_id(1) > 0)
        for h in range(B_HEADS):
            sl = slice(h * HEAD_DIM, (h + 1) * HEAD_DIM)
            q = q_ref[:, sl]
            sc = jnp.where(cur_ok, _dot_nt(q, kc_ref[:, sl]) * ATT_SCALE, NEG)
            sp = jnp.where(prev_ok, _dot_nt(q, kp_ref[:, sl]) * ATT_SCALE, NEG)
            m = jnp.maximum(jnp.max(sc, axis=-1, keepdims=True), jnp.max(sp, axis=-1, keepdims=True))
            pc = jnp.exp(sc - m)
            pp = jnp.exp(sp - m)
            l = jnp.sum(pc, axis=-1, keepdims=True) + jnp.sum(pp, axis=-1, keepdims=True)
            o_ref[:, sl] = (_dot(pc, vc_ref[:, sl]) + _dot(pp, vp_ref[:, sl])) / l
            lse_ref[:, sl] = jnp.broadcast_to(m + jnp.log(l), (SPAN, HEAD_DIM))

    blk = lambda f: pl.BlockSpec((SPAN, B_WIDTH), f)
    cur = lambda r, n: (n, r)
    prev = lambda r, n: (jnp.maximum(n - 1, 0), r)
    ov = jax.ShapeDtypeStruct((L, d * B_WIDTH), F32)
    o, lse = pl.pallas_call(
        body, grid=(d, nb),
        in_specs=[blk(lambda r, n: (n, r * N_GROUPS + gi)), blk(cur), blk(prev),
                  blk(lambda r, n: (n, 2 * r + 1)), blk(lambda r, n: (jnp.maximum(n - 1, 0), 2 * r + 1))],
        out_specs=[blk(cur), blk(cur)], out_shape=[ov, ov],
        compiler_params=_cp("parallel", "arbitrary"), name=name,
    )(qr.reshape(L, d * N_GROUPS * B_WIDTH), kr.reshape(L, d * B_WIDTH), kr.reshape(L, d * B_WIDTH),
      kv.reshape(L, d * 2 * B_WIDTH), kv.reshape(L, d * 2 * B_WIDTH))
    return o.reshape(T, B_WIDTH), lse.reshape(T, B_WIDTH)


def _dil_combine_fwd(os_, lses, *, tt, name):
    T = os_[0].shape[0]

    def body(o0, o1, o2, l0, l1, l2, y_ref, lse_ref):
        a, b, c = l0[...], l1[...], l2[...]
        m = jnp.maximum(jnp.maximum(a, b), c)
        wa, wb, wc = jnp.exp(a - m), jnp.exp(b - m), jnp.exp(c - m)
        den = wa + wb + wc
        y_ref[...] = (wa * o0[...] + wb * o1[...] + wc * o2[...]) / den
        lse_ref[...] = m + jnp.log(den)

    blk = pl.BlockSpec((tt, B_WIDTH), lambda i: (i, 0))
    sh = jax.ShapeDtypeStruct((T, B_WIDTH), F32)
    return pl.pallas_call(
        body, grid=(T // tt,), in_specs=[blk] * 6, out_specs=[blk, blk], out_shape=[sh, sh],
        compiler_params=_cp("parallel"), name=name)(*os_, *lses)


def _dil_bwd_prep(dmix, mix_main, *, tt, name, dep=None):
    T = mix_main.shape[0]

    def kernel_body(dy_ref, y_ref, dmm_ref, dd_ref):
        for h in range(B_HEADS):
            sl = slice(h * HEAD_DIM, (h + 1) * HEAD_DIM)
            dy = dy_ref[:, sl]
            dmm_ref[:, sl] = dy.astype(BF16)
            dd_ref[:, sl] = jnp.broadcast_to(jnp.sum(dy * y_ref[:, sl], axis=-1, keepdims=True), (tt, HEAD_DIM))

    blk = pl.BlockSpec((tt, B_WIDTH), lambda i: (i, 0))
    body, dep_specs, dep_args = _dep(kernel_body, 2, dep)
    return pl.pallas_call(
        body, grid=(T // tt,), in_specs=[blk, blk] + dep_specs, out_specs=[blk, blk],
        out_shape=[jax.ShapeDtypeStruct((T, B_WIDTH), BF16), jax.ShapeDtypeStruct((T, B_WIDTH), F32)],
        compiler_params=_cp("parallel"), name=name)(dmix, mix_main, *dep_args)


def _dil_bwd_dq(qr, kr, kv, dmm, lse, dd, gi, d, *, name):
    T = qr.shape[0]
    L, nb = _dil_views(T, d)

    def body(q_ref, kc_ref, kp_ref, vc_ref, vp_ref, dy_ref, lse_ref, dd_ref, dq_ref):
        cur_ok, prev_band = _band_masks()
        prev_ok = prev_band & (pl.program_id(1) > 0)
        for h in range(B_HEADS):
            sl = slice(h * HEAD_DIM, (h + 1) * HEAD_DIM)
            q, dy = q_ref[:, sl], dy_ref[:, sl]
            kc, kp = kc_ref[:, sl], kp_ref[:, sl]
            lse_h = jnp.max(lse_ref[:, sl], axis=-1, keepdims=True)
            dd_h = jnp.max(dd_ref[:, sl], axis=-1, keepdims=True)
            pc = jnp.exp(jnp.where(cur_ok, _dot_nt(q, kc) * ATT_SCALE, NEG) - lse_h)
            pp = jnp.exp(jnp.where(prev_ok, _dot_nt(q, kp) * ATT_SCALE, NEG) - lse_h)
            dsc = pc * (_dot_nt(dy, vc_ref[:, sl]) - dd_h) * ATT_SCALE
            dsp = pp * (_dot_nt(dy, vp_ref[:, sl]) - dd_h) * ATT_SCALE
            dq_ref[:, sl] = _dot(dsc, kc) + _dot(dsp, kp)

    blk = lambda f: pl.BlockSpec((SPAN, B_WIDTH), f)
    cur = lambda r, n: (n, r)
    prev = lambda r, n: (jnp.maximum(n - 1, 0), r)
    v2 = lambda a: a.reshape(L, d * a.shape[1])
    dq = pl.pallas_call(
        body, grid=(d, nb),
        in_specs=[blk(lambda r, n: (n, r * N_GROUPS + gi)), blk(cur), blk(prev),
                  blk(lambda r, n: (n, 2 * r + 1)), blk(lambda r, n: (jnp.maximum(n - 1, 0), 2 * r + 1)),
                  blk(cur), blk(cur), blk(cur)],
        out_specs=blk(cur), out_shape=jax.ShapeDtypeStruct((L, d * B_WIDTH), F32),
        compiler_params=_cp("parallel", "arbitrary"), name=name,
    )(v2(qr), v2(kr), v2(kr), v2(kv), v2(kv), v2(dmm), v2(lse), v2(dd))
    return dq.reshape(T, B_WIDTH)


def _dil_bwd_dkv(qr, kr, kv, dmm, lse, dd, gi, d, *, name):
    T = qr.shape[0]
    L, nb = _dil_views(T, d)

    def body(k_ref, v_ref, q0_ref, q1_ref, dy0_ref, dy1_ref, lse0_ref, lse1_ref, dd0_ref, dd1_ref, dk_ref, dv_ref):
        cur_ok, prev_band = _band_masks()
        next_ok = prev_band & (pl.program_id(1) < nb - 1)
        for h in range(B_HEADS):
            sl = slice(h * HEAD_DIM, (h + 1) * HEAD_DIM)
            k, v = k_ref[:, sl], v_ref[:, sl]
            dk = jnp.zeros((SPAN, HEAD_DIM), F32)
            dv = jnp.zeros((SPAN, HEAD_DIM), F32)
            for ok, q_ref, dy_ref, lse_ref, dd_ref in ((cur_ok, q0_ref, dy0_ref, lse0_ref, dd0_ref),
                                                         (next_ok, q1_ref, dy1_ref, lse1_ref, dd1_ref)):
                q, dy = q_ref[:, sl], dy_ref[:, sl]
                lse_h = jnp.max(lse_ref[:, sl], axis=-1, keepdims=True)
                dd_h = jnp.max(dd_ref[:, sl], axis=-1, keepdims=True)
                p = jnp.exp(jnp.where(ok, _dot_nt(q, k) * ATT_SCALE, NEG) - lse_h)
                ds = p * (_dot_nt(dy, v) - dd_h) * ATT_SCALE
                dk = dk + _dot_tn(ds, q)
                dv = dv + _dot_tn(p, dy)
            dk_ref[:, sl] = dk
            dv_ref[:, sl] = dv

    blk = lambda f: pl.BlockSpec((SPAN, B_WIDTH), f)
    cur = lambda r, n: (n, r)
    nxt = lambda r, n: (jnp.minimum(n + 1, nb - 1), r)
    qcur = lambda r, n: (n, r * N_GROUPS + gi)
    qnxt = lambda r, n: (jnp.minimum(n + 1, nb - 1), r * N_GROUPS + gi)
    v2 = lambda a: a.reshape(L, d * a.shape[1])
    ov = jax.ShapeDtypeStruct((L, d * B_WIDTH), F32)
    dk, dv = pl.pallas_call(
        body, grid=(d, nb),
        in_specs=[blk(cur), blk(lambda r, n: (n, 2 * r + 1)), blk(qcur), blk(qnxt),
                  blk(cur), blk(nxt), blk(cur), blk(nxt), blk(cur), blk(nxt)],
        out_specs=[blk(cur), blk(cur)], out_shape=[ov, ov],
        compiler_params=_cp("parallel", "arbitrary"), name=name,
    )(v2(kr), v2(kv), v2(qr), v2(qr), v2(dmm), v2(dmm), v2(lse), v2(lse), v2(dd), v2(dd))
    return dk.reshape(T, B_WIDTH), dv.reshape(T, B_WIDTH)


DILS_UNROLL = 4


def _dils_specs(gi, d, nblk):
    blk = lambda f: pl.BlockSpec((SPAN * d, HEAD_DIM), f)
    return {
        "q": blk(lambda h, n: (n, gi * B_HEADS + h)), "q_next": blk(lambda h, n: (jnp.minimum(n + 1, nblk - 1), gi * B_HEADS + h)),
        "cur": blk(lambda h, n: (n, h)), "prev": blk(lambda h, n: (jnp.maximum(n - 1, 0), h)),
        "next": blk(lambda h, n: (jnp.minimum(n + 1, nblk - 1), h)),
        "v": blk(lambda h, n: (n, B_HEADS + h)), "v_prev": blk(lambda h, n: (jnp.maximum(n - 1, 0), B_HEADS + h)),
    }


def _dils_fwd(qr, kr, kv, gi, d, *, name):
    T = qr.shape[0]
    nblk = T // (SPAN * d)
    sp = _dils_specs(gi, d, nblk)

    def body(q_ref, kc_ref, kp_ref, vc_ref, vp_ref, o_ref, lse_ref):
        cur_ok, prev_band = _band_masks()
        prev_ok = prev_band & (pl.program_id(1) > 0)

        def residue(r, carry):
            rows = pl.ds(r, SPAN, stride=d)
            q = q_ref[rows, :]
            sc = jnp.where(cur_ok, _dot_nt(q, kc_ref[rows, :]) * ATT_SCALE, NEG)
            sp_ = jnp.where(prev_ok, _dot_nt(q, kp_ref[rows, :]) * ATT_SCALE, NEG)
            m = jnp.maximum(jnp.max(sc, axis=-1, keepdims=True), jnp.max(sp_, axis=-1, keepdims=True))
            pc = jnp.exp(sc - m)
            pp = jnp.exp(sp_ - m)
            l = jnp.sum(pc, axis=-1, keepdims=True) + jnp.sum(pp, axis=-1, keepdims=True)
            o_ref[rows, :] = (_dot(pc, vc_ref[rows, :]) + _dot(pp, vp_ref[rows, :])) / l
            lse_ref[rows, :] = jnp.broadcast_to(m + jnp.log(l), (SPAN, HEAD_DIM))
            return carry

        lax.fori_loop(0, d, residue, 0, unroll=DILS_UNROLL)

    sh = jax.ShapeDtypeStruct((T, B_WIDTH), F32)
    return pl.pallas_call(
        body, grid=(B_HEADS, nblk), in_specs=[sp["q"], sp["cur"], sp["prev"], sp["v"], sp["v_prev"]],
        out_specs=[sp["cur"], sp["cur"]], out_shape=[sh, sh],
        compiler_params=_cp("parallel", "arbitrary"), name=name)(qr, kr, kr, kv, kv)


def _dils_bwd_dq(qr, kr, kv, dmix, lse, dd, gi, d, *, name):
    T = qr.shape[0]
    nblk = T // (SPAN * d)
    sp = _dils_specs(gi, d, nblk)

    def body(q_ref, kc_ref, kp_ref, vc_ref, vp_ref, dy_ref, lse_ref, dd_ref, dq_ref):
        cur_ok, prev_band = _band_masks()
        prev_ok = prev_band & (pl.program_id(1) > 0)

        def residue(r, carry):
            rows = pl.ds(r, SPAN, stride=d)
            q, dy = q_ref[rows, :], dy_ref[rows, :]
            kc, kp = kc_ref[rows, :], kp_ref[rows, :]
            lse_h = jnp.max(lse_ref[rows, :], axis=-1, keepdims=True)
            dd_h = jnp.max(dd_ref[rows, :], axis=-1, keepdims=True)
            pc = jnp.exp(jnp.where(cur_ok, _dot_nt(q, kc) * ATT_SCALE, NEG) - lse_h)
            pp = jnp.exp(jnp.where(prev_ok, _dot_nt(q, kp) * ATT_SCALE, NEG) - lse_h)
            dsc = pc * (_dot_nt(dy, vc_ref[rows, :]) - dd_h) * ATT_SCALE
            dsp = pp * (_dot_nt(dy, vp_ref[rows, :]) - dd_h) * ATT_SCALE
            dq_ref[rows, :] = _dot(dsc, kc) + _dot(dsp, kp)
            return carry

        lax.fori_loop(0, d, residue, 0, unroll=DILS_UNROLL)

    return pl.pallas_call(
        body, grid=(B_HEADS, nblk),
        in_specs=[sp["q"], sp["cur"], sp["prev"], sp["v"], sp["v_prev"], sp["cur"], sp["cur"], sp["cur"]],
        out_specs=sp["cur"], out_shape=jax.ShapeDtypeStruct((T, B_WIDTH), F32),
        compiler_params=_cp("parallel", "arbitrary"), name=name)(qr, kr, kr, kv, kv, dmix, lse, dd)


def _dils_bwd_dkv(qr, kr, kv, dmix, lse, dd, gi, d, *, name):
    T = qr.shape[0]
    nblk = T // (SPAN * d)
    sp = _dils_specs(gi, d, nblk)

    def body(k_ref, v_ref, q0_ref, q1_ref, dy0_ref, dy1_ref, lse0_ref, lse1_ref, dd0_ref, dd1_ref, dk_ref, dv_ref):
        cur_ok, prev_band = _band_masks()
        next_ok = prev_band & (pl.program_id(1) < nblk - 1)

        def residue(r, carry):
            rows = pl.ds(r, SPAN, stride=d)
            k, v = k_ref[rows, :], v_ref[rows, :]
            dk = jnp.zeros((SPAN, HEAD_DIM), F32)
            dv = jnp.zeros((SPAN, HEAD_DIM), F32)
            for ok, q_ref, dy_ref, lse_ref, dd_ref in ((cur_ok, q0_ref, dy0_ref, lse0_ref, dd0_ref),
                                                         (next_ok, q1_ref, dy1_ref, lse1_ref, dd1_ref)):
                q, dy = q_ref[rows, :], dy_ref[rows, :]
                lse_h = jnp.max(lse_ref[rows, :], axis=-1, keepdims=True)
                dd_h = jnp.max(dd_ref[rows, :], axis=-1, keepdims=True)
                p = jnp.exp(jnp.where(ok, _dot_nt(q, k) * ATT_SCALE, NEG) - lse_h)
                ds = p * (_dot_nt(dy, v) - dd_h) * ATT_SCALE
                dk = dk + _dot_tn(ds, q)
                dv = dv + _dot_tn(p, dy)
            dk_ref[rows, :] = dk
            dv_ref[rows, :] = dv
            return carry

        lax.fori_loop(0, d, residue, 0, unroll=DILS_UNROLL)

    sh = jax.ShapeDtypeStruct((T, B_WIDTH), F32)
    return pl.pallas_call(
        body, grid=(B_HEADS, nblk),
        in_specs=[sp["cur"], sp["v"], sp["q"], sp["q_next"], sp["cur"], sp["next"], sp["cur"], sp["next"], sp["cur"], sp["next"]],
        out_specs=[sp["cur"], sp["cur"]], out_shape=[sh, sh],
        compiler_params=_cp("parallel", "arbitrary"), name=name)(kr, kv, qr, qr, dmix, dmix, lse, lse, dd, dd)


A_MQ_COL = 4 * A_WIDTH // MEM_WIDTH
B_MQ_COL = N_GROUPS * B_WIDTH // MEM_WIDTH


def _row(v):
    return v.reshape(1, -1).astype(F32)


def _local_step(x, mem, tgt, get_w, P, put_g, first_dep=None):
    T = x.shape[0]
    cosf, sinsg = _rope_tables(T)
    lb_soft = jax.nn.softmax(P["a_lb_logits"].astype(F32), axis=0)
    lb = lb_soft[0:1]
    qw_heads = jnp.repeat(P["b_qnorm"][0], B_HEADS, axis=0).reshape(1, -1)
    kw_heads = jnp.tile(_row(P["b_knorm"]), (1, B_HEADS))
    mqw = [jnp.tile(_row(P["mem_qnorm"][l]), (1, MEM_HEADS)) for l in range(2)]
    mkw = [jnp.tile(_row(P["mem_knorm"][l]), (1, MEM_HEADS)) for l in range(2)]
    nmix = [_row(P["norm_mix"][l]) for l in range(2)]
    nffn = [_row(P["norm_ffn"][l]) for l in range(2)]
    mnorm = [_row(P["mem_norm"][l]) for l in range(2)]
    kvn = _row(P["kv_norm"])
    onorm = _row(P["a_onorm"])
    W = {}

    def w_of(name, after=None):
        if name not in W:
            W[name] = get_w(name, after)
        return W[name]

    proj_a, xn0 = _rms_matmul(x, nmix[0], w_of("a_w_in"), tt=512, tn=1664, wt=True, name="proj_a", dep=first_dep)
    mkv0, mn0 = _rms_matmul(mem, mnorm[0], w_of("w_mem_kv0"), tt=MEM_TOKENS, tn=2 * MEM_WIDTH, wt=False, name="mem_kv0")
    o_raw, st = _hgrn2_fwd(proj_a, lb, name="hgrn2_fwd")
    mm0 = _a_post_fwd(o_raw, proj_a, onorm, tt=512, name="a_post_fwd")
    mo0 = _mem_attn_fwd(proj_a, A_MQ_COL, mkv0, mqw[0], mkw[0], tt=512, name="mem_attn_fwd0")
    mix0 = jnp.concatenate([mm0, mo0], axis=1)
    hm0 = _mm_res(x, mix0, w_of("w_out0", mix0), tt=512, name="out_proj0")
    gu0, hn0 = _rms_matmul(hm0, nffn[0], w_of("w_gate_up0", hm0), tt=512, tn=1408, wt=True, name="gate_up0")
    h1 = _swiglu_down(hm0, gu0, w_of("w_down0", gu0), tt=256, name="down0")
    kv, hkn = _rms_matmul(h1, kvn, w_of("w_kv", h1), tt=512, tn=768, wt=True, name="kv_proj")
    kr = _headnorm_rope_fwd(kv, kw_heads, cosf, sinsg, col0=0, n_heads=B_HEADS, tt=512, name="k_prep")

    proj_b, xn1 = _rms_matmul(h1, nmix[1], w_of("b_w_in", kr), tt=512, tn=1280, wt=True, name="proj_b")
    mkv1, mn1 = _rms_matmul(mem, mnorm[1], w_of("w_mem_kv1", kr), tt=MEM_TOKENS, tn=2 * MEM_WIDTH, wt=False, name="mem_kv1")
    qr = _headnorm_rope_fwd(proj_b, qw_heads, cosf, sinsg, col0=0, n_heads=N_GROUPS * B_HEADS, tt=512, name="q_prep")
    outs = [(_dil_fwd if d == 1 else _dils_fwd)(qr, kr, kv, gi, d, name=f"dil_fwd{gi}") for gi, d in enumerate(DILATIONS)]
    mm1, lse_tot = _dil_combine_fwd([o for o, _ in outs], [s for _, s in outs], tt=512, name="dil_combine")
    mo1 = _mem_attn_fwd(proj_b, B_MQ_COL, mkv1, mqw[1], mkw[1], tt=512, name="mem_attn_fwd1")
    mix1 = jnp.concatenate([mm1, mo1], axis=1)
    hm1 = _mm_res(h1, mix1, w_of("w_out1", mix1), tt=512, name="out_proj1")
    gu1, hn1 = _rms_matmul(hm1, nffn[1], w_of("w_gate_up1", hm1), tt=512, tn=1408, wt=True, name="gate_up1")
    y = _swiglu_down(hm1, gu1, w_of("w_down1", gu1), tt=256, name="down1")
    dy, sq = _loss_kernel(y, tgt, tt=512, name="loss")

    gP = {}
    zeros_mem = jnp.zeros((MEM_TOKENS, D_MODEL), F32)

    def ffn_bwd(l, dh, hm, gu, hn):
        dgu, act = _swiglu_bwd(dh, gu, w_of(f"w_down{l}"), tt=256, name=f"swiglu_bwd{l}")
        g_wd = _mm_tn(act, dh, tt=512, tka=1408, name=f"g_w_down{l}")
        g_wgu = _mm_tn(dgu, hn, tt=512, tka=1408, name=f"g_w_gate_up{l}")
        sent = put_g({f"w_down{l}": g_wd, f"w_gate_up{l}": g_wgu})
        dhm, g_nf = _rms_bwd_dx(hm, nffn[l], w_of(f"w_gate_up{l}"), dgu, dh, tt=256, wt=True, name=f"gate_up_bwd{l}", dep=sent)
        return dhm, g_nf

    def mix_bwd(l, dhm, mix, proj, qcol, mkv, mn):
        dmix = _mm_nt(dhm, w_of(f"w_out{l}"), tt=512, name=f"out_proj_bwd{l}")
        g_wout = _mm_tn(mix, dhm, tt=512, tka=512, name=f"g_w_out{l}")
        dmq, dmkv, dqw, dkw = _mem_attn_bwd(proj, qcol, mkv, mqw[l], mkw[l], dmix, tt=512, name=f"mem_attn_bwd{l}")
        g_wmkv = _mm_tn(mn, dmkv, tt=MEM_TOKENS, tka=512, name=f"g_w_mem_kv{l}")
        sent = put_g({f"w_out{l}": g_wout, f"w_mem_kv{l}": g_wmkv})
        _, g_mn = _rms_bwd_dx(mem, mnorm[l], w_of(f"w_mem_kv{l}"), dmkv, zeros_mem, tt=MEM_TOKENS, wt=False, name=f"mem_kv_bwd{l}")
        fold = lambda v: v.reshape(MEM_HEADS, MEM_HEAD_DIM).sum(axis=0)
        return dmix, dmq, g_mn, fold(dqw), fold(dkw), sent

    dhm1, g_nf1 = ffn_bwd(1, dy, hm1, gu1, hn1)
    dmix1, dmq1, g_mn1, g_mq1, g_mk1, sent = mix_bwd(1, dhm1, mix1, proj_b, B_MQ_COL, mkv1, mn1)
    dmm, dd = _dil_bwd_prep(dmix1, mm1, tt=512, name="dil_bwd_prep", dep=sent)
    dqs, dks, dvs = [], [], []
    for gi, d in enumerate(DILATIONS):
        if d == 1:
            dqs.append(_dil_bwd_dq(qr, kr, kv, dmm, lse_tot, dd, gi, d, name=f"dil_bwd_dq{gi}"))
            dk_g, dv_g = _dil_bwd_dkv(qr, kr, kv, dmm, lse_tot, dd, gi, d, name=f"dil_bwd_dkv{gi}")
        else:
            dqs.append(_dils_bwd_dq(qr, kr, kv, dmix1, lse_tot, dd, gi, d, name=f"dil_bwd_dq{gi}"))
            dk_g, dv_g = _dils_bwd_dkv(qr, kr, kv, dmix1, lse_tot, dd, gi, d, name=f"dil_bwd_dkv{gi}")
        dks.append(dk_g)
        dvs.append(dv_g)
    dq_raw, dqw = _q_prep_bwd(proj_b, qw_heads, cosf, sinsg, dqs, tt=512, name="q_prep_bwd")
    dkv, dkw = _kv_prep_bwd(kv, kw_heads, cosf, sinsg, dks, dvs, tt=512, name="kv_prep_bwd")
    dproj_b = jnp.concatenate([dq_raw, dmq1], axis=1)
    g_wb = _mm_tn(dproj_b, xn1, tt=512, tka=1280, name="g_b_w_in")
    g_wkv = _mm_tn(dkv, hkn, tt=512, tka=768, name="g_w_kv")
    sent = put_g({"b_w_in": g_wb, "w_kv": g_wkv})
    dh1, g_nm1 = _rms_bwd_dx(h1, nmix[1], w_of("b_w_in"), dproj_b, dhm1, tt=256, wt=True, name="proj_b_bwd", dep=sent)
    dh1, g_kvn = _rms_bwd_dx(h1, kvn, w_of("w_kv"), dkv, dh1, tt=256, wt=True, name="kv_proj_bwd")

    dhm0, g_nf0 = ffn_bwd(0, dh1, hm0, gu0, hn0)
    dmix0, dmq0, g_mn0, g_mq0, g_mk0, sent = mix_bwd(0, dhm0, mix0, proj_a, A_MQ_COL, mkv0, mn0)
    do_raw, dg, g_onorm = _a_post_bwd(o_raw, proj_a, onorm, dmix0, tt=512, name="a_post_bwd", dep=sent)
    dq, dz, dv, dlb = _hgrn2_bwd(proj_a, lb, st, do_raw, name="hgrn2_bwd")
    dproj_a = jnp.concatenate([dq, dz, dv, dg, dmq0], axis=1)
    sent = put_g({"a_w_in": _mm_tn(dproj_a, xn0, tt=512, tka=1664, name="g_a_w_in")})
    gx, g_nm0 = _rms_bwd_dx(x, nmix[0], w_of("a_w_in"), dproj_a, dhm0, tt=256, wt=True, name="proj_a_bwd", dep=sent)

    dl0 = lb_soft[0:1] * lb_soft[1:2] * dlb
    gP["a_lb_logits"] = jnp.concatenate([dl0, -dl0], axis=0)
    gP["a_onorm"] = g_onorm
    gP["norm_mix"] = jnp.concatenate([g_nm0, g_nm1], axis=0)
    gP["norm_ffn"] = jnp.concatenate([g_nf0, g_nf1], axis=0)
    gP["b_qnorm"] = dqw.reshape(N_GROUPS, B_HEADS, HEAD_DIM).sum(axis=1)[None]
    gP["kv_norm"] = g_kvn.reshape(-1)
    gP["b_knorm"] = dkw.reshape(B_HEADS, HEAD_DIM).sum(axis=0)
    gP["mem_norm"] = jnp.concatenate([g_mn0, g_mn1], axis=0)
    gP["mem_qnorm"] = jnp.stack([g_mq0, g_mq1])
    gP["mem_knorm"] = jnp.stack([g_mk0, g_mk1])
    return sq, gx, gP


MESH_ID = pl.DeviceIdType.MESH
HBM_SPEC = pl.BlockSpec(memory_space=pltpu.HBM)


def _position():
    return lax.axis_index("x"), lax.axis_index("y"), lax.axis_index("c")


def _all_gather(blocks, *, name):
    n = len(blocks)

    def body(*refs):
        x_refs, out_refs = refs[:n], refs[n:2 * n]
        send_sems, recv_sems, local_sems = refs[2 * n:]
        x, y, c = _position()
        me, sibling = (x, y, c), (x, y, 1 - c)
        chips = [(1 - x, y), (x, 1 - y), (1 - x, 1 - y)]

        def slot(a, px, py, pc):
            return out_refs[a].at[4 * px + 2 * py + pc]

        def copy(a, k, blk, to, src=None):
            return pltpu.make_async_remote_copy(
                src_ref=slot(a, *blk) if src is None else src, dst_ref=slot(a, *blk),
                send_sem=send_sems.at[7 * a + k], recv_sem=recv_sems.at[7 * a + k], device_id=to, device_id_type=MESH_ID)

        mine = [pltpu.make_async_copy(x_refs[a], slot(a, *me), local_sems.at[a]) for a in range(n)]
        for cp in mine:
            cp.start()
        first = []
        for a in range(n):
            first.append(copy(a, 0, me, sibling, src=x_refs[a]))
            first += [copy(a, 1 + j, me, (*chip, c), src=x_refs[a]) for j, chip in enumerate(chips)]
        for cp in first:
            cp.start()
        passed = []
        for j, chip in enumerate(chips):
            for a in range(n):
                copy(a, 1 + j, (*chip, c), me).wait_recv()
                cp = copy(a, 4 + j, (*chip, c), sibling)
                cp.start()
                passed.append(cp)
        for a in range(n):
            copy(a, 0, sibling, me).wait_recv()
            for j, chip in enumerate(chips):
                copy(a, 4 + j, (*chip, 1 - c), me).wait_recv()
        for cp in first + passed:
            cp.wait_send()
        for cp in mine:
            cp.wait()

    return pl.pallas_call(
        body, out_shape=[jax.ShapeDtypeStruct((N_DEV,) + b.shape, b.dtype) for b in blocks],
        in_specs=[HBM_SPEC] * n, out_specs=[HBM_SPEC] * n,
        scratch_shapes=[pltpu.SemaphoreType.DMA((7 * n,)), pltpu.SemaphoreType.DMA((7 * n,)), pltpu.SemaphoreType.DMA((n,))],
        name=name)(*blocks)


SEM_SPEC = pl.BlockSpec(memory_space=pltpu.SEMAPHORE)
ANY_SPEC = pl.BlockSpec(memory_space=pl.ANY)
DATAFLOW = pltpu.SideEffectType.DATAFLOW_SIDE_EFFECTING


def _peer(k, x, y, c):
    return (1 - x if (k >> 2) & 1 else x, 1 - y if (k >> 1) & 1 else y, 1 - c if k & 1 else c)


def _own_slot_filled(own_block):
    x, y, c = _position()
    zone = lax.empty((N_DEV,) + own_block.shape, own_block.dtype)
    return lax.dynamic_update_slice_in_dim(zone, own_block[None], 4 * x + 2 * y + c, axis=0)


def _split_start(srcs, scatter, after, *, name):
    n = len(srcs)
    extra = [] if after is None else [after]
    x, y, c = _position()
    me = 4 * x + 2 * y + c
    lands = [_own_slot_filled(lax.dynamic_index_in_dim(s, me, 0, keepdims=False) if scatter else s) for s in srcs]

    def body(*refs):
        src_refs, land_refs = refs[:n], refs[n:2 * n]
        send_sems, recv_sems = refs[2 * n + len(extra)], refs[2 * n + len(extra) + 1]
        token = refs[-1]
        bx, by, bc = _position()
        bme = 4 * bx + 2 * by + bc
        for a in range(n):
            for k in range(1, N_DEV):
                tx, ty, tc = _peer(k, bx, by, bc)
                src = src_refs[a].at[4 * tx + 2 * ty + tc] if scatter else src_refs[a]
                pltpu.make_async_remote_copy(
                    src_ref=src, dst_ref=land_refs[a].at[bme],
                    send_sem=send_sems.at[7 * a + k - 1], recv_sem=recv_sems.at[7 * a + k - 1],
                    device_id=(tx, ty, tc), device_id_type=MESH_ID).start()
        token[...] = jnp.zeros_like(token)

    hbm = lambda a: pltpu.HBM(a.shape, a.dtype)
    outs = pl.pallas_call(
        body, name=name,
        out_shape=(pltpu.SemaphoreType.DMA((7 * n,)), pltpu.SemaphoreType.DMA((7 * n,)),
                   *[hbm(s) for s in srcs], *[hbm(l) for l in lands], jax.ShapeDtypeStruct((8, 128), F32)),
        in_specs=[HBM_SPEC] * (2 * n) + [ANY_SPEC] * len(extra),
        out_specs=(SEM_SPEC, SEM_SPEC, *[HBM_SPEC] * (2 * n), pl.BlockSpec(memory_space=pltpu.VMEM)),
        input_output_aliases={i: 2 + i for i in range(2 * n)},
        compiler_params=pltpu.CompilerParams(has_side_effects=DATAFLOW),
    )(*[pltpu.with_memory_space_constraint(s, pltpu.HBM) for s in srcs],
      *[pltpu.with_memory_space_constraint(l, pltpu.HBM) for l in lands], *extra)
    return {"n": n, "scatter": scatter, "send": outs[0], "recv": outs[1], "srcs": outs[2:2 + n],
            "lands": outs[2 + n:2 + 2 * n], "token": outs[-1]}


def _split_wait(handle, after, *, name):
    n, scatter = handle["n"], handle["scatter"]

    def body(*refs):
        src_refs, land_refs = refs[:n], refs[n:2 * n]
        send_sems, recv_sems = refs[2 * n], refs[2 * n + 1]
        bx, by, bc = _position()
        for a in range(n):
            for k in range(1, N_DEV):
                src = src_refs[a].at[0] if scatter else src_refs[a]
                cp = pltpu.make_async_remote_copy(
                    src_ref=src, dst_ref=land_refs[a].at[0],
                    send_sem=send_sems.at[7 * a + k - 1], recv_sem=recv_sems.at[7 * a + k - 1],
                    device_id=_peer(k, bx, by, bc), device_id_type=MESH_ID)
                cp.wait_send()
                cp.wait_recv()

    hbm = lambda a: pltpu.HBM(a.shape, a.dtype)
    outs = pl.pallas_call(
        body, name=name,
        out_shape=(*[hbm(s) for s in handle["srcs"]], *[hbm(l) for l in handle["lands"]]),
        in_specs=[HBM_SPEC] * (2 * n) + [SEM_SPEC, SEM_SPEC, ANY_SPEC],
        out_specs=tuple([HBM_SPEC] * (2 * n)),
        input_output_aliases={i: i for i in range(2 * n)},
        compiler_params=pltpu.CompilerParams(has_side_effects=DATAFLOW),
    )(*handle["srcs"], *handle["lands"], handle["send"], handle["recv"], after)
    return list(outs[n:])


def _sum_sources(parts, *, tr, name):
    n, R, C = parts.shape

    def body(p_ref, o_ref):
        acc = p_ref[0].astype(F32)
        for s in range(1, n):
            acc = acc + p_ref[s].astype(F32)
        o_ref[...] = acc

    return pl.pallas_call(
        body, grid=(R // tr,), in_specs=[pl.BlockSpec((n, tr, C), lambda i: (0, i, 0))],
        out_specs=pl.BlockSpec((tr, C), lambda i: (i, 0)),
        out_shape=jax.ShapeDtypeStruct((R, C), F32), compiler_params=_cp("parallel"), name=name)(parts)


def _adamw_math(g, w, m, v):
    c1 = 1.0 - ADAM_B1 ** ADAM_STEP
    c2 = 1.0 - ADAM_B2 ** ADAM_STEP
    nm = ADAM_B1 * m + (1.0 - ADAM_B1) * g
    nv = ADAM_B2 * v + (1.0 - ADAM_B2) * (g * g)
    return -ADAM_LR * ((nm / c1) / (jnp.sqrt(nv / c2) + ADAM_EPS) + ADAM_WD * w), nm, nv


def _reduce_adamw(received, w, m, v, *, col, tr, name):
    L, R, C = w.shape

    def body(*refs):
        p_refs = refs[:L]
        w_ref, m_ref, v_ref, g_ref, d_ref, nm_ref, nv_ref = refs[L:]
        for l in range(L):
            @pl.when(pl.program_id(0) == l)
            def _(p_ref=p_refs[l]):
                acc = p_ref[0].astype(F32)
                for s in range(1, N_DEV):
                    acc = acc + p_ref[s].astype(F32)
                g = acc.T if col else acc
                g_ref[...] = g
                d_ref[...], nm_ref[...], nv_ref[...] = _adamw_math(g, w_ref[...], m_ref[...], v_ref[...])

    p_spec = (pl.BlockSpec((N_DEV, C, tr), lambda l, i: (0, 0, i)) if col
              else pl.BlockSpec((N_DEV, tr, C), lambda l, i: (0, i, 0)))
    blk = pl.BlockSpec((None, tr, C), lambda l, i: (l, i, 0))
    sh = jax.ShapeDtypeStruct((L, R, C), F32)
    return pl.pallas_call(
        body, grid=(L, R // tr), in_specs=[p_spec] * L + [blk] * 3, out_specs=[blk] * 4, out_shape=[sh] * 4,
        compiler_params=_cp("parallel", "parallel"), name=name)(*received, w, m, v)


def _adamw(g, w, m, v, *, tr, name):
    L, R, C = w.shape

    def body(g_ref, w_ref, m_ref, v_ref, d_ref, nm_ref, nv_ref):
        d_ref[...], nm_ref[...], nv_ref[...] = _adamw_math(g_ref[...], w_ref[...], m_ref[...], v_ref[...])

    blk = pl.BlockSpec((None, tr, C), lambda l, i: (l, i, 0))
    sh = jax.ShapeDtypeStruct((L, R, C), F32)
    return pl.pallas_call(
        body, grid=(L, R // tr), in_specs=[blk] * 4, out_specs=[blk] * 3, out_shape=[sh] * 3,
        compiler_params=_cp("parallel", "parallel"), name=name)(g, w, m, v)


UNITS = {
    "a_w_in": ("a_w_in", 0, True), "w_mem_kv0": ("w_mem_kv", 0, False), "w_out0": ("w_out", 0, False),
    "w_gate_up0": ("w_gate_up", 0, True), "w_down0": ("w_down", 0, False), "w_kv": ("w_kv", None, True),
    "b_w_in": ("b_w_in", 0, True), "w_mem_kv1": ("w_mem_kv", 1, False), "w_out1": ("w_out", 1, False),
    "w_gate_up1": ("w_gate_up", 1, True), "w_down1": ("w_down", 1, False),
}
BIG = ("a_w_in", "b_w_in", "w_kv", "w_mem_kv", "w_out", "w_gate_up", "w_down")
ADAMW_ROW_TILE = {"a_w_in": 256, "b_w_in": 256, "w_kv": 256, "w_mem_kv": 128, "w_out": 128, "w_gate_up": 176, "w_down": 176}
TRANSPOSED_UPDATE = ("w_gate_up",)


def _wire_block(weights, unit):
    name, layer, col = UNITS[unit]
    a = weights[name] if layer is None else weights[name][layer]
    return (a.T if col else a).astype(BF16)


SMALL_REPLICATED = ("norm_mix", "norm_ffn", "b_qnorm", "kv_norm", "b_knorm", "mem_norm", "mem_qnorm", "mem_knorm")
SMALL_SHARDED = ("a_lb_logits", "a_onorm")
SMALL_ORDER = SMALL_REPLICATED + SMALL_SHARDED
LANES = 128


def _prod(shape):
    n = 1
    for s in shape:
        n *= s
    return n


def _pack_flat(arrays, rows, cols, dtype):
    flat = jnp.concatenate([a.reshape(-1).astype(dtype) for a in arrays])
    return jnp.pad(flat, (0, rows * cols - flat.shape[0])).reshape(rows, cols)


def _unpack_flat(packed, shapes):
    flat = packed.reshape(-1)
    out, off = [], 0
    for s in shapes:
        out.append(flat[off:off + _prod(s)].reshape(s))
        off += _prod(s)
    return out


def kernel(x, mem, norm_mix, norm_ffn, a_w_in, a_lb_logits, a_onorm, b_w_in, b_qnorm, kv_norm, w_kv, b_knorm, mem_norm, w_mem_kv, mem_qnorm, mem_knorm, w_out, w_gate_up, w_down, loss_target, m_norm_mix, m_norm_ffn, m_a_w_in, m_a_lb_logits, m_a_onorm, m_b_w_in, m_b_qnorm, m_kv_norm, m_w_kv, m_b_knorm, m_mem_norm, m_w_mem_kv, m_mem_qnorm, m_mem_knorm, m_w_out, m_w_gate_up, m_w_down, v_norm_mix, v_norm_ffn, v_a_w_in, v_a_lb_logits, v_a_onorm, v_b_w_in, v_b_qnorm, v_kv_norm, v_w_kv, v_b_knorm, v_mem_norm, v_w_mem_kv, v_mem_qnorm, v_mem_knorm, v_w_out, v_w_gate_up, v_w_down):
    names = ("norm_mix", "norm_ffn", "a_w_in", "a_lb_logits", "a_onorm", "b_w_in", "b_qnorm", "kv_norm", "w_kv", "b_knorm",
             "mem_norm", "w_mem_kv", "mem_qnorm", "mem_knorm", "w_out", "w_gate_up", "w_down")
    w = dict(zip(names, (norm_mix, norm_ffn, a_w_in, a_lb_logits, a_onorm, b_w_in, b_qnorm, kv_norm, w_kv, b_knorm,
                         mem_norm, w_mem_kv, mem_qnorm, mem_knorm, w_out, w_gate_up, w_down)))
    m = dict(zip(names, (m_norm_mix, m_norm_ffn, m_a_w_in, m_a_lb_logits, m_a_onorm, m_b_w_in, m_b_qnorm, m_kv_norm, m_w_kv,
                         m_b_knorm, m_mem_norm, m_w_mem_kv, m_mem_qnorm, m_mem_knorm, m_w_out, m_w_gate_up, m_w_down)))
    v = dict(zip(names, (v_norm_mix, v_norm_ffn, v_a_w_in, v_a_lb_logits, v_a_onorm, v_b_w_in, v_b_qnorm, v_kv_norm, v_w_kv,
                         v_b_knorm, v_mem_norm, v_w_mem_kv, v_mem_qnorm, v_mem_knorm, v_w_out, v_w_gate_up, v_w_down)))

    first = ["a_w_in", "w_mem_kv0"]
    gathered = _all_gather([_wire_block(w, u) for u in first] + [_pack_flat([a_lb_logits, a_onorm], 8, LANES, F32)],
                           name="gather_first")
    full = {u: g.reshape(-1, g.shape[-1]) for u, g in zip(first, gathered)}
    small_in = gathered[-1].reshape(N_DEV, -1)
    P = {n: w[n] for n in SMALL_REPLICATED}
    P["a_lb_logits"] = small_in[:, :192].reshape(N_DEV, 2, 96).transpose(1, 0, 2).reshape(2, A_WIDTH)
    P["a_onorm"] = small_in[:, 192:288].reshape(1, A_WIDTH)
    later = [["w_out0", "w_gate_up0"], ["w_down0", "w_kv"], ["b_w_in", "w_mem_kv1"], ["w_out1", "w_gate_up1", "w_down1"]]
    pending = {}
    token = gathered[-1]
    for i, group in enumerate(later):
        handle = _split_start([_wire_block(w, u) for u in group], False, token, name=f"gather{i}_start")
        token = handle["token"]
        for u in group:
            pending[u] = (i, group, handle)

    def get_w(unit, after):
        if unit not in full:
            i, group, handle = pending[unit]
            for u, land in zip(group, _split_wait(handle, after, name=f"gather{i}_wait")):
                full[u] = land.reshape(-1, land.shape[-1])
        return full[unit]

    sent = []

    def put_g(group):
        units = list(group)
        handle = _split_start([group[u].reshape(N_DEV, -1, group[u].shape[-1]) for u in units], True, None,
                              name=f"scatter{len(sent)}_start")
        sent.append((units, handle))
        return handle["token"]

    sq, gx, gP = _local_step(x[0], mem[0], loss_target[0], get_w, P, put_g, first_dep=token)
    loss = lax.psum(0.5 * jnp.sum(sq) / D_MODEL, ("x", "y", "c"))

    received = {}
    for i, (units, handle) in enumerate(sent):
        received.update(zip(units, _split_wait(handle, gx, name=f"scatter{i}_wait")))
    out = {"grad": {}, "delta": {}, "new_m": {}, "new_v": {}}
    for n in BIG:
        shape = w[n].shape
        as3 = lambda a: a.reshape((-1,) + shape[-2:])
        mine = [u for u, (wn, _, _) in UNITS.items() if wn == n]
        col = UNITS[mine[0]][2]
        flip = (lambda a: jnp.swapaxes(a, 1, 2)) if n in TRANSPOSED_UPDATE else (lambda a: a)
        res = _reduce_adamw([received[u] for u in mine], flip(as3(w[n])), flip(as3(m[n])), flip(as3(v[n])),
                            col=col and n not in TRANSPOSED_UPDATE, tr=ADAMW_ROW_TILE[n], name=f"adamw_{n}")
        for kind, r in zip(("grad", "delta", "new_m", "new_v"), res):
            out[kind][n] = flip(r).reshape(shape)

    full_shapes = [(2, A_WIDTH) if n == "a_lb_logits" else (1, A_WIDTH) if n == "a_onorm" else w[n].shape for n in SMALL_ORDER]
    n_small = sum(_prod(s) for s in full_shapes)
    rows_small = -(-n_small // (8 * LANES)) * 8
    g_all, = _all_gather([_pack_flat([gP[n] for n in SMALL_ORDER], rows_small, LANES, F32)], name="gather_small_grads")
    g_small = dict(zip(SMALL_ORDER, _unpack_flat(_sum_sources(g_all, tr=rows_small, name="sum_small_grads"), full_shapes)))
    me = 4 * lax.axis_index("x") + 2 * lax.axis_index("y") + lax.axis_index("c")
    for n in SMALL_SHARDED:
        g_small[n] = lax.dynamic_slice_in_dim(g_small[n], me * 96, 96, axis=1)
    shapes = [w[n].shape for n in SMALL_ORDER]
    rows_upd = -(-sum(_prod(s) for s in shapes) // (8 * LANES)) * 8
    pk = lambda d: _pack_flat([d[n] for n in SMALL_ORDER], rows_upd, LANES, F32)
    res = _adamw(pk(g_small)[None], pk(w)[None], pk(m)[None], pk(v)[None], tr=rows_upd, name="adamw_small")
    out["grad"].update(g_small)
    for kind, packed in zip(("delta", "new_m", "new_v"), res):
        out[kind].update(zip(SMALL_ORDER, _unpack_flat(packed[0], shapes)))

    return (loss, gx[None], *[out["grad"][n] for n in names], *[out["delta"][n] for n in names],
            *[out["new_m"][n] for n in names], *[out["new_v"][n] for n in names])
```

```python
import functools

import jax
import jax.numpy as jnp
from jax import lax
from jax.experimental import pallas as pl
from jax.experimental.pallas import tpu as pltpu

F32 = jnp.float32
BF16 = jnp.bfloat16

N_DEV = 8
D_MODEL = 1024
HEAD_DIM = 128
A_HEADS = 6
A_WIDTH = A_HEADS * HEAD_DIM
CHUNK = 64
B_HEADS = 6
B_WIDTH = B_HEADS * HEAD_DIM
DILATIONS = (1, 4, 16)
SPAN = 128
N_GROUPS = 3
ROPE_THETA = 10000.0
MEM_TOKENS = 256
MEM_HEADS = 4
MEM_HEAD_DIM = 64
MEM_WIDTH = MEM_HEADS * MEM_HEAD_DIM
FFN_HIDDEN = 2816
EPS = 1e-6

ADAM_LR = 0.001
ADAM_B1 = 0.9
ADAM_B2 = 0.999
ADAM_EPS = 1e-08
ADAM_WD = 0.01
ADAM_STEP = 10

V7X_VMEM_LIMIT_BYTES = 56 * 1024 * 1024

NT_DIMS = (((1,), (1,)), ((), ()))
TN_DIMS = (((0,), (0,)), ((), ()))


def _cp(*sem):
    return pltpu.CompilerParams(dimension_semantics=sem, vmem_limit_bytes=V7X_VMEM_LIMIT_BYTES)


def _dot(a, b):
    return jnp.dot(a.astype(BF16), b.astype(BF16), preferred_element_type=F32)


def _dot_nt(a, b):
    return lax.dot_general(a.astype(BF16), b.astype(BF16), NT_DIMS, preferred_element_type=F32)


def _dot_tn(a, b):
    return lax.dot_general(a.astype(BF16), b.astype(BF16), TN_DIMS, preferred_element_type=F32)


def _dot3(m01, x):
    hi = x.astype(BF16)
    r1 = x - hi.astype(F32)
    mid = r1.astype(BF16)
    lo = (r1 - mid.astype(F32)).astype(BF16)
    d = functools.partial(jnp.dot, preferred_element_type=F32)
    return d(m01, hi) + d(m01, mid) + d(m01, lo)


def _sigmoid(x):
    return 1.0 / (1.0 + jnp.exp(-x))


def _full(shape):
    return pl.BlockSpec(shape, lambda *_: (0,) * len(shape))


def _dep(body, n_in, dep):
    if dep is None:
        return body, [], []

    def with_dep(*refs):
        return body(*refs[:n_in], *refs[n_in + 1:])

    return with_dep, [pl.BlockSpec(memory_space=pl.ANY)], [dep]


def _rms_matmul(x, g, w, *, tt, tn, wt, name, dep=None):
    T, K = x.shape
    N = w.shape[0] if wt else w.shape[1]

    def kernel_body(x_ref, g_ref, w_ref, y_ref, xn_ref):
        @pl.when(pl.program_id(1) == 0)
        def _():
            xf = x_ref[...]
            r = lax.rsqrt(jnp.mean(xf * xf, axis=-1, keepdims=True) + EPS)
            xn_ref[...] = (xf * r * g_ref[...]).astype(BF16)

        y_ref[...] = (_dot_nt if wt else _dot)(xn_ref[...], w_ref[...])

    w_spec = pl.BlockSpec((tn, K), lambda i, j: (j, 0)) if wt else pl.BlockSpec((K, tn), lambda i, j: (0, j))
    body, dep_specs, dep_args = _dep(kernel_body, 3, dep)
    return pl.pallas_call(
        body, grid=(T // tt, N // tn),
        in_specs=[pl.BlockSpec((tt, K), lambda i, j: (i, 0)), _full((1, K)), w_spec] + dep_specs,
        out_specs=[pl.BlockSpec((tt, tn), lambda i, j: (i, j)), pl.BlockSpec((tt, K), lambda i, j: (i, 0))],
        out_shape=[jax.ShapeDtypeStruct((T, N), F32), jax.ShapeDtypeStruct((T, K), BF16)],
        compiler_params=_cp("parallel", "arbitrary"), name=name)(x, g, w, *dep_args)


def _mm_res(res, a, w, *, tt, name):
    T, K = a.shape
    N = w.shape[1]

    def body(r_ref, a_ref, w_ref, o_ref):
        o_ref[...] = r_ref[...] + _dot(a_ref[...], w_ref[...])

    return pl.pallas_call(
        body, grid=(T // tt,),
        in_specs=[pl.BlockSpec((tt, N), lambda i: (i, 0)), pl.BlockSpec((tt, K), lambda i: (i, 0)), _full((K, N))],
        out_specs=pl.BlockSpec((tt, N), lambda i: (i, 0)),
        out_shape=jax.ShapeDtypeStruct((T, N), F32),
        compiler_params=_cp("parallel"), name=name)(res, a, w)


def _swiglu_down(h, gu, wd, *, tt, name):
    T, D = h.shape
    Fh = wd.shape[0]

    def body(h_ref, gt_ref, up_ref, w_ref, o_ref):
        gt = gt_ref[...]
        act = gt * _sigmoid(gt) * up_ref[...]
        o_ref[...] = h_ref[...] + _dot(act, w_ref[...])

    return pl.pallas_call(
        body, grid=(T // tt,),
        in_specs=[pl.BlockSpec((tt, D), lambda i: (i, 0)), pl.BlockSpec((tt, Fh), lambda i: (i, 0)),
                  pl.BlockSpec((tt, Fh), lambda i: (i, 1)), _full((Fh, D))],
        out_specs=pl.BlockSpec((tt, D), lambda i: (i, 0)),
        out_shape=jax.ShapeDtypeStruct((T, D), F32),
        compiler_params=_cp("parallel"), name=name)(h, gu, gu, wd)


def _swiglu_bwd(dh, gu, wd_t, *, tt, name):
    T, D = dh.shape
    Fh = wd_t.shape[1]

    def body(dh_ref, gt_ref, up_ref, w_ref, dgu_ref, act_ref):
        gt = gt_ref[...]
        up = up_ref[...]
        s = _sigmoid(gt)
        silu = gt * s
        dact = _dot(dh_ref[...], w_ref[...])
        act_ref[...] = (silu * up).astype(BF16)
        dgu_ref[:, :Fh] = (dact * up * (s * (1.0 + gt * (1.0 - s)))).astype(BF16)
        dgu_ref[:, Fh:] = (dact * silu).astype(BF16)

    return pl.pallas_call(
        body, grid=(T // tt,),
        in_specs=[pl.BlockSpec((tt, D), lambda i: (i, 0)), pl.BlockSpec((tt, Fh), lambda i: (i, 0)),
                  pl.BlockSpec((tt, Fh), lambda i: (i, 1)), _full((D, Fh))],
        out_specs=[pl.BlockSpec((tt, 2 * Fh), lambda i: (i, 0)), pl.BlockSpec((tt, Fh), lambda i: (i, 0))],
        out_shape=[jax.ShapeDtypeStruct((T, 2 * Fh), BF16), jax.ShapeDtypeStruct((T, Fh), BF16)],
        compiler_params=_cp("parallel"), name=name)(dh, gu, gu, wd_t)


def _mm(a, w, *, tt, name):
    T, N = a.shape
    K = w.shape[1]

    def body(a_ref, w_ref, o_ref):
        o_ref[...] = _dot(a_ref[...], w_ref[...])

    return pl.pallas_call(
        body, grid=(T // tt,),
        in_specs=[pl.BlockSpec((tt, N), lambda i: (i, 0)), _full((N, K))],
        out_specs=pl.BlockSpec((tt, K), lambda i: (i, 0)),
        out_shape=jax.ShapeDtypeStruct((T, K), F32),
        compiler_params=_cp("parallel"), name=name)(a, w)


def _mm_tn(a, b, *, tt, tka, name):
    T, Ka = a.shape
    N = b.shape[1]
    last = T // tt - 1

    def body(a_ref, b_ref, o_ref, acc):
        @pl.when(pl.program_id(1) == 0)
        def _():
            acc[...] = jnp.zeros_like(acc)

        acc[...] += _dot_tn(a_ref[...], b_ref[...])

        @pl.when(pl.program_id(1) == last)
        def _():
            o_ref[...] = acc[...].astype(BF16)

    return pl.pallas_call(
        body, grid=(Ka // tka, T // tt),
        in_specs=[pl.BlockSpec((tt, tka), lambda j, t: (t, j)), pl.BlockSpec((tt, N), lambda j, t: (t, 0))],
        out_specs=pl.BlockSpec((tka, N), lambda j, t: (j, 0)),
        out_shape=jax.ShapeDtypeStruct((Ka, N), BF16),
        scratch_shapes=[pltpu.VMEM((tka, N), F32)],
        compiler_params=_cp("parallel", "arbitrary"), name=name)(a, b)


def _rms_bwd_dx(x, g, w, dy, dres, *, tt, wt, name, dep=None):
    T, K = x.shape
    N = w.shape[0] if wt else w.shape[1]

    def kernel_body(x_ref, g_ref, w_ref, dy_ref, dres_ref, dx_ref, dg_ref):
        @pl.when(pl.program_id(0) == 0)
        def _():
            dg_ref[...] = jnp.zeros_like(dg_ref)

        dxn = (_dot if wt else _dot_nt)(dy_ref[...], w_ref[...])
        xf = x_ref[...]
        r = lax.rsqrt(jnp.mean(xf * xf, axis=-1, keepdims=True) + EPS)
        xhat = xf * r
        dg_ref[...] += jnp.sum(dxn * xhat, axis=0, keepdims=True)
        dxhat = dxn * g_ref[...]
        dx_ref[...] = dres_ref[...] + r * (dxhat - xhat * jnp.mean(dxhat * xhat, axis=-1, keepdims=True))

    body, dep_specs, dep_args = _dep(kernel_body, 5, dep)
    return pl.pallas_call(
        body, grid=(T // tt,),
        in_specs=[pl.BlockSpec((tt, K), lambda i: (i, 0)), _full((1, K)), _full(w.shape),
                  pl.BlockSpec((tt, N), lambda i: (i, 0)), pl.BlockSpec((tt, K), lambda i: (i, 0))] + dep_specs,
        out_specs=[pl.BlockSpec((tt, K), lambda i: (i, 0)), _full((1, K))],
        out_shape=[jax.ShapeDtypeStruct((T, K), F32), jax.ShapeDtypeStruct((1, K), F32)],
        compiler_params=_cp("arbitrary"), name=name)(x, g, w, dy, dres, *dep_args)


def _loss_kernel(y, tgt, *, tt, name):
    T, D = y.shape

    def body(y_ref, t_ref, dy_ref, acc_ref):
        @pl.when(pl.program_id(0) == 0)
        def _():
            acc_ref[...] = jnp.zeros_like(acc_ref)

        e = y_ref[...] - t_ref[...]
        dy_ref[...] = e * (1.0 / D)
        acc_ref[...] += jnp.sum(e * e, axis=0, keepdims=True)

    return pl.pallas_call(
        body, grid=(T // tt,),
        in_specs=[pl.BlockSpec((tt, D), lambda i: (i, 0)), pl.BlockSpec((tt, D), lambda i: (i, 0))],
        out_specs=[pl.BlockSpec((tt, D), lambda i: (i, 0)), _full((1, D))],
        out_shape=[jax.ShapeDtypeStruct((T, D), F32), jax.ShapeDtypeStruct((1, D), F32)],
        compiler_params=_cp("arbitrary"), name=name)(y, tgt)


HGRN_TB = 512
HGRN_NCH = HGRN_TB // CHUNK
HGRN_HPB = 6


def _hgrn_chunk_fwd(q, z, lbv, tril01):
    sig = _sigmoid(z)
    f = lbv + (1.0 - lbv) * sig
    kk = 1.0 - f
    b = _dot3(tril01, jnp.log(f))
    bend = b[CHUNK - 1:CHUNK, :]
    sq = _sigmoid(q)
    eb = jnp.exp(b)
    emb = jnp.exp(-b)
    eo = jnp.exp(bend - b)
    dec = jnp.exp(bend)
    return sig, f, kk, sq, eb, emb, eo, dec


def _hgrn2_fwd(proj, lb, *, name):
    T = proj.shape[0]
    nT = T // HGRN_TB
    nC = T // CHUNK

    def body(q_ref, z_ref, v_ref, lb_ref, o_ref, st_ref, state):
        @pl.when(pl.program_id(1) == 0)
        def _():
            state[...] = jnp.zeros_like(state)

        row = lax.broadcasted_iota(jnp.int32, (CHUNK, CHUNK), 0)
        col = lax.broadcasted_iota(jnp.int32, (CHUNK, CHUNK), 1)
        causal = row >= col
        tril01 = causal.astype(BF16)

        def chunk(c, carry):
            rows = pl.ds(pl.multiple_of(c * CHUNK, CHUNK), CHUNK)
            for hh in range(HGRN_HPB):
                sl = slice(hh * HEAD_DIM, (hh + 1) * HEAD_DIM)
                q = q_ref[rows, sl]
                v = v_ref[rows, sl].astype(BF16)
                sig, f, kk, sq, eb, emb, eo, dec = _hgrn_chunk_fwd(q, z_ref[rows, sl], lb_ref[:, sl], tril01)
                qi = (q * sq * eb).astype(BF16)
                ki = (kk * emb).astype(BF16)
                ko = (kk * eo).astype(BF16)
                st = state[hh]
                att = jnp.where(causal, _dot_nt(qi, ki), 0.0)
                o_ref[rows, sl] = _dot(att, v) + _dot_nt(qi, st)
                st_ref[c, hh] = st
                state[hh] = st * dec + _dot_tn(v, ko)
            return carry

        lax.fori_loop(0, HGRN_NCH, chunk, 0)

    W = HGRN_HPB * HEAD_DIM
    nG = A_HEADS // HGRN_HPB
    hb = lambda off: pl.BlockSpec((HGRN_TB, W), lambda h, i: (i, off + h))
    return pl.pallas_call(
        body, grid=(nG, nT),
        in_specs=[hb(0), hb(nG), hb(2 * nG), pl.BlockSpec((1, W), lambda h, i: (0, h))],
        out_specs=[hb(0), pl.BlockSpec((HGRN_NCH, HGRN_HPB, HEAD_DIM, HEAD_DIM), lambda h, i: (i, h, 0, 0))],
        out_shape=[jax.ShapeDtypeStruct((T, A_WIDTH), F32), jax.ShapeDtypeStruct((nC, A_HEADS, HEAD_DIM, HEAD_DIM), F32)],
        scratch_shapes=[pltpu.VMEM((HGRN_HPB, HEAD_DIM, HEAD_DIM), F32)],
        compiler_params=_cp("parallel", "arbitrary"), name=name)(proj, proj, proj, lb)


def _hgrn2_bwd(proj, lb, st_all, do, *, name):
    T = proj.shape[0]
    nT = T // HGRN_TB

    def body(q_ref, z_ref, v_ref, lb_ref, st_ref, do_ref, dq_ref, dz_ref, dv_ref, dlb_ref, dstate):
        @pl.when(pl.program_id(1) == 0)
        def _():
            dstate[...] = jnp.zeros_like(dstate)
            dlb_ref[...] = jnp.zeros_like(dlb_ref)

        row = lax.broadcasted_iota(jnp.int32, (CHUNK, CHUNK), 0)
        col = lax.broadcasted_iota(jnp.int32, (CHUNK, CHUNK), 1)
        causal = row >= col
        tril01 = causal.astype(BF16)
        triu01 = (row <= col).astype(BF16)

        def chunk(cc, carry):
            c = HGRN_NCH - 1 - cc
            rows = pl.ds(pl.multiple_of(c * CHUNK, CHUNK), CHUNK)
            for hh in range(HGRN_HPB):
                sl = slice(hh * HEAD_DIM, (hh + 1) * HEAD_DIM)
                lbv = lb_ref[:, sl]
                q = q_ref[rows, sl]
                v = v_ref[rows, sl].astype(BF16)
                sig, f, kk, sq, eb, emb, eo, dec = _hgrn_chunk_fwd(q, z_ref[rows, sl], lbv, tril01)
                qi32 = q * sq * eb
                ki32 = kk * emb
                ko32 = kk * eo
                qi, ki, ko = qi32.astype(BF16), ki32.astype(BF16), ko32.astype(BF16)
                att = jnp.where(causal, _dot_nt(qi, ki), 0.0).astype(BF16)
                dout = do_ref[rows, sl].astype(BF16)
                st = st_ref[c, hh]
                dst = dstate[hh]
                dst16 = dst.astype(BF16)
                datt = jnp.where(causal, _dot_nt(dout, v), 0.0).astype(BF16)
                dqi = _dot(datt, ki) + _dot(dout, st)
                dki = _dot_tn(datt, qi)
                dv_ref[rows, sl] = (_dot_tn(att, dout) + _dot_nt(ko, dst16)).astype(BF16)
                dko = _dot(v, dst16)
                ddec = jnp.sum(dst * st, axis=0, keepdims=True)
                dstate[hh] = dst * dec + _dot_tn(dout, qi)
                dkk = dki * emb + dko * eo
                db = dqi * qi32 - dki * ki32 - dko * ko32
                dbend = jnp.sum(dko * ko32, axis=0, keepdims=True) + ddec * dec
                dlogf = _dot3(triu01, db) + dbend
                df = dlogf / f - dkk
                dz_ref[rows, sl] = (df * (1.0 - lbv) * sig * (1.0 - sig)).astype(BF16)
                dlb_ref[:, sl] += jnp.sum(df * (1.0 - sig), axis=0, keepdims=True)
                dq_ref[rows, sl] = (dqi * eb * (sq * (1.0 + q * (1.0 - sq)))).astype(BF16)
            return carry

        lax.fori_loop(0, HGRN_NCH, chunk, 0)

    W = HGRN_HPB * HEAD_DIM
    nG = A_HEADS // HGRN_HPB
    hb = lambda off: pl.BlockSpec((HGRN_TB, W), lambda h, i: (nT - 1 - i, off + h))
    hlb = pl.BlockSpec((1, W), lambda h, i: (0, h))
    o16 = jax.ShapeDtypeStruct((T, A_WIDTH), BF16)
    return pl.pallas_call(
        body, grid=(nG, nT),
        in_specs=[hb(0), hb(nG), hb(2 * nG), hlb,
                  pl.BlockSpec((HGRN_NCH, HGRN_HPB, HEAD_DIM, HEAD_DIM), lambda h, i: (nT - 1 - i, h, 0, 0)), hb(0)],
        out_specs=[hb(0), hb(0), hb(0), hlb],
        out_shape=[o16, o16, o16, jax.ShapeDtypeStruct((1, A_WIDTH), F32)],
        scratch_shapes=[pltpu.VMEM((HGRN_HPB, HEAD_DIM, HEAD_DIM), F32)],
        compiler_params=_cp("parallel", "arbitrary"), name=name)(proj, proj, proj, lb, st_all, do)


def _head_rms(x):
    r = lax.rsqrt(jnp.mean(x * x, axis=-1, keepdims=True) + EPS)
    return x * r, r


def _head_rms_bwd(dxhat, xhat, r):
    return r * (dxhat - xhat * jnp.mean(dxhat * xhat, axis=-1, keepdims=True))


def _a_post_fwd(o, proj, onorm, *, tt, name):
    T = o.shape[0]

    def body(o_ref, g_ref, w_ref, y_ref):
        for h in range(A_HEADS):
            sl = slice(h * HEAD_DIM, (h + 1) * HEAD_DIM)
            xhat, _ = _head_rms(o_ref[:, sl])
            g = g_ref[:, sl]
            y_ref[:, sl] = xhat * w_ref[:, sl] * (g * _sigmoid(g))

    blk = lambda c: pl.BlockSpec((tt, A_WIDTH), lambda i: (i, c))
    return pl.pallas_call(
        body, grid=(T // tt,), in_specs=[blk(0), blk(3), _full((1, A_WIDTH))], out_specs=blk(0),
        out_shape=jax.ShapeDtypeStruct((T, A_WIDTH), F32),
        compiler_params=_cp("parallel"), name=name)(o, proj, onorm)


def _a_post_bwd(o, proj, onorm, dmix, *, tt, name, dep=None):
    T = o.shape[0]

    def kernel_body(o_ref, g_ref, w_ref, dy_ref, do_ref, dg_ref, dw_ref):
        @pl.when(pl.program_id(0) == 0)
        def _():
            dw_ref[...] = jnp.zeros_like(dw_ref)

        for h in range(A_HEADS):
            sl = slice(h * HEAD_DIM, (h + 1) * HEAD_DIM)
            xhat, r = _head_rms(o_ref[:, sl])
            g = g_ref[:, sl]
            s = _sigmoid(g)
            dy = dy_ref[:, sl]
            w = w_ref[:, sl]
            dg_ref[:, sl] = (dy * xhat * w * (s * (1.0 + g * (1.0 - s)))).astype(BF16)
            dyn = dy * (g * s)
            dw_ref[:, sl] += jnp.sum(dyn * xhat, axis=0, keepdims=True)
            do_ref[:, sl] = _head_rms_bwd(dyn * w, xhat, r)

    blk = lambda c: pl.BlockSpec((tt, A_WIDTH), lambda i: (i, c))
    body, dep_specs, dep_args = _dep(kernel_body, 4, dep)
    return pl.pallas_call(
        body, grid=(T // tt,), in_specs=[blk(0), blk(3), _full((1, A_WIDTH)), blk(0)] + dep_specs,
        out_specs=[blk(0), blk(0), _full((1, A_WIDTH))],
        out_shape=[jax.ShapeDtypeStruct((T, A_WIDTH), F32), jax.ShapeDtypeStruct((T, A_WIDTH), BF16),
                   jax.ShapeDtypeStruct((1, A_WIDTH), F32)],
        compiler_params=_cp("arbitrary"), name=name)(o, proj, onorm, dmix, *dep_args)


def _mem_head_masks(n):
    lane = lax.broadcasted_iota(jnp.int32, (n, MEM_WIDTH), 1)
    return [(lane >= m * MEM_HEAD_DIM) & (lane < (m + 1) * MEM_HEAD_DIM) for m in range(MEM_HEADS)]


def _mem_head_rms(x, masks):
    x2 = x * x
    r = jnp.zeros_like(x)
    for mk in masks:
        ms = jnp.sum(jnp.where(mk, x2, 0.0), axis=-1, keepdims=True) * (1.0 / MEM_HEAD_DIM)
        r = jnp.where(mk, lax.rsqrt(ms + EPS), r)
    return x * r, r


def _mem_head_rms_bwd(dxhat, xhat, r, masks):
    t = dxhat * xhat
    m = jnp.zeros_like(t)
    for mk in masks:
        m = jnp.where(mk, jnp.sum(jnp.where(mk, t, 0.0), axis=-1, keepdims=True) * (1.0 / MEM_HEAD_DIM), m)
    return r * (dxhat - xhat * m)


MEM_SCALE = MEM_HEAD_DIM ** -0.5


def _mem_attn_fwd(proj, qcol, mkv, qn_w, kn_w, *, tt, name):
    T = proj.shape[0]

    def body(q_ref, k_ref, v_ref, qw_ref, kw_ref, o_ref):
        qmasks = _mem_head_masks(tt)
        kmasks = _mem_head_masks(MEM_TOKENS)
        qhat, _ = _mem_head_rms(q_ref[...], qmasks)
        qn = qhat * qw_ref[...]
        khat, _ = _mem_head_rms(k_ref[...], kmasks)
        kn = (khat * kw_ref[...]).astype(BF16)
        v = v_ref[...].astype(BF16)
        out = jnp.zeros((tt, MEM_WIDTH), F32)
        for m in range(MEM_HEADS):
            s = _dot_nt(jnp.where(qmasks[m], qn, 0.0), kn) * MEM_SCALE
            s = s - jnp.max(s, axis=-1, keepdims=True)
            p = jnp.exp(s)
            p = p / jnp.sum(p, axis=-1, keepdims=True)
            out = jnp.where(qmasks[m], _dot(p, v), out)
        o_ref[...] = out

    return pl.pallas_call(
        body, grid=(T // tt,),
        in_specs=[pl.BlockSpec((tt, MEM_WIDTH), lambda i: (i, qcol)), pl.BlockSpec((MEM_TOKENS, MEM_WIDTH), lambda i: (0, 0)),
                  pl.BlockSpec((MEM_TOKENS, MEM_WIDTH), lambda i: (0, 1)), _full((1, MEM_WIDTH)), _full((1, MEM_WIDTH))],
        out_specs=pl.BlockSpec((tt, MEM_WIDTH), lambda i: (i, 0)),
        out_shape=jax.ShapeDtypeStruct((T, MEM_WIDTH), F32),
        compiler_params=_cp("parallel"), name=name)(proj, mkv, mkv, qn_w, kn_w)


def _mem_attn_bwd(proj, qcol, mkv, qn_w, kn_w, dmix, *, tt, name):
    T = proj.shape[0]
    nsteps = T // tt
    ocol = (dmix.shape[1] - MEM_WIDTH) // MEM_WIDTH

    def body(q_ref, k_ref, v_ref, qw_ref, kw_ref, do_ref, dq_ref, dkv_ref, dqw_ref, dkw_ref, dk_acc, dv_acc):
        step = pl.program_id(0)

        @pl.when(step == 0)
        def _():
            dk_acc[...] = jnp.zeros_like(dk_acc)
            dv_acc[...] = jnp.zeros_like(dv_acc)
            dqw_ref[...] = jnp.zeros_like(dqw_ref)

        qmasks = _mem_head_masks(tt)
        kmasks = _mem_head_masks(MEM_TOKENS)
        qhat, qr = _mem_head_rms(q_ref[...], qmasks)
        qn = qhat * qw_ref[...]
        khat, kr = _mem_head_rms(k_ref[...], kmasks)
        kn = (khat * kw_ref[...]).astype(BF16)
        v = v_ref[...].astype(BF16)
        dout = do_ref[...]
        dqn = jnp.zeros((tt, MEM_WIDTH), F32)
        dkn = jnp.zeros((MEM_TOKENS, MEM_WIDTH), F32)
        dvv = jnp.zeros((MEM_TOKENS, MEM_WIDTH), F32)
        for m in range(MEM_HEADS):
            qm = jnp.where(qmasks[m], qn, 0.0).astype(BF16)
            s = _dot_nt(qm, kn) * MEM_SCALE
            s = s - jnp.max(s, axis=-1, keepdims=True)
            p = jnp.exp(s)
            p = p / jnp.sum(p, axis=-1, keepdims=True)
            dom = jnp.where(qmasks[m], dout, 0.0).astype(BF16)
            dp = _dot_nt(dom, v)
            ds = (p * (dp - jnp.sum(p * dp, axis=-1, keepdims=True)) * MEM_SCALE).astype(BF16)
            dqn = jnp.where(qmasks[m], _dot(ds, kn), dqn)
            dkn = jnp.where(kmasks[m], _dot_tn(ds, qm), dkn)
            dvv = jnp.where(kmasks[m], _dot_tn(p, dom), dvv)
        dqw_ref[...] += jnp.sum(dqn * qhat, axis=0, keepdims=True)
        dq_ref[...] = _mem_head_rms_bwd(dqn * qw_ref[...], qhat, qr, qmasks).astype(BF16)
        dk_acc[...] += dkn
        dv_acc[...] += dvv

        @pl.when(step == nsteps - 1)
        def _():
            dk = dk_acc[...]
            dkw_ref[...] = jnp.sum(dk * khat, axis=0, keepdims=True)
            dkv_ref[:, :MEM_WIDTH] = _mem_head_rms_bwd(dk * kw_ref[...], khat, kr, kmasks)
            dkv_ref[:, MEM_WIDTH:] = dv_acc[...]

    return pl.pallas_call(
        body, grid=(nsteps,),
        in_specs=[pl.BlockSpec((tt, MEM_WIDTH), lambda i: (i, qcol)), pl.BlockSpec((MEM_TOKENS, MEM_WIDTH), lambda i: (0, 0)),
                  pl.BlockSpec((MEM_TOKENS, MEM_WIDTH), lambda i: (0, 1)), _full((1, MEM_WIDTH)), _full((1, MEM_WIDTH)),
                  pl.BlockSpec((tt, MEM_WIDTH), lambda i: (i, ocol))],
        out_specs=[pl.BlockSpec((tt, MEM_WIDTH), lambda i: (i, 0)), _full((MEM_TOKENS, 2 * MEM_WIDTH)),
                   _full((1, MEM_WIDTH)), _full((1, MEM_WIDTH))],
        out_shape=[jax.ShapeDtypeStruct((T, MEM_WIDTH), BF16), jax.ShapeDtypeStruct((MEM_TOKENS, 2 * MEM_WIDTH), F32),
                   jax.ShapeDtypeStruct((1, MEM_WIDTH), F32), jax.ShapeDtypeStruct((1, MEM_WIDTH), F32)],
        scratch_shapes=[pltpu.VMEM((MEM_TOKENS, MEM_WIDTH), F32), pltpu.VMEM((MEM_TOKENS, MEM_WIDTH), F32)],
        compiler_params=_cp("arbitrary"), name=name)(proj, mkv, mkv, qn_w, kn_w, dmix)


HALF = HEAD_DIM // 2
ATT_SCALE = HEAD_DIM ** -0.5
NEG = -1e30


def _rope_tables(T):
    inv = ROPE_THETA ** (-jnp.arange(HALF, dtype=F32) / HALF)
    ang = jnp.arange(T, dtype=F32)[:, None] * inv[None, :]
    cos, sin = jnp.cos(ang), jnp.sin(ang)
    return jnp.concatenate([cos, cos], axis=-1), jnp.concatenate([-sin, sin], axis=-1)


def _rope(x, cosf, sinsg):
    return x * cosf + pltpu.roll(x, HALF, 1) * sinsg


def _rope_bwd(dy, cosf, sinsg):
    return dy * cosf + pltpu.roll(dy * sinsg, HALF, 1)


def _headnorm_rope_fwd(x, w_heads, cosf, sinsg, *, col0, n_heads, tt, name):
    T = x.shape[0]
    W = n_heads * HEAD_DIM

    def body(x_ref, w_ref, c_ref, s_ref, y_ref):
        c, s = c_ref[...], s_ref[...]
        for h in range(n_heads):
            sl = slice(h * HEAD_DIM, (h + 1) * HEAD_DIM)
            xhat, _ = _head_rms(x_ref[:, sl])
            y_ref[:, sl] = _rope(xhat * w_ref[:, sl], c, s)

    tbl = pl.BlockSpec((tt, HEAD_DIM), lambda i: (i, 0))
    return pl.pallas_call(
        body, grid=(T // tt,),
        in_specs=[pl.BlockSpec((tt, W), lambda i: (i, col0)), _full((1, W)), tbl, tbl],
        out_specs=pl.BlockSpec((tt, W), lambda i: (i, 0)),
        out_shape=jax.ShapeDtypeStruct((T, W), F32),
        compiler_params=_cp("parallel"), name=name)(x, w_heads, cosf, sinsg)


def _q_prep_bwd(proj, w_heads, cosf, sinsg, dqs, *, tt, name):
    T = proj.shape[0]
    W = N_GROUPS * B_WIDTH

    def body(x_ref, w_ref, c_ref, s_ref, d0, d1, d2, dx_ref, dw_ref):
        @pl.when(pl.program_id(0) == 0)
        def _():
            dw_ref[...] = jnp.zeros_like(dw_ref)

        c, s = c_ref[...], s_ref[...]
        for gi, d_ref in enumerate((d0, d1, d2)):
            for h in range(B_HEADS):
                sl = slice((gi * B_HEADS + h) * HEAD_DIM, (gi * B_HEADS + h + 1) * HEAD_DIM)
                xhat, r = _head_rms(x_ref[:, sl])
                dyn = _rope_bwd(d_ref[:, h * HEAD_DIM:(h + 1) * HEAD_DIM], c, s)
                dw_ref[:, sl] += jnp.sum(dyn * xhat, axis=0, keepdims=True)
                dx_ref[:, sl] = _head_rms_bwd(dyn * w_ref[:, sl], xhat, r).astype(BF16)

    tbl = pl.BlockSpec((tt, HEAD_DIM), lambda i: (i, 0))
    dyb = pl.BlockSpec((tt, B_WIDTH), lambda i: (i, 0))
    return pl.pallas_call(
        body, grid=(T // tt,),
        in_specs=[pl.BlockSpec((tt, W), lambda i: (i, 0)), _full((1, W)), tbl, tbl, dyb, dyb, dyb],
        out_specs=[pl.BlockSpec((tt, W), lambda i: (i, 0)), _full((1, W))],
        out_shape=[jax.ShapeDtypeStruct((T, W), BF16), jax.ShapeDtypeStruct((1, W), F32)],
        compiler_params=_cp("arbitrary"), name=name)(proj, w_heads, cosf, sinsg, *dqs)


def _kv_prep_bwd(kv, w_heads, cosf, sinsg, dks, dvs, *, tt, name):
    T = kv.shape[0]

    def body(x_ref, w_ref, c_ref, s_ref, k0, k1, k2, v0, v1, v2, dx_ref, dw_ref):
        @pl.when(pl.program_id(0) == 0)
        def _():
            dw_ref[...] = jnp.zeros_like(dw_ref)

        c, s = c_ref[...], s_ref[...]
        for h in range(B_HEADS):
            sl = slice(h * HEAD_DIM, (h + 1) * HEAD_DIM)
            vs = slice(B_WIDTH + h * HEAD_DIM, B_WIDTH + (h + 1) * HEAD_DIM)
            xhat, r = _head_rms(x_ref[:, sl])
            dyn = _rope_bwd(k0[:, sl] + k1[:, sl] + k2[:, sl], c, s)
            dw_ref[:, sl] += jnp.sum(dyn * xhat, axis=0, keepdims=True)
            dx_ref[:, sl] = _head_rms_bwd(dyn * w_ref[:, sl], xhat, r).astype(BF16)
            dx_ref[:, vs] = (v0[:, sl] + v1[:, sl] + v2[:, sl]).astype(BF16)

    tbl = pl.BlockSpec((tt, HEAD_DIM), lambda i: (i, 0))
    dyb = pl.BlockSpec((tt, B_WIDTH), lambda i: (i, 0))
    return pl.pallas_call(
        body, grid=(T // tt,),
        in_specs=[dyb, _full((1, B_WIDTH)), tbl, tbl] + [dyb] * 6,
        out_specs=[pl.BlockSpec((tt, 2 * B_WIDTH), lambda i: (i, 0)), _full((1, B_WIDTH))],
        out_shape=[jax.ShapeDtypeStruct((T, 2 * B_WIDTH), BF16), jax.ShapeDtypeStruct((1, B_WIDTH), F32)],
        compiler_params=_cp("arbitrary"), name=name)(kv, w_heads, cosf, sinsg, *dks, *dvs)


def _band_masks(n_is_first=None):
    row = lax.broadcasted_iota(jnp.int32, (SPAN, SPAN), 0)
    col = lax.broadcasted_iota(jnp.int32, (SPAN, SPAN), 1)
    return row >= col, col >= row


def _dil_views(T, d):
    L = T // d
    return L, L // SPAN


def _dil_fwd(qr, kr, kv, gi, d, *, name):
    T = qr.shape[0]
    L, nb = _dil_views(T, d)

    def body(q_ref, kc_ref, kp_ref, vc_ref, vp_ref, o_ref, lse_ref):
        cur_ok, prev_band = _band_masks()
        prev_ok = prev_band & (pl.program_id(1) > 0)
        for h in range(B_HEADS):
            sl = slice(h * HEAD_DIM, (h + 1) * HEAD_DIM)
            q = q_ref[:, sl]
            sc = jnp.where(cur_ok, _dot_nt(q, kc_ref[:, sl]) * ATT_SCALE, NEG)
            sp = jnp.where(prev_ok, _dot_nt(q, kp_ref[:, sl]) * ATT_SCALE, NEG)
            m = jnp.maximum(jnp.max(sc, axis=-1, keepdims=True), jnp.max(sp, axis=-1, keepdims=True))
            pc = jnp.exp(sc - m)
            pp = jnp.exp(sp - m)
            l = jnp.sum(pc, axis=-1, keepdims=True) + jnp.sum(pp, axis=-1, keepdims=True)
            o_ref[:, sl] = (_dot(pc, vc_ref[:, sl]) + _dot(pp, vp_ref[:, sl])) / l
            lse_ref[:, sl] = jnp.broadcast_to(m + jnp.log(l), (SPAN, HEAD_DIM))

    blk = lambda f: pl.BlockSpec((SPAN, B_WIDTH), f)
    cur = lambda r, n: (n, r)
    prev = lambda r, n: (jnp.maximum(n - 1, 0), r)
    ov = jax.ShapeDtypeStruct((L, d * B_WIDTH), F32)
    o, lse = pl.pallas_call(
        body, grid=(d, nb),
        in_specs=[blk(lambda r, n: (n, r * N_GROUPS + gi)), blk(cur), blk(prev),
                  blk(lambda r, n: (n, 2 * r + 1)), blk(lambda r, n: (jnp.maximum(n - 1, 0), 2 * r + 1))],
        out_specs=[blk(cur), blk(cur)], out_shape=[ov, ov],
        compiler_params=_cp("parallel", "arbitrary"), name=name,
    )(qr.reshape(L, d * N_GROUPS * B_WIDTH), kr.reshape(L, d * B_WIDTH), kr.reshape(L, d * B_WIDTH),
      kv.reshape(L, d * 2 * B_WIDTH), kv.reshape(L, d * 2 * B_WIDTH))
    return o.reshape(T, B_WIDTH), lse.reshape(T, B_WIDTH)


def _dil_combine_fwd(os_, lses, *, tt, name):
    T = os_[0].shape[0]

    def body(o0, o1, o2, l0, l1, l2, y_ref, lse_ref):
        a, b, c = l0[...], l1[...], l2[...]
        m = jnp.maximum(jnp.maximum(a, b), c)
        wa, wb, wc = jnp.exp(a - m), jnp.exp(b - m), jnp.exp(c - m)
        den = wa + wb + wc
        y_ref[...] = (wa * o0[...] + wb * o1[...] + wc * o2[...]) / den
        lse_ref[...] = m + jnp.log(den)

    blk = pl.BlockSpec((tt, B_WIDTH), lambda i: (i, 0))
    sh = jax.ShapeDtypeStruct((T, B_WIDTH), F32)
    return pl.pallas_call(
        body, grid=(T // tt,), in_specs=[blk] * 6, out_specs=[blk, blk], out_shape=[sh, sh],
        compiler_params=_cp("parallel"), name=name)(*os_, *lses)


def _dil_bwd_prep(dmix, mix_main, *, tt, name, dep=None):
    T = mix_main.shape[0]

    def kernel_body(dy_ref, y_ref, dmm_ref, dd_ref):
        for h in range(B_HEADS):
            sl = slice(h * HEAD_DIM, (h + 1) * HEAD_DIM)
            dy = dy_ref[:, sl]
            dmm_ref[:, sl] = dy.astype(BF16)
            dd_ref[:, sl] = jnp.broadcast_to(jnp.sum(dy * y_ref[:, sl], axis=-1, keepdims=True), (tt, HEAD_DIM))

    blk = pl.BlockSpec((tt, B_WIDTH), lambda i: (i, 0))
    body, dep_specs, dep_args = _dep(kernel_body, 2, dep)
    return pl.pallas_call(
        body, grid=(T // tt,), in_specs=[blk, blk] + dep_specs, out_specs=[blk, blk],
        out_shape=[jax.ShapeDtypeStruct((T, B_WIDTH), BF16), jax.ShapeDtypeStruct((T, B_WIDTH), F32)],
        compiler_params=_cp("parallel"), name=name)(dmix, mix_main, *dep_args)


def _dil_bwd_dq(qr, kr, kv, dmm, lse, dd, gi, d, *, name):
    T = qr.shape[0]
    L, nb = _dil_views(T, d)

    def body(q_ref, kc_ref, kp_ref, vc_ref, vp_ref, dy_ref, lse_ref, dd_ref, dq_ref):
        cur_ok, prev_band = _band_masks()
        prev_ok = prev_band & (pl.program_id(1) > 0)
        for h in range(B_HEADS):
            sl = slice(h * HEAD_DIM, (h + 1) * HEAD_DIM)
            q, dy = q_ref[:, sl], dy_ref[:, sl]
            kc, kp = kc_ref[:, sl], kp_ref[:, sl]
            lse_h = jnp.max(lse_ref[:, sl], axis=-1, keepdims=True)
            dd_h = jnp.max(dd_ref[:, sl], axis=-1, keepdims=True)
            pc = jnp.exp(jnp.where(cur_ok, _dot_nt(q, kc) * ATT_SCALE, NEG) - lse_h)
            pp = jnp.exp(jnp.where(prev_ok, _dot_nt(q, kp) * ATT_SCALE, NEG) - lse_h)
            dsc = pc * (_dot_nt(dy, vc_ref[:, sl]) - dd_h) * ATT_SCALE
            dsp = pp * (_dot_nt(dy, vp_ref[:, sl]) - dd_h) * ATT_SCALE
            dq_ref[:, sl] = _dot(dsc, kc) + _dot(dsp, kp)

    blk = lambda f: pl.BlockSpec((SPAN, B_WIDTH), f)
    cur = lambda r, n: (n, r)
    prev = lambda r, n: (jnp.maximum(n - 1, 0), r)
    v2 = lambda a: a.reshape(L, d * a.shape[1])
    dq = pl.pallas_call(
        body, grid=(d, nb),
        in_specs=[blk(lambda r, n: (n, r * N_GROUPS + gi)), blk(cur), blk(prev),
                  blk(lambda r, n: (n, 2 * r + 1)), blk(lambda r, n: (jnp.maximum(n - 1, 0), 2 * r + 1)),
                  blk(cur), blk(cur), blk(cur)],
        out_specs=blk(cur), out_shape=jax.ShapeDtypeStruct((L, d * B_WIDTH), F32),
        compiler_params=_cp("parallel", "arbitrary"), name=name,
    )(v2(qr), v2(kr), v2(kr), v2(kv), v2(kv), v2(dmm), v2(lse), v2(dd))
    return dq.reshape(T, B_WIDTH)


def _dil_bwd_dkv(qr, kr, kv, dmm, lse, dd, gi, d, *, name):
    T = qr.shape[0]
    L, nb = _dil_views(T, d)

    def body(k_ref, v_ref, q0_ref, q1_ref, dy0_ref, dy1_ref, lse0_ref, lse1_ref, dd0_ref, dd1_ref, dk_ref, dv_ref):
        cur_ok, prev_band = _band_masks()
        next_ok = prev_band & (pl.program_id(1) < nb - 1)
        for h in range(B_HEADS):
            sl = slice(h * HEAD_DIM, (h + 1) * HEAD_DIM)
            k, v = k_ref[:, sl], v_ref[:, sl]
            dk = jnp.zeros((SPAN, HEAD_DIM), F32)
            dv = jnp.zeros((SPAN, HEAD_DIM), F32)
            for ok, q_ref, dy_ref, lse_ref, dd_ref in ((cur_ok, q0_ref, dy0_ref, lse0_ref, dd0_ref),
                                                         (next_ok, q1_ref, dy1_ref, lse1_ref, dd1_ref)):
                q, dy = q_ref[:, sl], dy_ref[:, sl]
                lse_h = jnp.max(lse_ref[:, sl], axis=-1, keepdims=True)
                dd_h = jnp.max(dd_ref[:, sl], axis=-1, keepdims=True)
                p = jnp.exp(jnp.where(ok, _dot_nt(q, k) * ATT_SCALE, NEG) - lse_h)
                ds = p * (_dot_nt(dy, v) - dd_h) * ATT_SCALE
                dk = dk + _dot_tn(ds, q)
                dv = dv + _dot_tn(p, dy)
            dk_ref[:, sl] = dk
            dv_ref[:, sl] = dv

    blk = lambda f: pl.BlockSpec((SPAN, B_WIDTH), f)
    cur = lambda r, n: (n, r)
    nxt = lambda r, n: (jnp.minimum(n + 1, nb - 1), r)
    qcur = lambda r, n: (n, r * N_GROUPS + gi)
    qnxt = lambda r, n: (jnp.minimum(n + 1, nb - 1), r * N_GROUPS + gi)
    v2 = lambda a: a.reshape(L, d * a.shape[1])
    ov = jax.ShapeDtypeStruct((L, d * B_WIDTH), F32)
    dk, dv = pl.pallas_call(
        body, grid=(d, nb),
        in_specs=[blk(cur), blk(lambda r, n: (n, 2 * r + 1)), blk(qcur), blk(qnxt),
                  blk(cur), blk(nxt), blk(cur), blk(nxt), blk(cur), blk(nxt)],
        out_specs=[blk(cur), blk(cur)], out_shape=[ov, ov],
        compiler_params=_cp("parallel", "arbitrary"), name=name,
    )(v2(kr), v2(kv), v2(qr), v2(qr), v2(dmm), v2(dmm), v2(lse), v2(lse), v2(dd), v2(dd))
    return dk.reshape(T, B_WIDTH), dv.reshape(T, B_WIDTH)


DILS_UNROLL = 4


def _dils_specs(gi, d, nblk):
    blk = lambda f: pl.BlockSpec((SPAN * d, HEAD_DIM), f)
    return {
        "q": blk(lambda h, n: (n, gi * B_HEADS + h)), "q_next": blk(lambda h, n: (jnp.minimum(n + 1, nblk - 1), gi * B_HEADS + h)),
        "cur": blk(lambda h, n: (n, h)), "prev": blk(lambda h, n: (jnp.maximum(n - 1, 0), h)),
        "next": blk(lambda h, n: (jnp.minimum(n + 1, nblk - 1), h)),
        "v": blk(lambda h, n: (n, B_HEADS + h)), "v_prev": blk(lambda h, n: (jnp.maximum(n - 1, 0), B_HEADS + h)),
    }


def _dils_fwd(qr, kr, kv, gi, d, *, name):
    T = qr.shape[0]
    nblk = T // (SPAN * d)
    sp = _dils_specs(gi, d, nblk)

    def body(q_ref, kc_ref, kp_ref, vc_ref, vp_ref, o_ref, lse_ref):
        cur_ok, prev_band = _band_masks()
        prev_ok = prev_band & (pl.program_id(1) > 0)

        def residue(r, carry):
            rows = pl.ds(r, SPAN, stride=d)
            q = q_ref[rows, :]
            sc = jnp.where(cur_ok, _dot_nt(q, kc_ref[rows, :]) * ATT_SCALE, NEG)
            sp_ = jnp.where(prev_ok, _dot_nt(q, kp_ref[rows, :]) * ATT_SCALE, NEG)
            m = jnp.maximum(jnp.max(sc, axis=-1, keepdims=True), jnp.max(sp_, axis=-1, keepdims=True))
            pc = jnp.exp(sc - m)
            pp = jnp.exp(sp_ - m)
            l = jnp.sum(pc, axis=-1, keepdims=True) + jnp.sum(pp, axis=-1, keepdims=True)
            o_ref[rows, :] = (_dot(pc, vc_ref[rows, :]) + _dot(pp, vp_ref[rows, :])) / l
            lse_ref[rows, :] = jnp.broadcast_to(m + jnp.log(l), (SPAN, HEAD_DIM))
            return carry

        lax.fori_loop(0, d, residue, 0, unroll=DILS_UNROLL)

    sh = jax.ShapeDtypeStruct((T, B_WIDTH), F32)
    return pl.pallas_call(
        body, grid=(B_HEADS, nblk), in_specs=[sp["q"], sp["cur"], sp["prev"], sp["v"], sp["v_prev"]],
        out_specs=[sp["cur"], sp["cur"]], out_shape=[sh, sh],
        compiler_params=_cp("parallel", "arbitrary"), name=name)(qr, kr, kr, kv, kv)


def _dils_bwd_dq(qr, kr, kv, dmix, lse, dd, gi, d, *, name):
    T = qr.shape[0]
    nblk = T // (SPAN * d)
    sp = _dils_specs(gi, d, nblk)

    def body(q_ref, kc_ref, kp_ref, vc_ref, vp_ref, dy_ref, lse_ref, dd_ref, dq_ref):
        cur_ok, prev_band = _band_masks()
        prev_ok = prev_band & (pl.program_id(1) > 0)

        def residue(r, carry):
            rows = pl.ds(r, SPAN, stride=d)
            q, dy = q_ref[rows, :], dy_ref[rows, :]
            kc, kp = kc_ref[rows, :], kp_ref[rows, :]
            lse_h = jnp.max(lse_ref[rows, :], axis=-1, keepdims=True)
            dd_h = jnp.max(dd_ref[rows, :], axis=-1, keepdims=True)
            pc = jnp.exp(jnp.where(cur_ok, _dot_nt(q, kc) * ATT_SCALE, NEG) - lse_h)
            pp = jnp.exp(jnp.where(prev_ok, _dot_nt(q, kp) * ATT_SCALE, NEG) - lse_h)
            dsc = pc * (_dot_nt(dy, vc_ref[rows, :]) - dd_h) * ATT_SCALE
            dsp = pp * (_dot_nt(dy, vp_ref[rows, :]) - dd_h) * ATT_SCALE
            dq_ref[rows, :] = _dot(dsc, kc) + _dot(dsp, kp)
            return carry

        lax.fori_loop(0, d, residue, 0, unroll=DILS_UNROLL)

    return pl.pallas_call(
        body, grid=(B_HEADS, nblk),
        in_specs=[sp["q"], sp["cur"], sp["prev"], sp["v"], sp["v_prev"], sp["cur"], sp["cur"], sp["cur"]],
        out_specs=sp["cur"], out_shape=jax.ShapeDtypeStruct((T, B_WIDTH), F32),
        compiler_params=_cp("parallel", "arbitrary"), name=name)(qr, kr, kr, kv, kv, dmix, lse, dd)


def _dils_bwd_dkv(qr, kr, kv, dmix, lse, dd, gi, d, *, name):
    T = qr.shape[0]
    nblk = T // (SPAN * d)
    sp = _dils_specs(gi, d, nblk)

    def body(k_ref, v_ref, q0_ref, q1_ref, dy0_ref, dy1_ref, lse0_ref, lse1_ref, dd0_ref, dd1_ref, dk_ref, dv_ref):
        cur_ok, prev_band = _band_masks()
        next_ok = prev_band & (pl.program_id(1) < nblk - 1)

        def residue(r, carry):
            rows = pl.ds(r, SPAN, stride=d)
            k, v = k_ref[rows, :], v_ref[rows, :]
            dk = jnp.zeros((SPAN, HEAD_DIM), F32)
            dv = jnp.zeros((SPAN, HEAD_DIM), F32)
            for ok, q_ref, dy_ref, lse_ref, dd_ref in ((cur_ok, q0_ref, dy0_ref, lse0_ref, dd0_ref),
                                                         (next_ok, q1_ref, dy1_ref, lse1_ref, dd1_ref)):
                q, dy = q_ref[rows, :], dy_ref[rows, :]
                lse_h = jnp.max(lse_ref[rows, :], axis=-1, keepdims=True)
                dd_h = jnp.max(dd_ref[rows, :], axis=-1, keepdims=True)
                p = jnp.exp(jnp.where(ok, _dot_nt(q, k) * ATT_SCALE, NEG) - lse_h)
                ds = p * (_dot_nt(dy, v) - dd_h) * ATT_SCALE
                dk = dk + _dot_tn(ds, q)
                dv = dv + _dot_tn(p, dy)
            dk_ref[rows, :] = dk
            dv_ref[rows, :] = dv
            return carry

        lax.fori_loop(0, d, residue, 0, unroll=DILS_UNROLL)

    sh = jax.ShapeDtypeStruct((T, B_WIDTH), F32)
    return pl.pallas_call(
        body, grid=(B_HEADS, nblk),
        in_specs=[sp["cur"], sp["v"], sp["q"], sp["q_next"], sp["cur"], sp["next"], sp["cur"], sp["next"], sp["cur"], sp["next"]],
        out_specs=[sp["cur"], sp["cur"]], out_shape=[sh, sh],
        compiler_params=_cp("parallel", "arbitrary"), name=name)(kr, kv, qr, qr, dmix, dmix, lse, lse, dd, dd)


A_MQ_COL = 4 * A_WIDTH // MEM_WIDTH
B_MQ_COL = N_GROUPS * B_WIDTH // MEM_WIDTH


def _row(v):
    return v.reshape(1, -1).astype(F32)


def _local_step(x, mem, tgt, get_w, P, put_g, first_dep=None):
    T = x.shape[0]
    cosf, sinsg = _rope_tables(T)
    lb_soft = jax.nn.softmax(P["a_lb_logits"].astype(F32), axis=0)
    lb = lb_soft[0:1]
    qw_heads = jnp.repeat(P["b_qnorm"][0], B_HEADS, axis=0).reshape(1, -1)
    kw_heads = jnp.tile(_row(P["b_knorm"]), (1, B_HEADS))
    mqw = [jnp.tile(_row(P["mem_qnorm"][l]), (1, MEM_HEADS)) for l in range(2)]
    mkw = [jnp.tile(_row(P["mem_knorm"][l]), (1, MEM_HEADS)) for l in range(2)]
    nmix = [_row(P["norm_mix"][l]) for l in range(2)]
    nffn = [_row(P["norm_ffn"][l]) for l in range(2)]
    mnorm = [_row(P["mem_norm"][l]) for l in range(2)]
    kvn = _row(P["kv_norm"])
    onorm = _row(P["a_onorm"])
    W = {}

    def w_of(name, after=None):
        if name not in W:
            W[name] = get_w(name, after)
        return W[name]

    flips = {}

    def flipped(name, after=None):
        if name not in flips:
            flips[name] = w_of(name, after).T
        return flips[name]

    proj_a, xn0 = _rms_matmul(x, nmix[0], flipped("a_w_in"), tt=512, tn=1664, wt=False, name="proj_a", dep=first_dep)
    mkv0, mn0 = _rms_matmul(mem, mnorm[0], w_of("w_mem_kv0"), tt=MEM_TOKENS, tn=2 * MEM_WIDTH, wt=False, name="mem_kv0")
    o_raw, st = _hgrn2_fwd(proj_a, lb, name="hgrn2_fwd")
    mm0 = _a_post_fwd(o_raw, proj_a, onorm, tt=512, name="a_post_fwd")
    mo0 = _mem_attn_fwd(proj_a, A_MQ_COL, mkv0, mqw[0], mkw[0], tt=512, name="mem_attn_fwd0")
    mix0 = jnp.concatenate([mm0, mo0], axis=1)
    hm0 = _mm_res(x, mix0, w_of("w_out0", mix0), tt=512, name="out_proj0")
    gu0, hn0 = _rms_matmul(hm0, nffn[0], flipped("w_gate_up0", hm0), tt=512, tn=1408, wt=False, name="gate_up0")
    h1 = _swiglu_down(hm0, gu0, w_of("w_down0", gu0), tt=256, name="down0")
    kv, hkn = _rms_matmul(h1, kvn, flipped("w_kv", h1), tt=512, tn=768, wt=False, name="kv_proj")
    kr = _headnorm_rope_fwd(kv, kw_heads, cosf, sinsg, col0=0, n_heads=B_HEADS, tt=512, name="k_prep")

    proj_b, xn1 = _rms_matmul(h1, nmix[1], flipped("b_w_in", kr), tt=512, tn=1280, wt=False, name="proj_b")
    mkv1, mn1 = _rms_matmul(mem, mnorm[1], w_of("w_mem_kv1", kr), tt=MEM_TOKENS, tn=2 * MEM_WIDTH, wt=False, name="mem_kv1")
    qr = _headnorm_rope_fwd(proj_b, qw_heads, cosf, sinsg, col0=0, n_heads=N_GROUPS * B_HEADS, tt=512, name="q_prep")
    outs = [(_dil_fwd if d == 1 else _dils_fwd)(qr, kr, kv, gi, d, name=f"dil_fwd{gi}") for gi, d in enumerate(DILATIONS)]
    mm1, lse_tot = _dil_combine_fwd([o for o, _ in outs], [s for _, s in outs], tt=512, name="dil_combine")
    mo1 = _mem_attn_fwd(proj_b, B_MQ_COL, mkv1, mqw[1], mkw[1], tt=512, name="mem_attn_fwd1")
    mix1 = jnp.concatenate([mm1, mo1], axis=1)
    hm1 = _mm_res(h1, mix1, w_of("w_out1", mix1), tt=512, name="out_proj1")
    gu1, hn1 = _rms_matmul(hm1, nffn[1], flipped("w_gate_up1", hm1), tt=512, tn=1408, wt=False, name="gate_up1")
    y = _swiglu_down(hm1, gu1, w_of("w_down1", gu1), tt=256, name="down1")
    dy, sq = _loss_kernel(y, tgt, tt=512, name="loss")

    gP = {}
    zeros_mem = jnp.zeros((MEM_TOKENS, D_MODEL), F32)

    def ffn_bwd(l, dh, hm, gu, hn):
        dgu, act = _swiglu_bwd(dh, gu, flipped(f"w_down{l}"), tt=256, name=f"swiglu_bwd{l}")
        g_wd = _mm_tn(act, dh, tt=512, tka=1408, name=f"g_w_down{l}")
        g_wgu = _mm_tn(dgu, hn, tt=512, tka=1408, name=f"g_w_gate_up{l}")
        sent = put_g({f"w_down{l}": g_wd, f"w_gate_up{l}": g_wgu})
        dhm, g_nf = _rms_bwd_dx(hm, nffn[l], w_of(f"w_gate_up{l}"), dgu, dh, tt=256, wt=True, name=f"gate_up_bwd{l}", dep=sent)
        return dhm, g_nf

    def mix_bwd(l, dhm, mix, proj, qcol, mkv, mn):
        dmix = _mm(dhm, flipped(f"w_out{l}"), tt=512, name=f"out_proj_bwd{l}")
        g_wout = _mm_tn(mix, dhm, tt=512, tka=512, name=f"g_w_out{l}")
        dmq, dmkv, dqw, dkw = _mem_attn_bwd(proj, qcol, mkv, mqw[l], mkw[l], dmix, tt=512, name=f"mem_attn_bwd{l}")
        g_wmkv = _mm_tn(mn, dmkv, tt=MEM_TOKENS, tka=512, name=f"g_w_mem_kv{l}")
        sent = put_g({f"w_out{l}": g_wout, f"w_mem_kv{l}": g_wmkv})
        _, g_mn = _rms_bwd_dx(mem, mnorm[l], w_of(f"w_mem_kv{l}"), dmkv, zeros_mem, tt=MEM_TOKENS, wt=False, name=f"mem_kv_bwd{l}")
        fold = lambda v: v.reshape(MEM_HEADS, MEM_HEAD_DIM).sum(axis=0)
        return dmix, dmq, g_mn, fold(dqw), fold(dkw), sent

    dhm1, g_nf1 = ffn_bwd(1, dy, hm1, gu1, hn1)
    dmix1, dmq1, g_mn1, g_mq1, g_mk1, sent = mix_bwd(1, dhm1, mix1, proj_b, B_MQ_COL, mkv1, mn1)
    dmm, dd = _dil_bwd_prep(dmix1, mm1, tt=512, name="dil_bwd_prep", dep=sent)
    dqs, dks, dvs = [], [], []
    for gi, d in enumerate(DILATIONS):
        if d == 1:
            dqs.append(_dil_bwd_dq(qr, kr, kv, dmm, lse_tot, dd, gi, d, name=f"dil_bwd_dq{gi}"))
            dk_g, dv_g = _dil_bwd_dkv(qr, kr, kv, dmm, lse_tot, dd, gi, d, name=f"dil_bwd_dkv{gi}")
        else:
            dqs.append(_dils_bwd_dq(qr, kr, kv, dmix1, lse_tot, dd, gi, d, name=f"dil_bwd_dq{gi}"))
            dk_g, dv_g = _dils_bwd_dkv(qr, kr, kv, dmix1, lse_tot, dd, gi, d, name=f"dil_bwd_dkv{gi}")
        dks.append(dk_g)
        dvs.append(dv_g)
    dq_raw, dqw = _q_prep_bwd(proj_b, qw_heads, cosf, sinsg, dqs, tt=512, name="q_prep_bwd")
    dkv, dkw = _kv_prep_bwd(kv, kw_heads, cosf, sinsg, dks, dvs, tt=512, name="kv_prep_bwd")
    dproj_b = jnp.concatenate([dq_raw, dmq1], axis=1)
    g_wb = _mm_tn(dproj_b, xn1, tt=512, tka=1280, name="g_b_w_in")
    g_wkv = _mm_tn(dkv, hkn, tt=512, tka=768, name="g_w_kv")
    sent = put_g({"b_w_in": g_wb, "w_kv": g_wkv})
    dh1, g_nm1 = _rms_bwd_dx(h1, nmix[1], w_of("b_w_in"), dproj_b, dhm1, tt=256, wt=True, name="proj_b_bwd", dep=sent)
    dh1, g_kvn = _rms_bwd_dx(h1, kvn, w_of("w_kv"), dkv, dh1, tt=256, wt=True, name="kv_proj_bwd")

    dhm0, g_nf0 = ffn_bwd(0, dh1, hm0, gu0, hn0)
    dmix0, dmq0, g_mn0, g_mq0, g_mk0, sent = mix_bwd(0, dhm0, mix0, proj_a, A_MQ_COL, mkv0, mn0)
    do_raw, dg, g_onorm = _a_post_bwd(o_raw, proj_a, onorm, dmix0, tt=512, name="a_post_bwd", dep=sent)
    dq, dz, dv, dlb = _hgrn2_bwd(proj_a, lb, st, do_raw, name="hgrn2_bwd")
    dproj_a = jnp.concatenate([dq, dz, dv, dg, dmq0], axis=1)
    sent = put_g({"a_w_in": _mm_tn(dproj_a, xn0, tt=512, tka=1664, name="g_a_w_in")})
    gx, g_nm0 = _rms_bwd_dx(x, nmix[0], w_of("a_w_in"), dproj_a, dhm0, tt=256, wt=True, name="proj_a_bwd", dep=sent)

    dl0 = lb_soft[0:1] * lb_soft[1:2] * dlb
    gP["a_lb_logits"] = jnp.concatenate([dl0, -dl0], axis=0)
    gP["a_onorm"] = g_onorm
    gP["norm_mix"] = jnp.concatenate([g_nm0, g_nm1], axis=0)
    gP["norm_ffn"] = jnp.concatenate([g_nf0, g_nf1], axis=0)
    gP["b_qnorm"] = dqw.reshape(N_GROUPS, B_HEADS, HEAD_DIM).sum(axis=1)[None]
    gP["kv_norm"] = g_kvn.reshape(-1)
    gP["b_knorm"] = dkw.reshape(B_HEADS, HEAD_DIM).sum(axis=0)
    gP["mem_norm"] = jnp.concatenate([g_mn0, g_mn1], axis=0)
    gP["mem_qnorm"] = jnp.stack([g_mq0, g_mq1])
    gP["mem_knorm"] = jnp.stack([g_mk0, g_mk1])
    return sq, gx, gP


MESH_ID = pl.DeviceIdType.MESH
HBM_SPEC = pl.BlockSpec(memory_space=pltpu.HBM)


def _position():
    return lax.axis_index("x"), lax.axis_index("y"), lax.axis_index("c")


def _all_gather(blocks, *, name):
    n = len(blocks)

    def body(*refs):
        x_refs, out_refs = refs[:n], refs[n:2 * n]
        send_sems, recv_sems, local_sems = refs[2 * n:]
        x, y, c = _position()
        me, sibling = (x, y, c), (x, y, 1 - c)
        chips = [(1 - x, y), (x, 1 - y), (1 - x, 1 - y)]

        def slot(a, px, py, pc):
            return out_refs[a].at[4 * px + 2 * py + pc]

        def copy(a, k, blk, to, src=None):
            return pltpu.make_async_remote_copy(
                src_ref=slot(a, *blk) if src is None else src, dst_ref=slot(a, *blk),
                send_sem=send_sems.at[7 * a + k], recv_sem=recv_sems.at[7 * a + k], device_id=to, device_id_type=MESH_ID)

        mine = [pltpu.make_async_copy(x_refs[a], slot(a, *me), local_sems.at[a]) for a in range(n)]
        for cp in mine:
            cp.start()
        first = []
        for a in range(n):
            first.append(copy(a, 0, me, sibling, src=x_refs[a]))
            first += [copy(a, 1 + j, me, (*chip, c), src=x_refs[a]) for j, chip in enumerate(chips)]
        for cp in first:
            cp.start()
        passed = []
        for j, chip in enumerate(chips):
            for a in range(n):
                copy(a, 1 + j, (*chip, c), me).wait_recv()
                cp = copy(a, 4 + j, (*chip, c), sibling)
                cp.start()
                passed.append(cp)
        for a in range(n):
            copy(a, 0, sibling, me).wait_recv()
            for j, chip in enumerate(chips):
                copy(a, 4 + j, (*chip, 1 - c), me).wait_recv()
        for cp in first + passed:
            cp.wait_send()
        for cp in mine:
            cp.wait()

    return pl.pallas_call(
        body, out_shape=[jax.ShapeDtypeStruct((N_DEV,) + b.shape, b.dtype) for b in blocks],
        in_specs=[HBM_SPEC] * n, out_specs=[HBM_SPEC] * n,
        scratch_shapes=[pltpu.SemaphoreType.DMA((7 * n,)), pltpu.SemaphoreType.DMA((7 * n,)), pltpu.SemaphoreType.DMA((n,))],
        name=name)(*blocks)


SEM_SPEC = pl.BlockSpec(memory_space=pltpu.SEMAPHORE)
ANY_SPEC = pl.BlockSpec(memory_space=pl.ANY)
DATAFLOW = pltpu.SideEffectType.DATAFLOW_SIDE_EFFECTING


def _peer(k, x, y, c):
    return (1 - x if (k >> 2) & 1 else x, 1 - y if (k >> 1) & 1 else y, 1 - c if k & 1 else c)


def _own_slot_filled(own_block):
    x, y, c = _position()
    zone = lax.empty((N_DEV,) + own_block.shape, own_block.dtype)
    return lax.dynamic_update_slice_in_dim(zone, own_block[None], 4 * x + 2 * y + c, axis=0)


def _split_start(srcs, scatter, after, *, name):
    n = len(srcs)
    extra = [] if after is None else [after]
    x, y, c = _position()
    me = 4 * x + 2 * y + c
    lands = [_own_slot_filled(lax.dynamic_index_in_dim(s, me, 0, keepdims=False) if scatter else s) for s in srcs]

    def body(*refs):
        src_refs, land_refs = refs[:n], refs[n:2 * n]
        send_sems, recv_sems = refs[2 * n + len(extra)], refs[2 * n + len(extra) + 1]
        token = refs[-1]
        bx, by, bc = _position()
        bme = 4 * bx + 2 * by + bc
        for a in range(n):
            for k in range(1, N_DEV):
                tx, ty, tc = _peer(k, bx, by, bc)
                src = src_refs[a].at[4 * tx + 2 * ty + tc] if scatter else src_refs[a]
                pltpu.make_async_remote_copy(
                    src_ref=src, dst_ref=land_refs[a].at[bme],
                    send_sem=send_sems.at[7 * a + k - 1], recv_sem=recv_sems.at[7 * a + k - 1],
                    device_id=(tx, ty, tc), device_id_type=MESH_ID).start()
        token[...] = jnp.zeros_like(token)

    hbm = lambda a: pltpu.HBM(a.shape, a.dtype)
    outs = pl.pallas_call(
        body, name=name,
        out_shape=(pltpu.SemaphoreType.DMA((7 * n,)), pltpu.SemaphoreType.DMA((7 * n,)),
                   *[hbm(s) for s in srcs], *[hbm(l) for l in lands], jax.ShapeDtypeStruct((8, 128), F32)),
        in_specs=[HBM_SPEC] * (2 * n) + [ANY_SPEC] * len(extra),
        out_specs=(SEM_SPEC, SEM_SPEC, *[HBM_SPEC] * (2 * n), pl.BlockSpec(memory_space=pltpu.VMEM)),
        input_output_aliases={i: 2 + i for i in range(2 * n)},
        compiler_params=pltpu.CompilerParams(has_side_effects=DATAFLOW),
    )(*[pltpu.with_memory_space_constraint(s, pltpu.HBM) for s in srcs],
      *[pltpu.with_memory_space_constraint(l, pltpu.HBM) for l in lands], *extra)
    return {"n": n, "scatter": scatter, "send": outs[0], "recv": outs[1], "srcs": outs[2:2 + n],
            "lands": outs[2 + n:2 + 2 * n], "token": outs[-1]}


def _split_wait(handle, after, *, name):
    n, scatter = handle["n"], handle["scatter"]

    def body(*refs):
        src_refs, land_refs = refs[:n], refs[n:2 * n]
        send_sems, recv_sems = refs[2 * n], refs[2 * n + 1]
        bx, by, bc = _position()
        for a in range(n):
            for k in range(1, N_DEV):
                src = src_refs[a].at[0] if scatter else src_refs[a]
                cp = pltpu.make_async_remote_copy(
                    src_ref=src, dst_ref=land_refs[a].at[0],
                    send_sem=send_sems.at[7 * a + k - 1], recv_sem=recv_sems.at[7 * a + k - 1],
                    device_id=_peer(k, bx, by, bc), device_id_type=MESH_ID)
                cp.wait_send()
                cp.wait_recv()

    hbm = lambda a: pltpu.HBM(a.shape, a.dtype)
    outs = pl.pallas_call(
        body, name=name,
        out_shape=(*[hbm(s) for s in handle["srcs"]], *[hbm(l) for l in handle["lands"]]),
        in_specs=[HBM_SPEC] * (2 * n) + [SEM_SPEC, SEM_SPEC, ANY_SPEC],
        out_specs=tuple([HBM_SPEC] * (2 * n)),
        input_output_aliases={i: i for i in range(2 * n)},
        compiler_params=pltpu.CompilerParams(has_side_effects=DATAFLOW),
    )(*handle["srcs"], *handle["lands"], handle["send"], handle["recv"], after)
    return list(outs[n:])


def _sum_sources(parts, *, tr, name):
    n, R, C = parts.shape

    def body(p_ref, o_ref):
        acc = p_ref[0].astype(F32)
        for s in range(1, n):
            acc = acc + p_ref[s].astype(F32)
        o_ref[...] = acc

    return pl.pallas_call(
        body, grid=(R // tr,), in_specs=[pl.BlockSpec((n, tr, C), lambda i: (0, i, 0))],
        out_specs=pl.BlockSpec((tr, C), lambda i: (i, 0)),
        out_shape=jax.ShapeDtypeStruct((R, C), F32), compiler_params=_cp("parallel"), name=name)(parts)


def _adamw_math(g, w, m, v):
    c1 = 1.0 - ADAM_B1 ** ADAM_STEP
    c2 = 1.0 - ADAM_B2 ** ADAM_STEP
    nm = ADAM_B1 * m + (1.0 - ADAM_B1) * g
    nv = ADAM_B2 * v + (1.0 - ADAM_B2) * (g * g)
    return -ADAM_LR * ((nm / c1) / (jnp.sqrt(nv / c2) + ADAM_EPS) + ADAM_WD * w), nm, nv


def _reduce_adamw(received, w, m, v, *, col, tr, name):
    L, R, C = w.shape

    def body(*refs):
        p_refs = refs[:L]
        w_ref, m_ref, v_ref, g_ref, d_ref, nm_ref, nv_ref = refs[L:]
        for l in range(L):
            @pl.when(pl.program_id(0) == l)
            def _(p_ref=p_refs[l]):
                acc = p_ref[0].astype(F32)
                for s in range(1, N_DEV):
                    acc = acc + p_ref[s].astype(F32)
                g = acc.T if col else acc
                g_ref[...] = g
                d_ref[...], nm_ref[...], nv_ref[...] = _adamw_math(g, w_ref[...], m_ref[...], v_ref[...])

    p_spec = (pl.BlockSpec((N_DEV, C, tr), lambda l, i: (0, 0, i)) if col
              else pl.BlockSpec((N_DEV, tr, C), lambda l, i: (0, i, 0)))
    blk = pl.BlockSpec((None, tr, C), lambda l, i: (l, i, 0))
    sh = jax.ShapeDtypeStruct((L, R, C), F32)
    return pl.pallas_call(
        body, grid=(L, R // tr), in_specs=[p_spec] * L + [blk] * 3, out_specs=[blk] * 4, out_shape=[sh] * 4,
        compiler_params=_cp("parallel", "parallel"), name=name)(*received, w, m, v)


def _adamw(g, w, m, v, *, tr, name):
    L, R, C = w.shape

    def body(g_ref, w_ref, m_ref, v_ref, d_ref, nm_ref, nv_ref):
        d_ref[...], nm_ref[...], nv_ref[...] = _adamw_math(g_ref[...], w_ref[...], m_ref[...], v_ref[...])

    blk = pl.BlockSpec((None, tr, C), lambda l, i: (l, i, 0))
    sh = jax.ShapeDtypeStruct((L, R, C), F32)
    return pl.pallas_call(
        body, grid=(L, R // tr), in_specs=[blk] * 4, out_specs=[blk] * 3, out_shape=[sh] * 3,
        compiler_params=_cp("parallel", "parallel"), name=name)(g, w, m, v)


UNITS = {
    "a_w_in": ("a_w_in", 0, True), "w_mem_kv0": ("w_mem_kv", 0, False), "w_out0": ("w_out", 0, False),
    "w_gate_up0": ("w_gate_up", 0, True), "w_down0": ("w_down", 0, False), "w_kv": ("w_kv", None, True),
    "b_w_in": ("b_w_in", 0, True), "w_mem_kv1": ("w_mem_kv", 1, False), "w_out1": ("w_out", 1, False),
    "w_gate_up1": ("w_gate_up", 1, True), "w_down1": ("w_down", 1, False),
}
BIG = ("a_w_in", "b_w_in", "w_kv", "w_mem_kv", "w_out", "w_gate_up", "w_down")
ADAMW_ROW_TILE = {"a_w_in": 256, "b_w_in": 256, "w_kv": 256, "w_mem_kv": 128, "w_out": 128, "w_gate_up": 176, "w_down": 176}
TRANSPOSED_UPDATE = ("w_gate_up",)


def _wire_block(weights, unit):
    name, layer, col = UNITS[unit]
    a = weights[name] if layer is None else weights[name][layer]
    return (a.T if col else a).astype(BF16)


SMALL_REPLICATED = ("norm_mix", "norm_ffn", "b_qnorm", "kv_norm", "b_knorm", "mem_norm", "mem_qnorm", "mem_knorm")
SMALL_SHARDED = ("a_lb_logits", "a_onorm")
SMALL_ORDER = SMALL_REPLICATED + SMALL_SHARDED
LANES = 128


def _prod(shape):
    n = 1
    for s in shape:
        n *= s
    return n


def _pack_flat(arrays, rows, cols, dtype):
    flat = jnp.concatenate([a.reshape(-1).astype(dtype) for a in arrays])
    return jnp.pad(flat, (0, rows * cols - flat.shape[0])).reshape(rows, cols)


def _unpack_flat(packed, shapes):
    flat = packed.reshape(-1)
    out, off = [], 0
    for s in shapes:
        out.append(flat[off:off + _prod(s)].reshape(s))
        off += _prod(s)
    return out


def kernel(x, mem, norm_mix, norm_ffn, a_w_in, a_lb_logits, a_onorm, b_w_in, b_qnorm, kv_norm, w_kv, b_knorm, mem_norm, w_mem_kv, mem_qnorm, mem_knorm, w_out, w_gate_up, w_down, loss_target, m_norm_mix, m_norm_ffn, m_a_w_in, m_a_lb_logits, m_a_onorm, m_b_w_in, m_b_qnorm, m_kv_norm, m_w_kv, m_b_knorm, m_mem_norm, m_w_mem_kv, m_mem_qnorm, m_mem_knorm, m_w_out, m_w_gate_up, m_w_down, v_norm_mix, v_norm_ffn, v_a_w_in, v_a_lb_logits, v_a_onorm, v_b_w_in, v_b_qnorm, v_kv_norm, v_w_kv, v_b_knorm, v_mem_norm, v_w_mem_kv, v_mem_qnorm, v_mem_knorm, v_w_out, v_w_gate_up, v_w_down):
    names = ("norm_mix", "norm_ffn", "a_w_in", "a_lb_logits", "a_onorm", "b_w_in", "b_qnorm", "kv_norm", "w_kv", "b_knorm",
             "mem_norm", "w_mem_kv", "mem_qnorm", "mem_knorm", "w_out", "w_gate_up", "w_down")
    w = dict(zip(names, (norm_mix, norm_ffn, a_w_in, a_lb_logits, a_onorm, b_w_in, b_qnorm, kv_norm, w_kv, b_knorm,
                         mem_norm, w_mem_kv, mem_qnorm, mem_knorm, w_out, w_gate_up, w_down)))
    m = dict(zip(names, (m_norm_mix, m_norm_ffn, m_a_w_in, m_a_lb_logits, m_a_onorm, m_b_w_in, m_b_qnorm, m_kv_norm, m_w_kv,
                         m_b_knorm, m_mem_norm, m_w_mem_kv, m_mem_qnorm, m_mem_knorm, m_w_out, m_w_gate_up, m_w_down)))
    v = dict(zip(names, (v_norm_mix, v_norm_ffn, v_a_w_in, v_a_lb_logits, v_a_onorm, v_b_w_in, v_b_qnorm, v_kv_norm, v_w_kv,
                         v_b_knorm, v_mem_norm, v_w_mem_kv, v_mem_qnorm, v_mem_knorm, v_w_out, v_w_gate_up, v_w_down)))

    first = ["a_w_in", "w_mem_kv0"]
    gathered = _all_gather([_wire_block(w, u) for u in first] + [_pack_flat([a_lb_logits, a_onorm], 8, LANES, F32)],
                           name="gather_first")
    full = {u: g.reshape(-1, g.shape[-1]) for u, g in zip(first, gathered)}
    small_in = gathered[-1].reshape(N_DEV, -1)
    P = {n: w[n] for n in SMALL_REPLICATED}
    P["a_lb_logits"] = small_in[:, :192].reshape(N_DEV, 2, 96).transpose(1, 0, 2).reshape(2, A_WIDTH)
    P["a_onorm"] = small_in[:, 192:288].reshape(1, A_WIDTH)
    later = [["w_out0", "w_gate_up0"], ["w_down0", "w_kv"], ["b_w_in", "w_mem_kv1"], ["w_out1", "w_gate_up1", "w_down1"]]
    pending = {}
    token = gathered[-1]
    for i, group in enumerate(later):
        handle = _split_start([_wire_block(w, u) for u in group], False, token, name=f"gather{i}_start")
        token = handle["token"]
        for u in group:
            pending[u] = (i, group, handle)

    def get_w(unit, after):
        if unit not in full:
            i, group, handle = pending[unit]
            for u, land in zip(group, _split_wait(handle, after, name=f"gather{i}_wait")):
                full[u] = land.reshape(-1, land.shape[-1])
        return full[unit]

    sent = []

    def put_g(group):
        units = list(group)
        handle = _split_start([group[u].reshape(N_DEV, -1, group[u].shape[-1]) for u in units], True, None,
                              name=f"scatter{len(sent)}_start")
        sent.append((units, handle))
        return handle["token"]

    sq, gx, gP = _local_step(x[0], mem[0], loss_target[0], get_w, P, put_g, first_dep=token)
    loss = lax.psum(0.5 * jnp.sum(sq) / D_MODEL, ("x", "y", "c"))

    received = {}
    for i, (units, handle) in enumerate(sent):
        received.update(zip(units, _split_wait(handle, gx, name=f"scatter{i}_wait")))
    out = {"grad": {}, "delta": {}, "new_m": {}, "new_v": {}}
    for n in BIG:
        shape = w[n].shape
        as3 = lambda a: a.reshape((-1,) + shape[-2:])
        mine = [u for u, (wn, _, _) in UNITS.items() if wn == n]
        col = UNITS[mine[0]][2]
        flip = (lambda a: jnp.swapaxes(a, 1, 2)) if n in TRANSPOSED_UPDATE else (lambda a: a)
        res = _reduce_adamw([received[u] for u in mine], flip(as3(w[n])), flip(as3(m[n])), flip(as3(v[n])),
                            col=col and n not in TRANSPOSED_UPDATE, tr=ADAMW_ROW_TILE[n], name=f"adamw_{n}")
        for kind, r in zip(("grad", "delta", "new_m", "new_v"), res):
            out[kind][n] = flip(r).reshape(shape)

    full_shapes = [(2, A_WIDTH) if n == "a_lb_logits" else (1, A_WIDTH) if n == "a_onorm" else w[n].shape for n in SMALL_ORDER]
    n_small = sum(_prod(s) for s in full_shapes)
    rows_small = -(-n_small // (8 * LANES)) * 8
    g_all, = _all_gather([_pack_flat([gP[n] for n in SMALL_ORDER], rows_small, LANES, F32)], name="gather_small_grads")
    g_small = dict(zip(SMALL_ORDER, _unpack_flat(_sum_sources(g_all, tr=rows_small, name="sum_small_grads"), full_shapes)))
    me = 4 * lax.axis_index("x") + 2 * lax.axis_index("y") + lax.axis_index("c")
    for n in SMALL_SHARDED:
        g_small[n] = lax.dynamic_slice_in_dim(g_small[n], me * 96, 96, axis=1)
    shapes = [w[n].shape for n in SMALL_ORDER]
    rows_upd = -(-sum(_prod(s) for s in shapes) // (8 * LANES)) * 8
    pk = lambda d: _pack_flat([d[n] for n in SMALL_ORDER], rows_upd, LANES, F32)
    res = _adamw(pk(g_small)[None], pk(w)[None], pk(m)[None], pk(v)[None], tr=rows_upd, name="adamw_small")
    out["grad"].update(g_small)
    for kind, packed in zip(("delta", "new_m", "new_v"), res):
        out[kind].update(zip(SMALL_ORDER, _unpack_flat(packed[0], shapes)))

    return (loss, gx[None], *[out["grad"][n] for n in names], *[out["delta"][n] for n in names],
            *[out["new_m"][n] for n in names], *[out["new_v"][n] for n in names])
```

```python
import functools

import jax
import jax.numpy as jnp
from jax import lax
from jax.experimental import pallas as pl
from jax.experimental.pallas import tpu as pltpu

F32 = jnp.float32
BF16 = jnp.bfloat16

N_DEV = 8
D_MODEL = 1024
HEAD_DIM = 128
A_HEADS = 6
A_WIDTH = A_HEADS * HEAD_DIM
CHUNK = 64
B_HEADS = 6
B_WIDTH = B_HEADS * HEAD_DIM
DILATIONS = (1, 4, 16)
SPAN = 128
N_GROUPS = 3
ROPE_THETA = 10000.0
MEM_TOKENS = 256
MEM_HEADS = 4
MEM_HEAD_DIM = 64
MEM_WIDTH = MEM_HEADS * MEM_HEAD_DIM
FFN_HIDDEN = 2816
EPS = 1e-6

ADAM_LR = 0.001
ADAM_B1 = 0.9
ADAM_B2 = 0.999
ADAM_EPS = 1e-08
ADAM_WD = 0.01
ADAM_STEP = 10

V7X_VMEM_LIMIT_BYTES = 56 * 1024 * 1024

NT_DIMS = (((1,), (1,)), ((), ()))
TN_DIMS = (((0,), (0,)), ((), ()))


def _cp(*sem):
    return pltpu.CompilerParams(dimension_semantics=sem, vmem_limit_bytes=V7X_VMEM_LIMIT_BYTES)


def _dot(a, b):
    return jnp.dot(a.astype(BF16), b.astype(BF16), preferred_element_type=F32)


def _dot_nt(a, b):
    return lax.dot_general(a.astype(BF16), b.astype(BF16), NT_DIMS, preferred_element_type=F32)


def _dot_tn(a, b):
    return lax.dot_general(a.astype(BF16), b.astype(BF16), TN_DIMS, preferred_element_type=F32)


def _dot3(m01, x):
    hi = x.astype(BF16)
    r1 = x - hi.astype(F32)
    mid = r1.astype(BF16)
    lo = (r1 - mid.astype(F32)).astype(BF16)
    d = functools.partial(jnp.dot, preferred_element_type=F32)
    return d(m01, hi) + d(m01, mid) + d(m01, lo)


def _sigmoid(x):
    return 1.0 / (1.0 + jnp.exp(-x))


def _full(shape):
    return pl.BlockSpec(shape, lambda *_: (0,) * len(shape))


def _dep(body, n_in, dep):
    if dep is None:
        return body, [], []

    def with_dep(*refs):
        return body(*refs[:n_in], *refs[n_in + 1:])

    return with_dep, [pl.BlockSpec(memory_space=pl.ANY)], [dep]


def _rms_matmul(x, g, w, *, tt, tn, wt, name, out_dtype=F32, dep=None):
    T, K = x.shape
    N = w.shape[0] if wt else w.shape[1]

    def kernel_body(x_ref, g_ref, w_ref, y_ref, xn_ref):
        xf = x_ref[...]
        r = lax.rsqrt(jnp.mean(xf * xf, axis=-1, keepdims=True) + EPS)
        xn = (xf * r * g_ref[...]).astype(BF16)
        xn_ref[...] = xn
        for j in range(N // tn):
            cols = slice(j * tn, (j + 1) * tn)
            y = _dot_nt(xn, w_ref[cols, :]) if wt else _dot(xn, w_ref[:, cols])
            y_ref[:, cols] = y.astype(out_dtype)

    body, dep_specs, dep_args = _dep(kernel_body, 3, dep)
    return pl.pallas_call(
        body, grid=(T // tt,),
        in_specs=[pl.BlockSpec((tt, K), lambda i: (i, 0)), _full((1, K)), _full(w.shape)] + dep_specs,
        out_specs=[pl.BlockSpec((tt, N), lambda i: (i, 0)), pl.BlockSpec((tt, K), lambda i: (i, 0))],
        out_shape=[jax.ShapeDtypeStruct((T, N), out_dtype), jax.ShapeDtypeStruct((T, K), BF16)],
        compiler_params=_cp("parallel"), name=name)(x, g, w, *dep_args)


def _mm_res(res, a, w, *, tt, name):
    T, K = a.shape
    N = w.shape[1]

    def body(r_ref, a_ref, w_ref, o_ref):
        o_ref[...] = r_ref[...] + _dot(a_ref[...], w_ref[...])

    return pl.pallas_call(
        body, grid=(T // tt,),
        in_specs=[pl.BlockSpec((tt, N), lambda i: (i, 0)), pl.BlockSpec((tt, K), lambda i: (i, 0)), _full((K, N))],
        out_specs=pl.BlockSpec((tt, N), lambda i: (i, 0)),
        out_shape=jax.ShapeDtypeStruct((T, N), F32),
        compiler_params=_cp("parallel"), name=name)(res, a, w)


def _swiglu_down(h, gu, wd, *, tt, name):
    T, D = h.shape
    Fh = wd.shape[0]

    def body(h_ref, gt_ref, up_ref, w_ref, o_ref):
        gt = gt_ref[...].astype(F32)
        act = gt * _sigmoid(gt) * up_ref[...].astype(F32)
        o_ref[...] = h_ref[...] + _dot(act, w_ref[...])

    return pl.pallas_call(
        body, grid=(T // tt,),
        in_specs=[pl.BlockSpec((tt, D), lambda i: (i, 0)), pl.BlockSpec((tt, Fh), lambda i: (i, 0)),
                  pl.BlockSpec((tt, Fh), lambda i: (i, 1)), _full((Fh, D))],
        out_specs=pl.BlockSpec((tt, D), lambda i: (i, 0)),
        out_shape=jax.ShapeDtypeStruct((T, D), F32),
        compiler_params=_cp("parallel"), name=name)(h, gu, gu, wd)


def _swiglu_bwd(dh, gu, wd, *, tt, name):
    T, D = dh.shape
    Fh = wd.shape[0]

    def body(dh_ref, gt_ref, up_ref, w_ref, dgu_ref, act_ref):
        gt = gt_ref[...].astype(F32)
        up = up_ref[...].astype(F32)
        s = _sigmoid(gt)
        silu = gt * s
        dact = _dot_nt(dh_ref[...], w_ref[...])
        act_ref[...] = (silu * up).astype(BF16)
        dgu_ref[:, :Fh] = (dact * up * (s * (1.0 + gt * (1.0 - s)))).astype(BF16)
        dgu_ref[:, Fh:] = (dact * silu).astype(BF16)

    return pl.pallas_call(
        body, grid=(T // tt,),
        in_specs=[pl.BlockSpec((tt, D), lambda i: (i, 0)), pl.BlockSpec((tt, Fh), lambda i: (i, 0)),
                  pl.BlockSpec((tt, Fh), lambda i: (i, 1)), _full((Fh, D))],
        out_specs=[pl.BlockSpec((tt, 2 * Fh), lambda i: (i, 0)), pl.BlockSpec((tt, Fh), lambda i: (i, 0))],
        out_shape=[jax.ShapeDtypeStruct((T, 2 * Fh), BF16), jax.ShapeDtypeStruct((T, Fh), BF16)],
        compiler_params=_cp("parallel"), name=name)(dh, gu, gu, wd)


def _mm_nt(a, w, *, tt, name):
    T, N = a.shape
    K = w.shape[0]

    def body(a_ref, w_ref, o_ref):
        o_ref[...] = _dot_nt(a_ref[...], w_ref[...])

    return pl.pallas_call(
        body, grid=(T // tt,),
        in_specs=[pl.BlockSpec((tt, N), lambda i: (i, 0)), _full((K, N))],
        out_specs=pl.BlockSpec((tt, K), lambda i: (i, 0)),
        out_shape=jax.ShapeDtypeStruct((T, K), F32),
        compiler_params=_cp("parallel"), name=name)(a, w)


def _mm_tn(a, b, *, tt, tka, name):
    T, Ka = a.shape
    N = b.shape[1]
    last = T // tt - 1

    def body(a_ref, b_ref, o_ref, acc):
        @pl.when(pl.program_id(1) == 0)
        def _():
            acc[...] = jnp.zeros_like(acc)

        acc[...] += _dot_tn(a_ref[...], b_ref[...])

        @pl.when(pl.program_id(1) == last)
        def _():
            o_ref[...] = acc[...].astype(BF16)

    return pl.pallas_call(
        body, grid=(Ka // tka, T // tt),
        in_specs=[pl.BlockSpec((tt, tka), lambda j, t: (t, j)), pl.BlockSpec((tt, N), lambda j, t: (t, 0))],
        out_specs=pl.BlockSpec((tka, N), lambda j, t: (j, 0)),
        out_shape=jax.ShapeDtypeStruct((Ka, N), BF16),
        scratch_shapes=[pltpu.VMEM((tka, N), F32)],
        compiler_params=_cp("parallel", "arbitrary"), name=name)(a, b)


def _rms_bwd_dx(x, g, w, dy, dres, *, tt, wt, name, dep=None):
    T, K = x.shape
    N = w.shape[0] if wt else w.shape[1]

    def kernel_body(x_ref, g_ref, w_ref, dy_ref, dres_ref, dx_ref, dg_ref):
        @pl.when(pl.program_id(0) == 0)
        def _():
            dg_ref[...] = jnp.zeros_like(dg_ref)

        dxn = (_dot if wt else _dot_nt)(dy_ref[...], w_ref[...])
        xf = x_ref[...]
        r = lax.rsqrt(jnp.mean(xf * xf, axis=-1, keepdims=True) + EPS)
        xhat = xf * r
        dg_ref[...] += jnp.sum(dxn * xhat, axis=0, keepdims=True)
        dxhat = dxn * g_ref[...]
        dx_ref[...] = dres_ref[...] + r * (dxhat - xhat * jnp.mean(dxhat * xhat, axis=-1, keepdims=True))

    body, dep_specs, dep_args = _dep(kernel_body, 5, dep)
    return pl.pallas_call(
        body, grid=(T // tt,),
        in_specs=[pl.BlockSpec((tt, K), lambda i: (i, 0)), _full((1, K)), _full(w.shape),
                  pl.BlockSpec((tt, N), lambda i: (i, 0)), pl.BlockSpec((tt, K), lambda i: (i, 0))] + dep_specs,
        out_specs=[pl.BlockSpec((tt, K), lambda i: (i, 0)), _full((1, K))],
        out_shape=[jax.ShapeDtypeStruct((T, K), F32), jax.ShapeDtypeStruct((1, K), F32)],
        compiler_params=_cp("arbitrary"), name=name)(x, g, w, dy, dres, *dep_args)


def _loss_kernel(y, tgt, *, tt, name):
    T, D = y.shape

    def body(y_ref, t_ref, dy_ref, acc_ref):
        @pl.when(pl.program_id(0) == 0)
        def _():
            acc_ref[...] = jnp.zeros_like(acc_ref)

        e = y_ref[...] - t_ref[...]
        dy_ref[...] = e * (1.0 / D)
        acc_ref[...] += jnp.sum(e * e, axis=0, keepdims=True)

    return pl.pallas_call(
        body, grid=(T // tt,),
        in_specs=[pl.BlockSpec((tt, D), lambda i: (i, 0)), pl.BlockSpec((tt, D), lambda i: (i, 0))],
        out_specs=[pl.BlockSpec((tt, D), lambda i: (i, 0)), _full((1, D))],
        out_shape=[jax.ShapeDtypeStruct((T, D), F32), jax.ShapeDtypeStruct((1, D), F32)],
        compiler_params=_cp("arbitrary"), name=name)(y, tgt)


HGRN_TB = 512
HGRN_NCH = HGRN_TB // CHUNK
HGRN_HPB = 6


def _hgrn_chunk_fwd(q, z, lbv, tril01):
    sig = _sigmoid(z)
    f = lbv + (1.0 - lbv) * sig
    kk = 1.0 - f
    b = _dot3(tril01, jnp.log(f))
    bend = b[CHUNK - 1:CHUNK, :]
    sq = _sigmoid(q)
    eb = jnp.exp(b)
    emb = jnp.exp(-b)
    eo = jnp.exp(bend - b)
    dec = jnp.exp(bend)
    return sig, f, kk, sq, eb, emb, eo, dec


def _hgrn2_fwd(proj, lb, *, name):
    T = proj.shape[0]
    nT = T // HGRN_TB
    nC = T // CHUNK

    def body(q_ref, z_ref, v_ref, lb_ref, o_ref, st_ref, state):
        @pl.when(pl.program_id(1) == 0)
        def _():
            state[...] = jnp.zeros_like(state)

        row = lax.broadcasted_iota(jnp.int32, (CHUNK, CHUNK), 0)
        col = lax.broadcasted_iota(jnp.int32, (CHUNK, CHUNK), 1)
        causal = row >= col
        tril01 = causal.astype(BF16)

        def chunk(c, carry):
            rows = pl.ds(pl.multiple_of(c * CHUNK, CHUNK), CHUNK)
            for hh in range(HGRN_HPB):
                sl = slice(hh * HEAD_DIM, (hh + 1) * HEAD_DIM)
                q = q_ref[rows, sl]
                v = v_ref[rows, sl].astype(BF16)
                sig, f, kk, sq, eb, emb, eo, dec = _hgrn_chunk_fwd(q, z_ref[rows, sl], lb_ref[:, sl], tril01)
                qi = (q * sq * eb).astype(BF16)
                ki = (kk * emb).astype(BF16)
                ko = (kk * eo).astype(BF16)
                st = state[hh]
                att = jnp.where(causal, _dot_nt(qi, ki), 0.0)
                o_ref[rows, sl] = _dot(att, v) + _dot_nt(qi, st)
                st_ref[c, hh] = st
                state[hh] = st * dec + _dot_tn(v, ko)
            return carry

        lax.fori_loop(0, HGRN_NCH, chunk, 0)

    W = HGRN_HPB * HEAD_DIM
    nG = A_HEADS // HGRN_HPB
    hb = lambda off: pl.BlockSpec((HGRN_TB, W), lambda h, i: (i, off + h))
    return pl.pallas_call(
        body, grid=(nG, nT),
        in_specs=[hb(0), hb(nG), hb(2 * nG), pl.BlockSpec((1, W), lambda h, i: (0, h))],
        out_specs=[hb(0), pl.BlockSpec((HGRN_NCH, HGRN_HPB, HEAD_DIM, HEAD_DIM), lambda h, i: (i, h, 0, 0))],
        out_shape=[jax.ShapeDtypeStruct((T, A_WIDTH), F32), jax.ShapeDtypeStruct((nC, A_HEADS, HEAD_DIM, HEAD_DIM), F32)],
        scratch_shapes=[pltpu.VMEM((HGRN_HPB, HEAD_DIM, HEAD_DIM), F32)],
        compiler_params=_cp("parallel", "arbitrary"), name=name)(proj, proj, proj, lb)


def _hgrn2_bwd(proj, lb, st_all, do, *, name):
    T = proj.shape[0]
    nT = T // HGRN_TB

    def body(q_ref, z_ref, v_ref, lb_ref, st_ref, do_ref, dq_ref, dz_ref, dv_ref, dlb_ref, dstate):
        @pl.when(pl.program_id(1) == 0)
        def _():
            dstate[...] = jnp.zeros_like(dstate)
            dlb_ref[...] = jnp.zeros_like(dlb_ref)

        row = lax.broadcasted_iota(jnp.int32, (CHUNK, CHUNK), 0)
        col = lax.broadcasted_iota(jnp.int32, (CHUNK, CHUNK), 1)
        causal = row >= col
        tril01 = causal.astype(BF16)
        triu01 = (row <= col).astype(BF16)

        def chunk(cc, carry):
            c = HGRN_NCH - 1 - cc
            rows = pl.ds(pl.multiple_of(c * CHUNK, CHUNK), CHUNK)
            for hh in range(HGRN_HPB):
                sl = slice(hh * HEAD_DIM, (hh + 1) * HEAD_DIM)
                lbv = lb_ref[:, sl]
                q = q_ref[rows, sl]
                v = v_ref[rows, sl].astype(BF16)
                sig, f, kk, sq, eb, emb, eo, dec = _hgrn_chunk_fwd(q, z_ref[rows, sl], lbv, tril01)
                qi32 = q * sq * eb
                ki32 = kk * emb
                ko32 = kk * eo
                qi, ki, ko = qi32.astype(BF16), ki32.astype(BF16), ko32.astype(BF16)
                att = jnp.where(causal, _dot_nt(qi, ki), 0.0).astype(BF16)
                dout = do_ref[rows, sl].astype(BF16)
                st = st_ref[c, hh]
                dst = dstate[hh]
                dst16 = dst.astype(BF16)
                datt = jnp.where(causal, _dot_nt(dout, v), 0.0).astype(BF16)
                dqi = _dot(datt, ki) + _dot(dout, st)
                dki = _dot_tn(datt, qi)
                dv_ref[rows, sl] = (_dot_tn(att, dout) + _dot_nt(ko, dst16)).astype(BF16)
                dko = _dot(v, dst16)
                ddec = jnp.sum(dst * st, axis=0, keepdims=True)
                dstate[hh] = dst * dec + _dot_tn(dout, qi)
                dkk = dki * emb + dko * eo
                db = dqi * qi32 - dki * ki32 - dko * ko32
                dbend = jnp.sum(dko * ko32, axis=0, keepdims=True) + ddec * dec
                dlogf = _dot3(triu01, db) + dbend
                df = dlogf / f - dkk
                dz_ref[rows, sl] = (df * (1.0 - lbv) * sig * (1.0 - sig)).astype(BF16)
                dlb_ref[:, sl] += jnp.sum(df * (1.0 - sig), axis=0, keepdims=True)
                dq_ref[rows, sl] = (dqi * eb * (sq * (1.0 + q * (1.0 - sq)))).astype(BF16)
            return carry

        lax.fori_loop(0, HGRN_NCH, chunk, 0)

    W = HGRN_HPB * HEAD_DIM
    nG = A_HEADS // HGRN_HPB
    hb = lambda off: pl.BlockSpec((HGRN_TB, W), lambda h, i: (nT - 1 - i, off + h))
    hlb = pl.BlockSpec((1, W), lambda h, i: (0, h))
    o16 = jax.ShapeDtypeStruct((T, A_WIDTH), BF16)
    return pl.pallas_call(
        body, grid=(nG, nT),
        in_specs=[hb(0), hb(nG), hb(2 * nG), hlb,
                  pl.BlockSpec((HGRN_NCH, HGRN_HPB, HEAD_DIM, HEAD_DIM), lambda h, i: (nT - 1 - i, h, 0, 0)), hb(0)],
        out_specs=[hb(0), hb(0), hb(0), hlb],
        out_shape=[o16, o16, o16, jax.ShapeDtypeStruct((1, A_WIDTH), F32)],
        scratch_shapes=[pltpu.VMEM((HGRN_HPB, HEAD_DIM, HEAD_DIM), F32)],
        compiler_params=_cp("parallel", "arbitrary"), name=name)(proj, proj, proj, lb, st_all, do)


def _head_rms(x):
    r = lax.rsqrt(jnp.mean(x * x, axis=-1, keepdims=True) + EPS)
    return x * r, r


def _head_rms_bwd(dxhat, xhat, r):
    return r * (dxhat - xhat * jnp.mean(dxhat * xhat, axis=-1, keepdims=True))


def _a_post_fwd(o, proj, onorm, *, tt, name):
    T = o.shape[0]

    def body(o_ref, g_ref, w_ref, y_ref):
        for h in range(A_HEADS):
            sl = slice(h * HEAD_DIM, (h + 1) * HEAD_DIM)
            xhat, _ = _head_rms(o_ref[:, sl])
            g = g_ref[:, sl]
            y_ref[:, sl] = xhat * w_ref[:, sl] * (g * _sigmoid(g))

    blk = lambda c: pl.BlockSpec((tt, A_WIDTH), lambda i: (i, c))
    return pl.pallas_call(
        body, grid=(T // tt,), in_specs=[blk(0), blk(3), _full((1, A_WIDTH))], out_specs=blk(0),
        out_shape=jax.ShapeDtypeStruct((T, A_WIDTH), F32),
        compiler_params=_cp("parallel"), name=name)(o, proj, onorm)


def _a_post_bwd(o, proj, onorm, dmix, *, tt, name, dep=None):
    T = o.shape[0]

    def kernel_body(o_ref, g_ref, w_ref, dy_ref, do_ref, dg_ref, dw_ref):
        @pl.when(pl.program_id(0) == 0)
        def _():
            dw_ref[...] = jnp.zeros_like(dw_ref)

        for h in range(A_HEADS):
            sl = slice(h * HEAD_DIM, (h + 1) * HEAD_DIM)
            xhat, r = _head_rms(o_ref[:, sl])
            g = g_ref[:, sl]
            s = _sigmoid(g)
            dy = dy_ref[:, sl]
            w = w_ref[:, sl]
            dg_ref[:, sl] = (dy * xhat * w * (s * (1.0 + g * (1.0 - s)))).astype(BF16)
            dyn = dy * (g * s)
            dw_ref[:, sl] += jnp.sum(dyn * xhat, axis=0, keepdims=True)
            do_ref[:, sl] = _head_rms_bwd(dyn * w, xhat, r)

    blk = lambda c: pl.BlockSpec((tt, A_WIDTH), lambda i: (i, c))
    body, dep_specs, dep_args = _dep(kernel_body, 4, dep)
    return pl.pallas_call(
        body, grid=(T // tt,), in_specs=[blk(0), blk(3), _full((1, A_WIDTH)), blk(0)] + dep_specs,
        out_specs=[blk(0), blk(0), _full((1, A_WIDTH))],
        out_shape=[jax.ShapeDtypeStruct((T, A_WIDTH), F32), jax.ShapeDtypeStruct((T, A_WIDTH), BF16),
                   jax.ShapeDtypeStruct((1, A_WIDTH), F32)],
        compiler_params=_cp("arbitrary"), name=name)(o, proj, onorm, dmix, *dep_args)


def _mem_head_masks(n):
    lane = lax.broadcasted_iota(jnp.int32, (n, MEM_WIDTH), 1)
    return [(lane >= m * MEM_HEAD_DIM) & (lane < (m + 1) * MEM_HEAD_DIM) for m in range(MEM_HEADS)]


def _mem_head_rms(x, masks):
    x2 = x * x
    r = jnp.zeros_like(x)
    for mk in masks:
        ms = jnp.sum(jnp.where(mk, x2, 0.0), axis=-1, keepdims=True) * (1.0 / MEM_HEAD_DIM)
        r = jnp.where(mk, lax.rsqrt(ms + EPS), r)
    return x * r, r


def _mem_head_rms_bwd(dxhat, xhat, r, masks):
    t = dxhat * xhat
    m = jnp.zeros_like(t)
    for mk in masks:
        m = jnp.where(mk, jnp.sum(jnp.where(mk, t, 0.0), axis=-1, keepdims=True) * (1.0 / MEM_HEAD_DIM), m)
    return r * (dxhat - xhat * m)


MEM_SCALE = MEM_HEAD_DIM ** -0.5


def _mem_attn_fwd(proj, qcol, mkv, qn_w, kn_w, *, tt, name):
    T = proj.shape[0]

    def body(q_ref, k_ref, v_ref, qw_ref, kw_ref, o_ref):
        qmasks = _mem_head_masks(tt)
        kmasks = _mem_head_masks(MEM_TOKENS)
        qhat, _ = _mem_head_rms(q_ref[...], qmasks)
        qn = qhat * qw_ref[...]
        khat, _ = _mem_head_rms(k_ref[...], kmasks)
        kn = (khat * kw_ref[...]).astype(BF16)
        v = v_ref[...].astype(BF16)
        out = jnp.zeros((tt, MEM_WIDTH), F32)
        for m in range(MEM_HEADS):
            s = _dot_nt(jnp.where(qmasks[m], qn, 0.0), kn) * MEM_SCALE
            s = s - jnp.max(s, axis=-1, keepdims=True)
            p = jnp.exp(s)
            p = p / jnp.sum(p, axis=-1, keepdims=True)
            out = jnp.where(qmasks[m], _dot(p, v), out)
        o_ref[...] = out

    return pl.pallas_call(
        body, grid=(T // tt,),
        in_specs=[pl.BlockSpec((tt, MEM_WIDTH), lambda i: (i, qcol)), pl.BlockSpec((MEM_TOKENS, MEM_WIDTH), lambda i: (0, 0)),
                  pl.BlockSpec((MEM_TOKENS, MEM_WIDTH), lambda i: (0, 1)), _full((1, MEM_WIDTH)), _full((1, MEM_WIDTH))],
        out_specs=pl.BlockSpec((tt, MEM_WIDTH), lambda i: (i, 0)),
        out_shape=jax.ShapeDtypeStruct((T, MEM_WIDTH), F32),
        compiler_params=_cp("parallel"), name=name)(proj, mkv, mkv, qn_w, kn_w)


def _mem_attn_bwd(proj, qcol, mkv, qn_w, kn_w, dmix, *, tt, name):
    T = proj.shape[0]
    nsteps = T // tt
    ocol = (dmix.shape[1] - MEM_WIDTH) // MEM_WIDTH

    def body(q_ref, k_ref, v_ref, qw_ref, kw_ref, do_ref, dq_ref, dkv_ref, dqw_ref, dkw_ref, dk_acc, dv_acc):
        step = pl.program_id(0)

        @pl.when(step == 0)
        def _():
            dk_acc[...] = jnp.zeros_like(dk_acc)
            dv_acc[...] = jnp.zeros_like(dv_acc)
            dqw_ref[...] = jnp.zeros_like(dqw_ref)

        qmasks = _mem_head_masks(tt)
        kmasks = _mem_head_masks(MEM_TOKENS)
        qhat, qr = _mem_head_rms(q_ref[...], qmasks)
        qn = qhat * qw_ref[...]
        khat, kr = _mem_head_rms(k_ref[...], kmasks)
        kn = (khat * kw_ref[...]).astype(BF16)
        v = v_ref[...].astype(BF16)
        dout = do_ref[...]
        dqn = jnp.zeros((tt, MEM_WIDTH), F32)
        dkn = jnp.zeros((MEM_TOKENS, MEM_WIDTH), F32)
        dvv = jnp.zeros((MEM_TOKENS, MEM_WIDTH), F32)
        for m in range(MEM_HEADS):
            qm = jnp.where(qmasks[m], qn, 0.0).astype(BF16)
            s = _dot_nt(qm, kn) * MEM_SCALE
            s = s - jnp.max(s, axis=-1, keepdims=True)
            p = jnp.exp(s)
            p = p / jnp.sum(p, axis=-1, keepdims=True)
            dom = jnp.where(qmasks[m], dout, 0.0).astype(BF16)
            dp = _dot_nt(dom, v)
            ds = (p * (dp - jnp.sum(p * dp, axis=-1, keepdims=True)) * MEM_SCALE).astype(BF16)
            dqn = jnp.where(qmasks[m], _dot(ds, kn), dqn)
            dkn = jnp.where(kmasks[m], _dot_tn(ds, qm), dkn)
            dvv = jnp.where(kmasks[m], _dot_tn(p, dom), dvv)
        dqw_ref[...] += jnp.sum(dqn * qhat, axis=0, keepdims=True)
        dq_ref[...] = _mem_head_rms_bwd(dqn * qw_ref[...], qhat, qr, qmasks).astype(BF16)
        dk_acc[...] += dkn
        dv_acc[...] += dvv

        @pl.when(step == nsteps - 1)
        def _():
            dk = dk_acc[...]
            dkw_ref[...] = jnp.sum(dk * khat, axis=0, keepdims=True)
            dkv_ref[:, :MEM_WIDTH] = _mem_head_rms_bwd(dk * kw_ref[...], khat, kr, kmasks)
            dkv_ref[:, MEM_WIDTH:] = dv_acc[...]

    return pl.pallas_call(
        body, grid=(nsteps,),
        in_specs=[pl.BlockSpec((tt, MEM_WIDTH), lambda i: (i, qcol)), pl.BlockSpec((MEM_TOKENS, MEM_WIDTH), lambda i: (0, 0)),
                  pl.BlockSpec((MEM_TOKENS, MEM_WIDTH), lambda i: (0, 1)), _full((1, MEM_WIDTH)), _full((1, MEM_WIDTH)),
                  pl.BlockSpec((tt, MEM_WIDTH), lambda i: (i, ocol))],
        out_specs=[pl.BlockSpec((tt, MEM_WIDTH), lambda i: (i, 0)), _full((MEM_TOKENS, 2 * MEM_WIDTH)),
                   _full((1, MEM_WIDTH)), _full((1, MEM_WIDTH))],
        out_shape=[jax.ShapeDtypeStruct((T, MEM_WIDTH), BF16), jax.ShapeDtypeStruct((MEM_TOKENS, 2 * MEM_WIDTH), F32),
                   jax.ShapeDtypeStruct((1, MEM_WIDTH), F32), jax.ShapeDtypeStruct((1, MEM_WIDTH), F32)],
        scratch_shapes=[pltpu.VMEM((MEM_TOKENS, MEM_WIDTH), F32), pltpu.VMEM((MEM_TOKENS, MEM_WIDTH), F32)],
        compiler_params=_cp("arbitrary"), name=name)(proj, mkv, mkv, qn_w, kn_w, dmix)


HALF = HEAD_DIM // 2
ATT_SCALE = HEAD_DIM ** -0.5
NEG = -1e30


def _rope_tables(T):
    inv = ROPE_THETA ** (-jnp.arange(HALF, dtype=F32) / HALF)
    ang = jnp.arange(T, dtype=F32)[:, None] * inv[None, :]
    cos, sin = jnp.cos(ang), jnp.sin(ang)
    return jnp.concatenate([cos, cos], axis=-1), jnp.concatenate([-sin, sin], axis=-1)


def _rope(x, cosf, sinsg):
    return x * cosf + pltpu.roll(x, HALF, 1) * sinsg


def _rope_bwd(dy, cosf, sinsg):
    return dy * cosf + pltpu.roll(dy * sinsg, HALF, 1)


def _headnorm_rope_fwd(x, w_heads, cosf, sinsg, *, col0, n_heads, tt, name):
    T = x.shape[0]
    W = n_heads * HEAD_DIM

    def body(x_ref, w_ref, c_ref, s_ref, y_ref):
        c, s = c_ref[...], s_ref[...]
        for h in range(n_heads):
            sl = slice(h * HEAD_DIM, (h + 1) * HEAD_DIM)
            xhat, _ = _head_rms(x_ref[:, sl])
            y_ref[:, sl] = _rope(xhat * w_ref[:, sl], c, s)

    tbl = pl.BlockSpec((tt, HEAD_DIM), lambda i: (i, 0))
    return pl.pallas_call(
        body, grid=(T // tt,),
        in_specs=[pl.BlockSpec((tt, W), lambda i: (i, col0)), _full((1, W)), tbl, tbl],
        out_specs=pl.BlockSpec((tt, W), lambda i: (i, 0)),
        out_shape=jax.ShapeDtypeStruct((T, W), F32),
        compiler_params=_cp("parallel"), name=name)(x, w_heads, cosf, sinsg)


def _q_prep_bwd(proj, w_heads, cosf, sinsg, dqs, *, tt, name):
    T = proj.shape[0]
    W = N_GROUPS * B_WIDTH

    def body(x_ref, w_ref, c_ref, s_ref, d0, d1, d2, dx_ref, dw_ref):
        @pl.when(pl.program_id(0) == 0)
        def _():
            dw_ref[...] = jnp.zeros_like(dw_ref)

        c, s = c_ref[...], s_ref[...]
        for gi, d_ref in enumerate((d0, d1, d2)):
            for h in range(B_HEADS):
                sl = slice((gi * B_HEADS + h) * HEAD_DIM, (gi * B_HEADS + h + 1) * HEAD_DIM)
                xhat, r = _head_rms(x_ref[:, sl])
                dyn = _rope_bwd(d_ref[:, h * HEAD_DIM:(h + 1) * HEAD_DIM], c, s)
                dw_ref[:, sl] += jnp.sum(dyn * xhat, axis=0, keepdims=True)
                dx_ref[:, sl] = _head_rms_bwd(dyn * w_ref[:, sl], xhat, r).astype(BF16)

    tbl = pl.BlockSpec((tt, HEAD_DIM), lambda i: (i, 0))
    dyb = pl.BlockSpec((tt, B_WIDTH), lambda i: (i, 0))
    return pl.pallas_call(
        body, grid=(T // tt,),
        in_specs=[pl.BlockSpec((tt, W), lambda i: (i, 0)), _full((1, W)), tbl, tbl, dyb, dyb, dyb],
        out_specs=[pl.BlockSpec((tt, W), lambda i: (i, 0)), _full((1, W))],
        out_shape=[jax.ShapeDtypeStruct((T, W), BF16), jax.ShapeDtypeStruct((1, W), F32)],
        compiler_params=_cp("arbitrary"), name=name)(proj, w_heads, cosf, sinsg, *dqs)


def _kv_prep_bwd(kv, w_heads, cosf, sinsg, dks, dvs, *, tt, name):
    T = kv.shape[0]

    def body(x_ref, w_ref, c_ref, s_ref, k0, k1, k2, v0, v1, v2, dx_ref, dw_ref):
        @pl.when(pl.program_id(0) == 0)
        def _():
            dw_ref[...] = jnp.zeros_like(dw_ref)

        c, s = c_ref[...], s_ref[...]
        for h in range(B_HEADS):
            sl = slice(h * HEAD_DIM, (h + 1) * HEAD_DIM)
            vs = slice(B_WIDTH + h * HEAD_DIM, B_WIDTH + (h + 1) * HEAD_DIM)
            xhat, r = _head_rms(x_ref[:, sl])
            dyn = _rope_bwd(k0[:, sl] + k1[:, sl] + k2[:, sl], c, s)
            dw_ref[:, sl] += jnp.sum(dyn * xhat, axis=0, keepdims=True)
            dx_ref[:, sl] = _head_rms_bwd(dyn * w_ref[:, sl], xhat, r).astype(BF16)
            dx_ref[:, vs] = (v0[:, sl] + v1[:, sl] + v2[:, sl]).astype(BF16)

    tbl = pl.BlockSpec((tt, HEAD_DIM), lambda i: (i, 0))
    dyb = pl.BlockSpec((tt, B_WIDTH), lambda i: (i, 0))
    return pl.pallas_call(
        body, grid=(T // tt,),
        in_specs=[dyb, _full((1, B_WIDTH)), tbl, tbl] + [dyb] * 6,
        out_specs=[pl.BlockSpec((tt, 2 * B_WIDTH), lambda i: (i, 0)), _full((1, B_WIDTH))],
        out_shape=[jax.ShapeDtypeStruct((T, 2 * B_WIDTH), BF16), jax.ShapeDtypeStruct((1, B_WIDTH), F32)],
        compiler_params=_cp("arbitrary"), name=name)(kv, w_heads, cosf, sinsg, *dks, *dvs)


def _band_masks(n_is_first=None):
    row = lax.broadcasted_iota(jnp.int32, (SPAN, SPAN), 0)
    col = lax.broadcasted_iota(jnp.int32, (SPAN, SPAN), 1)
    return row >= col, col >= row


def _dil_views(T, d):
    L = T // d
    return L, L // SPAN


def _dil_fwd(qr, kr, kv, gi, d, *, name):
    T = qr.shape[0]
    L, nb = _dil_views(T, d)

    def body(q_ref, kc_ref, kp_ref, vc_ref, vp_ref, o_ref, lse_ref):
        cur_ok, prev_band = _band_masks()
        prev_ok = prev_band & (pl.program_id(1) > 0)
        for h in range(B_HEADS):
            sl = slice(h * HEAD_DIM, (h + 1) * HEAD_DIM)
            q = q_ref[:, sl]
            sc = jnp.where(cur_ok, _dot_nt(q, kc_ref[:, sl]) * ATT_SCALE, NEG)
            sp = jnp.where(prev_ok, _dot_nt(q, kp_ref[:, sl]) * ATT_SCALE, NEG)
            m = jnp.maximum(jnp.max(sc, axis=-1, keepdims=True), jnp.max(sp, axis=-1, keepdims=True))
            pc = jnp.exp(sc - m)
            pp = jnp.exp(sp - m)
            l = jnp.sum(pc, axis=-1, keepdims=True) + jnp.sum(pp, axis=-1, keepdims=True)
            o_ref[:, sl] = (_dot(pc, vc_ref[:, sl]) + _dot(pp, vp_ref[:, sl])) / l
            lse_ref[:, sl] = jnp.broadcast_to(m + jnp.log(l), (SPAN, HEAD_DIM))

    blk = lambda f: pl.BlockSpec((SPAN, B_WIDTH), f)
    cur = lambda r, n: (n, r)
    prev = lambda r, n: (jnp.maximum(n - 1, 0), r)
    ov = jax.ShapeDtypeStruct((L, d * B_WIDTH), F32)
    o, lse = pl.pallas_call(
        body, grid=(d, nb),
        in_specs=[blk(lambda r, n: (n, r * N_GROUPS + gi)), blk(cur), blk(prev),
                  blk(lambda r, n: (n, 2 * r + 1)), blk(lambda r, n: (jnp.maximum(n - 1, 0), 2 * r + 1))],
        out_specs=[blk(cur), blk(cur)], out_shape=[ov, ov],
        compiler_params=_cp("parallel", "arbitrary"), name=name,
    )(qr.reshape(L, d * N_GROUPS * B_WIDTH), kr.reshape(L, d * B_WIDTH), kr.reshape(L, d * B_WIDTH),
      kv.reshape(L, d * 2 * B_WIDTH), kv.reshape(L, d * 2 * B_WIDTH))
    return o.reshape(T, B_WIDTH), lse.reshape(T, B_WIDTH)


def _dil_combine_fwd(os_, lses, *, tt, name):
    T = os_[0].shape[0]

    def body(o0, o1, o2, l0, l1, l2, y_ref, lse_ref):
        a, b, c = l0[...], l1[...], l2[...]
        m = jnp.maximum(jnp.maximum(a, b), c)
        wa, wb, wc = jnp.exp(a - m), jnp.exp(b - m), jnp.exp(c - m)
        den = wa + wb + wc
        y_ref[...] = (wa * o0[...] + wb * o1[...] + wc * o2[...]) / den
        lse_ref[...] = m + jnp.log(den)

    blk = pl.BlockSpec((tt, B_WIDTH), lambda i: (i, 0))
    sh = jax.ShapeDtypeStruct((T, B_WIDTH), F32)
    return pl.pallas_call(
        body, grid=(T // tt,), in_specs=[blk] * 6, out_specs=[blk, blk], out_shape=[sh, sh],
        compiler_params=_cp("parallel"), name=name)(*os_, *lses)


def _dil_bwd_prep(dmix, mix_main, *, tt, name, dep=None):
    T = mix_main.shape[0]

    def kernel_body(dy_ref, y_ref, dmm_ref, dd_ref):
        for h in range(B_HEADS):
            sl = slice(h * HEAD_DIM, (h + 1) * HEAD_DIM)
            dy = dy_ref[:, sl]
            dmm_ref[:, sl] = dy.astype(BF16)
            dd_ref[:, sl] = jnp.broadcast_to(jnp.sum(dy * y_ref[:, sl], axis=-1, keepdims=True), (tt, HEAD_DIM))

    blk = pl.BlockSpec((tt, B_WIDTH), lambda i: (i, 0))
    body, dep_specs, dep_args = _dep(kernel_body, 2, dep)
    return pl.pallas_call(
        body, grid=(T // tt,), in_specs=[blk, blk] + dep_specs, out_specs=[blk, blk],
        out_shape=[jax.ShapeDtypeStruct((T, B_WIDTH), BF16), jax.ShapeDtypeStruct((T, B_WIDTH), F32)],
        compiler_params=_cp("parallel"), name=name)(dmix, mix_main, *dep_args)


def _dil_bwd_dq(qr, kr, kv, dmm, lse, dd, gi, d, *, name):
    T = qr.shape[0]
    L, nb = _dil_views(T, d)

    def body(q_ref, kc_ref, kp_ref, vc_ref, vp_ref, dy_ref, lse_ref, dd_ref, dq_ref):
        cur_ok, prev_band = _band_masks()
        prev_ok = prev_band & (pl.program_id(1) > 0)
        for h in range(B_HEADS):
            sl = slice(h * HEAD_DIM, (h + 1) * HEAD_DIM)
            q, dy = q_ref[:, sl], dy_ref[:, sl]
            kc, kp = kc_ref[:, sl], kp_ref[:, sl]
            lse_h = jnp.max(lse_ref[:, sl], axis=-1, keepdims=True)
            dd_h = jnp.max(dd_ref[:, sl], axis=-1, keepdims=True)
            pc = jnp.exp(jnp.where(cur_ok, _dot_nt(q, kc) * ATT_SCALE, NEG) - lse_h)
            pp = jnp.exp(jnp.where(prev_ok, _dot_nt(q, kp) * ATT_SCALE, NEG) - lse_h)
            dsc = pc * (_dot_nt(dy, vc_ref[:, sl]) - dd_h) * ATT_SCALE
            dsp = pp * (_dot_nt(dy, vp_ref[:, sl]) - dd_h) * ATT_SCALE
            dq_ref[:, sl] = _dot(dsc, kc) + _dot(dsp, kp)

    blk = lambda f: pl.BlockSpec((SPAN, B_WIDTH), f)
    cur = lambda r, n: (n, r)
    prev = lambda r, n: (jnp.maximum(n - 1, 0), r)
    v2 = lambda a: a.reshape(L, d * a.shape[1])
    dq = pl.pallas_call(
        body, grid=(d, nb),
        in_specs=[blk(lambda r, n: (n, r * N_GROUPS + gi)), blk(cur), blk(prev),
                  blk(lambda r, n: (n, 2 * r + 1)), blk(lambda r, n: (jnp.maximum(n - 1, 0), 2 * r + 1)),
                  blk(cur), blk(cur), blk(cur)],
        out_specs=blk(cur), out_shape=jax.ShapeDtypeStruct((L, d * B_WIDTH), F32),
        compiler_params=_cp("parallel", "arbitrary"), name=name,
    )(v2(qr), v2(kr), v2(kr), v2(kv), v2(kv), v2(dmm), v2(lse), v2(dd))
    return dq.reshape(T, B_WIDTH)


def _dil_bwd_dkv(qr, kr, kv, dmm, lse, dd, gi, d, *, name):
    T = qr.shape[0]
    L, nb = _dil_views(T, d)

    def body(k_ref, v_ref, q0_ref, q1_ref, dy0_ref, dy1_ref, lse0_ref, lse1_ref, dd0_ref, dd1_ref, dk_ref, dv_ref):
        cur_ok, prev_band = _band_masks()
        next_ok = prev_band & (pl.program_id(1) < nb - 1)
        for h in range(B_HEADS):
            sl = slice(h * HEAD_DIM, (h + 1) * HEAD_DIM)
            k, v = k_ref[:, sl], v_ref[:, sl]
            dk = jnp.zeros((SPAN, HEAD_DIM), F32)
            dv = jnp.zeros((SPAN, HEAD_DIM), F32)
            for ok, q_ref, dy_ref, lse_ref, dd_ref in ((cur_ok, q0_ref, dy0_ref, lse0_ref, dd0_ref),
                                                         (next_ok, q1_ref, dy1_ref, lse1_ref, dd1_ref)):
                q, dy = q_ref[:, sl], dy_ref[:, sl]
                lse_h = jnp.max(lse_ref[:, sl], axis=-1, keepdims=True)
                dd_h = jnp.max(dd_ref[:, sl], axis=-1, keepdims=True)
                p = jnp.exp(jnp.where(ok, _dot_nt(q, k) * ATT_SCALE, NEG) - lse_h)
                ds = p * (_dot_nt(dy, v) - dd_h) * ATT_SCALE
                dk = dk + _dot_tn(ds, q)
                dv = dv + _dot_tn(p, dy)
            dk_ref[:, sl] = dk
            dv_ref[:, sl] = dv

    blk = lambda f: pl.BlockSpec((SPAN, B_WIDTH), f)
    cur = lambda r, n: (n, r)
    nxt = lambda r, n: (jnp.minimum(n + 1, nb - 1), r)
    qcur = lambda r, n: (n, r * N_GROUPS + gi)
    qnxt = lambda r, n: (jnp.minimum(n + 1, nb - 1), r * N_GROUPS + gi)
    v2 = lambda a: a.reshape(L, d * a.shape[1])
    ov = jax.ShapeDtypeStruct((L, d * B_WIDTH), F32)
    dk, dv = pl.pallas_call(
        body, grid=(d, nb),
        in_specs=[blk(cur), blk(lambda r, n: (n, 2 * r + 1)), blk(qcur), blk(qnxt),
                  blk(cur), blk(nxt), blk(cur), blk(nxt), blk(cur), blk(nxt)],
        out_specs=[blk(cur), blk(cur)], out_shape=[ov, ov],
        compiler_params=_cp("parallel", "arbitrary"), name=name,
    )(v2(kr), v2(kv), v2(qr), v2(qr), v2(dmm), v2(dmm), v2(lse), v2(lse), v2(dd), v2(dd))
    return dk.reshape(T, B_WIDTH), dv.reshape(T, B_WIDTH)


DILS_UNROLL = 4


def _dils_specs(gi, d, nblk):
    blk = lambda f: pl.BlockSpec((SPAN * d, HEAD_DIM), f)
    return {
        "q": blk(lambda h, n: (n, gi * B_HEADS + h)), "q_next": blk(lambda h, n: (jnp.minimum(n + 1, nblk - 1), gi * B_HEADS + h)),
        "cur": blk(lambda h, n: (n, h)), "prev": blk(lambda h, n: (jnp.maximum(n - 1, 0), h)),
        "next": blk(lambda h, n: (jnp.minimum(n + 1, nblk - 1), h)),
        "v": blk(lambda h, n: (n, B_HEADS + h)), "v_prev": blk(lambda h, n: (jnp.maximum(n - 1, 0), B_HEADS + h)),
    }


def _dils_fwd(qr, kr, kv, gi, d, *, name):
    T = qr.shape[0]
    nblk = T // (SPAN * d)
    sp = _dils_specs(gi, d, nblk)

    def body(q_ref, kc_ref, kp_ref, vc_ref, vp_ref, o_ref, lse_ref):
        cur_ok, prev_band = _band_masks()
        prev_ok = prev_band & (pl.program_id(1) > 0)

        def residue(r, carry):
            rows = pl.ds(r, SPAN, stride=d)
            q = q_ref[rows, :]
            sc = jnp.where(cur_ok, _dot_nt(q, kc_ref[rows, :]) * ATT_SCALE, NEG)
            sp_ = jnp.where(prev_ok, _dot_nt(q, kp_ref[rows, :]) * ATT_SCALE, NEG)
            m = jnp.maximum(jnp.max(sc, axis=-1, keepdims=True), jnp.max(sp_, axis=-1, keepdims=True))
            pc = jnp.exp(sc - m)
            pp = jnp.exp(sp_ - m)
            l = jnp.sum(pc, axis=-1, keepdims=True) + jnp.sum(pp, axis=-1, keepdims=True)
            o_ref[rows, :] = (_dot(pc, vc_ref[rows, :]) + _dot(pp, vp_ref[rows, :])) / l
            lse_ref[rows, :] = jnp.broadcast_to(m + jnp.log(l), (SPAN, HEAD_DIM))
            return carry

        lax.fori_loop(0, d, residue, 0, unroll=DILS_UNROLL)

    sh = jax.ShapeDtypeStruct((T, B_WIDTH), F32)
    return pl.pallas_call(
        body, grid=(B_HEADS, nblk), in_specs=[sp["q"], sp["cur"], sp["prev"], sp["v"], sp["v_prev"]],
        out_specs=[sp["cur"], sp["cur"]], out_shape=[sh, sh],
        compiler_params=_cp("parallel", "arbitrary"), name=name)(qr, kr, kr, kv, kv)


def _dils_bwd_dq(qr, kr, kv, dmix, lse, dd, gi, d, *, name):
    T = qr.shape[0]
    nblk = T // (SPAN * d)
    sp = _dils_specs(gi, d, nblk)

    def body(q_ref, kc_ref, kp_ref, vc_ref, vp_ref, dy_ref, lse_ref, dd_ref, dq_ref):
        cur_ok, prev_band = _band_masks()
        prev_ok = prev_band & (pl.program_id(1) > 0)

        def residue(r, carry):
            rows = pl.ds(r, SPAN, stride=d)
            q, dy = q_ref[rows, :], dy_ref[rows, :]
            kc, kp = kc_ref[rows, :], kp_ref[rows, :]
            lse_h = jnp.max(lse_ref[rows, :], axis=-1, keepdims=True)
            dd_h = jnp.max(dd_ref[rows, :], axis=-1, keepdims=True)
            pc = jnp.exp(jnp.where(cur_ok, _dot_nt(q, kc) * ATT_SCALE, NEG) - lse_h)
            pp = jnp.exp(jnp.where(prev_ok, _dot_nt(q, kp) * ATT_SCALE, NEG) - lse_h)
            dsc = pc * (_dot_nt(dy, vc_ref[rows, :]) - dd_h) * ATT_SCALE
            dsp = pp * (_dot_nt(dy, vp_ref[rows, :]) - dd_h) * ATT_SCALE
            dq_ref[rows, :] = _dot(dsc, kc) + _dot(dsp, kp)
            return carry

        lax.fori_loop(0, d, residue, 0, unroll=DILS_UNROLL)

    return pl.pallas_call(
        body, grid=(B_HEADS, nblk),
        in_specs=[sp["q"], sp["cur"], sp["prev"], sp["v"], sp["v_prev"], sp["cur"], sp["cur"], sp["cur"]],
        out_specs=sp["cur"], out_shape=jax.ShapeDtypeStruct((T, B_WIDTH), F32),
        compiler_params=_cp("parallel", "arbitrary"), name=name)(qr, kr, kr, kv, kv, dmix, lse, dd)


def _dils_bwd_dkv(qr, kr, kv, dmix, lse, dd, gi, d, *, name):
    T = qr.shape[0]
    nblk = T // (SPAN * d)
    sp = _dils_specs(gi, d, nblk)

    def body(k_ref, v_ref, q0_ref, q1_ref, dy0_ref, dy1_ref, lse0_ref, lse1_ref, dd0_ref, dd1_ref, dk_ref, dv_ref):
        cur_ok, prev_band = _band_masks()
        next_ok = prev_band & (pl.program_id(1) < nblk - 1)

        def residue(r, carry):
            rows = pl.ds(r, SPAN, stride=d)
            k, v = k_ref[rows, :], v_ref[rows, :]
            dk = jnp.zeros((SPAN, HEAD_DIM), F32)
            dv = jnp.zeros((SPAN, HEAD_DIM), F32)
            for ok, q_ref, dy_ref, lse_ref, dd_ref in ((cur_ok, q0_ref, dy0_ref, lse0_ref, dd0_ref),
                                                         (next_ok, q1_ref, dy1_ref, lse1_ref, dd1_ref)):
                q, dy = q_ref[rows, :], dy_ref[rows, :]
                lse_h = jnp.max(lse_ref[rows, :], axis=-1, keepdims=True)
                dd_h = jnp.max(dd_ref[rows, :], axis=-1, keepdims=True)
                p = jnp.exp(jnp.where(ok, _dot_nt(q, k) * ATT_SCALE, NEG) - lse_h)
                ds = p * (_dot_nt(dy, v) - dd_h) * ATT_SCALE
                dk = dk + _dot_tn(ds, q)
                dv = dv + _dot_tn(p, dy)
            dk_ref[rows, :] = dk
            dv_ref[rows, :] = dv
            return carry

        lax.fori_loop(0, d, residue, 0, unroll=DILS_UNROLL)

    sh = jax.ShapeDtypeStruct((T, B_WIDTH), F32)
    return pl.pallas_call(
        body, grid=(B_HEADS, nblk),
        in_specs=[sp["cur"], sp["v"], sp["q"], sp["q_next"], sp["cur"], sp["next"], sp["cur"], sp["next"], sp["cur"], sp["next"]],
        out_specs=[sp["cur"], sp["cur"]], out_shape=[sh, sh],
        compiler_params=_cp("parallel", "arbitrary"), name=name)(kr, kv, qr, qr, dmix, dmix, lse, lse, dd, dd)


A_MQ_COL = 4 * A_WIDTH // MEM_WIDTH
B_MQ_COL = N_GROUPS * B_WIDTH // MEM_WIDTH


def _row(v):
    return v.reshape(1, -1).astype(F32)


def _local_step(x, mem, tgt, get_w, P, put_g, first_dep=None):
    T = x.shape[0]
    cosf, sinsg = _rope_tables(T)
    lb_soft = jax.nn.softmax(P["a_lb_logits"].astype(F32), axis=0)
    lb = lb_soft[0:1]
    qw_heads = jnp.repeat(P["b_qnorm"][0], B_HEADS, axis=0).reshape(1, -1)
    kw_heads = jnp.tile(_row(P["b_knorm"]), (1, B_HEADS))
    mqw = [jnp.tile(_row(P["mem_qnorm"][l]), (1, MEM_HEADS)) for l in range(2)]
    mkw = [jnp.tile(_row(P["mem_knorm"][l]), (1, MEM_HEADS)) for l in range(2)]
    nmix = [_row(P["norm_mix"][l]) for l in range(2)]
    nffn = [_row(P["norm_ffn"][l]) for l in range(2)]
    mnorm = [_row(P["mem_norm"][l]) for l in range(2)]
    kvn = _row(P["kv_norm"])
    onorm = _row(P["a_onorm"])
    W = {}

    def w_of(name, after=None):
        if name not in W:
            W[name] = get_w(name, after)
        return W[name]

    proj_a, xn0 = _rms_matmul(x, nmix[0], w_of("a_w_in"), tt=512, tn=1664, wt=True, name="proj_a", dep=first_dep)
    mkv0, mn0 = _rms_matmul(mem, mnorm[0], w_of("w_mem_kv0"), tt=MEM_TOKENS, tn=2 * MEM_WIDTH, wt=False, name="mem_kv0")
    o_raw, st = _hgrn2_fwd(proj_a, lb, name="hgrn2_fwd")
    mm0 = _a_post_fwd(o_raw, proj_a, onorm, tt=512, name="a_post_fwd")
    mo0 = _mem_attn_fwd(proj_a, A_MQ_COL, mkv0, mqw[0], mkw[0], tt=512, name="mem_attn_fwd0")
    mix0 = jnp.concatenate([mm0, mo0], axis=1)
    hm0 = _mm_res(x, mix0, w_of("w_out0", mix0), tt=512, name="out_proj0")
    gu0, hn0 = _rms_matmul(hm0, nffn[0], w_of("w_gate_up0", hm0), tt=512, tn=1408, wt=True, out_dtype=BF16, name="gate_up0")
    h1 = _swiglu_down(hm0, gu0, w_of("w_down0", gu0), tt=256, name="down0")
    kv, hkn = _rms_matmul(h1, kvn, w_of("w_kv", h1), tt=512, tn=768, wt=True, name="kv_proj")
    kr = _headnorm_rope_fwd(kv, kw_heads, cosf, sinsg, col0=0, n_heads=B_HEADS, tt=512, name="k_prep")

    proj_b, xn1 = _rms_matmul(h1, nmix[1], w_of("b_w_in", kr), tt=512, tn=1280, wt=True, name="proj_b")
    mkv1, mn1 = _rms_matmul(mem, mnorm[1], w_of("w_mem_kv1", kr), tt=MEM_TOKENS, tn=2 * MEM_WIDTH, wt=False, name="mem_kv1")
    qr = _headnorm_rope_fwd(proj_b, qw_heads, cosf, sinsg, col0=0, n_heads=N_GROUPS * B_HEADS, tt=512, name="q_prep")
    outs = [(_dil_fwd if d == 1 else _dils_fwd)(qr, kr, kv, gi, d, name=f"dil_fwd{gi}") for gi, d in enumerate(DILATIONS)]
    mm1, lse_tot = _dil_combine_fwd([o for o, _ in outs], [s for _, s in outs], tt=512, name="dil_combine")
    mo1 = _mem_attn_fwd(proj_b, B_MQ_COL, mkv1, mqw[1], mkw[1], tt=512, name="mem_attn_fwd1")
    mix1 = jnp.concatenate([mm1, mo1], axis=1)
    hm1 = _mm_res(h1, mix1, w_of("w_out1", mix1), tt=512, name="out_proj1")
    gu1, hn1 = _rms_matmul(hm1, nffn[1], w_of("w_gate_up1", hm1), tt=512, tn=1408, wt=True, out_dtype=BF16, name="gate_up1")
    y = _swiglu_down(hm1, gu1, w_of("w_down1", gu1), tt=256, name="down1")
    dy, sq = _loss_kernel(y, tgt, tt=512, name="loss")

    gP = {}
    zeros_mem = jnp.zeros((MEM_TOKENS, D_MODEL), F32)

    def ffn_bwd(l, dh, hm, gu, hn):
        dgu, act = _swiglu_bwd(dh, gu, w_of(f"w_down{l}"), tt=256, name=f"swiglu_bwd{l}")
        g_wd = _mm_tn(act, dh, tt=512, tka=1408, name=f"g_w_down{l}")
        g_wgu = _mm_tn(dgu, hn, tt=512, tka=1408, name=f"g_w_gate_up{l}")
        sent = put_g({f"w_down{l}": g_wd, f"w_gate_up{l}": g_wgu})
        dhm, g_nf = _rms_bwd_dx(hm, nffn[l], w_of(f"w_gate_up{l}"), dgu, dh, tt=256, wt=True, name=f"gate_up_bwd{l}", dep=sent)
        return dhm, g_nf

    def mix_bwd(l, dhm, mix, proj, qcol, mkv, mn):
        dmix = _mm_nt(dhm, w_of(f"w_out{l}"), tt=512, name=f"out_proj_bwd{l}")
        g_wout = _mm_tn(mix, dhm, tt=512, tka=512, name=f"g_w_out{l}")
        dmq, dmkv, dqw, dkw = _mem_attn_bwd(proj, qcol, mkv, mqw[l], mkw[l], dmix, tt=512, name=f"mem_attn_bwd{l}")
        g_wmkv = _mm_tn(mn, dmkv, tt=MEM_TOKENS, tka=512, name=f"g_w_mem_kv{l}")
        sent = put_g({f"w_out{l}": g_wout, f"w_mem_kv{l}": g_wmkv})
        _, g_mn = _rms_bwd_dx(mem, mnorm[l], w_of(f"w_mem_kv{l}"), dmkv, zeros_mem, tt=MEM_TOKENS, wt=False, name=f"mem_kv_bwd{l}")
        fold = lambda v: v.reshape(MEM_HEADS, MEM_HEAD_DIM).sum(axis=0)
        return dmix, dmq, g_mn, fold(dqw), fold(dkw), sent

    dhm1, g_nf1 = ffn_bwd(1, dy, hm1, gu1, hn1)
    dmix1, dmq1, g_mn1, g_mq1, g_mk1, sent = mix_bwd(1, dhm1, mix1, proj_b, B_MQ_COL, mkv1, mn1)
    dmm, dd = _dil_bwd_prep(dmix1, mm1, tt=512, name="dil_bwd_prep", dep=sent)
    dqs, dks, dvs = [], [], []
    for gi, d in enumerate(DILATIONS):
        if d == 1:
            dqs.append(_dil_bwd_dq(qr, kr, kv, dmm, lse_tot, dd, gi, d, name=f"dil_bwd_dq{gi}"))
            dk_g, dv_g = _dil_bwd_dkv(qr, kr, kv, dmm, lse_tot, dd, gi, d, name=f"dil_bwd_dkv{gi}")
        else:
            dqs.append(_dils_bwd_dq(qr, kr, kv, dmix1, lse_tot, dd, gi, d, name=f"dil_bwd_dq{gi}"))
            dk_g, dv_g = _dils_bwd_dkv(qr, kr, kv, dmix1, lse_tot, dd, gi, d, name=f"dil_bwd_dkv{gi}")
        dks.append(dk_g)
        dvs.append(dv_g)
    dq_raw, dqw = _q_prep_bwd(proj_b, qw_heads, cosf, sinsg, dqs, tt=512, name="q_prep_bwd")
    dkv, dkw = _kv_prep_bwd(kv, kw_heads, cosf, sinsg, dks, dvs, tt=512, name="kv_prep_bwd")
    dproj_b = jnp.concatenate([dq_raw, dmq1], axis=1)
    g_wb = _mm_tn(dproj_b, xn1, tt=512, tka=1280, name="g_b_w_in")
    g_wkv = _mm_tn(dkv, hkn, tt=512, tka=768, name="g_w_kv")
    sent = put_g({"b_w_in": g_wb, "w_kv": g_wkv})
    dh1, g_nm1 = _rms_bwd_dx(h1, nmix[1], w_of("b_w_in"), dproj_b, dhm1, tt=256, wt=True, name="proj_b_bwd", dep=sent)
    dh1, g_kvn = _rms_bwd_dx(h1, kvn, w_of("w_kv"), dkv, dh1, tt=256, wt=True, name="kv_proj_bwd")

    dhm0, g_nf0 = ffn_bwd(0, dh1, hm0, gu0, hn0)
    dmix0, dmq0, g_mn0, g_mq0, g_mk0, sent = mix_bwd(0, dhm0, mix0, proj_a, A_MQ_COL, mkv0, mn0)
    do_raw, dg, g_onorm = _a_post_bwd(o_raw, proj_a, onorm, dmix0, tt=512, name="a_post_bwd", dep=sent)
    dq, dz, dv, dlb = _hgrn2_bwd(proj_a, lb, st, do_raw, name="hgrn2_bwd")
    dproj_a = jnp.concatenate([dq, dz, dv, dg, dmq0], axis=1)
    sent = put_g({"a_w_in": _mm_tn(dproj_a, xn0, tt=512, tka=1664, name="g_a_w_in")})
    gx, g_nm0 = _rms_bwd_dx(x, nmix[0], w_of("a_w_in"), dproj_a, dhm0, tt=256, wt=True, name="proj_a_bwd", dep=sent)

    dl0 = lb_soft[0:1] * lb_soft[1:2] * dlb
    gP["a_lb_logits"] = jnp.concatenate([dl0, -dl0], axis=0)
    gP["a_onorm"] = g_onorm
    gP["norm_mix"] = jnp.concatenate([g_nm0, g_nm1], axis=0)
    gP["norm_ffn"] = jnp.concatenate([g_nf0, g_nf1], axis=0)
    gP["b_qnorm"] = dqw.reshape(N_GROUPS, B_HEADS, HEAD_DIM).sum(axis=1)[None]
    gP["kv_norm"] = g_kvn.reshape(-1)
    gP["b_knorm"] = dkw.reshape(B_HEADS, HEAD_DIM).sum(axis=0)
    gP["mem_norm"] = jnp.concatenate([g_mn0, g_mn1], axis=0)
    gP["mem_qnorm"] = jnp.stack([g_mq0, g_mq1])
    gP["mem_knorm"] = jnp.stack([g_mk0, g_mk1])
    return sq, gx, gP


MESH_ID = pl.DeviceIdType.MESH
HBM_SPEC = pl.BlockSpec(memory_space=pltpu.HBM)


def _position():
    return lax.axis_index("x"), lax.axis_index("y"), lax.axis_index("c")


def _all_gather(blocks, *, name):
    n = len(blocks)

    def body(*refs):
        x_refs, out_refs = refs[:n], refs[n:2 * n]
        send_sems, recv_sems, local_sems = refs[2 * n:]
        x, y, c = _position()
        me, sibling = (x, y, c), (x, y, 1 - c)
        chips = [(1 - x, y), (x, 1 - y), (1 - x, 1 - y)]

        def slot(a, px, py, pc):
            return out_refs[a].at[4 * px + 2 * py + pc]

        def copy(a, k, blk, to, src=None):
            return pltpu.make_async_remote_copy(
                src_ref=slot(a, *blk) if src is None else src, dst_ref=slot(a, *blk),
                send_sem=send_sems.at[7 * a + k], recv_sem=recv_sems.at[7 * a + k], device_id=to, device_id_type=MESH_ID)

        mine = [pltpu.make_async_copy(x_refs[a], slot(a, *me), local_sems.at[a]) for a in range(n)]
        for cp in mine:
            cp.start()
        first = []
        for a in range(n):
            first.append(copy(a, 0, me, sibling, src=x_refs[a]))
            first += [copy(a, 1 + j, me, (*chip, c), src=x_refs[a]) for j, chip in enumerate(chips)]
        for cp in first:
            cp.start()
        passed = []
        for j, chip in enumerate(chips):
            for a in range(n):
                copy(a, 1 + j, (*chip, c), me).wait_recv()
                cp = copy(a, 4 + j, (*chip, c), sibling)
                cp.start()
                passed.append(cp)
        for a in range(n):
            copy(a, 0, sibling, me).wait_recv()
            for j, chip in enumerate(chips):
                copy(a, 4 + j, (*chip, 1 - c), me).wait_recv()
        for cp in first + passed:
            cp.wait_send()
        for cp in mine:
            cp.wait()

    return pl.pallas_call(
        body, out_shape=[jax.ShapeDtypeStruct((N_DEV,) + b.shape, b.dtype) for b in blocks],
        in_specs=[HBM_SPEC] * n, out_specs=[HBM_SPEC] * n,
        scratch_shapes=[pltpu.SemaphoreType.DMA((7 * n,)), pltpu.SemaphoreType.DMA((7 * n,)), pltpu.SemaphoreType.DMA((n,))],
        name=name)(*blocks)


SEM_SPEC = pl.BlockSpec(memory_space=pltpu.SEMAPHORE)
ANY_SPEC = pl.BlockSpec(memory_space=pl.ANY)
DATAFLOW = pltpu.SideEffectType.DATAFLOW_SIDE_EFFECTING


def _peer(k, x, y, c):
    return (1 - x if (k >> 2) & 1 else x, 1 - y if (k >> 1) & 1 else y, 1 - c if k & 1 else c)


def _own_slot_filled(own_block):
    x, y, c = _position()
    zone = lax.empty((N_DEV,) + own_block.shape, own_block.dtype)
    return lax.dynamic_update_slice_in_dim(zone, own_block[None], 4 * x + 2 * y + c, axis=0)


def _split_start(srcs, scatter, after, *, name):
    n = len(srcs)
    extra = [] if after is None else [after]
    x, y, c = _position()
    me = 4 * x + 2 * y + c
    lands = [_own_slot_filled(lax.dynamic_index_in_dim(s, me, 0, keepdims=False) if scatter else s) for s in srcs]

    def body(*refs):
        src_refs, land_refs = refs[:n], refs[n:2 * n]
        send_sems, recv_sems = refs[2 * n + len(extra)], refs[2 * n + len(extra) + 1]
        token = refs[-1]
        bx, by, bc = _position()
        bme = 4 * bx + 2 * by + bc
        for a in range(n):
            for k in range(1, N_DEV):
                tx, ty, tc = _peer(k, bx, by, bc)
                src = src_refs[a].at[4 * tx + 2 * ty + tc] if scatter else src_refs[a]
                pltpu.make_async_remote_copy(
                    src_ref=src, dst_ref=land_refs[a].at[bme],
                    send_sem=send_sems.at[7 * a + k - 1], recv_sem=recv_sems.at[7 * a + k - 1],
                    device_id=(tx, ty, tc), device_id_type=MESH_ID).start()
        token[...] = jnp.zeros_like(token)

    hbm = lambda a: pltpu.HBM(a.shape, a.dtype)
    outs = pl.pallas_call(
        body, name=name,
        out_shape=(pltpu.SemaphoreType.DMA((7 * n,)), pltpu.SemaphoreType.DMA((7 * n,)),
                   *[hbm(s) for s in srcs], *[hbm(l) for l in lands], jax.ShapeDtypeStruct((8, 128), F32)),
        in_specs=[HBM_SPEC] * (2 * n) + [ANY_SPEC] * len(extra),
        out_specs=(SEM_SPEC, SEM_SPEC, *[HBM_SPEC] * (2 * n), pl.BlockSpec(memory_space=pltpu.VMEM)),
        input_output_aliases={i: 2 + i for i in range(2 * n)},
        compiler_params=pltpu.CompilerParams(has_side_effects=DATAFLOW),
    )(*[pltpu.with_memory_space_constraint(s, pltpu.HBM) for s in srcs],
      *[pltpu.with_memory_space_constraint(l, pltpu.HBM) for l in lands], *extra)
    return {"n": n, "scatter": scatter, "send": outs[0], "recv": outs[1], "srcs": outs[2:2 + n],
            "lands": outs[2 + n:2 + 2 * n], "token": outs[-1]}


def _split_wait(handle, after, *, name):
    n, scatter = handle["n"], handle["scatter"]

    def body(*refs):
        src_refs, land_refs = refs[:n], refs[n:2 * n]
        send_sems, recv_sems = refs[2 * n], refs[2 * n + 1]
        bx, by, bc = _position()
        for a in range(n):
            for k in range(1, N_DEV):
                src = src_refs[a].at[0] if scatter else src_refs[a]
                cp = pltpu.make_async_remote_copy(
                    src_ref=src, dst_ref=land_refs[a].at[0],
                    send_sem=send_sems.at[7 * a + k - 1], recv_sem=recv_sems.at[7 * a + k - 1],
                    device_id=_peer(k, bx, by, bc), device_id_type=MESH_ID)
                cp.wait_send()
                cp.wait_recv()

    hbm = lambda a: pltpu.HBM(a.shape, a.dtype)
    outs = pl.pallas_call(
        body, name=name,
        out_shape=(*[hbm(s) for s in handle["srcs"]], *[hbm(l) for l in handle["lands"]]),
        in_specs=[HBM_SPEC] * (2 * n) + [SEM_SPEC, SEM_SPEC, ANY_SPEC],
        out_specs=tuple([HBM_SPEC] * (2 * n)),
        input_output_aliases={i: i for i in range(2 * n)},
        compiler_params=pltpu.CompilerParams(has_side_effects=DATAFLOW),
    )(*handle["srcs"], *handle["lands"], handle["send"], handle["recv"], after)
    return list(outs[n:])


def _sum_sources(parts, *, tr, name):
    n, R, C = parts.shape

    def body(p_ref, o_ref):
        acc = p_ref[0].astype(F32)
        for s in range(1, n):
            acc = acc + p_ref[s].astype(F32)
        o_ref[...] = acc

    return pl.pallas_call(
        body, grid=(R // tr,), in_specs=[pl.BlockSpec((n, tr, C), lambda i: (0, i, 0))],
        out_specs=pl.BlockSpec((tr, C), lambda i: (i, 0)),
        out_shape=jax.ShapeDtypeStruct((R, C), F32), compiler_params=_cp("parallel"), name=name)(parts)


def _adamw_math(g, w, m, v):
    c1 = 1.0 - ADAM_B1 ** ADAM_STEP
    c2 = 1.0 - ADAM_B2 ** ADAM_STEP
    nm = ADAM_B1 * m + (1.0 - ADAM_B1) * g
    nv = ADAM_B2 * v + (1.0 - ADAM_B2) * (g * g)
    return -ADAM_LR * ((nm / c1) / (jnp.sqrt(nv / c2) + ADAM_EPS) + ADAM_WD * w), nm, nv


def _reduce_adamw(received, w, m, v, *, col, tr, name):
    L, R, C = w.shape

    def body(*refs):
        p_refs = refs[:L]
        w_ref, m_ref, v_ref, g_ref, d_ref, nm_ref, nv_ref = refs[L:]
        for l in range(L):
            @pl.when(pl.program_id(0) == l)
            def _(p_ref=p_refs[l]):
                acc = p_ref[0].astype(F32)
                for s in range(1, N_DEV):
                    acc = acc + p_ref[s].astype(F32)
                g = acc.T if col else acc
                g_ref[...] = g
                d_ref[...], nm_ref[...], nv_ref[...] = _adamw_math(g, w_ref[...], m_ref[...], v_ref[...])

    p_spec = (pl.BlockSpec((N_DEV, C, tr), lambda l, i: (0, 0, i)) if col
              else pl.BlockSpec((N_DEV, tr, C), lambda l, i: (0, i, 0)))
    blk = pl.BlockSpec((None, tr, C), lambda l, i: (l, i, 0))
    sh = jax.ShapeDtypeStruct((L, R, C), F32)
    return pl.pallas_call(
        body, grid=(L, R // tr), in_specs=[p_spec] * L + [blk] * 3, out_specs=[blk] * 4, out_shape=[sh] * 4,
        compiler_params=_cp("parallel", "parallel"), name=name)(*received, w, m, v)


def _adamw(g, w, m, v, *, tr, name):
    L, R, C = w.shape

    def body(g_ref, w_ref, m_ref, v_ref, d_ref, nm_ref, nv_ref):
        d_ref[...], nm_ref[...], nv_ref[...] = _adamw_math(g_ref[...], w_ref[...], m_ref[...], v_ref[...])

    blk = pl.BlockSpec((None, tr, C), lambda l, i: (l, i, 0))
    sh = jax.ShapeDtypeStruct((L, R, C), F32)
    return pl.pallas_call(
        body, grid=(L, R // tr), in_specs=[blk] * 4, out_specs=[blk] * 3, out_shape=[sh] * 3,
        compiler_params=_cp("parallel", "parallel"), name=name)(g, w, m, v)


UNITS = {
    "a_w_in": ("a_w_in", 0, True), "w_mem_kv0": ("w_mem_kv", 0, False), "w_out0": ("w_out", 0, False),
    "w_gate_up0": ("w_gate_up", 0, True), "w_down0": ("w_down", 0, False), "w_kv": ("w_kv", None, True),
    "b_w_in": ("b_w_in", 0, True), "w_mem_kv1": ("w_mem_kv", 1, False), "w_out1": ("w_out", 1, False),
    "w_gate_up1": ("w_gate_up", 1, True), "w_down1": ("w_down", 1, False),
}
BIG = ("a_w_in", "b_w_in", "w_kv", "w_mem_kv", "w_out", "w_gate_up", "w_down")
ADAMW_ROW_TILE = {"a_w_in": 256, "b_w_in": 256, "w_kv": 256, "w_mem_kv": 128, "w_out": 128, "w_gate_up": 176, "w_down": 176}
TRANSPOSED_UPDATE = ("w_gate_up",)


def _wire_block(weights, unit):
    name, layer, col = UNITS[unit]
    a = weights[name] if layer is None else weights[name][layer]
    return (a.T if col else a).astype(BF16)


SMALL_REPLICATED = ("norm_mix", "norm_ffn", "b_qnorm", "kv_norm", "b_knorm", "mem_norm", "mem_qnorm", "mem_knorm")
SMALL_SHARDED = ("a_lb_logits", "a_onorm")
SMALL_ORDER = SMALL_REPLICATED + SMALL_SHARDED
LANES = 128


def _prod(shape):
    n = 1
    for s in shape:
        n *= s
    return n


def _pack_flat(arrays, rows, cols, dtype):
    flat = jnp.concatenate([a.reshape(-1).astype(dtype) for a in arrays])
    return jnp.pad(flat, (0, rows * cols - flat.shape[0])).reshape(rows, cols)


def _unpack_flat(packed, shapes):
    flat = packed.reshape(-1)
    out, off = [], 0
    for s in shapes:
        out.append(flat[off:off + _prod(s)].reshape(s))
        off += _prod(s)
    return out


def kernel(x, mem, norm_mix, norm_ffn, a_w_in, a_lb_logits, a_onorm, b_w_in, b_qnorm, kv_norm, w_kv, b_knorm, mem_norm, w_mem_kv, mem_qnorm, mem_knorm, w_out, w_gate_up, w_down, loss_target, m_norm_mix, m_norm_ffn, m_a_w_in, m_a_lb_logits, m_a_onorm, m_b_w_in, m_b_qnorm, m_kv_norm, m_w_kv, m_b_knorm, m_mem_norm, m_w_mem_kv, m_mem_qnorm, m_mem_knorm, m_w_out, m_w_gate_up, m_w_down, v_norm_mix, v_norm_ffn, v_a_w_in, v_a_lb_logits, v_a_onorm, v_b_w_in, v_b_qnorm, v_kv_norm, v_w_kv, v_b_knorm, v_mem_norm, v_w_mem_kv, v_mem_qnorm, v_mem_knorm, v_w_out, v_w_gate_up, v_w_down):
    names = ("norm_mix", "norm_ffn", "a_w_in", "a_lb_logits", "a_onorm", "b_w_in", "b_qnorm", "kv_norm", "w_kv", "b_knorm",
             "mem_norm", "w_mem_kv", "mem_qnorm", "mem_knorm", "w_out", "w_gate_up", "w_down")
    w = dict(zip(names, (norm_mix, norm_ffn, a_w_in, a_lb_logits, a_onorm, b_w_in, b_qnorm, kv_norm, w_kv, b_knorm,
                         mem_norm, w_mem_kv, mem_qnorm, mem_knorm, w_out, w_gate_up, w_down)))
    m = dict(zip(names, (m_norm_mix, m_norm_ffn, m_a_w_in, m_a_lb_logits, m_a_onorm, m_b_w_in, m_b_qnorm, m_kv_norm, m_w_kv,
                         m_b_knorm, m_mem_norm, m_w_mem_kv, m_mem_qnorm, m_mem_knorm, m_w_out, m_w_gate_up, m_w_down)))
    v = dict(zip(names, (v_norm_mix, v_norm_ffn, v_a_w_in, v_a_lb_logits, v_a_onorm, v_b_w_in, v_b_qnorm, v_kv_norm, v_w_kv,
                         v_b_knorm, v_mem_norm, v_w_mem_kv, v_mem_qnorm, v_mem_knorm, v_w_out, v_w_gate_up, v_w_down)))

    first = ["a_w_in", "w_mem_kv0"]
    gathered = _all_gather([_wire_block(w, u) for u in first] + [_pack_flat([a_lb_logits, a_onorm], 8, LANES, F32)],
                           name="gather_first")
    full = {u: g.reshape(-1, g.shape[-1]) for u, g in zip(first, gathered)}
    small_in = gathered[-1].reshape(N_DEV, -1)
    P = {n: w[n] for n in SMALL_REPLICATED}
    P["a_lb_logits"] = small_in[:, :192].reshape(N_DEV, 2, 96).transpose(1, 0, 2).reshape(2, A_WIDTH)
    P["a_onorm"] = small_in[:, 192:288].reshape(1, A_WIDTH)
    later = [["w_out0", "w_gate_up0"], ["w_down0", "w_kv"], ["b_w_in", "w_mem_kv1"], ["w_out1", "w_gate_up1", "w_down1"]]
    pending = {}
    token = gathered[-1]
    for i, group in enumerate(later):
        handle = _split_start([_wire_block(w, u) for u in group], False, token, name=f"gather{i}_start")
        token = handle["token"]
        for u in group:
            pending[u] = (i, group, handle)

    def get_w(unit, after):
        if unit not in full:
            i, group, handle = pending[unit]
            for u, land in zip(group, _split_wait(handle, after, name=f"gather{i}_wait")):
                full[u] = land.reshape(-1, land.shape[-1])
        return full[unit]

    sent = []

    def put_g(group):
        units = list(group)
        handle = _split_start([group[u].reshape(N_DEV, -1, group[u].shape[-1]) for u in units], True, None,
                              name=f"scatter{len(sent)}_start")
        sent.append((units, handle))
        return handle["token"]

    sq, gx, gP = _local_step(x[0], mem[0], loss_target[0], get_w, P, put_g, first_dep=token)
    loss = lax.psum(0.5 * jnp.sum(sq) / D_MODEL, ("x", "y", "c"))

    received = {}
    for i, (units, handle) in enumerate(sent):
        received.update(zip(units, _split_wait(handle, gx, name=f"scatter{i}_wait")))
    out = {"grad": {}, "delta": {}, "new_m": {}, "new_v": {}}
    for n in BIG:
        shape = w[n].shape
        as3 = lambda a: a.reshape((-1,) + shape[-2:])
        mine = [u for u, (wn, _, _) in UNITS.items() if wn == n]
        col = UNITS[mine[0]][2]
        flip = (lambda a: jnp.swapaxes(a, 1, 2)) if n in TRANSPOSED_UPDATE else (lambda a: a)
        res = _reduce_adamw([received[u] for u in mine], flip(as3(w[n])), flip(as3(m[n])), flip(as3(v[n])),
                            col=col and n not in TRANSPOSED_UPDATE, tr=ADAMW_ROW_TILE[n], name=f"adamw_{n}")
        for kind, r in zip(("grad", "delta", "new_m", "new_v"), res):
            out[kind][n] = flip(r).reshape(shape)

    full_shapes = [(2, A_WIDTH) if n == "a_lb_logits" else (1, A_WIDTH) if n == "a_onorm" else w[n].shape for n in SMALL_ORDER]
    n_small = sum(_prod(s) for s in full_shapes)
    rows_small = -(-n_small // (8 * LANES)) * 8
    g_all, = _all_gather([_pack_flat([gP[n] for n in SMALL_ORDER], rows_small, LANES, F32)], name="gather_small_grads")
    g_small = dict(zip(SMALL_ORDER, _unpack_flat(_sum_sources(g_all, tr=rows_small, name="sum_small_grads"), full_shapes)))
    me = 4 * lax.axis_index("x") + 2 * lax.axis_index("y") + lax.axis_index("c")
    for n in SMALL_SHARDED:
        g_small[n] = lax.dynamic_slice_in_dim(g_small[n], me * 96, 96, axis=1)
    shapes = [w[n].shape for n in SMALL_ORDER]
    rows_upd = -(-sum(_prod(s) for s in shapes) // (8 * LANES)) * 8
    pk = lambda d: _pack_flat([d[n] for n in SMALL_ORDER], rows_upd, LANES, F32)
    res = _adamw(pk(g_small)[None], pk(w)[None], pk(m)[None], pk(v)[None], tr=rows_upd, name="adamw_small")
    out["grad"].update(g_small)
    for kind, packed in zip(("delta", "new_m", "new_v"), res):
        out[kind].update(zip(SMALL_ORDER, _unpack_flat(packed[0], shapes)))

    return (loss, gx[None], *[out["grad"][n] for n in names], *[out["delta"][n] for n in names],
            *[out["new_m"][n] for n in names], *[out["new_v"][n] for n in names])
```

```python
import functools

import jax
import jax.numpy as jnp
from jax import lax
from jax.experimental import pallas as pl
from jax.experimental.pallas import tpu as pltpu

F32 = jnp.float32
BF16 = jnp.bfloat16

N_DEV = 8
D_MODEL = 1024
HEAD_DIM = 128
A_HEADS = 6
A_WIDTH = A_HEADS * HEAD_DIM
CHUNK = 64
B_HEADS = 6
B_WIDTH = B_HEADS * HEAD_DIM
DILATIONS = (1, 4, 16)
SPAN = 128
N_GROUPS = 3
ROPE_THETA = 10000.0
MEM_TOKENS = 256
MEM_HEADS = 4
MEM_HEAD_DIM = 64
MEM_WIDTH = MEM_HEADS * MEM_HEAD_DIM
FFN_HIDDEN = 2816
EPS = 1e-6

ADAM_LR = 0.001
ADAM_B1 = 0.9
ADAM_B2 = 0.999
ADAM_EPS = 1e-08
ADAM_WD = 0.01
ADAM_STEP = 10

V7X_VMEM_LIMIT_BYTES = 56 * 1024 * 1024

NT_DIMS = (((1,), (1,)), ((), ()))
TN_DIMS = (((0,), (0,)), ((), ()))


def _cp(*sem):
    return pltpu.CompilerParams(dimension_semantics=sem, vmem_limit_bytes=V7X_VMEM_LIMIT_BYTES)


def _dot(a, b):
    return jnp.dot(a.astype(BF16), b.astype(BF16), preferred_element_type=F32)


def _dot_nt(a, b):
    return lax.dot_general(a.astype(BF16), b.astype(BF16), NT_DIMS, preferred_element_type=F32)


def _dot_tn(a, b):
    return lax.dot_general(a.astype(BF16), b.astype(BF16), TN_DIMS, preferred_element_type=F32)


def _dot3(m01, x):
    hi = x.astype(BF16)
    r1 = x - hi.astype(F32)
    mid = r1.astype(BF16)
    lo = (r1 - mid.astype(F32)).astype(BF16)
    d = functools.partial(jnp.dot, preferred_element_type=F32)
    return d(m01, hi) + d(m01, mid) + d(m01, lo)


def _sigmoid(x):
    return 1.0 / (1.0 + jnp.exp(-x))


def _full(shape):
    return pl.BlockSpec(shape, lambda *_: (0,) * len(shape))


def _dep(body, n_in, dep):
    if dep is None:
        return body, [], []

    def with_dep(*refs):
        return body(*refs[:n_in], *refs[n_in + 1:])

    return with_dep, [pl.BlockSpec(memory_space=pl.ANY)], [dep]


def _rms_matmul(x, g, w, *, tt, tn, wt, name, out_dtype=F32, dep=None):
    T, K = x.shape
    N = w.shape[0] if wt else w.shape[1]

    def kernel_body(x_ref, g_ref, w_ref, y_ref, xn_ref):
        xf = x_ref[...]
        r = lax.rsqrt(jnp.mean(xf * xf, axis=-1, keepdims=True) + EPS)
        xn = (xf * r * g_ref[...]).astype(BF16)
        xn_ref[...] = xn
        for j in range(N // tn):
            cols = slice(j * tn, (j + 1) * tn)
            y = _dot_nt(xn, w_ref[cols, :]) if wt else _dot(xn, w_ref[:, cols])
            y_ref[:, cols] = y.astype(out_dtype)

    body, dep_specs, dep_args = _dep(kernel_body, 3, dep)
    return pl.pallas_call(
        body, grid=(T // tt,),
        in_specs=[pl.BlockSpec((tt, K), lambda i: (i, 0)), _full((1, K)), _full(w.shape)] + dep_specs,
        out_specs=[pl.BlockSpec((tt, N), lambda i: (i, 0)), pl.BlockSpec((tt, K), lambda i: (i, 0))],
        out_shape=[jax.ShapeDtypeStruct((T, N), out_dtype), jax.ShapeDtypeStruct((T, K), BF16)],
        compiler_params=_cp("parallel"), name=name)(x, g, w, *dep_args)


def _mm_res(res, a1, a2, w, *, tt, name):
    T, K1 = a1.shape
    K2 = a2.shape[1]
    N = w.shape[1]

    def body(r_ref, a1_ref, a2_ref, w_ref, o_ref):
        o_ref[...] = r_ref[...] + _dot(a1_ref[...], w_ref[:K1, :]) + _dot(a2_ref[...], w_ref[K1:, :])

    return pl.pallas_call(
        body, grid=(T // tt,),
        in_specs=[pl.BlockSpec((tt, N), lambda i: (i, 0)), pl.BlockSpec((tt, K1), lambda i: (i, 0)),
                  pl.BlockSpec((tt, K2), lambda i: (i, 0)), _full((K1 + K2, N))],
        out_specs=pl.BlockSpec((tt, N), lambda i: (i, 0)),
        out_shape=jax.ShapeDtypeStruct((T, N), F32),
        compiler_params=_cp("parallel"), name=name)(res, a1, a2, w)


def _swiglu_down(h, gu, wd, *, tt, name):
    T, D = h.shape
    Fh = wd.shape[0]

    def body(h_ref, gt_ref, up_ref, w_ref, o_ref):
        gt = gt_ref[...].astype(F32)
        act = gt * _sigmoid(gt) * up_ref[...].astype(F32)
        o_ref[...] = h_ref[...] + _dot(act, w_ref[...])

    return pl.pallas_call(
        body, grid=(T // tt,),
        in_specs=[pl.BlockSpec((tt, D), lambda i: (i, 0)), pl.BlockSpec((tt, Fh), lambda i: (i, 0)),
                  pl.BlockSpec((tt, Fh), lambda i: (i, 1)), _full((Fh, D))],
        out_specs=pl.BlockSpec((tt, D), lambda i: (i, 0)),
        out_shape=jax.ShapeDtypeStruct((T, D), F32),
        compiler_params=_cp("parallel"), name=name)(h, gu, gu, wd)


def _swiglu_bwd(dh, gu, wd, *, tt, name):
    T, D = dh.shape
    Fh = wd.shape[0]
    last = T // tt - 1

    def body(dh_ref, gt_ref, up_ref, w_ref, dgu_ref, gw_ref, acc):
        @pl.when(pl.program_id(0) == 0)
        def _():
            acc[...] = jnp.zeros_like(acc)

        gt = gt_ref[...].astype(F32)
        up = up_ref[...].astype(F32)
        s = _sigmoid(gt)
        silu = gt * s
        dh16 = dh_ref[...].astype(BF16)
        dact = _dot_nt(dh16, w_ref[...])
        acc[...] += _dot_tn((silu * up).astype(BF16), dh16)
        dgu_ref[:, :Fh] = (dact * up * (s * (1.0 + gt * (1.0 - s)))).astype(BF16)
        dgu_ref[:, Fh:] = (dact * silu).astype(BF16)

        @pl.when(pl.program_id(0) == last)
        def _():
            gw_ref[...] = acc[...].astype(BF16)

    return pl.pallas_call(
        body, grid=(T // tt,),
        in_specs=[pl.BlockSpec((tt, D), lambda i: (i, 0)), pl.BlockSpec((tt, Fh), lambda i: (i, 0)),
                  pl.BlockSpec((tt, Fh), lambda i: (i, 1)), _full((Fh, D))],
        out_specs=[pl.BlockSpec((tt, 2 * Fh), lambda i: (i, 0)), _full((Fh, D))],
        out_shape=[jax.ShapeDtypeStruct((T, 2 * Fh), BF16), jax.ShapeDtypeStruct((Fh, D), BF16)],
        scratch_shapes=[pltpu.VMEM((Fh, D), F32)],
        compiler_params=_cp("arbitrary"), name=name)(dh, gu, gu, wd)


def _out_proj_bwd(dy, a1, a2, w, *, tt, name):
    T, N = dy.shape
    K1, K2 = a1.shape[1], a2.shape[1]
    K = K1 + K2
    last = T // tt - 1

    def body(dy_ref, a1_ref, a2_ref, w_ref, da_ref, gw_ref, acc):
        @pl.when(pl.program_id(0) == 0)
        def _():
            acc[...] = jnp.zeros_like(acc)

        dy16 = dy_ref[...].astype(BF16)
        da_ref[...] = _dot_nt(dy16, w_ref[...])
        acc[:K1, :] += _dot_tn(a1_ref[...], dy16)
        acc[K1:, :] += _dot_tn(a2_ref[...], dy16)

        @pl.when(pl.program_id(0) == last)
        def _():
            gw_ref[...] = acc[...].astype(BF16)

    return pl.pallas_call(
        body, grid=(T // tt,),
        in_specs=[pl.BlockSpec((tt, N), lambda i: (i, 0)), pl.BlockSpec((tt, K1), lambda i: (i, 0)),
                  pl.BlockSpec((tt, K2), lambda i: (i, 0)), _full((K, N))],
        out_specs=[pl.BlockSpec((tt, K), lambda i: (i, 0)), _full((K, N))],
        out_shape=[jax.ShapeDtypeStruct((T, K), F32), jax.ShapeDtypeStruct((K, N), BF16)],
        scratch_shapes=[pltpu.VMEM((K, N), F32)],
        compiler_params=_cp("arbitrary"), name=name)(dy, a1, a2, w)


def _mm_tn(a, b, *, tt, tka, name):
    T, Ka = a.shape
    N = b.shape[1]
    last = T // tt - 1

    def body(a_ref, b_ref, o_ref, acc):
        @pl.when(pl.program_id(1) == 0)
        def _():
            acc[...] = jnp.zeros_like(acc)

        acc[...] += _dot_tn(a_ref[...], b_ref[...])

        @pl.when(pl.program_id(1) == last)
        def _():
            o_ref[...] = acc[...].astype(BF16)

    return pl.pallas_call(
        body, grid=(Ka // tka, T // tt),
        in_specs=[pl.BlockSpec((tt, tka), lambda j, t: (t, j)), pl.BlockSpec((tt, N), lambda j, t: (t, 0))],
        out_specs=pl.BlockSpec((tka, N), lambda j, t: (j, 0)),
        out_shape=jax.ShapeDtypeStruct((Ka, N), BF16),
        scratch_shapes=[pltpu.VMEM((tka, N), F32)],
        compiler_params=_cp("parallel", "arbitrary"), name=name)(a, b)


def _rms_bwd_dx(x, g, w, dy, dres, *, tt, wt, name, dep=None):
    T, K = x.shape
    N = w.shape[0] if wt else w.shape[1]

    def kernel_body(x_ref, g_ref, w_ref, dy_ref, dres_ref, dx_ref, dg_ref):
        @pl.when(pl.program_id(0) == 0)
        def _():
            dg_ref[...] = jnp.zeros_like(dg_ref)

        dxn = (_dot if wt else _dot_nt)(dy_ref[...], w_ref[...])
        xf = x_ref[...]
        r = lax.rsqrt(jnp.mean(xf * xf, axis=-1, keepdims=True) + EPS)
        xhat = xf * r
        dg_ref[...] += jnp.sum(dxn * xhat, axis=0, keepdims=True)
        dxhat = dxn * g_ref[...]
        dx_ref[...] = dres_ref[...] + r * (dxhat - xhat * jnp.mean(dxhat * xhat, axis=-1, keepdims=True))

    body, dep_specs, dep_args = _dep(kernel_body, 5, dep)
    return pl.pallas_call(
        body, grid=(T // tt,),
        in_specs=[pl.BlockSpec((tt, K), lambda i: (i, 0)), _full((1, K)), _full(w.shape),
                  pl.BlockSpec((tt, N), lambda i: (i, 0)), pl.BlockSpec((tt, K), lambda i: (i, 0))] + dep_specs,
        out_specs=[pl.BlockSpec((tt, K), lambda i: (i, 0)), _full((1, K))],
        out_shape=[jax.ShapeDtypeStruct((T, K), F32), jax.ShapeDtypeStruct((1, K), F32)],
        compiler_params=_cp("arbitrary"), name=name)(x, g, w, dy, dres, *dep_args)


def _loss_kernel(y, tgt, *, tt, name):
    T, D = y.shape

    def body(y_ref, t_ref, dy_ref, acc_ref):
        @pl.when(pl.program_id(0) == 0)
        def _():
            acc_ref[...] = jnp.zeros_like(acc_ref)

        e = y_ref[...] - t_ref[...]
        dy_ref[...] = e * (1.0 / D)
        acc_ref[...] += jnp.sum(e * e, axis=0, keepdims=True)

    return pl.pallas_call(
        body, grid=(T // tt,),
        in_specs=[pl.BlockSpec((tt, D), lambda i: (i, 0)), pl.BlockSpec((tt, D), lambda i: (i, 0))],
        out_specs=[pl.BlockSpec((tt, D), lambda i: (i, 0)), _full((1, D))],
        out_shape=[jax.ShapeDtypeStruct((T, D), F32), jax.ShapeDtypeStruct((1, D), F32)],
        compiler_params=_cp("arbitrary"), name=name)(y, tgt)


HGRN_TB = 512
HGRN_NCH = HGRN_TB // CHUNK
HGRN_HPB = 6


def _hgrn_chunk_fwd(q, z, lbv, tril01):
    sig = _sigmoid(z)
    f = lbv + (1.0 - lbv) * sig
    kk = 1.0 - f
    b = _dot3(tril01, jnp.log(f))
    bend = b[CHUNK - 1:CHUNK, :]
    sq = _sigmoid(q)
    eb = jnp.exp(b)
    emb = jnp.exp(-b)
    eo = jnp.exp(bend - b)
    dec = jnp.exp(bend)
    return sig, f, kk, sq, eb, emb, eo, dec


def _hgrn2_fwd(proj, lb, *, name):
    T = proj.shape[0]
    nT = T // HGRN_TB
    nC = T // CHUNK

    def body(q_ref, z_ref, v_ref, lb_ref, o_ref, st_ref, state):
        @pl.when(pl.program_id(1) == 0)
        def _():
            state[...] = jnp.zeros_like(state)

        row = lax.broadcasted_iota(jnp.int32, (CHUNK, CHUNK), 0)
        col = lax.broadcasted_iota(jnp.int32, (CHUNK, CHUNK), 1)
        causal = row >= col
        tril01 = causal.astype(BF16)

        def chunk(c, carry):
            rows = pl.ds(pl.multiple_of(c * CHUNK, CHUNK), CHUNK)
            for hh in range(HGRN_HPB):
                sl = slice(hh * HEAD_DIM, (hh + 1) * HEAD_DIM)
                q = q_ref[rows, sl]
                v = v_ref[rows, sl].astype(BF16)
                sig, f, kk, sq, eb, emb, eo, dec = _hgrn_chunk_fwd(q, z_ref[rows, sl], lb_ref[:, sl], tril01)
                qi = (q * sq * eb).astype(BF16)
                ki = (kk * emb).astype(BF16)
                ko = (kk * eo).astype(BF16)
                st = state[hh]
                att = jnp.where(causal, _dot_nt(qi, ki), 0.0)
                o_ref[rows, sl] = _dot(att, v) + _dot_nt(qi, st)
                st_ref[c, hh] = st
                state[hh] = st * dec + _dot_tn(v, ko)
            return carry

        lax.fori_loop(0, HGRN_NCH, chunk, 0)

    W = HGRN_HPB * HEAD_DIM
    nG = A_HEADS // HGRN_HPB
    hb = lambda off: pl.BlockSpec((HGRN_TB, W), lambda h, i: (i, off + h))
    return pl.pallas_call(
        body, grid=(nG, nT),
        in_specs=[hb(0), hb(nG), hb(2 * nG), pl.BlockSpec((1, W), lambda h, i: (0, h))],
        out_specs=[hb(0), pl.BlockSpec((HGRN_NCH, HGRN_HPB, HEAD_DIM, HEAD_DIM), lambda h, i: (i, h, 0, 0))],
        out_shape=[jax.ShapeDtypeStruct((T, A_WIDTH), F32), jax.ShapeDtypeStruct((nC, A_HEADS, HEAD_DIM, HEAD_DIM), F32)],
        scratch_shapes=[pltpu.VMEM((HGRN_HPB, HEAD_DIM, HEAD_DIM), F32)],
        compiler_params=_cp("parallel", "arbitrary"), name=name)(proj, proj, proj, lb)


def _hgrn2_bwd(proj, lb, st_all, do, *, name):
    T = proj.shape[0]
    nT = T // HGRN_TB

    def body(q_ref, z_ref, v_ref, lb_ref, st_ref, do_ref, dq_ref, dz_ref, dv_ref, dlb_ref, dstate):
        @pl.when(pl.program_id(1) == 0)
        def _():
            dstate[...] = jnp.zeros_like(dstate)
            dlb_ref[...] = jnp.zeros_like(dlb_ref)

        row = lax.broadcasted_iota(jnp.int32, (CHUNK, CHUNK), 0)
        col = lax.broadcasted_iota(jnp.int32, (CHUNK, CHUNK), 1)
        causal = row >= col
        tril01 = causal.astype(BF16)
        triu01 = (row <= col).astype(BF16)

        def chunk(cc, carry):
            c = HGRN_NCH - 1 - cc
            rows = pl.ds(pl.multiple_of(c * CHUNK, CHUNK), CHUNK)
            for hh in range(HGRN_HPB):
                sl = slice(hh * HEAD_DIM, (hh + 1) * HEAD_DIM)
                lbv = lb_ref[:, sl]
                q = q_ref[rows, sl]
                v = v_ref[rows, sl].astype(BF16)
                sig, f, kk, sq, eb, emb, eo, dec = _hgrn_chunk_fwd(q, z_ref[rows, sl], lbv, tril01)
                qi32 = q * sq * eb
                ki32 = kk * emb
                ko32 = kk * eo
                qi, ki, ko = qi32.astype(BF16), ki32.astype(BF16), ko32.astype(BF16)
                att = jnp.where(causal, _dot_nt(qi, ki), 0.0).astype(BF16)
                dout = do_ref[rows, sl].astype(BF16)
                st = st_ref[c, hh]
                dst = dstate[hh]
                dst16 = dst.astype(BF16)
                datt = jnp.where(causal, _dot_nt(dout, v), 0.0).astype(BF16)
                dqi = _dot(datt, ki) + _dot(dout, st)
                dki = _dot_tn(datt, qi)
                dv_ref[rows, sl] = (_dot_tn(att, dout) + _dot_nt(ko, dst16)).astype(BF16)
                dko = _dot(v, dst16)
                ddec = jnp.sum(dst * st, axis=0, keepdims=True)
                dstate[hh] = dst * dec + _dot_tn(dout, qi)
                dkk = dki * emb + dko * eo
                db = dqi * qi32 - dki * ki32 - dko * ko32
                dbend = jnp.sum(dko * ko32, axis=0, keepdims=True) + ddec * dec
                dlogf = _dot3(triu01, db) + dbend
                df = dlogf / f - dkk
                dz_ref[rows, sl] = (df * (1.0 - lbv) * sig * (1.0 - sig)).astype(BF16)
                dlb_ref[:, sl] += jnp.sum(df * (1.0 - sig), axis=0, keepdims=True)
                dq_ref[rows, sl] = (dqi * eb * (sq * (1.0 + q * (1.0 - sq)))).astype(BF16)
            return carry

        lax.fori_loop(0, HGRN_NCH, chunk, 0)

    W = HGRN_HPB * HEAD_DIM
    nG = A_HEADS // HGRN_HPB
    hb = lambda off: pl.BlockSpec((HGRN_TB, W), lambda h, i: (nT - 1 - i, off + h))
    hlb = pl.BlockSpec((1, W), lambda h, i: (0, h))
    o16 = jax.ShapeDtypeStruct((T, A_WIDTH), BF16)
    return pl.pallas_call(
        body, grid=(nG, nT),
        in_specs=[hb(0), hb(nG), hb(2 * nG), hlb,
                  pl.BlockSpec((HGRN_NCH, HGRN_HPB, HEAD_DIM, HEAD_DIM), lambda h, i: (nT - 1 - i, h, 0, 0)), hb(0)],
        out_specs=[hb(0), hb(0), hb(0), hlb],
        out_shape=[o16, o16, o16, jax.ShapeDtypeStruct((1, A_WIDTH), F32)],
        scratch_shapes=[pltpu.VMEM((HGRN_HPB, HEAD_DIM, HEAD_DIM), F32)],
        compiler_params=_cp("parallel", "arbitrary"), name=name)(proj, proj, proj, lb, st_all, do)


def _head_rms(x):
    r = lax.rsqrt(jnp.mean(x * x, axis=-1, keepdims=True) + EPS)
    return x * r, r


def _head_rms_bwd(dxhat, xhat, r):
    return r * (dxhat - xhat * jnp.mean(dxhat * xhat, axis=-1, keepdims=True))


def _a_post_fwd(o, proj, onorm, *, tt, name):
    T = o.shape[0]

    def body(o_ref, g_ref, w_ref, y_ref):
        for h in range(A_HEADS):
            sl = slice(h * HEAD_DIM, (h + 1) * HEAD_DIM)
            xhat, _ = _head_rms(o_ref[:, sl])
            g = g_ref[:, sl]
            y_ref[:, sl] = xhat * w_ref[:, sl] * (g * _sigmoid(g))

    blk = lambda c: pl.BlockSpec((tt, A_WIDTH), lambda i: (i, c))
    return pl.pallas_call(
        body, grid=(T // tt,), in_specs=[blk(0), blk(3), _full((1, A_WIDTH))], out_specs=blk(0),
        out_shape=jax.ShapeDtypeStruct((T, A_WIDTH), F32),
        compiler_params=_cp("parallel"), name=name)(o, proj, onorm)


def _a_post_bwd(o, proj, onorm, dmix, *, tt, name, dep=None):
    T = o.shape[0]

    def kernel_body(o_ref, g_ref, w_ref, dy_ref, do_ref, dg_ref, dw_ref):
        @pl.when(pl.program_id(0) == 0)
        def _():
            dw_ref[...] = jnp.zeros_like(dw_ref)

        for h in range(A_HEADS):
            sl = slice(h * HEAD_DIM, (h + 1) * HEAD_DIM)
            xhat, r = _head_rms(o_ref[:, sl])
            g = g_ref[:, sl]
            s = _sigmoid(g)
            dy = dy_ref[:, sl]
            w = w_ref[:, sl]
            dg_ref[:, sl] = (dy * xhat * w * (s * (1.0 + g * (1.0 - s)))).astype(BF16)
            dyn = dy * (g * s)
            dw_ref[:, sl] += jnp.sum(dyn * xhat, axis=0, keepdims=True)
            do_ref[:, sl] = _head_rms_bwd(dyn * w, xhat, r)

    blk = lambda c: pl.BlockSpec((tt, A_WIDTH), lambda i: (i, c))
    body, dep_specs, dep_args = _dep(kernel_body, 4, dep)
    return pl.pallas_call(
        body, grid=(T // tt,), in_specs=[blk(0), blk(3), _full((1, A_WIDTH)), blk(0)] + dep_specs,
        out_specs=[blk(0), blk(0), _full((1, A_WIDTH))],
        out_shape=[jax.ShapeDtypeStruct((T, A_WIDTH), F32), jax.ShapeDtypeStruct((T, A_WIDTH), BF16),
                   jax.ShapeDtypeStruct((1, A_WIDTH), F32)],
        compiler_params=_cp("arbitrary"), name=name)(o, proj, onorm, dmix, *dep_args)


def _mem_head_masks(n):
    lane = lax.broadcasted_iota(jnp.int32, (n, MEM_WIDTH), 1)
    return [(lane >= m * MEM_HEAD_DIM) & (lane < (m + 1) * MEM_HEAD_DIM) for m in range(MEM_HEADS)]


def _mem_head_rms(x, masks):
    x2 = x * x
    r = jnp.zeros_like(x)
    for mk in masks:
        ms = jnp.sum(jnp.where(mk, x2, 0.0), axis=-1, keepdims=True) * (1.0 / MEM_HEAD_DIM)
        r = jnp.where(mk, lax.rsqrt(ms + EPS), r)
    return x * r, r


def _mem_head_rms_bwd(dxhat, xhat, r, masks):
    t = dxhat * xhat
    m = jnp.zeros_like(t)
    for mk in masks:
        m = jnp.where(mk, jnp.sum(jnp.where(mk, t, 0.0), axis=-1, keepdims=True) * (1.0 / MEM_HEAD_DIM), m)
    return r * (dxhat - xhat * m)


MEM_SCALE = MEM_HEAD_DIM ** -0.5


def _mem_attn_fwd(proj, qcol, mkv, qn_w, kn_w, *, tt, name):
    T = proj.shape[0]

    def body(q_ref, k_ref, v_ref, qw_ref, kw_ref, o_ref):
        qmasks = _mem_head_masks(tt)
        kmasks = _mem_head_masks(MEM_TOKENS)
        qhat, _ = _mem_head_rms(q_ref[...], qmasks)
        qn = qhat * qw_ref[...]
        khat, _ = _mem_head_rms(k_ref[...], kmasks)
        kn = (khat * kw_ref[...]).astype(BF16)
        v = v_ref[...].astype(BF16)
        out = jnp.zeros((tt, MEM_WIDTH), F32)
        for m in range(MEM_HEADS):
            s = _dot_nt(jnp.where(qmasks[m], qn, 0.0), kn) * MEM_SCALE
            s = s - jnp.max(s, axis=-1, keepdims=True)
            p = jnp.exp(s)
            p = p / jnp.sum(p, axis=-1, keepdims=True)
            out = jnp.where(qmasks[m], _dot(p, v), out)
        o_ref[...] = out

    return pl.pallas_call(
        body, grid=(T // tt,),
        in_specs=[pl.BlockSpec((tt, MEM_WIDTH), lambda i: (i, qcol)), pl.BlockSpec((MEM_TOKENS, MEM_WIDTH), lambda i: (0, 0)),
                  pl.BlockSpec((MEM_TOKENS, MEM_WIDTH), lambda i: (0, 1)), _full((1, MEM_WIDTH)), _full((1, MEM_WIDTH))],
        out_specs=pl.BlockSpec((tt, MEM_WIDTH), lambda i: (i, 0)),
        out_shape=jax.ShapeDtypeStruct((T, MEM_WIDTH), F32),
        compiler_params=_cp("parallel"), name=name)(proj, mkv, mkv, qn_w, kn_w)


def _mem_attn_bwd(proj, qcol, mkv, qn_w, kn_w, dmix, *, tt, name):
    T = proj.shape[0]
    nsteps = T // tt
    ocol = (dmix.shape[1] - MEM_WIDTH) // MEM_WIDTH

    def body(q_ref, k_ref, v_ref, qw_ref, kw_ref, do_ref, dq_ref, dkv_ref, dqw_ref, dkw_ref, dk_acc, dv_acc):
        step = pl.program_id(0)

        @pl.when(step == 0)
        def _():
            dk_acc[...] = jnp.zeros_like(dk_acc)
            dv_acc[...] = jnp.zeros_like(dv_acc)
            dqw_ref[...] = jnp.zeros_like(dqw_ref)

        qmasks = _mem_head_masks(tt)
        kmasks = _mem_head_masks(MEM_TOKENS)
        qhat, qr = _mem_head_rms(q_ref[...], qmasks)
        qn = qhat * qw_ref[...]
        khat, kr = _mem_head_rms(k_ref[...], kmasks)
        kn = (khat * kw_ref[...]).astype(BF16)
        v = v_ref[...].astype(BF16)
        dout = do_ref[...]
        dqn = jnp.zeros((tt, MEM_WIDTH), F32)
        dkn = jnp.zeros((MEM_TOKENS, MEM_WIDTH), F32)
        dvv = jnp.zeros((MEM_TOKENS, MEM_WIDTH), F32)
        for m in range(MEM_HEADS):
            qm = jnp.where(qmasks[m], qn, 0.0).astype(BF16)
            s = _dot_nt(qm, kn) * MEM_SCALE
            s = s - jnp.max(s, axis=-1, keepdims=True)
            p = jnp.exp(s)
            p = p / jnp.sum(p, axis=-1, keepdims=True)
            dom = jnp.where(qmasks[m], dout, 0.0).astype(BF16)
            dp = _dot_nt(dom, v)
            ds = (p * (dp - jnp.sum(p * dp, axis=-1, keepdims=True)) * MEM_SCALE).astype(BF16)
            dqn = jnp.where(qmasks[m], _dot(ds, kn), dqn)
            dkn = jnp.where(kmasks[m], _dot_tn(ds, qm), dkn)
            dvv = jnp.where(kmasks[m], _dot_tn(p, dom), dvv)
        dqw_ref[...] += jnp.sum(dqn * qhat, axis=0, keepdims=True)
        dq_ref[...] = _mem_head_rms_bwd(dqn * qw_ref[...], qhat, qr, qmasks).astype(BF16)
        dk_acc[...] += dkn
        dv_acc[...] += dvv

        @pl.when(step == nsteps - 1)
        def _():
            dk = dk_acc[...]
            dkw_ref[...] = jnp.sum(dk * khat, axis=0, keepdims=True)
            dkv_ref[:, :MEM_WIDTH] = _mem_head_rms_bwd(dk * kw_ref[...], khat, kr, kmasks)
            dkv_ref[:, MEM_WIDTH:] = dv_acc[...]

    return pl.pallas_call(
        body, grid=(nsteps,),
        in_specs=[pl.BlockSpec((tt, MEM_WIDTH), lambda i: (i, qcol)), pl.BlockSpec((MEM_TOKENS, MEM_WIDTH), lambda i: (0, 0)),
                  pl.BlockSpec((MEM_TOKENS, MEM_WIDTH), lambda i: (0, 1)), _full((1, MEM_WIDTH)), _full((1, MEM_WIDTH)),
                  pl.BlockSpec((tt, MEM_WIDTH), lambda i: (i, ocol))],
        out_specs=[pl.BlockSpec((tt, MEM_WIDTH), lambda i: (i, 0)), _full((MEM_TOKENS, 2 * MEM_WIDTH)),
                   _full((1, MEM_WIDTH)), _full((1, MEM_WIDTH))],
        out_shape=[jax.ShapeDtypeStruct((T, MEM_WIDTH), BF16), jax.ShapeDtypeStruct((MEM_TOKENS, 2 * MEM_WIDTH), F32),
                   jax.ShapeDtypeStruct((1, MEM_WIDTH), F32), jax.ShapeDtypeStruct((1, MEM_WIDTH), F32)],
        scratch_shapes=[pltpu.VMEM((MEM_TOKENS, MEM_WIDTH), F32), pltpu.VMEM((MEM_TOKENS, MEM_WIDTH), F32)],
        compiler_params=_cp("arbitrary"), name=name)(proj, mkv, mkv, qn_w, kn_w, dmix)


HALF = HEAD_DIM // 2
ATT_SCALE = HEAD_DIM ** -0.5
NEG = -1e30


def _rope_tables(T):
    inv = ROPE_THETA ** (-jnp.arange(HALF, dtype=F32) / HALF)
    ang = jnp.arange(T, dtype=F32)[:, None] * inv[None, :]
    cos, sin = jnp.cos(ang), jnp.sin(ang)
    return jnp.concatenate([cos, cos], axis=-1), jnp.concatenate([-sin, sin], axis=-1)


def _rope(x, cosf, sinsg):
    return x * cosf + pltpu.roll(x, HALF, 1) * sinsg


def _rope_bwd(dy, cosf, sinsg):
    return dy * cosf + pltpu.roll(dy * sinsg, HALF, 1)


def _headnorm_rope_fwd(x, w_heads, cosf, sinsg, *, col0, n_heads, tt, name):
    T = x.shape[0]
    W = n_heads * HEAD_DIM

    def body(x_ref, w_ref, c_ref, s_ref, y_ref):
        c, s = c_ref[...], s_ref[...]
        for h in range(n_heads):
            sl = slice(h * HEAD_DIM, (h + 1) * HEAD_DIM)
            xhat, _ = _head_rms(x_ref[:, sl])
            y_ref[:, sl] = _rope(xhat * w_ref[:, sl], c, s)

    tbl = pl.BlockSpec((tt, HEAD_DIM), lambda i: (i, 0))
    return pl.pallas_call(
        body, grid=(T // tt,),
        in_specs=[pl.BlockSpec((tt, W), lambda i: (i, col0)), _full((1, W)), tbl, tbl],
        out_specs=pl.BlockSpec((tt, W), lambda i: (i, 0)),
        out_shape=jax.ShapeDtypeStruct((T, W), F32),
        compiler_params=_cp("parallel"), name=name)(x, w_heads, cosf, sinsg)


def _q_prep_bwd(proj, w_heads, cosf, sinsg, dqs, *, tt, name):
    T = proj.shape[0]
    W = N_GROUPS * B_WIDTH

    def body(x_ref, w_ref, c_ref, s_ref, d0, d1, d2, dx_ref, dw_ref):
        @pl.when(pl.program_id(0) == 0)
        def _():
            dw_ref[...] = jnp.zeros_like(dw_ref)

        c, s = c_ref[...], s_ref[...]
        for gi, d_ref in enumerate((d0, d1, d2)):
            for h in range(B_HEADS):
                sl = slice((gi * B_HEADS + h) * HEAD_DIM, (gi * B_HEADS + h + 1) * HEAD_DIM)
                xhat, r = _head_rms(x_ref[:, sl])
                dyn = _rope_bwd(d_ref[:, h * HEAD_DIM:(h + 1) * HEAD_DIM], c, s)
                dw_ref[:, sl] += jnp.sum(dyn * xhat, axis=0, keepdims=True)
                dx_ref[:, sl] = _head_rms_bwd(dyn * w_ref[:, sl], xhat, r).astype(BF16)

    tbl = pl.BlockSpec((tt, HEAD_DIM), lambda i: (i, 0))
    dyb = pl.BlockSpec((tt, B_WIDTH), lambda i: (i, 0))
    return pl.pallas_call(
        body, grid=(T // tt,),
        in_specs=[pl.BlockSpec((tt, W), lambda i: (i, 0)), _full((1, W)), tbl, tbl, dyb, dyb, dyb],
        out_specs=[pl.BlockSpec((tt, W), lambda i: (i, 0)), _full((1, W))],
        out_shape=[jax.ShapeDtypeStruct((T, W), BF16), jax.ShapeDtypeStruct((1, W), F32)],
        compiler_params=_cp("arbitrary"), name=name)(proj, w_heads, cosf, sinsg, *dqs)


def _kv_prep_bwd(kv, w_heads, cosf, sinsg, dks, dvs, *, tt, name):
    T = kv.shape[0]

    def body(x_ref, w_ref, c_ref, s_ref, k0, k1, k2, v0, v1, v2, dx_ref, dw_ref):
        @pl.when(pl.program_id(0) == 0)
        def _():
            dw_ref[...] = jnp.zeros_like(dw_ref)

        c, s = c_ref[...], s_ref[...]
        for h in range(B_HEADS):
            sl = slice(h * HEAD_DIM, (h + 1) * HEAD_DIM)
            vs = slice(B_WIDTH + h * HEAD_DIM, B_WIDTH + (h + 1) * HEAD_DIM)
            xhat, r = _head_rms(x_ref[:, sl])
            dyn = _rope_bwd(k0[:, sl] + k1[:, sl] + k2[:, sl], c, s)
            dw_ref[:, sl] += jnp.sum(dyn * xhat, axis=0, keepdims=True)
            dx_ref[:, sl] = _head_rms_bwd(dyn * w_ref[:, sl], xhat, r).astype(BF16)
            dx_ref[:, vs] = (v0[:, sl] + v1[:, sl] + v2[:, sl]).astype(BF16)

    tbl = pl.BlockSpec((tt, HEAD_DIM), lambda i: (i, 0))
    dyb = pl.BlockSpec((tt, B_WIDTH), lambda i: (i, 0))
    return pl.pallas_call(
        body, grid=(T // tt,),
        in_specs=[dyb, _full((1, B_WIDTH)), tbl, tbl] + [dyb] * 6,
        out_specs=[pl.BlockSpec((tt, 2 * B_WIDTH), lambda i: (i, 0)), _full((1, B_WIDTH))],
        out_shape=[jax.ShapeDtypeStruct((T, 2 * B_WIDTH), BF16), jax.ShapeDtypeStruct((1, B_WIDTH), F32)],
        compiler_params=_cp("arbitrary"), name=name)(kv, w_heads, cosf, sinsg, *dks, *dvs)


def _band_masks(n_is_first=None):
    row = lax.broadcasted_iota(jnp.int32, (SPAN, SPAN), 0)
    col = lax.broadcasted_iota(jnp.int32, (SPAN, SPAN), 1)
    return row >= col, col >= row


def _dil_views(T, d):
    L = T // d
    return L, L // SPAN


def _dil_fwd(qr, kr, kv, gi, d, *, name):
    T = qr.shape[0]
    L, nb = _dil_views(T, d)

    def body(q_ref, kc_ref, kp_ref, vc_ref, vp_ref, o_ref, lse_ref):
        cur_ok, prev_band = _band_masks()
        prev_ok = prev_band & (pl.program_id(1) > 0)
        for h in range(B_HEADS):
            sl = slice(h * HEAD_DIM, (h + 1) * HEAD_DIM)
            q = q_ref[:, sl]
            sc = jnp.where(cur_ok, _dot_nt(q, kc_ref[:, sl]) * ATT_SCALE, NEG)
            sp = jnp.where(prev_ok, _dot_nt(q, kp_ref[:, sl]) * ATT_SCALE, NEG)
            m = jnp.maximum(jnp.max(sc, axis=-1, keepdims=True), jnp.max(sp, axis=-1, keepdims=True))
            pc = jnp.exp(sc - m)
            pp = jnp.exp(sp - m)
            l = jnp.sum(pc, axis=-1, keepdims=True) + jnp.sum(pp, axis=-1, keepdims=True)
            o_ref[:, sl] = (_dot(pc, vc_ref[:, sl]) + _dot(pp, vp_ref[:, sl])) / l
            lse_ref[:, sl] = jnp.broadcast_to(m + jnp.log(l), (SPAN, HEAD_DIM))

    blk = lambda f: pl.BlockSpec((SPAN, B_WIDTH), f)
    cur = lambda r, n: (n, r)
    prev = lambda r, n: (jnp.maximum(n - 1, 0), r)
    ov = jax.ShapeDtypeStruct((L, d * B_WIDTH), F32)
    o, lse = pl.pallas_call(
        body, grid=(d, nb),
        in_specs=[blk(lambda r, n: (n, r * N_GROUPS + gi)), blk(cur), blk(prev),
                  blk(lambda r, n: (n, 2 * r + 1)), blk(lambda r, n: (jnp.maximum(n - 1, 0), 2 * r + 1))],
        out_specs=[blk(cur), blk(cur)], out_shape=[ov, ov],
        compiler_params=_cp("parallel", "arbitrary"), name=name,
    )(qr.reshape(L, d * N_GROUPS * B_WIDTH), kr.reshape(L, d * B_WIDTH), kr.reshape(L, d * B_WIDTH),
      kv.reshape(L, d * 2 * B_WIDTH), kv.reshape(L, d * 2 * B_WIDTH))
    return o.reshape(T, B_WIDTH), lse.reshape(T, B_WIDTH)


def _dil_combine_fwd(os_, lses, *, tt, name):
    T = os_[0].shape[0]

    def body(o0, o1, o2, l0, l1, l2, y_ref, lse_ref):
        a, b, c = l0[...], l1[...], l2[...]
        m = jnp.maximum(jnp.maximum(a, b), c)
        wa, wb, wc = jnp.exp(a - m), jnp.exp(b - m), jnp.exp(c - m)
        den = wa + wb + wc
        y_ref[...] = (wa * o0[...] + wb * o1[...] + wc * o2[...]) / den
        lse_ref[...] = m + jnp.log(den)

    blk = pl.BlockSpec((tt, B_WIDTH), lambda i: (i, 0))
    sh = jax.ShapeDtypeStruct((T, B_WIDTH), F32)
    return pl.pallas_call(
        body, grid=(T // tt,), in_specs=[blk] * 6, out_specs=[blk, blk], out_shape=[sh, sh],
        compiler_params=_cp("parallel"), name=name)(*os_, *lses)


def _dil_bwd_prep(dmix, mix_main, *, tt, name, dep=None):
    T = mix_main.shape[0]

    def kernel_body(dy_ref, y_ref, dmm_ref, dd_ref):
        for h in range(B_HEADS):
            sl = slice(h * HEAD_DIM, (h + 1) * HEAD_DIM)
            dy = dy_ref[:, sl]
            dmm_ref[:, sl] = dy.astype(BF16)
            dd_ref[:, sl] = jnp.broadcast_to(jnp.sum(dy * y_ref[:, sl], axis=-1, keepdims=True), (tt, HEAD_DIM))

    blk = pl.BlockSpec((tt, B_WIDTH), lambda i: (i, 0))
    body, dep_specs, dep_args = _dep(kernel_body, 2, dep)
    return pl.pallas_call(
        body, grid=(T // tt,), in_specs=[blk, blk] + dep_specs, out_specs=[blk, blk],
        out_shape=[jax.ShapeDtypeStruct((T, B_WIDTH), BF16), jax.ShapeDtypeStruct((T, B_WIDTH), F32)],
        compiler_params=_cp("parallel"), name=name)(dmix, mix_main, *dep_args)


def _dil_bwd_dq(qr, kr, kv, dmm, lse, dd, gi, d, *, name):
    T = qr.shape[0]
    L, nb = _dil_views(T, d)

    def body(q_ref, kc_ref, kp_ref, vc_ref, vp_ref, dy_ref, lse_ref, dd_ref, dq_ref):
        cur_ok, prev_band = _band_masks()
        prev_ok = prev_band & (pl.program_id(1) > 0)
        for h in range(B_HEADS):
            sl = slice(h * HEAD_DIM, (h + 1) * HEAD_DIM)
            q, dy = q_ref[:, sl], dy_ref[:, sl]
            kc, kp = kc_ref[:, sl], kp_ref[:, sl]
            lse_h = jnp.max(lse_ref[:, sl], axis=-1, keepdims=True)
            dd_h = jnp.max(dd_ref[:, sl], axis=-1, keepdims=True)
            pc = jnp.exp(jnp.where(cur_ok, _dot_nt(q, kc) * ATT_SCALE, NEG) - lse_h)
            pp = jnp.exp(jnp.where(prev_ok, _dot_nt(q, kp) * ATT_SCALE, NEG) - lse_h)
            dsc = pc * (_dot_nt(dy, vc_ref[:, sl]) - dd_h) * ATT_SCALE
            dsp = pp * (_dot_nt(dy, vp_ref[:, sl]) - dd_h) * ATT_SCALE
            dq_ref[:, sl] = _dot(dsc, kc) + _dot(dsp, kp)

    blk = lambda f: pl.BlockSpec((SPAN, B_WIDTH), f)
    cur = lambda r, n: (n, r)
    prev = lambda r, n: (jnp.maximum(n - 1, 0), r)
    v2 = lambda a: a.reshape(L, d * a.shape[1])
    dq = pl.pallas_call(
        body, grid=(d, nb),
        in_specs=[blk(lambda r, n: (n, r * N_GROUPS + gi)), blk(cur), blk(prev),
                  blk(lambda r, n: (n, 2 * r + 1)), blk(lambda r, n: (jnp.maximum(n - 1, 0), 2 * r + 1)),
                  blk(cur), blk(cur), blk(cur)],
        out_specs=blk(cur), out_shape=jax.ShapeDtypeStruct((L, d * B_WIDTH), F32),
        compiler_params=_cp("parallel", "arbitrary"), name=name,
    )(v2(qr), v2(kr), v2(kr), v2(kv), v2(kv), v2(dmm), v2(lse), v2(dd))
    return dq.reshape(T, B_WIDTH)


def _dil_bwd_dkv(qr, kr, kv, dmm, lse, dd, gi, d, *, name):
    T = qr.shape[0]
    L, nb = _dil_views(T, d)

    def body(k_ref, v_ref, q0_ref, q1_ref, dy0_ref, dy1_ref, lse0_ref, lse1_ref, dd0_ref, dd1_ref, dk_ref, dv_ref):
        cur_ok, prev_band = _band_masks()
        next_ok = prev_band & (pl.program_id(1) < nb - 1)
        for h in range(B_HEADS):
            sl = slice(h * HEAD_DIM, (h + 1) * HEAD_DIM)
            k, v = k_ref[:, sl], v_ref[:, sl]
            dk = jnp.zeros((SPAN, HEAD_DIM), F32)
            dv = jnp.zeros((SPAN, HEAD_DIM), F32)
            for ok, q_ref, dy_ref, lse_ref, dd_ref in ((cur_ok, q0_ref, dy0_ref, lse0_ref, dd0_ref),
                                                         (next_ok, q1_ref, dy1_ref, lse1_ref, dd1_ref)):
                q, dy = q_ref[:, sl], dy_ref[:, sl]
                lse_h = jnp.max(lse_ref[:, sl], axis=-1, keepdims=True)
                dd_h = jnp.max(dd_ref[:, sl], axis=-1, keepdims=True)
                p = jnp.exp(jnp.where(ok, _dot_nt(q, k) * ATT_SCALE, NEG) - lse_h)
                ds = p * (_dot_nt(dy, v) - dd_h) * ATT_SCALE
                dk = dk + _dot_tn(ds, q)
                dv = dv + _dot_tn(p, dy)
            dk_ref[:, sl] = dk
            dv_ref[:, sl] = dv

    blk = lambda f: pl.BlockSpec((SPAN, B_WIDTH), f)
    cur = lambda r, n: (n, r)
    nxt = lambda r, n: (jnp.minimum(n + 1, nb - 1), r)
    qcur = lambda r, n: (n, r * N_GROUPS + gi)
    qnxt = lambda r, n: (jnp.minimum(n + 1, nb - 1), r * N_GROUPS + gi)
    v2 = lambda a: a.reshape(L, d * a.shape[1])
    ov = jax.ShapeDtypeStruct((L, d * B_WIDTH), F32)
    dk, dv = pl.pallas_call(
        body, grid=(d, nb),
        in_specs=[blk(cur), blk(lambda r, n: (n, 2 * r + 1)), blk(qcur), blk(qnxt),
                  blk(cur), blk(nxt), blk(cur), blk(nxt), blk(cur), blk(nxt)],
        out_specs=[blk(cur), blk(cur)], out_shape=[ov, ov],
        compiler_params=_cp("parallel", "arbitrary"), name=name,
    )(v2(kr), v2(kv), v2(qr), v2(qr), v2(dmm), v2(dmm), v2(lse), v2(lse), v2(dd), v2(dd))
    return dk.reshape(T, B_WIDTH), dv.reshape(T, B_WIDTH)


DILS_UNROLL = 4


def _dils_specs(gi, d, nblk):
    blk = lambda f: pl.BlockSpec((SPAN * d, HEAD_DIM), f)
    return {
        "q": blk(lambda h, n: (n, gi * B_HEADS + h)), "q_next": blk(lambda h, n: (jnp.minimum(n + 1, nblk - 1), gi * B_HEADS + h)),
        "cur": blk(lambda h, n: (n, h)), "prev": blk(lambda h, n: (jnp.maximum(n - 1, 0), h)),
        "next": blk(lambda h, n: (jnp.minimum(n + 1, nblk - 1), h)),
        "v": blk(lambda h, n: (n, B_HEADS + h)), "v_prev": blk(lambda h, n: (jnp.maximum(n - 1, 0), B_HEADS + h)),
    }


def _dils_fwd(qr, kr, kv, gi, d, *, name):
    T = qr.shape[0]
    nblk = T // (SPAN * d)
    sp = _dils_specs(gi, d, nblk)

    def body(q_ref, kc_ref, kp_ref, vc_ref, vp_ref, o_ref, lse_ref):
        cur_ok, prev_band = _band_masks()
        prev_ok = prev_band & (pl.program_id(1) > 0)

        def residue(r, carry):
            rows = pl.ds(r, SPAN, stride=d)
            q = q_ref[rows, :]
            sc = jnp.where(cur_ok, _dot_nt(q, kc_ref[rows, :]) * ATT_SCALE, NEG)
            sp_ = jnp.where(prev_ok, _dot_nt(q, kp_ref[rows, :]) * ATT_SCALE, NEG)
            m = jnp.maximum(jnp.max(sc, axis=-1, keepdims=True), jnp.max(sp_, axis=-1, keepdims=True))
            pc = jnp.exp(sc - m)
            pp = jnp.exp(sp_ - m)
            l = jnp.sum(pc, axis=-1, keepdims=True) + jnp.sum(pp, axis=-1, keepdims=True)
            o_ref[rows, :] = (_dot(pc, vc_ref[rows, :]) + _dot(pp, vp_ref[rows, :])) / l
            lse_ref[rows, :] = jnp.broadcast_to(m + jnp.log(l), (SPAN, HEAD_DIM))
            return carry

        lax.fori_loop(0, d, residue, 0, unroll=DILS_UNROLL)

    sh = jax.ShapeDtypeStruct((T, B_WIDTH), F32)
    return pl.pallas_call(
        body, grid=(B_HEADS, nblk), in_specs=[sp["q"], sp["cur"], sp["prev"], sp["v"], sp["v_prev"]],
        out_specs=[sp["cur"], sp["cur"]], out_shape=[sh, sh],
        compiler_params=_cp("parallel", "arbitrary"), name=name)(qr, kr, kr, kv, kv)


def _dils_bwd_dq(qr, kr, kv, dmix, lse, dd, gi, d, *, name):
    T = qr.shape[0]
    nblk = T // (SPAN * d)
    sp = _dils_specs(gi, d, nblk)

    def body(q_ref, kc_ref, kp_ref, vc_ref, vp_ref, dy_ref, lse_ref, dd_ref, dq_ref):
        cur_ok, prev_band = _band_masks()
        prev_ok = prev_band & (pl.program_id(1) > 0)

        def residue(r, carry):
            rows = pl.ds(r, SPAN, stride=d)
            q, dy = q_ref[rows, :], dy_ref[rows, :]
            kc, kp = kc_ref[rows, :], kp_ref[rows, :]
            lse_h = jnp.max(lse_ref[rows, :], axis=-1, keepdims=True)
            dd_h = jnp.max(dd_ref[rows, :], axis=-1, keepdims=True)
            pc = jnp.exp(jnp.where(cur_ok, _dot_nt(q, kc) * ATT_SCALE, NEG) - lse_h)
            pp = jnp.exp(jnp.where(prev_ok, _dot_nt(q, kp) * ATT_SCALE, NEG) - lse_h)
            dsc = pc * (_dot_nt(dy, vc_ref[rows, :]) - dd_h) * ATT_SCALE
            dsp = pp * (_dot_nt(dy, vp_ref[rows, :]) - dd_h) * ATT_SCALE
            dq_ref[rows, :] = _dot(dsc, kc) + _dot(dsp, kp)
            return carry

        lax.fori_loop(0, d, residue, 0, unroll=DILS_UNROLL)

    return pl.pallas_call(
        body, grid=(B_HEADS, nblk),
        in_specs=[sp["q"], sp["cur"], sp["prev"], sp["v"], sp["v_prev"], sp["cur"], sp["cur"], sp["cur"]],
        out_specs=sp["cur"], out_shape=jax.ShapeDtypeStruct((T, B_WIDTH), F32),
        compiler_params=_cp("parallel", "arbitrary"), name=name)(qr, kr, kr, kv, kv, dmix, lse, dd)


def _dils_bwd_dkv(qr, kr, kv, dmix, lse, dd, gi, d, *, name):
    T = qr.shape[0]
    nblk = T // (SPAN * d)
    sp = _dils_specs(gi, d, nblk)

    def body(k_ref, v_ref, q0_ref, q1_ref, dy0_ref, dy1_ref, lse0_ref, lse1_ref, dd0_ref, dd1_ref, dk_ref, dv_ref):
        cur_ok, prev_band = _band_masks()
        next_ok = prev_band & (pl.program_id(1) < nblk - 1)

        def residue(r, carry):
            rows = pl.ds(r, SPAN, stride=d)
            k, v = k_ref[rows, :], v_ref[rows, :]
            dk = jnp.zeros((SPAN, HEAD_DIM), F32)
            dv = jnp.zeros((SPAN, HEAD_DIM), F32)
            for ok, q_ref, dy_ref, lse_ref, dd_ref in ((cur_ok, q0_ref, dy0_ref, lse0_ref, dd0_ref),
                                                         (next_ok, q1_ref, dy1_ref, lse1_ref, dd1_ref)):
                q, dy = q_ref[rows, :], dy_ref[rows, :]
                lse_h = jnp.max(lse_ref[rows, :], axis=-1, keepdims=True)
                dd_h = jnp.max(dd_ref[rows, :], axis=-1, keepdims=True)
                p = jnp.exp(jnp.where(ok, _dot_nt(q, k) * ATT_SCALE, NEG) - lse_h)
                ds = p * (_dot_nt(dy, v) - dd_h) * ATT_SCALE
                dk = dk + _dot_tn(ds, q)
                dv = dv + _dot_tn(p, dy)
            dk_ref[rows, :] = dk
            dv_ref[rows, :] = dv
            return carry

        lax.fori_loop(0, d, residue, 0, unroll=DILS_UNROLL)

    sh = jax.ShapeDtypeStruct((T, B_WIDTH), F32)
    return pl.pallas_call(
        body, grid=(B_HEADS, nblk),
        in_specs=[sp["cur"], sp["v"], sp["q"], sp["q_next"], sp["cur"], sp["next"], sp["cur"], sp["next"], sp["cur"], sp["next"]],
        out_specs=[sp["cur"], sp["cur"]], out_shape=[sh, sh],
        compiler_params=_cp("parallel", "arbitrary"), name=name)(kr, kv, qr, qr, dmix, dmix, lse, lse, dd, dd)


A_MQ_COL = 4 * A_WIDTH // MEM_WIDTH
B_MQ_COL = N_GROUPS * B_WIDTH // MEM_WIDTH


def _row(v):
    return v.reshape(1, -1).astype(F32)


def _local_step(x, mem, tgt, get_w, P, put_g, first_dep=None):
    T = x.shape[0]
    cosf, sinsg = _rope_tables(T)
    lb_soft = jax.nn.softmax(P["a_lb_logits"].astype(F32), axis=0)
    lb = lb_soft[0:1]
    qw_heads = jnp.repeat(P["b_qnorm"][0], B_HEADS, axis=0).reshape(1, -1)
    kw_heads = jnp.tile(_row(P["b_knorm"]), (1, B_HEADS))
    mqw = [jnp.tile(_row(P["mem_qnorm"][l]), (1, MEM_HEADS)) for l in range(2)]
    mkw = [jnp.tile(_row(P["mem_knorm"][l]), (1, MEM_HEADS)) for l in range(2)]
    nmix = [_row(P["norm_mix"][l]) for l in range(2)]
    nffn = [_row(P["norm_ffn"][l]) for l in range(2)]
    mnorm = [_row(P["mem_norm"][l]) for l in range(2)]
    kvn = _row(P["kv_norm"])
    onorm = _row(P["a_onorm"])
    W = {}

    def w_of(name, after=None):
        if name not in W:
            W[name] = get_w(name, after)
        return W[name]

    proj_a, xn0 = _rms_matmul(x, nmix[0], w_of("a_w_in"), tt=512, tn=1664, wt=True, name="proj_a", dep=first_dep)
    mkv0, mn0 = _rms_matmul(mem, mnorm[0], w_of("w_mem_kv0"), tt=MEM_TOKENS, tn=2 * MEM_WIDTH, wt=False, name="mem_kv0")
    o_raw, st = _hgrn2_fwd(proj_a, lb, name="hgrn2_fwd")
    mm0 = _a_post_fwd(o_raw, proj_a, onorm, tt=512, name="a_post_fwd")
    mo0 = _mem_attn_fwd(proj_a, A_MQ_COL, mkv0, mqw[0], mkw[0], tt=512, name="mem_attn_fwd0")
    hm0 = _mm_res(x, mm0, mo0, w_of("w_out0", mo0), tt=512, name="out_proj0")
    gu0, hn0 = _rms_matmul(hm0, nffn[0], w_of("w_gate_up0", hm0), tt=512, tn=1408, wt=True, out_dtype=BF16, name="gate_up0")
    h1 = _swiglu_down(hm0, gu0, w_of("w_down0", gu0), tt=256, name="down0")
    kv, hkn = _rms_matmul(h1, kvn, w_of("w_kv", h1), tt=512, tn=768, wt=True, name="kv_proj")
    kr = _headnorm_rope_fwd(kv, kw_heads, cosf, sinsg, col0=0, n_heads=B_HEADS, tt=512, name="k_prep")

    proj_b, xn1 = _rms_matmul(h1, nmix[1], w_of("b_w_in", kr), tt=512, tn=1280, wt=True, name="proj_b")
    mkv1, mn1 = _rms_matmul(mem, mnorm[1], w_of("w_mem_kv1", kr), tt=MEM_TOKENS, tn=2 * MEM_WIDTH, wt=False, name="mem_kv1")
    qr = _headnorm_rope_fwd(proj_b, qw_heads, cosf, sinsg, col0=0, n_heads=N_GROUPS * B_HEADS, tt=512, name="q_prep")
    outs = [(_dil_fwd if d == 1 else _dils_fwd)(qr, kr, kv, gi, d, name=f"dil_fwd{gi}") for gi, d in enumerate(DILATIONS)]
    mm1, lse_tot = _dil_combine_fwd([o for o, _ in outs], [s for _, s in outs], tt=512, name="dil_combine")
    mo1 = _mem_attn_fwd(proj_b, B_MQ_COL, mkv1, mqw[1], mkw[1], tt=512, name="mem_attn_fwd1")
    hm1 = _mm_res(h1, mm1, mo1, w_of("w_out1", mo1), tt=512, name="out_proj1")
    gu1, hn1 = _rms_matmul(hm1, nffn[1], w_of("w_gate_up1", hm1), tt=512, tn=1408, wt=True, out_dtype=BF16, name="gate_up1")
    y = _swiglu_down(hm1, gu1, w_of("w_down1", gu1), tt=256, name="down1")
    dy, sq = _loss_kernel(y, tgt, tt=512, name="loss")

    gP = {}
    zeros_mem = jnp.zeros((MEM_TOKENS, D_MODEL), F32)

    def ffn_bwd(l, dh, hm, gu, hn):
        dgu, g_wd = _swiglu_bwd(dh, gu, w_of(f"w_down{l}"), tt=256, name=f"swiglu_bwd{l}")
        g_wgu = _mm_tn(dgu, hn, tt=512, tka=1408, name=f"g_w_gate_up{l}")
        sent = put_g({f"w_down{l}": g_wd, f"w_gate_up{l}": g_wgu})
        dhm, g_nf = _rms_bwd_dx(hm, nffn[l], w_of(f"w_gate_up{l}"), dgu, dh, tt=256, wt=True, name=f"gate_up_bwd{l}", dep=sent)
        return dhm, g_nf

    def mix_bwd(l, dhm, mix_main, mix_mem, proj, qcol, mkv, mn):
        dmix, g_wout = _out_proj_bwd(dhm, mix_main, mix_mem, w_of(f"w_out{l}"), tt=512, name=f"out_proj_bwd{l}")
        dmq, dmkv, dqw, dkw = _mem_attn_bwd(proj, qcol, mkv, mqw[l], mkw[l], dmix, tt=512, name=f"mem_attn_bwd{l}")
        g_wmkv = _mm_tn(mn, dmkv, tt=MEM_TOKENS, tka=512, name=f"g_w_mem_kv{l}")
        sent = put_g({f"w_out{l}": g_wout, f"w_mem_kv{l}": g_wmkv})
        _, g_mn = _rms_bwd_dx(mem, mnorm[l], w_of(f"w_mem_kv{l}"), dmkv, zeros_mem, tt=MEM_TOKENS, wt=False, name=f"mem_kv_bwd{l}")
        fold = lambda v: v.reshape(MEM_HEADS, MEM_HEAD_DIM).sum(axis=0)
        return dmix, dmq, g_mn, fold(dqw), fold(dkw), sent

    dhm1, g_nf1 = ffn_bwd(1, dy, hm1, gu1, hn1)
    dmix1, dmq1, g_mn1, g_mq1, g_mk1, sent = mix_bwd(1, dhm1, mm1, mo1, proj_b, B_MQ_COL, mkv1, mn1)
    dmm, dd = _dil_bwd_prep(dmix1, mm1, tt=512, name="dil_bwd_prep", dep=sent)
    dqs, dks, dvs = [], [], []
    for gi, d in enumerate(DILATIONS):
        if d == 1:
            dqs.append(_dil_bwd_dq(qr, kr, kv, dmm, lse_tot, dd, gi, d, name=f"dil_bwd_dq{gi}"))
            dk_g, dv_g = _dil_bwd_dkv(qr, kr, kv, dmm, lse_tot, dd, gi, d, name=f"dil_bwd_dkv{gi}")
        else:
            dqs.append(_dils_bwd_dq(qr, kr, kv, dmix1, lse_tot, dd, gi, d, name=f"dil_bwd_dq{gi}"))
            dk_g, dv_g = _dils_bwd_dkv(qr, kr, kv, dmix1, lse_tot, dd, gi, d, name=f"dil_bwd_dkv{gi}")
        dks.append(dk_g)
        dvs.append(dv_g)
    dq_raw, dqw = _q_prep_bwd(proj_b, qw_heads, cosf, sinsg, dqs, tt=512, name="q_prep_bwd")
    dkv, dkw = _kv_prep_bwd(kv, kw_heads, cosf, sinsg, dks, dvs, tt=512, name="kv_prep_bwd")
    dproj_b = jnp.concatenate([dq_raw, dmq1], axis=1)
    g_wb = _mm_tn(dproj_b, xn1, tt=512, tka=1280, name="g_b_w_in")
    g_wkv = _mm_tn(dkv, hkn, tt=512, tka=768, name="g_w_kv")
    sent = put_g({"b_w_in": g_wb, "w_kv": g_wkv})
    dh1, g_nm1 = _rms_bwd_dx(h1, nmix[1], w_of("b_w_in"), dproj_b, dhm1, tt=256, wt=True, name="proj_b_bwd", dep=sent)
    dh1, g_kvn = _rms_bwd_dx(h1, kvn, w_of("w_kv"), dkv, dh1, tt=256, wt=True, name="kv_proj_bwd")

    dhm0, g_nf0 = ffn_bwd(0, dh1, hm0, gu0, hn0)
    dmix0, dmq0, g_mn0, g_mq0, g_mk0, sent = mix_bwd(0, dhm0, mm0, mo0, proj_a, A_MQ_COL, mkv0, mn0)
    do_raw, dg, g_onorm = _a_post_bwd(o_raw, proj_a, onorm, dmix0, tt=512, name="a_post_bwd", dep=sent)
    dq, dz, dv, dlb = _hgrn2_bwd(proj_a, lb, st, do_raw, name="hgrn2_bwd")
    dproj_a = jnp.concatenate([dq, dz, dv, dg, dmq0], axis=1)
    sent = put_g({"a_w_in": _mm_tn(dproj_a, xn0, tt=512, tka=1664, name="g_a_w_in")})
    gx, g_nm0 = _rms_bwd_dx(x, nmix[0], w_of("a_w_in"), dproj_a, dhm0, tt=256, wt=True, name="proj_a_bwd", dep=sent)

    dl0 = lb_soft[0:1] * lb_soft[1:2] * dlb
    gP["a_lb_logits"] = jnp.concatenate([dl0, -dl0], axis=0)
    gP["a_onorm"] = g_onorm
    gP["norm_mix"] = jnp.concatenate([g_nm0, g_nm1], axis=0)
    gP["norm_ffn"] = jnp.concatenate([g_nf0, g_nf1], axis=0)
    gP["b_qnorm"] = dqw.reshape(N_GROUPS, B_HEADS, HEAD_DIM).sum(axis=1)[None]
    gP["kv_norm"] = g_kvn.reshape(-1)
    gP["b_knorm"] = dkw.reshape(B_HEADS, HEAD_DIM).sum(axis=0)
    gP["mem_norm"] = jnp.concatenate([g_mn0, g_mn1], axis=0)
    gP["mem_qnorm"] = jnp.stack([g_mq0, g_mq1])
    gP["mem_knorm"] = jnp.stack([g_mk0, g_mk1])
    return sq, gx, gP


MESH_ID = pl.DeviceIdType.MESH
HBM_SPEC = pl.BlockSpec(memory_space=pltpu.HBM)


def _position():
    return lax.axis_index("x"), lax.axis_index("y"), lax.axis_index("c")


def _all_gather(blocks, *, name):
    n = len(blocks)

    def body(*refs):
        x_refs, out_refs = refs[:n], refs[n:2 * n]
        send_sems, recv_sems, local_sems = refs[2 * n:]
        x, y, c = _position()
        me, sibling = (x, y, c), (x, y, 1 - c)
        chips = [(1 - x, y), (x, 1 - y), (1 - x, 1 - y)]

        def slot(a, px, py, pc):
            return out_refs[a].at[4 * px + 2 * py + pc]

        def copy(a, k, blk, to, src=None):
            return pltpu.make_async_remote_copy(
                src_ref=slot(a, *blk) if src is None else src, dst_ref=slot(a, *blk),
                send_sem=send_sems.at[7 * a + k], recv_sem=recv_sems.at[7 * a + k], device_id=to, device_id_type=MESH_ID)

        mine = [pltpu.make_async_copy(x_refs[a], slot(a, *me), local_sems.at[a]) for a in range(n)]
        for cp in mine:
            cp.start()
        first = []
        for a in range(n):
            first.append(copy(a, 0, me, sibling, src=x_refs[a]))
            first += [copy(a, 1 + j, me, (*chip, c), src=x_refs[a]) for j, chip in enumerate(chips)]
        for cp in first:
            cp.start()
        passed = []
        for j, chip in enumerate(chips):
            for a in range(n):
                copy(a, 1 + j, (*chip, c), me).wait_recv()
                cp = copy(a, 4 + j, (*chip, c), sibling)
                cp.start()
                passed.append(cp)
        for a in range(n):
            copy(a, 0, sibling, me).wait_recv()
            for j, chip in enumerate(chips):
                copy(a, 4 + j, (*chip, 1 - c), me).wait_recv()
        for cp in first + passed:
            cp.wait_send()
        for cp in mine:
            cp.wait()

    return pl.pallas_call(
        body, out_shape=[jax.ShapeDtypeStruct((N_DEV,) + b.shape, b.dtype) for b in blocks],
        in_specs=[HBM_SPEC] * n, out_specs=[HBM_SPEC] * n,
        scratch_shapes=[pltpu.SemaphoreType.DMA((7 * n,)), pltpu.SemaphoreType.DMA((7 * n,)), pltpu.SemaphoreType.DMA((n,))],
        name=name)(*blocks)


SEM_SPEC = pl.BlockSpec(memory_space=pltpu.SEMAPHORE)
ANY_SPEC = pl.BlockSpec(memory_space=pl.ANY)
DATAFLOW = pltpu.SideEffectType.DATAFLOW_SIDE_EFFECTING


def _peer(k, x, y, c):
    return (1 - x if (k >> 2) & 1 else x, 1 - y if (k >> 1) & 1 else y, 1 - c if k & 1 else c)


def _own_slot_filled(own_block):
    x, y, c = _position()
    zone = lax.empty((N_DEV,) + own_block.shape, own_block.dtype)
    return lax.dynamic_update_slice_in_dim(zone, own_block[None], 4 * x + 2 * y + c, axis=0)


def _split_start(srcs, scatter, after, *, name):
    n = len(srcs)
    extra = [] if after is None else [after]
    x, y, c = _position()
    me = 4 * x + 2 * y + c
    lands = [_own_slot_filled(lax.dynamic_index_in_dim(s, me, 0, keepdims=False) if scatter else s) for s in srcs]

    def body(*refs):
        src_refs, land_refs = refs[:n], refs[n:2 * n]
        send_sems, recv_sems = refs[2 * n + len(extra)], refs[2 * n + len(extra) + 1]
        token = refs[-1]
        bx, by, bc = _position()
        bme = 4 * bx + 2 * by + bc
        for a in range(n):
            for k in range(1, N_DEV):
                tx, ty, tc = _peer(k, bx, by, bc)
                src = src_refs[a].at[4 * tx + 2 * ty + tc] if scatter else src_refs[a]
                pltpu.make_async_remote_copy(
                    src_ref=src, dst_ref=land_refs[a].at[bme],
                    send_sem=send_sems.at[7 * a + k - 1], recv_sem=recv_sems.at[7 * a + k - 1],
                    device_id=(tx, ty, tc), device_id_type=MESH_ID).start()
        token[...] = jnp.zeros_like(token)

    hbm = lambda a: pltpu.HBM(a.shape, a.dtype)
    outs = pl.pallas_call(
        body, name=name,
        out_shape=(pltpu.SemaphoreType.DMA((7 * n,)), pltpu.SemaphoreType.DMA((7 * n,)),
                   *[hbm(s) for s in srcs], *[hbm(l) for l in lands], jax.ShapeDtypeStruct((8, 128), F32)),
        in_specs=[HBM_SPEC] * (2 * n) + [ANY_SPEC] * len(extra),
        out_specs=(SEM_SPEC, SEM_SPEC, *[HBM_SPEC] * (2 * n), pl.BlockSpec(memory_space=pltpu.VMEM)),
        input_output_aliases={i: 2 + i for i in range(2 * n)},
        compiler_params=pltpu.CompilerParams(has_side_effects=DATAFLOW),
    )(*[pltpu.with_memory_space_constraint(s, pltpu.HBM) for s in srcs],
      *[pltpu.with_memory_space_constraint(l, pltpu.HBM) for l in lands], *extra)
    return {"n": n, "scatter": scatter, "send": outs[0], "recv": outs[1], "srcs": outs[2:2 + n],
            "lands": outs[2 + n:2 + 2 * n], "token": outs[-1]}


def _split_wait(handle, after, *, name):
    n, scatter = handle["n"], handle["scatter"]

    def body(*refs):
        src_refs, land_refs = refs[:n], refs[n:2 * n]
        send_sems, recv_sems = refs[2 * n], refs[2 * n + 1]
        bx, by, bc = _position()
        for a in range(n):
            for k in range(1, N_DEV):
                src = src_refs[a].at[0] if scatter else src_refs[a]
                cp = pltpu.make_async_remote_copy(
                    src_ref=src, dst_ref=land_refs[a].at[0],
                    send_sem=send_sems.at[7 * a + k - 1], recv_sem=recv_sems.at[7 * a + k - 1],
                    device_id=_peer(k, bx, by, bc), device_id_type=MESH_ID)
                cp.wait_send()
                cp.wait_recv()

    hbm = lambda a: pltpu.HBM(a.shape, a.dtype)
    outs = pl.pallas_call(
        body, name=name,
        out_shape=(*[hbm(s) for s in handle["srcs"]], *[hbm(l) for l in handle["lands"]]),
        in_specs=[HBM_SPEC] * (2 * n) + [SEM_SPEC, SEM_SPEC, ANY_SPEC],
        out_specs=tuple([HBM_SPEC] * (2 * n)),
        input_output_aliases={i: i for i in range(2 * n)},
        compiler_params=pltpu.CompilerParams(has_side_effects=DATAFLOW),
    )(*handle["srcs"], *handle["lands"], handle["send"], handle["recv"], after)
    return list(outs[n:])


def _sum_sources(parts, *, tr, name):
    n, R, C = parts.shape

    def body(p_ref, o_ref):
        acc = p_ref[0].astype(F32)
        for s in range(1, n):
            acc = acc + p_ref[s].astype(F32)
        o_ref[...] = acc

    return pl.pallas_call(
        body, grid=(R // tr,), in_specs=[pl.BlockSpec((n, tr, C), lambda i: (0, i, 0))],
        out_specs=pl.BlockSpec((tr, C), lambda i: (i, 0)),
        out_shape=jax.ShapeDtypeStruct((R, C), F32), compiler_params=_cp("parallel"), name=name)(parts)


def _adamw_math(g, w, m, v):
    c1 = 1.0 - ADAM_B1 ** ADAM_STEP
    c2 = 1.0 - ADAM_B2 ** ADAM_STEP
    nm = ADAM_B1 * m + (1.0 - ADAM_B1) * g
    nv = ADAM_B2 * v + (1.0 - ADAM_B2) * (g * g)
    return -ADAM_LR * ((nm / c1) / (jnp.sqrt(nv / c2) + ADAM_EPS) + ADAM_WD * w), nm, nv


def _reduce_adamw(received, w, m, v, *, col, tr, name):
    L, R, C = w.shape

    def body(*refs):
        p_refs = refs[:L]
        w_ref, m_ref, v_ref, g_ref, d_ref, nm_ref, nv_ref = refs[L:]
        for l in range(L):
            @pl.when(pl.program_id(0) == l)
            def _(p_ref=p_refs[l]):
                acc = p_ref[0].astype(F32)
                for s in range(1, N_DEV):
                    acc = acc + p_ref[s].astype(F32)
                g = acc.T if col else acc
                g_ref[...] = g
                d_ref[...], nm_ref[...], nv_ref[...] = _adamw_math(g, w_ref[...], m_ref[...], v_ref[...])

    p_spec = (pl.BlockSpec((N_DEV, C, tr), lambda l, i: (0, 0, i)) if col
              else pl.BlockSpec((N_DEV, tr, C), lambda l, i: (0, i, 0)))
    blk = pl.BlockSpec((None, tr, C), lambda l, i: (l, i, 0))
    sh = jax.ShapeDtypeStruct((L, R, C), F32)
    return pl.pallas_call(
        body, grid=(L, R // tr), in_specs=[p_spec] * L + [blk] * 3, out_specs=[blk] * 4, out_shape=[sh] * 4,
        compiler_params=_cp("parallel", "parallel"), name=name)(*received, w, m, v)


def _adamw(g, w, m, v, *, tr, name):
    L, R, C = w.shape

    def body(g_ref, w_ref, m_ref, v_ref, d_ref, nm_ref, nv_ref):
        d_ref[...], nm_ref[...], nv_ref[...] = _adamw_math(g_ref[...], w_ref[...], m_ref[...], v_ref[...])

    blk = pl.BlockSpec((None, tr, C), lambda l, i: (l, i, 0))
    sh = jax.ShapeDtypeStruct((L, R, C), F32)
    return pl.pallas_call(
        body, grid=(L, R // tr), in_specs=[blk] * 4, out_specs=[blk] * 3, out_shape=[sh] * 3,
        compiler_params=_cp("parallel", "parallel"), name=name)(g, w, m, v)


UNITS = {
    "a_w_in": ("a_w_in", 0, True), "w_mem_kv0": ("w_mem_kv", 0, False), "w_out0": ("w_out", 0, False),
    "w_gate_up0": ("w_gate_up", 0, True), "w_down0": ("w_down", 0, False), "w_kv": ("w_kv", None, True),
    "b_w_in": ("b_w_in", 0, True), "w_mem_kv1": ("w_mem_kv", 1, False), "w_out1": ("w_out", 1, False),
    "w_gate_up1": ("w_gate_up", 1, True), "w_down1": ("w_down", 1, False),
}
BIG = ("a_w_in", "b_w_in", "w_kv", "w_mem_kv", "w_out", "w_gate_up", "w_down")
ADAMW_ROW_TILE = {"a_w_in": 256, "b_w_in": 256, "w_kv": 256, "w_mem_kv": 128, "w_out": 128, "w_gate_up": 176, "w_down": 176}
TRANSPOSED_UPDATE = ("w_gate_up",)


def _wire_block(weights, unit):
    name, layer, col = UNITS[unit]
    a = weights[name] if layer is None else weights[name][layer]
    return (a.T if col else a).astype(BF16)


SMALL_REPLICATED = ("norm_mix", "norm_ffn", "b_qnorm", "kv_norm", "b_knorm", "mem_norm", "mem_qnorm", "mem_knorm")
SMALL_SHARDED = ("a_lb_logits", "a_onorm")
SMALL_ORDER = SMALL_REPLICATED + SMALL_SHARDED
LANES = 128


def _prod(shape):
    n = 1
    for s in shape:
        n *= s
    return n


def _pack_flat(arrays, rows, cols, dtype):
    flat = jnp.concatenate([a.reshape(-1).astype(dtype) for a in arrays])
    return jnp.pad(flat, (0, rows * cols - flat.shape[0])).reshape(rows, cols)


def _unpack_flat(packed, shapes):
    flat = packed.reshape(-1)
    out, off = [], 0
    for s in shapes:
        out.append(flat[off:off + _prod(s)].reshape(s))
        off += _prod(s)
    return out


def kernel(x, mem, norm_mix, norm_ffn, a_w_in, a_lb_logits, a_onorm, b_w_in, b_qnorm, kv_norm, w_kv, b_knorm, mem_norm, w_mem_kv, mem_qnorm, mem_knorm, w_out, w_gate_up, w_down, loss_target, m_norm_mix, m_norm_ffn, m_a_w_in, m_a_lb_logits, m_a_onorm, m_b_w_in, m_b_qnorm, m_kv_norm, m_w_kv, m_b_knorm, m_mem_norm, m_w_mem_kv, m_mem_qnorm, m_mem_knorm, m_w_out, m_w_gate_up, m_w_down, v_norm_mix, v_norm_ffn, v_a_w_in, v_a_lb_logits, v_a_onorm, v_b_w_in, v_b_qnorm, v_kv_norm, v_w_kv, v_b_knorm, v_mem_norm, v_w_mem_kv, v_mem_qnorm, v_mem_knorm, v_w_out, v_w_gate_up, v_w_down):
    names = ("norm_mix", "norm_ffn", "a_w_in", "a_lb_logits", "a_onorm", "b_w_in", "b_qnorm", "kv_norm", "w_kv", "b_knorm",
             "mem_norm", "w_mem_kv", "mem_qnorm", "mem_knorm", "w_out", "w_gate_up", "w_down")
    w = dict(zip(names, (norm_mix, norm_ffn, a_w_in, a_lb_logits, a_onorm, b_w_in, b_qnorm, kv_norm, w_kv, b_knorm,
                         mem_norm, w_mem_kv, mem_qnorm, mem_knorm, w_out, w_gate_up, w_down)))
    m = dict(zip(names, (m_norm_mix, m_norm_ffn, m_a_w_in, m_a_lb_logits, m_a_onorm, m_b_w_in, m_b_qnorm, m_kv_norm, m_w_kv,
                         m_b_knorm, m_mem_norm, m_w_mem_kv, m_mem_qnorm, m_mem_knorm, m_w_out, m_w_gate_up, m_w_down)))
    v = dict(zip(names, (v_norm_mix, v_norm_ffn, v_a_w_in, v_a_lb_logits, v_a_onorm, v_b_w_in, v_b_qnorm, v_kv_norm, v_w_kv,
                         v_b_knorm, v_mem_norm, v_w_mem_kv, v_mem_qnorm, v_mem_knorm, v_w_out, v_w_gate_up, v_w_down)))

    first = ["a_w_in", "w_mem_kv0"]
    gathered = _all_gather([_wire_block(w, u) for u in first] + [_pack_flat([a_lb_logits, a_onorm], 8, LANES, F32)],
                           name="gather_first")
    full = {u: g.reshape(-1, g.shape[-1]) for u, g in zip(first, gathered)}
    small_in = gathered[-1].reshape(N_DEV, -1)
    P = {n: w[n] for n in SMALL_REPLICATED}
    P["a_lb_logits"] = small_in[:, :192].reshape(N_DEV, 2, 96).transpose(1, 0, 2).reshape(2, A_WIDTH)
    P["a_onorm"] = small_in[:, 192:288].reshape(1, A_WIDTH)
    later = [["w_out0", "w_gate_up0"], ["w_down0", "w_kv"], ["b_w_in", "w_mem_kv1"], ["w_out1", "w_gate_up1", "w_down1"]]
    pending = {}
    token = gathered[-1]
    for i, group in enumerate(later):
        handle = _split_start([_wire_block(w, u) for u in group], False, token, name=f"gather{i}_start")
        token = handle["token"]
        for u in group:
            pending[u] = (i, group, handle)

    def get_w(unit, after):
        if unit not in full:
            i, group, handle = pending[unit]
            for u, land in zip(group, _split_wait(handle, after, name=f"gather{i}_wait")):
                full[u] = land.reshape(-1, land.shape[-1])
        return full[unit]

    sent = []

    def put_g(group):
        units = list(group)
        handle = _split_start([group[u].reshape(N_DEV, -1, group[u].shape[-1]) for u in units], True, None,
                              name=f"scatter{len(sent)}_start")
        sent.append((units, handle))
        return handle["token"]

    sq, gx, gP = _local_step(x[0], mem[0], loss_target[0], get_w, P, put_g, first_dep=token)
    loss = lax.psum(0.5 * jnp.sum(sq) / D_MODEL, ("x", "y", "c"))

    received = {}
    for i, (units, handle) in enumerate(sent):
        received.update(zip(units, _split_wait(handle, gx, name=f"scatter{i}_wait")))
    out = {"grad": {}, "delta": {}, "new_m": {}, "new_v": {}}
    for n in BIG:
        shape = w[n].shape
        as3 = lambda a: a.reshape((-1,) + shape[-2:])
        mine = [u for u, (wn, _, _) in UNITS.items() if wn == n]
        col = UNITS[mine[0]][2]
        flip = (lambda a: jnp.swapaxes(a, 1, 2)) if n in TRANSPOSED_UPDATE else (lambda a: a)
        res = _reduce_adamw([received[u] for u in mine], flip(as3(w[n])), flip(as3(m[n])), flip(as3(v[n])),
                            col=col and n not in TRANSPOSED_UPDATE, tr=ADAMW_ROW_TILE[n], name=f"adamw_{n}")
        for kind, r in zip(("grad", "delta", "new_m", "new_v"), res):
            out[kind][n] = flip(r).reshape(shape)

    full_shapes = [(2, A_WIDTH) if n == "a_lb_logits" else (1, A_WIDTH) if n == "a_onorm" else w[n].shape for n in SMALL_ORDER]
    n_small = sum(_prod(s) for s in full_shapes)
    rows_small = -(-n_small // (8 * LANES)) * 8
    g_all, = _all_gather([_pack_flat([gP[n] for n in SMALL_ORDER], rows_small, LANES, F32)], name="gather_small_grads")
    g_small = dict(zip(SMALL_ORDER, _unpack_flat(_sum_sources(g_all, tr=rows_small, name="sum_small_grads"), full_shapes)))
    me = 4 * lax.axis_index("x") + 2 * lax.axis_index("y") + lax.axis_index("c")
    for n in SMALL_SHARDED:
        g_small[n] = lax.dynamic_slice_in_dim(g_small[n], me * 96, 96, axis=1)
    shapes = [w[n].shape for n in SMALL_ORDER]
    rows_upd = -(-sum(_prod(s) for s in shapes) // (8 * LANES)) * 8
    pk = lambda d: _pack_flat([d[n] for n in SMALL_ORDER], rows_upd, LANES, F32)
    res = _adamw(pk(g_small)[None], pk(w)[None], pk(m)[None], pk(v)[None], tr=rows_upd, name="adamw_small")
    out["grad"].update(g_small)
    for kind, packed in zip(("delta", "new_m", "new_v"), res):
        out[kind].update(zip(SMALL_ORDER, _unpack_flat(packed[0], shapes)))

    return (loss, gx[None], *[out["grad"][n] for n in names], *[out["delta"][n] for n in names],
            *[out["new_m"][n] for n in names], *[out["new_v"][n] for n in names])
```

```python
import functools

import jax
import jax.numpy as jnp
from jax import lax
from jax.experimental import pallas as pl
from jax.experimental.pallas import tpu as pltpu

F32 = jnp.float32
BF16 = jnp.bfloat16

N_DEV = 8
D_MODEL = 1024
HEAD_DIM = 128
A_HEADS = 6
A_WIDTH = A_HEADS * HEAD_DIM
CHUNK = 64
B_HEADS = 6
B_WIDTH = B_HEADS * HEAD_DIM
DILATIONS = (1, 4, 16)
SPAN = 128
N_GROUPS = 3
ROPE_THETA = 10000.0
MEM_TOKENS = 256
MEM_HEADS = 4
MEM_HEAD_DIM = 64
MEM_WIDTH = MEM_HEADS * MEM_HEAD_DIM
FFN_HIDDEN = 2816
EPS = 1e-6

ADAM_LR = 0.001
ADAM_B1 = 0.9
ADAM_B2 = 0.999
ADAM_EPS = 1e-08
ADAM_WD = 0.01
ADAM_STEP = 10

V7X_VMEM_LIMIT_BYTES = 56 * 1024 * 1024

NT_DIMS = (((1,), (1,)), ((), ()))
TN_DIMS = (((0,), (0,)), ((), ()))


def _cp(*sem):
    return pltpu.CompilerParams(dimension_semantics=sem, vmem_limit_bytes=V7X_VMEM_LIMIT_BYTES)


def _dot(a, b):
    return jnp.dot(a.astype(BF16), b.astype(BF16), preferred_element_type=F32)


def _dot_nt(a, b):
    return lax.dot_general(a.astype(BF16), b.astype(BF16), NT_DIMS, preferred_element_type=F32)


def _dot_tn(a, b):
    return lax.dot_general(a.astype(BF16), b.astype(BF16), TN_DIMS, preferred_element_type=F32)


def _dot3(m01, x):
    hi = x.astype(BF16)
    r1 = x - hi.astype(F32)
    mid = r1.astype(BF16)
    lo = (r1 - mid.astype(F32)).astype(BF16)
    d = functools.partial(jnp.dot, preferred_element_type=F32)
    return d(m01, hi) + d(m01, mid) + d(m01, lo)


def _sigmoid(x):
    return 1.0 / (1.0 + jnp.exp(-x))


def _full(shape):
    return pl.BlockSpec(shape, lambda *_: (0,) * len(shape))


def _dep(body, n_in, dep):
    if dep is None:
        return body, [], []

    def with_dep(*refs):
        return body(*refs[:n_in], *refs[n_in + 1:])

    return with_dep, [pl.BlockSpec(memory_space=pl.ANY)], [dep]


def _rms_matmul(x, g, w, *, tt, tn, wt, name, out_dtype=F32, dep=None):
    T, K = x.shape
    N = w.shape[0] if wt else w.shape[1]

    def kernel_body(x_ref, g_ref, w_ref, y_ref, xn_ref):
        xf = x_ref[...]
        r = lax.rsqrt(jnp.mean(xf * xf, axis=-1, keepdims=True) + EPS)
        xn = (xf * r * g_ref[...]).astype(BF16)
        xn_ref[...] = xn
        for j in range(N // tn):
            cols = slice(j * tn, (j + 1) * tn)
            y = _dot_nt(xn, w_ref[cols, :]) if wt else _dot(xn, w_ref[:, cols])
            y_ref[:, cols] = y.astype(out_dtype)

    body, dep_specs, dep_args = _dep(kernel_body, 3, dep)
    return pl.pallas_call(
        body, grid=(T // tt,),
        in_specs=[pl.BlockSpec((tt, K), lambda i: (i, 0)), _full((1, K)), _full(w.shape)] + dep_specs,
        out_specs=[pl.BlockSpec((tt, N), lambda i: (i, 0)), pl.BlockSpec((tt, K), lambda i: (i, 0))],
        out_shape=[jax.ShapeDtypeStruct((T, N), out_dtype), jax.ShapeDtypeStruct((T, K), BF16)],
        compiler_params=_cp("parallel"), name=name)(x, g, w, *dep_args)


def _mm_res(res, a1, a2, w, *, tt, name):
    T, K1 = a1.shape
    K2 = a2.shape[1]
    N = w.shape[1]

    def body(r_ref, a1_ref, a2_ref, w_ref, o_ref):
        o_ref[...] = r_ref[...] + _dot(a1_ref[...], w_ref[:K1, :]) + _dot(a2_ref[...], w_ref[K1:, :])

    return pl.pallas_call(
        body, grid=(T // tt,),
        in_specs=[pl.BlockSpec((tt, N), lambda i: (i, 0)), pl.BlockSpec((tt, K1), lambda i: (i, 0)),
                  pl.BlockSpec((tt, K2), lambda i: (i, 0)), _full((K1 + K2, N))],
        out_specs=pl.BlockSpec((tt, N), lambda i: (i, 0)),
        out_shape=jax.ShapeDtypeStruct((T, N), F32),
        compiler_params=_cp("parallel"), name=name)(res, a1, a2, w)


def _swiglu_down(h, gu, wd, *, tt, name):
    T, D = h.shape
    Fh = wd.shape[0]

    def body(h_ref, gt_ref, up_ref, w_ref, o_ref):
        gt = gt_ref[...].astype(F32)
        act = gt * _sigmoid(gt) * up_ref[...].astype(F32)
        o_ref[...] = h_ref[...] + _dot(act, w_ref[...])

    return pl.pallas_call(
        body, grid=(T // tt,),
        in_specs=[pl.BlockSpec((tt, D), lambda i: (i, 0)), pl.BlockSpec((tt, Fh), lambda i: (i, 0)),
                  pl.BlockSpec((tt, Fh), lambda i: (i, 1)), _full((Fh, D))],
        out_specs=pl.BlockSpec((tt, D), lambda i: (i, 0)),
        out_shape=jax.ShapeDtypeStruct((T, D), F32),
        compiler_params=_cp("parallel"), name=name)(h, gu, gu, wd)


def _swiglu_bwd(dh, gu, wd, *, tt, name):
    T, D = dh.shape
    Fh = wd.shape[0]
    last = T // tt - 1

    def body(dh_ref, gt_ref, up_ref, w_ref, dgu_ref, gw_ref, acc):
        @pl.when(pl.program_id(0) == 0)
        def _():
            acc[...] = jnp.zeros_like(acc)

        gt = gt_ref[...].astype(F32)
        up = up_ref[...].astype(F32)
        s = _sigmoid(gt)
        silu = gt * s
        dh16 = dh_ref[...].astype(BF16)
        dact = _dot_nt(dh16, w_ref[...])
        acc[...] += _dot_tn((silu * up).astype(BF16), dh16)
        dgu_ref[:, :Fh] = (dact * up * (s * (1.0 + gt * (1.0 - s)))).astype(BF16)
        dgu_ref[:, Fh:] = (dact * silu).astype(BF16)

        @pl.when(pl.program_id(0) == last)
        def _():
            gw_ref[...] = acc[...].astype(BF16)

    return pl.pallas_call(
        body, grid=(T // tt,),
        in_specs=[pl.BlockSpec((tt, D), lambda i: (i, 0)), pl.BlockSpec((tt, Fh), lambda i: (i, 0)),
                  pl.BlockSpec((tt, Fh), lambda i: (i, 1)), _full((Fh, D))],
        out_specs=[pl.BlockSpec((tt, 2 * Fh), lambda i: (i, 0)), _full((Fh, D))],
        out_shape=[jax.ShapeDtypeStruct((T, 2 * Fh), BF16), jax.ShapeDtypeStruct((Fh, D), BF16)],
        scratch_shapes=[pltpu.VMEM((Fh, D), F32)],
        compiler_params=_cp("arbitrary"), name=name)(dh, gu, gu, wd)


def _out_proj_bwd(dy, a1, a2, w, *, tt, name):
    T, N = dy.shape
    K1, K2 = a1.shape[1], a2.shape[1]
    K = K1 + K2
    last = T // tt - 1

    def body(dy_ref, a1_ref, a2_ref, w_ref, da_ref, gw_ref, acc):
        @pl.when(pl.program_id(0) == 0)
        def _():
            acc[...] = jnp.zeros_like(acc)

        dy16 = dy_ref[...].astype(BF16)
        da_ref[...] = _dot_nt(dy16, w_ref[...])
        acc[:K1, :] += _dot_tn(a1_ref[...], dy16)
        acc[K1:, :] += _dot_tn(a2_ref[...], dy16)

        @pl.when(pl.program_id(0) == last)
        def _():
            gw_ref[...] = acc[...].astype(BF16)

    return pl.pallas_call(
        body, grid=(T // tt,),
        in_specs=[pl.BlockSpec((tt, N), lambda i: (i, 0)), pl.BlockSpec((tt, K1), lambda i: (i, 0)),
                  pl.BlockSpec((tt, K2), lambda i: (i, 0)), _full((K, N))],
        out_specs=[pl.BlockSpec((tt, K), lambda i: (i, 0)), _full((K, N))],
        out_shape=[jax.ShapeDtypeStruct((T, K), F32), jax.ShapeDtypeStruct((K, N), BF16)],
        scratch_shapes=[pltpu.VMEM((K, N), F32)],
        compiler_params=_cp("arbitrary"), name=name)(dy, a1, a2, w)


def _mm_tn(a, b, *, tt, tka, name):
    T, Ka = a.shape
    N = b.shape[1]
    last = T // tt - 1

    def body(a_ref, b_ref, o_ref, acc):
        @pl.when(pl.program_id(1) == 0)
        def _():
            acc[...] = jnp.zeros_like(acc)

        acc[...] += _dot_tn(a_ref[...], b_ref[...])

        @pl.when(pl.program_id(1) == last)
        def _():
            o_ref[...] = acc[...].astype(BF16)

    return pl.pallas_call(
        body, grid=(Ka // tka, T // tt),
        in_specs=[pl.BlockSpec((tt, tka), lambda j, t: (t, j)), pl.BlockSpec((tt, N), lambda j, t: (t, 0))],
        out_specs=pl.BlockSpec((tka, N), lambda j, t: (j, 0)),
        out_shape=jax.ShapeDtypeStruct((Ka, N), BF16),
        scratch_shapes=[pltpu.VMEM((tka, N), F32)],
        compiler_params=_cp("parallel", "arbitrary"), name=name)(a, b)


def _rms_bwd_dx(x, g, w, dy, dres, *, tt, wt, name, dep=None):
    T, K = x.shape
    N = w.shape[0] if wt else w.shape[1]

    def kernel_body(x_ref, g_ref, w_ref, dy_ref, dres_ref, dx_ref, dg_ref):
        @pl.when(pl.program_id(0) == 0)
        def _():
            dg_ref[...] = jnp.zeros_like(dg_ref)

        dxn = (_dot if wt else _dot_nt)(dy_ref[...], w_ref[...])
        xf = x_ref[...]
        r = lax.rsqrt(jnp.mean(xf * xf, axis=-1, keepdims=True) + EPS)
        xhat = xf * r
        dg_ref[...] += jnp.sum(dxn * xhat, axis=0, keepdims=True)
        dxhat = dxn * g_ref[...]
        dx_ref[...] = dres_ref[...] + r * (dxhat - xhat * jnp.mean(dxhat * xhat, axis=-1, keepdims=True))

    body, dep_specs, dep_args = _dep(kernel_body, 5, dep)
    return pl.pallas_call(
        body, grid=(T // tt,),
        in_specs=[pl.BlockSpec((tt, K), lambda i: (i, 0)), _full((1, K)), _full(w.shape),
                  pl.BlockSpec((tt, N), lambda i: (i, 0)), pl.BlockSpec((tt, K), lambda i: (i, 0))] + dep_specs,
        out_specs=[pl.BlockSpec((tt, K), lambda i: (i, 0)), _full((1, K))],
        out_shape=[jax.ShapeDtypeStruct((T, K), F32), jax.ShapeDtypeStruct((1, K), F32)],
        compiler_params=_cp("arbitrary"), name=name)(x, g, w, dy, dres, *dep_args)


def _loss_kernel(y, tgt, *, tt, name):
    T, D = y.shape

    def body(y_ref, t_ref, dy_ref, acc_ref):
        @pl.when(pl.program_id(0) == 0)
        def _():
            acc_ref[...] = jnp.zeros_like(acc_ref)

        e = y_ref[...] - t_ref[...]
        dy_ref[...] = e * (1.0 / D)
        acc_ref[...] += jnp.sum(e * e, axis=0, keepdims=True)

    return pl.pallas_call(
        body, grid=(T // tt,),
        in_specs=[pl.BlockSpec((tt, D), lambda i: (i, 0)), pl.BlockSpec((tt, D), lambda i: (i, 0))],
        out_specs=[pl.BlockSpec((tt, D), lambda i: (i, 0)), _full((1, D))],
        out_shape=[jax.ShapeDtypeStruct((T, D), F32), jax.ShapeDtypeStruct((1, D), F32)],
        compiler_params=_cp("arbitrary"), name=name)(y, tgt)


HGRN_TB = 512
HGRN_NCH = HGRN_TB // CHUNK
HGRN_HPB = 6


def _hgrn_chunk_fwd(q, z, lbv, tril01):
    sig = _sigmoid(z)
    f = lbv + (1.0 - lbv) * sig
    kk = 1.0 - f
    b = _dot3(tril01, jnp.log(f))
    bend = b[CHUNK - 1:CHUNK, :]
    sq = _sigmoid(q)
    eb = jnp.exp(b)
    emb = jnp.exp(-b)
    eo = jnp.exp(bend - b)
    dec = jnp.exp(bend)
    return sig, f, kk, sq, eb, emb, eo, dec


def _hgrn2_fwd(proj, lb, *, name):
    T = proj.shape[0]
    nT = T // HGRN_TB
    nC = T // CHUNK

    def body(q_ref, z_ref, v_ref, lb_ref, o_ref, st_ref, state):
        @pl.when(pl.program_id(1) == 0)
        def _():
            state[...] = jnp.zeros_like(state)

        row = lax.broadcasted_iota(jnp.int32, (CHUNK, CHUNK), 0)
        col = lax.broadcasted_iota(jnp.int32, (CHUNK, CHUNK), 1)
        causal = row >= col
        tril01 = causal.astype(BF16)

        def chunk(c, carry):
            rows = pl.ds(pl.multiple_of(c * CHUNK, CHUNK), CHUNK)
            for hh in range(HGRN_HPB):
                sl = slice(hh * HEAD_DIM, (hh + 1) * HEAD_DIM)
                q = q_ref[rows, sl]
                v = v_ref[rows, sl].astype(BF16)
                sig, f, kk, sq, eb, emb, eo, dec = _hgrn_chunk_fwd(q, z_ref[rows, sl], lb_ref[:, sl], tril01)
                qi = (q * sq * eb).astype(BF16)
                ki = (kk * emb).astype(BF16)
                ko = (kk * eo).astype(BF16)
                st = state[hh]
                att = jnp.where(causal, _dot_nt(qi, ki), 0.0)
                o_ref[rows, sl] = _dot(att, v) + _dot_nt(qi, st)
                st_ref[c, hh] = st
                state[hh] = st * dec + _dot_tn(v, ko)
            return carry

        lax.fori_loop(0, HGRN_NCH, chunk, 0)

    W = HGRN_HPB * HEAD_DIM
    nG = A_HEADS // HGRN_HPB
    hb = lambda off: pl.BlockSpec((HGRN_TB, W), lambda h, i: (i, off + h))
    return pl.pallas_call(
        body, grid=(nG, nT),
        in_specs=[hb(0), hb(nG), hb(2 * nG), pl.BlockSpec((1, W), lambda h, i: (0, h))],
        out_specs=[hb(0), pl.BlockSpec((HGRN_NCH, HGRN_HPB, HEAD_DIM, HEAD_DIM), lambda h, i: (i, h, 0, 0))],
        out_shape=[jax.ShapeDtypeStruct((T, A_WIDTH), F32), jax.ShapeDtypeStruct((nC, A_HEADS, HEAD_DIM, HEAD_DIM), F32)],
        scratch_shapes=[pltpu.VMEM((HGRN_HPB, HEAD_DIM, HEAD_DIM), F32)],
        compiler_params=_cp("parallel", "arbitrary"), name=name)(proj, proj, proj, lb)


def _hgrn2_bwd(proj, lb, st_all, do, *, name):
    T = proj.shape[0]
    nT = T // HGRN_TB

    def body(q_ref, z_ref, v_ref, lb_ref, st_ref, do_ref, dq_ref, dz_ref, dv_ref, dlb_ref, dstate):
        @pl.when(pl.program_id(1) == 0)
        def _():
            dstate[...] = jnp.zeros_like(dstate)
            dlb_ref[...] = jnp.zeros_like(dlb_ref)

        row = lax.broadcasted_iota(jnp.int32, (CHUNK, CHUNK), 0)
        col = lax.broadcasted_iota(jnp.int32, (CHUNK, CHUNK), 1)
        causal = row >= col
        tril01 = causal.astype(BF16)
        triu01 = (row <= col).astype(BF16)

        def chunk(cc, carry):
            c = HGRN_NCH - 1 - cc
            rows = pl.ds(pl.multiple_of(c * CHUNK, CHUNK), CHUNK)
            for hh in range(HGRN_HPB):
                sl = slice(hh * HEAD_DIM, (hh + 1) * HEAD_DIM)
                lbv = lb_ref[:, sl]
                q = q_ref[rows, sl]
                v = v_ref[rows, sl].astype(BF16)
                sig, f, kk, sq, eb, emb, eo, dec = _hgrn_chunk_fwd(q, z_ref[rows, sl], lbv, tril01)
                qi32 = q * sq * eb
                ki32 = kk * emb
                ko32 = kk * eo
                qi, ki, ko = qi32.astype(BF16), ki32.astype(BF16), ko32.astype(BF16)
                att = jnp.where(causal, _dot_nt(qi, ki), 0.0).astype(BF16)
                dout = do_ref[rows, sl].astype(BF16)
                st = st_ref[c, hh]
                dst = dstate[hh]
                dst16 = dst.astype(BF16)
                datt = jnp.where(causal, _dot_nt(dout, v), 0.0).astype(BF16)
                dqi = _dot(datt, ki) + _dot(dout, st)
                dki = _dot_tn(datt, qi)
                dv_ref[rows, sl] = (_dot_tn(att, dout) + _dot_nt(ko, dst16)).astype(BF16)
                dko = _dot(v, dst16)
                ddec = jnp.sum(dst * st, axis=0, keepdims=True)
                dstate[hh] = dst * dec + _dot_tn(dout, qi)
                dkk = dki * emb + dko * eo
                db = dqi * qi32 - dki * ki32 - dko * ko32
                dbend = jnp.sum(dko * ko32, axis=0, keepdims=True) + ddec * dec
                dlogf = _dot3(triu01, db) + dbend
                df = dlogf / f - dkk
                dz_ref[rows, sl] = (df * (1.0 - lbv) * sig * (1.0 - sig)).astype(BF16)
                dlb_ref[:, sl] += jnp.sum(df * (1.0 - sig), axis=0, keepdims=True)
                dq_ref[rows, sl] = (dqi * eb * (sq * (1.0 + q * (1.0 - sq)))).astype(BF16)
            return carry

        lax.fori_loop(0, HGRN_NCH, chunk, 0)

    W = HGRN_HPB * HEAD_DIM
    nG = A_HEADS // HGRN_HPB
    hb = lambda off: pl.BlockSpec((HGRN_TB, W), lambda h, i: (nT - 1 - i, off + h))
    hlb = pl.BlockSpec((1, W), lambda h, i: (0, h))
    o16 = jax.ShapeDtypeStruct((T, A_WIDTH), BF16)
    return pl.pallas_call(
        body, grid=(nG, nT),
        in_specs=[hb(0), hb(nG), hb(2 * nG), hlb,
                  pl.BlockSpec((HGRN_NCH, HGRN_HPB, HEAD_DIM, HEAD_DIM), lambda h, i: (nT - 1 - i, h, 0, 0)), hb(0)],
        out_specs=[hb(0), hb(0), hb(0), hlb],
        out_shape=[o16, o16, o16, jax.ShapeDtypeStruct((1, A_WIDTH), F32)],
        scratch_shapes=[pltpu.VMEM((HGRN_HPB, HEAD_DIM, HEAD_DIM), F32)],
        compiler_params=_cp("parallel", "arbitrary"), name=name)(proj, proj, proj, lb, st_all, do)


def _head_rms(x):
    r = lax.rsqrt(jnp.mean(x * x, axis=-1, keepdims=True) + EPS)
    return x * r, r


def _head_rms_bwd(dxhat, xhat, r):
    return r * (dxhat - xhat * jnp.mean(dxhat * xhat, axis=-1, keepdims=True))


def _a_post_fwd(o, proj, onorm, *, tt, name):
    T = o.shape[0]

    def body(o_ref, g_ref, w_ref, y_ref):
        for h in range(A_HEADS):
            sl = slice(h * HEAD_DIM, (h + 1) * HEAD_DIM)
            xhat, _ = _head_rms(o_ref[:, sl])
            g = g_ref[:, sl]
            y_ref[:, sl] = xhat * w_ref[:, sl] * (g * _sigmoid(g))

    blk = lambda c: pl.BlockSpec((tt, A_WIDTH), lambda i: (i, c))
    return pl.pallas_call(
        body, grid=(T // tt,), in_specs=[blk(0), blk(3), _full((1, A_WIDTH))], out_specs=blk(0),
        out_shape=jax.ShapeDtypeStruct((T, A_WIDTH), F32),
        compiler_params=_cp("parallel"), name=name)(o, proj, onorm)


def _a_post_bwd(o, proj, onorm, dmix, *, tt, name, dep=None):
    T = o.shape[0]

    def kernel_body(o_ref, g_ref, w_ref, dy_ref, do_ref, dg_ref, dw_ref):
        @pl.when(pl.program_id(0) == 0)
        def _():
            dw_ref[...] = jnp.zeros_like(dw_ref)

        for h in range(A_HEADS):
            sl = slice(h * HEAD_DIM, (h + 1) * HEAD_DIM)
            xhat, r = _head_rms(o_ref[:, sl])
            g = g_ref[:, sl]
            s = _sigmoid(g)
            dy = dy_ref[:, sl]
            w = w_ref[:, sl]
            dg_ref[:, sl] = (dy * xhat * w * (s * (1.0 + g * (1.0 - s)))).astype(BF16)
            dyn = dy * (g * s)
            dw_ref[:, sl] += jnp.sum(dyn * xhat, axis=0, keepdims=True)
            do_ref[:, sl] = _head_rms_bwd(dyn * w, xhat, r)

    blk = lambda c: pl.BlockSpec((tt, A_WIDTH), lambda i: (i, c))
    body, dep_specs, dep_args = _dep(kernel_body, 4, dep)
    return pl.pallas_call(
        body, grid=(T // tt,), in_specs=[blk(0), blk(3), _full((1, A_WIDTH)), blk(0)] + dep_specs,
        out_specs=[blk(0), blk(0), _full((1, A_WIDTH))],
        out_shape=[jax.ShapeDtypeStruct((T, A_WIDTH), F32), jax.ShapeDtypeStruct((T, A_WIDTH), BF16),
                   jax.ShapeDtypeStruct((1, A_WIDTH), F32)],
        compiler_params=_cp("arbitrary"), name=name)(o, proj, onorm, dmix, *dep_args)


def _mem_head_masks(n):
    lane = lax.broadcasted_iota(jnp.int32, (n, MEM_WIDTH), 1)
    return [(lane >= m * MEM_HEAD_DIM) & (lane < (m + 1) * MEM_HEAD_DIM) for m in range(MEM_HEADS)]


def _mem_head_rms(x, masks):
    x2 = x * x
    r = jnp.zeros_like(x)
    for mk in masks:
        ms = jnp.sum(jnp.where(mk, x2, 0.0), axis=-1, keepdims=True) * (1.0 / MEM_HEAD_DIM)
        r = jnp.where(mk, lax.rsqrt(ms + EPS), r)
    return x * r, r


def _mem_head_rms_bwd(dxhat, xhat, r, masks):
    t = dxhat * xhat
    m = jnp.zeros_like(t)
    for mk in masks:
        m = jnp.where(mk, jnp.sum(jnp.where(mk, t, 0.0), axis=-1, keepdims=True) * (1.0 / MEM_HEAD_DIM), m)
    return r * (dxhat - xhat * m)


MEM_SCALE = MEM_HEAD_DIM ** -0.5


def _mem_attn_fwd(proj, qcol, mkv, qn_w, kn_w, *, tt, name):
    T = proj.shape[0]

    def body(q_ref, k_ref, v_ref, qw_ref, kw_ref, o_ref):
        qmasks = _mem_head_masks(tt)
        kmasks = _mem_head_masks(MEM_TOKENS)
        qhat, _ = _mem_head_rms(q_ref[...], qmasks)
        qn = qhat * qw_ref[...]
        khat, _ = _mem_head_rms(k_ref[...], kmasks)
        kn = (khat * kw_ref[...]).astype(BF16)
        v = v_ref[...].astype(BF16)
        out = jnp.zeros((tt, MEM_WIDTH), F32)
        for m in range(MEM_HEADS):
            s = _dot_nt(jnp.where(qmasks[m], qn, 0.0), kn) * MEM_SCALE
            s = s - jnp.max(s, axis=-1, keepdims=True)
            p = jnp.exp(s)
            p = p / jnp.sum(p, axis=-1, keepdims=True)
            out = jnp.where(qmasks[m], _dot(p, v), out)
        o_ref[...] = out

    return pl.pallas_call(
        body, grid=(T // tt,),
        in_specs=[pl.BlockSpec((tt, MEM_WIDTH), lambda i: (i, qcol)), pl.BlockSpec((MEM_TOKENS, MEM_WIDTH), lambda i: (0, 0)),
                  pl.BlockSpec((MEM_TOKENS, MEM_WIDTH), lambda i: (0, 1)), _full((1, MEM_WIDTH)), _full((1, MEM_WIDTH))],
        out_specs=pl.BlockSpec((tt, MEM_WIDTH), lambda i: (i, 0)),
        out_shape=jax.ShapeDtypeStruct((T, MEM_WIDTH), F32),
        compiler_params=_cp("parallel"), name=name)(proj, mkv, mkv, qn_w, kn_w)


def _mem_attn_bwd(proj, qcol, mkv, qn_w, kn_w, dmix, *, tt, name):
    T = proj.shape[0]
    nsteps = T // tt
    ocol = (dmix.shape[1] - MEM_WIDTH) // MEM_WIDTH

    def body(q_ref, k_ref, v_ref, qw_ref, kw_ref, do_ref, dq_ref, dkv_ref, dqw_ref, dkw_ref, dk_acc, dv_acc):
        step = pl.program_id(0)

        @pl.when(step == 0)
        def _():
            dk_acc[...] = jnp.zeros_like(dk_acc)
            dv_acc[...] = jnp.zeros_like(dv_acc)
            dqw_ref[...] = jnp.zeros_like(dqw_ref)

        qmasks = _mem_head_masks(tt)
        kmasks = _mem_head_masks(MEM_TOKENS)
        qhat, qr = _mem_head_rms(q_ref[...], qmasks)
        qn = qhat * qw_ref[...]
        khat, kr = _mem_head_rms(k_ref[...], kmasks)
        kn = (khat * kw_ref[...]).astype(BF16)
        v = v_ref[...].astype(BF16)
        dout = do_ref[...]
        dqn = jnp.zeros((tt, MEM_WIDTH), F32)
        dkn = jnp.zeros((MEM_TOKENS, MEM_WIDTH), F32)
        dvv = jnp.zeros((MEM_TOKENS, MEM_WIDTH), F32)
        for m in range(MEM_HEADS):
            qm = jnp.where(qmasks[m], qn, 0.0).astype(BF16)
            s = _dot_nt(qm, kn) * MEM_SCALE
            s = s - jnp.max(s, axis=-1, keepdims=True)
            p = jnp.exp(s)
            p = p / jnp.sum(p, axis=-1, keepdims=True)
            dom = jnp.where(qmasks[m], dout, 0.0).astype(BF16)
            dp = _dot_nt(dom, v)
            ds = (p * (dp - jnp.sum(p * dp, axis=-1, keepdims=True)) * MEM_SCALE).astype(BF16)
            dqn = jnp.where(qmasks[m], _dot(ds, kn), dqn)
            dkn = jnp.where(kmasks[m], _dot_tn(ds, qm), dkn)
            dvv = jnp.where(kmasks[m], _dot_tn(p, dom), dvv)
        dqw_ref[...] += jnp.sum(dqn * qhat, axis=0, keepdims=True)
        dq_ref[...] = _mem_head_rms_bwd(dqn * qw_ref[...], qhat, qr, qmasks).astype(BF16)
        dk_acc[...] += dkn
        dv_acc[...] += dvv

        @pl.when(step == nsteps - 1)
        def _():
            dk = dk_acc[...]
            dkw_ref[...] = jnp.sum(dk * khat, axis=0, keepdims=True)
            dkv_ref[:, :MEM_WIDTH] = _mem_head_rms_bwd(dk * kw_ref[...], khat, kr, kmasks)
            dkv_ref[:, MEM_WIDTH:] = dv_acc[...]

    return pl.pallas_call(
        body, grid=(nsteps,),
        in_specs=[pl.BlockSpec((tt, MEM_WIDTH), lambda i: (i, qcol)), pl.BlockSpec((MEM_TOKENS, MEM_WIDTH), lambda i: (0, 0)),
                  pl.BlockSpec((MEM_TOKENS, MEM_WIDTH), lambda i: (0, 1)), _full((1, MEM_WIDTH)), _full((1, MEM_WIDTH)),
                  pl.BlockSpec((tt, MEM_WIDTH), lambda i: (i, ocol))],
        out_specs=[pl.BlockSpec((tt, MEM_WIDTH), lambda i: (i, 0)), _full((MEM_TOKENS, 2 * MEM_WIDTH)),
                   _full((1, MEM_WIDTH)), _full((1, MEM_WIDTH))],
        out_shape=[jax.ShapeDtypeStruct((T, MEM_WIDTH), BF16), jax.ShapeDtypeStruct((MEM_TOKENS, 2 * MEM_WIDTH), F32),
                   jax.ShapeDtypeStruct((1, MEM_WIDTH), F32), jax.ShapeDtypeStruct((1, MEM_WIDTH), F32)],
        scratch_shapes=[pltpu.VMEM((MEM_TOKENS, MEM_WIDTH), F32), pltpu.VMEM((MEM_TOKENS, MEM_WIDTH), F32)],
        compiler_params=_cp("arbitrary"), name=name)(proj, mkv, mkv, qn_w, kn_w, dmix)


HALF = HEAD_DIM // 2
ATT_SCALE = HEAD_DIM ** -0.5
NEG = -1e30


def _rope_tables(T):
    inv = ROPE_THETA ** (-jnp.arange(HALF, dtype=F32) / HALF)
    ang = jnp.arange(T, dtype=F32)[:, None] * inv[None, :]
    cos, sin = jnp.cos(ang), jnp.sin(ang)
    return jnp.concatenate([cos, cos], axis=-1), jnp.concatenate([-sin, sin], axis=-1)


def _rope(x, cosf, sinsg):
    return x * cosf + pltpu.roll(x, HALF, 1) * sinsg


def _rope_bwd(dy, cosf, sinsg):
    return dy * cosf + pltpu.roll(dy * sinsg, HALF, 1)


def _headnorm_rope_fwd(x, w_heads, cosf, sinsg, *, col0, n_heads, tt, name):
    T = x.shape[0]
    W = n_heads * HEAD_DIM

    def body(x_ref, w_ref, c_ref, s_ref, y_ref):
        c, s = c_ref[...], s_ref[...]
        for h in range(n_heads):
            sl = slice(h * HEAD_DIM, (h + 1) * HEAD_DIM)
            xhat, _ = _head_rms(x_ref[:, sl])
            y_ref[:, sl] = _rope(xhat * w_ref[:, sl], c, s)

    tbl = pl.BlockSpec((tt, HEAD_DIM), lambda i: (i, 0))
    return pl.pallas_call(
        body, grid=(T // tt,),
        in_specs=[pl.BlockSpec((tt, W), lambda i: (i, col0)), _full((1, W)), tbl, tbl],
        out_specs=pl.BlockSpec((tt, W), lambda i: (i, 0)),
        out_shape=jax.ShapeDtypeStruct((T, W), F32),
        compiler_params=_cp("parallel"), name=name)(x, w_heads, cosf, sinsg)


def _q_prep_bwd(proj, w_heads, cosf, sinsg, dqs, *, tt, name):
    T = proj.shape[0]
    W = N_GROUPS * B_WIDTH

    def body(x_ref, w_ref, c_ref, s_ref, d0, d1, d2, dx_ref, dw_ref):
        @pl.when(pl.program_id(0) == 0)
        def _():
            dw_ref[...] = jnp.zeros_like(dw_ref)

        c, s = c_ref[...], s_ref[...]
        for gi, d_ref in enumerate((d0, d1, d2)):
            for h in range(B_HEADS):
                sl = slice((gi * B_HEADS + h) * HEAD_DIM, (gi * B_HEADS + h + 1) * HEAD_DIM)
                xhat, r = _head_rms(x_ref[:, sl])
                dyn = _rope_bwd(d_ref[:, h * HEAD_DIM:(h + 1) * HEAD_DIM], c, s)
                dw_ref[:, sl] += jnp.sum(dyn * xhat, axis=0, keepdims=True)
                dx_ref[:, sl] = _head_rms_bwd(dyn * w_ref[:, sl], xhat, r).astype(BF16)

    tbl = pl.BlockSpec((tt, HEAD_DIM), lambda i: (i, 0))
    dyb = pl.BlockSpec((tt, B_WIDTH), lambda i: (i, 0))
    return pl.pallas_call(
        body, grid=(T // tt,),
        in_specs=[pl.BlockSpec((tt, W), lambda i: (i, 0)), _full((1, W)), tbl, tbl, dyb, dyb, dyb],
        out_specs=[pl.BlockSpec((tt, W), lambda i: (i, 0)), _full((1, W))],
        out_shape=[jax.ShapeDtypeStruct((T, W), BF16), jax.ShapeDtypeStruct((1, W), F32)],
        compiler_params=_cp("arbitrary"), name=name)(proj, w_heads, cosf, sinsg, *dqs)


def _kv_prep_bwd(kv, w_heads, cosf, sinsg, dks, dvs, *, tt, name):
    T = kv.shape[0]

    def body(x_ref, w_ref, c_ref, s_ref, k0, k1, k2, v0, v1, v2, dx_ref, dw_ref):
        @pl.when(pl.program_id(0) == 0)
        def _():
            dw_ref[...] = jnp.zeros_like(dw_ref)

        c, s = c_ref[...], s_ref[...]
        for h in range(B_HEADS):
            sl = slice(h * HEAD_DIM, (h + 1) * HEAD_DIM)
            vs = slice(B_WIDTH + h * HEAD_DIM, B_WIDTH + (h + 1) * HEAD_DIM)
            xhat, r = _head_rms(x_ref[:, sl])
            dyn = _rope_bwd(k0[:, sl] + k1[:, sl] + k2[:, sl], c, s)
            dw_ref[:, sl] += jnp.sum(dyn * xhat, axis=0, keepdims=True)
            dx_ref[:, sl] = _head_rms_bwd(dyn * w_ref[:, sl], xhat, r).astype(BF16)
            dx_ref[:, vs] = (v0[:, sl] + v1[:, sl] + v2[:, sl]).astype(BF16)

    tbl = pl.BlockSpec((tt, HEAD_DIM), lambda i: (i, 0))
    dyb = pl.BlockSpec((tt, B_WIDTH), lambda i: (i, 0))
    return pl.pallas_call(
        body, grid=(T // tt,),
        in_specs=[dyb, _full((1, B_WIDTH)), tbl, tbl] + [dyb] * 6,
        out_specs=[pl.BlockSpec((tt, 2 * B_WIDTH), lambda i: (i, 0)), _full((1, B_WIDTH))],
        out_shape=[jax.ShapeDtypeStruct((T, 2 * B_WIDTH), BF16), jax.ShapeDtypeStruct((1, B_WIDTH), F32)],
        compiler_params=_cp("arbitrary"), name=name)(kv, w_heads, cosf, sinsg, *dks, *dvs)


def _band_masks(n_is_first=None):
    row = lax.broadcasted_iota(jnp.int32, (SPAN, SPAN), 0)
    col = lax.broadcasted_iota(jnp.int32, (SPAN, SPAN), 1)
    return row >= col, col >= row


def _dil_views(T, d):
    L = T // d
    return L, L // SPAN


def _dil_fwd(qr, kr, kv, gi, d, *, name):
    T = qr.shape[0]
    L, nb = _dil_views(T, d)

    def body(q_ref, kc_ref, kp_ref, vc_ref, vp_ref, o_ref, lse_ref):
        cur_ok, prev_band = _band_masks()
        prev_ok = prev_band & (pl.program_id(1) > 0)
        for h in range(B_HEADS):
            sl = slice(h * HEAD_DIM, (h + 1) * HEAD_DIM)
            q = q_ref[:, sl]
            sc = jnp.where(cur_ok, _dot_nt(q, kc_ref[:, sl]) * ATT_SCALE, NEG)
            sp = jnp.where(prev_ok, _dot_nt(q, kp_ref[:, sl]) * ATT_SCALE, NEG)
            m = jnp.maximum(jnp.max(sc, axis=-1, keepdims=True), jnp.max(sp, axis=-1, keepdims=True))
            pc = jnp.exp(sc - m)
            pp = jnp.exp(sp - m)
            l = jnp.sum(pc, axis=-1, keepdims=True) + jnp.sum(pp, axis=-1, keepdims=True)
            o_ref[:, sl] = (_dot(pc, vc_ref[:, sl]) + _dot(pp, vp_ref[:, sl])) / l
            lse_ref[:, sl] = jnp.broadcast_to(m + jnp.log(l), (SPAN, HEAD_DIM))

    blk = lambda f: pl.BlockSpec((SPAN, B_WIDTH), f)
    cur = lambda r, n: (n, r)
    prev = lambda r, n: (jnp.maximum(n - 1, 0), r)
    ov = jax.ShapeDtypeStruct((L, d * B_WIDTH), F32)
    o, lse = pl.pallas_call(
        body, grid=(d, nb),
        in_specs=[blk(lambda r, n: (n, r * N_GROUPS + gi)), blk(cur), blk(prev),
                  blk(lambda r, n: (n, 2 * r + 1)), blk(lambda r, n: (jnp.maximum(n - 1, 0), 2 * r + 1))],
        out_specs=[blk(cur), blk(cur)], out_shape=[ov, ov],
        compiler_params=_cp("parallel", "arbitrary"), name=name,
    )(qr.reshape(L, d * N_GROUPS * B_WIDTH), kr.reshape(L, d * B_WIDTH), kr.reshape(L, d * B_WIDTH),
      kv.reshape(L, d * 2 * B_WIDTH), kv.reshape(L, d * 2 * B_WIDTH))
    return o.reshape(T, B_WIDTH), lse.reshape(T, B_WIDTH)


def _dil_combine_fwd(os_, lses, *, tt, name):
    T = os_[0].shape[0]

    def body(o0, o1, o2, l0, l1, l2, y_ref, lse_ref):
        a, b, c = l0[...], l1[...], l2[...]
        m = jnp.maximum(jnp.maximum(a, b), c)
        wa, wb, wc = jnp.exp(a - m), jnp.exp(b - m), jnp.exp(c - m)
        den = wa + wb + wc
        y_ref[...] = (wa * o0[...] + wb * o1[...] + wc * o2[...]) / den
        lse_ref[...] = m + jnp.log(den)

    blk = pl.BlockSpec((tt, B_WIDTH), lambda i: (i, 0))
    sh = jax.ShapeDtypeStruct((T, B_WIDTH), F32)
    return pl.pallas_call(
        body, grid=(T // tt,), in_specs=[blk] * 6, out_specs=[blk, blk], out_shape=[sh, sh],
        compiler_params=_cp("parallel"), name=name)(*os_, *lses)


def _dil_bwd_prep(dmix, mix_main, *, tt, name, dep=None):
    T = mix_main.shape[0]

    def kernel_body(dy_ref, y_ref, dd_ref):
        for h in range(B_HEADS):
            sl = slice(h * HEAD_DIM, (h + 1) * HEAD_DIM)
            dd_ref[:, sl] = jnp.broadcast_to(jnp.sum(dy_ref[:, sl] * y_ref[:, sl], axis=-1, keepdims=True), (tt, HEAD_DIM))

    blk = pl.BlockSpec((tt, B_WIDTH), lambda i: (i, 0))
    body, dep_specs, dep_args = _dep(kernel_body, 2, dep)
    return pl.pallas_call(
        body, grid=(T // tt,), in_specs=[blk, blk] + dep_specs, out_specs=blk,
        out_shape=jax.ShapeDtypeStruct((T, B_WIDTH), F32),
        compiler_params=_cp("parallel"), name=name)(dmix, mix_main, *dep_args)


DILS_UNROLL = 4


def _dils_specs(gi, d, nblk):
    blk = lambda f: pl.BlockSpec((SPAN * d, HEAD_DIM), f)
    return {
        "q": blk(lambda h, n: (n, gi * B_HEADS + h)), "q_next": blk(lambda h, n: (jnp.minimum(n + 1, nblk - 1), gi * B_HEADS + h)),
        "cur": blk(lambda h, n: (n, h)), "prev": blk(lambda h, n: (jnp.maximum(n - 1, 0), h)),
        "next": blk(lambda h, n: (jnp.minimum(n + 1, nblk - 1), h)),
        "v": blk(lambda h, n: (n, B_HEADS + h)), "v_prev": blk(lambda h, n: (jnp.maximum(n - 1, 0), B_HEADS + h)),
    }


def _dils_fwd(qr, kr, kv, gi, d, *, name):
    T = qr.shape[0]
    nblk = T // (SPAN * d)
    sp = _dils_specs(gi, d, nblk)

    def body(q_ref, kc_ref, kp_ref, vc_ref, vp_ref, o_ref, lse_ref):
        cur_ok, prev_band = _band_masks()
        prev_ok = prev_band & (pl.program_id(1) > 0)

        def residue(r, carry):
            rows = pl.ds(r, SPAN, stride=d)
            q = q_ref[rows, :]
            sc = jnp.where(cur_ok, _dot_nt(q, kc_ref[rows, :]) * ATT_SCALE, NEG)
            sp_ = jnp.where(prev_ok, _dot_nt(q, kp_ref[rows, :]) * ATT_SCALE, NEG)
            m = jnp.maximum(jnp.max(sc, axis=-1, keepdims=True), jnp.max(sp_, axis=-1, keepdims=True))
            pc = jnp.exp(sc - m)
            pp = jnp.exp(sp_ - m)
            l = jnp.sum(pc, axis=-1, keepdims=True) + jnp.sum(pp, axis=-1, keepdims=True)
            o_ref[rows, :] = (_dot(pc, vc_ref[rows, :]) + _dot(pp, vp_ref[rows, :])) / l
            lse_ref[rows, :] = jnp.broadcast_to(m + jnp.log(l), (SPAN, HEAD_DIM))
            return carry

        lax.fori_loop(0, d, residue, 0, unroll=DILS_UNROLL)

    sh = jax.ShapeDtypeStruct((T, B_WIDTH), F32)
    return pl.pallas_call(
        body, grid=(B_HEADS, nblk), in_specs=[sp["q"], sp["cur"], sp["prev"], sp["v"], sp["v_prev"]],
        out_specs=[sp["cur"], sp["cur"]], out_shape=[sh, sh],
        compiler_params=_cp("parallel", "arbitrary"), name=name)(qr, kr, kr, kv, kv)


DIL_BWD_GROUP = {1: 4, 4: 1, 16: 1}


def _dil_bwd(qr, kr, kv, dmix, lse, dd, gi, d, *, name):
    T = qr.shape[0]
    G = DIL_BWD_GROUP[d]
    band = SPAN * d
    tb = G * band
    nblk = T // tb

    def body(q_ref, dy_ref, lse_ref, dd_ref, kc_ref, kp_ref, vc_ref, vp_ref, dq_ref, dk_ref, dv_ref):
        n = pl.program_id(1)

        @pl.when(n == 0)
        def _():
            dk_ref[...] = jnp.zeros_like(dk_ref)
            dv_ref[...] = jnp.zeros_like(dv_ref)

        cur_ok, prev_band = _band_masks()
        base = pl.multiple_of(n * tb, SPAN)
        for j in range(G):
            def residue(r, carry, j=j):
                off = j * band + r
                rows = pl.ds(off, SPAN, stride=d)
                q, dy = q_ref[rows, :], dy_ref[rows, :]
                lse_h = jnp.max(lse_ref[rows, :], axis=-1, keepdims=True)
                dd_h = jnp.max(dd_ref[rows, :], axis=-1, keepdims=True)
                kc, vc = kc_ref[rows, :], vc_ref[rows, :]
                if j > 0:
                    before = pl.ds(off - band, SPAN, stride=d)
                    kp, vp = kc_ref[before, :], vc_ref[before, :]
                    prev_ok = prev_band
                else:
                    before = pl.ds((G - 1) * band + r, SPAN, stride=d)
                    kp, vp = kp_ref[before, :], vp_ref[before, :]
                    prev_ok = prev_band & (n > 0)
                pc = jnp.exp(jnp.where(cur_ok, _dot_nt(q, kc) * ATT_SCALE, NEG) - lse_h)
                pp = jnp.exp(jnp.where(prev_ok, _dot_nt(q, kp) * ATT_SCALE, NEG) - lse_h)
                dsc = pc * (_dot_nt(dy, vc) - dd_h) * ATT_SCALE
                dsp = pp * (_dot_nt(dy, vp) - dd_h) * ATT_SCALE
                dq_ref[rows, :] = _dot(dsc, kc) + _dot(dsp, kp)
                here = pl.ds(base + off, SPAN, stride=d)
                dk_ref[here, :] += _dot_tn(dsc, q)
                dv_ref[here, :] += _dot_tn(pc, dy)
                there = pl.ds(jnp.maximum(base + off - band, r), SPAN, stride=d)
                dk_ref[there, :] += _dot_tn(dsp, q)
                dv_ref[there, :] += _dot_tn(pp, dy)
                return carry

            lax.fori_loop(0, d, residue, 0, unroll=min(d, DILS_UNROLL))

    blk = lambda f: pl.BlockSpec((tb, HEAD_DIM), f)
    cur = lambda h, n: (n, h)
    prev = lambda h, n: (jnp.maximum(n - 1, 0), h)
    whole = pl.BlockSpec((T, HEAD_DIM), lambda h, n: (0, h))
    sh = jax.ShapeDtypeStruct((T, B_WIDTH), F32)
    return pl.pallas_call(
        body, grid=(B_HEADS, nblk),
        in_specs=[blk(lambda h, n: (n, gi * B_HEADS + h)), blk(cur), blk(cur), blk(cur), blk(cur), blk(prev),
                  blk(lambda h, n: (n, B_HEADS + h)), blk(lambda h, n: (jnp.maximum(n - 1, 0), B_HEADS + h))],
        out_specs=[blk(cur), whole, whole], out_shape=[sh, sh, sh],
        compiler_params=_cp("parallel", "arbitrary"), name=name)(qr, dmix, lse, dd, kr, kr, kv, kv)


A_MQ_COL = 4 * A_WIDTH // MEM_WIDTH
B_MQ_COL = N_GROUPS * B_WIDTH // MEM_WIDTH


def _row(v):
    return v.reshape(1, -1).astype(F32)


def _local_step(x, mem, tgt, get_w, P, put_g, first_dep=None):
    T = x.shape[0]
    cosf, sinsg = _rope_tables(T)
    lb_soft = jax.nn.softmax(P["a_lb_logits"].astype(F32), axis=0)
    lb = lb_soft[0:1]
    qw_heads = jnp.repeat(P["b_qnorm"][0], B_HEADS, axis=0).reshape(1, -1)
    kw_heads = jnp.tile(_row(P["b_knorm"]), (1, B_HEADS))
    mqw = [jnp.tile(_row(P["mem_qnorm"][l]), (1, MEM_HEADS)) for l in range(2)]
    mkw = [jnp.tile(_row(P["mem_knorm"][l]), (1, MEM_HEADS)) for l in range(2)]
    nmix = [_row(P["norm_mix"][l]) for l in range(2)]
    nffn = [_row(P["norm_ffn"][l]) for l in range(2)]
    mnorm = [_row(P["mem_norm"][l]) for l in range(2)]
    kvn = _row(P["kv_norm"])
    onorm = _row(P["a_onorm"])
    W = {}

    def w_of(name, after=None):
        if name not in W:
            W[name] = get_w(name, after)
        return W[name]

    proj_a, xn0 = _rms_matmul(x, nmix[0], w_of("a_w_in"), tt=512, tn=1664, wt=True, name="proj_a", dep=first_dep)
    mkv0, mn0 = _rms_matmul(mem, mnorm[0], w_of("w_mem_kv0"), tt=MEM_TOKENS, tn=2 * MEM_WIDTH, wt=False, name="mem_kv0")
    o_raw, st = _hgrn2_fwd(proj_a, lb, name="hgrn2_fwd")
    mm0 = _a_post_fwd(o_raw, proj_a, onorm, tt=512, name="a_post_fwd")
    mo0 = _mem_attn_fwd(proj_a, A_MQ_COL, mkv0, mqw[0], mkw[0], tt=512, name="mem_attn_fwd0")
    hm0 = _mm_res(x, mm0, mo0, w_of("w_out0", mo0), tt=512, name="out_proj0")
    gu0, hn0 = _rms_matmul(hm0, nffn[0], w_of("w_gate_up0", hm0), tt=512, tn=1408, wt=True, out_dtype=BF16, name="gate_up0")
    h1 = _swiglu_down(hm0, gu0, w_of("w_down0", gu0), tt=256, name="down0")
    kv, hkn = _rms_matmul(h1, kvn, w_of("w_kv", h1), tt=512, tn=768, wt=True, name="kv_proj")
    kr = _headnorm_rope_fwd(kv, kw_heads, cosf, sinsg, col0=0, n_heads=B_HEADS, tt=512, name="k_prep")

    proj_b, xn1 = _rms_matmul(h1, nmix[1], w_of("b_w_in", kr), tt=512, tn=1280, wt=True, name="proj_b")
    mkv1, mn1 = _rms_matmul(mem, mnorm[1], w_of("w_mem_kv1", kr), tt=MEM_TOKENS, tn=2 * MEM_WIDTH, wt=False, name="mem_kv1")
    qr = _headnorm_rope_fwd(proj_b, qw_heads, cosf, sinsg, col0=0, n_heads=N_GROUPS * B_HEADS, tt=512, name="q_prep")
    outs = [(_dil_fwd if d == 1 else _dils_fwd)(qr, kr, kv, gi, d, name=f"dil_fwd{gi}") for gi, d in enumerate(DILATIONS)]
    mm1, lse_tot = _dil_combine_fwd([o for o, _ in outs], [s for _, s in outs], tt=512, name="dil_combine")
    mo1 = _mem_attn_fwd(proj_b, B_MQ_COL, mkv1, mqw[1], mkw[1], tt=512, name="mem_attn_fwd1")
    hm1 = _mm_res(h1, mm1, mo1, w_of("w_out1", mo1), tt=512, name="out_proj1")
    gu1, hn1 = _rms_matmul(hm1, nffn[1], w_of("w_gate_up1", hm1), tt=512, tn=1408, wt=True, out_dtype=BF16, name="gate_up1")
    y = _swiglu_down(hm1, gu1, w_of("w_down1", gu1), tt=256, name="down1")
    dy, sq = _loss_kernel(y, tgt, tt=512, name="loss")

    gP = {}
    zeros_mem = jnp.zeros((MEM_TOKENS, D_MODEL), F32)

    def ffn_bwd(l, dh, hm, gu, hn):
        dgu, g_wd = _swiglu_bwd(dh, gu, w_of(f"w_down{l}"), tt=256, name=f"swiglu_bwd{l}")
        g_wgu = _mm_tn(dgu, hn, tt=512, tka=1408, name=f"g_w_gate_up{l}")
        sent = put_g({f"w_down{l}": g_wd, f"w_gate_up{l}": g_wgu})
        dhm, g_nf = _rms_bwd_dx(hm, nffn[l], w_of(f"w_gate_up{l}"), dgu, dh, tt=256, wt=True, name=f"gate_up_bwd{l}", dep=sent)
        return dhm, g_nf

    def mix_bwd(l, dhm, mix_main, mix_mem, proj, qcol, mkv, mn):
        dmix, g_wout = _out_proj_bwd(dhm, mix_main, mix_mem, w_of(f"w_out{l}"), tt=512, name=f"out_proj_bwd{l}")
        dmq, dmkv, dqw, dkw = _mem_attn_bwd(proj, qcol, mkv, mqw[l], mkw[l], dmix, tt=512, name=f"mem_attn_bwd{l}")
        g_wmkv = _mm_tn(mn, dmkv, tt=MEM_TOKENS, tka=512, name=f"g_w_mem_kv{l}")
        sent = put_g({f"w_out{l}": g_wout, f"w_mem_kv{l}": g_wmkv})
        _, g_mn = _rms_bwd_dx(mem, mnorm[l], w_of(f"w_mem_kv{l}"), dmkv, zeros_mem, tt=MEM_TOKENS, wt=False, name=f"mem_kv_bwd{l}")
        fold = lambda v: v.reshape(MEM_HEADS, MEM_HEAD_DIM).sum(axis=0)
        return dmix, dmq, g_mn, fold(dqw), fold(dkw), sent

    dhm1, g_nf1 = ffn_bwd(1, dy, hm1, gu1, hn1)
    dmix1, dmq1, g_mn1, g_mq1, g_mk1, sent = mix_bwd(1, dhm1, mm1, mo1, proj_b, B_MQ_COL, mkv1, mn1)
    dd = _dil_bwd_prep(dmix1, mm1, tt=512, name="dil_bwd_prep", dep=sent)
    dqs, dks, dvs = [], [], []
    for gi, d in enumerate(DILATIONS):
        dq_g, dk_g, dv_g = _dil_bwd(qr, kr, kv, dmix1, lse_tot, dd, gi, d, name=f"dil_bwd{gi}")
        dqs.append(dq_g)
        dks.append(dk_g)
        dvs.append(dv_g)
    dq_raw, dqw = _q_prep_bwd(proj_b, qw_heads, cosf, sinsg, dqs, tt=512, name="q_prep_bwd")
    dkv, dkw = _kv_prep_bwd(kv, kw_heads, cosf, sinsg, dks, dvs, tt=512, name="kv_prep_bwd")
    dproj_b = jnp.concatenate([dq_raw, dmq1], axis=1)
    g_wb = _mm_tn(dproj_b, xn1, tt=512, tka=1280, name="g_b_w_in")
    g_wkv = _mm_tn(dkv, hkn, tt=512, tka=768, name="g_w_kv")
    sent = put_g({"b_w_in": g_wb, "w_kv": g_wkv})
    dh1, g_nm1 = _rms_bwd_dx(h1, nmix[1], w_of("b_w_in"), dproj_b, dhm1, tt=256, wt=True, name="proj_b_bwd", dep=sent)
    dh1, g_kvn = _rms_bwd_dx(h1, kvn, w_of("w_kv"), dkv, dh1, tt=256, wt=True, name="kv_proj_bwd")

    dhm0, g_nf0 = ffn_bwd(0, dh1, hm0, gu0, hn0)
    dmix0, dmq0, g_mn0, g_mq0, g_mk0, sent = mix_bwd(0, dhm0, mm0, mo0, proj_a, A_MQ_COL, mkv0, mn0)
    do_raw, dg, g_onorm = _a_post_bwd(o_raw, proj_a, onorm, dmix0, tt=512, name="a_post_bwd", dep=sent)
    dq, dz, dv, dlb = _hgrn2_bwd(proj_a, lb, st, do_raw, name="hgrn2_bwd")
    dproj_a = jnp.concatenate([dq, dz, dv, dg, dmq0], axis=1)
    sent = put_g({"a_w_in": _mm_tn(dproj_a, xn0, tt=512, tka=1664, name="g_a_w_in")})
    gx, g_nm0 = _rms_bwd_dx(x, nmix[0], w_of("a_w_in"), dproj_a, dhm0, tt=256, wt=True, name="proj_a_bwd", dep=sent)

    dl0 = lb_soft[0:1] * lb_soft[1:2] * dlb
    gP["a_lb_logits"] = jnp.concatenate([dl0, -dl0], axis=0)
    gP["a_onorm"] = g_onorm
    gP["norm_mix"] = jnp.concatenate([g_nm0, g_nm1], axis=0)
    gP["norm_ffn"] = jnp.concatenate([g_nf0, g_nf1], axis=0)
    gP["b_qnorm"] = dqw.reshape(N_GROUPS, B_HEADS, HEAD_DIM).sum(axis=1)[None]
    gP["kv_norm"] = g_kvn.reshape(-1)
    gP["b_knorm"] = dkw.reshape(B_HEADS, HEAD_DIM).sum(axis=0)
    gP["mem_norm"] = jnp.concatenate([g_mn0, g_mn1], axis=0)
    gP["mem_qnorm"] = jnp.stack([g_mq0, g_mq1])
    gP["mem_knorm"] = jnp.stack([g_mk0, g_mk1])
    return sq, gx, gP


MESH_ID = pl.DeviceIdType.MESH
HBM_SPEC = pl.BlockSpec(memory_space=pltpu.HBM)


def _position():
    return lax.axis_index("x"), lax.axis_index("y"), lax.axis_index("c")


def _all_gather(blocks, *, name):
    n = len(blocks)

    def body(*refs):
        x_refs, out_refs = refs[:n], refs[n:2 * n]
        send_sems, recv_sems, local_sems = refs[2 * n:]
        x, y, c = _position()
        me, sibling = (x, y, c), (x, y, 1 - c)
        chips = [(1 - x, y), (x, 1 - y), (1 - x, 1 - y)]

        def slot(a, px, py, pc):
            return out_refs[a].at[4 * px + 2 * py + pc]

        def copy(a, k, blk, to, src=None):
            return pltpu.make_async_remote_copy(
                src_ref=slot(a, *blk) if src is None else src, dst_ref=slot(a, *blk),
                send_sem=send_sems.at[7 * a + k], recv_sem=recv_sems.at[7 * a + k], device_id=to, device_id_type=MESH_ID)

        mine = [pltpu.make_async_copy(x_refs[a], slot(a, *me), local_sems.at[a]) for a in range(n)]
        for cp in mine:
            cp.start()
        first = []
        for a in range(n):
            first.append(copy(a, 0, me, sibling, src=x_refs[a]))
            first += [copy(a, 1 + j, me, (*chip, c), src=x_refs[a]) for j, chip in enumerate(chips)]
        for cp in first:
            cp.start()
        passed = []
        for j, chip in enumerate(chips):
            for a in range(n):
                copy(a, 1 + j, (*chip, c), me).wait_recv()
                cp = copy(a, 4 + j, (*chip, c), sibling)
                cp.start()
                passed.append(cp)
        for a in range(n):
            copy(a, 0, sibling, me).wait_recv()
            for j, chip in enumerate(chips):
                copy(a, 4 + j, (*chip, 1 - c), me).wait_recv()
        for cp in first + passed:
            cp.wait_send()
        for cp in mine:
            cp.wait()

    return pl.pallas_call(
        body, out_shape=[jax.ShapeDtypeStruct((N_DEV,) + b.shape, b.dtype) for b in blocks],
        in_specs=[HBM_SPEC] * n, out_specs=[HBM_SPEC] * n,
        scratch_shapes=[pltpu.SemaphoreType.DMA((7 * n,)), pltpu.SemaphoreType.DMA((7 * n,)), pltpu.SemaphoreType.DMA((n,))],
        name=name)(*blocks)


SEM_SPEC = pl.BlockSpec(memory_space=pltpu.SEMAPHORE)
ANY_SPEC = pl.BlockSpec(memory_space=pl.ANY)
DATAFLOW = pltpu.SideEffectType.DATAFLOW_SIDE_EFFECTING


def _peer(k, x, y, c):
    return (1 - x if (k >> 2) & 1 else x, 1 - y if (k >> 1) & 1 else y, 1 - c if k & 1 else c)


def _own_slot_filled(own_block):
    x, y, c = _position()
    zone = lax.empty((N_DEV,) + own_block.shape, own_block.dtype)
    return lax.dynamic_update_slice_in_dim(zone, own_block[None], 4 * x + 2 * y + c, axis=0)


def _split_start(srcs, scatter, after, *, name):
    n = len(srcs)
    extra = [] if after is None else [after]
    x, y, c = _position()
    me = 4 * x + 2 * y + c
    lands = [_own_slot_filled(lax.dynamic_index_in_dim(s, me, 0, keepdims=False) if scatter else s) for s in srcs]

    def body(*refs):
        src_refs, land_refs = refs[:n], refs[n:2 * n]
        send_sems, recv_sems = refs[2 * n + len(extra)], refs[2 * n + len(extra) + 1]
        token = refs[-1]
        bx, by, bc = _position()
        bme = 4 * bx + 2 * by + bc
        for a in range(n):
            for k in range(1, N_DEV):
                tx, ty, tc = _peer(k, bx, by, bc)
                src = src_refs[a].at[4 * tx + 2 * ty + tc] if scatter else src_refs[a]
                pltpu.make_async_remote_copy(
                    src_ref=src, dst_ref=land_refs[a].at[bme],
                    send_sem=send_sems.at[7 * a + k - 1], recv_sem=recv_sems.at[7 * a + k - 1],
                    device_id=(tx, ty, tc), device_id_type=MESH_ID).start()
        token[...] = jnp.zeros_like(token)

    hbm = lambda a: pltpu.HBM(a.shape, a.dtype)
    outs = pl.pallas_call(
        body, name=name,
        out_shape=(pltpu.SemaphoreType.DMA((7 * n,)), pltpu.SemaphoreType.DMA((7 * n,)),
                   *[hbm(s) for s in srcs], *[hbm(l) for l in lands], jax.ShapeDtypeStruct((8, 128), F32)),
        in_specs=[HBM_SPEC] * (2 * n) + [ANY_SPEC] * len(extra),
        out_specs=(SEM_SPEC, SEM_SPEC, *[HBM_SPEC] * (2 * n), pl.BlockSpec(memory_space=pltpu.VMEM)),
        input_output_aliases={i: 2 + i for i in range(2 * n)},
        compiler_params=pltpu.CompilerParams(has_side_effects=DATAFLOW),
    )(*[pltpu.with_memory_space_constraint(s, pltpu.HBM) for s in srcs],
      *[pltpu.with_memory_space_constraint(l, pltpu.HBM) for l in lands], *extra)
    return {"n": n, "scatter": scatter, "send": outs[0], "recv": outs[1], "srcs": outs[2:2 + n],
            "lands": outs[2 + n:2 + 2 * n], "token": outs[-1]}


def _split_wait(handle, after, *, name):
    n, scatter = handle["n"], handle["scatter"]

    def body(*refs):
        src_refs, land_refs = refs[:n], refs[n:2 * n]
        send_sems, recv_sems = refs[2 * n], refs[2 * n + 1]
        bx, by, bc = _position()
        for a in range(n):
            for k in range(1, N_DEV):
                src = src_refs[a].at[0] if scatter else src_refs[a]
                cp = pltpu.make_async_remote_copy(
                    src_ref=src, dst_ref=land_refs[a].at[0],
                    send_sem=send_sems.at[7 * a + k - 1], recv_sem=recv_sems.at[7 * a + k - 1],
                    device_id=_peer(k, bx, by, bc), device_id_type=MESH_ID)
                cp.wait_send()
                cp.wait_recv()

    hbm = lambda a: pltpu.HBM(a.shape, a.dtype)
    outs = pl.pallas_call(
        body, name=name,
        out_shape=(*[hbm(s) for s in handle["srcs"]], *[hbm(l) for l in handle["lands"]]),
        in_specs=[HBM_SPEC] * (2 * n) + [SEM_SPEC, SEM_SPEC, ANY_SPEC],
        out_specs=tuple([HBM_SPEC] * (2 * n)),
        input_output_aliases={i: i for i in range(2 * n)},
        compiler_params=pltpu.CompilerParams(has_side_effects=DATAFLOW),
    )(*handle["srcs"], *handle["lands"], handle["send"], handle["recv"], after)
    return list(outs[n:])


def _sum_sources(parts, *, tr, name):
    n, R, C = parts.shape

    def body(p_ref, o_ref):
        acc = p_ref[0].astype(F32)
        for s in range(1, n):
            acc = acc + p_ref[s].astype(F32)
        o_ref[...] = acc

    return pl.pallas_call(
        body, grid=(R // tr,), in_specs=[pl.BlockSpec((n, tr, C), lambda i: (0, i, 0))],
        out_specs=pl.BlockSpec((tr, C), lambda i: (i, 0)),
        out_shape=jax.ShapeDtypeStruct((R, C), F32), compiler_params=_cp("parallel"), name=name)(parts)


def _adamw_math(g, w, m, v):
    c1 = 1.0 - ADAM_B1 ** ADAM_STEP
    c2 = 1.0 - ADAM_B2 ** ADAM_STEP
    nm = ADAM_B1 * m + (1.0 - ADAM_B1) * g
    nv = ADAM_B2 * v + (1.0 - ADAM_B2) * (g * g)
    return -ADAM_LR * ((nm / c1) / (jnp.sqrt(nv / c2) + ADAM_EPS) + ADAM_WD * w), nm, nv


def _reduce_adamw(received, w, m, v, *, col, tr, name):
    L, R, C = w.shape

    def body(*refs):
        p_refs = refs[:L]
        w_ref, m_ref, v_ref, g_ref, d_ref, nm_ref, nv_ref = refs[L:]
        for l in range(L):
            @pl.when(pl.program_id(0) == l)
            def _(p_ref=p_refs[l]):
                acc = p_ref[0].astype(F32)
                for s in range(1, N_DEV):
                    acc = acc + p_ref[s].astype(F32)
                g = acc.T if col else acc
                g_ref[...] = g
                d_ref[...], nm_ref[...], nv_ref[...] = _adamw_math(g, w_ref[...], m_ref[...], v_ref[...])

    p_spec = (pl.BlockSpec((N_DEV, C, tr), lambda l, i: (0, 0, i)) if col
              else pl.BlockSpec((N_DEV, tr, C), lambda l, i: (0, i, 0)))
    blk = pl.BlockSpec((None, tr, C), lambda l, i: (l, i, 0))
    sh = jax.ShapeDtypeStruct((L, R, C), F32)
    return pl.pallas_call(
        body, grid=(L, R // tr), in_specs=[p_spec] * L + [blk] * 3, out_specs=[blk] * 4, out_shape=[sh] * 4,
        compiler_params=_cp("parallel", "parallel"), name=name)(*received, w, m, v)


def _adamw(g, w, m, v, *, tr, name):
    L, R, C = w.shape

    def body(g_ref, w_ref, m_ref, v_ref, d_ref, nm_ref, nv_ref):
        d_ref[...], nm_ref[...], nv_ref[...] = _adamw_math(g_ref[...], w_ref[...], m_ref[...], v_ref[...])

    blk = pl.BlockSpec((None, tr, C), lambda l, i: (l, i, 0))
    sh = jax.ShapeDtypeStruct((L, R, C), F32)
    return pl.pallas_call(
        body, grid=(L, R // tr), in_specs=[blk] * 4, out_specs=[blk] * 3, out_shape=[sh] * 3,
        compiler_params=_cp("parallel", "parallel"), name=name)(g, w, m, v)


UNITS = {
    "a_w_in": ("a_w_in", 0, True), "w_mem_kv0": ("w_mem_kv", 0, False), "w_out0": ("w_out", 0, False),
    "w_gate_up0": ("w_gate_up", 0, True), "w_down0": ("w_down", 0, False), "w_kv": ("w_kv", None, True),
    "b_w_in": ("b_w_in", 0, True), "w_mem_kv1": ("w_mem_kv", 1, False), "w_out1": ("w_out", 1, False),
    "w_gate_up1": ("w_gate_up", 1, True), "w_down1": ("w_down", 1, False),
}
BIG = ("a_w_in", "b_w_in", "w_kv", "w_mem_kv", "w_out", "w_gate_up", "w_down")
ADAMW_ROW_TILE = {"a_w_in": 256, "b_w_in": 256, "w_kv": 256, "w_mem_kv": 128, "w_out": 128, "w_gate_up": 176, "w_down": 176}
TRANSPOSED_UPDATE = ("w_gate_up",)


def _wire_block(weights, unit):
    name, layer, col = UNITS[unit]
    a = weights[name] if layer is None else weights[name][layer]
    return (a.T if col else a).astype(BF16)


SMALL_REPLICATED = ("norm_mix", "norm_ffn", "b_qnorm", "kv_norm", "b_knorm", "mem_norm", "mem_qnorm", "mem_knorm")
SMALL_SHARDED = ("a_lb_logits", "a_onorm")
SMALL_ORDER = SMALL_REPLICATED + SMALL_SHARDED
LANES = 128


def _prod(shape):
    n = 1
    for s in shape:
        n *= s
    return n


def _pack_flat(arrays, rows, cols, dtype):
    flat = jnp.concatenate([a.reshape(-1).astype(dtype) for a in arrays])
    return jnp.pad(flat, (0, rows * cols - flat.shape[0])).reshape(rows, cols)


def _unpack_flat(packed, shapes):
    flat = packed.reshape(-1)
    out, off = [], 0
    for s in shapes:
        out.append(flat[off:off + _prod(s)].reshape(s))
        off += _prod(s)
    return out


def kernel(x, mem, norm_mix, norm_ffn, a_w_in, a_lb_logits, a_onorm, b_w_in, b_qnorm, kv_norm, w_kv, b_knorm, mem_norm, w_mem_kv, mem_qnorm, mem_knorm, w_out, w_gate_up, w_down, loss_target, m_norm_mix, m_norm_ffn, m_a_w_in, m_a_lb_logits, m_a_onorm, m_b_w_in, m_b_qnorm, m_kv_norm, m_w_kv, m_b_knorm, m_mem_norm, m_w_mem_kv, m_mem_qnorm, m_mem_knorm, m_w_out, m_w_gate_up, m_w_down, v_norm_mix, v_norm_ffn, v_a_w_in, v_a_lb_logits, v_a_onorm, v_b_w_in, v_b_qnorm, v_kv_norm, v_w_kv, v_b_knorm, v_mem_norm, v_w_mem_kv, v_mem_qnorm, v_mem_knorm, v_w_out, v_w_gate_up, v_w_down):
    names = ("norm_mix", "norm_ffn", "a_w_in", "a_lb_logits", "a_onorm", "b_w_in", "b_qnorm", "kv_norm", "w_kv", "b_knorm",
             "mem_norm", "w_mem_kv", "mem_qnorm", "mem_knorm", "w_out", "w_gate_up", "w_down")
    w = dict(zip(names, (norm_mix, norm_ffn, a_w_in, a_lb_logits, a_onorm, b_w_in, b_qnorm, kv_norm, w_kv, b_knorm,
                         mem_norm, w_mem_kv, mem_qnorm, mem_knorm, w_out, w_gate_up, w_down)))
    m = dict(zip(names, (m_norm_mix, m_norm_ffn, m_a_w_in, m_a_lb_logits, m_a_onorm, m_b_w_in, m_b_qnorm, m_kv_norm, m_w_kv,
                         m_b_knorm, m_mem_norm, m_w_mem_kv, m_mem_qnorm, m_mem_knorm, m_w_out, m_w_gate_up, m_w_down)))
    v = dict(zip(names, (v_norm_mix, v_norm_ffn, v_a_w_in, v_a_lb_logits, v_a_onorm, v_b_w_in, v_b_qnorm, v_kv_norm, v_w_kv,
                         v_b_knorm, v_mem_norm, v_w_mem_kv, v_mem_qnorm, v_mem_knorm, v_w_out, v_w_gate_up, v_w_down)))

    first = ["a_w_in", "w_mem_kv0"]
    gathered = _all_gather([_wire_block(w, u) for u in first] + [_pack_flat([a_lb_logits, a_onorm], 8, LANES, F32)],
                           name="gather_first")
    full = {u: g.reshape(-1, g.shape[-1]) for u, g in zip(first, gathered)}
    small_in = gathered[-1].reshape(N_DEV, -1)
    P = {n: w[n] for n in SMALL_REPLICATED}
    P["a_lb_logits"] = small_in[:, :192].reshape(N_DEV, 2, 96).transpose(1, 0, 2).reshape(2, A_WIDTH)
    P["a_onorm"] = small_in[:, 192:288].reshape(1, A_WIDTH)
    later = [["w_out0", "w_gate_up0"], ["w_down0", "w_kv"], ["b_w_in", "w_mem_kv1"], ["w_out1", "w_gate_up1", "w_down1"]]
    pending = {}
    token = gathered[-1]
    for i, group in enumerate(later):
        handle = _split_start([_wire_block(w, u) for u in group], False, token, name=f"gather{i}_start")
        token = handle["token"]
        for u in group:
            pending[u] = (i, group, handle)

    def get_w(unit, after):
        if unit not in full:
            i, group, handle = pending[unit]
            for u, land in zip(group, _split_wait(handle, after, name=f"gather{i}_wait")):
                full[u] = land.reshape(-1, land.shape[-1])
        return full[unit]

    sent = []

    def put_g(group):
        units = list(group)
        handle = _split_start([group[u].reshape(N_DEV, -1, group[u].shape[-1]) for u in units], True, None,
                              name=f"scatter{len(sent)}_start")
        sent.append((units, handle))
        return handle["token"]

    sq, gx, gP = _local_step(x[0], mem[0], loss_target[0], get_w, P, put_g, first_dep=token)
    loss = lax.psum(0.5 * jnp.sum(sq) / D_MODEL, ("x", "y", "c"))

    received = {}
    for i, (units, handle) in enumerate(sent):
        received.update(zip(units, _split_wait(handle, gx, name=f"scatter{i}_wait")))
    out = {"grad": {}, "delta": {}, "new_m": {}, "new_v": {}}
    for n in BIG:
        shape = w[n].shape
        as3 = lambda a: a.reshape((-1,) + shape[-2:])
        mine = [u for u, (wn, _, _) in UNITS.items() if wn == n]
        col = UNITS[mine[0]][2]
        flip = (lambda a: jnp.swapaxes(a, 1, 2)) if n in TRANSPOSED_UPDATE else (lambda a: a)
        res = _reduce_adamw([received[u] for u in mine], flip(as3(w[n])), flip(as3(m[n])), flip(as3(v[n])),
                            col=col and n not in TRANSPOSED_UPDATE, tr=ADAMW_ROW_TILE[n], name=f"adamw_{n}")
        for kind, r in zip(("grad", "delta", "new_m", "new_v"), res):
            out[kind][n] = flip(r).reshape(shape)

    full_shapes = [(2, A_WIDTH) if n == "a_lb_logits" else (1, A_WIDTH) if n == "a_onorm" else w[n].shape for n in SMALL_ORDER]
    n_small = sum(_prod(s) for s in full_shapes)
    rows_small = -(-n_small // (8 * LANES)) * 8
    g_all, = _all_gather([_pack_flat([gP[n] for n in SMALL_ORDER], rows_small, LANES, F32)], name="gather_small_grads")
    g_small = dict(zip(SMALL_ORDER, _unpack_flat(_sum_sources(g_all, tr=rows_small, name="sum_small_grads"), full_shapes)))
    me = 4 * lax.axis_index("x") + 2 * lax.axis_index("y") + lax.axis_index("c")
    for n in SMALL_SHARDED:
        g_small[n] = lax.dynamic_slice_in_dim(g_small[n], me * 96, 96, axis=1)
    shapes = [w[n].shape for n in SMALL_ORDER]
    rows_upd = -(-sum(_prod(s) for s in shapes) // (8 * LANES)) * 8
    pk = lambda d: _pack_flat([d[n] for n in SMALL_ORDER], rows_upd, LANES, F32)
    res = _adamw(pk(g_small)[None], pk(w)[None], pk(m)[None], pk(v)[None], tr=rows_upd, name="adamw_small")
    out["grad"].update(g_small)
    for kind, packed in zip(("delta", "new_m", "new_v"), res):
        out[kind].update(zip(SMALL_ORDER, _unpack_flat(packed[0], shapes)))

    return (loss, gx[None], *[out["grad"][n] for n in names], *[out["delta"][n] for n in names],
            *[out["new_m"][n] for n in names], *[out["new_v"][n] for n in names])
```

```python
import functools

import jax
import jax.numpy as jnp
from jax import lax
from jax.experimental import pallas as pl
from jax.experimental.pallas import tpu as pltpu

F32 = jnp.float32
BF16 = jnp.bfloat16

N_DEV = 8
D_MODEL = 1024
HEAD_DIM = 128
A_HEADS = 6
A_WIDTH = A_HEADS * HEAD_DIM
CHUNK = 64
B_HEADS = 6
B_WIDTH = B_HEADS * HEAD_DIM
DILATIONS = (1, 4, 16)
SPAN = 128
N_GROUPS = 3
ROPE_THETA = 10000.0
MEM_TOKENS = 256
MEM_HEADS = 4
MEM_HEAD_DIM = 64
MEM_WIDTH = MEM_HEADS * MEM_HEAD_DIM
FFN_HIDDEN = 2816
EPS = 1e-6

ADAM_LR = 0.001
ADAM_B1 = 0.9
ADAM_B2 = 0.999
ADAM_EPS = 1e-08
ADAM_WD = 0.01
ADAM_STEP = 10

V7X_VMEM_LIMIT_BYTES = 56 * 1024 * 1024

NT_DIMS = (((1,), (1,)), ((), ()))
TN_DIMS = (((0,), (0,)), ((), ()))


def _cp(*sem):
    return pltpu.CompilerParams(dimension_semantics=sem, vmem_limit_bytes=V7X_VMEM_LIMIT_BYTES)


def _dot(a, b):
    return jnp.dot(a.astype(BF16), b.astype(BF16), preferred_element_type=F32)


def _dot_nt(a, b):
    return lax.dot_general(a.astype(BF16), b.astype(BF16), NT_DIMS, preferred_element_type=F32)


def _dot_tn(a, b):
    return lax.dot_general(a.astype(BF16), b.astype(BF16), TN_DIMS, preferred_element_type=F32)


def _dot3(m01, x):
    hi = x.astype(BF16)
    r1 = x - hi.astype(F32)
    mid = r1.astype(BF16)
    lo = (r1 - mid.astype(F32)).astype(BF16)
    d = functools.partial(jnp.dot, preferred_element_type=F32)
    return d(m01, hi) + d(m01, mid) + d(m01, lo)


def _sigmoid(x):
    return 1.0 / (1.0 + jnp.exp(-x))


def _full(shape):
    return pl.BlockSpec(shape, lambda *_: (0,) * len(shape))


def _dep(body, n_in, dep):
    if dep is None:
        return body, [], []

    def with_dep(*refs):
        return body(*refs[:n_in], *refs[n_in + 1:])

    return with_dep, [pl.BlockSpec(memory_space=pl.ANY)], [dep]


def _rms_matmul(x, g, w, *, tt, tn, wt, name, out_dtype=F32, dep=None):
    T, K = x.shape
    N = w.shape[0] if wt else w.shape[1]

    def kernel_body(x_ref, g_ref, w_ref, y_ref, xn_ref):
        xf = x_ref[...]
        r = lax.rsqrt(jnp.mean(xf * xf, axis=-1, keepdims=True) + EPS)
        xn = (xf * r * g_ref[...]).astype(BF16)
        xn_ref[...] = xn
        for j in range(N // tn):
            cols = slice(j * tn, (j + 1) * tn)
            y = _dot_nt(xn, w_ref[cols, :]) if wt else _dot(xn, w_ref[:, cols])
            y_ref[:, cols] = y.astype(out_dtype)

    body, dep_specs, dep_args = _dep(kernel_body, 3, dep)
    return pl.pallas_call(
        body, grid=(T // tt,),
        in_specs=[pl.BlockSpec((tt, K), lambda i: (i, 0)), _full((1, K)), _full(w.shape)] + dep_specs,
        out_specs=[pl.BlockSpec((tt, N), lambda i: (i, 0)), pl.BlockSpec((tt, K), lambda i: (i, 0))],
        out_shape=[jax.ShapeDtypeStruct((T, N), out_dtype), jax.ShapeDtypeStruct((T, K), BF16)],
        compiler_params=_cp("parallel"), name=name)(x, g, w, *dep_args)


def _mm_res(res, a1, a2, w, *, tt, name):
    T, K1 = a1.shape
    K2 = a2.shape[1]
    N = w.shape[1]

    def body(r_ref, a1_ref, a2_ref, w_ref, o_ref):
        o_ref[...] = r_ref[...] + _dot(a1_ref[...], w_ref[:K1, :]) + _dot(a2_ref[...], w_ref[K1:, :])

    return pl.pallas_call(
        body, grid=(T // tt,),
        in_specs=[pl.BlockSpec((tt, N), lambda i: (i, 0)), pl.BlockSpec((tt, K1), lambda i: (i, 0)),
                  pl.BlockSpec((tt, K2), lambda i: (i, 0)), _full((K1 + K2, N))],
        out_specs=pl.BlockSpec((tt, N), lambda i: (i, 0)),
        out_shape=jax.ShapeDtypeStruct((T, N), F32),
        compiler_params=_cp("parallel"), name=name)(res, a1, a2, w)


def _swiglu_down(h, gu, wd, *, tt, name):
    T, D = h.shape
    Fh = wd.shape[0]

    def body(h_ref, gt_ref, up_ref, w_ref, o_ref):
        gt = gt_ref[...].astype(F32)
        act = gt * _sigmoid(gt) * up_ref[...].astype(F32)
        o_ref[...] = h_ref[...] + _dot(act, w_ref[...])

    return pl.pallas_call(
        body, grid=(T // tt,),
        in_specs=[pl.BlockSpec((tt, D), lambda i: (i, 0)), pl.BlockSpec((tt, Fh), lambda i: (i, 0)),
                  pl.BlockSpec((tt, Fh), lambda i: (i, 1)), _full((Fh, D))],
        out_specs=pl.BlockSpec((tt, D), lambda i: (i, 0)),
        out_shape=jax.ShapeDtypeStruct((T, D), F32),
        compiler_params=_cp("parallel"), name=name)(h, gu, gu, wd)


def _swiglu_bwd(dh, gu, wd, *, tt, name):
    T, D = dh.shape
    Fh = wd.shape[0]
    last = T // tt - 1

    def body(dh_ref, gt_ref, up_ref, w_ref, dgu_ref, gw_ref, acc):
        @pl.when(pl.program_id(0) == 0)
        def _():
            acc[...] = jnp.zeros_like(acc)

        gt = gt_ref[...].astype(F32)
        up = up_ref[...].astype(F32)
        s = _sigmoid(gt)
        silu = gt * s
        dh16 = dh_ref[...].astype(BF16)
        dact = _dot_nt(dh16, w_ref[...])
        acc[...] += _dot_tn((silu * up).astype(BF16), dh16)
        dgu_ref[:, :Fh] = (dact * up * (s * (1.0 + gt * (1.0 - s)))).astype(BF16)
        dgu_ref[:, Fh:] = (dact * silu).astype(BF16)

        @pl.when(pl.program_id(0) == last)
        def _():
            gw_ref[...] = acc[...].astype(BF16)

    return pl.pallas_call(
        body, grid=(T // tt,),
        in_specs=[pl.BlockSpec((tt, D), lambda i: (i, 0)), pl.BlockSpec((tt, Fh), lambda i: (i, 0)),
                  pl.BlockSpec((tt, Fh), lambda i: (i, 1)), _full((Fh, D))],
        out_specs=[pl.BlockSpec((tt, 2 * Fh), lambda i: (i, 0)), _full((Fh, D))],
        out_shape=[jax.ShapeDtypeStruct((T, 2 * Fh), BF16), jax.ShapeDtypeStruct((Fh, D), BF16)],
        scratch_shapes=[pltpu.VMEM((Fh, D), F32)],
        compiler_params=_cp("arbitrary"), name=name)(dh, gu, gu, wd)


def _out_proj_bwd(dy, a1, a2, w, *, tt, name):
    T, N = dy.shape
    K1, K2 = a1.shape[1], a2.shape[1]
    K = K1 + K2
    last = T // tt - 1

    def body(dy_ref, a1_ref, a2_ref, w_ref, da_ref, gw_ref, acc):
        @pl.when(pl.program_id(0) == 0)
        def _():
            acc[...] = jnp.zeros_like(acc)

        dy16 = dy_ref[...].astype(BF16)
        da_ref[...] = _dot_nt(dy16, w_ref[...])
        acc[:K1, :] += _dot_tn(a1_ref[...], dy16)
        acc[K1:, :] += _dot_tn(a2_ref[...], dy16)

        @pl.when(pl.program_id(0) == last)
        def _():
            gw_ref[...] = acc[...].astype(BF16)

    return pl.pallas_call(
        body, grid=(T // tt,),
        in_specs=[pl.BlockSpec((tt, N), lambda i: (i, 0)), pl.BlockSpec((tt, K1), lambda i: (i, 0)),
                  pl.BlockSpec((tt, K2), lambda i: (i, 0)), _full((K, N))],
        out_specs=[pl.BlockSpec((tt, K), lambda i: (i, 0)), _full((K, N))],
        out_shape=[jax.ShapeDtypeStruct((T, K), F32), jax.ShapeDtypeStruct((K, N), BF16)],
        scratch_shapes=[pltpu.VMEM((K, N), F32)],
        compiler_params=_cp("arbitrary"), name=name)(dy, a1, a2, w)


def _mm_tn(a, b, *, tt, tka, name):
    T, Ka = a.shape
    N = b.shape[1]
    last = T // tt - 1

    def body(a_ref, b_ref, o_ref, acc):
        @pl.when(pl.program_id(1) == 0)
        def _():
            acc[...] = jnp.zeros_like(acc)

        acc[...] += _dot_tn(a_ref[...], b_ref[...])

        @pl.when(pl.program_id(1) == last)
        def _():
            o_ref[...] = acc[...].astype(BF16)

    return pl.pallas_call(
        body, grid=(Ka // tka, T // tt),
        in_specs=[pl.BlockSpec((tt, tka), lambda j, t: (t, j)), pl.BlockSpec((tt, N), lambda j, t: (t, 0))],
        out_specs=pl.BlockSpec((tka, N), lambda j, t: (j, 0)),
        out_shape=jax.ShapeDtypeStruct((Ka, N), BF16),
        scratch_shapes=[pltpu.VMEM((tka, N), F32)],
        compiler_params=_cp("parallel", "arbitrary"), name=name)(a, b)


def _mm_tn_pieces(pieces, b, *, tt, name):
    n = len(pieces)
    T = b.shape[0]
    N = b.shape[1]
    widths = [p.shape[1] for p in pieces]
    Ka = sum(widths)
    last = T // tt - 1

    def body(*refs):
        p_refs = refs[:n]
        b_ref, o_ref, acc = refs[n:]

        @pl.when(pl.program_id(0) == 0)
        def _():
            acc[...] = jnp.zeros_like(acc)

        bv = b_ref[...].astype(BF16)
        off = 0
        for p_ref, wd in zip(p_refs, widths):
            acc[off:off + wd, :] += _dot_tn(p_ref[...], bv)
            off += wd

        @pl.when(pl.program_id(0) == last)
        def _():
            o_ref[...] = acc[...].astype(BF16)

    return pl.pallas_call(
        body, grid=(T // tt,),
        in_specs=[pl.BlockSpec((tt, wd), lambda t: (t, 0)) for wd in widths] + [pl.BlockSpec((tt, N), lambda t: (t, 0))],
        out_specs=_full((Ka, N)), out_shape=jax.ShapeDtypeStruct((Ka, N), BF16),
        scratch_shapes=[pltpu.VMEM((Ka, N), F32)],
        compiler_params=_cp("arbitrary"), name=name)(*pieces, b)


def _rms_bwd_dx(x, g, w, dy, dres, *, tt, wt, name, dep=None):
    pieces = list(dy) if isinstance(dy, (list, tuple)) else [dy]
    n = len(pieces)
    widths = [p.shape[1] for p in pieces]
    T, K = x.shape

    def kernel_body(x_ref, g_ref, w_ref, *rest):
        dy_refs = rest[:n]
        dres_ref, dx_ref, dg_ref = rest[n:]

        @pl.when(pl.program_id(0) == 0)
        def _():
            dg_ref[...] = jnp.zeros_like(dg_ref)

        if n == 1:
            dxn = (_dot if wt else _dot_nt)(dy_refs[0][...], w_ref[...])
        else:
            dxn, off = 0.0, 0
            for dy_ref, wd in zip(dy_refs, widths):
                dxn = dxn + _dot(dy_ref[...], w_ref[off:off + wd, :])
                off += wd
        xf = x_ref[...]
        r = lax.rsqrt(jnp.mean(xf * xf, axis=-1, keepdims=True) + EPS)
        xhat = xf * r
        dg_ref[...] += jnp.sum(dxn * xhat, axis=0, keepdims=True)
        dxhat = dxn * g_ref[...]
        dx_ref[...] = dres_ref[...] + r * (dxhat - xhat * jnp.mean(dxhat * xhat, axis=-1, keepdims=True))

    assert n == 1 or wt
    body, dep_specs, dep_args = _dep(kernel_body, 4 + n, dep)
    return pl.pallas_call(
        body, grid=(T // tt,),
        in_specs=[pl.BlockSpec((tt, K), lambda i: (i, 0)), _full((1, K)), _full(w.shape)]
        + [pl.BlockSpec((tt, wd), lambda i: (i, 0)) for wd in widths]
        + [pl.BlockSpec((tt, K), lambda i: (i, 0))] + dep_specs,
        out_specs=[pl.BlockSpec((tt, K), lambda i: (i, 0)), _full((1, K))],
        out_shape=[jax.ShapeDtypeStruct((T, K), F32), jax.ShapeDtypeStruct((1, K), F32)],
        compiler_params=_cp("arbitrary"), name=name)(x, g, w, *pieces, dres, *dep_args)


def _loss_kernel(y, tgt, *, tt, name):
    T, D = y.shape

    def body(y_ref, t_ref, dy_ref, acc_ref):
        @pl.when(pl.program_id(0) == 0)
        def _():
            acc_ref[...] = jnp.zeros_like(acc_ref)

        e = y_ref[...] - t_ref[...]
        dy_ref[...] = e * (1.0 / D)
        acc_ref[...] += jnp.sum(e * e, axis=0, keepdims=True)

    return pl.pallas_call(
        body, grid=(T // tt,),
        in_specs=[pl.BlockSpec((tt, D), lambda i: (i, 0)), pl.BlockSpec((tt, D), lambda i: (i, 0))],
        out_specs=[pl.BlockSpec((tt, D), lambda i: (i, 0)), _full((1, D))],
        out_shape=[jax.ShapeDtypeStruct((T, D), F32), jax.ShapeDtypeStruct((1, D), F32)],
        compiler_params=_cp("arbitrary"), name=name)(y, tgt)


HGRN_TB = 512
HGRN_NCH = HGRN_TB // CHUNK
HGRN_HPB = 6


def _hgrn_chunk_fwd(q, z, lbv, tril01):
    sig = _sigmoid(z)
    f = lbv + (1.0 - lbv) * sig
    kk = 1.0 - f
    b = _dot3(tril01, jnp.log(f))
    bend = b[CHUNK - 1:CHUNK, :]
    sq = _sigmoid(q)
    eb = jnp.exp(b)
    emb = jnp.exp(-b)
    eo = jnp.exp(bend - b)
    dec = jnp.exp(bend)
    return sig, f, kk, sq, eb, emb, eo, dec


def _hgrn2_fwd(proj, lb, *, name):
    T = proj.shape[0]
    nT = T // HGRN_TB
    nC = T // CHUNK

    def body(q_ref, z_ref, v_ref, lb_ref, o_ref, st_ref, state):
        @pl.when(pl.program_id(1) == 0)
        def _():
            state[...] = jnp.zeros_like(state)

        row = lax.broadcasted_iota(jnp.int32, (CHUNK, CHUNK), 0)
        col = lax.broadcasted_iota(jnp.int32, (CHUNK, CHUNK), 1)
        causal = row >= col
        tril01 = causal.astype(BF16)

        def chunk(c, carry):
            rows = pl.ds(pl.multiple_of(c * CHUNK, CHUNK), CHUNK)
            for hh in range(HGRN_HPB):
                sl = slice(hh * HEAD_DIM, (hh + 1) * HEAD_DIM)
                q = q_ref[rows, sl]
                v = v_ref[rows, sl].astype(BF16)
                sig, f, kk, sq, eb, emb, eo, dec = _hgrn_chunk_fwd(q, z_ref[rows, sl], lb_ref[:, sl], tril01)
                qi = (q * sq * eb).astype(BF16)
                ki = (kk * emb).astype(BF16)
                ko = (kk * eo).astype(BF16)
                st = state[hh]
                att = jnp.where(causal, _dot_nt(qi, ki), 0.0)
                o_ref[rows, sl] = _dot(att, v) + _dot_nt(qi, st)
                st_ref[c, hh] = st
                state[hh] = st * dec + _dot_tn(v, ko)
            return carry

        lax.fori_loop(0, HGRN_NCH, chunk, 0)

    W = HGRN_HPB * HEAD_DIM
    nG = A_HEADS // HGRN_HPB
    hb = lambda off: pl.BlockSpec((HGRN_TB, W), lambda h, i: (i, off + h))
    return pl.pallas_call(
        body, grid=(nG, nT),
        in_specs=[hb(0), hb(nG), hb(2 * nG), pl.BlockSpec((1, W), lambda h, i: (0, h))],
        out_specs=[hb(0), pl.BlockSpec((HGRN_NCH, HGRN_HPB, HEAD_DIM, HEAD_DIM), lambda h, i: (i, h, 0, 0))],
        out_shape=[jax.ShapeDtypeStruct((T, A_WIDTH), F32), jax.ShapeDtypeStruct((nC, A_HEADS, HEAD_DIM, HEAD_DIM), F32)],
        scratch_shapes=[pltpu.VMEM((HGRN_HPB, HEAD_DIM, HEAD_DIM), F32)],
        compiler_params=_cp("parallel", "arbitrary"), name=name)(proj, proj, proj, lb)


def _hgrn2_bwd(proj, lb, st_all, do, *, name):
    T = proj.shape[0]
    nT = T // HGRN_TB

    def body(q_ref, z_ref, v_ref, lb_ref, st_ref, do_ref, dq_ref, dz_ref, dv_ref, dlb_ref, dstate):
        @pl.when(pl.program_id(1) == 0)
        def _():
            dstate[...] = jnp.zeros_like(dstate)
            dlb_ref[...] = jnp.zeros_like(dlb_ref)

        row = lax.broadcasted_iota(jnp.int32, (CHUNK, CHUNK), 0)
        col = lax.broadcasted_iota(jnp.int32, (CHUNK, CHUNK), 1)
        causal = row >= col
        tril01 = causal.astype(BF16)
        triu01 = (row <= col).astype(BF16)

        def chunk(cc, carry):
            c = HGRN_NCH - 1 - cc
            rows = pl.ds(pl.multiple_of(c * CHUNK, CHUNK), CHUNK)
            for hh in range(HGRN_HPB):
                sl = slice(hh * HEAD_DIM, (hh + 1) * HEAD_DIM)
                lbv = lb_ref[:, sl]
                q = q_ref[rows, sl]
                v = v_ref[rows, sl].astype(BF16)
                sig, f, kk, sq, eb, emb, eo, dec = _hgrn_chunk_fwd(q, z_ref[rows, sl], lbv, tril01)
                qi32 = q * sq * eb
                ki32 = kk * emb
                ko32 = kk * eo
                qi, ki, ko = qi32.astype(BF16), ki32.astype(BF16), ko32.astype(BF16)
                att = jnp.where(causal, _dot_nt(qi, ki), 0.0).astype(BF16)
                dout = do_ref[rows, sl].astype(BF16)
                st = st_ref[c, hh]
                dst = dstate[hh]
                dst16 = dst.astype(BF16)
                datt = jnp.where(causal, _dot_nt(dout, v), 0.0).astype(BF16)
                dqi = _dot(datt, ki) + _dot(dout, st)
                dki = _dot_tn(datt, qi)
                dv_ref[rows, sl] = (_dot_tn(att, dout) + _dot_nt(ko, dst16)).astype(BF16)
                dko = _dot(v, dst16)
                ddec = jnp.sum(dst * st, axis=0, keepdims=True)
                dstate[hh] = dst * dec + _dot_tn(dout, qi)
                dkk = dki * emb + dko * eo
                db = dqi * qi32 - dki * ki32 - dko * ko32
                dbend = jnp.sum(dko * ko32, axis=0, keepdims=True) + ddec * dec
                dlogf = _dot3(triu01, db) + dbend
                df = dlogf / f - dkk
                dz_ref[rows, sl] = (df * (1.0 - lbv) * sig * (1.0 - sig)).astype(BF16)
                dlb_ref[:, sl] += jnp.sum(df * (1.0 - sig), axis=0, keepdims=True)
                dq_ref[rows, sl] = (dqi * eb * (sq * (1.0 + q * (1.0 - sq)))).astype(BF16)
            return carry

        lax.fori_loop(0, HGRN_NCH, chunk, 0)

    W = HGRN_HPB * HEAD_DIM
    nG = A_HEADS // HGRN_HPB
    hb = lambda off: pl.BlockSpec((HGRN_TB, W), lambda h, i: (nT - 1 - i, off + h))
    hlb = pl.BlockSpec((1, W), lambda h, i: (0, h))
    o16 = jax.ShapeDtypeStruct((T, A_WIDTH), BF16)
    return pl.pallas_call(
        body, grid=(nG, nT),
        in_specs=[hb(0), hb(nG), hb(2 * nG), hlb,
                  pl.BlockSpec((HGRN_NCH, HGRN_HPB, HEAD_DIM, HEAD_DIM), lambda h, i: (nT - 1 - i, h, 0, 0)), hb(0)],
        out_specs=[hb(0), hb(0), hb(0), hlb],
        out_shape=[o16, o16, o16, jax.ShapeDtypeStruct((1, A_WIDTH), F32)],
        scratch_shapes=[pltpu.VMEM((HGRN_HPB, HEAD_DIM, HEAD_DIM), F32)],
        compiler_params=_cp("parallel", "arbitrary"), name=name)(proj, proj, proj, lb, st_all, do)


def _head_rms(x):
    r = lax.rsqrt(jnp.mean(x * x, axis=-1, keepdims=True) + EPS)
    return x * r, r


def _head_rms_bwd(dxhat, xhat, r):
    return r * (dxhat - xhat * jnp.mean(dxhat * xhat, axis=-1, keepdims=True))


def _a_post_fwd(o, proj, onorm, *, tt, name):
    T = o.shape[0]

    def body(o_ref, g_ref, w_ref, y_ref):
        for h in range(A_HEADS):
            sl = slice(h * HEAD_DIM, (h + 1) * HEAD_DIM)
            xhat, _ = _head_rms(o_ref[:, sl])
            g = g_ref[:, sl]
            y_ref[:, sl] = xhat * w_ref[:, sl] * (g * _sigmoid(g))

    blk = lambda c: pl.BlockSpec((tt, A_WIDTH), lambda i: (i, c))
    return pl.pallas_call(
        body, grid=(T // tt,), in_specs=[blk(0), blk(3), _full((1, A_WIDTH))], out_specs=blk(0),
        out_shape=jax.ShapeDtypeStruct((T, A_WIDTH), F32),
        compiler_params=_cp("parallel"), name=name)(o, proj, onorm)


def _a_post_bwd(o, proj, onorm, dmix, *, tt, name, dep=None):
    T = o.shape[0]

    def kernel_body(o_ref, g_ref, w_ref, dy_ref, do_ref, dg_ref, dw_ref):
        @pl.when(pl.program_id(0) == 0)
        def _():
            dw_ref[...] = jnp.zeros_like(dw_ref)

        for h in range(A_HEADS):
            sl = slice(h * HEAD_DIM, (h + 1) * HEAD_DIM)
            xhat, r = _head_rms(o_ref[:, sl])
            g = g_ref[:, sl]
            s = _sigmoid(g)
            dy = dy_ref[:, sl]
            w = w_ref[:, sl]
            dg_ref[:, sl] = (dy * xhat * w * (s * (1.0 + g * (1.0 - s)))).astype(BF16)
            dyn = dy * (g * s)
            dw_ref[:, sl] += jnp.sum(dyn * xhat, axis=0, keepdims=True)
            do_ref[:, sl] = _head_rms_bwd(dyn * w, xhat, r)

    blk = lambda c: pl.BlockSpec((tt, A_WIDTH), lambda i: (i, c))
    body, dep_specs, dep_args = _dep(kernel_body, 4, dep)
    return pl.pallas_call(
        body, grid=(T // tt,), in_specs=[blk(0), blk(3), _full((1, A_WIDTH)), blk(0)] + dep_specs,
        out_specs=[blk(0), blk(0), _full((1, A_WIDTH))],
        out_shape=[jax.ShapeDtypeStruct((T, A_WIDTH), F32), jax.ShapeDtypeStruct((T, A_WIDTH), BF16),
                   jax.ShapeDtypeStruct((1, A_WIDTH), F32)],
        compiler_params=_cp("arbitrary"), name=name)(o, proj, onorm, dmix, *dep_args)


def _mem_head_masks(n):
    lane = lax.broadcasted_iota(jnp.int32, (n, MEM_WIDTH), 1)
    return [(lane >= m * MEM_HEAD_DIM) & (lane < (m + 1) * MEM_HEAD_DIM) for m in range(MEM_HEADS)]


def _mem_head_rms(x, masks):
    x2 = x * x
    r = jnp.zeros_like(x)
    for mk in masks:
        ms = jnp.sum(jnp.where(mk, x2, 0.0), axis=-1, keepdims=True) * (1.0 / MEM_HEAD_DIM)
        r = jnp.where(mk, lax.rsqrt(ms + EPS), r)
    return x * r, r


def _mem_head_rms_bwd(dxhat, xhat, r, masks):
    t = dxhat * xhat
    m = jnp.zeros_like(t)
    for mk in masks:
        m = jnp.where(mk, jnp.sum(jnp.where(mk, t, 0.0), axis=-1, keepdims=True) * (1.0 / MEM_HEAD_DIM), m)
    return r * (dxhat - xhat * m)


MEM_SCALE = MEM_HEAD_DIM ** -0.5


def _mem_attn_fwd(proj, qcol, mkv, qn_w, kn_w, *, tt, name):
    T = proj.shape[0]

    def body(q_ref, k_ref, v_ref, qw_ref, kw_ref, o_ref):
        qmasks = _mem_head_masks(tt)
        kmasks = _mem_head_masks(MEM_TOKENS)
        qhat, _ = _mem_head_rms(q_ref[...], qmasks)
        qn = qhat * qw_ref[...]
        khat, _ = _mem_head_rms(k_ref[...], kmasks)
        kn = (khat * kw_ref[...]).astype(BF16)
        v = v_ref[...].astype(BF16)
        out = jnp.zeros((tt, MEM_WIDTH), F32)
        for m in range(MEM_HEADS):
            s = _dot_nt(jnp.where(qmasks[m], qn, 0.0), kn) * MEM_SCALE
            s = s - jnp.max(s, axis=-1, keepdims=True)
            p = jnp.exp(s)
            p = p / jnp.sum(p, axis=-1, keepdims=True)
            out = jnp.where(qmasks[m], _dot(p, v), out)
        o_ref[...] = out

    return pl.pallas_call(
        body, grid=(T // tt,),
        in_specs=[pl.BlockSpec((tt, MEM_WIDTH), lambda i: (i, qcol)), pl.BlockSpec((MEM_TOKENS, MEM_WIDTH), lambda i: (0, 0)),
                  pl.BlockSpec((MEM_TOKENS, MEM_WIDTH), lambda i: (0, 1)), _full((1, MEM_WIDTH)), _full((1, MEM_WIDTH))],
        out_specs=pl.BlockSpec((tt, MEM_WIDTH), lambda i: (i, 0)),
        out_shape=jax.ShapeDtypeStruct((T, MEM_WIDTH), F32),
        compiler_params=_cp("parallel"), name=name)(proj, mkv, mkv, qn_w, kn_w)


def _mem_attn_bwd(proj, qcol, mkv, qn_w, kn_w, dmix, *, tt, name):
    T = proj.shape[0]
    nsteps = T // tt
    ocol = (dmix.shape[1] - MEM_WIDTH) // MEM_WIDTH

    def body(q_ref, k_ref, v_ref, qw_ref, kw_ref, do_ref, dq_ref, dkv_ref, dqw_ref, dkw_ref, dk_acc, dv_acc):
        step = pl.program_id(0)

        @pl.when(step == 0)
        def _():
            dk_acc[...] = jnp.zeros_like(dk_acc)
            dv_acc[...] = jnp.zeros_like(dv_acc)
            dqw_ref[...] = jnp.zeros_like(dqw_ref)

        qmasks = _mem_head_masks(tt)
        kmasks = _mem_head_masks(MEM_TOKENS)
        qhat, qr = _mem_head_rms(q_ref[...], qmasks)
        qn = qhat * qw_ref[...]
        khat, kr = _mem_head_rms(k_ref[...], kmasks)
        kn = (khat * kw_ref[...]).astype(BF16)
        v = v_ref[...].astype(BF16)
        dout = do_ref[...]
        dqn = jnp.zeros((tt, MEM_WIDTH), F32)
        dkn = jnp.zeros((MEM_TOKENS, MEM_WIDTH), F32)
        dvv = jnp.zeros((MEM_TOKENS, MEM_WIDTH), F32)
        for m in range(MEM_HEADS):
            qm = jnp.where(qmasks[m], qn, 0.0).astype(BF16)
            s = _dot_nt(qm, kn) * MEM_SCALE
            s = s - jnp.max(s, axis=-1, keepdims=True)
            p = jnp.exp(s)
            p = p / jnp.sum(p, axis=-1, keepdims=True)
            dom = jnp.where(qmasks[m], dout, 0.0).astype(BF16)
            dp = _dot_nt(dom, v)
            ds = (p * (dp - jnp.sum(p * dp, axis=-1, keepdims=True)) * MEM_SCALE).astype(BF16)
            dqn = jnp.where(qmasks[m], _dot(ds, kn), dqn)
            dkn = jnp.where(kmasks[m], _dot_tn(ds, qm), dkn)
            dvv = jnp.where(kmasks[m], _dot_tn(p, dom), dvv)
        dqw_ref[...] += jnp.sum(dqn * qhat, axis=0, keepdims=True)
        dq_ref[...] = _mem_head_rms_bwd(dqn * qw_ref[...], qhat, qr, qmasks).astype(BF16)
        dk_acc[...] += dkn
        dv_acc[...] += dvv

        @pl.when(step == nsteps - 1)
        def _():
            dk = dk_acc[...]
            dkw_ref[...] = jnp.sum(dk * khat, axis=0, keepdims=True)
            dkv_ref[:, :MEM_WIDTH] = _mem_head_rms_bwd(dk * kw_ref[...], khat, kr, kmasks)
            dkv_ref[:, MEM_WIDTH:] = dv_acc[...]

    return pl.pallas_call(
        body, grid=(nsteps,),
        in_specs=[pl.BlockSpec((tt, MEM_WIDTH), lambda i: (i, qcol)), pl.BlockSpec((MEM_TOKENS, MEM_WIDTH), lambda i: (0, 0)),
                  pl.BlockSpec((MEM_TOKENS, MEM_WIDTH), lambda i: (0, 1)), _full((1, MEM_WIDTH)), _full((1, MEM_WIDTH)),
                  pl.BlockSpec((tt, MEM_WIDTH), lambda i: (i, ocol))],
        out_specs=[pl.BlockSpec((tt, MEM_WIDTH), lambda i: (i, 0)), _full((MEM_TOKENS, 2 * MEM_WIDTH)),
                   _full((1, MEM_WIDTH)), _full((1, MEM_WIDTH))],
        out_shape=[jax.ShapeDtypeStruct((T, MEM_WIDTH), BF16), jax.ShapeDtypeStruct((MEM_TOKENS, 2 * MEM_WIDTH), F32),
                   jax.ShapeDtypeStruct((1, MEM_WIDTH), F32), jax.ShapeDtypeStruct((1, MEM_WIDTH), F32)],
        scratch_shapes=[pltpu.VMEM((MEM_TOKENS, MEM_WIDTH), F32), pltpu.VMEM((MEM_TOKENS, MEM_WIDTH), F32)],
        compiler_params=_cp("arbitrary"), name=name)(proj, mkv, mkv, qn_w, kn_w, dmix)


HALF = HEAD_DIM // 2
ATT_SCALE = HEAD_DIM ** -0.5
NEG = -1e30


def _rope_tables(T):
    inv = ROPE_THETA ** (-jnp.arange(HALF, dtype=F32) / HALF)
    ang = jnp.arange(T, dtype=F32)[:, None] * inv[None, :]
    cos, sin = jnp.cos(ang), jnp.sin(ang)
    return jnp.concatenate([cos, cos], axis=-1), jnp.concatenate([-sin, sin], axis=-1)


def _rope(x, cosf, sinsg):
    return x * cosf + pltpu.roll(x, HALF, 1) * sinsg


def _rope_bwd(dy, cosf, sinsg):
    return dy * cosf + pltpu.roll(dy * sinsg, HALF, 1)


def _headnorm_rope_fwd(x, w_heads, cosf, sinsg, *, col0, n_heads, tt, name):
    T = x.shape[0]
    W = n_heads * HEAD_DIM

    def body(x_ref, w_ref, c_ref, s_ref, y_ref):
        c, s = c_ref[...], s_ref[...]
        for h in range(n_heads):
            sl = slice(h * HEAD_DIM, (h + 1) * HEAD_DIM)
            xhat, _ = _head_rms(x_ref[:, sl])
            y_ref[:, sl] = _rope(xhat * w_ref[:, sl], c, s)

    tbl = pl.BlockSpec((tt, HEAD_DIM), lambda i: (i, 0))
    return pl.pallas_call(
        body, grid=(T // tt,),
        in_specs=[pl.BlockSpec((tt, W), lambda i: (i, col0)), _full((1, W)), tbl, tbl],
        out_specs=pl.BlockSpec((tt, W), lambda i: (i, 0)),
        out_shape=jax.ShapeDtypeStruct((T, W), F32),
        compiler_params=_cp("parallel"), name=name)(x, w_heads, cosf, sinsg)


def _q_prep_bwd(proj, w_heads, cosf, sinsg, dqs, *, tt, name):
    T = proj.shape[0]
    W = N_GROUPS * B_WIDTH

    def body(x_ref, w_ref, c_ref, s_ref, d0, d1, d2, dx_ref, dw_ref):
        @pl.when(pl.program_id(0) == 0)
        def _():
            dw_ref[...] = jnp.zeros_like(dw_ref)

        c, s = c_ref[...], s_ref[...]
        for gi, d_ref in enumerate((d0, d1, d2)):
            for h in range(B_HEADS):
                sl = slice((gi * B_HEADS + h) * HEAD_DIM, (gi * B_HEADS + h + 1) * HEAD_DIM)
                xhat, r = _head_rms(x_ref[:, sl])
                dyn = _rope_bwd(d_ref[:, h * HEAD_DIM:(h + 1) * HEAD_DIM], c, s)
                dw_ref[:, sl] += jnp.sum(dyn * xhat, axis=0, keepdims=True)
                dx_ref[:, sl] = _head_rms_bwd(dyn * w_ref[:, sl], xhat, r).astype(BF16)

    tbl = pl.BlockSpec((tt, HEAD_DIM), lambda i: (i, 0))
    dyb = pl.BlockSpec((tt, B_WIDTH), lambda i: (i, 0))
    return pl.pallas_call(
        body, grid=(T // tt,),
        in_specs=[pl.BlockSpec((tt, W), lambda i: (i, 0)), _full((1, W)), tbl, tbl, dyb, dyb, dyb],
        out_specs=[pl.BlockSpec((tt, W), lambda i: (i, 0)), _full((1, W))],
        out_shape=[jax.ShapeDtypeStruct((T, W), BF16), jax.ShapeDtypeStruct((1, W), F32)],
        compiler_params=_cp("arbitrary"), name=name)(proj, w_heads, cosf, sinsg, *dqs)


def _kv_prep_bwd(kv, w_heads, cosf, sinsg, dks, dvs, *, tt, name):
    T = kv.shape[0]

    def body(x_ref, w_ref, c_ref, s_ref, k0, k1, k2, v0, v1, v2, dx_ref, dw_ref):
        @pl.when(pl.program_id(0) == 0)
        def _():
            dw_ref[...] = jnp.zeros_like(dw_ref)

        c, s = c_ref[...], s_ref[...]
        for h in range(B_HEADS):
            sl = slice(h * HEAD_DIM, (h + 1) * HEAD_DIM)
            vs = slice(B_WIDTH + h * HEAD_DIM, B_WIDTH + (h + 1) * HEAD_DIM)
            xhat, r = _head_rms(x_ref[:, sl])
            dyn = _rope_bwd(k0[:, sl] + k1[:, sl] + k2[:, sl], c, s)
            dw_ref[:, sl] += jnp.sum(dyn * xhat, axis=0, keepdims=True)
            dx_ref[:, sl] = _head_rms_bwd(dyn * w_ref[:, sl], xhat, r).astype(BF16)
            dx_ref[:, vs] = (v0[:, sl] + v1[:, sl] + v2[:, sl]).astype(BF16)

    tbl = pl.BlockSpec((tt, HEAD_DIM), lambda i: (i, 0))
    dyb = pl.BlockSpec((tt, B_WIDTH), lambda i: (i, 0))
    return pl.pallas_call(
        body, grid=(T // tt,),
        in_specs=[dyb, _full((1, B_WIDTH)), tbl, tbl] + [dyb] * 6,
        out_specs=[pl.BlockSpec((tt, 2 * B_WIDTH), lambda i: (i, 0)), _full((1, B_WIDTH))],
        out_shape=[jax.ShapeDtypeStruct((T, 2 * B_WIDTH), BF16), jax.ShapeDtypeStruct((1, B_WIDTH), F32)],
        compiler_params=_cp("arbitrary"), name=name)(kv, w_heads, cosf, sinsg, *dks, *dvs)


def _band_masks(n_is_first=None):
    row = lax.broadcasted_iota(jnp.int32, (SPAN, SPAN), 0)
    col = lax.broadcasted_iota(jnp.int32, (SPAN, SPAN), 1)
    return row >= col, col >= row


def _dil_views(T, d):
    L = T // d
    return L, L // SPAN


def _dil_fwd(qr, kr, kv, gi, d, *, name):
    T = qr.shape[0]
    L, nb = _dil_views(T, d)

    def body(q_ref, kc_ref, kp_ref, vc_ref, vp_ref, o_ref, lse_ref):
        cur_ok, prev_band = _band_masks()
        prev_ok = prev_band & (pl.program_id(1) > 0)
        for h in range(B_HEADS):
            sl = slice(h * HEAD_DIM, (h + 1) * HEAD_DIM)
            q = q_ref[:, sl]
            sc = jnp.where(cur_ok, _dot_nt(q, kc_ref[:, sl]) * ATT_SCALE, NEG)
            sp = jnp.where(prev_ok, _dot_nt(q, kp_ref[:, sl]) * ATT_SCALE, NEG)
            m = jnp.maximum(jnp.max(sc, axis=-1, keepdims=True), jnp.max(sp, axis=-1, keepdims=True))
            pc = jnp.exp(sc - m)
            pp = jnp.exp(sp - m)
            l = jnp.sum(pc, axis=-1, keepdims=True) + jnp.sum(pp, axis=-1, keepdims=True)
            o_ref[:, sl] = (_dot(pc, vc_ref[:, sl]) + _dot(pp, vp_ref[:, sl])) / l
            lse_ref[:, sl] = jnp.broadcast_to(m + jnp.log(l), (SPAN, HEAD_DIM))

    blk = lambda f: pl.BlockSpec((SPAN, B_WIDTH), f)
    cur = lambda r, n: (n, r)
    prev = lambda r, n: (jnp.maximum(n - 1, 0), r)
    ov = jax.ShapeDtypeStruct((L, d * B_WIDTH), F32)
    o, lse = pl.pallas_call(
        body, grid=(d, nb),
        in_specs=[blk(lambda r, n: (n, r * N_GROUPS + gi)), blk(cur), blk(prev),
                  blk(lambda r, n: (n, 2 * r + 1)), blk(lambda r, n: (jnp.maximum(n - 1, 0), 2 * r + 1))],
        out_specs=[blk(cur), blk(cur)], out_shape=[ov, ov],
        compiler_params=_cp("parallel", "arbitrary"), name=name,
    )(qr.reshape(L, d * N_GROUPS * B_WIDTH), kr.reshape(L, d * B_WIDTH), kr.reshape(L, d * B_WIDTH),
      kv.reshape(L, d * 2 * B_WIDTH), kv.reshape(L, d * 2 * B_WIDTH))
    return o.reshape(T, B_WIDTH), lse.reshape(T, B_WIDTH)


def _dil_combine_fwd(os_, lses, *, tt, name):
    T = os_[0].shape[0]

    def body(o0, o1, o2, l0, l1, l2, y_ref, lse_ref):
        a, b, c = l0[...], l1[...], l2[...]
        m = jnp.maximum(jnp.maximum(a, b), c)
        wa, wb, wc = jnp.exp(a - m), jnp.exp(b - m), jnp.exp(c - m)
        den = wa + wb + wc
        y_ref[...] = (wa * o0[...] + wb * o1[...] + wc * o2[...]) / den
        lse_ref[...] = m + jnp.log(den)

    blk = pl.BlockSpec((tt, B_WIDTH), lambda i: (i, 0))
    sh = jax.ShapeDtypeStruct((T, B_WIDTH), F32)
    return pl.pallas_call(
        body, grid=(T // tt,), in_specs=[blk] * 6, out_specs=[blk, blk], out_shape=[sh, sh],
        compiler_params=_cp("parallel"), name=name)(*os_, *lses)


def _dil_bwd_prep(dmix, mix_main, *, tt, name, dep=None):
    T = mix_main.shape[0]

    def kernel_body(dy_ref, y_ref, dd_ref):
        for h in range(B_HEADS):
            sl = slice(h * HEAD_DIM, (h + 1) * HEAD_DIM)
            dd_ref[:, sl] = jnp.broadcast_to(jnp.sum(dy_ref[:, sl] * y_ref[:, sl], axis=-1, keepdims=True), (tt, HEAD_DIM))

    blk = pl.BlockSpec((tt, B_WIDTH), lambda i: (i, 0))
    body, dep_specs, dep_args = _dep(kernel_body, 2, dep)
    return pl.pallas_call(
        body, grid=(T // tt,), in_specs=[blk, blk] + dep_specs, out_specs=blk,
        out_shape=jax.ShapeDtypeStruct((T, B_WIDTH), F32),
        compiler_params=_cp("parallel"), name=name)(dmix, mix_main, *dep_args)


DILS_UNROLL = 4


def _dils_specs(gi, d, nblk):
    blk = lambda f: pl.BlockSpec((SPAN * d, HEAD_DIM), f)
    return {
        "q": blk(lambda h, n: (n, gi * B_HEADS + h)), "q_next": blk(lambda h, n: (jnp.minimum(n + 1, nblk - 1), gi * B_HEADS + h)),
        "cur": blk(lambda h, n: (n, h)), "prev": blk(lambda h, n: (jnp.maximum(n - 1, 0), h)),
        "next": blk(lambda h, n: (jnp.minimum(n + 1, nblk - 1), h)),
        "v": blk(lambda h, n: (n, B_HEADS + h)), "v_prev": blk(lambda h, n: (jnp.maximum(n - 1, 0), B_HEADS + h)),
    }


def _dils_fwd(qr, kr, kv, gi, d, *, name):
    T = qr.shape[0]
    nblk = T // (SPAN * d)
    sp = _dils_specs(gi, d, nblk)

    def body(q_ref, kc_ref, kp_ref, vc_ref, vp_ref, o_ref, lse_ref):
        cur_ok, prev_band = _band_masks()
        prev_ok = prev_band & (pl.program_id(1) > 0)

        def residue(r, carry):
            rows = pl.ds(r, SPAN, stride=d)
            q = q_ref[rows, :]
            sc = jnp.where(cur_ok, _dot_nt(q, kc_ref[rows, :]) * ATT_SCALE, NEG)
            sp_ = jnp.where(prev_ok, _dot_nt(q, kp_ref[rows, :]) * ATT_SCALE, NEG)
            m = jnp.maximum(jnp.max(sc, axis=-1, keepdims=True), jnp.max(sp_, axis=-1, keepdims=True))
            pc = jnp.exp(sc - m)
            pp = jnp.exp(sp_ - m)
            l = jnp.sum(pc, axis=-1, keepdims=True) + jnp.sum(pp, axis=-1, keepdims=True)
            o_ref[rows, :] = (_dot(pc, vc_ref[rows, :]) + _dot(pp, vp_ref[rows, :])) / l
            lse_ref[rows, :] = jnp.broadcast_to(m + jnp.log(l), (SPAN, HEAD_DIM))
            return carry

        lax.fori_loop(0, d, residue, 0, unroll=DILS_UNROLL)

    sh = jax.ShapeDtypeStruct((T, B_WIDTH), F32)
    return pl.pallas_call(
        body, grid=(B_HEADS, nblk), in_specs=[sp["q"], sp["cur"], sp["prev"], sp["v"], sp["v_prev"]],
        out_specs=[sp["cur"], sp["cur"]], out_shape=[sh, sh],
        compiler_params=_cp("parallel", "arbitrary"), name=name)(qr, kr, kr, kv, kv)


DIL_BWD_GROUP = {1: 4, 4: 1, 16: 1}


def _dil_bwd(qr, kr, kv, dmix, lse, dd, gi, d, *, name):
    T = qr.shape[0]
    G = DIL_BWD_GROUP[d]
    band = SPAN * d
    tb = G * band
    nblk = T // tb

    def body(q_ref, dy_ref, lse_ref, dd_ref, kc_ref, kp_ref, vc_ref, vp_ref, dq_ref, dk_ref, dv_ref):
        n = pl.program_id(1)

        @pl.when(n == 0)
        def _():
            dk_ref[...] = jnp.zeros_like(dk_ref)
            dv_ref[...] = jnp.zeros_like(dv_ref)

        cur_ok, prev_band = _band_masks()
        base = pl.multiple_of(n * tb, SPAN)
        for j in range(G):
            def residue(r, carry, j=j):
                off = j * band + r
                rows = pl.ds(off, SPAN, stride=d)
                q, dy = q_ref[rows, :], dy_ref[rows, :]
                lse_h = jnp.max(lse_ref[rows, :], axis=-1, keepdims=True)
                dd_h = jnp.max(dd_ref[rows, :], axis=-1, keepdims=True)
                kc, vc = kc_ref[rows, :], vc_ref[rows, :]
                if j > 0:
                    before = pl.ds(off - band, SPAN, stride=d)
                    kp, vp = kc_ref[before, :], vc_ref[before, :]
                    prev_ok = prev_band
                else:
                    before = pl.ds((G - 1) * band + r, SPAN, stride=d)
                    kp, vp = kp_ref[before, :], vp_ref[before, :]
                    prev_ok = prev_band & (n > 0)
                pc = jnp.exp(jnp.where(cur_ok, _dot_nt(q, kc) * ATT_SCALE, NEG) - lse_h)
                pp = jnp.exp(jnp.where(prev_ok, _dot_nt(q, kp) * ATT_SCALE, NEG) - lse_h)
                dsc = pc * (_dot_nt(dy, vc) - dd_h) * ATT_SCALE
                dsp = pp * (_dot_nt(dy, vp) - dd_h) * ATT_SCALE
                dq_ref[rows, :] = _dot(dsc, kc) + _dot(dsp, kp)
                here = pl.ds(base + off, SPAN, stride=d)
                dk_ref[here, :] += _dot_tn(dsc, q)
                dv_ref[here, :] += _dot_tn(pc, dy)
                there = pl.ds(jnp.maximum(base + off - band, r), SPAN, stride=d)
                dk_ref[there, :] += _dot_tn(dsp, q)
                dv_ref[there, :] += _dot_tn(pp, dy)
                return carry

            lax.fori_loop(0, d, residue, 0, unroll=min(d, DILS_UNROLL))

    blk = lambda f: pl.BlockSpec((tb, HEAD_DIM), f)
    cur = lambda h, n: (n, h)
    prev = lambda h, n: (jnp.maximum(n - 1, 0), h)
    whole = pl.BlockSpec((T, HEAD_DIM), lambda h, n: (0, h))
    sh = jax.ShapeDtypeStruct((T, B_WIDTH), F32)
    return pl.pallas_call(
        body, grid=(B_HEADS, nblk),
        in_specs=[blk(lambda h, n: (n, gi * B_HEADS + h)), blk(cur), blk(cur), blk(cur), blk(cur), blk(prev),
                  blk(lambda h, n: (n, B_HEADS + h)), blk(lambda h, n: (jnp.maximum(n - 1, 0), B_HEADS + h))],
        out_specs=[blk(cur), whole, whole], out_shape=[sh, sh, sh],
        compiler_params=_cp("parallel", "arbitrary"), name=name)(qr, dmix, lse, dd, kr, kr, kv, kv)


A_MQ_COL = 4 * A_WIDTH // MEM_WIDTH
B_MQ_COL = N_GROUPS * B_WIDTH // MEM_WIDTH


def _row(v):
    return v.reshape(1, -1).astype(F32)


def _local_step(x, mem, tgt, get_w, P, put_g, first_dep=None):
    T = x.shape[0]
    cosf, sinsg = _rope_tables(T)
    lb_soft = jax.nn.softmax(P["a_lb_logits"].astype(F32), axis=0)
    lb = lb_soft[0:1]
    qw_heads = jnp.repeat(P["b_qnorm"][0], B_HEADS, axis=0).reshape(1, -1)
    kw_heads = jnp.tile(_row(P["b_knorm"]), (1, B_HEADS))
    mqw = [jnp.tile(_row(P["mem_qnorm"][l]), (1, MEM_HEADS)) for l in range(2)]
    mkw = [jnp.tile(_row(P["mem_knorm"][l]), (1, MEM_HEADS)) for l in range(2)]
    nmix = [_row(P["norm_mix"][l]) for l in range(2)]
    nffn = [_row(P["norm_ffn"][l]) for l in range(2)]
    mnorm = [_row(P["mem_norm"][l]) for l in range(2)]
    kvn = _row(P["kv_norm"])
    onorm = _row(P["a_onorm"])
    W = {}

    def w_of(name, after=None):
        if name not in W:
            W[name] = get_w(name, after)
        return W[name]

    proj_a, xn0 = _rms_matmul(x, nmix[0], w_of("a_w_in"), tt=512, tn=1664, wt=True, name="proj_a", dep=first_dep)
    mkv0, mn0 = _rms_matmul(mem, mnorm[0], w_of("w_mem_kv0"), tt=MEM_TOKENS, tn=2 * MEM_WIDTH, wt=False, name="mem_kv0")
    o_raw, st = _hgrn2_fwd(proj_a, lb, name="hgrn2_fwd")
    mm0 = _a_post_fwd(o_raw, proj_a, onorm, tt=512, name="a_post_fwd")
    mo0 = _mem_attn_fwd(proj_a, A_MQ_COL, mkv0, mqw[0], mkw[0], tt=512, name="mem_attn_fwd0")
    hm0 = _mm_res(x, mm0, mo0, w_of("w_out0", mo0), tt=512, name="out_proj0")
    gu0, hn0 = _rms_matmul(hm0, nffn[0], w_of("w_gate_up0", hm0), tt=512, tn=1408, wt=True, out_dtype=BF16, name="gate_up0")
    h1 = _swiglu_down(hm0, gu0, w_of("w_down0", gu0), tt=256, name="down0")
    kv, hkn = _rms_matmul(h1, kvn, w_of("w_kv", h1), tt=512, tn=768, wt=True, name="kv_proj")
    kr = _headnorm_rope_fwd(kv, kw_heads, cosf, sinsg, col0=0, n_heads=B_HEADS, tt=512, name="k_prep")

    proj_b, xn1 = _rms_matmul(h1, nmix[1], w_of("b_w_in", kr), tt=512, tn=1280, wt=True, name="proj_b")
    mkv1, mn1 = _rms_matmul(mem, mnorm[1], w_of("w_mem_kv1", kr), tt=MEM_TOKENS, tn=2 * MEM_WIDTH, wt=False, name="mem_kv1")
    qr = _headnorm_rope_fwd(proj_b, qw_heads, cosf, sinsg, col0=0, n_heads=N_GROUPS * B_HEADS, tt=512, name="q_prep")
    outs = [(_dil_fwd if d == 1 else _dils_fwd)(qr, kr, kv, gi, d, name=f"dil_fwd{gi}") for gi, d in enumerate(DILATIONS)]
    mm1, lse_tot = _dil_combine_fwd([o for o, _ in outs], [s for _, s in outs], tt=512, name="dil_combine")
    mo1 = _mem_attn_fwd(proj_b, B_MQ_COL, mkv1, mqw[1], mkw[1], tt=512, name="mem_attn_fwd1")
    hm1 = _mm_res(h1, mm1, mo1, w_of("w_out1", mo1), tt=512, name="out_proj1")
    gu1, hn1 = _rms_matmul(hm1, nffn[1], w_of("w_gate_up1", hm1), tt=512, tn=1408, wt=True, out_dtype=BF16, name="gate_up1")
    y = _swiglu_down(hm1, gu1, w_of("w_down1", gu1), tt=256, name="down1")
    dy, sq = _loss_kernel(y, tgt, tt=512, name="loss")

    gP = {}
    zeros_mem = jnp.zeros((MEM_TOKENS, D_MODEL), F32)

    def ffn_bwd(l, dh, hm, gu, hn):
        dgu, g_wd = _swiglu_bwd(dh, gu, w_of(f"w_down{l}"), tt=256, name=f"swiglu_bwd{l}")
        g_wgu = _mm_tn(dgu, hn, tt=512, tka=1408, name=f"g_w_gate_up{l}")
        sent = put_g({f"w_down{l}": g_wd, f"w_gate_up{l}": g_wgu})
        dhm, g_nf = _rms_bwd_dx(hm, nffn[l], w_of(f"w_gate_up{l}"), dgu, dh, tt=256, wt=True, name=f"gate_up_bwd{l}", dep=sent)
        return dhm, g_nf

    def mix_bwd(l, dhm, mix_main, mix_mem, proj, qcol, mkv, mn):
        dmix, g_wout = _out_proj_bwd(dhm, mix_main, mix_mem, w_of(f"w_out{l}"), tt=512, name=f"out_proj_bwd{l}")
        dmq, dmkv, dqw, dkw = _mem_attn_bwd(proj, qcol, mkv, mqw[l], mkw[l], dmix, tt=512, name=f"mem_attn_bwd{l}")
        g_wmkv = _mm_tn(mn, dmkv, tt=MEM_TOKENS, tka=512, name=f"g_w_mem_kv{l}")
        sent = put_g({f"w_out{l}": g_wout, f"w_mem_kv{l}": g_wmkv})
        _, g_mn = _rms_bwd_dx(mem, mnorm[l], w_of(f"w_mem_kv{l}"), dmkv, zeros_mem, tt=MEM_TOKENS, wt=False, name=f"mem_kv_bwd{l}")
        fold = lambda v: v.reshape(MEM_HEADS, MEM_HEAD_DIM).sum(axis=0)
        return dmix, dmq, g_mn, fold(dqw), fold(dkw), sent

    dhm1, g_nf1 = ffn_bwd(1, dy, hm1, gu1, hn1)
    dmix1, dmq1, g_mn1, g_mq1, g_mk1, sent = mix_bwd(1, dhm1, mm1, mo1, proj_b, B_MQ_COL, mkv1, mn1)
    dd = _dil_bwd_prep(dmix1, mm1, tt=512, name="dil_bwd_prep", dep=sent)
    dqs, dks, dvs = [], [], []
    for gi, d in enumerate(DILATIONS):
        dq_g, dk_g, dv_g = _dil_bwd(qr, kr, kv, dmix1, lse_tot, dd, gi, d, name=f"dil_bwd{gi}")
        dqs.append(dq_g)
        dks.append(dk_g)
        dvs.append(dv_g)
    dq_raw, dqw = _q_prep_bwd(proj_b, qw_heads, cosf, sinsg, dqs, tt=512, name="q_prep_bwd")
    dkv, dkw = _kv_prep_bwd(kv, kw_heads, cosf, sinsg, dks, dvs, tt=512, name="kv_prep_bwd")
    dproj_b = [dq_raw, dmq1]
    g_wb = _mm_tn_pieces(dproj_b, xn1, tt=512, name="g_b_w_in")
    g_wkv = _mm_tn(dkv, hkn, tt=512, tka=768, name="g_w_kv")
    sent = put_g({"b_w_in": g_wb, "w_kv": g_wkv})
    dh1, g_nm1 = _rms_bwd_dx(h1, nmix[1], w_of("b_w_in"), dproj_b, dhm1, tt=256, wt=True, name="proj_b_bwd", dep=sent)
    dh1, g_kvn = _rms_bwd_dx(h1, kvn, w_of("w_kv"), dkv, dh1, tt=256, wt=True, name="kv_proj_bwd")

    dhm0, g_nf0 = ffn_bwd(0, dh1, hm0, gu0, hn0)
    dmix0, dmq0, g_mn0, g_mq0, g_mk0, sent = mix_bwd(0, dhm0, mm0, mo0, proj_a, A_MQ_COL, mkv0, mn0)
    do_raw, dg, g_onorm = _a_post_bwd(o_raw, proj_a, onorm, dmix0, tt=512, name="a_post_bwd", dep=sent)
    dq, dz, dv, dlb = _hgrn2_bwd(proj_a, lb, st, do_raw, name="hgrn2_bwd")
    dproj_a = [dq, dz, dv, dg, dmq0]
    sent = put_g({"a_w_in": _mm_tn_pieces(dproj_a, xn0, tt=512, name="g_a_w_in")})
    gx, g_nm0 = _rms_bwd_dx(x, nmix[0], w_of("a_w_in"), dproj_a, dhm0, tt=256, wt=True, name="proj_a_bwd", dep=sent)

    dl0 = lb_soft[0:1] * lb_soft[1:2] * dlb
    gP["a_lb_logits"] = jnp.concatenate([dl0, -dl0], axis=0)
    gP["a_onorm"] = g_onorm
    gP["norm_mix"] = jnp.concatenate([g_nm0, g_nm1], axis=0)
    gP["norm_ffn"] = jnp.concatenate([g_nf0, g_nf1], axis=0)
    gP["b_qnorm"] = dqw.reshape(N_GROUPS, B_HEADS, HEAD_DIM).sum(axis=1)[None]
    gP["kv_norm"] = g_kvn.reshape(-1)
    gP["b_knorm"] = dkw.reshape(B_HEADS, HEAD_DIM).sum(axis=0)
    gP["mem_norm"] = jnp.concatenate([g_mn0, g_mn1], axis=0)
    gP["mem_qnorm"] = jnp.stack([g_mq0, g_mq1])
    gP["mem_knorm"] = jnp.stack([g_mk0, g_mk1])
    return sq, gx, gP


MESH_ID = pl.DeviceIdType.MESH
HBM_SPEC = pl.BlockSpec(memory_space=pltpu.HBM)


def _position():
    return lax.axis_index("x"), lax.axis_index("y"), lax.axis_index("c")


def _all_gather(blocks, *, name):
    n = len(blocks)

    def body(*refs):
        x_refs, out_refs = refs[:n], refs[n:2 * n]
        send_sems, recv_sems, local_sems = refs[2 * n:]
        x, y, c = _position()
        me, sibling = (x, y, c), (x, y, 1 - c)
        chips = [(1 - x, y), (x, 1 - y), (1 - x, 1 - y)]

        def slot(a, px, py, pc):
            return out_refs[a].at[4 * px + 2 * py + pc]

        def copy(a, k, blk, to, src=None):
            return pltpu.make_async_remote_copy(
                src_ref=slot(a, *blk) if src is None else src, dst_ref=slot(a, *blk),
                send_sem=send_sems.at[7 * a + k], recv_sem=recv_sems.at[7 * a + k], device_id=to, device_id_type=MESH_ID)

        mine = [pltpu.make_async_copy(x_refs[a], slot(a, *me), local_sems.at[a]) for a in range(n)]
        for cp in mine:
            cp.start()
        first = []
        for a in range(n):
            first.append(copy(a, 0, me, sibling, src=x_refs[a]))
            first += [copy(a, 1 + j, me, (*chip, c), src=x_refs[a]) for j, chip in enumerate(chips)]
        for cp in first:
            cp.start()
        passed = []
        for j, chip in enumerate(chips):
            for a in range(n):
                copy(a, 1 + j, (*chip, c), me).wait_recv()
                cp = copy(a, 4 + j, (*chip, c), sibling)
                cp.start()
                passed.append(cp)
        for a in range(n):
            copy(a, 0, sibling, me).wait_recv()
            for j, chip in enumerate(chips):
                copy(a, 4 + j, (*chip, 1 - c), me).wait_recv()
        for cp in first + passed:
            cp.wait_send()
        for cp in mine:
            cp.wait()

    return pl.pallas_call(
        body, out_shape=[jax.ShapeDtypeStruct((N_DEV,) + b.shape, b.dtype) for b in blocks],
        in_specs=[HBM_SPEC] * n, out_specs=[HBM_SPEC] * n,
        scratch_shapes=[pltpu.SemaphoreType.DMA((7 * n,)), pltpu.SemaphoreType.DMA((7 * n,)), pltpu.SemaphoreType.DMA((n,))],
        name=name)(*blocks)


SEM_SPEC = pl.BlockSpec(memory_space=pltpu.SEMAPHORE)
ANY_SPEC = pl.BlockSpec(memory_space=pl.ANY)
DATAFLOW = pltpu.SideEffectType.DATAFLOW_SIDE_EFFECTING


def _peer(k, x, y, c):
    return (1 - x if (k >> 2) & 1 else x, 1 - y if (k >> 1) & 1 else y, 1 - c if k & 1 else c)


def _own_slot_filled(own_block):
    x, y, c = _position()
    zone = lax.empty((N_DEV,) + own_block.shape, own_block.dtype)
    return lax.dynamic_update_slice_in_dim(zone, own_block[None], 4 * x + 2 * y + c, axis=0)


def _split_start(srcs, scatter, after, *, name):
    n = len(srcs)
    extra = [] if after is None else [after]
    x, y, c = _position()
    me = 4 * x + 2 * y + c
    lands = [_own_slot_filled(lax.dynamic_index_in_dim(s, me, 0, keepdims=False) if scatter else s) for s in srcs]

    def body(*refs):
        src_refs, land_refs = refs[:n], refs[n:2 * n]
        send_sems, recv_sems = refs[2 * n + len(extra)], refs[2 * n + len(extra) + 1]
        token = refs[-1]
        bx, by, bc = _position()
        bme = 4 * bx + 2 * by + bc
        for a in range(n):
            for k in range(1, N_DEV):
                tx, ty, tc = _peer(k, bx, by, bc)
                src = src_refs[a].at[4 * tx + 2 * ty + tc] if scatter else src_refs[a]
                pltpu.make_async_remote_copy(
                    src_ref=src, dst_ref=land_refs[a].at[bme],
                    send_sem=send_sems.at[7 * a + k - 1], recv_sem=recv_sems.at[7 * a + k - 1],
                    device_id=(tx, ty, tc), device_id_type=MESH_ID).start()
        token[...] = jnp.zeros_like(token)

    hbm = lambda a: pltpu.HBM(a.shape, a.dtype)
    outs = pl.pallas_call(
        body, name=name,
        out_shape=(pltpu.SemaphoreType.DMA((7 * n,)), pltpu.SemaphoreType.DMA((7 * n,)),
                   *[hbm(s) for s in srcs], *[hbm(l) for l in lands], jax.ShapeDtypeStruct((8, 128), F32)),
        in_specs=[HBM_SPEC] * (2 * n) + [ANY_SPEC] * len(extra),
        out_specs=(SEM_SPEC, SEM_SPEC, *[HBM_SPEC] * (2 * n), pl.BlockSpec(memory_space=pltpu.VMEM)),
        input_output_aliases={i: 2 + i for i in range(2 * n)},
        compiler_params=pltpu.CompilerParams(has_side_effects=DATAFLOW),
    )(*[pltpu.with_memory_space_constraint(s, pltpu.HBM) for s in srcs],
      *[pltpu.with_memory_space_constraint(l, pltpu.HBM) for l in lands], *extra)
    return {"n": n, "scatter": scatter, "send": outs[0], "recv": outs[1], "srcs": outs[2:2 + n],
            "lands": outs[2 + n:2 + 2 * n], "token": outs[-1]}


def _split_wait(handle, after, *, name):
    n, scatter = handle["n"], handle["scatter"]

    def body(*refs):
        src_refs, land_refs = refs[:n], refs[n:2 * n]
        send_sems, recv_sems = refs[2 * n], refs[2 * n + 1]
        bx, by, bc = _position()
        for a in range(n):
            for k in range(1, N_DEV):
                src = src_refs[a].at[0] if scatter else src_refs[a]
                cp = pltpu.make_async_remote_copy(
                    src_ref=src, dst_ref=land_refs[a].at[0],
                    send_sem=send_sems.at[7 * a + k - 1], recv_sem=recv_sems.at[7 * a + k - 1],
                    device_id=_peer(k, bx, by, bc), device_id_type=MESH_ID)
                cp.wait_send()
                cp.wait_recv()

    hbm = lambda a: pltpu.HBM(a.shape, a.dtype)
    outs = pl.pallas_call(
        body, name=name,
        out_shape=(*[hbm(s) for s in handle["srcs"]], *[hbm(l) for l in handle["lands"]]),
        in_specs=[HBM_SPEC] * (2 * n) + [SEM_SPEC, SEM_SPEC, ANY_SPEC],
        out_specs=tuple([HBM_SPEC] * (2 * n)),
        input_output_aliases={i: i for i in range(2 * n)},
        compiler_params=pltpu.CompilerParams(has_side_effects=DATAFLOW),
    )(*handle["srcs"], *handle["lands"], handle["send"], handle["recv"], after)
    return list(outs[n:])


def _sum_sources(parts, *, tr, name):
    n, R, C = parts.shape

    def body(p_ref, o_ref):
        acc = p_ref[0].astype(F32)
        for s in range(1, n):
            acc = acc + p_ref[s].astype(F32)
        o_ref[...] = acc

    return pl.pallas_call(
        body, grid=(R // tr,), in_specs=[pl.BlockSpec((n, tr, C), lambda i: (0, i, 0))],
        out_specs=pl.BlockSpec((tr, C), lambda i: (i, 0)),
        out_shape=jax.ShapeDtypeStruct((R, C), F32), compiler_params=_cp("parallel"), name=name)(parts)


def _adamw_math(g, w, m, v):
    c1 = 1.0 - ADAM_B1 ** ADAM_STEP
    c2 = 1.0 - ADAM_B2 ** ADAM_STEP
    nm = ADAM_B1 * m + (1.0 - ADAM_B1) * g
    nv = ADAM_B2 * v + (1.0 - ADAM_B2) * (g * g)
    return -ADAM_LR * ((nm / c1) / (jnp.sqrt(nv / c2) + ADAM_EPS) + ADAM_WD * w), nm, nv


def _reduce_adamw(received, w, m, v, *, col, tr, name):
    L, R, C = w.shape

    def body(*refs):
        p_refs = refs[:L]
        w_ref, m_ref, v_ref, g_ref, d_ref, nm_ref, nv_ref = refs[L:]
        for l in range(L):
            @pl.when(pl.program_id(0) == l)
            def _(p_ref=p_refs[l]):
                acc = p_ref[0].astype(F32)
                for s in range(1, N_DEV):
                    acc = acc + p_ref[s].astype(F32)
                g = acc.T if col else acc
                g_ref[...] = g
                d_ref[...], nm_ref[...], nv_ref[...] = _adamw_math(g, w_ref[...], m_ref[...], v_ref[...])

    p_spec = (pl.BlockSpec((N_DEV, C, tr), lambda l, i: (0, 0, i)) if col
              else pl.BlockSpec((N_DEV, tr, C), lambda l, i: (0, i, 0)))
    blk = pl.BlockSpec((None, tr, C), lambda l, i: (l, i, 0))
    sh = jax.ShapeDtypeStruct((L, R, C), F32)
    return pl.pallas_call(
        body, grid=(L, R // tr), in_specs=[p_spec] * L + [blk] * 3, out_specs=[blk] * 4, out_shape=[sh] * 4,
        compiler_params=_cp("parallel", "parallel"), name=name)(*received, w, m, v)


def _adamw(g, w, m, v, *, tr, name):
    L, R, C = w.shape

    def body(g_ref, w_ref, m_ref, v_ref, d_ref, nm_ref, nv_ref):
        d_ref[...], nm_ref[...], nv_ref[...] = _adamw_math(g_ref[...], w_ref[...], m_ref[...], v_ref[...])

    blk = pl.BlockSpec((None, tr, C), lambda l, i: (l, i, 0))
    sh = jax.ShapeDtypeStruct((L, R, C), F32)
    return pl.pallas_call(
        body, grid=(L, R // tr), in_specs=[blk] * 4, out_specs=[blk] * 3, out_shape=[sh] * 3,
        compiler_params=_cp("parallel", "parallel"), name=name)(g, w, m, v)


UNITS = {
    "a_w_in": ("a_w_in", 0, True), "w_mem_kv0": ("w_mem_kv", 0, False), "w_out0": ("w_out", 0, False),
    "w_gate_up0": ("w_gate_up", 0, True), "w_down0": ("w_down", 0, False), "w_kv": ("w_kv", None, True),
    "b_w_in": ("b_w_in", 0, True), "w_mem_kv1": ("w_mem_kv", 1, False), "w_out1": ("w_out", 1, False),
    "w_gate_up1": ("w_gate_up", 1, True), "w_down1": ("w_down", 1, False),
}
BIG = ("a_w_in", "b_w_in", "w_kv", "w_mem_kv", "w_out", "w_gate_up", "w_down")
ADAMW_ROW_TILE = {"a_w_in": 256, "b_w_in": 256, "w_kv": 256, "w_mem_kv": 128, "w_out": 128, "w_gate_up": 176, "w_down": 176}
TRANSPOSED_UPDATE = ("w_gate_up",)


def _wire_block(weights, unit):
    name, layer, col = UNITS[unit]
    a = weights[name] if layer is None else weights[name][layer]
    return (a.T if col else a).astype(BF16)


SMALL_REPLICATED = ("norm_mix", "norm_ffn", "b_qnorm", "kv_norm", "b_knorm", "mem_norm", "mem_qnorm", "mem_knorm")
SMALL_SHARDED = ("a_lb_logits", "a_onorm")
SMALL_ORDER = SMALL_REPLICATED + SMALL_SHARDED
LANES = 128


def _prod(shape):
    n = 1
    for s in shape:
        n *= s
    return n


def _pack_flat(arrays, rows, cols, dtype):
    flat = jnp.concatenate([a.reshape(-1).astype(dtype) for a in arrays])
    return jnp.pad(flat, (0, rows * cols - flat.shape[0])).reshape(rows, cols)


def _unpack_flat(packed, shapes):
    flat = packed.reshape(-1)
    out, off = [], 0
    for s in shapes:
        out.append(flat[off:off + _prod(s)].reshape(s))
        off += _prod(s)
    return out


def kernel(x, mem, norm_mix, norm_ffn, a_w_in, a_lb_logits, a_onorm, b_w_in, b_qnorm, kv_norm, w_kv, b_knorm, mem_norm, w_mem_kv, mem_qnorm, mem_knorm, w_out, w_gate_up, w_down, loss_target, m_norm_mix, m_norm_ffn, m_a_w_in, m_a_lb_logits, m_a_onorm, m_b_w_in, m_b_qnorm, m_kv_norm, m_w_kv, m_b_knorm, m_mem_norm, m_w_mem_kv, m_mem_qnorm, m_mem_knorm, m_w_out, m_w_gate_up, m_w_down, v_norm_mix, v_norm_ffn, v_a_w_in, v_a_lb_logits, v_a_onorm, v_b_w_in, v_b_qnorm, v_kv_norm, v_w_kv, v_b_knorm, v_mem_norm, v_w_mem_kv, v_mem_qnorm, v_mem_knorm, v_w_out, v_w_gate_up, v_w_down):
    names = ("norm_mix", "norm_ffn", "a_w_in", "a_lb_logits", "a_onorm", "b_w_in", "b_qnorm", "kv_norm", "w_kv", "b_knorm",
             "mem_norm", "w_mem_kv", "mem_qnorm", "mem_knorm", "w_out", "w_gate_up", "w_down")
    w = dict(zip(names, (norm_mix, norm_ffn, a_w_in, a_lb_logits, a_onorm, b_w_in, b_qnorm, kv_norm, w_kv, b_knorm,
                         mem_norm, w_mem_kv, mem_qnorm, mem_knorm, w_out, w_gate_up, w_down)))
    m = dict(zip(names, (m_norm_mix, m_norm_ffn, m_a_w_in, m_a_lb_logits, m_a_onorm, m_b_w_in, m_b_qnorm, m_kv_norm, m_w_kv,
                         m_b_knorm, m_mem_norm, m_w_mem_kv, m_mem_qnorm, m_mem_knorm, m_w_out, m_w_gate_up, m_w_down)))
    v = dict(zip(names, (v_norm_mix, v_norm_ffn, v_a_w_in, v_a_lb_logits, v_a_onorm, v_b_w_in, v_b_qnorm, v_kv_norm, v_w_kv,
                         v_b_knorm, v_mem_norm, v_w_mem_kv, v_mem_qnorm, v_mem_knorm, v_w_out, v_w_gate_up, v_w_down)))

    first = ["a_w_in", "w_mem_kv0"]
    gathered = _all_gather([_wire_block(w, u) for u in first] + [_pack_flat([a_lb_logits, a_onorm], 8, LANES, F32)],
                           name="gather_first")
    full = {u: g.reshape(-1, g.shape[-1]) for u, g in zip(first, gathered)}
    small_in = gathered[-1].reshape(N_DEV, -1)
    P = {n: w[n] for n in SMALL_REPLICATED}
    P["a_lb_logits"] = small_in[:, :192].reshape(N_DEV, 2, 96).transpose(1, 0, 2).reshape(2, A_WIDTH)
    P["a_onorm"] = small_in[:, 192:288].reshape(1, A_WIDTH)
    later = [["w_out0", "w_gate_up0"], ["w_down0", "w_kv"], ["b_w_in", "w_mem_kv1"], ["w_out1", "w_gate_up1", "w_down1"]]
    pending = {}
    token = gathered[-1]
    for i, group in enumerate(later):
        handle = _split_start([_wire_block(w, u) for u in group], False, token, name=f"gather{i}_start")
        token = handle["token"]
        for u in group:
            pending[u] = (i, group, handle)

    def get_w(unit, after):
        if unit not in full:
            i, group, handle = pending[unit]
            for u, land in zip(group, _split_wait(handle, after, name=f"gather{i}_wait")):
                full[u] = land.reshape(-1, land.shape[-1])
        return full[unit]

    sent = []

    def put_g(group):
        units = list(group)
        handle = _split_start([group[u].reshape(N_DEV, -1, group[u].shape[-1]) for u in units], True, None,
                              name=f"scatter{len(sent)}_start")
        sent.append((units, handle))
        return handle["token"]

    sq, gx, gP = _local_step(x[0], mem[0], loss_target[0], get_w, P, put_g, first_dep=token)
    loss_here = (0.5 * jnp.sum(sq) / D_MODEL).reshape(1)

    received = {}
    for i, (units, handle) in enumerate(sent):
        received.update(zip(units, _split_wait(handle, gx, name=f"scatter{i}_wait")))
    out = {"grad": {}, "delta": {}, "new_m": {}, "new_v": {}}
    for n in BIG:
        shape = w[n].shape
        as3 = lambda a: a.reshape((-1,) + shape[-2:])
        mine = [u for u, (wn, _, _) in UNITS.items() if wn == n]
        col = UNITS[mine[0]][2]
        flip = (lambda a: jnp.swapaxes(a, 1, 2)) if n in TRANSPOSED_UPDATE else (lambda a: a)
        res = _reduce_adamw([received[u] for u in mine], flip(as3(w[n])), flip(as3(m[n])), flip(as3(v[n])),
                            col=col and n not in TRANSPOSED_UPDATE, tr=ADAMW_ROW_TILE[n], name=f"adamw_{n}")
        for kind, r in zip(("grad", "delta", "new_m", "new_v"), res):
            out[kind][n] = flip(r).reshape(shape)

    full_shapes = [(2, A_WIDTH) if n == "a_lb_logits" else (1, A_WIDTH) if n == "a_onorm" else w[n].shape for n in SMALL_ORDER]
    n_small = sum(_prod(s) for s in full_shapes) + 1
    rows_small = -(-n_small // (8 * LANES)) * 8
    g_all, = _all_gather([_pack_flat([gP[n] for n in SMALL_ORDER] + [loss_here], rows_small, LANES, F32)],
                         name="gather_small_grads")
    summed = _unpack_flat(_sum_sources(g_all, tr=rows_small, name="sum_small_grads"), full_shapes + [(1,)])
    g_small = dict(zip(SMALL_ORDER, summed))
    loss = summed[-1].reshape(())
    me = 4 * lax.axis_index("x") + 2 * lax.axis_index("y") + lax.axis_index("c")
    for n in SMALL_SHARDED:
        g_small[n] = lax.dynamic_slice_in_dim(g_small[n], me * 96, 96, axis=1)
    shapes = [w[n].shape for n in SMALL_ORDER]
    rows_upd = -(-sum(_prod(s) for s in shapes) // (8 * LANES)) * 8
    pk = lambda d: _pack_flat([d[n] for n in SMALL_ORDER], rows_upd, LANES, F32)
    res = _adamw(pk(g_small)[None], pk(w)[None], pk(m)[None], pk(v)[None], tr=rows_upd, name="adamw_small")
    out["grad"].update(g_small)
    for kind, packed in zip(("delta", "new_m", "new_v"), res):
        out[kind].update(zip(SMALL_ORDER, _unpack_flat(packed[0], shapes)))

    return (loss, gx[None], *[out["grad"][n] for n in names], *[out["delta"][n] for n in names],
            *[out["new_m"][n] for n in names], *[out["new_v"][n] for n in names])
```

```python
import functools

import jax
import jax.numpy as jnp
from jax import lax
from jax.experimental import pallas as pl
from jax.experimental.pallas import tpu as pltpu

F32 = jnp.float32
BF16 = jnp.bfloat16

N_DEV = 8
D_MODEL = 1024
HEAD_DIM = 128
A_HEADS = 6
A_WIDTH = A_HEADS * HEAD_DIM
CHUNK = 64
B_HEADS = 6
B_WIDTH = B_HEADS * HEAD_DIM
DILATIONS = (1, 4, 16)
SPAN = 128
N_GROUPS = 3
ROPE_THETA = 10000.0
MEM_TOKENS = 256
MEM_HEADS = 4
MEM_HEAD_DIM = 64
MEM_WIDTH = MEM_HEADS * MEM_HEAD_DIM
FFN_HIDDEN = 2816
EPS = 1e-6

ADAM_LR = 0.001
ADAM_B1 = 0.9
ADAM_B2 = 0.999
ADAM_EPS = 1e-08
ADAM_WD = 0.01
ADAM_STEP = 10

V7X_VMEM_LIMIT_BYTES = 56 * 1024 * 1024

NT_DIMS = (((1,), (1,)), ((), ()))
TN_DIMS = (((0,), (0,)), ((), ()))


def _cp(*sem):
    return pltpu.CompilerParams(dimension_semantics=sem, vmem_limit_bytes=V7X_VMEM_LIMIT_BYTES)


def _dot(a, b):
    return jnp.dot(a.astype(BF16), b.astype(BF16), preferred_element_type=F32)


def _dot_nt(a, b):
    return lax.dot_general(a.astype(BF16), b.astype(BF16), NT_DIMS, preferred_element_type=F32)


def _dot_tn(a, b):
    return lax.dot_general(a.astype(BF16), b.astype(BF16), TN_DIMS, preferred_element_type=F32)


def _dot3(m01, x):
    hi = x.astype(BF16)
    r1 = x - hi.astype(F32)
    mid = r1.astype(BF16)
    lo = (r1 - mid.astype(F32)).astype(BF16)
    d = functools.partial(jnp.dot, preferred_element_type=F32)
    return d(m01, hi) + d(m01, mid) + d(m01, lo)


def _sigmoid(x):
    return 1.0 / (1.0 + jnp.exp(-x))


def _full(shape):
    return pl.BlockSpec(shape, lambda *_: (0,) * len(shape))


def _dep(body, n_in, dep):
    if dep is None:
        return body, [], []

    def with_dep(*refs):
        return body(*refs[:n_in], *refs[n_in + 1:])

    return with_dep, [pl.BlockSpec(memory_space=pl.ANY)], [dep]


def _rms_matmul(x, g, w, *, tt, tn, wt, name, out_dtype=F32, dep=None):
    T, K = x.shape
    N = w.shape[0] if wt else w.shape[1]

    def kernel_body(x_ref, g_ref, w_ref, y_ref, xn_ref):
        xf = x_ref[...]
        r = lax.rsqrt(jnp.mean(xf * xf, axis=-1, keepdims=True) + EPS)
        xn = (xf * r * g_ref[...]).astype(BF16)
        xn_ref[...] = xn
        for j in range(N // tn):
            cols = slice(j * tn, (j + 1) * tn)
            y = _dot_nt(xn, w_ref[cols, :]) if wt else _dot(xn, w_ref[:, cols])
            y_ref[:, cols] = y.astype(out_dtype)

    body, dep_specs, dep_args = _dep(kernel_body, 3, dep)
    return pl.pallas_call(
        body, grid=(T // tt,),
        in_specs=[pl.BlockSpec((tt, K), lambda i: (i, 0)), _full((1, K)), _full(w.shape)] + dep_specs,
        out_specs=[pl.BlockSpec((tt, N), lambda i: (i, 0)), pl.BlockSpec((tt, K), lambda i: (i, 0))],
        out_shape=[jax.ShapeDtypeStruct((T, N), out_dtype), jax.ShapeDtypeStruct((T, K), BF16)],
        compiler_params=_cp("parallel"), name=name)(x, g, w, *dep_args)


def _mm_res(res, a1, a2, w, *, tt, name):
    T, K1 = a1.shape
    K2 = a2.shape[1]
    N = w.shape[1]

    def body(r_ref, a1_ref, a2_ref, w_ref, o_ref):
        o_ref[...] = r_ref[...] + _dot(a1_ref[...], w_ref[:K1, :]) + _dot(a2_ref[...], w_ref[K1:, :])

    return pl.pallas_call(
        body, grid=(T // tt,),
        in_specs=[pl.BlockSpec((tt, N), lambda i: (i, 0)), pl.BlockSpec((tt, K1), lambda i: (i, 0)),
                  pl.BlockSpec((tt, K2), lambda i: (i, 0)), _full((K1 + K2, N))],
        out_specs=pl.BlockSpec((tt, N), lambda i: (i, 0)),
        out_shape=jax.ShapeDtypeStruct((T, N), F32),
        compiler_params=_cp("parallel"), name=name)(res, a1, a2, w)


def _swiglu_down(h, gu, wd, *, tt, name):
    T, D = h.shape
    Fh = wd.shape[0]

    def body(h_ref, gt_ref, up_ref, w_ref, o_ref):
        gt = gt_ref[...].astype(F32)
        act = gt * _sigmoid(gt) * up_ref[...].astype(F32)
        o_ref[...] = h_ref[...] + _dot(act, w_ref[...])

    return pl.pallas_call(
        body, grid=(T // tt,),
        in_specs=[pl.BlockSpec((tt, D), lambda i: (i, 0)), pl.BlockSpec((tt, Fh), lambda i: (i, 0)),
                  pl.BlockSpec((tt, Fh), lambda i: (i, 1)), _full((Fh, D))],
        out_specs=pl.BlockSpec((tt, D), lambda i: (i, 0)),
        out_shape=jax.ShapeDtypeStruct((T, D), F32),
        compiler_params=_cp("parallel"), name=name)(h, gu, gu, wd)


def _swiglu_bwd(dh, gu, wd, *, tt, name):
    T, D = dh.shape
    Fh = wd.shape[0]
    last = T // tt - 1

    def body(dh_ref, gt_ref, up_ref, w_ref, dgu_ref, gw_ref, acc):
        @pl.when(pl.program_id(0) == 0)
        def _():
            acc[...] = jnp.zeros_like(acc)

        gt = gt_ref[...].astype(F32)
        up = up_ref[...].astype(F32)
        s = _sigmoid(gt)
        silu = gt * s
        dh16 = dh_ref[...].astype(BF16)
        dact = _dot_nt(dh16, w_ref[...])
        acc[...] += _dot_tn((silu * up).astype(BF16), dh16)
        dgu_ref[:, :Fh] = (dact * up * (s * (1.0 + gt * (1.0 - s)))).astype(BF16)
        dgu_ref[:, Fh:] = (dact * silu).astype(BF16)

        @pl.when(pl.program_id(0) == last)
        def _():
            gw_ref[...] = acc[...].astype(BF16)

    return pl.pallas_call(
        body, grid=(T // tt,),
        in_specs=[pl.BlockSpec((tt, D), lambda i: (i, 0)), pl.BlockSpec((tt, Fh), lambda i: (i, 0)),
                  pl.BlockSpec((tt, Fh), lambda i: (i, 1)), _full((Fh, D))],
        out_specs=[pl.BlockSpec((tt, 2 * Fh), lambda i: (i, 0)), _full((Fh, D))],
        out_shape=[jax.ShapeDtypeStruct((T, 2 * Fh), BF16), jax.ShapeDtypeStruct((Fh, D), BF16)],
        scratch_shapes=[pltpu.VMEM((Fh, D), F32)],
        compiler_params=_cp("arbitrary"), name=name)(dh, gu, gu, wd)


def _out_proj_bwd(dy, a1, a2, w, *, tt, name):
    T, N = dy.shape
    K1, K2 = a1.shape[1], a2.shape[1]
    K = K1 + K2
    last = T // tt - 1

    def body(dy_ref, a1_ref, a2_ref, w_ref, da_ref, gw_ref, acc):
        @pl.when(pl.program_id(0) == 0)
        def _():
            acc[...] = jnp.zeros_like(acc)

        dy16 = dy_ref[...].astype(BF16)
        da_ref[...] = _dot_nt(dy16, w_ref[...])
        acc[:K1, :] += _dot_tn(a1_ref[...], dy16)
        acc[K1:, :] += _dot_tn(a2_ref[...], dy16)

        @pl.when(pl.program_id(0) == last)
        def _():
            gw_ref[...] = acc[...].astype(BF16)

    return pl.pallas_call(
        body, grid=(T // tt,),
        in_specs=[pl.BlockSpec((tt, N), lambda i: (i, 0)), pl.BlockSpec((tt, K1), lambda i: (i, 0)),
                  pl.BlockSpec((tt, K2), lambda i: (i, 0)), _full((K, N))],
        out_specs=[pl.BlockSpec((tt, K), lambda i: (i, 0)), _full((K, N))],
        out_shape=[jax.ShapeDtypeStruct((T, K), F32), jax.ShapeDtypeStruct((K, N), BF16)],
        scratch_shapes=[pltpu.VMEM((K, N), F32)],
        compiler_params=_cp("arbitrary"), name=name)(dy, a1, a2, w)


def _mm_tn(a, b, *, tt, tka, name):
    T, Ka = a.shape
    N = b.shape[1]
    last = T // tt - 1

    def body(a_ref, b_ref, o_ref, acc):
        @pl.when(pl.program_id(1) == 0)
        def _():
            acc[...] = jnp.zeros_like(acc)

        acc[...] += _dot_tn(a_ref[...], b_ref[...])

        @pl.when(pl.program_id(1) == last)
        def _():
            o_ref[...] = acc[...].astype(BF16)

    return pl.pallas_call(
        body, grid=(Ka // tka, T // tt),
        in_specs=[pl.BlockSpec((tt, tka), lambda j, t: (t, j)), pl.BlockSpec((tt, N), lambda j, t: (t, 0))],
        out_specs=pl.BlockSpec((tka, N), lambda j, t: (j, 0)),
        out_shape=jax.ShapeDtypeStruct((Ka, N), BF16),
        scratch_shapes=[pltpu.VMEM((tka, N), F32)],
        compiler_params=_cp("parallel", "arbitrary"), name=name)(a, b)


def _mm_tn_pieces(pieces, b, *, tt, name):
    n = len(pieces)
    T = b.shape[0]
    N = b.shape[1]
    widths = [p.shape[1] for p in pieces]
    Ka = sum(widths)
    last = T // tt - 1

    def body(*refs):
        p_refs = refs[:n]
        b_ref, o_ref, acc = refs[n:]

        @pl.when(pl.program_id(0) == 0)
        def _():
            acc[...] = jnp.zeros_like(acc)

        bv = b_ref[...].astype(BF16)
        off = 0
        for p_ref, wd in zip(p_refs, widths):
            acc[off:off + wd, :] += _dot_tn(p_ref[...], bv)
            off += wd

        @pl.when(pl.program_id(0) == last)
        def _():
            o_ref[...] = acc[...].astype(BF16)

    return pl.pallas_call(
        body, grid=(T // tt,),
        in_specs=[pl.BlockSpec((tt, wd), lambda t: (t, 0)) for wd in widths] + [pl.BlockSpec((tt, N), lambda t: (t, 0))],
        out_specs=_full((Ka, N)), out_shape=jax.ShapeDtypeStruct((Ka, N), BF16),
        scratch_shapes=[pltpu.VMEM((Ka, N), F32)],
        compiler_params=_cp("arbitrary"), name=name)(*pieces, b)


def _rms_bwd_dx(x, g, w, dy, dres, *, tt, wt, name, dep=None):
    pieces = list(dy) if isinstance(dy, (list, tuple)) else [dy]
    n = len(pieces)
    widths = [p.shape[1] for p in pieces]
    T, K = x.shape

    def kernel_body(x_ref, g_ref, w_ref, *rest):
        dy_refs = rest[:n]
        dres_ref, dx_ref, dg_ref = rest[n:]

        @pl.when(pl.program_id(0) == 0)
        def _():
            dg_ref[...] = jnp.zeros_like(dg_ref)

        if n == 1:
            dxn = (_dot if wt else _dot_nt)(dy_refs[0][...], w_ref[...])
        else:
            dxn, off = 0.0, 0
            for dy_ref, wd in zip(dy_refs, widths):
                dxn = dxn + _dot(dy_ref[...], w_ref[off:off + wd, :])
                off += wd
        xf = x_ref[...]
        r = lax.rsqrt(jnp.mean(xf * xf, axis=-1, keepdims=True) + EPS)
        xhat = xf * r
        dg_ref[...] += jnp.sum(dxn * xhat, axis=0, keepdims=True)
        dxhat = dxn * g_ref[...]
        dx_ref[...] = dres_ref[...] + r * (dxhat - xhat * jnp.mean(dxhat * xhat, axis=-1, keepdims=True))

    assert n == 1 or wt
    body, dep_specs, dep_args = _dep(kernel_body, 4 + n, dep)
    return pl.pallas_call(
        body, grid=(T // tt,),
        in_specs=[pl.BlockSpec((tt, K), lambda i: (i, 0)), _full((1, K)), _full(w.shape)]
        + [pl.BlockSpec((tt, wd), lambda i: (i, 0)) for wd in widths]
        + [pl.BlockSpec((tt, K), lambda i: (i, 0))] + dep_specs,
        out_specs=[pl.BlockSpec((tt, K), lambda i: (i, 0)), _full((1, K))],
        out_shape=[jax.ShapeDtypeStruct((T, K), F32), jax.ShapeDtypeStruct((1, K), F32)],
        compiler_params=_cp("arbitrary"), name=name)(x, g, w, *pieces, dres, *dep_args)


def _loss_kernel(y, tgt, *, tt, name):
    T, D = y.shape

    def body(y_ref, t_ref, dy_ref, acc_ref):
        @pl.when(pl.program_id(0) == 0)
        def _():
            acc_ref[...] = jnp.zeros_like(acc_ref)

        e = y_ref[...] - t_ref[...]
        dy_ref[...] = e * (1.0 / D)
        acc_ref[...] += jnp.sum(e * e, axis=0, keepdims=True)

    return pl.pallas_call(
        body, grid=(T // tt,),
        in_specs=[pl.BlockSpec((tt, D), lambda i: (i, 0)), pl.BlockSpec((tt, D), lambda i: (i, 0))],
        out_specs=[pl.BlockSpec((tt, D), lambda i: (i, 0)), _full((1, D))],
        out_shape=[jax.ShapeDtypeStruct((T, D), F32), jax.ShapeDtypeStruct((1, D), F32)],
        compiler_params=_cp("arbitrary"), name=name)(y, tgt)


HGRN_TB = 512
HGRN_NCH = HGRN_TB // CHUNK
HGRN_HPB = 6


def _hgrn_chunk_fwd(q, z, lbv, tril01):
    sig = _sigmoid(z)
    f = lbv + (1.0 - lbv) * sig
    kk = 1.0 - f
    b = _dot3(tril01, jnp.log(f))
    bend = b[CHUNK - 1:CHUNK, :]
    sq = _sigmoid(q)
    eb = jnp.exp(b)
    emb = jnp.exp(-b)
    eo = jnp.exp(bend - b)
    dec = jnp.exp(bend)
    return sig, f, kk, sq, eb, emb, eo, dec


def _hgrn2_fwd(proj, lb, *, name):
    T = proj.shape[0]
    nT = T // HGRN_TB
    nC = T // CHUNK

    def body(q_ref, z_ref, v_ref, lb_ref, o_ref, st_ref, state):
        @pl.when(pl.program_id(1) == 0)
        def _():
            state[...] = jnp.zeros_like(state)

        row = lax.broadcasted_iota(jnp.int32, (CHUNK, CHUNK), 0)
        col = lax.broadcasted_iota(jnp.int32, (CHUNK, CHUNK), 1)
        causal = row >= col
        tril01 = causal.astype(BF16)

        def chunk(c, carry):
            rows = pl.ds(pl.multiple_of(c * CHUNK, CHUNK), CHUNK)
            for hh in range(HGRN_HPB):
                sl = slice(hh * HEAD_DIM, (hh + 1) * HEAD_DIM)
                q = q_ref[rows, sl]
                v = v_ref[rows, sl].astype(BF16)
                sig, f, kk, sq, eb, emb, eo, dec = _hgrn_chunk_fwd(q, z_ref[rows, sl], lb_ref[:, sl], tril01)
                qi = (q * sq * eb).astype(BF16)
                ki = (kk * emb).astype(BF16)
                ko = (kk * eo).astype(BF16)
                st = state[hh]
                att = jnp.where(causal, _dot_nt(qi, ki), 0.0)
                o_ref[rows, sl] = _dot(att, v) + _dot_nt(qi, st)
                st_ref[c, hh] = st
                state[hh] = st * dec + _dot_tn(v, ko)
            return carry

        lax.fori_loop(0, HGRN_NCH, chunk, 0)

    W = HGRN_HPB * HEAD_DIM
    nG = A_HEADS // HGRN_HPB
    hb = lambda off: pl.BlockSpec((HGRN_TB, W), lambda h, i: (i, off + h))
    return pl.pallas_call(
        body, grid=(nG, nT),
        in_specs=[hb(0), hb(nG), hb(2 * nG), pl.BlockSpec((1, W), lambda h, i: (0, h))],
        out_specs=[hb(0), pl.BlockSpec((HGRN_NCH, HGRN_HPB, HEAD_DIM, HEAD_DIM), lambda h, i: (i, h, 0, 0))],
        out_shape=[jax.ShapeDtypeStruct((T, A_WIDTH), F32), jax.ShapeDtypeStruct((nC, A_HEADS, HEAD_DIM, HEAD_DIM), F32)],
        scratch_shapes=[pltpu.VMEM((HGRN_HPB, HEAD_DIM, HEAD_DIM), F32)],
        compiler_params=_cp("parallel", "arbitrary"), name=name)(proj, proj, proj, lb)


def _hgrn2_bwd(proj, lb, st_all, do, *, name):
    T = proj.shape[0]
    nT = T // HGRN_TB

    def body(q_ref, z_ref, v_ref, lb_ref, st_ref, do_ref, dq_ref, dz_ref, dv_ref, dlb_ref, dstate):
        @pl.when(pl.program_id(1) == 0)
        def _():
            dstate[...] = jnp.zeros_like(dstate)
            dlb_ref[...] = jnp.zeros_like(dlb_ref)

        row = lax.broadcasted_iota(jnp.int32, (CHUNK, CHUNK), 0)
        col = lax.broadcasted_iota(jnp.int32, (CHUNK, CHUNK), 1)
        causal = row >= col
        tril01 = causal.astype(BF16)
        triu01 = (row <= col).astype(BF16)

        def chunk(cc, carry):
            c = HGRN_NCH - 1 - cc
            rows = pl.ds(pl.multiple_of(c * CHUNK, CHUNK), CHUNK)
            for hh in range(HGRN_HPB):
                sl = slice(hh * HEAD_DIM, (hh + 1) * HEAD_DIM)
                lbv = lb_ref[:, sl]
                q = q_ref[rows, sl]
                v = v_ref[rows, sl].astype(BF16)
                sig, f, kk, sq, eb, emb, eo, dec = _hgrn_chunk_fwd(q, z_ref[rows, sl], lbv, tril01)
                qi32 = q * sq * eb
                ki32 = kk * emb
                ko32 = kk * eo
                qi, ki, ko = qi32.astype(BF16), ki32.astype(BF16), ko32.astype(BF16)
                att = jnp.where(causal, _dot_nt(qi, ki), 0.0).astype(BF16)
                dout = do_ref[rows, sl].astype(BF16)
                st = st_ref[c, hh]
                dst = dstate[hh]
                dst16 = dst.astype(BF16)
                datt = jnp.where(causal, _dot_nt(dout, v), 0.0).astype(BF16)
                dqi = _dot(datt, ki) + _dot(dout, st)
                dki = _dot_tn(datt, qi)
                dv_ref[rows, sl] = (_dot_tn(att, dout) + _dot_nt(ko, dst16)).astype(BF16)
                dko = _dot(v, dst16)
                ddec = jnp.sum(dst * st, axis=0, keepdims=True)
                dstate[hh] = dst * dec + _dot_tn(dout, qi)
                dkk = dki * emb + dko * eo
                db = dqi * qi32 - dki * ki32 - dko * ko32
                dbend = jnp.sum(dko * ko32, axis=0, keepdims=True) + ddec * dec
                dlogf = _dot3(triu01, db) + dbend
                df = dlogf / f - dkk
                dz_ref[rows, sl] = (df * (1.0 - lbv) * sig * (1.0 - sig)).astype(BF16)
                dlb_ref[:, sl] += jnp.sum(df * (1.0 - sig), axis=0, keepdims=True)
                dq_ref[rows, sl] = (dqi * eb * (sq * (1.0 + q * (1.0 - sq)))).astype(BF16)
            return carry

        lax.fori_loop(0, HGRN_NCH, chunk, 0)

    W = HGRN_HPB * HEAD_DIM
    nG = A_HEADS // HGRN_HPB
    hb = lambda off: pl.BlockSpec((HGRN_TB, W), lambda h, i: (nT - 1 - i, off + h))
    hlb = pl.BlockSpec((1, W), lambda h, i: (0, h))
    o16 = jax.ShapeDtypeStruct((T, A_WIDTH), BF16)
    return pl.pallas_call(
        body, grid=(nG, nT),
        in_specs=[hb(0), hb(nG), hb(2 * nG), hlb,
                  pl.BlockSpec((HGRN_NCH, HGRN_HPB, HEAD_DIM, HEAD_DIM), lambda h, i: (nT - 1 - i, h, 0, 0)), hb(0)],
        out_specs=[hb(0), hb(0), hb(0), hlb],
        out_shape=[o16, o16, o16, jax.ShapeDtypeStruct((1, A_WIDTH), F32)],
        scratch_shapes=[pltpu.VMEM((HGRN_HPB, HEAD_DIM, HEAD_DIM), F32)],
        compiler_params=_cp("parallel", "arbitrary"), name=name)(proj, proj, proj, lb, st_all, do)


def _head_rms(x):
    r = lax.rsqrt(jnp.mean(x * x, axis=-1, keepdims=True) + EPS)
    return x * r, r


def _head_rms_bwd(dxhat, xhat, r):
    return r * (dxhat - xhat * jnp.mean(dxhat * xhat, axis=-1, keepdims=True))


def _a_post_fwd(o, proj, onorm, *, tt, name):
    T = o.shape[0]

    def body(o_ref, g_ref, w_ref, y_ref):
        for h in range(A_HEADS):
            sl = slice(h * HEAD_DIM, (h + 1) * HEAD_DIM)
            xhat, _ = _head_rms(o_ref[:, sl])
            g = g_ref[:, sl]
            y_ref[:, sl] = xhat * w_ref[:, sl] * (g * _sigmoid(g))

    blk = lambda c: pl.BlockSpec((tt, A_WIDTH), lambda i: (i, c))
    return pl.pallas_call(
        body, grid=(T // tt,), in_specs=[blk(0), blk(3), _full((1, A_WIDTH))], out_specs=blk(0),
        out_shape=jax.ShapeDtypeStruct((T, A_WIDTH), F32),
        compiler_params=_cp("parallel"), name=name)(o, proj, onorm)


def _a_post_bwd(o, proj, onorm, dmix, *, tt, name, dep=None):
    T = o.shape[0]

    def kernel_body(o_ref, g_ref, w_ref, dy_ref, do_ref, dg_ref, dw_ref):
        @pl.when(pl.program_id(0) == 0)
        def _():
            dw_ref[...] = jnp.zeros_like(dw_ref)

        for h in range(A_HEADS):
            sl = slice(h * HEAD_DIM, (h + 1) * HEAD_DIM)
            xhat, r = _head_rms(o_ref[:, sl])
            g = g_ref[:, sl]
            s = _sigmoid(g)
            dy = dy_ref[:, sl]
            w = w_ref[:, sl]
            dg_ref[:, sl] = (dy * xhat * w * (s * (1.0 + g * (1.0 - s)))).astype(BF16)
            dyn = dy * (g * s)
            dw_ref[:, sl] += jnp.sum(dyn * xhat, axis=0, keepdims=True)
            do_ref[:, sl] = _head_rms_bwd(dyn * w, xhat, r)

    blk = lambda c: pl.BlockSpec((tt, A_WIDTH), lambda i: (i, c))
    body, dep_specs, dep_args = _dep(kernel_body, 4, dep)
    return pl.pallas_call(
        body, grid=(T // tt,), in_specs=[blk(0), blk(3), _full((1, A_WIDTH)), blk(0)] + dep_specs,
        out_specs=[blk(0), blk(0), _full((1, A_WIDTH))],
        out_shape=[jax.ShapeDtypeStruct((T, A_WIDTH), F32), jax.ShapeDtypeStruct((T, A_WIDTH), BF16),
                   jax.ShapeDtypeStruct((1, A_WIDTH), F32)],
        compiler_params=_cp("arbitrary"), name=name)(o, proj, onorm, dmix, *dep_args)


def _mem_head_masks(n):
    lane = lax.broadcasted_iota(jnp.int32, (n, MEM_WIDTH), 1)
    return [(lane >= m * MEM_HEAD_DIM) & (lane < (m + 1) * MEM_HEAD_DIM) for m in range(MEM_HEADS)]


def _mem_head_rms(x, masks):
    x2 = x * x
    r = jnp.zeros_like(x)
    for mk in masks:
        ms = jnp.sum(jnp.where(mk, x2, 0.0), axis=-1, keepdims=True) * (1.0 / MEM_HEAD_DIM)
        r = jnp.where(mk, lax.rsqrt(ms + EPS), r)
    return x * r, r


def _mem_head_rms_bwd(dxhat, xhat, r, masks):
    t = dxhat * xhat
    m = jnp.zeros_like(t)
    for mk in masks:
        m = jnp.where(mk, jnp.sum(jnp.where(mk, t, 0.0), axis=-1, keepdims=True) * (1.0 / MEM_HEAD_DIM), m)
    return r * (dxhat - xhat * m)


MEM_SCALE = MEM_HEAD_DIM ** -0.5


def _mem_attn_fwd(proj, qcol, mkv, qn_w, kn_w, *, tt, name):
    T = proj.shape[0]

    def body(q_ref, k_ref, v_ref, qw_ref, kw_ref, o_ref):
        qmasks = _mem_head_masks(tt)
        kmasks = _mem_head_masks(MEM_TOKENS)
        qhat, _ = _mem_head_rms(q_ref[...], qmasks)
        qn = qhat * qw_ref[...]
        khat, _ = _mem_head_rms(k_ref[...], kmasks)
        kn = (khat * kw_ref[...]).astype(BF16)
        v = v_ref[...].astype(BF16)
        out = jnp.zeros((tt, MEM_WIDTH), F32)
        for m in range(MEM_HEADS):
            s = _dot_nt(jnp.where(qmasks[m], qn, 0.0), kn) * MEM_SCALE
            s = s - jnp.max(s, axis=-1, keepdims=True)
            p = jnp.exp(s)
            p = p / jnp.sum(p, axis=-1, keepdims=True)
            out = jnp.where(qmasks[m], _dot(p, v), out)
        o_ref[...] = out

    return pl.pallas_call(
        body, grid=(T // tt,),
        in_specs=[pl.BlockSpec((tt, MEM_WIDTH), lambda i: (i, qcol)), pl.BlockSpec((MEM_TOKENS, MEM_WIDTH), lambda i: (0, 0)),
                  pl.BlockSpec((MEM_TOKENS, MEM_WIDTH), lambda i: (0, 1)), _full((1, MEM_WIDTH)), _full((1, MEM_WIDTH))],
        out_specs=pl.BlockSpec((tt, MEM_WIDTH), lambda i: (i, 0)),
        out_shape=jax.ShapeDtypeStruct((T, MEM_WIDTH), F32),
        compiler_params=_cp("parallel"), name=name)(proj, mkv, mkv, qn_w, kn_w)


def _mem_attn_bwd(proj, qcol, mkv, qn_w, kn_w, dmix, *, tt, name):
    T = proj.shape[0]
    nsteps = T // tt
    ocol = (dmix.shape[1] - MEM_WIDTH) // MEM_WIDTH

    def body(q_ref, k_ref, v_ref, qw_ref, kw_ref, do_ref, dq_ref, dkv_ref, dqw_ref, dkw_ref, dk_acc, dv_acc):
        step = pl.program_id(0)

        @pl.when(step == 0)
        def _():
            dk_acc[...] = jnp.zeros_like(dk_acc)
            dv_acc[...] = jnp.zeros_like(dv_acc)
            dqw_ref[...] = jnp.zeros_like(dqw_ref)

        qmasks = _mem_head_masks(tt)
        kmasks = _mem_head_masks(MEM_TOKENS)
        qhat, qr = _mem_head_rms(q_ref[...], qmasks)
        qn = qhat * qw_ref[...]
        khat, kr = _mem_head_rms(k_ref[...], kmasks)
        kn = (khat * kw_ref[...]).astype(BF16)
        v = v_ref[...].astype(BF16)
        dout = do_ref[...]
        dqn = jnp.zeros((tt, MEM_WIDTH), F32)
        dkn = jnp.zeros((MEM_TOKENS, MEM_WIDTH), F32)
        dvv = jnp.zeros((MEM_TOKENS, MEM_WIDTH), F32)
        for m in range(MEM_HEADS):
            qm = jnp.where(qmasks[m], qn, 0.0).astype(BF16)
            s = _dot_nt(qm, kn) * MEM_SCALE
            s = s - jnp.max(s, axis=-1, keepdims=True)
            p = jnp.exp(s)
            p = p / jnp.sum(p, axis=-1, keepdims=True)
            dom = jnp.where(qmasks[m], dout, 0.0).astype(BF16)
            dp = _dot_nt(dom, v)
            ds = (p * (dp - jnp.sum(p * dp, axis=-1, keepdims=True)) * MEM_SCALE).astype(BF16)
            dqn = jnp.where(qmasks[m], _dot(ds, kn), dqn)
            dkn = jnp.where(kmasks[m], _dot_tn(ds, qm), dkn)
            dvv = jnp.where(kmasks[m], _dot_tn(p, dom), dvv)
        dqw_ref[...] += jnp.sum(dqn * qhat, axis=0, keepdims=True)
        dq_ref[...] = _mem_head_rms_bwd(dqn * qw_ref[...], qhat, qr, qmasks).astype(BF16)
        dk_acc[...] += dkn
        dv_acc[...] += dvv

        @pl.when(step == nsteps - 1)
        def _():
            dk = dk_acc[...]
            dkw_ref[...] = jnp.sum(dk * khat, axis=0, keepdims=True)
            dkv_ref[:, :MEM_WIDTH] = _mem_head_rms_bwd(dk * kw_ref[...], khat, kr, kmasks)
            dkv_ref[:, MEM_WIDTH:] = dv_acc[...]

    return pl.pallas_call(
        body, grid=(nsteps,),
        in_specs=[pl.BlockSpec((tt, MEM_WIDTH), lambda i: (i, qcol)), pl.BlockSpec((MEM_TOKENS, MEM_WIDTH), lambda i: (0, 0)),
                  pl.BlockSpec((MEM_TOKENS, MEM_WIDTH), lambda i: (0, 1)), _full((1, MEM_WIDTH)), _full((1, MEM_WIDTH)),
                  pl.BlockSpec((tt, MEM_WIDTH), lambda i: (i, ocol))],
        out_specs=[pl.BlockSpec((tt, MEM_WIDTH), lambda i: (i, 0)), _full((MEM_TOKENS, 2 * MEM_WIDTH)),
                   _full((1, MEM_WIDTH)), _full((1, MEM_WIDTH))],
        out_shape=[jax.ShapeDtypeStruct((T, MEM_WIDTH), BF16), jax.ShapeDtypeStruct((MEM_TOKENS, 2 * MEM_WIDTH), F32),
                   jax.ShapeDtypeStruct((1, MEM_WIDTH), F32), jax.ShapeDtypeStruct((1, MEM_WIDTH), F32)],
        scratch_shapes=[pltpu.VMEM((MEM_TOKENS, MEM_WIDTH), F32), pltpu.VMEM((MEM_TOKENS, MEM_WIDTH), F32)],
        compiler_params=_cp("arbitrary"), name=name)(proj, mkv, mkv, qn_w, kn_w, dmix)


HALF = HEAD_DIM // 2
ATT_SCALE = HEAD_DIM ** -0.5
NEG = -1e30


def _rope_tables(T):
    inv = ROPE_THETA ** (-jnp.arange(HALF, dtype=F32) / HALF)
    ang = jnp.arange(T, dtype=F32)[:, None] * inv[None, :]
    cos, sin = jnp.cos(ang), jnp.sin(ang)
    return jnp.concatenate([cos, cos], axis=-1), jnp.concatenate([-sin, sin], axis=-1)


def _rope(x, cosf, sinsg):
    return x * cosf + pltpu.roll(x, HALF, 1) * sinsg


def _rope_bwd(dy, cosf, sinsg):
    return dy * cosf + pltpu.roll(dy * sinsg, HALF, 1)


def _headnorm_rope_fwd(x, w_heads, cosf, sinsg, *, col0, n_heads, tt, name):
    T = x.shape[0]
    W = n_heads * HEAD_DIM

    def body(x_ref, w_ref, c_ref, s_ref, y_ref):
        c, s = c_ref[...], s_ref[...]
        for h in range(n_heads):
            sl = slice(h * HEAD_DIM, (h + 1) * HEAD_DIM)
            xhat, _ = _head_rms(x_ref[:, sl])
            y_ref[:, sl] = _rope(xhat * w_ref[:, sl], c, s)

    tbl = pl.BlockSpec((tt, HEAD_DIM), lambda i: (i, 0))
    return pl.pallas_call(
        body, grid=(T // tt,),
        in_specs=[pl.BlockSpec((tt, W), lambda i: (i, col0)), _full((1, W)), tbl, tbl],
        out_specs=pl.BlockSpec((tt, W), lambda i: (i, 0)),
        out_shape=jax.ShapeDtypeStruct((T, W), F32),
        compiler_params=_cp("parallel"), name=name)(x, w_heads, cosf, sinsg)


def _q_prep_bwd(proj, w_heads, cosf, sinsg, dqs, *, tt, name):
    T = proj.shape[0]
    W = N_GROUPS * B_WIDTH

    def body(x_ref, w_ref, c_ref, s_ref, d0, d1, d2, dx_ref, dw_ref):
        @pl.when(pl.program_id(0) == 0)
        def _():
            dw_ref[...] = jnp.zeros_like(dw_ref)

        c, s = c_ref[...], s_ref[...]
        for gi, d_ref in enumerate((d0, d1, d2)):
            for h in range(B_HEADS):
                sl = slice((gi * B_HEADS + h) * HEAD_DIM, (gi * B_HEADS + h + 1) * HEAD_DIM)
                xhat, r = _head_rms(x_ref[:, sl])
                dyn = _rope_bwd(d_ref[:, h * HEAD_DIM:(h + 1) * HEAD_DIM], c, s)
                dw_ref[:, sl] += jnp.sum(dyn * xhat, axis=0, keepdims=True)
                dx_ref[:, sl] = _head_rms_bwd(dyn * w_ref[:, sl], xhat, r).astype(BF16)

    tbl = pl.BlockSpec((tt, HEAD_DIM), lambda i: (i, 0))
    dyb = pl.BlockSpec((tt, B_WIDTH), lambda i: (i, 0))
    return pl.pallas_call(
        body, grid=(T // tt,),
        in_specs=[pl.BlockSpec((tt, W), lambda i: (i, 0)), _full((1, W)), tbl, tbl, dyb, dyb, dyb],
        out_specs=[pl.BlockSpec((tt, W), lambda i: (i, 0)), _full((1, W))],
        out_shape=[jax.ShapeDtypeStruct((T, W), BF16), jax.ShapeDtypeStruct((1, W), F32)],
        compiler_params=_cp("arbitrary"), name=name)(proj, w_heads, cosf, sinsg, *dqs)


def _kv_prep_bwd(kv, w_heads, cosf, sinsg, dks, dvs, *, tt, name):
    T = kv.shape[0]

    def body(x_ref, w_ref, c_ref, s_ref, k0, k1, k2, v0, v1, v2, dx_ref, dw_ref):
        @pl.when(pl.program_id(0) == 0)
        def _():
            dw_ref[...] = jnp.zeros_like(dw_ref)

        c, s = c_ref[...], s_ref[...]
        for h in range(B_HEADS):
            sl = slice(h * HEAD_DIM, (h + 1) * HEAD_DIM)
            vs = slice(B_WIDTH + h * HEAD_DIM, B_WIDTH + (h + 1) * HEAD_DIM)
            xhat, r = _head_rms(x_ref[:, sl])
            dyn = _rope_bwd(k0[:, sl] + k1[:, sl] + k2[:, sl], c, s)
            dw_ref[:, sl] += jnp.sum(dyn * xhat, axis=0, keepdims=True)
            dx_ref[:, sl] = _head_rms_bwd(dyn * w_ref[:, sl], xhat, r).astype(BF16)
            dx_ref[:, vs] = (v0[:, sl] + v1[:, sl] + v2[:, sl]).astype(BF16)

    tbl = pl.BlockSpec((tt, HEAD_DIM), lambda i: (i, 0))
    dyb = pl.BlockSpec((tt, B_WIDTH), lambda i: (i, 0))
    return pl.pallas_call(
        body, grid=(T // tt,),
        in_specs=[dyb, _full((1, B_WIDTH)), tbl, tbl] + [dyb] * 6,
        out_specs=[pl.BlockSpec((tt, 2 * B_WIDTH), lambda i: (i, 0)), _full((1, B_WIDTH))],
        out_shape=[jax.ShapeDtypeStruct((T, 2 * B_WIDTH), BF16), jax.ShapeDtypeStruct((1, B_WIDTH), F32)],
        compiler_params=_cp("arbitrary"), name=name)(kv, w_heads, cosf, sinsg, *dks, *dvs)


def _band_masks(n_is_first=None):
    row = lax.broadcasted_iota(jnp.int32, (SPAN, SPAN), 0)
    col = lax.broadcasted_iota(jnp.int32, (SPAN, SPAN), 1)
    return row >= col, col >= row


def _dil_views(T, d):
    L = T // d
    return L, L // SPAN


def _dil_fwd(qr, kr, kv, gi, d, *, name):
    T = qr.shape[0]
    L, nb = _dil_views(T, d)

    def body(q_ref, kc_ref, kp_ref, vc_ref, vp_ref, o_ref, lse_ref):
        cur_ok, prev_band = _band_masks()
        prev_ok = prev_band & (pl.program_id(1) > 0)
        for h in range(B_HEADS):
            sl = slice(h * HEAD_DIM, (h + 1) * HEAD_DIM)
            q = q_ref[:, sl]
            sc = jnp.where(cur_ok, _dot_nt(q, kc_ref[:, sl]) * ATT_SCALE, NEG)
            sp = jnp.where(prev_ok, _dot_nt(q, kp_ref[:, sl]) * ATT_SCALE, NEG)
            m = jnp.maximum(jnp.max(sc, axis=-1, keepdims=True), jnp.max(sp, axis=-1, keepdims=True))
            pc = jnp.exp(sc - m)
            pp = jnp.exp(sp - m)
            l = jnp.sum(pc, axis=-1, keepdims=True) + jnp.sum(pp, axis=-1, keepdims=True)
            o_ref[:, sl] = (_dot(pc, vc_ref[:, sl]) + _dot(pp, vp_ref[:, sl])) / l
            lse_ref[:, sl] = jnp.broadcast_to(m + jnp.log(l), (SPAN, HEAD_DIM))

    blk = lambda f: pl.BlockSpec((SPAN, B_WIDTH), f)
    cur = lambda r, n: (n, r)
    prev = lambda r, n: (jnp.maximum(n - 1, 0), r)
    ov = jax.ShapeDtypeStruct((L, d * B_WIDTH), F32)
    o, lse = pl.pallas_call(
        body, grid=(d, nb),
        in_specs=[blk(lambda r, n: (n, r * N_GROUPS + gi)), blk(cur), blk(prev),
                  blk(lambda r, n: (n, 2 * r + 1)), blk(lambda r, n: (jnp.maximum(n - 1, 0), 2 * r + 1))],
        out_specs=[blk(cur), blk(cur)], out_shape=[ov, ov],
        compiler_params=_cp("parallel", "arbitrary"), name=name,
    )(qr.reshape(L, d * N_GROUPS * B_WIDTH), kr.reshape(L, d * B_WIDTH), kr.reshape(L, d * B_WIDTH),
      kv.reshape(L, d * 2 * B_WIDTH), kv.reshape(L, d * 2 * B_WIDTH))
    return o.reshape(T, B_WIDTH), lse.reshape(T, B_WIDTH)


def _dil_combine_fwd(os_, lses, *, tt, name):
    T = os_[0].shape[0]

    def body(o0, o1, o2, l0, l1, l2, y_ref, lse_ref):
        a, b, c = l0[...], l1[...], l2[...]
        m = jnp.maximum(jnp.maximum(a, b), c)
        wa, wb, wc = jnp.exp(a - m), jnp.exp(b - m), jnp.exp(c - m)
        den = wa + wb + wc
        y_ref[...] = (wa * o0[...] + wb * o1[...] + wc * o2[...]) / den
        lse_ref[...] = m + jnp.log(den)

    blk = pl.BlockSpec((tt, B_WIDTH), lambda i: (i, 0))
    sh = jax.ShapeDtypeStruct((T, B_WIDTH), F32)
    return pl.pallas_call(
        body, grid=(T // tt,), in_specs=[blk] * 6, out_specs=[blk, blk], out_shape=[sh, sh],
        compiler_params=_cp("parallel"), name=name)(*os_, *lses)


def _dil_bwd_prep(dmix, mix_main, *, tt, name, dep=None):
    T = mix_main.shape[0]

    def kernel_body(dy_ref, y_ref, dd_ref):
        for h in range(B_HEADS):
            sl = slice(h * HEAD_DIM, (h + 1) * HEAD_DIM)
            dd_ref[:, sl] = jnp.broadcast_to(jnp.sum(dy_ref[:, sl] * y_ref[:, sl], axis=-1, keepdims=True), (tt, HEAD_DIM))

    blk = pl.BlockSpec((tt, B_WIDTH), lambda i: (i, 0))
    body, dep_specs, dep_args = _dep(kernel_body, 2, dep)
    return pl.pallas_call(
        body, grid=(T // tt,), in_specs=[blk, blk] + dep_specs, out_specs=blk,
        out_shape=jax.ShapeDtypeStruct((T, B_WIDTH), F32),
        compiler_params=_cp("parallel"), name=name)(dmix, mix_main, *dep_args)


DILS_UNROLL = 4


def _dils_specs(gi, d, nblk):
    blk = lambda f: pl.BlockSpec((SPAN * d, HEAD_DIM), f)
    return {
        "q": blk(lambda h, n: (n, gi * B_HEADS + h)), "q_next": blk(lambda h, n: (jnp.minimum(n + 1, nblk - 1), gi * B_HEADS + h)),
        "cur": blk(lambda h, n: (n, h)), "prev": blk(lambda h, n: (jnp.maximum(n - 1, 0), h)),
        "next": blk(lambda h, n: (jnp.minimum(n + 1, nblk - 1), h)),
        "v": blk(lambda h, n: (n, B_HEADS + h)), "v_prev": blk(lambda h, n: (jnp.maximum(n - 1, 0), B_HEADS + h)),
    }


def _dils_fwd(qr, kr, kv, gi, d, *, name):
    T = qr.shape[0]
    nblk = T // (SPAN * d)
    sp = _dils_specs(gi, d, nblk)

    def body(q_ref, kc_ref, kp_ref, vc_ref, vp_ref, o_ref, lse_ref):
        cur_ok, prev_band = _band_masks()
        prev_ok = prev_band & (pl.program_id(1) > 0)

        def residue(r, carry):
            rows = pl.ds(r, SPAN, stride=d)
            q = q_ref[rows, :]
            sc = jnp.where(cur_ok, _dot_nt(q, kc_ref[rows, :]) * ATT_SCALE, NEG)
            sp_ = jnp.where(prev_ok, _dot_nt(q, kp_ref[rows, :]) * ATT_SCALE, NEG)
            m = jnp.maximum(jnp.max(sc, axis=-1, keepdims=True), jnp.max(sp_, axis=-1, keepdims=True))
            pc = jnp.exp(sc - m)
            pp = jnp.exp(sp_ - m)
            l = jnp.sum(pc, axis=-1, keepdims=True) + jnp.sum(pp, axis=-1, keepdims=True)
            o_ref[rows, :] = (_dot(pc, vc_ref[rows, :]) + _dot(pp, vp_ref[rows, :])) / l
            lse_ref[rows, :] = jnp.broadcast_to(m + jnp.log(l), (SPAN, HEAD_DIM))
            return carry

        lax.fori_loop(0, d, residue, 0, unroll=DILS_UNROLL)

    sh = jax.ShapeDtypeStruct((T, B_WIDTH), F32)
    return pl.pallas_call(
        body, grid=(B_HEADS, nblk), in_specs=[sp["q"], sp["cur"], sp["prev"], sp["v"], sp["v_prev"]],
        out_specs=[sp["cur"], sp["cur"]], out_shape=[sh, sh],
        compiler_params=_cp("parallel", "arbitrary"), name=name)(qr, kr, kr, kv, kv)


DIL_BWD_GROUP = {1: 4, 4: 1, 16: 1}


def _dil_bwd(qr, kr, kv, dmix, lse, dd, gi, d, *, name):
    T = qr.shape[0]
    G = DIL_BWD_GROUP[d]
    band = SPAN * d
    tb = G * band
    nblk = T // tb

    def body(q_ref, dy_ref, lse_ref, dd_ref, kc_ref, kp_ref, vc_ref, vp_ref, dq_ref, dk_ref, dv_ref):
        n = pl.program_id(1)

        @pl.when(n == 0)
        def _():
            dk_ref[...] = jnp.zeros_like(dk_ref)
            dv_ref[...] = jnp.zeros_like(dv_ref)

        cur_ok, prev_band = _band_masks()
        base = pl.multiple_of(n * tb, SPAN)
        for j in range(G):
            def residue(r, carry, j=j):
                off = j * band + r
                rows = pl.ds(off, SPAN, stride=d)
                q, dy = q_ref[rows, :], dy_ref[rows, :]
                lse_h = jnp.max(lse_ref[rows, :], axis=-1, keepdims=True)
                dd_h = jnp.max(dd_ref[rows, :], axis=-1, keepdims=True)
                kc, vc = kc_ref[rows, :], vc_ref[rows, :]
                if j > 0:
                    before = pl.ds(off - band, SPAN, stride=d)
                    kp, vp = kc_ref[before, :], vc_ref[before, :]
                    prev_ok = prev_band
                else:
                    before = pl.ds((G - 1) * band + r, SPAN, stride=d)
                    kp, vp = kp_ref[before, :], vp_ref[before, :]
                    prev_ok = prev_band & (n > 0)
                pc = jnp.exp(jnp.where(cur_ok, _dot_nt(q, kc) * ATT_SCALE, NEG) - lse_h)
                pp = jnp.exp(jnp.where(prev_ok, _dot_nt(q, kp) * ATT_SCALE, NEG) - lse_h)
                dsc = pc * (_dot_nt(dy, vc) - dd_h) * ATT_SCALE
                dsp = pp * (_dot_nt(dy, vp) - dd_h) * ATT_SCALE
                dq_ref[rows, :] = _dot(dsc, kc) + _dot(dsp, kp)
                here = pl.ds(base + off, SPAN, stride=d)
                dk_ref[here, :] += _dot_tn(dsc, q)
                dv_ref[here, :] += _dot_tn(pc, dy)
                there = pl.ds(jnp.maximum(base + off - band, r), SPAN, stride=d)
                dk_ref[there, :] += _dot_tn(dsp, q)
                dv_ref[there, :] += _dot_tn(pp, dy)
                return carry

            lax.fori_loop(0, d, residue, 0, unroll=min(d, DILS_UNROLL))

    blk = lambda f: pl.BlockSpec((tb, HEAD_DIM), f)
    cur = lambda h, n: (n, h)
    prev = lambda h, n: (jnp.maximum(n - 1, 0), h)
    whole = pl.BlockSpec((T, HEAD_DIM), lambda h, n: (0, h))
    sh = jax.ShapeDtypeStruct((T, B_WIDTH), F32)
    return pl.pallas_call(
        body, grid=(B_HEADS, nblk),
        in_specs=[blk(lambda h, n: (n, gi * B_HEADS + h)), blk(cur), blk(cur), blk(cur), blk(cur), blk(prev),
                  blk(lambda h, n: (n, B_HEADS + h)), blk(lambda h, n: (jnp.maximum(n - 1, 0), B_HEADS + h))],
        out_specs=[blk(cur), whole, whole], out_shape=[sh, sh, sh],
        compiler_params=_cp("parallel", "arbitrary"), name=name)(qr, dmix, lse, dd, kr, kr, kv, kv)


A_MQ_COL = 4 * A_WIDTH // MEM_WIDTH
B_MQ_COL = N_GROUPS * B_WIDTH // MEM_WIDTH


def _row(v):
    return v.reshape(1, -1).astype(F32)


def _local_step(x, mem, tgt, get_w, P, put_g, first_dep=None, forward_point=lambda i, value: value):
    T = x.shape[0]
    cosf, sinsg = _rope_tables(T)
    lb_soft = jax.nn.softmax(P["a_lb_logits"].astype(F32), axis=0)
    lb = lb_soft[0:1]
    qw_heads = jnp.repeat(P["b_qnorm"][0], B_HEADS, axis=0).reshape(1, -1)
    kw_heads = jnp.tile(_row(P["b_knorm"]), (1, B_HEADS))
    mqw = [jnp.tile(_row(P["mem_qnorm"][l]), (1, MEM_HEADS)) for l in range(2)]
    mkw = [jnp.tile(_row(P["mem_knorm"][l]), (1, MEM_HEADS)) for l in range(2)]
    nmix = [_row(P["norm_mix"][l]) for l in range(2)]
    nffn = [_row(P["norm_ffn"][l]) for l in range(2)]
    mnorm = [_row(P["mem_norm"][l]) for l in range(2)]
    kvn = _row(P["kv_norm"])
    onorm = _row(P["a_onorm"])
    W = {}

    def w_of(name, after=None):
        if name not in W:
            W[name] = get_w(name, after)
        return W[name]

    proj_a, xn0 = _rms_matmul(x, nmix[0], w_of("a_w_in"), tt=512, tn=1664, wt=True, name="proj_a", dep=first_dep)
    mkv0, mn0 = _rms_matmul(mem, mnorm[0], w_of("w_mem_kv0"), tt=MEM_TOKENS, tn=2 * MEM_WIDTH, wt=False, name="mem_kv0")
    o_raw, st = _hgrn2_fwd(proj_a, lb, name="hgrn2_fwd")
    o_raw = forward_point(0, o_raw)
    mm0 = _a_post_fwd(o_raw, proj_a, onorm, tt=512, name="a_post_fwd")
    mo0 = _mem_attn_fwd(proj_a, A_MQ_COL, mkv0, mqw[0], mkw[0], tt=512, name="mem_attn_fwd0")
    hm0 = _mm_res(x, mm0, mo0, w_of("w_out0", mo0), tt=512, name="out_proj0")
    hm0 = forward_point(1, hm0)
    gu0, hn0 = _rms_matmul(hm0, nffn[0], w_of("w_gate_up0", hm0), tt=512, tn=1408, wt=True, out_dtype=BF16, name="gate_up0")
    h1 = _swiglu_down(hm0, gu0, w_of("w_down0", gu0), tt=256, name="down0")
    h1 = forward_point(2, h1)
    kv, hkn = _rms_matmul(h1, kvn, w_of("w_kv", h1), tt=512, tn=768, wt=True, name="kv_proj")
    kr = _headnorm_rope_fwd(kv, kw_heads, cosf, sinsg, col0=0, n_heads=B_HEADS, tt=512, name="k_prep")

    proj_b, xn1 = _rms_matmul(h1, nmix[1], w_of("b_w_in", kr), tt=512, tn=1280, wt=True, name="proj_b")
    proj_b = forward_point(3, proj_b)
    mkv1, mn1 = _rms_matmul(mem, mnorm[1], w_of("w_mem_kv1", kr), tt=MEM_TOKENS, tn=2 * MEM_WIDTH, wt=False, name="mem_kv1")
    qr = _headnorm_rope_fwd(proj_b, qw_heads, cosf, sinsg, col0=0, n_heads=N_GROUPS * B_HEADS, tt=512, name="q_prep")
    outs = [(_dil_fwd if d == 1 else _dils_fwd)(qr, kr, kv, gi, d, name=f"dil_fwd{gi}") for gi, d in enumerate(DILATIONS)]
    mm1, lse_tot = _dil_combine_fwd([o for o, _ in outs], [s for _, s in outs], tt=512, name="dil_combine")
    mo1 = _mem_attn_fwd(proj_b, B_MQ_COL, mkv1, mqw[1], mkw[1], tt=512, name="mem_attn_fwd1")
    hm1 = _mm_res(h1, mm1, mo1, w_of("w_out1", mo1), tt=512, name="out_proj1")
    gu1, hn1 = _rms_matmul(hm1, nffn[1], w_of("w_gate_up1", hm1), tt=512, tn=1408, wt=True, out_dtype=BF16, name="gate_up1")
    y = _swiglu_down(hm1, gu1, w_of("w_down1", gu1), tt=256, name="down1")
    dy, sq = _loss_kernel(y, tgt, tt=512, name="loss")

    gP = {}
    zeros_mem = jnp.zeros((MEM_TOKENS, D_MODEL), F32)

    def ffn_bwd(l, dh, hm, gu, hn):
        dgu, g_wd = _swiglu_bwd(dh, gu, w_of(f"w_down{l}"), tt=256, name=f"swiglu_bwd{l}")
        g_wgu = _mm_tn(dgu, hn, tt=512, tka=1408, name=f"g_w_gate_up{l}")
        sent = put_g({f"w_down{l}": g_wd, f"w_gate_up{l}": g_wgu})
        dhm, g_nf = _rms_bwd_dx(hm, nffn[l], w_of(f"w_gate_up{l}"), dgu, dh, tt=256, wt=True, name=f"gate_up_bwd{l}", dep=sent)
        return dhm, g_nf

    def mix_bwd(l, dhm, mix_main, mix_mem, proj, qcol, mkv, mn):
        dmix, g_wout = _out_proj_bwd(dhm, mix_main, mix_mem, w_of(f"w_out{l}"), tt=512, name=f"out_proj_bwd{l}")
        dmq, dmkv, dqw, dkw = _mem_attn_bwd(proj, qcol, mkv, mqw[l], mkw[l], dmix, tt=512, name=f"mem_attn_bwd{l}")
        g_wmkv = _mm_tn(mn, dmkv, tt=MEM_TOKENS, tka=512, name=f"g_w_mem_kv{l}")
        sent = put_g({f"w_out{l}": g_wout, f"w_mem_kv{l}": g_wmkv})
        _, g_mn = _rms_bwd_dx(mem, mnorm[l], w_of(f"w_mem_kv{l}"), dmkv, zeros_mem, tt=MEM_TOKENS, wt=False, name=f"mem_kv_bwd{l}")
        fold = lambda v: v.reshape(MEM_HEADS, MEM_HEAD_DIM).sum(axis=0)
        return dmix, dmq, g_mn, fold(dqw), fold(dkw), sent

    dhm1, g_nf1 = ffn_bwd(1, dy, hm1, gu1, hn1)
    dmix1, dmq1, g_mn1, g_mq1, g_mk1, sent = mix_bwd(1, dhm1, mm1, mo1, proj_b, B_MQ_COL, mkv1, mn1)
    dd = _dil_bwd_prep(dmix1, mm1, tt=512, name="dil_bwd_prep", dep=sent)
    dqs, dks, dvs = [], [], []
    for gi, d in enumerate(DILATIONS):
        dq_g, dk_g, dv_g = _dil_bwd(qr, kr, kv, dmix1, lse_tot, dd, gi, d, name=f"dil_bwd{gi}")
        dqs.append(dq_g)
        dks.append(dk_g)
        dvs.append(dv_g)
    dq_raw, dqw = _q_prep_bwd(proj_b, qw_heads, cosf, sinsg, dqs, tt=512, name="q_prep_bwd")
    dkv, dkw = _kv_prep_bwd(kv, kw_heads, cosf, sinsg, dks, dvs, tt=512, name="kv_prep_bwd")
    dproj_b = [dq_raw, dmq1]
    g_wb = _mm_tn_pieces(dproj_b, xn1, tt=512, name="g_b_w_in")
    g_wkv = _mm_tn(dkv, hkn, tt=512, tka=768, name="g_w_kv")
    sent = put_g({"b_w_in": g_wb, "w_kv": g_wkv})
    dh1, g_nm1 = _rms_bwd_dx(h1, nmix[1], w_of("b_w_in"), dproj_b, dhm1, tt=256, wt=True, name="proj_b_bwd", dep=sent)
    dh1, g_kvn = _rms_bwd_dx(h1, kvn, w_of("w_kv"), dkv, dh1, tt=256, wt=True, name="kv_proj_bwd")

    dhm0, g_nf0 = ffn_bwd(0, dh1, hm0, gu0, hn0)
    dmix0, dmq0, g_mn0, g_mq0, g_mk0, sent = mix_bwd(0, dhm0, mm0, mo0, proj_a, A_MQ_COL, mkv0, mn0)
    do_raw, dg, g_onorm = _a_post_bwd(o_raw, proj_a, onorm, dmix0, tt=512, name="a_post_bwd", dep=sent)
    dq, dz, dv, dlb = _hgrn2_bwd(proj_a, lb, st, do_raw, name="hgrn2_bwd")
    dproj_a = [dq, dz, dv, dg, dmq0]
    sent = put_g({"a_w_in": _mm_tn_pieces(dproj_a, xn0, tt=512, name="g_a_w_in")})
    gx, g_nm0 = _rms_bwd_dx(x, nmix[0], w_of("a_w_in"), dproj_a, dhm0, tt=256, wt=True, name="proj_a_bwd", dep=sent)

    dl0 = lb_soft[0:1] * lb_soft[1:2] * dlb
    gP["a_lb_logits"] = jnp.concatenate([dl0, -dl0], axis=0)
    gP["a_onorm"] = g_onorm
    gP["norm_mix"] = jnp.concatenate([g_nm0, g_nm1], axis=0)
    gP["norm_ffn"] = jnp.concatenate([g_nf0, g_nf1], axis=0)
    gP["b_qnorm"] = dqw.reshape(N_GROUPS, B_HEADS, HEAD_DIM).sum(axis=1)[None]
    gP["kv_norm"] = g_kvn.reshape(-1)
    gP["b_knorm"] = dkw.reshape(B_HEADS, HEAD_DIM).sum(axis=0)
    gP["mem_norm"] = jnp.concatenate([g_mn0, g_mn1], axis=0)
    gP["mem_qnorm"] = jnp.stack([g_mq0, g_mq1])
    gP["mem_knorm"] = jnp.stack([g_mk0, g_mk1])
    return sq, gx, gP


MESH_ID = pl.DeviceIdType.MESH
HBM_SPEC = pl.BlockSpec(memory_space=pltpu.HBM)


def _position():
    return lax.axis_index("x"), lax.axis_index("y"), lax.axis_index("c")


def _all_gather(blocks, *, name):
    n = len(blocks)

    def body(*refs):
        x_refs, out_refs = refs[:n], refs[n:2 * n]
        send_sems, recv_sems, local_sems = refs[2 * n:]
        x, y, c = _position()
        me, sibling = (x, y, c), (x, y, 1 - c)
        chips = [(1 - x, y), (x, 1 - y), (1 - x, 1 - y)]

        def slot(a, px, py, pc):
            return out_refs[a].at[4 * px + 2 * py + pc]

        def copy(a, k, blk, to, src=None):
            return pltpu.make_async_remote_copy(
                src_ref=slot(a, *blk) if src is None else src, dst_ref=slot(a, *blk),
                send_sem=send_sems.at[7 * a + k], recv_sem=recv_sems.at[7 * a + k], device_id=to, device_id_type=MESH_ID)

        mine = [pltpu.make_async_copy(x_refs[a], slot(a, *me), local_sems.at[a]) for a in range(n)]
        for cp in mine:
            cp.start()
        first = []
        for a in range(n):
            first.append(copy(a, 0, me, sibling, src=x_refs[a]))
            first += [copy(a, 1 + j, me, (*chip, c), src=x_refs[a]) for j, chip in enumerate(chips)]
        for cp in first:
            cp.start()
        passed = []
        for j, chip in enumerate(chips):
            for a in range(n):
                copy(a, 1 + j, (*chip, c), me).wait_recv()
                cp = copy(a, 4 + j, (*chip, c), sibling)
                cp.start()
                passed.append(cp)
        for a in range(n):
            copy(a, 0, sibling, me).wait_recv()
            for j, chip in enumerate(chips):
                copy(a, 4 + j, (*chip, 1 - c), me).wait_recv()
        for cp in first + passed:
            cp.wait_send()
        for cp in mine:
            cp.wait()

    return pl.pallas_call(
        body, out_shape=[jax.ShapeDtypeStruct((N_DEV,) + b.shape, b.dtype) for b in blocks],
        in_specs=[HBM_SPEC] * n, out_specs=[HBM_SPEC] * n,
        scratch_shapes=[pltpu.SemaphoreType.DMA((7 * n,)), pltpu.SemaphoreType.DMA((7 * n,)), pltpu.SemaphoreType.DMA((n,))],
        name=name)(*blocks)


SEM_SPEC = pl.BlockSpec(memory_space=pltpu.SEMAPHORE)
ANY_SPEC = pl.BlockSpec(memory_space=pl.ANY)
DATAFLOW = pltpu.SideEffectType.DATAFLOW_SIDE_EFFECTING


def _peer(k, x, y, c):
    return (1 - x if (k >> 2) & 1 else x, 1 - y if (k >> 1) & 1 else y, 1 - c if k & 1 else c)


def _own_slot_filled(own_block):
    x, y, c = _position()
    zone = lax.empty((N_DEV,) + own_block.shape, own_block.dtype)
    return lax.dynamic_update_slice_in_dim(zone, own_block[None], 4 * x + 2 * y + c, axis=0)


ALL_PEERS = tuple(range(1, N_DEV))
SIBLING_AND_SAME_CORE = (1, 2, 4, 6)
SAME_CORE = (2, 4, 6)


def _split_start(srcs, scatter, after, *, name, relations=ALL_PEERS, carried=None):
    n = len(srcs)
    extra = ([] if after is None else [after]) + ([] if carried is None else [carried])
    n_carried = 0 if carried is None else 1
    x, y, c = _position()
    me = 4 * x + 2 * y + c
    lands = [_own_slot_filled(lax.dynamic_index_in_dim(s, me, 0, keepdims=False) if scatter else s) for s in srcs]

    def body(*refs):
        src_refs, land_refs = refs[:n], refs[n:2 * n]
        send_sems, recv_sems = refs[2 * n + len(extra)], refs[2 * n + len(extra) + 1]
        token = refs[2 * n + len(extra) + 2 + 2 * n]
        bx, by, bc = _position()
        bme = 4 * bx + 2 * by + bc
        for a in range(n):
            for k in relations:
                tx, ty, tc = _peer(k, bx, by, bc)
                src = src_refs[a].at[4 * tx + 2 * ty + tc] if scatter else src_refs[a]
                pltpu.make_async_remote_copy(
                    src_ref=src, dst_ref=land_refs[a].at[bme],
                    send_sem=send_sems.at[7 * a + k - 1], recv_sem=recv_sems.at[7 * a + k - 1],
                    device_id=(tx, ty, tc), device_id_type=MESH_ID).start()
        token[...] = jnp.zeros_like(token)

    hbm = lambda a: pltpu.HBM(a.shape, a.dtype)
    outs = pl.pallas_call(
        body, name=name,
        out_shape=(pltpu.SemaphoreType.DMA((7 * n,)), pltpu.SemaphoreType.DMA((7 * n,)),
                   *[hbm(s) for s in srcs], *[hbm(l) for l in lands], jax.ShapeDtypeStruct((8, 128), F32),
                   *([hbm(carried)] if n_carried else [])),
        in_specs=[HBM_SPEC] * (2 * n) + [ANY_SPEC] * len(extra),
        out_specs=(SEM_SPEC, SEM_SPEC, *[HBM_SPEC] * (2 * n), pl.BlockSpec(memory_space=pltpu.VMEM), *([ANY_SPEC] * n_carried)),
        input_output_aliases={**{i: 2 + i for i in range(2 * n)},
                              **({2 * n + len(extra) - 1: 2 * n + 3} if n_carried else {})},
        compiler_params=pltpu.CompilerParams(has_side_effects=DATAFLOW),
    )(*[pltpu.with_memory_space_constraint(s, pltpu.HBM) for s in srcs],
      *[pltpu.with_memory_space_constraint(l, pltpu.HBM) for l in lands], *extra)
    return {"n": n, "relations": relations, "send": outs[0], "recv": outs[1], "srcs": list(outs[2:2 + n]),
            "lands": list(outs[2 + n:2 + 2 * n]), "token": outs[2 * n + 2], "carried": outs[-1] if n_carried else None}


def _forward_start(lands, carried, *, name):
    n = len(lands)

    def body(*refs):
        land_refs = refs[:n]
        send_sems, recv_sems = refs[n + 1], refs[n + 2]
        bx, by, bc = _position()
        for a in range(n):
            for k in SAME_CORE:
                tx, ty, tc = _peer(k, bx, by, bc)
                block = land_refs[a].at[4 * tx + 2 * ty + tc]
                pltpu.make_async_remote_copy(
                    src_ref=block, dst_ref=block,
                    send_sem=send_sems.at[7 * a + k - 1], recv_sem=recv_sems.at[7 * a + k - 1],
                    device_id=(bx, by, 1 - bc), device_id_type=MESH_ID).start()

    hbm = lambda a: pltpu.HBM(a.shape, a.dtype)
    outs = pl.pallas_call(
        body, name=name,
        out_shape=(pltpu.SemaphoreType.DMA((7 * n,)), pltpu.SemaphoreType.DMA((7 * n,)),
                   *[hbm(l) for l in lands], hbm(carried)),
        in_specs=[HBM_SPEC] * n + [ANY_SPEC],
        out_specs=(SEM_SPEC, SEM_SPEC, *[HBM_SPEC] * n, ANY_SPEC),
        input_output_aliases={i: 2 + i for i in range(n + 1)},
        compiler_params=pltpu.CompilerParams(has_side_effects=DATAFLOW),
    )(*lands, carried)
    handle = {"n": n, "relations": SAME_CORE, "send": outs[0], "recv": outs[1], "srcs": [], "lands": list(outs[2:2 + n])}
    return handle, outs[-1]


def _split_wait(handle, after, *, name):
    n, ns = handle["n"], len(handle["srcs"])

    def body(*refs):
        land_refs = refs[ns:ns + n]
        send_sems, recv_sems = refs[ns + n], refs[ns + n + 1]
        bx, by, bc = _position()
        for a in range(n):
            for k in handle["relations"]:
                block = land_refs[a].at[0]
                cp = pltpu.make_async_remote_copy(
                    src_ref=block, dst_ref=block,
                    send_sem=send_sems.at[7 * a + k - 1], recv_sem=recv_sems.at[7 * a + k - 1],
                    device_id=_peer(k, bx, by, bc), device_id_type=MESH_ID)
                cp.wait_send()
                cp.wait_recv()

    hbm = lambda a: pltpu.HBM(a.shape, a.dtype)
    outs = pl.pallas_call(
        body, name=name,
        out_shape=(*[hbm(s) for s in handle["srcs"]], *[hbm(l) for l in handle["lands"]]),
        in_specs=[HBM_SPEC] * (ns + n) + [SEM_SPEC, SEM_SPEC, ANY_SPEC],
        out_specs=tuple([HBM_SPEC] * (ns + n)),
        input_output_aliases={i: i for i in range(ns + n)},
        compiler_params=pltpu.CompilerParams(has_side_effects=DATAFLOW),
    )(*handle["srcs"], *handle["lands"], handle["send"], handle["recv"], after)
    return list(outs[ns:])


def _sum_sources(parts, *, tr, name):
    n, R, C = parts.shape

    def body(p_ref, o_ref):
        acc = p_ref[0].astype(F32)
        for s in range(1, n):
            acc = acc + p_ref[s].astype(F32)
        o_ref[...] = acc

    return pl.pallas_call(
        body, grid=(R // tr,), in_specs=[pl.BlockSpec((n, tr, C), lambda i: (0, i, 0))],
        out_specs=pl.BlockSpec((tr, C), lambda i: (i, 0)),
        out_shape=jax.ShapeDtypeStruct((R, C), F32), compiler_params=_cp("parallel"), name=name)(parts)


def _adamw_math(g, w, m, v):
    c1 = 1.0 - ADAM_B1 ** ADAM_STEP
    c2 = 1.0 - ADAM_B2 ** ADAM_STEP
    nm = ADAM_B1 * m + (1.0 - ADAM_B1) * g
    nv = ADAM_B2 * v + (1.0 - ADAM_B2) * (g * g)
    return -ADAM_LR * ((nm / c1) / (jnp.sqrt(nv / c2) + ADAM_EPS) + ADAM_WD * w), nm, nv


def _reduce_adamw(received, w, m, v, *, col, tr, name):
    L, R, C = w.shape

    def body(*refs):
        p_refs = refs[:L]
        w_ref, m_ref, v_ref, g_ref, d_ref, nm_ref, nv_ref = refs[L:]
        for l in range(L):
            @pl.when(pl.program_id(0) == l)
            def _(p_ref=p_refs[l]):
                acc = p_ref[0].astype(F32)
                for s in range(1, N_DEV):
                    acc = acc + p_ref[s].astype(F32)
                g = acc.T if col else acc
                g_ref[...] = g
                d_ref[...], nm_ref[...], nv_ref[...] = _adamw_math(g, w_ref[...], m_ref[...], v_ref[...])

    p_spec = (pl.BlockSpec((N_DEV, C, tr), lambda l, i: (0, 0, i)) if col
              else pl.BlockSpec((N_DEV, tr, C), lambda l, i: (0, i, 0)))
    blk = pl.BlockSpec((None, tr, C), lambda l, i: (l, i, 0))
    sh = jax.ShapeDtypeStruct((L, R, C), F32)
    return pl.pallas_call(
        body, grid=(L, R // tr), in_specs=[p_spec] * L + [blk] * 3, out_specs=[blk] * 4, out_shape=[sh] * 4,
        compiler_params=_cp("parallel", "parallel"), name=name)(*received, w, m, v)


def _adamw(g, w, m, v, *, tr, name):
    L, R, C = w.shape

    def body(g_ref, w_ref, m_ref, v_ref, d_ref, nm_ref, nv_ref):
        d_ref[...], nm_ref[...], nv_ref[...] = _adamw_math(g_ref[...], w_ref[...], m_ref[...], v_ref[...])

    blk = pl.BlockSpec((None, tr, C), lambda l, i: (l, i, 0))
    sh = jax.ShapeDtypeStruct((L, R, C), F32)
    return pl.pallas_call(
        body, grid=(L, R // tr), in_specs=[blk] * 4, out_specs=[blk] * 3, out_shape=[sh] * 3,
        compiler_params=_cp("parallel", "parallel"), name=name)(g, w, m, v)


UNITS = {
    "a_w_in": ("a_w_in", 0, True), "w_mem_kv0": ("w_mem_kv", 0, False), "w_out0": ("w_out", 0, False),
    "w_gate_up0": ("w_gate_up", 0, True), "w_down0": ("w_down", 0, False), "w_kv": ("w_kv", None, True),
    "b_w_in": ("b_w_in", 0, True), "w_mem_kv1": ("w_mem_kv", 1, False), "w_out1": ("w_out", 1, False),
    "w_gate_up1": ("w_gate_up", 1, True), "w_down1": ("w_down", 1, False),
}
BIG = ("a_w_in", "b_w_in", "w_kv", "w_mem_kv", "w_out", "w_gate_up", "w_down")
ADAMW_ROW_TILE = {"a_w_in": 256, "b_w_in": 256, "w_kv": 256, "w_mem_kv": 128, "w_out": 128, "w_gate_up": 176, "w_down": 176}
TRANSPOSED_UPDATE = ("w_gate_up",)


def _wire_block(weights, unit):
    name, layer, col = UNITS[unit]
    a = weights[name] if layer is None else weights[name][layer]
    return (a.T if col else a).astype(BF16)


SMALL_REPLICATED = ("norm_mix", "norm_ffn", "b_qnorm", "kv_norm", "b_knorm", "mem_norm", "mem_qnorm", "mem_knorm")
SMALL_SHARDED = ("a_lb_logits", "a_onorm")
SMALL_ORDER = SMALL_REPLICATED + SMALL_SHARDED
LANES = 128


def _prod(shape):
    n = 1
    for s in shape:
        n *= s
    return n


def _pack_flat(arrays, rows, cols, dtype):
    flat = jnp.concatenate([a.reshape(-1).astype(dtype) for a in arrays])
    return jnp.pad(flat, (0, rows * cols - flat.shape[0])).reshape(rows, cols)


def _unpack_flat(packed, shapes):
    flat = packed.reshape(-1)
    out, off = [], 0
    for s in shapes:
        out.append(flat[off:off + _prod(s)].reshape(s))
        off += _prod(s)
    return out


def kernel(x, mem, norm_mix, norm_ffn, a_w_in, a_lb_logits, a_onorm, b_w_in, b_qnorm, kv_norm, w_kv, b_knorm, mem_norm, w_mem_kv, mem_qnorm, mem_knorm, w_out, w_gate_up, w_down, loss_target, m_norm_mix, m_norm_ffn, m_a_w_in, m_a_lb_logits, m_a_onorm, m_b_w_in, m_b_qnorm, m_kv_norm, m_w_kv, m_b_knorm, m_mem_norm, m_w_mem_kv, m_mem_qnorm, m_mem_knorm, m_w_out, m_w_gate_up, m_w_down, v_norm_mix, v_norm_ffn, v_a_w_in, v_a_lb_logits, v_a_onorm, v_b_w_in, v_b_qnorm, v_kv_norm, v_w_kv, v_b_knorm, v_mem_norm, v_w_mem_kv, v_mem_qnorm, v_mem_knorm, v_w_out, v_w_gate_up, v_w_down):
    names = ("norm_mix", "norm_ffn", "a_w_in", "a_lb_logits", "a_onorm", "b_w_in", "b_qnorm", "kv_norm", "w_kv", "b_knorm",
             "mem_norm", "w_mem_kv", "mem_qnorm", "mem_knorm", "w_out", "w_gate_up", "w_down")
    w = dict(zip(names, (norm_mix, norm_ffn, a_w_in, a_lb_logits, a_onorm, b_w_in, b_qnorm, kv_norm, w_kv, b_knorm,
                         mem_norm, w_mem_kv, mem_qnorm, mem_knorm, w_out, w_gate_up, w_down)))
    m = dict(zip(names, (m_norm_mix, m_norm_ffn, m_a_w_in, m_a_lb_logits, m_a_onorm, m_b_w_in, m_b_qnorm, m_kv_norm, m_w_kv,
                         m_b_knorm, m_mem_norm, m_w_mem_kv, m_mem_qnorm, m_mem_knorm, m_w_out, m_w_gate_up, m_w_down)))
    v = dict(zip(names, (v_norm_mix, v_norm_ffn, v_a_w_in, v_a_lb_logits, v_a_onorm, v_b_w_in, v_b_qnorm, v_kv_norm, v_w_kv,
                         v_b_knorm, v_mem_norm, v_w_mem_kv, v_mem_qnorm, v_mem_knorm, v_w_out, v_w_gate_up, v_w_down)))

    first = ["a_w_in", "w_mem_kv0"]
    gathered = _all_gather([_wire_block(w, u) for u in first] + [_pack_flat([a_lb_logits, a_onorm], 8, LANES, F32)],
                           name="gather_first")
    full = {u: g.reshape(-1, g.shape[-1]) for u, g in zip(first, gathered)}
    small_in = gathered[-1].reshape(N_DEV, -1)
    P = {n: w[n] for n in SMALL_REPLICATED}
    P["a_lb_logits"] = small_in[:, :192].reshape(N_DEV, 2, 96).transpose(1, 0, 2).reshape(2, A_WIDTH)
    P["a_onorm"] = small_in[:, 192:288].reshape(1, A_WIDTH)
    later = [["w_out0", "w_gate_up0"], ["w_down0", "w_kv"], ["b_w_in", "w_mem_kv1"], ["w_out1", "w_gate_up1", "w_down1"]]
    first_half, second_half = {}, {}

    def start_first_half(i, after, carried=None):
        first_half[i] = _split_start([_wire_block(w, u) for u in later[i]], False, after, name=f"gather{i}_start",
                                     relations=SIBLING_AND_SAME_CORE, carried=carried)
        return first_half[i]

    token = start_first_half(0, gathered[-1])["token"]
    token = start_first_half(1, token)["token"]

    def forward_point(i, value):
        landed = _split_wait(first_half[i], value, name=f"gather{i}_landed")
        second_half[i], value = _forward_start(landed, value, name=f"gather{i}_forward")
        if i + 2 < len(later):
            value = start_first_half(i + 2, None, carried=value)["carried"]
        return value

    def get_w(unit, after):
        if unit not in full:
            i = [unit in group for group in later].index(True)
            for u, land in zip(later[i], _split_wait(second_half[i], after, name=f"gather{i}_wait")):
                full[u] = land.reshape(-1, land.shape[-1])
        return full[unit]

    sent = []

    def put_g(group):
        units = list(group)
        handle = _split_start([group[u].reshape(N_DEV, -1, group[u].shape[-1]) for u in units], True, None,
                              name=f"scatter{len(sent)}_start")
        sent.append((units, handle))
        return handle["token"]

    sq, gx, gP = _local_step(x[0], mem[0], loss_target[0], get_w, P, put_g, first_dep=token, forward_point=forward_point)
    loss_here = (0.5 * jnp.sum(sq) / D_MODEL).reshape(1)

    received = {}
    for i, (units, handle) in enumerate(sent):
        received.update(zip(units, _split_wait(handle, gx, name=f"scatter{i}_wait")))
    out = {"grad": {}, "delta": {}, "new_m": {}, "new_v": {}}
    for n in BIG:
        shape = w[n].shape
        as3 = lambda a: a.reshape((-1,) + shape[-2:])
        mine = [u for u, (wn, _, _) in UNITS.items() if wn == n]
        col = UNITS[mine[0]][2]
        flip = (lambda a: jnp.swapaxes(a, 1, 2)) if n in TRANSPOSED_UPDATE else (lambda a: a)
        res = _reduce_adamw([received[u] for u in mine], flip(as3(w[n])), flip(as3(m[n])), flip(as3(v[n])),
                            col=col and n not in TRANSPOSED_UPDATE, tr=ADAMW_ROW_TILE[n], name=f"adamw_{n}")
        for kind, r in zip(("grad", "delta", "new_m", "new_v"), res):
            out[kind][n] = flip(r).reshape(shape)

    full_shapes = [(2, A_WIDTH) if n == "a_lb_logits" else (1, A_WIDTH) if n == "a_onorm" else w[n].shape for n in SMALL_ORDER]
    n_small = sum(_prod(s) for s in full_shapes) + 1
    rows_small = -(-n_small // (8 * LANES)) * 8
    g_all, = _all_gather([_pack_flat([gP[n] for n in SMALL_ORDER] + [loss_here], rows_small, LANES, F32)],
                         name="gather_small_grads")
    summed = _unpack_flat(_sum_sources(g_all, tr=rows_small, name="sum_small_grads"), full_shapes + [(1,)])
    g_small = dict(zip(SMALL_ORDER, summed))
    loss = summed[-1].reshape(())
    me = 4 * lax.axis_index("x") + 2 * lax.axis_index("y") + lax.axis_index("c")
    for n in SMALL_SHARDED:
        g_small[n] = lax.dynamic_slice_in_dim(g_small[n], me * 96, 96, axis=1)
    shapes = [w[n].shape for n in SMALL_ORDER]
    rows_upd = -(-sum(_prod(s) for s in shapes) // (8 * LANES)) * 8
    pk = lambda d: _pack_flat([d[n] for n in SMALL_ORDER], rows_upd, LANES, F32)
    res = _adamw(pk(g_small)[None], pk(w)[None], pk(m)[None], pk(v)[None], tr=rows_upd, name="adamw_small")
    out["grad"].update(g_small)
    for kind, packed in zip(("delta", "new_m", "new_v"), res):
        out[kind].update(zip(SMALL_ORDER, _unpack_flat(packed[0], shapes)))

    return (loss, gx[None], *[out["grad"][n] for n in names], *[out["delta"][n] for n in names],
            *[out["new_m"][n] for n in names], *[out["new_v"][n] for n in names])
```

```python
import functools

import jax
import jax.numpy as jnp
from jax import lax
from jax.experimental import pallas as pl
from jax.experimental.pallas import tpu as pltpu

F32 = jnp.float32
BF16 = jnp.bfloat16

N_DEV = 8
D_MODEL = 1024
HEAD_DIM = 128
A_HEADS = 6
A_WIDTH = A_HEADS * HEAD_DIM
CHUNK = 64
B_HEADS = 6
B_WIDTH = B_HEADS * HEAD_DIM
DILATIONS = (1, 4, 16)
SPAN = 128
N_GROUPS = 3
ROPE_THETA = 10000.0
MEM_TOKENS = 256
MEM_HEADS = 4
MEM_HEAD_DIM = 64
MEM_WIDTH = MEM_HEADS * MEM_HEAD_DIM
FFN_HIDDEN = 2816
EPS = 1e-6

ADAM_LR = 0.001
ADAM_B1 = 0.9
ADAM_B2 = 0.999
ADAM_EPS = 1e-08
ADAM_WD = 0.01
ADAM_STEP = 10

V7X_VMEM_LIMIT_BYTES = 56 * 1024 * 1024

NT_DIMS = (((1,), (1,)), ((), ()))
TN_DIMS = (((0,), (0,)), ((), ()))


def _cp(*sem):
    return pltpu.CompilerParams(dimension_semantics=sem, vmem_limit_bytes=V7X_VMEM_LIMIT_BYTES)


def _dot(a, b):
    return jnp.dot(a.astype(BF16), b.astype(BF16), preferred_element_type=F32)


def _dot_nt(a, b):
    return lax.dot_general(a.astype(BF16), b.astype(BF16), NT_DIMS, preferred_element_type=F32)


def _dot_tn(a, b):
    return lax.dot_general(a.astype(BF16), b.astype(BF16), TN_DIMS, preferred_element_type=F32)


def _dot3(m01, x):
    hi = x.astype(BF16)
    r1 = x - hi.astype(F32)
    mid = r1.astype(BF16)
    lo = (r1 - mid.astype(F32)).astype(BF16)
    d = functools.partial(jnp.dot, preferred_element_type=F32)
    return d(m01, hi) + d(m01, mid) + d(m01, lo)


def _sigmoid(x):
    return 1.0 / (1.0 + jnp.exp(-x))


def _full(shape):
    return pl.BlockSpec(shape, lambda *_: (0,) * len(shape))


def _dep(body, n_in, dep):
    if dep is None:
        return body, [], []

    def with_dep(*refs):
        return body(*refs[:n_in], *refs[n_in + 1:])

    return with_dep, [pl.BlockSpec(memory_space=pl.ANY)], [dep]


def _rms_matmul(x, g, w, *, tt, tn, wt, name, out_dtype=F32, dep=None):
    T, K = x.shape
    N = w.shape[0] if wt else w.shape[1]

    def kernel_body(x_ref, g_ref, w_ref, y_ref, xn_ref):
        xf = x_ref[...]
        r = lax.rsqrt(jnp.mean(xf * xf, axis=-1, keepdims=True) + EPS)
        xn = (xf * r * g_ref[...]).astype(BF16)
        xn_ref[...] = xn
        for j in range(N // tn):
            cols = slice(j * tn, (j + 1) * tn)
            y = _dot_nt(xn, w_ref[cols, :]) if wt else _dot(xn, w_ref[:, cols])
            y_ref[:, cols] = y.astype(out_dtype)

    body, dep_specs, dep_args = _dep(kernel_body, 3, dep)
    return pl.pallas_call(
        body, grid=(T // tt,),
        in_specs=[pl.BlockSpec((tt, K), lambda i: (i, 0)), _full((1, K)), _full(w.shape)] + dep_specs,
        out_specs=[pl.BlockSpec((tt, N), lambda i: (i, 0)), pl.BlockSpec((tt, K), lambda i: (i, 0))],
        out_shape=[jax.ShapeDtypeStruct((T, N), out_dtype), jax.ShapeDtypeStruct((T, K), BF16)],
        compiler_params=_cp("parallel"), name=name)(x, g, w, *dep_args)


def _mm_res(res, a1, a2, w, *, tt, name):
    T, K1 = a1.shape
    K2 = a2.shape[1]
    N = w.shape[1]

    def body(r_ref, a1_ref, a2_ref, w_ref, o_ref):
        o_ref[...] = r_ref[...] + _dot(a1_ref[...], w_ref[:K1, :]) + _dot(a2_ref[...], w_ref[K1:, :])

    return pl.pallas_call(
        body, grid=(T // tt,),
        in_specs=[pl.BlockSpec((tt, N), lambda i: (i, 0)), pl.BlockSpec((tt, K1), lambda i: (i, 0)),
                  pl.BlockSpec((tt, K2), lambda i: (i, 0)), _full((K1 + K2, N))],
        out_specs=pl.BlockSpec((tt, N), lambda i: (i, 0)),
        out_shape=jax.ShapeDtypeStruct((T, N), F32),
        compiler_params=_cp("parallel"), name=name)(res, a1, a2, w)


def _swiglu_down(h, gu, wd, *, tt, name):
    T, D = h.shape
    Fh = wd.shape[0]

    def body(h_ref, gt_ref, up_ref, w_ref, o_ref):
        gt = gt_ref[...].astype(F32)
        act = gt * _sigmoid(gt) * up_ref[...].astype(F32)
        o_ref[...] = h_ref[...] + _dot(act, w_ref[...])

    return pl.pallas_call(
        body, grid=(T // tt,),
        in_specs=[pl.BlockSpec((tt, D), lambda i: (i, 0)), pl.BlockSpec((tt, Fh), lambda i: (i, 0)),
                  pl.BlockSpec((tt, Fh), lambda i: (i, 1)), _full((Fh, D))],
        out_specs=pl.BlockSpec((tt, D), lambda i: (i, 0)),
        out_shape=jax.ShapeDtypeStruct((T, D), F32),
        compiler_params=_cp("parallel"), name=name)(h, gu, gu, wd)


def _swiglu_bwd(dh, gu, wd, *, tt, name):
    T, D = dh.shape
    Fh = wd.shape[0]
    last = T // tt - 1

    def body(dh_ref, gt_ref, up_ref, w_ref, dgu_ref, gw_ref, acc):
        @pl.when(pl.program_id(0) == 0)
        def _():
            acc[...] = jnp.zeros_like(acc)

        gt = gt_ref[...].astype(F32)
        up = up_ref[...].astype(F32)
        s = _sigmoid(gt)
        silu = gt * s
        dh16 = dh_ref[...].astype(BF16)
        dact = _dot_nt(dh16, w_ref[...])
        acc[...] += _dot_tn((silu * up).astype(BF16), dh16)
        dgu_ref[:, :Fh] = (dact * up * (s * (1.0 + gt * (1.0 - s)))).astype(BF16)
        dgu_ref[:, Fh:] = (dact * silu).astype(BF16)

        @pl.when(pl.program_id(0) == last)
        def _():
            gw_ref[...] = acc[...].astype(BF16)

    return pl.pallas_call(
        body, grid=(T // tt,),
        in_specs=[pl.BlockSpec((tt, D), lambda i: (i, 0)), pl.BlockSpec((tt, Fh), lambda i: (i, 0)),
                  pl.BlockSpec((tt, Fh), lambda i: (i, 1)), _full((Fh, D))],
        out_specs=[pl.BlockSpec((tt, 2 * Fh), lambda i: (i, 0)), _full((Fh, D))],
        out_shape=[jax.ShapeDtypeStruct((T, 2 * Fh), BF16), jax.ShapeDtypeStruct((Fh, D), BF16)],
        scratch_shapes=[pltpu.VMEM((Fh, D), F32)],
        compiler_params=_cp("arbitrary"), name=name)(dh, gu, gu, wd)


def _out_proj_bwd(dy, a1, a2, w, *, tt, name):
    T, N = dy.shape
    K1, K2 = a1.shape[1], a2.shape[1]
    K = K1 + K2
    last = T // tt - 1

    def body(dy_ref, a1_ref, a2_ref, w_ref, da_ref, gw_ref, acc):
        @pl.when(pl.program_id(0) == 0)
        def _():
            acc[...] = jnp.zeros_like(acc)

        dy16 = dy_ref[...].astype(BF16)
        da_ref[...] = _dot_nt(dy16, w_ref[...])
        acc[:K1, :] += _dot_tn(a1_ref[...], dy16)
        acc[K1:, :] += _dot_tn(a2_ref[...], dy16)

        @pl.when(pl.program_id(0) == last)
        def _():
            gw_ref[...] = acc[...].astype(BF16)

    return pl.pallas_call(
        body, grid=(T // tt,),
        in_specs=[pl.BlockSpec((tt, N), lambda i: (i, 0)), pl.BlockSpec((tt, K1), lambda i: (i, 0)),
                  pl.BlockSpec((tt, K2), lambda i: (i, 0)), _full((K, N))],
        out_specs=[pl.BlockSpec((tt, K), lambda i: (i, 0)), _full((K, N))],
        out_shape=[jax.ShapeDtypeStruct((T, K), F32), jax.ShapeDtypeStruct((K, N), BF16)],
        scratch_shapes=[pltpu.VMEM((K, N), F32)],
        compiler_params=_cp("arbitrary"), name=name)(dy, a1, a2, w)


def _mm_tn(a, b, *, tt, tka, name):
    T, Ka = a.shape
    N = b.shape[1]
    last = T // tt - 1

    def body(a_ref, b_ref, o_ref, acc):
        @pl.when(pl.program_id(1) == 0)
        def _():
            acc[...] = jnp.zeros_like(acc)

        acc[...] += _dot_tn(a_ref[...], b_ref[...])

        @pl.when(pl.program_id(1) == last)
        def _():
            o_ref[...] = acc[...].astype(BF16)

    return pl.pallas_call(
        body, grid=(Ka // tka, T // tt),
        in_specs=[pl.BlockSpec((tt, tka), lambda j, t: (t, j)), pl.BlockSpec((tt, N), lambda j, t: (t, 0))],
        out_specs=pl.BlockSpec((tka, N), lambda j, t: (j, 0)),
        out_shape=jax.ShapeDtypeStruct((Ka, N), BF16),
        scratch_shapes=[pltpu.VMEM((tka, N), F32)],
        compiler_params=_cp("parallel", "arbitrary"), name=name)(a, b)


def _mm_tn_pieces(pieces, b, *, tt, name):
    n = len(pieces)
    T = b.shape[0]
    N = b.shape[1]
    widths = [p.shape[1] for p in pieces]
    Ka = sum(widths)
    last = T // tt - 1

    def body(*refs):
        p_refs = refs[:n]
        b_ref, o_ref, acc = refs[n:]

        @pl.when(pl.program_id(0) == 0)
        def _():
            acc[...] = jnp.zeros_like(acc)

        bv = b_ref[...].astype(BF16)
        off = 0
        for p_ref, wd in zip(p_refs, widths):
            acc[off:off + wd, :] += _dot_tn(p_ref[...], bv)
            off += wd

        @pl.when(pl.program_id(0) == last)
        def _():
            o_ref[...] = acc[...].astype(BF16)

    return pl.pallas_call(
        body, grid=(T // tt,),
        in_specs=[pl.BlockSpec((tt, wd), lambda t: (t, 0)) for wd in widths] + [pl.BlockSpec((tt, N), lambda t: (t, 0))],
        out_specs=_full((Ka, N)), out_shape=jax.ShapeDtypeStruct((Ka, N), BF16),
        scratch_shapes=[pltpu.VMEM((Ka, N), F32)],
        compiler_params=_cp("arbitrary"), name=name)(*pieces, b)


def _rms_bwd_dx(x, g, w, dy, dres, *, tt, wt, name, dep=None):
    pieces = list(dy) if isinstance(dy, (list, tuple)) else [dy]
    n = len(pieces)
    widths = [p.shape[1] for p in pieces]
    T, K = x.shape

    def kernel_body(x_ref, g_ref, w_ref, *rest):
        dy_refs = rest[:n]
        dres_ref, dx_ref, dg_ref = rest[n:]

        @pl.when(pl.program_id(0) == 0)
        def _():
            dg_ref[...] = jnp.zeros_like(dg_ref)

        if n == 1:
            dxn = (_dot if wt else _dot_nt)(dy_refs[0][...], w_ref[...])
        else:
            dxn, off = 0.0, 0
            for dy_ref, wd in zip(dy_refs, widths):
                dxn = dxn + _dot(dy_ref[...], w_ref[off:off + wd, :])
                off += wd
        xf = x_ref[...]
        r = lax.rsqrt(jnp.mean(xf * xf, axis=-1, keepdims=True) + EPS)
        xhat = xf * r
        dg_ref[...] += jnp.sum(dxn * xhat, axis=0, keepdims=True)
        dxhat = dxn * g_ref[...]
        dx_ref[...] = dres_ref[...] + r * (dxhat - xhat * jnp.mean(dxhat * xhat, axis=-1, keepdims=True))

    assert n == 1 or wt
    body, dep_specs, dep_args = _dep(kernel_body, 4 + n, dep)
    return pl.pallas_call(
        body, grid=(T // tt,),
        in_specs=[pl.BlockSpec((tt, K), lambda i: (i, 0)), _full((1, K)), _full(w.shape)]
        + [pl.BlockSpec((tt, wd), lambda i: (i, 0)) for wd in widths]
        + [pl.BlockSpec((tt, K), lambda i: (i, 0))] + dep_specs,
        out_specs=[pl.BlockSpec((tt, K), lambda i: (i, 0)), _full((1, K))],
        out_shape=[jax.ShapeDtypeStruct((T, K), F32), jax.ShapeDtypeStruct((1, K), F32)],
        compiler_params=_cp("arbitrary"), name=name)(x, g, w, *pieces, dres, *dep_args)


def _loss_kernel(y, tgt, *, tt, name):
    T, D = y.shape

    def body(y_ref, t_ref, dy_ref, acc_ref):
        @pl.when(pl.program_id(0) == 0)
        def _():
            acc_ref[...] = jnp.zeros_like(acc_ref)

        e = y_ref[...] - t_ref[...]
        dy_ref[...] = e * (1.0 / D)
        acc_ref[...] += jnp.sum(e * e, axis=0, keepdims=True)

    return pl.pallas_call(
        body, grid=(T // tt,),
        in_specs=[pl.BlockSpec((tt, D), lambda i: (i, 0)), pl.BlockSpec((tt, D), lambda i: (i, 0))],
        out_specs=[pl.BlockSpec((tt, D), lambda i: (i, 0)), _full((1, D))],
        out_shape=[jax.ShapeDtypeStruct((T, D), F32), jax.ShapeDtypeStruct((1, D), F32)],
        compiler_params=_cp("arbitrary"), name=name)(y, tgt)


HGRN_TB = 512
HGRN_NCH = HGRN_TB // CHUNK
HGRN_HPB = 6


def _hgrn_chunk_fwd(q, z, lbv, tril01):
    sig = _sigmoid(z)
    f = lbv + (1.0 - lbv) * sig
    kk = 1.0 - f
    b = _dot3(tril01, jnp.log(f))
    bend = b[CHUNK - 1:CHUNK, :]
    sq = _sigmoid(q)
    eb = jnp.exp(b)
    emb = jnp.exp(-b)
    eo = jnp.exp(bend - b)
    dec = jnp.exp(bend)
    return sig, f, kk, sq, eb, emb, eo, dec


def _hgrn2_fwd(proj, lb, *, name):
    T = proj.shape[0]
    nT = T // HGRN_TB
    nC = T // CHUNK

    def body(q_ref, z_ref, v_ref, lb_ref, o_ref, st_ref, state):
        @pl.when(pl.program_id(1) == 0)
        def _():
            state[...] = jnp.zeros_like(state)

        row = lax.broadcasted_iota(jnp.int32, (CHUNK, CHUNK), 0)
        col = lax.broadcasted_iota(jnp.int32, (CHUNK, CHUNK), 1)
        causal = row >= col
        tril01 = causal.astype(BF16)

        def chunk(c, carry):
            rows = pl.ds(pl.multiple_of(c * CHUNK, CHUNK), CHUNK)
            for hh in range(HGRN_HPB):
                sl = slice(hh * HEAD_DIM, (hh + 1) * HEAD_DIM)
                q = q_ref[rows, sl]
                v = v_ref[rows, sl].astype(BF16)
                sig, f, kk, sq, eb, emb, eo, dec = _hgrn_chunk_fwd(q, z_ref[rows, sl], lb_ref[:, sl], tril01)
                qi = (q * sq * eb).astype(BF16)
                ki = (kk * emb).astype(BF16)
                ko = (kk * eo).astype(BF16)
                st = state[hh]
                att = jnp.where(causal, _dot_nt(qi, ki), 0.0)
                o_ref[rows, sl] = _dot(att, v) + _dot_nt(qi, st)
                st_ref[c, hh] = st
                state[hh] = st * dec + _dot_tn(v, ko)
            return carry

        lax.fori_loop(0, HGRN_NCH, chunk, 0)

    W = HGRN_HPB * HEAD_DIM
    nG = A_HEADS // HGRN_HPB
    hb = lambda off: pl.BlockSpec((HGRN_TB, W), lambda h, i: (i, off + h))
    return pl.pallas_call(
        body, grid=(nG, nT),
        in_specs=[hb(0), hb(nG), hb(2 * nG), pl.BlockSpec((1, W), lambda h, i: (0, h))],
        out_specs=[hb(0), pl.BlockSpec((HGRN_NCH, HGRN_HPB, HEAD_DIM, HEAD_DIM), lambda h, i: (i, h, 0, 0))],
        out_shape=[jax.ShapeDtypeStruct((T, A_WIDTH), F32), jax.ShapeDtypeStruct((nC, A_HEADS, HEAD_DIM, HEAD_DIM), F32)],
        scratch_shapes=[pltpu.VMEM((HGRN_HPB, HEAD_DIM, HEAD_DIM), F32)],
        compiler_params=_cp("parallel", "arbitrary"), name=name)(proj, proj, proj, lb)


def _hgrn2_bwd(proj, lb, st_all, do, *, name):
    T = proj.shape[0]
    nT = T // HGRN_TB

    def body(q_ref, z_ref, v_ref, lb_ref, st_ref, do_ref, dq_ref, dz_ref, dv_ref, dlb_ref, dstate):
        @pl.when(pl.program_id(1) == 0)
        def _():
            dstate[...] = jnp.zeros_like(dstate)
            dlb_ref[...] = jnp.zeros_like(dlb_ref)

        row = lax.broadcasted_iota(jnp.int32, (CHUNK, CHUNK), 0)
        col = lax.broadcasted_iota(jnp.int32, (CHUNK, CHUNK), 1)
        causal = row >= col
        tril01 = causal.astype(BF16)
        triu01 = (row <= col).astype(BF16)

        def chunk(cc, carry):
            c = HGRN_NCH - 1 - cc
            rows = pl.ds(pl.multiple_of(c * CHUNK, CHUNK), CHUNK)
            for hh in range(HGRN_HPB):
                sl = slice(hh * HEAD_DIM, (hh + 1) * HEAD_DIM)
                lbv = lb_ref[:, sl]
                q = q_ref[rows, sl]
                v = v_ref[rows, sl].astype(BF16)
                sig, f, kk, sq, eb, emb, eo, dec = _hgrn_chunk_fwd(q, z_ref[rows, sl], lbv, tril01)
                qi32 = q * sq * eb
                ki32 = kk * emb
                ko32 = kk * eo
                qi, ki, ko = qi32.astype(BF16), ki32.astype(BF16), ko32.astype(BF16)
                att = jnp.where(causal, _dot_nt(qi, ki), 0.0).astype(BF16)
                dout = do_ref[rows, sl].astype(BF16)
                st = st_ref[c, hh]
                dst = dstate[hh]
                dst16 = dst.astype(BF16)
                datt = jnp.where(causal, _dot_nt(dout, v), 0.0).astype(BF16)
                dqi = _dot(datt, ki) + _dot(dout, st)
                dki = _dot_tn(datt, qi)
                dv_ref[rows, sl] = (_dot_tn(att, dout) + _dot_nt(ko, dst16)).astype(BF16)
                dko = _dot(v, dst16)
                ddec = jnp.sum(dst * st, axis=0, keepdims=True)
                dstate[hh] = dst * dec + _dot_tn(dout, qi)
                dkk = dki * emb + dko * eo
                db = dqi * qi32 - dki * ki32 - dko * ko32
                dbend = jnp.sum(dko * ko32, axis=0, keepdims=True) + ddec * dec
                dlogf = _dot3(triu01, db) + dbend
                df = dlogf / f - dkk
                dz_ref[rows, sl] = (df * (1.0 - lbv) * sig * (1.0 - sig)).astype(BF16)
                dlb_ref[:, sl] += jnp.sum(df * (1.0 - sig), axis=0, keepdims=True)
                dq_ref[rows, sl] = (dqi * eb * (sq * (1.0 + q * (1.0 - sq)))).astype(BF16)
            return carry

        lax.fori_loop(0, HGRN_NCH, chunk, 0)

    W = HGRN_HPB * HEAD_DIM
    nG = A_HEADS // HGRN_HPB
    hb = lambda off: pl.BlockSpec((HGRN_TB, W), lambda h, i: (nT - 1 - i, off + h))
    hlb = pl.BlockSpec((1, W), lambda h, i: (0, h))
    o16 = jax.ShapeDtypeStruct((T, A_WIDTH), BF16)
    return pl.pallas_call(
        body, grid=(nG, nT),
        in_specs=[hb(0), hb(nG), hb(2 * nG), hlb,
                  pl.BlockSpec((HGRN_NCH, HGRN_HPB, HEAD_DIM, HEAD_DIM), lambda h, i: (nT - 1 - i, h, 0, 0)), hb(0)],
        out_specs=[hb(0), hb(0), hb(0), hlb],
        out_shape=[o16, o16, o16, jax.ShapeDtypeStruct((1, A_WIDTH), F32)],
        scratch_shapes=[pltpu.VMEM((HGRN_HPB, HEAD_DIM, HEAD_DIM), F32)],
        compiler_params=_cp("parallel", "arbitrary"), name=name)(proj, proj, proj, lb, st_all, do)


def _head_rms(x):
    r = lax.rsqrt(jnp.mean(x * x, axis=-1, keepdims=True) + EPS)
    return x * r, r


def _head_rms_bwd(dxhat, xhat, r):
    return r * (dxhat - xhat * jnp.mean(dxhat * xhat, axis=-1, keepdims=True))


def _a_post_fwd(o, proj, onorm, *, tt, name):
    T = o.shape[0]

    def body(o_ref, g_ref, w_ref, y_ref):
        for h in range(A_HEADS):
            sl = slice(h * HEAD_DIM, (h + 1) * HEAD_DIM)
            xhat, _ = _head_rms(o_ref[:, sl])
            g = g_ref[:, sl]
            y_ref[:, sl] = xhat * w_ref[:, sl] * (g * _sigmoid(g))

    blk = lambda c: pl.BlockSpec((tt, A_WIDTH), lambda i: (i, c))
    return pl.pallas_call(
        body, grid=(T // tt,), in_specs=[blk(0), blk(3), _full((1, A_WIDTH))], out_specs=blk(0),
        out_shape=jax.ShapeDtypeStruct((T, A_WIDTH), F32),
        compiler_params=_cp("parallel"), name=name)(o, proj, onorm)


def _a_post_bwd(o, proj, onorm, dmix, *, tt, name, dep=None):
    T = o.shape[0]

    def kernel_body(o_ref, g_ref, w_ref, dy_ref, do_ref, dg_ref, dw_ref):
        @pl.when(pl.program_id(0) == 0)
        def _():
            dw_ref[...] = jnp.zeros_like(dw_ref)

        for h in range(A_HEADS):
            sl = slice(h * HEAD_DIM, (h + 1) * HEAD_DIM)
            xhat, r = _head_rms(o_ref[:, sl])
            g = g_ref[:, sl]
            s = _sigmoid(g)
            dy = dy_ref[:, sl]
            w = w_ref[:, sl]
            dg_ref[:, sl] = (dy * xhat * w * (s * (1.0 + g * (1.0 - s)))).astype(BF16)
            dyn = dy * (g * s)
            dw_ref[:, sl] += jnp.sum(dyn * xhat, axis=0, keepdims=True)
            do_ref[:, sl] = _head_rms_bwd(dyn * w, xhat, r)

    blk = lambda c: pl.BlockSpec((tt, A_WIDTH), lambda i: (i, c))
    body, dep_specs, dep_args = _dep(kernel_body, 4, dep)
    return pl.pallas_call(
        body, grid=(T // tt,), in_specs=[blk(0), blk(3), _full((1, A_WIDTH)), blk(0)] + dep_specs,
        out_specs=[blk(0), blk(0), _full((1, A_WIDTH))],
        out_shape=[jax.ShapeDtypeStruct((T, A_WIDTH), F32), jax.ShapeDtypeStruct((T, A_WIDTH), BF16),
                   jax.ShapeDtypeStruct((1, A_WIDTH), F32)],
        compiler_params=_cp("arbitrary"), name=name)(o, proj, onorm, dmix, *dep_args)


def _mem_head_masks(n):
    lane = lax.broadcasted_iota(jnp.int32, (n, MEM_WIDTH), 1)
    return [(lane >= m * MEM_HEAD_DIM) & (lane < (m + 1) * MEM_HEAD_DIM) for m in range(MEM_HEADS)]


def _mem_head_rms(x, masks):
    x2 = x * x
    r = jnp.zeros_like(x)
    for mk in masks:
        ms = jnp.sum(jnp.where(mk, x2, 0.0), axis=-1, keepdims=True) * (1.0 / MEM_HEAD_DIM)
        r = jnp.where(mk, lax.rsqrt(ms + EPS), r)
    return x * r, r


def _mem_head_rms_bwd(dxhat, xhat, r, masks):
    t = dxhat * xhat
    m = jnp.zeros_like(t)
    for mk in masks:
        m = jnp.where(mk, jnp.sum(jnp.where(mk, t, 0.0), axis=-1, keepdims=True) * (1.0 / MEM_HEAD_DIM), m)
    return r * (dxhat - xhat * m)


MEM_SCALE = MEM_HEAD_DIM ** -0.5


def _mem_attn_fwd(proj, qcol, mkv, qn_w, kn_w, *, tt, name):
    T = proj.shape[0]

    def body(q_ref, k_ref, v_ref, qw_ref, kw_ref, o_ref):
        qmasks = _mem_head_masks(tt)
        kmasks = _mem_head_masks(MEM_TOKENS)
        qhat, _ = _mem_head_rms(q_ref[...], qmasks)
        qn = qhat * qw_ref[...]
        khat, _ = _mem_head_rms(k_ref[...], kmasks)
        kn = (khat * kw_ref[...]).astype(BF16)
        v = v_ref[...].astype(BF16)
        out = jnp.zeros((tt, MEM_WIDTH), F32)
        for m in range(MEM_HEADS):
            s = _dot_nt(jnp.where(qmasks[m], qn, 0.0), kn) * MEM_SCALE
            s = s - jnp.max(s, axis=-1, keepdims=True)
            p = jnp.exp(s)
            p = p / jnp.sum(p, axis=-1, keepdims=True)
            out = jnp.where(qmasks[m], _dot(p, v), out)
        o_ref[...] = out

    return pl.pallas_call(
        body, grid=(T // tt,),
        in_specs=[pl.BlockSpec((tt, MEM_WIDTH), lambda i: (i, qcol)), pl.BlockSpec((MEM_TOKENS, MEM_WIDTH), lambda i: (0, 0)),
                  pl.BlockSpec((MEM_TOKENS, MEM_WIDTH), lambda i: (0, 1)), _full((1, MEM_WIDTH)), _full((1, MEM_WIDTH))],
        out_specs=pl.BlockSpec((tt, MEM_WIDTH), lambda i: (i, 0)),
        out_shape=jax.ShapeDtypeStruct((T, MEM_WIDTH), F32),
        compiler_params=_cp("parallel"), name=name)(proj, mkv, mkv, qn_w, kn_w)


def _mem_attn_bwd(proj, qcol, mkv, qn_w, kn_w, dmix, *, tt, name):
    T = proj.shape[0]
    nsteps = T // tt
    ocol = (dmix.shape[1] - MEM_WIDTH) // MEM_WIDTH

    def body(q_ref, k_ref, v_ref, qw_ref, kw_ref, do_ref, dq_ref, dkv_ref, dqw_ref, dkw_ref, dk_acc, dv_acc):
        step = pl.program_id(0)

        @pl.when(step == 0)
        def _():
            dk_acc[...] = jnp.zeros_like(dk_acc)
            dv_acc[...] = jnp.zeros_like(dv_acc)
            dqw_ref[...] = jnp.zeros_like(dqw_ref)

        qmasks = _mem_head_masks(tt)
        kmasks = _mem_head_masks(MEM_TOKENS)
        qhat, qr = _mem_head_rms(q_ref[...], qmasks)
        qn = qhat * qw_ref[...]
        khat, kr = _mem_head_rms(k_ref[...], kmasks)
        kn = (khat * kw_ref[...]).astype(BF16)
        v = v_ref[...].astype(BF16)
        dout = do_ref[...]
        dqn = jnp.zeros((tt, MEM_WIDTH), F32)
        dkn = jnp.zeros((MEM_TOKENS, MEM_WIDTH), F32)
        dvv = jnp.zeros((MEM_TOKENS, MEM_WIDTH), F32)
        for m in range(MEM_HEADS):
            qm = jnp.where(qmasks[m], qn, 0.0).astype(BF16)
            s = _dot_nt(qm, kn) * MEM_SCALE
            s = s - jnp.max(s, axis=-1, keepdims=True)
            p = jnp.exp(s)
            p = p / jnp.sum(p, axis=-1, keepdims=True)
            dom = jnp.where(qmasks[m], dout, 0.0).astype(BF16)
            dp = _dot_nt(dom, v)
            ds = (p * (dp - jnp.sum(p * dp, axis=-1, keepdims=True)) * MEM_SCALE).astype(BF16)
            dqn = jnp.where(qmasks[m], _dot(ds, kn), dqn)
            dkn = jnp.where(kmasks[m], _dot_tn(ds, qm), dkn)
            dvv = jnp.where(kmasks[m], _dot_tn(p, dom), dvv)
        dqw_ref[...] += jnp.sum(dqn * qhat, axis=0, keepdims=True)
        dq_ref[...] = _mem_head_rms_bwd(dqn * qw_ref[...], qhat, qr, qmasks).astype(BF16)
        dk_acc[...] += dkn
        dv_acc[...] += dvv

        @pl.when(step == nsteps - 1)
        def _():
            dk = dk_acc[...]
            dkw_ref[...] = jnp.sum(dk * khat, axis=0, keepdims=True)
            dkv_ref[:, :MEM_WIDTH] = _mem_head_rms_bwd(dk * kw_ref[...], khat, kr, kmasks)
            dkv_ref[:, MEM_WIDTH:] = dv_acc[...]

    return pl.pallas_call(
        body, grid=(nsteps,),
        in_specs=[pl.BlockSpec((tt, MEM_WIDTH), lambda i: (i, qcol)), pl.BlockSpec((MEM_TOKENS, MEM_WIDTH), lambda i: (0, 0)),
                  pl.BlockSpec((MEM_TOKENS, MEM_WIDTH), lambda i: (0, 1)), _full((1, MEM_WIDTH)), _full((1, MEM_WIDTH)),
                  pl.BlockSpec((tt, MEM_WIDTH), lambda i: (i, ocol))],
        out_specs=[pl.BlockSpec((tt, MEM_WIDTH), lambda i: (i, 0)), _full((MEM_TOKENS, 2 * MEM_WIDTH)),
                   _full((1, MEM_WIDTH)), _full((1, MEM_WIDTH))],
        out_shape=[jax.ShapeDtypeStruct((T, MEM_WIDTH), BF16), jax.ShapeDtypeStruct((MEM_TOKENS, 2 * MEM_WIDTH), F32),
                   jax.ShapeDtypeStruct((1, MEM_WIDTH), F32), jax.ShapeDtypeStruct((1, MEM_WIDTH), F32)],
        scratch_shapes=[pltpu.VMEM((MEM_TOKENS, MEM_WIDTH), F32), pltpu.VMEM((MEM_TOKENS, MEM_WIDTH), F32)],
        compiler_params=_cp("arbitrary"), name=name)(proj, mkv, mkv, qn_w, kn_w, dmix)


HALF = HEAD_DIM // 2
ATT_SCALE = HEAD_DIM ** -0.5
NEG = -1e30


def _rope_tables(T):
    inv = ROPE_THETA ** (-jnp.arange(HALF, dtype=F32) / HALF)
    ang = jnp.arange(T, dtype=F32)[:, None] * inv[None, :]
    cos, sin = jnp.cos(ang), jnp.sin(ang)
    return jnp.concatenate([cos, cos], axis=-1), jnp.concatenate([-sin, sin], axis=-1)


def _rope(x, cosf, sinsg):
    return x * cosf + pltpu.roll(x, HALF, 1) * sinsg


def _rope_bwd(dy, cosf, sinsg):
    return dy * cosf + pltpu.roll(dy * sinsg, HALF, 1)


def _headnorm_rope_fwd(x, w_heads, cosf, sinsg, *, col0, n_heads, tt, name):
    T = x.shape[0]
    W = n_heads * HEAD_DIM

    def body(x_ref, w_ref, c_ref, s_ref, y_ref):
        c, s = c_ref[...], s_ref[...]
        for h in range(n_heads):
            sl = slice(h * HEAD_DIM, (h + 1) * HEAD_DIM)
            xhat, _ = _head_rms(x_ref[:, sl])
            y_ref[:, sl] = _rope(xhat * w_ref[:, sl], c, s)

    tbl = pl.BlockSpec((tt, HEAD_DIM), lambda i: (i, 0))
    return pl.pallas_call(
        body, grid=(T // tt,),
        in_specs=[pl.BlockSpec((tt, W), lambda i: (i, col0)), _full((1, W)), tbl, tbl],
        out_specs=pl.BlockSpec((tt, W), lambda i: (i, 0)),
        out_shape=jax.ShapeDtypeStruct((T, W), F32),
        compiler_params=_cp("parallel"), name=name)(x, w_heads, cosf, sinsg)


def _q_prep_bwd(proj, w_heads, cosf, sinsg, dqs, *, tt, name):
    T = proj.shape[0]
    W = N_GROUPS * B_WIDTH

    def body(x_ref, w_ref, c_ref, s_ref, d0, d1, d2, dx_ref, dw_ref):
        @pl.when(pl.program_id(0) == 0)
        def _():
            dw_ref[...] = jnp.zeros_like(dw_ref)

        c, s = c_ref[...], s_ref[...]
        for gi, d_ref in enumerate((d0, d1, d2)):
            for h in range(B_HEADS):
                sl = slice((gi * B_HEADS + h) * HEAD_DIM, (gi * B_HEADS + h + 1) * HEAD_DIM)
                xhat, r = _head_rms(x_ref[:, sl])
                dyn = _rope_bwd(d_ref[:, h * HEAD_DIM:(h + 1) * HEAD_DIM], c, s)
                dw_ref[:, sl] += jnp.sum(dyn * xhat, axis=0, keepdims=True)
                dx_ref[:, sl] = _head_rms_bwd(dyn * w_ref[:, sl], xhat, r).astype(BF16)

    tbl = pl.BlockSpec((tt, HEAD_DIM), lambda i: (i, 0))
    dyb = pl.BlockSpec((tt, B_WIDTH), lambda i: (i, 0))
    return pl.pallas_call(
        body, grid=(T // tt,),
        in_specs=[pl.BlockSpec((tt, W), lambda i: (i, 0)), _full((1, W)), tbl, tbl, dyb, dyb, dyb],
        out_specs=[pl.BlockSpec((tt, W), lambda i: (i, 0)), _full((1, W))],
        out_shape=[jax.ShapeDtypeStruct((T, W), BF16), jax.ShapeDtypeStruct((1, W), F32)],
        compiler_params=_cp("arbitrary"), name=name)(proj, w_heads, cosf, sinsg, *dqs)


def _kv_prep_bwd(kv, w_heads, cosf, sinsg, dks, dvs, *, tt, name):
    T = kv.shape[0]

    def body(x_ref, w_ref, c_ref, s_ref, k0, k1, k2, v0, v1, v2, dx_ref, dw_ref):
        @pl.when(pl.program_id(0) == 0)
        def _():
            dw_ref[...] = jnp.zeros_like(dw_ref)

        c, s = c_ref[...], s_ref[...]
        for h in range(B_HEADS):
            sl = slice(h * HEAD_DIM, (h + 1) * HEAD_DIM)
            vs = slice(B_WIDTH + h * HEAD_DIM, B_WIDTH + (h + 1) * HEAD_DIM)
            xhat, r = _head_rms(x_ref[:, sl])
            dyn = _rope_bwd(k0[:, sl] + k1[:, sl] + k2[:, sl], c, s)
            dw_ref[:, sl] += jnp.sum(dyn * xhat, axis=0, keepdims=True)
            dx_ref[:, sl] = _head_rms_bwd(dyn * w_ref[:, sl], xhat, r).astype(BF16)
            dx_ref[:, vs] = (v0[:, sl] + v1[:, sl] + v2[:, sl]).astype(BF16)

    tbl = pl.BlockSpec((tt, HEAD_DIM), lambda i: (i, 0))
    dyb = pl.BlockSpec((tt, B_WIDTH), lambda i: (i, 0))
    return pl.pallas_call(
        body, grid=(T // tt,),
        in_specs=[dyb, _full((1, B_WIDTH)), tbl, tbl] + [dyb] * 6,
        out_specs=[pl.BlockSpec((tt, 2 * B_WIDTH), lambda i: (i, 0)), _full((1, B_WIDTH))],
        out_shape=[jax.ShapeDtypeStruct((T, 2 * B_WIDTH), BF16), jax.ShapeDtypeStruct((1, B_WIDTH), F32)],
        compiler_params=_cp("arbitrary"), name=name)(kv, w_heads, cosf, sinsg, *dks, *dvs)


def _band_masks(n_is_first=None):
    row = lax.broadcasted_iota(jnp.int32, (SPAN, SPAN), 0)
    col = lax.broadcasted_iota(jnp.int32, (SPAN, SPAN), 1)
    return row >= col, col >= row


def _dil_views(T, d):
    L = T // d
    return L, L // SPAN


def _dil_fwd(qr, kr, kv, gi, d, *, name):
    T = qr.shape[0]
    L, nb = _dil_views(T, d)

    def body(q_ref, kc_ref, kp_ref, vc_ref, vp_ref, o_ref, lse_ref):
        cur_ok, prev_band = _band_masks()
        prev_ok = prev_band & (pl.program_id(1) > 0)
        for h in range(B_HEADS):
            sl = slice(h * HEAD_DIM, (h + 1) * HEAD_DIM)
            q = q_ref[:, sl]
            sc = jnp.where(cur_ok, _dot_nt(q, kc_ref[:, sl]) * ATT_SCALE, NEG)
            sp = jnp.where(prev_ok, _dot_nt(q, kp_ref[:, sl]) * ATT_SCALE, NEG)
            m = jnp.maximum(jnp.max(sc, axis=-1, keepdims=True), jnp.max(sp, axis=-1, keepdims=True))
            pc = jnp.exp(sc - m)
            pp = jnp.exp(sp - m)
            l = jnp.sum(pc, axis=-1, keepdims=True) + jnp.sum(pp, axis=-1, keepdims=True)
            o_ref[:, sl] = (_dot(pc, vc_ref[:, sl]) + _dot(pp, vp_ref[:, sl])) / l
            lse_ref[:, sl] = jnp.broadcast_to(m + jnp.log(l), (SPAN, HEAD_DIM))

    blk = lambda f: pl.BlockSpec((SPAN, B_WIDTH), f)
    cur = lambda r, n: (n, r)
    prev = lambda r, n: (jnp.maximum(n - 1, 0), r)
    ov = jax.ShapeDtypeStruct((L, d * B_WIDTH), F32)
    o, lse = pl.pallas_call(
        body, grid=(d, nb),
        in_specs=[blk(lambda r, n: (n, r * N_GROUPS + gi)), blk(cur), blk(prev),
                  blk(lambda r, n: (n, 2 * r + 1)), blk(lambda r, n: (jnp.maximum(n - 1, 0), 2 * r + 1))],
        out_specs=[blk(cur), blk(cur)], out_shape=[ov, ov],
        compiler_params=_cp("parallel", "arbitrary"), name=name,
    )(qr.reshape(L, d * N_GROUPS * B_WIDTH), kr.reshape(L, d * B_WIDTH), kr.reshape(L, d * B_WIDTH),
      kv.reshape(L, d * 2 * B_WIDTH), kv.reshape(L, d * 2 * B_WIDTH))
    return o.reshape(T, B_WIDTH), lse.reshape(T, B_WIDTH)


def _dil_combine_fwd(os_, lses, *, tt, name):
    T = os_[0].shape[0]

    def body(o0, o1, o2, l0, l1, l2, y_ref, lse_ref):
        a, b, c = l0[...], l1[...], l2[...]
        m = jnp.maximum(jnp.maximum(a, b), c)
        wa, wb, wc = jnp.exp(a - m), jnp.exp(b - m), jnp.exp(c - m)
        den = wa + wb + wc
        y_ref[...] = (wa * o0[...] + wb * o1[...] + wc * o2[...]) / den
        lse_ref[...] = m + jnp.log(den)

    blk = pl.BlockSpec((tt, B_WIDTH), lambda i: (i, 0))
    sh = jax.ShapeDtypeStruct((T, B_WIDTH), F32)
    return pl.pallas_call(
        body, grid=(T // tt,), in_specs=[blk] * 6, out_specs=[blk, blk], out_shape=[sh, sh],
        compiler_params=_cp("parallel"), name=name)(*os_, *lses)


def _dil_bwd_prep(dmix, mix_main, *, tt, name, dep=None):
    T = mix_main.shape[0]

    def kernel_body(dy_ref, y_ref, dd_ref):
        for h in range(B_HEADS):
            sl = slice(h * HEAD_DIM, (h + 1) * HEAD_DIM)
            dd_ref[:, sl] = jnp.broadcast_to(jnp.sum(dy_ref[:, sl] * y_ref[:, sl], axis=-1, keepdims=True), (tt, HEAD_DIM))

    blk = pl.BlockSpec((tt, B_WIDTH), lambda i: (i, 0))
    body, dep_specs, dep_args = _dep(kernel_body, 2, dep)
    return pl.pallas_call(
        body, grid=(T // tt,), in_specs=[blk, blk] + dep_specs, out_specs=blk,
        out_shape=jax.ShapeDtypeStruct((T, B_WIDTH), F32),
        compiler_params=_cp("parallel"), name=name)(dmix, mix_main, *dep_args)


DILS_UNROLL = 4


def _dils_specs(gi, d, nblk):
    blk = lambda f: pl.BlockSpec((SPAN * d, HEAD_DIM), f)
    return {
        "q": blk(lambda h, n: (n, gi * B_HEADS + h)), "q_next": blk(lambda h, n: (jnp.minimum(n + 1, nblk - 1), gi * B_HEADS + h)),
        "cur": blk(lambda h, n: (n, h)), "prev": blk(lambda h, n: (jnp.maximum(n - 1, 0), h)),
        "next": blk(lambda h, n: (jnp.minimum(n + 1, nblk - 1), h)),
        "v": blk(lambda h, n: (n, B_HEADS + h)), "v_prev": blk(lambda h, n: (jnp.maximum(n - 1, 0), B_HEADS + h)),
    }


def _dils_fwd(qr, kr, kv, gi, d, *, name):
    T = qr.shape[0]
    nblk = T // (SPAN * d)
    sp = _dils_specs(gi, d, nblk)

    def body(q_ref, kc_ref, kp_ref, vc_ref, vp_ref, o_ref, lse_ref):
        cur_ok, prev_band = _band_masks()
        prev_ok = prev_band & (pl.program_id(1) > 0)

        def residue(r, carry):
            rows = pl.ds(r, SPAN, stride=d)
            q = q_ref[rows, :]
            sc = jnp.where(cur_ok, _dot_nt(q, kc_ref[rows, :]) * ATT_SCALE, NEG)
            sp_ = jnp.where(prev_ok, _dot_nt(q, kp_ref[rows, :]) * ATT_SCALE, NEG)
            m = jnp.maximum(jnp.max(sc, axis=-1, keepdims=True), jnp.max(sp_, axis=-1, keepdims=True))
            pc = jnp.exp(sc - m)
            pp = jnp.exp(sp_ - m)
            l = jnp.sum(pc, axis=-1, keepdims=True) + jnp.sum(pp, axis=-1, keepdims=True)
            o_ref[rows, :] = (_dot(pc, vc_ref[rows, :]) + _dot(pp, vp_ref[rows, :])) / l
            lse_ref[rows, :] = jnp.broadcast_to(m + jnp.log(l), (SPAN, HEAD_DIM))
            return carry

        lax.fori_loop(0, d, residue, 0, unroll=DILS_UNROLL)

    sh = jax.ShapeDtypeStruct((T, B_WIDTH), F32)
    return pl.pallas_call(
        body, grid=(B_HEADS, nblk), in_specs=[sp["q"], sp["cur"], sp["prev"], sp["v"], sp["v_prev"]],
        out_specs=[sp["cur"], sp["cur"]], out_shape=[sh, sh],
        compiler_params=_cp("parallel", "arbitrary"), name=name)(qr, kr, kr, kv, kv)


DIL_BWD_GROUP = {1: 4, 4: 1, 16: 1}


def _dil_bwd(qr, kr, kv, dmix, lse, dd, gi, d, *, name):
    T = qr.shape[0]
    G = DIL_BWD_GROUP[d]
    band = SPAN * d
    tb = G * band
    nblk = T // tb

    def body(q_ref, dy_ref, lse_ref, dd_ref, kc_ref, kp_ref, vc_ref, vp_ref, dq_ref, dk_ref, dv_ref):
        n = pl.program_id(1)

        @pl.when(n == 0)
        def _():
            dk_ref[...] = jnp.zeros_like(dk_ref)
            dv_ref[...] = jnp.zeros_like(dv_ref)

        cur_ok, prev_band = _band_masks()
        base = pl.multiple_of(n * tb, SPAN)
        for j in range(G):
            def residue(r, carry, j=j):
                off = j * band + r
                rows = pl.ds(off, SPAN, stride=d)
                q, dy = q_ref[rows, :], dy_ref[rows, :]
                lse_h = jnp.max(lse_ref[rows, :], axis=-1, keepdims=True)
                dd_h = jnp.max(dd_ref[rows, :], axis=-1, keepdims=True)
                kc, vc = kc_ref[rows, :], vc_ref[rows, :]
                if j > 0:
                    before = pl.ds(off - band, SPAN, stride=d)
                    kp, vp = kc_ref[before, :], vc_ref[before, :]
                    prev_ok = prev_band
                else:
                    before = pl.ds((G - 1) * band + r, SPAN, stride=d)
                    kp, vp = kp_ref[before, :], vp_ref[before, :]
                    prev_ok = prev_band & (n > 0)
                pc = jnp.exp(jnp.where(cur_ok, _dot_nt(q, kc) * ATT_SCALE, NEG) - lse_h)
                pp = jnp.exp(jnp.where(prev_ok, _dot_nt(q, kp) * ATT_SCALE, NEG) - lse_h)
                dsc = pc * (_dot_nt(dy, vc) - dd_h) * ATT_SCALE
                dsp = pp * (_dot_nt(dy, vp) - dd_h) * ATT_SCALE
                dq_ref[rows, :] = _dot(dsc, kc) + _dot(dsp, kp)
                here = pl.ds(base + off, SPAN, stride=d)
                dk_ref[here, :] += _dot_tn(dsc, q)
                dv_ref[here, :] += _dot_tn(pc, dy)
                there = pl.ds(jnp.maximum(base + off - band, r), SPAN, stride=d)
                dk_ref[there, :] += _dot_tn(dsp, q)
                dv_ref[there, :] += _dot_tn(pp, dy)
                return carry

            lax.fori_loop(0, d, residue, 0, unroll=min(d, DILS_UNROLL))

    blk = lambda f: pl.BlockSpec((tb, HEAD_DIM), f)
    cur = lambda h, n: (n, h)
    prev = lambda h, n: (jnp.maximum(n - 1, 0), h)
    whole = pl.BlockSpec((T, HEAD_DIM), lambda h, n: (0, h))
    sh = jax.ShapeDtypeStruct((T, B_WIDTH), F32)
    return pl.pallas_call(
        body, grid=(B_HEADS, nblk),
        in_specs=[blk(lambda h, n: (n, gi * B_HEADS + h)), blk(cur), blk(cur), blk(cur), blk(cur), blk(prev),
                  blk(lambda h, n: (n, B_HEADS + h)), blk(lambda h, n: (jnp.maximum(n - 1, 0), B_HEADS + h))],
        out_specs=[blk(cur), whole, whole], out_shape=[sh, sh, sh],
        compiler_params=_cp("parallel", "arbitrary"), name=name)(qr, dmix, lse, dd, kr, kr, kv, kv)


A_MQ_COL = 4 * A_WIDTH // MEM_WIDTH
B_MQ_COL = N_GROUPS * B_WIDTH // MEM_WIDTH


def _row(v):
    return v.reshape(1, -1).astype(F32)


def _local_step(x, mem, tgt, get_w, P, put_g, first_dep=None, forward_point=lambda i, value: value):
    T = x.shape[0]
    cosf, sinsg = _rope_tables(T)
    lb_soft = jax.nn.softmax(P["a_lb_logits"].astype(F32), axis=0)
    lb = lb_soft[0:1]
    qw_heads = jnp.repeat(P["b_qnorm"][0], B_HEADS, axis=0).reshape(1, -1)
    kw_heads = jnp.tile(_row(P["b_knorm"]), (1, B_HEADS))
    mqw = [jnp.tile(_row(P["mem_qnorm"][l]), (1, MEM_HEADS)) for l in range(2)]
    mkw = [jnp.tile(_row(P["mem_knorm"][l]), (1, MEM_HEADS)) for l in range(2)]
    nmix = [_row(P["norm_mix"][l]) for l in range(2)]
    nffn = [_row(P["norm_ffn"][l]) for l in range(2)]
    mnorm = [_row(P["mem_norm"][l]) for l in range(2)]
    kvn = _row(P["kv_norm"])
    onorm = _row(P["a_onorm"])
    W = {}

    def w_of(name, after=None):
        if name not in W:
            W[name] = get_w(name, after)
        return W[name]

    proj_a, xn0 = _rms_matmul(x, nmix[0], w_of("a_w_in"), tt=512, tn=1664, wt=True, name="proj_a", dep=first_dep)
    mkv0, mn0 = _rms_matmul(mem, mnorm[0], w_of("w_mem_kv0"), tt=MEM_TOKENS, tn=2 * MEM_WIDTH, wt=False, name="mem_kv0")
    o_raw, st = _hgrn2_fwd(proj_a, lb, name="hgrn2_fwd")
    o_raw = forward_point(0, o_raw)
    mm0 = _a_post_fwd(o_raw, proj_a, onorm, tt=512, name="a_post_fwd")
    mo0 = _mem_attn_fwd(proj_a, A_MQ_COL, mkv0, mqw[0], mkw[0], tt=512, name="mem_attn_fwd0")
    hm0 = _mm_res(x, mm0, mo0, w_of("w_out0", mo0), tt=512, name="out_proj0")
    hm0 = forward_point(1, hm0)
    gu0, hn0 = _rms_matmul(hm0, nffn[0], w_of("w_gate_up0", hm0), tt=512, tn=1408, wt=True, out_dtype=BF16, name="gate_up0")
    h1 = _swiglu_down(hm0, gu0, w_of("w_down0", gu0), tt=256, name="down0")
    h1 = forward_point(2, h1)
    kv, hkn = _rms_matmul(h1, kvn, w_of("w_kv", h1), tt=512, tn=768, wt=True, name="kv_proj")
    kr = _headnorm_rope_fwd(kv, kw_heads, cosf, sinsg, col0=0, n_heads=B_HEADS, tt=512, name="k_prep")

    proj_b, xn1 = _rms_matmul(h1, nmix[1], w_of("b_w_in", kr), tt=512, tn=1280, wt=True, name="proj_b")
    proj_b = forward_point(3, proj_b)
    mkv1, mn1 = _rms_matmul(mem, mnorm[1], w_of("w_mem_kv1", kr), tt=MEM_TOKENS, tn=2 * MEM_WIDTH, wt=False, name="mem_kv1")
    qr = _headnorm_rope_fwd(proj_b, qw_heads, cosf, sinsg, col0=0, n_heads=N_GROUPS * B_HEADS, tt=512, name="q_prep")
    outs = [(_dil_fwd if d == 1 else _dils_fwd)(qr, kr, kv, gi, d, name=f"dil_fwd{gi}") for gi, d in enumerate(DILATIONS)]
    mm1, lse_tot = _dil_combine_fwd([o for o, _ in outs], [s for _, s in outs], tt=512, name="dil_combine")
    mo1 = _mem_attn_fwd(proj_b, B_MQ_COL, mkv1, mqw[1], mkw[1], tt=512, name="mem_attn_fwd1")
    hm1 = _mm_res(h1, mm1, mo1, w_of("w_out1", mo1), tt=512, name="out_proj1")
    gu1, hn1 = _rms_matmul(hm1, nffn[1], w_of("w_gate_up1", hm1), tt=512, tn=1408, wt=True, out_dtype=BF16, name="gate_up1")
    y = _swiglu_down(hm1, gu1, w_of("w_down1", gu1), tt=256, name="down1")
    dy, sq = _loss_kernel(y, tgt, tt=512, name="loss")

    gP = {}
    zeros_mem = jnp.zeros((MEM_TOKENS, D_MODEL), F32)

    def ffn_bwd(l, dh, hm, gu, hn):
        dgu, g_wd = _swiglu_bwd(dh, gu, w_of(f"w_down{l}"), tt=256, name=f"swiglu_bwd{l}")
        g_wgu = _mm_tn(dgu, hn, tt=512, tka=1408, name=f"g_w_gate_up{l}")
        sent = put_g({f"w_down{l}": g_wd, f"w_gate_up{l}": g_wgu})
        dhm, g_nf = _rms_bwd_dx(hm, nffn[l], w_of(f"w_gate_up{l}"), dgu, dh, tt=256, wt=True, name=f"gate_up_bwd{l}", dep=sent)
        return dhm, g_nf

    def mix_bwd(l, dhm, mix_main, mix_mem, proj, qcol, mkv, mn):
        dmix, g_wout = _out_proj_bwd(dhm, mix_main, mix_mem, w_of(f"w_out{l}"), tt=512, name=f"out_proj_bwd{l}")
        dmq, dmkv, dqw, dkw = _mem_attn_bwd(proj, qcol, mkv, mqw[l], mkw[l], dmix, tt=512, name=f"mem_attn_bwd{l}")
        g_wmkv = _mm_tn(mn, dmkv, tt=MEM_TOKENS, tka=512, name=f"g_w_mem_kv{l}")
        sent = put_g({f"w_out{l}": g_wout, f"w_mem_kv{l}": g_wmkv})
        _, g_mn = _rms_bwd_dx(mem, mnorm[l], w_of(f"w_mem_kv{l}"), dmkv, zeros_mem, tt=MEM_TOKENS, wt=False, name=f"mem_kv_bwd{l}")
        fold = lambda v: v.reshape(MEM_HEADS, MEM_HEAD_DIM).sum(axis=0)
        return dmix, dmq, g_mn, fold(dqw), fold(dkw), sent

    dhm1, g_nf1 = ffn_bwd(1, dy, hm1, gu1, hn1)
    dmix1, dmq1, g_mn1, g_mq1, g_mk1, sent = mix_bwd(1, dhm1, mm1, mo1, proj_b, B_MQ_COL, mkv1, mn1)
    dd = _dil_bwd_prep(dmix1, mm1, tt=512, name="dil_bwd_prep", dep=sent)
    dqs, dks, dvs = [], [], []
    for gi, d in enumerate(DILATIONS):
        dq_g, dk_g, dv_g = _dil_bwd(qr, kr, kv, dmix1, lse_tot, dd, gi, d, name=f"dil_bwd{gi}")
        dqs.append(dq_g)
        dks.append(dk_g)
        dvs.append(dv_g)
    dq_raw, dqw = _q_prep_bwd(proj_b, qw_heads, cosf, sinsg, dqs, tt=512, name="q_prep_bwd")
    dkv, dkw = _kv_prep_bwd(kv, kw_heads, cosf, sinsg, dks, dvs, tt=512, name="kv_prep_bwd")
    dproj_b = [dq_raw, dmq1]
    g_wb = _mm_tn_pieces(dproj_b, xn1, tt=512, name="g_b_w_in")
    g_wkv = _mm_tn(dkv, hkn, tt=512, tka=768, name="g_w_kv")
    sent = put_g({"b_w_in": g_wb, "w_kv": g_wkv})
    dh1, g_nm1 = _rms_bwd_dx(h1, nmix[1], w_of("b_w_in"), dproj_b, dhm1, tt=256, wt=True, name="proj_b_bwd", dep=sent)
    dh1, g_kvn = _rms_bwd_dx(h1, kvn, w_of("w_kv"), dkv, dh1, tt=256, wt=True, name="kv_proj_bwd")

    dhm0, g_nf0 = ffn_bwd(0, dh1, hm0, gu0, hn0)
    dmix0, dmq0, g_mn0, g_mq0, g_mk0, sent = mix_bwd(0, dhm0, mm0, mo0, proj_a, A_MQ_COL, mkv0, mn0)
    do_raw, dg, g_onorm = _a_post_bwd(o_raw, proj_a, onorm, dmix0, tt=512, name="a_post_bwd", dep=sent)
    dq, dz, dv, dlb = _hgrn2_bwd(proj_a, lb, st, do_raw, name="hgrn2_bwd")
    dproj_a = [dq, dz, dv, dg, dmq0]
    sent = put_g({"a_w_in": _mm_tn_pieces(dproj_a, xn0, tt=512, name="g_a_w_in")})
    gx, g_nm0 = _rms_bwd_dx(x, nmix[0], w_of("a_w_in"), dproj_a, dhm0, tt=256, wt=True, name="proj_a_bwd", dep=sent)

    dl0 = lb_soft[0:1] * lb_soft[1:2] * dlb
    gP["a_lb_logits"] = jnp.concatenate([dl0, -dl0], axis=0)
    gP["a_onorm"] = g_onorm
    gP["norm_mix"] = jnp.concatenate([g_nm0, g_nm1], axis=0)
    gP["norm_ffn"] = jnp.concatenate([g_nf0, g_nf1], axis=0)
    gP["b_qnorm"] = dqw.reshape(N_GROUPS, B_HEADS, HEAD_DIM).sum(axis=1)[None]
    gP["kv_norm"] = g_kvn.reshape(-1)
    gP["b_knorm"] = dkw.reshape(B_HEADS, HEAD_DIM).sum(axis=0)
    gP["mem_norm"] = jnp.concatenate([g_mn0, g_mn1], axis=0)
    gP["mem_qnorm"] = jnp.stack([g_mq0, g_mq1])
    gP["mem_knorm"] = jnp.stack([g_mk0, g_mk1])
    return sq, gx, gP


MESH_ID = pl.DeviceIdType.MESH
HBM_SPEC = pl.BlockSpec(memory_space=pltpu.HBM)


def _position():
    return lax.axis_index("x"), lax.axis_index("y"), lax.axis_index("c")


def _all_gather(blocks, *, name):
    n = len(blocks)

    def body(*refs):
        x_refs, out_refs = refs[:n], refs[n:2 * n]
        send_sems, recv_sems, local_sems = refs[2 * n:]
        x, y, c = _position()
        me, sibling = (x, y, c), (x, y, 1 - c)
        chips = [(1 - x, y), (x, 1 - y), (1 - x, 1 - y)]

        def slot(a, px, py, pc):
            return out_refs[a].at[4 * px + 2 * py + pc]

        def copy(a, k, blk, to, src=None):
            return pltpu.make_async_remote_copy(
                src_ref=slot(a, *blk) if src is None else src, dst_ref=slot(a, *blk),
                send_sem=send_sems.at[7 * a + k], recv_sem=recv_sems.at[7 * a + k], device_id=to, device_id_type=MESH_ID)

        mine = [pltpu.make_async_copy(x_refs[a], slot(a, *me), local_sems.at[a]) for a in range(n)]
        for cp in mine:
            cp.start()
        first = []
        for a in range(n):
            first.append(copy(a, 0, me, sibling, src=x_refs[a]))
            first += [copy(a, 1 + j, me, (*chip, c), src=x_refs[a]) for j, chip in enumerate(chips)]
        for cp in first:
            cp.start()
        passed = []
        for j, chip in enumerate(chips):
            for a in range(n):
                copy(a, 1 + j, (*chip, c), me).wait_recv()
                cp = copy(a, 4 + j, (*chip, c), sibling)
                cp.start()
                passed.append(cp)
        for a in range(n):
            copy(a, 0, sibling, me).wait_recv()
            for j, chip in enumerate(chips):
                copy(a, 4 + j, (*chip, 1 - c), me).wait_recv()
        for cp in first + passed:
            cp.wait_send()
        for cp in mine:
            cp.wait()

    return pl.pallas_call(
        body, out_shape=[jax.ShapeDtypeStruct((N_DEV,) + b.shape, b.dtype) for b in blocks],
        in_specs=[HBM_SPEC] * n, out_specs=[HBM_SPEC] * n,
        scratch_shapes=[pltpu.SemaphoreType.DMA((7 * n,)), pltpu.SemaphoreType.DMA((7 * n,)), pltpu.SemaphoreType.DMA((n,))],
        name=name)(*blocks)


def _all_gather_direct(block, *, name):
    def body(x_ref, out_ref, send_sems, recv_sems, local_sem):
        x, y, c = _position()
        me = 4 * x + 2 * y + c
        mine = pltpu.make_async_copy(x_ref, out_ref.at[me], local_sem)
        mine.start()
        copies = []
        for k in ALL_PEERS:
            cp = pltpu.make_async_remote_copy(
                src_ref=x_ref, dst_ref=out_ref.at[me], send_sem=send_sems.at[k - 1], recv_sem=recv_sems.at[k - 1],
                device_id=_peer(k, x, y, c), device_id_type=MESH_ID)
            cp.start()
            copies.append(cp)
        for cp in copies:
            cp.wait()
        mine.wait()

    return pl.pallas_call(
        body, out_shape=jax.ShapeDtypeStruct((N_DEV,) + block.shape, block.dtype),
        in_specs=[HBM_SPEC], out_specs=HBM_SPEC,
        scratch_shapes=[pltpu.SemaphoreType.DMA((7,)), pltpu.SemaphoreType.DMA((7,)), pltpu.SemaphoreType.DMA],
        name=name)(block)


SEM_SPEC = pl.BlockSpec(memory_space=pltpu.SEMAPHORE)
ANY_SPEC = pl.BlockSpec(memory_space=pl.ANY)
DATAFLOW = pltpu.SideEffectType.DATAFLOW_SIDE_EFFECTING


def _peer(k, x, y, c):
    return (1 - x if (k >> 2) & 1 else x, 1 - y if (k >> 1) & 1 else y, 1 - c if k & 1 else c)


def _own_slot_filled(own_block):
    x, y, c = _position()
    zone = lax.empty((N_DEV,) + own_block.shape, own_block.dtype)
    return lax.dynamic_update_slice_in_dim(zone, own_block[None], 4 * x + 2 * y + c, axis=0)


ALL_PEERS = tuple(range(1, N_DEV))
SIBLING_AND_SAME_CORE = (1, 2, 4, 6)
SAME_CORE = (2, 4, 6)


def _split_start(srcs, scatter, after, *, name, relations=ALL_PEERS, carried=None):
    n = len(srcs)
    extra = ([] if after is None else [after]) + ([] if carried is None else [carried])
    n_carried = 0 if carried is None else 1
    x, y, c = _position()
    me = 4 * x + 2 * y + c
    lands = [_own_slot_filled(lax.dynamic_index_in_dim(s, me, 0, keepdims=False) if scatter else s) for s in srcs]

    def body(*refs):
        src_refs, land_refs = refs[:n], refs[n:2 * n]
        send_sems, recv_sems = refs[2 * n + len(extra)], refs[2 * n + len(extra) + 1]
        token = refs[2 * n + len(extra) + 2 + 2 * n]
        bx, by, bc = _position()
        bme = 4 * bx + 2 * by + bc
        for a in range(n):
            for k in relations:
                tx, ty, tc = _peer(k, bx, by, bc)
                src = src_refs[a].at[4 * tx + 2 * ty + tc] if scatter else src_refs[a]
                pltpu.make_async_remote_copy(
                    src_ref=src, dst_ref=land_refs[a].at[bme],
                    send_sem=send_sems.at[7 * a + k - 1], recv_sem=recv_sems.at[7 * a + k - 1],
                    device_id=(tx, ty, tc), device_id_type=MESH_ID).start()
        token[...] = jnp.zeros_like(token)

    hbm = lambda a: pltpu.HBM(a.shape, a.dtype)
    outs = pl.pallas_call(
        body, name=name,
        out_shape=(pltpu.SemaphoreType.DMA((7 * n,)), pltpu.SemaphoreType.DMA((7 * n,)),
                   *[hbm(s) for s in srcs], *[hbm(l) for l in lands], jax.ShapeDtypeStruct((8, 128), F32),
                   *([hbm(carried)] if n_carried else [])),
        in_specs=[HBM_SPEC] * (2 * n) + [ANY_SPEC] * len(extra),
        out_specs=(SEM_SPEC, SEM_SPEC, *[HBM_SPEC] * (2 * n), pl.BlockSpec(memory_space=pltpu.VMEM), *([ANY_SPEC] * n_carried)),
        input_output_aliases={**{i: 2 + i for i in range(2 * n)},
                              **({2 * n + len(extra) - 1: 2 * n + 3} if n_carried else {})},
        compiler_params=pltpu.CompilerParams(has_side_effects=DATAFLOW),
    )(*[pltpu.with_memory_space_constraint(s, pltpu.HBM) for s in srcs],
      *[pltpu.with_memory_space_constraint(l, pltpu.HBM) for l in lands], *extra)
    return {"n": n, "relations": relations, "send": outs[0], "recv": outs[1], "srcs": list(outs[2:2 + n]),
            "lands": list(outs[2 + n:2 + 2 * n]), "token": outs[2 * n + 2], "carried": outs[-1] if n_carried else None}


def _forward_start(lands, carried, *, name):
    n = len(lands)

    def body(*refs):
        land_refs = refs[:n]
        send_sems, recv_sems = refs[n + 1], refs[n + 2]
        bx, by, bc = _position()
        for a in range(n):
            for k in SAME_CORE:
                tx, ty, tc = _peer(k, bx, by, bc)
                block = land_refs[a].at[4 * tx + 2 * ty + tc]
                pltpu.make_async_remote_copy(
                    src_ref=block, dst_ref=block,
                    send_sem=send_sems.at[7 * a + k - 1], recv_sem=recv_sems.at[7 * a + k - 1],
                    device_id=(bx, by, 1 - bc), device_id_type=MESH_ID).start()

    hbm = lambda a: pltpu.HBM(a.shape, a.dtype)
    outs = pl.pallas_call(
        body, name=name,
        out_shape=(pltpu.SemaphoreType.DMA((7 * n,)), pltpu.SemaphoreType.DMA((7 * n,)),
                   *[hbm(l) for l in lands], hbm(carried)),
        in_specs=[HBM_SPEC] * n + [ANY_SPEC],
        out_specs=(SEM_SPEC, SEM_SPEC, *[HBM_SPEC] * n, ANY_SPEC),
        input_output_aliases={i: 2 + i for i in range(n + 1)},
        compiler_params=pltpu.CompilerParams(has_side_effects=DATAFLOW),
    )(*lands, carried)
    handle = {"n": n, "relations": SAME_CORE, "send": outs[0], "recv": outs[1], "srcs": [], "lands": list(outs[2:2 + n])}
    return handle, outs[-1]


def _split_wait(handle, after, *, name):
    n, ns = handle["n"], len(handle["srcs"])

    def body(*refs):
        land_refs = refs[ns:ns + n]
        send_sems, recv_sems = refs[ns + n], refs[ns + n + 1]
        bx, by, bc = _position()
        for a in range(n):
            for k in handle["relations"]:
                block = land_refs[a].at[0]
                cp = pltpu.make_async_remote_copy(
                    src_ref=block, dst_ref=block,
                    send_sem=send_sems.at[7 * a + k - 1], recv_sem=recv_sems.at[7 * a + k - 1],
                    device_id=_peer(k, bx, by, bc), device_id_type=MESH_ID)
                cp.wait_send()
                cp.wait_recv()

    hbm = lambda a: pltpu.HBM(a.shape, a.dtype)
    outs = pl.pallas_call(
        body, name=name,
        out_shape=(*[hbm(s) for s in handle["srcs"]], *[hbm(l) for l in handle["lands"]]),
        in_specs=[HBM_SPEC] * (ns + n) + [SEM_SPEC, SEM_SPEC, ANY_SPEC],
        out_specs=tuple([HBM_SPEC] * (ns + n)),
        input_output_aliases={i: i for i in range(ns + n)},
        compiler_params=pltpu.CompilerParams(has_side_effects=DATAFLOW),
    )(*handle["srcs"], *handle["lands"], handle["send"], handle["recv"], after)
    return list(outs[ns:])


def _sum_sources(parts, *, tr, name):
    n, R, C = parts.shape

    def body(p_ref, o_ref):
        acc = p_ref[0].astype(F32)
        for s in range(1, n):
            acc = acc + p_ref[s].astype(F32)
        o_ref[...] = acc

    return pl.pallas_call(
        body, grid=(R // tr,), in_specs=[pl.BlockSpec((n, tr, C), lambda i: (0, i, 0))],
        out_specs=pl.BlockSpec((tr, C), lambda i: (i, 0)),
        out_shape=jax.ShapeDtypeStruct((R, C), F32), compiler_params=_cp("parallel"), name=name)(parts)


def _adamw_math(g, w, m, v):
    c1 = 1.0 - ADAM_B1 ** ADAM_STEP
    c2 = 1.0 - ADAM_B2 ** ADAM_STEP
    nm = ADAM_B1 * m + (1.0 - ADAM_B1) * g
    nv = ADAM_B2 * v + (1.0 - ADAM_B2) * (g * g)
    return -ADAM_LR * ((nm / c1) / (jnp.sqrt(nv / c2) + ADAM_EPS) + ADAM_WD * w), nm, nv


def _reduce_adamw(received, w, m, v, *, tr, name):
    L, R, C = w.shape

    def body(*refs):
        p_refs = refs[:L]
        w_ref, m_ref, v_ref, g_ref, d_ref, nm_ref, nv_ref = refs[L:]
        for l in range(L):
            @pl.when(pl.program_id(0) == l)
            def _(p_ref=p_refs[l]):
                acc = p_ref[0].astype(F32)
                for s in range(1, N_DEV):
                    acc = acc + p_ref[s].astype(F32)
                g_ref[...] = acc
                d_ref[...], nm_ref[...], nv_ref[...] = _adamw_math(acc, w_ref[...], m_ref[...], v_ref[...])

    p_spec = pl.BlockSpec((N_DEV, tr, C), lambda l, i: (0, i, 0))
    blk = pl.BlockSpec((None, tr, C), lambda l, i: (l, i, 0))
    sh = jax.ShapeDtypeStruct((L, R, C), F32)
    return pl.pallas_call(
        body, grid=(L, R // tr), in_specs=[p_spec] * L + [blk] * 3, out_specs=[blk] * 4, out_shape=[sh] * 4,
        compiler_params=_cp("parallel", "parallel"), name=name)(*received, w, m, v)


def _adamw(g, w, m, v, *, tr, name):
    L, R, C = w.shape

    def body(g_ref, w_ref, m_ref, v_ref, d_ref, nm_ref, nv_ref):
        d_ref[...], nm_ref[...], nv_ref[...] = _adamw_math(g_ref[...], w_ref[...], m_ref[...], v_ref[...])

    blk = pl.BlockSpec((None, tr, C), lambda l, i: (l, i, 0))
    sh = jax.ShapeDtypeStruct((L, R, C), F32)
    return pl.pallas_call(
        body, grid=(L, R // tr), in_specs=[blk] * 4, out_specs=[blk] * 3, out_shape=[sh] * 3,
        compiler_params=_cp("parallel", "parallel"), name=name)(g, w, m, v)


UNITS = {
    "a_w_in": ("a_w_in", 0, True), "w_mem_kv0": ("w_mem_kv", 0, False), "w_out0": ("w_out", 0, False),
    "w_gate_up0": ("w_gate_up", 0, True), "w_down0": ("w_down", 0, False), "w_kv": ("w_kv", None, True),
    "b_w_in": ("b_w_in", 0, True), "w_mem_kv1": ("w_mem_kv", 1, False), "w_out1": ("w_out", 1, False),
    "w_gate_up1": ("w_gate_up", 1, True), "w_down1": ("w_down", 1, False),
}
BIG = ("a_w_in", "b_w_in", "w_kv", "w_mem_kv", "w_out", "w_gate_up", "w_down")
ADAMW_ROW_TILE = {"a_w_in": 208, "b_w_in": 160, "w_kv": 192, "w_mem_kv": 128, "w_out": 128, "w_gate_up": 176, "w_down": 176}


def _wire_block(weights, unit):
    name, layer, col = UNITS[unit]
    a = weights[name] if layer is None else weights[name][layer]
    return (a.T if col else a).astype(BF16)


SMALL_REPLICATED = ("norm_mix", "norm_ffn", "b_qnorm", "kv_norm", "b_knorm", "mem_norm", "mem_qnorm", "mem_knorm")
SMALL_SHARDED = ("a_lb_logits", "a_onorm")
SMALL_ORDER = SMALL_REPLICATED + SMALL_SHARDED
LANES = 128


def _prod(shape):
    n = 1
    for s in shape:
        n *= s
    return n


def _pack_flat(arrays, rows, cols, dtype):
    flat = jnp.concatenate([a.reshape(-1).astype(dtype) for a in arrays])
    return jnp.pad(flat, (0, rows * cols - flat.shape[0])).reshape(rows, cols)


def _unpack_flat(packed, shapes):
    flat = packed.reshape(-1)
    out, off = [], 0
    for s in shapes:
        out.append(flat[off:off + _prod(s)].reshape(s))
        off += _prod(s)
    return out


def kernel(x, mem, norm_mix, norm_ffn, a_w_in, a_lb_logits, a_onorm, b_w_in, b_qnorm, kv_norm, w_kv, b_knorm, mem_norm, w_mem_kv, mem_qnorm, mem_knorm, w_out, w_gate_up, w_down, loss_target, m_norm_mix, m_norm_ffn, m_a_w_in, m_a_lb_logits, m_a_onorm, m_b_w_in, m_b_qnorm, m_kv_norm, m_w_kv, m_b_knorm, m_mem_norm, m_w_mem_kv, m_mem_qnorm, m_mem_knorm, m_w_out, m_w_gate_up, m_w_down, v_norm_mix, v_norm_ffn, v_a_w_in, v_a_lb_logits, v_a_onorm, v_b_w_in, v_b_qnorm, v_kv_norm, v_w_kv, v_b_knorm, v_mem_norm, v_w_mem_kv, v_mem_qnorm, v_mem_knorm, v_w_out, v_w_gate_up, v_w_down):
    names = ("norm_mix", "norm_ffn", "a_w_in", "a_lb_logits", "a_onorm", "b_w_in", "b_qnorm", "kv_norm", "w_kv", "b_knorm",
             "mem_norm", "w_mem_kv", "mem_qnorm", "mem_knorm", "w_out", "w_gate_up", "w_down")
    w = dict(zip(names, (norm_mix, norm_ffn, a_w_in, a_lb_logits, a_onorm, b_w_in, b_qnorm, kv_norm, w_kv, b_knorm,
                         mem_norm, w_mem_kv, mem_qnorm, mem_knorm, w_out, w_gate_up, w_down)))
    m = dict(zip(names, (m_norm_mix, m_norm_ffn, m_a_w_in, m_a_lb_logits, m_a_onorm, m_b_w_in, m_b_qnorm, m_kv_norm, m_w_kv,
                         m_b_knorm, m_mem_norm, m_w_mem_kv, m_mem_qnorm, m_mem_knorm, m_w_out, m_w_gate_up, m_w_down)))
    v = dict(zip(names, (v_norm_mix, v_norm_ffn, v_a_w_in, v_a_lb_logits, v_a_onorm, v_b_w_in, v_b_qnorm, v_kv_norm, v_w_kv,
                         v_b_knorm, v_mem_norm, v_w_mem_kv, v_mem_qnorm, v_mem_knorm, v_w_out, v_w_gate_up, v_w_down)))

    first = ["a_w_in", "w_mem_kv0"]
    gathered = _all_gather([_wire_block(w, u) for u in first] + [_pack_flat([a_lb_logits, a_onorm], 8, LANES, F32)],
                           name="gather_first")
    full = {u: g.reshape(-1, g.shape[-1]) for u, g in zip(first, gathered)}
    small_in = gathered[-1].reshape(N_DEV, -1)
    P = {n: w[n] for n in SMALL_REPLICATED}
    P["a_lb_logits"] = small_in[:, :192].reshape(N_DEV, 2, 96).transpose(1, 0, 2).reshape(2, A_WIDTH)
    P["a_onorm"] = small_in[:, 192:288].reshape(1, A_WIDTH)
    later = [["w_out0", "w_gate_up0"], ["w_down0", "w_kv"], ["b_w_in", "w_mem_kv1"], ["w_out1", "w_gate_up1", "w_down1"]]
    first_half, second_half = {}, {}

    def start_first_half(i, after, carried=None):
        first_half[i] = _split_start([_wire_block(w, u) for u in later[i]], False, after, name=f"gather{i}_start",
                                     relations=SIBLING_AND_SAME_CORE, carried=carried)
        return first_half[i]

    token = start_first_half(0, gathered[-1])["token"]
    token = start_first_half(1, token)["token"]

    def forward_point(i, value):
        landed = _split_wait(first_half[i], value, name=f"gather{i}_landed")
        second_half[i], value = _forward_start(landed, value, name=f"gather{i}_forward")
        if i + 2 < len(later):
            value = start_first_half(i + 2, None, carried=value)["carried"]
        return value

    def get_w(unit, after):
        if unit not in full:
            i = [unit in group for group in later].index(True)
            for u, land in zip(later[i], _split_wait(second_half[i], after, name=f"gather{i}_wait")):
                full[u] = land.reshape(-1, land.shape[-1])
        return full[unit]

    sent = []

    def put_g(group):
        units = list(group)
        handle = _split_start([group[u].reshape(N_DEV, -1, group[u].shape[-1]) for u in units], True, None,
                              name=f"scatter{len(sent)}_start")
        sent.append((units, handle))
        return handle["token"]

    sq, gx, gP = _local_step(x[0], mem[0], loss_target[0], get_w, P, put_g, first_dep=token, forward_point=forward_point)
    loss_here = (0.5 * jnp.sum(sq) / D_MODEL).reshape(1)

    received = {}
    for i, (units, handle) in enumerate(sent):
        received.update(zip(units, _split_wait(handle, gx, name=f"scatter{i}_wait")))
    out = {"grad": {}, "delta": {}, "new_m": {}, "new_v": {}}
    for n in BIG:
        shape = w[n].shape
        as3 = lambda a: a.reshape((-1,) + shape[-2:])
        mine = [u for u, (wn, _, _) in UNITS.items() if wn == n]
        flip = (lambda a: jnp.swapaxes(a, 1, 2)) if UNITS[mine[0]][2] else (lambda a: a)
        res = _reduce_adamw([received[u] for u in mine], flip(as3(w[n])), flip(as3(m[n])), flip(as3(v[n])),
                            tr=ADAMW_ROW_TILE[n], name=f"adamw_{n}")
        for kind, r in zip(("grad", "delta", "new_m", "new_v"), res):
            out[kind][n] = flip(r).reshape(shape)

    full_shapes = [(2, A_WIDTH) if n == "a_lb_logits" else (1, A_WIDTH) if n == "a_onorm" else w[n].shape for n in SMALL_ORDER]
    n_small = sum(_prod(s) for s in full_shapes) + 1
    rows_small = -(-n_small // (8 * LANES)) * 8
    g_all = _all_gather_direct(_pack_flat([gP[n] for n in SMALL_ORDER] + [loss_here], rows_small, LANES, F32),
                               name="gather_small_grads")
    summed = _unpack_flat(_sum_sources(g_all, tr=rows_small, name="sum_small_grads"), full_shapes + [(1,)])
    g_small = dict(zip(SMALL_ORDER, summed))
    loss = summed[-1].reshape(())
    me = 4 * lax.axis_index("x") + 2 * lax.axis_index("y") + lax.axis_index("c")
    for n in SMALL_SHARDED:
        g_small[n] = lax.dynamic_slice_in_dim(g_small[n], me * 96, 96, axis=1)
    shapes = [w[n].shape for n in SMALL_ORDER]
    rows_upd = -(-sum(_prod(s) for s in shapes) // (8 * LANES)) * 8
    pk = lambda d: _pack_flat([d[n] for n in SMALL_ORDER], rows_upd, LANES, F32)
    res = _adamw(pk(g_small)[None], pk(w)[None], pk(m)[None], pk(v)[None], tr=rows_upd, name="adamw_small")
    out["grad"].update(g_small)
    for kind, packed in zip(("delta", "new_m", "new_v"), res):
        out[kind].update(zip(SMALL_ORDER, _unpack_flat(packed[0], shapes)))

    return (loss, gx[None], *[out["grad"][n] for n in names], *[out["delta"][n] for n in names],
            *[out["new_m"][n] for n in names], *[out["new_v"][n] for n in names])
```

```python
import functools

import jax
import jax.numpy as jnp
from jax import lax
from jax.experimental import pallas as pl
from jax.experimental.pallas import tpu as pltpu

F32 = jnp.float32
BF16 = jnp.bfloat16

N_DEV = 8
D_MODEL = 1024
HEAD_DIM = 128
A_HEADS = 6
A_WIDTH = A_HEADS * HEAD_DIM
CHUNK = 64
B_HEADS = 6
B_WIDTH = B_HEADS * HEAD_DIM
DILATIONS = (1, 4, 16)
SPAN = 128
N_GROUPS = 3
ROPE_THETA = 10000.0
MEM_TOKENS = 256
MEM_HEADS = 4
MEM_HEAD_DIM = 64
MEM_WIDTH = MEM_HEADS * MEM_HEAD_DIM
FFN_HIDDEN = 2816
EPS = 1e-6

ADAM_LR = 0.001
ADAM_B1 = 0.9
ADAM_B2 = 0.999
ADAM_EPS = 1e-08
ADAM_WD = 0.01
ADAM_STEP = 10

V7X_VMEM_LIMIT_BYTES = 56 * 1024 * 1024

NT_DIMS = (((1,), (1,)), ((), ()))
TN_DIMS = (((0,), (0,)), ((), ()))


def _cp(*sem):
    return pltpu.CompilerParams(dimension_semantics=sem, vmem_limit_bytes=V7X_VMEM_LIMIT_BYTES)


def _dot(a, b):
    return jnp.dot(a.astype(BF16), b.astype(BF16), preferred_element_type=F32)


def _dot_nt(a, b):
    return lax.dot_general(a.astype(BF16), b.astype(BF16), NT_DIMS, preferred_element_type=F32)


def _dot_tn(a, b):
    return lax.dot_general(a.astype(BF16), b.astype(BF16), TN_DIMS, preferred_element_type=F32)


def _dot3(m01, x):
    hi = x.astype(BF16)
    r1 = x - hi.astype(F32)
    mid = r1.astype(BF16)
    lo = (r1 - mid.astype(F32)).astype(BF16)
    d = functools.partial(jnp.dot, preferred_element_type=F32)
    return d(m01, hi) + d(m01, mid) + d(m01, lo)


def _sigmoid(x):
    return 1.0 / (1.0 + jnp.exp(-x))


def _full(shape):
    return pl.BlockSpec(shape, lambda *_: (0,) * len(shape))


def _dep(body, n_in, dep):
    if dep is None:
        return body, [], []

    def with_dep(*refs):
        return body(*refs[:n_in], *refs[n_in + 1:])

    return with_dep, [pl.BlockSpec(memory_space=pl.ANY)], [dep]


def _rms_matmul(x, g, w, *, tt, tn, wt, name, out_dtype=F32, dep=None):
    T, K = x.shape
    N = w.shape[0] if wt else w.shape[1]

    def kernel_body(x_ref, g_ref, w_ref, y_ref, xn_ref):
        xf = x_ref[...]
        r = lax.rsqrt(jnp.mean(xf * xf, axis=-1, keepdims=True) + EPS)
        xn = (xf * r * g_ref[...]).astype(BF16)
        xn_ref[...] = xn
        for j in range(N // tn):
            cols = slice(j * tn, (j + 1) * tn)
            y = _dot_nt(xn, w_ref[cols, :]) if wt else _dot(xn, w_ref[:, cols])
            y_ref[:, cols] = y.astype(out_dtype)

    body, dep_specs, dep_args = _dep(kernel_body, 3, dep)
    return pl.pallas_call(
        body, grid=(T // tt,),
        in_specs=[pl.BlockSpec((tt, K), lambda i: (i, 0)), _full((1, K)), _full(w.shape)] + dep_specs,
        out_specs=[pl.BlockSpec((tt, N), lambda i: (i, 0)), pl.BlockSpec((tt, K), lambda i: (i, 0))],
        out_shape=[jax.ShapeDtypeStruct((T, N), out_dtype), jax.ShapeDtypeStruct((T, K), BF16)],
        compiler_params=_cp("parallel"), name=name)(x, g, w, *dep_args)


def _mm_res(res, a1, a2, w, *, tt, name):
    T, K1 = a1.shape
    K2 = a2.shape[1]
    N = w.shape[1]

    def body(r_ref, a1_ref, a2_ref, w_ref, o_ref):
        o_ref[...] = r_ref[...] + _dot(a1_ref[...], w_ref[:K1, :]) + _dot(a2_ref[...], w_ref[K1:, :])

    return pl.pallas_call(
        body, grid=(T // tt,),
        in_specs=[pl.BlockSpec((tt, N), lambda i: (i, 0)), pl.BlockSpec((tt, K1), lambda i: (i, 0)),
                  pl.BlockSpec((tt, K2), lambda i: (i, 0)), _full((K1 + K2, N))],
        out_specs=pl.BlockSpec((tt, N), lambda i: (i, 0)),
        out_shape=jax.ShapeDtypeStruct((T, N), F32),
        compiler_params=_cp("parallel"), name=name)(res, a1, a2, w)


def _swiglu_down(h, gu, wd, *, tt, name):
    T, D = h.shape
    Fh = wd.shape[0]

    def body(h_ref, gt_ref, up_ref, w_ref, o_ref):
        gt = gt_ref[...].astype(F32)
        act = gt * _sigmoid(gt) * up_ref[...].astype(F32)
        o_ref[...] = h_ref[...] + _dot(act, w_ref[...])

    return pl.pallas_call(
        body, grid=(T // tt,),
        in_specs=[pl.BlockSpec((tt, D), lambda i: (i, 0)), pl.BlockSpec((tt, Fh), lambda i: (i, 0)),
                  pl.BlockSpec((tt, Fh), lambda i: (i, 1)), _full((Fh, D))],
        out_specs=pl.BlockSpec((tt, D), lambda i: (i, 0)),
        out_shape=jax.ShapeDtypeStruct((T, D), F32),
        compiler_params=_cp("parallel"), name=name)(h, gu, gu, wd)


def _swiglu_bwd(dh, gu, wd, *, tt, name):
    T, D = dh.shape
    Fh = wd.shape[0]
    last = T // tt - 1

    def body(dh_ref, gt_ref, up_ref, w_ref, dgu_ref, gw_ref, acc):
        @pl.when(pl.program_id(0) == 0)
        def _():
            acc[...] = jnp.zeros_like(acc)

        gt = gt_ref[...].astype(F32)
        up = up_ref[...].astype(F32)
        s = _sigmoid(gt)
        silu = gt * s
        dh16 = dh_ref[...].astype(BF16)
        dact = _dot_nt(dh16, w_ref[...])
        acc[...] += _dot_tn((silu * up).astype(BF16), dh16)
        dgu_ref[:, :Fh] = (dact * up * (s * (1.0 + gt * (1.0 - s)))).astype(BF16)
        dgu_ref[:, Fh:] = (dact * silu).astype(BF16)

        @pl.when(pl.program_id(0) == last)
        def _():
            gw_ref[...] = acc[...].astype(BF16)

    return pl.pallas_call(
        body, grid=(T // tt,),
        in_specs=[pl.BlockSpec((tt, D), lambda i: (i, 0)), pl.BlockSpec((tt, Fh), lambda i: (i, 0)),
                  pl.BlockSpec((tt, Fh), lambda i: (i, 1)), _full((Fh, D))],
        out_specs=[pl.BlockSpec((tt, 2 * Fh), lambda i: (i, 0)), _full((Fh, D))],
        out_shape=[jax.ShapeDtypeStruct((T, 2 * Fh), BF16), jax.ShapeDtypeStruct((Fh, D), BF16)],
        scratch_shapes=[pltpu.VMEM((Fh, D), F32)],
        compiler_params=_cp("arbitrary"), name=name)(dh, gu, gu, wd)


def _out_proj_bwd(dy, a1, a2, w, *, tt, name):
    T, N = dy.shape
    K1, K2 = a1.shape[1], a2.shape[1]
    K = K1 + K2
    last = T // tt - 1

    def body(dy_ref, a1_ref, a2_ref, w_ref, da_ref, gw_ref, acc):
        @pl.when(pl.program_id(0) == 0)
        def _():
            acc[...] = jnp.zeros_like(acc)

        dy16 = dy_ref[...].astype(BF16)
        da_ref[...] = _dot_nt(dy16, w_ref[...])
        acc[:K1, :] += _dot_tn(a1_ref[...], dy16)
        acc[K1:, :] += _dot_tn(a2_ref[...], dy16)

        @pl.when(pl.program_id(0) == last)
        def _():
            gw_ref[...] = acc[...].astype(BF16)

    return pl.pallas_call(
        body, grid=(T // tt,),
        in_specs=[pl.BlockSpec((tt, N), lambda i: (i, 0)), pl.BlockSpec((tt, K1), lambda i: (i, 0)),
                  pl.BlockSpec((tt, K2), lambda i: (i, 0)), _full((K, N))],
        out_specs=[pl.BlockSpec((tt, K), lambda i: (i, 0)), _full((K, N))],
        out_shape=[jax.ShapeDtypeStruct((T, K), F32), jax.ShapeDtypeStruct((K, N), BF16)],
        scratch_shapes=[pltpu.VMEM((K, N), F32)],
        compiler_params=_cp("arbitrary"), name=name)(dy, a1, a2, w)


def _mm_tn(a, b, *, tt, tka, name):
    T, Ka = a.shape
    N = b.shape[1]
    last = T // tt - 1

    def body(a_ref, b_ref, o_ref, acc):
        @pl.when(pl.program_id(1) == 0)
        def _():
            acc[...] = jnp.zeros_like(acc)

        acc[...] += _dot_tn(a_ref[...], b_ref[...])

        @pl.when(pl.program_id(1) == last)
        def _():
            o_ref[...] = acc[...].astype(BF16)

    return pl.pallas_call(
        body, grid=(Ka // tka, T // tt),
        in_specs=[pl.BlockSpec((tt, tka), lambda j, t: (t, j)), pl.BlockSpec((tt, N), lambda j, t: (t, 0))],
        out_specs=pl.BlockSpec((tka, N), lambda j, t: (j, 0)),
        out_shape=jax.ShapeDtypeStruct((Ka, N), BF16),
        scratch_shapes=[pltpu.VMEM((tka, N), F32)],
        compiler_params=_cp("parallel", "arbitrary"), name=name)(a, b)


def _mm_tn_pieces(pieces, b, *, tt, name):
    n = len(pieces)
    T = b.shape[0]
    N = b.shape[1]
    widths = [p.shape[1] for p in pieces]
    Ka = sum(widths)
    last = T // tt - 1

    def body(*refs):
        p_refs = refs[:n]
        b_ref, o_ref, acc = refs[n:]

        @pl.when(pl.program_id(0) == 0)
        def _():
            acc[...] = jnp.zeros_like(acc)

        bv = b_ref[...].astype(BF16)
        off = 0
        for p_ref, wd in zip(p_refs, widths):
            acc[off:off + wd, :] += _dot_tn(p_ref[...], bv)
            off += wd

        @pl.when(pl.program_id(0) == last)
        def _():
            o_ref[...] = acc[...].astype(BF16)

    return pl.pallas_call(
        body, grid=(T // tt,),
        in_specs=[pl.BlockSpec((tt, wd), lambda t: (t, 0)) for wd in widths] + [pl.BlockSpec((tt, N), lambda t: (t, 0))],
        out_specs=_full((Ka, N)), out_shape=jax.ShapeDtypeStruct((Ka, N), BF16),
        scratch_shapes=[pltpu.VMEM((Ka, N), F32)],
        compiler_params=_cp("arbitrary"), name=name)(*pieces, b)


def _rms_bwd_dx(x, g, w, dy, dres, *, tt, wt, name, dep=None):
    pieces = list(dy) if isinstance(dy, (list, tuple)) else [dy]
    n = len(pieces)
    widths = [p.shape[1] for p in pieces]
    T, K = x.shape

    def kernel_body(x_ref, g_ref, w_ref, *rest):
        dy_refs = rest[:n]
        dres_ref, dx_ref, dg_ref = rest[n:]

        @pl.when(pl.program_id(0) == 0)
        def _():
            dg_ref[...] = jnp.zeros_like(dg_ref)

        if n == 1:
            dxn = (_dot if wt else _dot_nt)(dy_refs[0][...], w_ref[...])
        else:
            dxn, off = 0.0, 0
            for dy_ref, wd in zip(dy_refs, widths):
                dxn = dxn + _dot(dy_ref[...], w_ref[off:off + wd, :])
                off += wd
        xf = x_ref[...]
        r = lax.rsqrt(jnp.mean(xf * xf, axis=-1, keepdims=True) + EPS)
        xhat = xf * r
        dg_ref[...] += jnp.sum(dxn * xhat, axis=0, keepdims=True)
        dxhat = dxn * g_ref[...]
        dx_ref[...] = dres_ref[...] + r * (dxhat - xhat * jnp.mean(dxhat * xhat, axis=-1, keepdims=True))

    assert n == 1 or wt
    body, dep_specs, dep_args = _dep(kernel_body, 4 + n, dep)
    return pl.pallas_call(
        body, grid=(T // tt,),
        in_specs=[pl.BlockSpec((tt, K), lambda i: (i, 0)), _full((1, K)), _full(w.shape)]
        + [pl.BlockSpec((tt, wd), lambda i: (i, 0)) for wd in widths]
        + [pl.BlockSpec((tt, K), lambda i: (i, 0))] + dep_specs,
        out_specs=[pl.BlockSpec((tt, K), lambda i: (i, 0)), _full((1, K))],
        out_shape=[jax.ShapeDtypeStruct((T, K), F32), jax.ShapeDtypeStruct((1, K), F32)],
        compiler_params=_cp("arbitrary"), name=name)(x, g, w, *pieces, dres, *dep_args)


def _loss_kernel(y, tgt, *, tt, name):
    T, D = y.shape

    def body(y_ref, t_ref, dy_ref, acc_ref):
        @pl.when(pl.program_id(0) == 0)
        def _():
            acc_ref[...] = jnp.zeros_like(acc_ref)

        e = y_ref[...] - t_ref[...]
        dy_ref[...] = e * (1.0 / D)
        acc_ref[...] += jnp.sum(e * e, axis=0, keepdims=True)

    return pl.pallas_call(
        body, grid=(T // tt,),
        in_specs=[pl.BlockSpec((tt, D), lambda i: (i, 0)), pl.BlockSpec((tt, D), lambda i: (i, 0))],
        out_specs=[pl.BlockSpec((tt, D), lambda i: (i, 0)), _full((1, D))],
        out_shape=[jax.ShapeDtypeStruct((T, D), F32), jax.ShapeDtypeStruct((1, D), F32)],
        compiler_params=_cp("arbitrary"), name=name)(y, tgt)


HGRN_TB = 512
HGRN_NCH = HGRN_TB // CHUNK
HGRN_HPB = 6


def _hgrn_chunk_fwd(q, z, lbv, tril01):
    sig = _sigmoid(z)
    f = lbv + (1.0 - lbv) * sig
    kk = 1.0 - f
    b = _dot3(tril01, jnp.log(f))
    bend = b[CHUNK - 1:CHUNK, :]
    sq = _sigmoid(q)
    eb = jnp.exp(b)
    emb = jnp.exp(-b)
    eo = jnp.exp(bend - b)
    dec = jnp.exp(bend)
    return sig, f, kk, sq, eb, emb, eo, dec


def _hgrn2_fwd(proj, lb, *, name):
    T = proj.shape[0]
    nT = T // HGRN_TB
    nC = T // CHUNK

    def body(q_ref, z_ref, v_ref, lb_ref, o_ref, st_ref, state):
        @pl.when(pl.program_id(1) == 0)
        def _():
            state[...] = jnp.zeros_like(state)

        row = lax.broadcasted_iota(jnp.int32, (CHUNK, CHUNK), 0)
        col = lax.broadcasted_iota(jnp.int32, (CHUNK, CHUNK), 1)
        causal = row >= col
        tril01 = causal.astype(BF16)

        def chunk(c, carry):
            rows = pl.ds(pl.multiple_of(c * CHUNK, CHUNK), CHUNK)
            for hh in range(HGRN_HPB):
                sl = slice(hh * HEAD_DIM, (hh + 1) * HEAD_DIM)
                q = q_ref[rows, sl]
                v = v_ref[rows, sl].astype(BF16)
                sig, f, kk, sq, eb, emb, eo, dec = _hgrn_chunk_fwd(q, z_ref[rows, sl], lb_ref[:, sl], tril01)
                qi = (q * sq * eb).astype(BF16)
                ki = (kk * emb).astype(BF16)
                ko = (kk * eo).astype(BF16)
                st = state[hh]
                att = jnp.where(causal, _dot_nt(qi, ki), 0.0)
                o_ref[rows, sl] = _dot(att, v) + _dot_nt(qi, st)
                st_ref[c, hh] = st
                state[hh] = st * dec + _dot_tn(v, ko)
            return carry

        lax.fori_loop(0, HGRN_NCH, chunk, 0)

    W = HGRN_HPB * HEAD_DIM
    nG = A_HEADS // HGRN_HPB
    hb = lambda off: pl.BlockSpec((HGRN_TB, W), lambda h, i: (i, off + h))
    return pl.pallas_call(
        body, grid=(nG, nT),
        in_specs=[hb(0), hb(nG), hb(2 * nG), pl.BlockSpec((1, W), lambda h, i: (0, h))],
        out_specs=[hb(0), pl.BlockSpec((HGRN_NCH, HGRN_HPB, HEAD_DIM, HEAD_DIM), lambda h, i: (i, h, 0, 0))],
        out_shape=[jax.ShapeDtypeStruct((T, A_WIDTH), F32), jax.ShapeDtypeStruct((nC, A_HEADS, HEAD_DIM, HEAD_DIM), F32)],
        scratch_shapes=[pltpu.VMEM((HGRN_HPB, HEAD_DIM, HEAD_DIM), F32)],
        compiler_params=_cp("parallel", "arbitrary"), name=name)(proj, proj, proj, lb)


def _hgrn2_bwd(proj, lb, st_all, do, *, name):
    T = proj.shape[0]
    nT = T // HGRN_TB

    def body(q_ref, z_ref, v_ref, lb_ref, st_ref, do_ref, dq_ref, dz_ref, dv_ref, dlb_ref, dstate):
        @pl.when(pl.program_id(1) == 0)
        def _():
            dstate[...] = jnp.zeros_like(dstate)
            dlb_ref[...] = jnp.zeros_like(dlb_ref)

        row = lax.broadcasted_iota(jnp.int32, (CHUNK, CHUNK), 0)
        col = lax.broadcasted_iota(jnp.int32, (CHUNK, CHUNK), 1)
        causal = row >= col
        tril01 = causal.astype(BF16)
        triu01 = (row <= col).astype(BF16)

        def chunk(cc, carry):
            c = HGRN_NCH - 1 - cc
            rows = pl.ds(pl.multiple_of(c * CHUNK, CHUNK), CHUNK)
            for hh in range(HGRN_HPB):
                sl = slice(hh * HEAD_DIM, (hh + 1) * HEAD_DIM)
                lbv = lb_ref[:, sl]
                q = q_ref[rows, sl]
                v = v_ref[rows, sl].astype(BF16)
                sig, f, kk, sq, eb, emb, eo, dec = _hgrn_chunk_fwd(q, z_ref[rows, sl], lbv, tril01)
                qi32 = q * sq * eb
                ki32 = kk * emb
                ko32 = kk * eo
                qi, ki, ko = qi32.astype(BF16), ki32.astype(BF16), ko32.astype(BF16)
                att = jnp.where(causal, _dot_nt(qi, ki), 0.0).astype(BF16)
                dout = do_ref[rows, sl].astype(BF16)
                st = st_ref[c, hh]
                dst = dstate[hh]
                dst16 = dst.astype(BF16)
                datt = jnp.where(causal, _dot_nt(dout, v), 0.0).astype(BF16)
                dqi = _dot(datt, ki) + _dot(dout, st)
                dki = _dot_tn(datt, qi)
                dv_ref[rows, sl] = (_dot_tn(att, dout) + _dot_nt(ko, dst16)).astype(BF16)
                dko = _dot(v, dst16)
                ddec = jnp.sum(dst * st, axis=0, keepdims=True)
                dstate[hh] = dst * dec + _dot_tn(dout, qi)
                dkk = dki * emb + dko * eo
                db = dqi * qi32 - dki * ki32 - dko * ko32
                dbend = jnp.sum(dko * ko32, axis=0, keepdims=True) + ddec * dec
                dlogf = _dot3(triu01, db) + dbend
                df = dlogf / f - dkk
                dz_ref[rows, sl] = (df * (1.0 - lbv) * sig * (1.0 - sig)).astype(BF16)
                dlb_ref[:, sl] += jnp.sum(df * (1.0 - sig), axis=0, keepdims=True)
                dq_ref[rows, sl] = (dqi * eb * (sq * (1.0 + q * (1.0 - sq)))).astype(BF16)
            return carry

        lax.fori_loop(0, HGRN_NCH, chunk, 0)

    W = HGRN_HPB * HEAD_DIM
    nG = A_HEADS // HGRN_HPB
    hb = lambda off: pl.BlockSpec((HGRN_TB, W), lambda h, i: (nT - 1 - i, off + h))
    hlb = pl.BlockSpec((1, W), lambda h, i: (0, h))
    o16 = jax.ShapeDtypeStruct((T, A_WIDTH), BF16)
    return pl.pallas_call(
        body, grid=(nG, nT),
        in_specs=[hb(0), hb(nG), hb(2 * nG), hlb,
                  pl.BlockSpec((HGRN_NCH, HGRN_HPB, HEAD_DIM, HEAD_DIM), lambda h, i: (nT - 1 - i, h, 0, 0)), hb(0)],
        out_specs=[hb(0), hb(0), hb(0), hlb],
        out_shape=[o16, o16, o16, jax.ShapeDtypeStruct((1, A_WIDTH), F32)],
        scratch_shapes=[pltpu.VMEM((HGRN_HPB, HEAD_DIM, HEAD_DIM), F32)],
        compiler_params=_cp("parallel", "arbitrary"), name=name)(proj, proj, proj, lb, st_all, do)


def _head_rms(x):
    r = lax.rsqrt(jnp.mean(x * x, axis=-1, keepdims=True) + EPS)
    return x * r, r


def _head_rms_bwd(dxhat, xhat, r):
    return r * (dxhat - xhat * jnp.mean(dxhat * xhat, axis=-1, keepdims=True))


def _a_post_fwd(o, proj, onorm, *, tt, name):
    T = o.shape[0]

    def body(o_ref, g_ref, w_ref, y_ref):
        for h in range(A_HEADS):
            sl = slice(h * HEAD_DIM, (h + 1) * HEAD_DIM)
            xhat, _ = _head_rms(o_ref[:, sl])
            g = g_ref[:, sl]
            y_ref[:, sl] = xhat * w_ref[:, sl] * (g * _sigmoid(g))

    blk = lambda c: pl.BlockSpec((tt, A_WIDTH), lambda i: (i, c))
    return pl.pallas_call(
        body, grid=(T // tt,), in_specs=[blk(0), blk(3), _full((1, A_WIDTH))], out_specs=blk(0),
        out_shape=jax.ShapeDtypeStruct((T, A_WIDTH), F32),
        compiler_params=_cp("parallel"), name=name)(o, proj, onorm)


def _a_post_bwd(o, proj, onorm, dmix, *, tt, name, dep=None):
    T = o.shape[0]

    def kernel_body(o_ref, g_ref, w_ref, dy_ref, do_ref, dg_ref, dw_ref):
        @pl.when(pl.program_id(0) == 0)
        def _():
            dw_ref[...] = jnp.zeros_like(dw_ref)

        for h in range(A_HEADS):
            sl = slice(h * HEAD_DIM, (h + 1) * HEAD_DIM)
            xhat, r = _head_rms(o_ref[:, sl])
            g = g_ref[:, sl]
            s = _sigmoid(g)
            dy = dy_ref[:, sl]
            w = w_ref[:, sl]
            dg_ref[:, sl] = (dy * xhat * w * (s * (1.0 + g * (1.0 - s)))).astype(BF16)
            dyn = dy * (g * s)
            dw_ref[:, sl] += jnp.sum(dyn * xhat, axis=0, keepdims=True)
            do_ref[:, sl] = _head_rms_bwd(dyn * w, xhat, r)

    blk = lambda c: pl.BlockSpec((tt, A_WIDTH), lambda i: (i, c))
    body, dep_specs, dep_args = _dep(kernel_body, 4, dep)
    return pl.pallas_call(
        body, grid=(T // tt,), in_specs=[blk(0), blk(3), _full((1, A_WIDTH)), blk(0)] + dep_specs,
        out_specs=[blk(0), blk(0), _full((1, A_WIDTH))],
        out_shape=[jax.ShapeDtypeStruct((T, A_WIDTH), F32), jax.ShapeDtypeStruct((T, A_WIDTH), BF16),
                   jax.ShapeDtypeStruct((1, A_WIDTH), F32)],
        compiler_params=_cp("arbitrary"), name=name)(o, proj, onorm, dmix, *dep_args)


def _mem_head_masks(n):
    lane = lax.broadcasted_iota(jnp.int32, (n, MEM_WIDTH), 1)
    return [(lane >= m * MEM_HEAD_DIM) & (lane < (m + 1) * MEM_HEAD_DIM) for m in range(MEM_HEADS)]


def _mem_head_rms(x, masks):
    x2 = x * x
    r = jnp.zeros_like(x)
    for mk in masks:
        ms = jnp.sum(jnp.where(mk, x2, 0.0), axis=-1, keepdims=True) * (1.0 / MEM_HEAD_DIM)
        r = jnp.where(mk, lax.rsqrt(ms + EPS), r)
    return x * r, r


def _mem_head_rms_bwd(dxhat, xhat, r, masks):
    t = dxhat * xhat
    m = jnp.zeros_like(t)
    for mk in masks:
        m = jnp.where(mk, jnp.sum(jnp.where(mk, t, 0.0), axis=-1, keepdims=True) * (1.0 / MEM_HEAD_DIM), m)
    return r * (dxhat - xhat * m)


MEM_SCALE = MEM_HEAD_DIM ** -0.5


def _mem_attn_fwd(proj, qcol, mkv, qn_w, kn_w, *, tt, name):
    T = proj.shape[0]

    def body(q_ref, k_ref, v_ref, qw_ref, kw_ref, o_ref):
        qmasks = _mem_head_masks(tt)
        kmasks = _mem_head_masks(MEM_TOKENS)
        qhat, _ = _mem_head_rms(q_ref[...], qmasks)
        qn = qhat * qw_ref[...]
        khat, _ = _mem_head_rms(k_ref[...], kmasks)
        kn = (khat * kw_ref[...]).astype(BF16)
        v = v_ref[...].astype(BF16)
        out = jnp.zeros((tt, MEM_WIDTH), F32)
        for m in range(MEM_HEADS):
            s = _dot_nt(jnp.where(qmasks[m], qn, 0.0), kn) * MEM_SCALE
            s = s - jnp.max(s, axis=-1, keepdims=True)
            p = jnp.exp(s)
            p = p / jnp.sum(p, axis=-1, keepdims=True)
            out = jnp.where(qmasks[m], _dot(p, v), out)
        o_ref[...] = out

    return pl.pallas_call(
        body, grid=(T // tt,),
        in_specs=[pl.BlockSpec((tt, MEM_WIDTH), lambda i: (i, qcol)), pl.BlockSpec((MEM_TOKENS, MEM_WIDTH), lambda i: (0, 0)),
                  pl.BlockSpec((MEM_TOKENS, MEM_WIDTH), lambda i: (0, 1)), _full((1, MEM_WIDTH)), _full((1, MEM_WIDTH))],
        out_specs=pl.BlockSpec((tt, MEM_WIDTH), lambda i: (i, 0)),
        out_shape=jax.ShapeDtypeStruct((T, MEM_WIDTH), F32),
        compiler_params=_cp("parallel"), name=name)(proj, mkv, mkv, qn_w, kn_w)


def _mem_attn_bwd(proj, qcol, mkv, qn_w, kn_w, dmix, *, tt, name):
    T = proj.shape[0]
    nsteps = T // tt
    ocol = (dmix.shape[1] - MEM_WIDTH) // MEM_WIDTH

    def body(q_ref, k_ref, v_ref, qw_ref, kw_ref, do_ref, dq_ref, dkv_ref, dqw_ref, dkw_ref, dk_acc, dv_acc):
        step = pl.program_id(0)

        @pl.when(step == 0)
        def _():
            dk_acc[...] = jnp.zeros_like(dk_acc)
            dv_acc[...] = jnp.zeros_like(dv_acc)
            dqw_ref[...] = jnp.zeros_like(dqw_ref)

        qmasks = _mem_head_masks(tt)
        kmasks = _mem_head_masks(MEM_TOKENS)
        qhat, qr = _mem_head_rms(q_ref[...], qmasks)
        qn = qhat * qw_ref[...]
        khat, kr = _mem_head_rms(k_ref[...], kmasks)
        kn = (khat * kw_ref[...]).astype(BF16)
        v = v_ref[...].astype(BF16)
        dout = do_ref[...]
        dqn = jnp.zeros((tt, MEM_WIDTH), F32)
        dkn = jnp.zeros((MEM_TOKENS, MEM_WIDTH), F32)
        dvv = jnp.zeros((MEM_TOKENS, MEM_WIDTH), F32)
        for m in range(MEM_HEADS):
            qm = jnp.where(qmasks[m], qn, 0.0).astype(BF16)
            s = _dot_nt(qm, kn) * MEM_SCALE
            s = s - jnp.max(s, axis=-1, keepdims=True)
            p = jnp.exp(s)
            p = p / jnp.sum(p, axis=-1, keepdims=True)
            dom = jnp.where(qmasks[m], dout, 0.0).astype(BF16)
            dp = _dot_nt(dom, v)
            ds = (p * (dp - jnp.sum(p * dp, axis=-1, keepdims=True)) * MEM_SCALE).astype(BF16)
            dqn = jnp.where(qmasks[m], _dot(ds, kn), dqn)
            dkn = jnp.where(kmasks[m], _dot_tn(ds, qm), dkn)
            dvv = jnp.where(kmasks[m], _dot_tn(p, dom), dvv)
        dqw_ref[...] += jnp.sum(dqn * qhat, axis=0, keepdims=True)
        dq_ref[...] = _mem_head_rms_bwd(dqn * qw_ref[...], qhat, qr, qmasks).astype(BF16)
        dk_acc[...] += dkn
        dv_acc[...] += dvv

        @pl.when(step == nsteps - 1)
        def _():
            dk = dk_acc[...]
            dkw_ref[...] = jnp.sum(dk * khat, axis=0, keepdims=True)
            dkv_ref[:, :MEM_WIDTH] = _mem_head_rms_bwd(dk * kw_ref[...], khat, kr, kmasks)
            dkv_ref[:, MEM_WIDTH:] = dv_acc[...]

    return pl.pallas_call(
        body, grid=(nsteps,),
        in_specs=[pl.BlockSpec((tt, MEM_WIDTH), lambda i: (i, qcol)), pl.BlockSpec((MEM_TOKENS, MEM_WIDTH), lambda i: (0, 0)),
                  pl.BlockSpec((MEM_TOKENS, MEM_WIDTH), lambda i: (0, 1)), _full((1, MEM_WIDTH)), _full((1, MEM_WIDTH)),
                  pl.BlockSpec((tt, MEM_WIDTH), lambda i: (i, ocol))],
        out_specs=[pl.BlockSpec((tt, MEM_WIDTH), lambda i: (i, 0)), _full((MEM_TOKENS, 2 * MEM_WIDTH)),
                   _full((1, MEM_WIDTH)), _full((1, MEM_WIDTH))],
        out_shape=[jax.ShapeDtypeStruct((T, MEM_WIDTH), BF16), jax.ShapeDtypeStruct((MEM_TOKENS, 2 * MEM_WIDTH), F32),
                   jax.ShapeDtypeStruct((1, MEM_WIDTH), F32), jax.ShapeDtypeStruct((1, MEM_WIDTH), F32)],
        scratch_shapes=[pltpu.VMEM((MEM_TOKENS, MEM_WIDTH), F32), pltpu.VMEM((MEM_TOKENS, MEM_WIDTH), F32)],
        compiler_params=_cp("arbitrary"), name=name)(proj, mkv, mkv, qn_w, kn_w, dmix)


HALF = HEAD_DIM // 2
ATT_SCALE = HEAD_DIM ** -0.5
NEG = -1e30


def _rope_tables(T):
    inv = ROPE_THETA ** (-jnp.arange(HALF, dtype=F32) / HALF)
    ang = jnp.arange(T, dtype=F32)[:, None] * inv[None, :]
    cos, sin = jnp.cos(ang), jnp.sin(ang)
    return jnp.concatenate([cos, cos], axis=-1), jnp.concatenate([-sin, sin], axis=-1)


def _rope(x, cosf, sinsg):
    return x * cosf + pltpu.roll(x, HALF, 1) * sinsg


def _rope_bwd(dy, cosf, sinsg):
    return dy * cosf + pltpu.roll(dy * sinsg, HALF, 1)


def _headnorm_rope_fwd(x, w_heads, cosf, sinsg, *, col0, n_heads, tt, name):
    T = x.shape[0]
    W = n_heads * HEAD_DIM

    def body(x_ref, w_ref, c_ref, s_ref, y_ref):
        c, s = c_ref[...], s_ref[...]
        for h in range(n_heads):
            sl = slice(h * HEAD_DIM, (h + 1) * HEAD_DIM)
            xhat, _ = _head_rms(x_ref[:, sl])
            y_ref[:, sl] = _rope(xhat * w_ref[:, sl], c, s)

    tbl = pl.BlockSpec((tt, HEAD_DIM), lambda i: (i, 0))
    return pl.pallas_call(
        body, grid=(T // tt,),
        in_specs=[pl.BlockSpec((tt, W), lambda i: (i, col0)), _full((1, W)), tbl, tbl],
        out_specs=pl.BlockSpec((tt, W), lambda i: (i, 0)),
        out_shape=jax.ShapeDtypeStruct((T, W), F32),
        compiler_params=_cp("parallel"), name=name)(x, w_heads, cosf, sinsg)


def _q_prep_bwd(proj, w_heads, cosf, sinsg, dqs, *, tt, name):
    T = proj.shape[0]
    W = N_GROUPS * B_WIDTH

    def body(x_ref, w_ref, c_ref, s_ref, d0, d1, d2, dx_ref, dw_ref):
        @pl.when(pl.program_id(0) == 0)
        def _():
            dw_ref[...] = jnp.zeros_like(dw_ref)

        c, s = c_ref[...], s_ref[...]
        for gi, d_ref in enumerate((d0, d1, d2)):
            for h in range(B_HEADS):
                sl = slice((gi * B_HEADS + h) * HEAD_DIM, (gi * B_HEADS + h + 1) * HEAD_DIM)
                xhat, r = _head_rms(x_ref[:, sl])
                dyn = _rope_bwd(d_ref[:, h * HEAD_DIM:(h + 1) * HEAD_DIM], c, s)
                dw_ref[:, sl] += jnp.sum(dyn * xhat, axis=0, keepdims=True)
                dx_ref[:, sl] = _head_rms_bwd(dyn * w_ref[:, sl], xhat, r).astype(BF16)

    tbl = pl.BlockSpec((tt, HEAD_DIM), lambda i: (i, 0))
    dyb = pl.BlockSpec((tt, B_WIDTH), lambda i: (i, 0))
    return pl.pallas_call(
        body, grid=(T // tt,),
        in_specs=[pl.BlockSpec((tt, W), lambda i: (i, 0)), _full((1, W)), tbl, tbl, dyb, dyb, dyb],
        out_specs=[pl.BlockSpec((tt, W), lambda i: (i, 0)), _full((1, W))],
        out_shape=[jax.ShapeDtypeStruct((T, W), BF16), jax.ShapeDtypeStruct((1, W), F32)],
        compiler_params=_cp("arbitrary"), name=name)(proj, w_heads, cosf, sinsg, *dqs)


def _kv_prep_bwd(kv, w_heads, cosf, sinsg, dks, dvs, *, tt, name):
    T = kv.shape[0]

    def body(x_ref, w_ref, c_ref, s_ref, k0, k1, k2, v0, v1, v2, dx_ref, dw_ref):
        @pl.when(pl.program_id(0) == 0)
        def _():
            dw_ref[...] = jnp.zeros_like(dw_ref)

        c, s = c_ref[...], s_ref[...]
        for h in range(B_HEADS):
            sl = slice(h * HEAD_DIM, (h + 1) * HEAD_DIM)
            vs = slice(B_WIDTH + h * HEAD_DIM, B_WIDTH + (h + 1) * HEAD_DIM)
            xhat, r = _head_rms(x_ref[:, sl])
            dyn = _rope_bwd(k0[:, sl] + k1[:, sl] + k2[:, sl], c, s)
            dw_ref[:, sl] += jnp.sum(dyn * xhat, axis=0, keepdims=True)
            dx_ref[:, sl] = _head_rms_bwd(dyn * w_ref[:, sl], xhat, r).astype(BF16)
            dx_ref[:, vs] = (v0[:, sl] + v1[:, sl] + v2[:, sl]).astype(BF16)

    tbl = pl.BlockSpec((tt, HEAD_DIM), lambda i: (i, 0))
    dyb = pl.BlockSpec((tt, B_WIDTH), lambda i: (i, 0))
    return pl.pallas_call(
        body, grid=(T // tt,),
        in_specs=[dyb, _full((1, B_WIDTH)), tbl, tbl] + [dyb] * 6,
        out_specs=[pl.BlockSpec((tt, 2 * B_WIDTH), lambda i: (i, 0)), _full((1, B_WIDTH))],
        out_shape=[jax.ShapeDtypeStruct((T, 2 * B_WIDTH), BF16), jax.ShapeDtypeStruct((1, B_WIDTH), F32)],
        compiler_params=_cp("arbitrary"), name=name)(kv, w_heads, cosf, sinsg, *dks, *dvs)


def _band_masks(n_is_first=None):
    row = lax.broadcasted_iota(jnp.int32, (SPAN, SPAN), 0)
    col = lax.broadcasted_iota(jnp.int32, (SPAN, SPAN), 1)
    return row >= col, col >= row


def _dil_views(T, d):
    L = T // d
    return L, L // SPAN


def _dil_fwd(qr, kr, kv, gi, d, *, name):
    T = qr.shape[0]
    L, nb = _dil_views(T, d)

    def body(q_ref, kc_ref, kp_ref, vc_ref, vp_ref, o_ref, lse_ref):
        cur_ok, prev_band = _band_masks()
        prev_ok = prev_band & (pl.program_id(1) > 0)
        for h in range(B_HEADS):
            sl = slice(h * HEAD_DIM, (h + 1) * HEAD_DIM)
            q = q_ref[:, sl]
            sc = jnp.where(cur_ok, _dot_nt(q, kc_ref[:, sl]) * ATT_SCALE, NEG)
            sp = jnp.where(prev_ok, _dot_nt(q, kp_ref[:, sl]) * ATT_SCALE, NEG)
            m = jnp.maximum(jnp.max(sc, axis=-1, keepdims=True), jnp.max(sp, axis=-1, keepdims=True))
            pc = jnp.exp(sc - m)
            pp = jnp.exp(sp - m)
            l = jnp.sum(pc, axis=-1, keepdims=True) + jnp.sum(pp, axis=-1, keepdims=True)
            o_ref[:, sl] = (_dot(pc, vc_ref[:, sl]) + _dot(pp, vp_ref[:, sl])) / l
            lse_ref[:, sl] = jnp.broadcast_to(m + jnp.log(l), (SPAN, HEAD_DIM))

    blk = lambda f: pl.BlockSpec((SPAN, B_WIDTH), f)
    cur = lambda r, n: (n, r)
    prev = lambda r, n: (jnp.maximum(n - 1, 0), r)
    ov = jax.ShapeDtypeStruct((L, d * B_WIDTH), F32)
    o, lse = pl.pallas_call(
        body, grid=(d, nb),
        in_specs=[blk(lambda r, n: (n, r * N_GROUPS + gi)), blk(cur), blk(prev),
                  blk(lambda r, n: (n, 2 * r + 1)), blk(lambda r, n: (jnp.maximum(n - 1, 0), 2 * r + 1))],
        out_specs=[blk(cur), blk(cur)], out_shape=[ov, ov],
        compiler_params=_cp("parallel", "arbitrary"), name=name,
    )(qr.reshape(L, d * N_GROUPS * B_WIDTH), kr.reshape(L, d * B_WIDTH), kr.reshape(L, d * B_WIDTH),
      kv.reshape(L, d * 2 * B_WIDTH), kv.reshape(L, d * 2 * B_WIDTH))
    return o.reshape(T, B_WIDTH), lse.reshape(T, B_WIDTH)


def _dil_combine_fwd(os_, lses, *, tt, name):
    T = os_[0].shape[0]

    def body(o0, o1, o2, l0, l1, l2, y_ref, lse_ref):
        a, b, c = l0[...], l1[...], l2[...]
        m = jnp.maximum(jnp.maximum(a, b), c)
        wa, wb, wc = jnp.exp(a - m), jnp.exp(b - m), jnp.exp(c - m)
        den = wa + wb + wc
        y_ref[...] = (wa * o0[...] + wb * o1[...] + wc * o2[...]) / den
        lse_ref[...] = m + jnp.log(den)

    blk = pl.BlockSpec((tt, B_WIDTH), lambda i: (i, 0))
    sh = jax.ShapeDtypeStruct((T, B_WIDTH), F32)
    return pl.pallas_call(
        body, grid=(T // tt,), in_specs=[blk] * 6, out_specs=[blk, blk], out_shape=[sh, sh],
        compiler_params=_cp("parallel"), name=name)(*os_, *lses)


def _dil_bwd_prep(dmix, mix_main, *, tt, name, dep=None):
    T = mix_main.shape[0]

    def kernel_body(dy_ref, y_ref, dd_ref):
        for h in range(B_HEADS):
            sl = slice(h * HEAD_DIM, (h + 1) * HEAD_DIM)
            dd_ref[:, sl] = jnp.broadcast_to(jnp.sum(dy_ref[:, sl] * y_ref[:, sl], axis=-1, keepdims=True), (tt, HEAD_DIM))

    blk = pl.BlockSpec((tt, B_WIDTH), lambda i: (i, 0))
    body, dep_specs, dep_args = _dep(kernel_body, 2, dep)
    return pl.pallas_call(
        body, grid=(T // tt,), in_specs=[blk, blk] + dep_specs, out_specs=blk,
        out_shape=jax.ShapeDtypeStruct((T, B_WIDTH), F32),
        compiler_params=_cp("parallel"), name=name)(dmix, mix_main, *dep_args)


DILS_UNROLL = 4


def _dils_specs(gi, d, nblk):
    blk = lambda f: pl.BlockSpec((SPAN * d, HEAD_DIM), f)
    return {
        "q": blk(lambda h, n: (n, gi * B_HEADS + h)), "q_next": blk(lambda h, n: (jnp.minimum(n + 1, nblk - 1), gi * B_HEADS + h)),
        "cur": blk(lambda h, n: (n, h)), "prev": blk(lambda h, n: (jnp.maximum(n - 1, 0), h)),
        "next": blk(lambda h, n: (jnp.minimum(n + 1, nblk - 1), h)),
        "v": blk(lambda h, n: (n, B_HEADS + h)), "v_prev": blk(lambda h, n: (jnp.maximum(n - 1, 0), B_HEADS + h)),
    }


def _dils_fwd(qr, kr, kv, gi, d, *, name):
    T = qr.shape[0]
    nblk = T // (SPAN * d)
    sp = _dils_specs(gi, d, nblk)

    def body(q_ref, kc_ref, kp_ref, vc_ref, vp_ref, o_ref, lse_ref):
        cur_ok, prev_band = _band_masks()
        prev_ok = prev_band & (pl.program_id(1) > 0)

        def residue(r, carry):
            rows = pl.ds(r, SPAN, stride=d)
            q = q_ref[rows, :]
            sc = jnp.where(cur_ok, _dot_nt(q, kc_ref[rows, :]) * ATT_SCALE, NEG)
            sp_ = jnp.where(prev_ok, _dot_nt(q, kp_ref[rows, :]) * ATT_SCALE, NEG)
            m = jnp.maximum(jnp.max(sc, axis=-1, keepdims=True), jnp.max(sp_, axis=-1, keepdims=True))
            pc = jnp.exp(sc - m)
            pp = jnp.exp(sp_ - m)
            l = jnp.sum(pc, axis=-1, keepdims=True) + jnp.sum(pp, axis=-1, keepdims=True)
            o_ref[rows, :] = (_dot(pc, vc_ref[rows, :]) + _dot(pp, vp_ref[rows, :])) / l
            lse_ref[rows, :] = jnp.broadcast_to(m + jnp.log(l), (SPAN, HEAD_DIM))
            return carry

        lax.fori_loop(0, d, residue, 0, unroll=DILS_UNROLL)

    sh = jax.ShapeDtypeStruct((T, B_WIDTH), F32)
    return pl.pallas_call(
        body, grid=(B_HEADS, nblk), in_specs=[sp["q"], sp["cur"], sp["prev"], sp["v"], sp["v_prev"]],
        out_specs=[sp["cur"], sp["cur"]], out_shape=[sh, sh],
        compiler_params=_cp("parallel", "arbitrary"), name=name)(qr, kr, kr, kv, kv)


DIL_BWD_GROUP = {1: 4, 4: 1, 16: 1}


def _dil_bwd(qr, kr, kv, dmix, lse, dd, gi, d, *, name):
    T = qr.shape[0]
    G = DIL_BWD_GROUP[d]
    band = SPAN * d
    tb = G * band
    nblk = T // tb

    def body(q_ref, dy_ref, lse_ref, dd_ref, kc_ref, kp_ref, vc_ref, vp_ref, dq_ref, dk_ref, dv_ref):
        n = pl.program_id(1)

        @pl.when(n == 0)
        def _():
            dk_ref[...] = jnp.zeros_like(dk_ref)
            dv_ref[...] = jnp.zeros_like(dv_ref)

        cur_ok, prev_band = _band_masks()
        base = pl.multiple_of(n * tb, SPAN)
        for j in range(G):
            def residue(r, carry, j=j):
                off = j * band + r
                rows = pl.ds(off, SPAN, stride=d)
                q, dy = q_ref[rows, :], dy_ref[rows, :]
                lse_h = jnp.max(lse_ref[rows, :], axis=-1, keepdims=True)
                dd_h = jnp.max(dd_ref[rows, :], axis=-1, keepdims=True)
                kc, vc = kc_ref[rows, :], vc_ref[rows, :]
                if j > 0:
                    before = pl.ds(off - band, SPAN, stride=d)
                    kp, vp = kc_ref[before, :], vc_ref[before, :]
                    prev_ok = prev_band
                else:
                    before = pl.ds((G - 1) * band + r, SPAN, stride=d)
                    kp, vp = kp_ref[before, :], vp_ref[before, :]
                    prev_ok = prev_band & (n > 0)
                pc = jnp.exp(jnp.where(cur_ok, _dot_nt(q, kc) * ATT_SCALE, NEG) - lse_h)
                pp = jnp.exp(jnp.where(prev_ok, _dot_nt(q, kp) * ATT_SCALE, NEG) - lse_h)
                dsc = pc * (_dot_nt(dy, vc) - dd_h) * ATT_SCALE
                dsp = pp * (_dot_nt(dy, vp) - dd_h) * ATT_SCALE
                dq_ref[rows, :] = _dot(dsc, kc) + _dot(dsp, kp)
                here = pl.ds(base + off, SPAN, stride=d)
                dk_ref[here, :] += _dot_tn(dsc, q)
                dv_ref[here, :] += _dot_tn(pc, dy)
                there = pl.ds(jnp.maximum(base + off - band, r), SPAN, stride=d)
                dk_ref[there, :] += _dot_tn(dsp, q)
                dv_ref[there, :] += _dot_tn(pp, dy)
                return carry

            lax.fori_loop(0, d, residue, 0, unroll=min(d, DILS_UNROLL))

    blk = lambda f: pl.BlockSpec((tb, HEAD_DIM), f)
    cur = lambda h, n: (n, h)
    prev = lambda h, n: (jnp.maximum(n - 1, 0), h)
    whole = pl.BlockSpec((T, HEAD_DIM), lambda h, n: (0, h))
    sh = jax.ShapeDtypeStruct((T, B_WIDTH), F32)
    return pl.pallas_call(
        body, grid=(B_HEADS, nblk),
        in_specs=[blk(lambda h, n: (n, gi * B_HEADS + h)), blk(cur), blk(cur), blk(cur), blk(cur), blk(prev),
                  blk(lambda h, n: (n, B_HEADS + h)), blk(lambda h, n: (jnp.maximum(n - 1, 0), B_HEADS + h))],
        out_specs=[blk(cur), whole, whole], out_shape=[sh, sh, sh],
        compiler_params=_cp("parallel", "arbitrary"), name=name)(qr, dmix, lse, dd, kr, kr, kv, kv)


A_MQ_COL = 4 * A_WIDTH // MEM_WIDTH
B_MQ_COL = N_GROUPS * B_WIDTH // MEM_WIDTH


def _row(v):
    return v.reshape(1, -1).astype(F32)


def _local_step(x, mem, tgt, get_w, P, put_g, first_dep=None, forward_point=lambda i, value: value):
    T = x.shape[0]
    cosf, sinsg = _rope_tables(T)
    lb_soft = jax.nn.softmax(P["a_lb_logits"].astype(F32), axis=0)
    lb = lb_soft[0:1]
    qw_heads = jnp.repeat(P["b_qnorm"][0], B_HEADS, axis=0).reshape(1, -1)
    kw_heads = jnp.tile(_row(P["b_knorm"]), (1, B_HEADS))
    mqw = [jnp.tile(_row(P["mem_qnorm"][l]), (1, MEM_HEADS)) for l in range(2)]
    mkw = [jnp.tile(_row(P["mem_knorm"][l]), (1, MEM_HEADS)) for l in range(2)]
    nmix = [_row(P["norm_mix"][l]) for l in range(2)]
    nffn = [_row(P["norm_ffn"][l]) for l in range(2)]
    mnorm = [_row(P["mem_norm"][l]) for l in range(2)]
    kvn = _row(P["kv_norm"])
    onorm = _row(P["a_onorm"])
    W = {}

    def w_of(name, after=None):
        if name not in W:
            W[name] = get_w(name, after)
        return W[name]

    proj_a, xn0 = _rms_matmul(x, nmix[0], w_of("a_w_in"), tt=512, tn=1664, wt=True, name="proj_a", dep=first_dep)
    mkv0, mn0 = _rms_matmul(mem, mnorm[0], w_of("w_mem_kv0"), tt=MEM_TOKENS, tn=2 * MEM_WIDTH, wt=False, name="mem_kv0")
    o_raw, st = _hgrn2_fwd(proj_a, lb, name="hgrn2_fwd")
    o_raw = forward_point(0, o_raw)
    mm0 = _a_post_fwd(o_raw, proj_a, onorm, tt=512, name="a_post_fwd")
    mo0 = _mem_attn_fwd(proj_a, A_MQ_COL, mkv0, mqw[0], mkw[0], tt=512, name="mem_attn_fwd0")
    hm0 = _mm_res(x, mm0, mo0, w_of("w_out0", mo0), tt=512, name="out_proj0")
    hm0 = forward_point(1, hm0)
    gu0, hn0 = _rms_matmul(hm0, nffn[0], w_of("w_gate_up0", hm0), tt=512, tn=1408, wt=True, out_dtype=BF16, name="gate_up0")
    h1 = _swiglu_down(hm0, gu0, w_of("w_down0", gu0), tt=512, name="down0")
    h1 = forward_point(2, h1)
    kv, hkn = _rms_matmul(h1, kvn, w_of("w_kv", h1), tt=512, tn=768, wt=True, name="kv_proj")
    kr = _headnorm_rope_fwd(kv, kw_heads, cosf, sinsg, col0=0, n_heads=B_HEADS, tt=512, name="k_prep")

    proj_b, xn1 = _rms_matmul(h1, nmix[1], w_of("b_w_in", kr), tt=512, tn=1280, wt=True, name="proj_b")
    proj_b = forward_point(3, proj_b)
    mkv1, mn1 = _rms_matmul(mem, mnorm[1], w_of("w_mem_kv1", kr), tt=MEM_TOKENS, tn=2 * MEM_WIDTH, wt=False, name="mem_kv1")
    qr = _headnorm_rope_fwd(proj_b, qw_heads, cosf, sinsg, col0=0, n_heads=N_GROUPS * B_HEADS, tt=512, name="q_prep")
    outs = [(_dil_fwd if d == 1 else _dils_fwd)(qr, kr, kv, gi, d, name=f"dil_fwd{gi}") for gi, d in enumerate(DILATIONS)]
    mm1, lse_tot = _dil_combine_fwd([o for o, _ in outs], [s for _, s in outs], tt=512, name="dil_combine")
    mo1 = _mem_attn_fwd(proj_b, B_MQ_COL, mkv1, mqw[1], mkw[1], tt=512, name="mem_attn_fwd1")
    hm1 = _mm_res(h1, mm1, mo1, w_of("w_out1", mo1), tt=512, name="out_proj1")
    gu1, hn1 = _rms_matmul(hm1, nffn[1], w_of("w_gate_up1", hm1), tt=512, tn=1408, wt=True, out_dtype=BF16, name="gate_up1")
    y = _swiglu_down(hm1, gu1, w_of("w_down1", gu1), tt=512, name="down1")
    dy, sq = _loss_kernel(y, tgt, tt=512, name="loss")

    gP = {}
    zeros_mem = jnp.zeros((MEM_TOKENS, D_MODEL), F32)

    def ffn_bwd(l, dh, hm, gu, hn):
        dgu, g_wd = _swiglu_bwd(dh, gu, w_of(f"w_down{l}"), tt=256, name=f"swiglu_bwd{l}")
        g_wgu = _mm_tn(dgu, hn, tt=512, tka=1408, name=f"g_w_gate_up{l}")
        sent = put_g({f"w_down{l}": g_wd, f"w_gate_up{l}": g_wgu})
        dhm, g_nf = _rms_bwd_dx(hm, nffn[l], w_of(f"w_gate_up{l}"), dgu, dh, tt=256, wt=True, name=f"gate_up_bwd{l}", dep=sent)
        return dhm, g_nf

    def mix_bwd(l, dhm, mix_main, mix_mem, proj, qcol, mkv, mn):
        dmix, g_wout = _out_proj_bwd(dhm, mix_main, mix_mem, w_of(f"w_out{l}"), tt=512, name=f"out_proj_bwd{l}")
        dmq, dmkv, dqw, dkw = _mem_attn_bwd(proj, qcol, mkv, mqw[l], mkw[l], dmix, tt=512, name=f"mem_attn_bwd{l}")
        g_wmkv = _mm_tn(mn, dmkv, tt=MEM_TOKENS, tka=512, name=f"g_w_mem_kv{l}")
        sent = put_g({f"w_out{l}": g_wout, f"w_mem_kv{l}": g_wmkv})
        _, g_mn = _rms_bwd_dx(mem, mnorm[l], w_of(f"w_mem_kv{l}"), dmkv, zeros_mem, tt=MEM_TOKENS, wt=False, name=f"mem_kv_bwd{l}")
        fold = lambda v: v.reshape(MEM_HEADS, MEM_HEAD_DIM).sum(axis=0)
        return dmix, dmq, g_mn, fold(dqw), fold(dkw), sent

    dhm1, g_nf1 = ffn_bwd(1, dy, hm1, gu1, hn1)
    dmix1, dmq1, g_mn1, g_mq1, g_mk1, sent = mix_bwd(1, dhm1, mm1, mo1, proj_b, B_MQ_COL, mkv1, mn1)
    dd = _dil_bwd_prep(dmix1, mm1, tt=512, name="dil_bwd_prep", dep=sent)
    dqs, dks, dvs = [], [], []
    for gi, d in enumerate(DILATIONS):
        dq_g, dk_g, dv_g = _dil_bwd(qr, kr, kv, dmix1, lse_tot, dd, gi, d, name=f"dil_bwd{gi}")
        dqs.append(dq_g)
        dks.append(dk_g)
        dvs.append(dv_g)
    dq_raw, dqw = _q_prep_bwd(proj_b, qw_heads, cosf, sinsg, dqs, tt=512, name="q_prep_bwd")
    dkv, dkw = _kv_prep_bwd(kv, kw_heads, cosf, sinsg, dks, dvs, tt=512, name="kv_prep_bwd")
    dproj_b = [dq_raw, dmq1]
    g_wb = _mm_tn_pieces(dproj_b, xn1, tt=512, name="g_b_w_in")
    g_wkv = _mm_tn(dkv, hkn, tt=512, tka=768, name="g_w_kv")
    sent = put_g({"b_w_in": g_wb, "w_kv": g_wkv})
    dh1, g_nm1 = _rms_bwd_dx(h1, nmix[1], w_of("b_w_in"), dproj_b, dhm1, tt=512, wt=True, name="proj_b_bwd", dep=sent)
    dh1, g_kvn = _rms_bwd_dx(h1, kvn, w_of("w_kv"), dkv, dh1, tt=512, wt=True, name="kv_proj_bwd")

    dhm0, g_nf0 = ffn_bwd(0, dh1, hm0, gu0, hn0)
    dmix0, dmq0, g_mn0, g_mq0, g_mk0, sent = mix_bwd(0, dhm0, mm0, mo0, proj_a, A_MQ_COL, mkv0, mn0)
    do_raw, dg, g_onorm = _a_post_bwd(o_raw, proj_a, onorm, dmix0, tt=512, name="a_post_bwd", dep=sent)
    dq, dz, dv, dlb = _hgrn2_bwd(proj_a, lb, st, do_raw, name="hgrn2_bwd")
    dproj_a = [dq, dz, dv, dg, dmq0]
    sent = put_g({"a_w_in": _mm_tn_pieces(dproj_a, xn0, tt=512, name="g_a_w_in")})
    gx, g_nm0 = _rms_bwd_dx(x, nmix[0], w_of("a_w_in"), dproj_a, dhm0, tt=512, wt=True, name="proj_a_bwd", dep=sent)

    dl0 = lb_soft[0:1] * lb_soft[1:2] * dlb
    gP["a_lb_logits"] = jnp.concatenate([dl0, -dl0], axis=0)
    gP["a_onorm"] = g_onorm
    gP["norm_mix"] = jnp.concatenate([g_nm0, g_nm1], axis=0)
    gP["norm_ffn"] = jnp.concatenate([g_nf0, g_nf1], axis=0)
    gP["b_qnorm"] = dqw.reshape(N_GROUPS, B_HEADS, HEAD_DIM).sum(axis=1)[None]
    gP["kv_norm"] = g_kvn.reshape(-1)
    gP["b_knorm"] = dkw.reshape(B_HEADS, HEAD_DIM).sum(axis=0)
    gP["mem_norm"] = jnp.concatenate([g_mn0, g_mn1], axis=0)
    gP["mem_qnorm"] = jnp.stack([g_mq0, g_mq1])
    gP["mem_knorm"] = jnp.stack([g_mk0, g_mk1])
    return sq, gx, gP


MESH_ID = pl.DeviceIdType.MESH
HBM_SPEC = pl.BlockSpec(memory_space=pltpu.HBM)


def _position():
    return lax.axis_index("x"), lax.axis_index("y"), lax.axis_index("c")


def _all_gather(blocks, *, name):
    n = len(blocks)

    def body(*refs):
        x_refs, out_refs = refs[:n], refs[n:2 * n]
        send_sems, recv_sems, local_sems = refs[2 * n:]
        x, y, c = _position()
        me, sibling = (x, y, c), (x, y, 1 - c)
        chips = [(1 - x, y), (x, 1 - y), (1 - x, 1 - y)]

        def slot(a, px, py, pc):
            return out_refs[a].at[4 * px + 2 * py + pc]

        def copy(a, k, blk, to, src=None):
            return pltpu.make_async_remote_copy(
                src_ref=slot(a, *blk) if src is None else src, dst_ref=slot(a, *blk),
                send_sem=send_sems.at[7 * a + k], recv_sem=recv_sems.at[7 * a + k], device_id=to, device_id_type=MESH_ID)

        mine = [pltpu.make_async_copy(x_refs[a], slot(a, *me), local_sems.at[a]) for a in range(n)]
        for cp in mine:
            cp.start()
        first = []
        for a in range(n):
            first.append(copy(a, 0, me, sibling, src=x_refs[a]))
            first += [copy(a, 1 + j, me, (*chip, c), src=x_refs[a]) for j, chip in enumerate(chips)]
        for cp in first:
            cp.start()
        passed = []
        for j, chip in enumerate(chips):
            for a in range(n):
                copy(a, 1 + j, (*chip, c), me).wait_recv()
                cp = copy(a, 4 + j, (*chip, c), sibling)
                cp.start()
                passed.append(cp)
        for a in range(n):
            copy(a, 0, sibling, me).wait_recv()
            for j, chip in enumerate(chips):
                copy(a, 4 + j, (*chip, 1 - c), me).wait_recv()
        for cp in first + passed:
            cp.wait_send()
        for cp in mine:
            cp.wait()

    return pl.pallas_call(
        body, out_shape=[jax.ShapeDtypeStruct((N_DEV,) + b.shape, b.dtype) for b in blocks],
        in_specs=[HBM_SPEC] * n, out_specs=[HBM_SPEC] * n,
        scratch_shapes=[pltpu.SemaphoreType.DMA((7 * n,)), pltpu.SemaphoreType.DMA((7 * n,)), pltpu.SemaphoreType.DMA((n,))],
        name=name)(*blocks)


def _all_gather_direct(block, after, *, name):
    def body(x_ref, after_ref, out_ref, send_sems, recv_sems, local_sem):
        x, y, c = _position()
        me = 4 * x + 2 * y + c
        mine = pltpu.make_async_copy(x_ref, out_ref.at[me], local_sem)
        mine.start()
        copies = []
        for k in ALL_PEERS:
            cp = pltpu.make_async_remote_copy(
                src_ref=x_ref, dst_ref=out_ref.at[me], send_sem=send_sems.at[k - 1], recv_sem=recv_sems.at[k - 1],
                device_id=_peer(k, x, y, c), device_id_type=MESH_ID)
            cp.start()
            copies.append(cp)
        for cp in copies:
            cp.wait()
        mine.wait()

    return pl.pallas_call(
        body, out_shape=jax.ShapeDtypeStruct((N_DEV,) + block.shape, block.dtype),
        in_specs=[HBM_SPEC, pl.BlockSpec(memory_space=pl.ANY)], out_specs=HBM_SPEC,
        scratch_shapes=[pltpu.SemaphoreType.DMA((7,)), pltpu.SemaphoreType.DMA((7,)), pltpu.SemaphoreType.DMA],
        name=name)(block, after)


SEM_SPEC = pl.BlockSpec(memory_space=pltpu.SEMAPHORE)
ANY_SPEC = pl.BlockSpec(memory_space=pl.ANY)
DATAFLOW = pltpu.SideEffectType.DATAFLOW_SIDE_EFFECTING


def _peer(k, x, y, c):
    return (1 - x if (k >> 2) & 1 else x, 1 - y if (k >> 1) & 1 else y, 1 - c if k & 1 else c)


def _own_slot_filled(own_block):
    x, y, c = _position()
    zone = lax.empty((N_DEV,) + own_block.shape, own_block.dtype)
    return lax.dynamic_update_slice_in_dim(zone, own_block[None], 4 * x + 2 * y + c, axis=0)


ALL_PEERS = tuple(range(1, N_DEV))
SIBLING_AND_SAME_CORE = (1, 2, 4, 6)
SAME_CORE = (2, 4, 6)


def _split_start(srcs, scatter, after, *, name, relations=ALL_PEERS, carried=None):
    n = len(srcs)
    extra = ([] if after is None else [after]) + ([] if carried is None else [carried])
    n_carried = 0 if carried is None else 1
    x, y, c = _position()
    me = 4 * x + 2 * y + c
    lands = [_own_slot_filled(lax.dynamic_index_in_dim(s, me, 0, keepdims=False) if scatter else s) for s in srcs]

    def body(*refs):
        src_refs, land_refs = refs[:n], refs[n:2 * n]
        send_sems, recv_sems = refs[2 * n + len(extra)], refs[2 * n + len(extra) + 1]
        token = refs[2 * n + len(extra) + 2 + 2 * n]
        bx, by, bc = _position()
        bme = 4 * bx + 2 * by + bc
        for a in range(n):
            for k in relations:
                tx, ty, tc = _peer(k, bx, by, bc)
                src = src_refs[a].at[4 * tx + 2 * ty + tc] if scatter else src_refs[a]
                pltpu.make_async_remote_copy(
                    src_ref=src, dst_ref=land_refs[a].at[bme],
                    send_sem=send_sems.at[7 * a + k - 1], recv_sem=recv_sems.at[7 * a + k - 1],
                    device_id=(tx, ty, tc), device_id_type=MESH_ID).start()
        token[...] = jnp.zeros_like(token)

    hbm = lambda a: pltpu.HBM(a.shape, a.dtype)
    outs = pl.pallas_call(
        body, name=name,
        out_shape=(pltpu.SemaphoreType.DMA((7 * n,)), pltpu.SemaphoreType.DMA((7 * n,)),
                   *[hbm(s) for s in srcs], *[hbm(l) for l in lands], jax.ShapeDtypeStruct((8, 128), F32),
                   *([hbm(carried)] if n_carried else [])),
        in_specs=[HBM_SPEC] * (2 * n) + [ANY_SPEC] * len(extra),
        out_specs=(SEM_SPEC, SEM_SPEC, *[HBM_SPEC] * (2 * n), pl.BlockSpec(memory_space=pltpu.VMEM), *([ANY_SPEC] * n_carried)),
        input_output_aliases={**{i: 2 + i for i in range(2 * n)},
                              **({2 * n + len(extra) - 1: 2 * n + 3} if n_carried else {})},
        compiler_params=pltpu.CompilerParams(has_side_effects=DATAFLOW),
    )(*[pltpu.with_memory_space_constraint(s, pltpu.HBM) for s in srcs],
      *[pltpu.with_memory_space_constraint(l, pltpu.HBM) for l in lands], *extra)
    return {"n": n, "relations": relations, "send": outs[0], "recv": outs[1], "srcs": list(outs[2:2 + n]),
            "lands": list(outs[2 + n:2 + 2 * n]), "token": outs[2 * n + 2], "carried": outs[-1] if n_carried else None}


def _forward_start(lands, carried, *, name):
    n = len(lands)

    def body(*refs):
        land_refs = refs[:n]
        send_sems, recv_sems = refs[n + 1], refs[n + 2]
        bx, by, bc = _position()
        for a in range(n):
            for k in SAME_CORE:
                tx, ty, tc = _peer(k, bx, by, bc)
                block = land_refs[a].at[4 * tx + 2 * ty + tc]
                pltpu.make_async_remote_copy(
                    src_ref=block, dst_ref=block,
                    send_sem=send_sems.at[7 * a + k - 1], recv_sem=recv_sems.at[7 * a + k - 1],
                    device_id=(bx, by, 1 - bc), device_id_type=MESH_ID).start()

    hbm = lambda a: pltpu.HBM(a.shape, a.dtype)
    outs = pl.pallas_call(
        body, name=name,
        out_shape=(pltpu.SemaphoreType.DMA((7 * n,)), pltpu.SemaphoreType.DMA((7 * n,)),
                   *[hbm(l) for l in lands], hbm(carried)),
        in_specs=[HBM_SPEC] * n + [ANY_SPEC],
        out_specs=(SEM_SPEC, SEM_SPEC, *[HBM_SPEC] * n, ANY_SPEC),
        input_output_aliases={i: 2 + i for i in range(n + 1)},
        compiler_params=pltpu.CompilerParams(has_side_effects=DATAFLOW),
    )(*lands, carried)
    handle = {"n": n, "relations": SAME_CORE, "send": outs[0], "recv": outs[1], "srcs": [], "lands": list(outs[2:2 + n])}
    return handle, outs[-1]


def _split_wait(handle, after, *, name):
    n, ns = handle["n"], len(handle["srcs"])

    def body(*refs):
        land_refs = refs[ns:ns + n]
        send_sems, recv_sems = refs[ns + n], refs[ns + n + 1]
        bx, by, bc = _position()
        for a in range(n):
            for k in handle["relations"]:
                block = land_refs[a].at[0]
                cp = pltpu.make_async_remote_copy(
                    src_ref=block, dst_ref=block,
                    send_sem=send_sems.at[7 * a + k - 1], recv_sem=recv_sems.at[7 * a + k - 1],
                    device_id=_peer(k, bx, by, bc), device_id_type=MESH_ID)
                cp.wait_send()
                cp.wait_recv()

    hbm = lambda a: pltpu.HBM(a.shape, a.dtype)
    outs = pl.pallas_call(
        body, name=name,
        out_shape=(*[hbm(s) for s in handle["srcs"]], *[hbm(l) for l in handle["lands"]]),
        in_specs=[HBM_SPEC] * (ns + n) + [SEM_SPEC, SEM_SPEC, ANY_SPEC],
        out_specs=tuple([HBM_SPEC] * (ns + n)),
        input_output_aliases={i: i for i in range(ns + n)},
        compiler_params=pltpu.CompilerParams(has_side_effects=DATAFLOW),
    )(*handle["srcs"], *handle["lands"], handle["send"], handle["recv"], after)
    return list(outs[ns:])


def _sum_sources(parts, *, tr, name):
    n, R, C = parts.shape

    def body(p_ref, o_ref):
        acc = p_ref[0].astype(F32)
        for s in range(1, n):
            acc = acc + p_ref[s].astype(F32)
        o_ref[...] = acc

    return pl.pallas_call(
        body, grid=(R // tr,), in_specs=[pl.BlockSpec((n, tr, C), lambda i: (0, i, 0))],
        out_specs=pl.BlockSpec((tr, C), lambda i: (i, 0)),
        out_shape=jax.ShapeDtypeStruct((R, C), F32), compiler_params=_cp("parallel"), name=name)(parts)


def _adamw_math(g, w, m, v):
    c1 = 1.0 - ADAM_B1 ** ADAM_STEP
    c2 = 1.0 - ADAM_B2 ** ADAM_STEP
    nm = ADAM_B1 * m + (1.0 - ADAM_B1) * g
    nv = ADAM_B2 * v + (1.0 - ADAM_B2) * (g * g)
    return -ADAM_LR * ((nm / c1) / (jnp.sqrt(nv / c2) + ADAM_EPS) + ADAM_WD * w), nm, nv


def _reduce_adamw(received, w, m, v, *, tr, name):
    L, R, C = w.shape

    def body(*refs):
        p_refs = refs[:L]
        w_ref, m_ref, v_ref, g_ref, d_ref, nm_ref, nv_ref = refs[L:]
        for l in range(L):
            @pl.when(pl.program_id(0) == l)
            def _(p_ref=p_refs[l]):
                acc = p_ref[0].astype(F32)
                for s in range(1, N_DEV):
                    acc = acc + p_ref[s].astype(F32)
                g_ref[...] = acc
                d_ref[...], nm_ref[...], nv_ref[...] = _adamw_math(acc, w_ref[...], m_ref[...], v_ref[...])

    p_spec = pl.BlockSpec((N_DEV, tr, C), lambda l, i: (0, i, 0))
    blk = pl.BlockSpec((None, tr, C), lambda l, i: (l, i, 0))
    sh = jax.ShapeDtypeStruct((L, R, C), F32)
    return pl.pallas_call(
        body, grid=(L, R // tr), in_specs=[p_spec] * L + [blk] * 3, out_specs=[blk] * 4, out_shape=[sh] * 4,
        compiler_params=_cp("parallel", "parallel"), name=name)(*received, w, m, v)


def _adamw(g, w, m, v, *, tr, name):
    L, R, C = w.shape

    def body(g_ref, w_ref, m_ref, v_ref, d_ref, nm_ref, nv_ref):
        d_ref[...], nm_ref[...], nv_ref[...] = _adamw_math(g_ref[...], w_ref[...], m_ref[...], v_ref[...])

    blk = pl.BlockSpec((None, tr, C), lambda l, i: (l, i, 0))
    sh = jax.ShapeDtypeStruct((L, R, C), F32)
    return pl.pallas_call(
        body, grid=(L, R // tr), in_specs=[blk] * 4, out_specs=[blk] * 3, out_shape=[sh] * 3,
        compiler_params=_cp("parallel", "parallel"), name=name)(g, w, m, v)


UNITS = {
    "a_w_in": ("a_w_in", 0, True), "w_mem_kv0": ("w_mem_kv", 0, False), "w_out0": ("w_out", 0, False),
    "w_gate_up0": ("w_gate_up", 0, True), "w_down0": ("w_down", 0, False), "w_kv": ("w_kv", None, True),
    "b_w_in": ("b_w_in", 0, True), "w_mem_kv1": ("w_mem_kv", 1, False), "w_out1": ("w_out", 1, False),
    "w_gate_up1": ("w_gate_up", 1, True), "w_down1": ("w_down", 1, False),
}
BIG = ("a_w_in", "b_w_in", "w_kv", "w_mem_kv", "w_out", "w_gate_up", "w_down")
ADAMW_ROW_TILE = {"a_w_in": 208, "b_w_in": 160, "w_kv": 192, "w_mem_kv": 128, "w_out": 128, "w_gate_up": 176, "w_down": 176}


def _wire_block(weights, unit):
    name, layer, col = UNITS[unit]
    a = weights[name] if layer is None else weights[name][layer]
    return (a.T if col else a).astype(BF16)


SMALL_REPLICATED = ("norm_mix", "norm_ffn", "b_qnorm", "kv_norm", "b_knorm", "mem_norm", "mem_qnorm", "mem_knorm")
SMALL_SHARDED = ("a_lb_logits", "a_onorm")
SMALL_ORDER = SMALL_REPLICATED + SMALL_SHARDED
LANES = 128


def _prod(shape):
    n = 1
    for s in shape:
        n *= s
    return n


def _pack_flat(arrays, rows, cols, dtype):
    flat = jnp.concatenate([a.reshape(-1).astype(dtype) for a in arrays])
    return jnp.pad(flat, (0, rows * cols - flat.shape[0])).reshape(rows, cols)


def _unpack_flat(packed, shapes):
    flat = packed.reshape(-1)
    out, off = [], 0
    for s in shapes:
        out.append(flat[off:off + _prod(s)].reshape(s))
        off += _prod(s)
    return out


def kernel(x, mem, norm_mix, norm_ffn, a_w_in, a_lb_logits, a_onorm, b_w_in, b_qnorm, kv_norm, w_kv, b_knorm, mem_norm, w_mem_kv, mem_qnorm, mem_knorm, w_out, w_gate_up, w_down, loss_target, m_norm_mix, m_norm_ffn, m_a_w_in, m_a_lb_logits, m_a_onorm, m_b_w_in, m_b_qnorm, m_kv_norm, m_w_kv, m_b_knorm, m_mem_norm, m_w_mem_kv, m_mem_qnorm, m_mem_knorm, m_w_out, m_w_gate_up, m_w_down, v_norm_mix, v_norm_ffn, v_a_w_in, v_a_lb_logits, v_a_onorm, v_b_w_in, v_b_qnorm, v_kv_norm, v_w_kv, v_b_knorm, v_mem_norm, v_w_mem_kv, v_mem_qnorm, v_mem_knorm, v_w_out, v_w_gate_up, v_w_down):
    names = ("norm_mix", "norm_ffn", "a_w_in", "a_lb_logits", "a_onorm", "b_w_in", "b_qnorm", "kv_norm", "w_kv", "b_knorm",
             "mem_norm", "w_mem_kv", "mem_qnorm", "mem_knorm", "w_out", "w_gate_up", "w_down")
    w = dict(zip(names, (norm_mix, norm_ffn, a_w_in, a_lb_logits, a_onorm, b_w_in, b_qnorm, kv_norm, w_kv, b_knorm,
                         mem_norm, w_mem_kv, mem_qnorm, mem_knorm, w_out, w_gate_up, w_down)))
    m = dict(zip(names, (m_norm_mix, m_norm_ffn, m_a_w_in, m_a_lb_logits, m_a_onorm, m_b_w_in, m_b_qnorm, m_kv_norm, m_w_kv,
                         m_b_knorm, m_mem_norm, m_w_mem_kv, m_mem_qnorm, m_mem_knorm, m_w_out, m_w_gate_up, m_w_down)))
    v = dict(zip(names, (v_norm_mix, v_norm_ffn, v_a_w_in, v_a_lb_logits, v_a_onorm, v_b_w_in, v_b_qnorm, v_kv_norm, v_w_kv,
                         v_b_knorm, v_mem_norm, v_w_mem_kv, v_mem_qnorm, v_mem_knorm, v_w_out, v_w_gate_up, v_w_down)))

    first = ["a_w_in", "w_mem_kv0"]
    gathered = _all_gather([_wire_block(w, u) for u in first] + [_pack_flat([a_lb_logits, a_onorm], 8, LANES, F32)],
                           name="gather_first")
    full = {u: g.reshape(-1, g.shape[-1]) for u, g in zip(first, gathered)}
    small_in = gathered[-1].reshape(N_DEV, -1)
    P = {n: w[n] for n in SMALL_REPLICATED}
    P["a_lb_logits"] = small_in[:, :192].reshape(N_DEV, 2, 96).transpose(1, 0, 2).reshape(2, A_WIDTH)
    P["a_onorm"] = small_in[:, 192:288].reshape(1, A_WIDTH)
    later = [["w_out0", "w_gate_up0"], ["w_down0", "w_kv"], ["b_w_in", "w_mem_kv1"], ["w_out1", "w_gate_up1", "w_down1"]]
    first_half, second_half = {}, {}

    def start_first_half(i, after, carried=None):
        first_half[i] = _split_start([_wire_block(w, u) for u in later[i]], False, after, name=f"gather{i}_start",
                                     relations=SIBLING_AND_SAME_CORE, carried=carried)
        return first_half[i]

    token = start_first_half(0, gathered[-1])["token"]
    token = start_first_half(1, token)["token"]

    def forward_point(i, value):
        landed = _split_wait(first_half[i], value, name=f"gather{i}_landed")
        second_half[i], value = _forward_start(landed, value, name=f"gather{i}_forward")
        if i + 2 < len(later):
            value = start_first_half(i + 2, None, carried=value)["carried"]
        return value

    def get_w(unit, after):
        if unit not in full:
            i = [unit in group for group in later].index(True)
            for u, land in zip(later[i], _split_wait(second_half[i], after, name=f"gather{i}_wait")):
                full[u] = land.reshape(-1, land.shape[-1])
        return full[unit]

    sent = []

    def put_g(group):
        units = list(group)
        handle = _split_start([group[u].reshape(N_DEV, -1, group[u].shape[-1]) for u in units], True, None,
                              name=f"scatter{len(sent)}_start")
        sent.append((units, handle))
        return handle["token"]

    sq, gx, gP = _local_step(x[0], mem[0], loss_target[0], get_w, P, put_g, first_dep=token, forward_point=forward_point)
    loss_here = (0.5 * jnp.sum(sq) / D_MODEL).reshape(1)

    received = {}
    group_of = {u: i for i, (units, _) in enumerate(sent) for u in units}
    out = {"grad": {}, "delta": {}, "new_m": {}, "new_v": {}}
    newest = [gx]

    def update_big(n):
        shape = w[n].shape
        as3 = lambda a: a.reshape((-1,) + shape[-2:])
        mine = [u for u, (wn, _, _) in UNITS.items() if wn == n]
        for i in sorted({group_of[u] for u in mine}):
            if sent[i][0][0] not in received:
                received.update(zip(sent[i][0], _split_wait(sent[i][1], newest[0], name=f"scatter{i}_wait")))
        flip = (lambda a: jnp.swapaxes(a, 1, 2)) if UNITS[mine[0]][2] else (lambda a: a)
        res = _reduce_adamw([received[u] for u in mine], flip(as3(w[n])), flip(as3(m[n])), flip(as3(v[n])),
                            tr=ADAMW_ROW_TILE[n], name=f"adamw_{n}")
        newest[0] = res[1]
        for kind, r in zip(("grad", "delta", "new_m", "new_v"), res):
            out[kind][n] = flip(r).reshape(shape)

    for n in ("w_down", "w_gate_up", "w_out", "w_mem_kv", "b_w_in", "w_kv"):
        update_big(n)

    full_shapes = [(2, A_WIDTH) if n == "a_lb_logits" else (1, A_WIDTH) if n == "a_onorm" else w[n].shape for n in SMALL_ORDER]
    n_small = sum(_prod(s) for s in full_shapes) + 1
    rows_small = -(-n_small // (8 * LANES)) * 8
    g_all = _all_gather_direct(_pack_flat([gP[n] for n in SMALL_ORDER] + [loss_here], rows_small, LANES, F32),
                               newest[0], name="gather_small_grads")
    summed = _unpack_flat(_sum_sources(g_all, tr=rows_small, name="sum_small_grads"), full_shapes + [(1,)])
    g_small = dict(zip(SMALL_ORDER, summed))
    loss = summed[-1].reshape(())
    me = 4 * lax.axis_index("x") + 2 * lax.axis_index("y") + lax.axis_index("c")
    for n in SMALL_SHARDED:
        g_small[n] = lax.dynamic_slice_in_dim(g_small[n], me * 96, 96, axis=1)
    shapes = [w[n].shape for n in SMALL_ORDER]
    rows_upd = -(-sum(_prod(s) for s in shapes) // (8 * LANES)) * 8
    pk = lambda d: _pack_flat([d[n] for n in SMALL_ORDER], rows_upd, LANES, F32)
    res = _adamw(pk(g_small)[None], pk(w)[None], pk(m)[None], pk(v)[None], tr=rows_upd, name="adamw_small")
    out["grad"].update(g_small)
    for kind, packed in zip(("delta", "new_m", "new_v"), res):
        out[kind].update(zip(SMALL_ORDER, _unpack_flat(packed[0], shapes)))
    newest[0] = res[0]
    update_big("a_w_in")

    return (loss, gx[None], *[out["grad"][n] for n in names], *[out["delta"][n] for n in names],
            *[out["new_m"][n] for n in names], *[out["new_v"][n] for n in names])
```

```python
import functools

import jax
import jax.numpy as jnp
from jax import lax
from jax.experimental import pallas as pl
from jax.experimental.pallas import tpu as pltpu

F32 = jnp.float32
BF16 = jnp.bfloat16

N_DEV = 8
D_MODEL = 1024
HEAD_DIM = 128
A_HEADS = 6
A_WIDTH = A_HEADS * HEAD_DIM
CHUNK = 64
B_HEADS = 6
B_WIDTH = B_HEADS * HEAD_DIM
DILATIONS = (1, 4, 16)
SPAN = 128
N_GROUPS = 3
ROPE_THETA = 10000.0
MEM_TOKENS = 256
MEM_HEADS = 4
MEM_HEAD_DIM = 64
MEM_WIDTH = MEM_HEADS * MEM_HEAD_DIM
FFN_HIDDEN = 2816
EPS = 1e-6

ADAM_LR = 0.001
ADAM_B1 = 0.9
ADAM_B2 = 0.999
ADAM_EPS = 1e-08
ADAM_WD = 0.01
ADAM_STEP = 10

V7X_VMEM_LIMIT_BYTES = 56 * 1024 * 1024

NT_DIMS = (((1,), (1,)), ((), ()))
TN_DIMS = (((0,), (0,)), ((), ()))


def _cp(*sem):
    return pltpu.CompilerParams(dimension_semantics=sem, vmem_limit_bytes=V7X_VMEM_LIMIT_BYTES)


def _dot(a, b):
    return jnp.dot(a.astype(BF16), b.astype(BF16), preferred_element_type=F32)


def _dot_nt(a, b):
    return lax.dot_general(a.astype(BF16), b.astype(BF16), NT_DIMS, preferred_element_type=F32)


def _dot_tn(a, b):
    return lax.dot_general(a.astype(BF16), b.astype(BF16), TN_DIMS, preferred_element_type=F32)


def _dot3(m01, x):
    hi = x.astype(BF16)
    r1 = x - hi.astype(F32)
    mid = r1.astype(BF16)
    lo = (r1 - mid.astype(F32)).astype(BF16)
    d = functools.partial(jnp.dot, preferred_element_type=F32)
    return d(m01, hi) + d(m01, mid) + d(m01, lo)


def _sigmoid(x):
    return 1.0 / (1.0 + jnp.exp(-x))


def _full(shape):
    return pl.BlockSpec(shape, lambda *_: (0,) * len(shape))


def _dep(body, n_in, dep):
    if dep is None:
        return body, [], []

    def with_dep(*refs):
        return body(*refs[:n_in], *refs[n_in + 1:])

    return with_dep, [pl.BlockSpec(memory_space=pl.ANY)], [dep]


def _rms_matmul(x, g, w, *, tt, tn, wt, name, out_dtype=F32, dep=None):
    T, K = x.shape
    N = w.shape[0] if wt else w.shape[1]

    def kernel_body(x_ref, g_ref, w_ref, y_ref, xn_ref):
        xf = x_ref[...]
        r = lax.rsqrt(jnp.mean(xf * xf, axis=-1, keepdims=True) + EPS)
        xn = (xf * r * g_ref[...]).astype(BF16)
        xn_ref[...] = xn
        for j in range(N // tn):
            cols = slice(j * tn, (j + 1) * tn)
            y = _dot_nt(xn, w_ref[cols, :]) if wt else _dot(xn, w_ref[:, cols])
            y_ref[:, cols] = y.astype(out_dtype)

    body, dep_specs, dep_args = _dep(kernel_body, 3, dep)
    return pl.pallas_call(
        body, grid=(T // tt,),
        in_specs=[pl.BlockSpec((tt, K), lambda i: (i, 0)), _full((1, K)), _full(w.shape)] + dep_specs,
        out_specs=[pl.BlockSpec((tt, N), lambda i: (i, 0)), pl.BlockSpec((tt, K), lambda i: (i, 0))],
        out_shape=[jax.ShapeDtypeStruct((T, N), out_dtype), jax.ShapeDtypeStruct((T, K), BF16)],
        compiler_params=_cp("parallel"), name=name)(x, g, w, *dep_args)


def _mm_res(res, a1, a2, w, *, tt, name):
    T, K1 = a1.shape
    K2 = a2.shape[1]
    N = w.shape[1]

    def body(r_ref, a1_ref, a2_ref, w_ref, o_ref):
        o_ref[...] = r_ref[...] + _dot(a1_ref[...], w_ref[:K1, :]) + _dot(a2_ref[...], w_ref[K1:, :])

    return pl.pallas_call(
        body, grid=(T // tt,),
        in_specs=[pl.BlockSpec((tt, N), lambda i: (i, 0)), pl.BlockSpec((tt, K1), lambda i: (i, 0)),
                  pl.BlockSpec((tt, K2), lambda i: (i, 0)), _full((K1 + K2, N))],
        out_specs=pl.BlockSpec((tt, N), lambda i: (i, 0)),
        out_shape=jax.ShapeDtypeStruct((T, N), F32),
        compiler_params=_cp("parallel"), name=name)(res, a1, a2, w)


def _swiglu_down(h, gu, wd, *, tt, name):
    T, D = h.shape
    Fh = wd.shape[0]

    def body(h_ref, gt_ref, up_ref, w_ref, o_ref):
        gt = gt_ref[...].astype(F32)
        act = gt * _sigmoid(gt) * up_ref[...].astype(F32)
        o_ref[...] = h_ref[...] + _dot(act, w_ref[...])

    return pl.pallas_call(
        body, grid=(T // tt,),
        in_specs=[pl.BlockSpec((tt, D), lambda i: (i, 0)), pl.BlockSpec((tt, Fh), lambda i: (i, 0)),
                  pl.BlockSpec((tt, Fh), lambda i: (i, 1)), _full((Fh, D))],
        out_specs=pl.BlockSpec((tt, D), lambda i: (i, 0)),
        out_shape=jax.ShapeDtypeStruct((T, D), F32),
        compiler_params=_cp("parallel"), name=name)(h, gu, gu, wd)


def _swiglu_down_loss(h, gu, wd, tgt, *, tt, name):
    T, D = h.shape
    Fh = wd.shape[0]

    def body(h_ref, gt_ref, up_ref, w_ref, t_ref, dy_ref, acc_ref):
        @pl.when(pl.program_id(0) == 0)
        def _():
            acc_ref[...] = jnp.zeros_like(acc_ref)

        gt = gt_ref[...].astype(F32)
        act = gt * _sigmoid(gt) * up_ref[...].astype(F32)
        e = h_ref[...] + _dot(act, w_ref[...]) - t_ref[...]
        dy_ref[...] = e * (1.0 / D)
        acc_ref[...] += jnp.sum(e * e, axis=0, keepdims=True)

    row = pl.BlockSpec((tt, D), lambda i: (i, 0))
    return pl.pallas_call(
        body, grid=(T // tt,),
        in_specs=[row, pl.BlockSpec((tt, Fh), lambda i: (i, 0)), pl.BlockSpec((tt, Fh), lambda i: (i, 1)), _full((Fh, D)), row],
        out_specs=[row, _full((1, D))],
        out_shape=[jax.ShapeDtypeStruct((T, D), F32), jax.ShapeDtypeStruct((1, D), F32)],
        compiler_params=_cp("arbitrary"), name=name)(h, gu, gu, wd, tgt)


def _swiglu_bwd(dh, gu, wd, *, tt, name):
    T, D = dh.shape
    Fh = wd.shape[0]
    last = T // tt - 1

    def body(dh_ref, gt_ref, up_ref, w_ref, dgu_ref, gw_ref, acc):
        @pl.when(pl.program_id(0) == 0)
        def _():
            acc[...] = jnp.zeros_like(acc)

        gt = gt_ref[...].astype(F32)
        up = up_ref[...].astype(F32)
        s = _sigmoid(gt)
        silu = gt * s
        dh16 = dh_ref[...].astype(BF16)
        dact = _dot_nt(dh16, w_ref[...])
        acc[...] += _dot_tn((silu * up).astype(BF16), dh16)
        dgu_ref[:, :Fh] = (dact * up * (s * (1.0 + gt * (1.0 - s)))).astype(BF16)
        dgu_ref[:, Fh:] = (dact * silu).astype(BF16)

        @pl.when(pl.program_id(0) == last)
        def _():
            gw_ref[...] = acc[...].astype(BF16)

    return pl.pallas_call(
        body, grid=(T // tt,),
        in_specs=[pl.BlockSpec((tt, D), lambda i: (i, 0)), pl.BlockSpec((tt, Fh), lambda i: (i, 0)),
                  pl.BlockSpec((tt, Fh), lambda i: (i, 1)), _full((Fh, D))],
        out_specs=[pl.BlockSpec((tt, 2 * Fh), lambda i: (i, 0)), _full((Fh, D))],
        out_shape=[jax.ShapeDtypeStruct((T, 2 * Fh), BF16), jax.ShapeDtypeStruct((Fh, D), BF16)],
        scratch_shapes=[pltpu.VMEM((Fh, D), F32)],
        compiler_params=_cp("arbitrary"), name=name)(dh, gu, gu, wd)


def _out_proj_bwd(dy, a1, a2, w, *, tt, name, head_dots=False):
    T, N = dy.shape
    K1, K2 = a1.shape[1], a2.shape[1]
    K = K1 + K2
    last = T // tt - 1

    def body(dy_ref, a1_ref, a2_ref, w_ref, da_ref, gw_ref, *rest):
        acc = rest[-1]

        @pl.when(pl.program_id(0) == 0)
        def _():
            acc[...] = jnp.zeros_like(acc)

        dy16 = dy_ref[...].astype(BF16)
        da = _dot_nt(dy16, w_ref[...])
        da_ref[...] = da
        acc[:K1, :] += _dot_tn(a1_ref[...], dy16)
        acc[K1:, :] += _dot_tn(a2_ref[...], dy16)
        if head_dots:
            for h in range(K1 // HEAD_DIM):
                sl = slice(h * HEAD_DIM, (h + 1) * HEAD_DIM)
                rest[0][:, sl] = jnp.broadcast_to(jnp.sum(da[:, sl] * a1_ref[:, sl], axis=-1, keepdims=True), (tt, HEAD_DIM))

        @pl.when(pl.program_id(0) == last)
        def _():
            gw_ref[...] = acc[...].astype(BF16)

    extra_specs = [pl.BlockSpec((tt, K1), lambda i: (i, 0))] if head_dots else []
    extra_shapes = [jax.ShapeDtypeStruct((T, K1), F32)] if head_dots else []
    return pl.pallas_call(
        body, grid=(T // tt,),
        in_specs=[pl.BlockSpec((tt, N), lambda i: (i, 0)), pl.BlockSpec((tt, K1), lambda i: (i, 0)),
                  pl.BlockSpec((tt, K2), lambda i: (i, 0)), _full((K, N))],
        out_specs=[pl.BlockSpec((tt, K), lambda i: (i, 0)), _full((K, N))] + extra_specs,
        out_shape=[jax.ShapeDtypeStruct((T, K), F32), jax.ShapeDtypeStruct((K, N), BF16)] + extra_shapes,
        scratch_shapes=[pltpu.VMEM((K, N), F32)],
        compiler_params=_cp("arbitrary"), name=name)(dy, a1, a2, w)


def _mm_tn(a, b, *, tt, tka, name):
    T, Ka = a.shape
    N = b.shape[1]
    last = T // tt - 1

    def body(a_ref, b_ref, o_ref, acc):
        @pl.when(pl.program_id(1) == 0)
        def _():
            acc[...] = jnp.zeros_like(acc)

        acc[...] += _dot_tn(a_ref[...], b_ref[...])

        @pl.when(pl.program_id(1) == last)
        def _():
            o_ref[...] = acc[...].astype(BF16)

    return pl.pallas_call(
        body, grid=(Ka // tka, T // tt),
        in_specs=[pl.BlockSpec((tt, tka), lambda j, t: (t, j)), pl.BlockSpec((tt, N), lambda j, t: (t, 0))],
        out_specs=pl.BlockSpec((tka, N), lambda j, t: (j, 0)),
        out_shape=jax.ShapeDtypeStruct((Ka, N), BF16),
        scratch_shapes=[pltpu.VMEM((tka, N), F32)],
        compiler_params=_cp("parallel", "arbitrary"), name=name)(a, b)


def _mm_tn_pieces(pieces, b, *, tt, name):
    n = len(pieces)
    T = b.shape[0]
    N = b.shape[1]
    widths = [p.shape[1] for p in pieces]
    Ka = sum(widths)
    last = T // tt - 1

    def body(*refs):
        p_refs = refs[:n]
        b_ref, o_ref, acc = refs[n:]

        @pl.when(pl.program_id(0) == 0)
        def _():
            acc[...] = jnp.zeros_like(acc)

        bv = b_ref[...].astype(BF16)
        off = 0
        for p_ref, wd in zip(p_refs, widths):
            acc[off:off + wd, :] += _dot_tn(p_ref[...], bv)
            off += wd

        @pl.when(pl.program_id(0) == last)
        def _():
            o_ref[...] = acc[...].astype(BF16)

    return pl.pallas_call(
        body, grid=(T // tt,),
        in_specs=[pl.BlockSpec((tt, wd), lambda t: (t, 0)) for wd in widths] + [pl.BlockSpec((tt, N), lambda t: (t, 0))],
        out_specs=_full((Ka, N)), out_shape=jax.ShapeDtypeStruct((Ka, N), BF16),
        scratch_shapes=[pltpu.VMEM((Ka, N), F32)],
        compiler_params=_cp("arbitrary"), name=name)(*pieces, b)


def _rms_bwd_dx(x, g, w, dy, dres, *, tt, wt, name, dep=None):
    pieces = list(dy) if isinstance(dy, (list, tuple)) else [dy]
    n = len(pieces)
    widths = [p.shape[1] for p in pieces]
    T, K = x.shape

    def kernel_body(x_ref, g_ref, w_ref, *rest):
        dy_refs = rest[:n]
        dres_ref, dx_ref, dg_ref = rest[n:]

        @pl.when(pl.program_id(0) == 0)
        def _():
            dg_ref[...] = jnp.zeros_like(dg_ref)

        if n == 1:
            dxn = (_dot if wt else _dot_nt)(dy_refs[0][...], w_ref[...])
        else:
            dxn, off = 0.0, 0
            for dy_ref, wd in zip(dy_refs, widths):
                dxn = dxn + _dot(dy_ref[...], w_ref[off:off + wd, :])
                off += wd
        xf = x_ref[...]
        r = lax.rsqrt(jnp.mean(xf * xf, axis=-1, keepdims=True) + EPS)
        xhat = xf * r
        dg_ref[...] += jnp.sum(dxn * xhat, axis=0, keepdims=True)
        dxhat = dxn * g_ref[...]
        dx_ref[...] = dres_ref[...] + r * (dxhat - xhat * jnp.mean(dxhat * xhat, axis=-1, keepdims=True))

    assert n == 1 or wt
    body, dep_specs, dep_args = _dep(kernel_body, 4 + n, dep)
    return pl.pallas_call(
        body, grid=(T // tt,),
        in_specs=[pl.BlockSpec((tt, K), lambda i: (i, 0)), _full((1, K)), _full(w.shape)]
        + [pl.BlockSpec((tt, wd), lambda i: (i, 0)) for wd in widths]
        + [pl.BlockSpec((tt, K), lambda i: (i, 0))] + dep_specs,
        out_specs=[pl.BlockSpec((tt, K), lambda i: (i, 0)), _full((1, K))],
        out_shape=[jax.ShapeDtypeStruct((T, K), F32), jax.ShapeDtypeStruct((1, K), F32)],
        compiler_params=_cp("arbitrary"), name=name)(x, g, w, *pieces, dres, *dep_args)


HGRN_TB = 512
HGRN_NCH = HGRN_TB // CHUNK
HGRN_HPB = 6


def _hgrn_chunk_fwd(q, z, lbv, tril01):
    sig = _sigmoid(z)
    f = lbv + (1.0 - lbv) * sig
    kk = 1.0 - f
    b = _dot3(tril01, jnp.log(f))
    bend = b[CHUNK - 1:CHUNK, :]
    sq = _sigmoid(q)
    eb = jnp.exp(b)
    emb = jnp.exp(-b)
    eo = jnp.exp(bend - b)
    dec = jnp.exp(bend)
    return sig, f, kk, sq, eb, emb, eo, dec


def _hgrn2_fwd(proj, lb, *, name):
    T = proj.shape[0]
    nT = T // HGRN_TB
    nC = T // CHUNK

    def body(q_ref, z_ref, v_ref, lb_ref, o_ref, st_ref, state):
        @pl.when(pl.program_id(1) == 0)
        def _():
            state[...] = jnp.zeros_like(state)

        row = lax.broadcasted_iota(jnp.int32, (CHUNK, CHUNK), 0)
        col = lax.broadcasted_iota(jnp.int32, (CHUNK, CHUNK), 1)
        causal = row >= col
        tril01 = causal.astype(BF16)

        def chunk(c, carry):
            rows = pl.ds(pl.multiple_of(c * CHUNK, CHUNK), CHUNK)
            for hh in range(HGRN_HPB):
                sl = slice(hh * HEAD_DIM, (hh + 1) * HEAD_DIM)
                q = q_ref[rows, sl]
                v = v_ref[rows, sl].astype(BF16)
                sig, f, kk, sq, eb, emb, eo, dec = _hgrn_chunk_fwd(q, z_ref[rows, sl], lb_ref[:, sl], tril01)
                qi = (q * sq * eb).astype(BF16)
                ki = (kk * emb).astype(BF16)
                ko = (kk * eo).astype(BF16)
                st = state[hh]
                att = jnp.where(causal, _dot_nt(qi, ki), 0.0)
                o_ref[rows, sl] = _dot(att, v) + _dot_nt(qi, st)
                st_ref[c, hh] = st
                state[hh] = st * dec + _dot_tn(v, ko)
            return carry

        lax.fori_loop(0, HGRN_NCH, chunk, 0)

    W = HGRN_HPB * HEAD_DIM
    nG = A_HEADS // HGRN_HPB
    hb = lambda off: pl.BlockSpec((HGRN_TB, W), lambda h, i: (i, off + h))
    return pl.pallas_call(
        body, grid=(nG, nT),
        in_specs=[hb(0), hb(nG), hb(2 * nG), pl.BlockSpec((1, W), lambda h, i: (0, h))],
        out_specs=[hb(0), pl.BlockSpec((HGRN_NCH, HGRN_HPB, HEAD_DIM, HEAD_DIM), lambda h, i: (i, h, 0, 0))],
        out_shape=[jax.ShapeDtypeStruct((T, A_WIDTH), F32), jax.ShapeDtypeStruct((nC, A_HEADS, HEAD_DIM, HEAD_DIM), F32)],
        scratch_shapes=[pltpu.VMEM((HGRN_HPB, HEAD_DIM, HEAD_DIM), F32)],
        compiler_params=_cp("parallel", "arbitrary"), name=name)(proj, proj, proj, lb)


def _hgrn2_bwd(proj, lb, st_all, do, *, name):
    T = proj.shape[0]
    nT = T // HGRN_TB

    def body(q_ref, z_ref, v_ref, lb_ref, st_ref, do_ref, dq_ref, dz_ref, dv_ref, dlb_ref, dstate):
        @pl.when(pl.program_id(1) == 0)
        def _():
            dstate[...] = jnp.zeros_like(dstate)
            dlb_ref[...] = jnp.zeros_like(dlb_ref)

        row = lax.broadcasted_iota(jnp.int32, (CHUNK, CHUNK), 0)
        col = lax.broadcasted_iota(jnp.int32, (CHUNK, CHUNK), 1)
        causal = row >= col
        tril01 = causal.astype(BF16)
        triu01 = (row <= col).astype(BF16)

        def chunk(cc, carry):
            c = HGRN_NCH - 1 - cc
            rows = pl.ds(pl.multiple_of(c * CHUNK, CHUNK), CHUNK)
            for hh in range(HGRN_HPB):
                sl = slice(hh * HEAD_DIM, (hh + 1) * HEAD_DIM)
                lbv = lb_ref[:, sl]
                q = q_ref[rows, sl]
                v = v_ref[rows, sl].astype(BF16)
                sig, f, kk, sq, eb, emb, eo, dec = _hgrn_chunk_fwd(q, z_ref[rows, sl], lbv, tril01)
                qi32 = q * sq * eb
                ki32 = kk * emb
                ko32 = kk * eo
                qi, ki, ko = qi32.astype(BF16), ki32.astype(BF16), ko32.astype(BF16)
                att = jnp.where(causal, _dot_nt(qi, ki), 0.0).astype(BF16)
                dout = do_ref[rows, sl].astype(BF16)
                st = st_ref[c, hh]
                dst = dstate[hh]
                dst16 = dst.astype(BF16)
                datt = jnp.where(causal, _dot_nt(dout, v), 0.0).astype(BF16)
                dqi = _dot(datt, ki) + _dot(dout, st)
                dki = _dot_tn(datt, qi)
                dv_ref[rows, sl] = (_dot_tn(att, dout) + _dot_nt(ko, dst16)).astype(BF16)
                dko = _dot(v, dst16)
                ddec = jnp.sum(dst * st, axis=0, keepdims=True)
                dstate[hh] = dst * dec + _dot_tn(dout, qi)
                dkk = dki * emb + dko * eo
                db = dqi * qi32 - dki * ki32 - dko * ko32
                dbend = jnp.sum(dko * ko32, axis=0, keepdims=True) + ddec * dec
                dlogf = _dot3(triu01, db) + dbend
                df = dlogf / f - dkk
                dz_ref[rows, sl] = (df * (1.0 - lbv) * sig * (1.0 - sig)).astype(BF16)
                dlb_ref[:, sl] += jnp.sum(df * (1.0 - sig), axis=0, keepdims=True)
                dq_ref[rows, sl] = (dqi * eb * (sq * (1.0 + q * (1.0 - sq)))).astype(BF16)
            return carry

        lax.fori_loop(0, HGRN_NCH, chunk, 0)

    W = HGRN_HPB * HEAD_DIM
    nG = A_HEADS // HGRN_HPB
    hb = lambda off: pl.BlockSpec((HGRN_TB, W), lambda h, i: (nT - 1 - i, off + h))
    hlb = pl.BlockSpec((1, W), lambda h, i: (0, h))
    o16 = jax.ShapeDtypeStruct((T, A_WIDTH), BF16)
    return pl.pallas_call(
        body, grid=(nG, nT),
        in_specs=[hb(0), hb(nG), hb(2 * nG), hlb,
                  pl.BlockSpec((HGRN_NCH, HGRN_HPB, HEAD_DIM, HEAD_DIM), lambda h, i: (nT - 1 - i, h, 0, 0)), hb(0)],
        out_specs=[hb(0), hb(0), hb(0), hlb],
        out_shape=[o16, o16, o16, jax.ShapeDtypeStruct((1, A_WIDTH), F32)],
        scratch_shapes=[pltpu.VMEM((HGRN_HPB, HEAD_DIM, HEAD_DIM), F32)],
        compiler_params=_cp("parallel", "arbitrary"), name=name)(proj, proj, proj, lb, st_all, do)


def _head_rms(x):
    r = lax.rsqrt(jnp.mean(x * x, axis=-1, keepdims=True) + EPS)
    return x * r, r


def _head_rms_bwd(dxhat, xhat, r):
    return r * (dxhat - xhat * jnp.mean(dxhat * xhat, axis=-1, keepdims=True))


def _a_post_fwd(o, proj, onorm, *, tt, name):
    T = o.shape[0]

    def body(o_ref, g_ref, w_ref, y_ref):
        for h in range(A_HEADS):
            sl = slice(h * HEAD_DIM, (h + 1) * HEAD_DIM)
            xhat, _ = _head_rms(o_ref[:, sl])
            g = g_ref[:, sl]
            y_ref[:, sl] = xhat * w_ref[:, sl] * (g * _sigmoid(g))

    blk = lambda c: pl.BlockSpec((tt, A_WIDTH), lambda i: (i, c))
    return pl.pallas_call(
        body, grid=(T // tt,), in_specs=[blk(0), blk(3), _full((1, A_WIDTH))], out_specs=blk(0),
        out_shape=jax.ShapeDtypeStruct((T, A_WIDTH), F32),
        compiler_params=_cp("parallel"), name=name)(o, proj, onorm)


def _a_post_bwd(o, proj, onorm, dmix, *, tt, name, dep=None):
    T = o.shape[0]

    def kernel_body(o_ref, g_ref, w_ref, dy_ref, do_ref, dg_ref, dw_ref):
        @pl.when(pl.program_id(0) == 0)
        def _():
            dw_ref[...] = jnp.zeros_like(dw_ref)

        for h in range(A_HEADS):
            sl = slice(h * HEAD_DIM, (h + 1) * HEAD_DIM)
            xhat, r = _head_rms(o_ref[:, sl])
            g = g_ref[:, sl]
            s = _sigmoid(g)
            dy = dy_ref[:, sl]
            w = w_ref[:, sl]
            dg_ref[:, sl] = (dy * xhat * w * (s * (1.0 + g * (1.0 - s)))).astype(BF16)
            dyn = dy * (g * s)
            dw_ref[:, sl] += jnp.sum(dyn * xhat, axis=0, keepdims=True)
            do_ref[:, sl] = _head_rms_bwd(dyn * w, xhat, r)

    blk = lambda c: pl.BlockSpec((tt, A_WIDTH), lambda i: (i, c))
    body, dep_specs, dep_args = _dep(kernel_body, 4, dep)
    return pl.pallas_call(
        body, grid=(T // tt,), in_specs=[blk(0), blk(3), _full((1, A_WIDTH)), blk(0)] + dep_specs,
        out_specs=[blk(0), blk(0), _full((1, A_WIDTH))],
        out_shape=[jax.ShapeDtypeStruct((T, A_WIDTH), F32), jax.ShapeDtypeStruct((T, A_WIDTH), BF16),
                   jax.ShapeDtypeStruct((1, A_WIDTH), F32)],
        compiler_params=_cp("arbitrary"), name=name)(o, proj, onorm, dmix, *dep_args)


def _mem_head_masks(n):
    lane = lax.broadcasted_iota(jnp.int32, (n, MEM_WIDTH), 1)
    return [(lane >= m * MEM_HEAD_DIM) & (lane < (m + 1) * MEM_HEAD_DIM) for m in range(MEM_HEADS)]


def _mem_head_rms(x, masks):
    x2 = x * x
    r = jnp.zeros_like(x)
    for mk in masks:
        ms = jnp.sum(jnp.where(mk, x2, 0.0), axis=-1, keepdims=True) * (1.0 / MEM_HEAD_DIM)
        r = jnp.where(mk, lax.rsqrt(ms + EPS), r)
    return x * r, r


def _mem_head_rms_bwd(dxhat, xhat, r, masks):
    t = dxhat * xhat
    m = jnp.zeros_like(t)
    for mk in masks:
        m = jnp.where(mk, jnp.sum(jnp.where(mk, t, 0.0), axis=-1, keepdims=True) * (1.0 / MEM_HEAD_DIM), m)
    return r * (dxhat - xhat * m)


MEM_SCALE = MEM_HEAD_DIM ** -0.5


def _mem_attn_fwd(proj, qcol, mkv, qn_w, kn_w, *, tt, name):
    T = proj.shape[0]

    def body(q_ref, k_ref, v_ref, qw_ref, kw_ref, o_ref):
        qmasks = _mem_head_masks(tt)
        kmasks = _mem_head_masks(MEM_TOKENS)
        qhat, _ = _mem_head_rms(q_ref[...], qmasks)
        qn = qhat * qw_ref[...]
        khat, _ = _mem_head_rms(k_ref[...], kmasks)
        kn = (khat * kw_ref[...]).astype(BF16)
        v = v_ref[...].astype(BF16)
        out = jnp.zeros((tt, MEM_WIDTH), F32)
        for m in range(MEM_HEADS):
            s = _dot_nt(jnp.where(qmasks[m], qn, 0.0), kn) * MEM_SCALE
            s = s - jnp.max(s, axis=-1, keepdims=True)
            p = jnp.exp(s)
            p = p / jnp.sum(p, axis=-1, keepdims=True)
            out = jnp.where(qmasks[m], _dot(p, v), out)
        o_ref[...] = out

    return pl.pallas_call(
        body, grid=(T // tt,),
        in_specs=[pl.BlockSpec((tt, MEM_WIDTH), lambda i: (i, qcol)), pl.BlockSpec((MEM_TOKENS, MEM_WIDTH), lambda i: (0, 0)),
                  pl.BlockSpec((MEM_TOKENS, MEM_WIDTH), lambda i: (0, 1)), _full((1, MEM_WIDTH)), _full((1, MEM_WIDTH))],
        out_specs=pl.BlockSpec((tt, MEM_WIDTH), lambda i: (i, 0)),
        out_shape=jax.ShapeDtypeStruct((T, MEM_WIDTH), F32),
        compiler_params=_cp("parallel"), name=name)(proj, mkv, mkv, qn_w, kn_w)


def _mem_attn_bwd(proj, qcol, mkv, qn_w, kn_w, dmix, *, tt, name):
    T = proj.shape[0]
    nsteps = T // tt
    ocol = (dmix.shape[1] - MEM_WIDTH) // MEM_WIDTH

    def body(q_ref, k_ref, v_ref, qw_ref, kw_ref, do_ref, dq_ref, dkv_ref, dqw_ref, dkw_ref, dk_acc, dv_acc):
        step = pl.program_id(0)

        @pl.when(step == 0)
        def _():
            dk_acc[...] = jnp.zeros_like(dk_acc)
            dv_acc[...] = jnp.zeros_like(dv_acc)
            dqw_ref[...] = jnp.zeros_like(dqw_ref)

        qmasks = _mem_head_masks(tt)
        kmasks = _mem_head_masks(MEM_TOKENS)
        qhat, qr = _mem_head_rms(q_ref[...], qmasks)
        qn = qhat * qw_ref[...]
        khat, kr = _mem_head_rms(k_ref[...], kmasks)
        kn = (khat * kw_ref[...]).astype(BF16)
        v = v_ref[...].astype(BF16)
        dout = do_ref[...]
        dqn = jnp.zeros((tt, MEM_WIDTH), F32)
        dkn = jnp.zeros((MEM_TOKENS, MEM_WIDTH), F32)
        dvv = jnp.zeros((MEM_TOKENS, MEM_WIDTH), F32)
        for m in range(MEM_HEADS):
            qm = jnp.where(qmasks[m], qn, 0.0).astype(BF16)
            s = _dot_nt(qm, kn) * MEM_SCALE
            s = s - jnp.max(s, axis=-1, keepdims=True)
            p = jnp.exp(s)
            p = p / jnp.sum(p, axis=-1, keepdims=True)
            dom = jnp.where(qmasks[m], dout, 0.0).astype(BF16)
            dp = _dot_nt(dom, v)
            ds = (p * (dp - jnp.sum(p * dp, axis=-1, keepdims=True)) * MEM_SCALE).astype(BF16)
            dqn = jnp.where(qmasks[m], _dot(ds, kn), dqn)
            dkn = jnp.where(kmasks[m], _dot_tn(ds, qm), dkn)
            dvv = jnp.where(kmasks[m], _dot_tn(p, dom), dvv)
        dqw_ref[...] += jnp.sum(dqn * qhat, axis=0, keepdims=True)
        dq_ref[...] = _mem_head_rms_bwd(dqn * qw_ref[...], qhat, qr, qmasks).astype(BF16)
        dk_acc[...] += dkn
        dv_acc[...] += dvv

        @pl.when(step == nsteps - 1)
        def _():
            dk = dk_acc[...]
            dkw_ref[...] = jnp.sum(dk * khat, axis=0, keepdims=True)
            dkv_ref[:, :MEM_WIDTH] = _mem_head_rms_bwd(dk * kw_ref[...], khat, kr, kmasks)
            dkv_ref[:, MEM_WIDTH:] = dv_acc[...]

    return pl.pallas_call(
        body, grid=(nsteps,),
        in_specs=[pl.BlockSpec((tt, MEM_WIDTH), lambda i: (i, qcol)), pl.BlockSpec((MEM_TOKENS, MEM_WIDTH), lambda i: (0, 0)),
                  pl.BlockSpec((MEM_TOKENS, MEM_WIDTH), lambda i: (0, 1)), _full((1, MEM_WIDTH)), _full((1, MEM_WIDTH)),
                  pl.BlockSpec((tt, MEM_WIDTH), lambda i: (i, ocol))],
        out_specs=[pl.BlockSpec((tt, MEM_WIDTH), lambda i: (i, 0)), _full((MEM_TOKENS, 2 * MEM_WIDTH)),
                   _full((1, MEM_WIDTH)), _full((1, MEM_WIDTH))],
        out_shape=[jax.ShapeDtypeStruct((T, MEM_WIDTH), BF16), jax.ShapeDtypeStruct((MEM_TOKENS, 2 * MEM_WIDTH), F32),
                   jax.ShapeDtypeStruct((1, MEM_WIDTH), F32), jax.ShapeDtypeStruct((1, MEM_WIDTH), F32)],
        scratch_shapes=[pltpu.VMEM((MEM_TOKENS, MEM_WIDTH), F32), pltpu.VMEM((MEM_TOKENS, MEM_WIDTH), F32)],
        compiler_params=_cp("arbitrary"), name=name)(proj, mkv, mkv, qn_w, kn_w, dmix)


HALF = HEAD_DIM // 2
ATT_SCALE = HEAD_DIM ** -0.5
NEG = -1e30


def _rope_tables(T):
    inv = ROPE_THETA ** (-jnp.arange(HALF, dtype=F32) / HALF)
    ang = jnp.arange(T, dtype=F32)[:, None] * inv[None, :]
    cos, sin = jnp.cos(ang), jnp.sin(ang)
    return jnp.concatenate([cos, cos], axis=-1), jnp.concatenate([-sin, sin], axis=-1)


def _rope(x, cosf, sinsg):
    return x * cosf + pltpu.roll(x, HALF, 1) * sinsg


def _rope_bwd(dy, cosf, sinsg):
    return dy * cosf + pltpu.roll(dy * sinsg, HALF, 1)


def _headnorm_rope_fwd(x, w_heads, cosf, sinsg, *, col0, n_heads, tt, name):
    T = x.shape[0]
    W = n_heads * HEAD_DIM

    def body(x_ref, w_ref, c_ref, s_ref, y_ref):
        c, s = c_ref[...], s_ref[...]
        for h in range(n_heads):
            sl = slice(h * HEAD_DIM, (h + 1) * HEAD_DIM)
            xhat, _ = _head_rms(x_ref[:, sl])
            y_ref[:, sl] = _rope(xhat * w_ref[:, sl], c, s)

    tbl = pl.BlockSpec((tt, HEAD_DIM), lambda i: (i, 0))
    return pl.pallas_call(
        body, grid=(T // tt,),
        in_specs=[pl.BlockSpec((tt, W), lambda i: (i, col0)), _full((1, W)), tbl, tbl],
        out_specs=pl.BlockSpec((tt, W), lambda i: (i, 0)),
        out_shape=jax.ShapeDtypeStruct((T, W), F32),
        compiler_params=_cp("parallel"), name=name)(x, w_heads, cosf, sinsg)


def _q_prep_bwd(proj, w_heads, cosf, sinsg, dqs, *, tt, name):
    T = proj.shape[0]
    W = N_GROUPS * B_WIDTH

    def body(x_ref, w_ref, c_ref, s_ref, d0, d1, d2, dx_ref, dw_ref):
        @pl.when(pl.program_id(0) == 0)
        def _():
            dw_ref[...] = jnp.zeros_like(dw_ref)

        c, s = c_ref[...], s_ref[...]
        for gi, d_ref in enumerate((d0, d1, d2)):
            for h in range(B_HEADS):
                sl = slice((gi * B_HEADS + h) * HEAD_DIM, (gi * B_HEADS + h + 1) * HEAD_DIM)
                xhat, r = _head_rms(x_ref[:, sl])
                dyn = _rope_bwd(d_ref[:, h * HEAD_DIM:(h + 1) * HEAD_DIM], c, s)
                dw_ref[:, sl] += jnp.sum(dyn * xhat, axis=0, keepdims=True)
                dx_ref[:, sl] = _head_rms_bwd(dyn * w_ref[:, sl], xhat, r).astype(BF16)

    tbl = pl.BlockSpec((tt, HEAD_DIM), lambda i: (i, 0))
    dyb = pl.BlockSpec((tt, B_WIDTH), lambda i: (i, 0))
    return pl.pallas_call(
        body, grid=(T // tt,),
        in_specs=[pl.BlockSpec((tt, W), lambda i: (i, 0)), _full((1, W)), tbl, tbl, dyb, dyb, dyb],
        out_specs=[pl.BlockSpec((tt, W), lambda i: (i, 0)), _full((1, W))],
        out_shape=[jax.ShapeDtypeStruct((T, W), BF16), jax.ShapeDtypeStruct((1, W), F32)],
        compiler_params=_cp("arbitrary"), name=name)(proj, w_heads, cosf, sinsg, *dqs)


def _kv_prep_bwd(kv, w_heads, cosf, sinsg, dks, dvs, *, tt, name):
    T = kv.shape[0]

    def body(x_ref, w_ref, c_ref, s_ref, k0, k1, k2, v0, v1, v2, dx_ref, dw_ref):
        @pl.when(pl.program_id(0) == 0)
        def _():
            dw_ref[...] = jnp.zeros_like(dw_ref)

        c, s = c_ref[...], s_ref[...]
        for h in range(B_HEADS):
            sl = slice(h * HEAD_DIM, (h + 1) * HEAD_DIM)
            vs = slice(B_WIDTH + h * HEAD_DIM, B_WIDTH + (h + 1) * HEAD_DIM)
            xhat, r = _head_rms(x_ref[:, sl])
            dyn = _rope_bwd(k0[:, sl] + k1[:, sl] + k2[:, sl], c, s)
            dw_ref[:, sl] += jnp.sum(dyn * xhat, axis=0, keepdims=True)
            dx_ref[:, sl] = _head_rms_bwd(dyn * w_ref[:, sl], xhat, r).astype(BF16)
            dx_ref[:, vs] = (v0[:, sl] + v1[:, sl] + v2[:, sl]).astype(BF16)

    tbl = pl.BlockSpec((tt, HEAD_DIM), lambda i: (i, 0))
    dyb = pl.BlockSpec((tt, B_WIDTH), lambda i: (i, 0))
    return pl.pallas_call(
        body, grid=(T // tt,),
        in_specs=[dyb, _full((1, B_WIDTH)), tbl, tbl] + [dyb] * 6,
        out_specs=[pl.BlockSpec((tt, 2 * B_WIDTH), lambda i: (i, 0)), _full((1, B_WIDTH))],
        out_shape=[jax.ShapeDtypeStruct((T, 2 * B_WIDTH), BF16), jax.ShapeDtypeStruct((1, B_WIDTH), F32)],
        compiler_params=_cp("arbitrary"), name=name)(kv, w_heads, cosf, sinsg, *dks, *dvs)


def _band_masks(n_is_first=None):
    row = lax.broadcasted_iota(jnp.int32, (SPAN, SPAN), 0)
    col = lax.broadcasted_iota(jnp.int32, (SPAN, SPAN), 1)
    return row >= col, col >= row


def _dil_views(T, d):
    L = T // d
    return L, L // SPAN


def _dil_fwd(qr, kr, kv, gi, d, *, name):
    T = qr.shape[0]
    L, nb = _dil_views(T, d)

    def body(q_ref, kc_ref, kp_ref, vc_ref, vp_ref, o_ref, lse_ref):
        cur_ok, prev_band = _band_masks()
        prev_ok = prev_band & (pl.program_id(1) > 0)
        for h in range(B_HEADS):
            sl = slice(h * HEAD_DIM, (h + 1) * HEAD_DIM)
            q = q_ref[:, sl]
            sc = jnp.where(cur_ok, _dot_nt(q, kc_ref[:, sl]) * ATT_SCALE, NEG)
            sp = jnp.where(prev_ok, _dot_nt(q, kp_ref[:, sl]) * ATT_SCALE, NEG)
            m = jnp.maximum(jnp.max(sc, axis=-1, keepdims=True), jnp.max(sp, axis=-1, keepdims=True))
            pc = jnp.exp(sc - m)
            pp = jnp.exp(sp - m)
            l = jnp.sum(pc, axis=-1, keepdims=True) + jnp.sum(pp, axis=-1, keepdims=True)
            o_ref[:, sl] = (_dot(pc, vc_ref[:, sl]) + _dot(pp, vp_ref[:, sl])) / l
            lse_ref[:, sl] = jnp.broadcast_to(m + jnp.log(l), (SPAN, HEAD_DIM))

    blk = lambda f: pl.BlockSpec((SPAN, B_WIDTH), f)
    cur = lambda r, n: (n, r)
    prev = lambda r, n: (jnp.maximum(n - 1, 0), r)
    ov = jax.ShapeDtypeStruct((L, d * B_WIDTH), F32)
    o, lse = pl.pallas_call(
        body, grid=(d, nb),
        in_specs=[blk(lambda r, n: (n, r * N_GROUPS + gi)), blk(cur), blk(prev),
                  blk(lambda r, n: (n, 2 * r + 1)), blk(lambda r, n: (jnp.maximum(n - 1, 0), 2 * r + 1))],
        out_specs=[blk(cur), blk(cur)], out_shape=[ov, ov],
        compiler_params=_cp("parallel", "arbitrary"), name=name,
    )(qr.reshape(L, d * N_GROUPS * B_WIDTH), kr.reshape(L, d * B_WIDTH), kr.reshape(L, d * B_WIDTH),
      kv.reshape(L, d * 2 * B_WIDTH), kv.reshape(L, d * 2 * B_WIDTH))
    return o.reshape(T, B_WIDTH), lse.reshape(T, B_WIDTH)


def _dil_combine_fwd(os_, lses, *, tt, name):
    T = os_[0].shape[0]

    def body(o0, o1, o2, l0, l1, l2, y_ref, lse_ref):
        a, b, c = l0[...], l1[...], l2[...]
        m = jnp.maximum(jnp.maximum(a, b), c)
        wa, wb, wc = jnp.exp(a - m), jnp.exp(b - m), jnp.exp(c - m)
        den = wa + wb + wc
        y_ref[...] = (wa * o0[...] + wb * o1[...] + wc * o2[...]) / den
        lse_ref[...] = m + jnp.log(den)

    blk = pl.BlockSpec((tt, B_WIDTH), lambda i: (i, 0))
    sh = jax.ShapeDtypeStruct((T, B_WIDTH), F32)
    return pl.pallas_call(
        body, grid=(T // tt,), in_specs=[blk] * 6, out_specs=[blk, blk], out_shape=[sh, sh],
        compiler_params=_cp("parallel"), name=name)(*os_, *lses)


DILS_UNROLL = 4


def _dils_specs(gi, d, nblk):
    blk = lambda f: pl.BlockSpec((SPAN * d, HEAD_DIM), f)
    return {
        "q": blk(lambda h, n: (n, gi * B_HEADS + h)), "q_next": blk(lambda h, n: (jnp.minimum(n + 1, nblk - 1), gi * B_HEADS + h)),
        "cur": blk(lambda h, n: (n, h)), "prev": blk(lambda h, n: (jnp.maximum(n - 1, 0), h)),
        "next": blk(lambda h, n: (jnp.minimum(n + 1, nblk - 1), h)),
        "v": blk(lambda h, n: (n, B_HEADS + h)), "v_prev": blk(lambda h, n: (jnp.maximum(n - 1, 0), B_HEADS + h)),
    }


def _dils_fwd(qr, kr, kv, gi, d, *, name):
    T = qr.shape[0]
    nblk = T // (SPAN * d)
    sp = _dils_specs(gi, d, nblk)

    def body(q_ref, kc_ref, kp_ref, vc_ref, vp_ref, o_ref, lse_ref):
        cur_ok, prev_band = _band_masks()
        prev_ok = prev_band & (pl.program_id(1) > 0)

        def residue(r, carry):
            rows = pl.ds(r, SPAN, stride=d)
            q = q_ref[rows, :]
            sc = jnp.where(cur_ok, _dot_nt(q, kc_ref[rows, :]) * ATT_SCALE, NEG)
            sp_ = jnp.where(prev_ok, _dot_nt(q, kp_ref[rows, :]) * ATT_SCALE, NEG)
            m = jnp.maximum(jnp.max(sc, axis=-1, keepdims=True), jnp.max(sp_, axis=-1, keepdims=True))
            pc = jnp.exp(sc - m)
            pp = jnp.exp(sp_ - m)
            l = jnp.sum(pc, axis=-1, keepdims=True) + jnp.sum(pp, axis=-1, keepdims=True)
            o_ref[rows, :] = (_dot(pc, vc_ref[rows, :]) + _dot(pp, vp_ref[rows, :])) / l
            lse_ref[rows, :] = jnp.broadcast_to(m + jnp.log(l), (SPAN, HEAD_DIM))
            return carry

        lax.fori_loop(0, d, residue, 0, unroll=DILS_UNROLL)

    sh = jax.ShapeDtypeStruct((T, B_WIDTH), F32)
    return pl.pallas_call(
        body, grid=(B_HEADS, nblk), in_specs=[sp["q"], sp["cur"], sp["prev"], sp["v"], sp["v_prev"]],
        out_specs=[sp["cur"], sp["cur"]], out_shape=[sh, sh],
        compiler_params=_cp("parallel", "arbitrary"), name=name)(qr, kr, kr, kv, kv)


DIL_BWD_GROUP = {1: 4, 4: 1, 16: 1}


def _dil_bwd(qr, kr, kv, dmix, lse, dd, gi, d, *, name, dep=None):
    T = qr.shape[0]
    G = DIL_BWD_GROUP[d]
    band = SPAN * d
    tb = G * band
    nblk = T // tb

    def kernel_body(q_ref, dy_ref, lse_ref, dd_ref, kc_ref, kp_ref, vc_ref, vp_ref, dq_ref, dk_ref, dv_ref):
        n = pl.program_id(1)

        @pl.when(n == 0)
        def _():
            dk_ref[...] = jnp.zeros_like(dk_ref)
            dv_ref[...] = jnp.zeros_like(dv_ref)

        cur_ok, prev_band = _band_masks()
        base = pl.multiple_of(n * tb, SPAN)
        for j in range(G):
            def residue(r, carry, j=j):
                off = j * band + r
                rows = pl.ds(off, SPAN, stride=d)
                q, dy = q_ref[rows, :], dy_ref[rows, :]
                lse_h = jnp.max(lse_ref[rows, :], axis=-1, keepdims=True)
                dd_h = jnp.max(dd_ref[rows, :], axis=-1, keepdims=True)
                kc, vc = kc_ref[rows, :], vc_ref[rows, :]
                if j > 0:
                    before = pl.ds(off - band, SPAN, stride=d)
                    kp, vp = kc_ref[before, :], vc_ref[before, :]
                    prev_ok = prev_band
                else:
                    before = pl.ds((G - 1) * band + r, SPAN, stride=d)
                    kp, vp = kp_ref[before, :], vp_ref[before, :]
                    prev_ok = prev_band & (n > 0)
                pc = jnp.exp(jnp.where(cur_ok, _dot_nt(q, kc) * ATT_SCALE, NEG) - lse_h)
                pp = jnp.exp(jnp.where(prev_ok, _dot_nt(q, kp) * ATT_SCALE, NEG) - lse_h)
                dsc = pc * (_dot_nt(dy, vc) - dd_h) * ATT_SCALE
                dsp = pp * (_dot_nt(dy, vp) - dd_h) * ATT_SCALE
                dq_ref[rows, :] = _dot(dsc, kc) + _dot(dsp, kp)
                here = pl.ds(base + off, SPAN, stride=d)
                dk_ref[here, :] += _dot_tn(dsc, q)
                dv_ref[here, :] += _dot_tn(pc, dy)
                there = pl.ds(jnp.maximum(base + off - band, r), SPAN, stride=d)
                dk_ref[there, :] += _dot_tn(dsp, q)
                dv_ref[there, :] += _dot_tn(pp, dy)
                return carry

            lax.fori_loop(0, d, residue, 0, unroll=min(d, DILS_UNROLL))

    blk = lambda f: pl.BlockSpec((tb, HEAD_DIM), f)
    cur = lambda h, n: (n, h)
    prev = lambda h, n: (jnp.maximum(n - 1, 0), h)
    whole = pl.BlockSpec((T, HEAD_DIM), lambda h, n: (0, h))
    sh = jax.ShapeDtypeStruct((T, B_WIDTH), F32)
    body, dep_specs, dep_args = _dep(kernel_body, 8, dep)
    return pl.pallas_call(
        body, grid=(B_HEADS, nblk),
        in_specs=[blk(lambda h, n: (n, gi * B_HEADS + h)), blk(cur), blk(cur), blk(cur), blk(cur), blk(prev),
                  blk(lambda h, n: (n, B_HEADS + h)), blk(lambda h, n: (jnp.maximum(n - 1, 0), B_HEADS + h))] + dep_specs,
        out_specs=[blk(cur), whole, whole], out_shape=[sh, sh, sh],
        compiler_params=_cp("parallel", "arbitrary"), name=name)(qr, dmix, lse, dd, kr, kr, kv, kv, *dep_args)


A_MQ_COL = 4 * A_WIDTH // MEM_WIDTH
B_MQ_COL = N_GROUPS * B_WIDTH // MEM_WIDTH


def _row(v):
    return v.reshape(1, -1).astype(F32)


def _local_step(x, mem, tgt, get_w, P, put_g, first_dep=None, forward_point=lambda i, value: value):
    T = x.shape[0]
    cosf, sinsg = _rope_tables(T)
    lb_soft = jax.nn.softmax(P["a_lb_logits"].astype(F32), axis=0)
    lb = lb_soft[0:1]
    qw_heads = jnp.repeat(P["b_qnorm"][0], B_HEADS, axis=0).reshape(1, -1)
    kw_heads = jnp.tile(_row(P["b_knorm"]), (1, B_HEADS))
    mqw = [jnp.tile(_row(P["mem_qnorm"][l]), (1, MEM_HEADS)) for l in range(2)]
    mkw = [jnp.tile(_row(P["mem_knorm"][l]), (1, MEM_HEADS)) for l in range(2)]
    nmix = [_row(P["norm_mix"][l]) for l in range(2)]
    nffn = [_row(P["norm_ffn"][l]) for l in range(2)]
    mnorm = [_row(P["mem_norm"][l]) for l in range(2)]
    kvn = _row(P["kv_norm"])
    onorm = _row(P["a_onorm"])
    W = {}

    def w_of(name, after=None):
        if name not in W:
            W[name] = get_w(name, after)
        return W[name]

    proj_a, xn0 = _rms_matmul(x, nmix[0], w_of("a_w_in"), tt=512, tn=1664, wt=True, name="proj_a", dep=first_dep)
    mkv0, mn0 = _rms_matmul(mem, mnorm[0], w_of("w_mem_kv0"), tt=MEM_TOKENS, tn=2 * MEM_WIDTH, wt=False, name="mem_kv0")
    o_raw, st = _hgrn2_fwd(proj_a, lb, name="hgrn2_fwd")
    o_raw = forward_point(0, o_raw)
    mm0 = _a_post_fwd(o_raw, proj_a, onorm, tt=512, name="a_post_fwd")
    mo0 = _mem_attn_fwd(proj_a, A_MQ_COL, mkv0, mqw[0], mkw[0], tt=512, name="mem_attn_fwd0")
    hm0 = _mm_res(x, mm0, mo0, w_of("w_out0", mo0), tt=512, name="out_proj0")
    hm0 = forward_point(1, hm0)
    gu0, hn0 = _rms_matmul(hm0, nffn[0], w_of("w_gate_up0", hm0), tt=512, tn=1408, wt=True, out_dtype=BF16, name="gate_up0")
    h1 = _swiglu_down(hm0, gu0, w_of("w_down0", gu0), tt=512, name="down0")
    h1 = forward_point(2, h1)
    kv, hkn = _rms_matmul(h1, kvn, w_of("w_kv", h1), tt=512, tn=768, wt=True, name="kv_proj")
    kr = _headnorm_rope_fwd(kv, kw_heads, cosf, sinsg, col0=0, n_heads=B_HEADS, tt=512, name="k_prep")

    proj_b, xn1 = _rms_matmul(h1, nmix[1], w_of("b_w_in", kr), tt=512, tn=1280, wt=True, name="proj_b")
    proj_b = forward_point(3, proj_b)
    mkv1, mn1 = _rms_matmul(mem, mnorm[1], w_of("w_mem_kv1", kr), tt=MEM_TOKENS, tn=2 * MEM_WIDTH, wt=False, name="mem_kv1")
    qr = _headnorm_rope_fwd(proj_b, qw_heads, cosf, sinsg, col0=0, n_heads=N_GROUPS * B_HEADS, tt=512, name="q_prep")
    outs = [(_dil_fwd if d == 1 else _dils_fwd)(qr, kr, kv, gi, d, name=f"dil_fwd{gi}") for gi, d in enumerate(DILATIONS)]
    mm1, lse_tot = _dil_combine_fwd([o for o, _ in outs], [s for _, s in outs], tt=512, name="dil_combine")
    mo1 = _mem_attn_fwd(proj_b, B_MQ_COL, mkv1, mqw[1], mkw[1], tt=512, name="mem_attn_fwd1")
    hm1 = _mm_res(h1, mm1, mo1, w_of("w_out1", mo1), tt=512, name="out_proj1")
    gu1, hn1 = _rms_matmul(hm1, nffn[1], w_of("w_gate_up1", hm1), tt=512, tn=1408, wt=True, out_dtype=BF16, name="gate_up1")
    dy, sq = _swiglu_down_loss(hm1, gu1, w_of("w_down1", gu1), tgt, tt=512, name="down1_loss")

    gP = {}
    zeros_mem = jnp.zeros((MEM_TOKENS, D_MODEL), F32)

    def ffn_bwd(l, dh, hm, gu, hn):
        dgu, g_wd = _swiglu_bwd(dh, gu, w_of(f"w_down{l}"), tt=256, name=f"swiglu_bwd{l}")
        g_wgu = _mm_tn(dgu, hn, tt=512, tka=1408, name=f"g_w_gate_up{l}")
        sent = put_g({f"w_down{l}": g_wd, f"w_gate_up{l}": g_wgu})
        dhm, g_nf = _rms_bwd_dx(hm, nffn[l], w_of(f"w_gate_up{l}"), dgu, dh, tt=512, wt=True, name=f"gate_up_bwd{l}", dep=sent)
        return dhm, g_nf

    def mix_bwd(l, dhm, mix_main, mix_mem, proj, qcol, mkv, mn):
        dmix, g_wout, *head_dots = _out_proj_bwd(dhm, mix_main, mix_mem, w_of(f"w_out{l}"), tt=512, name=f"out_proj_bwd{l}",
                                                 head_dots=l == 1)
        dmq, dmkv, dqw, dkw = _mem_attn_bwd(proj, qcol, mkv, mqw[l], mkw[l], dmix, tt=512, name=f"mem_attn_bwd{l}")
        g_wmkv = _mm_tn(mn, dmkv, tt=MEM_TOKENS, tka=512, name=f"g_w_mem_kv{l}")
        sent = put_g({f"w_out{l}": g_wout, f"w_mem_kv{l}": g_wmkv})
        _, g_mn = _rms_bwd_dx(mem, mnorm[l], w_of(f"w_mem_kv{l}"), dmkv, zeros_mem, tt=MEM_TOKENS, wt=False, name=f"mem_kv_bwd{l}")
        fold = lambda v: v.reshape(MEM_HEADS, MEM_HEAD_DIM).sum(axis=0)
        return dmix, dmq, g_mn, fold(dqw), fold(dkw), sent, head_dots

    dhm1, g_nf1 = ffn_bwd(1, dy, hm1, gu1, hn1)
    dmix1, dmq1, g_mn1, g_mq1, g_mk1, sent, (dd,) = mix_bwd(1, dhm1, mm1, mo1, proj_b, B_MQ_COL, mkv1, mn1)
    dqs, dks, dvs = [], [], []
    for gi, d in enumerate(DILATIONS):
        dq_g, dk_g, dv_g = _dil_bwd(qr, kr, kv, dmix1, lse_tot, dd, gi, d, name=f"dil_bwd{gi}", dep=sent if gi == 0 else None)
        dqs.append(dq_g)
        dks.append(dk_g)
        dvs.append(dv_g)
    dq_raw, dqw = _q_prep_bwd(proj_b, qw_heads, cosf, sinsg, dqs, tt=512, name="q_prep_bwd")
    dkv, dkw = _kv_prep_bwd(kv, kw_heads, cosf, sinsg, dks, dvs, tt=512, name="kv_prep_bwd")
    dproj_b = [dq_raw, dmq1]
    g_wb = _mm_tn_pieces(dproj_b, xn1, tt=512, name="g_b_w_in")
    g_wkv = _mm_tn(dkv, hkn, tt=512, tka=768, name="g_w_kv")
    sent = put_g({"b_w_in": g_wb, "w_kv": g_wkv})
    dh1, g_nm1 = _rms_bwd_dx(h1, nmix[1], w_of("b_w_in"), dproj_b, dhm1, tt=512, wt=True, name="proj_b_bwd", dep=sent)
    dh1, g_kvn = _rms_bwd_dx(h1, kvn, w_of("w_kv"), dkv, dh1, tt=512, wt=True, name="kv_proj_bwd")

    dhm0, g_nf0 = ffn_bwd(0, dh1, hm0, gu0, hn0)
    dmix0, dmq0, g_mn0, g_mq0, g_mk0, sent, _ = mix_bwd(0, dhm0, mm0, mo0, proj_a, A_MQ_COL, mkv0, mn0)
    do_raw, dg, g_onorm = _a_post_bwd(o_raw, proj_a, onorm, dmix0, tt=512, name="a_post_bwd", dep=sent)
    dq, dz, dv, dlb = _hgrn2_bwd(proj_a, lb, st, do_raw, name="hgrn2_bwd")
    dproj_a = [dq, dz, dv, dg, dmq0]
    sent = put_g({"a_w_in": _mm_tn_pieces(dproj_a, xn0, tt=512, name="g_a_w_in")})
    gx, g_nm0 = _rms_bwd_dx(x, nmix[0], w_of("a_w_in"), dproj_a, dhm0, tt=512, wt=True, name="proj_a_bwd", dep=sent)

    dl0 = lb_soft[0:1] * lb_soft[1:2] * dlb
    gP["a_lb_logits"] = jnp.concatenate([dl0, -dl0], axis=0)
    gP["a_onorm"] = g_onorm
    gP["norm_mix"] = jnp.concatenate([g_nm0, g_nm1], axis=0)
    gP["norm_ffn"] = jnp.concatenate([g_nf0, g_nf1], axis=0)
    gP["b_qnorm"] = dqw.reshape(N_GROUPS, B_HEADS, HEAD_DIM).sum(axis=1)[None]
    gP["kv_norm"] = g_kvn.reshape(-1)
    gP["b_knorm"] = dkw.reshape(B_HEADS, HEAD_DIM).sum(axis=0)
    gP["mem_norm"] = jnp.concatenate([g_mn0, g_mn1], axis=0)
    gP["mem_qnorm"] = jnp.stack([g_mq0, g_mq1])
    gP["mem_knorm"] = jnp.stack([g_mk0, g_mk1])
    return sq, gx, gP


MESH_ID = pl.DeviceIdType.MESH
HBM_SPEC = pl.BlockSpec(memory_space=pltpu.HBM)


def _position():
    return lax.axis_index("x"), lax.axis_index("y"), lax.axis_index("c")


def _all_gather(blocks, *, name):
    n = len(blocks)

    def body(*refs):
        x_refs, out_refs = refs[:n], refs[n:2 * n]
        send_sems, recv_sems, local_sems = refs[2 * n:]
        x, y, c = _position()
        me, sibling = (x, y, c), (x, y, 1 - c)
        chips = [(1 - x, y), (x, 1 - y), (1 - x, 1 - y)]

        def slot(a, px, py, pc):
            return out_refs[a].at[4 * px + 2 * py + pc]

        def copy(a, k, blk, to, src=None):
            return pltpu.make_async_remote_copy(
                src_ref=slot(a, *blk) if src is None else src, dst_ref=slot(a, *blk),
                send_sem=send_sems.at[7 * a + k], recv_sem=recv_sems.at[7 * a + k], device_id=to, device_id_type=MESH_ID)

        mine = [pltpu.make_async_copy(x_refs[a], slot(a, *me), local_sems.at[a]) for a in range(n)]
        for cp in mine:
            cp.start()
        first = []
        for a in range(n):
            first.append(copy(a, 0, me, sibling, src=x_refs[a]))
            first += [copy(a, 1 + j, me, (*chip, c), src=x_refs[a]) for j, chip in enumerate(chips)]
        for cp in first:
            cp.start()
        passed = []
        for j, chip in enumerate(chips):
            for a in range(n):
                copy(a, 1 + j, (*chip, c), me).wait_recv()
                cp = copy(a, 4 + j, (*chip, c), sibling)
                cp.start()
                passed.append(cp)
        for a in range(n):
            copy(a, 0, sibling, me).wait_recv()
            for j, chip in enumerate(chips):
                copy(a, 4 + j, (*chip, 1 - c), me).wait_recv()
        for cp in first + passed:
            cp.wait_send()
        for cp in mine:
            cp.wait()

    return pl.pallas_call(
        body, out_shape=[jax.ShapeDtypeStruct((N_DEV,) + b.shape, b.dtype) for b in blocks],
        in_specs=[HBM_SPEC] * n, out_specs=[HBM_SPEC] * n,
        scratch_shapes=[pltpu.SemaphoreType.DMA((7 * n,)), pltpu.SemaphoreType.DMA((7 * n,)), pltpu.SemaphoreType.DMA((n,))],
        name=name)(*blocks)


def _all_gather_direct(block, after, *, name):
    def body(x_ref, after_ref, out_ref, send_sems, recv_sems, local_sem):
        x, y, c = _position()
        me = 4 * x + 2 * y + c
        mine = pltpu.make_async_copy(x_ref, out_ref.at[me], local_sem)
        mine.start()
        copies = []
        for k in ALL_PEERS:
            cp = pltpu.make_async_remote_copy(
                src_ref=x_ref, dst_ref=out_ref.at[me], send_sem=send_sems.at[k - 1], recv_sem=recv_sems.at[k - 1],
                device_id=_peer(k, x, y, c), device_id_type=MESH_ID)
            cp.start()
            copies.append(cp)
        for cp in copies:
            cp.wait()
        mine.wait()

    return pl.pallas_call(
        body, out_shape=jax.ShapeDtypeStruct((N_DEV,) + block.shape, block.dtype),
        in_specs=[HBM_SPEC, pl.BlockSpec(memory_space=pl.ANY)], out_specs=HBM_SPEC,
        scratch_shapes=[pltpu.SemaphoreType.DMA((7,)), pltpu.SemaphoreType.DMA((7,)), pltpu.SemaphoreType.DMA],
        name=name)(block, after)


SEM_SPEC = pl.BlockSpec(memory_space=pltpu.SEMAPHORE)
ANY_SPEC = pl.BlockSpec(memory_space=pl.ANY)
DATAFLOW = pltpu.SideEffectType.DATAFLOW_SIDE_EFFECTING


def _peer(k, x, y, c):
    return (1 - x if (k >> 2) & 1 else x, 1 - y if (k >> 1) & 1 else y, 1 - c if k & 1 else c)


def _own_slot_filled(own_block):
    x, y, c = _position()
    zone = lax.empty((N_DEV,) + own_block.shape, own_block.dtype)
    return lax.dynamic_update_slice_in_dim(zone, own_block[None], 4 * x + 2 * y + c, axis=0)


ALL_PEERS = tuple(range(1, N_DEV))
SIBLING_AND_SAME_CORE = (1, 2, 4, 6)
SAME_CORE = (2, 4, 6)


def _split_start(srcs, scatter, after, *, name, relations=ALL_PEERS, carried=None):
    n = len(srcs)
    extra = ([] if after is None else [after]) + ([] if carried is None else [carried])
    n_carried = 0 if carried is None else 1
    x, y, c = _position()
    me = 4 * x + 2 * y + c
    lands = [_own_slot_filled(lax.dynamic_index_in_dim(s, me, 0, keepdims=False) if scatter else s) for s in srcs]

    def body(*refs):
        src_refs, land_refs = refs[:n], refs[n:2 * n]
        send_sems, recv_sems = refs[2 * n + len(extra)], refs[2 * n + len(extra) + 1]
        token = refs[2 * n + len(extra) + 2 + 2 * n]
        bx, by, bc = _position()
        bme = 4 * bx + 2 * by + bc
        for a in range(n):
            for k in relations:
                tx, ty, tc = _peer(k, bx, by, bc)
                src = src_refs[a].at[4 * tx + 2 * ty + tc] if scatter else src_refs[a]
                pltpu.make_async_remote_copy(
                    src_ref=src, dst_ref=land_refs[a].at[bme],
                    send_sem=send_sems.at[7 * a + k - 1], recv_sem=recv_sems.at[7 * a + k - 1],
                    device_id=(tx, ty, tc), device_id_type=MESH_ID).start()
        token[...] = jnp.zeros_like(token)

    hbm = lambda a: pltpu.HBM(a.shape, a.dtype)
    outs = pl.pallas_call(
        body, name=name,
        out_shape=(pltpu.SemaphoreType.DMA((7 * n,)), pltpu.SemaphoreType.DMA((7 * n,)),
                   *[hbm(s) for s in srcs], *[hbm(l) for l in lands], jax.ShapeDtypeStruct((8, 128), F32),
                   *([hbm(carried)] if n_carried else [])),
        in_specs=[HBM_SPEC] * (2 * n) + [ANY_SPEC] * len(extra),
        out_specs=(SEM_SPEC, SEM_SPEC, *[HBM_SPEC] * (2 * n), pl.BlockSpec(memory_space=pltpu.VMEM), *([ANY_SPEC] * n_carried)),
        input_output_aliases={**{i: 2 + i for i in range(2 * n)},
                              **({2 * n + len(extra) - 1: 2 * n + 3} if n_carried else {})},
        compiler_params=pltpu.CompilerParams(has_side_effects=DATAFLOW),
    )(*[pltpu.with_memory_space_constraint(s, pltpu.HBM) for s in srcs],
      *[pltpu.with_memory_space_constraint(l, pltpu.HBM) for l in lands], *extra)
    return {"n": n, "relations": relations, "send": outs[0], "recv": outs[1], "srcs": list(outs[2:2 + n]),
            "lands": list(outs[2 + n:2 + 2 * n]), "token": outs[2 * n + 2], "carried": outs[-1] if n_carried else None}


def _forward_start(lands, carried, *, name):
    n = len(lands)

    def body(*refs):
        land_refs = refs[:n]
        send_sems, recv_sems = refs[n + 1], refs[n + 2]
        bx, by, bc = _position()
        for a in range(n):
            for k in SAME_CORE:
                tx, ty, tc = _peer(k, bx, by, bc)
                block = land_refs[a].at[4 * tx + 2 * ty + tc]
                pltpu.make_async_remote_copy(
                    src_ref=block, dst_ref=block,
                    send_sem=send_sems.at[7 * a + k - 1], recv_sem=recv_sems.at[7 * a + k - 1],
                    device_id=(bx, by, 1 - bc), device_id_type=MESH_ID).start()

    hbm = lambda a: pltpu.HBM(a.shape, a.dtype)
    outs = pl.pallas_call(
        body, name=name,
        out_shape=(pltpu.SemaphoreType.DMA((7 * n,)), pltpu.SemaphoreType.DMA((7 * n,)),
                   *[hbm(l) for l in lands], hbm(carried)),
        in_specs=[HBM_SPEC] * n + [ANY_SPEC],
        out_specs=(SEM_SPEC, SEM_SPEC, *[HBM_SPEC] * n, ANY_SPEC),
        input_output_aliases={i: 2 + i for i in range(n + 1)},
        compiler_params=pltpu.CompilerParams(has_side_effects=DATAFLOW),
    )(*lands, carried)
    handle = {"n": n, "relations": SAME_CORE, "send": outs[0], "recv": outs[1], "srcs": [], "lands": list(outs[2:2 + n])}
    return handle, outs[-1]


def _split_wait(handle, after, *, name):
    n, ns = handle["n"], len(handle["srcs"])

    def body(*refs):
        land_refs = refs[ns:ns + n]
        send_sems, recv_sems = refs[ns + n], refs[ns + n + 1]
        bx, by, bc = _position()
        for a in range(n):
            for k in handle["relations"]:
                block = land_refs[a].at[0]
                cp = pltpu.make_async_remote_copy(
                    src_ref=block, dst_ref=block,
                    send_sem=send_sems.at[7 * a + k - 1], recv_sem=recv_sems.at[7 * a + k - 1],
                    device_id=_peer(k, bx, by, bc), device_id_type=MESH_ID)
                cp.wait_send()
                cp.wait_recv()

    hbm = lambda a: pltpu.HBM(a.shape, a.dtype)
    outs = pl.pallas_call(
        body, name=name,
        out_shape=(*[hbm(s) for s in handle["srcs"]], *[hbm(l) for l in handle["lands"]]),
        in_specs=[HBM_SPEC] * (ns + n) + [SEM_SPEC, SEM_SPEC, ANY_SPEC],
        out_specs=tuple([HBM_SPEC] * (ns + n)),
        input_output_aliases={i: i for i in range(ns + n)},
        compiler_params=pltpu.CompilerParams(has_side_effects=DATAFLOW),
    )(*handle["srcs"], *handle["lands"], handle["send"], handle["recv"], after)
    return list(outs[ns:])


def _sum_sources(parts, *, tr, name):
    n, R, C = parts.shape

    def body(p_ref, o_ref):
        acc = p_ref[0].astype(F32)
        for s in range(1, n):
            acc = acc + p_ref[s].astype(F32)
        o_ref[...] = acc

    return pl.pallas_call(
        body, grid=(R // tr,), in_specs=[pl.BlockSpec((n, tr, C), lambda i: (0, i, 0))],
        out_specs=pl.BlockSpec((tr, C), lambda i: (i, 0)),
        out_shape=jax.ShapeDtypeStruct((R, C), F32), compiler_params=_cp("parallel"), name=name)(parts)


def _adamw_math(g, w, m, v):
    c1 = 1.0 - ADAM_B1 ** ADAM_STEP
    c2 = 1.0 - ADAM_B2 ** ADAM_STEP
    nm = ADAM_B1 * m + (1.0 - ADAM_B1) * g
    nv = ADAM_B2 * v + (1.0 - ADAM_B2) * (g * g)
    return -ADAM_LR * ((nm / c1) / (jnp.sqrt(nv / c2) + ADAM_EPS) + ADAM_WD * w), nm, nv


def _reduce_adamw(received, w, m, v, *, tr, name):
    L, R, C = w.shape

    def body(*refs):
        p_refs = refs[:L]
        w_ref, m_ref, v_ref, g_ref, d_ref, nm_ref, nv_ref = refs[L:]
        for l in range(L):
            @pl.when(pl.program_id(0) == l)
            def _(p_ref=p_refs[l]):
                acc = p_ref[0].astype(F32)
                for s in range(1, N_DEV):
                    acc = acc + p_ref[s].astype(F32)
                g_ref[...] = acc
                d_ref[...], nm_ref[...], nv_ref[...] = _adamw_math(acc, w_ref[...], m_ref[...], v_ref[...])

    p_spec = pl.BlockSpec((N_DEV, tr, C), lambda l, i: (0, i, 0))
    blk = pl.BlockSpec((None, tr, C), lambda l, i: (l, i, 0))
    sh = jax.ShapeDtypeStruct((L, R, C), F32)
    return pl.pallas_call(
        body, grid=(L, R // tr), in_specs=[p_spec] * L + [blk] * 3, out_specs=[blk] * 4, out_shape=[sh] * 4,
        compiler_params=_cp("parallel", "parallel"), name=name)(*received, w, m, v)


def _adamw(g, w, m, v, *, tr, name):
    L, R, C = w.shape

    def body(g_ref, w_ref, m_ref, v_ref, d_ref, nm_ref, nv_ref):
        d_ref[...], nm_ref[...], nv_ref[...] = _adamw_math(g_ref[...], w_ref[...], m_ref[...], v_ref[...])

    blk = pl.BlockSpec((None, tr, C), lambda l, i: (l, i, 0))
    sh = jax.ShapeDtypeStruct((L, R, C), F32)
    return pl.pallas_call(
        body, grid=(L, R // tr), in_specs=[blk] * 4, out_specs=[blk] * 3, out_shape=[sh] * 3,
        compiler_params=_cp("parallel", "parallel"), name=name)(g, w, m, v)


UNITS = {
    "a_w_in": ("a_w_in", 0, True), "w_mem_kv0": ("w_mem_kv", 0, False), "w_out0": ("w_out", 0, False),
    "w_gate_up0": ("w_gate_up", 0, True), "w_down0": ("w_down", 0, False), "w_kv": ("w_kv", None, True),
    "b_w_in": ("b_w_in", 0, True), "w_mem_kv1": ("w_mem_kv", 1, False), "w_out1": ("w_out", 1, False),
    "w_gate_up1": ("w_gate_up", 1, True), "w_down1": ("w_down", 1, False),
}
BIG = ("a_w_in", "b_w_in", "w_kv", "w_mem_kv", "w_out", "w_gate_up", "w_down")
ADAMW_ROW_TILE = {"a_w_in": 208, "b_w_in": 160, "w_kv": 192, "w_mem_kv": 128, "w_out": 128, "w_gate_up": 176, "w_down": 176}


def _wire_block(weights, unit):
    name, layer, col = UNITS[unit]
    a = weights[name] if layer is None else weights[name][layer]
    return (a.T if col else a).astype(BF16)


SMALL_REPLICATED = ("norm_mix", "norm_ffn", "b_qnorm", "kv_norm", "b_knorm", "mem_norm", "mem_qnorm", "mem_knorm")
SMALL_SHARDED = ("a_lb_logits", "a_onorm")
SMALL_ORDER = SMALL_REPLICATED + SMALL_SHARDED
LANES = 128


def _prod(shape):
    n = 1
    for s in shape:
        n *= s
    return n


def _pack_flat(arrays, rows, cols, dtype):
    flat = jnp.concatenate([a.reshape(-1).astype(dtype) for a in arrays])
    return jnp.pad(flat, (0, rows * cols - flat.shape[0])).reshape(rows, cols)


def _unpack_flat(packed, shapes):
    flat = packed.reshape(-1)
    out, off = [], 0
    for s in shapes:
        out.append(flat[off:off + _prod(s)].reshape(s))
        off += _prod(s)
    return out


def kernel(x, mem, norm_mix, norm_ffn, a_w_in, a_lb_logits, a_onorm, b_w_in, b_qnorm, kv_norm, w_kv, b_knorm, mem_norm, w_mem_kv, mem_qnorm, mem_knorm, w_out, w_gate_up, w_down, loss_target, m_norm_mix, m_norm_ffn, m_a_w_in, m_a_lb_logits, m_a_onorm, m_b_w_in, m_b_qnorm, m_kv_norm, m_w_kv, m_b_knorm, m_mem_norm, m_w_mem_kv, m_mem_qnorm, m_mem_knorm, m_w_out, m_w_gate_up, m_w_down, v_norm_mix, v_norm_ffn, v_a_w_in, v_a_lb_logits, v_a_onorm, v_b_w_in, v_b_qnorm, v_kv_norm, v_w_kv, v_b_knorm, v_mem_norm, v_w_mem_kv, v_mem_qnorm, v_mem_knorm, v_w_out, v_w_gate_up, v_w_down):
    names = ("norm_mix", "norm_ffn", "a_w_in", "a_lb_logits", "a_onorm", "b_w_in", "b_qnorm", "kv_norm", "w_kv", "b_knorm",
             "mem_norm", "w_mem_kv", "mem_qnorm", "mem_knorm", "w_out", "w_gate_up", "w_down")
    w = dict(zip(names, (norm_mix, norm_ffn, a_w_in, a_lb_logits, a_onorm, b_w_in, b_qnorm, kv_norm, w_kv, b_knorm,
                         mem_norm, w_mem_kv, mem_qnorm, mem_knorm, w_out, w_gate_up, w_down)))
    m = dict(zip(names, (m_norm_mix, m_norm_ffn, m_a_w_in, m_a_lb_logits, m_a_onorm, m_b_w_in, m_b_qnorm, m_kv_norm, m_w_kv,
                         m_b_knorm, m_mem_norm, m_w_mem_kv, m_mem_qnorm, m_mem_knorm, m_w_out, m_w_gate_up, m_w_down)))
    v = dict(zip(names, (v_norm_mix, v_norm_ffn, v_a_w_in, v_a_lb_logits, v_a_onorm, v_b_w_in, v_b_qnorm, v_kv_norm, v_w_kv,
                         v_b_knorm, v_mem_norm, v_w_mem_kv, v_mem_qnorm, v_mem_knorm, v_w_out, v_w_gate_up, v_w_down)))

    first = ["a_w_in", "w_mem_kv0"]
    gathered = _all_gather([_wire_block(w, u) for u in first] + [_pack_flat([a_lb_logits, a_onorm], 8, LANES, F32)],
                           name="gather_first")
    full = {u: g.reshape(-1, g.shape[-1]) for u, g in zip(first, gathered)}
    small_in = gathered[-1].reshape(N_DEV, -1)
    P = {n: w[n] for n in SMALL_REPLICATED}
    P["a_lb_logits"] = small_in[:, :192].reshape(N_DEV, 2, 96).transpose(1, 0, 2).reshape(2, A_WIDTH)
    P["a_onorm"] = small_in[:, 192:288].reshape(1, A_WIDTH)
    later = [["w_out0", "w_gate_up0"], ["w_down0", "w_kv"], ["b_w_in", "w_mem_kv1"], ["w_out1", "w_gate_up1", "w_down1"]]
    first_half, second_half = {}, {}

    def start_first_half(i, after, carried=None):
        first_half[i] = _split_start([_wire_block(w, u) for u in later[i]], False, after, name=f"gather{i}_start",
                                     relations=SIBLING_AND_SAME_CORE, carried=carried)
        return first_half[i]

    token = start_first_half(0, gathered[-1])["token"]
    token = start_first_half(1, token)["token"]

    def forward_point(i, value):
        landed = _split_wait(first_half[i], value, name=f"gather{i}_landed")
        second_half[i], value = _forward_start(landed, value, name=f"gather{i}_forward")
        if i + 2 < len(later):
            value = start_first_half(i + 2, None, carried=value)["carried"]
        return value

    def get_w(unit, after):
        if unit not in full:
            i = [unit in group for group in later].index(True)
            for u, land in zip(later[i], _split_wait(second_half[i], after, name=f"gather{i}_wait")):
                full[u] = land.reshape(-1, land.shape[-1])
        return full[unit]

    sent = []

    def put_g(group):
        units = list(group)
        handle = _split_start([group[u].reshape(N_DEV, -1, group[u].shape[-1]) for u in units], True, None,
                              name=f"scatter{len(sent)}_start")
        sent.append((units, handle))
        return handle["token"]

    sq, gx, gP = _local_step(x[0], mem[0], loss_target[0], get_w, P, put_g, first_dep=token, forward_point=forward_point)
    loss_here = (0.5 * jnp.sum(sq) / D_MODEL).reshape(1)

    received = {}
    group_of = {u: i for i, (units, _) in enumerate(sent) for u in units}
    out = {"grad": {}, "delta": {}, "new_m": {}, "new_v": {}}
    newest = [gx]

    def update_big(n):
        shape = w[n].shape
        as3 = lambda a: a.reshape((-1,) + shape[-2:])
        mine = [u for u, (wn, _, _) in UNITS.items() if wn == n]
        for i in sorted({group_of[u] for u in mine}):
            if sent[i][0][0] not in received:
                received.update(zip(sent[i][0], _split_wait(sent[i][1], newest[0], name=f"scatter{i}_wait")))
        flip = (lambda a: jnp.swapaxes(a, 1, 2)) if UNITS[mine[0]][2] else (lambda a: a)
        res = _reduce_adamw([received[u] for u in mine], flip(as3(w[n])), flip(as3(m[n])), flip(as3(v[n])),
                            tr=ADAMW_ROW_TILE[n], name=f"adamw_{n}")
        newest[0] = res[1]
        for kind, r in zip(("grad", "delta", "new_m", "new_v"), res):
            out[kind][n] = flip(r).reshape(shape)

    for n in ("w_down", "w_gate_up", "w_out", "w_mem_kv", "b_w_in", "w_kv"):
        update_big(n)

    full_shapes = [(2, A_WIDTH) if n == "a_lb_logits" else (1, A_WIDTH) if n == "a_onorm" else w[n].shape for n in SMALL_ORDER]
    n_small = sum(_prod(s) for s in full_shapes) + 1
    rows_small = -(-n_small // (8 * LANES)) * 8
    g_all = _all_gather_direct(_pack_flat([gP[n] for n in SMALL_ORDER] + [loss_here], rows_small, LANES, F32),
                               newest[0], name="gather_small_grads")
    summed = _unpack_flat(_sum_sources(g_all, tr=rows_small, name="sum_small_grads"), full_shapes + [(1,)])
    g_small = dict(zip(SMALL_ORDER, summed))
    loss = summed[-1].reshape(())
    me = 4 * lax.axis_index("x") + 2 * lax.axis_index("y") + lax.axis_index("c")
    for n in SMALL_SHARDED:
        g_small[n] = lax.dynamic_slice_in_dim(g_small[n], me * 96, 96, axis=1)
    shapes = [w[n].shape for n in SMALL_ORDER]
    rows_upd = -(-sum(_prod(s) for s in shapes) // (8 * LANES)) * 8
    pk = lambda d: _pack_flat([d[n] for n in SMALL_ORDER], rows_upd, LANES, F32)
    res = _adamw(pk(g_small)[None], pk(w)[None], pk(m)[None], pk(v)[None], tr=rows_upd, name="adamw_small")
    out["grad"].update(g_small)
    for kind, packed in zip(("delta", "new_m", "new_v"), res):
        out[kind].update(zip(SMALL_ORDER, _unpack_flat(packed[0], shapes)))
    newest[0] = res[0]
    update_big("a_w_in")

    return (loss, gx[None], *[out["grad"][n] for n in names], *[out["delta"][n] for n in names],
            *[out["new_m"][n] for n in names], *[out["new_v"][n] for n in names])
```

```python
import functools

import jax
import jax.numpy as jnp
from jax import lax
from jax.experimental import pallas as pl
from jax.experimental.pallas import tpu as pltpu

F32 = jnp.float32
BF16 = jnp.bfloat16

N_DEV = 8
D_MODEL = 1024
HEAD_DIM = 128
A_HEADS = 6
A_WIDTH = A_HEADS * HEAD_DIM
CHUNK = 64
B_HEADS = 6
B_WIDTH = B_HEADS * HEAD_DIM
DILATIONS = (1, 4, 16)
SPAN = 128
N_GROUPS = 3
ROPE_THETA = 10000.0
MEM_TOKENS = 256
MEM_HEADS = 4
MEM_HEAD_DIM = 64
MEM_WIDTH = MEM_HEADS * MEM_HEAD_DIM
FFN_HIDDEN = 2816
EPS = 1e-6

ADAM_LR = 0.001
ADAM_B1 = 0.9
ADAM_B2 = 0.999
ADAM_EPS = 1e-08
ADAM_WD = 0.01
ADAM_STEP = 10

V7X_VMEM_LIMIT_BYTES = 56 * 1024 * 1024

NT_DIMS = (((1,), (1,)), ((), ()))
TN_DIMS = (((0,), (0,)), ((), ()))


def _cp(*sem):
    return pltpu.CompilerParams(dimension_semantics=sem, vmem_limit_bytes=V7X_VMEM_LIMIT_BYTES)


def _dot(a, b):
    return jnp.dot(a.astype(BF16), b.astype(BF16), preferred_element_type=F32)


def _dot_nt(a, b):
    return lax.dot_general(a.astype(BF16), b.astype(BF16), NT_DIMS, preferred_element_type=F32)


def _dot_tn(a, b):
    return lax.dot_general(a.astype(BF16), b.astype(BF16), TN_DIMS, preferred_element_type=F32)


def _dot3(m01, x):
    hi = x.astype(BF16)
    r1 = x - hi.astype(F32)
    mid = r1.astype(BF16)
    lo = (r1 - mid.astype(F32)).astype(BF16)
    d = functools.partial(jnp.dot, preferred_element_type=F32)
    return d(m01, hi) + d(m01, mid) + d(m01, lo)


def _sigmoid(x):
    return 1.0 / (1.0 + jnp.exp(-x))


def _full(shape):
    return pl.BlockSpec(shape, lambda *_: (0,) * len(shape))


def _dep(body, n_in, dep):
    if dep is None:
        return body, [], []

    def with_dep(*refs):
        return body(*refs[:n_in], *refs[n_in + 1:])

    return with_dep, [pl.BlockSpec(memory_space=pl.ANY)], [dep]


def _rms_matmul(x, g, w, *, tt, tn, wt, name, out_dtype=F32, dep=None, rotate=None):
    T, K = x.shape
    N = w.shape[0] if wt else w.shape[1]
    n_rot = 0 if rotate is None else rotate[0].shape[1] // HEAD_DIM
    extra_in = [] if rotate is None else list(rotate)

    def kernel_body(x_ref, g_ref, w_ref, *rest):
        y_ref, xn_ref = rest[len(extra_in)], rest[len(extra_in) + 1]
        xf = x_ref[...]
        r = lax.rsqrt(jnp.mean(xf * xf, axis=-1, keepdims=True) + EPS)
        xn = (xf * r * g_ref[...]).astype(BF16)
        xn_ref[...] = xn
        for j in range(N // tn):
            cols = slice(j * tn, (j + 1) * tn)
            y = _dot_nt(xn, w_ref[cols, :]) if wt else _dot(xn, w_ref[:, cols])
            y_ref[:, cols] = y.astype(out_dtype)
            for h in range(j * tn // HEAD_DIM, min((j + 1) * tn // HEAD_DIM, n_rot)):
                gw_ref, c_ref, s_ref, yr_ref = rest[0], rest[1], rest[2], rest[len(extra_in) + 2]
                sl = slice(h * HEAD_DIM, (h + 1) * HEAD_DIM)
                xhat, _ = _head_rms(y[:, h * HEAD_DIM - j * tn:(h + 1) * HEAD_DIM - j * tn])
                yr_ref[:, sl] = _rope(xhat * gw_ref[:, sl], c_ref[...], s_ref[...])

    tbl = pl.BlockSpec((tt, HEAD_DIM), lambda i: (i, 0))
    rot_specs = [] if rotate is None else [_full((1, n_rot * HEAD_DIM)), tbl, tbl]
    body, dep_specs, dep_args = _dep(kernel_body, 3 + len(extra_in), dep)
    return pl.pallas_call(
        body, grid=(T // tt,),
        in_specs=[pl.BlockSpec((tt, K), lambda i: (i, 0)), _full((1, K)), _full(w.shape)] + rot_specs + dep_specs,
        out_specs=[pl.BlockSpec((tt, N), lambda i: (i, 0)), pl.BlockSpec((tt, K), lambda i: (i, 0))]
        + ([] if rotate is None else [pl.BlockSpec((tt, n_rot * HEAD_DIM), lambda i: (i, 0))]),
        out_shape=[jax.ShapeDtypeStruct((T, N), out_dtype), jax.ShapeDtypeStruct((T, K), BF16)]
        + ([] if rotate is None else [jax.ShapeDtypeStruct((T, n_rot * HEAD_DIM), F32)]),
        compiler_params=_cp("parallel"), name=name)(x, g, w, *extra_in, *dep_args)


def _mm_res(res, a1, a2, w, *, tt, name):
    T, K1 = a1.shape
    K2 = a2.shape[1]
    N = w.shape[1]

    def body(r_ref, a1_ref, a2_ref, w_ref, o_ref):
        o_ref[...] = r_ref[...] + _dot(a1_ref[...], w_ref[:K1, :]) + _dot(a2_ref[...], w_ref[K1:, :])

    return pl.pallas_call(
        body, grid=(T // tt,),
        in_specs=[pl.BlockSpec((tt, N), lambda i: (i, 0)), pl.BlockSpec((tt, K1), lambda i: (i, 0)),
                  pl.BlockSpec((tt, K2), lambda i: (i, 0)), _full((K1 + K2, N))],
        out_specs=pl.BlockSpec((tt, N), lambda i: (i, 0)),
        out_shape=jax.ShapeDtypeStruct((T, N), F32),
        compiler_params=_cp("parallel"), name=name)(res, a1, a2, w)


def _swiglu_down(h, gu, wd, *, tt, name):
    T, D = h.shape
    Fh = wd.shape[0]

    def body(h_ref, gt_ref, up_ref, w_ref, o_ref):
        gt = gt_ref[...].astype(F32)
        act = gt * _sigmoid(gt) * up_ref[...].astype(F32)
        o_ref[...] = h_ref[...] + _dot(act, w_ref[...])

    return pl.pallas_call(
        body, grid=(T // tt,),
        in_specs=[pl.BlockSpec((tt, D), lambda i: (i, 0)), pl.BlockSpec((tt, Fh), lambda i: (i, 0)),
                  pl.BlockSpec((tt, Fh), lambda i: (i, 1)), _full((Fh, D))],
        out_specs=pl.BlockSpec((tt, D), lambda i: (i, 0)),
        out_shape=jax.ShapeDtypeStruct((T, D), F32),
        compiler_params=_cp("parallel"), name=name)(h, gu, gu, wd)


def _swiglu_down_loss(h, gu, wd, tgt, *, tt, name):
    T, D = h.shape
    Fh = wd.shape[0]

    def body(h_ref, gt_ref, up_ref, w_ref, t_ref, dy_ref, acc_ref):
        @pl.when(pl.program_id(0) == 0)
        def _():
            acc_ref[...] = jnp.zeros_like(acc_ref)

        gt = gt_ref[...].astype(F32)
        act = gt * _sigmoid(gt) * up_ref[...].astype(F32)
        e = h_ref[...] + _dot(act, w_ref[...]) - t_ref[...]
        dy_ref[...] = e * (1.0 / D)
        acc_ref[...] += jnp.sum(e * e, axis=0, keepdims=True)

    row = pl.BlockSpec((tt, D), lambda i: (i, 0))
    return pl.pallas_call(
        body, grid=(T // tt,),
        in_specs=[row, pl.BlockSpec((tt, Fh), lambda i: (i, 0)), pl.BlockSpec((tt, Fh), lambda i: (i, 1)), _full((Fh, D)), row],
        out_specs=[row, _full((1, D))],
        out_shape=[jax.ShapeDtypeStruct((T, D), F32), jax.ShapeDtypeStruct((1, D), F32)],
        compiler_params=_cp("arbitrary"), name=name)(h, gu, gu, wd, tgt)


def _swiglu_bwd(dh, gu, wd, *, tt, name):
    T, D = dh.shape
    Fh = wd.shape[0]
    last = T // tt - 1

    def body(dh_ref, gt_ref, up_ref, w_ref, dgu_ref, gw_ref, acc):
        @pl.when(pl.program_id(0) == 0)
        def _():
            acc[...] = jnp.zeros_like(acc)

        gt = gt_ref[...].astype(F32)
        up = up_ref[...].astype(F32)
        s = _sigmoid(gt)
        silu = gt * s
        dh16 = dh_ref[...].astype(BF16)
        dact = _dot_nt(dh16, w_ref[...])
        acc[...] += _dot_tn((silu * up).astype(BF16), dh16)
        dgu_ref[:, :Fh] = (dact * up * (s * (1.0 + gt * (1.0 - s)))).astype(BF16)
        dgu_ref[:, Fh:] = (dact * silu).astype(BF16)

        @pl.when(pl.program_id(0) == last)
        def _():
            gw_ref[...] = acc[...].astype(BF16)

    return pl.pallas_call(
        body, grid=(T // tt,),
        in_specs=[pl.BlockSpec((tt, D), lambda i: (i, 0)), pl.BlockSpec((tt, Fh), lambda i: (i, 0)),
                  pl.BlockSpec((tt, Fh), lambda i: (i, 1)), _full((Fh, D))],
        out_specs=[pl.BlockSpec((tt, 2 * Fh), lambda i: (i, 0)), _full((Fh, D))],
        out_shape=[jax.ShapeDtypeStruct((T, 2 * Fh), BF16), jax.ShapeDtypeStruct((Fh, D), BF16)],
        scratch_shapes=[pltpu.VMEM((Fh, D), F32)],
        compiler_params=_cp("arbitrary"), name=name)(dh, gu, gu, wd)


def _out_proj_bwd(dy, a1, a2, w, *, tt, name, head_dots=False):
    T, N = dy.shape
    K1, K2 = a1.shape[1], a2.shape[1]
    K = K1 + K2
    last = T // tt - 1

    def body(dy_ref, a1_ref, a2_ref, w_ref, da_ref, gw_ref, *rest):
        acc = rest[-1]

        @pl.when(pl.program_id(0) == 0)
        def _():
            acc[...] = jnp.zeros_like(acc)

        dy16 = dy_ref[...].astype(BF16)
        da = _dot_nt(dy16, w_ref[...])
        da_ref[...] = da
        acc[:K1, :] += _dot_tn(a1_ref[...], dy16)
        acc[K1:, :] += _dot_tn(a2_ref[...], dy16)
        if head_dots:
            for h in range(K1 // HEAD_DIM):
                sl = slice(h * HEAD_DIM, (h + 1) * HEAD_DIM)
                rest[0][:, sl] = jnp.broadcast_to(jnp.sum(da[:, sl] * a1_ref[:, sl], axis=-1, keepdims=True), (tt, HEAD_DIM))

        @pl.when(pl.program_id(0) == last)
        def _():
            gw_ref[...] = acc[...].astype(BF16)

    extra_specs = [pl.BlockSpec((tt, K1), lambda i: (i, 0))] if head_dots else []
    extra_shapes = [jax.ShapeDtypeStruct((T, K1), F32)] if head_dots else []
    return pl.pallas_call(
        body, grid=(T // tt,),
        in_specs=[pl.BlockSpec((tt, N), lambda i: (i, 0)), pl.BlockSpec((tt, K1), lambda i: (i, 0)),
                  pl.BlockSpec((tt, K2), lambda i: (i, 0)), _full((K, N))],
        out_specs=[pl.BlockSpec((tt, K), lambda i: (i, 0)), _full((K, N))] + extra_specs,
        out_shape=[jax.ShapeDtypeStruct((T, K), F32), jax.ShapeDtypeStruct((K, N), BF16)] + extra_shapes,
        scratch_shapes=[pltpu.VMEM((K, N), F32)],
        compiler_params=_cp("arbitrary"), name=name)(dy, a1, a2, w)


def _mm_tn(a, b, *, tt, tka, name):
    T, Ka = a.shape
    N = b.shape[1]
    last = T // tt - 1

    def body(a_ref, b_ref, o_ref, acc):
        @pl.when(pl.program_id(1) == 0)
        def _():
            acc[...] = jnp.zeros_like(acc)

        acc[...] += _dot_tn(a_ref[...], b_ref[...])

        @pl.when(pl.program_id(1) == last)
        def _():
            o_ref[...] = acc[...].astype(BF16)

    return pl.pallas_call(
        body, grid=(Ka // tka, T // tt),
        in_specs=[pl.BlockSpec((tt, tka), lambda j, t: (t, j)), pl.BlockSpec((tt, N), lambda j, t: (t, 0))],
        out_specs=pl.BlockSpec((tka, N), lambda j, t: (j, 0)),
        out_shape=jax.ShapeDtypeStruct((Ka, N), BF16),
        scratch_shapes=[pltpu.VMEM((tka, N), F32)],
        compiler_params=_cp("parallel", "arbitrary"), name=name)(a, b)


def _mm_tn_pieces(pieces, b, *, tt, name):
    n = len(pieces)
    T = b.shape[0]
    N = b.shape[1]
    widths = [p.shape[1] for p in pieces]
    Ka = sum(widths)
    last = T // tt - 1

    def body(*refs):
        p_refs = refs[:n]
        b_ref, o_ref, acc = refs[n:]

        @pl.when(pl.program_id(0) == 0)
        def _():
            acc[...] = jnp.zeros_like(acc)

        bv = b_ref[...].astype(BF16)
        off = 0
        for p_ref, wd in zip(p_refs, widths):
            acc[off:off + wd, :] += _dot_tn(p_ref[...], bv)
            off += wd

        @pl.when(pl.program_id(0) == last)
        def _():
            o_ref[...] = acc[...].astype(BF16)

    return pl.pallas_call(
        body, grid=(T // tt,),
        in_specs=[pl.BlockSpec((tt, wd), lambda t: (t, 0)) for wd in widths] + [pl.BlockSpec((tt, N), lambda t: (t, 0))],
        out_specs=_full((Ka, N)), out_shape=jax.ShapeDtypeStruct((Ka, N), BF16),
        scratch_shapes=[pltpu.VMEM((Ka, N), F32)],
        compiler_params=_cp("arbitrary"), name=name)(*pieces, b)


def _rms_bwd_dx(x, g, w, dy, dres, *, tt, wt, name, dep=None):
    pieces = list(dy) if isinstance(dy, (list, tuple)) else [dy]
    n = len(pieces)
    widths = [p.shape[1] for p in pieces]
    T, K = x.shape

    def kernel_body(x_ref, g_ref, w_ref, *rest):
        dy_refs = rest[:n]
        dres_ref, dx_ref, dg_ref = rest[n:]

        @pl.when(pl.program_id(0) == 0)
        def _():
            dg_ref[...] = jnp.zeros_like(dg_ref)

        if n == 1:
            dxn = (_dot if wt else _dot_nt)(dy_refs[0][...], w_ref[...])
        else:
            dxn, off = 0.0, 0
            for dy_ref, wd in zip(dy_refs, widths):
                dxn = dxn + _dot(dy_ref[...], w_ref[off:off + wd, :])
                off += wd
        xf = x_ref[...]
        r = lax.rsqrt(jnp.mean(xf * xf, axis=-1, keepdims=True) + EPS)
        xhat = xf * r
        dg_ref[...] += jnp.sum(dxn * xhat, axis=0, keepdims=True)
        dxhat = dxn * g_ref[...]
        dx_ref[...] = dres_ref[...] + r * (dxhat - xhat * jnp.mean(dxhat * xhat, axis=-1, keepdims=True))

    assert n == 1 or wt
    body, dep_specs, dep_args = _dep(kernel_body, 4 + n, dep)
    return pl.pallas_call(
        body, grid=(T // tt,),
        in_specs=[pl.BlockSpec((tt, K), lambda i: (i, 0)), _full((1, K)), _full(w.shape)]
        + [pl.BlockSpec((tt, wd), lambda i: (i, 0)) for wd in widths]
        + [pl.BlockSpec((tt, K), lambda i: (i, 0))] + dep_specs,
        out_specs=[pl.BlockSpec((tt, K), lambda i: (i, 0)), _full((1, K))],
        out_shape=[jax.ShapeDtypeStruct((T, K), F32), jax.ShapeDtypeStruct((1, K), F32)],
        compiler_params=_cp("arbitrary"), name=name)(x, g, w, *pieces, dres, *dep_args)


HGRN_TB = 512
HGRN_NCH = HGRN_TB // CHUNK
HGRN_HPB = 6


def _hgrn_chunk_fwd(q, z, lbv, tril01):
    sig = _sigmoid(z)
    f = lbv + (1.0 - lbv) * sig
    kk = 1.0 - f
    b = _dot3(tril01, jnp.log(f))
    bend = b[CHUNK - 1:CHUNK, :]
    sq = _sigmoid(q)
    eb = jnp.exp(b)
    emb = jnp.exp(-b)
    eo = jnp.exp(bend - b)
    dec = jnp.exp(bend)
    return sig, f, kk, sq, eb, emb, eo, dec


def _hgrn2_fwd(proj, lb, *, name):
    T = proj.shape[0]
    nT = T // HGRN_TB
    nC = T // CHUNK

    def body(q_ref, z_ref, v_ref, lb_ref, o_ref, st_ref, state):
        @pl.when(pl.program_id(1) == 0)
        def _():
            state[...] = jnp.zeros_like(state)

        row = lax.broadcasted_iota(jnp.int32, (CHUNK, CHUNK), 0)
        col = lax.broadcasted_iota(jnp.int32, (CHUNK, CHUNK), 1)
        causal = row >= col
        tril01 = causal.astype(BF16)

        def chunk(c, carry):
            rows = pl.ds(pl.multiple_of(c * CHUNK, CHUNK), CHUNK)
            for hh in range(HGRN_HPB):
                sl = slice(hh * HEAD_DIM, (hh + 1) * HEAD_DIM)
                q = q_ref[rows, sl]
                v = v_ref[rows, sl].astype(BF16)
                sig, f, kk, sq, eb, emb, eo, dec = _hgrn_chunk_fwd(q, z_ref[rows, sl], lb_ref[:, sl], tril01)
                qi = (q * sq * eb).astype(BF16)
                ki = (kk * emb).astype(BF16)
                ko = (kk * eo).astype(BF16)
                st = state[hh]
                att = jnp.where(causal, _dot_nt(qi, ki), 0.0)
                o_ref[rows, sl] = _dot(att, v) + _dot_nt(qi, st)
                st_ref[c, hh] = st
                state[hh] = st * dec + _dot_tn(v, ko)
            return carry

        lax.fori_loop(0, HGRN_NCH, chunk, 0)

    W = HGRN_HPB * HEAD_DIM
    nG = A_HEADS // HGRN_HPB
    hb = lambda off: pl.BlockSpec((HGRN_TB, W), lambda h, i: (i, off + h))
    return pl.pallas_call(
        body, grid=(nG, nT),
        in_specs=[hb(0), hb(nG), hb(2 * nG), pl.BlockSpec((1, W), lambda h, i: (0, h))],
        out_specs=[hb(0), pl.BlockSpec((HGRN_NCH, HGRN_HPB, HEAD_DIM, HEAD_DIM), lambda h, i: (i, h, 0, 0))],
        out_shape=[jax.ShapeDtypeStruct((T, A_WIDTH), F32), jax.ShapeDtypeStruct((nC, A_HEADS, HEAD_DIM, HEAD_DIM), F32)],
        scratch_shapes=[pltpu.VMEM((HGRN_HPB, HEAD_DIM, HEAD_DIM), F32)],
        compiler_params=_cp("parallel", "arbitrary"), name=name)(proj, proj, proj, lb)


def _hgrn2_bwd(proj, lb, st_all, do, *, name):
    T = proj.shape[0]
    nT = T // HGRN_TB

    def body(q_ref, z_ref, v_ref, lb_ref, st_ref, do_ref, dq_ref, dz_ref, dv_ref, dlb_ref, dstate):
        @pl.when(pl.program_id(1) == 0)
        def _():
            dstate[...] = jnp.zeros_like(dstate)
            dlb_ref[...] = jnp.zeros_like(dlb_ref)

        row = lax.broadcasted_iota(jnp.int32, (CHUNK, CHUNK), 0)
        col = lax.broadcasted_iota(jnp.int32, (CHUNK, CHUNK), 1)
        causal = row >= col
        tril01 = causal.astype(BF16)
        triu01 = (row <= col).astype(BF16)

        def chunk(cc, carry):
            c = HGRN_NCH - 1 - cc
            rows = pl.ds(pl.multiple_of(c * CHUNK, CHUNK), CHUNK)
            for hh in range(HGRN_HPB):
                sl = slice(hh * HEAD_DIM, (hh + 1) * HEAD_DIM)
                lbv = lb_ref[:, sl]
                q = q_ref[rows, sl]
                v = v_ref[rows, sl].astype(BF16)
                sig, f, kk, sq, eb, emb, eo, dec = _hgrn_chunk_fwd(q, z_ref[rows, sl], lbv, tril01)
                qi32 = q * sq * eb
                ki32 = kk * emb
                ko32 = kk * eo
                qi, ki, ko = qi32.astype(BF16), ki32.astype(BF16), ko32.astype(BF16)
                att = jnp.where(causal, _dot_nt(qi, ki), 0.0).astype(BF16)
                dout = do_ref[rows, sl].astype(BF16)
                st = st_ref[c, hh]
                dst = dstate[hh]
                dst16 = dst.astype(BF16)
                datt = jnp.where(causal, _dot_nt(dout, v), 0.0).astype(BF16)
                dqi = _dot(datt, ki) + _dot(dout, st)
                dki = _dot_tn(datt, qi)
                dv_ref[rows, sl] = (_dot_tn(att, dout) + _dot_nt(ko, dst16)).astype(BF16)
                dko = _dot(v, dst16)
                ddec = jnp.sum(dst * st, axis=0, keepdims=True)
                dstate[hh] = dst * dec + _dot_tn(dout, qi)
                dkk = dki * emb + dko * eo
                db = dqi * qi32 - dki * ki32 - dko * ko32
                dbend = jnp.sum(dko * ko32, axis=0, keepdims=True) + ddec * dec
                dlogf = _dot3(triu01, db) + dbend
                df = dlogf / f - dkk
                dz_ref[rows, sl] = (df * (1.0 - lbv) * sig * (1.0 - sig)).astype(BF16)
                dlb_ref[:, sl] += jnp.sum(df * (1.0 - sig), axis=0, keepdims=True)
                dq_ref[rows, sl] = (dqi * eb * (sq * (1.0 + q * (1.0 - sq)))).astype(BF16)
            return carry

        lax.fori_loop(0, HGRN_NCH, chunk, 0)

    W = HGRN_HPB * HEAD_DIM
    nG = A_HEADS // HGRN_HPB
    hb = lambda off: pl.BlockSpec((HGRN_TB, W), lambda h, i: (nT - 1 - i, off + h))
    hlb = pl.BlockSpec((1, W), lambda h, i: (0, h))
    o16 = jax.ShapeDtypeStruct((T, A_WIDTH), BF16)
    return pl.pallas_call(
        body, grid=(nG, nT),
        in_specs=[hb(0), hb(nG), hb(2 * nG), hlb,
                  pl.BlockSpec((HGRN_NCH, HGRN_HPB, HEAD_DIM, HEAD_DIM), lambda h, i: (nT - 1 - i, h, 0, 0)), hb(0)],
        out_specs=[hb(0), hb(0), hb(0), hlb],
        out_shape=[o16, o16, o16, jax.ShapeDtypeStruct((1, A_WIDTH), F32)],
        scratch_shapes=[pltpu.VMEM((HGRN_HPB, HEAD_DIM, HEAD_DIM), F32)],
        compiler_params=_cp("parallel", "arbitrary"), name=name)(proj, proj, proj, lb, st_all, do)


def _head_rms(x):
    r = lax.rsqrt(jnp.mean(x * x, axis=-1, keepdims=True) + EPS)
    return x * r, r


def _head_rms_bwd(dxhat, xhat, r):
    return r * (dxhat - xhat * jnp.mean(dxhat * xhat, axis=-1, keepdims=True))


def _a_post_fwd(o, proj, onorm, *, tt, name):
    T = o.shape[0]

    def body(o_ref, g_ref, w_ref, y_ref):
        for h in range(A_HEADS):
            sl = slice(h * HEAD_DIM, (h + 1) * HEAD_DIM)
            xhat, _ = _head_rms(o_ref[:, sl])
            g = g_ref[:, sl]
            y_ref[:, sl] = xhat * w_ref[:, sl] * (g * _sigmoid(g))

    blk = lambda c: pl.BlockSpec((tt, A_WIDTH), lambda i: (i, c))
    return pl.pallas_call(
        body, grid=(T // tt,), in_specs=[blk(0), blk(3), _full((1, A_WIDTH))], out_specs=blk(0),
        out_shape=jax.ShapeDtypeStruct((T, A_WIDTH), F32),
        compiler_params=_cp("parallel"), name=name)(o, proj, onorm)


def _a_post_bwd(o, proj, onorm, dmix, *, tt, name, dep=None):
    T = o.shape[0]

    def kernel_body(o_ref, g_ref, w_ref, dy_ref, do_ref, dg_ref, dw_ref):
        @pl.when(pl.program_id(0) == 0)
        def _():
            dw_ref[...] = jnp.zeros_like(dw_ref)

        for h in range(A_HEADS):
            sl = slice(h * HEAD_DIM, (h + 1) * HEAD_DIM)
            xhat, r = _head_rms(o_ref[:, sl])
            g = g_ref[:, sl]
            s = _sigmoid(g)
            dy = dy_ref[:, sl]
            w = w_ref[:, sl]
            dg_ref[:, sl] = (dy * xhat * w * (s * (1.0 + g * (1.0 - s)))).astype(BF16)
            dyn = dy * (g * s)
            dw_ref[:, sl] += jnp.sum(dyn * xhat, axis=0, keepdims=True)
            do_ref[:, sl] = _head_rms_bwd(dyn * w, xhat, r)

    blk = lambda c: pl.BlockSpec((tt, A_WIDTH), lambda i: (i, c))
    body, dep_specs, dep_args = _dep(kernel_body, 4, dep)
    return pl.pallas_call(
        body, grid=(T // tt,), in_specs=[blk(0), blk(3), _full((1, A_WIDTH)), blk(0)] + dep_specs,
        out_specs=[blk(0), blk(0), _full((1, A_WIDTH))],
        out_shape=[jax.ShapeDtypeStruct((T, A_WIDTH), F32), jax.ShapeDtypeStruct((T, A_WIDTH), BF16),
                   jax.ShapeDtypeStruct((1, A_WIDTH), F32)],
        compiler_params=_cp("arbitrary"), name=name)(o, proj, onorm, dmix, *dep_args)


def _mem_head_masks(n):
    lane = lax.broadcasted_iota(jnp.int32, (n, MEM_WIDTH), 1)
    return [(lane >= m * MEM_HEAD_DIM) & (lane < (m + 1) * MEM_HEAD_DIM) for m in range(MEM_HEADS)]


def _mem_head_rms(x, masks):
    x2 = x * x
    r = jnp.zeros_like(x)
    for mk in masks:
        ms = jnp.sum(jnp.where(mk, x2, 0.0), axis=-1, keepdims=True) * (1.0 / MEM_HEAD_DIM)
        r = jnp.where(mk, lax.rsqrt(ms + EPS), r)
    return x * r, r


def _mem_head_rms_bwd(dxhat, xhat, r, masks):
    t = dxhat * xhat
    m = jnp.zeros_like(t)
    for mk in masks:
        m = jnp.where(mk, jnp.sum(jnp.where(mk, t, 0.0), axis=-1, keepdims=True) * (1.0 / MEM_HEAD_DIM), m)
    return r * (dxhat - xhat * m)


MEM_SCALE = MEM_HEAD_DIM ** -0.5


def _mem_attn_fwd(proj, qcol, mkv, qn_w, kn_w, *, tt, name):
    T = proj.shape[0]

    def body(q_ref, k_ref, v_ref, qw_ref, kw_ref, o_ref):
        qmasks = _mem_head_masks(tt)
        kmasks = _mem_head_masks(MEM_TOKENS)
        qhat, _ = _mem_head_rms(q_ref[...], qmasks)
        qn = qhat * qw_ref[...]
        khat, _ = _mem_head_rms(k_ref[...], kmasks)
        kn = (khat * kw_ref[...]).astype(BF16)
        v = v_ref[...].astype(BF16)
        out = jnp.zeros((tt, MEM_WIDTH), F32)
        for m in range(MEM_HEADS):
            s = _dot_nt(jnp.where(qmasks[m], qn, 0.0), kn) * MEM_SCALE
            s = s - jnp.max(s, axis=-1, keepdims=True)
            p = jnp.exp(s)
            p = p / jnp.sum(p, axis=-1, keepdims=True)
            out = jnp.where(qmasks[m], _dot(p, v), out)
        o_ref[...] = out

    return pl.pallas_call(
        body, grid=(T // tt,),
        in_specs=[pl.BlockSpec((tt, MEM_WIDTH), lambda i: (i, qcol)), pl.BlockSpec((MEM_TOKENS, MEM_WIDTH), lambda i: (0, 0)),
                  pl.BlockSpec((MEM_TOKENS, MEM_WIDTH), lambda i: (0, 1)), _full((1, MEM_WIDTH)), _full((1, MEM_WIDTH))],
        out_specs=pl.BlockSpec((tt, MEM_WIDTH), lambda i: (i, 0)),
        out_shape=jax.ShapeDtypeStruct((T, MEM_WIDTH), F32),
        compiler_params=_cp("parallel"), name=name)(proj, mkv, mkv, qn_w, kn_w)


def _mem_attn_bwd(proj, qcol, mkv, qn_w, kn_w, dmix, *, tt, name):
    T = proj.shape[0]
    nsteps = T // tt
    ocol = (dmix.shape[1] - MEM_WIDTH) // MEM_WIDTH

    def body(q_ref, k_ref, v_ref, qw_ref, kw_ref, do_ref, dq_ref, dkv_ref, dqw_ref, dkw_ref, dk_acc, dv_acc):
        step = pl.program_id(0)

        @pl.when(step == 0)
        def _():
            dk_acc[...] = jnp.zeros_like(dk_acc)
            dv_acc[...] = jnp.zeros_like(dv_acc)
            dqw_ref[...] = jnp.zeros_like(dqw_ref)

        qmasks = _mem_head_masks(tt)
        kmasks = _mem_head_masks(MEM_TOKENS)
        qhat, qr = _mem_head_rms(q_ref[...], qmasks)
        qn = qhat * qw_ref[...]
        khat, kr = _mem_head_rms(k_ref[...], kmasks)
        kn = (khat * kw_ref[...]).astype(BF16)
        v = v_ref[...].astype(BF16)
        dout = do_ref[...]
        dqn = jnp.zeros((tt, MEM_WIDTH), F32)
        dkn = jnp.zeros((MEM_TOKENS, MEM_WIDTH), F32)
        dvv = jnp.zeros((MEM_TOKENS, MEM_WIDTH), F32)
        for m in range(MEM_HEADS):
            qm = jnp.where(qmasks[m], qn, 0.0).astype(BF16)
            s = _dot_nt(qm, kn) * MEM_SCALE
            s = s - jnp.max(s, axis=-1, keepdims=True)
            p = jnp.exp(s)
            p = p / jnp.sum(p, axis=-1, keepdims=True)
            dom = jnp.where(qmasks[m], dout, 0.0).astype(BF16)
            dp = _dot_nt(dom, v)
            ds = (p * (dp - jnp.sum(p * dp, axis=-1, keepdims=True)) * MEM_SCALE).astype(BF16)
            dqn = jnp.where(qmasks[m], _dot(ds, kn), dqn)
            dkn = jnp.where(kmasks[m], _dot_tn(ds, qm), dkn)
            dvv = jnp.where(kmasks[m], _dot_tn(p, dom), dvv)
        dqw_ref[...] += jnp.sum(dqn * qhat, axis=0, keepdims=True)
        dq_ref[...] = _mem_head_rms_bwd(dqn * qw_ref[...], qhat, qr, qmasks).astype(BF16)
        dk_acc[...] += dkn
        dv_acc[...] += dvv

        @pl.when(step == nsteps - 1)
        def _():
            dk = dk_acc[...]
            dkw_ref[...] = jnp.sum(dk * khat, axis=0, keepdims=True)
            dkv_ref[:, :MEM_WIDTH] = _mem_head_rms_bwd(dk * kw_ref[...], khat, kr, kmasks)
            dkv_ref[:, MEM_WIDTH:] = dv_acc[...]

    return pl.pallas_call(
        body, grid=(nsteps,),
        in_specs=[pl.BlockSpec((tt, MEM_WIDTH), lambda i: (i, qcol)), pl.BlockSpec((MEM_TOKENS, MEM_WIDTH), lambda i: (0, 0)),
                  pl.BlockSpec((MEM_TOKENS, MEM_WIDTH), lambda i: (0, 1)), _full((1, MEM_WIDTH)), _full((1, MEM_WIDTH)),
                  pl.BlockSpec((tt, MEM_WIDTH), lambda i: (i, ocol))],
        out_specs=[pl.BlockSpec((tt, MEM_WIDTH), lambda i: (i, 0)), _full((MEM_TOKENS, 2 * MEM_WIDTH)),
                   _full((1, MEM_WIDTH)), _full((1, MEM_WIDTH))],
        out_shape=[jax.ShapeDtypeStruct((T, MEM_WIDTH), BF16), jax.ShapeDtypeStruct((MEM_TOKENS, 2 * MEM_WIDTH), F32),
                   jax.ShapeDtypeStruct((1, MEM_WIDTH), F32), jax.ShapeDtypeStruct((1, MEM_WIDTH), F32)],
        scratch_shapes=[pltpu.VMEM((MEM_TOKENS, MEM_WIDTH), F32), pltpu.VMEM((MEM_TOKENS, MEM_WIDTH), F32)],
        compiler_params=_cp("arbitrary"), name=name)(proj, mkv, mkv, qn_w, kn_w, dmix)


HALF = HEAD_DIM // 2
ATT_SCALE = HEAD_DIM ** -0.5
NEG = -1e30


def _rope_tables(T):
    inv = ROPE_THETA ** (-jnp.arange(HALF, dtype=F32) / HALF)
    ang = jnp.arange(T, dtype=F32)[:, None] * inv[None, :]
    cos, sin = jnp.cos(ang), jnp.sin(ang)
    return jnp.concatenate([cos, cos], axis=-1), jnp.concatenate([-sin, sin], axis=-1)


def _rope(x, cosf, sinsg):
    return x * cosf + pltpu.roll(x, HALF, 1) * sinsg


def _rope_bwd(dy, cosf, sinsg):
    return dy * cosf + pltpu.roll(dy * sinsg, HALF, 1)


def _q_prep_bwd(proj, w_heads, cosf, sinsg, dqs, *, tt, name):
    T = proj.shape[0]
    W = N_GROUPS * B_WIDTH

    def body(x_ref, w_ref, c_ref, s_ref, d0, d1, d2, dx_ref, dw_ref):
        @pl.when(pl.program_id(0) == 0)
        def _():
            dw_ref[...] = jnp.zeros_like(dw_ref)

        c, s = c_ref[...], s_ref[...]
        for gi, d_ref in enumerate((d0, d1, d2)):
            for h in range(B_HEADS):
                sl = slice((gi * B_HEADS + h) * HEAD_DIM, (gi * B_HEADS + h + 1) * HEAD_DIM)
                xhat, r = _head_rms(x_ref[:, sl])
                dyn = _rope_bwd(d_ref[:, h * HEAD_DIM:(h + 1) * HEAD_DIM], c, s)
                dw_ref[:, sl] += jnp.sum(dyn * xhat, axis=0, keepdims=True)
                dx_ref[:, sl] = _head_rms_bwd(dyn * w_ref[:, sl], xhat, r).astype(BF16)

    tbl = pl.BlockSpec((tt, HEAD_DIM), lambda i: (i, 0))
    dyb = pl.BlockSpec((tt, B_WIDTH), lambda i: (i, 0))
    return pl.pallas_call(
        body, grid=(T // tt,),
        in_specs=[pl.BlockSpec((tt, W), lambda i: (i, 0)), _full((1, W)), tbl, tbl, dyb, dyb, dyb],
        out_specs=[pl.BlockSpec((tt, W), lambda i: (i, 0)), _full((1, W))],
        out_shape=[jax.ShapeDtypeStruct((T, W), BF16), jax.ShapeDtypeStruct((1, W), F32)],
        compiler_params=_cp("arbitrary"), name=name)(proj, w_heads, cosf, sinsg, *dqs)


def _kv_prep_bwd(kv, w_heads, cosf, sinsg, dks, dvs, *, tt, name):
    T = kv.shape[0]

    def body(x_ref, w_ref, c_ref, s_ref, k0, k1, k2, v0, v1, v2, dx_ref, dw_ref):
        @pl.when(pl.program_id(0) == 0)
        def _():
            dw_ref[...] = jnp.zeros_like(dw_ref)

        c, s = c_ref[...], s_ref[...]
        for h in range(B_HEADS):
            sl = slice(h * HEAD_DIM, (h + 1) * HEAD_DIM)
            vs = slice(B_WIDTH + h * HEAD_DIM, B_WIDTH + (h + 1) * HEAD_DIM)
            xhat, r = _head_rms(x_ref[:, sl])
            dyn = _rope_bwd(k0[:, sl] + k1[:, sl] + k2[:, sl], c, s)
            dw_ref[:, sl] += jnp.sum(dyn * xhat, axis=0, keepdims=True)
            dx_ref[:, sl] = _head_rms_bwd(dyn * w_ref[:, sl], xhat, r).astype(BF16)
            dx_ref[:, vs] = (v0[:, sl] + v1[:, sl] + v2[:, sl]).astype(BF16)

    tbl = pl.BlockSpec((tt, HEAD_DIM), lambda i: (i, 0))
    dyb = pl.BlockSpec((tt, B_WIDTH), lambda i: (i, 0))
    return pl.pallas_call(
        body, grid=(T // tt,),
        in_specs=[dyb, _full((1, B_WIDTH)), tbl, tbl] + [dyb] * 6,
        out_specs=[pl.BlockSpec((tt, 2 * B_WIDTH), lambda i: (i, 0)), _full((1, B_WIDTH))],
        out_shape=[jax.ShapeDtypeStruct((T, 2 * B_WIDTH), BF16), jax.ShapeDtypeStruct((1, B_WIDTH), F32)],
        compiler_params=_cp("arbitrary"), name=name)(kv, w_heads, cosf, sinsg, *dks, *dvs)


def _band_masks(n_is_first=None):
    row = lax.broadcasted_iota(jnp.int32, (SPAN, SPAN), 0)
    col = lax.broadcasted_iota(jnp.int32, (SPAN, SPAN), 1)
    return row >= col, col >= row


def _dil_views(T, d):
    L = T // d
    return L, L // SPAN


def _dil_fwd(qr, kr, kv, gi, d, *, name):
    T = qr.shape[0]
    L, nb = _dil_views(T, d)

    def body(q_ref, kc_ref, kp_ref, vc_ref, vp_ref, o_ref, lse_ref):
        cur_ok, prev_band = _band_masks()
        prev_ok = prev_band & (pl.program_id(1) > 0)
        for h in range(B_HEADS):
            sl = slice(h * HEAD_DIM, (h + 1) * HEAD_DIM)
            q = q_ref[:, sl]
            sc = jnp.where(cur_ok, _dot_nt(q, kc_ref[:, sl]) * ATT_SCALE, NEG)
            sp = jnp.where(prev_ok, _dot_nt(q, kp_ref[:, sl]) * ATT_SCALE, NEG)
            m = jnp.maximum(jnp.max(sc, axis=-1, keepdims=True), jnp.max(sp, axis=-1, keepdims=True))
            pc = jnp.exp(sc - m)
            pp = jnp.exp(sp - m)
            l = jnp.sum(pc, axis=-1, keepdims=True) + jnp.sum(pp, axis=-1, keepdims=True)
            o_ref[:, sl] = (_dot(pc, vc_ref[:, sl]) + _dot(pp, vp_ref[:, sl])) / l
            lse_ref[:, sl] = jnp.broadcast_to(m + jnp.log(l), (SPAN, HEAD_DIM))

    blk = lambda f: pl.BlockSpec((SPAN, B_WIDTH), f)
    cur = lambda r, n: (n, r)
    prev = lambda r, n: (jnp.maximum(n - 1, 0), r)
    ov = jax.ShapeDtypeStruct((L, d * B_WIDTH), F32)
    o, lse = pl.pallas_call(
        body, grid=(d, nb),
        in_specs=[blk(lambda r, n: (n, r * N_GROUPS + gi)), blk(cur), blk(prev),
                  blk(lambda r, n: (n, 2 * r + 1)), blk(lambda r, n: (jnp.maximum(n - 1, 0), 2 * r + 1))],
        out_specs=[blk(cur), blk(cur)], out_shape=[ov, ov],
        compiler_params=_cp("parallel", "arbitrary"), name=name,
    )(qr.reshape(L, d * N_GROUPS * B_WIDTH), kr.reshape(L, d * B_WIDTH), kr.reshape(L, d * B_WIDTH),
      kv.reshape(L, d * 2 * B_WIDTH), kv.reshape(L, d * 2 * B_WIDTH))
    return o.reshape(T, B_WIDTH), lse.reshape(T, B_WIDTH)


def _dil_combine_fwd(os_, lses, *, tt, name):
    T = os_[0].shape[0]

    def body(o0, o1, o2, l0, l1, l2, y_ref, lse_ref):
        a, b, c = l0[...], l1[...], l2[...]
        m = jnp.maximum(jnp.maximum(a, b), c)
        wa, wb, wc = jnp.exp(a - m), jnp.exp(b - m), jnp.exp(c - m)
        den = wa + wb + wc
        y_ref[...] = (wa * o0[...] + wb * o1[...] + wc * o2[...]) / den
        lse_ref[...] = m + jnp.log(den)

    blk = pl.BlockSpec((tt, B_WIDTH), lambda i: (i, 0))
    sh = jax.ShapeDtypeStruct((T, B_WIDTH), F32)
    return pl.pallas_call(
        body, grid=(T // tt,), in_specs=[blk] * 6, out_specs=[blk, blk], out_shape=[sh, sh],
        compiler_params=_cp("parallel"), name=name)(*os_, *lses)


DILS_UNROLL = 4


def _dils_specs(gi, d, nblk):
    blk = lambda f: pl.BlockSpec((SPAN * d, HEAD_DIM), f)
    return {
        "q": blk(lambda h, n: (n, gi * B_HEADS + h)), "q_next": blk(lambda h, n: (jnp.minimum(n + 1, nblk - 1), gi * B_HEADS + h)),
        "cur": blk(lambda h, n: (n, h)), "prev": blk(lambda h, n: (jnp.maximum(n - 1, 0), h)),
        "next": blk(lambda h, n: (jnp.minimum(n + 1, nblk - 1), h)),
        "v": blk(lambda h, n: (n, B_HEADS + h)), "v_prev": blk(lambda h, n: (jnp.maximum(n - 1, 0), B_HEADS + h)),
    }


def _dils_fwd(qr, kr, kv, gi, d, *, name):
    T = qr.shape[0]
    nblk = T // (SPAN * d)
    sp = _dils_specs(gi, d, nblk)

    def body(q_ref, kc_ref, kp_ref, vc_ref, vp_ref, o_ref, lse_ref):
        cur_ok, prev_band = _band_masks()
        prev_ok = prev_band & (pl.program_id(1) > 0)

        def residue(r, carry):
            rows = pl.ds(r, SPAN, stride=d)
            q = q_ref[rows, :]
            sc = jnp.where(cur_ok, _dot_nt(q, kc_ref[rows, :]) * ATT_SCALE, NEG)
            sp_ = jnp.where(prev_ok, _dot_nt(q, kp_ref[rows, :]) * ATT_SCALE, NEG)
            m = jnp.maximum(jnp.max(sc, axis=-1, keepdims=True), jnp.max(sp_, axis=-1, keepdims=True))
            pc = jnp.exp(sc - m)
            pp = jnp.exp(sp_ - m)
            l = jnp.sum(pc, axis=-1, keepdims=True) + jnp.sum(pp, axis=-1, keepdims=True)
            o_ref[rows, :] = (_dot(pc, vc_ref[rows, :]) + _dot(pp, vp_ref[rows, :])) / l
            lse_ref[rows, :] = jnp.broadcast_to(m + jnp.log(l), (SPAN, HEAD_DIM))
            return carry

        lax.fori_loop(0, d, residue, 0, unroll=DILS_UNROLL)

    sh = jax.ShapeDtypeStruct((T, B_WIDTH), F32)
    return pl.pallas_call(
        body, grid=(B_HEADS, nblk), in_specs=[sp["q"], sp["cur"], sp["prev"], sp["v"], sp["v_prev"]],
        out_specs=[sp["cur"], sp["cur"]], out_shape=[sh, sh],
        compiler_params=_cp("parallel", "arbitrary"), name=name)(qr, kr, kr, kv, kv)


DIL_BWD_GROUP = {1: 4, 4: 1, 16: 1}


def _dil_bwd(qr, kr, kv, dmix, lse, dd, gi, d, *, name, dep=None):
    T = qr.shape[0]
    G = DIL_BWD_GROUP[d]
    band = SPAN * d
    tb = G * band
    nblk = T // tb

    def kernel_body(q_ref, dy_ref, lse_ref, dd_ref, kc_ref, kp_ref, vc_ref, vp_ref, dq_ref, dk_ref, dv_ref):
        n = pl.program_id(1)

        @pl.when(n == 0)
        def _():
            dk_ref[...] = jnp.zeros_like(dk_ref)
            dv_ref[...] = jnp.zeros_like(dv_ref)

        cur_ok, prev_band = _band_masks()
        base = pl.multiple_of(n * tb, SPAN)
        for j in range(G):
            def residue(r, carry, j=j):
                off = j * band + r
                rows = pl.ds(off, SPAN, stride=d)
                q, dy = q_ref[rows, :], dy_ref[rows, :]
                lse_h = jnp.max(lse_ref[rows, :], axis=-1, keepdims=True)
                dd_h = jnp.max(dd_ref[rows, :], axis=-1, keepdims=True)
                kc, vc = kc_ref[rows, :], vc_ref[rows, :]
                if j > 0:
                    before = pl.ds(off - band, SPAN, stride=d)
                    kp, vp = kc_ref[before, :], vc_ref[before, :]
                    prev_ok = prev_band
                else:
                    before = pl.ds((G - 1) * band + r, SPAN, stride=d)
                    kp, vp = kp_ref[before, :], vp_ref[before, :]
                    prev_ok = prev_band & (n > 0)
                pc = jnp.exp(jnp.where(cur_ok, _dot_nt(q, kc) * ATT_SCALE, NEG) - lse_h)
                pp = jnp.exp(jnp.where(prev_ok, _dot_nt(q, kp) * ATT_SCALE, NEG) - lse_h)
                dsc = pc * (_dot_nt(dy, vc) - dd_h) * ATT_SCALE
                dsp = pp * (_dot_nt(dy, vp) - dd_h) * ATT_SCALE
                dq_ref[rows, :] = _dot(dsc, kc) + _dot(dsp, kp)
                here = pl.ds(base + off, SPAN, stride=d)
                dk_ref[here, :] += _dot_tn(dsc, q)
                dv_ref[here, :] += _dot_tn(pc, dy)
                there = pl.ds(jnp.maximum(base + off - band, r), SPAN, stride=d)
                dk_ref[there, :] += _dot_tn(dsp, q)
                dv_ref[there, :] += _dot_tn(pp, dy)
                return carry

            lax.fori_loop(0, d, residue, 0, unroll=min(d, DILS_UNROLL))

    blk = lambda f: pl.BlockSpec((tb, HEAD_DIM), f)
    cur = lambda h, n: (n, h)
    prev = lambda h, n: (jnp.maximum(n - 1, 0), h)
    whole = pl.BlockSpec((T, HEAD_DIM), lambda h, n: (0, h))
    sh = jax.ShapeDtypeStruct((T, B_WIDTH), F32)
    body, dep_specs, dep_args = _dep(kernel_body, 8, dep)
    return pl.pallas_call(
        body, grid=(B_HEADS, nblk),
        in_specs=[blk(lambda h, n: (n, gi * B_HEADS + h)), blk(cur), blk(cur), blk(cur), blk(cur), blk(prev),
                  blk(lambda h, n: (n, B_HEADS + h)), blk(lambda h, n: (jnp.maximum(n - 1, 0), B_HEADS + h))] + dep_specs,
        out_specs=[blk(cur), whole, whole], out_shape=[sh, sh, sh],
        compiler_params=_cp("parallel", "arbitrary"), name=name)(qr, dmix, lse, dd, kr, kr, kv, kv, *dep_args)


A_MQ_COL = 4 * A_WIDTH // MEM_WIDTH
B_MQ_COL = N_GROUPS * B_WIDTH // MEM_WIDTH


def _row(v):
    return v.reshape(1, -1).astype(F32)


def _local_step(x, mem, tgt, get_w, P, put_g, first_dep=None, forward_point=lambda i, value: value):
    T = x.shape[0]
    cosf, sinsg = _rope_tables(T)
    lb_soft = jax.nn.softmax(P["a_lb_logits"].astype(F32), axis=0)
    lb = lb_soft[0:1]
    qw_heads = jnp.repeat(P["b_qnorm"][0], B_HEADS, axis=0).reshape(1, -1)
    kw_heads = jnp.tile(_row(P["b_knorm"]), (1, B_HEADS))
    mqw = [jnp.tile(_row(P["mem_qnorm"][l]), (1, MEM_HEADS)) for l in range(2)]
    mkw = [jnp.tile(_row(P["mem_knorm"][l]), (1, MEM_HEADS)) for l in range(2)]
    nmix = [_row(P["norm_mix"][l]) for l in range(2)]
    nffn = [_row(P["norm_ffn"][l]) for l in range(2)]
    mnorm = [_row(P["mem_norm"][l]) for l in range(2)]
    kvn = _row(P["kv_norm"])
    onorm = _row(P["a_onorm"])
    W = {}

    def w_of(name, after=None):
        if name not in W:
            W[name] = get_w(name, after)
        return W[name]

    proj_a, xn0 = _rms_matmul(x, nmix[0], w_of("a_w_in"), tt=512, tn=1664, wt=True, name="proj_a", dep=first_dep)
    mkv0, mn0 = _rms_matmul(mem, mnorm[0], w_of("w_mem_kv0"), tt=MEM_TOKENS, tn=2 * MEM_WIDTH, wt=False, name="mem_kv0")
    o_raw, st = _hgrn2_fwd(proj_a, lb, name="hgrn2_fwd")
    o_raw = forward_point(0, o_raw)
    mm0 = _a_post_fwd(o_raw, proj_a, onorm, tt=512, name="a_post_fwd")
    mo0 = _mem_attn_fwd(proj_a, A_MQ_COL, mkv0, mqw[0], mkw[0], tt=512, name="mem_attn_fwd0")
    hm0 = _mm_res(x, mm0, mo0, w_of("w_out0", mo0), tt=512, name="out_proj0")
    hm0 = forward_point(1, hm0)
    gu0, hn0 = _rms_matmul(hm0, nffn[0], w_of("w_gate_up0", hm0), tt=512, tn=1408, wt=True, out_dtype=BF16, name="gate_up0")
    h1 = _swiglu_down(hm0, gu0, w_of("w_down0", gu0), tt=512, name="down0")
    h1 = forward_point(2, h1)
    kv, hkn, kr = _rms_matmul(h1, kvn, w_of("w_kv", h1), tt=512, tn=768, wt=True, name="kv_proj",
                              rotate=(kw_heads, cosf, sinsg))

    proj_b, xn1, qr = _rms_matmul(h1, nmix[1], w_of("b_w_in", kr), tt=512, tn=1280, wt=True, name="proj_b",
                                  rotate=(qw_heads, cosf, sinsg))
    proj_b = forward_point(3, proj_b)
    mkv1, mn1 = _rms_matmul(mem, mnorm[1], w_of("w_mem_kv1", kr), tt=MEM_TOKENS, tn=2 * MEM_WIDTH, wt=False, name="mem_kv1")
    outs = [(_dil_fwd if d == 1 else _dils_fwd)(qr, kr, kv, gi, d, name=f"dil_fwd{gi}") for gi, d in enumerate(DILATIONS)]
    mm1, lse_tot = _dil_combine_fwd([o for o, _ in outs], [s for _, s in outs], tt=512, name="dil_combine")
    mo1 = _mem_attn_fwd(proj_b, B_MQ_COL, mkv1, mqw[1], mkw[1], tt=512, name="mem_attn_fwd1")
    hm1 = _mm_res(h1, mm1, mo1, w_of("w_out1", mo1), tt=512, name="out_proj1")
    gu1, hn1 = _rms_matmul(hm1, nffn[1], w_of("w_gate_up1", hm1), tt=512, tn=1408, wt=True, out_dtype=BF16, name="gate_up1")
    dy, sq = _swiglu_down_loss(hm1, gu1, w_of("w_down1", gu1), tgt, tt=512, name="down1_loss")

    gP = {}
    zeros_mem = jnp.zeros((MEM_TOKENS, D_MODEL), F32)

    def ffn_bwd(l, dh, hm, gu, hn):
        dgu, g_wd = _swiglu_bwd(dh, gu, w_of(f"w_down{l}"), tt=256, name=f"swiglu_bwd{l}")
        g_wgu = _mm_tn(dgu, hn, tt=512, tka=1408, name=f"g_w_gate_up{l}")
        sent = put_g({f"w_down{l}": g_wd, f"w_gate_up{l}": g_wgu})
        dhm, g_nf = _rms_bwd_dx(hm, nffn[l], w_of(f"w_gate_up{l}"), dgu, dh, tt=512, wt=True, name=f"gate_up_bwd{l}", dep=sent)
        return dhm, g_nf

    def mix_bwd(l, dhm, mix_main, mix_mem, proj, qcol, mkv, mn):
        dmix, g_wout, *head_dots = _out_proj_bwd(dhm, mix_main, mix_mem, w_of(f"w_out{l}"), tt=512, name=f"out_proj_bwd{l}",
                                                 head_dots=l == 1)
        dmq, dmkv, dqw, dkw = _mem_attn_bwd(proj, qcol, mkv, mqw[l], mkw[l], dmix, tt=512, name=f"mem_attn_bwd{l}")
        g_wmkv = _mm_tn(mn, dmkv, tt=MEM_TOKENS, tka=512, name=f"g_w_mem_kv{l}")
        sent = put_g({f"w_out{l}": g_wout, f"w_mem_kv{l}": g_wmkv})
        _, g_mn = _rms_bwd_dx(mem, mnorm[l], w_of(f"w_mem_kv{l}"), dmkv, zeros_mem, tt=MEM_TOKENS, wt=False, name=f"mem_kv_bwd{l}")
        fold = lambda v: v.reshape(MEM_HEADS, MEM_HEAD_DIM).sum(axis=0)
        return dmix, dmq, g_mn, fold(dqw), fold(dkw), sent, head_dots

    dhm1, g_nf1 = ffn_bwd(1, dy, hm1, gu1, hn1)
    dmix1, dmq1, g_mn1, g_mq1, g_mk1, sent, (dd,) = mix_bwd(1, dhm1, mm1, mo1, proj_b, B_MQ_COL, mkv1, mn1)
    dqs, dks, dvs = [], [], []
    for gi, d in enumerate(DILATIONS):
        dq_g, dk_g, dv_g = _dil_bwd(qr, kr, kv, dmix1, lse_tot, dd, gi, d, name=f"dil_bwd{gi}", dep=sent if gi == 0 else None)
        dqs.append(dq_g)
        dks.append(dk_g)
        dvs.append(dv_g)
    dq_raw, dqw = _q_prep_bwd(proj_b, qw_heads, cosf, sinsg, dqs, tt=512, name="q_prep_bwd")
    dkv, dkw = _kv_prep_bwd(kv, kw_heads, cosf, sinsg, dks, dvs, tt=512, name="kv_prep_bwd")
    dproj_b = [dq_raw, dmq1]
    g_wb = _mm_tn_pieces(dproj_b, xn1, tt=512, name="g_b_w_in")
    g_wkv = _mm_tn(dkv, hkn, tt=512, tka=768, name="g_w_kv")
    sent = put_g({"b_w_in": g_wb, "w_kv": g_wkv})
    dh1, g_nm1 = _rms_bwd_dx(h1, nmix[1], w_of("b_w_in"), dproj_b, dhm1, tt=512, wt=True, name="proj_b_bwd", dep=sent)
    dh1, g_kvn = _rms_bwd_dx(h1, kvn, w_of("w_kv"), dkv, dh1, tt=512, wt=True, name="kv_proj_bwd")

    dhm0, g_nf0 = ffn_bwd(0, dh1, hm0, gu0, hn0)
    dmix0, dmq0, g_mn0, g_mq0, g_mk0, sent, _ = mix_bwd(0, dhm0, mm0, mo0, proj_a, A_MQ_COL, mkv0, mn0)
    do_raw, dg, g_onorm = _a_post_bwd(o_raw, proj_a, onorm, dmix0, tt=512, name="a_post_bwd", dep=sent)
    dq, dz, dv, dlb = _hgrn2_bwd(proj_a, lb, st, do_raw, name="hgrn2_bwd")
    dproj_a = [dq, dz, dv, dg, dmq0]
    sent = put_g({"a_w_in": _mm_tn_pieces(dproj_a, xn0, tt=512, name="g_a_w_in")})
    gx, g_nm0 = _rms_bwd_dx(x, nmix[0], w_of("a_w_in"), dproj_a, dhm0, tt=512, wt=True, name="proj_a_bwd", dep=sent)

    dl0 = lb_soft[0:1] * lb_soft[1:2] * dlb
    gP["a_lb_logits"] = jnp.concatenate([dl0, -dl0], axis=0)
    gP["a_onorm"] = g_onorm
    gP["norm_mix"] = jnp.concatenate([g_nm0, g_nm1], axis=0)
    gP["norm_ffn"] = jnp.concatenate([g_nf0, g_nf1], axis=0)
    gP["b_qnorm"] = dqw.reshape(N_GROUPS, B_HEADS, HEAD_DIM).sum(axis=1)[None]
    gP["kv_norm"] = g_kvn.reshape(-1)
    gP["b_knorm"] = dkw.reshape(B_HEADS, HEAD_DIM).sum(axis=0)
    gP["mem_norm"] = jnp.concatenate([g_mn0, g_mn1], axis=0)
    gP["mem_qnorm"] = jnp.stack([g_mq0, g_mq1])
    gP["mem_knorm"] = jnp.stack([g_mk0, g_mk1])
    return sq, gx, gP


MESH_ID = pl.DeviceIdType.MESH
HBM_SPEC = pl.BlockSpec(memory_space=pltpu.HBM)


def _position():
    return lax.axis_index("x"), lax.axis_index("y"), lax.axis_index("c")


def _all_gather(blocks, *, name):
    n = len(blocks)

    def body(*refs):
        x_refs, out_refs = refs[:n], refs[n:2 * n]
        send_sems, recv_sems, local_sems = refs[2 * n:]
        x, y, c = _position()
        me, sibling = (x, y, c), (x, y, 1 - c)
        chips = [(1 - x, y), (x, 1 - y), (1 - x, 1 - y)]

        def slot(a, px, py, pc):
            return out_refs[a].at[4 * px + 2 * py + pc]

        def copy(a, k, blk, to, src=None):
            return pltpu.make_async_remote_copy(
                src_ref=slot(a, *blk) if src is None else src, dst_ref=slot(a, *blk),
                send_sem=send_sems.at[7 * a + k], recv_sem=recv_sems.at[7 * a + k], device_id=to, device_id_type=MESH_ID)

        mine = [pltpu.make_async_copy(x_refs[a], slot(a, *me), local_sems.at[a]) for a in range(n)]
        for cp in mine:
            cp.start()
        first = []
        for a in range(n):
            first.append(copy(a, 0, me, sibling, src=x_refs[a]))
            first += [copy(a, 1 + j, me, (*chip, c), src=x_refs[a]) for j, chip in enumerate(chips)]
        for cp in first:
            cp.start()
        passed = []
        for j, chip in enumerate(chips):
            for a in range(n):
                copy(a, 1 + j, (*chip, c), me).wait_recv()
                cp = copy(a, 4 + j, (*chip, c), sibling)
                cp.start()
                passed.append(cp)
        for a in range(n):
            copy(a, 0, sibling, me).wait_recv()
            for j, chip in enumerate(chips):
                copy(a, 4 + j, (*chip, 1 - c), me).wait_recv()
        for cp in first + passed:
            cp.wait_send()
        for cp in mine:
            cp.wait()

    return pl.pallas_call(
        body, out_shape=[jax.ShapeDtypeStruct((N_DEV,) + b.shape, b.dtype) for b in blocks],
        in_specs=[HBM_SPEC] * n, out_specs=[HBM_SPEC] * n,
        scratch_shapes=[pltpu.SemaphoreType.DMA((7 * n,)), pltpu.SemaphoreType.DMA((7 * n,)), pltpu.SemaphoreType.DMA((n,))],
        name=name)(*blocks)


def _all_gather_direct(block, after, *, name):
    def body(x_ref, after_ref, out_ref, send_sems, recv_sems, local_sem):
        x, y, c = _position()
        me = 4 * x + 2 * y + c
        mine = pltpu.make_async_copy(x_ref, out_ref.at[me], local_sem)
        mine.start()
        copies = []
        for k in ALL_PEERS:
            cp = pltpu.make_async_remote_copy(
                src_ref=x_ref, dst_ref=out_ref.at[me], send_sem=send_sems.at[k - 1], recv_sem=recv_sems.at[k - 1],
                device_id=_peer(k, x, y, c), device_id_type=MESH_ID)
            cp.start()
            copies.append(cp)
        for cp in copies:
            cp.wait()
        mine.wait()

    return pl.pallas_call(
        body, out_shape=jax.ShapeDtypeStruct((N_DEV,) + block.shape, block.dtype),
        in_specs=[HBM_SPEC, pl.BlockSpec(memory_space=pl.ANY)], out_specs=HBM_SPEC,
        scratch_shapes=[pltpu.SemaphoreType.DMA((7,)), pltpu.SemaphoreType.DMA((7,)), pltpu.SemaphoreType.DMA],
        name=name)(block, after)


SEM_SPEC = pl.BlockSpec(memory_space=pltpu.SEMAPHORE)
ANY_SPEC = pl.BlockSpec(memory_space=pl.ANY)
DATAFLOW = pltpu.SideEffectType.DATAFLOW_SIDE_EFFECTING


def _peer(k, x, y, c):
    return (1 - x if (k >> 2) & 1 else x, 1 - y if (k >> 1) & 1 else y, 1 - c if k & 1 else c)


def _own_slot_filled(own_block):
    x, y, c = _position()
    zone = lax.empty((N_DEV,) + own_block.shape, own_block.dtype)
    return lax.dynamic_update_slice_in_dim(zone, own_block[None], 4 * x + 2 * y + c, axis=0)


ALL_PEERS = tuple(range(1, N_DEV))
SIBLING_AND_SAME_CORE = (1, 2, 4, 6)
SAME_CORE = (2, 4, 6)


def _split_start(srcs, scatter, after, *, name, relations=ALL_PEERS, carried=None):
    n = len(srcs)
    extra = ([] if after is None else [after]) + ([] if carried is None else [carried])
    n_carried = 0 if carried is None else 1
    x, y, c = _position()
    me = 4 * x + 2 * y + c
    lands = [_own_slot_filled(lax.dynamic_index_in_dim(s, me, 0, keepdims=False) if scatter else s) for s in srcs]

    def body(*refs):
        src_refs, land_refs = refs[:n], refs[n:2 * n]
        send_sems, recv_sems = refs[2 * n + len(extra)], refs[2 * n + len(extra) + 1]
        token = refs[2 * n + len(extra) + 2 + 2 * n]
        bx, by, bc = _position()
        bme = 4 * bx + 2 * by + bc
        for a in range(n):
            for k in relations:
                tx, ty, tc = _peer(k, bx, by, bc)
                src = src_refs[a].at[4 * tx + 2 * ty + tc] if scatter else src_refs[a]
                pltpu.make_async_remote_copy(
                    src_ref=src, dst_ref=land_refs[a].at[bme],
                    send_sem=send_sems.at[7 * a + k - 1], recv_sem=recv_sems.at[7 * a + k - 1],
                    device_id=(tx, ty, tc), device_id_type=MESH_ID).start()
        token[...] = jnp.zeros_like(token)

    hbm = lambda a: pltpu.HBM(a.shape, a.dtype)
    outs = pl.pallas_call(
        body, name=name,
        out_shape=(pltpu.SemaphoreType.DMA((7 * n,)), pltpu.SemaphoreType.DMA((7 * n,)),
                   *[hbm(s) for s in srcs], *[hbm(l) for l in lands], jax.ShapeDtypeStruct((8, 128), F32),
                   *([hbm(carried)] if n_carried else [])),
        in_specs=[HBM_SPEC] * (2 * n) + [ANY_SPEC] * len(extra),
        out_specs=(SEM_SPEC, SEM_SPEC, *[HBM_SPEC] * (2 * n), pl.BlockSpec(memory_space=pltpu.VMEM), *([ANY_SPEC] * n_carried)),
        input_output_aliases={**{i: 2 + i for i in range(2 * n)},
                              **({2 * n + len(extra) - 1: 2 * n + 3} if n_carried else {})},
        compiler_params=pltpu.CompilerParams(has_side_effects=DATAFLOW),
    )(*[pltpu.with_memory_space_constraint(s, pltpu.HBM) for s in srcs],
      *[pltpu.with_memory_space_constraint(l, pltpu.HBM) for l in lands], *extra)
    return {"n": n, "relations": relations, "send": outs[0], "recv": outs[1], "srcs": list(outs[2:2 + n]),
            "lands": list(outs[2 + n:2 + 2 * n]), "token": outs[2 * n + 2], "carried": outs[-1] if n_carried else None}


def _forward_start(lands, carried, *, name):
    n = len(lands)

    def body(*refs):
        land_refs = refs[:n]
        send_sems, recv_sems = refs[n + 1], refs[n + 2]
        bx, by, bc = _position()
        for a in range(n):
            for k in SAME_CORE:
                tx, ty, tc = _peer(k, bx, by, bc)
                block = land_refs[a].at[4 * tx + 2 * ty + tc]
                pltpu.make_async_remote_copy(
                    src_ref=block, dst_ref=block,
                    send_sem=send_sems.at[7 * a + k - 1], recv_sem=recv_sems.at[7 * a + k - 1],
                    device_id=(bx, by, 1 - bc), device_id_type=MESH_ID).start()

    hbm = lambda a: pltpu.HBM(a.shape, a.dtype)
    outs = pl.pallas_call(
        body, name=name,
        out_shape=(pltpu.SemaphoreType.DMA((7 * n,)), pltpu.SemaphoreType.DMA((7 * n,)),
                   *[hbm(l) for l in lands], hbm(carried)),
        in_specs=[HBM_SPEC] * n + [ANY_SPEC],
        out_specs=(SEM_SPEC, SEM_SPEC, *[HBM_SPEC] * n, ANY_SPEC),
        input_output_aliases={i: 2 + i for i in range(n + 1)},
        compiler_params=pltpu.CompilerParams(has_side_effects=DATAFLOW),
    )(*lands, carried)
    handle = {"n": n, "relations": SAME_CORE, "send": outs[0], "recv": outs[1], "srcs": [], "lands": list(outs[2:2 + n])}
    return handle, outs[-1]


def _split_wait(handle, after, *, name):
    n, ns = handle["n"], len(handle["srcs"])

    def body(*refs):
        land_refs = refs[ns:ns + n]
        send_sems, recv_sems = refs[ns + n], refs[ns + n + 1]
        bx, by, bc = _position()
        for a in range(n):
            for k in handle["relations"]:
                block = land_refs[a].at[0]
                cp = pltpu.make_async_remote_copy(
                    src_ref=block, dst_ref=block,
                    send_sem=send_sems.at[7 * a + k - 1], recv_sem=recv_sems.at[7 * a + k - 1],
                    device_id=_peer(k, bx, by, bc), device_id_type=MESH_ID)
                cp.wait_send()
                cp.wait_recv()

    hbm = lambda a: pltpu.HBM(a.shape, a.dtype)
    outs = pl.pallas_call(
        body, name=name,
        out_shape=(*[hbm(s) for s in handle["srcs"]], *[hbm(l) for l in handle["lands"]]),
        in_specs=[HBM_SPEC] * (ns + n) + [SEM_SPEC, SEM_SPEC, ANY_SPEC],
        out_specs=tuple([HBM_SPEC] * (ns + n)),
        input_output_aliases={i: i for i in range(ns + n)},
        compiler_params=pltpu.CompilerParams(has_side_effects=DATAFLOW),
    )(*handle["srcs"], *handle["lands"], handle["send"], handle["recv"], after)
    return list(outs[ns:])


def _sum_sources(parts, *, tr, name):
    n, R, C = parts.shape

    def body(p_ref, o_ref):
        acc = p_ref[0].astype(F32)
        for s in range(1, n):
            acc = acc + p_ref[s].astype(F32)
        o_ref[...] = acc

    return pl.pallas_call(
        body, grid=(R // tr,), in_specs=[pl.BlockSpec((n, tr, C), lambda i: (0, i, 0))],
        out_specs=pl.BlockSpec((tr, C), lambda i: (i, 0)),
        out_shape=jax.ShapeDtypeStruct((R, C), F32), compiler_params=_cp("parallel"), name=name)(parts)


def _adamw_math(g, w, m, v):
    c1 = 1.0 - ADAM_B1 ** ADAM_STEP
    c2 = 1.0 - ADAM_B2 ** ADAM_STEP
    nm = ADAM_B1 * m + (1.0 - ADAM_B1) * g
    nv = ADAM_B2 * v + (1.0 - ADAM_B2) * (g * g)
    return -ADAM_LR * ((nm / c1) / (jnp.sqrt(nv / c2) + ADAM_EPS) + ADAM_WD * w), nm, nv


def _reduce_adamw(received, w, m, v, *, tr, name):
    L, R, C = w.shape

    def body(*refs):
        p_refs = refs[:L]
        w_ref, m_ref, v_ref, g_ref, d_ref, nm_ref, nv_ref = refs[L:]
        for l in range(L):
            @pl.when(pl.program_id(0) == l)
            def _(p_ref=p_refs[l]):
                acc = p_ref[0].astype(F32)
                for s in range(1, N_DEV):
                    acc = acc + p_ref[s].astype(F32)
                g_ref[...] = acc
                d_ref[...], nm_ref[...], nv_ref[...] = _adamw_math(acc, w_ref[...], m_ref[...], v_ref[...])

    p_spec = pl.BlockSpec((N_DEV, tr, C), lambda l, i: (0, i, 0))
    blk = pl.BlockSpec((None, tr, C), lambda l, i: (l, i, 0))
    sh = jax.ShapeDtypeStruct((L, R, C), F32)
    return pl.pallas_call(
        body, grid=(L, R // tr), in_specs=[p_spec] * L + [blk] * 3, out_specs=[blk] * 4, out_shape=[sh] * 4,
        compiler_params=_cp("parallel", "parallel"), name=name)(*received, w, m, v)


def _adamw(g, w, m, v, *, tr, name):
    L, R, C = w.shape

    def body(g_ref, w_ref, m_ref, v_ref, d_ref, nm_ref, nv_ref):
        d_ref[...], nm_ref[...], nv_ref[...] = _adamw_math(g_ref[...], w_ref[...], m_ref[...], v_ref[...])

    blk = pl.BlockSpec((None, tr, C), lambda l, i: (l, i, 0))
    sh = jax.ShapeDtypeStruct((L, R, C), F32)
    return pl.pallas_call(
        body, grid=(L, R // tr), in_specs=[blk] * 4, out_specs=[blk] * 3, out_shape=[sh] * 3,
        compiler_params=_cp("parallel", "parallel"), name=name)(g, w, m, v)


UNITS = {
    "a_w_in": ("a_w_in", 0, True), "w_mem_kv0": ("w_mem_kv", 0, False), "w_out0": ("w_out", 0, False),
    "w_gate_up0": ("w_gate_up", 0, True), "w_down0": ("w_down", 0, False), "w_kv": ("w_kv", None, True),
    "b_w_in": ("b_w_in", 0, True), "w_mem_kv1": ("w_mem_kv", 1, False), "w_out1": ("w_out", 1, False),
    "w_gate_up1": ("w_gate_up", 1, True), "w_down1": ("w_down", 1, False),
}
BIG = ("a_w_in", "b_w_in", "w_kv", "w_mem_kv", "w_out", "w_gate_up", "w_down")
ADAMW_ROW_TILE = {"a_w_in": 208, "b_w_in": 160, "w_kv": 192, "w_mem_kv": 128, "w_out": 128, "w_gate_up": 176, "w_down": 176}


def _wire_block(weights, unit):
    name, layer, col = UNITS[unit]
    a = weights[name] if layer is None else weights[name][layer]
    return (a.T if col else a).astype(BF16)


SMALL_REPLICATED = ("norm_mix", "norm_ffn", "b_qnorm", "kv_norm", "b_knorm", "mem_norm", "mem_qnorm", "mem_knorm")
SMALL_SHARDED = ("a_lb_logits", "a_onorm")
SMALL_ORDER = SMALL_REPLICATED + SMALL_SHARDED
LANES = 128


def _prod(shape):
    n = 1
    for s in shape:
        n *= s
    return n


def _pack_flat(arrays, rows, cols, dtype):
    flat = jnp.concatenate([a.reshape(-1).astype(dtype) for a in arrays])
    return jnp.pad(flat, (0, rows * cols - flat.shape[0])).reshape(rows, cols)


def _unpack_flat(packed, shapes):
    flat = packed.reshape(-1)
    out, off = [], 0
    for s in shapes:
        out.append(flat[off:off + _prod(s)].reshape(s))
        off += _prod(s)
    return out


def kernel(x, mem, norm_mix, norm_ffn, a_w_in, a_lb_logits, a_onorm, b_w_in, b_qnorm, kv_norm, w_kv, b_knorm, mem_norm, w_mem_kv, mem_qnorm, mem_knorm, w_out, w_gate_up, w_down, loss_target, m_norm_mix, m_norm_ffn, m_a_w_in, m_a_lb_logits, m_a_onorm, m_b_w_in, m_b_qnorm, m_kv_norm, m_w_kv, m_b_knorm, m_mem_norm, m_w_mem_kv, m_mem_qnorm, m_mem_knorm, m_w_out, m_w_gate_up, m_w_down, v_norm_mix, v_norm_ffn, v_a_w_in, v_a_lb_logits, v_a_onorm, v_b_w_in, v_b_qnorm, v_kv_norm, v_w_kv, v_b_knorm, v_mem_norm, v_w_mem_kv, v_mem_qnorm, v_mem_knorm, v_w_out, v_w_gate_up, v_w_down):
    names = ("norm_mix", "norm_ffn", "a_w_in", "a_lb_logits", "a_onorm", "b_w_in", "b_qnorm", "kv_norm", "w_kv", "b_knorm",
             "mem_norm", "w_mem_kv", "mem_qnorm", "mem_knorm", "w_out", "w_gate_up", "w_down")
    w = dict(zip(names, (norm_mix, norm_ffn, a_w_in, a_lb_logits, a_onorm, b_w_in, b_qnorm, kv_norm, w_kv, b_knorm,
                         mem_norm, w_mem_kv, mem_qnorm, mem_knorm, w_out, w_gate_up, w_down)))
    m = dict(zip(names, (m_norm_mix, m_norm_ffn, m_a_w_in, m_a_lb_logits, m_a_onorm, m_b_w_in, m_b_qnorm, m_kv_norm, m_w_kv,
                         m_b_knorm, m_mem_norm, m_w_mem_kv, m_mem_qnorm, m_mem_knorm, m_w_out, m_w_gate_up, m_w_down)))
    v = dict(zip(names, (v_norm_mix, v_norm_ffn, v_a_w_in, v_a_lb_logits, v_a_onorm, v_b_w_in, v_b_qnorm, v_kv_norm, v_w_kv,
                         v_b_knorm, v_mem_norm, v_w_mem_kv, v_mem_qnorm, v_mem_knorm, v_w_out, v_w_gate_up, v_w_down)))

    first = ["a_w_in", "w_mem_kv0"]
    gathered = _all_gather([_wire_block(w, u) for u in first] + [_pack_flat([a_lb_logits, a_onorm], 8, LANES, F32)],
                           name="gather_first")
    full = {u: g.reshape(-1, g.shape[-1]) for u, g in zip(first, gathered)}
    small_in = gathered[-1].reshape(N_DEV, -1)
    P = {n: w[n] for n in SMALL_REPLICATED}
    P["a_lb_logits"] = small_in[:, :192].reshape(N_DEV, 2, 96).transpose(1, 0, 2).reshape(2, A_WIDTH)
    P["a_onorm"] = small_in[:, 192:288].reshape(1, A_WIDTH)
    later = [["w_out0", "w_gate_up0"], ["w_down0", "w_kv"], ["b_w_in", "w_mem_kv1"], ["w_out1", "w_gate_up1", "w_down1"]]
    first_half, second_half = {}, {}

    def start_first_half(i, after, carried=None):
        first_half[i] = _split_start([_wire_block(w, u) for u in later[i]], False, after, name=f"gather{i}_start",
                                     relations=SIBLING_AND_SAME_CORE, carried=carried)
        return first_half[i]

    token = start_first_half(0, gathered[-1])["token"]
    token = start_first_half(1, token)["token"]

    def forward_point(i, value):
        landed = _split_wait(first_half[i], value, name=f"gather{i}_landed")
        second_half[i], value = _forward_start(landed, value, name=f"gather{i}_forward")
        if i + 2 < len(later):
            value = start_first_half(i + 2, None, carried=value)["carried"]
        return value

    def get_w(unit, after):
        if unit not in full:
            i = [unit in group for group in later].index(True)
            for u, land in zip(later[i], _split_wait(second_half[i], after, name=f"gather{i}_wait")):
                full[u] = land.reshape(-1, land.shape[-1])
        return full[unit]

    sent = []

    def put_g(group):
        units = list(group)
        handle = _split_start([group[u].reshape(N_DEV, -1, group[u].shape[-1]) for u in units], True, None,
                              name=f"scatter{len(sent)}_start")
        sent.append((units, handle))
        return handle["token"]

    sq, gx, gP = _local_step(x[0], mem[0], loss_target[0], get_w, P, put_g, first_dep=token, forward_point=forward_point)
    loss_here = (0.5 * jnp.sum(sq) / D_MODEL).reshape(1)

    received = {}
    group_of = {u: i for i, (units, _) in enumerate(sent) for u in units}
    out = {"grad": {}, "delta": {}, "new_m": {}, "new_v": {}}
    newest = [gx]

    def update_big(n):
        shape = w[n].shape
        as3 = lambda a: a.reshape((-1,) + shape[-2:])
        mine = [u for u, (wn, _, _) in UNITS.items() if wn == n]
        for i in sorted({group_of[u] for u in mine}):
            if sent[i][0][0] not in received:
                received.update(zip(sent[i][0], _split_wait(sent[i][1], newest[0], name=f"scatter{i}_wait")))
        flip = (lambda a: jnp.swapaxes(a, 1, 2)) if UNITS[mine[0]][2] else (lambda a: a)
        res = _reduce_adamw([received[u] for u in mine], flip(as3(w[n])), flip(as3(m[n])), flip(as3(v[n])),
                            tr=ADAMW_ROW_TILE[n], name=f"adamw_{n}")
        newest[0] = res[1]
        for kind, r in zip(("grad", "delta", "new_m", "new_v"), res):
            out[kind][n] = flip(r).reshape(shape)

    for n in ("w_down", "w_gate_up", "w_out", "w_mem_kv", "b_w_in", "w_kv"):
        update_big(n)

    full_shapes = [(2, A_WIDTH) if n == "a_lb_logits" else (1, A_WIDTH) if n == "a_onorm" else w[n].shape for n in SMALL_ORDER]
    n_small = sum(_prod(s) for s in full_shapes) + 1
    rows_small = -(-n_small // (8 * LANES)) * 8
    g_all = _all_gather_direct(_pack_flat([gP[n] for n in SMALL_ORDER] + [loss_here], rows_small, LANES, F32),
                               newest[0], name="gather_small_grads")
    summed = _unpack_flat(_sum_sources(g_all, tr=rows_small, name="sum_small_grads"), full_shapes + [(1,)])
    g_small = dict(zip(SMALL_ORDER, summed))
    loss = summed[-1].reshape(())
    me = 4 * lax.axis_index("x") + 2 * lax.axis_index("y") + lax.axis_index("c")
    for n in SMALL_SHARDED:
        g_small[n] = lax.dynamic_slice_in_dim(g_small[n], me * 96, 96, axis=1)
    shapes = [w[n].shape for n in SMALL_ORDER]
    rows_upd = -(-sum(_prod(s) for s in shapes) // (8 * LANES)) * 8
    pk = lambda d: _pack_flat([d[n] for n in SMALL_ORDER], rows_upd, LANES, F32)
    res = _adamw(pk(g_small)[None], pk(w)[None], pk(m)[None], pk(v)[None], tr=rows_upd, name="adamw_small")
    out["grad"].update(g_small)
    for kind, packed in zip(("delta", "new_m", "new_v"), res):
        out[kind].update(zip(SMALL_ORDER, _unpack_flat(packed[0], shapes)))
    newest[0] = res[0]
    update_big("a_w_in")

    return (loss, gx[None], *[out["grad"][n] for n in names], *[out["delta"][n] for n in names],
            *[out["new_m"][n] for n in names], *[out["new_v"][n] for n in names])
```

```python
import functools

import jax
import jax.numpy as jnp
from jax import lax
from jax.experimental import pallas as pl
from jax.experimental.pallas import tpu as pltpu

F32 = jnp.float32
BF16 = jnp.bfloat16

N_DEV = 8
D_MODEL = 1024
HEAD_DIM = 128
A_HEADS = 6
A_WIDTH = A_HEADS * HEAD_DIM
CHUNK = 64
B_HEADS = 6
B_WIDTH = B_HEADS * HEAD_DIM
DILATIONS = (1, 4, 16)
SPAN = 128
N_GROUPS = 3
ROPE_THETA = 10000.0
MEM_TOKENS = 256
MEM_HEADS = 4
MEM_HEAD_DIM = 64
MEM_WIDTH = MEM_HEADS * MEM_HEAD_DIM
FFN_HIDDEN = 2816
EPS = 1e-6

ADAM_LR = 0.001
ADAM_B1 = 0.9
ADAM_B2 = 0.999
ADAM_EPS = 1e-08
ADAM_WD = 0.01
ADAM_STEP = 10

V7X_VMEM_LIMIT_BYTES = 56 * 1024 * 1024

NT_DIMS = (((1,), (1,)), ((), ()))
TN_DIMS = (((0,), (0,)), ((), ()))


def _cp(*sem):
    return pltpu.CompilerParams(dimension_semantics=sem, vmem_limit_bytes=V7X_VMEM_LIMIT_BYTES)


def _dot(a, b):
    return jnp.dot(a.astype(BF16), b.astype(BF16), preferred_element_type=F32)


def _dot_nt(a, b):
    return lax.dot_general(a.astype(BF16), b.astype(BF16), NT_DIMS, preferred_element_type=F32)


def _dot_tn(a, b):
    return lax.dot_general(a.astype(BF16), b.astype(BF16), TN_DIMS, preferred_element_type=F32)


def _dot3(m01, x):
    hi = x.astype(BF16)
    r1 = x - hi.astype(F32)
    mid = r1.astype(BF16)
    lo = (r1 - mid.astype(F32)).astype(BF16)
    d = functools.partial(jnp.dot, preferred_element_type=F32)
    return d(m01, hi) + d(m01, mid) + d(m01, lo)


def _sigmoid(x):
    return 1.0 / (1.0 + jnp.exp(-x))


def _full(shape):
    return pl.BlockSpec(shape, lambda *_: (0,) * len(shape))


def _dep(body, n_in, dep):
    if dep is None:
        return body, [], []

    def with_dep(*refs):
        return body(*refs[:n_in], *refs[n_in + 1:])

    return with_dep, [pl.BlockSpec(memory_space=pl.ANY)], [dep]


def _rms_matmul(x, g, w, *, tt, tn, wt, name, out_dtype=F32, dep=None, rotate=None):
    T, K = x.shape
    N = w.shape[0] if wt else w.shape[1]
    n_rot = 0 if rotate is None else rotate[0].shape[1] // HEAD_DIM
    extra_in = [] if rotate is None else list(rotate)

    def kernel_body(x_ref, g_ref, w_ref, *rest):
        y_ref, xn_ref = rest[len(extra_in)], rest[len(extra_in) + 1]
        xf = x_ref[...]
        r = lax.rsqrt(jnp.mean(xf * xf, axis=-1, keepdims=True) + EPS)
        xn = (xf * r * g_ref[...]).astype(BF16)
        xn_ref[...] = xn
        for j in range(N // tn):
            cols = slice(j * tn, (j + 1) * tn)
            y = _dot_nt(xn, w_ref[cols, :]) if wt else _dot(xn, w_ref[:, cols])
            y_ref[:, cols] = y.astype(out_dtype)
            for h in range(j * tn // HEAD_DIM, min((j + 1) * tn // HEAD_DIM, n_rot)):
                gw_ref, c_ref, s_ref, yr_ref = rest[0], rest[1], rest[2], rest[len(extra_in) + 2]
                sl = slice(h * HEAD_DIM, (h + 1) * HEAD_DIM)
                xhat, _ = _head_rms(y[:, h * HEAD_DIM - j * tn:(h + 1) * HEAD_DIM - j * tn])
                yr_ref[:, sl] = _rope(xhat * gw_ref[:, sl], c_ref[...], s_ref[...])

    tbl = pl.BlockSpec((tt, HEAD_DIM), lambda i: (i, 0))
    rot_specs = [] if rotate is None else [_full((1, n_rot * HEAD_DIM)), tbl, tbl]
    body, dep_specs, dep_args = _dep(kernel_body, 3 + len(extra_in), dep)
    return pl.pallas_call(
        body, grid=(T // tt,),
        in_specs=[pl.BlockSpec((tt, K), lambda i: (i, 0)), _full((1, K)), _full(w.shape)] + rot_specs + dep_specs,
        out_specs=[pl.BlockSpec((tt, N), lambda i: (i, 0)), pl.BlockSpec((tt, K), lambda i: (i, 0))]
        + ([] if rotate is None else [pl.BlockSpec((tt, n_rot * HEAD_DIM), lambda i: (i, 0))]),
        out_shape=[jax.ShapeDtypeStruct((T, N), out_dtype), jax.ShapeDtypeStruct((T, K), BF16)]
        + ([] if rotate is None else [jax.ShapeDtypeStruct((T, n_rot * HEAD_DIM), F32)]),
        compiler_params=_cp("parallel"), name=name)(x, g, w, *extra_in, *dep_args)


def _mm_res(res, a1, a2, w, *, tt, name):
    T, K1 = a1.shape
    K2 = a2.shape[1]
    N = w.shape[1]

    def body(r_ref, a1_ref, a2_ref, w_ref, o_ref):
        o_ref[...] = r_ref[...] + _dot(a1_ref[...], w_ref[:K1, :]) + _dot(a2_ref[...], w_ref[K1:, :])

    return pl.pallas_call(
        body, grid=(T // tt,),
        in_specs=[pl.BlockSpec((tt, N), lambda i: (i, 0)), pl.BlockSpec((tt, K1), lambda i: (i, 0)),
                  pl.BlockSpec((tt, K2), lambda i: (i, 0)), _full((K1 + K2, N))],
        out_specs=pl.BlockSpec((tt, N), lambda i: (i, 0)),
        out_shape=jax.ShapeDtypeStruct((T, N), F32),
        compiler_params=_cp("parallel"), name=name)(res, a1, a2, w)


def _swiglu_down(h, gu, wd, *, tt, name):
    T, D = h.shape
    Fh = wd.shape[0]

    def body(h_ref, gt_ref, up_ref, w_ref, o_ref):
        gt = gt_ref[...].astype(F32)
        act = gt * _sigmoid(gt) * up_ref[...].astype(F32)
        o_ref[...] = h_ref[...] + _dot(act, w_ref[...])

    return pl.pallas_call(
        body, grid=(T // tt,),
        in_specs=[pl.BlockSpec((tt, D), lambda i: (i, 0)), pl.BlockSpec((tt, Fh), lambda i: (i, 0)),
                  pl.BlockSpec((tt, Fh), lambda i: (i, 1)), _full((Fh, D))],
        out_specs=pl.BlockSpec((tt, D), lambda i: (i, 0)),
        out_shape=jax.ShapeDtypeStruct((T, D), F32),
        compiler_params=_cp("parallel"), name=name)(h, gu, gu, wd)


def _swiglu_down_loss(h, gu, wd, tgt, *, tt, name):
    T, D = h.shape
    Fh = wd.shape[0]

    def body(h_ref, gt_ref, up_ref, w_ref, t_ref, dy_ref, acc_ref):
        @pl.when(pl.program_id(0) == 0)
        def _():
            acc_ref[...] = jnp.zeros_like(acc_ref)

        gt = gt_ref[...].astype(F32)
        act = gt * _sigmoid(gt) * up_ref[...].astype(F32)
        e = h_ref[...] + _dot(act, w_ref[...]) - t_ref[...]
        dy_ref[...] = e * (1.0 / D)
        acc_ref[...] += jnp.sum(e * e, axis=0, keepdims=True)

    row = pl.BlockSpec((tt, D), lambda i: (i, 0))
    return pl.pallas_call(
        body, grid=(T // tt,),
        in_specs=[row, pl.BlockSpec((tt, Fh), lambda i: (i, 0)), pl.BlockSpec((tt, Fh), lambda i: (i, 1)), _full((Fh, D)), row],
        out_specs=[row, _full((1, D))],
        out_shape=[jax.ShapeDtypeStruct((T, D), F32), jax.ShapeDtypeStruct((1, D), F32)],
        compiler_params=_cp("arbitrary"), name=name)(h, gu, gu, wd, tgt)


def _swiglu_bwd(dh, gu, wd, *, tt, name):
    T, D = dh.shape
    Fh = wd.shape[0]
    last = T // tt - 1

    def body(dh_ref, gt_ref, up_ref, w_ref, dgu_ref, gw_ref, acc):
        @pl.when(pl.program_id(0) == 0)
        def _():
            acc[...] = jnp.zeros_like(acc)

        gt = gt_ref[...].astype(F32)
        up = up_ref[...].astype(F32)
        s = _sigmoid(gt)
        silu = gt * s
        dh16 = dh_ref[...].astype(BF16)
        dact = _dot_nt(dh16, w_ref[...])
        acc[...] += _dot_tn((silu * up).astype(BF16), dh16)
        dgu_ref[:, :Fh] = (dact * up * (s * (1.0 + gt * (1.0 - s)))).astype(BF16)
        dgu_ref[:, Fh:] = (dact * silu).astype(BF16)

        @pl.when(pl.program_id(0) == last)
        def _():
            gw_ref[...] = acc[...].astype(BF16)

    return pl.pallas_call(
        body, grid=(T // tt,),
        in_specs=[pl.BlockSpec((tt, D), lambda i: (i, 0)), pl.BlockSpec((tt, Fh), lambda i: (i, 0)),
                  pl.BlockSpec((tt, Fh), lambda i: (i, 1)), _full((Fh, D))],
        out_specs=[pl.BlockSpec((tt, 2 * Fh), lambda i: (i, 0)), _full((Fh, D))],
        out_shape=[jax.ShapeDtypeStruct((T, 2 * Fh), BF16), jax.ShapeDtypeStruct((Fh, D), BF16)],
        scratch_shapes=[pltpu.VMEM((Fh, D), F32)],
        compiler_params=_cp("arbitrary"), name=name)(dh, gu, gu, wd)


def _out_proj_bwd(dy, a1, a2, w, *, tt, name, head_dots=False):
    T, N = dy.shape
    K1, K2 = a1.shape[1], a2.shape[1]
    K = K1 + K2
    last = T // tt - 1

    def body(dy_ref, a1_ref, a2_ref, w_ref, da_ref, gw_ref, *rest):
        acc = rest[-1]

        @pl.when(pl.program_id(0) == 0)
        def _():
            acc[...] = jnp.zeros_like(acc)

        dy16 = dy_ref[...].astype(BF16)
        da = _dot_nt(dy16, w_ref[...])
        da_ref[...] = da
        acc[:K1, :] += _dot_tn(a1_ref[...], dy16)
        acc[K1:, :] += _dot_tn(a2_ref[...], dy16)
        if head_dots:
            for h in range(K1 // HEAD_DIM):
                sl = slice(h * HEAD_DIM, (h + 1) * HEAD_DIM)
                rest[0][:, sl] = jnp.broadcast_to(jnp.sum(da[:, sl] * a1_ref[:, sl], axis=-1, keepdims=True), (tt, HEAD_DIM))

        @pl.when(pl.program_id(0) == last)
        def _():
            gw_ref[...] = acc[...].astype(BF16)

    extra_specs = [pl.BlockSpec((tt, K1), lambda i: (i, 0))] if head_dots else []
    extra_shapes = [jax.ShapeDtypeStruct((T, K1), F32)] if head_dots else []
    return pl.pallas_call(
        body, grid=(T // tt,),
        in_specs=[pl.BlockSpec((tt, N), lambda i: (i, 0)), pl.BlockSpec((tt, K1), lambda i: (i, 0)),
                  pl.BlockSpec((tt, K2), lambda i: (i, 0)), _full((K, N))],
        out_specs=[pl.BlockSpec((tt, K), lambda i: (i, 0)), _full((K, N))] + extra_specs,
        out_shape=[jax.ShapeDtypeStruct((T, K), F32), jax.ShapeDtypeStruct((K, N), BF16)] + extra_shapes,
        scratch_shapes=[pltpu.VMEM((K, N), F32)],
        compiler_params=_cp("arbitrary"), name=name)(dy, a1, a2, w)


def _mm_tn(a, b, *, tt, tka, name):
    T, Ka = a.shape
    N = b.shape[1]
    last = T // tt - 1

    def body(a_ref, b_ref, o_ref, acc):
        @pl.when(pl.program_id(1) == 0)
        def _():
            acc[...] = jnp.zeros_like(acc)

        acc[...] += _dot_tn(a_ref[...], b_ref[...])

        @pl.when(pl.program_id(1) == last)
        def _():
            o_ref[...] = acc[...].astype(BF16)

    return pl.pallas_call(
        body, grid=(Ka // tka, T // tt),
        in_specs=[pl.BlockSpec((tt, tka), lambda j, t: (t, j)), pl.BlockSpec((tt, N), lambda j, t: (t, 0))],
        out_specs=pl.BlockSpec((tka, N), lambda j, t: (j, 0)),
        out_shape=jax.ShapeDtypeStruct((Ka, N), BF16),
        scratch_shapes=[pltpu.VMEM((tka, N), F32)],
        compiler_params=_cp("parallel", "arbitrary"), name=name)(a, b)


def _mm_tn_pieces(pieces, b, *, tt, name):
    n = len(pieces)
    T = b.shape[0]
    N = b.shape[1]
    widths = [p.shape[1] for p in pieces]
    Ka = sum(widths)
    last = T // tt - 1

    def body(*refs):
        p_refs = refs[:n]
        b_ref, o_ref, acc = refs[n:]

        @pl.when(pl.program_id(0) == 0)
        def _():
            acc[...] = jnp.zeros_like(acc)

        bv = b_ref[...].astype(BF16)
        off = 0
        for p_ref, wd in zip(p_refs, widths):
            acc[off:off + wd, :] += _dot_tn(p_ref[...], bv)
            off += wd

        @pl.when(pl.program_id(0) == last)
        def _():
            o_ref[...] = acc[...].astype(BF16)

    return pl.pallas_call(
        body, grid=(T // tt,),
        in_specs=[pl.BlockSpec((tt, wd), lambda t: (t, 0)) for wd in widths] + [pl.BlockSpec((tt, N), lambda t: (t, 0))],
        out_specs=_full((Ka, N)), out_shape=jax.ShapeDtypeStruct((Ka, N), BF16),
        scratch_shapes=[pltpu.VMEM((Ka, N), F32)],
        compiler_params=_cp("arbitrary"), name=name)(*pieces, b)


def _rms_bwd_dx(x, g, w, dy, dres, *, tt, wt, name, dep=None):
    pieces = list(dy) if isinstance(dy, (list, tuple)) else [dy]
    n = len(pieces)
    widths = [p.shape[1] for p in pieces]
    T, K = x.shape

    def kernel_body(x_ref, g_ref, w_ref, *rest):
        dy_refs = rest[:n]
        dres_ref, dx_ref, dg_ref = rest[n:]

        @pl.when(pl.program_id(0) == 0)
        def _():
            dg_ref[...] = jnp.zeros_like(dg_ref)

        if n == 1:
            dxn = (_dot if wt else _dot_nt)(dy_refs[0][...], w_ref[...])
        else:
            dxn, off = 0.0, 0
            for dy_ref, wd in zip(dy_refs, widths):
                dxn = dxn + _dot(dy_ref[...], w_ref[off:off + wd, :])
                off += wd
        xf = x_ref[...]
        r = lax.rsqrt(jnp.mean(xf * xf, axis=-1, keepdims=True) + EPS)
        xhat = xf * r
        dg_ref[...] += jnp.sum(dxn * xhat, axis=0, keepdims=True)
        dxhat = dxn * g_ref[...]
        dx_ref[...] = dres_ref[...] + r * (dxhat - xhat * jnp.mean(dxhat * xhat, axis=-1, keepdims=True))

    assert n == 1 or wt
    body, dep_specs, dep_args = _dep(kernel_body, 4 + n, dep)
    return pl.pallas_call(
        body, grid=(T // tt,),
        in_specs=[pl.BlockSpec((tt, K), lambda i: (i, 0)), _full((1, K)), _full(w.shape)]
        + [pl.BlockSpec((tt, wd), lambda i: (i, 0)) for wd in widths]
        + [pl.BlockSpec((tt, K), lambda i: (i, 0))] + dep_specs,
        out_specs=[pl.BlockSpec((tt, K), lambda i: (i, 0)), _full((1, K))],
        out_shape=[jax.ShapeDtypeStruct((T, K), F32), jax.ShapeDtypeStruct((1, K), F32)],
        compiler_params=_cp("arbitrary"), name=name)(x, g, w, *pieces, dres, *dep_args)


HGRN_TB = 512
HGRN_NCH = HGRN_TB // CHUNK
HGRN_HPB = 6


def _hgrn_chunk_fwd(q, z, lbv, tril01):
    sig = _sigmoid(z)
    f = lbv + (1.0 - lbv) * sig
    kk = 1.0 - f
    b = _dot3(tril01, jnp.log(f))
    bend = b[CHUNK - 1:CHUNK, :]
    sq = _sigmoid(q)
    eb = jnp.exp(b)
    emb = jnp.exp(-b)
    eo = jnp.exp(bend - b)
    dec = jnp.exp(bend)
    return sig, f, kk, sq, eb, emb, eo, dec


def _hgrn2_fwd(proj, lb, *, name):
    T = proj.shape[0]
    nT = T // HGRN_TB
    nC = T // CHUNK

    def body(q_ref, z_ref, v_ref, lb_ref, o_ref, st_ref, state):
        @pl.when(pl.program_id(1) == 0)
        def _():
            state[...] = jnp.zeros_like(state)

        row = lax.broadcasted_iota(jnp.int32, (CHUNK, CHUNK), 0)
        col = lax.broadcasted_iota(jnp.int32, (CHUNK, CHUNK), 1)
        causal = row >= col
        tril01 = causal.astype(BF16)

        def chunk(c, carry):
            rows = pl.ds(pl.multiple_of(c * CHUNK, CHUNK), CHUNK)
            for hh in range(HGRN_HPB):
                sl = slice(hh * HEAD_DIM, (hh + 1) * HEAD_DIM)
                q = q_ref[rows, sl]
                v = v_ref[rows, sl].astype(BF16)
                sig, f, kk, sq, eb, emb, eo, dec = _hgrn_chunk_fwd(q, z_ref[rows, sl], lb_ref[:, sl], tril01)
                qi = (q * sq * eb).astype(BF16)
                ki = (kk * emb).astype(BF16)
                ko = (kk * eo).astype(BF16)
                st = state[hh]
                att = jnp.where(causal, _dot_nt(qi, ki), 0.0)
                o_ref[rows, sl] = _dot(att, v) + _dot_nt(qi, st)
                st_ref[c, hh] = st
                state[hh] = st * dec + _dot_tn(v, ko)
            return carry

        lax.fori_loop(0, HGRN_NCH, chunk, 0)

    W = HGRN_HPB * HEAD_DIM
    nG = A_HEADS // HGRN_HPB
    hb = lambda off: pl.BlockSpec((HGRN_TB, W), lambda h, i: (i, off + h))
    return pl.pallas_call(
        body, grid=(nG, nT),
        in_specs=[hb(0), hb(nG), hb(2 * nG), pl.BlockSpec((1, W), lambda h, i: (0, h))],
        out_specs=[hb(0), pl.BlockSpec((HGRN_NCH, HGRN_HPB, HEAD_DIM, HEAD_DIM), lambda h, i: (i, h, 0, 0))],
        out_shape=[jax.ShapeDtypeStruct((T, A_WIDTH), F32), jax.ShapeDtypeStruct((nC, A_HEADS, HEAD_DIM, HEAD_DIM), F32)],
        scratch_shapes=[pltpu.VMEM((HGRN_HPB, HEAD_DIM, HEAD_DIM), F32)],
        compiler_params=_cp("parallel", "arbitrary"), name=name)(proj, proj, proj, lb)


def _hgrn2_bwd(proj, lb, st_all, do, *, name):
    T = proj.shape[0]
    nT = T // HGRN_TB

    def body(q_ref, z_ref, v_ref, lb_ref, st_ref, do_ref, dq_ref, dz_ref, dv_ref, dlb_ref, dstate):
        @pl.when(pl.program_id(1) == 0)
        def _():
            dstate[...] = jnp.zeros_like(dstate)
            dlb_ref[...] = jnp.zeros_like(dlb_ref)

        row = lax.broadcasted_iota(jnp.int32, (CHUNK, CHUNK), 0)
        col = lax.broadcasted_iota(jnp.int32, (CHUNK, CHUNK), 1)
        causal = row >= col
        tril01 = causal.astype(BF16)
        triu01 = (row <= col).astype(BF16)

        def chunk(cc, carry):
            c = HGRN_NCH - 1 - cc
            rows = pl.ds(pl.multiple_of(c * CHUNK, CHUNK), CHUNK)
            for hh in range(HGRN_HPB):
                sl = slice(hh * HEAD_DIM, (hh + 1) * HEAD_DIM)
                lbv = lb_ref[:, sl]
                q = q_ref[rows, sl]
                v = v_ref[rows, sl].astype(BF16)
                sig, f, kk, sq, eb, emb, eo, dec = _hgrn_chunk_fwd(q, z_ref[rows, sl], lbv, tril01)
                qi32 = q * sq * eb
                ki32 = kk * emb
                ko32 = kk * eo
                qi, ki, ko = qi32.astype(BF16), ki32.astype(BF16), ko32.astype(BF16)
                att = jnp.where(causal, _dot_nt(qi, ki), 0.0).astype(BF16)
                dout = do_ref[rows, sl].astype(BF16)
                st = st_ref[c, hh]
                dst = dstate[hh]
                dst16 = dst.astype(BF16)
                datt = jnp.where(causal, _dot_nt(dout, v), 0.0).astype(BF16)
                dqi = _dot(datt, ki) + _dot(dout, st)
                dki = _dot_tn(datt, qi)
                dv_ref[rows, sl] = (_dot_tn(att, dout) + _dot_nt(ko, dst16)).astype(BF16)
                dko = _dot(v, dst16)
                ddec = jnp.sum(dst * st, axis=0, keepdims=True)
                dstate[hh] = dst * dec + _dot_tn(dout, qi)
                dkk = dki * emb + dko * eo
                db = dqi * qi32 - dki * ki32 - dko * ko32
                dbend = jnp.sum(dko * ko32, axis=0, keepdims=True) + ddec * dec
                dlogf = _dot3(triu01, db) + dbend
                df = dlogf / f - dkk
                dz_ref[rows, sl] = (df * (1.0 - lbv) * sig * (1.0 - sig)).astype(BF16)
                dlb_ref[:, sl] += jnp.sum(df * (1.0 - sig), axis=0, keepdims=True)
                dq_ref[rows, sl] = (dqi * eb * (sq * (1.0 + q * (1.0 - sq)))).astype(BF16)
            return carry

        lax.fori_loop(0, HGRN_NCH, chunk, 0)

    W = HGRN_HPB * HEAD_DIM
    nG = A_HEADS // HGRN_HPB
    hb = lambda off: pl.BlockSpec((HGRN_TB, W), lambda h, i: (nT - 1 - i, off + h))
    hlb = pl.BlockSpec((1, W), lambda h, i: (0, h))
    o16 = jax.ShapeDtypeStruct((T, A_WIDTH), BF16)
    return pl.pallas_call(
        body, grid=(nG, nT),
        in_specs=[hb(0), hb(nG), hb(2 * nG), hlb,
                  pl.BlockSpec((HGRN_NCH, HGRN_HPB, HEAD_DIM, HEAD_DIM), lambda h, i: (nT - 1 - i, h, 0, 0)), hb(0)],
        out_specs=[hb(0), hb(0), hb(0), hlb],
        out_shape=[o16, o16, o16, jax.ShapeDtypeStruct((1, A_WIDTH), F32)],
        scratch_shapes=[pltpu.VMEM((HGRN_HPB, HEAD_DIM, HEAD_DIM), F32)],
        compiler_params=_cp("parallel", "arbitrary"), name=name)(proj, proj, proj, lb, st_all, do)


def _head_rms(x):
    r = lax.rsqrt(jnp.mean(x * x, axis=-1, keepdims=True) + EPS)
    return x * r, r


def _head_rms_bwd(dxhat, xhat, r):
    return r * (dxhat - xhat * jnp.mean(dxhat * xhat, axis=-1, keepdims=True))


def _a_post_fwd(o, proj, onorm, *, tt, name):
    T = o.shape[0]

    def body(o_ref, g_ref, w_ref, y_ref):
        for h in range(A_HEADS):
            sl = slice(h * HEAD_DIM, (h + 1) * HEAD_DIM)
            xhat, _ = _head_rms(o_ref[:, sl])
            g = g_ref[:, sl]
            y_ref[:, sl] = xhat * w_ref[:, sl] * (g * _sigmoid(g))

    blk = lambda c: pl.BlockSpec((tt, A_WIDTH), lambda i: (i, c))
    return pl.pallas_call(
        body, grid=(T // tt,), in_specs=[blk(0), blk(3), _full((1, A_WIDTH))], out_specs=blk(0),
        out_shape=jax.ShapeDtypeStruct((T, A_WIDTH), F32),
        compiler_params=_cp("parallel"), name=name)(o, proj, onorm)


def _a_post_bwd(o, proj, onorm, dmix, *, tt, name, dep=None):
    T = o.shape[0]

    def kernel_body(o_ref, g_ref, w_ref, dy_ref, do_ref, dg_ref, dw_ref):
        @pl.when(pl.program_id(0) == 0)
        def _():
            dw_ref[...] = jnp.zeros_like(dw_ref)

        for h in range(A_HEADS):
            sl = slice(h * HEAD_DIM, (h + 1) * HEAD_DIM)
            xhat, r = _head_rms(o_ref[:, sl])
            g = g_ref[:, sl]
            s = _sigmoid(g)
            dy = dy_ref[:, sl]
            w = w_ref[:, sl]
            dg_ref[:, sl] = (dy * xhat * w * (s * (1.0 + g * (1.0 - s)))).astype(BF16)
            dyn = dy * (g * s)
            dw_ref[:, sl] += jnp.sum(dyn * xhat, axis=0, keepdims=True)
            do_ref[:, sl] = _head_rms_bwd(dyn * w, xhat, r)

    blk = lambda c: pl.BlockSpec((tt, A_WIDTH), lambda i: (i, c))
    body, dep_specs, dep_args = _dep(kernel_body, 4, dep)
    return pl.pallas_call(
        body, grid=(T // tt,), in_specs=[blk(0), blk(3), _full((1, A_WIDTH)), blk(0)] + dep_specs,
        out_specs=[blk(0), blk(0), _full((1, A_WIDTH))],
        out_shape=[jax.ShapeDtypeStruct((T, A_WIDTH), F32), jax.ShapeDtypeStruct((T, A_WIDTH), BF16),
                   jax.ShapeDtypeStruct((1, A_WIDTH), F32)],
        compiler_params=_cp("arbitrary"), name=name)(o, proj, onorm, dmix, *dep_args)


def _mem_head_masks(n):
    lane = lax.broadcasted_iota(jnp.int32, (n, MEM_WIDTH), 1)
    return [(lane >= m * MEM_HEAD_DIM) & (lane < (m + 1) * MEM_HEAD_DIM) for m in range(MEM_HEADS)]


def _mem_head_rms(x, masks):
    x2 = x * x
    r = jnp.zeros_like(x)
    for mk in masks:
        ms = jnp.sum(jnp.where(mk, x2, 0.0), axis=-1, keepdims=True) * (1.0 / MEM_HEAD_DIM)
        r = jnp.where(mk, lax.rsqrt(ms + EPS), r)
    return x * r, r


def _mem_head_rms_bwd(dxhat, xhat, r, masks):
    t = dxhat * xhat
    m = jnp.zeros_like(t)
    for mk in masks:
        m = jnp.where(mk, jnp.sum(jnp.where(mk, t, 0.0), axis=-1, keepdims=True) * (1.0 / MEM_HEAD_DIM), m)
    return r * (dxhat - xhat * m)


MEM_SCALE = MEM_HEAD_DIM ** -0.5


def _mem_attn_fwd(proj, qcol, mkv, qn_w, kn_w, *, tt, name):
    T = proj.shape[0]

    def body(q_ref, k_ref, v_ref, qw_ref, kw_ref, o_ref):
        qmasks = _mem_head_masks(tt)
        kmasks = _mem_head_masks(MEM_TOKENS)
        qhat, _ = _mem_head_rms(q_ref[...], qmasks)
        qn = qhat * qw_ref[...]
        khat, _ = _mem_head_rms(k_ref[...], kmasks)
        kn = (khat * kw_ref[...]).astype(BF16)
        v = v_ref[...].astype(BF16)
        out = jnp.zeros((tt, MEM_WIDTH), F32)
        for m in range(MEM_HEADS):
            s = _dot_nt(jnp.where(qmasks[m], qn, 0.0), kn) * MEM_SCALE
            s = s - jnp.max(s, axis=-1, keepdims=True)
            p = jnp.exp(s)
            p = p / jnp.sum(p, axis=-1, keepdims=True)
            out = jnp.where(qmasks[m], _dot(p, v), out)
        o_ref[...] = out

    return pl.pallas_call(
        body, grid=(T // tt,),
        in_specs=[pl.BlockSpec((tt, MEM_WIDTH), lambda i: (i, qcol)), pl.BlockSpec((MEM_TOKENS, MEM_WIDTH), lambda i: (0, 0)),
                  pl.BlockSpec((MEM_TOKENS, MEM_WIDTH), lambda i: (0, 1)), _full((1, MEM_WIDTH)), _full((1, MEM_WIDTH))],
        out_specs=pl.BlockSpec((tt, MEM_WIDTH), lambda i: (i, 0)),
        out_shape=jax.ShapeDtypeStruct((T, MEM_WIDTH), F32),
        compiler_params=_cp("parallel"), name=name)(proj, mkv, mkv, qn_w, kn_w)


def _mem_attn_bwd(proj, qcol, mkv, qn_w, kn_w, dmix, *, tt, name):
    T = proj.shape[0]
    nsteps = T // tt
    ocol = (dmix.shape[1] - MEM_WIDTH) // MEM_WIDTH

    def body(q_ref, k_ref, v_ref, qw_ref, kw_ref, do_ref, dq_ref, dkv_ref, dqw_ref, dkw_ref, dk_acc, dv_acc):
        step = pl.program_id(0)

        @pl.when(step == 0)
        def _():
            dk_acc[...] = jnp.zeros_like(dk_acc)
            dv_acc[...] = jnp.zeros_like(dv_acc)
            dqw_ref[...] = jnp.zeros_like(dqw_ref)

        qmasks = _mem_head_masks(tt)
        kmasks = _mem_head_masks(MEM_TOKENS)
        qhat, qr = _mem_head_rms(q_ref[...], qmasks)
        qn = qhat * qw_ref[...]
        khat, kr = _mem_head_rms(k_ref[...], kmasks)
        kn = (khat * kw_ref[...]).astype(BF16)
        v = v_ref[...].astype(BF16)
        dout = do_ref[...]
        dqn = jnp.zeros((tt, MEM_WIDTH), F32)
        dkn = jnp.zeros((MEM_TOKENS, MEM_WIDTH), F32)
        dvv = jnp.zeros((MEM_TOKENS, MEM_WIDTH), F32)
        for m in range(MEM_HEADS):
            qm = jnp.where(qmasks[m], qn, 0.0).astype(BF16)
            s = _dot_nt(qm, kn) * MEM_SCALE
            s = s - jnp.max(s, axis=-1, keepdims=True)
            p = jnp.exp(s)
            p = p / jnp.sum(p, axis=-1, keepdims=True)
            dom = jnp.where(qmasks[m], dout, 0.0).astype(BF16)
            dp = _dot_nt(dom, v)
            ds = (p * (dp - jnp.sum(p * dp, axis=-1, keepdims=True)) * MEM_SCALE).astype(BF16)
            dqn = jnp.where(qmasks[m], _dot(ds, kn), dqn)
            dkn = jnp.where(kmasks[m], _dot_tn(ds, qm), dkn)
            dvv = jnp.where(kmasks[m], _dot_tn(p, dom), dvv)
        dqw_ref[...] += jnp.sum(dqn * qhat, axis=0, keepdims=True)
        dq_ref[...] = _mem_head_rms_bwd(dqn * qw_ref[...], qhat, qr, qmasks).astype(BF16)
        dk_acc[...] += dkn
        dv_acc[...] += dvv

        @pl.when(step == nsteps - 1)
        def _():
            dk = dk_acc[...]
            dkw_ref[...] = jnp.sum(dk * khat, axis=0, keepdims=True)
            dkv_ref[:, :MEM_WIDTH] = _mem_head_rms_bwd(dk * kw_ref[...], khat, kr, kmasks)
            dkv_ref[:, MEM_WIDTH:] = dv_acc[...]

    return pl.pallas_call(
        body, grid=(nsteps,),
        in_specs=[pl.BlockSpec((tt, MEM_WIDTH), lambda i: (i, qcol)), pl.BlockSpec((MEM_TOKENS, MEM_WIDTH), lambda i: (0, 0)),
                  pl.BlockSpec((MEM_TOKENS, MEM_WIDTH), lambda i: (0, 1)), _full((1, MEM_WIDTH)), _full((1, MEM_WIDTH)),
                  pl.BlockSpec((tt, MEM_WIDTH), lambda i: (i, ocol))],
        out_specs=[pl.BlockSpec((tt, MEM_WIDTH), lambda i: (i, 0)), _full((MEM_TOKENS, 2 * MEM_WIDTH)),
                   _full((1, MEM_WIDTH)), _full((1, MEM_WIDTH))],
        out_shape=[jax.ShapeDtypeStruct((T, MEM_WIDTH), BF16), jax.ShapeDtypeStruct((MEM_TOKENS, 2 * MEM_WIDTH), F32),
                   jax.ShapeDtypeStruct((1, MEM_WIDTH), F32), jax.ShapeDtypeStruct((1, MEM_WIDTH), F32)],
        scratch_shapes=[pltpu.VMEM((MEM_TOKENS, MEM_WIDTH), F32), pltpu.VMEM((MEM_TOKENS, MEM_WIDTH), F32)],
        compiler_params=_cp("arbitrary"), name=name)(proj, mkv, mkv, qn_w, kn_w, dmix)


HALF = HEAD_DIM // 2
ATT_SCALE = HEAD_DIM ** -0.5
NEG = -1e30


def _rope_tables(T):
    inv = ROPE_THETA ** (-jnp.arange(HALF, dtype=F32) / HALF)
    ang = jnp.arange(T, dtype=F32)[:, None] * inv[None, :]
    cos, sin = jnp.cos(ang), jnp.sin(ang)
    return jnp.concatenate([cos, cos], axis=-1), jnp.concatenate([-sin, sin], axis=-1)


def _rope(x, cosf, sinsg):
    return x * cosf + pltpu.roll(x, HALF, 1) * sinsg


def _rope_bwd(dy, cosf, sinsg):
    return dy * cosf + pltpu.roll(dy * sinsg, HALF, 1)


def _q_prep_bwd(proj, w_heads, cosf, sinsg, dqs, *, tt, name):
    T = proj.shape[0]
    W = N_GROUPS * B_WIDTH

    def body(x_ref, w_ref, c_ref, s_ref, d0, d1, d2, dx_ref, dw_ref):
        @pl.when(pl.program_id(0) == 0)
        def _():
            dw_ref[...] = jnp.zeros_like(dw_ref)

        c, s = c_ref[...], s_ref[...]
        for gi, d_ref in enumerate((d0, d1, d2)):
            for h in range(B_HEADS):
                sl = slice((gi * B_HEADS + h) * HEAD_DIM, (gi * B_HEADS + h + 1) * HEAD_DIM)
                xhat, r = _head_rms(x_ref[:, sl])
                dyn = _rope_bwd(d_ref[:, h * HEAD_DIM:(h + 1) * HEAD_DIM], c, s)
                dw_ref[:, sl] += jnp.sum(dyn * xhat, axis=0, keepdims=True)
                dx_ref[:, sl] = _head_rms_bwd(dyn * w_ref[:, sl], xhat, r).astype(BF16)

    tbl = pl.BlockSpec((tt, HEAD_DIM), lambda i: (i, 0))
    dyb = pl.BlockSpec((tt, B_WIDTH), lambda i: (i, 0))
    return pl.pallas_call(
        body, grid=(T // tt,),
        in_specs=[pl.BlockSpec((tt, W), lambda i: (i, 0)), _full((1, W)), tbl, tbl, dyb, dyb, dyb],
        out_specs=[pl.BlockSpec((tt, W), lambda i: (i, 0)), _full((1, W))],
        out_shape=[jax.ShapeDtypeStruct((T, W), BF16), jax.ShapeDtypeStruct((1, W), F32)],
        compiler_params=_cp("arbitrary"), name=name)(proj, w_heads, cosf, sinsg, *dqs)


def _kv_prep_bwd(kv, w_heads, cosf, sinsg, dks, dvs, *, tt, name):
    T = kv.shape[0]

    def body(x_ref, w_ref, c_ref, s_ref, k0, k1, k2, v0, v1, v2, dx_ref, dw_ref):
        @pl.when(pl.program_id(0) == 0)
        def _():
            dw_ref[...] = jnp.zeros_like(dw_ref)

        c, s = c_ref[...], s_ref[...]
        for h in range(B_HEADS):
            sl = slice(h * HEAD_DIM, (h + 1) * HEAD_DIM)
            vs = slice(B_WIDTH + h * HEAD_DIM, B_WIDTH + (h + 1) * HEAD_DIM)
            xhat, r = _head_rms(x_ref[:, sl])
            dyn = _rope_bwd(k0[:, sl] + k1[:, sl] + k2[:, sl], c, s)
            dw_ref[:, sl] += jnp.sum(dyn * xhat, axis=0, keepdims=True)
            dx_ref[:, sl] = _head_rms_bwd(dyn * w_ref[:, sl], xhat, r).astype(BF16)
            dx_ref[:, vs] = (v0[:, sl] + v1[:, sl] + v2[:, sl]).astype(BF16)

    tbl = pl.BlockSpec((tt, HEAD_DIM), lambda i: (i, 0))
    dyb = pl.BlockSpec((tt, B_WIDTH), lambda i: (i, 0))
    return pl.pallas_call(
        body, grid=(T // tt,),
        in_specs=[dyb, _full((1, B_WIDTH)), tbl, tbl] + [dyb] * 6,
        out_specs=[pl.BlockSpec((tt, 2 * B_WIDTH), lambda i: (i, 0)), _full((1, B_WIDTH))],
        out_shape=[jax.ShapeDtypeStruct((T, 2 * B_WIDTH), BF16), jax.ShapeDtypeStruct((1, B_WIDTH), F32)],
        compiler_params=_cp("arbitrary"), name=name)(kv, w_heads, cosf, sinsg, *dks, *dvs)


def _band_masks(n_is_first=None):
    row = lax.broadcasted_iota(jnp.int32, (SPAN, SPAN), 0)
    col = lax.broadcasted_iota(jnp.int32, (SPAN, SPAN), 1)
    return row >= col, col >= row


def _dil_views(T, d):
    L = T // d
    return L, L // SPAN


def _dil_fwd(qr, kr, kv, gi, d, *, name):
    T = qr.shape[0]
    L, nb = _dil_views(T, d)

    def body(q_ref, kc_ref, kp_ref, vc_ref, vp_ref, o_ref, lse_ref):
        cur_ok, prev_band = _band_masks()
        prev_ok = prev_band & (pl.program_id(1) > 0)
        for h in range(B_HEADS):
            sl = slice(h * HEAD_DIM, (h + 1) * HEAD_DIM)
            q = q_ref[:, sl]
            sc = jnp.where(cur_ok, _dot_nt(q, kc_ref[:, sl]) * ATT_SCALE, NEG)
            sp = jnp.where(prev_ok, _dot_nt(q, kp_ref[:, sl]) * ATT_SCALE, NEG)
            m = jnp.maximum(jnp.max(sc, axis=-1, keepdims=True), jnp.max(sp, axis=-1, keepdims=True))
            pc = jnp.exp(sc - m)
            pp = jnp.exp(sp - m)
            l = jnp.sum(pc, axis=-1, keepdims=True) + jnp.sum(pp, axis=-1, keepdims=True)
            o_ref[:, sl] = (_dot(pc, vc_ref[:, sl]) + _dot(pp, vp_ref[:, sl])) / l
            lse_ref[:, sl] = jnp.broadcast_to(m + jnp.log(l), (SPAN, HEAD_DIM))

    blk = lambda f: pl.BlockSpec((SPAN, B_WIDTH), f)
    cur = lambda r, n: (n, r)
    prev = lambda r, n: (jnp.maximum(n - 1, 0), r)
    ov = jax.ShapeDtypeStruct((L, d * B_WIDTH), F32)
    o, lse = pl.pallas_call(
        body, grid=(d, nb),
        in_specs=[blk(lambda r, n: (n, r * N_GROUPS + gi)), blk(cur), blk(prev),
                  blk(lambda r, n: (n, 2 * r + 1)), blk(lambda r, n: (jnp.maximum(n - 1, 0), 2 * r + 1))],
        out_specs=[blk(cur), blk(cur)], out_shape=[ov, ov],
        compiler_params=_cp("parallel", "arbitrary"), name=name,
    )(qr.reshape(L, d * N_GROUPS * B_WIDTH), kr.reshape(L, d * B_WIDTH), kr.reshape(L, d * B_WIDTH),
      kv.reshape(L, d * 2 * B_WIDTH), kv.reshape(L, d * 2 * B_WIDTH))
    return o.reshape(T, B_WIDTH), lse.reshape(T, B_WIDTH)


def _dil_combine_fwd(os_, lses, *, tt, name):
    T = os_[0].shape[0]

    def body(o0, o1, o2, l0, l1, l2, y_ref, lse_ref):
        a, b, c = l0[...], l1[...], l2[...]
        m = jnp.maximum(jnp.maximum(a, b), c)
        wa, wb, wc = jnp.exp(a - m), jnp.exp(b - m), jnp.exp(c - m)
        den = wa + wb + wc
        y_ref[...] = (wa * o0[...] + wb * o1[...] + wc * o2[...]) / den
        lse_ref[...] = m + jnp.log(den)

    blk = pl.BlockSpec((tt, B_WIDTH), lambda i: (i, 0))
    sh = jax.ShapeDtypeStruct((T, B_WIDTH), F32)
    return pl.pallas_call(
        body, grid=(T // tt,), in_specs=[blk] * 6, out_specs=[blk, blk], out_shape=[sh, sh],
        compiler_params=_cp("parallel"), name=name)(*os_, *lses)


DILS_UNROLL = 4


def _dils_specs(gi, d, nblk):
    blk = lambda f: pl.BlockSpec((SPAN * d, HEAD_DIM), f)
    return {
        "q": blk(lambda h, n: (n, gi * B_HEADS + h)), "q_next": blk(lambda h, n: (jnp.minimum(n + 1, nblk - 1), gi * B_HEADS + h)),
        "cur": blk(lambda h, n: (n, h)), "prev": blk(lambda h, n: (jnp.maximum(n - 1, 0), h)),
        "next": blk(lambda h, n: (jnp.minimum(n + 1, nblk - 1), h)),
        "v": blk(lambda h, n: (n, B_HEADS + h)), "v_prev": blk(lambda h, n: (jnp.maximum(n - 1, 0), B_HEADS + h)),
    }


def _dils_fwd(qr, kr, kv, gi, d, *, name):
    T = qr.shape[0]
    nblk = T // (SPAN * d)
    sp = _dils_specs(gi, d, nblk)

    def body(q_ref, kc_ref, vc_ref, o_ref, lse_ref, k_before, v_before):
        @pl.when(pl.program_id(1) == 0)
        def _():
            k_before[...] = jnp.zeros_like(k_before)
            v_before[...] = jnp.zeros_like(v_before)

        cur_ok, prev_band = _band_masks()
        prev_ok = prev_band & (pl.program_id(1) > 0)

        def residue(r, carry):
            rows = pl.ds(r, SPAN, stride=d)
            q, kc, vc = q_ref[rows, :], kc_ref[rows, :].astype(BF16), vc_ref[rows, :].astype(BF16)
            sc = jnp.where(cur_ok, _dot_nt(q, kc) * ATT_SCALE, NEG)
            sp_ = jnp.where(prev_ok, _dot_nt(q, k_before[r]) * ATT_SCALE, NEG)
            m = jnp.maximum(jnp.max(sc, axis=-1, keepdims=True), jnp.max(sp_, axis=-1, keepdims=True))
            pc = jnp.exp(sc - m)
            pp = jnp.exp(sp_ - m)
            l = jnp.sum(pc, axis=-1, keepdims=True) + jnp.sum(pp, axis=-1, keepdims=True)
            o_ref[rows, :] = (_dot(pc, vc) + _dot(pp, v_before[r])) / l
            lse_ref[rows, :] = jnp.broadcast_to(m + jnp.log(l), (SPAN, HEAD_DIM))
            k_before[r] = kc
            v_before[r] = vc
            return carry

        lax.fori_loop(0, d, residue, 0, unroll=DILS_UNROLL)

    sh = jax.ShapeDtypeStruct((T, B_WIDTH), F32)
    return pl.pallas_call(
        body, grid=(B_HEADS, nblk), in_specs=[sp["q"], sp["cur"], sp["v"]],
        out_specs=[sp["cur"], sp["cur"]], out_shape=[sh, sh],
        scratch_shapes=[pltpu.VMEM((d, SPAN, HEAD_DIM), BF16), pltpu.VMEM((d, SPAN, HEAD_DIM), BF16)],
        compiler_params=_cp("parallel", "arbitrary"), name=name)(qr, kr, kv)


DIL_BWD_GROUP = {1: 4, 4: 1, 16: 1}


def _dil_bwd(qr, kr, kv, dmix, lse, dd, gi, d, *, name, dep=None):
    T = qr.shape[0]
    G = DIL_BWD_GROUP[d]
    band = SPAN * d
    tb = G * band
    nblk = T // tb
    n_units = T // SPAN

    def kernel_body(q_ref, dy_ref, lse_ref, dd_ref, kc_ref, kp_ref, vc_ref, vp_ref, dq_ref, dk_ref, dv_ref, dk_acc, dv_acc):
        n = pl.program_id(1)

        @pl.when(n == 0)
        def _():
            dk_acc[...] = jnp.zeros_like(dk_acc)
            dv_acc[...] = jnp.zeros_like(dv_acc)

        cur_ok, prev_band = _band_masks()
        for j in range(G):
            def residue(r, carry, j=j):
                off = j * band + r
                rows = pl.ds(off, SPAN, stride=d)
                q, dy = q_ref[rows, :], dy_ref[rows, :]
                lse_h = jnp.max(lse_ref[rows, :], axis=-1, keepdims=True)
                dd_h = jnp.max(dd_ref[rows, :], axis=-1, keepdims=True)
                kc, vc = kc_ref[rows, :], vc_ref[rows, :]
                if j > 0:
                    before = pl.ds(off - band, SPAN, stride=d)
                    kp, vp = kc_ref[before, :], vc_ref[before, :]
                    prev_ok = prev_band
                else:
                    before = pl.ds((G - 1) * band + r, SPAN, stride=d)
                    kp, vp = kp_ref[before, :], vp_ref[before, :]
                    prev_ok = prev_band & (n > 0)
                pc = jnp.exp(jnp.where(cur_ok, _dot_nt(q, kc) * ATT_SCALE, NEG) - lse_h)
                pp = jnp.exp(jnp.where(prev_ok, _dot_nt(q, kp) * ATT_SCALE, NEG) - lse_h)
                dsc = pc * (_dot_nt(dy, vc) - dd_h) * ATT_SCALE
                dsp = pp * (_dot_nt(dy, vp) - dd_h) * ATT_SCALE
                dq_ref[rows, :] = _dot(dsc, kc) + _dot(dsp, kp)
                u = (n * G + j) * d + r
                here = pl.ds(pl.multiple_of(u * SPAN, SPAN), SPAN)
                dk_acc[here, :] += _dot_tn(dsc, q)
                dv_acc[here, :] += _dot_tn(pc, dy)
                there = pl.ds(pl.multiple_of(jnp.maximum(u - d, 0) * SPAN, SPAN), SPAN)
                dk_acc[there, :] += _dot_tn(dsp, q)
                dv_acc[there, :] += _dot_tn(pp, dy)
                return carry

            lax.fori_loop(0, d, residue, 0, unroll=min(d, DILS_UNROLL))

        @pl.when(n == nblk - 1)
        def _():
            def place(u, carry):
                rows = pl.ds((u // d) * band + u % d, SPAN, stride=d)
                src = pl.ds(pl.multiple_of(u * SPAN, SPAN), SPAN)
                dk_ref[rows, :] = dk_acc[src, :]
                dv_ref[rows, :] = dv_acc[src, :]
                return carry

            lax.fori_loop(0, n_units, place, 0)

    blk = lambda f: pl.BlockSpec((tb, HEAD_DIM), f)
    cur = lambda h, n: (n, h)
    prev = lambda h, n: (jnp.maximum(n - 1, 0), h)
    whole = pl.BlockSpec((T, HEAD_DIM), lambda h, n: (0, h))
    sh = jax.ShapeDtypeStruct((T, B_WIDTH), F32)
    body, dep_specs, dep_args = _dep(kernel_body, 8, dep)
    return pl.pallas_call(
        body, grid=(B_HEADS, nblk),
        in_specs=[blk(lambda h, n: (n, gi * B_HEADS + h)), blk(cur), blk(cur), blk(cur), blk(cur), blk(prev),
                  blk(lambda h, n: (n, B_HEADS + h)), blk(lambda h, n: (jnp.maximum(n - 1, 0), B_HEADS + h))] + dep_specs,
        out_specs=[blk(cur), whole, whole], out_shape=[sh, sh, sh],
        scratch_shapes=[pltpu.VMEM((T, HEAD_DIM), F32), pltpu.VMEM((T, HEAD_DIM), F32)],
        compiler_params=_cp("parallel", "arbitrary"), name=name)(qr, dmix, lse, dd, kr, kr, kv, kv, *dep_args)


A_MQ_COL = 4 * A_WIDTH // MEM_WIDTH
B_MQ_COL = N_GROUPS * B_WIDTH // MEM_WIDTH


def _row(v):
    return v.reshape(1, -1).astype(F32)


def _local_step(x, mem, tgt, get_w, P, put_g, first_dep=None, forward_point=lambda i, value: value):
    T = x.shape[0]
    cosf, sinsg = _rope_tables(T)
    lb_soft = jax.nn.softmax(P["a_lb_logits"].astype(F32), axis=0)
    lb = lb_soft[0:1]
    qw_heads = jnp.repeat(P["b_qnorm"][0], B_HEADS, axis=0).reshape(1, -1)
    kw_heads = jnp.tile(_row(P["b_knorm"]), (1, B_HEADS))
    mqw = [jnp.tile(_row(P["mem_qnorm"][l]), (1, MEM_HEADS)) for l in range(2)]
    mkw = [jnp.tile(_row(P["mem_knorm"][l]), (1, MEM_HEADS)) for l in range(2)]
    nmix = [_row(P["norm_mix"][l]) for l in range(2)]
    nffn = [_row(P["norm_ffn"][l]) for l in range(2)]
    mnorm = [_row(P["mem_norm"][l]) for l in range(2)]
    kvn = _row(P["kv_norm"])
    onorm = _row(P["a_onorm"])
    W = {}

    def w_of(name, after=None):
        if name not in W:
            W[name] = get_w(name, after)
        return W[name]

    proj_a, xn0 = _rms_matmul(x, nmix[0], w_of("a_w_in"), tt=512, tn=1664, wt=True, name="proj_a", dep=first_dep)
    mkv0, mn0 = _rms_matmul(mem, mnorm[0], w_of("w_mem_kv0"), tt=MEM_TOKENS, tn=2 * MEM_WIDTH, wt=False, name="mem_kv0")
    o_raw, st = _hgrn2_fwd(proj_a, lb, name="hgrn2_fwd")
    o_raw = forward_point(0, o_raw)
    mm0 = _a_post_fwd(o_raw, proj_a, onorm, tt=512, name="a_post_fwd")
    mo0 = _mem_attn_fwd(proj_a, A_MQ_COL, mkv0, mqw[0], mkw[0], tt=512, name="mem_attn_fwd0")
    hm0 = _mm_res(x, mm0, mo0, w_of("w_out0", mo0), tt=512, name="out_proj0")
    hm0 = forward_point(1, hm0)
    gu0, hn0 = _rms_matmul(hm0, nffn[0], w_of("w_gate_up0", hm0), tt=512, tn=1408, wt=True, out_dtype=BF16, name="gate_up0")
    h1 = _swiglu_down(hm0, gu0, w_of("w_down0", gu0), tt=512, name="down0")
    h1 = forward_point(2, h1)
    kv, hkn, kr = _rms_matmul(h1, kvn, w_of("w_kv", h1), tt=512, tn=768, wt=True, name="kv_proj",
                              rotate=(kw_heads, cosf, sinsg))

    proj_b, xn1, qr = _rms_matmul(h1, nmix[1], w_of("b_w_in", kr), tt=512, tn=1280, wt=True, name="proj_b",
                                  rotate=(qw_heads, cosf, sinsg))
    proj_b = forward_point(3, proj_b)
    mkv1, mn1 = _rms_matmul(mem, mnorm[1], w_of("w_mem_kv1", kr), tt=MEM_TOKENS, tn=2 * MEM_WIDTH, wt=False, name="mem_kv1")
    outs = [(_dil_fwd if d == 1 else _dils_fwd)(qr, kr, kv, gi, d, name=f"dil_fwd{gi}") for gi, d in enumerate(DILATIONS)]
    mm1, lse_tot = _dil_combine_fwd([o for o, _ in outs], [s for _, s in outs], tt=512, name="dil_combine")
    mo1 = _mem_attn_fwd(proj_b, B_MQ_COL, mkv1, mqw[1], mkw[1], tt=512, name="mem_attn_fwd1")
    hm1 = _mm_res(h1, mm1, mo1, w_of("w_out1", mo1), tt=512, name="out_proj1")
    gu1, hn1 = _rms_matmul(hm1, nffn[1], w_of("w_gate_up1", hm1), tt=512, tn=1408, wt=True, out_dtype=BF16, name="gate_up1")
    dy, sq = _swiglu_down_loss(hm1, gu1, w_of("w_down1", gu1), tgt, tt=512, name="down1_loss")

    gP = {}
    zeros_mem = jnp.zeros((MEM_TOKENS, D_MODEL), F32)

    def ffn_bwd(l, dh, hm, gu, hn):
        dgu, g_wd = _swiglu_bwd(dh, gu, w_of(f"w_down{l}"), tt=256, name=f"swiglu_bwd{l}")
        g_wgu = _mm_tn(dgu, hn, tt=512, tka=1408, name=f"g_w_gate_up{l}")
        sent = put_g({f"w_down{l}": g_wd, f"w_gate_up{l}": g_wgu})
        dhm, g_nf = _rms_bwd_dx(hm, nffn[l], w_of(f"w_gate_up{l}"), dgu, dh, tt=512, wt=True, name=f"gate_up_bwd{l}", dep=sent)
        return dhm, g_nf

    def mix_bwd(l, dhm, mix_main, mix_mem, proj, qcol, mkv, mn):
        dmix, g_wout, *head_dots = _out_proj_bwd(dhm, mix_main, mix_mem, w_of(f"w_out{l}"), tt=512, name=f"out_proj_bwd{l}",
                                                 head_dots=l == 1)
        dmq, dmkv, dqw, dkw = _mem_attn_bwd(proj, qcol, mkv, mqw[l], mkw[l], dmix, tt=512, name=f"mem_attn_bwd{l}")
        g_wmkv = _mm_tn(mn, dmkv, tt=MEM_TOKENS, tka=512, name=f"g_w_mem_kv{l}")
        sent = put_g({f"w_out{l}": g_wout, f"w_mem_kv{l}": g_wmkv})
        _, g_mn = _rms_bwd_dx(mem, mnorm[l], w_of(f"w_mem_kv{l}"), dmkv, zeros_mem, tt=MEM_TOKENS, wt=False, name=f"mem_kv_bwd{l}")
        fold = lambda v: v.reshape(MEM_HEADS, MEM_HEAD_DIM).sum(axis=0)
        return dmix, dmq, g_mn, fold(dqw), fold(dkw), sent, head_dots

    dhm1, g_nf1 = ffn_bwd(1, dy, hm1, gu1, hn1)
    dmix1, dmq1, g_mn1, g_mq1, g_mk1, sent, (dd,) = mix_bwd(1, dhm1, mm1, mo1, proj_b, B_MQ_COL, mkv1, mn1)
    dqs, dks, dvs = [], [], []
    for gi, d in enumerate(DILATIONS):
        dq_g, dk_g, dv_g = _dil_bwd(qr, kr, kv, dmix1, lse_tot, dd, gi, d, name=f"dil_bwd{gi}", dep=sent if gi == 0 else None)
        dqs.append(dq_g)
        dks.append(dk_g)
        dvs.append(dv_g)
    dq_raw, dqw = _q_prep_bwd(proj_b, qw_heads, cosf, sinsg, dqs, tt=512, name="q_prep_bwd")
    dkv, dkw = _kv_prep_bwd(kv, kw_heads, cosf, sinsg, dks, dvs, tt=512, name="kv_prep_bwd")
    dproj_b = [dq_raw, dmq1]
    g_wb = _mm_tn_pieces(dproj_b, xn1, tt=512, name="g_b_w_in")
    g_wkv = _mm_tn(dkv, hkn, tt=512, tka=768, name="g_w_kv")
    sent = put_g({"b_w_in": g_wb, "w_kv": g_wkv})
    dh1, g_nm1 = _rms_bwd_dx(h1, nmix[1], w_of("b_w_in"), dproj_b, dhm1, tt=512, wt=True, name="proj_b_bwd", dep=sent)
    dh1, g_kvn = _rms_bwd_dx(h1, kvn, w_of("w_kv"), dkv, dh1, tt=512, wt=True, name="kv_proj_bwd")

    dhm0, g_nf0 = ffn_bwd(0, dh1, hm0, gu0, hn0)
    dmix0, dmq0, g_mn0, g_mq0, g_mk0, sent, _ = mix_bwd(0, dhm0, mm0, mo0, proj_a, A_MQ_COL, mkv0, mn0)
    do_raw, dg, g_onorm = _a_post_bwd(o_raw, proj_a, onorm, dmix0, tt=512, name="a_post_bwd", dep=sent)
    dq, dz, dv, dlb = _hgrn2_bwd(proj_a, lb, st, do_raw, name="hgrn2_bwd")
    dproj_a = [dq, dz, dv, dg, dmq0]
    sent = put_g({"a_w_in": _mm_tn_pieces(dproj_a, xn0, tt=512, name="g_a_w_in")})
    gx, g_nm0 = _rms_bwd_dx(x, nmix[0], w_of("a_w_in"), dproj_a, dhm0, tt=512, wt=True, name="proj_a_bwd", dep=sent)

    dl0 = lb_soft[0:1] * lb_soft[1:2] * dlb
    gP["a_lb_logits"] = jnp.concatenate([dl0, -dl0], axis=0)
    gP["a_onorm"] = g_onorm
    gP["norm_mix"] = jnp.concatenate([g_nm0, g_nm1], axis=0)
    gP["norm_ffn"] = jnp.concatenate([g_nf0, g_nf1], axis=0)
    gP["b_qnorm"] = dqw.reshape(N_GROUPS, B_HEADS, HEAD_DIM).sum(axis=1)[None]
    gP["kv_norm"] = g_kvn.reshape(-1)
    gP["b_knorm"] = dkw.reshape(B_HEADS, HEAD_DIM).sum(axis=0)
    gP["mem_norm"] = jnp.concatenate([g_mn0, g_mn1], axis=0)
    gP["mem_qnorm"] = jnp.stack([g_mq0, g_mq1])
    gP["mem_knorm"] = jnp.stack([g_mk0, g_mk1])
    return sq, gx, gP


MESH_ID = pl.DeviceIdType.MESH
HBM_SPEC = pl.BlockSpec(memory_space=pltpu.HBM)


def _position():
    return lax.axis_index("x"), lax.axis_index("y"), lax.axis_index("c")


def _all_gather(blocks, *, name):
    n = len(blocks)

    def body(*refs):
        x_refs, out_refs = refs[:n], refs[n:2 * n]
        send_sems, recv_sems, local_sems = refs[2 * n:]
        x, y, c = _position()
        me, sibling = (x, y, c), (x, y, 1 - c)
        chips = [(1 - x, y), (x, 1 - y), (1 - x, 1 - y)]

        def slot(a, px, py, pc):
            return out_refs[a].at[4 * px + 2 * py + pc]

        def copy(a, k, blk, to, src=None):
            return pltpu.make_async_remote_copy(
                src_ref=slot(a, *blk) if src is None else src, dst_ref=slot(a, *blk),
                send_sem=send_sems.at[7 * a + k], recv_sem=recv_sems.at[7 * a + k], device_id=to, device_id_type=MESH_ID)

        mine = [pltpu.make_async_copy(x_refs[a], slot(a, *me), local_sems.at[a]) for a in range(n)]
        for cp in mine:
            cp.start()
        first = []
        for a in range(n):
            first.append(copy(a, 0, me, sibling, src=x_refs[a]))
            first += [copy(a, 1 + j, me, (*chip, c), src=x_refs[a]) for j, chip in enumerate(chips)]
        for cp in first:
            cp.start()
        passed = []
        for j, chip in enumerate(chips):
            for a in range(n):
                copy(a, 1 + j, (*chip, c), me).wait_recv()
                cp = copy(a, 4 + j, (*chip, c), sibling)
                cp.start()
                passed.append(cp)
        for a in range(n):
            copy(a, 0, sibling, me).wait_recv()
            for j, chip in enumerate(chips):
                copy(a, 4 + j, (*chip, 1 - c), me).wait_recv()
        for cp in first + passed:
            cp.wait_send()
        for cp in mine:
            cp.wait()

    return pl.pallas_call(
        body, out_shape=[jax.ShapeDtypeStruct((N_DEV,) + b.shape, b.dtype) for b in blocks],
        in_specs=[HBM_SPEC] * n, out_specs=[HBM_SPEC] * n,
        scratch_shapes=[pltpu.SemaphoreType.DMA((7 * n,)), pltpu.SemaphoreType.DMA((7 * n,)), pltpu.SemaphoreType.DMA((n,))],
        name=name)(*blocks)


def _all_gather_direct(block, after, *, name):
    def body(x_ref, after_ref, out_ref, send_sems, recv_sems, local_sem):
        x, y, c = _position()
        me = 4 * x + 2 * y + c
        mine = pltpu.make_async_copy(x_ref, out_ref.at[me], local_sem)
        mine.start()
        copies = []
        for k in ALL_PEERS:
            cp = pltpu.make_async_remote_copy(
                src_ref=x_ref, dst_ref=out_ref.at[me], send_sem=send_sems.at[k - 1], recv_sem=recv_sems.at[k - 1],
                device_id=_peer(k, x, y, c), device_id_type=MESH_ID)
            cp.start()
            copies.append(cp)
        for cp in copies:
            cp.wait()
        mine.wait()

    return pl.pallas_call(
        body, out_shape=jax.ShapeDtypeStruct((N_DEV,) + block.shape, block.dtype),
        in_specs=[HBM_SPEC, pl.BlockSpec(memory_space=pl.ANY)], out_specs=HBM_SPEC,
        scratch_shapes=[pltpu.SemaphoreType.DMA((7,)), pltpu.SemaphoreType.DMA((7,)), pltpu.SemaphoreType.DMA],
        name=name)(block, after)


SEM_SPEC = pl.BlockSpec(memory_space=pltpu.SEMAPHORE)
ANY_SPEC = pl.BlockSpec(memory_space=pl.ANY)
DATAFLOW = pltpu.SideEffectType.DATAFLOW_SIDE_EFFECTING


def _peer(k, x, y, c):
    return (1 - x if (k >> 2) & 1 else x, 1 - y if (k >> 1) & 1 else y, 1 - c if k & 1 else c)


def _own_slot_filled(own_block):
    x, y, c = _position()
    zone = lax.empty((N_DEV,) + own_block.shape, own_block.dtype)
    return lax.dynamic_update_slice_in_dim(zone, own_block[None], 4 * x + 2 * y + c, axis=0)


ALL_PEERS = tuple(range(1, N_DEV))
SIBLING_AND_SAME_CORE = (1, 2, 4, 6)
SAME_CORE = (2, 4, 6)


def _split_start(srcs, scatter, after, *, name, relations=ALL_PEERS, carried=None):
    n = len(srcs)
    extra = ([] if after is None else [after]) + ([] if carried is None else [carried])
    n_carried = 0 if carried is None else 1
    x, y, c = _position()
    me = 4 * x + 2 * y + c
    lands = [_own_slot_filled(lax.dynamic_index_in_dim(s, me, 0, keepdims=False) if scatter else s) for s in srcs]

    def body(*refs):
        src_refs, land_refs = refs[:n], refs[n:2 * n]
        send_sems, recv_sems = refs[2 * n + len(extra)], refs[2 * n + len(extra) + 1]
        token = refs[2 * n + len(extra) + 2 + 2 * n]
        bx, by, bc = _position()
        bme = 4 * bx + 2 * by + bc
        for a in range(n):
            for k in relations:
                tx, ty, tc = _peer(k, bx, by, bc)
                src = src_refs[a].at[4 * tx + 2 * ty + tc] if scatter else src_refs[a]
                pltpu.make_async_remote_copy(
                    src_ref=src, dst_ref=land_refs[a].at[bme],
                    send_sem=send_sems.at[7 * a + k - 1], recv_sem=recv_sems.at[7 * a + k - 1],
                    device_id=(tx, ty, tc), device_id_type=MESH_ID).start()
        token[...] = jnp.zeros_like(token)

    hbm = lambda a: pltpu.HBM(a.shape, a.dtype)
    outs = pl.pallas_call(
        body, name=name,
        out_shape=(pltpu.SemaphoreType.DMA((7 * n,)), pltpu.SemaphoreType.DMA((7 * n,)),
                   *[hbm(s) for s in srcs], *[hbm(l) for l in lands], jax.ShapeDtypeStruct((8, 128), F32),
                   *([hbm(carried)] if n_carried else [])),
        in_specs=[HBM_SPEC] * (2 * n) + [ANY_SPEC] * len(extra),
        out_specs=(SEM_SPEC, SEM_SPEC, *[HBM_SPEC] * (2 * n), pl.BlockSpec(memory_space=pltpu.VMEM), *([ANY_SPEC] * n_carried)),
        input_output_aliases={**{i: 2 + i for i in range(2 * n)},
                              **({2 * n + len(extra) - 1: 2 * n + 3} if n_carried else {})},
        compiler_params=pltpu.CompilerParams(has_side_effects=DATAFLOW),
    )(*[pltpu.with_memory_space_constraint(s, pltpu.HBM) for s in srcs],
      *[pltpu.with_memory_space_constraint(l, pltpu.HBM) for l in lands], *extra)
    return {"n": n, "relations": relations, "send": outs[0], "recv": outs[1], "srcs": list(outs[2:2 + n]),
            "lands": list(outs[2 + n:2 + 2 * n]), "token": outs[2 * n + 2], "carried": outs[-1] if n_carried else None}


def _forward_start(lands, carried, *, name):
    n = len(lands)

    def body(*refs):
        land_refs = refs[:n]
        send_sems, recv_sems = refs[n + 1], refs[n + 2]
        bx, by, bc = _position()
        for a in range(n):
            for k in SAME_CORE:
                tx, ty, tc = _peer(k, bx, by, bc)
                block = land_refs[a].at[4 * tx + 2 * ty + tc]
                pltpu.make_async_remote_copy(
                    src_ref=block, dst_ref=block,
                    send_sem=send_sems.at[7 * a + k - 1], recv_sem=recv_sems.at[7 * a + k - 1],
                    device_id=(bx, by, 1 - bc), device_id_type=MESH_ID).start()

    hbm = lambda a: pltpu.HBM(a.shape, a.dtype)
    outs = pl.pallas_call(
        body, name=name,
        out_shape=(pltpu.SemaphoreType.DMA((7 * n,)), pltpu.SemaphoreType.DMA((7 * n,)),
                   *[hbm(l) for l in lands], hbm(carried)),
        in_specs=[HBM_SPEC] * n + [ANY_SPEC],
        out_specs=(SEM_SPEC, SEM_SPEC, *[HBM_SPEC] * n, ANY_SPEC),
        input_output_aliases={i: 2 + i for i in range(n + 1)},
        compiler_params=pltpu.CompilerParams(has_side_effects=DATAFLOW),
    )(*lands, carried)
    handle = {"n": n, "relations": SAME_CORE, "send": outs[0], "recv": outs[1], "srcs": [], "lands": list(outs[2:2 + n])}
    return handle, outs[-1]


def _split_wait(handle, after, *, name):
    n, ns = handle["n"], len(handle["srcs"])

    def body(*refs):
        land_refs = refs[ns:ns + n]
        send_sems, recv_sems = refs[ns + n], refs[ns + n + 1]
        bx, by, bc = _position()
        for a in range(n):
            for k in handle["relations"]:
                block = land_refs[a].at[0]
                cp = pltpu.make_async_remote_copy(
                    src_ref=block, dst_ref=block,
                    send_sem=send_sems.at[7 * a + k - 1], recv_sem=recv_sems.at[7 * a + k - 1],
                    device_id=_peer(k, bx, by, bc), device_id_type=MESH_ID)
                cp.wait_send()
                cp.wait_recv()

    hbm = lambda a: pltpu.HBM(a.shape, a.dtype)
    outs = pl.pallas_call(
        body, name=name,
        out_shape=(*[hbm(s) for s in handle["srcs"]], *[hbm(l) for l in handle["lands"]]),
        in_specs=[HBM_SPEC] * (ns + n) + [SEM_SPEC, SEM_SPEC, ANY_SPEC],
        out_specs=tuple([HBM_SPEC] * (ns + n)),
        input_output_aliases={i: i for i in range(ns + n)},
        compiler_params=pltpu.CompilerParams(has_side_effects=DATAFLOW),
    )(*handle["srcs"], *handle["lands"], handle["send"], handle["recv"], after)
    return list(outs[ns:])


def _sum_sources(parts, *, tr, name):
    n, R, C = parts.shape

    def body(p_ref, o_ref):
        acc = p_ref[0].astype(F32)
        for s in range(1, n):
            acc = acc + p_ref[s].astype(F32)
        o_ref[...] = acc

    return pl.pallas_call(
        body, grid=(R // tr,), in_specs=[pl.BlockSpec((n, tr, C), lambda i: (0, i, 0))],
        out_specs=pl.BlockSpec((tr, C), lambda i: (i, 0)),
        out_shape=jax.ShapeDtypeStruct((R, C), F32), compiler_params=_cp("parallel"), name=name)(parts)


def _adamw_math(g, w, m, v):
    c1 = 1.0 - ADAM_B1 ** ADAM_STEP
    c2 = 1.0 - ADAM_B2 ** ADAM_STEP
    nm = ADAM_B1 * m + (1.0 - ADAM_B1) * g
    nv = ADAM_B2 * v + (1.0 - ADAM_B2) * (g * g)
    return -ADAM_LR * ((nm / c1) / (jnp.sqrt(nv / c2) + ADAM_EPS) + ADAM_WD * w), nm, nv


def _reduce_adamw(received, w, m, v, *, tr, name):
    L, R, C = w.shape

    def body(*refs):
        p_refs = refs[:L]
        w_ref, m_ref, v_ref, g_ref, d_ref, nm_ref, nv_ref = refs[L:]
        for l in range(L):
            @pl.when(pl.program_id(0) == l)
            def _(p_ref=p_refs[l]):
                acc = p_ref[0].astype(F32)
                for s in range(1, N_DEV):
                    acc = acc + p_ref[s].astype(F32)
                g_ref[...] = acc
                d_ref[...], nm_ref[...], nv_ref[...] = _adamw_math(acc, w_ref[...], m_ref[...], v_ref[...])

    p_spec = pl.BlockSpec((N_DEV, tr, C), lambda l, i: (0, i, 0))
    blk = pl.BlockSpec((None, tr, C), lambda l, i: (l, i, 0))
    sh = jax.ShapeDtypeStruct((L, R, C), F32)
    return pl.pallas_call(
        body, grid=(L, R // tr), in_specs=[p_spec] * L + [blk] * 3, out_specs=[blk] * 4, out_shape=[sh] * 4,
        compiler_params=_cp("parallel", "parallel"), name=name)(*received, w, m, v)


def _adamw(g, w, m, v, *, tr, name):
    L, R, C = w.shape

    def body(g_ref, w_ref, m_ref, v_ref, d_ref, nm_ref, nv_ref):
        d_ref[...], nm_ref[...], nv_ref[...] = _adamw_math(g_ref[...], w_ref[...], m_ref[...], v_ref[...])

    blk = pl.BlockSpec((None, tr, C), lambda l, i: (l, i, 0))
    sh = jax.ShapeDtypeStruct((L, R, C), F32)
    return pl.pallas_call(
        body, grid=(L, R // tr), in_specs=[blk] * 4, out_specs=[blk] * 3, out_shape=[sh] * 3,
        compiler_params=_cp("parallel", "parallel"), name=name)(g, w, m, v)


UNITS = {
    "a_w_in": ("a_w_in", 0, True), "w_mem_kv0": ("w_mem_kv", 0, False), "w_out0": ("w_out", 0, False),
    "w_gate_up0": ("w_gate_up", 0, True), "w_down0": ("w_down", 0, False), "w_kv": ("w_kv", None, True),
    "b_w_in": ("b_w_in", 0, True), "w_mem_kv1": ("w_mem_kv", 1, False), "w_out1": ("w_out", 1, False),
    "w_gate_up1": ("w_gate_up", 1, True), "w_down1": ("w_down", 1, False),
}
BIG = ("a_w_in", "b_w_in", "w_kv", "w_mem_kv", "w_out", "w_gate_up", "w_down")
ADAMW_ROW_TILE = {"a_w_in": 208, "b_w_in": 160, "w_kv": 192, "w_mem_kv": 128, "w_out": 128, "w_gate_up": 176, "w_down": 176}


def _wire_block(weights, unit):
    name, layer, col = UNITS[unit]
    a = weights[name] if layer is None else weights[name][layer]
    return (a.T if col else a).astype(BF16)


SMALL_REPLICATED = ("norm_mix", "norm_ffn", "b_qnorm", "kv_norm", "b_knorm", "mem_norm", "mem_qnorm", "mem_knorm")
SMALL_SHARDED = ("a_lb_logits", "a_onorm")
SMALL_ORDER = SMALL_REPLICATED + SMALL_SHARDED
LANES = 128


def _prod(shape):
    n = 1
    for s in shape:
        n *= s
    return n


def _pack_flat(arrays, rows, cols, dtype):
    flat = jnp.concatenate([a.reshape(-1).astype(dtype) for a in arrays])
    return jnp.pad(flat, (0, rows * cols - flat.shape[0])).reshape(rows, cols)


def _unpack_flat(packed, shapes):
    flat = packed.reshape(-1)
    out, off = [], 0
    for s in shapes:
        out.append(flat[off:off + _prod(s)].reshape(s))
        off += _prod(s)
    return out


def kernel(x, mem, norm_mix, norm_ffn, a_w_in, a_lb_logits, a_onorm, b_w_in, b_qnorm, kv_norm, w_kv, b_knorm, mem_norm, w_mem_kv, mem_qnorm, mem_knorm, w_out, w_gate_up, w_down, loss_target, m_norm_mix, m_norm_ffn, m_a_w_in, m_a_lb_logits, m_a_onorm, m_b_w_in, m_b_qnorm, m_kv_norm, m_w_kv, m_b_knorm, m_mem_norm, m_w_mem_kv, m_mem_qnorm, m_mem_knorm, m_w_out, m_w_gate_up, m_w_down, v_norm_mix, v_norm_ffn, v_a_w_in, v_a_lb_logits, v_a_onorm, v_b_w_in, v_b_qnorm, v_kv_norm, v_w_kv, v_b_knorm, v_mem_norm, v_w_mem_kv, v_mem_qnorm, v_mem_knorm, v_w_out, v_w_gate_up, v_w_down):
    names = ("norm_mix", "norm_ffn", "a_w_in", "a_lb_logits", "a_onorm", "b_w_in", "b_qnorm", "kv_norm", "w_kv", "b_knorm",
             "mem_norm", "w_mem_kv", "mem_qnorm", "mem_knorm", "w_out", "w_gate_up", "w_down")
    w = dict(zip(names, (norm_mix, norm_ffn, a_w_in, a_lb_logits, a_onorm, b_w_in, b_qnorm, kv_norm, w_kv, b_knorm,
                         mem_norm, w_mem_kv, mem_qnorm, mem_knorm, w_out, w_gate_up, w_down)))
    m = dict(zip(names, (m_norm_mix, m_norm_ffn, m_a_w_in, m_a_lb_logits, m_a_onorm, m_b_w_in, m_b_qnorm, m_kv_norm, m_w_kv,
                         m_b_knorm, m_mem_norm, m_w_mem_kv, m_mem_qnorm, m_mem_knorm, m_w_out, m_w_gate_up, m_w_down)))
    v = dict(zip(names, (v_norm_mix, v_norm_ffn, v_a_w_in, v_a_lb_logits, v_a_onorm, v_b_w_in, v_b_qnorm, v_kv_norm, v_w_kv,
                         v_b_knorm, v_mem_norm, v_w_mem_kv, v_mem_qnorm, v_mem_knorm, v_w_out, v_w_gate_up, v_w_down)))

    first = ["a_w_in", "w_mem_kv0"]
    gathered = _all_gather([_wire_block(w, u) for u in first] + [_pack_flat([a_lb_logits, a_onorm], 8, LANES, F32)],
                           name="gather_first")
    full = {u: g.reshape(-1, g.shape[-1]) for u, g in zip(first, gathered)}
    small_in = gathered[-1].reshape(N_DEV, -1)
    P = {n: w[n] for n in SMALL_REPLICATED}
    P["a_lb_logits"] = small_in[:, :192].reshape(N_DEV, 2, 96).transpose(1, 0, 2).reshape(2, A_WIDTH)
    P["a_onorm"] = small_in[:, 192:288].reshape(1, A_WIDTH)
    later = [["w_out0", "w_gate_up0"], ["w_down0", "w_kv"], ["b_w_in", "w_mem_kv1"], ["w_out1", "w_gate_up1", "w_down1"]]
    first_half, second_half = {}, {}

    def start_first_half(i, after, carried=None):
        first_half[i] = _split_start([_wire_block(w, u) for u in later[i]], False, after, name=f"gather{i}_start",
                                     relations=SIBLING_AND_SAME_CORE, carried=carried)
        return first_half[i]

    token = start_first_half(0, gathered[-1])["token"]
    token = start_first_half(1, token)["token"]

    def forward_point(i, value):
        landed = _split_wait(first_half[i], value, name=f"gather{i}_landed")
        second_half[i], value = _forward_start(landed, value, name=f"gather{i}_forward")
        if i + 2 < len(later):
            value = start_first_half(i + 2, None, carried=value)["carried"]
        return value

    def get_w(unit, after):
        if unit not in full:
            i = [unit in group for group in later].index(True)
            for u, land in zip(later[i], _split_wait(second_half[i], after, name=f"gather{i}_wait")):
                full[u] = land.reshape(-1, land.shape[-1])
        return full[unit]

    sent = []

    def put_g(group):
        units = list(group)
        handle = _split_start([group[u].reshape(N_DEV, -1, group[u].shape[-1]) for u in units], True, None,
                              name=f"scatter{len(sent)}_start")
        sent.append((units, handle))
        return handle["token"]

    sq, gx, gP = _local_step(x[0], mem[0], loss_target[0], get_w, P, put_g, first_dep=token, forward_point=forward_point)
    loss_here = (0.5 * jnp.sum(sq) / D_MODEL).reshape(1)

    received = {}
    group_of = {u: i for i, (units, _) in enumerate(sent) for u in units}
    out = {"grad": {}, "delta": {}, "new_m": {}, "new_v": {}}
    newest = [gx]

    def update_big(n):
        shape = w[n].shape
        as3 = lambda a: a.reshape((-1,) + shape[-2:])
        mine = [u for u, (wn, _, _) in UNITS.items() if wn == n]
        for i in sorted({group_of[u] for u in mine}):
            if sent[i][0][0] not in received:
                received.update(zip(sent[i][0], _split_wait(sent[i][1], newest[0], name=f"scatter{i}_wait")))
        flip = (lambda a: jnp.swapaxes(a, 1, 2)) if UNITS[mine[0]][2] else (lambda a: a)
        res = _reduce_adamw([received[u] for u in mine], flip(as3(w[n])), flip(as3(m[n])), flip(as3(v[n])),
                            tr=ADAMW_ROW_TILE[n], name=f"adamw_{n}")
        newest[0] = res[1]
        for kind, r in zip(("grad", "delta", "new_m", "new_v"), res):
            out[kind][n] = flip(r).reshape(shape)

    for n in ("w_down", "w_gate_up", "w_out", "w_mem_kv", "b_w_in", "w_kv"):
        update_big(n)

    full_shapes = [(2, A_WIDTH) if n == "a_lb_logits" else (1, A_WIDTH) if n == "a_onorm" else w[n].shape for n in SMALL_ORDER]
    n_small = sum(_prod(s) for s in full_shapes) + 1
    rows_small = -(-n_small // (8 * LANES)) * 8
    g_all = _all_gather_direct(_pack_flat([gP[n] for n in SMALL_ORDER] + [loss_here], rows_small, LANES, F32),
                               newest[0], name="gather_small_grads")
    summed = _unpack_flat(_sum_sources(g_all, tr=rows_small, name="sum_small_grads"), full_shapes + [(1,)])
    g_small = dict(zip(SMALL_ORDER, summed))
    loss = summed[-1].reshape(())
    me = 4 * lax.axis_index("x") + 2 * lax.axis_index("y") + lax.axis_index("c")
    for n in SMALL_SHARDED:
        g_small[n] = lax.dynamic_slice_in_dim(g_small[n], me * 96, 96, axis=1)
    shapes = [w[n].shape for n in SMALL_ORDER]
    rows_upd = -(-sum(_prod(s) for s in shapes) // (8 * LANES)) * 8
    pk = lambda d: _pack_flat([d[n] for n in SMALL_ORDER], rows_upd, LANES, F32)
    res = _adamw(pk(g_small)[None], pk(w)[None], pk(m)[None], pk(v)[None], tr=rows_upd, name="adamw_small")
    out["grad"].update(g_small)
    for kind, packed in zip(("delta", "new_m", "new_v"), res):
        out[kind].update(zip(SMALL_ORDER, _unpack_flat(packed[0], shapes)))
    newest[0] = res[0]
    update_big("a_w_in")

    return (loss, gx[None], *[out["grad"][n] for n in names], *[out["delta"][n] for n in names],
            *[out["new_m"][n] for n in names], *[out["new_v"][n] for n in names])
```

```python
import functools

import jax
import jax.numpy as jnp
from jax import lax
from jax.experimental import pallas as pl
from jax.experimental.pallas import tpu as pltpu

F32 = jnp.float32
BF16 = jnp.bfloat16

N_DEV = 8
D_MODEL = 1024
HEAD_DIM = 128
A_HEADS = 6
A_WIDTH = A_HEADS * HEAD_DIM
CHUNK = 64
B_HEADS = 6
B_WIDTH = B_HEADS * HEAD_DIM
DILATIONS = (1, 4, 16)
SPAN = 128
N_GROUPS = 3
ROPE_THETA = 10000.0
MEM_TOKENS = 256
MEM_HEADS = 4
MEM_HEAD_DIM = 64
MEM_WIDTH = MEM_HEADS * MEM_HEAD_DIM
FFN_HIDDEN = 2816
EPS = 1e-6

ADAM_LR = 0.001
ADAM_B1 = 0.9
ADAM_B2 = 0.999
ADAM_EPS = 1e-08
ADAM_WD = 0.01
ADAM_STEP = 10

V7X_VMEM_LIMIT_BYTES = 56 * 1024 * 1024

NT_DIMS = (((1,), (1,)), ((), ()))
TN_DIMS = (((0,), (0,)), ((), ()))


def _cp(*sem):
    return pltpu.CompilerParams(dimension_semantics=sem, vmem_limit_bytes=V7X_VMEM_LIMIT_BYTES)


def _dot(a, b):
    return jnp.dot(a.astype(BF16), b.astype(BF16), preferred_element_type=F32)


def _dot_nt(a, b):
    return lax.dot_general(a.astype(BF16), b.astype(BF16), NT_DIMS, preferred_element_type=F32)


def _dot_tn(a, b):
    return lax.dot_general(a.astype(BF16), b.astype(BF16), TN_DIMS, preferred_element_type=F32)


def _dot3(m01, x):
    hi = x.astype(BF16)
    r1 = x - hi.astype(F32)
    mid = r1.astype(BF16)
    lo = (r1 - mid.astype(F32)).astype(BF16)
    d = functools.partial(jnp.dot, preferred_element_type=F32)
    return d(m01, hi) + d(m01, mid) + d(m01, lo)


def _sigmoid(x):
    return 1.0 / (1.0 + jnp.exp(-x))


def _full(shape):
    return pl.BlockSpec(shape, lambda *_: (0,) * len(shape))


def _dep(body, n_in, dep):
    if dep is None:
        return body, [], []

    def with_dep(*refs):
        return body(*refs[:n_in], *refs[n_in + 1:])

    return with_dep, [pl.BlockSpec(memory_space=pl.ANY)], [dep]


def _rms_matmul(x, g, w, *, tt, tn, wt, name, out_dtype=F32, dep=None, rotate=None):
    T, K = x.shape
    N = w.shape[0] if wt else w.shape[1]
    n_rot = 0 if rotate is None else rotate[0].shape[1] // HEAD_DIM
    extra_in = [] if rotate is None else list(rotate)

    def kernel_body(x_ref, g_ref, w_ref, *rest):
        y_ref, xn_ref = rest[len(extra_in)], rest[len(extra_in) + 1]
        xf = x_ref[...]
        r = lax.rsqrt(jnp.mean(xf * xf, axis=-1, keepdims=True) + EPS)
        xn = (xf * r * g_ref[...]).astype(BF16)
        xn_ref[...] = xn
        for j in range(N // tn):
            cols = slice(j * tn, (j + 1) * tn)
            y = _dot_nt(xn, w_ref[cols, :]) if wt else _dot(xn, w_ref[:, cols])
            y_ref[:, cols] = y.astype(out_dtype)
            for h in range(j * tn // HEAD_DIM, min((j + 1) * tn // HEAD_DIM, n_rot)):
                gw_ref, c_ref, s_ref, yr_ref = rest[0], rest[1], rest[2], rest[len(extra_in) + 2]
                sl = slice(h * HEAD_DIM, (h + 1) * HEAD_DIM)
                xhat, _ = _head_rms(y[:, h * HEAD_DIM - j * tn:(h + 1) * HEAD_DIM - j * tn])
                yr_ref[:, sl] = _rope(xhat * gw_ref[:, sl], c_ref[...], s_ref[...])

    tbl = pl.BlockSpec((tt, HEAD_DIM), lambda i: (i, 0))
    rot_specs = [] if rotate is None else [_full((1, n_rot * HEAD_DIM)), tbl, tbl]
    body, dep_specs, dep_args = _dep(kernel_body, 3 + len(extra_in), dep)
    return pl.pallas_call(
        body, grid=(T // tt,),
        in_specs=[pl.BlockSpec((tt, K), lambda i: (i, 0)), _full((1, K)), _full(w.shape)] + rot_specs + dep_specs,
        out_specs=[pl.BlockSpec((tt, N), lambda i: (i, 0)), pl.BlockSpec((tt, K), lambda i: (i, 0))]
        + ([] if rotate is None else [pl.BlockSpec((tt, n_rot * HEAD_DIM), lambda i: (i, 0))]),
        out_shape=[jax.ShapeDtypeStruct((T, N), out_dtype), jax.ShapeDtypeStruct((T, K), BF16)]
        + ([] if rotate is None else [jax.ShapeDtypeStruct((T, n_rot * HEAD_DIM), F32)]),
        compiler_params=_cp("parallel"), name=name)(x, g, w, *extra_in, *dep_args)


def _mm_res(res, a1, a2, w, *, tt, name):
    T, K1 = a1.shape
    K2 = a2.shape[1]
    N = w.shape[1]

    def body(r_ref, a1_ref, a2_ref, w_ref, o_ref):
        o_ref[...] = r_ref[...] + _dot(a1_ref[...], w_ref[:K1, :]) + _dot(a2_ref[...], w_ref[K1:, :])

    return pl.pallas_call(
        body, grid=(T // tt,),
        in_specs=[pl.BlockSpec((tt, N), lambda i: (i, 0)), pl.BlockSpec((tt, K1), lambda i: (i, 0)),
                  pl.BlockSpec((tt, K2), lambda i: (i, 0)), _full((K1 + K2, N))],
        out_specs=pl.BlockSpec((tt, N), lambda i: (i, 0)),
        out_shape=jax.ShapeDtypeStruct((T, N), F32),
        compiler_params=_cp("parallel"), name=name)(res, a1, a2, w)


def _swiglu_down(h, gu, wd, *, tt, name):
    T, D = h.shape
    Fh = wd.shape[0]

    def body(h_ref, gt_ref, up_ref, w_ref, o_ref):
        gt = gt_ref[...].astype(F32)
        act = gt * _sigmoid(gt) * up_ref[...].astype(F32)
        o_ref[...] = h_ref[...] + _dot(act, w_ref[...])

    return pl.pallas_call(
        body, grid=(T // tt,),
        in_specs=[pl.BlockSpec((tt, D), lambda i: (i, 0)), pl.BlockSpec((tt, Fh), lambda i: (i, 0)),
                  pl.BlockSpec((tt, Fh), lambda i: (i, 1)), _full((Fh, D))],
        out_specs=pl.BlockSpec((tt, D), lambda i: (i, 0)),
        out_shape=jax.ShapeDtypeStruct((T, D), F32),
        compiler_params=_cp("parallel"), name=name)(h, gu, gu, wd)


def _swiglu_down_loss(h, gu, wd, tgt, *, tt, name):
    T, D = h.shape
    Fh = wd.shape[0]

    def body(h_ref, gt_ref, up_ref, w_ref, t_ref, dy_ref, acc_ref):
        @pl.when(pl.program_id(0) == 0)
        def _():
            acc_ref[...] = jnp.zeros_like(acc_ref)

        gt = gt_ref[...].astype(F32)
        act = gt * _sigmoid(gt) * up_ref[...].astype(F32)
        e = h_ref[...] + _dot(act, w_ref[...]) - t_ref[...]
        dy_ref[...] = e * (1.0 / D)
        acc_ref[...] += jnp.sum(e * e, axis=0, keepdims=True)

    row = pl.BlockSpec((tt, D), lambda i: (i, 0))
    return pl.pallas_call(
        body, grid=(T // tt,),
        in_specs=[row, pl.BlockSpec((tt, Fh), lambda i: (i, 0)), pl.BlockSpec((tt, Fh), lambda i: (i, 1)), _full((Fh, D)), row],
        out_specs=[row, _full((1, D))],
        out_shape=[jax.ShapeDtypeStruct((T, D), F32), jax.ShapeDtypeStruct((1, D), F32)],
        compiler_params=_cp("arbitrary"), name=name)(h, gu, gu, wd, tgt)


def _swiglu_bwd(dh, gu, wd, *, tt, name):
    T, D = dh.shape
    Fh = wd.shape[0]
    last = T // tt - 1

    def body(dh_ref, gt_ref, up_ref, w_ref, dgu_ref, gw_ref, acc):
        @pl.when(pl.program_id(0) == 0)
        def _():
            acc[...] = jnp.zeros_like(acc)

        gt = gt_ref[...].astype(F32)
        up = up_ref[...].astype(F32)
        s = _sigmoid(gt)
        silu = gt * s
        dh16 = dh_ref[...].astype(BF16)
        dact = _dot_nt(dh16, w_ref[...])
        acc[...] += _dot_tn((silu * up).astype(BF16), dh16)
        dgu_ref[:, :Fh] = (dact * up * (s * (1.0 + gt * (1.0 - s)))).astype(BF16)
        dgu_ref[:, Fh:] = (dact * silu).astype(BF16)

        @pl.when(pl.program_id(0) == last)
        def _():
            gw_ref[...] = acc[...].astype(BF16)

    return pl.pallas_call(
        body, grid=(T // tt,),
        in_specs=[pl.BlockSpec((tt, D), lambda i: (i, 0)), pl.BlockSpec((tt, Fh), lambda i: (i, 0)),
                  pl.BlockSpec((tt, Fh), lambda i: (i, 1)), _full((Fh, D))],
        out_specs=[pl.BlockSpec((tt, 2 * Fh), lambda i: (i, 0)), _full((Fh, D))],
        out_shape=[jax.ShapeDtypeStruct((T, 2 * Fh), BF16), jax.ShapeDtypeStruct((Fh, D), BF16)],
        scratch_shapes=[pltpu.VMEM((Fh, D), F32)],
        compiler_params=_cp("arbitrary"), name=name)(dh, gu, gu, wd)


def _out_proj_bwd(dy, a1, a2, w, *, tt, name, head_dots=False):
    T, N = dy.shape
    K1, K2 = a1.shape[1], a2.shape[1]
    K = K1 + K2
    last = T // tt - 1

    def body(dy_ref, a1_ref, a2_ref, w_ref, da_ref, gw_ref, *rest):
        acc = rest[-1]

        @pl.when(pl.program_id(0) == 0)
        def _():
            acc[...] = jnp.zeros_like(acc)

        dy16 = dy_ref[...].astype(BF16)
        da = _dot_nt(dy16, w_ref[...])
        da_ref[...] = da
        acc[:K1, :] += _dot_tn(a1_ref[...], dy16)
        acc[K1:, :] += _dot_tn(a2_ref[...], dy16)
        if head_dots:
            for h in range(K1 // HEAD_DIM):
                sl = slice(h * HEAD_DIM, (h + 1) * HEAD_DIM)
                rest[0][:, sl] = jnp.broadcast_to(jnp.sum(da[:, sl] * a1_ref[:, sl], axis=-1, keepdims=True), (tt, HEAD_DIM))

        @pl.when(pl.program_id(0) == last)
        def _():
            gw_ref[...] = acc[...].astype(BF16)

    extra_specs = [pl.BlockSpec((tt, K1), lambda i: (i, 0))] if head_dots else []
    extra_shapes = [jax.ShapeDtypeStruct((T, K1), F32)] if head_dots else []
    return pl.pallas_call(
        body, grid=(T // tt,),
        in_specs=[pl.BlockSpec((tt, N), lambda i: (i, 0)), pl.BlockSpec((tt, K1), lambda i: (i, 0)),
                  pl.BlockSpec((tt, K2), lambda i: (i, 0)), _full((K, N))],
        out_specs=[pl.BlockSpec((tt, K), lambda i: (i, 0)), _full((K, N))] + extra_specs,
        out_shape=[jax.ShapeDtypeStruct((T, K), F32), jax.ShapeDtypeStruct((K, N), BF16)] + extra_shapes,
        scratch_shapes=[pltpu.VMEM((K, N), F32)],
        compiler_params=_cp("arbitrary"), name=name)(dy, a1, a2, w)


def _mm_tn(a, b, *, tt, tka, name):
    T, Ka = a.shape
    N = b.shape[1]
    last = T // tt - 1

    def body(a_ref, b_ref, o_ref, acc):
        @pl.when(pl.program_id(1) == 0)
        def _():
            acc[...] = jnp.zeros_like(acc)

        acc[...] += _dot_tn(a_ref[...], b_ref[...])

        @pl.when(pl.program_id(1) == last)
        def _():
            o_ref[...] = acc[...].astype(BF16)

    return pl.pallas_call(
        body, grid=(Ka // tka, T // tt),
        in_specs=[pl.BlockSpec((tt, tka), lambda j, t: (t, j)), pl.BlockSpec((tt, N), lambda j, t: (t, 0))],
        out_specs=pl.BlockSpec((tka, N), lambda j, t: (j, 0)),
        out_shape=jax.ShapeDtypeStruct((Ka, N), BF16),
        scratch_shapes=[pltpu.VMEM((tka, N), F32)],
        compiler_params=_cp("parallel", "arbitrary"), name=name)(a, b)


def _mm_tn_pieces(pieces, b, *, tt, name):
    n = len(pieces)
    T = b.shape[0]
    N = b.shape[1]
    widths = [p.shape[1] for p in pieces]
    Ka = sum(widths)
    last = T // tt - 1

    def body(*refs):
        p_refs = refs[:n]
        b_ref, o_ref, acc = refs[n:]

        @pl.when(pl.program_id(0) == 0)
        def _():
            acc[...] = jnp.zeros_like(acc)

        bv = b_ref[...].astype(BF16)
        off = 0
        for p_ref, wd in zip(p_refs, widths):
            acc[off:off + wd, :] += _dot_tn(p_ref[...], bv)
            off += wd

        @pl.when(pl.program_id(0) == last)
        def _():
            o_ref[...] = acc[...].astype(BF16)

    return pl.pallas_call(
        body, grid=(T // tt,),
        in_specs=[pl.BlockSpec((tt, wd), lambda t: (t, 0)) for wd in widths] + [pl.BlockSpec((tt, N), lambda t: (t, 0))],
        out_specs=_full((Ka, N)), out_shape=jax.ShapeDtypeStruct((Ka, N), BF16),
        scratch_shapes=[pltpu.VMEM((Ka, N), F32)],
        compiler_params=_cp("arbitrary"), name=name)(*pieces, b)


def _rms_bwd_dx(x, g, w, dy, dres, *, tt, wt, name, dep=None):
    pieces = list(dy) if isinstance(dy, (list, tuple)) else [dy]
    n = len(pieces)
    widths = [p.shape[1] for p in pieces]
    T, K = x.shape

    def kernel_body(x_ref, g_ref, w_ref, *rest):
        dy_refs = rest[:n]
        dres_ref, dx_ref, dg_ref = rest[n:]

        @pl.when(pl.program_id(0) == 0)
        def _():
            dg_ref[...] = jnp.zeros_like(dg_ref)

        if n == 1:
            dxn = (_dot if wt else _dot_nt)(dy_refs[0][...], w_ref[...])
        else:
            dxn, off = 0.0, 0
            for dy_ref, wd in zip(dy_refs, widths):
                dxn = dxn + _dot(dy_ref[...], w_ref[off:off + wd, :])
                off += wd
        xf = x_ref[...]
        r = lax.rsqrt(jnp.mean(xf * xf, axis=-1, keepdims=True) + EPS)
        xhat = xf * r
        dg_ref[...] += jnp.sum(dxn * xhat, axis=0, keepdims=True)
        dxhat = dxn * g_ref[...]
        dx_ref[...] = dres_ref[...] + r * (dxhat - xhat * jnp.mean(dxhat * xhat, axis=-1, keepdims=True))

    assert n == 1 or wt
    body, dep_specs, dep_args = _dep(kernel_body, 4 + n, dep)
    return pl.pallas_call(
        body, grid=(T // tt,),
        in_specs=[pl.BlockSpec((tt, K), lambda i: (i, 0)), _full((1, K)), _full(w.shape)]
        + [pl.BlockSpec((tt, wd), lambda i: (i, 0)) for wd in widths]
        + [pl.BlockSpec((tt, K), lambda i: (i, 0))] + dep_specs,
        out_specs=[pl.BlockSpec((tt, K), lambda i: (i, 0)), _full((1, K))],
        out_shape=[jax.ShapeDtypeStruct((T, K), F32), jax.ShapeDtypeStruct((1, K), F32)],
        compiler_params=_cp("arbitrary"), name=name)(x, g, w, *pieces, dres, *dep_args)


HGRN_TB = 512
HGRN_NCH = HGRN_TB // CHUNK
HGRN_HPB = 6


def _hgrn_chunk_fwd(q, z, lbv, tril01):
    sig = _sigmoid(z)
    f = lbv + (1.0 - lbv) * sig
    kk = 1.0 - f
    b = _dot3(tril01, jnp.log(f))
    bend = b[CHUNK - 1:CHUNK, :]
    sq = _sigmoid(q)
    eb = jnp.exp(b)
    emb = jnp.exp(-b)
    eo = jnp.exp(bend - b)
    dec = jnp.exp(bend)
    return sig, f, kk, sq, eb, emb, eo, dec


def _hgrn2_fwd(proj, lb, *, name):
    T = proj.shape[0]
    nT = T // HGRN_TB
    nC = T // CHUNK

    def body(q_ref, z_ref, v_ref, lb_ref, o_ref, st_ref, state):
        @pl.when(pl.program_id(1) == 0)
        def _():
            state[...] = jnp.zeros_like(state)

        row = lax.broadcasted_iota(jnp.int32, (CHUNK, CHUNK), 0)
        col = lax.broadcasted_iota(jnp.int32, (CHUNK, CHUNK), 1)
        causal = row >= col
        tril01 = causal.astype(BF16)

        def chunk(c, carry):
            rows = pl.ds(pl.multiple_of(c * CHUNK, CHUNK), CHUNK)
            for hh in range(HGRN_HPB):
                sl = slice(hh * HEAD_DIM, (hh + 1) * HEAD_DIM)
                q = q_ref[rows, sl]
                v = v_ref[rows, sl].astype(BF16)
                sig, f, kk, sq, eb, emb, eo, dec = _hgrn_chunk_fwd(q, z_ref[rows, sl], lb_ref[:, sl], tril01)
                qi = (q * sq * eb).astype(BF16)
                ki = (kk * emb).astype(BF16)
                ko = (kk * eo).astype(BF16)
                st = state[hh]
                att = jnp.where(causal, _dot_nt(qi, ki), 0.0)
                o_ref[rows, sl] = _dot(att, v) + _dot_nt(qi, st)
                st_ref[c, hh] = st
                state[hh] = st * dec + _dot_tn(v, ko)
            return carry

        lax.fori_loop(0, HGRN_NCH, chunk, 0)

    W = HGRN_HPB * HEAD_DIM
    nG = A_HEADS // HGRN_HPB
    hb = lambda off: pl.BlockSpec((HGRN_TB, W), lambda h, i: (i, off + h))
    return pl.pallas_call(
        body, grid=(nG, nT),
        in_specs=[hb(0), hb(nG), hb(2 * nG), pl.BlockSpec((1, W), lambda h, i: (0, h))],
        out_specs=[hb(0), pl.BlockSpec((HGRN_NCH, HGRN_HPB, HEAD_DIM, HEAD_DIM), lambda h, i: (i, h, 0, 0))],
        out_shape=[jax.ShapeDtypeStruct((T, A_WIDTH), F32), jax.ShapeDtypeStruct((nC, A_HEADS, HEAD_DIM, HEAD_DIM), F32)],
        scratch_shapes=[pltpu.VMEM((HGRN_HPB, HEAD_DIM, HEAD_DIM), F32)],
        compiler_params=_cp("parallel", "arbitrary"), name=name)(proj, proj, proj, lb)


def _hgrn2_bwd(proj, lb, st_all, do, *, name):
    T = proj.shape[0]
    nT = T // HGRN_TB

    def body(q_ref, z_ref, v_ref, lb_ref, st_ref, do_ref, dq_ref, dz_ref, dv_ref, dlb_ref, dstate):
        @pl.when(pl.program_id(1) == 0)
        def _():
            dstate[...] = jnp.zeros_like(dstate)
            dlb_ref[...] = jnp.zeros_like(dlb_ref)

        row = lax.broadcasted_iota(jnp.int32, (CHUNK, CHUNK), 0)
        col = lax.broadcasted_iota(jnp.int32, (CHUNK, CHUNK), 1)
        causal = row >= col
        tril01 = causal.astype(BF16)
        triu01 = (row <= col).astype(BF16)

        def chunk(cc, carry):
            c = HGRN_NCH - 1 - cc
            rows = pl.ds(pl.multiple_of(c * CHUNK, CHUNK), CHUNK)
            for hh in range(HGRN_HPB):
                sl = slice(hh * HEAD_DIM, (hh + 1) * HEAD_DIM)
                lbv = lb_ref[:, sl]
                q = q_ref[rows, sl]
                v = v_ref[rows, sl].astype(BF16)
                sig, f, kk, sq, eb, emb, eo, dec = _hgrn_chunk_fwd(q, z_ref[rows, sl], lbv, tril01)
                qi32 = q * sq * eb
                ki32 = kk * emb
                ko32 = kk * eo
                qi, ki, ko = qi32.astype(BF16), ki32.astype(BF16), ko32.astype(BF16)
                att = jnp.where(causal, _dot_nt(qi, ki), 0.0).astype(BF16)
                dout = do_ref[rows, sl].astype(BF16)
                st = st_ref[c, hh]
                dst = dstate[hh]
                dst16 = dst.astype(BF16)
                datt = jnp.where(causal, _dot_nt(dout, v), 0.0).astype(BF16)
                dqi = _dot(datt, ki) + _dot(dout, st)
                dki = _dot_tn(datt, qi)
                dv_ref[rows, sl] = (_dot_tn(att, dout) + _dot_nt(ko, dst16)).astype(BF16)
                dko = _dot(v, dst16)
                ddec = jnp.sum(dst * st, axis=0, keepdims=True)
                dstate[hh] = dst * dec + _dot_tn(dout, qi)
                dkk = dki * emb + dko * eo
                db = dqi * qi32 - dki * ki32 - dko * ko32
                dbend = jnp.sum(dko * ko32, axis=0, keepdims=True) + ddec * dec
                dlogf = _dot3(triu01, db) + dbend
                df = dlogf / f - dkk
                dz_ref[rows, sl] = (df * (1.0 - lbv) * sig * (1.0 - sig)).astype(BF16)
                dlb_ref[:, sl] += jnp.sum(df * (1.0 - sig), axis=0, keepdims=True)
                dq_ref[rows, sl] = (dqi * eb * (sq * (1.0 + q * (1.0 - sq)))).astype(BF16)
            return carry

        lax.fori_loop(0, HGRN_NCH, chunk, 0)

    W = HGRN_HPB * HEAD_DIM
    nG = A_HEADS // HGRN_HPB
    hb = lambda off: pl.BlockSpec((HGRN_TB, W), lambda h, i: (nT - 1 - i, off + h))
    hlb = pl.BlockSpec((1, W), lambda h, i: (0, h))
    o16 = jax.ShapeDtypeStruct((T, A_WIDTH), BF16)
    return pl.pallas_call(
        body, grid=(nG, nT),
        in_specs=[hb(0), hb(nG), hb(2 * nG), hlb,
                  pl.BlockSpec((HGRN_NCH, HGRN_HPB, HEAD_DIM, HEAD_DIM), lambda h, i: (nT - 1 - i, h, 0, 0)), hb(0)],
        out_specs=[hb(0), hb(0), hb(0), hlb],
        out_shape=[o16, o16, o16, jax.ShapeDtypeStruct((1, A_WIDTH), F32)],
        scratch_shapes=[pltpu.VMEM((HGRN_HPB, HEAD_DIM, HEAD_DIM), F32)],
        compiler_params=_cp("parallel", "arbitrary"), name=name)(proj, proj, proj, lb, st_all, do)


def _head_rms(x):
    r = lax.rsqrt(jnp.mean(x * x, axis=-1, keepdims=True) + EPS)
    return x * r, r


def _head_rms_bwd(dxhat, xhat, r):
    return r * (dxhat - xhat * jnp.mean(dxhat * xhat, axis=-1, keepdims=True))


def _a_post_fwd(o, proj, onorm, *, tt, name):
    T = o.shape[0]

    def body(o_ref, g_ref, w_ref, y_ref):
        for h in range(A_HEADS):
            sl = slice(h * HEAD_DIM, (h + 1) * HEAD_DIM)
            xhat, _ = _head_rms(o_ref[:, sl])
            g = g_ref[:, sl]
            y_ref[:, sl] = xhat * w_ref[:, sl] * (g * _sigmoid(g))

    blk = lambda c: pl.BlockSpec((tt, A_WIDTH), lambda i: (i, c))
    return pl.pallas_call(
        body, grid=(T // tt,), in_specs=[blk(0), blk(3), _full((1, A_WIDTH))], out_specs=blk(0),
        out_shape=jax.ShapeDtypeStruct((T, A_WIDTH), F32),
        compiler_params=_cp("parallel"), name=name)(o, proj, onorm)


def _a_post_bwd(o, proj, onorm, dmix, *, tt, name, dep=None):
    T = o.shape[0]

    def kernel_body(o_ref, g_ref, w_ref, dy_ref, do_ref, dg_ref, dw_ref):
        @pl.when(pl.program_id(0) == 0)
        def _():
            dw_ref[...] = jnp.zeros_like(dw_ref)

        for h in range(A_HEADS):
            sl = slice(h * HEAD_DIM, (h + 1) * HEAD_DIM)
            xhat, r = _head_rms(o_ref[:, sl])
            g = g_ref[:, sl]
            s = _sigmoid(g)
            dy = dy_ref[:, sl]
            w = w_ref[:, sl]
            dg_ref[:, sl] = (dy * xhat * w * (s * (1.0 + g * (1.0 - s)))).astype(BF16)
            dyn = dy * (g * s)
            dw_ref[:, sl] += jnp.sum(dyn * xhat, axis=0, keepdims=True)
            do_ref[:, sl] = _head_rms_bwd(dyn * w, xhat, r)

    blk = lambda c: pl.BlockSpec((tt, A_WIDTH), lambda i: (i, c))
    body, dep_specs, dep_args = _dep(kernel_body, 4, dep)
    return pl.pallas_call(
        body, grid=(T // tt,), in_specs=[blk(0), blk(3), _full((1, A_WIDTH)), blk(0)] + dep_specs,
        out_specs=[blk(0), blk(0), _full((1, A_WIDTH))],
        out_shape=[jax.ShapeDtypeStruct((T, A_WIDTH), F32), jax.ShapeDtypeStruct((T, A_WIDTH), BF16),
                   jax.ShapeDtypeStruct((1, A_WIDTH), F32)],
        compiler_params=_cp("arbitrary"), name=name)(o, proj, onorm, dmix, *dep_args)


def _mem_head_masks(n):
    lane = lax.broadcasted_iota(jnp.int32, (n, MEM_WIDTH), 1)
    return [(lane >= m * MEM_HEAD_DIM) & (lane < (m + 1) * MEM_HEAD_DIM) for m in range(MEM_HEADS)]


def _mem_head_rms(x, masks):
    x2 = x * x
    r = jnp.zeros_like(x)
    for mk in masks:
        ms = jnp.sum(jnp.where(mk, x2, 0.0), axis=-1, keepdims=True) * (1.0 / MEM_HEAD_DIM)
        r = jnp.where(mk, lax.rsqrt(ms + EPS), r)
    return x * r, r


def _mem_head_rms_bwd(dxhat, xhat, r, masks):
    t = dxhat * xhat
    m = jnp.zeros_like(t)
    for mk in masks:
        m = jnp.where(mk, jnp.sum(jnp.where(mk, t, 0.0), axis=-1, keepdims=True) * (1.0 / MEM_HEAD_DIM), m)
    return r * (dxhat - xhat * m)


MEM_SCALE = MEM_HEAD_DIM ** -0.5


def _mem_attn_fwd(proj, qcol, mkv, qn_w, kn_w, *, tt, name):
    T = proj.shape[0]

    def body(q_ref, k_ref, v_ref, qw_ref, kw_ref, o_ref):
        qmasks = _mem_head_masks(tt)
        kmasks = _mem_head_masks(MEM_TOKENS)
        qhat, _ = _mem_head_rms(q_ref[...], qmasks)
        qn = qhat * qw_ref[...]
        khat, _ = _mem_head_rms(k_ref[...], kmasks)
        kn = (khat * kw_ref[...]).astype(BF16)
        v = v_ref[...].astype(BF16)
        out = jnp.zeros((tt, MEM_WIDTH), F32)
        for m in range(MEM_HEADS):
            s = _dot_nt(jnp.where(qmasks[m], qn, 0.0), kn) * MEM_SCALE
            s = s - jnp.max(s, axis=-1, keepdims=True)
            p = jnp.exp(s)
            p = p / jnp.sum(p, axis=-1, keepdims=True)
            out = jnp.where(qmasks[m], _dot(p, v), out)
        o_ref[...] = out

    return pl.pallas_call(
        body, grid=(T // tt,),
        in_specs=[pl.BlockSpec((tt, MEM_WIDTH), lambda i: (i, qcol)), pl.BlockSpec((MEM_TOKENS, MEM_WIDTH), lambda i: (0, 0)),
                  pl.BlockSpec((MEM_TOKENS, MEM_WIDTH), lambda i: (0, 1)), _full((1, MEM_WIDTH)), _full((1, MEM_WIDTH))],
        out_specs=pl.BlockSpec((tt, MEM_WIDTH), lambda i: (i, 0)),
        out_shape=jax.ShapeDtypeStruct((T, MEM_WIDTH), F32),
        compiler_params=_cp("parallel"), name=name)(proj, mkv, mkv, qn_w, kn_w)


def _mem_attn_bwd(proj, qcol, mkv, qn_w, kn_w, dmix, *, tt, name):
    T = proj.shape[0]
    nsteps = T // tt
    ocol = (dmix.shape[1] - MEM_WIDTH) // MEM_WIDTH

    def body(q_ref, k_ref, v_ref, qw_ref, kw_ref, do_ref, dq_ref, dkv_ref, dqw_ref, dkw_ref, dk_acc, dv_acc):
        step = pl.program_id(0)

        @pl.when(step == 0)
        def _():
            dk_acc[...] = jnp.zeros_like(dk_acc)
            dv_acc[...] = jnp.zeros_like(dv_acc)
            dqw_ref[...] = jnp.zeros_like(dqw_ref)

        qmasks = _mem_head_masks(tt)
        kmasks = _mem_head_masks(MEM_TOKENS)
        qhat, qr = _mem_head_rms(q_ref[...], qmasks)
        qn = qhat * qw_ref[...]
        khat, kr = _mem_head_rms(k_ref[...], kmasks)
        kn = (khat * kw_ref[...]).astype(BF16)
        v = v_ref[...].astype(BF16)
        dout = do_ref[...]
        dqn = jnp.zeros((tt, MEM_WIDTH), F32)
        dkn = jnp.zeros((MEM_TOKENS, MEM_WIDTH), F32)
        dvv = jnp.zeros((MEM_TOKENS, MEM_WIDTH), F32)
        for m in range(MEM_HEADS):
            qm = jnp.where(qmasks[m], qn, 0.0).astype(BF16)
            s = _dot_nt(qm, kn) * MEM_SCALE
            s = s - jnp.max(s, axis=-1, keepdims=True)
            p = jnp.exp(s)
            p = p / jnp.sum(p, axis=-1, keepdims=True)
            dom = jnp.where(qmasks[m], dout, 0.0).astype(BF16)
            dp = _dot_nt(dom, v)
            ds = (p * (dp - jnp.sum(p * dp, axis=-1, keepdims=True)) * MEM_SCALE).astype(BF16)
            dqn = jnp.where(qmasks[m], _dot(ds, kn), dqn)
            dkn = jnp.where(kmasks[m], _dot_tn(ds, qm), dkn)
            dvv = jnp.where(kmasks[m], _dot_tn(p, dom), dvv)
        dqw_ref[...] += jnp.sum(dqn * qhat, axis=0, keepdims=True)
        dq_ref[...] = _mem_head_rms_bwd(dqn * qw_ref[...], qhat, qr, qmasks).astype(BF16)
        dk_acc[...] += dkn
        dv_acc[...] += dvv

        @pl.when(step == nsteps - 1)
        def _():
            dk = dk_acc[...]
            dkw_ref[...] = jnp.sum(dk * khat, axis=0, keepdims=True)
            dkv_ref[:, :MEM_WIDTH] = _mem_head_rms_bwd(dk * kw_ref[...], khat, kr, kmasks)
            dkv_ref[:, MEM_WIDTH:] = dv_acc[...]

    return pl.pallas_call(
        body, grid=(nsteps,),
        in_specs=[pl.BlockSpec((tt, MEM_WIDTH), lambda i: (i, qcol)), pl.BlockSpec((MEM_TOKENS, MEM_WIDTH), lambda i: (0, 0)),
                  pl.BlockSpec((MEM_TOKENS, MEM_WIDTH), lambda i: (0, 1)), _full((1, MEM_WIDTH)), _full((1, MEM_WIDTH)),
                  pl.BlockSpec((tt, MEM_WIDTH), lambda i: (i, ocol))],
        out_specs=[pl.BlockSpec((tt, MEM_WIDTH), lambda i: (i, 0)), _full((MEM_TOKENS, 2 * MEM_WIDTH)),
                   _full((1, MEM_WIDTH)), _full((1, MEM_WIDTH))],
        out_shape=[jax.ShapeDtypeStruct((T, MEM_WIDTH), BF16), jax.ShapeDtypeStruct((MEM_TOKENS, 2 * MEM_WIDTH), F32),
                   jax.ShapeDtypeStruct((1, MEM_WIDTH), F32), jax.ShapeDtypeStruct((1, MEM_WIDTH), F32)],
        scratch_shapes=[pltpu.VMEM((MEM_TOKENS, MEM_WIDTH), F32), pltpu.VMEM((MEM_TOKENS, MEM_WIDTH), F32)],
        compiler_params=_cp("arbitrary"), name=name)(proj, mkv, mkv, qn_w, kn_w, dmix)


HALF = HEAD_DIM // 2
ATT_SCALE = HEAD_DIM ** -0.5
NEG = -1e30


def _rope_tables(T):
    inv = ROPE_THETA ** (-jnp.arange(HALF, dtype=F32) / HALF)
    ang = jnp.arange(T, dtype=F32)[:, None] * inv[None, :]
    cos, sin = jnp.cos(ang), jnp.sin(ang)
    return jnp.concatenate([cos, cos], axis=-1), jnp.concatenate([-sin, sin], axis=-1)


def _rope(x, cosf, sinsg):
    return x * cosf + pltpu.roll(x, HALF, 1) * sinsg


def _rope_bwd(dy, cosf, sinsg):
    return dy * cosf + pltpu.roll(dy * sinsg, HALF, 1)


def _q_prep_bwd(proj, w_heads, cosf, sinsg, dqs, *, tt, name):
    T = proj.shape[0]
    W = N_GROUPS * B_WIDTH

    def body(x_ref, w_ref, c_ref, s_ref, d0, d1, d2, dx_ref, dw_ref):
        @pl.when(pl.program_id(0) == 0)
        def _():
            dw_ref[...] = jnp.zeros_like(dw_ref)

        c, s = c_ref[...], s_ref[...]
        for gi, d_ref in enumerate((d0, d1, d2)):
            for h in range(B_HEADS):
                sl = slice((gi * B_HEADS + h) * HEAD_DIM, (gi * B_HEADS + h + 1) * HEAD_DIM)
                xhat, r = _head_rms(x_ref[:, sl])
                dyn = _rope_bwd(d_ref[:, h * HEAD_DIM:(h + 1) * HEAD_DIM], c, s)
                dw_ref[:, sl] += jnp.sum(dyn * xhat, axis=0, keepdims=True)
                dx_ref[:, sl] = _head_rms_bwd(dyn * w_ref[:, sl], xhat, r).astype(BF16)

    tbl = pl.BlockSpec((tt, HEAD_DIM), lambda i: (i, 0))
    dyb = pl.BlockSpec((tt, B_WIDTH), lambda i: (i, 0))
    return pl.pallas_call(
        body, grid=(T // tt,),
        in_specs=[pl.BlockSpec((tt, W), lambda i: (i, 0)), _full((1, W)), tbl, tbl, dyb, dyb, dyb],
        out_specs=[pl.BlockSpec((tt, W), lambda i: (i, 0)), _full((1, W))],
        out_shape=[jax.ShapeDtypeStruct((T, W), BF16), jax.ShapeDtypeStruct((1, W), F32)],
        compiler_params=_cp("arbitrary"), name=name)(proj, w_heads, cosf, sinsg, *dqs)


def _kv_prep_bwd(kv, w_heads, cosf, sinsg, dks, dvs, *, tt, name):
    T = kv.shape[0]

    def body(x_ref, w_ref, c_ref, s_ref, k0, k1, k2, v0, v1, v2, dx_ref, dw_ref):
        @pl.when(pl.program_id(0) == 0)
        def _():
            dw_ref[...] = jnp.zeros_like(dw_ref)

        c, s = c_ref[...], s_ref[...]
        for h in range(B_HEADS):
            sl = slice(h * HEAD_DIM, (h + 1) * HEAD_DIM)
            vs = slice(B_WIDTH + h * HEAD_DIM, B_WIDTH + (h + 1) * HEAD_DIM)
            xhat, r = _head_rms(x_ref[:, sl])
            dyn = _rope_bwd(k0[:, sl] + k1[:, sl] + k2[:, sl], c, s)
            dw_ref[:, sl] += jnp.sum(dyn * xhat, axis=0, keepdims=True)
            dx_ref[:, sl] = _head_rms_bwd(dyn * w_ref[:, sl], xhat, r).astype(BF16)
            dx_ref[:, vs] = (v0[:, sl] + v1[:, sl] + v2[:, sl]).astype(BF16)

    tbl = pl.BlockSpec((tt, HEAD_DIM), lambda i: (i, 0))
    dyb = pl.BlockSpec((tt, B_WIDTH), lambda i: (i, 0))
    return pl.pallas_call(
        body, grid=(T // tt,),
        in_specs=[dyb, _full((1, B_WIDTH)), tbl, tbl] + [dyb] * 6,
        out_specs=[pl.BlockSpec((tt, 2 * B_WIDTH), lambda i: (i, 0)), _full((1, B_WIDTH))],
        out_shape=[jax.ShapeDtypeStruct((T, 2 * B_WIDTH), BF16), jax.ShapeDtypeStruct((1, B_WIDTH), F32)],
        compiler_params=_cp("arbitrary"), name=name)(kv, w_heads, cosf, sinsg, *dks, *dvs)


def _band_masks(n_is_first=None):
    row = lax.broadcasted_iota(jnp.int32, (SPAN, SPAN), 0)
    col = lax.broadcasted_iota(jnp.int32, (SPAN, SPAN), 1)
    return row >= col, col >= row


def _dil_views(T, d):
    L = T // d
    return L, L // SPAN


def _dil_fwd(qr, kr, kv, gi, d, *, name):
    T = qr.shape[0]
    L, nb = _dil_views(T, d)

    def body(q_ref, kc_ref, kp_ref, vc_ref, vp_ref, o_ref, lse_ref):
        cur_ok, prev_band = _band_masks()
        prev_ok = prev_band & (pl.program_id(1) > 0)
        for h in range(B_HEADS):
            sl = slice(h * HEAD_DIM, (h + 1) * HEAD_DIM)
            q = q_ref[:, sl]
            sc = jnp.where(cur_ok, _dot_nt(q, kc_ref[:, sl]) * ATT_SCALE, NEG)
            sp = jnp.where(prev_ok, _dot_nt(q, kp_ref[:, sl]) * ATT_SCALE, NEG)
            m = jnp.maximum(jnp.max(sc, axis=-1, keepdims=True), jnp.max(sp, axis=-1, keepdims=True))
            pc = jnp.exp(sc - m)
            pp = jnp.exp(sp - m)
            l = jnp.sum(pc, axis=-1, keepdims=True) + jnp.sum(pp, axis=-1, keepdims=True)
            o_ref[:, sl] = (_dot(pc, vc_ref[:, sl]) + _dot(pp, vp_ref[:, sl])) / l
            lse_ref[:, sl] = jnp.broadcast_to(m + jnp.log(l), (SPAN, HEAD_DIM))

    blk = lambda f: pl.BlockSpec((SPAN, B_WIDTH), f)
    cur = lambda r, n: (n, r)
    prev = lambda r, n: (jnp.maximum(n - 1, 0), r)
    ov = jax.ShapeDtypeStruct((L, d * B_WIDTH), F32)
    o, lse = pl.pallas_call(
        body, grid=(d, nb),
        in_specs=[blk(lambda r, n: (n, r * N_GROUPS + gi)), blk(cur), blk(prev),
                  blk(lambda r, n: (n, 2 * r + 1)), blk(lambda r, n: (jnp.maximum(n - 1, 0), 2 * r + 1))],
        out_specs=[blk(cur), blk(cur)], out_shape=[ov, ov],
        compiler_params=_cp("parallel", "arbitrary"), name=name,
    )(qr.reshape(L, d * N_GROUPS * B_WIDTH), kr.reshape(L, d * B_WIDTH), kr.reshape(L, d * B_WIDTH),
      kv.reshape(L, d * 2 * B_WIDTH), kv.reshape(L, d * 2 * B_WIDTH))
    return o.reshape(T, B_WIDTH), lse.reshape(T, B_WIDTH)


def _dil_combine_fwd(os_, lses, *, tt, name):
    T = os_[0].shape[0]

    def body(o0, o1, o2, l0, l1, l2, y_ref, lse_ref):
        a, b, c = l0[...], l1[...], l2[...]
        m = jnp.maximum(jnp.maximum(a, b), c)
        wa, wb, wc = jnp.exp(a - m), jnp.exp(b - m), jnp.exp(c - m)
        den = wa + wb + wc
        y_ref[...] = (wa * o0[...] + wb * o1[...] + wc * o2[...]) / den
        lse_ref[...] = m + jnp.log(den)

    blk = pl.BlockSpec((tt, B_WIDTH), lambda i: (i, 0))
    sh = jax.ShapeDtypeStruct((T, B_WIDTH), F32)
    return pl.pallas_call(
        body, grid=(T // tt,), in_specs=[blk] * 6, out_specs=[blk, blk], out_shape=[sh, sh],
        compiler_params=_cp("parallel"), name=name)(*os_, *lses)


DILS_UNROLL = 4


def _dils_specs(gi, d, nblk):
    blk = lambda f: pl.BlockSpec((SPAN * d, HEAD_DIM), f)
    return {
        "q": blk(lambda h, n: (n, gi * B_HEADS + h)), "q_next": blk(lambda h, n: (jnp.minimum(n + 1, nblk - 1), gi * B_HEADS + h)),
        "cur": blk(lambda h, n: (n, h)), "prev": blk(lambda h, n: (jnp.maximum(n - 1, 0), h)),
        "next": blk(lambda h, n: (jnp.minimum(n + 1, nblk - 1), h)),
        "v": blk(lambda h, n: (n, B_HEADS + h)), "v_prev": blk(lambda h, n: (jnp.maximum(n - 1, 0), B_HEADS + h)),
    }


def _dils_fwd(qr, kr, kv, gi, d, *, name):
    T = qr.shape[0]
    nblk = T // (SPAN * d)
    sp = _dils_specs(gi, d, nblk)

    def body(q_ref, kc_ref, vc_ref, o_ref, lse_ref, k_before, v_before):
        @pl.when(pl.program_id(1) == 0)
        def _():
            k_before[...] = jnp.zeros_like(k_before)
            v_before[...] = jnp.zeros_like(v_before)

        cur_ok, prev_band = _band_masks()
        prev_ok = prev_band & (pl.program_id(1) > 0)

        def residue(r, carry):
            rows = pl.ds(r, SPAN, stride=d)
            q, kc, vc = q_ref[rows, :], kc_ref[rows, :].astype(BF16), vc_ref[rows, :].astype(BF16)
            sc = jnp.where(cur_ok, _dot_nt(q, kc) * ATT_SCALE, NEG)
            sp_ = jnp.where(prev_ok, _dot_nt(q, k_before[r]) * ATT_SCALE, NEG)
            m = jnp.maximum(jnp.max(sc, axis=-1, keepdims=True), jnp.max(sp_, axis=-1, keepdims=True))
            pc = jnp.exp(sc - m)
            pp = jnp.exp(sp_ - m)
            l = jnp.sum(pc, axis=-1, keepdims=True) + jnp.sum(pp, axis=-1, keepdims=True)
            o_ref[rows, :] = (_dot(pc, vc) + _dot(pp, v_before[r])) / l
            lse_ref[rows, :] = jnp.broadcast_to(m + jnp.log(l), (SPAN, HEAD_DIM))
            k_before[r] = kc
            v_before[r] = vc
            return carry

        lax.fori_loop(0, d, residue, 0, unroll=DILS_UNROLL)

    sh = jax.ShapeDtypeStruct((T, B_WIDTH), F32)
    return pl.pallas_call(
        body, grid=(B_HEADS, nblk), in_specs=[sp["q"], sp["cur"], sp["v"]],
        out_specs=[sp["cur"], sp["cur"]], out_shape=[sh, sh],
        scratch_shapes=[pltpu.VMEM((d, SPAN, HEAD_DIM), BF16), pltpu.VMEM((d, SPAN, HEAD_DIM), BF16)],
        compiler_params=_cp("parallel", "arbitrary"), name=name)(qr, kr, kv)


DIL_BWD_GROUP = {1: 4, 4: 1, 16: 1}


def _dil_bwd(qr, kr, kv, dmix, lse, dd, gi, d, *, name, dep=None):
    T = qr.shape[0]
    G = DIL_BWD_GROUP[d]
    band = SPAN * d
    tb = G * band
    nblk = T // tb
    n_units = T // SPAN

    keep = G == 1

    def kernel_body(q_ref, dy_ref, lse_ref, dd_ref, kc_ref, vc_ref, *rest):
        if keep:
            dq_ref, dk_ref, dv_ref, dk_acc, dv_acc, k_before, v_before = rest
        else:
            kp_ref, vp_ref, dq_ref, dk_ref, dv_ref, dk_acc, dv_acc = rest
        n = pl.program_id(1)

        @pl.when(n == 0)
        def _():
            dk_acc[...] = jnp.zeros_like(dk_acc)
            dv_acc[...] = jnp.zeros_like(dv_acc)
            if keep:
                k_before[...] = jnp.zeros_like(k_before)
                v_before[...] = jnp.zeros_like(v_before)

        cur_ok, prev_band = _band_masks()
        for j in range(G):
            def residue(r, carry, j=j):
                off = j * band + r
                rows = pl.ds(off, SPAN, stride=d)
                q, dy = q_ref[rows, :], dy_ref[rows, :]
                lse_h = jnp.max(lse_ref[rows, :], axis=-1, keepdims=True)
                dd_h = jnp.max(dd_ref[rows, :], axis=-1, keepdims=True)
                kc, vc = kc_ref[rows, :].astype(BF16), vc_ref[rows, :].astype(BF16)
                if j > 0:
                    before = pl.ds(off - band, SPAN, stride=d)
                    kp, vp = kc_ref[before, :], vc_ref[before, :]
                    prev_ok = prev_band
                elif keep:
                    kp, vp = k_before[r], v_before[r]
                    k_before[r] = kc
                    v_before[r] = vc
                    prev_ok = prev_band & (n > 0)
                else:
                    before = pl.ds((G - 1) * band + r, SPAN, stride=d)
                    kp, vp = kp_ref[before, :], vp_ref[before, :]
                    prev_ok = prev_band & (n > 0)
                pc = jnp.exp(jnp.where(cur_ok, _dot_nt(q, kc) * ATT_SCALE, NEG) - lse_h)
                pp = jnp.exp(jnp.where(prev_ok, _dot_nt(q, kp) * ATT_SCALE, NEG) - lse_h)
                dsc = pc * (_dot_nt(dy, vc) - dd_h) * ATT_SCALE
                dsp = pp * (_dot_nt(dy, vp) - dd_h) * ATT_SCALE
                dq_ref[rows, :] = _dot(dsc, kc) + _dot(dsp, kp)
                u = (n * G + j) * d + r
                here = pl.ds(pl.multiple_of(u * SPAN, SPAN), SPAN)
                dk_acc[here, :] += _dot_tn(dsc, q)
                dv_acc[here, :] += _dot_tn(pc, dy)
                there = pl.ds(pl.multiple_of(jnp.maximum(u - d, 0) * SPAN, SPAN), SPAN)
                dk_acc[there, :] += _dot_tn(dsp, q)
                dv_acc[there, :] += _dot_tn(pp, dy)
                return carry

            lax.fori_loop(0, d, residue, 0, unroll=min(d, DILS_UNROLL))

        @pl.when(n == nblk - 1)
        def _():
            def place(u, carry):
                rows = pl.ds((u // d) * band + u % d, SPAN, stride=d)
                src = pl.ds(pl.multiple_of(u * SPAN, SPAN), SPAN)
                dk_ref[rows, :] = dk_acc[src, :]
                dv_ref[rows, :] = dv_acc[src, :]
                return carry

            lax.fori_loop(0, n_units, place, 0)

    blk = lambda f: pl.BlockSpec((tb, HEAD_DIM), f)
    cur = lambda h, n: (n, h)
    prev = lambda h, n: (jnp.maximum(n - 1, 0), h)
    whole = pl.BlockSpec((T, HEAD_DIM), lambda h, n: (0, h))
    sh = jax.ShapeDtypeStruct((T, B_WIDTH), F32)
    v_cur = blk(lambda h, n: (n, B_HEADS + h))
    if keep:
        kv_specs, kv_args = [blk(cur), v_cur], [kr, kv]
        kept = [pltpu.VMEM((d, SPAN, HEAD_DIM), BF16), pltpu.VMEM((d, SPAN, HEAD_DIM), BF16)]
    else:
        kv_specs = [blk(cur), v_cur, blk(prev), blk(lambda h, n: (jnp.maximum(n - 1, 0), B_HEADS + h))]
        kv_args, kept = [kr, kv, kr, kv], []
    body, dep_specs, dep_args = _dep(kernel_body, 4 + len(kv_args), dep)
    return pl.pallas_call(
        body, grid=(B_HEADS, nblk),
        in_specs=[blk(lambda h, n: (n, gi * B_HEADS + h)), blk(cur), blk(cur), blk(cur)] + kv_specs + dep_specs,
        out_specs=[blk(cur), whole, whole], out_shape=[sh, sh, sh],
        scratch_shapes=[pltpu.VMEM((T, HEAD_DIM), F32), pltpu.VMEM((T, HEAD_DIM), F32)] + kept,
        compiler_params=_cp("parallel", "arbitrary"), name=name)(qr, dmix, lse, dd, *kv_args, *dep_args)


A_MQ_COL = 4 * A_WIDTH // MEM_WIDTH
B_MQ_COL = N_GROUPS * B_WIDTH // MEM_WIDTH


def _row(v):
    return v.reshape(1, -1).astype(F32)


def _local_step(x, mem, tgt, get_w, P, put_g, first_dep=None, forward_point=lambda i, value: value):
    T = x.shape[0]
    cosf, sinsg = _rope_tables(T)
    lb_soft = jax.nn.softmax(P["a_lb_logits"].astype(F32), axis=0)
    lb = lb_soft[0:1]
    qw_heads = jnp.repeat(P["b_qnorm"][0], B_HEADS, axis=0).reshape(1, -1)
    kw_heads = jnp.tile(_row(P["b_knorm"]), (1, B_HEADS))
    mqw = [jnp.tile(_row(P["mem_qnorm"][l]), (1, MEM_HEADS)) for l in range(2)]
    mkw = [jnp.tile(_row(P["mem_knorm"][l]), (1, MEM_HEADS)) for l in range(2)]
    nmix = [_row(P["norm_mix"][l]) for l in range(2)]
    nffn = [_row(P["norm_ffn"][l]) for l in range(2)]
    mnorm = [_row(P["mem_norm"][l]) for l in range(2)]
    kvn = _row(P["kv_norm"])
    onorm = _row(P["a_onorm"])
    W = {}

    def w_of(name, after=None):
        if name not in W:
            W[name] = get_w(name, after)
        return W[name]

    proj_a, xn0 = _rms_matmul(x, nmix[0], w_of("a_w_in"), tt=512, tn=1664, wt=True, name="proj_a", dep=first_dep)
    mkv0, mn0 = _rms_matmul(mem, mnorm[0], w_of("w_mem_kv0"), tt=MEM_TOKENS, tn=2 * MEM_WIDTH, wt=False, name="mem_kv0")
    o_raw, st = _hgrn2_fwd(proj_a, lb, name="hgrn2_fwd")
    o_raw = forward_point(0, o_raw)
    mm0 = _a_post_fwd(o_raw, proj_a, onorm, tt=512, name="a_post_fwd")
    mo0 = _mem_attn_fwd(proj_a, A_MQ_COL, mkv0, mqw[0], mkw[0], tt=512, name="mem_attn_fwd0")
    hm0 = _mm_res(x, mm0, mo0, w_of("w_out0", mo0), tt=512, name="out_proj0")
    hm0 = forward_point(1, hm0)
    gu0, hn0 = _rms_matmul(hm0, nffn[0], w_of("w_gate_up0", hm0), tt=512, tn=1408, wt=True, out_dtype=BF16, name="gate_up0")
    h1 = _swiglu_down(hm0, gu0, w_of("w_down0", gu0), tt=512, name="down0")
    h1 = forward_point(2, h1)
    kv, hkn, kr = _rms_matmul(h1, kvn, w_of("w_kv", h1), tt=512, tn=768, wt=True, name="kv_proj",
                              rotate=(kw_heads, cosf, sinsg))

    proj_b, xn1, qr = _rms_matmul(h1, nmix[1], w_of("b_w_in", kr), tt=512, tn=1280, wt=True, name="proj_b",
                                  rotate=(qw_heads, cosf, sinsg))
    proj_b = forward_point(3, proj_b)
    mkv1, mn1 = _rms_matmul(mem, mnorm[1], w_of("w_mem_kv1", kr), tt=MEM_TOKENS, tn=2 * MEM_WIDTH, wt=False, name="mem_kv1")
    outs = [(_dil_fwd if d == 1 else _dils_fwd)(qr, kr, kv, gi, d, name=f"dil_fwd{gi}") for gi, d in enumerate(DILATIONS)]
    mm1, lse_tot = _dil_combine_fwd([o for o, _ in outs], [s for _, s in outs], tt=512, name="dil_combine")
    mo1 = _mem_attn_fwd(proj_b, B_MQ_COL, mkv1, mqw[1], mkw[1], tt=512, name="mem_attn_fwd1")
    hm1 = _mm_res(h1, mm1, mo1, w_of("w_out1", mo1), tt=512, name="out_proj1")
    gu1, hn1 = _rms_matmul(hm1, nffn[1], w_of("w_gate_up1", hm1), tt=512, tn=1408, wt=True, out_dtype=BF16, name="gate_up1")
    dy, sq = _swiglu_down_loss(hm1, gu1, w_of("w_down1", gu1), tgt, tt=512, name="down1_loss")

    gP = {}
    zeros_mem = jnp.zeros((MEM_TOKENS, D_MODEL), F32)

    def ffn_bwd(l, dh, hm, gu, hn):
        dgu, g_wd = _swiglu_bwd(dh, gu, w_of(f"w_down{l}"), tt=256, name=f"swiglu_bwd{l}")
        g_wgu = _mm_tn(dgu, hn, tt=512, tka=1408, name=f"g_w_gate_up{l}")
        sent = put_g({f"w_down{l}": g_wd, f"w_gate_up{l}": g_wgu})
        dhm, g_nf = _rms_bwd_dx(hm, nffn[l], w_of(f"w_gate_up{l}"), dgu, dh, tt=512, wt=True, name=f"gate_up_bwd{l}", dep=sent)
        return dhm, g_nf

    def mix_bwd(l, dhm, mix_main, mix_mem, proj, qcol, mkv, mn):
        dmix, g_wout, *head_dots = _out_proj_bwd(dhm, mix_main, mix_mem, w_of(f"w_out{l}"), tt=512, name=f"out_proj_bwd{l}",
                                                 head_dots=l == 1)
        dmq, dmkv, dqw, dkw = _mem_attn_bwd(proj, qcol, mkv, mqw[l], mkw[l], dmix, tt=512, name=f"mem_attn_bwd{l}")
        g_wmkv = _mm_tn(mn, dmkv, tt=MEM_TOKENS, tka=512, name=f"g_w_mem_kv{l}")
        sent = put_g({f"w_out{l}": g_wout, f"w_mem_kv{l}": g_wmkv})
        _, g_mn = _rms_bwd_dx(mem, mnorm[l], w_of(f"w_mem_kv{l}"), dmkv, zeros_mem, tt=MEM_TOKENS, wt=False, name=f"mem_kv_bwd{l}")
        fold = lambda v: v.reshape(MEM_HEADS, MEM_HEAD_DIM).sum(axis=0)
        return dmix, dmq, g_mn, fold(dqw), fold(dkw), sent, head_dots

    dhm1, g_nf1 = ffn_bwd(1, dy, hm1, gu1, hn1)
    dmix1, dmq1, g_mn1, g_mq1, g_mk1, sent, (dd,) = mix_bwd(1, dhm1, mm1, mo1, proj_b, B_MQ_COL, mkv1, mn1)
    dqs, dks, dvs = [], [], []
    for gi, d in enumerate(DILATIONS):
        dq_g, dk_g, dv_g = _dil_bwd(qr, kr, kv, dmix1, lse_tot, dd, gi, d, name=f"dil_bwd{gi}", dep=sent if gi == 0 else None)
        dqs.append(dq_g)
        dks.append(dk_g)
        dvs.append(dv_g)
    dq_raw, dqw = _q_prep_bwd(proj_b, qw_heads, cosf, sinsg, dqs, tt=512, name="q_prep_bwd")
    dkv, dkw = _kv_prep_bwd(kv, kw_heads, cosf, sinsg, dks, dvs, tt=512, name="kv_prep_bwd")
    dproj_b = [dq_raw, dmq1]
    g_wb = _mm_tn_pieces(dproj_b, xn1, tt=512, name="g_b_w_in")
    g_wkv = _mm_tn(dkv, hkn, tt=512, tka=768, name="g_w_kv")
    sent = put_g({"b_w_in": g_wb, "w_kv": g_wkv})
    dh1, g_nm1 = _rms_bwd_dx(h1, nmix[1], w_of("b_w_in"), dproj_b, dhm1, tt=512, wt=True, name="proj_b_bwd", dep=sent)
    dh1, g_kvn = _rms_bwd_dx(h1, kvn, w_of("w_kv"), dkv, dh1, tt=512, wt=True, name="kv_proj_bwd")

    dhm0, g_nf0 = ffn_bwd(0, dh1, hm0, gu0, hn0)
    dmix0, dmq0, g_mn0, g_mq0, g_mk0, sent, _ = mix_bwd(0, dhm0, mm0, mo0, proj_a, A_MQ_COL, mkv0, mn0)
    do_raw, dg, g_onorm = _a_post_bwd(o_raw, proj_a, onorm, dmix0, tt=512, name="a_post_bwd", dep=sent)
    dq, dz, dv, dlb = _hgrn2_bwd(proj_a, lb, st, do_raw, name="hgrn2_bwd")
    dproj_a = [dq, dz, dv, dg, dmq0]
    sent = put_g({"a_w_in": _mm_tn_pieces(dproj_a, xn0, tt=512, name="g_a_w_in")})
    gx, g_nm0 = _rms_bwd_dx(x, nmix[0], w_of("a_w_in"), dproj_a, dhm0, tt=512, wt=True, name="proj_a_bwd", dep=sent)

    dl0 = lb_soft[0:1] * lb_soft[1:2] * dlb
    gP["a_lb_logits"] = jnp.concatenate([dl0, -dl0], axis=0)
    gP["a_onorm"] = g_onorm
    gP["norm_mix"] = jnp.concatenate([g_nm0, g_nm1], axis=0)
    gP["norm_ffn"] = jnp.concatenate([g_nf0, g_nf1], axis=0)
    gP["b_qnorm"] = dqw.reshape(N_GROUPS, B_HEADS, HEAD_DIM).sum(axis=1)[None]
    gP["kv_norm"] = g_kvn.reshape(-1)
    gP["b_knorm"] = dkw.reshape(B_HEADS, HEAD_DIM).sum(axis=0)
    gP["mem_norm"] = jnp.concatenate([g_mn0, g_mn1], axis=0)
    gP["mem_qnorm"] = jnp.stack([g_mq0, g_mq1])
    gP["mem_knorm"] = jnp.stack([g_mk0, g_mk1])
    return sq, gx, gP


MESH_ID = pl.DeviceIdType.MESH
HBM_SPEC = pl.BlockSpec(memory_space=pltpu.HBM)


def _position():
    return lax.axis_index("x"), lax.axis_index("y"), lax.axis_index("c")


def _all_gather(blocks, *, name):
    n = len(blocks)

    def body(*refs):
        x_refs, out_refs = refs[:n], refs[n:2 * n]
        send_sems, recv_sems, local_sems = refs[2 * n:]
        x, y, c = _position()
        me, sibling = (x, y, c), (x, y, 1 - c)
        chips = [(1 - x, y), (x, 1 - y), (1 - x, 1 - y)]

        def slot(a, px, py, pc):
            return out_refs[a].at[4 * px + 2 * py + pc]

        def copy(a, k, blk, to, src=None):
            return pltpu.make_async_remote_copy(
                src_ref=slot(a, *blk) if src is None else src, dst_ref=slot(a, *blk),
                send_sem=send_sems.at[7 * a + k], recv_sem=recv_sems.at[7 * a + k], device_id=to, device_id_type=MESH_ID)

        mine = [pltpu.make_async_copy(x_refs[a], slot(a, *me), local_sems.at[a]) for a in range(n)]
        for cp in mine:
            cp.start()
        first = []
        for a in range(n):
            first.append(copy(a, 0, me, sibling, src=x_refs[a]))
            first += [copy(a, 1 + j, me, (*chip, c), src=x_refs[a]) for j, chip in enumerate(chips)]
        for cp in first:
            cp.start()
        passed = []
        for j, chip in enumerate(chips):
            for a in range(n):
                copy(a, 1 + j, (*chip, c), me).wait_recv()
                cp = copy(a, 4 + j, (*chip, c), sibling)
                cp.start()
                passed.append(cp)
        for a in range(n):
            copy(a, 0, sibling, me).wait_recv()
            for j, chip in enumerate(chips):
                copy(a, 4 + j, (*chip, 1 - c), me).wait_recv()
        for cp in first + passed:
            cp.wait_send()
        for cp in mine:
            cp.wait()

    return pl.pallas_call(
        body, out_shape=[jax.ShapeDtypeStruct((N_DEV,) + b.shape, b.dtype) for b in blocks],
        in_specs=[HBM_SPEC] * n, out_specs=[HBM_SPEC] * n,
        scratch_shapes=[pltpu.SemaphoreType.DMA((7 * n,)), pltpu.SemaphoreType.DMA((7 * n,)), pltpu.SemaphoreType.DMA((n,))],
        name=name)(*blocks)


def _all_gather_direct(block, after, *, name):
    def body(x_ref, after_ref, out_ref, send_sems, recv_sems, local_sem):
        x, y, c = _position()
        me = 4 * x + 2 * y + c
        mine = pltpu.make_async_copy(x_ref, out_ref.at[me], local_sem)
        mine.start()
        copies = []
        for k in ALL_PEERS:
            cp = pltpu.make_async_remote_copy(
                src_ref=x_ref, dst_ref=out_ref.at[me], send_sem=send_sems.at[k - 1], recv_sem=recv_sems.at[k - 1],
                device_id=_peer(k, x, y, c), device_id_type=MESH_ID)
            cp.start()
            copies.append(cp)
        for cp in copies:
            cp.wait()
        mine.wait()

    return pl.pallas_call(
        body, out_shape=jax.ShapeDtypeStruct((N_DEV,) + block.shape, block.dtype),
        in_specs=[HBM_SPEC, pl.BlockSpec(memory_space=pl.ANY)], out_specs=HBM_SPEC,
        scratch_shapes=[pltpu.SemaphoreType.DMA((7,)), pltpu.SemaphoreType.DMA((7,)), pltpu.SemaphoreType.DMA],
        name=name)(block, after)


SEM_SPEC = pl.BlockSpec(memory_space=pltpu.SEMAPHORE)
ANY_SPEC = pl.BlockSpec(memory_space=pl.ANY)
DATAFLOW = pltpu.SideEffectType.DATAFLOW_SIDE_EFFECTING


def _peer(k, x, y, c):
    return (1 - x if (k >> 2) & 1 else x, 1 - y if (k >> 1) & 1 else y, 1 - c if k & 1 else c)


def _own_slot_filled(own_block):
    x, y, c = _position()
    zone = lax.empty((N_DEV,) + own_block.shape, own_block.dtype)
    return lax.dynamic_update_slice_in_dim(zone, own_block[None], 4 * x + 2 * y + c, axis=0)


ALL_PEERS = tuple(range(1, N_DEV))
SIBLING_AND_SAME_CORE = (1, 2, 4, 6)
SAME_CORE = (2, 4, 6)


def _split_start(srcs, scatter, after, *, name, relations=ALL_PEERS, carried=None):
    n = len(srcs)
    extra = ([] if after is None else [after]) + ([] if carried is None else [carried])
    n_carried = 0 if carried is None else 1
    x, y, c = _position()
    me = 4 * x + 2 * y + c
    lands = [_own_slot_filled(lax.dynamic_index_in_dim(s, me, 0, keepdims=False) if scatter else s) for s in srcs]

    def body(*refs):
        src_refs, land_refs = refs[:n], refs[n:2 * n]
        send_sems, recv_sems = refs[2 * n + len(extra)], refs[2 * n + len(extra) + 1]
        token = refs[2 * n + len(extra) + 2 + 2 * n]
        bx, by, bc = _position()
        bme = 4 * bx + 2 * by + bc
        for a in range(n):
            for k in relations:
                tx, ty, tc = _peer(k, bx, by, bc)
                src = src_refs[a].at[4 * tx + 2 * ty + tc] if scatter else src_refs[a]
                pltpu.make_async_remote_copy(
                    src_ref=src, dst_ref=land_refs[a].at[bme],
                    send_sem=send_sems.at[7 * a + k - 1], recv_sem=recv_sems.at[7 * a + k - 1],
                    device_id=(tx, ty, tc), device_id_type=MESH_ID).start()
        token[...] = jnp.zeros_like(token)

    hbm = lambda a: pltpu.HBM(a.shape, a.dtype)
    outs = pl.pallas_call(
        body, name=name,
        out_shape=(pltpu.SemaphoreType.DMA((7 * n,)), pltpu.SemaphoreType.DMA((7 * n,)),
                   *[hbm(s) for s in srcs], *[hbm(l) for l in lands], jax.ShapeDtypeStruct((8, 128), F32),
                   *([hbm(carried)] if n_carried else [])),
        in_specs=[HBM_SPEC] * (2 * n) + [ANY_SPEC] * len(extra),
        out_specs=(SEM_SPEC, SEM_SPEC, *[HBM_SPEC] * (2 * n), pl.BlockSpec(memory_space=pltpu.VMEM), *([ANY_SPEC] * n_carried)),
        input_output_aliases={**{i: 2 + i for i in range(2 * n)},
                              **({2 * n + len(extra) - 1: 2 * n + 3} if n_carried else {})},
        compiler_params=pltpu.CompilerParams(has_side_effects=DATAFLOW),
    )(*[pltpu.with_memory_space_constraint(s, pltpu.HBM) for s in srcs],
      *[pltpu.with_memory_space_constraint(l, pltpu.HBM) for l in lands], *extra)
    return {"n": n, "relations": relations, "send": outs[0], "recv": outs[1], "srcs": list(outs[2:2 + n]),
            "lands": list(outs[2 + n:2 + 2 * n]), "token": outs[2 * n + 2], "carried": outs[-1] if n_carried else None}


def _forward_start(lands, carried, *, name):
    n = len(lands)

    def body(*refs):
        land_refs = refs[:n]
        send_sems, recv_sems = refs[n + 1], refs[n + 2]
        bx, by, bc = _position()
        for a in range(n):
            for k in SAME_CORE:
                tx, ty, tc = _peer(k, bx, by, bc)
                block = land_refs[a].at[4 * tx + 2 * ty + tc]
                pltpu.make_async_remote_copy(
                    src_ref=block, dst_ref=block,
                    send_sem=send_sems.at[7 * a + k - 1], recv_sem=recv_sems.at[7 * a + k - 1],
                    device_id=(bx, by, 1 - bc), device_id_type=MESH_ID).start()

    hbm = lambda a: pltpu.HBM(a.shape, a.dtype)
    outs = pl.pallas_call(
        body, name=name,
        out_shape=(pltpu.SemaphoreType.DMA((7 * n,)), pltpu.SemaphoreType.DMA((7 * n,)),
                   *[hbm(l) for l in lands], hbm(carried)),
        in_specs=[HBM_SPEC] * n + [ANY_SPEC],
        out_specs=(SEM_SPEC, SEM_SPEC, *[HBM_SPEC] * n, ANY_SPEC),
        input_output_aliases={i: 2 + i for i in range(n + 1)},
        compiler_params=pltpu.CompilerParams(has_side_effects=DATAFLOW),
    )(*lands, carried)
    handle = {"n": n, "relations": SAME_CORE, "send": outs[0], "recv": outs[1], "srcs": [], "lands": list(outs[2:2 + n])}
    return handle, outs[-1]


def _split_wait(handle, after, *, name):
    n, ns = handle["n"], len(handle["srcs"])

    def body(*refs):
        land_refs = refs[ns:ns + n]
        send_sems, recv_sems = refs[ns + n], refs[ns + n + 1]
        bx, by, bc = _position()
        for a in range(n):
            for k in handle["relations"]:
                block = land_refs[a].at[0]
                cp = pltpu.make_async_remote_copy(
                    src_ref=block, dst_ref=block,
                    send_sem=send_sems.at[7 * a + k - 1], recv_sem=recv_sems.at[7 * a + k - 1],
                    device_id=_peer(k, bx, by, bc), device_id_type=MESH_ID)
                cp.wait_send()
                cp.wait_recv()

    hbm = lambda a: pltpu.HBM(a.shape, a.dtype)
    outs = pl.pallas_call(
        body, name=name,
        out_shape=(*[hbm(s) for s in handle["srcs"]], *[hbm(l) for l in handle["lands"]]),
        in_specs=[HBM_SPEC] * (ns + n) + [SEM_SPEC, SEM_SPEC, ANY_SPEC],
        out_specs=tuple([HBM_SPEC] * (ns + n)),
        input_output_aliases={i: i for i in range(ns + n)},
        compiler_params=pltpu.CompilerParams(has_side_effects=DATAFLOW),
    )(*handle["srcs"], *handle["lands"], handle["send"], handle["recv"], after)
    return list(outs[ns:])


def _sum_sources(parts, *, tr, name):
    n, R, C = parts.shape

    def body(p_ref, o_ref):
        acc = p_ref[0].astype(F32)
        for s in range(1, n):
            acc = acc + p_ref[s].astype(F32)
        o_ref[...] = acc

    return pl.pallas_call(
        body, grid=(R // tr,), in_specs=[pl.BlockSpec((n, tr, C), lambda i: (0, i, 0))],
        out_specs=pl.BlockSpec((tr, C), lambda i: (i, 0)),
        out_shape=jax.ShapeDtypeStruct((R, C), F32), compiler_params=_cp("parallel"), name=name)(parts)


def _adamw_math(g, w, m, v):
    c1 = 1.0 - ADAM_B1 ** ADAM_STEP
    c2 = 1.0 - ADAM_B2 ** ADAM_STEP
    nm = ADAM_B1 * m + (1.0 - ADAM_B1) * g
    nv = ADAM_B2 * v + (1.0 - ADAM_B2) * (g * g)
    return -ADAM_LR * ((nm / c1) / (jnp.sqrt(nv / c2) + ADAM_EPS) + ADAM_WD * w), nm, nv


def _reduce_adamw(received, w, m, v, *, tr, name):
    L, R, C = w.shape

    def body(*refs):
        p_refs = refs[:L]
        w_ref, m_ref, v_ref, g_ref, d_ref, nm_ref, nv_ref = refs[L:]
        for l in range(L):
            @pl.when(pl.program_id(0) == l)
            def _(p_ref=p_refs[l]):
                acc = p_ref[0].astype(F32)
                for s in range(1, N_DEV):
                    acc = acc + p_ref[s].astype(F32)
                g_ref[...] = acc
                d_ref[...], nm_ref[...], nv_ref[...] = _adamw_math(acc, w_ref[...], m_ref[...], v_ref[...])

    p_spec = pl.BlockSpec((N_DEV, tr, C), lambda l, i: (0, i, 0))
    blk = pl.BlockSpec((None, tr, C), lambda l, i: (l, i, 0))
    sh = jax.ShapeDtypeStruct((L, R, C), F32)
    return pl.pallas_call(
        body, grid=(L, R // tr), in_specs=[p_spec] * L + [blk] * 3, out_specs=[blk] * 4, out_shape=[sh] * 4,
        compiler_params=_cp("parallel", "parallel"), name=name)(*received, w, m, v)


def _adamw(g, w, m, v, *, tr, name):
    L, R, C = w.shape

    def body(g_ref, w_ref, m_ref, v_ref, d_ref, nm_ref, nv_ref):
        d_ref[...], nm_ref[...], nv_ref[...] = _adamw_math(g_ref[...], w_ref[...], m_ref[...], v_ref[...])

    blk = pl.BlockSpec((None, tr, C), lambda l, i: (l, i, 0))
    sh = jax.ShapeDtypeStruct((L, R, C), F32)
    return pl.pallas_call(
        body, grid=(L, R // tr), in_specs=[blk] * 4, out_specs=[blk] * 3, out_shape=[sh] * 3,
        compiler_params=_cp("parallel", "parallel"), name=name)(g, w, m, v)


UNITS = {
    "a_w_in": ("a_w_in", 0, True), "w_mem_kv0": ("w_mem_kv", 0, False), "w_out0": ("w_out", 0, False),
    "w_gate_up0": ("w_gate_up", 0, True), "w_down0": ("w_down", 0, False), "w_kv": ("w_kv", None, True),
    "b_w_in": ("b_w_in", 0, True), "w_mem_kv1": ("w_mem_kv", 1, False), "w_out1": ("w_out", 1, False),
    "w_gate_up1": ("w_gate_up", 1, True), "w_down1": ("w_down", 1, False),
}
BIG = ("a_w_in", "b_w_in", "w_kv", "w_mem_kv", "w_out", "w_gate_up", "w_down")
ADAMW_ROW_TILE = {"a_w_in": 208, "b_w_in": 160, "w_kv": 192, "w_mem_kv": 128, "w_out": 128, "w_gate_up": 352, "w_down": 352}


def _wire_block(weights, unit):
    name, layer, col = UNITS[unit]
    a = weights[name] if layer is None else weights[name][layer]
    return (a.T if col else a).astype(BF16)


SMALL_REPLICATED = ("norm_mix", "norm_ffn", "b_qnorm", "kv_norm", "b_knorm", "mem_norm", "mem_qnorm", "mem_knorm")
SMALL_SHARDED = ("a_lb_logits", "a_onorm")
SMALL_ORDER = SMALL_REPLICATED + SMALL_SHARDED
LANES = 128


def _prod(shape):
    n = 1
    for s in shape:
        n *= s
    return n


def _pack_flat(arrays, rows, cols, dtype):
    flat = jnp.concatenate([a.reshape(-1).astype(dtype) for a in arrays])
    return jnp.pad(flat, (0, rows * cols - flat.shape[0])).reshape(rows, cols)


def _unpack_flat(packed, shapes):
    flat = packed.reshape(-1)
    out, off = [], 0
    for s in shapes:
        out.append(flat[off:off + _prod(s)].reshape(s))
        off += _prod(s)
    return out


def kernel(x, mem, norm_mix, norm_ffn, a_w_in, a_lb_logits, a_onorm, b_w_in, b_qnorm, kv_norm, w_kv, b_knorm, mem_norm, w_mem_kv, mem_qnorm, mem_knorm, w_out, w_gate_up, w_down, loss_target, m_norm_mix, m_norm_ffn, m_a_w_in, m_a_lb_logits, m_a_onorm, m_b_w_in, m_b_qnorm, m_kv_norm, m_w_kv, m_b_knorm, m_mem_norm, m_w_mem_kv, m_mem_qnorm, m_mem_knorm, m_w_out, m_w_gate_up, m_w_down, v_norm_mix, v_norm_ffn, v_a_w_in, v_a_lb_logits, v_a_onorm, v_b_w_in, v_b_qnorm, v_kv_norm, v_w_kv, v_b_knorm, v_mem_norm, v_w_mem_kv, v_mem_qnorm, v_mem_knorm, v_w_out, v_w_gate_up, v_w_down):
    names = ("norm_mix", "norm_ffn", "a_w_in", "a_lb_logits", "a_onorm", "b_w_in", "b_qnorm", "kv_norm", "w_kv", "b_knorm",
             "mem_norm", "w_mem_kv", "mem_qnorm", "mem_knorm", "w_out", "w_gate_up", "w_down")
    w = dict(zip(names, (norm_mix, norm_ffn, a_w_in, a_lb_logits, a_onorm, b_w_in, b_qnorm, kv_norm, w_kv, b_knorm,
                         mem_norm, w_mem_kv, mem_qnorm, mem_knorm, w_out, w_gate_up, w_down)))
    m = dict(zip(names, (m_norm_mix, m_norm_ffn, m_a_w_in, m_a_lb_logits, m_a_onorm, m_b_w_in, m_b_qnorm, m_kv_norm, m_w_kv,
                         m_b_knorm, m_mem_norm, m_w_mem_kv, m_mem_qnorm, m_mem_knorm, m_w_out, m_w_gate_up, m_w_down)))
    v = dict(zip(names, (v_norm_mix, v_norm_ffn, v_a_w_in, v_a_lb_logits, v_a_onorm, v_b_w_in, v_b_qnorm, v_kv_norm, v_w_kv,
                         v_b_knorm, v_mem_norm, v_w_mem_kv, v_mem_qnorm, v_mem_knorm, v_w_out, v_w_gate_up, v_w_down)))

    first = ["a_w_in", "w_mem_kv0"]
    gathered = _all_gather([_wire_block(w, u) for u in first] + [_pack_flat([a_lb_logits, a_onorm], 8, LANES, F32)],
                           name="gather_first")
    full = {u: g.reshape(-1, g.shape[-1]) for u, g in zip(first, gathered)}
    small_in = gathered[-1].reshape(N_DEV, -1)
    P = {n: w[n] for n in SMALL_REPLICATED}
    P["a_lb_logits"] = small_in[:, :192].reshape(N_DEV, 2, 96).transpose(1, 0, 2).reshape(2, A_WIDTH)
    P["a_onorm"] = small_in[:, 192:288].reshape(1, A_WIDTH)
    later = [["w_out0", "w_gate_up0"], ["w_down0", "w_kv"], ["b_w_in", "w_mem_kv1"], ["w_out1", "w_gate_up1", "w_down1"]]
    first_half, second_half = {}, {}

    def start_first_half(i, after, carried=None):
        first_half[i] = _split_start([_wire_block(w, u) for u in later[i]], False, after, name=f"gather{i}_start",
                                     relations=SIBLING_AND_SAME_CORE, carried=carried)
        return first_half[i]

    token = start_first_half(0, gathered[-1])["token"]
    token = start_first_half(1, token)["token"]

    def forward_point(i, value):
        landed = _split_wait(first_half[i], value, name=f"gather{i}_landed")
        second_half[i], value = _forward_start(landed, value, name=f"gather{i}_forward")
        if i + 2 < len(later):
            value = start_first_half(i + 2, None, carried=value)["carried"]
        return value

    def get_w(unit, after):
        if unit not in full:
            i = [unit in group for group in later].index(True)
            for u, land in zip(later[i], _split_wait(second_half[i], after, name=f"gather{i}_wait")):
                full[u] = land.reshape(-1, land.shape[-1])
        return full[unit]

    sent = []

    def put_g(group):
        units = list(group)
        handle = _split_start([group[u].reshape(N_DEV, -1, group[u].shape[-1]) for u in units], True, None,
                              name=f"scatter{len(sent)}_start")
        sent.append((units, handle))
        return handle["token"]

    sq, gx, gP = _local_step(x[0], mem[0], loss_target[0], get_w, P, put_g, first_dep=token, forward_point=forward_point)
    loss_here = (0.5 * jnp.sum(sq) / D_MODEL).reshape(1)

    received = {}
    group_of = {u: i for i, (units, _) in enumerate(sent) for u in units}
    out = {"grad": {}, "delta": {}, "new_m": {}, "new_v": {}}
    newest = [gx]

    def update_big(n):
        shape = w[n].shape
        as3 = lambda a: a.reshape((-1,) + shape[-2:])
        mine = [u for u, (wn, _, _) in UNITS.items() if wn == n]
        for i in sorted({group_of[u] for u in mine}):
            if sent[i][0][0] not in received:
                received.update(zip(sent[i][0], _split_wait(sent[i][1], newest[0], name=f"scatter{i}_wait")))
        flip = (lambda a: jnp.swapaxes(a, 1, 2)) if UNITS[mine[0]][2] else (lambda a: a)
        res = _reduce_adamw([received[u] for u in mine], flip(as3(w[n])), flip(as3(m[n])), flip(as3(v[n])),
                            tr=ADAMW_ROW_TILE[n], name=f"adamw_{n}")
        newest[0] = res[1]
        for kind, r in zip(("grad", "delta", "new_m", "new_v"), res):
            out[kind][n] = flip(r).reshape(shape)

    for n in ("w_down", "w_gate_up", "w_out", "w_mem_kv", "b_w_in", "w_kv"):
        update_big(n)

    full_shapes = [(2, A_WIDTH) if n == "a_lb_logits" else (1, A_WIDTH) if n == "a_onorm" else w[n].shape for n in SMALL_ORDER]
    n_small = sum(_prod(s) for s in full_shapes) + 1
    rows_small = -(-n_small // (8 * LANES)) * 8
    g_all = _all_gather_direct(_pack_flat([gP[n] for n in SMALL_ORDER] + [loss_here], rows_small, LANES, F32),
                               newest[0], name="gather_small_grads")
    summed = _unpack_flat(_sum_sources(g_all, tr=rows_small, name="sum_small_grads"), full_shapes + [(1,)])
    g_small = dict(zip(SMALL_ORDER, summed))
    loss = summed[-1].reshape(())
    me = 4 * lax.axis_index("x") + 2 * lax.axis_index("y") + lax.axis_index("c")
    for n in SMALL_SHARDED:
        g_small[n] = lax.dynamic_slice_in_dim(g_small[n], me * 96, 96, axis=1)
    shapes = [w[n].shape for n in SMALL_ORDER]
    rows_upd = -(-sum(_prod(s) for s in shapes) // (8 * LANES)) * 8
    pk = lambda d: _pack_flat([d[n] for n in SMALL_ORDER], rows_upd, LANES, F32)
    res = _adamw(pk(g_small)[None], pk(w)[None], pk(m)[None], pk(v)[None], tr=rows_upd, name="adamw_small")
    out["grad"].update(g_small)
    for kind, packed in zip(("delta", "new_m", "new_v"), res):
        out[kind].update(zip(SMALL_ORDER, _unpack_flat(packed[0], shapes)))
    newest[0] = res[0]
    update_big("a_w_in")

    return (loss, gx[None], *[out["grad"][n] for n in names], *[out["delta"][n] for n in names],
            *[out["new_m"][n] for n in names], *[out["new_v"][n] for n in names])
```

```python
import functools

import jax
import jax.numpy as jnp
import numpy as np
from jax import lax
from jax.experimental import pallas as pl
from jax.experimental.pallas import tpu as pltpu

F32 = jnp.float32
BF16 = jnp.bfloat16

N_DEV = 8
D_MODEL = 1024
HEAD_DIM = 128
A_HEADS = 6
A_WIDTH = A_HEADS * HEAD_DIM
CHUNK = 64
B_HEADS = 6
B_WIDTH = B_HEADS * HEAD_DIM
DILATIONS = (1, 4, 16)
SPAN = 128
N_GROUPS = 3
ROPE_THETA = 10000.0
MEM_TOKENS = 256
MEM_HEADS = 4
MEM_HEAD_DIM = 64
MEM_WIDTH = MEM_HEADS * MEM_HEAD_DIM
FFN_HIDDEN = 2816
EPS = 1e-6

ADAM_LR = 0.001
ADAM_B1 = 0.9
ADAM_B2 = 0.999
ADAM_EPS = 1e-08
ADAM_WD = 0.01
ADAM_STEP = 10

V7X_VMEM_LIMIT_BYTES = 56 * 1024 * 1024

NT_DIMS = (((1,), (1,)), ((), ()))
TN_DIMS = (((0,), (0,)), ((), ()))


def _cp(*sem):
    return pltpu.CompilerParams(dimension_semantics=sem, vmem_limit_bytes=V7X_VMEM_LIMIT_BYTES)


def _dot(a, b):
    return jnp.dot(a.astype(BF16), b.astype(BF16), preferred_element_type=F32)


def _dot_nt(a, b):
    return lax.dot_general(a.astype(BF16), b.astype(BF16), NT_DIMS, preferred_element_type=F32)


def _dot_tn(a, b):
    return lax.dot_general(a.astype(BF16), b.astype(BF16), TN_DIMS, preferred_element_type=F32)


def _dot3(m01, x):
    hi = x.astype(BF16)
    r1 = x - hi.astype(F32)
    mid = r1.astype(BF16)
    lo = (r1 - mid.astype(F32)).astype(BF16)
    d = functools.partial(jnp.dot, preferred_element_type=F32)
    return d(m01, hi) + d(m01, mid) + d(m01, lo)


def _sigmoid(x):
    return 1.0 / (1.0 + jnp.exp(-x))


def _full(shape):
    return pl.BlockSpec(shape, lambda *_: (0,) * len(shape))


def _dep(body, n_in, dep):
    if dep is None:
        return body, [], []

    def with_dep(*refs):
        return body(*refs[:n_in], *refs[n_in + 1:])

    return with_dep, [pl.BlockSpec(memory_space=pl.ANY)], [dep]


def _rms_matmul(x, g, w, *, tt, tn, wt, name, out_dtype=F32, dep=None, rotate=None):
    T, K = x.shape
    N = w.shape[0] if wt else w.shape[1]
    n_rot = 0 if rotate is None else rotate[0].shape[1] // HEAD_DIM
    extra_in = [] if rotate is None else list(rotate)

    def kernel_body(x_ref, g_ref, w_ref, *rest):
        y_ref, xn_ref = rest[len(extra_in)], rest[len(extra_in) + 1]
        xf = x_ref[...]
        r = lax.rsqrt(jnp.mean(xf * xf, axis=-1, keepdims=True) + EPS)
        xn = (xf * r * g_ref[...]).astype(BF16)
        xn_ref[...] = xn
        for j in range(N // tn):
            cols = slice(j * tn, (j + 1) * tn)
            y = _dot_nt(xn, w_ref[cols, :]) if wt else _dot(xn, w_ref[:, cols])
            y_ref[:, cols] = y.astype(out_dtype)
            for h in range(j * tn // HEAD_DIM, min((j + 1) * tn // HEAD_DIM, n_rot)):
                gw_ref, c_ref, s_ref, yr_ref = rest[0], rest[1], rest[2], rest[len(extra_in) + 2]
                sl = slice(h * HEAD_DIM, (h + 1) * HEAD_DIM)
                xhat, _ = _head_rms(y[:, h * HEAD_DIM - j * tn:(h + 1) * HEAD_DIM - j * tn])
                yr_ref[:, sl] = _rope(xhat * gw_ref[:, sl], c_ref[...], s_ref[...])

    tbl = pl.BlockSpec((tt, HEAD_DIM), lambda i: (i, 0))
    rot_specs = [] if rotate is None else [_full((1, n_rot * HEAD_DIM)), tbl, tbl]
    body, dep_specs, dep_args = _dep(kernel_body, 3 + len(extra_in), dep)
    return pl.pallas_call(
        body, grid=(T // tt,),
        in_specs=[pl.BlockSpec((tt, K), lambda i: (i, 0)), _full((1, K)), _full(w.shape)] + rot_specs + dep_specs,
        out_specs=[pl.BlockSpec((tt, N), lambda i: (i, 0)), pl.BlockSpec((tt, K), lambda i: (i, 0))]
        + ([] if rotate is None else [pl.BlockSpec((tt, n_rot * HEAD_DIM), lambda i: (i, 0))]),
        out_shape=[jax.ShapeDtypeStruct((T, N), out_dtype), jax.ShapeDtypeStruct((T, K), BF16)]
        + ([] if rotate is None else [jax.ShapeDtypeStruct((T, n_rot * HEAD_DIM), F32)]),
        compiler_params=_cp("parallel"), name=name)(x, g, w, *extra_in, *dep_args)


def _mm_res(res, a1, a2, w, *, tt, name):
    T, K1 = a1.shape
    K2 = a2.shape[1]
    N = w.shape[1]

    def body(r_ref, a1_ref, a2_ref, w_ref, o_ref):
        o_ref[...] = r_ref[...] + _dot(a1_ref[...], w_ref[:K1, :]) + _dot(a2_ref[...], w_ref[K1:, :])

    return pl.pallas_call(
        body, grid=(T // tt,),
        in_specs=[pl.BlockSpec((tt, N), lambda i: (i, 0)), pl.BlockSpec((tt, K1), lambda i: (i, 0)),
                  pl.BlockSpec((tt, K2), lambda i: (i, 0)), _full((K1 + K2, N))],
        out_specs=pl.BlockSpec((tt, N), lambda i: (i, 0)),
        out_shape=jax.ShapeDtypeStruct((T, N), F32),
        compiler_params=_cp("parallel"), name=name)(res, a1, a2, w)


def _swiglu_down(h, gu, wd, *, tt, name):
    T, D = h.shape
    Fh = wd.shape[0]

    def body(h_ref, gt_ref, up_ref, w_ref, o_ref):
        gt = gt_ref[...].astype(F32)
        act = gt * _sigmoid(gt) * up_ref[...].astype(F32)
        o_ref[...] = h_ref[...] + _dot(act, w_ref[...])

    return pl.pallas_call(
        body, grid=(T // tt,),
        in_specs=[pl.BlockSpec((tt, D), lambda i: (i, 0)), pl.BlockSpec((tt, Fh), lambda i: (i, 0)),
                  pl.BlockSpec((tt, Fh), lambda i: (i, 1)), _full((Fh, D))],
        out_specs=pl.BlockSpec((tt, D), lambda i: (i, 0)),
        out_shape=jax.ShapeDtypeStruct((T, D), F32),
        compiler_params=_cp("parallel"), name=name)(h, gu, gu, wd)


def _swiglu_down_loss(h, gu, wd, tgt, *, tt, name):
    T, D = h.shape
    Fh = wd.shape[0]

    def body(h_ref, gt_ref, up_ref, w_ref, t_ref, dy_ref, acc_ref):
        @pl.when(pl.program_id(0) == 0)
        def _():
            acc_ref[...] = jnp.zeros_like(acc_ref)

        gt = gt_ref[...].astype(F32)
        act = gt * _sigmoid(gt) * up_ref[...].astype(F32)
        e = h_ref[...] + _dot(act, w_ref[...]) - t_ref[...]
        dy_ref[...] = e * (1.0 / D)
        acc_ref[...] += jnp.sum(e * e, axis=0, keepdims=True)

    row = pl.BlockSpec((tt, D), lambda i: (i, 0))
    return pl.pallas_call(
        body, grid=(T // tt,),
        in_specs=[row, pl.BlockSpec((tt, Fh), lambda i: (i, 0)), pl.BlockSpec((tt, Fh), lambda i: (i, 1)), _full((Fh, D)), row],
        out_specs=[row, _full((1, D))],
        out_shape=[jax.ShapeDtypeStruct((T, D), F32), jax.ShapeDtypeStruct((1, D), F32)],
        compiler_params=_cp("arbitrary"), name=name)(h, gu, gu, wd, tgt)


def _swiglu_bwd(dh, gu, wd, *, tt, name):
    T, D = dh.shape
    Fh = wd.shape[0]
    last = T // tt - 1

    def body(dh_ref, gt_ref, up_ref, w_ref, dgu_ref, gw_ref, acc):
        @pl.when(pl.program_id(0) == 0)
        def _():
            acc[...] = jnp.zeros_like(acc)

        gt = gt_ref[...].astype(F32)
        up = up_ref[...].astype(F32)
        s = _sigmoid(gt)
        silu = gt * s
        dh16 = dh_ref[...].astype(BF16)
        dact = _dot_nt(dh16, w_ref[...])
        acc[...] += _dot_tn((silu * up).astype(BF16), dh16)
        dgu_ref[:, :Fh] = (dact * up * (s * (1.0 + gt * (1.0 - s)))).astype(BF16)
        dgu_ref[:, Fh:] = (dact * silu).astype(BF16)

        @pl.when(pl.program_id(0) == last)
        def _():
            gw_ref[...] = acc[...].astype(BF16)

    return pl.pallas_call(
        body, grid=(T // tt,),
        in_specs=[pl.BlockSpec((tt, D), lambda i: (i, 0)), pl.BlockSpec((tt, Fh), lambda i: (i, 0)),
                  pl.BlockSpec((tt, Fh), lambda i: (i, 1)), _full((Fh, D))],
        out_specs=[pl.BlockSpec((tt, 2 * Fh), lambda i: (i, 0)), _full((Fh, D))],
        out_shape=[jax.ShapeDtypeStruct((T, 2 * Fh), BF16), jax.ShapeDtypeStruct((Fh, D), BF16)],
        scratch_shapes=[pltpu.VMEM((Fh, D), F32)],
        compiler_params=_cp("arbitrary"), name=name)(dh, gu, gu, wd)


def _out_proj_bwd(dy, a1, a2, w, *, tt, name, head_dots=False):
    T, N = dy.shape
    K1, K2 = a1.shape[1], a2.shape[1]
    K = K1 + K2
    last = T // tt - 1

    def body(dy_ref, a1_ref, a2_ref, w_ref, da_ref, gw_ref, *rest):
        acc = rest[-1]

        @pl.when(pl.program_id(0) == 0)
        def _():
            acc[...] = jnp.zeros_like(acc)

        dy16 = dy_ref[...].astype(BF16)
        da = _dot_nt(dy16, w_ref[...])
        da_ref[...] = da
        acc[:K1, :] += _dot_tn(a1_ref[...], dy16)
        acc[K1:, :] += _dot_tn(a2_ref[...], dy16)
        if head_dots:
            for h in range(K1 // HEAD_DIM):
                sl = slice(h * HEAD_DIM, (h + 1) * HEAD_DIM)
                rest[0][:, sl] = jnp.broadcast_to(jnp.sum(da[:, sl] * a1_ref[:, sl], axis=-1, keepdims=True), (tt, HEAD_DIM))

        @pl.when(pl.program_id(0) == last)
        def _():
            gw_ref[...] = acc[...].astype(BF16)

    extra_specs = [pl.BlockSpec((tt, K1), lambda i: (i, 0))] if head_dots else []
    extra_shapes = [jax.ShapeDtypeStruct((T, K1), F32)] if head_dots else []
    return pl.pallas_call(
        body, grid=(T // tt,),
        in_specs=[pl.BlockSpec((tt, N), lambda i: (i, 0)), pl.BlockSpec((tt, K1), lambda i: (i, 0)),
                  pl.BlockSpec((tt, K2), lambda i: (i, 0)), _full((K, N))],
        out_specs=[pl.BlockSpec((tt, K), lambda i: (i, 0)), _full((K, N))] + extra_specs,
        out_shape=[jax.ShapeDtypeStruct((T, K), F32), jax.ShapeDtypeStruct((K, N), BF16)] + extra_shapes,
        scratch_shapes=[pltpu.VMEM((K, N), F32)],
        compiler_params=_cp("arbitrary"), name=name)(dy, a1, a2, w)


def _mm_tn(a, b, *, tt, tka, name):
    T, Ka = a.shape
    N = b.shape[1]
    last = T // tt - 1

    def body(a_ref, b_ref, o_ref, acc):
        @pl.when(pl.program_id(1) == 0)
        def _():
            acc[...] = jnp.zeros_like(acc)

        acc[...] += _dot_tn(a_ref[...], b_ref[...])

        @pl.when(pl.program_id(1) == last)
        def _():
            o_ref[...] = acc[...].astype(BF16)

    return pl.pallas_call(
        body, grid=(Ka // tka, T // tt),
        in_specs=[pl.BlockSpec((tt, tka), lambda j, t: (t, j)), pl.BlockSpec((tt, N), lambda j, t: (t, 0))],
        out_specs=pl.BlockSpec((tka, N), lambda j, t: (j, 0)),
        out_shape=jax.ShapeDtypeStruct((Ka, N), BF16),
        scratch_shapes=[pltpu.VMEM((tka, N), F32)],
        compiler_params=_cp("parallel", "arbitrary"), name=name)(a, b)


def _mm_tn_pieces(pieces, b, *, tt, name):
    n = len(pieces)
    T = b.shape[0]
    N = b.shape[1]
    widths = [p.shape[1] for p in pieces]
    Ka = sum(widths)
    last = T // tt - 1

    def body(*refs):
        p_refs = refs[:n]
        b_ref, o_ref, acc = refs[n:]

        @pl.when(pl.program_id(0) == 0)
        def _():
            acc[...] = jnp.zeros_like(acc)

        bv = b_ref[...].astype(BF16)
        off = 0
        for p_ref, wd in zip(p_refs, widths):
            acc[off:off + wd, :] += _dot_tn(p_ref[...], bv)
            off += wd

        @pl.when(pl.program_id(0) == last)
        def _():
            o_ref[...] = acc[...].astype(BF16)

    return pl.pallas_call(
        body, grid=(T // tt,),
        in_specs=[pl.BlockSpec((tt, wd), lambda t: (t, 0)) for wd in widths] + [pl.BlockSpec((tt, N), lambda t: (t, 0))],
        out_specs=_full((Ka, N)), out_shape=jax.ShapeDtypeStruct((Ka, N), BF16),
        scratch_shapes=[pltpu.VMEM((Ka, N), F32)],
        compiler_params=_cp("arbitrary"), name=name)(*pieces, b)


def _rms_bwd_dx(x, g, w, dy, dres, *, tt, wt, name, dep=None):
    pieces = list(dy) if isinstance(dy, (list, tuple)) else [dy]
    n = len(pieces)
    widths = [p.shape[1] for p in pieces]
    T, K = x.shape

    def kernel_body(x_ref, g_ref, w_ref, *rest):
        dy_refs = rest[:n]
        dres_ref, dx_ref, dg_ref = rest[n:]

        @pl.when(pl.program_id(0) == 0)
        def _():
            dg_ref[...] = jnp.zeros_like(dg_ref)

        if n == 1:
            dxn = (_dot if wt else _dot_nt)(dy_refs[0][...], w_ref[...])
        else:
            dxn, off = 0.0, 0
            for dy_ref, wd in zip(dy_refs, widths):
                dxn = dxn + _dot(dy_ref[...], w_ref[off:off + wd, :])
                off += wd
        xf = x_ref[...]
        r = lax.rsqrt(jnp.mean(xf * xf, axis=-1, keepdims=True) + EPS)
        xhat = xf * r
        dg_ref[...] += jnp.sum(dxn * xhat, axis=0, keepdims=True)
        dxhat = dxn * g_ref[...]
        dx_ref[...] = dres_ref[...] + r * (dxhat - xhat * jnp.mean(dxhat * xhat, axis=-1, keepdims=True))

    assert n == 1 or wt
    body, dep_specs, dep_args = _dep(kernel_body, 4 + n, dep)
    return pl.pallas_call(
        body, grid=(T // tt,),
        in_specs=[pl.BlockSpec((tt, K), lambda i: (i, 0)), _full((1, K)), _full(w.shape)]
        + [pl.BlockSpec((tt, wd), lambda i: (i, 0)) for wd in widths]
        + [pl.BlockSpec((tt, K), lambda i: (i, 0))] + dep_specs,
        out_specs=[pl.BlockSpec((tt, K), lambda i: (i, 0)), _full((1, K))],
        out_shape=[jax.ShapeDtypeStruct((T, K), F32), jax.ShapeDtypeStruct((1, K), F32)],
        compiler_params=_cp("arbitrary"), name=name)(x, g, w, *pieces, dres, *dep_args)


HGRN_TB = 512
HGRN_NCH = HGRN_TB // CHUNK
HGRN_HPB = 6


def _hgrn_chunk_fwd(q, z, lbv, tril01):
    sig = _sigmoid(z)
    f = lbv + (1.0 - lbv) * sig
    kk = 1.0 - f
    b = _dot3(tril01, jnp.log(f))
    bend = b[CHUNK - 1:CHUNK, :]
    sq = _sigmoid(q)
    eb = jnp.exp(b)
    emb = jnp.exp(-b)
    eo = jnp.exp(bend - b)
    dec = jnp.exp(bend)
    return sig, f, kk, sq, eb, emb, eo, dec


def _hgrn2_fwd(proj, lb, *, name):
    T = proj.shape[0]
    nT = T // HGRN_TB
    nC = T // CHUNK

    def body(q_ref, z_ref, v_ref, lb_ref, o_ref, st_ref, state):
        @pl.when(pl.program_id(1) == 0)
        def _():
            state[...] = jnp.zeros_like(state)

        row = lax.broadcasted_iota(jnp.int32, (CHUNK, CHUNK), 0)
        col = lax.broadcasted_iota(jnp.int32, (CHUNK, CHUNK), 1)
        causal = row >= col
        tril01 = causal.astype(BF16)

        def chunk(c, carry):
            rows = pl.ds(pl.multiple_of(c * CHUNK, CHUNK), CHUNK)
            for hh in range(HGRN_HPB):
                sl = slice(hh * HEAD_DIM, (hh + 1) * HEAD_DIM)
                q = q_ref[rows, sl]
                v = v_ref[rows, sl].astype(BF16)
                sig, f, kk, sq, eb, emb, eo, dec = _hgrn_chunk_fwd(q, z_ref[rows, sl], lb_ref[:, sl], tril01)
                qi = (q * sq * eb).astype(BF16)
                ki = (kk * emb).astype(BF16)
                ko = (kk * eo).astype(BF16)
                st = state[hh]
                att = jnp.where(causal, _dot_nt(qi, ki), 0.0)
                o_ref[rows, sl] = _dot(att, v) + _dot_nt(qi, st)
                st_ref[c, hh] = st
                state[hh] = st * dec + _dot_tn(v, ko)
            return carry

        lax.fori_loop(0, HGRN_NCH, chunk, 0)

    W = HGRN_HPB * HEAD_DIM
    nG = A_HEADS // HGRN_HPB
    hb = lambda off: pl.BlockSpec((HGRN_TB, W), lambda h, i: (i, off + h))
    return pl.pallas_call(
        body, grid=(nG, nT),
        in_specs=[hb(0), hb(nG), hb(2 * nG), pl.BlockSpec((1, W), lambda h, i: (0, h))],
        out_specs=[hb(0), pl.BlockSpec((HGRN_NCH, HGRN_HPB, HEAD_DIM, HEAD_DIM), lambda h, i: (i, h, 0, 0))],
        out_shape=[jax.ShapeDtypeStruct((T, A_WIDTH), F32), jax.ShapeDtypeStruct((nC, A_HEADS, HEAD_DIM, HEAD_DIM), F32)],
        scratch_shapes=[pltpu.VMEM((HGRN_HPB, HEAD_DIM, HEAD_DIM), F32)],
        compiler_params=_cp("parallel", "arbitrary"), name=name)(proj, proj, proj, lb)


def _hgrn2_bwd(proj, lb, st_all, do, *, name):
    T = proj.shape[0]
    nT = T // HGRN_TB

    def body(q_ref, z_ref, v_ref, lb_ref, st_ref, do_ref, dq_ref, dz_ref, dv_ref, dlb_ref, dstate):
        @pl.when(pl.program_id(1) == 0)
        def _():
            dstate[...] = jnp.zeros_like(dstate)
            dlb_ref[...] = jnp.zeros_like(dlb_ref)

        row = lax.broadcasted_iota(jnp.int32, (CHUNK, CHUNK), 0)
        col = lax.broadcasted_iota(jnp.int32, (CHUNK, CHUNK), 1)
        causal = row >= col
        tril01 = causal.astype(BF16)
        triu01 = (row <= col).astype(BF16)

        def chunk(cc, carry):
            c = HGRN_NCH - 1 - cc
            rows = pl.ds(pl.multiple_of(c * CHUNK, CHUNK), CHUNK)
            for hh in range(HGRN_HPB):
                sl = slice(hh * HEAD_DIM, (hh + 1) * HEAD_DIM)
                lbv = lb_ref[:, sl]
                q = q_ref[rows, sl]
                v = v_ref[rows, sl].astype(BF16)
                sig, f, kk, sq, eb, emb, eo, dec = _hgrn_chunk_fwd(q, z_ref[rows, sl], lbv, tril01)
                qi32 = q * sq * eb
                ki32 = kk * emb
                ko32 = kk * eo
                qi, ki, ko = qi32.astype(BF16), ki32.astype(BF16), ko32.astype(BF16)
                att = jnp.where(causal, _dot_nt(qi, ki), 0.0).astype(BF16)
                dout = do_ref[rows, sl].astype(BF16)
                st = st_ref[c, hh]
                dst = dstate[hh]
                dst16 = dst.astype(BF16)
                datt = jnp.where(causal, _dot_nt(dout, v), 0.0).astype(BF16)
                dqi = _dot(datt, ki) + _dot(dout, st)
                dki = _dot_tn(datt, qi)
                dv_ref[rows, sl] = (_dot_tn(att, dout) + _dot_nt(ko, dst16)).astype(BF16)
                dko = _dot(v, dst16)
                ddec = jnp.sum(dst * st, axis=0, keepdims=True)
                dstate[hh] = dst * dec + _dot_tn(dout, qi)
                dkk = dki * emb + dko * eo
                db = dqi * qi32 - dki * ki32 - dko * ko32
                dbend = jnp.sum(dko * ko32, axis=0, keepdims=True) + ddec * dec
                dlogf = _dot3(triu01, db) + dbend
                df = dlogf / f - dkk
                dz_ref[rows, sl] = (df * (1.0 - lbv) * sig * (1.0 - sig)).astype(BF16)
                dlb_ref[:, sl] += jnp.sum(df * (1.0 - sig), axis=0, keepdims=True)
                dq_ref[rows, sl] = (dqi * eb * (sq * (1.0 + q * (1.0 - sq)))).astype(BF16)
            return carry

        lax.fori_loop(0, HGRN_NCH, chunk, 0)

    W = HGRN_HPB * HEAD_DIM
    nG = A_HEADS // HGRN_HPB
    hb = lambda off: pl.BlockSpec((HGRN_TB, W), lambda h, i: (nT - 1 - i, off + h))
    hlb = pl.BlockSpec((1, W), lambda h, i: (0, h))
    o16 = jax.ShapeDtypeStruct((T, A_WIDTH), BF16)
    return pl.pallas_call(
        body, grid=(nG, nT),
        in_specs=[hb(0), hb(nG), hb(2 * nG), hlb,
                  pl.BlockSpec((HGRN_NCH, HGRN_HPB, HEAD_DIM, HEAD_DIM), lambda h, i: (nT - 1 - i, h, 0, 0)), hb(0)],
        out_specs=[hb(0), hb(0), hb(0), hlb],
        out_shape=[o16, o16, o16, jax.ShapeDtypeStruct((1, A_WIDTH), F32)],
        scratch_shapes=[pltpu.VMEM((HGRN_HPB, HEAD_DIM, HEAD_DIM), F32)],
        compiler_params=_cp("parallel", "arbitrary"), name=name)(proj, proj, proj, lb, st_all, do)


def _head_rms(x):
    r = lax.rsqrt(jnp.mean(x * x, axis=-1, keepdims=True) + EPS)
    return x * r, r


def _head_rms_bwd(dxhat, xhat, r):
    return r * (dxhat - xhat * jnp.mean(dxhat * xhat, axis=-1, keepdims=True))


def _a_post_fwd(o, proj, onorm, *, tt, name):
    T = o.shape[0]

    def body(o_ref, g_ref, w_ref, y_ref):
        for h in range(A_HEADS):
            sl = slice(h * HEAD_DIM, (h + 1) * HEAD_DIM)
            xhat, _ = _head_rms(o_ref[:, sl])
            g = g_ref[:, sl]
            y_ref[:, sl] = xhat * w_ref[:, sl] * (g * _sigmoid(g))

    blk = lambda c: pl.BlockSpec((tt, A_WIDTH), lambda i: (i, c))
    return pl.pallas_call(
        body, grid=(T // tt,), in_specs=[blk(0), blk(3), _full((1, A_WIDTH))], out_specs=blk(0),
        out_shape=jax.ShapeDtypeStruct((T, A_WIDTH), F32),
        compiler_params=_cp("parallel"), name=name)(o, proj, onorm)


def _a_post_bwd(o, proj, onorm, dmix, *, tt, name, dep=None):
    T = o.shape[0]

    def kernel_body(o_ref, g_ref, w_ref, dy_ref, do_ref, dg_ref, dw_ref):
        @pl.when(pl.program_id(0) == 0)
        def _():
            dw_ref[...] = jnp.zeros_like(dw_ref)

        for h in range(A_HEADS):
            sl = slice(h * HEAD_DIM, (h + 1) * HEAD_DIM)
            xhat, r = _head_rms(o_ref[:, sl])
            g = g_ref[:, sl]
            s = _sigmoid(g)
            dy = dy_ref[:, sl]
            w = w_ref[:, sl]
            dg_ref[:, sl] = (dy * xhat * w * (s * (1.0 + g * (1.0 - s)))).astype(BF16)
            dyn = dy * (g * s)
            dw_ref[:, sl] += jnp.sum(dyn * xhat, axis=0, keepdims=True)
            do_ref[:, sl] = _head_rms_bwd(dyn * w, xhat, r)

    blk = lambda c: pl.BlockSpec((tt, A_WIDTH), lambda i: (i, c))
    body, dep_specs, dep_args = _dep(kernel_body, 4, dep)
    return pl.pallas_call(
        body, grid=(T // tt,), in_specs=[blk(0), blk(3), _full((1, A_WIDTH)), blk(0)] + dep_specs,
        out_specs=[blk(0), blk(0), _full((1, A_WIDTH))],
        out_shape=[jax.ShapeDtypeStruct((T, A_WIDTH), F32), jax.ShapeDtypeStruct((T, A_WIDTH), BF16),
                   jax.ShapeDtypeStruct((1, A_WIDTH), F32)],
        compiler_params=_cp("arbitrary"), name=name)(o, proj, onorm, dmix, *dep_args)


def _mem_head_masks(n):
    lane = lax.broadcasted_iota(jnp.int32, (n, MEM_WIDTH), 1)
    return [(lane >= m * MEM_HEAD_DIM) & (lane < (m + 1) * MEM_HEAD_DIM) for m in range(MEM_HEADS)]


def _mem_head_rms(x, masks):
    x2 = x * x
    r = jnp.zeros_like(x)
    for mk in masks:
        ms = jnp.sum(jnp.where(mk, x2, 0.0), axis=-1, keepdims=True) * (1.0 / MEM_HEAD_DIM)
        r = jnp.where(mk, lax.rsqrt(ms + EPS), r)
    return x * r, r


def _mem_head_rms_bwd(dxhat, xhat, r, masks):
    t = dxhat * xhat
    m = jnp.zeros_like(t)
    for mk in masks:
        m = jnp.where(mk, jnp.sum(jnp.where(mk, t, 0.0), axis=-1, keepdims=True) * (1.0 / MEM_HEAD_DIM), m)
    return r * (dxhat - xhat * m)


MEM_SCALE = MEM_HEAD_DIM ** -0.5


def _mem_attn_fwd(proj, qcol, mkv, qn_w, kn_w, *, tt, name):
    T = proj.shape[0]

    def body(q_ref, k_ref, v_ref, qw_ref, kw_ref, o_ref):
        qmasks = _mem_head_masks(tt)
        kmasks = _mem_head_masks(MEM_TOKENS)
        qhat, _ = _mem_head_rms(q_ref[...], qmasks)
        qn = qhat * qw_ref[...]
        khat, _ = _mem_head_rms(k_ref[...], kmasks)
        kn = (khat * kw_ref[...]).astype(BF16)
        v = v_ref[...].astype(BF16)
        out = jnp.zeros((tt, MEM_WIDTH), F32)
        for m in range(MEM_HEADS):
            s = _dot_nt(jnp.where(qmasks[m], qn, 0.0), kn) * MEM_SCALE
            s = s - jnp.max(s, axis=-1, keepdims=True)
            p = jnp.exp(s)
            p = p / jnp.sum(p, axis=-1, keepdims=True)
            out = jnp.where(qmasks[m], _dot(p, v), out)
        o_ref[...] = out

    return pl.pallas_call(
        body, grid=(T // tt,),
        in_specs=[pl.BlockSpec((tt, MEM_WIDTH), lambda i: (i, qcol)), pl.BlockSpec((MEM_TOKENS, MEM_WIDTH), lambda i: (0, 0)),
                  pl.BlockSpec((MEM_TOKENS, MEM_WIDTH), lambda i: (0, 1)), _full((1, MEM_WIDTH)), _full((1, MEM_WIDTH))],
        out_specs=pl.BlockSpec((tt, MEM_WIDTH), lambda i: (i, 0)),
        out_shape=jax.ShapeDtypeStruct((T, MEM_WIDTH), F32),
        compiler_params=_cp("parallel"), name=name)(proj, mkv, mkv, qn_w, kn_w)


def _mem_attn_bwd(proj, qcol, mkv, qn_w, kn_w, dmix, *, tt, name):
    T = proj.shape[0]
    nsteps = T // tt
    ocol = (dmix.shape[1] - MEM_WIDTH) // MEM_WIDTH

    def body(q_ref, k_ref, v_ref, qw_ref, kw_ref, do_ref, dq_ref, dkv_ref, dqw_ref, dkw_ref, dk_acc, dv_acc):
        step = pl.program_id(0)

        @pl.when(step == 0)
        def _():
            dk_acc[...] = jnp.zeros_like(dk_acc)
            dv_acc[...] = jnp.zeros_like(dv_acc)
            dqw_ref[...] = jnp.zeros_like(dqw_ref)

        qmasks = _mem_head_masks(tt)
        kmasks = _mem_head_masks(MEM_TOKENS)
        qhat, qr = _mem_head_rms(q_ref[...], qmasks)
        qn = qhat * qw_ref[...]
        khat, kr = _mem_head_rms(k_ref[...], kmasks)
        kn = (khat * kw_ref[...]).astype(BF16)
        v = v_ref[...].astype(BF16)
        dout = do_ref[...]
        dqn = jnp.zeros((tt, MEM_WIDTH), F32)
        dkn = jnp.zeros((MEM_TOKENS, MEM_WIDTH), F32)
        dvv = jnp.zeros((MEM_TOKENS, MEM_WIDTH), F32)
        for m in range(MEM_HEADS):
            qm = jnp.where(qmasks[m], qn, 0.0).astype(BF16)
            s = _dot_nt(qm, kn) * MEM_SCALE
            s = s - jnp.max(s, axis=-1, keepdims=True)
            p = jnp.exp(s)
            p = p / jnp.sum(p, axis=-1, keepdims=True)
            dom = jnp.where(qmasks[m], dout, 0.0).astype(BF16)
            dp = _dot_nt(dom, v)
            ds = (p * (dp - jnp.sum(p * dp, axis=-1, keepdims=True)) * MEM_SCALE).astype(BF16)
            dqn = jnp.where(qmasks[m], _dot(ds, kn), dqn)
            dkn = jnp.where(kmasks[m], _dot_tn(ds, qm), dkn)
            dvv = jnp.where(kmasks[m], _dot_tn(p, dom), dvv)
        dqw_ref[...] += jnp.sum(dqn * qhat, axis=0, keepdims=True)
        dq_ref[...] = _mem_head_rms_bwd(dqn * qw_ref[...], qhat, qr, qmasks).astype(BF16)
        dk_acc[...] += dkn
        dv_acc[...] += dvv

        @pl.when(step == nsteps - 1)
        def _():
            dk = dk_acc[...]
            dkw_ref[...] = jnp.sum(dk * khat, axis=0, keepdims=True)
            dkv_ref[:, :MEM_WIDTH] = _mem_head_rms_bwd(dk * kw_ref[...], khat, kr, kmasks)
            dkv_ref[:, MEM_WIDTH:] = dv_acc[...]

    return pl.pallas_call(
        body, grid=(nsteps,),
        in_specs=[pl.BlockSpec((tt, MEM_WIDTH), lambda i: (i, qcol)), pl.BlockSpec((MEM_TOKENS, MEM_WIDTH), lambda i: (0, 0)),
                  pl.BlockSpec((MEM_TOKENS, MEM_WIDTH), lambda i: (0, 1)), _full((1, MEM_WIDTH)), _full((1, MEM_WIDTH)),
                  pl.BlockSpec((tt, MEM_WIDTH), lambda i: (i, ocol))],
        out_specs=[pl.BlockSpec((tt, MEM_WIDTH), lambda i: (i, 0)), _full((MEM_TOKENS, 2 * MEM_WIDTH)),
                   _full((1, MEM_WIDTH)), _full((1, MEM_WIDTH))],
        out_shape=[jax.ShapeDtypeStruct((T, MEM_WIDTH), BF16), jax.ShapeDtypeStruct((MEM_TOKENS, 2 * MEM_WIDTH), F32),
                   jax.ShapeDtypeStruct((1, MEM_WIDTH), F32), jax.ShapeDtypeStruct((1, MEM_WIDTH), F32)],
        scratch_shapes=[pltpu.VMEM((MEM_TOKENS, MEM_WIDTH), F32), pltpu.VMEM((MEM_TOKENS, MEM_WIDTH), F32)],
        compiler_params=_cp("arbitrary"), name=name)(proj, mkv, mkv, qn_w, kn_w, dmix)


HALF = HEAD_DIM // 2
ATT_SCALE = HEAD_DIM ** -0.5
NEG = -1e30


def _rope_tables(T):
    inv = np.float32(ROPE_THETA) ** (-np.arange(HALF, dtype=np.float32) / np.float32(HALF))
    ang = np.arange(T, dtype=np.float32)[:, None] * inv[None, :].astype(np.float32)
    cos, sin = np.cos(ang).astype(np.float32), np.sin(ang).astype(np.float32)
    return jnp.asarray(np.concatenate([cos, cos], axis=-1)), jnp.asarray(np.concatenate([-sin, sin], axis=-1))


def _rope(x, cosf, sinsg):
    return x * cosf + pltpu.roll(x, HALF, 1) * sinsg


def _rope_bwd(dy, cosf, sinsg):
    return dy * cosf + pltpu.roll(dy * sinsg, HALF, 1)


def _q_prep_bwd(proj, w_heads, cosf, sinsg, dqs, *, tt, name):
    T = proj.shape[0]
    W = N_GROUPS * B_WIDTH

    def body(x_ref, w_ref, c_ref, s_ref, d0, d1, d2, dx_ref, dw_ref):
        @pl.when(pl.program_id(0) == 0)
        def _():
            dw_ref[...] = jnp.zeros_like(dw_ref)

        c, s = c_ref[...], s_ref[...]
        for gi, d_ref in enumerate((d0, d1, d2)):
            for h in range(B_HEADS):
                sl = slice((gi * B_HEADS + h) * HEAD_DIM, (gi * B_HEADS + h + 1) * HEAD_DIM)
                xhat, r = _head_rms(x_ref[:, sl])
                dyn = _rope_bwd(d_ref[:, h * HEAD_DIM:(h + 1) * HEAD_DIM], c, s)
                dw_ref[:, sl] += jnp.sum(dyn * xhat, axis=0, keepdims=True)
                dx_ref[:, sl] = _head_rms_bwd(dyn * w_ref[:, sl], xhat, r).astype(BF16)

    tbl = pl.BlockSpec((tt, HEAD_DIM), lambda i: (i, 0))
    dyb = pl.BlockSpec((tt, B_WIDTH), lambda i: (i, 0))
    return pl.pallas_call(
        body, grid=(T // tt,),
        in_specs=[pl.BlockSpec((tt, W), lambda i: (i, 0)), _full((1, W)), tbl, tbl, dyb, dyb, dyb],
        out_specs=[pl.BlockSpec((tt, W), lambda i: (i, 0)), _full((1, W))],
        out_shape=[jax.ShapeDtypeStruct((T, W), BF16), jax.ShapeDtypeStruct((1, W), F32)],
        compiler_params=_cp("arbitrary"), name=name)(proj, w_heads, cosf, sinsg, *dqs)


def _kv_prep_bwd(kv, w_heads, cosf, sinsg, dks, dvs, *, tt, name):
    T = kv.shape[0]

    def body(x_ref, w_ref, c_ref, s_ref, k0, k1, k2, v0, v1, v2, dx_ref, dw_ref):
        @pl.when(pl.program_id(0) == 0)
        def _():
            dw_ref[...] = jnp.zeros_like(dw_ref)

        c, s = c_ref[...], s_ref[...]
        for h in range(B_HEADS):
            sl = slice(h * HEAD_DIM, (h + 1) * HEAD_DIM)
            vs = slice(B_WIDTH + h * HEAD_DIM, B_WIDTH + (h + 1) * HEAD_DIM)
            xhat, r = _head_rms(x_ref[:, sl])
            dyn = _rope_bwd(k0[:, sl] + k1[:, sl] + k2[:, sl], c, s)
            dw_ref[:, sl] += jnp.sum(dyn * xhat, axis=0, keepdims=True)
            dx_ref[:, sl] = _head_rms_bwd(dyn * w_ref[:, sl], xhat, r).astype(BF16)
            dx_ref[:, vs] = (v0[:, sl] + v1[:, sl] + v2[:, sl]).astype(BF16)

    tbl = pl.BlockSpec((tt, HEAD_DIM), lambda i: (i, 0))
    dyb = pl.BlockSpec((tt, B_WIDTH), lambda i: (i, 0))
    return pl.pallas_call(
        body, grid=(T // tt,),
        in_specs=[dyb, _full((1, B_WIDTH)), tbl, tbl] + [dyb] * 6,
        out_specs=[pl.BlockSpec((tt, 2 * B_WIDTH), lambda i: (i, 0)), _full((1, B_WIDTH))],
        out_shape=[jax.ShapeDtypeStruct((T, 2 * B_WIDTH), BF16), jax.ShapeDtypeStruct((1, B_WIDTH), F32)],
        compiler_params=_cp("arbitrary"), name=name)(kv, w_heads, cosf, sinsg, *dks, *dvs)


def _band_masks(n_is_first=None):
    row = lax.broadcasted_iota(jnp.int32, (SPAN, SPAN), 0)
    col = lax.broadcasted_iota(jnp.int32, (SPAN, SPAN), 1)
    return row >= col, col >= row


def _dil_views(T, d):
    L = T // d
    return L, L // SPAN


def _dil_fwd(qr, kr, kv, gi, d, *, name):
    T = qr.shape[0]
    L, nb = _dil_views(T, d)

    def body(q_ref, kc_ref, kp_ref, vc_ref, vp_ref, o_ref, lse_ref):
        cur_ok, prev_band = _band_masks()
        prev_ok = prev_band & (pl.program_id(1) > 0)
        for h in range(B_HEADS):
            sl = slice(h * HEAD_DIM, (h + 1) * HEAD_DIM)
            q = q_ref[:, sl]
            sc = jnp.where(cur_ok, _dot_nt(q, kc_ref[:, sl]) * ATT_SCALE, NEG)
            sp = jnp.where(prev_ok, _dot_nt(q, kp_ref[:, sl]) * ATT_SCALE, NEG)
            m = jnp.maximum(jnp.max(sc, axis=-1, keepdims=True), jnp.max(sp, axis=-1, keepdims=True))
            pc = jnp.exp(sc - m)
            pp = jnp.exp(sp - m)
            l = jnp.sum(pc, axis=-1, keepdims=True) + jnp.sum(pp, axis=-1, keepdims=True)
            o_ref[:, sl] = (_dot(pc, vc_ref[:, sl]) + _dot(pp, vp_ref[:, sl])) / l
            lse_ref[:, sl] = jnp.broadcast_to(m + jnp.log(l), (SPAN, HEAD_DIM))

    blk = lambda f: pl.BlockSpec((SPAN, B_WIDTH), f)
    cur = lambda r, n: (n, r)
    prev = lambda r, n: (jnp.maximum(n - 1, 0), r)
    ov = jax.ShapeDtypeStruct((L, d * B_WIDTH), F32)
    o, lse = pl.pallas_call(
        body, grid=(d, nb),
        in_specs=[blk(lambda r, n: (n, r * N_GROUPS + gi)), blk(cur), blk(prev),
                  blk(lambda r, n: (n, 2 * r + 1)), blk(lambda r, n: (jnp.maximum(n - 1, 0), 2 * r + 1))],
        out_specs=[blk(cur), blk(cur)], out_shape=[ov, ov],
        compiler_params=_cp("parallel", "arbitrary"), name=name,
    )(qr.reshape(L, d * N_GROUPS * B_WIDTH), kr.reshape(L, d * B_WIDTH), kr.reshape(L, d * B_WIDTH),
      kv.reshape(L, d * 2 * B_WIDTH), kv.reshape(L, d * 2 * B_WIDTH))
    return o.reshape(T, B_WIDTH), lse.reshape(T, B_WIDTH)


def _dil_combine_fwd(os_, lses, *, tt, name):
    T = os_[0].shape[0]

    def body(o0, o1, o2, l0, l1, l2, y_ref, lse_ref):
        a, b, c = l0[...], l1[...], l2[...]
        m = jnp.maximum(jnp.maximum(a, b), c)
        wa, wb, wc = jnp.exp(a - m), jnp.exp(b - m), jnp.exp(c - m)
        den = wa + wb + wc
        y_ref[...] = (wa * o0[...] + wb * o1[...] + wc * o2[...]) / den
        lse_ref[...] = m + jnp.log(den)

    blk = pl.BlockSpec((tt, B_WIDTH), lambda i: (i, 0))
    sh = jax.ShapeDtypeStruct((T, B_WIDTH), F32)
    return pl.pallas_call(
        body, grid=(T // tt,), in_specs=[blk] * 6, out_specs=[blk, blk], out_shape=[sh, sh],
        compiler_params=_cp("parallel"), name=name)(*os_, *lses)


DILS_UNROLL = 4


def _dils_specs(gi, d, nblk):
    blk = lambda f: pl.BlockSpec((SPAN * d, HEAD_DIM), f)
    return {
        "q": blk(lambda h, n: (n, gi * B_HEADS + h)), "q_next": blk(lambda h, n: (jnp.minimum(n + 1, nblk - 1), gi * B_HEADS + h)),
        "cur": blk(lambda h, n: (n, h)), "prev": blk(lambda h, n: (jnp.maximum(n - 1, 0), h)),
        "next": blk(lambda h, n: (jnp.minimum(n + 1, nblk - 1), h)),
        "v": blk(lambda h, n: (n, B_HEADS + h)), "v_prev": blk(lambda h, n: (jnp.maximum(n - 1, 0), B_HEADS + h)),
    }


def _dils_fwd(qr, kr, kv, gi, d, *, name):
    T = qr.shape[0]
    nblk = T // (SPAN * d)
    sp = _dils_specs(gi, d, nblk)

    def body(q_ref, kc_ref, vc_ref, o_ref, lse_ref, k_before, v_before):
        @pl.when(pl.program_id(1) == 0)
        def _():
            k_before[...] = jnp.zeros_like(k_before)
            v_before[...] = jnp.zeros_like(v_before)

        cur_ok, prev_band = _band_masks()
        prev_ok = prev_band & (pl.program_id(1) > 0)

        def residue(r, carry):
            rows = pl.ds(r, SPAN, stride=d)
            q, kc, vc = q_ref[rows, :], kc_ref[rows, :].astype(BF16), vc_ref[rows, :].astype(BF16)
            sc = jnp.where(cur_ok, _dot_nt(q, kc) * ATT_SCALE, NEG)
            sp_ = jnp.where(prev_ok, _dot_nt(q, k_before[r]) * ATT_SCALE, NEG)
            m = jnp.maximum(jnp.max(sc, axis=-1, keepdims=True), jnp.max(sp_, axis=-1, keepdims=True))
            pc = jnp.exp(sc - m)
            pp = jnp.exp(sp_ - m)
            l = jnp.sum(pc, axis=-1, keepdims=True) + jnp.sum(pp, axis=-1, keepdims=True)
            o_ref[rows, :] = (_dot(pc, vc) + _dot(pp, v_before[r])) / l
            lse_ref[rows, :] = jnp.broadcast_to(m + jnp.log(l), (SPAN, HEAD_DIM))
            k_before[r] = kc
            v_before[r] = vc
            return carry

        lax.fori_loop(0, d, residue, 0, unroll=DILS_UNROLL)

    sh = jax.ShapeDtypeStruct((T, B_WIDTH), F32)
    return pl.pallas_call(
        body, grid=(B_HEADS, nblk), in_specs=[sp["q"], sp["cur"], sp["v"]],
        out_specs=[sp["cur"], sp["cur"]], out_shape=[sh, sh],
        scratch_shapes=[pltpu.VMEM((d, SPAN, HEAD_DIM), BF16), pltpu.VMEM((d, SPAN, HEAD_DIM), BF16)],
        compiler_params=_cp("parallel", "arbitrary"), name=name)(qr, kr, kv)


DIL_BWD_GROUP = {1: 4, 4: 1, 16: 1}


def _dil_bwd(qr, kr, kv, dmix, lse, dd, gi, d, *, name, dep=None):
    T = qr.shape[0]
    G = DIL_BWD_GROUP[d]
    band = SPAN * d
    tb = G * band
    nblk = T // tb
    n_units = T // SPAN

    keep = G == 1

    def kernel_body(q_ref, dy_ref, lse_ref, dd_ref, kc_ref, vc_ref, *rest):
        if keep:
            dq_ref, dk_ref, dv_ref, dk_acc, dv_acc, k_before, v_before = rest
        else:
            kp_ref, vp_ref, dq_ref, dk_ref, dv_ref, dk_acc, dv_acc = rest
        n = pl.program_id(1)

        @pl.when(n == 0)
        def _():
            dk_acc[...] = jnp.zeros_like(dk_acc)
            dv_acc[...] = jnp.zeros_like(dv_acc)
            if keep:
                k_before[...] = jnp.zeros_like(k_before)
                v_before[...] = jnp.zeros_like(v_before)

        cur_ok, prev_band = _band_masks()
        for j in range(G):
            def residue(r, carry, j=j):
                off = j * band + r
                rows = pl.ds(off, SPAN, stride=d)
                q, dy = q_ref[rows, :], dy_ref[rows, :]
                lse_h = jnp.max(lse_ref[rows, :], axis=-1, keepdims=True)
                dd_h = jnp.max(dd_ref[rows, :], axis=-1, keepdims=True)
                kc, vc = kc_ref[rows, :].astype(BF16), vc_ref[rows, :].astype(BF16)
                if j > 0:
                    before = pl.ds(off - band, SPAN, stride=d)
                    kp, vp = kc_ref[before, :], vc_ref[before, :]
                    prev_ok = prev_band
                elif keep:
                    kp, vp = k_before[r], v_before[r]
                    k_before[r] = kc
                    v_before[r] = vc
                    prev_ok = prev_band & (n > 0)
                else:
                    before = pl.ds((G - 1) * band + r, SPAN, stride=d)
                    kp, vp = kp_ref[before, :], vp_ref[before, :]
                    prev_ok = prev_band & (n > 0)
                pc = jnp.exp(jnp.where(cur_ok, _dot_nt(q, kc) * ATT_SCALE, NEG) - lse_h)
                pp = jnp.exp(jnp.where(prev_ok, _dot_nt(q, kp) * ATT_SCALE, NEG) - lse_h)
                dsc = pc * (_dot_nt(dy, vc) - dd_h) * ATT_SCALE
                dsp = pp * (_dot_nt(dy, vp) - dd_h) * ATT_SCALE
                dq_ref[rows, :] = _dot(dsc, kc) + _dot(dsp, kp)
                u = (n * G + j) * d + r
                here = pl.ds(pl.multiple_of(u * SPAN, SPAN), SPAN)
                dk_acc[here, :] += _dot_tn(dsc, q)
                dv_acc[here, :] += _dot_tn(pc, dy)
                there = pl.ds(pl.multiple_of(jnp.maximum(u - d, 0) * SPAN, SPAN), SPAN)
                dk_acc[there, :] += _dot_tn(dsp, q)
                dv_acc[there, :] += _dot_tn(pp, dy)
                return carry

            lax.fori_loop(0, d, residue, 0, unroll=min(d, DILS_UNROLL))

        @pl.when(n == nblk - 1)
        def _():
            def place(u, carry):
                rows = pl.ds((u // d) * band + u % d, SPAN, stride=d)
                src = pl.ds(pl.multiple_of(u * SPAN, SPAN), SPAN)
                dk_ref[rows, :] = dk_acc[src, :]
                dv_ref[rows, :] = dv_acc[src, :]
                return carry

            lax.fori_loop(0, n_units, place, 0)

    blk = lambda f: pl.BlockSpec((tb, HEAD_DIM), f)
    cur = lambda h, n: (n, h)
    prev = lambda h, n: (jnp.maximum(n - 1, 0), h)
    whole = pl.BlockSpec((T, HEAD_DIM), lambda h, n: (0, h))
    sh = jax.ShapeDtypeStruct((T, B_WIDTH), F32)
    v_cur = blk(lambda h, n: (n, B_HEADS + h))
    if keep:
        kv_specs, kv_args = [blk(cur), v_cur], [kr, kv]
        kept = [pltpu.VMEM((d, SPAN, HEAD_DIM), BF16), pltpu.VMEM((d, SPAN, HEAD_DIM), BF16)]
    else:
        kv_specs = [blk(cur), v_cur, blk(prev), blk(lambda h, n: (jnp.maximum(n - 1, 0), B_HEADS + h))]
        kv_args, kept = [kr, kv, kr, kv], []
    body, dep_specs, dep_args = _dep(kernel_body, 4 + len(kv_args), dep)
    return pl.pallas_call(
        body, grid=(B_HEADS, nblk),
        in_specs=[blk(lambda h, n: (n, gi * B_HEADS + h)), blk(cur), blk(cur), blk(cur)] + kv_specs + dep_specs,
        out_specs=[blk(cur), whole, whole], out_shape=[sh, sh, sh],
        scratch_shapes=[pltpu.VMEM((T, HEAD_DIM), F32), pltpu.VMEM((T, HEAD_DIM), F32)] + kept,
        compiler_params=_cp("parallel", "arbitrary"), name=name)(qr, dmix, lse, dd, *kv_args, *dep_args)


A_MQ_COL = 4 * A_WIDTH // MEM_WIDTH
B_MQ_COL = N_GROUPS * B_WIDTH // MEM_WIDTH


def _row(v):
    return v.reshape(1, -1).astype(F32)


def _local_step(x, mem, tgt, get_w, P, put_g, first_dep=None, forward_point=lambda i, value: value):
    T = x.shape[0]
    cosf, sinsg = _rope_tables(T)
    lb_soft = jax.nn.softmax(P["a_lb_logits"].astype(F32), axis=0)
    lb = lb_soft[0:1]
    qw_heads = jnp.repeat(P["b_qnorm"][0], B_HEADS, axis=0).reshape(1, -1)
    kw_heads = jnp.tile(_row(P["b_knorm"]), (1, B_HEADS))
    mqw = [jnp.tile(_row(P["mem_qnorm"][l]), (1, MEM_HEADS)) for l in range(2)]
    mkw = [jnp.tile(_row(P["mem_knorm"][l]), (1, MEM_HEADS)) for l in range(2)]
    nmix = [_row(P["norm_mix"][l]) for l in range(2)]
    nffn = [_row(P["norm_ffn"][l]) for l in range(2)]
    mnorm = [_row(P["mem_norm"][l]) for l in range(2)]
    kvn = _row(P["kv_norm"])
    onorm = _row(P["a_onorm"])
    W = {}

    def w_of(name, after=None):
        if name not in W:
            W[name] = get_w(name, after)
        return W[name]

    proj_a, xn0 = _rms_matmul(x, nmix[0], w_of("a_w_in"), tt=512, tn=1664, wt=True, name="proj_a", dep=first_dep)
    mkv0, mn0 = _rms_matmul(mem, mnorm[0], w_of("w_mem_kv0"), tt=MEM_TOKENS, tn=2 * MEM_WIDTH, wt=False, name="mem_kv0")
    o_raw, st = _hgrn2_fwd(proj_a, lb, name="hgrn2_fwd")
    o_raw = forward_point(0, o_raw)
    mm0 = _a_post_fwd(o_raw, proj_a, onorm, tt=512, name="a_post_fwd")
    mo0 = _mem_attn_fwd(proj_a, A_MQ_COL, mkv0, mqw[0], mkw[0], tt=512, name="mem_attn_fwd0")
    hm0 = _mm_res(x, mm0, mo0, w_of("w_out0", mo0), tt=512, name="out_proj0")
    hm0 = forward_point(1, hm0)
    gu0, hn0 = _rms_matmul(hm0, nffn[0], w_of("w_gate_up0", hm0), tt=512, tn=1408, wt=True, out_dtype=BF16, name="gate_up0")
    h1 = _swiglu_down(hm0, gu0, w_of("w_down0", gu0), tt=512, name="down0")
    h1 = forward_point(2, h1)
    kv, hkn, kr = _rms_matmul(h1, kvn, w_of("w_kv", h1), tt=512, tn=768, wt=True, name="kv_proj",
                              rotate=(kw_heads, cosf, sinsg))

    proj_b, xn1, qr = _rms_matmul(h1, nmix[1], w_of("b_w_in", kr), tt=512, tn=1280, wt=True, name="proj_b",
                                  rotate=(qw_heads, cosf, sinsg))
    proj_b = forward_point(3, proj_b)
    mkv1, mn1 = _rms_matmul(mem, mnorm[1], w_of("w_mem_kv1", kr), tt=MEM_TOKENS, tn=2 * MEM_WIDTH, wt=False, name="mem_kv1")
    outs = [(_dil_fwd if d == 1 else _dils_fwd)(qr, kr, kv, gi, d, name=f"dil_fwd{gi}") for gi, d in enumerate(DILATIONS)]
    mm1, lse_tot = _dil_combine_fwd([o for o, _ in outs], [s for _, s in outs], tt=512, name="dil_combine")
    mo1 = _mem_attn_fwd(proj_b, B_MQ_COL, mkv1, mqw[1], mkw[1], tt=512, name="mem_attn_fwd1")
    hm1 = _mm_res(h1, mm1, mo1, w_of("w_out1", mo1), tt=512, name="out_proj1")
    gu1, hn1 = _rms_matmul(hm1, nffn[1], w_of("w_gate_up1", hm1), tt=512, tn=1408, wt=True, out_dtype=BF16, name="gate_up1")
    dy, sq = _swiglu_down_loss(hm1, gu1, w_of("w_down1", gu1), tgt, tt=512, name="down1_loss")

    gP = {}
    zeros_mem = jnp.zeros((MEM_TOKENS, D_MODEL), F32)

    def ffn_bwd(l, dh, hm, gu, hn):
        dgu, g_wd = _swiglu_bwd(dh, gu, w_of(f"w_down{l}"), tt=256, name=f"swiglu_bwd{l}")
        g_wgu = _mm_tn(dgu, hn, tt=512, tka=1408, name=f"g_w_gate_up{l}")
        sent = put_g({f"w_down{l}": g_wd, f"w_gate_up{l}": g_wgu})
        dhm, g_nf = _rms_bwd_dx(hm, nffn[l], w_of(f"w_gate_up{l}"), dgu, dh, tt=512, wt=True, name=f"gate_up_bwd{l}", dep=sent)
        return dhm, g_nf

    def mix_bwd(l, dhm, mix_main, mix_mem, proj, qcol, mkv, mn):
        dmix, g_wout, *head_dots = _out_proj_bwd(dhm, mix_main, mix_mem, w_of(f"w_out{l}"), tt=512, name=f"out_proj_bwd{l}",
                                                 head_dots=l == 1)
        dmq, dmkv, dqw, dkw = _mem_attn_bwd(proj, qcol, mkv, mqw[l], mkw[l], dmix, tt=512, name=f"mem_attn_bwd{l}")
        g_wmkv = _mm_tn(mn, dmkv, tt=MEM_TOKENS, tka=512, name=f"g_w_mem_kv{l}")
        sent = put_g({f"w_out{l}": g_wout, f"w_mem_kv{l}": g_wmkv})
        _, g_mn = _rms_bwd_dx(mem, mnorm[l], w_of(f"w_mem_kv{l}"), dmkv, zeros_mem, tt=MEM_TOKENS, wt=False, name=f"mem_kv_bwd{l}")
        fold = lambda v: v.reshape(MEM_HEADS, MEM_HEAD_DIM).sum(axis=0)
        return dmix, dmq, g_mn, fold(dqw), fold(dkw), sent, head_dots

    dhm1, g_nf1 = ffn_bwd(1, dy, hm1, gu1, hn1)
    dmix1, dmq1, g_mn1, g_mq1, g_mk1, sent, (dd,) = mix_bwd(1, dhm1, mm1, mo1, proj_b, B_MQ_COL, mkv1, mn1)
    dqs, dks, dvs = [], [], []
    for gi, d in enumerate(DILATIONS):
        dq_g, dk_g, dv_g = _dil_bwd(qr, kr, kv, dmix1, lse_tot, dd, gi, d, name=f"dil_bwd{gi}", dep=sent if gi == 0 else None)
        dqs.append(dq_g)
        dks.append(dk_g)
        dvs.append(dv_g)
    dq_raw, dqw = _q_prep_bwd(proj_b, qw_heads, cosf, sinsg, dqs, tt=512, name="q_prep_bwd")
    dkv, dkw = _kv_prep_bwd(kv, kw_heads, cosf, sinsg, dks, dvs, tt=512, name="kv_prep_bwd")
    dproj_b = [dq_raw, dmq1]
    g_wb = _mm_tn_pieces(dproj_b, xn1, tt=512, name="g_b_w_in")
    g_wkv = _mm_tn(dkv, hkn, tt=512, tka=768, name="g_w_kv")
    sent = put_g({"b_w_in": g_wb, "w_kv": g_wkv})
    dh1, g_nm1 = _rms_bwd_dx(h1, nmix[1], w_of("b_w_in"), dproj_b, dhm1, tt=512, wt=True, name="proj_b_bwd", dep=sent)
    dh1, g_kvn = _rms_bwd_dx(h1, kvn, w_of("w_kv"), dkv, dh1, tt=512, wt=True, name="kv_proj_bwd")

    dhm0, g_nf0 = ffn_bwd(0, dh1, hm0, gu0, hn0)
    dmix0, dmq0, g_mn0, g_mq0, g_mk0, sent, _ = mix_bwd(0, dhm0, mm0, mo0, proj_a, A_MQ_COL, mkv0, mn0)
    do_raw, dg, g_onorm = _a_post_bwd(o_raw, proj_a, onorm, dmix0, tt=512, name="a_post_bwd", dep=sent)
    dq, dz, dv, dlb = _hgrn2_bwd(proj_a, lb, st, do_raw, name="hgrn2_bwd")
    dproj_a = [dq, dz, dv, dg, dmq0]
    sent = put_g({"a_w_in": _mm_tn_pieces(dproj_a, xn0, tt=512, name="g_a_w_in")})
    gx, g_nm0 = _rms_bwd_dx(x, nmix[0], w_of("a_w_in"), dproj_a, dhm0, tt=512, wt=True, name="proj_a_bwd", dep=sent)

    dl0 = lb_soft[0:1] * lb_soft[1:2] * dlb
    gP["a_lb_logits"] = jnp.concatenate([dl0, -dl0], axis=0)
    gP["a_onorm"] = g_onorm
    gP["norm_mix"] = jnp.concatenate([g_nm0, g_nm1], axis=0)
    gP["norm_ffn"] = jnp.concatenate([g_nf0, g_nf1], axis=0)
    gP["b_qnorm"] = dqw.reshape(N_GROUPS, B_HEADS, HEAD_DIM).sum(axis=1)[None]
    gP["kv_norm"] = g_kvn.reshape(-1)
    gP["b_knorm"] = dkw.reshape(B_HEADS, HEAD_DIM).sum(axis=0)
    gP["mem_norm"] = jnp.concatenate([g_mn0, g_mn1], axis=0)
    gP["mem_qnorm"] = jnp.stack([g_mq0, g_mq1])
    gP["mem_knorm"] = jnp.stack([g_mk0, g_mk1])
    return sq, gx, gP


MESH_ID = pl.DeviceIdType.MESH
HBM_SPEC = pl.BlockSpec(memory_space=pltpu.HBM)


def _position():
    return lax.axis_index("x"), lax.axis_index("y"), lax.axis_index("c")


def _all_gather_direct(block, after, *, name):
    def body(x_ref, after_ref, out_ref, send_sems, recv_sems, local_sem):
        x, y, c = _position()
        me = 4 * x + 2 * y + c
        mine = pltpu.make_async_copy(x_ref, out_ref.at[me], local_sem)
        mine.start()
        copies = []
        for k in ALL_PEERS:
            cp = pltpu.make_async_remote_copy(
                src_ref=x_ref, dst_ref=out_ref.at[me], send_sem=send_sems.at[k - 1], recv_sem=recv_sems.at[k - 1],
                device_id=_peer(k, x, y, c), device_id_type=MESH_ID)
            cp.start()
            copies.append(cp)
        for cp in copies:
            cp.wait()
        mine.wait()

    return pl.pallas_call(
        body, out_shape=jax.ShapeDtypeStruct((N_DEV,) + block.shape, block.dtype),
        in_specs=[HBM_SPEC, pl.BlockSpec(memory_space=pl.ANY)], out_specs=HBM_SPEC,
        scratch_shapes=[pltpu.SemaphoreType.DMA((7,)), pltpu.SemaphoreType.DMA((7,)), pltpu.SemaphoreType.DMA],
        name=name)(block, after)


SEM_SPEC = pl.BlockSpec(memory_space=pltpu.SEMAPHORE)
ANY_SPEC = pl.BlockSpec(memory_space=pl.ANY)
DATAFLOW = pltpu.SideEffectType.DATAFLOW_SIDE_EFFECTING


def _peer(k, x, y, c):
    return (1 - x if (k >> 2) & 1 else x, 1 - y if (k >> 1) & 1 else y, 1 - c if k & 1 else c)


def _own_slot_filled(own_block):
    x, y, c = _position()
    zone = lax.empty((N_DEV,) + own_block.shape, own_block.dtype)
    return lax.dynamic_update_slice_in_dim(zone, own_block[None], 4 * x + 2 * y + c, axis=0)


ALL_PEERS = tuple(range(1, N_DEV))
SIBLING_AND_SAME_CORE = (1, 2, 4, 6)
SAME_CORE = (2, 4, 6)


def _split_start(srcs, scatter, after, *, name, relations=ALL_PEERS, carried=None):
    n = len(srcs)
    extra = ([] if after is None else [after]) + ([] if carried is None else [carried])
    n_carried = 0 if carried is None else 1
    x, y, c = _position()
    me = 4 * x + 2 * y + c
    lands = [_own_slot_filled(lax.dynamic_index_in_dim(s, me, 0, keepdims=False) if scatter else s) for s in srcs]

    def body(*refs):
        src_refs, land_refs = refs[:n], refs[n:2 * n]
        send_sems, recv_sems = refs[2 * n + len(extra)], refs[2 * n + len(extra) + 1]
        token = refs[2 * n + len(extra) + 2 + 2 * n]
        bx, by, bc = _position()
        bme = 4 * bx + 2 * by + bc
        for a in range(n):
            for k in relations:
                tx, ty, tc = _peer(k, bx, by, bc)
                src = src_refs[a].at[4 * tx + 2 * ty + tc] if scatter else src_refs[a]
                pltpu.make_async_remote_copy(
                    src_ref=src, dst_ref=land_refs[a].at[bme],
                    send_sem=send_sems.at[7 * a + k - 1], recv_sem=recv_sems.at[7 * a + k - 1],
                    device_id=(tx, ty, tc), device_id_type=MESH_ID).start()
        token[...] = jnp.zeros_like(token)

    hbm = lambda a: pltpu.HBM(a.shape, a.dtype)
    outs = pl.pallas_call(
        body, name=name,
        out_shape=(pltpu.SemaphoreType.DMA((7 * n,)), pltpu.SemaphoreType.DMA((7 * n,)),
                   *[hbm(s) for s in srcs], *[hbm(l) for l in lands], jax.ShapeDtypeStruct((8, 128), F32),
                   *([hbm(carried)] if n_carried else [])),
        in_specs=[HBM_SPEC] * (2 * n) + [ANY_SPEC] * len(extra),
        out_specs=(SEM_SPEC, SEM_SPEC, *[HBM_SPEC] * (2 * n), pl.BlockSpec(memory_space=pltpu.VMEM), *([ANY_SPEC] * n_carried)),
        input_output_aliases={**{i: 2 + i for i in range(2 * n)},
                              **({2 * n + len(extra) - 1: 2 * n + 3} if n_carried else {})},
        compiler_params=pltpu.CompilerParams(has_side_effects=DATAFLOW),
    )(*[pltpu.with_memory_space_constraint(s, pltpu.HBM) for s in srcs],
      *[pltpu.with_memory_space_constraint(l, pltpu.HBM) for l in lands], *extra)
    return {"n": n, "relations": relations, "send": outs[0], "recv": outs[1], "srcs": list(outs[2:2 + n]),
            "lands": list(outs[2 + n:2 + 2 * n]), "token": outs[2 * n + 2], "carried": outs[-1] if n_carried else None}


def _forward_start(lands, carried, *, name):
    n = len(lands)

    def body(*refs):
        land_refs = refs[:n]
        send_sems, recv_sems = refs[n + 1], refs[n + 2]
        bx, by, bc = _position()
        for a in range(n):
            for k in SAME_CORE:
                tx, ty, tc = _peer(k, bx, by, bc)
                block = land_refs[a].at[4 * tx + 2 * ty + tc]
                pltpu.make_async_remote_copy(
                    src_ref=block, dst_ref=block,
                    send_sem=send_sems.at[7 * a + k - 1], recv_sem=recv_sems.at[7 * a + k - 1],
                    device_id=(bx, by, 1 - bc), device_id_type=MESH_ID).start()

    hbm = lambda a: pltpu.HBM(a.shape, a.dtype)
    outs = pl.pallas_call(
        body, name=name,
        out_shape=(pltpu.SemaphoreType.DMA((7 * n,)), pltpu.SemaphoreType.DMA((7 * n,)),
                   *[hbm(l) for l in lands], hbm(carried)),
        in_specs=[HBM_SPEC] * n + [ANY_SPEC],
        out_specs=(SEM_SPEC, SEM_SPEC, *[HBM_SPEC] * n, ANY_SPEC),
        input_output_aliases={i: 2 + i for i in range(n + 1)},
        compiler_params=pltpu.CompilerParams(has_side_effects=DATAFLOW),
    )(*lands, carried)
    handle = {"n": n, "relations": SAME_CORE, "send": outs[0], "recv": outs[1], "srcs": [], "lands": list(outs[2:2 + n])}
    return handle, outs[-1]


def _split_wait(handle, after, *, name):
    n, ns = handle["n"], len(handle["srcs"])

    def body(*refs):
        land_refs = refs[ns:ns + n]
        send_sems, recv_sems = refs[ns + n], refs[ns + n + 1]
        bx, by, bc = _position()
        for a in range(n):
            for k in handle["relations"]:
                block = land_refs[a].at[0]
                cp = pltpu.make_async_remote_copy(
                    src_ref=block, dst_ref=block,
                    send_sem=send_sems.at[7 * a + k - 1], recv_sem=recv_sems.at[7 * a + k - 1],
                    device_id=_peer(k, bx, by, bc), device_id_type=MESH_ID)
                cp.wait_send()
                cp.wait_recv()

    hbm = lambda a: pltpu.HBM(a.shape, a.dtype)
    outs = pl.pallas_call(
        body, name=name,
        out_shape=(*[hbm(s) for s in handle["srcs"]], *[hbm(l) for l in handle["lands"]]),
        in_specs=[HBM_SPEC] * (ns + n) + [SEM_SPEC, SEM_SPEC, ANY_SPEC],
        out_specs=tuple([HBM_SPEC] * (ns + n)),
        input_output_aliases={i: i for i in range(ns + n)},
        compiler_params=pltpu.CompilerParams(has_side_effects=DATAFLOW),
    )(*handle["srcs"], *handle["lands"], handle["send"], handle["recv"], after)
    return list(outs[ns:])


def _sum_sources(parts, *, tr, name):
    n, R, C = parts.shape

    def body(p_ref, o_ref):
        acc = p_ref[0].astype(F32)
        for s in range(1, n):
            acc = acc + p_ref[s].astype(F32)
        o_ref[...] = acc

    return pl.pallas_call(
        body, grid=(R // tr,), in_specs=[pl.BlockSpec((n, tr, C), lambda i: (0, i, 0))],
        out_specs=pl.BlockSpec((tr, C), lambda i: (i, 0)),
        out_shape=jax.ShapeDtypeStruct((R, C), F32), compiler_params=_cp("parallel"), name=name)(parts)


def _adamw_math(g, w, m, v):
    c1 = 1.0 - ADAM_B1 ** ADAM_STEP
    c2 = 1.0 - ADAM_B2 ** ADAM_STEP
    nm = ADAM_B1 * m + (1.0 - ADAM_B1) * g
    nv = ADAM_B2 * v + (1.0 - ADAM_B2) * (g * g)
    return -ADAM_LR * ((nm / c1) / (jnp.sqrt(nv / c2) + ADAM_EPS) + ADAM_WD * w), nm, nv


def _reduce_adamw(received, w, m, v, *, tr, name):
    L, R, C = w.shape

    def body(*refs):
        p_refs = refs[:L]
        w_ref, m_ref, v_ref, g_ref, d_ref, nm_ref, nv_ref = refs[L:]
        for l in range(L):
            @pl.when(pl.program_id(0) == l)
            def _(p_ref=p_refs[l]):
                acc = p_ref[0].astype(F32)
                for s in range(1, N_DEV):
                    acc = acc + p_ref[s].astype(F32)
                g_ref[...] = acc
                d_ref[...], nm_ref[...], nv_ref[...] = _adamw_math(acc, w_ref[...], m_ref[...], v_ref[...])

    p_spec = pl.BlockSpec((N_DEV, tr, C), lambda l, i: (0, i, 0))
    blk = pl.BlockSpec((None, tr, C), lambda l, i: (l, i, 0))
    sh = jax.ShapeDtypeStruct((L, R, C), F32)
    return pl.pallas_call(
        body, grid=(L, R // tr), in_specs=[p_spec] * L + [blk] * 3, out_specs=[blk] * 4, out_shape=[sh] * 4,
        compiler_params=_cp("parallel", "parallel"), name=name)(*received, w, m, v)


def _adamw(g, w, m, v, *, tr, name):
    L, R, C = w.shape

    def body(g_ref, w_ref, m_ref, v_ref, d_ref, nm_ref, nv_ref):
        d_ref[...], nm_ref[...], nv_ref[...] = _adamw_math(g_ref[...], w_ref[...], m_ref[...], v_ref[...])

    blk = pl.BlockSpec((None, tr, C), lambda l, i: (l, i, 0))
    sh = jax.ShapeDtypeStruct((L, R, C), F32)
    return pl.pallas_call(
        body, grid=(L, R // tr), in_specs=[blk] * 4, out_specs=[blk] * 3, out_shape=[sh] * 3,
        compiler_params=_cp("parallel", "parallel"), name=name)(g, w, m, v)


UNITS = {
    "a_w_in": ("a_w_in", 0, True), "w_mem_kv0": ("w_mem_kv", 0, False), "w_out0": ("w_out", 0, False),
    "w_gate_up0": ("w_gate_up", 0, True), "w_down0": ("w_down", 0, False), "w_kv": ("w_kv", None, True),
    "b_w_in": ("b_w_in", 0, True), "w_mem_kv1": ("w_mem_kv", 1, False), "w_out1": ("w_out", 1, False),
    "w_gate_up1": ("w_gate_up", 1, True), "w_down1": ("w_down", 1, False),
}
BIG = ("a_w_in", "b_w_in", "w_kv", "w_mem_kv", "w_out", "w_gate_up", "w_down")
ADAMW_ROW_TILE = {"a_w_in": 208, "b_w_in": 160, "w_kv": 192, "w_mem_kv": 128, "w_out": 128, "w_gate_up": 352, "w_down": 352}


def _wire_block(weights, unit):
    name, layer, col = UNITS[unit]
    a = weights[name] if layer is None else weights[name][layer]
    return (a.T if col else a).astype(BF16)


SMALL_REPLICATED = ("norm_mix", "norm_ffn", "b_qnorm", "kv_norm", "b_knorm", "mem_norm", "mem_qnorm", "mem_knorm")
SMALL_SHARDED = ("a_lb_logits", "a_onorm")
SMALL_ORDER = SMALL_REPLICATED + SMALL_SHARDED
LANES = 128


def _prod(shape):
    n = 1
    for s in shape:
        n *= s
    return n


def _pack_flat(arrays, rows, cols, dtype):
    flat = jnp.concatenate([a.reshape(-1).astype(dtype) for a in arrays])
    return jnp.pad(flat, (0, rows * cols - flat.shape[0])).reshape(rows, cols)


def _unpack_flat(packed, shapes):
    flat = packed.reshape(-1)
    out, off = [], 0
    for s in shapes:
        out.append(flat[off:off + _prod(s)].reshape(s))
        off += _prod(s)
    return out


def kernel(x, mem, norm_mix, norm_ffn, a_w_in, a_lb_logits, a_onorm, b_w_in, b_qnorm, kv_norm, w_kv, b_knorm, mem_norm, w_mem_kv, mem_qnorm, mem_knorm, w_out, w_gate_up, w_down, loss_target, m_norm_mix, m_norm_ffn, m_a_w_in, m_a_lb_logits, m_a_onorm, m_b_w_in, m_b_qnorm, m_kv_norm, m_w_kv, m_b_knorm, m_mem_norm, m_w_mem_kv, m_mem_qnorm, m_mem_knorm, m_w_out, m_w_gate_up, m_w_down, v_norm_mix, v_norm_ffn, v_a_w_in, v_a_lb_logits, v_a_onorm, v_b_w_in, v_b_qnorm, v_kv_norm, v_w_kv, v_b_knorm, v_mem_norm, v_w_mem_kv, v_mem_qnorm, v_mem_knorm, v_w_out, v_w_gate_up, v_w_down):
    names = ("norm_mix", "norm_ffn", "a_w_in", "a_lb_logits", "a_onorm", "b_w_in", "b_qnorm", "kv_norm", "w_kv", "b_knorm",
             "mem_norm", "w_mem_kv", "mem_qnorm", "mem_knorm", "w_out", "w_gate_up", "w_down")
    w = dict(zip(names, (norm_mix, norm_ffn, a_w_in, a_lb_logits, a_onorm, b_w_in, b_qnorm, kv_norm, w_kv, b_knorm,
                         mem_norm, w_mem_kv, mem_qnorm, mem_knorm, w_out, w_gate_up, w_down)))
    m = dict(zip(names, (m_norm_mix, m_norm_ffn, m_a_w_in, m_a_lb_logits, m_a_onorm, m_b_w_in, m_b_qnorm, m_kv_norm, m_w_kv,
                         m_b_knorm, m_mem_norm, m_w_mem_kv, m_mem_qnorm, m_mem_knorm, m_w_out, m_w_gate_up, m_w_down)))
    v = dict(zip(names, (v_norm_mix, v_norm_ffn, v_a_w_in, v_a_lb_logits, v_a_onorm, v_b_w_in, v_b_qnorm, v_kv_norm, v_w_kv,
                         v_b_knorm, v_mem_norm, v_w_mem_kv, v_mem_qnorm, v_mem_knorm, v_w_out, v_w_gate_up, v_w_down)))

    first = ["a_w_in", "w_mem_kv0"]
    later = [["w_out0", "w_gate_up0"], ["w_down0", "w_kv"], ["b_w_in", "w_mem_kv1"], ["w_out1", "w_gate_up1", "w_down1"]]
    first_half, second_half = {}, {}

    def start_first_half(i, after, carried=None):
        first_half[i] = _split_start([_wire_block(w, u) for u in later[i]], False, after, name=f"gather{i}_start",
                                     relations=SIBLING_AND_SAME_CORE, carried=carried)
        return first_half[i]

    opening = _split_start([_wire_block(w, u) for u in first] + [_pack_flat([a_lb_logits, a_onorm], 8, LANES, F32)],
                           False, None, name="gather_first_start", relations=SIBLING_AND_SAME_CORE)
    token = start_first_half(0, opening["token"])["token"]
    token = start_first_half(1, token)["token"]
    opening, token = _forward_start(_split_wait(opening, token, name="gather_first_landed"), token, name="gather_first_forward")
    gathered = _split_wait(opening, token, name="gather_first_wait")
    full = {u: g.reshape(-1, g.shape[-1]) for u, g in zip(first, gathered)}
    small_in = gathered[-1].reshape(N_DEV, -1)
    P = {n: w[n] for n in SMALL_REPLICATED}
    P["a_lb_logits"] = small_in[:, :192].reshape(N_DEV, 2, 96).transpose(1, 0, 2).reshape(2, A_WIDTH)
    P["a_onorm"] = small_in[:, 192:288].reshape(1, A_WIDTH)

    def forward_point(i, value):
        landed = _split_wait(first_half[i], value, name=f"gather{i}_landed")
        second_half[i], value = _forward_start(landed, value, name=f"gather{i}_forward")
        if i + 2 < len(later):
            value = start_first_half(i + 2, None, carried=value)["carried"]
        return value

    def get_w(unit, after):
        if unit not in full:
            i = [unit in group for group in later].index(True)
            for u, land in zip(later[i], _split_wait(second_half[i], after, name=f"gather{i}_wait")):
                full[u] = land.reshape(-1, land.shape[-1])
        return full[unit]

    sent = []

    def put_g(group):
        units = list(group)
        handle = _split_start([group[u].reshape(N_DEV, -1, group[u].shape[-1]) for u in units], True, None,
                              name=f"scatter{len(sent)}_start")
        sent.append((units, handle))
        return handle["token"]

    sq, gx, gP = _local_step(x[0], mem[0], loss_target[0], get_w, P, put_g, forward_point=forward_point)
    loss_here = (0.5 * jnp.sum(sq) / D_MODEL).reshape(1)

    received = {}
    group_of = {u: i for i, (units, _) in enumerate(sent) for u in units}
    out = {"grad": {}, "delta": {}, "new_m": {}, "new_v": {}}
    newest = [gx]

    def update_big(n):
        shape = w[n].shape
        as3 = lambda a: a.reshape((-1,) + shape[-2:])
        mine = [u for u, (wn, _, _) in UNITS.items() if wn == n]
        for i in sorted({group_of[u] for u in mine}):
            if sent[i][0][0] not in received:
                received.update(zip(sent[i][0], _split_wait(sent[i][1], newest[0], name=f"scatter{i}_wait")))
        flip = (lambda a: jnp.swapaxes(a, 1, 2)) if UNITS[mine[0]][2] else (lambda a: a)
        res = _reduce_adamw([received[u] for u in mine], flip(as3(w[n])), flip(as3(m[n])), flip(as3(v[n])),
                            tr=ADAMW_ROW_TILE[n], name=f"adamw_{n}")
        newest[0] = res[1]
        for kind, r in zip(("grad", "delta", "new_m", "new_v"), res):
            out[kind][n] = flip(r).reshape(shape)

    for n in ("w_down", "w_gate_up", "w_out", "w_mem_kv", "b_w_in", "w_kv"):
        update_big(n)

    full_shapes = [(2, A_WIDTH) if n == "a_lb_logits" else (1, A_WIDTH) if n == "a_onorm" else w[n].shape for n in SMALL_ORDER]
    n_small = sum(_prod(s) for s in full_shapes) + 1
    rows_small = -(-n_small // (8 * LANES)) * 8
    g_all = _all_gather_direct(_pack_flat([gP[n] for n in SMALL_ORDER] + [loss_here], rows_small, LANES, F32),
                               newest[0], name="gather_small_grads")
    summed = _unpack_flat(_sum_sources(g_all, tr=rows_small, name="sum_small_grads"), full_shapes + [(1,)])
    g_small = dict(zip(SMALL_ORDER, summed))
    loss = summed[-1].reshape(())
    me = 4 * lax.axis_index("x") + 2 * lax.axis_index("y") + lax.axis_index("c")
    for n in SMALL_SHARDED:
        g_small[n] = lax.dynamic_slice_in_dim(g_small[n], me * 96, 96, axis=1)
    shapes = [w[n].shape for n in SMALL_ORDER]
    rows_upd = -(-sum(_prod(s) for s in shapes) // (8 * LANES)) * 8
    pk = lambda d: _pack_flat([d[n] for n in SMALL_ORDER], rows_upd, LANES, F32)
    res = _adamw(pk(g_small)[None], pk(w)[None], pk(m)[None], pk(v)[None], tr=rows_upd, name="adamw_small")
    out["grad"].update(g_small)
    for kind, packed in zip(("delta", "new_m", "new_v"), res):
        out[kind].update(zip(SMALL_ORDER, _unpack_flat(packed[0], shapes)))
    newest[0] = res[0]
    update_big("a_w_in")

    return (loss, gx[None], *[out["grad"][n] for n in names], *[out["delta"][n] for n in names],
            *[out["new_m"][n] for n in names], *[out["new_v"][n] for n in names])
```

```python
import functools

import jax
import jax.numpy as jnp
import numpy as np
from jax import lax
from jax.experimental import pallas as pl
from jax.experimental.pallas import tpu as pltpu

F32 = jnp.float32
BF16 = jnp.bfloat16

N_DEV = 8
D_MODEL = 1024
HEAD_DIM = 128
A_HEADS = 6
A_WIDTH = A_HEADS * HEAD_DIM
CHUNK = 64
B_HEADS = 6
B_WIDTH = B_HEADS * HEAD_DIM
DILATIONS = (1, 4, 16)
SPAN = 128
N_GROUPS = 3
ROPE_THETA = 10000.0
MEM_TOKENS = 256
MEM_HEADS = 4
MEM_HEAD_DIM = 64
MEM_WIDTH = MEM_HEADS * MEM_HEAD_DIM
FFN_HIDDEN = 2816
EPS = 1e-6

ADAM_LR = 0.001
ADAM_B1 = 0.9
ADAM_B2 = 0.999
ADAM_EPS = 1e-08
ADAM_WD = 0.01
ADAM_STEP = 10

V7X_VMEM_LIMIT_BYTES = 56 * 1024 * 1024

NT_DIMS = (((1,), (1,)), ((), ()))
TN_DIMS = (((0,), (0,)), ((), ()))


def _cp(*sem):
    return pltpu.CompilerParams(dimension_semantics=sem, vmem_limit_bytes=V7X_VMEM_LIMIT_BYTES)


def _dot(a, b):
    return jnp.dot(a.astype(BF16), b.astype(BF16), preferred_element_type=F32)


def _dot_nt(a, b):
    return lax.dot_general(a.astype(BF16), b.astype(BF16), NT_DIMS, preferred_element_type=F32)


def _dot_tn(a, b):
    return lax.dot_general(a.astype(BF16), b.astype(BF16), TN_DIMS, preferred_element_type=F32)


def _dot3(m01, x):
    hi = x.astype(BF16)
    r1 = x - hi.astype(F32)
    mid = r1.astype(BF16)
    lo = (r1 - mid.astype(F32)).astype(BF16)
    d = functools.partial(jnp.dot, preferred_element_type=F32)
    return d(m01, hi) + d(m01, mid) + d(m01, lo)


def _sigmoid(x):
    return 0.5 * jnp.tanh(0.5 * x) + 0.5


def _full(shape):
    return pl.BlockSpec(shape, lambda *_: (0,) * len(shape))


def _dep(body, n_in, dep):
    if dep is None:
        return body, [], []

    def with_dep(*refs):
        return body(*refs[:n_in], *refs[n_in + 1:])

    return with_dep, [pl.BlockSpec(memory_space=pl.ANY)], [dep]


def _rms_matmul(x, g, w, *, tt, tn, wt, name, out_dtype=F32, dep=None, rotate=None):
    T, K = x.shape
    N = w.shape[0] if wt else w.shape[1]
    n_rot = 0 if rotate is None else rotate[0].shape[1] // HEAD_DIM
    extra_in = [] if rotate is None else list(rotate)

    def kernel_body(x_ref, g_ref, w_ref, *rest):
        y_ref, xn_ref = rest[len(extra_in)], rest[len(extra_in) + 1]
        xf = x_ref[...]
        r = lax.rsqrt(jnp.mean(xf * xf, axis=-1, keepdims=True) + EPS)
        xn = (xf * r * g_ref[...]).astype(BF16)
        xn_ref[...] = xn
        for j in range(N // tn):
            cols = slice(j * tn, (j + 1) * tn)
            y = _dot_nt(xn, w_ref[cols, :]) if wt else _dot(xn, w_ref[:, cols])
            y_ref[:, cols] = y.astype(out_dtype)
            for h in range(j * tn // HEAD_DIM, min((j + 1) * tn // HEAD_DIM, n_rot)):
                gw_ref, c_ref, s_ref, yr_ref = rest[0], rest[1], rest[2], rest[len(extra_in) + 2]
                sl = slice(h * HEAD_DIM, (h + 1) * HEAD_DIM)
                xhat, _ = _head_rms(y[:, h * HEAD_DIM - j * tn:(h + 1) * HEAD_DIM - j * tn])
                yr_ref[:, sl] = _rope(xhat * gw_ref[:, sl], c_ref[...], s_ref[...])

    tbl = pl.BlockSpec((tt, HEAD_DIM), lambda i: (i, 0))
    rot_specs = [] if rotate is None else [_full((1, n_rot * HEAD_DIM)), tbl, tbl]
    body, dep_specs, dep_args = _dep(kernel_body, 3 + len(extra_in), dep)
    return pl.pallas_call(
        body, grid=(T // tt,),
        in_specs=[pl.BlockSpec((tt, K), lambda i: (i, 0)), _full((1, K)), _full(w.shape)] + rot_specs + dep_specs,
        out_specs=[pl.BlockSpec((tt, N), lambda i: (i, 0)), pl.BlockSpec((tt, K), lambda i: (i, 0))]
        + ([] if rotate is None else [pl.BlockSpec((tt, n_rot * HEAD_DIM), lambda i: (i, 0))]),
        out_shape=[jax.ShapeDtypeStruct((T, N), out_dtype), jax.ShapeDtypeStruct((T, K), BF16)]
        + ([] if rotate is None else [jax.ShapeDtypeStruct((T, n_rot * HEAD_DIM), F32)]),
        compiler_params=_cp("parallel"), name=name)(x, g, w, *extra_in, *dep_args)


def _mm_res(res, a1, a2, w, *, tt, name):
    T, K1 = a1.shape
    K2 = a2.shape[1]
    N = w.shape[1]

    def body(r_ref, a1_ref, a2_ref, w_ref, o_ref):
        o_ref[...] = r_ref[...] + _dot(a1_ref[...], w_ref[:K1, :]) + _dot(a2_ref[...], w_ref[K1:, :])

    return pl.pallas_call(
        body, grid=(T // tt,),
        in_specs=[pl.BlockSpec((tt, N), lambda i: (i, 0)), pl.BlockSpec((tt, K1), lambda i: (i, 0)),
                  pl.BlockSpec((tt, K2), lambda i: (i, 0)), _full((K1 + K2, N))],
        out_specs=pl.BlockSpec((tt, N), lambda i: (i, 0)),
        out_shape=jax.ShapeDtypeStruct((T, N), F32),
        compiler_params=_cp("parallel"), name=name)(res, a1, a2, w)


def _swiglu_down(h, gu, wd, *, tt, name):
    T, D = h.shape
    Fh = wd.shape[0]

    def body(h_ref, gt_ref, up_ref, w_ref, o_ref):
        gt = gt_ref[...].astype(F32)
        act = gt * _sigmoid(gt) * up_ref[...].astype(F32)
        o_ref[...] = h_ref[...] + _dot(act, w_ref[...])

    return pl.pallas_call(
        body, grid=(T // tt,),
        in_specs=[pl.BlockSpec((tt, D), lambda i: (i, 0)), pl.BlockSpec((tt, Fh), lambda i: (i, 0)),
                  pl.BlockSpec((tt, Fh), lambda i: (i, 1)), _full((Fh, D))],
        out_specs=pl.BlockSpec((tt, D), lambda i: (i, 0)),
        out_shape=jax.ShapeDtypeStruct((T, D), F32),
        compiler_params=_cp("parallel"), name=name)(h, gu, gu, wd)


def _swiglu_down_loss(h, gu, wd, tgt, *, tt, name):
    T, D = h.shape
    Fh = wd.shape[0]

    def body(h_ref, gt_ref, up_ref, w_ref, t_ref, dy_ref, acc_ref):
        @pl.when(pl.program_id(0) == 0)
        def _():
            acc_ref[...] = jnp.zeros_like(acc_ref)

        gt = gt_ref[...].astype(F32)
        act = gt * _sigmoid(gt) * up_ref[...].astype(F32)
        e = h_ref[...] + _dot(act, w_ref[...]) - t_ref[...]
        dy_ref[...] = e * (1.0 / D)
        acc_ref[...] += jnp.sum(e * e, axis=0, keepdims=True)

    row = pl.BlockSpec((tt, D), lambda i: (i, 0))
    return pl.pallas_call(
        body, grid=(T // tt,),
        in_specs=[row, pl.BlockSpec((tt, Fh), lambda i: (i, 0)), pl.BlockSpec((tt, Fh), lambda i: (i, 1)), _full((Fh, D)), row],
        out_specs=[row, _full((1, D))],
        out_shape=[jax.ShapeDtypeStruct((T, D), F32), jax.ShapeDtypeStruct((1, D), F32)],
        compiler_params=_cp("arbitrary"), name=name)(h, gu, gu, wd, tgt)


SWIGLU_COLS = 256


def _swiglu_bwd(dh, gu, wd, *, tt, name):
    T, D = dh.shape
    Fh = wd.shape[0]
    last = T // tt - 1

    def body(dh_ref, gt_ref, up_ref, w_ref, dgu_ref, gw_ref, acc):
        @pl.when(pl.program_id(0) == 0)
        def _():
            acc[...] = jnp.zeros_like(acc)

        dh16 = dh_ref[...].astype(BF16)
        for c0 in range(0, Fh, SWIGLU_COLS):
            cols = slice(c0, c0 + SWIGLU_COLS)
            gt = gt_ref[:, cols].astype(F32)
            up = up_ref[:, cols].astype(F32)
            s = _sigmoid(gt)
            silu = gt * s
            dact = _dot_nt(dh16, w_ref[cols, :])
            acc[cols, :] += _dot_tn((silu * up).astype(BF16), dh16)
            dgu_ref[:, cols] = (dact * up * (s * (1.0 + gt * (1.0 - s)))).astype(BF16)
            dgu_ref[:, Fh + c0:Fh + c0 + SWIGLU_COLS] = (dact * silu).astype(BF16)

        @pl.when(pl.program_id(0) == last)
        def _():
            gw_ref[...] = acc[...].astype(BF16)

    return pl.pallas_call(
        body, grid=(T // tt,),
        in_specs=[pl.BlockSpec((tt, D), lambda i: (i, 0)), pl.BlockSpec((tt, Fh), lambda i: (i, 0)),
                  pl.BlockSpec((tt, Fh), lambda i: (i, 1)), _full((Fh, D))],
        out_specs=[pl.BlockSpec((tt, 2 * Fh), lambda i: (i, 0)), _full((Fh, D))],
        out_shape=[jax.ShapeDtypeStruct((T, 2 * Fh), BF16), jax.ShapeDtypeStruct((Fh, D), BF16)],
        scratch_shapes=[pltpu.VMEM((Fh, D), F32)],
        compiler_params=_cp("arbitrary"), name=name)(dh, gu, gu, wd)


def _out_proj_bwd(dy, a1, a2, w, *, tt, name, head_dots=False):
    T, N = dy.shape
    K1, K2 = a1.shape[1], a2.shape[1]
    K = K1 + K2
    last = T // tt - 1

    def body(dy_ref, a1_ref, a2_ref, w_ref, da_ref, gw_ref, *rest):
        acc = rest[-1]

        @pl.when(pl.program_id(0) == 0)
        def _():
            acc[...] = jnp.zeros_like(acc)

        dy16 = dy_ref[...].astype(BF16)
        da = _dot_nt(dy16, w_ref[...])
        da_ref[...] = da
        acc[:K1, :] += _dot_tn(a1_ref[...], dy16)
        acc[K1:, :] += _dot_tn(a2_ref[...], dy16)
        if head_dots:
            for h in range(K1 // HEAD_DIM):
                sl = slice(h * HEAD_DIM, (h + 1) * HEAD_DIM)
                rest[0][:, sl] = jnp.broadcast_to(jnp.sum(da[:, sl] * a1_ref[:, sl], axis=-1, keepdims=True), (tt, HEAD_DIM))

        @pl.when(pl.program_id(0) == last)
        def _():
            gw_ref[...] = acc[...].astype(BF16)

    extra_specs = [pl.BlockSpec((tt, K1), lambda i: (i, 0))] if head_dots else []
    extra_shapes = [jax.ShapeDtypeStruct((T, K1), F32)] if head_dots else []
    return pl.pallas_call(
        body, grid=(T // tt,),
        in_specs=[pl.BlockSpec((tt, N), lambda i: (i, 0)), pl.BlockSpec((tt, K1), lambda i: (i, 0)),
                  pl.BlockSpec((tt, K2), lambda i: (i, 0)), _full((K, N))],
        out_specs=[pl.BlockSpec((tt, K), lambda i: (i, 0)), _full((K, N))] + extra_specs,
        out_shape=[jax.ShapeDtypeStruct((T, K), F32), jax.ShapeDtypeStruct((K, N), BF16)] + extra_shapes,
        scratch_shapes=[pltpu.VMEM((K, N), F32)],
        compiler_params=_cp("arbitrary"), name=name)(dy, a1, a2, w)


def _mm_tn(a, b, *, tt, tka, name):
    T, Ka = a.shape
    N = b.shape[1]
    last = T // tt - 1

    def body(a_ref, b_ref, o_ref, acc):
        @pl.when(pl.program_id(1) == 0)
        def _():
            acc[...] = jnp.zeros_like(acc)

        acc[...] += _dot_tn(a_ref[...], b_ref[...])

        @pl.when(pl.program_id(1) == last)
        def _():
            o_ref[...] = acc[...].astype(BF16)

    return pl.pallas_call(
        body, grid=(Ka // tka, T // tt),
        in_specs=[pl.BlockSpec((tt, tka), lambda j, t: (t, j)), pl.BlockSpec((tt, N), lambda j, t: (t, 0))],
        out_specs=pl.BlockSpec((tka, N), lambda j, t: (j, 0)),
        out_shape=jax.ShapeDtypeStruct((Ka, N), BF16),
        scratch_shapes=[pltpu.VMEM((tka, N), F32)],
        compiler_params=_cp("parallel", "arbitrary"), name=name)(a, b)


def _mm_tn_pieces(pieces, b, *, tt, name):
    n = len(pieces)
    T = b.shape[0]
    N = b.shape[1]
    widths = [p.shape[1] for p in pieces]
    Ka = sum(widths)
    last = T // tt - 1

    def body(*refs):
        p_refs = refs[:n]
        b_ref, o_ref, acc = refs[n:]

        @pl.when(pl.program_id(0) == 0)
        def _():
            acc[...] = jnp.zeros_like(acc)

        bv = b_ref[...].astype(BF16)
        off = 0
        for p_ref, wd in zip(p_refs, widths):
            acc[off:off + wd, :] += _dot_tn(p_ref[...], bv)
            off += wd

        @pl.when(pl.program_id(0) == last)
        def _():
            o_ref[...] = acc[...].astype(BF16)

    return pl.pallas_call(
        body, grid=(T // tt,),
        in_specs=[pl.BlockSpec((tt, wd), lambda t: (t, 0)) for wd in widths] + [pl.BlockSpec((tt, N), lambda t: (t, 0))],
        out_specs=_full((Ka, N)), out_shape=jax.ShapeDtypeStruct((Ka, N), BF16),
        scratch_shapes=[pltpu.VMEM((Ka, N), F32)],
        compiler_params=_cp("arbitrary"), name=name)(*pieces, b)


def _rms_bwd_dx(x, g, w, dy, dres, *, tt, wt, name, dep=None):
    pieces = list(dy) if isinstance(dy, (list, tuple)) else [dy]
    n = len(pieces)
    widths = [p.shape[1] for p in pieces]
    T, K = x.shape

    def kernel_body(x_ref, g_ref, w_ref, *rest):
        dy_refs = rest[:n]
        dres_ref, dx_ref, dg_ref = rest[n:]

        @pl.when(pl.program_id(0) == 0)
        def _():
            dg_ref[...] = jnp.zeros_like(dg_ref)

        if n == 1:
            dxn = (_dot if wt else _dot_nt)(dy_refs[0][...], w_ref[...])
        else:
            dxn, off = 0.0, 0
            for dy_ref, wd in zip(dy_refs, widths):
                dxn = dxn + _dot(dy_ref[...], w_ref[off:off + wd, :])
                off += wd
        xf = x_ref[...]
        r = lax.rsqrt(jnp.mean(xf * xf, axis=-1, keepdims=True) + EPS)
        xhat = xf * r
        dg_ref[...] += jnp.sum(dxn * xhat, axis=0, keepdims=True)
        dxhat = dxn * g_ref[...]
        dx_ref[...] = dres_ref[...] + r * (dxhat - xhat * jnp.mean(dxhat * xhat, axis=-1, keepdims=True))

    assert n == 1 or wt
    body, dep_specs, dep_args = _dep(kernel_body, 4 + n, dep)
    return pl.pallas_call(
        body, grid=(T // tt,),
        in_specs=[pl.BlockSpec((tt, K), lambda i: (i, 0)), _full((1, K)), _full(w.shape)]
        + [pl.BlockSpec((tt, wd), lambda i: (i, 0)) for wd in widths]
        + [pl.BlockSpec((tt, K), lambda i: (i, 0))] + dep_specs,
        out_specs=[pl.BlockSpec((tt, K), lambda i: (i, 0)), _full((1, K))],
        out_shape=[jax.ShapeDtypeStruct((T, K), F32), jax.ShapeDtypeStruct((1, K), F32)],
        compiler_params=_cp("arbitrary"), name=name)(x, g, w, *pieces, dres, *dep_args)


HGRN_TB = 512
HGRN_NCH = HGRN_TB // CHUNK
HGRN_HPB = 6


def _hgrn_chunk_fwd(q, z, lbv, tril01):
    sig = _sigmoid(z)
    f = lbv + (1.0 - lbv) * sig
    kk = 1.0 - f
    b = _dot3(tril01, jnp.log(f))
    bend = b[CHUNK - 1:CHUNK, :]
    sq = _sigmoid(q)
    eb = jnp.exp(b)
    emb = jnp.exp(-b)
    eo = jnp.exp(bend - b)
    dec = jnp.exp(bend)
    return sig, f, kk, sq, eb, emb, eo, dec


def _hgrn2_fwd(proj, lb, *, name):
    T = proj.shape[0]
    nT = T // HGRN_TB
    nC = T // CHUNK

    def body(q_ref, z_ref, v_ref, lb_ref, o_ref, st_ref, state):
        @pl.when(pl.program_id(1) == 0)
        def _():
            state[...] = jnp.zeros_like(state)

        row = lax.broadcasted_iota(jnp.int32, (CHUNK, CHUNK), 0)
        col = lax.broadcasted_iota(jnp.int32, (CHUNK, CHUNK), 1)
        causal = row >= col
        tril01 = causal.astype(BF16)

        def chunk(c, carry):
            rows = pl.ds(pl.multiple_of(c * CHUNK, CHUNK), CHUNK)
            for hh in range(HGRN_HPB):
                sl = slice(hh * HEAD_DIM, (hh + 1) * HEAD_DIM)
                q = q_ref[rows, sl]
                v = v_ref[rows, sl].astype(BF16)
                sig, f, kk, sq, eb, emb, eo, dec = _hgrn_chunk_fwd(q, z_ref[rows, sl], lb_ref[:, sl], tril01)
                qi = (q * sq * eb).astype(BF16)
                ki = (kk * emb).astype(BF16)
                ko = (kk * eo).astype(BF16)
                st = state[hh]
                att = jnp.where(causal, _dot_nt(qi, ki), 0.0)
                o_ref[rows, sl] = _dot(att, v) + _dot_nt(qi, st)
                st_ref[c, hh] = st
                state[hh] = st * dec + _dot_tn(v, ko)
            return carry

        lax.fori_loop(0, HGRN_NCH, chunk, 0)

    W = HGRN_HPB * HEAD_DIM
    nG = A_HEADS // HGRN_HPB
    hb = lambda off: pl.BlockSpec((HGRN_TB, W), lambda h, i: (i, off + h))
    return pl.pallas_call(
        body, grid=(nG, nT),
        in_specs=[hb(0), hb(nG), hb(2 * nG), pl.BlockSpec((1, W), lambda h, i: (0, h))],
        out_specs=[hb(0), pl.BlockSpec((HGRN_NCH, HGRN_HPB, HEAD_DIM, HEAD_DIM), lambda h, i: (i, h, 0, 0))],
        out_shape=[jax.ShapeDtypeStruct((T, A_WIDTH), F32), jax.ShapeDtypeStruct((nC, A_HEADS, HEAD_DIM, HEAD_DIM), F32)],
        scratch_shapes=[pltpu.VMEM((HGRN_HPB, HEAD_DIM, HEAD_DIM), F32)],
        compiler_params=_cp("parallel", "arbitrary"), name=name)(proj, proj, proj, lb)


def _hgrn2_bwd(proj, lb, st_all, do, *, name):
    T = proj.shape[0]
    nT = T // HGRN_TB

    def body(q_ref, z_ref, v_ref, lb_ref, st_ref, do_ref, dq_ref, dz_ref, dv_ref, dlb_ref, dstate):
        @pl.when(pl.program_id(1) == 0)
        def _():
            dstate[...] = jnp.zeros_like(dstate)
            dlb_ref[...] = jnp.zeros_like(dlb_ref)

        row = lax.broadcasted_iota(jnp.int32, (CHUNK, CHUNK), 0)
        col = lax.broadcasted_iota(jnp.int32, (CHUNK, CHUNK), 1)
        causal = row >= col
        tril01 = causal.astype(BF16)
        triu01 = (row <= col).astype(BF16)

        def chunk(cc, carry):
            c = HGRN_NCH - 1 - cc
            rows = pl.ds(pl.multiple_of(c * CHUNK, CHUNK), CHUNK)
            for hh in range(HGRN_HPB):
                sl = slice(hh * HEAD_DIM, (hh + 1) * HEAD_DIM)
                lbv = lb_ref[:, sl]
                q = q_ref[rows, sl]
                v = v_ref[rows, sl].astype(BF16)
                sig, f, kk, sq, eb, emb, eo, dec = _hgrn_chunk_fwd(q, z_ref[rows, sl], lbv, tril01)
                qi32 = q * sq * eb
                ki32 = kk * emb
                ko32 = kk * eo
                qi, ki, ko = qi32.astype(BF16), ki32.astype(BF16), ko32.astype(BF16)
                att = jnp.where(causal, _dot_nt(qi, ki), 0.0).astype(BF16)
                dout = do_ref[rows, sl].astype(BF16)
                st = st_ref[c, hh]
                dst = dstate[hh]
                dst16 = dst.astype(BF16)
                datt = jnp.where(causal, _dot_nt(dout, v), 0.0).astype(BF16)
                dqi = _dot(datt, ki) + _dot(dout, st)
                dki = _dot_tn(datt, qi)
                dv_ref[rows, sl] = (_dot_tn(att, dout) + _dot_nt(ko, dst16)).astype(BF16)
                dko = _dot(v, dst16)
                ddec = jnp.sum(dst * st, axis=0, keepdims=True)
                dstate[hh] = dst * dec + _dot_tn(dout, qi)
                dkk = dki * emb + dko * eo
                db = dqi * qi32 - dki * ki32 - dko * ko32
                dbend = jnp.sum(dko * ko32, axis=0, keepdims=True) + ddec * dec
                dlogf = _dot3(triu01, db) + dbend
                df = dlogf / f - dkk
                dz_ref[rows, sl] = (df * (1.0 - lbv) * sig * (1.0 - sig)).astype(BF16)
                dlb_ref[:, sl] += jnp.sum(df * (1.0 - sig), axis=0, keepdims=True)
                dq_ref[rows, sl] = (dqi * eb * (sq * (1.0 + q * (1.0 - sq)))).astype(BF16)
            return carry

        lax.fori_loop(0, HGRN_NCH, chunk, 0)

    W = HGRN_HPB * HEAD_DIM
    nG = A_HEADS // HGRN_HPB
    hb = lambda off: pl.BlockSpec((HGRN_TB, W), lambda h, i: (nT - 1 - i, off + h))
    hlb = pl.BlockSpec((1, W), lambda h, i: (0, h))
    o16 = jax.ShapeDtypeStruct((T, A_WIDTH), BF16)
    return pl.pallas_call(
        body, grid=(nG, nT),
        in_specs=[hb(0), hb(nG), hb(2 * nG), hlb,
                  pl.BlockSpec((HGRN_NCH, HGRN_HPB, HEAD_DIM, HEAD_DIM), lambda h, i: (nT - 1 - i, h, 0, 0)), hb(0)],
        out_specs=[hb(0), hb(0), hb(0), hlb],
        out_shape=[o16, o16, o16, jax.ShapeDtypeStruct((1, A_WIDTH), F32)],
        scratch_shapes=[pltpu.VMEM((HGRN_HPB, HEAD_DIM, HEAD_DIM), F32)],
        compiler_params=_cp("parallel", "arbitrary"), name=name)(proj, proj, proj, lb, st_all, do)


def _head_rms(x):
    r = lax.rsqrt(jnp.mean(x * x, axis=-1, keepdims=True) + EPS)
    return x * r, r


def _head_rms_bwd(dxhat, xhat, r):
    return r * (dxhat - xhat * jnp.mean(dxhat * xhat, axis=-1, keepdims=True))


def _a_post_fwd(o, proj, onorm, *, tt, name):
    T = o.shape[0]

    def body(o_ref, g_ref, w_ref, y_ref):
        for h in range(A_HEADS):
            sl = slice(h * HEAD_DIM, (h + 1) * HEAD_DIM)
            xhat, _ = _head_rms(o_ref[:, sl])
            g = g_ref[:, sl]
            y_ref[:, sl] = xhat * w_ref[:, sl] * (g * _sigmoid(g))

    blk = lambda c: pl.BlockSpec((tt, A_WIDTH), lambda i: (i, c))
    return pl.pallas_call(
        body, grid=(T // tt,), in_specs=[blk(0), blk(3), _full((1, A_WIDTH))], out_specs=blk(0),
        out_shape=jax.ShapeDtypeStruct((T, A_WIDTH), F32),
        compiler_params=_cp("parallel"), name=name)(o, proj, onorm)


def _a_post_bwd(o, proj, onorm, dmix, *, tt, name, dep=None):
    T = o.shape[0]

    def kernel_body(o_ref, g_ref, w_ref, dy_ref, do_ref, dg_ref, dw_ref):
        @pl.when(pl.program_id(0) == 0)
        def _():
            dw_ref[...] = jnp.zeros_like(dw_ref)

        for h in range(A_HEADS):
            sl = slice(h * HEAD_DIM, (h + 1) * HEAD_DIM)
            xhat, r = _head_rms(o_ref[:, sl])
            g = g_ref[:, sl]
            s = _sigmoid(g)
            dy = dy_ref[:, sl]
            w = w_ref[:, sl]
            dg_ref[:, sl] = (dy * xhat * w * (s * (1.0 + g * (1.0 - s)))).astype(BF16)
            dyn = dy * (g * s)
            dw_ref[:, sl] += jnp.sum(dyn * xhat, axis=0, keepdims=True)
            do_ref[:, sl] = _head_rms_bwd(dyn * w, xhat, r)

    blk = lambda c: pl.BlockSpec((tt, A_WIDTH), lambda i: (i, c))
    body, dep_specs, dep_args = _dep(kernel_body, 4, dep)
    return pl.pallas_call(
        body, grid=(T // tt,), in_specs=[blk(0), blk(3), _full((1, A_WIDTH)), blk(0)] + dep_specs,
        out_specs=[blk(0), blk(0), _full((1, A_WIDTH))],
        out_shape=[jax.ShapeDtypeStruct((T, A_WIDTH), F32), jax.ShapeDtypeStruct((T, A_WIDTH), BF16),
                   jax.ShapeDtypeStruct((1, A_WIDTH), F32)],
        compiler_params=_cp("arbitrary"), name=name)(o, proj, onorm, dmix, *dep_args)


def _mem_head_masks(n):
    lane = lax.broadcasted_iota(jnp.int32, (n, MEM_WIDTH), 1)
    return [(lane >= m * MEM_HEAD_DIM) & (lane < (m + 1) * MEM_HEAD_DIM) for m in range(MEM_HEADS)]


def _mem_head_rms(x, masks):
    x2 = x * x
    r = jnp.zeros_like(x)
    for mk in masks:
        ms = jnp.sum(jnp.where(mk, x2, 0.0), axis=-1, keepdims=True) * (1.0 / MEM_HEAD_DIM)
        r = jnp.where(mk, lax.rsqrt(ms + EPS), r)
    return x * r, r


def _mem_head_rms_bwd(dxhat, xhat, r, masks):
    t = dxhat * xhat
    m = jnp.zeros_like(t)
    for mk in masks:
        m = jnp.where(mk, jnp.sum(jnp.where(mk, t, 0.0), axis=-1, keepdims=True) * (1.0 / MEM_HEAD_DIM), m)
    return r * (dxhat - xhat * m)


MEM_SCALE = MEM_HEAD_DIM ** -0.5


def _mem_attn_fwd(proj, qcol, mkv, qn_w, kn_w, *, tt, name):
    T = proj.shape[0]

    def body(q_ref, k_ref, v_ref, qw_ref, kw_ref, o_ref):
        qmasks = _mem_head_masks(tt)
        kmasks = _mem_head_masks(MEM_TOKENS)
        qhat, _ = _mem_head_rms(q_ref[...], qmasks)
        qn = qhat * qw_ref[...]
        khat, _ = _mem_head_rms(k_ref[...], kmasks)
        kn = (khat * kw_ref[...]).astype(BF16)
        v = v_ref[...].astype(BF16)
        out = jnp.zeros((tt, MEM_WIDTH), F32)
        for m in range(MEM_HEADS):
            s = _dot_nt(jnp.where(qmasks[m], qn, 0.0), kn) * MEM_SCALE
            s = s - jnp.max(s, axis=-1, keepdims=True)
            p = jnp.exp(s)
            p = p / jnp.sum(p, axis=-1, keepdims=True)
            out = jnp.where(qmasks[m], _dot(p, v), out)
        o_ref[...] = out

    return pl.pallas_call(
        body, grid=(T // tt,),
        in_specs=[pl.BlockSpec((tt, MEM_WIDTH), lambda i: (i, qcol)), pl.BlockSpec((MEM_TOKENS, MEM_WIDTH), lambda i: (0, 0)),
                  pl.BlockSpec((MEM_TOKENS, MEM_WIDTH), lambda i: (0, 1)), _full((1, MEM_WIDTH)), _full((1, MEM_WIDTH))],
        out_specs=pl.BlockSpec((tt, MEM_WIDTH), lambda i: (i, 0)),
        out_shape=jax.ShapeDtypeStruct((T, MEM_WIDTH), F32),
        compiler_params=_cp("parallel"), name=name)(proj, mkv, mkv, qn_w, kn_w)


def _mem_attn_bwd(proj, qcol, mkv, qn_w, kn_w, dmix, *, tt, name):
    T = proj.shape[0]
    nsteps = T // tt
    ocol = (dmix.shape[1] - MEM_WIDTH) // MEM_WIDTH

    def body(q_ref, k_ref, v_ref, qw_ref, kw_ref, do_ref, dq_ref, dkv_ref, dqw_ref, dkw_ref, dk_acc, dv_acc):
        step = pl.program_id(0)

        @pl.when(step == 0)
        def _():
            dk_acc[...] = jnp.zeros_like(dk_acc)
            dv_acc[...] = jnp.zeros_like(dv_acc)
            dqw_ref[...] = jnp.zeros_like(dqw_ref)

        qmasks = _mem_head_masks(tt)
        kmasks = _mem_head_masks(MEM_TOKENS)
        qhat, qr = _mem_head_rms(q_ref[...], qmasks)
        qn = qhat * qw_ref[...]
        khat, kr = _mem_head_rms(k_ref[...], kmasks)
        kn = (khat * kw_ref[...]).astype(BF16)
        v = v_ref[...].astype(BF16)
        dout = do_ref[...]
        dqn = jnp.zeros((tt, MEM_WIDTH), F32)
        dkn = jnp.zeros((MEM_TOKENS, MEM_WIDTH), F32)
        dvv = jnp.zeros((MEM_TOKENS, MEM_WIDTH), F32)
        for m in range(MEM_HEADS):
            qm = jnp.where(qmasks[m], qn, 0.0).astype(BF16)
            s = _dot_nt(qm, kn) * MEM_SCALE
            s = s - jnp.max(s, axis=-1, keepdims=True)
            p = jnp.exp(s)
            p = p / jnp.sum(p, axis=-1, keepdims=True)
            dom = jnp.where(qmasks[m], dout, 0.0).astype(BF16)
            dp = _dot_nt(dom, v)
            ds = (p * (dp - jnp.sum(p * dp, axis=-1, keepdims=True)) * MEM_SCALE).astype(BF16)
            dqn = jnp.where(qmasks[m], _dot(ds, kn), dqn)
            dkn = jnp.where(kmasks[m], _dot_tn(ds, qm), dkn)
            dvv = jnp.where(kmasks[m], _dot_tn(p, dom), dvv)
        dqw_ref[...] += jnp.sum(dqn * qhat, axis=0, keepdims=True)
        dq_ref[...] = _mem_head_rms_bwd(dqn * qw_ref[...], qhat, qr, qmasks).astype(BF16)
        dk_acc[...] += dkn
        dv_acc[...] += dvv

        @pl.when(step == nsteps - 1)
        def _():
            dk = dk_acc[...]
            dkw_ref[...] = jnp.sum(dk * khat, axis=0, keepdims=True)
            dkv_ref[:, :MEM_WIDTH] = _mem_head_rms_bwd(dk * kw_ref[...], khat, kr, kmasks)
            dkv_ref[:, MEM_WIDTH:] = dv_acc[...]

    return pl.pallas_call(
        body, grid=(nsteps,),
        in_specs=[pl.BlockSpec((tt, MEM_WIDTH), lambda i: (i, qcol)), pl.BlockSpec((MEM_TOKENS, MEM_WIDTH), lambda i: (0, 0)),
                  pl.BlockSpec((MEM_TOKENS, MEM_WIDTH), lambda i: (0, 1)), _full((1, MEM_WIDTH)), _full((1, MEM_WIDTH)),
                  pl.BlockSpec((tt, MEM_WIDTH), lambda i: (i, ocol))],
        out_specs=[pl.BlockSpec((tt, MEM_WIDTH), lambda i: (i, 0)), _full((MEM_TOKENS, 2 * MEM_WIDTH)),
                   _full((1, MEM_WIDTH)), _full((1, MEM_WIDTH))],
        out_shape=[jax.ShapeDtypeStruct((T, MEM_WIDTH), BF16), jax.ShapeDtypeStruct((MEM_TOKENS, 2 * MEM_WIDTH), F32),
                   jax.ShapeDtypeStruct((1, MEM_WIDTH), F32), jax.ShapeDtypeStruct((1, MEM_WIDTH), F32)],
        scratch_shapes=[pltpu.VMEM((MEM_TOKENS, MEM_WIDTH), F32), pltpu.VMEM((MEM_TOKENS, MEM_WIDTH), F32)],
        compiler_params=_cp("arbitrary"), name=name)(proj, mkv, mkv, qn_w, kn_w, dmix)


HALF = HEAD_DIM // 2
ATT_SCALE = HEAD_DIM ** -0.5
NEG = -1e30


def _rope_tables(T):
    inv = np.float32(ROPE_THETA) ** (-np.arange(HALF, dtype=np.float32) / np.float32(HALF))
    ang = np.arange(T, dtype=np.float32)[:, None] * inv[None, :].astype(np.float32)
    cos, sin = np.cos(ang).astype(np.float32), np.sin(ang).astype(np.float32)
    return jnp.asarray(np.concatenate([cos, cos], axis=-1)), jnp.asarray(np.concatenate([-sin, sin], axis=-1))


def _rope(x, cosf, sinsg):
    return x * cosf + pltpu.roll(x, HALF, 1) * sinsg


def _rope_bwd(dy, cosf, sinsg):
    return dy * cosf + pltpu.roll(dy * sinsg, HALF, 1)


def _q_prep_bwd(proj, w_heads, cosf, sinsg, dqs, *, tt, name):
    T = proj.shape[0]
    W = N_GROUPS * B_WIDTH

    def body(x_ref, w_ref, c_ref, s_ref, d0, d1, d2, dx_ref, dw_ref):
        @pl.when(pl.program_id(0) == 0)
        def _():
            dw_ref[...] = jnp.zeros_like(dw_ref)

        c, s = c_ref[...], s_ref[...]
        for gi, d_ref in enumerate((d0, d1, d2)):
            for h in range(B_HEADS):
                sl = slice((gi * B_HEADS + h) * HEAD_DIM, (gi * B_HEADS + h + 1) * HEAD_DIM)
                xhat, r = _head_rms(x_ref[:, sl])
                dyn = _rope_bwd(d_ref[:, h * HEAD_DIM:(h + 1) * HEAD_DIM], c, s)
                dw_ref[:, sl] += jnp.sum(dyn * xhat, axis=0, keepdims=True)
                dx_ref[:, sl] = _head_rms_bwd(dyn * w_ref[:, sl], xhat, r).astype(BF16)

    tbl = pl.BlockSpec((tt, HEAD_DIM), lambda i: (i, 0))
    dyb = pl.BlockSpec((tt, B_WIDTH), lambda i: (i, 0))
    return pl.pallas_call(
        body, grid=(T // tt,),
        in_specs=[pl.BlockSpec((tt, W), lambda i: (i, 0)), _full((1, W)), tbl, tbl, dyb, dyb, dyb],
        out_specs=[pl.BlockSpec((tt, W), lambda i: (i, 0)), _full((1, W))],
        out_shape=[jax.ShapeDtypeStruct((T, W), BF16), jax.ShapeDtypeStruct((1, W), F32)],
        compiler_params=_cp("arbitrary"), name=name)(proj, w_heads, cosf, sinsg, *dqs)


def _kv_prep_bwd(kv, w_heads, cosf, sinsg, dks, dvs, *, tt, name):
    T = kv.shape[0]

    def body(x_ref, w_ref, c_ref, s_ref, k0, k1, k2, v0, v1, v2, dx_ref, dw_ref):
        @pl.when(pl.program_id(0) == 0)
        def _():
            dw_ref[...] = jnp.zeros_like(dw_ref)

        c, s = c_ref[...], s_ref[...]
        for h in range(B_HEADS):
            sl = slice(h * HEAD_DIM, (h + 1) * HEAD_DIM)
            vs = slice(B_WIDTH + h * HEAD_DIM, B_WIDTH + (h + 1) * HEAD_DIM)
            xhat, r = _head_rms(x_ref[:, sl])
            dyn = _rope_bwd(k0[:, sl] + k1[:, sl] + k2[:, sl], c, s)
            dw_ref[:, sl] += jnp.sum(dyn * xhat, axis=0, keepdims=True)
            dx_ref[:, sl] = _head_rms_bwd(dyn * w_ref[:, sl], xhat, r).astype(BF16)
            dx_ref[:, vs] = (v0[:, sl] + v1[:, sl] + v2[:, sl]).astype(BF16)

    tbl = pl.BlockSpec((tt, HEAD_DIM), lambda i: (i, 0))
    dyb = pl.BlockSpec((tt, B_WIDTH), lambda i: (i, 0))
    return pl.pallas_call(
        body, grid=(T // tt,),
        in_specs=[dyb, _full((1, B_WIDTH)), tbl, tbl] + [dyb] * 6,
        out_specs=[pl.BlockSpec((tt, 2 * B_WIDTH), lambda i: (i, 0)), _full((1, B_WIDTH))],
        out_shape=[jax.ShapeDtypeStruct((T, 2 * B_WIDTH), BF16), jax.ShapeDtypeStruct((1, B_WIDTH), F32)],
        compiler_params=_cp("arbitrary"), name=name)(kv, w_heads, cosf, sinsg, *dks, *dvs)


def _band_masks(n_is_first=None):
    row = lax.broadcasted_iota(jnp.int32, (SPAN, SPAN), 0)
    col = lax.broadcasted_iota(jnp.int32, (SPAN, SPAN), 1)
    return row >= col, col >= row


def _dil_views(T, d):
    L = T // d
    return L, L // SPAN


def _dil_fwd(qr, kr, kv, gi, d, *, name):
    T = qr.shape[0]
    L, nb = _dil_views(T, d)

    def body(q_ref, kc_ref, kp_ref, vc_ref, vp_ref, o_ref, lse_ref):
        cur_ok, prev_band = _band_masks()
        prev_ok = prev_band & (pl.program_id(1) > 0)
        for h in range(B_HEADS):
            sl = slice(h * HEAD_DIM, (h + 1) * HEAD_DIM)
            q = q_ref[:, sl]
            sc = jnp.where(cur_ok, _dot_nt(q, kc_ref[:, sl]) * ATT_SCALE, NEG)
            sp = jnp.where(prev_ok, _dot_nt(q, kp_ref[:, sl]) * ATT_SCALE, NEG)
            m = jnp.maximum(jnp.max(sc, axis=-1, keepdims=True), jnp.max(sp, axis=-1, keepdims=True))
            pc = jnp.exp(sc - m)
            pp = jnp.exp(sp - m)
            l = jnp.sum(pc, axis=-1, keepdims=True) + jnp.sum(pp, axis=-1, keepdims=True)
            o_ref[:, sl] = (_dot(pc, vc_ref[:, sl]) + _dot(pp, vp_ref[:, sl])) / l
            lse_ref[:, sl] = jnp.broadcast_to(m + jnp.log(l), (SPAN, HEAD_DIM))

    blk = lambda f: pl.BlockSpec((SPAN, B_WIDTH), f)
    cur = lambda r, n: (n, r)
    prev = lambda r, n: (jnp.maximum(n - 1, 0), r)
    ov = jax.ShapeDtypeStruct((L, d * B_WIDTH), F32)
    o, lse = pl.pallas_call(
        body, grid=(d, nb),
        in_specs=[blk(lambda r, n: (n, r * N_GROUPS + gi)), blk(cur), blk(prev),
                  blk(lambda r, n: (n, 2 * r + 1)), blk(lambda r, n: (jnp.maximum(n - 1, 0), 2 * r + 1))],
        out_specs=[blk(cur), blk(cur)], out_shape=[ov, ov],
        compiler_params=_cp("parallel", "arbitrary"), name=name,
    )(qr.reshape(L, d * N_GROUPS * B_WIDTH), kr.reshape(L, d * B_WIDTH), kr.reshape(L, d * B_WIDTH),
      kv.reshape(L, d * 2 * B_WIDTH), kv.reshape(L, d * 2 * B_WIDTH))
    return o.reshape(T, B_WIDTH), lse.reshape(T, B_WIDTH)


def _dil_combine_fwd(os_, lses, *, tt, name):
    T = os_[0].shape[0]

    def body(o0, o1, o2, l0, l1, l2, y_ref, lse_ref):
        a, b, c = l0[...], l1[...], l2[...]
        m = jnp.maximum(jnp.maximum(a, b), c)
        wa, wb, wc = jnp.exp(a - m), jnp.exp(b - m), jnp.exp(c - m)
        den = wa + wb + wc
        y_ref[...] = (wa * o0[...] + wb * o1[...] + wc * o2[...]) / den
        lse_ref[...] = m + jnp.log(den)

    blk = pl.BlockSpec((tt, B_WIDTH), lambda i: (i, 0))
    sh = jax.ShapeDtypeStruct((T, B_WIDTH), F32)
    return pl.pallas_call(
        body, grid=(T // tt,), in_specs=[blk] * 6, out_specs=[blk, blk], out_shape=[sh, sh],
        compiler_params=_cp("parallel"), name=name)(*os_, *lses)


DILS_UNROLL = 4


def _dils_specs(gi, d, nblk):
    blk = lambda f: pl.BlockSpec((SPAN * d, HEAD_DIM), f)
    return {
        "q": blk(lambda h, n: (n, gi * B_HEADS + h)), "q_next": blk(lambda h, n: (jnp.minimum(n + 1, nblk - 1), gi * B_HEADS + h)),
        "cur": blk(lambda h, n: (n, h)), "prev": blk(lambda h, n: (jnp.maximum(n - 1, 0), h)),
        "next": blk(lambda h, n: (jnp.minimum(n + 1, nblk - 1), h)),
        "v": blk(lambda h, n: (n, B_HEADS + h)), "v_prev": blk(lambda h, n: (jnp.maximum(n - 1, 0), B_HEADS + h)),
    }


def _dils_fwd(qr, kr, kv, gi, d, *, name):
    T = qr.shape[0]
    nblk = T // (SPAN * d)
    sp = _dils_specs(gi, d, nblk)

    def body(q_ref, kc_ref, vc_ref, o_ref, lse_ref, k_before, v_before):
        @pl.when(pl.program_id(1) == 0)
        def _():
            k_before[...] = jnp.zeros_like(k_before)
            v_before[...] = jnp.zeros_like(v_before)

        cur_ok, prev_band = _band_masks()
        prev_ok = prev_band & (pl.program_id(1) > 0)

        def residue(r, carry):
            rows = pl.ds(r, SPAN, stride=d)
            q, kc, vc = q_ref[rows, :], kc_ref[rows, :].astype(BF16), vc_ref[rows, :].astype(BF16)
            sc = jnp.where(cur_ok, _dot_nt(q, kc) * ATT_SCALE, NEG)
            sp_ = jnp.where(prev_ok, _dot_nt(q, k_before[r]) * ATT_SCALE, NEG)
            m = jnp.maximum(jnp.max(sc, axis=-1, keepdims=True), jnp.max(sp_, axis=-1, keepdims=True))
            pc = jnp.exp(sc - m)
            pp = jnp.exp(sp_ - m)
            l = jnp.sum(pc, axis=-1, keepdims=True) + jnp.sum(pp, axis=-1, keepdims=True)
            o_ref[rows, :] = (_dot(pc, vc) + _dot(pp, v_before[r])) / l
            lse_ref[rows, :] = jnp.broadcast_to(m + jnp.log(l), (SPAN, HEAD_DIM))
            k_before[r] = kc
            v_before[r] = vc
            return carry

        lax.fori_loop(0, d, residue, 0, unroll=DILS_UNROLL)

    sh = jax.ShapeDtypeStruct((T, B_WIDTH), F32)
    return pl.pallas_call(
        body, grid=(B_HEADS, nblk), in_specs=[sp["q"], sp["cur"], sp["v"]],
        out_specs=[sp["cur"], sp["cur"]], out_shape=[sh, sh],
        scratch_shapes=[pltpu.VMEM((d, SPAN, HEAD_DIM), BF16), pltpu.VMEM((d, SPAN, HEAD_DIM), BF16)],
        compiler_params=_cp("parallel", "arbitrary"), name=name)(qr, kr, kv)


DIL_BWD_GROUP = {1: 4, 4: 1, 16: 1}


def _dil_bwd(qr, kr, kv, dmix, lse, dd, gi, d, *, name, dep=None):
    T = qr.shape[0]
    G = DIL_BWD_GROUP[d]
    band = SPAN * d
    tb = G * band
    nblk = T // tb
    n_units = T // SPAN

    keep = G == 1

    def kernel_body(q_ref, dy_ref, lse_ref, dd_ref, kc_ref, vc_ref, *rest):
        if keep:
            dq_ref, dk_ref, dv_ref, dk_acc, dv_acc, k_before, v_before = rest
        else:
            kp_ref, vp_ref, dq_ref, dk_ref, dv_ref, dk_acc, dv_acc = rest
        n = pl.program_id(1)

        @pl.when(n == 0)
        def _():
            dk_acc[...] = jnp.zeros_like(dk_acc)
            dv_acc[...] = jnp.zeros_like(dv_acc)
            if keep:
                k_before[...] = jnp.zeros_like(k_before)
                v_before[...] = jnp.zeros_like(v_before)

        cur_ok, prev_band = _band_masks()
        for j in range(G):
            def residue(r, carry, j=j):
                off = j * band + r
                rows = pl.ds(off, SPAN, stride=d)
                q, dy = q_ref[rows, :], dy_ref[rows, :]
                lse_h = jnp.max(lse_ref[rows, :], axis=-1, keepdims=True)
                dd_h = jnp.max(dd_ref[rows, :], axis=-1, keepdims=True)
                kc, vc = kc_ref[rows, :].astype(BF16), vc_ref[rows, :].astype(BF16)
                if j > 0:
                    before = pl.ds(off - band, SPAN, stride=d)
                    kp, vp = kc_ref[before, :], vc_ref[before, :]
                    prev_ok = prev_band
                elif keep:
                    kp, vp = k_before[r], v_before[r]
                    k_before[r] = kc
                    v_before[r] = vc
                    prev_ok = prev_band & (n > 0)
                else:
                    before = pl.ds((G - 1) * band + r, SPAN, stride=d)
                    kp, vp = kp_ref[before, :], vp_ref[before, :]
                    prev_ok = prev_band & (n > 0)
                pc = jnp.exp(jnp.where(cur_ok, _dot_nt(q, kc) * ATT_SCALE, NEG) - lse_h)
                pp = jnp.exp(jnp.where(prev_ok, _dot_nt(q, kp) * ATT_SCALE, NEG) - lse_h)
                dsc = pc * (_dot_nt(dy, vc) - dd_h) * ATT_SCALE
                dsp = pp * (_dot_nt(dy, vp) - dd_h) * ATT_SCALE
                dq_ref[rows, :] = _dot(dsc, kc) + _dot(dsp, kp)
                u = (n * G + j) * d + r
                here = pl.ds(pl.multiple_of(u * SPAN, SPAN), SPAN)
                dk_acc[here, :] += _dot_tn(dsc, q)
                dv_acc[here, :] += _dot_tn(pc, dy)
                there = pl.ds(pl.multiple_of(jnp.maximum(u - d, 0) * SPAN, SPAN), SPAN)
                dk_acc[there, :] += _dot_tn(dsp, q)
                dv_acc[there, :] += _dot_tn(pp, dy)
                return carry

            lax.fori_loop(0, d, residue, 0, unroll=min(d, DILS_UNROLL))

        @pl.when(n == nblk - 1)
        def _():
            def place(u, carry):
                rows = pl.ds((u // d) * band + u % d, SPAN, stride=d)
                src = pl.ds(pl.multiple_of(u * SPAN, SPAN), SPAN)
                dk_ref[rows, :] = dk_acc[src, :]
                dv_ref[rows, :] = dv_acc[src, :]
                return carry

            lax.fori_loop(0, n_units, place, 0)

    blk = lambda f: pl.BlockSpec((tb, HEAD_DIM), f)
    cur = lambda h, n: (n, h)
    prev = lambda h, n: (jnp.maximum(n - 1, 0), h)
    whole = pl.BlockSpec((T, HEAD_DIM), lambda h, n: (0, h))
    sh = jax.ShapeDtypeStruct((T, B_WIDTH), F32)
    v_cur = blk(lambda h, n: (n, B_HEADS + h))
    if keep:
        kv_specs, kv_args = [blk(cur), v_cur], [kr, kv]
        kept = [pltpu.VMEM((d, SPAN, HEAD_DIM), BF16), pltpu.VMEM((d, SPAN, HEAD_DIM), BF16)]
    else:
        kv_specs = [blk(cur), v_cur, blk(prev), blk(lambda h, n: (jnp.maximum(n - 1, 0), B_HEADS + h))]
        kv_args, kept = [kr, kv, kr, kv], []
    body, dep_specs, dep_args = _dep(kernel_body, 4 + len(kv_args), dep)
    return pl.pallas_call(
        body, grid=(B_HEADS, nblk),
        in_specs=[blk(lambda h, n: (n, gi * B_HEADS + h)), blk(cur), blk(cur), blk(cur)] + kv_specs + dep_specs,
        out_specs=[blk(cur), whole, whole], out_shape=[sh, sh, sh],
        scratch_shapes=[pltpu.VMEM((T, HEAD_DIM), F32), pltpu.VMEM((T, HEAD_DIM), F32)] + kept,
        compiler_params=_cp("parallel", "arbitrary"), name=name)(qr, dmix, lse, dd, *kv_args, *dep_args)


A_MQ_COL = 4 * A_WIDTH // MEM_WIDTH
B_MQ_COL = N_GROUPS * B_WIDTH // MEM_WIDTH


def _row(v):
    return v.reshape(1, -1).astype(F32)


def _local_step(x, mem, tgt, get_w, P, put_g, first_dep=None, forward_point=lambda i, value: value):
    T = x.shape[0]
    cosf, sinsg = _rope_tables(T)
    lb_soft = jax.nn.softmax(P["a_lb_logits"].astype(F32), axis=0)
    lb = lb_soft[0:1]
    qw_heads = jnp.repeat(P["b_qnorm"][0], B_HEADS, axis=0).reshape(1, -1)
    kw_heads = jnp.tile(_row(P["b_knorm"]), (1, B_HEADS))
    mqw = [jnp.tile(_row(P["mem_qnorm"][l]), (1, MEM_HEADS)) for l in range(2)]
    mkw = [jnp.tile(_row(P["mem_knorm"][l]), (1, MEM_HEADS)) for l in range(2)]
    nmix = [_row(P["norm_mix"][l]) for l in range(2)]
    nffn = [_row(P["norm_ffn"][l]) for l in range(2)]
    mnorm = [_row(P["mem_norm"][l]) for l in range(2)]
    kvn = _row(P["kv_norm"])
    onorm = _row(P["a_onorm"])
    W = {}

    def w_of(name, after=None):
        if name not in W:
            W[name] = get_w(name, after)
        return W[name]

    proj_a, xn0 = _rms_matmul(x, nmix[0], w_of("a_w_in"), tt=512, tn=1664, wt=True, name="proj_a", dep=first_dep)
    mkv0, mn0 = _rms_matmul(mem, mnorm[0], w_of("w_mem_kv0"), tt=MEM_TOKENS, tn=2 * MEM_WIDTH, wt=False, name="mem_kv0")
    o_raw, st = _hgrn2_fwd(proj_a, lb, name="hgrn2_fwd")
    o_raw = forward_point(0, o_raw)
    mm0 = _a_post_fwd(o_raw, proj_a, onorm, tt=512, name="a_post_fwd")
    mo0 = _mem_attn_fwd(proj_a, A_MQ_COL, mkv0, mqw[0], mkw[0], tt=512, name="mem_attn_fwd0")
    hm0 = _mm_res(x, mm0, mo0, w_of("w_out0", mo0), tt=512, name="out_proj0")
    hm0 = forward_point(1, hm0)
    gu0, hn0 = _rms_matmul(hm0, nffn[0], w_of("w_gate_up0", hm0), tt=512, tn=1408, wt=True, out_dtype=BF16, name="gate_up0")
    h1 = _swiglu_down(hm0, gu0, w_of("w_down0", gu0), tt=512, name="down0")
    h1 = forward_point(2, h1)
    kv, hkn, kr = _rms_matmul(h1, kvn, w_of("w_kv", h1), tt=512, tn=768, wt=True, name="kv_proj",
                              rotate=(kw_heads, cosf, sinsg))

    proj_b, xn1, qr = _rms_matmul(h1, nmix[1], w_of("b_w_in", kr), tt=512, tn=1280, wt=True, name="proj_b",
                                  rotate=(qw_heads, cosf, sinsg))
    proj_b = forward_point(3, proj_b)
    mkv1, mn1 = _rms_matmul(mem, mnorm[1], w_of("w_mem_kv1", kr), tt=MEM_TOKENS, tn=2 * MEM_WIDTH, wt=False, name="mem_kv1")
    outs = [(_dil_fwd if d == 1 else _dils_fwd)(qr, kr, kv, gi, d, name=f"dil_fwd{gi}") for gi, d in enumerate(DILATIONS)]
    mm1, lse_tot = _dil_combine_fwd([o for o, _ in outs], [s for _, s in outs], tt=512, name="dil_combine")
    mo1 = _mem_attn_fwd(proj_b, B_MQ_COL, mkv1, mqw[1], mkw[1], tt=512, name="mem_attn_fwd1")
    hm1 = _mm_res(h1, mm1, mo1, w_of("w_out1", mo1), tt=512, name="out_proj1")
    gu1, hn1 = _rms_matmul(hm1, nffn[1], w_of("w_gate_up1", hm1), tt=512, tn=1408, wt=True, out_dtype=BF16, name="gate_up1")
    dy, sq = _swiglu_down_loss(hm1, gu1, w_of("w_down1", gu1), tgt, tt=512, name="down1_loss")

    gP = {}
    zeros_mem = jnp.zeros((MEM_TOKENS, D_MODEL), F32)

    def ffn_bwd(l, dh, hm, gu, hn):
        dgu, g_wd = _swiglu_bwd(dh, gu, w_of(f"w_down{l}"), tt=256, name=f"swiglu_bwd{l}")
        g_wgu = _mm_tn(dgu, hn, tt=512, tka=1408, name=f"g_w_gate_up{l}")
        sent = put_g({f"w_down{l}": g_wd, f"w_gate_up{l}": g_wgu})
        dhm, g_nf = _rms_bwd_dx(hm, nffn[l], w_of(f"w_gate_up{l}"), dgu, dh, tt=512, wt=True, name=f"gate_up_bwd{l}", dep=sent)
        return dhm, g_nf

    def mix_bwd(l, dhm, mix_main, mix_mem, proj, qcol, mkv, mn):
        dmix, g_wout, *head_dots = _out_proj_bwd(dhm, mix_main, mix_mem, w_of(f"w_out{l}"), tt=512, name=f"out_proj_bwd{l}",
                                                 head_dots=l == 1)
        dmq, dmkv, dqw, dkw = _mem_attn_bwd(proj, qcol, mkv, mqw[l], mkw[l], dmix, tt=512, name=f"mem_attn_bwd{l}")
        g_wmkv = _mm_tn(mn, dmkv, tt=MEM_TOKENS, tka=512, name=f"g_w_mem_kv{l}")
        sent = put_g({f"w_out{l}": g_wout, f"w_mem_kv{l}": g_wmkv})
        _, g_mn = _rms_bwd_dx(mem, mnorm[l], w_of(f"w_mem_kv{l}"), dmkv, zeros_mem, tt=MEM_TOKENS, wt=False, name=f"mem_kv_bwd{l}")
        fold = lambda v: v.reshape(MEM_HEADS, MEM_HEAD_DIM).sum(axis=0)
        return dmix, dmq, g_mn, fold(dqw), fold(dkw), sent, head_dots

    dhm1, g_nf1 = ffn_bwd(1, dy, hm1, gu1, hn1)
    dmix1, dmq1, g_mn1, g_mq1, g_mk1, sent, (dd,) = mix_bwd(1, dhm1, mm1, mo1, proj_b, B_MQ_COL, mkv1, mn1)
    dqs, dks, dvs = [], [], []
    for gi, d in enumerate(DILATIONS):
        dq_g, dk_g, dv_g = _dil_bwd(qr, kr, kv, dmix1, lse_tot, dd, gi, d, name=f"dil_bwd{gi}", dep=sent if gi == 0 else None)
        dqs.append(dq_g)
        dks.append(dk_g)
        dvs.append(dv_g)
    dq_raw, dqw = _q_prep_bwd(proj_b, qw_heads, cosf, sinsg, dqs, tt=512, name="q_prep_bwd")
    dkv, dkw = _kv_prep_bwd(kv, kw_heads, cosf, sinsg, dks, dvs, tt=512, name="kv_prep_bwd")
    dproj_b = [dq_raw, dmq1]
    g_wb = _mm_tn_pieces(dproj_b, xn1, tt=512, name="g_b_w_in")
    g_wkv = _mm_tn(dkv, hkn, tt=512, tka=768, name="g_w_kv")
    sent = put_g({"b_w_in": g_wb, "w_kv": g_wkv})
    dh1, g_nm1 = _rms_bwd_dx(h1, nmix[1], w_of("b_w_in"), dproj_b, dhm1, tt=512, wt=True, name="proj_b_bwd", dep=sent)
    dh1, g_kvn = _rms_bwd_dx(h1, kvn, w_of("w_kv"), dkv, dh1, tt=512, wt=True, name="kv_proj_bwd")

    dhm0, g_nf0 = ffn_bwd(0, dh1, hm0, gu0, hn0)
    dmix0, dmq0, g_mn0, g_mq0, g_mk0, sent, _ = mix_bwd(0, dhm0, mm0, mo0, proj_a, A_MQ_COL, mkv0, mn0)
    do_raw, dg, g_onorm = _a_post_bwd(o_raw, proj_a, onorm, dmix0, tt=512, name="a_post_bwd", dep=sent)
    dq, dz, dv, dlb = _hgrn2_bwd(proj_a, lb, st, do_raw, name="hgrn2_bwd")
    dproj_a = [dq, dz, dv, dg, dmq0]
    sent = put_g({"a_w_in": _mm_tn_pieces(dproj_a, xn0, tt=512, name="g_a_w_in")})
    gx, g_nm0 = _rms_bwd_dx(x, nmix[0], w_of("a_w_in"), dproj_a, dhm0, tt=512, wt=True, name="proj_a_bwd", dep=sent)

    dl0 = lb_soft[0:1] * lb_soft[1:2] * dlb
    gP["a_lb_logits"] = jnp.concatenate([dl0, -dl0], axis=0)
    gP["a_onorm"] = g_onorm
    gP["norm_mix"] = jnp.concatenate([g_nm0, g_nm1], axis=0)
    gP["norm_ffn"] = jnp.concatenate([g_nf0, g_nf1], axis=0)
    gP["b_qnorm"] = dqw.reshape(N_GROUPS, B_HEADS, HEAD_DIM).sum(axis=1)[None]
    gP["kv_norm"] = g_kvn.reshape(-1)
    gP["b_knorm"] = dkw.reshape(B_HEADS, HEAD_DIM).sum(axis=0)
    gP["mem_norm"] = jnp.concatenate([g_mn0, g_mn1], axis=0)
    gP["mem_qnorm"] = jnp.stack([g_mq0, g_mq1])
    gP["mem_knorm"] = jnp.stack([g_mk0, g_mk1])
    return sq, gx, gP


MESH_ID = pl.DeviceIdType.MESH
HBM_SPEC = pl.BlockSpec(memory_space=pltpu.HBM)


def _position():
    return lax.axis_index("x"), lax.axis_index("y"), lax.axis_index("c")


def _all_gather_direct(block, after, *, name):
    def body(x_ref, after_ref, out_ref, send_sems, recv_sems, local_sem):
        x, y, c = _position()
        me = 4 * x + 2 * y + c
        mine = pltpu.make_async_copy(x_ref, out_ref.at[me], local_sem)
        mine.start()
        copies = []
        for k in ALL_PEERS:
            cp = pltpu.make_async_remote_copy(
                src_ref=x_ref, dst_ref=out_ref.at[me], send_sem=send_sems.at[k - 1], recv_sem=recv_sems.at[k - 1],
                device_id=_peer(k, x, y, c), device_id_type=MESH_ID)
            cp.start()
            copies.append(cp)
        for cp in copies:
            cp.wait()
        mine.wait()

    return pl.pallas_call(
        body, out_shape=jax.ShapeDtypeStruct((N_DEV,) + block.shape, block.dtype),
        in_specs=[HBM_SPEC, pl.BlockSpec(memory_space=pl.ANY)], out_specs=HBM_SPEC,
        scratch_shapes=[pltpu.SemaphoreType.DMA((7,)), pltpu.SemaphoreType.DMA((7,)), pltpu.SemaphoreType.DMA],
        name=name)(block, after)


SEM_SPEC = pl.BlockSpec(memory_space=pltpu.SEMAPHORE)
ANY_SPEC = pl.BlockSpec(memory_space=pl.ANY)
DATAFLOW = pltpu.SideEffectType.DATAFLOW_SIDE_EFFECTING


def _peer(k, x, y, c):
    return (1 - x if (k >> 2) & 1 else x, 1 - y if (k >> 1) & 1 else y, 1 - c if k & 1 else c)


def _own_slot_filled(own_block):
    x, y, c = _position()
    zone = lax.empty((N_DEV,) + own_block.shape, own_block.dtype)
    return lax.dynamic_update_slice_in_dim(zone, own_block[None], 4 * x + 2 * y + c, axis=0)


ALL_PEERS = tuple(range(1, N_DEV))
SIBLING_AND_SAME_CORE = (1, 2, 4, 6)
SAME_CORE = (2, 4, 6)


def _split_start(srcs, scatter, after, *, name, relations=ALL_PEERS, carried=None):
    n = len(srcs)
    extra = ([] if after is None else [after]) + ([] if carried is None else [carried])
    n_carried = 0 if carried is None else 1
    x, y, c = _position()
    me = 4 * x + 2 * y + c
    lands = [_own_slot_filled(lax.dynamic_index_in_dim(s, me, 0, keepdims=False) if scatter else s) for s in srcs]

    def body(*refs):
        src_refs, land_refs = refs[:n], refs[n:2 * n]
        send_sems, recv_sems = refs[2 * n + len(extra)], refs[2 * n + len(extra) + 1]
        token = refs[2 * n + len(extra) + 2 + 2 * n]
        bx, by, bc = _position()
        bme = 4 * bx + 2 * by + bc
        for a in range(n):
            for k in relations:
                tx, ty, tc = _peer(k, bx, by, bc)
                src = src_refs[a].at[4 * tx + 2 * ty + tc] if scatter else src_refs[a]
                pltpu.make_async_remote_copy(
                    src_ref=src, dst_ref=land_refs[a].at[bme],
                    send_sem=send_sems.at[7 * a + k - 1], recv_sem=recv_sems.at[7 * a + k - 1],
                    device_id=(tx, ty, tc), device_id_type=MESH_ID).start()
        token[...] = jnp.zeros_like(token)

    hbm = lambda a: pltpu.HBM(a.shape, a.dtype)
    outs = pl.pallas_call(
        body, name=name,
        out_shape=(pltpu.SemaphoreType.DMA((7 * n,)), pltpu.SemaphoreType.DMA((7 * n,)),
                   *[hbm(s) for s in srcs], *[hbm(l) for l in lands], jax.ShapeDtypeStruct((8, 128), F32),
                   *([hbm(carried)] if n_carried else [])),
        in_specs=[HBM_SPEC] * (2 * n) + [ANY_SPEC] * len(extra),
        out_specs=(SEM_SPEC, SEM_SPEC, *[HBM_SPEC] * (2 * n), pl.BlockSpec(memory_space=pltpu.VMEM), *([ANY_SPEC] * n_carried)),
        input_output_aliases={**{i: 2 + i for i in range(2 * n)},
                              **({2 * n + len(extra) - 1: 2 * n + 3} if n_carried else {})},
        compiler_params=pltpu.CompilerParams(has_side_effects=DATAFLOW),
    )(*[pltpu.with_memory_space_constraint(s, pltpu.HBM) for s in srcs],
      *[pltpu.with_memory_space_constraint(l, pltpu.HBM) for l in lands], *extra)
    return {"n": n, "relations": relations, "send": outs[0], "recv": outs[1], "srcs": list(outs[2:2 + n]),
            "lands": list(outs[2 + n:2 + 2 * n]), "token": outs[2 * n + 2], "carried": outs[-1] if n_carried else None}


def _forward_start(lands, carried, *, name):
    n = len(lands)

    def body(*refs):
        land_refs = refs[:n]
        send_sems, recv_sems = refs[n + 1], refs[n + 2]
        bx, by, bc = _position()
        for a in range(n):
            for k in SAME_CORE:
                tx, ty, tc = _peer(k, bx, by, bc)
                block = land_refs[a].at[4 * tx + 2 * ty + tc]
                pltpu.make_async_remote_copy(
                    src_ref=block, dst_ref=block,
                    send_sem=send_sems.at[7 * a + k - 1], recv_sem=recv_sems.at[7 * a + k - 1],
                    device_id=(bx, by, 1 - bc), device_id_type=MESH_ID).start()

    hbm = lambda a: pltpu.HBM(a.shape, a.dtype)
    outs = pl.pallas_call(
        body, name=name,
        out_shape=(pltpu.SemaphoreType.DMA((7 * n,)), pltpu.SemaphoreType.DMA((7 * n,)),
                   *[hbm(l) for l in lands], hbm(carried)),
        in_specs=[HBM_SPEC] * n + [ANY_SPEC],
        out_specs=(SEM_SPEC, SEM_SPEC, *[HBM_SPEC] * n, ANY_SPEC),
        input_output_aliases={i: 2 + i for i in range(n + 1)},
        compiler_params=pltpu.CompilerParams(has_side_effects=DATAFLOW),
    )(*lands, carried)
    handle = {"n": n, "relations": SAME_CORE, "send": outs[0], "recv": outs[1], "srcs": [], "lands": list(outs[2:2 + n])}
    return handle, outs[-1]


def _split_wait(handle, after, *, name):
    n, ns = handle["n"], len(handle["srcs"])

    def body(*refs):
        land_refs = refs[ns:ns + n]
        send_sems, recv_sems = refs[ns + n], refs[ns + n + 1]
        bx, by, bc = _position()
        for a in range(n):
            for k in handle["relations"]:
                block = land_refs[a].at[0]
                cp = pltpu.make_async_remote_copy(
                    src_ref=block, dst_ref=block,
                    send_sem=send_sems.at[7 * a + k - 1], recv_sem=recv_sems.at[7 * a + k - 1],
                    device_id=_peer(k, bx, by, bc), device_id_type=MESH_ID)
                cp.wait_send()
                cp.wait_recv()

    hbm = lambda a: pltpu.HBM(a.shape, a.dtype)
    outs = pl.pallas_call(
        body, name=name,
        out_shape=(*[hbm(s) for s in handle["srcs"]], *[hbm(l) for l in handle["lands"]]),
        in_specs=[HBM_SPEC] * (ns + n) + [SEM_SPEC, SEM_SPEC, ANY_SPEC],
        out_specs=tuple([HBM_SPEC] * (ns + n)),
        input_output_aliases={i: i for i in range(ns + n)},
        compiler_params=pltpu.CompilerParams(has_side_effects=DATAFLOW),
    )(*handle["srcs"], *handle["lands"], handle["send"], handle["recv"], after)
    return list(outs[ns:])


def _sum_sources(parts, *, tr, name):
    n, R, C = parts.shape

    def body(p_ref, o_ref):
        acc = p_ref[0].astype(F32)
        for s in range(1, n):
            acc = acc + p_ref[s].astype(F32)
        o_ref[...] = acc

    return pl.pallas_call(
        body, grid=(R // tr,), in_specs=[pl.BlockSpec((n, tr, C), lambda i: (0, i, 0))],
        out_specs=pl.BlockSpec((tr, C), lambda i: (i, 0)),
        out_shape=jax.ShapeDtypeStruct((R, C), F32), compiler_params=_cp("parallel"), name=name)(parts)


def _adamw_math(g, w, m, v):
    c1 = 1.0 - ADAM_B1 ** ADAM_STEP
    c2 = 1.0 - ADAM_B2 ** ADAM_STEP
    nm = ADAM_B1 * m + (1.0 - ADAM_B1) * g
    nv = ADAM_B2 * v + (1.0 - ADAM_B2) * (g * g)
    return -ADAM_LR * ((nm / c1) / (jnp.sqrt(nv / c2) + ADAM_EPS) + ADAM_WD * w), nm, nv


def _reduce_adamw(received, w, m, v, *, tr, name):
    L, R, C = w.shape

    def body(*refs):
        p_refs = refs[:L]
        w_ref, m_ref, v_ref, g_ref, d_ref, nm_ref, nv_ref = refs[L:]
        for l in range(L):
            @pl.when(pl.program_id(0) == l)
            def _(p_ref=p_refs[l]):
                acc = p_ref[0].astype(F32)
                for s in range(1, N_DEV):
                    acc = acc + p_ref[s].astype(F32)
                g_ref[...] = acc
                d_ref[...], nm_ref[...], nv_ref[...] = _adamw_math(acc, w_ref[...], m_ref[...], v_ref[...])

    p_spec = pl.BlockSpec((N_DEV, tr, C), lambda l, i: (0, i, 0))
    blk = pl.BlockSpec((None, tr, C), lambda l, i: (l, i, 0))
    sh = jax.ShapeDtypeStruct((L, R, C), F32)
    return pl.pallas_call(
        body, grid=(L, R // tr), in_specs=[p_spec] * L + [blk] * 3, out_specs=[blk] * 4, out_shape=[sh] * 4,
        compiler_params=_cp("parallel", "parallel"), name=name)(*received, w, m, v)


def _adamw(g, w, m, v, *, tr, name):
    L, R, C = w.shape

    def body(g_ref, w_ref, m_ref, v_ref, d_ref, nm_ref, nv_ref):
        d_ref[...], nm_ref[...], nv_ref[...] = _adamw_math(g_ref[...], w_ref[...], m_ref[...], v_ref[...])

    blk = pl.BlockSpec((None, tr, C), lambda l, i: (l, i, 0))
    sh = jax.ShapeDtypeStruct((L, R, C), F32)
    return pl.pallas_call(
        body, grid=(L, R // tr), in_specs=[blk] * 4, out_specs=[blk] * 3, out_shape=[sh] * 3,
        compiler_params=_cp("parallel", "parallel"), name=name)(g, w, m, v)


UNITS = {
    "a_w_in": ("a_w_in", 0, True), "w_mem_kv0": ("w_mem_kv", 0, False), "w_out0": ("w_out", 0, False),
    "w_gate_up0": ("w_gate_up", 0, True), "w_down0": ("w_down", 0, False), "w_kv": ("w_kv", None, True),
    "b_w_in": ("b_w_in", 0, True), "w_mem_kv1": ("w_mem_kv", 1, False), "w_out1": ("w_out", 1, False),
    "w_gate_up1": ("w_gate_up", 1, True), "w_down1": ("w_down", 1, False),
}
BIG = ("a_w_in", "b_w_in", "w_kv", "w_mem_kv", "w_out", "w_gate_up", "w_down")
ADAMW_ROW_TILE = {"a_w_in": 208, "b_w_in": 160, "w_kv": 192, "w_mem_kv": 128, "w_out": 128, "w_gate_up": 352, "w_down": 352}


def _wire_block(weights, unit):
    name, layer, col = UNITS[unit]
    a = weights[name] if layer is None else weights[name][layer]
    return (a.T if col else a).astype(BF16)


SMALL_REPLICATED = ("norm_mix", "norm_ffn", "b_qnorm", "kv_norm", "b_knorm", "mem_norm", "mem_qnorm", "mem_knorm")
SMALL_SHARDED = ("a_lb_logits", "a_onorm")
SMALL_ORDER = SMALL_REPLICATED + SMALL_SHARDED
LANES = 128


def _prod(shape):
    n = 1
    for s in shape:
        n *= s
    return n


def _pack_flat(arrays, rows, cols, dtype):
    flat = jnp.concatenate([a.reshape(-1).astype(dtype) for a in arrays])
    return jnp.pad(flat, (0, rows * cols - flat.shape[0])).reshape(rows, cols)


def _unpack_flat(packed, shapes):
    flat = packed.reshape(-1)
    out, off = [], 0
    for s in shapes:
        out.append(flat[off:off + _prod(s)].reshape(s))
        off += _prod(s)
    return out


def kernel(x, mem, norm_mix, norm_ffn, a_w_in, a_lb_logits, a_onorm, b_w_in, b_qnorm, kv_norm, w_kv, b_knorm, mem_norm, w_mem_kv, mem_qnorm, mem_knorm, w_out, w_gate_up, w_down, loss_target, m_norm_mix, m_norm_ffn, m_a_w_in, m_a_lb_logits, m_a_onorm, m_b_w_in, m_b_qnorm, m_kv_norm, m_w_kv, m_b_knorm, m_mem_norm, m_w_mem_kv, m_mem_qnorm, m_mem_knorm, m_w_out, m_w_gate_up, m_w_down, v_norm_mix, v_norm_ffn, v_a_w_in, v_a_lb_logits, v_a_onorm, v_b_w_in, v_b_qnorm, v_kv_norm, v_w_kv, v_b_knorm, v_mem_norm, v_w_mem_kv, v_mem_qnorm, v_mem_knorm, v_w_out, v_w_gate_up, v_w_down):
    names = ("norm_mix", "norm_ffn", "a_w_in", "a_lb_logits", "a_onorm", "b_w_in", "b_qnorm", "kv_norm", "w_kv", "b_knorm",
             "mem_norm", "w_mem_kv", "mem_qnorm", "mem_knorm", "w_out", "w_gate_up", "w_down")
    w = dict(zip(names, (norm_mix, norm_ffn, a_w_in, a_lb_logits, a_onorm, b_w_in, b_qnorm, kv_norm, w_kv, b_knorm,
                         mem_norm, w_mem_kv, mem_qnorm, mem_knorm, w_out, w_gate_up, w_down)))
    m = dict(zip(names, (m_norm_mix, m_norm_ffn, m_a_w_in, m_a_lb_logits, m_a_onorm, m_b_w_in, m_b_qnorm, m_kv_norm, m_w_kv,
                         m_b_knorm, m_mem_norm, m_w_mem_kv, m_mem_qnorm, m_mem_knorm, m_w_out, m_w_gate_up, m_w_down)))
    v = dict(zip(names, (v_norm_mix, v_norm_ffn, v_a_w_in, v_a_lb_logits, v_a_onorm, v_b_w_in, v_b_qnorm, v_kv_norm, v_w_kv,
                         v_b_knorm, v_mem_norm, v_w_mem_kv, v_mem_qnorm, v_mem_knorm, v_w_out, v_w_gate_up, v_w_down)))

    first = ["a_w_in", "w_mem_kv0"]
    later = [["w_out0", "w_gate_up0"], ["w_down0", "w_kv"], ["b_w_in", "w_mem_kv1"], ["w_out1", "w_gate_up1", "w_down1"]]
    first_half, second_half = {}, {}

    def start_first_half(i, after, carried=None):
        first_half[i] = _split_start([_wire_block(w, u) for u in later[i]], False, after, name=f"gather{i}_start",
                                     relations=SIBLING_AND_SAME_CORE, carried=carried)
        return first_half[i]

    opening = _split_start([_wire_block(w, u) for u in first] + [_pack_flat([a_lb_logits, a_onorm], 8, LANES, F32)],
                           False, None, name="gather_first_start", relations=SIBLING_AND_SAME_CORE)
    token = start_first_half(0, opening["token"])["token"]
    token = start_first_half(1, token)["token"]
    opening, token = _forward_start(_split_wait(opening, token, name="gather_first_landed"), token, name="gather_first_forward")
    gathered = _split_wait(opening, token, name="gather_first_wait")
    full = {u: g.reshape(-1, g.shape[-1]) for u, g in zip(first, gathered)}
    small_in = gathered[-1].reshape(N_DEV, -1)
    P = {n: w[n] for n in SMALL_REPLICATED}
    P["a_lb_logits"] = small_in[:, :192].reshape(N_DEV, 2, 96).transpose(1, 0, 2).reshape(2, A_WIDTH)
    P["a_onorm"] = small_in[:, 192:288].reshape(1, A_WIDTH)

    def forward_point(i, value):
        landed = _split_wait(first_half[i], value, name=f"gather{i}_landed")
        second_half[i], value = _forward_start(landed, value, name=f"gather{i}_forward")
        if i + 2 < len(later):
            value = start_first_half(i + 2, None, carried=value)["carried"]
        return value

    def get_w(unit, after):
        if unit not in full:
            i = [unit in group for group in later].index(True)
            for u, land in zip(later[i], _split_wait(second_half[i], after, name=f"gather{i}_wait")):
                full[u] = land.reshape(-1, land.shape[-1])
        return full[unit]

    sent = []

    def put_g(group):
        units = list(group)
        handle = _split_start([group[u].reshape(N_DEV, -1, group[u].shape[-1]) for u in units], True, None,
                              name=f"scatter{len(sent)}_start")
        sent.append((units, handle))
        return handle["token"]

    sq, gx, gP = _local_step(x[0], mem[0], loss_target[0], get_w, P, put_g, forward_point=forward_point)
    loss_here = (0.5 * jnp.sum(sq) / D_MODEL).reshape(1)

    received = {}
    group_of = {u: i for i, (units, _) in enumerate(sent) for u in units}
    out = {"grad": {}, "delta": {}, "new_m": {}, "new_v": {}}
    newest = [gx]

    def update_big(n):
        shape = w[n].shape
        as3 = lambda a: a.reshape((-1,) + shape[-2:])
        mine = [u for u, (wn, _, _) in UNITS.items() if wn == n]
        for i in sorted({group_of[u] for u in mine}):
            if sent[i][0][0] not in received:
                received.update(zip(sent[i][0], _split_wait(sent[i][1], newest[0], name=f"scatter{i}_wait")))
        flip = (lambda a: jnp.swapaxes(a, 1, 2)) if UNITS[mine[0]][2] else (lambda a: a)
        res = _reduce_adamw([received[u] for u in mine], flip(as3(w[n])), flip(as3(m[n])), flip(as3(v[n])),
                            tr=ADAMW_ROW_TILE[n], name=f"adamw_{n}")
        newest[0] = res[1]
        for kind, r in zip(("grad", "delta", "new_m", "new_v"), res):
            out[kind][n] = flip(r).reshape(shape)

    for n in ("w_down", "w_gate_up", "w_out", "w_mem_kv", "b_w_in", "w_kv"):
        update_big(n)

    full_shapes = [(2, A_WIDTH) if n == "a_lb_logits" else (1, A_WIDTH) if n == "a_onorm" else w[n].shape for n in SMALL_ORDER]
    n_small = sum(_prod(s) for s in full_shapes) + 1
    rows_small = -(-n_small // (8 * LANES)) * 8
    g_all = _all_gather_direct(_pack_flat([gP[n] for n in SMALL_ORDER] + [loss_here], rows_small, LANES, F32),
                               newest[0], name="gather_small_grads")
    summed = _unpack_flat(_sum_sources(g_all, tr=rows_small, name="sum_small_grads"), full_shapes + [(1,)])
    g_small = dict(zip(SMALL_ORDER, summed))
    loss = summed[-1].reshape(())
    me = 4 * lax.axis_index("x") + 2 * lax.axis_index("y") + lax.axis_index("c")
    for n in SMALL_SHARDED:
        g_small[n] = lax.dynamic_slice_in_dim(g_small[n], me * 96, 96, axis=1)
    shapes = [w[n].shape for n in SMALL_ORDER]
    rows_upd = -(-sum(_prod(s) for s in shapes) // (8 * LANES)) * 8
    pk = lambda d: _pack_flat([d[n] for n in SMALL_ORDER], rows_upd, LANES, F32)
    res = _adamw(pk(g_small)[None], pk(w)[None], pk(m)[None], pk(v)[None], tr=rows_upd, name="adamw_small")
    out["grad"].update(g_small)
    for kind, packed in zip(("delta", "new_m", "new_v"), res):
        out[kind].update(zip(SMALL_ORDER, _unpack_flat(packed[0], shapes)))
    newest[0] = res[0]
    update_big("a_w_in")

    return (loss, gx[None], *[out["grad"][n] for n in names], *[out["delta"][n] for n in names],
            *[out["new_m"][n] for n in names], *[out["new_v"][n] for n in names])
```

```python
import functools

import jax
import jax.numpy as jnp
import numpy as np
from jax import lax
from jax.experimental import pallas as pl
from jax.experimental.pallas import tpu as pltpu

F32 = jnp.float32
BF16 = jnp.bfloat16

N_DEV = 8
D_MODEL = 1024
HEAD_DIM = 128
A_HEADS = 6
A_WIDTH = A_HEADS * HEAD_DIM
CHUNK = 64
B_HEADS = 6
B_WIDTH = B_HEADS * HEAD_DIM
DILATIONS = (1, 4, 16)
SPAN = 128
N_GROUPS = 3
ROPE_THETA = 10000.0
MEM_TOKENS = 256
MEM_HEADS = 4
MEM_HEAD_DIM = 64
MEM_WIDTH = MEM_HEADS * MEM_HEAD_DIM
FFN_HIDDEN = 2816
EPS = 1e-6

ADAM_LR = 0.001
ADAM_B1 = 0.9
ADAM_B2 = 0.999
ADAM_EPS = 1e-08
ADAM_WD = 0.01
ADAM_STEP = 10

V7X_VMEM_LIMIT_BYTES = 56 * 1024 * 1024

NT_DIMS = (((1,), (1,)), ((), ()))
TN_DIMS = (((0,), (0,)), ((), ()))


def _cp(*sem):
    return pltpu.CompilerParams(dimension_semantics=sem, vmem_limit_bytes=V7X_VMEM_LIMIT_BYTES)


def _dot(a, b):
    return jnp.dot(a.astype(BF16), b.astype(BF16), preferred_element_type=F32)


def _dot_nt(a, b):
    return lax.dot_general(a.astype(BF16), b.astype(BF16), NT_DIMS, preferred_element_type=F32)


def _dot_tn(a, b):
    return lax.dot_general(a.astype(BF16), b.astype(BF16), TN_DIMS, preferred_element_type=F32)


def _dot3(m01, x):
    hi = x.astype(BF16)
    r1 = x - hi.astype(F32)
    mid = r1.astype(BF16)
    lo = (r1 - mid.astype(F32)).astype(BF16)
    d = functools.partial(jnp.dot, preferred_element_type=F32)
    return d(m01, hi) + d(m01, mid) + d(m01, lo)


def _sigmoid(x):
    return 0.5 * jnp.tanh(0.5 * x) + 0.5


def _full(shape):
    return pl.BlockSpec(shape, lambda *_: (0,) * len(shape))


def _dep(body, n_in, dep):
    if dep is None:
        return body, [], []

    def with_dep(*refs):
        return body(*refs[:n_in], *refs[n_in + 1:])

    return with_dep, [pl.BlockSpec(memory_space=pl.ANY)], [dep]


def _rms_matmul(x, g, w, *, tt, tn, wt, name, out_dtype=F32, dep=None, rotate=None):
    T, K = x.shape
    N = w.shape[0] if wt else w.shape[1]
    n_rot = 0 if rotate is None else rotate[0].shape[1] // HEAD_DIM
    extra_in = [] if rotate is None else list(rotate)

    def kernel_body(x_ref, g_ref, w_ref, *rest):
        y_ref, xn_ref = rest[len(extra_in)], rest[len(extra_in) + 1]
        xf = x_ref[...]
        r = lax.rsqrt(jnp.mean(xf * xf, axis=-1, keepdims=True) + EPS)
        xn = (xf * r * g_ref[...]).astype(BF16)
        xn_ref[...] = xn
        for j in range(N // tn):
            cols = slice(j * tn, (j + 1) * tn)
            y = _dot_nt(xn, w_ref[cols, :]) if wt else _dot(xn, w_ref[:, cols])
            y_ref[:, cols] = y.astype(out_dtype)
            for h in range(j * tn // HEAD_DIM, min((j + 1) * tn // HEAD_DIM, n_rot)):
                gw_ref, c_ref, s_ref, yr_ref = rest[0], rest[1], rest[2], rest[len(extra_in) + 2]
                sl = slice(h * HEAD_DIM, (h + 1) * HEAD_DIM)
                xhat, _ = _head_rms(y[:, h * HEAD_DIM - j * tn:(h + 1) * HEAD_DIM - j * tn])
                yr_ref[:, sl] = _rope(xhat * gw_ref[:, sl], c_ref[...], s_ref[...])

    tbl = pl.BlockSpec((tt, HEAD_DIM), lambda i: (i, 0))
    rot_specs = [] if rotate is None else [_full((1, n_rot * HEAD_DIM)), tbl, tbl]
    body, dep_specs, dep_args = _dep(kernel_body, 3 + len(extra_in), dep)
    return pl.pallas_call(
        body, grid=(T // tt,),
        in_specs=[pl.BlockSpec((tt, K), lambda i: (i, 0)), _full((1, K)), _full(w.shape)] + rot_specs + dep_specs,
        out_specs=[pl.BlockSpec((tt, N), lambda i: (i, 0)), pl.BlockSpec((tt, K), lambda i: (i, 0))]
        + ([] if rotate is None else [pl.BlockSpec((tt, n_rot * HEAD_DIM), lambda i: (i, 0))]),
        out_shape=[jax.ShapeDtypeStruct((T, N), out_dtype), jax.ShapeDtypeStruct((T, K), BF16)]
        + ([] if rotate is None else [jax.ShapeDtypeStruct((T, n_rot * HEAD_DIM), F32)]),
        compiler_params=_cp("parallel"), name=name)(x, g, w, *extra_in, *dep_args)


def _mm_res(res, a1, a2, w, *, tt, name):
    T, K1 = a1.shape
    K2 = a2.shape[1]
    N = w.shape[1]

    def body(r_ref, a1_ref, a2_ref, w_ref, o_ref):
        o_ref[...] = r_ref[...] + _dot(a1_ref[...], w_ref[:K1, :]) + _dot(a2_ref[...], w_ref[K1:, :])

    return pl.pallas_call(
        body, grid=(T // tt,),
        in_specs=[pl.BlockSpec((tt, N), lambda i: (i, 0)), pl.BlockSpec((tt, K1), lambda i: (i, 0)),
                  pl.BlockSpec((tt, K2), lambda i: (i, 0)), _full((K1 + K2, N))],
        out_specs=pl.BlockSpec((tt, N), lambda i: (i, 0)),
        out_shape=jax.ShapeDtypeStruct((T, N), F32),
        compiler_params=_cp("parallel"), name=name)(res, a1, a2, w)


def _swiglu_down(h, gu, wd, *, tt, name):
    T, D = h.shape
    Fh = wd.shape[0]

    def body(h_ref, gt_ref, up_ref, w_ref, o_ref):
        gt = gt_ref[...].astype(F32)
        act = gt * _sigmoid(gt) * up_ref[...].astype(F32)
        o_ref[...] = h_ref[...] + _dot(act, w_ref[...])

    return pl.pallas_call(
        body, grid=(T // tt,),
        in_specs=[pl.BlockSpec((tt, D), lambda i: (i, 0)), pl.BlockSpec((tt, Fh), lambda i: (i, 0)),
                  pl.BlockSpec((tt, Fh), lambda i: (i, 1)), _full((Fh, D))],
        out_specs=pl.BlockSpec((tt, D), lambda i: (i, 0)),
        out_shape=jax.ShapeDtypeStruct((T, D), F32),
        compiler_params=_cp("parallel"), name=name)(h, gu, gu, wd)


def _swiglu_down_loss(h, gu, wd, tgt, *, tt, name):
    T, D = h.shape
    Fh = wd.shape[0]

    def body(h_ref, gt_ref, up_ref, w_ref, t_ref, dy_ref, acc_ref):
        @pl.when(pl.program_id(0) == 0)
        def _():
            acc_ref[...] = jnp.zeros_like(acc_ref)

        gt = gt_ref[...].astype(F32)
        act = gt * _sigmoid(gt) * up_ref[...].astype(F32)
        e = h_ref[...] + _dot(act, w_ref[...]) - t_ref[...]
        dy_ref[...] = e * (1.0 / D)
        acc_ref[...] += jnp.sum(e * e, axis=0, keepdims=True)

    row = pl.BlockSpec((tt, D), lambda i: (i, 0))
    return pl.pallas_call(
        body, grid=(T // tt,),
        in_specs=[row, pl.BlockSpec((tt, Fh), lambda i: (i, 0)), pl.BlockSpec((tt, Fh), lambda i: (i, 1)), _full((Fh, D)), row],
        out_specs=[row, _full((1, D))],
        out_shape=[jax.ShapeDtypeStruct((T, D), F32), jax.ShapeDtypeStruct((1, D), F32)],
        compiler_params=_cp("arbitrary"), name=name)(h, gu, gu, wd, tgt)


SWIGLU_COLS = 256


def _swiglu_bwd(dh, gu, wd, *, tt, name):
    T, D = dh.shape
    Fh = wd.shape[0]
    last = T // tt - 1

    def body(dh_ref, gt_ref, up_ref, w_ref, dgu_ref, gw_ref, acc):
        @pl.when(pl.program_id(0) == 0)
        def _():
            acc[...] = jnp.zeros_like(acc)

        dh16 = dh_ref[...].astype(BF16)
        for c0 in range(0, Fh, SWIGLU_COLS):
            cols = slice(c0, c0 + SWIGLU_COLS)
            gt = gt_ref[:, cols].astype(F32)
            up = up_ref[:, cols].astype(F32)
            s = _sigmoid(gt)
            silu = gt * s
            dact = _dot_nt(dh16, w_ref[cols, :])
            acc[cols, :] += _dot_tn((silu * up).astype(BF16), dh16)
            dgu_ref[:, cols] = (dact * up * (s * (1.0 + gt * (1.0 - s)))).astype(BF16)
            dgu_ref[:, Fh + c0:Fh + c0 + SWIGLU_COLS] = (dact * silu).astype(BF16)

        @pl.when(pl.program_id(0) == last)
        def _():
            gw_ref[...] = acc[...].astype(BF16)

    return pl.pallas_call(
        body, grid=(T // tt,),
        in_specs=[pl.BlockSpec((tt, D), lambda i: (i, 0)), pl.BlockSpec((tt, Fh), lambda i: (i, 0)),
                  pl.BlockSpec((tt, Fh), lambda i: (i, 1)), _full((Fh, D))],
        out_specs=[pl.BlockSpec((tt, 2 * Fh), lambda i: (i, 0)), _full((Fh, D))],
        out_shape=[jax.ShapeDtypeStruct((T, 2 * Fh), BF16), jax.ShapeDtypeStruct((Fh, D), BF16)],
        scratch_shapes=[pltpu.VMEM((Fh, D), F32)],
        compiler_params=_cp("arbitrary"), name=name)(dh, gu, gu, wd)


def _out_proj_bwd(dy, a1, a2, w, *, tt, name, head_dots=False):
    T, N = dy.shape
    K1, K2 = a1.shape[1], a2.shape[1]
    K = K1 + K2
    last = T // tt - 1

    def body(dy_ref, a1_ref, a2_ref, w_ref, da_ref, gw_ref, *rest):
        acc = rest[-1]

        @pl.when(pl.program_id(0) == 0)
        def _():
            acc[...] = jnp.zeros_like(acc)

        dy16 = dy_ref[...].astype(BF16)
        da = _dot_nt(dy16, w_ref[...])
        da_ref[...] = da
        acc[:K1, :] += _dot_tn(a1_ref[...], dy16)
        acc[K1:, :] += _dot_tn(a2_ref[...], dy16)
        if head_dots:
            for h in range(K1 // HEAD_DIM):
                sl = slice(h * HEAD_DIM, (h + 1) * HEAD_DIM)
                rest[0][:, sl] = jnp.broadcast_to(jnp.sum(da[:, sl] * a1_ref[:, sl], axis=-1, keepdims=True), (tt, HEAD_DIM))

        @pl.when(pl.program_id(0) == last)
        def _():
            gw_ref[...] = acc[...].astype(BF16)

    extra_specs = [pl.BlockSpec((tt, K1), lambda i: (i, 0))] if head_dots else []
    extra_shapes = [jax.ShapeDtypeStruct((T, K1), F32)] if head_dots else []
    return pl.pallas_call(
        body, grid=(T // tt,),
        in_specs=[pl.BlockSpec((tt, N), lambda i: (i, 0)), pl.BlockSpec((tt, K1), lambda i: (i, 0)),
                  pl.BlockSpec((tt, K2), lambda i: (i, 0)), _full((K, N))],
        out_specs=[pl.BlockSpec((tt, K), lambda i: (i, 0)), _full((K, N))] + extra_specs,
        out_shape=[jax.ShapeDtypeStruct((T, K), F32), jax.ShapeDtypeStruct((K, N), BF16)] + extra_shapes,
        scratch_shapes=[pltpu.VMEM((K, N), F32)],
        compiler_params=_cp("arbitrary"), name=name)(dy, a1, a2, w)


def _mm_tn(a, b, *, tt, tka, name):
    T, Ka = a.shape
    N = b.shape[1]
    last = T // tt - 1

    def body(a_ref, b_ref, o_ref, acc):
        @pl.when(pl.program_id(1) == 0)
        def _():
            acc[...] = jnp.zeros_like(acc)

        acc[...] += _dot_tn(a_ref[...], b_ref[...])

        @pl.when(pl.program_id(1) == last)
        def _():
            o_ref[...] = acc[...].astype(BF16)

    return pl.pallas_call(
        body, grid=(Ka // tka, T // tt),
        in_specs=[pl.BlockSpec((tt, tka), lambda j, t: (t, j)), pl.BlockSpec((tt, N), lambda j, t: (t, 0))],
        out_specs=pl.BlockSpec((tka, N), lambda j, t: (j, 0)),
        out_shape=jax.ShapeDtypeStruct((Ka, N), BF16),
        scratch_shapes=[pltpu.VMEM((tka, N), F32)],
        compiler_params=_cp("parallel", "arbitrary"), name=name)(a, b)


def _mm_tn_pieces(pieces, b, *, tt, name):
    n = len(pieces)
    T = b.shape[0]
    N = b.shape[1]
    widths = [p.shape[1] for p in pieces]
    Ka = sum(widths)
    last = T // tt - 1

    def body(*refs):
        p_refs = refs[:n]
        b_ref, o_ref, acc = refs[n:]

        @pl.when(pl.program_id(0) == 0)
        def _():
            acc[...] = jnp.zeros_like(acc)

        bv = b_ref[...].astype(BF16)
        off = 0
        for p_ref, wd in zip(p_refs, widths):
            acc[off:off + wd, :] += _dot_tn(p_ref[...], bv)
            off += wd

        @pl.when(pl.program_id(0) == last)
        def _():
            o_ref[...] = acc[...].astype(BF16)

    return pl.pallas_call(
        body, grid=(T // tt,),
        in_specs=[pl.BlockSpec((tt, wd), lambda t: (t, 0)) for wd in widths] + [pl.BlockSpec((tt, N), lambda t: (t, 0))],
        out_specs=_full((Ka, N)), out_shape=jax.ShapeDtypeStruct((Ka, N), BF16),
        scratch_shapes=[pltpu.VMEM((Ka, N), F32)],
        compiler_params=_cp("arbitrary"), name=name)(*pieces, b)


def _rms_bwd_dx(x, g, w, dy, dres, *, tt, wt, name, dep=None):
    pieces = list(dy) if isinstance(dy, (list, tuple)) else [dy]
    n = len(pieces)
    widths = [p.shape[1] for p in pieces]
    T, K = x.shape

    def kernel_body(x_ref, g_ref, w_ref, *rest):
        dy_refs = rest[:n]
        dres_ref, dx_ref, dg_ref = rest[n:]

        @pl.when(pl.program_id(0) == 0)
        def _():
            dg_ref[...] = jnp.zeros_like(dg_ref)

        if n == 1:
            dxn = (_dot if wt else _dot_nt)(dy_refs[0][...], w_ref[...])
        else:
            dxn, off = 0.0, 0
            for dy_ref, wd in zip(dy_refs, widths):
                dxn = dxn + _dot(dy_ref[...], w_ref[off:off + wd, :])
                off += wd
        xf = x_ref[...]
        r = lax.rsqrt(jnp.mean(xf * xf, axis=-1, keepdims=True) + EPS)
        xhat = xf * r
        dg_ref[...] += jnp.sum(dxn * xhat, axis=0, keepdims=True)
        dxhat = dxn * g_ref[...]
        dx_ref[...] = dres_ref[...] + r * (dxhat - xhat * jnp.mean(dxhat * xhat, axis=-1, keepdims=True))

    assert n == 1 or wt
    body, dep_specs, dep_args = _dep(kernel_body, 4 + n, dep)
    return pl.pallas_call(
        body, grid=(T // tt,),
        in_specs=[pl.BlockSpec((tt, K), lambda i: (i, 0)), _full((1, K)), _full(w.shape)]
        + [pl.BlockSpec((tt, wd), lambda i: (i, 0)) for wd in widths]
        + [pl.BlockSpec((tt, K), lambda i: (i, 0))] + dep_specs,
        out_specs=[pl.BlockSpec((tt, K), lambda i: (i, 0)), _full((1, K))],
        out_shape=[jax.ShapeDtypeStruct((T, K), F32), jax.ShapeDtypeStruct((1, K), F32)],
        compiler_params=_cp("arbitrary"), name=name)(x, g, w, *pieces, dres, *dep_args)


HGRN_TB = 512
HGRN_NCH = HGRN_TB // CHUNK
HGRN_HPB = 6


def _hgrn_chunk_fwd(q, z, lbv, tril01):
    sig = _sigmoid(z)
    f = lbv + (1.0 - lbv) * sig
    kk = 1.0 - f
    b = _dot3(tril01, jnp.log(f))
    bend = b[CHUNK - 1:CHUNK, :]
    sq = _sigmoid(q)
    eb = jnp.exp(b)
    emb = jnp.exp(-b)
    eo = jnp.exp(bend - b)
    dec = jnp.exp(bend)
    return sig, f, kk, sq, eb, emb, eo, dec


def _hgrn2_fwd(proj, lb, *, name):
    T = proj.shape[0]
    nT = T // HGRN_TB
    nC = T // CHUNK

    def body(q_ref, z_ref, v_ref, lb_ref, o_ref, st_ref, state):
        @pl.when(pl.program_id(1) == 0)
        def _():
            state[...] = jnp.zeros_like(state)

        row = lax.broadcasted_iota(jnp.int32, (CHUNK, CHUNK), 0)
        col = lax.broadcasted_iota(jnp.int32, (CHUNK, CHUNK), 1)
        causal = row >= col
        tril01 = causal.astype(BF16)

        def chunk(c, carry):
            rows = pl.ds(pl.multiple_of(c * CHUNK, CHUNK), CHUNK)
            for hh in range(HGRN_HPB):
                sl = slice(hh * HEAD_DIM, (hh + 1) * HEAD_DIM)
                q = q_ref[rows, sl]
                v = v_ref[rows, sl].astype(BF16)
                sig, f, kk, sq, eb, emb, eo, dec = _hgrn_chunk_fwd(q, z_ref[rows, sl], lb_ref[:, sl], tril01)
                qi = (q * sq * eb).astype(BF16)
                ki = (kk * emb).astype(BF16)
                ko = (kk * eo).astype(BF16)
                st = state[hh]
                att = jnp.where(causal, _dot_nt(qi, ki), 0.0)
                o_ref[rows, sl] = _dot(att, v) + _dot_nt(qi, st)
                st_ref[c, hh] = st
                state[hh] = st * dec + _dot_tn(v, ko)
            return carry

        lax.fori_loop(0, HGRN_NCH, chunk, 0)

    W = HGRN_HPB * HEAD_DIM
    nG = A_HEADS // HGRN_HPB
    hb = lambda off: pl.BlockSpec((HGRN_TB, W), lambda h, i: (i, off + h))
    return pl.pallas_call(
        body, grid=(nG, nT),
        in_specs=[hb(0), hb(nG), hb(2 * nG), pl.BlockSpec((1, W), lambda h, i: (0, h))],
        out_specs=[hb(0), pl.BlockSpec((HGRN_NCH, HGRN_HPB, HEAD_DIM, HEAD_DIM), lambda h, i: (i, h, 0, 0))],
        out_shape=[jax.ShapeDtypeStruct((T, A_WIDTH), F32), jax.ShapeDtypeStruct((nC, A_HEADS, HEAD_DIM, HEAD_DIM), F32)],
        scratch_shapes=[pltpu.VMEM((HGRN_HPB, HEAD_DIM, HEAD_DIM), F32)],
        compiler_params=_cp("parallel", "arbitrary"), name=name)(proj, proj, proj, lb)


def _hgrn2_bwd(proj, lb, st_all, do, *, name):
    T = proj.shape[0]
    nT = T // HGRN_TB

    def body(q_ref, z_ref, v_ref, lb_ref, st_ref, do_ref, dq_ref, dz_ref, dv_ref, dlb_ref, dstate):
        @pl.when(pl.program_id(1) == 0)
        def _():
            dstate[...] = jnp.zeros_like(dstate)
            dlb_ref[...] = jnp.zeros_like(dlb_ref)

        row = lax.broadcasted_iota(jnp.int32, (CHUNK, CHUNK), 0)
        col = lax.broadcasted_iota(jnp.int32, (CHUNK, CHUNK), 1)
        causal = row >= col
        tril01 = causal.astype(BF16)
        triu01 = (row <= col).astype(BF16)

        def chunk(cc, carry):
            c = HGRN_NCH - 1 - cc
            rows = pl.ds(pl.multiple_of(c * CHUNK, CHUNK), CHUNK)
            for hh in range(HGRN_HPB):
                sl = slice(hh * HEAD_DIM, (hh + 1) * HEAD_DIM)
                lbv = lb_ref[:, sl]
                q = q_ref[rows, sl]
                v = v_ref[rows, sl].astype(BF16)
                sig, f, kk, sq, eb, emb, eo, dec = _hgrn_chunk_fwd(q, z_ref[rows, sl], lbv, tril01)
                qi32 = q * sq * eb
                ki32 = kk * emb
                ko32 = kk * eo
                qi, ki, ko = qi32.astype(BF16), ki32.astype(BF16), ko32.astype(BF16)
                att = jnp.where(causal, _dot_nt(qi, ki), 0.0).astype(BF16)
                dout = do_ref[rows, sl].astype(BF16)
                st = st_ref[c, hh]
                dst = dstate[hh]
                dst16 = dst.astype(BF16)
                datt = jnp.where(causal, _dot_nt(dout, v), 0.0).astype(BF16)
                dqi = _dot(datt, ki) + _dot(dout, st)
                dki = _dot_tn(datt, qi)
                dv_ref[rows, sl] = (_dot_tn(att, dout) + _dot_nt(ko, dst16)).astype(BF16)
                dko = _dot(v, dst16)
                ddec = jnp.sum(dst * st, axis=0, keepdims=True)
                dstate[hh] = dst * dec + _dot_tn(dout, qi)
                dkk = dki * emb + dko * eo
                db = dqi * qi32 - dki * ki32 - dko * ko32
                dbend = jnp.sum(dko * ko32, axis=0, keepdims=True) + ddec * dec
                dlogf = _dot3(triu01, db) + dbend
                df = dlogf / f - dkk
                dz_ref[rows, sl] = (df * (1.0 - lbv) * sig * (1.0 - sig)).astype(BF16)
                dlb_ref[:, sl] += jnp.sum(df * (1.0 - sig), axis=0, keepdims=True)
                dq_ref[rows, sl] = (dqi * eb * (sq * (1.0 + q * (1.0 - sq)))).astype(BF16)
            return carry

        lax.fori_loop(0, HGRN_NCH, chunk, 0)

    W = HGRN_HPB * HEAD_DIM
    nG = A_HEADS // HGRN_HPB
    hb = lambda off: pl.BlockSpec((HGRN_TB, W), lambda h, i: (nT - 1 - i, off + h))
    hlb = pl.BlockSpec((1, W), lambda h, i: (0, h))
    o16 = jax.ShapeDtypeStruct((T, A_WIDTH), BF16)
    return pl.pallas_call(
        body, grid=(nG, nT),
        in_specs=[hb(0), hb(nG), hb(2 * nG), hlb,
                  pl.BlockSpec((HGRN_NCH, HGRN_HPB, HEAD_DIM, HEAD_DIM), lambda h, i: (nT - 1 - i, h, 0, 0)), hb(0)],
        out_specs=[hb(0), hb(0), hb(0), hlb],
        out_shape=[o16, o16, o16, jax.ShapeDtypeStruct((1, A_WIDTH), F32)],
        scratch_shapes=[pltpu.VMEM((HGRN_HPB, HEAD_DIM, HEAD_DIM), F32)],
        compiler_params=_cp("parallel", "arbitrary"), name=name)(proj, proj, proj, lb, st_all, do)


def _head_rms(x):
    r = lax.rsqrt(jnp.mean(x * x, axis=-1, keepdims=True) + EPS)
    return x * r, r


def _head_rms_bwd(dxhat, xhat, r):
    return r * (dxhat - xhat * jnp.mean(dxhat * xhat, axis=-1, keepdims=True))


def _a_post_fwd(o, proj, onorm, *, tt, name):
    T = o.shape[0]

    def body(o_ref, g_ref, w_ref, y_ref):
        for h in range(A_HEADS):
            sl = slice(h * HEAD_DIM, (h + 1) * HEAD_DIM)
            xhat, _ = _head_rms(o_ref[:, sl])
            g = g_ref[:, sl]
            y_ref[:, sl] = xhat * w_ref[:, sl] * (g * _sigmoid(g))

    blk = lambda c: pl.BlockSpec((tt, A_WIDTH), lambda i: (i, c))
    return pl.pallas_call(
        body, grid=(T // tt,), in_specs=[blk(0), blk(3), _full((1, A_WIDTH))], out_specs=blk(0),
        out_shape=jax.ShapeDtypeStruct((T, A_WIDTH), F32),
        compiler_params=_cp("parallel"), name=name)(o, proj, onorm)


def _a_post_bwd(o, proj, onorm, dmix, *, tt, name, dep=None):
    T = o.shape[0]

    def kernel_body(o_ref, g_ref, w_ref, dy_ref, do_ref, dg_ref, dw_ref):
        @pl.when(pl.program_id(0) == 0)
        def _():
            dw_ref[...] = jnp.zeros_like(dw_ref)

        for h in range(A_HEADS):
            sl = slice(h * HEAD_DIM, (h + 1) * HEAD_DIM)
            xhat, r = _head_rms(o_ref[:, sl])
            g = g_ref[:, sl]
            s = _sigmoid(g)
            dy = dy_ref[:, sl]
            w = w_ref[:, sl]
            dg_ref[:, sl] = (dy * xhat * w * (s * (1.0 + g * (1.0 - s)))).astype(BF16)
            dyn = dy * (g * s)
            dw_ref[:, sl] += jnp.sum(dyn * xhat, axis=0, keepdims=True)
            do_ref[:, sl] = _head_rms_bwd(dyn * w, xhat, r)

    blk = lambda c: pl.BlockSpec((tt, A_WIDTH), lambda i: (i, c))
    body, dep_specs, dep_args = _dep(kernel_body, 4, dep)
    return pl.pallas_call(
        body, grid=(T // tt,), in_specs=[blk(0), blk(3), _full((1, A_WIDTH)), blk(0)] + dep_specs,
        out_specs=[blk(0), blk(0), _full((1, A_WIDTH))],
        out_shape=[jax.ShapeDtypeStruct((T, A_WIDTH), F32), jax.ShapeDtypeStruct((T, A_WIDTH), BF16),
                   jax.ShapeDtypeStruct((1, A_WIDTH), F32)],
        compiler_params=_cp("arbitrary"), name=name)(o, proj, onorm, dmix, *dep_args)


def _mem_head_masks(n):
    lane = lax.broadcasted_iota(jnp.int32, (n, MEM_WIDTH), 1)
    return [(lane >= m * MEM_HEAD_DIM) & (lane < (m + 1) * MEM_HEAD_DIM) for m in range(MEM_HEADS)]


def _mem_head_rms(x, masks):
    x2 = x * x
    r = jnp.zeros_like(x)
    for mk in masks:
        ms = jnp.sum(jnp.where(mk, x2, 0.0), axis=-1, keepdims=True) * (1.0 / MEM_HEAD_DIM)
        r = jnp.where(mk, lax.rsqrt(ms + EPS), r)
    return x * r, r


def _mem_head_rms_bwd(dxhat, xhat, r, masks):
    t = dxhat * xhat
    m = jnp.zeros_like(t)
    for mk in masks:
        m = jnp.where(mk, jnp.sum(jnp.where(mk, t, 0.0), axis=-1, keepdims=True) * (1.0 / MEM_HEAD_DIM), m)
    return r * (dxhat - xhat * m)


MEM_SCALE = MEM_HEAD_DIM ** -0.5


def _mem_attn_fwd(proj, qcol, mkv, qn_w, kn_w, *, tt, name):
    T = proj.shape[0]

    def body(q_ref, k_ref, v_ref, qw_ref, kw_ref, o_ref):
        qmasks = _mem_head_masks(tt)
        kmasks = _mem_head_masks(MEM_TOKENS)
        qhat, _ = _mem_head_rms(q_ref[...], qmasks)
        qn = qhat * qw_ref[...]
        khat, _ = _mem_head_rms(k_ref[...], kmasks)
        kn = (khat * kw_ref[...]).astype(BF16)
        v = v_ref[...].astype(BF16)
        out = jnp.zeros((tt, MEM_WIDTH), F32)
        for m in range(MEM_HEADS):
            s = _dot_nt(jnp.where(qmasks[m], qn, 0.0), kn) * MEM_SCALE
            s = s - jnp.max(s, axis=-1, keepdims=True)
            p = jnp.exp(s)
            p = p / jnp.sum(p, axis=-1, keepdims=True)
            out = jnp.where(qmasks[m], _dot(p, v), out)
        o_ref[...] = out

    return pl.pallas_call(
        body, grid=(T // tt,),
        in_specs=[pl.BlockSpec((tt, MEM_WIDTH), lambda i: (i, qcol)), pl.BlockSpec((MEM_TOKENS, MEM_WIDTH), lambda i: (0, 0)),
                  pl.BlockSpec((MEM_TOKENS, MEM_WIDTH), lambda i: (0, 1)), _full((1, MEM_WIDTH)), _full((1, MEM_WIDTH))],
        out_specs=pl.BlockSpec((tt, MEM_WIDTH), lambda i: (i, 0)),
        out_shape=jax.ShapeDtypeStruct((T, MEM_WIDTH), F32),
        compiler_params=_cp("parallel"), name=name)(proj, mkv, mkv, qn_w, kn_w)


def _mem_attn_bwd(proj, qcol, mkv, qn_w, kn_w, dmix, *, tt, name):
    T = proj.shape[0]
    nsteps = T // tt
    ocol = (dmix.shape[1] - MEM_WIDTH) // MEM_WIDTH

    def body(q_ref, k_ref, v_ref, qw_ref, kw_ref, do_ref, dq_ref, dkv_ref, dqw_ref, dkw_ref, dk_acc, dv_acc):
        step = pl.program_id(0)

        @pl.when(step == 0)
        def _():
            dk_acc[...] = jnp.zeros_like(dk_acc)
            dv_acc[...] = jnp.zeros_like(dv_acc)
            dqw_ref[...] = jnp.zeros_like(dqw_ref)

        qmasks = _mem_head_masks(tt)
        kmasks = _mem_head_masks(MEM_TOKENS)
        qhat, qr = _mem_head_rms(q_ref[...], qmasks)
        qn = qhat * qw_ref[...]
        khat, kr = _mem_head_rms(k_ref[...], kmasks)
        kn = (khat * kw_ref[...]).astype(BF16)
        v = v_ref[...].astype(BF16)
        dout = do_ref[...]
        dqn = jnp.zeros((tt, MEM_WIDTH), F32)
        dkn = jnp.zeros((MEM_TOKENS, MEM_WIDTH), F32)
        dvv = jnp.zeros((MEM_TOKENS, MEM_WIDTH), F32)
        for m in range(MEM_HEADS):
            qm = jnp.where(qmasks[m], qn, 0.0).astype(BF16)
            s = _dot_nt(qm, kn) * MEM_SCALE
            s = s - jnp.max(s, axis=-1, keepdims=True)
            p = jnp.exp(s)
            p = p / jnp.sum(p, axis=-1, keepdims=True)
            dom = jnp.where(qmasks[m], dout, 0.0).astype(BF16)
            dp = _dot_nt(dom, v)
            ds = (p * (dp - jnp.sum(p * dp, axis=-1, keepdims=True)) * MEM_SCALE).astype(BF16)
            dqn = jnp.where(qmasks[m], _dot(ds, kn), dqn)
            dkn = jnp.where(kmasks[m], _dot_tn(ds, qm), dkn)
            dvv = jnp.where(kmasks[m], _dot_tn(p, dom), dvv)
        dqw_ref[...] += jnp.sum(dqn * qhat, axis=0, keepdims=True)
        dq_ref[...] = _mem_head_rms_bwd(dqn * qw_ref[...], qhat, qr, qmasks).astype(BF16)
        dk_acc[...] += dkn
        dv_acc[...] += dvv

        @pl.when(step == nsteps - 1)
        def _():
            dk = dk_acc[...]
            dkw_ref[...] = jnp.sum(dk * khat, axis=0, keepdims=True)
            dkv_ref[:, :MEM_WIDTH] = _mem_head_rms_bwd(dk * kw_ref[...], khat, kr, kmasks)
            dkv_ref[:, MEM_WIDTH:] = dv_acc[...]

    return pl.pallas_call(
        body, grid=(nsteps,),
        in_specs=[pl.BlockSpec((tt, MEM_WIDTH), lambda i: (i, qcol)), pl.BlockSpec((MEM_TOKENS, MEM_WIDTH), lambda i: (0, 0)),
                  pl.BlockSpec((MEM_TOKENS, MEM_WIDTH), lambda i: (0, 1)), _full((1, MEM_WIDTH)), _full((1, MEM_WIDTH)),
                  pl.BlockSpec((tt, MEM_WIDTH), lambda i: (i, ocol))],
        out_specs=[pl.BlockSpec((tt, MEM_WIDTH), lambda i: (i, 0)), _full((MEM_TOKENS, 2 * MEM_WIDTH)),
                   _full((1, MEM_WIDTH)), _full((1, MEM_WIDTH))],
        out_shape=[jax.ShapeDtypeStruct((T, MEM_WIDTH), BF16), jax.ShapeDtypeStruct((MEM_TOKENS, 2 * MEM_WIDTH), F32),
                   jax.ShapeDtypeStruct((1, MEM_WIDTH), F32), jax.ShapeDtypeStruct((1, MEM_WIDTH), F32)],
        scratch_shapes=[pltpu.VMEM((MEM_TOKENS, MEM_WIDTH), F32), pltpu.VMEM((MEM_TOKENS, MEM_WIDTH), F32)],
        compiler_params=_cp("arbitrary"), name=name)(proj, mkv, mkv, qn_w, kn_w, dmix)


HALF = HEAD_DIM // 2
ATT_SCALE = HEAD_DIM ** -0.5
NEG = -1e30


def _rope_tables(T):
    inv = np.float32(ROPE_THETA) ** (-np.arange(HALF, dtype=np.float32) / np.float32(HALF))
    ang = np.arange(T, dtype=np.float32)[:, None] * inv[None, :].astype(np.float32)
    cos, sin = np.cos(ang).astype(np.float32), np.sin(ang).astype(np.float32)
    return jnp.asarray(np.concatenate([cos, cos], axis=-1)), jnp.asarray(np.concatenate([-sin, sin], axis=-1))


def _rope(x, cosf, sinsg):
    return x * cosf + pltpu.roll(x, HALF, 1) * sinsg


def _rope_bwd(dy, cosf, sinsg):
    return dy * cosf + pltpu.roll(dy * sinsg, HALF, 1)


def _q_prep_bwd(proj, w_heads, cosf, sinsg, dqs, *, tt, name):
    T = proj.shape[0]
    W = N_GROUPS * B_WIDTH

    def body(x_ref, w_ref, c_ref, s_ref, d0, d1, d2, dx_ref, dw_ref):
        @pl.when(pl.program_id(0) == 0)
        def _():
            dw_ref[...] = jnp.zeros_like(dw_ref)

        c, s = c_ref[...], s_ref[...]
        for gi, d_ref in enumerate((d0, d1, d2)):
            for h in range(B_HEADS):
                sl = slice((gi * B_HEADS + h) * HEAD_DIM, (gi * B_HEADS + h + 1) * HEAD_DIM)
                xhat, r = _head_rms(x_ref[:, sl])
                dyn = _rope_bwd(d_ref[:, h * HEAD_DIM:(h + 1) * HEAD_DIM], c, s)
                dw_ref[:, sl] += jnp.sum(dyn * xhat, axis=0, keepdims=True)
                dx_ref[:, sl] = _head_rms_bwd(dyn * w_ref[:, sl], xhat, r).astype(BF16)

    tbl = pl.BlockSpec((tt, HEAD_DIM), lambda i: (i, 0))
    dyb = pl.BlockSpec((tt, B_WIDTH), lambda i: (i, 0))
    return pl.pallas_call(
        body, grid=(T // tt,),
        in_specs=[pl.BlockSpec((tt, W), lambda i: (i, 0)), _full((1, W)), tbl, tbl, dyb, dyb, dyb],
        out_specs=[pl.BlockSpec((tt, W), lambda i: (i, 0)), _full((1, W))],
        out_shape=[jax.ShapeDtypeStruct((T, W), BF16), jax.ShapeDtypeStruct((1, W), F32)],
        compiler_params=_cp("arbitrary"), name=name)(proj, w_heads, cosf, sinsg, *dqs)


def _kv_prep_bwd(kv, w_heads, cosf, sinsg, dks, dvs, *, tt, name):
    T = kv.shape[0]

    def body(x_ref, w_ref, c_ref, s_ref, k0, k1, k2, v0, v1, v2, dx_ref, dw_ref):
        @pl.when(pl.program_id(0) == 0)
        def _():
            dw_ref[...] = jnp.zeros_like(dw_ref)

        c, s = c_ref[...], s_ref[...]
        for h in range(B_HEADS):
            sl = slice(h * HEAD_DIM, (h + 1) * HEAD_DIM)
            vs = slice(B_WIDTH + h * HEAD_DIM, B_WIDTH + (h + 1) * HEAD_DIM)
            xhat, r = _head_rms(x_ref[:, sl])
            dyn = _rope_bwd(k0[:, sl] + k1[:, sl] + k2[:, sl], c, s)
            dw_ref[:, sl] += jnp.sum(dyn * xhat, axis=0, keepdims=True)
            dx_ref[:, sl] = _head_rms_bwd(dyn * w_ref[:, sl], xhat, r).astype(BF16)
            dx_ref[:, vs] = (v0[:, sl] + v1[:, sl] + v2[:, sl]).astype(BF16)

    tbl = pl.BlockSpec((tt, HEAD_DIM), lambda i: (i, 0))
    dyb = pl.BlockSpec((tt, B_WIDTH), lambda i: (i, 0))
    return pl.pallas_call(
        body, grid=(T // tt,),
        in_specs=[dyb, _full((1, B_WIDTH)), tbl, tbl] + [dyb] * 6,
        out_specs=[pl.BlockSpec((tt, 2 * B_WIDTH), lambda i: (i, 0)), _full((1, B_WIDTH))],
        out_shape=[jax.ShapeDtypeStruct((T, 2 * B_WIDTH), BF16), jax.ShapeDtypeStruct((1, B_WIDTH), F32)],
        compiler_params=_cp("arbitrary"), name=name)(kv, w_heads, cosf, sinsg, *dks, *dvs)


def _band_masks(n_is_first=None):
    row = lax.broadcasted_iota(jnp.int32, (SPAN, SPAN), 0)
    col = lax.broadcasted_iota(jnp.int32, (SPAN, SPAN), 1)
    return row >= col, col >= row


def _dil_views(T, d):
    L = T // d
    return L, L // SPAN


def _dil_fwd(qr, kr, kv, gi, d, *, name):
    T = qr.shape[0]
    L, nb = _dil_views(T, d)

    def body(q_ref, kc_ref, kp_ref, vc_ref, vp_ref, o_ref, lse_ref):
        cur_ok, prev_band = _band_masks()
        prev_ok = prev_band & (pl.program_id(1) > 0)
        for h in range(B_HEADS):
            sl = slice(h * HEAD_DIM, (h + 1) * HEAD_DIM)
            q = q_ref[:, sl]
            sc = jnp.where(cur_ok, _dot_nt(q, kc_ref[:, sl]) * ATT_SCALE, NEG)
            sp = jnp.where(prev_ok, _dot_nt(q, kp_ref[:, sl]) * ATT_SCALE, NEG)
            m = jnp.maximum(jnp.max(sc, axis=-1, keepdims=True), jnp.max(sp, axis=-1, keepdims=True))
            pc = jnp.exp(sc - m)
            pp = jnp.exp(sp - m)
            l = jnp.sum(pc, axis=-1, keepdims=True) + jnp.sum(pp, axis=-1, keepdims=True)
            o_ref[:, sl] = (_dot(pc, vc_ref[:, sl]) + _dot(pp, vp_ref[:, sl])) / l
            lse_ref[:, sl] = jnp.broadcast_to(m + jnp.log(l), (SPAN, HEAD_DIM))

    blk = lambda f: pl.BlockSpec((SPAN, B_WIDTH), f)
    cur = lambda r, n: (n, r)
    prev = lambda r, n: (jnp.maximum(n - 1, 0), r)
    ov = jax.ShapeDtypeStruct((L, d * B_WIDTH), F32)
    o, lse = pl.pallas_call(
        body, grid=(d, nb),
        in_specs=[blk(lambda r, n: (n, r * N_GROUPS + gi)), blk(cur), blk(prev),
                  blk(lambda r, n: (n, 2 * r + 1)), blk(lambda r, n: (jnp.maximum(n - 1, 0), 2 * r + 1))],
        out_specs=[blk(cur), blk(cur)], out_shape=[ov, ov],
        compiler_params=_cp("parallel", "arbitrary"), name=name,
    )(qr.reshape(L, d * N_GROUPS * B_WIDTH), kr.reshape(L, d * B_WIDTH), kr.reshape(L, d * B_WIDTH),
      kv.reshape(L, d * 2 * B_WIDTH), kv.reshape(L, d * 2 * B_WIDTH))
    return o.reshape(T, B_WIDTH), lse.reshape(T, B_WIDTH)


def _dil_combine_fwd(os_, lses, *, tt, name):
    T = os_[0].shape[0]

    def body(o0, o1, o2, l0, l1, l2, y_ref, lse_ref):
        a, b, c = l0[...], l1[...], l2[...]
        m = jnp.maximum(jnp.maximum(a, b), c)
        wa, wb, wc = jnp.exp(a - m), jnp.exp(b - m), jnp.exp(c - m)
        den = wa + wb + wc
        y_ref[...] = (wa * o0[...] + wb * o1[...] + wc * o2[...]) / den
        lse_ref[...] = m + jnp.log(den)

    blk = pl.BlockSpec((tt, B_WIDTH), lambda i: (i, 0))
    sh = jax.ShapeDtypeStruct((T, B_WIDTH), F32)
    return pl.pallas_call(
        body, grid=(T // tt,), in_specs=[blk] * 6, out_specs=[blk, blk], out_shape=[sh, sh],
        compiler_params=_cp("parallel"), name=name)(*os_, *lses)


DILS_UNROLL = 4


def _dils_specs(gi, d, nblk):
    blk = lambda f: pl.BlockSpec((SPAN * d, HEAD_DIM), f)
    return {
        "q": blk(lambda h, n: (n, gi * B_HEADS + h)), "q_next": blk(lambda h, n: (jnp.minimum(n + 1, nblk - 1), gi * B_HEADS + h)),
        "cur": blk(lambda h, n: (n, h)), "prev": blk(lambda h, n: (jnp.maximum(n - 1, 0), h)),
        "next": blk(lambda h, n: (jnp.minimum(n + 1, nblk - 1), h)),
        "v": blk(lambda h, n: (n, B_HEADS + h)), "v_prev": blk(lambda h, n: (jnp.maximum(n - 1, 0), B_HEADS + h)),
    }


def _dils_fwd(qr, kr, kv, gi, d, *, name):
    T = qr.shape[0]
    nblk = T // (SPAN * d)
    sp = _dils_specs(gi, d, nblk)

    def body(q_ref, kc_ref, vc_ref, o_ref, lse_ref, k_before, v_before):
        @pl.when(pl.program_id(1) == 0)
        def _():
            k_before[...] = jnp.zeros_like(k_before)
            v_before[...] = jnp.zeros_like(v_before)

        cur_ok, prev_band = _band_masks()
        prev_ok = prev_band & (pl.program_id(1) > 0)

        def residue(r, carry):
            rows = pl.ds(r, SPAN, stride=d)
            q, kc, vc = q_ref[rows, :], kc_ref[rows, :].astype(BF16), vc_ref[rows, :].astype(BF16)
            sc = jnp.where(cur_ok, _dot_nt(q, kc) * ATT_SCALE, NEG)
            sp_ = jnp.where(prev_ok, _dot_nt(q, k_before[r]) * ATT_SCALE, NEG)
            m = jnp.maximum(jnp.max(sc, axis=-1, keepdims=True), jnp.max(sp_, axis=-1, keepdims=True))
            pc = jnp.exp(sc - m)
            pp = jnp.exp(sp_ - m)
            l = jnp.sum(pc, axis=-1, keepdims=True) + jnp.sum(pp, axis=-1, keepdims=True)
            o_ref[rows, :] = (_dot(pc, vc) + _dot(pp, v_before[r])) / l
            lse_ref[rows, :] = jnp.broadcast_to(m + jnp.log(l), (SPAN, HEAD_DIM))
            k_before[r] = kc
            v_before[r] = vc
            return carry

        lax.fori_loop(0, d, residue, 0, unroll=DILS_UNROLL)

    sh = jax.ShapeDtypeStruct((T, B_WIDTH), F32)
    return pl.pallas_call(
        body, grid=(B_HEADS, nblk), in_specs=[sp["q"], sp["cur"], sp["v"]],
        out_specs=[sp["cur"], sp["cur"]], out_shape=[sh, sh],
        scratch_shapes=[pltpu.VMEM((d, SPAN, HEAD_DIM), BF16), pltpu.VMEM((d, SPAN, HEAD_DIM), BF16)],
        compiler_params=_cp("parallel", "arbitrary"), name=name)(qr, kr, kv)


DIL_BWD_GROUP = {1: 4, 4: 1, 16: 1}


def _dil_bwd(qr, kr, kv, dmix, lse, dd, gi, d, *, name, dep=None):
    T = qr.shape[0]
    G = DIL_BWD_GROUP[d]
    band = SPAN * d
    tb = G * band
    nblk = T // tb
    n_units = T // SPAN

    keep = G == 1

    def kernel_body(q_ref, dy_ref, lse_ref, dd_ref, kc_ref, vc_ref, *rest):
        if keep:
            dq_ref, dk_ref, dv_ref, dk_acc, dv_acc, k_before, v_before = rest
        else:
            kp_ref, vp_ref, dq_ref, dk_ref, dv_ref, dk_acc, dv_acc = rest
        n = pl.program_id(1)

        @pl.when(n == 0)
        def _():
            dk_acc[...] = jnp.zeros_like(dk_acc)
            dv_acc[...] = jnp.zeros_like(dv_acc)
            if keep:
                k_before[...] = jnp.zeros_like(k_before)
                v_before[...] = jnp.zeros_like(v_before)

        cur_ok, prev_band = _band_masks()
        for j in range(G):
            def residue(r, carry, j=j):
                off = j * band + r
                rows = pl.ds(off, SPAN, stride=d)
                q, dy = q_ref[rows, :], dy_ref[rows, :]
                lse_h = jnp.max(lse_ref[rows, :], axis=-1, keepdims=True)
                dd_h = jnp.max(dd_ref[rows, :], axis=-1, keepdims=True)
                kc, vc = kc_ref[rows, :].astype(BF16), vc_ref[rows, :].astype(BF16)
                if j > 0:
                    before = pl.ds(off - band, SPAN, stride=d)
                    kp, vp = kc_ref[before, :], vc_ref[before, :]
                    prev_ok = prev_band
                elif keep:
                    kp, vp = k_before[r], v_before[r]
                    k_before[r] = kc
                    v_before[r] = vc
                    prev_ok = prev_band & (n > 0)
                else:
                    before = pl.ds((G - 1) * band + r, SPAN, stride=d)
                    kp, vp = kp_ref[before, :], vp_ref[before, :]
                    prev_ok = prev_band & (n > 0)
                pc = jnp.exp(jnp.where(cur_ok, _dot_nt(q, kc) * ATT_SCALE, NEG) - lse_h)
                pp = jnp.exp(jnp.where(prev_ok, _dot_nt(q, kp) * ATT_SCALE, NEG) - lse_h)
                dsc = pc * (_dot_nt(dy, vc) - dd_h) * ATT_SCALE
                dsp = pp * (_dot_nt(dy, vp) - dd_h) * ATT_SCALE
                dq_ref[rows, :] = _dot(dsc, kc) + _dot(dsp, kp)
                u = (n * G + j) * d + r
                here = pl.ds(pl.multiple_of(u * SPAN, SPAN), SPAN)
                dk_acc[here, :] += _dot_tn(dsc, q)
                dv_acc[here, :] += _dot_tn(pc, dy)
                there = pl.ds(pl.multiple_of(jnp.maximum(u - d, 0) * SPAN, SPAN), SPAN)
                dk_acc[there, :] += _dot_tn(dsp, q)
                dv_acc[there, :] += _dot_tn(pp, dy)
                return carry

            lax.fori_loop(0, d, residue, 0, unroll=min(d, DILS_UNROLL))

        @pl.when(n == nblk - 1)
        def _():
            def place(u, carry):
                rows = pl.ds((u // d) * band + u % d, SPAN, stride=d)
                src = pl.ds(pl.multiple_of(u * SPAN, SPAN), SPAN)
                dk_ref[rows, :] = dk_acc[src, :]
                dv_ref[rows, :] = dv_acc[src, :]
                return carry

            lax.fori_loop(0, n_units, place, 0)

    blk = lambda f: pl.BlockSpec((tb, HEAD_DIM), f)
    cur = lambda h, n: (n, h)
    prev = lambda h, n: (jnp.maximum(n - 1, 0), h)
    whole = pl.BlockSpec((T, HEAD_DIM), lambda h, n: (0, h))
    sh = jax.ShapeDtypeStruct((T, B_WIDTH), F32)
    v_cur = blk(lambda h, n: (n, B_HEADS + h))
    if keep:
        kv_specs, kv_args = [blk(cur), v_cur], [kr, kv]
        kept = [pltpu.VMEM((d, SPAN, HEAD_DIM), BF16), pltpu.VMEM((d, SPAN, HEAD_DIM), BF16)]
    else:
        kv_specs = [blk(cur), v_cur, blk(prev), blk(lambda h, n: (jnp.maximum(n - 1, 0), B_HEADS + h))]
        kv_args, kept = [kr, kv, kr, kv], []
    body, dep_specs, dep_args = _dep(kernel_body, 4 + len(kv_args), dep)
    return pl.pallas_call(
        body, grid=(B_HEADS, nblk),
        in_specs=[blk(lambda h, n: (n, gi * B_HEADS + h)), blk(cur), blk(cur), blk(cur)] + kv_specs + dep_specs,
        out_specs=[blk(cur), whole, whole], out_shape=[sh, sh, sh],
        scratch_shapes=[pltpu.VMEM((T, HEAD_DIM), F32), pltpu.VMEM((T, HEAD_DIM), F32)] + kept,
        compiler_params=_cp("parallel", "arbitrary"), name=name)(qr, dmix, lse, dd, *kv_args, *dep_args)


A_MQ_COL = 4 * A_WIDTH // MEM_WIDTH
B_MQ_COL = N_GROUPS * B_WIDTH // MEM_WIDTH


def _row(v):
    return v.reshape(1, -1).astype(F32)


def _local_step(x, mem, tgt, get_w, P, put_g, first_dep=None, forward_point=lambda i, value: value):
    T = x.shape[0]
    cosf, sinsg = _rope_tables(T)
    lb_soft = jax.nn.softmax(P["a_lb_logits"].astype(F32), axis=0)
    lb = lb_soft[0:1]
    qw_heads = jnp.repeat(P["b_qnorm"][0], B_HEADS, axis=0).reshape(1, -1)
    kw_heads = jnp.tile(_row(P["b_knorm"]), (1, B_HEADS))
    mqw = [jnp.tile(_row(P["mem_qnorm"][l]), (1, MEM_HEADS)) for l in range(2)]
    mkw = [jnp.tile(_row(P["mem_knorm"][l]), (1, MEM_HEADS)) for l in range(2)]
    nmix = [_row(P["norm_mix"][l]) for l in range(2)]
    nffn = [_row(P["norm_ffn"][l]) for l in range(2)]
    mnorm = [_row(P["mem_norm"][l]) for l in range(2)]
    kvn = _row(P["kv_norm"])
    onorm = _row(P["a_onorm"])
    W = {}

    def w_of(name, after=None):
        if name not in W:
            W[name] = get_w(name, after)
        return W[name]

    proj_a, xn0 = _rms_matmul(x, nmix[0], w_of("a_w_in"), tt=512, tn=1664, wt=True, name="proj_a", dep=first_dep)
    mkv0, mn0 = _rms_matmul(mem, mnorm[0], w_of("w_mem_kv0"), tt=MEM_TOKENS, tn=2 * MEM_WIDTH, wt=False, name="mem_kv0")
    o_raw, st = _hgrn2_fwd(proj_a, lb, name="hgrn2_fwd")
    o_raw = forward_point(0, o_raw)
    mm0 = _a_post_fwd(o_raw, proj_a, onorm, tt=256, name="a_post_fwd")
    mo0 = _mem_attn_fwd(proj_a, A_MQ_COL, mkv0, mqw[0], mkw[0], tt=256, name="mem_attn_fwd0")
    hm0 = _mm_res(x, mm0, mo0, w_of("w_out0", mo0), tt=512, name="out_proj0")
    hm0 = forward_point(1, hm0)
    gu0, hn0 = _rms_matmul(hm0, nffn[0], w_of("w_gate_up0", hm0), tt=512, tn=1408, wt=True, out_dtype=BF16, name="gate_up0")
    h1 = _swiglu_down(hm0, gu0, w_of("w_down0", gu0), tt=512, name="down0")
    h1 = forward_point(2, h1)
    kv, hkn, kr = _rms_matmul(h1, kvn, w_of("w_kv", h1), tt=512, tn=768, wt=True, name="kv_proj",
                              rotate=(kw_heads, cosf, sinsg))

    proj_b, xn1, qr = _rms_matmul(h1, nmix[1], w_of("b_w_in", kr), tt=512, tn=1280, wt=True, name="proj_b",
                                  rotate=(qw_heads, cosf, sinsg))
    proj_b = forward_point(3, proj_b)
    mkv1, mn1 = _rms_matmul(mem, mnorm[1], w_of("w_mem_kv1", kr), tt=MEM_TOKENS, tn=2 * MEM_WIDTH, wt=False, name="mem_kv1")
    outs = [(_dil_fwd if d == 1 else _dils_fwd)(qr, kr, kv, gi, d, name=f"dil_fwd{gi}") for gi, d in enumerate(DILATIONS)]
    mm1, lse_tot = _dil_combine_fwd([o for o, _ in outs], [s for _, s in outs], tt=256, name="dil_combine")
    mo1 = _mem_attn_fwd(proj_b, B_MQ_COL, mkv1, mqw[1], mkw[1], tt=256, name="mem_attn_fwd1")
    hm1 = _mm_res(h1, mm1, mo1, w_of("w_out1", mo1), tt=512, name="out_proj1")
    gu1, hn1 = _rms_matmul(hm1, nffn[1], w_of("w_gate_up1", hm1), tt=512, tn=1408, wt=True, out_dtype=BF16, name="gate_up1")
    dy, sq = _swiglu_down_loss(hm1, gu1, w_of("w_down1", gu1), tgt, tt=512, name="down1_loss")

    gP = {}
    zeros_mem = jnp.zeros((MEM_TOKENS, D_MODEL), F32)

    def ffn_bwd(l, dh, hm, gu, hn):
        dgu, g_wd = _swiglu_bwd(dh, gu, w_of(f"w_down{l}"), tt=256, name=f"swiglu_bwd{l}")
        g_wgu = _mm_tn(dgu, hn, tt=512, tka=1408, name=f"g_w_gate_up{l}")
        sent = put_g({f"w_down{l}": g_wd, f"w_gate_up{l}": g_wgu})
        dhm, g_nf = _rms_bwd_dx(hm, nffn[l], w_of(f"w_gate_up{l}"), dgu, dh, tt=512, wt=True, name=f"gate_up_bwd{l}", dep=sent)
        return dhm, g_nf

    def mix_bwd(l, dhm, mix_main, mix_mem, proj, qcol, mkv, mn):
        dmix, g_wout, *head_dots = _out_proj_bwd(dhm, mix_main, mix_mem, w_of(f"w_out{l}"), tt=512, name=f"out_proj_bwd{l}",
                                                 head_dots=l == 1)
        dmq, dmkv, dqw, dkw = _mem_attn_bwd(proj, qcol, mkv, mqw[l], mkw[l], dmix, tt=256, name=f"mem_attn_bwd{l}")
        g_wmkv = _mm_tn(mn, dmkv, tt=MEM_TOKENS, tka=512, name=f"g_w_mem_kv{l}")
        sent = put_g({f"w_out{l}": g_wout, f"w_mem_kv{l}": g_wmkv})
        _, g_mn = _rms_bwd_dx(mem, mnorm[l], w_of(f"w_mem_kv{l}"), dmkv, zeros_mem, tt=MEM_TOKENS, wt=False, name=f"mem_kv_bwd{l}")
        fold = lambda v: v.reshape(MEM_HEADS, MEM_HEAD_DIM).sum(axis=0)
        return dmix, dmq, g_mn, fold(dqw), fold(dkw), sent, head_dots

    dhm1, g_nf1 = ffn_bwd(1, dy, hm1, gu1, hn1)
    dmix1, dmq1, g_mn1, g_mq1, g_mk1, sent, (dd,) = mix_bwd(1, dhm1, mm1, mo1, proj_b, B_MQ_COL, mkv1, mn1)
    dqs, dks, dvs = [], [], []
    for gi, d in enumerate(DILATIONS):
        dq_g, dk_g, dv_g = _dil_bwd(qr, kr, kv, dmix1, lse_tot, dd, gi, d, name=f"dil_bwd{gi}", dep=sent if gi == 0 else None)
        dqs.append(dq_g)
        dks.append(dk_g)
        dvs.append(dv_g)
    dq_raw, dqw = _q_prep_bwd(proj_b, qw_heads, cosf, sinsg, dqs, tt=256, name="q_prep_bwd")
    dkv, dkw = _kv_prep_bwd(kv, kw_heads, cosf, sinsg, dks, dvs, tt=256, name="kv_prep_bwd")
    dproj_b = [dq_raw, dmq1]
    g_wb = _mm_tn_pieces(dproj_b, xn1, tt=512, name="g_b_w_in")
    g_wkv = _mm_tn(dkv, hkn, tt=512, tka=768, name="g_w_kv")
    sent = put_g({"b_w_in": g_wb, "w_kv": g_wkv})
    dh1, g_nm1 = _rms_bwd_dx(h1, nmix[1], w_of("b_w_in"), dproj_b, dhm1, tt=512, wt=True, name="proj_b_bwd", dep=sent)
    dh1, g_kvn = _rms_bwd_dx(h1, kvn, w_of("w_kv"), dkv, dh1, tt=512, wt=True, name="kv_proj_bwd")

    dhm0, g_nf0 = ffn_bwd(0, dh1, hm0, gu0, hn0)
    dmix0, dmq0, g_mn0, g_mq0, g_mk0, sent, _ = mix_bwd(0, dhm0, mm0, mo0, proj_a, A_MQ_COL, mkv0, mn0)
    do_raw, dg, g_onorm = _a_post_bwd(o_raw, proj_a, onorm, dmix0, tt=256, name="a_post_bwd", dep=sent)
    dq, dz, dv, dlb = _hgrn2_bwd(proj_a, lb, st, do_raw, name="hgrn2_bwd")
    dproj_a = [dq, dz, dv, dg, dmq0]
    sent = put_g({"a_w_in": _mm_tn_pieces(dproj_a, xn0, tt=512, name="g_a_w_in")})
    gx, g_nm0 = _rms_bwd_dx(x, nmix[0], w_of("a_w_in"), dproj_a, dhm0, tt=512, wt=True, name="proj_a_bwd", dep=sent)

    dl0 = lb_soft[0:1] * lb_soft[1:2] * dlb
    gP["a_lb_logits"] = jnp.concatenate([dl0, -dl0], axis=0)
    gP["a_onorm"] = g_onorm
    gP["norm_mix"] = jnp.concatenate([g_nm0, g_nm1], axis=0)
    gP["norm_ffn"] = jnp.concatenate([g_nf0, g_nf1], axis=0)
    gP["b_qnorm"] = dqw.reshape(N_GROUPS, B_HEADS, HEAD_DIM).sum(axis=1)[None]
    gP["kv_norm"] = g_kvn.reshape(-1)
    gP["b_knorm"] = dkw.reshape(B_HEADS, HEAD_DIM).sum(axis=0)
    gP["mem_norm"] = jnp.concatenate([g_mn0, g_mn1], axis=0)
    gP["mem_qnorm"] = jnp.stack([g_mq0, g_mq1])
    gP["mem_knorm"] = jnp.stack([g_mk0, g_mk1])
    return sq, gx, gP


MESH_ID = pl.DeviceIdType.MESH
HBM_SPEC = pl.BlockSpec(memory_space=pltpu.HBM)


def _position():
    return lax.axis_index("x"), lax.axis_index("y"), lax.axis_index("c")


def _all_gather_direct(block, after, *, name):
    def body(x_ref, after_ref, out_ref, send_sems, recv_sems, local_sem):
        x, y, c = _position()
        me = 4 * x + 2 * y + c
        mine = pltpu.make_async_copy(x_ref, out_ref.at[me], local_sem)
        mine.start()
        copies = []
        for k in ALL_PEERS:
            cp = pltpu.make_async_remote_copy(
                src_ref=x_ref, dst_ref=out_ref.at[me], send_sem=send_sems.at[k - 1], recv_sem=recv_sems.at[k - 1],
                device_id=_peer(k, x, y, c), device_id_type=MESH_ID)
            cp.start()
            copies.append(cp)
        for cp in copies:
            cp.wait()
        mine.wait()

    return pl.pallas_call(
        body, out_shape=jax.ShapeDtypeStruct((N_DEV,) + block.shape, block.dtype),
        in_specs=[HBM_SPEC, pl.BlockSpec(memory_space=pl.ANY)], out_specs=HBM_SPEC,
        scratch_shapes=[pltpu.SemaphoreType.DMA((7,)), pltpu.SemaphoreType.DMA((7,)), pltpu.SemaphoreType.DMA],
        name=name)(block, after)


SEM_SPEC = pl.BlockSpec(memory_space=pltpu.SEMAPHORE)
ANY_SPEC = pl.BlockSpec(memory_space=pl.ANY)
DATAFLOW = pltpu.SideEffectType.DATAFLOW_SIDE_EFFECTING


def _peer(k, x, y, c):
    return (1 - x if (k >> 2) & 1 else x, 1 - y if (k >> 1) & 1 else y, 1 - c if k & 1 else c)


def _own_slot_filled(own_block):
    x, y, c = _position()
    zone = lax.empty((N_DEV,) + own_block.shape, own_block.dtype)
    return lax.dynamic_update_slice_in_dim(zone, own_block[None], 4 * x + 2 * y + c, axis=0)


ALL_PEERS = tuple(range(1, N_DEV))
SIBLING_AND_SAME_CORE = (1, 2, 4, 6)
SAME_CORE = (2, 4, 6)


def _split_start(srcs, scatter, after, *, name, relations=ALL_PEERS, carried=None):
    n = len(srcs)
    extra = ([] if after is None else [after]) + ([] if carried is None else [carried])
    n_carried = 0 if carried is None else 1
    x, y, c = _position()
    me = 4 * x + 2 * y + c
    lands = [_own_slot_filled(lax.dynamic_index_in_dim(s, me, 0, keepdims=False) if scatter else s) for s in srcs]

    def body(*refs):
        src_refs, land_refs = refs[:n], refs[n:2 * n]
        send_sems, recv_sems = refs[2 * n + len(extra)], refs[2 * n + len(extra) + 1]
        token = refs[2 * n + len(extra) + 2 + 2 * n]
        bx, by, bc = _position()
        bme = 4 * bx + 2 * by + bc
        for a in range(n):
            for k in relations:
                tx, ty, tc = _peer(k, bx, by, bc)
                src = src_refs[a].at[4 * tx + 2 * ty + tc] if scatter else src_refs[a]
                pltpu.make_async_remote_copy(
                    src_ref=src, dst_ref=land_refs[a].at[bme],
                    send_sem=send_sems.at[7 * a + k - 1], recv_sem=recv_sems.at[7 * a + k - 1],
                    device_id=(tx, ty, tc), device_id_type=MESH_ID).start()
        token[...] = jnp.zeros_like(token)

    hbm = lambda a: pltpu.HBM(a.shape, a.dtype)
    outs = pl.pallas_call(
        body, name=name,
        out_shape=(pltpu.SemaphoreType.DMA((7 * n,)), pltpu.SemaphoreType.DMA((7 * n,)),
                   *[hbm(s) for s in srcs], *[hbm(l) for l in lands], jax.ShapeDtypeStruct((8, 128), F32),
                   *([hbm(carried)] if n_carried else [])),
        in_specs=[HBM_SPEC] * (2 * n) + [ANY_SPEC] * len(extra),
        out_specs=(SEM_SPEC, SEM_SPEC, *[HBM_SPEC] * (2 * n), pl.BlockSpec(memory_space=pltpu.VMEM), *([ANY_SPEC] * n_carried)),
        input_output_aliases={**{i: 2 + i for i in range(2 * n)},
                              **({2 * n + len(extra) - 1: 2 * n + 3} if n_carried else {})},
        compiler_params=pltpu.CompilerParams(has_side_effects=DATAFLOW),
    )(*[pltpu.with_memory_space_constraint(s, pltpu.HBM) for s in srcs],
      *[pltpu.with_memory_space_constraint(l, pltpu.HBM) for l in lands], *extra)
    return {"n": n, "relations": relations, "send": outs[0], "recv": outs[1], "srcs": list(outs[2:2 + n]),
            "lands": list(outs[2 + n:2 + 2 * n]), "token": outs[2 * n + 2], "carried": outs[-1] if n_carried else None}


def _forward_start(lands, carried, *, name):
    n = len(lands)

    def body(*refs):
        land_refs = refs[:n]
        send_sems, recv_sems = refs[n + 1], refs[n + 2]
        bx, by, bc = _position()
        for a in range(n):
            for k in SAME_CORE:
                tx, ty, tc = _peer(k, bx, by, bc)
                block = land_refs[a].at[4 * tx + 2 * ty + tc]
                pltpu.make_async_remote_copy(
                    src_ref=block, dst_ref=block,
                    send_sem=send_sems.at[7 * a + k - 1], recv_sem=recv_sems.at[7 * a + k - 1],
                    device_id=(bx, by, 1 - bc), device_id_type=MESH_ID).start()

    hbm = lambda a: pltpu.HBM(a.shape, a.dtype)
    outs = pl.pallas_call(
        body, name=name,
        out_shape=(pltpu.SemaphoreType.DMA((7 * n,)), pltpu.SemaphoreType.DMA((7 * n,)),
                   *[hbm(l) for l in lands], hbm(carried)),
        in_specs=[HBM_SPEC] * n + [ANY_SPEC],
        out_specs=(SEM_SPEC, SEM_SPEC, *[HBM_SPEC] * n, ANY_SPEC),
        input_output_aliases={i: 2 + i for i in range(n + 1)},
        compiler_params=pltpu.CompilerParams(has_side_effects=DATAFLOW),
    )(*lands, carried)
    handle = {"n": n, "relations": SAME_CORE, "send": outs[0], "recv": outs[1], "srcs": [], "lands": list(outs[2:2 + n])}
    return handle, outs[-1]


def _split_wait(handle, after, *, name):
    n, ns = handle["n"], len(handle["srcs"])

    def body(*refs):
        land_refs = refs[ns:ns + n]
        send_sems, recv_sems = refs[ns + n], refs[ns + n + 1]
        bx, by, bc = _position()
        for a in range(n):
            for k in handle["relations"]:
                block = land_refs[a].at[0]
                cp = pltpu.make_async_remote_copy(
                    src_ref=block, dst_ref=block,
                    send_sem=send_sems.at[7 * a + k - 1], recv_sem=recv_sems.at[7 * a + k - 1],
                    device_id=_peer(k, bx, by, bc), device_id_type=MESH_ID)
                cp.wait_send()
                cp.wait_recv()

    hbm = lambda a: pltpu.HBM(a.shape, a.dtype)
    outs = pl.pallas_call(
        body, name=name,
        out_shape=(*[hbm(s) for s in handle["srcs"]], *[hbm(l) for l in handle["lands"]]),
        in_specs=[HBM_SPEC] * (ns + n) + [SEM_SPEC, SEM_SPEC, ANY_SPEC],
        out_specs=tuple([HBM_SPEC] * (ns + n)),
        input_output_aliases={i: i for i in range(ns + n)},
        compiler_params=pltpu.CompilerParams(has_side_effects=DATAFLOW),
    )(*handle["srcs"], *handle["lands"], handle["send"], handle["recv"], after)
    return list(outs[ns:])


def _sum_sources(parts, *, tr, name):
    n, R, C = parts.shape

    def body(p_ref, o_ref):
        acc = p_ref[0].astype(F32)
        for s in range(1, n):
            acc = acc + p_ref[s].astype(F32)
        o_ref[...] = acc

    return pl.pallas_call(
        body, grid=(R // tr,), in_specs=[pl.BlockSpec((n, tr, C), lambda i: (0, i, 0))],
        out_specs=pl.BlockSpec((tr, C), lambda i: (i, 0)),
        out_shape=jax.ShapeDtypeStruct((R, C), F32), compiler_params=_cp("parallel"), name=name)(parts)


def _adamw_math(g, w, m, v):
    c1 = 1.0 - ADAM_B1 ** ADAM_STEP
    c2 = 1.0 - ADAM_B2 ** ADAM_STEP
    nm = ADAM_B1 * m + (1.0 - ADAM_B1) * g
    nv = ADAM_B2 * v + (1.0 - ADAM_B2) * (g * g)
    return -ADAM_LR * ((nm / c1) / (jnp.sqrt(nv / c2) + ADAM_EPS) + ADAM_WD * w), nm, nv


def _reduce_adamw(received, w, m, v, *, tr, name):
    L, R, C = w.shape

    def body(*refs):
        p_refs = refs[:L]
        w_ref, m_ref, v_ref, g_ref, d_ref, nm_ref, nv_ref = refs[L:]
        for l in range(L):
            @pl.when(pl.program_id(0) == l)
            def _(p_ref=p_refs[l]):
                acc = p_ref[0].astype(F32)
                for s in range(1, N_DEV):
                    acc = acc + p_ref[s].astype(F32)
                g_ref[...] = acc
                d_ref[...], nm_ref[...], nv_ref[...] = _adamw_math(acc, w_ref[...], m_ref[...], v_ref[...])

    p_spec = pl.BlockSpec((N_DEV, tr, C), lambda l, i: (0, i, 0))
    blk = pl.BlockSpec((None, tr, C), lambda l, i: (l, i, 0))
    sh = jax.ShapeDtypeStruct((L, R, C), F32)
    return pl.pallas_call(
        body, grid=(L, R // tr), in_specs=[p_spec] * L + [blk] * 3, out_specs=[blk] * 4, out_shape=[sh] * 4,
        compiler_params=_cp("parallel", "parallel"), name=name)(*received, w, m, v)


def _adamw(g, w, m, v, *, tr, name):
    L, R, C = w.shape

    def body(g_ref, w_ref, m_ref, v_ref, d_ref, nm_ref, nv_ref):
        d_ref[...], nm_ref[...], nv_ref[...] = _adamw_math(g_ref[...], w_ref[...], m_ref[...], v_ref[...])

    blk = pl.BlockSpec((None, tr, C), lambda l, i: (l, i, 0))
    sh = jax.ShapeDtypeStruct((L, R, C), F32)
    return pl.pallas_call(
        body, grid=(L, R // tr), in_specs=[blk] * 4, out_specs=[blk] * 3, out_shape=[sh] * 3,
        compiler_params=_cp("parallel", "parallel"), name=name)(g, w, m, v)


UNITS = {
    "a_w_in": ("a_w_in", 0, True), "w_mem_kv0": ("w_mem_kv", 0, False), "w_out0": ("w_out", 0, False),
    "w_gate_up0": ("w_gate_up", 0, True), "w_down0": ("w_down", 0, False), "w_kv": ("w_kv", None, True),
    "b_w_in": ("b_w_in", 0, True), "w_mem_kv1": ("w_mem_kv", 1, False), "w_out1": ("w_out", 1, False),
    "w_gate_up1": ("w_gate_up", 1, True), "w_down1": ("w_down", 1, False),
}
BIG = ("a_w_in", "b_w_in", "w_kv", "w_mem_kv", "w_out", "w_gate_up", "w_down")
ADAMW_ROW_TILE = {"a_w_in": 208, "b_w_in": 160, "w_kv": 192, "w_mem_kv": 128, "w_out": 128, "w_gate_up": 352, "w_down": 352}


def _wire_block(weights, unit):
    name, layer, col = UNITS[unit]
    a = weights[name] if layer is None else weights[name][layer]
    return (a.T if col else a).astype(BF16)


SMALL_REPLICATED = ("norm_mix", "norm_ffn", "b_qnorm", "kv_norm", "b_knorm", "mem_norm", "mem_qnorm", "mem_knorm")
SMALL_SHARDED = ("a_lb_logits", "a_onorm")
SMALL_ORDER = SMALL_REPLICATED + SMALL_SHARDED
LANES = 128


def _prod(shape):
    n = 1
    for s in shape:
        n *= s
    return n


def _pack_flat(arrays, rows, cols, dtype):
    flat = jnp.concatenate([a.reshape(-1).astype(dtype) for a in arrays])
    return jnp.pad(flat, (0, rows * cols - flat.shape[0])).reshape(rows, cols)


def _unpack_flat(packed, shapes):
    flat = packed.reshape(-1)
    out, off = [], 0
    for s in shapes:
        out.append(flat[off:off + _prod(s)].reshape(s))
        off += _prod(s)
    return out


def kernel(x, mem, norm_mix, norm_ffn, a_w_in, a_lb_logits, a_onorm, b_w_in, b_qnorm, kv_norm, w_kv, b_knorm, mem_norm, w_mem_kv, mem_qnorm, mem_knorm, w_out, w_gate_up, w_down, loss_target, m_norm_mix, m_norm_ffn, m_a_w_in, m_a_lb_logits, m_a_onorm, m_b_w_in, m_b_qnorm, m_kv_norm, m_w_kv, m_b_knorm, m_mem_norm, m_w_mem_kv, m_mem_qnorm, m_mem_knorm, m_w_out, m_w_gate_up, m_w_down, v_norm_mix, v_norm_ffn, v_a_w_in, v_a_lb_logits, v_a_onorm, v_b_w_in, v_b_qnorm, v_kv_norm, v_w_kv, v_b_knorm, v_mem_norm, v_w_mem_kv, v_mem_qnorm, v_mem_knorm, v_w_out, v_w_gate_up, v_w_down):
    names = ("norm_mix", "norm_ffn", "a_w_in", "a_lb_logits", "a_onorm", "b_w_in", "b_qnorm", "kv_norm", "w_kv", "b_knorm",
             "mem_norm", "w_mem_kv", "mem_qnorm", "mem_knorm", "w_out", "w_gate_up", "w_down")
    w = dict(zip(names, (norm_mix, norm_ffn, a_w_in, a_lb_logits, a_onorm, b_w_in, b_qnorm, kv_norm, w_kv, b_knorm,
                         mem_norm, w_mem_kv, mem_qnorm, mem_knorm, w_out, w_gate_up, w_down)))
    m = dict(zip(names, (m_norm_mix, m_norm_ffn, m_a_w_in, m_a_lb_logits, m_a_onorm, m_b_w_in, m_b_qnorm, m_kv_norm, m_w_kv,
                         m_b_knorm, m_mem_norm, m_w_mem_kv, m_mem_qnorm, m_mem_knorm, m_w_out, m_w_gate_up, m_w_down)))
    v = dict(zip(names, (v_norm_mix, v_norm_ffn, v_a_w_in, v_a_lb_logits, v_a_onorm, v_b_w_in, v_b_qnorm, v_kv_norm, v_w_kv,
                         v_b_knorm, v_mem_norm, v_w_mem_kv, v_mem_qnorm, v_mem_knorm, v_w_out, v_w_gate_up, v_w_down)))

    first = ["a_w_in", "w_mem_kv0"]
    later = [["w_out0", "w_gate_up0"], ["w_down0", "w_kv"], ["b_w_in", "w_mem_kv1"], ["w_out1", "w_gate_up1", "w_down1"]]
    first_half, second_half = {}, {}

    def start_first_half(i, after, carried=None):
        first_half[i] = _split_start([_wire_block(w, u) for u in later[i]], False, after, name=f"gather{i}_start",
                                     relations=SIBLING_AND_SAME_CORE, carried=carried)
        return first_half[i]

    opening = _split_start([_wire_block(w, u) for u in first] + [_pack_flat([a_lb_logits, a_onorm], 8, LANES, F32)],
                           False, None, name="gather_first_start", relations=SIBLING_AND_SAME_CORE)
    token = start_first_half(0, opening["token"])["token"]
    token = start_first_half(1, token)["token"]
    opening, token = _forward_start(_split_wait(opening, token, name="gather_first_landed"), token, name="gather_first_forward")
    gathered = _split_wait(opening, token, name="gather_first_wait")
    full = {u: g.reshape(-1, g.shape[-1]) for u, g in zip(first, gathered)}
    small_in = gathered[-1].reshape(N_DEV, -1)
    P = {n: w[n] for n in SMALL_REPLICATED}
    P["a_lb_logits"] = small_in[:, :192].reshape(N_DEV, 2, 96).transpose(1, 0, 2).reshape(2, A_WIDTH)
    P["a_onorm"] = small_in[:, 192:288].reshape(1, A_WIDTH)

    def forward_point(i, value):
        landed = _split_wait(first_half[i], value, name=f"gather{i}_landed")
        second_half[i], value = _forward_start(landed, value, name=f"gather{i}_forward")
        if i + 2 < len(later):
            value = start_first_half(i + 2, None, carried=value)["carried"]
        return value

    def get_w(unit, after):
        if unit not in full:
            i = [unit in group for group in later].index(True)
            for u, land in zip(later[i], _split_wait(second_half[i], after, name=f"gather{i}_wait")):
                full[u] = land.reshape(-1, land.shape[-1])
        return full[unit]

    sent = []

    def put_g(group):
        units = list(group)
        handle = _split_start([group[u].reshape(N_DEV, -1, group[u].shape[-1]) for u in units], True, None,
                              name=f"scatter{len(sent)}_start")
        sent.append((units, handle))
        return handle["token"]

    sq, gx, gP = _local_step(x[0], mem[0], loss_target[0], get_w, P, put_g, forward_point=forward_point)
    loss_here = (0.5 * jnp.sum(sq) / D_MODEL).reshape(1)

    received = {}
    group_of = {u: i for i, (units, _) in enumerate(sent) for u in units}
    out = {"grad": {}, "delta": {}, "new_m": {}, "new_v": {}}
    newest = [gx]

    def update_big(n):
        shape = w[n].shape
        as3 = lambda a: a.reshape((-1,) + shape[-2:])
        mine = [u for u, (wn, _, _) in UNITS.items() if wn == n]
        for i in sorted({group_of[u] for u in mine}):
            if sent[i][0][0] not in received:
                received.update(zip(sent[i][0], _split_wait(sent[i][1], newest[0], name=f"scatter{i}_wait")))
        flip = (lambda a: jnp.swapaxes(a, 1, 2)) if UNITS[mine[0]][2] else (lambda a: a)
        res = _reduce_adamw([received[u] for u in mine], flip(as3(w[n])), flip(as3(m[n])), flip(as3(v[n])),
                            tr=ADAMW_ROW_TILE[n], name=f"adamw_{n}")
        newest[0] = res[1]
        for kind, r in zip(("grad", "delta", "new_m", "new_v"), res):
            out[kind][n] = flip(r).reshape(shape)

    for n in ("w_down", "w_gate_up", "w_out", "w_mem_kv", "b_w_in", "w_kv"):
        update_big(n)

    full_shapes = [(2, A_WIDTH) if n == "a_lb_logits" else (1, A_WIDTH) if n == "a_onorm" else w[n].shape for n in SMALL_ORDER]
    n_small = sum(_prod(s) for s in full_shapes) + 1
    rows_small = -(-n_small // (8 * LANES)) * 8
    g_all = _all_gather_direct(_pack_flat([gP[n] for n in SMALL_ORDER] + [loss_here], rows_small, LANES, F32),
                               newest[0], name="gather_small_grads")
    summed = _unpack_flat(_sum_sources(g_all, tr=rows_small, name="sum_small_grads"), full_shapes + [(1,)])
    g_small = dict(zip(SMALL_ORDER, summed))
    loss = summed[-1].reshape(())
    me = 4 * lax.axis_index("x") + 2 * lax.axis_index("y") + lax.axis_index("c")
    for n in SMALL_SHARDED:
        g_small[n] = lax.dynamic_slice_in_dim(g_small[n], me * 96, 96, axis=1)
    shapes = [w[n].shape for n in SMALL_ORDER]
    rows_upd = -(-sum(_prod(s) for s in shapes) // (8 * LANES)) * 8
    pk = lambda d: _pack_flat([d[n] for n in SMALL_ORDER], rows_upd, LANES, F32)
    res = _adamw(pk(g_small)[None], pk(w)[None], pk(m)[None], pk(v)[None], tr=rows_upd, name="adamw_small")
    out["grad"].update(g_small)
    for kind, packed in zip(("delta", "new_m", "new_v"), res):
        out[kind].update(zip(SMALL_ORDER, _unpack_flat(packed[0], shapes)))
    newest[0] = res[0]
    update_big("a_w_in")

    return (loss, gx[None], *[out["grad"][n] for n in names], *[out["delta"][n] for n in names],
            *[out["new_m"][n] for n in names], *[out["new_v"][n] for n in names])
```

```python
import functools

import jax
import jax.numpy as jnp
import numpy as np
from jax import lax
from jax.experimental import pallas as pl
from jax.experimental.pallas import tpu as pltpu

F32 = jnp.float32
BF16 = jnp.bfloat16

N_DEV = 8
D_MODEL = 1024
HEAD_DIM = 128
A_HEADS = 6
A_WIDTH = A_HEADS * HEAD_DIM
CHUNK = 64
B_HEADS = 6
B_WIDTH = B_HEADS * HEAD_DIM
DILATIONS = (1, 4, 16)
SPAN = 128
N_GROUPS = 3
ROPE_THETA = 10000.0
MEM_TOKENS = 256
MEM_HEADS = 4
MEM_HEAD_DIM = 64
MEM_WIDTH = MEM_HEADS * MEM_HEAD_DIM
FFN_HIDDEN = 2816
EPS = 1e-6

ADAM_LR = 0.001
ADAM_B1 = 0.9
ADAM_B2 = 0.999
ADAM_EPS = 1e-08
ADAM_WD = 0.01
ADAM_STEP = 10

V7X_VMEM_LIMIT_BYTES = 56 * 1024 * 1024

NT_DIMS = (((1,), (1,)), ((), ()))
TN_DIMS = (((0,), (0,)), ((), ()))


def _cp(*sem):
    return pltpu.CompilerParams(dimension_semantics=sem, vmem_limit_bytes=V7X_VMEM_LIMIT_BYTES)


def _dot(a, b):
    return jnp.dot(a.astype(BF16), b.astype(BF16), preferred_element_type=F32)


def _dot_nt(a, b):
    return lax.dot_general(a.astype(BF16), b.astype(BF16), NT_DIMS, preferred_element_type=F32)


def _dot_tn(a, b):
    return lax.dot_general(a.astype(BF16), b.astype(BF16), TN_DIMS, preferred_element_type=F32)


def _dot3(m01, x):
    hi = x.astype(BF16)
    r1 = x - hi.astype(F32)
    mid = r1.astype(BF16)
    lo = (r1 - mid.astype(F32)).astype(BF16)
    d = functools.partial(jnp.dot, preferred_element_type=F32)
    return d(m01, hi) + d(m01, mid) + d(m01, lo)


def _sigmoid(x):
    return 0.5 * jnp.tanh(0.5 * x) + 0.5


def _full(shape):
    return pl.BlockSpec(shape, lambda *_: (0,) * len(shape))


def _dep(body, n_in, dep):
    if dep is None:
        return body, [], []

    def with_dep(*refs):
        return body(*refs[:n_in], *refs[n_in + 1:])

    return with_dep, [pl.BlockSpec(memory_space=pl.ANY)], [dep]


def _rms_matmul(x, g, w, *, tt, tn, wt, name, out_dtype=F32, dep=None, rotate=None):
    T, K = x.shape
    N = w.shape[0] if wt else w.shape[1]
    n_rot = 0 if rotate is None else rotate[0].shape[1] // HEAD_DIM
    extra_in = [] if rotate is None else list(rotate)

    def kernel_body(x_ref, g_ref, w_ref, *rest):
        y_ref, xn_ref = rest[len(extra_in)], rest[len(extra_in) + 1]
        xf = x_ref[...]
        r = lax.rsqrt(jnp.mean(xf * xf, axis=-1, keepdims=True) + EPS)
        xn = (xf * r * g_ref[...]).astype(BF16)
        xn_ref[...] = xn
        for j in range(N // tn):
            cols = slice(j * tn, (j + 1) * tn)
            y = _dot_nt(xn, w_ref[cols, :]) if wt else _dot(xn, w_ref[:, cols])
            y_ref[:, cols] = y.astype(out_dtype)
            for h in range(j * tn // HEAD_DIM, min((j + 1) * tn // HEAD_DIM, n_rot)):
                gw_ref, c_ref, s_ref, yr_ref = rest[0], rest[1], rest[2], rest[len(extra_in) + 2]
                sl = slice(h * HEAD_DIM, (h + 1) * HEAD_DIM)
                xhat, _ = _head_rms(y[:, h * HEAD_DIM - j * tn:(h + 1) * HEAD_DIM - j * tn])
                yr_ref[:, sl] = _rope(xhat * gw_ref[:, sl], c_ref[...], s_ref[...])

    tbl = pl.BlockSpec((tt, HEAD_DIM), lambda i: (i, 0))
    rot_specs = [] if rotate is None else [_full((1, n_rot * HEAD_DIM)), tbl, tbl]
    body, dep_specs, dep_args = _dep(kernel_body, 3 + len(extra_in), dep)
    return pl.pallas_call(
        body, grid=(T // tt,),
        in_specs=[pl.BlockSpec((tt, K), lambda i: (i, 0)), _full((1, K)), _full(w.shape)] + rot_specs + dep_specs,
        out_specs=[pl.BlockSpec((tt, N), lambda i: (i, 0)), pl.BlockSpec((tt, K), lambda i: (i, 0))]
        + ([] if rotate is None else [pl.BlockSpec((tt, n_rot * HEAD_DIM), lambda i: (i, 0))]),
        out_shape=[jax.ShapeDtypeStruct((T, N), out_dtype), jax.ShapeDtypeStruct((T, K), BF16)]
        + ([] if rotate is None else [jax.ShapeDtypeStruct((T, n_rot * HEAD_DIM), F32)]),
        compiler_params=_cp("parallel"), name=name)(x, g, w, *extra_in, *dep_args)


def _mm_res(res, a1, a2, w, *, tt, name):
    T, K1 = a1.shape
    K2 = a2.shape[1]
    N = w.shape[1]

    def body(r_ref, a1_ref, a2_ref, w_ref, o_ref):
        o_ref[...] = r_ref[...] + _dot(a1_ref[...], w_ref[:K1, :]) + _dot(a2_ref[...], w_ref[K1:, :])

    return pl.pallas_call(
        body, grid=(T // tt,),
        in_specs=[pl.BlockSpec((tt, N), lambda i: (i, 0)), pl.BlockSpec((tt, K1), lambda i: (i, 0)),
                  pl.BlockSpec((tt, K2), lambda i: (i, 0)), _full((K1 + K2, N))],
        out_specs=pl.BlockSpec((tt, N), lambda i: (i, 0)),
        out_shape=jax.ShapeDtypeStruct((T, N), F32),
        compiler_params=_cp("parallel"), name=name)(res, a1, a2, w)


def _swiglu_down(h, gu, wd, *, tt, name):
    T, D = h.shape
    Fh = wd.shape[0]

    def body(h_ref, gt_ref, up_ref, w_ref, o_ref):
        gt = gt_ref[...].astype(F32)
        act = gt * _sigmoid(gt) * up_ref[...].astype(F32)
        o_ref[...] = h_ref[...] + _dot(act, w_ref[...])

    return pl.pallas_call(
        body, grid=(T // tt,),
        in_specs=[pl.BlockSpec((tt, D), lambda i: (i, 0)), pl.BlockSpec((tt, Fh), lambda i: (i, 0)),
                  pl.BlockSpec((tt, Fh), lambda i: (i, 1)), _full((Fh, D))],
        out_specs=pl.BlockSpec((tt, D), lambda i: (i, 0)),
        out_shape=jax.ShapeDtypeStruct((T, D), F32),
        compiler_params=_cp("parallel"), name=name)(h, gu, gu, wd)


def _swiglu_down_loss(h, gu, wd, tgt, *, tt, name):
    T, D = h.shape
    Fh = wd.shape[0]

    def body(h_ref, gt_ref, up_ref, w_ref, t_ref, dy_ref, acc_ref):
        @pl.when(pl.program_id(0) == 0)
        def _():
            acc_ref[...] = jnp.zeros_like(acc_ref)

        gt = gt_ref[...].astype(F32)
        act = gt * _sigmoid(gt) * up_ref[...].astype(F32)
        e = h_ref[...] + _dot(act, w_ref[...]) - t_ref[...]
        dy_ref[...] = e * (1.0 / D)
        acc_ref[...] += jnp.sum(e * e, axis=0, keepdims=True)

    row = pl.BlockSpec((tt, D), lambda i: (i, 0))
    return pl.pallas_call(
        body, grid=(T // tt,),
        in_specs=[row, pl.BlockSpec((tt, Fh), lambda i: (i, 0)), pl.BlockSpec((tt, Fh), lambda i: (i, 1)), _full((Fh, D)), row],
        out_specs=[row, _full((1, D))],
        out_shape=[jax.ShapeDtypeStruct((T, D), F32), jax.ShapeDtypeStruct((1, D), F32)],
        compiler_params=_cp("arbitrary"), name=name)(h, gu, gu, wd, tgt)


SWIGLU_COLS = 256


def _swiglu_bwd(dh, gu, wd, *, tt, name):
    T, D = dh.shape
    Fh = wd.shape[0]
    last = T // tt - 1

    def body(dh_ref, gt_ref, up_ref, w_ref, dgu_ref, gw_ref, acc):
        @pl.when(pl.program_id(0) == 0)
        def _():
            acc[...] = jnp.zeros_like(acc)

        dh16 = dh_ref[...].astype(BF16)
        for c0 in range(0, Fh, SWIGLU_COLS):
            cols = slice(c0, c0 + SWIGLU_COLS)
            gt = gt_ref[:, cols].astype(F32)
            up = up_ref[:, cols].astype(F32)
            s = _sigmoid(gt)
            silu = gt * s
            dact = _dot_nt(dh16, w_ref[cols, :])
            acc[cols, :] += _dot_tn((silu * up).astype(BF16), dh16)
            dgu_ref[:, cols] = (dact * up * (s * (1.0 + gt * (1.0 - s)))).astype(BF16)
            dgu_ref[:, Fh + c0:Fh + c0 + SWIGLU_COLS] = (dact * silu).astype(BF16)

        @pl.when(pl.program_id(0) == last)
        def _():
            gw_ref[...] = acc[...].astype(BF16)

    return pl.pallas_call(
        body, grid=(T // tt,),
        in_specs=[pl.BlockSpec((tt, D), lambda i: (i, 0)), pl.BlockSpec((tt, Fh), lambda i: (i, 0)),
                  pl.BlockSpec((tt, Fh), lambda i: (i, 1)), _full((Fh, D))],
        out_specs=[pl.BlockSpec((tt, 2 * Fh), lambda i: (i, 0)), _full((Fh, D))],
        out_shape=[jax.ShapeDtypeStruct((T, 2 * Fh), BF16), jax.ShapeDtypeStruct((Fh, D), BF16)],
        scratch_shapes=[pltpu.VMEM((Fh, D), F32)],
        compiler_params=_cp("arbitrary"), name=name)(dh, gu, gu, wd)


def _out_proj_bwd(dy, a1, a2, w, *, tt, name, head_dots=False):
    T, N = dy.shape
    K1, K2 = a1.shape[1], a2.shape[1]
    K = K1 + K2
    last = T // tt - 1

    def body(dy_ref, a1_ref, a2_ref, w_ref, da_ref, gw_ref, *rest):
        acc = rest[-1]

        @pl.when(pl.program_id(0) == 0)
        def _():
            acc[...] = jnp.zeros_like(acc)

        dy16 = dy_ref[...].astype(BF16)
        da = _dot_nt(dy16, w_ref[...])
        da_ref[...] = da
        acc[:K1, :] += _dot_tn(a1_ref[...], dy16)
        acc[K1:, :] += _dot_tn(a2_ref[...], dy16)
        if head_dots:
            for h in range(K1 // HEAD_DIM):
                sl = slice(h * HEAD_DIM, (h + 1) * HEAD_DIM)
                rest[0][:, sl] = jnp.broadcast_to(jnp.sum(da[:, sl] * a1_ref[:, sl], axis=-1, keepdims=True), (tt, HEAD_DIM))

        @pl.when(pl.program_id(0) == last)
        def _():
            gw_ref[...] = acc[...].astype(BF16)

    extra_specs = [pl.BlockSpec((tt, K1), lambda i: (i, 0))] if head_dots else []
    extra_shapes = [jax.ShapeDtypeStruct((T, K1), F32)] if head_dots else []
    return pl.pallas_call(
        body, grid=(T // tt,),
        in_specs=[pl.BlockSpec((tt, N), lambda i: (i, 0)), pl.BlockSpec((tt, K1), lambda i: (i, 0)),
                  pl.BlockSpec((tt, K2), lambda i: (i, 0)), _full((K, N))],
        out_specs=[pl.BlockSpec((tt, K), lambda i: (i, 0)), _full((K, N))] + extra_specs,
        out_shape=[jax.ShapeDtypeStruct((T, K), F32), jax.ShapeDtypeStruct((K, N), BF16)] + extra_shapes,
        scratch_shapes=[pltpu.VMEM((K, N), F32)],
        compiler_params=_cp("arbitrary"), name=name)(dy, a1, a2, w)


def _mm_tn(a, b, *, tt, tka, name):
    T, Ka = a.shape
    N = b.shape[1]
    last = T // tt - 1

    def body(a_ref, b_ref, o_ref, acc):
        @pl.when(pl.program_id(1) == 0)
        def _():
            acc[...] = jnp.zeros_like(acc)

        acc[...] += _dot_tn(a_ref[...], b_ref[...])

        @pl.when(pl.program_id(1) == last)
        def _():
            o_ref[...] = acc[...].astype(BF16)

    return pl.pallas_call(
        body, grid=(Ka // tka, T // tt),
        in_specs=[pl.BlockSpec((tt, tka), lambda j, t: (t, j)), pl.BlockSpec((tt, N), lambda j, t: (t, 0))],
        out_specs=pl.BlockSpec((tka, N), lambda j, t: (j, 0)),
        out_shape=jax.ShapeDtypeStruct((Ka, N), BF16),
        scratch_shapes=[pltpu.VMEM((tka, N), F32)],
        compiler_params=_cp("parallel", "arbitrary"), name=name)(a, b)


def _mm_tn_pieces(pieces, b, *, tt, name):
    n = len(pieces)
    T = b.shape[0]
    N = b.shape[1]
    widths = [p.shape[1] for p in pieces]
    Ka = sum(widths)
    last = T // tt - 1

    def body(*refs):
        p_refs = refs[:n]
        b_ref, o_ref, acc = refs[n:]

        @pl.when(pl.program_id(0) == 0)
        def _():
            acc[...] = jnp.zeros_like(acc)

        bv = b_ref[...].astype(BF16)
        off = 0
        for p_ref, wd in zip(p_refs, widths):
            acc[off:off + wd, :] += _dot_tn(p_ref[...], bv)
            off += wd

        @pl.when(pl.program_id(0) == last)
        def _():
            o_ref[...] = acc[...].astype(BF16)

    return pl.pallas_call(
        body, grid=(T // tt,),
        in_specs=[pl.BlockSpec((tt, wd), lambda t: (t, 0)) for wd in widths] + [pl.BlockSpec((tt, N), lambda t: (t, 0))],
        out_specs=_full((Ka, N)), out_shape=jax.ShapeDtypeStruct((Ka, N), BF16),
        scratch_shapes=[pltpu.VMEM((Ka, N), F32)],
        compiler_params=_cp("arbitrary"), name=name)(*pieces, b)


def _rms_bwd_dx(x, g, w, dy, dres, *, tt, wt, name, dep=None):
    pieces = list(dy) if isinstance(dy, (list, tuple)) else [dy]
    n = len(pieces)
    widths = [p.shape[1] for p in pieces]
    T, K = x.shape

    def kernel_body(x_ref, g_ref, w_ref, *rest):
        dy_refs = rest[:n]
        dres_ref, dx_ref, dg_ref = rest[n:]

        @pl.when(pl.program_id(0) == 0)
        def _():
            dg_ref[...] = jnp.zeros_like(dg_ref)

        if n == 1:
            dxn = (_dot if wt else _dot_nt)(dy_refs[0][...], w_ref[...])
        else:
            dxn, off = 0.0, 0
            for dy_ref, wd in zip(dy_refs, widths):
                dxn = dxn + _dot(dy_ref[...], w_ref[off:off + wd, :])
                off += wd
        xf = x_ref[...]
        r = lax.rsqrt(jnp.mean(xf * xf, axis=-1, keepdims=True) + EPS)
        xhat = xf * r
        dg_ref[...] += jnp.sum(dxn * xhat, axis=0, keepdims=True)
        dxhat = dxn * g_ref[...]
        dx_ref[...] = dres_ref[...] + r * (dxhat - xhat * jnp.mean(dxhat * xhat, axis=-1, keepdims=True))

    assert n == 1 or wt
    body, dep_specs, dep_args = _dep(kernel_body, 4 + n, dep)
    return pl.pallas_call(
        body, grid=(T // tt,),
        in_specs=[pl.BlockSpec((tt, K), lambda i: (i, 0)), _full((1, K)), _full(w.shape)]
        + [pl.BlockSpec((tt, wd), lambda i: (i, 0)) for wd in widths]
        + [pl.BlockSpec((tt, K), lambda i: (i, 0))] + dep_specs,
        out_specs=[pl.BlockSpec((tt, K), lambda i: (i, 0)), _full((1, K))],
        out_shape=[jax.ShapeDtypeStruct((T, K), F32), jax.ShapeDtypeStruct((1, K), F32)],
        compiler_params=_cp("arbitrary"), name=name)(x, g, w, *pieces, dres, *dep_args)


HGRN_TB = 512
HGRN_NCH = HGRN_TB // CHUNK
HGRN_HPB = 6


def _hgrn_chunk_fwd(q, z, lbv, tril01):
    sig = _sigmoid(z)
    f = lbv + (1.0 - lbv) * sig
    kk = 1.0 - f
    b = _dot3(tril01, jnp.log(f))
    bend = b[CHUNK - 1:CHUNK, :]
    sq = _sigmoid(q)
    eb = jnp.exp(b)
    emb = jnp.exp(-b)
    eo = jnp.exp(bend - b)
    dec = jnp.exp(bend)
    return sig, f, kk, sq, eb, emb, eo, dec


def _hgrn2_fwd(proj, lb, *, name):
    T = proj.shape[0]
    nT = T // HGRN_TB
    nC = T // CHUNK

    def body(q_ref, z_ref, v_ref, lb_ref, o_ref, st_ref, state):
        @pl.when(pl.program_id(1) == 0)
        def _():
            state[...] = jnp.zeros_like(state)

        row = lax.broadcasted_iota(jnp.int32, (CHUNK, CHUNK), 0)
        col = lax.broadcasted_iota(jnp.int32, (CHUNK, CHUNK), 1)
        causal = row >= col
        tril01 = causal.astype(BF16)

        def chunk(c, carry):
            rows = pl.ds(pl.multiple_of(c * CHUNK, CHUNK), CHUNK)
            for hh in range(HGRN_HPB):
                sl = slice(hh * HEAD_DIM, (hh + 1) * HEAD_DIM)
                q = q_ref[rows, sl]
                v = v_ref[rows, sl].astype(BF16)
                sig, f, kk, sq, eb, emb, eo, dec = _hgrn_chunk_fwd(q, z_ref[rows, sl], lb_ref[:, sl], tril01)
                qi = (q * sq * eb).astype(BF16)
                ki = (kk * emb).astype(BF16)
                ko = (kk * eo).astype(BF16)
                st = state[hh]
                att = jnp.where(causal, _dot_nt(qi, ki), 0.0)
                o_ref[rows, sl] = _dot(att, v) + _dot_nt(qi, st)
                st_ref[c, hh] = st
                state[hh] = st * dec + _dot_tn(v, ko)
            return carry

        lax.fori_loop(0, HGRN_NCH, chunk, 0)

    W = HGRN_HPB * HEAD_DIM
    nG = A_HEADS // HGRN_HPB
    hb = lambda off: pl.BlockSpec((HGRN_TB, W), lambda h, i: (i, off + h))
    return pl.pallas_call(
        body, grid=(nG, nT),
        in_specs=[hb(0), hb(nG), hb(2 * nG), pl.BlockSpec((1, W), lambda h, i: (0, h))],
        out_specs=[hb(0), pl.BlockSpec((HGRN_NCH, HGRN_HPB, HEAD_DIM, HEAD_DIM), lambda h, i: (i, h, 0, 0))],
        out_shape=[jax.ShapeDtypeStruct((T, A_WIDTH), F32), jax.ShapeDtypeStruct((nC, A_HEADS, HEAD_DIM, HEAD_DIM), F32)],
        scratch_shapes=[pltpu.VMEM((HGRN_HPB, HEAD_DIM, HEAD_DIM), F32)],
        compiler_params=_cp("parallel", "arbitrary"), name=name)(proj, proj, proj, lb)


def _hgrn2_bwd(proj, lb, st_all, do, *, name):
    T = proj.shape[0]
    nT = T // HGRN_TB

    def body(q_ref, z_ref, v_ref, lb_ref, st_ref, do_ref, dq_ref, dz_ref, dv_ref, dlb_ref, dstate):
        @pl.when(pl.program_id(1) == 0)
        def _():
            dstate[...] = jnp.zeros_like(dstate)
            dlb_ref[...] = jnp.zeros_like(dlb_ref)

        row = lax.broadcasted_iota(jnp.int32, (CHUNK, CHUNK), 0)
        col = lax.broadcasted_iota(jnp.int32, (CHUNK, CHUNK), 1)
        causal = row >= col
        tril01 = causal.astype(BF16)
        triu01 = (row <= col).astype(BF16)

        def chunk(cc, carry):
            c = HGRN_NCH - 1 - cc
            rows = pl.ds(pl.multiple_of(c * CHUNK, CHUNK), CHUNK)
            for hh in range(HGRN_HPB):
                sl = slice(hh * HEAD_DIM, (hh + 1) * HEAD_DIM)
                lbv = lb_ref[:, sl]
                q = q_ref[rows, sl]
                v = v_ref[rows, sl].astype(BF16)
                sig, f, kk, sq, eb, emb, eo, dec = _hgrn_chunk_fwd(q, z_ref[rows, sl], lbv, tril01)
                qi32 = q * sq * eb
                ki32 = kk * emb
                ko32 = kk * eo
                qi, ki, ko = qi32.astype(BF16), ki32.astype(BF16), ko32.astype(BF16)
                att = jnp.where(causal, _dot_nt(qi, ki), 0.0).astype(BF16)
                dout = do_ref[rows, sl].astype(BF16)
                st = st_ref[c, hh]
                dst = dstate[hh]
                dst16 = dst.astype(BF16)
                datt = jnp.where(causal, _dot_nt(dout, v), 0.0).astype(BF16)
                dqi = _dot(datt, ki) + _dot(dout, st)
                dki = _dot_tn(datt, qi)
                dv_ref[rows, sl] = (_dot_tn(att, dout) + _dot_nt(ko, dst16)).astype(BF16)
                dko = _dot(v, dst16)
                ddec = jnp.sum(dst * st, axis=0, keepdims=True)
                dstate[hh] = dst * dec + _dot_tn(dout, qi)
                dkk = dki * emb + dko * eo
                db = dqi * qi32 - dki * ki32 - dko * ko32
                dbend = jnp.sum(dko * ko32, axis=0, keepdims=True) + ddec * dec
                dlogf = _dot3(triu01, db) + dbend
                df = dlogf / f - dkk
                dz_ref[rows, sl] = (df * (1.0 - lbv) * sig * (1.0 - sig)).astype(BF16)
                dlb_ref[:, sl] += jnp.sum(df * (1.0 - sig), axis=0, keepdims=True)
                dq_ref[rows, sl] = (dqi * eb * (sq * (1.0 + q * (1.0 - sq)))).astype(BF16)
            return carry

        lax.fori_loop(0, HGRN_NCH, chunk, 0)

    W = HGRN_HPB * HEAD_DIM
    nG = A_HEADS // HGRN_HPB
    hb = lambda off: pl.BlockSpec((HGRN_TB, W), lambda h, i: (nT - 1 - i, off + h))
    hlb = pl.BlockSpec((1, W), lambda h, i: (0, h))
    o16 = jax.ShapeDtypeStruct((T, A_WIDTH), BF16)
    return pl.pallas_call(
        body, grid=(nG, nT),
        in_specs=[hb(0), hb(nG), hb(2 * nG), hlb,
                  pl.BlockSpec((HGRN_NCH, HGRN_HPB, HEAD_DIM, HEAD_DIM), lambda h, i: (nT - 1 - i, h, 0, 0)), hb(0)],
        out_specs=[hb(0), hb(0), hb(0), hlb],
        out_shape=[o16, o16, o16, jax.ShapeDtypeStruct((1, A_WIDTH), F32)],
        scratch_shapes=[pltpu.VMEM((HGRN_HPB, HEAD_DIM, HEAD_DIM), F32)],
        compiler_params=_cp("parallel", "arbitrary"), name=name)(proj, proj, proj, lb, st_all, do)


def _head_rms(x):
    r = lax.rsqrt(jnp.mean(x * x, axis=-1, keepdims=True) + EPS)
    return x * r, r


def _head_rms_bwd(dxhat, xhat, r):
    return r * (dxhat - xhat * jnp.mean(dxhat * xhat, axis=-1, keepdims=True))


def _a_post_fwd(o, proj, onorm, *, tt, name):
    T = o.shape[0]

    def body(o_ref, g_ref, w_ref, y_ref):
        for h in range(A_HEADS):
            sl = slice(h * HEAD_DIM, (h + 1) * HEAD_DIM)
            xhat, _ = _head_rms(o_ref[:, sl])
            g = g_ref[:, sl]
            y_ref[:, sl] = xhat * w_ref[:, sl] * (g * _sigmoid(g))

    blk = lambda c: pl.BlockSpec((tt, A_WIDTH), lambda i: (i, c))
    return pl.pallas_call(
        body, grid=(T // tt,), in_specs=[blk(0), blk(3), _full((1, A_WIDTH))], out_specs=blk(0),
        out_shape=jax.ShapeDtypeStruct((T, A_WIDTH), F32),
        compiler_params=_cp("parallel"), name=name)(o, proj, onorm)


def _a_post_bwd(o, proj, onorm, dmix, *, tt, name, dep=None):
    T = o.shape[0]

    def kernel_body(o_ref, g_ref, w_ref, dy_ref, do_ref, dg_ref, dw_ref):
        @pl.when(pl.program_id(0) == 0)
        def _():
            dw_ref[...] = jnp.zeros_like(dw_ref)

        for h in range(A_HEADS):
            sl = slice(h * HEAD_DIM, (h + 1) * HEAD_DIM)
            xhat, r = _head_rms(o_ref[:, sl])
            g = g_ref[:, sl]
            s = _sigmoid(g)
            dy = dy_ref[:, sl]
            w = w_ref[:, sl]
            dg_ref[:, sl] = (dy * xhat * w * (s * (1.0 + g * (1.0 - s)))).astype(BF16)
            dyn = dy * (g * s)
            dw_ref[:, sl] += jnp.sum(dyn * xhat, axis=0, keepdims=True)
            do_ref[:, sl] = _head_rms_bwd(dyn * w, xhat, r)

    blk = lambda c: pl.BlockSpec((tt, A_WIDTH), lambda i: (i, c))
    body, dep_specs, dep_args = _dep(kernel_body, 4, dep)
    return pl.pallas_call(
        body, grid=(T // tt,), in_specs=[blk(0), blk(3), _full((1, A_WIDTH)), blk(0)] + dep_specs,
        out_specs=[blk(0), blk(0), _full((1, A_WIDTH))],
        out_shape=[jax.ShapeDtypeStruct((T, A_WIDTH), F32), jax.ShapeDtypeStruct((T, A_WIDTH), BF16),
                   jax.ShapeDtypeStruct((1, A_WIDTH), F32)],
        compiler_params=_cp("arbitrary"), name=name)(o, proj, onorm, dmix, *dep_args)


def _mem_head_masks(n):
    lane = lax.broadcasted_iota(jnp.int32, (n, MEM_WIDTH), 1)
    return [(lane >= m * MEM_HEAD_DIM) & (lane < (m + 1) * MEM_HEAD_DIM) for m in range(MEM_HEADS)]


def _mem_head_rms(x, masks):
    x2 = x * x
    r = jnp.zeros_like(x)
    for mk in masks:
        ms = jnp.sum(jnp.where(mk, x2, 0.0), axis=-1, keepdims=True) * (1.0 / MEM_HEAD_DIM)
        r = jnp.where(mk, lax.rsqrt(ms + EPS), r)
    return x * r, r


def _mem_head_rms_bwd(dxhat, xhat, r, masks):
    t = dxhat * xhat
    m = jnp.zeros_like(t)
    for mk in masks:
        m = jnp.where(mk, jnp.sum(jnp.where(mk, t, 0.0), axis=-1, keepdims=True) * (1.0 / MEM_HEAD_DIM), m)
    return r * (dxhat - xhat * m)


MEM_SCALE = MEM_HEAD_DIM ** -0.5


def _mem_attn_fwd(proj, qcol, mkv, qn_w, kn_w, *, tt, name):
    T = proj.shape[0]

    def body(q_ref, k_ref, v_ref, qw_ref, kw_ref, o_ref):
        qmasks = _mem_head_masks(tt)
        kmasks = _mem_head_masks(MEM_TOKENS)
        qhat, _ = _mem_head_rms(q_ref[...], qmasks)
        qn = qhat * qw_ref[...]
        khat, _ = _mem_head_rms(k_ref[...], kmasks)
        kn = (khat * kw_ref[...]).astype(BF16)
        v = v_ref[...].astype(BF16)
        out = jnp.zeros((tt, MEM_WIDTH), F32)
        for m in range(MEM_HEADS):
            s = _dot_nt(jnp.where(qmasks[m], qn, 0.0), kn) * MEM_SCALE
            s = s - jnp.max(s, axis=-1, keepdims=True)
            p = jnp.exp(s)
            p = p / jnp.sum(p, axis=-1, keepdims=True)
            out = jnp.where(qmasks[m], _dot(p, v), out)
        o_ref[...] = out

    return pl.pallas_call(
        body, grid=(T // tt,),
        in_specs=[pl.BlockSpec((tt, MEM_WIDTH), lambda i: (i, qcol)), pl.BlockSpec((MEM_TOKENS, MEM_WIDTH), lambda i: (0, 0)),
                  pl.BlockSpec((MEM_TOKENS, MEM_WIDTH), lambda i: (0, 1)), _full((1, MEM_WIDTH)), _full((1, MEM_WIDTH))],
        out_specs=pl.BlockSpec((tt, MEM_WIDTH), lambda i: (i, 0)),
        out_shape=jax.ShapeDtypeStruct((T, MEM_WIDTH), F32),
        compiler_params=_cp("parallel"), name=name)(proj, mkv, mkv, qn_w, kn_w)


def _mem_attn_bwd(proj, qcol, mkv, qn_w, kn_w, dmix, *, tt, name):
    T = proj.shape[0]
    nsteps = T // tt
    ocol = (dmix.shape[1] - MEM_WIDTH) // MEM_WIDTH

    def body(q_ref, k_ref, v_ref, qw_ref, kw_ref, do_ref, dq_ref, dkv_ref, dqw_ref, dkw_ref, dk_acc, dv_acc):
        step = pl.program_id(0)

        @pl.when(step == 0)
        def _():
            dk_acc[...] = jnp.zeros_like(dk_acc)
            dv_acc[...] = jnp.zeros_like(dv_acc)
            dqw_ref[...] = jnp.zeros_like(dqw_ref)

        qmasks = _mem_head_masks(tt)
        kmasks = _mem_head_masks(MEM_TOKENS)
        qhat, qr = _mem_head_rms(q_ref[...], qmasks)
        qn = qhat * qw_ref[...]
        khat, kr = _mem_head_rms(k_ref[...], kmasks)
        kn = (khat * kw_ref[...]).astype(BF16)
        v = v_ref[...].astype(BF16)
        dout = do_ref[...]
        dqn = jnp.zeros((tt, MEM_WIDTH), F32)
        dkn = jnp.zeros((MEM_TOKENS, MEM_WIDTH), F32)
        dvv = jnp.zeros((MEM_TOKENS, MEM_WIDTH), F32)
        for m in range(MEM_HEADS):
            qm = jnp.where(qmasks[m], qn, 0.0).astype(BF16)
            s = _dot_nt(qm, kn) * MEM_SCALE
            s = s - jnp.max(s, axis=-1, keepdims=True)
            p = jnp.exp(s)
            p = p / jnp.sum(p, axis=-1, keepdims=True)
            dom = jnp.where(qmasks[m], dout, 0.0).astype(BF16)
            dp = _dot_nt(dom, v)
            ds = (p * (dp - jnp.sum(p * dp, axis=-1, keepdims=True)) * MEM_SCALE).astype(BF16)
            dqn = jnp.where(qmasks[m], _dot(ds, kn), dqn)
            dkn = jnp.where(kmasks[m], _dot_tn(ds, qm), dkn)
            dvv = jnp.where(kmasks[m], _dot_tn(p, dom), dvv)
        dqw_ref[...] += jnp.sum(dqn * qhat, axis=0, keepdims=True)
        dq_ref[...] = _mem_head_rms_bwd(dqn * qw_ref[...], qhat, qr, qmasks).astype(BF16)
        dk_acc[...] += dkn
        dv_acc[...] += dvv

        @pl.when(step == nsteps - 1)
        def _():
            dk = dk_acc[...]
            dkw_ref[...] = jnp.sum(dk * khat, axis=0, keepdims=True)
            dkv_ref[:, :MEM_WIDTH] = _mem_head_rms_bwd(dk * kw_ref[...], khat, kr, kmasks)
            dkv_ref[:, MEM_WIDTH:] = dv_acc[...]

    return pl.pallas_call(
        body, grid=(nsteps,),
        in_specs=[pl.BlockSpec((tt, MEM_WIDTH), lambda i: (i, qcol)), pl.BlockSpec((MEM_TOKENS, MEM_WIDTH), lambda i: (0, 0)),
                  pl.BlockSpec((MEM_TOKENS, MEM_WIDTH), lambda i: (0, 1)), _full((1, MEM_WIDTH)), _full((1, MEM_WIDTH)),
                  pl.BlockSpec((tt, MEM_WIDTH), lambda i: (i, ocol))],
        out_specs=[pl.BlockSpec((tt, MEM_WIDTH), lambda i: (i, 0)), _full((MEM_TOKENS, 2 * MEM_WIDTH)),
                   _full((1, MEM_WIDTH)), _full((1, MEM_WIDTH))],
        out_shape=[jax.ShapeDtypeStruct((T, MEM_WIDTH), BF16), jax.ShapeDtypeStruct((MEM_TOKENS, 2 * MEM_WIDTH), F32),
                   jax.ShapeDtypeStruct((1, MEM_WIDTH), F32), jax.ShapeDtypeStruct((1, MEM_WIDTH), F32)],
        scratch_shapes=[pltpu.VMEM((MEM_TOKENS, MEM_WIDTH), F32), pltpu.VMEM((MEM_TOKENS, MEM_WIDTH), F32)],
        compiler_params=_cp("arbitrary"), name=name)(proj, mkv, mkv, qn_w, kn_w, dmix)


HALF = HEAD_DIM // 2
ATT_SCALE = HEAD_DIM ** -0.5
NEG = -1e30


def _rope_tables(T):
    inv = np.float32(ROPE_THETA) ** (-np.arange(HALF, dtype=np.float32) / np.float32(HALF))
    ang = np.arange(T, dtype=np.float32)[:, None] * inv[None, :].astype(np.float32)
    cos, sin = np.cos(ang).astype(np.float32), np.sin(ang).astype(np.float32)
    return jnp.asarray(np.concatenate([cos, cos], axis=-1)), jnp.asarray(np.concatenate([-sin, sin], axis=-1))


def _rope(x, cosf, sinsg):
    return x * cosf + pltpu.roll(x, HALF, 1) * sinsg


def _rope_bwd(dy, cosf, sinsg):
    return dy * cosf + pltpu.roll(dy * sinsg, HALF, 1)


def _q_prep_bwd(proj, w_heads, cosf, sinsg, dqs, *, tt, name):
    T = proj.shape[0]
    W = N_GROUPS * B_WIDTH

    def body(x_ref, w_ref, c_ref, s_ref, d0, d1, d2, dx_ref, dw_ref):
        @pl.when(pl.program_id(0) == 0)
        def _():
            dw_ref[...] = jnp.zeros_like(dw_ref)

        c, s = c_ref[...], s_ref[...]
        for gi, d_ref in enumerate((d0, d1, d2)):
            for h in range(B_HEADS):
                sl = slice((gi * B_HEADS + h) * HEAD_DIM, (gi * B_HEADS + h + 1) * HEAD_DIM)
                xhat, r = _head_rms(x_ref[:, sl])
                dyn = _rope_bwd(d_ref[:, h * HEAD_DIM:(h + 1) * HEAD_DIM], c, s)
                dw_ref[:, sl] += jnp.sum(dyn * xhat, axis=0, keepdims=True)
                dx_ref[:, sl] = _head_rms_bwd(dyn * w_ref[:, sl], xhat, r).astype(BF16)

    tbl = pl.BlockSpec((tt, HEAD_DIM), lambda i: (i, 0))
    dyb = pl.BlockSpec((tt, B_WIDTH), lambda i: (i, 0))
    return pl.pallas_call(
        body, grid=(T // tt,),
        in_specs=[pl.BlockSpec((tt, W), lambda i: (i, 0)), _full((1, W)), tbl, tbl, dyb, dyb, dyb],
        out_specs=[pl.BlockSpec((tt, W), lambda i: (i, 0)), _full((1, W))],
        out_shape=[jax.ShapeDtypeStruct((T, W), BF16), jax.ShapeDtypeStruct((1, W), F32)],
        compiler_params=_cp("arbitrary"), name=name)(proj, w_heads, cosf, sinsg, *dqs)


def _kv_prep_bwd(kv, w_heads, cosf, sinsg, dks, dvs, *, tt, name):
    T = kv.shape[0]

    def body(x_ref, w_ref, c_ref, s_ref, k0, k1, k2, v0, v1, v2, dx_ref, dw_ref):
        @pl.when(pl.program_id(0) == 0)
        def _():
            dw_ref[...] = jnp.zeros_like(dw_ref)

        c, s = c_ref[...], s_ref[...]
        for h in range(B_HEADS):
            sl = slice(h * HEAD_DIM, (h + 1) * HEAD_DIM)
            vs = slice(B_WIDTH + h * HEAD_DIM, B_WIDTH + (h + 1) * HEAD_DIM)
            xhat, r = _head_rms(x_ref[:, sl])
            dyn = _rope_bwd(k0[:, sl] + k1[:, sl] + k2[:, sl], c, s)
            dw_ref[:, sl] += jnp.sum(dyn * xhat, axis=0, keepdims=True)
            dx_ref[:, sl] = _head_rms_bwd(dyn * w_ref[:, sl], xhat, r).astype(BF16)
            dx_ref[:, vs] = (v0[:, sl] + v1[:, sl] + v2[:, sl]).astype(BF16)

    tbl = pl.BlockSpec((tt, HEAD_DIM), lambda i: (i, 0))
    dyb = pl.BlockSpec((tt, B_WIDTH), lambda i: (i, 0))
    return pl.pallas_call(
        body, grid=(T // tt,),
        in_specs=[dyb, _full((1, B_WIDTH)), tbl, tbl] + [dyb] * 6,
        out_specs=[pl.BlockSpec((tt, 2 * B_WIDTH), lambda i: (i, 0)), _full((1, B_WIDTH))],
        out_shape=[jax.ShapeDtypeStruct((T, 2 * B_WIDTH), BF16), jax.ShapeDtypeStruct((1, B_WIDTH), F32)],
        compiler_params=_cp("arbitrary"), name=name)(kv, w_heads, cosf, sinsg, *dks, *dvs)


def _band_masks(n_is_first=None):
    row = lax.broadcasted_iota(jnp.int32, (SPAN, SPAN), 0)
    col = lax.broadcasted_iota(jnp.int32, (SPAN, SPAN), 1)
    return row >= col, col >= row


def _dil_views(T, d):
    L = T // d
    return L, L // SPAN


def _dil_fwd(qr, kr, kv, gi, d, *, name):
    T = qr.shape[0]
    L, nb = _dil_views(T, d)

    def body(q_ref, kc_ref, kp_ref, vc_ref, vp_ref, o_ref, lse_ref):
        cur_ok, prev_band = _band_masks()
        prev_ok = prev_band & (pl.program_id(1) > 0)
        for h in range(B_HEADS):
            sl = slice(h * HEAD_DIM, (h + 1) * HEAD_DIM)
            q = q_ref[:, sl]
            sc = jnp.where(cur_ok, _dot_nt(q, kc_ref[:, sl]) * ATT_SCALE, NEG)
            sp = jnp.where(prev_ok, _dot_nt(q, kp_ref[:, sl]) * ATT_SCALE, NEG)
            m = jnp.maximum(jnp.max(sc, axis=-1, keepdims=True), jnp.max(sp, axis=-1, keepdims=True))
            pc = jnp.exp(sc - m)
            pp = jnp.exp(sp - m)
            l = jnp.sum(pc, axis=-1, keepdims=True) + jnp.sum(pp, axis=-1, keepdims=True)
            o_ref[:, sl] = (_dot(pc, vc_ref[:, sl]) + _dot(pp, vp_ref[:, sl])) / l
            lse_ref[:, sl] = jnp.broadcast_to(m + jnp.log(l), (SPAN, HEAD_DIM))

    blk = lambda f: pl.BlockSpec((SPAN, B_WIDTH), f)
    cur = lambda r, n: (n, r)
    prev = lambda r, n: (jnp.maximum(n - 1, 0), r)
    ov = jax.ShapeDtypeStruct((L, d * B_WIDTH), F32)
    o, lse = pl.pallas_call(
        body, grid=(d, nb),
        in_specs=[blk(lambda r, n: (n, r * N_GROUPS + gi)), blk(cur), blk(prev),
                  blk(lambda r, n: (n, 2 * r + 1)), blk(lambda r, n: (jnp.maximum(n - 1, 0), 2 * r + 1))],
        out_specs=[blk(cur), blk(cur)], out_shape=[ov, ov],
        compiler_params=_cp("parallel", "arbitrary"), name=name,
    )(qr.reshape(L, d * N_GROUPS * B_WIDTH), kr.reshape(L, d * B_WIDTH), kr.reshape(L, d * B_WIDTH),
      kv.reshape(L, d * 2 * B_WIDTH), kv.reshape(L, d * 2 * B_WIDTH))
    return o.reshape(T, B_WIDTH), lse.reshape(T, B_WIDTH)


def _dil_combine_fwd(os_, lses, *, tt, name):
    T = os_[0].shape[0]

    def body(o0, o1, o2, l0, l1, l2, y_ref, lse_ref):
        a, b, c = l0[...], l1[...], l2[...]
        m = jnp.maximum(jnp.maximum(a, b), c)
        wa, wb, wc = jnp.exp(a - m), jnp.exp(b - m), jnp.exp(c - m)
        den = wa + wb + wc
        y_ref[...] = (wa * o0[...] + wb * o1[...] + wc * o2[...]) / den
        lse_ref[...] = m + jnp.log(den)

    blk = pl.BlockSpec((tt, B_WIDTH), lambda i: (i, 0))
    sh = jax.ShapeDtypeStruct((T, B_WIDTH), F32)
    return pl.pallas_call(
        body, grid=(T // tt,), in_specs=[blk] * 6, out_specs=[blk, blk], out_shape=[sh, sh],
        compiler_params=_cp("parallel"), name=name)(*os_, *lses)


DILS_UNROLL = 4


def _dils_specs(gi, d, nblk):
    blk = lambda f: pl.BlockSpec((SPAN * d, HEAD_DIM), f)
    return {
        "q": blk(lambda h, n: (n, gi * B_HEADS + h)), "q_next": blk(lambda h, n: (jnp.minimum(n + 1, nblk - 1), gi * B_HEADS + h)),
        "cur": blk(lambda h, n: (n, h)), "prev": blk(lambda h, n: (jnp.maximum(n - 1, 0), h)),
        "next": blk(lambda h, n: (jnp.minimum(n + 1, nblk - 1), h)),
        "v": blk(lambda h, n: (n, B_HEADS + h)), "v_prev": blk(lambda h, n: (jnp.maximum(n - 1, 0), B_HEADS + h)),
    }


def _dils_fwd(qr, kr, kv, gi, d, *, name):
    T = qr.shape[0]
    nblk = T // (SPAN * d)
    sp = _dils_specs(gi, d, nblk)

    def body(q_ref, kc_ref, vc_ref, o_ref, lse_ref, k_before, v_before):
        @pl.when(pl.program_id(1) == 0)
        def _():
            k_before[...] = jnp.zeros_like(k_before)
            v_before[...] = jnp.zeros_like(v_before)

        cur_ok, prev_band = _band_masks()
        prev_ok = prev_band & (pl.program_id(1) > 0)

        def residue(r, carry):
            rows = pl.ds(r, SPAN, stride=d)
            q, kc, vc = q_ref[rows, :], kc_ref[rows, :].astype(BF16), vc_ref[rows, :].astype(BF16)
            sc = jnp.where(cur_ok, _dot_nt(q, kc) * ATT_SCALE, NEG)
            sp_ = jnp.where(prev_ok, _dot_nt(q, k_before[r]) * ATT_SCALE, NEG)
            m = jnp.maximum(jnp.max(sc, axis=-1, keepdims=True), jnp.max(sp_, axis=-1, keepdims=True))
            pc = jnp.exp(sc - m)
            pp = jnp.exp(sp_ - m)
            l = jnp.sum(pc, axis=-1, keepdims=True) + jnp.sum(pp, axis=-1, keepdims=True)
            o_ref[rows, :] = (_dot(pc, vc) + _dot(pp, v_before[r])) / l
            lse_ref[rows, :] = jnp.broadcast_to(m + jnp.log(l), (SPAN, HEAD_DIM))
            k_before[r] = kc
            v_before[r] = vc
            return carry

        lax.fori_loop(0, d, residue, 0, unroll=DILS_UNROLL)

    sh = jax.ShapeDtypeStruct((T, B_WIDTH), F32)
    return pl.pallas_call(
        body, grid=(B_HEADS, nblk), in_specs=[sp["q"], sp["cur"], sp["v"]],
        out_specs=[sp["cur"], sp["cur"]], out_shape=[sh, sh],
        scratch_shapes=[pltpu.VMEM((d, SPAN, HEAD_DIM), BF16), pltpu.VMEM((d, SPAN, HEAD_DIM), BF16)],
        compiler_params=_cp("parallel", "arbitrary"), name=name)(qr, kr, kv)


DIL_BWD_GROUP = {1: 4, 4: 1, 16: 1}


def _dil_bwd(qr, kr, kv, dmix, lse, dd, gi, d, *, name, dep=None):
    T = qr.shape[0]
    G = DIL_BWD_GROUP[d]
    band = SPAN * d
    tb = G * band
    nblk = T // tb
    n_units = T // SPAN

    keep = G == 1

    def kernel_body(q_ref, dy_ref, lse_ref, dd_ref, kc_ref, vc_ref, *rest):
        if keep:
            dq_ref, dk_ref, dv_ref, dk_acc, dv_acc, k_before, v_before = rest
        else:
            kp_ref, vp_ref, dq_ref, dk_ref, dv_ref, dk_acc, dv_acc = rest
        n = pl.program_id(1)

        @pl.when(n == 0)
        def _():
            dk_acc[...] = jnp.zeros_like(dk_acc)
            dv_acc[...] = jnp.zeros_like(dv_acc)
            if keep:
                k_before[...] = jnp.zeros_like(k_before)
                v_before[...] = jnp.zeros_like(v_before)

        cur_ok, prev_band = _band_masks()
        for j in range(G):
            def residue(r, carry, j=j):
                off = j * band + r
                rows = pl.ds(off, SPAN, stride=d)
                q, dy = q_ref[rows, :], dy_ref[rows, :]
                lse_h = jnp.max(lse_ref[rows, :], axis=-1, keepdims=True)
                dd_h = jnp.max(dd_ref[rows, :], axis=-1, keepdims=True)
                kc, vc = kc_ref[rows, :].astype(BF16), vc_ref[rows, :].astype(BF16)
                if j > 0:
                    before = pl.ds(off - band, SPAN, stride=d)
                    kp, vp = kc_ref[before, :], vc_ref[before, :]
                    prev_ok = prev_band
                elif keep:
                    kp, vp = k_before[r], v_before[r]
                    k_before[r] = kc
                    v_before[r] = vc
                    prev_ok = prev_band & (n > 0)
                else:
                    before = pl.ds((G - 1) * band + r, SPAN, stride=d)
                    kp, vp = kp_ref[before, :], vp_ref[before, :]
                    prev_ok = prev_band & (n > 0)
                pc = jnp.exp(jnp.where(cur_ok, _dot_nt(q, kc) * ATT_SCALE, NEG) - lse_h)
                pp = jnp.exp(jnp.where(prev_ok, _dot_nt(q, kp) * ATT_SCALE, NEG) - lse_h)
                dsc = pc * (_dot_nt(dy, vc) - dd_h) * ATT_SCALE
                dsp = pp * (_dot_nt(dy, vp) - dd_h) * ATT_SCALE
                dq_ref[rows, :] = _dot(dsc, kc) + _dot(dsp, kp)
                u = (n * G + j) * d + r
                here = pl.ds(pl.multiple_of(u * SPAN, SPAN), SPAN)
                dk_acc[here, :] += _dot_tn(dsc, q)
                dv_acc[here, :] += _dot_tn(pc, dy)
                there = pl.ds(pl.multiple_of(jnp.maximum(u - d, 0) * SPAN, SPAN), SPAN)
                dk_acc[there, :] += _dot_tn(dsp, q)
                dv_acc[there, :] += _dot_tn(pp, dy)
                return carry

            lax.fori_loop(0, d, residue, 0, unroll=min(d, DILS_UNROLL))

        @pl.when(n == nblk - 1)
        def _():
            def place(u, carry):
                rows = pl.ds((u // d) * band + u % d, SPAN, stride=d)
                src = pl.ds(pl.multiple_of(u * SPAN, SPAN), SPAN)
                dk_ref[rows, :] = dk_acc[src, :]
                dv_ref[rows, :] = dv_acc[src, :]
                return carry

            lax.fori_loop(0, n_units, place, 0)

    blk = lambda f: pl.BlockSpec((tb, HEAD_DIM), f)
    cur = lambda h, n: (n, h)
    prev = lambda h, n: (jnp.maximum(n - 1, 0), h)
    whole = pl.BlockSpec((T, HEAD_DIM), lambda h, n: (0, h))
    sh = jax.ShapeDtypeStruct((T, B_WIDTH), F32)
    v_cur = blk(lambda h, n: (n, B_HEADS + h))
    if keep:
        kv_specs, kv_args = [blk(cur), v_cur], [kr, kv]
        kept = [pltpu.VMEM((d, SPAN, HEAD_DIM), BF16), pltpu.VMEM((d, SPAN, HEAD_DIM), BF16)]
    else:
        kv_specs = [blk(cur), v_cur, blk(prev), blk(lambda h, n: (jnp.maximum(n - 1, 0), B_HEADS + h))]
        kv_args, kept = [kr, kv, kr, kv], []
    body, dep_specs, dep_args = _dep(kernel_body, 4 + len(kv_args), dep)
    return pl.pallas_call(
        body, grid=(B_HEADS, nblk),
        in_specs=[blk(lambda h, n: (n, gi * B_HEADS + h)), blk(cur), blk(cur), blk(cur)] + kv_specs + dep_specs,
        out_specs=[blk(cur), whole, whole], out_shape=[sh, sh, sh],
        scratch_shapes=[pltpu.VMEM((T, HEAD_DIM), F32), pltpu.VMEM((T, HEAD_DIM), F32)] + kept,
        compiler_params=_cp("parallel", "arbitrary"), name=name)(qr, dmix, lse, dd, *kv_args, *dep_args)


A_MQ_COL = 4 * A_WIDTH // MEM_WIDTH
B_MQ_COL = N_GROUPS * B_WIDTH // MEM_WIDTH


def _row(v):
    return v.reshape(1, -1).astype(F32)


def _local_step(x, mem, tgt, get_w, P, put_g, first_dep=None, forward_point=lambda i, value: value):
    T = x.shape[0]
    cosf, sinsg = _rope_tables(T)
    lb_soft = jax.nn.softmax(P["a_lb_logits"].astype(F32), axis=0)
    lb = lb_soft[0:1]
    qw_heads = jnp.repeat(P["b_qnorm"][0], B_HEADS, axis=0).reshape(1, -1)
    kw_heads = jnp.tile(_row(P["b_knorm"]), (1, B_HEADS))
    mqw = [jnp.tile(_row(P["mem_qnorm"][l]), (1, MEM_HEADS)) for l in range(2)]
    mkw = [jnp.tile(_row(P["mem_knorm"][l]), (1, MEM_HEADS)) for l in range(2)]
    nmix = [_row(P["norm_mix"][l]) for l in range(2)]
    nffn = [_row(P["norm_ffn"][l]) for l in range(2)]
    mnorm = [_row(P["mem_norm"][l]) for l in range(2)]
    kvn = _row(P["kv_norm"])
    onorm = _row(P["a_onorm"])
    W = {}

    def w_of(name, after=None):
        if name not in W:
            W[name] = get_w(name, after)
        return W[name]

    proj_a, xn0 = _rms_matmul(x, nmix[0], w_of("a_w_in"), tt=512, tn=1664, wt=True, name="proj_a", dep=first_dep)
    mkv0, mn0 = _rms_matmul(mem, mnorm[0], w_of("w_mem_kv0"), tt=MEM_TOKENS, tn=2 * MEM_WIDTH, wt=False, name="mem_kv0")
    o_raw, st = _hgrn2_fwd(proj_a, lb, name="hgrn2_fwd")
    o_raw = forward_point(0, o_raw)
    mm0 = _a_post_fwd(o_raw, proj_a, onorm, tt=512, name="a_post_fwd")
    mo0 = _mem_attn_fwd(proj_a, A_MQ_COL, mkv0, mqw[0], mkw[0], tt=1024, name="mem_attn_fwd0")
    hm0 = _mm_res(x, mm0, mo0, w_of("w_out0", mo0), tt=512, name="out_proj0")
    hm0 = forward_point(1, hm0)
    gu0, hn0 = _rms_matmul(hm0, nffn[0], w_of("w_gate_up0", hm0), tt=512, tn=1408, wt=True, out_dtype=BF16, name="gate_up0")
    h1 = _swiglu_down(hm0, gu0, w_of("w_down0", gu0), tt=512, name="down0")
    h1 = forward_point(2, h1)
    kv, hkn, kr = _rms_matmul(h1, kvn, w_of("w_kv", h1), tt=512, tn=768, wt=True, name="kv_proj",
                              rotate=(kw_heads, cosf, sinsg))

    proj_b, xn1, qr = _rms_matmul(h1, nmix[1], w_of("b_w_in", kr), tt=512, tn=1280, wt=True, name="proj_b",
                                  rotate=(qw_heads, cosf, sinsg))
    proj_b = forward_point(3, proj_b)
    mkv1, mn1 = _rms_matmul(mem, mnorm[1], w_of("w_mem_kv1", kr), tt=MEM_TOKENS, tn=2 * MEM_WIDTH, wt=False, name="mem_kv1")
    outs = [(_dil_fwd if d == 1 else _dils_fwd)(qr, kr, kv, gi, d, name=f"dil_fwd{gi}") for gi, d in enumerate(DILATIONS)]
    mm1, lse_tot = _dil_combine_fwd([o for o, _ in outs], [s for _, s in outs], tt=512, name="dil_combine")
    mo1 = _mem_attn_fwd(proj_b, B_MQ_COL, mkv1, mqw[1], mkw[1], tt=1024, name="mem_attn_fwd1")
    hm1 = _mm_res(h1, mm1, mo1, w_of("w_out1", mo1), tt=512, name="out_proj1")
    gu1, hn1 = _rms_matmul(hm1, nffn[1], w_of("w_gate_up1", hm1), tt=512, tn=1408, wt=True, out_dtype=BF16, name="gate_up1")
    dy, sq = _swiglu_down_loss(hm1, gu1, w_of("w_down1", gu1), tgt, tt=512, name="down1_loss")

    gP = {}
    zeros_mem = jnp.zeros((MEM_TOKENS, D_MODEL), F32)

    def ffn_bwd(l, dh, hm, gu, hn):
        dgu, g_wd = _swiglu_bwd(dh, gu, w_of(f"w_down{l}"), tt=256, name=f"swiglu_bwd{l}")
        g_wgu = _mm_tn(dgu, hn, tt=512, tka=1408, name=f"g_w_gate_up{l}")
        sent = put_g({f"w_down{l}": g_wd, f"w_gate_up{l}": g_wgu})
        dhm, g_nf = _rms_bwd_dx(hm, nffn[l], w_of(f"w_gate_up{l}"), dgu, dh, tt=512, wt=True, name=f"gate_up_bwd{l}", dep=sent)
        return dhm, g_nf

    def mix_bwd(l, dhm, mix_main, mix_mem, proj, qcol, mkv, mn):
        dmix, g_wout, *head_dots = _out_proj_bwd(dhm, mix_main, mix_mem, w_of(f"w_out{l}"), tt=512, name=f"out_proj_bwd{l}",
                                                 head_dots=l == 1)
        dmq, dmkv, dqw, dkw = _mem_attn_bwd(proj, qcol, mkv, mqw[l], mkw[l], dmix, tt=1024, name=f"mem_attn_bwd{l}")
        g_wmkv = _mm_tn(mn, dmkv, tt=MEM_TOKENS, tka=512, name=f"g_w_mem_kv{l}")
        sent = put_g({f"w_out{l}": g_wout, f"w_mem_kv{l}": g_wmkv})
        _, g_mn = _rms_bwd_dx(mem, mnorm[l], w_of(f"w_mem_kv{l}"), dmkv, zeros_mem, tt=MEM_TOKENS, wt=False, name=f"mem_kv_bwd{l}")
        fold = lambda v: v.reshape(MEM_HEADS, MEM_HEAD_DIM).sum(axis=0)
        return dmix, dmq, g_mn, fold(dqw), fold(dkw), sent, head_dots

    dhm1, g_nf1 = ffn_bwd(1, dy, hm1, gu1, hn1)
    dmix1, dmq1, g_mn1, g_mq1, g_mk1, sent, (dd,) = mix_bwd(1, dhm1, mm1, mo1, proj_b, B_MQ_COL, mkv1, mn1)
    dqs, dks, dvs = [], [], []
    for gi, d in enumerate(DILATIONS):
        dq_g, dk_g, dv_g = _dil_bwd(qr, kr, kv, dmix1, lse_tot, dd, gi, d, name=f"dil_bwd{gi}", dep=sent if gi == 0 else None)
        dqs.append(dq_g)
        dks.append(dk_g)
        dvs.append(dv_g)
    dq_raw, dqw = _q_prep_bwd(proj_b, qw_heads, cosf, sinsg, dqs, tt=512, name="q_prep_bwd")
    dkv, dkw = _kv_prep_bwd(kv, kw_heads, cosf, sinsg, dks, dvs, tt=512, name="kv_prep_bwd")
    dproj_b = [dq_raw, dmq1]
    g_wb = _mm_tn_pieces(dproj_b, xn1, tt=512, name="g_b_w_in")
    g_wkv = _mm_tn(dkv, hkn, tt=512, tka=768, name="g_w_kv")
    sent = put_g({"b_w_in": g_wb, "w_kv": g_wkv})
    dh1, g_nm1 = _rms_bwd_dx(h1, nmix[1], w_of("b_w_in"), dproj_b, dhm1, tt=512, wt=True, name="proj_b_bwd", dep=sent)
    dh1, g_kvn = _rms_bwd_dx(h1, kvn, w_of("w_kv"), dkv, dh1, tt=512, wt=True, name="kv_proj_bwd")

    dhm0, g_nf0 = ffn_bwd(0, dh1, hm0, gu0, hn0)
    dmix0, dmq0, g_mn0, g_mq0, g_mk0, sent, _ = mix_bwd(0, dhm0, mm0, mo0, proj_a, A_MQ_COL, mkv0, mn0)
    do_raw, dg, g_onorm = _a_post_bwd(o_raw, proj_a, onorm, dmix0, tt=512, name="a_post_bwd", dep=sent)
    dq, dz, dv, dlb = _hgrn2_bwd(proj_a, lb, st, do_raw, name="hgrn2_bwd")
    dproj_a = [dq, dz, dv, dg, dmq0]
    sent = put_g({"a_w_in": _mm_tn_pieces(dproj_a, xn0, tt=512, name="g_a_w_in")})
    gx, g_nm0 = _rms_bwd_dx(x, nmix[0], w_of("a_w_in"), dproj_a, dhm0, tt=512, wt=True, name="proj_a_bwd", dep=sent)

    dl0 = lb_soft[0:1] * lb_soft[1:2] * dlb
    gP["a_lb_logits"] = jnp.concatenate([dl0, -dl0], axis=0)
    gP["a_onorm"] = g_onorm
    gP["norm_mix"] = jnp.concatenate([g_nm0, g_nm1], axis=0)
    gP["norm_ffn"] = jnp.concatenate([g_nf0, g_nf1], axis=0)
    gP["b_qnorm"] = dqw.reshape(N_GROUPS, B_HEADS, HEAD_DIM).sum(axis=1)[None]
    gP["kv_norm"] = g_kvn.reshape(-1)
    gP["b_knorm"] = dkw.reshape(B_HEADS, HEAD_DIM).sum(axis=0)
    gP["mem_norm"] = jnp.concatenate([g_mn0, g_mn1], axis=0)
    gP["mem_qnorm"] = jnp.stack([g_mq0, g_mq1])
    gP["mem_knorm"] = jnp.stack([g_mk0, g_mk1])
    return sq, gx, gP


MESH_ID = pl.DeviceIdType.MESH
HBM_SPEC = pl.BlockSpec(memory_space=pltpu.HBM)


def _position():
    return lax.axis_index("x"), lax.axis_index("y"), lax.axis_index("c")


def _all_gather_direct(block, after, *, name):
    def body(x_ref, after_ref, out_ref, send_sems, recv_sems, local_sem):
        x, y, c = _position()
        me = 4 * x + 2 * y + c
        mine = pltpu.make_async_copy(x_ref, out_ref.at[me], local_sem)
        mine.start()
        copies = []
        for k in ALL_PEERS:
            cp = pltpu.make_async_remote_copy(
                src_ref=x_ref, dst_ref=out_ref.at[me], send_sem=send_sems.at[k - 1], recv_sem=recv_sems.at[k - 1],
                device_id=_peer(k, x, y, c), device_id_type=MESH_ID)
            cp.start()
            copies.append(cp)
        for cp in copies:
            cp.wait()
        mine.wait()

    return pl.pallas_call(
        body, out_shape=jax.ShapeDtypeStruct((N_DEV,) + block.shape, block.dtype),
        in_specs=[HBM_SPEC, pl.BlockSpec(memory_space=pl.ANY)], out_specs=HBM_SPEC,
        scratch_shapes=[pltpu.SemaphoreType.DMA((7,)), pltpu.SemaphoreType.DMA((7,)), pltpu.SemaphoreType.DMA],
        name=name)(block, after)


SEM_SPEC = pl.BlockSpec(memory_space=pltpu.SEMAPHORE)
ANY_SPEC = pl.BlockSpec(memory_space=pl.ANY)
DATAFLOW = pltpu.SideEffectType.DATAFLOW_SIDE_EFFECTING


def _peer(k, x, y, c):
    return (1 - x if (k >> 2) & 1 else x, 1 - y if (k >> 1) & 1 else y, 1 - c if k & 1 else c)


def _own_slot_filled(own_block):
    x, y, c = _position()
    zone = lax.empty((N_DEV,) + own_block.shape, own_block.dtype)
    return lax.dynamic_update_slice_in_dim(zone, own_block[None], 4 * x + 2 * y + c, axis=0)


ALL_PEERS = tuple(range(1, N_DEV))
SIBLING_AND_SAME_CORE = (1, 2, 4, 6)
SAME_CORE = (2, 4, 6)


def _split_start(srcs, scatter, after, *, name, relations=ALL_PEERS, carried=None):
    n = len(srcs)
    extra = ([] if after is None else [after]) + ([] if carried is None else [carried])
    n_carried = 0 if carried is None else 1
    x, y, c = _position()
    me = 4 * x + 2 * y + c
    lands = [_own_slot_filled(lax.dynamic_index_in_dim(s, me, 0, keepdims=False) if scatter else s) for s in srcs]

    def body(*refs):
        src_refs, land_refs = refs[:n], refs[n:2 * n]
        send_sems, recv_sems = refs[2 * n + len(extra)], refs[2 * n + len(extra) + 1]
        token = refs[2 * n + len(extra) + 2 + 2 * n]
        bx, by, bc = _position()
        bme = 4 * bx + 2 * by + bc
        for a in range(n):
            for k in relations:
                tx, ty, tc = _peer(k, bx, by, bc)
                src = src_refs[a].at[4 * tx + 2 * ty + tc] if scatter else src_refs[a]
                pltpu.make_async_remote_copy(
                    src_ref=src, dst_ref=land_refs[a].at[bme],
                    send_sem=send_sems.at[7 * a + k - 1], recv_sem=recv_sems.at[7 * a + k - 1],
                    device_id=(tx, ty, tc), device_id_type=MESH_ID).start()
        token[...] = jnp.zeros_like(token)

    hbm = lambda a: pltpu.HBM(a.shape, a.dtype)
    outs = pl.pallas_call(
        body, name=name,
        out_shape=(pltpu.SemaphoreType.DMA((7 * n,)), pltpu.SemaphoreType.DMA((7 * n,)),
                   *[hbm(s) for s in srcs], *[hbm(l) for l in lands], jax.ShapeDtypeStruct((8, 128), F32),
                   *([hbm(carried)] if n_carried else [])),
        in_specs=[HBM_SPEC] * (2 * n) + [ANY_SPEC] * len(extra),
        out_specs=(SEM_SPEC, SEM_SPEC, *[HBM_SPEC] * (2 * n), pl.BlockSpec(memory_space=pltpu.VMEM), *([ANY_SPEC] * n_carried)),
        input_output_aliases={**{i: 2 + i for i in range(2 * n)},
                              **({2 * n + len(extra) - 1: 2 * n + 3} if n_carried else {})},
        compiler_params=pltpu.CompilerParams(has_side_effects=DATAFLOW),
    )(*[pltpu.with_memory_space_constraint(s, pltpu.HBM) for s in srcs],
      *[pltpu.with_memory_space_constraint(l, pltpu.HBM) for l in lands], *extra)
    return {"n": n, "relations": relations, "send": outs[0], "recv": outs[1], "srcs": list(outs[2:2 + n]),
            "lands": list(outs[2 + n:2 + 2 * n]), "token": outs[2 * n + 2], "carried": outs[-1] if n_carried else None}


def _forward_start(lands, carried, *, name):
    n = len(lands)

    def body(*refs):
        land_refs = refs[:n]
        send_sems, recv_sems = refs[n + 1], refs[n + 2]
        bx, by, bc = _position()
        for a in range(n):
            for k in SAME_CORE:
                tx, ty, tc = _peer(k, bx, by, bc)
                block = land_refs[a].at[4 * tx + 2 * ty + tc]
                pltpu.make_async_remote_copy(
                    src_ref=block, dst_ref=block,
                    send_sem=send_sems.at[7 * a + k - 1], recv_sem=recv_sems.at[7 * a + k - 1],
                    device_id=(bx, by, 1 - bc), device_id_type=MESH_ID).start()

    hbm = lambda a: pltpu.HBM(a.shape, a.dtype)
    outs = pl.pallas_call(
        body, name=name,
        out_shape=(pltpu.SemaphoreType.DMA((7 * n,)), pltpu.SemaphoreType.DMA((7 * n,)),
                   *[hbm(l) for l in lands], hbm(carried)),
        in_specs=[HBM_SPEC] * n + [ANY_SPEC],
        out_specs=(SEM_SPEC, SEM_SPEC, *[HBM_SPEC] * n, ANY_SPEC),
        input_output_aliases={i: 2 + i for i in range(n + 1)},
        compiler_params=pltpu.CompilerParams(has_side_effects=DATAFLOW),
    )(*lands, carried)
    handle = {"n": n, "relations": SAME_CORE, "send": outs[0], "recv": outs[1], "srcs": [], "lands": list(outs[2:2 + n])}
    return handle, outs[-1]


def _split_wait(handle, after, *, name):
    n, ns = handle["n"], len(handle["srcs"])

    def body(*refs):
        land_refs = refs[ns:ns + n]
        send_sems, recv_sems = refs[ns + n], refs[ns + n + 1]
        bx, by, bc = _position()
        for a in range(n):
            for k in handle["relations"]:
                block = land_refs[a].at[0]
                cp = pltpu.make_async_remote_copy(
                    src_ref=block, dst_ref=block,
                    send_sem=send_sems.at[7 * a + k - 1], recv_sem=recv_sems.at[7 * a + k - 1],
                    device_id=_peer(k, bx, by, bc), device_id_type=MESH_ID)
                cp.wait_send()
                cp.wait_recv()

    hbm = lambda a: pltpu.HBM(a.shape, a.dtype)
    outs = pl.pallas_call(
        body, name=name,
        out_shape=(*[hbm(s) for s in handle["srcs"]], *[hbm(l) for l in handle["lands"]]),
        in_specs=[HBM_SPEC] * (ns + n) + [SEM_SPEC, SEM_SPEC, ANY_SPEC],
        out_specs=tuple([HBM_SPEC] * (ns + n)),
        input_output_aliases={i: i for i in range(ns + n)},
        compiler_params=pltpu.CompilerParams(has_side_effects=DATAFLOW),
    )(*handle["srcs"], *handle["lands"], handle["send"], handle["recv"], after)
    return list(outs[ns:])


def _sum_sources(parts, *, tr, name):
    n, R, C = parts.shape

    def body(p_ref, o_ref):
        acc = p_ref[0].astype(F32)
        for s in range(1, n):
            acc = acc + p_ref[s].astype(F32)
        o_ref[...] = acc

    return pl.pallas_call(
        body, grid=(R // tr,), in_specs=[pl.BlockSpec((n, tr, C), lambda i: (0, i, 0))],
        out_specs=pl.BlockSpec((tr, C), lambda i: (i, 0)),
        out_shape=jax.ShapeDtypeStruct((R, C), F32), compiler_params=_cp("parallel"), name=name)(parts)


def _adamw_math(g, w, m, v):
    c1 = 1.0 - ADAM_B1 ** ADAM_STEP
    c2 = 1.0 - ADAM_B2 ** ADAM_STEP
    nm = ADAM_B1 * m + (1.0 - ADAM_B1) * g
    nv = ADAM_B2 * v + (1.0 - ADAM_B2) * (g * g)
    return -ADAM_LR * ((nm / c1) / (jnp.sqrt(nv / c2) + ADAM_EPS) + ADAM_WD * w), nm, nv


def _reduce_adamw(received, w, m, v, *, tr, name):
    L, R, C = w.shape

    def body(*refs):
        p_refs = refs[:L]
        w_ref, m_ref, v_ref, g_ref, d_ref, nm_ref, nv_ref = refs[L:]
        for l in range(L):
            @pl.when(pl.program_id(0) == l)
            def _(p_ref=p_refs[l]):
                acc = p_ref[0].astype(F32)
                for s in range(1, N_DEV):
                    acc = acc + p_ref[s].astype(F32)
                g_ref[...] = acc
                d_ref[...], nm_ref[...], nv_ref[...] = _adamw_math(acc, w_ref[...], m_ref[...], v_ref[...])

    p_spec = pl.BlockSpec((N_DEV, tr, C), lambda l, i: (0, i, 0))
    blk = pl.BlockSpec((None, tr, C), lambda l, i: (l, i, 0))
    sh = jax.ShapeDtypeStruct((L, R, C), F32)
    return pl.pallas_call(
        body, grid=(L, R // tr), in_specs=[p_spec] * L + [blk] * 3, out_specs=[blk] * 4, out_shape=[sh] * 4,
        compiler_params=_cp("parallel", "parallel"), name=name)(*received, w, m, v)


def _adamw(g, w, m, v, *, tr, name):
    L, R, C = w.shape

    def body(g_ref, w_ref, m_ref, v_ref, d_ref, nm_ref, nv_ref):
        d_ref[...], nm_ref[...], nv_ref[...] = _adamw_math(g_ref[...], w_ref[...], m_ref[...], v_ref[...])

    blk = pl.BlockSpec((None, tr, C), lambda l, i: (l, i, 0))
    sh = jax.ShapeDtypeStruct((L, R, C), F32)
    return pl.pallas_call(
        body, grid=(L, R // tr), in_specs=[blk] * 4, out_specs=[blk] * 3, out_shape=[sh] * 3,
        compiler_params=_cp("parallel", "parallel"), name=name)(g, w, m, v)


UNITS = {
    "a_w_in": ("a_w_in", 0, True), "w_mem_kv0": ("w_mem_kv", 0, False), "w_out0": ("w_out", 0, False),
    "w_gate_up0": ("w_gate_up", 0, True), "w_down0": ("w_down", 0, False), "w_kv": ("w_kv", None, True),
    "b_w_in": ("b_w_in", 0, True), "w_mem_kv1": ("w_mem_kv", 1, False), "w_out1": ("w_out", 1, False),
    "w_gate_up1": ("w_gate_up", 1, True), "w_down1": ("w_down", 1, False),
}
BIG = ("a_w_in", "b_w_in", "w_kv", "w_mem_kv", "w_out", "w_gate_up", "w_down")
ADAMW_ROW_TILE = {"a_w_in": 208, "b_w_in": 160, "w_kv": 192, "w_mem_kv": 128, "w_out": 128, "w_gate_up": 352, "w_down": 352}


def _wire_block(weights, unit):
    name, layer, col = UNITS[unit]
    a = weights[name] if layer is None else weights[name][layer]
    return (a.T if col else a).astype(BF16)


SMALL_REPLICATED = ("norm_mix", "norm_ffn", "b_qnorm", "kv_norm", "b_knorm", "mem_norm", "mem_qnorm", "mem_knorm")
SMALL_SHARDED = ("a_lb_logits", "a_onorm")
SMALL_ORDER = SMALL_REPLICATED + SMALL_SHARDED
LANES = 128


def _prod(shape):
    n = 1
    for s in shape:
        n *= s
    return n


def _pack_flat(arrays, rows, cols, dtype):
    flat = jnp.concatenate([a.reshape(-1).astype(dtype) for a in arrays])
    return jnp.pad(flat, (0, rows * cols - flat.shape[0])).reshape(rows, cols)


def _unpack_flat(packed, shapes):
    flat = packed.reshape(-1)
    out, off = [], 0
    for s in shapes:
        out.append(flat[off:off + _prod(s)].reshape(s))
        off += _prod(s)
    return out


def kernel(x, mem, norm_mix, norm_ffn, a_w_in, a_lb_logits, a_onorm, b_w_in, b_qnorm, kv_norm, w_kv, b_knorm, mem_norm, w_mem_kv, mem_qnorm, mem_knorm, w_out, w_gate_up, w_down, loss_target, m_norm_mix, m_norm_ffn, m_a_w_in, m_a_lb_logits, m_a_onorm, m_b_w_in, m_b_qnorm, m_kv_norm, m_w_kv, m_b_knorm, m_mem_norm, m_w_mem_kv, m_mem_qnorm, m_mem_knorm, m_w_out, m_w_gate_up, m_w_down, v_norm_mix, v_norm_ffn, v_a_w_in, v_a_lb_logits, v_a_onorm, v_b_w_in, v_b_qnorm, v_kv_norm, v_w_kv, v_b_knorm, v_mem_norm, v_w_mem_kv, v_mem_qnorm, v_mem_knorm, v_w_out, v_w_gate_up, v_w_down):
    names = ("norm_mix", "norm_ffn", "a_w_in", "a_lb_logits", "a_onorm", "b_w_in", "b_qnorm", "kv_norm", "w_kv", "b_knorm",
             "mem_norm", "w_mem_kv", "mem_qnorm", "mem_knorm", "w_out", "w_gate_up", "w_down")
    w = dict(zip(names, (norm_mix, norm_ffn, a_w_in, a_lb_logits, a_onorm, b_w_in, b_qnorm, kv_norm, w_kv, b_knorm,
                         mem_norm, w_mem_kv, mem_qnorm, mem_knorm, w_out, w_gate_up, w_down)))
    m = dict(zip(names, (m_norm_mix, m_norm_ffn, m_a_w_in, m_a_lb_logits, m_a_onorm, m_b_w_in, m_b_qnorm, m_kv_norm, m_w_kv,
                         m_b_knorm, m_mem_norm, m_w_mem_kv, m_mem_qnorm, m_mem_knorm, m_w_out, m_w_gate_up, m_w_down)))
    v = dict(zip(names, (v_norm_mix, v_norm_ffn, v_a_w_in, v_a_lb_logits, v_a_onorm, v_b_w_in, v_b_qnorm, v_kv_norm, v_w_kv,
                         v_b_knorm, v_mem_norm, v_w_mem_kv, v_mem_qnorm, v_mem_knorm, v_w_out, v_w_gate_up, v_w_down)))

    first = ["a_w_in", "w_mem_kv0"]
    later = [["w_out0", "w_gate_up0"], ["w_down0", "w_kv"], ["b_w_in", "w_mem_kv1"], ["w_out1", "w_gate_up1", "w_down1"]]
    first_half, second_half = {}, {}

    def start_first_half(i, after, carried=None):
        first_half[i] = _split_start([_wire_block(w, u) for u in later[i]], False, after, name=f"gather{i}_start",
                                     relations=SIBLING_AND_SAME_CORE, carried=carried)
        return first_half[i]

    opening = _split_start([_wire_block(w, u) for u in first] + [_pack_flat([a_lb_logits, a_onorm], 8, LANES, F32)],
                           False, None, name="gather_first_start", relations=SIBLING_AND_SAME_CORE)
    token = start_first_half(0, opening["token"])["token"]
    token = start_first_half(1, token)["token"]
    opening, token = _forward_start(_split_wait(opening, token, name="gather_first_landed"), token, name="gather_first_forward")
    gathered = _split_wait(opening, token, name="gather_first_wait")
    full = {u: g.reshape(-1, g.shape[-1]) for u, g in zip(first, gathered)}
    small_in = gathered[-1].reshape(N_DEV, -1)
    P = {n: w[n] for n in SMALL_REPLICATED}
    P["a_lb_logits"] = small_in[:, :192].reshape(N_DEV, 2, 96).transpose(1, 0, 2).reshape(2, A_WIDTH)
    P["a_onorm"] = small_in[:, 192:288].reshape(1, A_WIDTH)

    def forward_point(i, value):
        landed = _split_wait(first_half[i], value, name=f"gather{i}_landed")
        second_half[i], value = _forward_start(landed, value, name=f"gather{i}_forward")
        if i + 2 < len(later):
            value = start_first_half(i + 2, None, carried=value)["carried"]
        return value

    def get_w(unit, after):
        if unit not in full:
            i = [unit in group for group in later].index(True)
            for u, land in zip(later[i], _split_wait(second_half[i], after, name=f"gather{i}_wait")):
                full[u] = land.reshape(-1, land.shape[-1])
        return full[unit]

    sent = []

    def put_g(group):
        units = list(group)
        handle = _split_start([group[u].reshape(N_DEV, -1, group[u].shape[-1]) for u in units], True, None,
                              name=f"scatter{len(sent)}_start")
        sent.append((units, handle))
        return handle["token"]

    sq, gx, gP = _local_step(x[0], mem[0], loss_target[0], get_w, P, put_g, forward_point=forward_point)
    loss_here = (0.5 * jnp.sum(sq) / D_MODEL).reshape(1)

    received = {}
    group_of = {u: i for i, (units, _) in enumerate(sent) for u in units}
    out = {"grad": {}, "delta": {}, "new_m": {}, "new_v": {}}
    newest = [gx]

    def update_big(n):
        shape = w[n].shape
        as3 = lambda a: a.reshape((-1,) + shape[-2:])
        mine = [u for u, (wn, _, _) in UNITS.items() if wn == n]
        for i in sorted({group_of[u] for u in mine}):
            if sent[i][0][0] not in received:
                received.update(zip(sent[i][0], _split_wait(sent[i][1], newest[0], name=f"scatter{i}_wait")))
        flip = (lambda a: jnp.swapaxes(a, 1, 2)) if UNITS[mine[0]][2] else (lambda a: a)
        res = _reduce_adamw([received[u] for u in mine], flip(as3(w[n])), flip(as3(m[n])), flip(as3(v[n])),
                            tr=ADAMW_ROW_TILE[n], name=f"adamw_{n}")
        newest[0] = res[1]
        for kind, r in zip(("grad", "delta", "new_m", "new_v"), res):
            out[kind][n] = flip(r).reshape(shape)

    for n in ("w_down", "w_gate_up", "w_out", "w_mem_kv", "b_w_in", "w_kv"):
        update_big(n)

    full_shapes = [(2, A_WIDTH) if n == "a_lb_logits" else (1, A_WIDTH) if n == "a_onorm" else w[n].shape for n in SMALL_ORDER]
    n_small = sum(_prod(s) for s in full_shapes) + 1
    rows_small = -(-n_small // (8 * LANES)) * 8
    g_all = _all_gather_direct(_pack_flat([gP[n] for n in SMALL_ORDER] + [loss_here], rows_small, LANES, F32),
                               newest[0], name="gather_small_grads")
    summed = _unpack_flat(_sum_sources(g_all, tr=rows_small, name="sum_small_grads"), full_shapes + [(1,)])
    g_small = dict(zip(SMALL_ORDER, summed))
    loss = summed[-1].reshape(())
    me = 4 * lax.axis_index("x") + 2 * lax.axis_index("y") + lax.axis_index("c")
    for n in SMALL_SHARDED:
        g_small[n] = lax.dynamic_slice_in_dim(g_small[n], me * 96, 96, axis=1)
    shapes = [w[n].shape for n in SMALL_ORDER]
    rows_upd = -(-sum(_prod(s) for s in shapes) // (8 * LANES)) * 8
    pk = lambda d: _pack_flat([d[n] for n in SMALL_ORDER], rows_upd, LANES, F32)
    res = _adamw(pk(g_small)[None], pk(w)[None], pk(m)[None], pk(v)[None], tr=rows_upd, name="adamw_small")
    out["grad"].update(g_small)
    for kind, packed in zip(("delta", "new_m", "new_v"), res):
        out[kind].update(zip(SMALL_ORDER, _unpack_flat(packed[0], shapes)))
    newest[0] = res[0]
    update_big("a_w_in")

    return (loss, gx[None], *[out["grad"][n] for n in names], *[out["delta"][n] for n in names],
            *[out["new_m"][n] for n in names], *[out["new_v"][n] for n in names])
```

```python
import functools

import jax
import jax.numpy as jnp
import numpy as np
from jax import lax
from jax.experimental import pallas as pl
from jax.experimental.pallas import tpu as pltpu

F32 = jnp.float32
BF16 = jnp.bfloat16

N_DEV = 8
D_MODEL = 1024
HEAD_DIM = 128
A_HEADS = 6
A_WIDTH = A_HEADS * HEAD_DIM
CHUNK = 64
B_HEADS = 6
B_WIDTH = B_HEADS * HEAD_DIM
DILATIONS = (1, 4, 16)
SPAN = 128
N_GROUPS = 3
ROPE_THETA = 10000.0
MEM_TOKENS = 256
MEM_HEADS = 4
MEM_HEAD_DIM = 64
MEM_WIDTH = MEM_HEADS * MEM_HEAD_DIM
FFN_HIDDEN = 2816
EPS = 1e-6

ADAM_LR = 0.001
ADAM_B1 = 0.9
ADAM_B2 = 0.999
ADAM_EPS = 1e-08
ADAM_WD = 0.01
ADAM_STEP = 10

V7X_VMEM_LIMIT_BYTES = 56 * 1024 * 1024

NT_DIMS = (((1,), (1,)), ((), ()))
TN_DIMS = (((0,), (0,)), ((), ()))


def _cp(*sem):
    return pltpu.CompilerParams(dimension_semantics=sem, vmem_limit_bytes=V7X_VMEM_LIMIT_BYTES)


def _dot(a, b):
    return jnp.dot(a.astype(BF16), b.astype(BF16), preferred_element_type=F32)


def _dot_nt(a, b):
    return lax.dot_general(a.astype(BF16), b.astype(BF16), NT_DIMS, preferred_element_type=F32)


def _dot_tn(a, b):
    return lax.dot_general(a.astype(BF16), b.astype(BF16), TN_DIMS, preferred_element_type=F32)


def _dot3(m01, x):
    hi = x.astype(BF16)
    r1 = x - hi.astype(F32)
    mid = r1.astype(BF16)
    lo = (r1 - mid.astype(F32)).astype(BF16)
    d = functools.partial(jnp.dot, preferred_element_type=F32)
    return d(m01, hi) + d(m01, mid) + d(m01, lo)


def _sigmoid(x):
    return 0.5 * jnp.tanh(0.5 * x) + 0.5


def _full(shape):
    return pl.BlockSpec(shape, lambda *_: (0,) * len(shape))


def _dep(body, n_in, dep):
    if dep is None:
        return body, [], []

    def with_dep(*refs):
        return body(*refs[:n_in], *refs[n_in + 1:])

    return with_dep, [pl.BlockSpec(memory_space=pl.ANY)], [dep]


def _rms_matmul(x, g, w, *, tt, tn, wt, name, out_dtype=F32, dep=None, rotate=None):
    T, K = x.shape
    N = w.shape[0] if wt else w.shape[1]
    n_rot = 0 if rotate is None else rotate[0].shape[1] // HEAD_DIM
    extra_in = [] if rotate is None else list(rotate)

    def kernel_body(x_ref, g_ref, w_ref, *rest):
        y_ref, xn_ref = rest[len(extra_in)], rest[len(extra_in) + 1]
        xf = x_ref[...]
        r = lax.rsqrt(jnp.mean(xf * xf, axis=-1, keepdims=True) + EPS)
        xn = (xf * r * g_ref[...]).astype(BF16)
        xn_ref[...] = xn
        for j in range(N // tn):
            cols = slice(j * tn, (j + 1) * tn)
            y = _dot_nt(xn, w_ref[cols, :]) if wt else _dot(xn, w_ref[:, cols])
            y_ref[:, cols] = y.astype(out_dtype)
            for h in range(j * tn // HEAD_DIM, min((j + 1) * tn // HEAD_DIM, n_rot)):
                gw_ref, c_ref, s_ref, yr_ref = rest[0], rest[1], rest[2], rest[len(extra_in) + 2]
                sl = slice(h * HEAD_DIM, (h + 1) * HEAD_DIM)
                xhat, _ = _head_rms(y[:, h * HEAD_DIM - j * tn:(h + 1) * HEAD_DIM - j * tn])
                yr_ref[:, sl] = _rope(xhat * gw_ref[:, sl], c_ref[...], s_ref[...])

    tbl = pl.BlockSpec((tt, HEAD_DIM), lambda i: (i, 0))
    rot_specs = [] if rotate is None else [_full((1, n_rot * HEAD_DIM)), tbl, tbl]
    body, dep_specs, dep_args = _dep(kernel_body, 3 + len(extra_in), dep)
    return pl.pallas_call(
        body, grid=(T // tt,),
        in_specs=[pl.BlockSpec((tt, K), lambda i: (i, 0)), _full((1, K)), _full(w.shape)] + rot_specs + dep_specs,
        out_specs=[pl.BlockSpec((tt, N), lambda i: (i, 0)), pl.BlockSpec((tt, K), lambda i: (i, 0))]
        + ([] if rotate is None else [pl.BlockSpec((tt, n_rot * HEAD_DIM), lambda i: (i, 0))]),
        out_shape=[jax.ShapeDtypeStruct((T, N), out_dtype), jax.ShapeDtypeStruct((T, K), BF16)]
        + ([] if rotate is None else [jax.ShapeDtypeStruct((T, n_rot * HEAD_DIM), F32)]),
        compiler_params=_cp("parallel"), name=name)(x, g, w, *extra_in, *dep_args)


def _mm_res(res, a1, a2, w, *, tt, name):
    T, K1 = a1.shape
    K2 = a2.shape[1]
    N = w.shape[1]

    def body(r_ref, a1_ref, a2_ref, w_ref, o_ref):
        o_ref[...] = r_ref[...] + _dot(a1_ref[...], w_ref[:K1, :]) + _dot(a2_ref[...], w_ref[K1:, :])

    return pl.pallas_call(
        body, grid=(T // tt,),
        in_specs=[pl.BlockSpec((tt, N), lambda i: (i, 0)), pl.BlockSpec((tt, K1), lambda i: (i, 0)),
                  pl.BlockSpec((tt, K2), lambda i: (i, 0)), _full((K1 + K2, N))],
        out_specs=pl.BlockSpec((tt, N), lambda i: (i, 0)),
        out_shape=jax.ShapeDtypeStruct((T, N), F32),
        compiler_params=_cp("parallel"), name=name)(res, a1, a2, w)


def _swiglu_down(h, gu, wd, *, tt, name):
    T, D = h.shape
    Fh = wd.shape[0]

    def body(h_ref, gt_ref, up_ref, w_ref, o_ref):
        gt = gt_ref[...].astype(F32)
        act = gt * _sigmoid(gt) * up_ref[...].astype(F32)
        o_ref[...] = h_ref[...] + _dot(act, w_ref[...])

    return pl.pallas_call(
        body, grid=(T // tt,),
        in_specs=[pl.BlockSpec((tt, D), lambda i: (i, 0)), pl.BlockSpec((tt, Fh), lambda i: (i, 0)),
                  pl.BlockSpec((tt, Fh), lambda i: (i, 1)), _full((Fh, D))],
        out_specs=pl.BlockSpec((tt, D), lambda i: (i, 0)),
        out_shape=jax.ShapeDtypeStruct((T, D), F32),
        compiler_params=_cp("parallel"), name=name)(h, gu, gu, wd)


def _swiglu_down_loss(h, gu, wd, tgt, *, tt, name):
    T, D = h.shape
    Fh = wd.shape[0]

    def body(h_ref, gt_ref, up_ref, w_ref, t_ref, dy_ref, acc_ref):
        @pl.when(pl.program_id(0) == 0)
        def _():
            acc_ref[...] = jnp.zeros_like(acc_ref)

        gt = gt_ref[...].astype(F32)
        act = gt * _sigmoid(gt) * up_ref[...].astype(F32)
        e = h_ref[...] + _dot(act, w_ref[...]) - t_ref[...]
        dy_ref[...] = e * (1.0 / D)
        acc_ref[...] += jnp.sum(e * e, axis=0, keepdims=True)

    row = pl.BlockSpec((tt, D), lambda i: (i, 0))
    return pl.pallas_call(
        body, grid=(T // tt,),
        in_specs=[row, pl.BlockSpec((tt, Fh), lambda i: (i, 0)), pl.BlockSpec((tt, Fh), lambda i: (i, 1)), _full((Fh, D)), row],
        out_specs=[row, _full((1, D))],
        out_shape=[jax.ShapeDtypeStruct((T, D), F32), jax.ShapeDtypeStruct((1, D), F32)],
        compiler_params=_cp("arbitrary"), name=name)(h, gu, gu, wd, tgt)


SWIGLU_COLS = 256


def _swiglu_bwd(dh, gu, wd, *, tt, name):
    T, D = dh.shape
    Fh = wd.shape[0]
    last = T // tt - 1

    def body(dh_ref, gt_ref, up_ref, w_ref, dgu_ref, gw_ref, acc):
        @pl.when(pl.program_id(0) == 0)
        def _():
            acc[...] = jnp.zeros_like(acc)

        dh16 = dh_ref[...].astype(BF16)
        for c0 in range(0, Fh, SWIGLU_COLS):
            cols = slice(c0, c0 + SWIGLU_COLS)
            gt = gt_ref[:, cols].astype(F32)
            up = up_ref[:, cols].astype(F32)
            s = _sigmoid(gt)
            silu = gt * s
            dact = _dot_nt(dh16, w_ref[cols, :])
            acc[cols, :] += _dot_tn((silu * up).astype(BF16), dh16)
            dgu_ref[:, cols] = (dact * up * (s * (1.0 + gt * (1.0 - s)))).astype(BF16)
            dgu_ref[:, Fh + c0:Fh + c0 + SWIGLU_COLS] = (dact * silu).astype(BF16)

        @pl.when(pl.program_id(0) == last)
        def _():
            gw_ref[...] = acc[...].astype(BF16)

    return pl.pallas_call(
        body, grid=(T // tt,),
        in_specs=[pl.BlockSpec((tt, D), lambda i: (i, 0)), pl.BlockSpec((tt, Fh), lambda i: (i, 0)),
                  pl.BlockSpec((tt, Fh), lambda i: (i, 1)), _full((Fh, D))],
        out_specs=[pl.BlockSpec((tt, 2 * Fh), lambda i: (i, 0)), _full((Fh, D))],
        out_shape=[jax.ShapeDtypeStruct((T, 2 * Fh), BF16), jax.ShapeDtypeStruct((Fh, D), BF16)],
        scratch_shapes=[pltpu.VMEM((Fh, D), F32)],
        compiler_params=_cp("arbitrary"), name=name)(dh, gu, gu, wd)


def _out_proj_bwd(dy, a1, a2, w, *, tt, name, head_dots=False):
    T, N = dy.shape
    K1, K2 = a1.shape[1], a2.shape[1]
    K = K1 + K2
    last = T // tt - 1

    def body(dy_ref, a1_ref, a2_ref, w_ref, da_ref, gw_ref, *rest):
        acc = rest[-1]

        @pl.when(pl.program_id(0) == 0)
        def _():
            acc[...] = jnp.zeros_like(acc)

        dy16 = dy_ref[...].astype(BF16)
        da = _dot_nt(dy16, w_ref[...])
        da_ref[...] = da
        acc[:K1, :] += _dot_tn(a1_ref[...], dy16)
        acc[K1:, :] += _dot_tn(a2_ref[...], dy16)
        if head_dots:
            for h in range(K1 // HEAD_DIM):
                sl = slice(h * HEAD_DIM, (h + 1) * HEAD_DIM)
                rest[0][:, sl] = jnp.broadcast_to(jnp.sum(da[:, sl] * a1_ref[:, sl], axis=-1, keepdims=True), (tt, HEAD_DIM))

        @pl.when(pl.program_id(0) == last)
        def _():
            gw_ref[...] = acc[...].astype(BF16)

    extra_specs = [pl.BlockSpec((tt, K1), lambda i: (i, 0))] if head_dots else []
    extra_shapes = [jax.ShapeDtypeStruct((T, K1), F32)] if head_dots else []
    return pl.pallas_call(
        body, grid=(T // tt,),
        in_specs=[pl.BlockSpec((tt, N), lambda i: (i, 0)), pl.BlockSpec((tt, K1), lambda i: (i, 0)),
                  pl.BlockSpec((tt, K2), lambda i: (i, 0)), _full((K, N))],
        out_specs=[pl.BlockSpec((tt, K), lambda i: (i, 0)), _full((K, N))] + extra_specs,
        out_shape=[jax.ShapeDtypeStruct((T, K), F32), jax.ShapeDtypeStruct((K, N), BF16)] + extra_shapes,
        scratch_shapes=[pltpu.VMEM((K, N), F32)],
        compiler_params=_cp("arbitrary"), name=name)(dy, a1, a2, w)


def _mm_tn(a, b, *, tt, tka, name):
    T, Ka = a.shape
    N = b.shape[1]
    last = T // tt - 1

    def body(a_ref, b_ref, o_ref, acc):
        @pl.when(pl.program_id(1) == 0)
        def _():
            acc[...] = jnp.zeros_like(acc)

        acc[...] += _dot_tn(a_ref[...], b_ref[...])

        @pl.when(pl.program_id(1) == last)
        def _():
            o_ref[...] = acc[...].astype(BF16)

    return pl.pallas_call(
        body, grid=(Ka // tka, T // tt),
        in_specs=[pl.BlockSpec((tt, tka), lambda j, t: (t, j)), pl.BlockSpec((tt, N), lambda j, t: (t, 0))],
        out_specs=pl.BlockSpec((tka, N), lambda j, t: (j, 0)),
        out_shape=jax.ShapeDtypeStruct((Ka, N), BF16),
        scratch_shapes=[pltpu.VMEM((tka, N), F32)],
        compiler_params=_cp("parallel", "arbitrary"), name=name)(a, b)


def _mm_tn_pieces(pieces, b, *, tt, name):
    n = len(pieces)
    T = b.shape[0]
    N = b.shape[1]
    widths = [p.shape[1] for p in pieces]
    Ka = sum(widths)
    last = T // tt - 1

    def body(*refs):
        p_refs = refs[:n]
        b_ref, o_ref, acc = refs[n:]

        @pl.when(pl.program_id(0) == 0)
        def _():
            acc[...] = jnp.zeros_like(acc)

        bv = b_ref[...].astype(BF16)
        off = 0
        for p_ref, wd in zip(p_refs, widths):
            acc[off:off + wd, :] += _dot_tn(p_ref[...], bv)
            off += wd

        @pl.when(pl.program_id(0) == last)
        def _():
            o_ref[...] = acc[...].astype(BF16)

    return pl.pallas_call(
        body, grid=(T // tt,),
        in_specs=[pl.BlockSpec((tt, wd), lambda t: (t, 0)) for wd in widths] + [pl.BlockSpec((tt, N), lambda t: (t, 0))],
        out_specs=_full((Ka, N)), out_shape=jax.ShapeDtypeStruct((Ka, N), BF16),
        scratch_shapes=[pltpu.VMEM((Ka, N), F32)],
        compiler_params=_cp("arbitrary"), name=name)(*pieces, b)


def _rms_bwd_dx(x, g, w, dy, dres, *, tt, wt, name, dep=None):
    pieces = list(dy) if isinstance(dy, (list, tuple)) else [dy]
    n = len(pieces)
    widths = [p.shape[1] for p in pieces]
    T, K = x.shape

    def kernel_body(x_ref, g_ref, w_ref, *rest):
        dy_refs = rest[:n]
        dres_ref, dx_ref, dg_ref = rest[n:]

        @pl.when(pl.program_id(0) == 0)
        def _():
            dg_ref[...] = jnp.zeros_like(dg_ref)

        if n == 1:
            dxn = (_dot if wt else _dot_nt)(dy_refs[0][...], w_ref[...])
        else:
            dxn, off = 0.0, 0
            for dy_ref, wd in zip(dy_refs, widths):
                dxn = dxn + _dot(dy_ref[...], w_ref[off:off + wd, :])
                off += wd
        xf = x_ref[...]
        r = lax.rsqrt(jnp.mean(xf * xf, axis=-1, keepdims=True) + EPS)
        xhat = xf * r
        dg_ref[...] += jnp.sum(dxn * xhat, axis=0, keepdims=True)
        dxhat = dxn * g_ref[...]
        dx_ref[...] = dres_ref[...] + r * (dxhat - xhat * jnp.mean(dxhat * xhat, axis=-1, keepdims=True))

    assert n == 1 or wt
    body, dep_specs, dep_args = _dep(kernel_body, 4 + n, dep)
    return pl.pallas_call(
        body, grid=(T // tt,),
        in_specs=[pl.BlockSpec((tt, K), lambda i: (i, 0)), _full((1, K)), _full(w.shape)]
        + [pl.BlockSpec((tt, wd), lambda i: (i, 0)) for wd in widths]
        + [pl.BlockSpec((tt, K), lambda i: (i, 0))] + dep_specs,
        out_specs=[pl.BlockSpec((tt, K), lambda i: (i, 0)), _full((1, K))],
        out_shape=[jax.ShapeDtypeStruct((T, K), F32), jax.ShapeDtypeStruct((1, K), F32)],
        compiler_params=_cp("arbitrary"), name=name)(x, g, w, *pieces, dres, *dep_args)


HGRN_TB = 512
HGRN_NCH = HGRN_TB // CHUNK
HGRN_UNROLL = 4
HGRN_HPB = 6


def _hgrn_chunk_fwd(q, z, lbv, tril01):
    sig = _sigmoid(z)
    f = lbv + (1.0 - lbv) * sig
    kk = 1.0 - f
    b = _dot3(tril01, jnp.log(f))
    bend = b[CHUNK - 1:CHUNK, :]
    sq = _sigmoid(q)
    eb = jnp.exp(b)
    emb = jnp.exp(-b)
    eo = jnp.exp(bend - b)
    dec = jnp.exp(bend)
    return sig, f, kk, sq, eb, emb, eo, dec


def _hgrn2_fwd(proj, lb, *, name):
    T = proj.shape[0]
    nT = T // HGRN_TB
    nC = T // CHUNK

    def body(q_ref, z_ref, v_ref, lb_ref, o_ref, st_ref, state):
        @pl.when(pl.program_id(1) == 0)
        def _():
            state[...] = jnp.zeros_like(state)

        row = lax.broadcasted_iota(jnp.int32, (CHUNK, CHUNK), 0)
        col = lax.broadcasted_iota(jnp.int32, (CHUNK, CHUNK), 1)
        causal = row >= col
        tril01 = causal.astype(BF16)

        def chunk(c, carry):
            rows = pl.ds(pl.multiple_of(c * CHUNK, CHUNK), CHUNK)
            for hh in range(HGRN_HPB):
                sl = slice(hh * HEAD_DIM, (hh + 1) * HEAD_DIM)
                q = q_ref[rows, sl]
                v = v_ref[rows, sl].astype(BF16)
                sig, f, kk, sq, eb, emb, eo, dec = _hgrn_chunk_fwd(q, z_ref[rows, sl], lb_ref[:, sl], tril01)
                qi = (q * sq * eb).astype(BF16)
                ki = (kk * emb).astype(BF16)
                ko = (kk * eo).astype(BF16)
                st = state[hh]
                att = jnp.where(causal, _dot_nt(qi, ki), 0.0)
                o_ref[rows, sl] = _dot(att, v) + _dot_nt(qi, st)
                st_ref[c, hh] = st
                state[hh] = st * dec + _dot_tn(v, ko)
            return carry

        lax.fori_loop(0, HGRN_NCH, chunk, 0, unroll=HGRN_UNROLL)

    W = HGRN_HPB * HEAD_DIM
    nG = A_HEADS // HGRN_HPB
    hb = lambda off: pl.BlockSpec((HGRN_TB, W), lambda h, i: (i, off + h))
    return pl.pallas_call(
        body, grid=(nG, nT),
        in_specs=[hb(0), hb(nG), hb(2 * nG), pl.BlockSpec((1, W), lambda h, i: (0, h))],
        out_specs=[hb(0), pl.BlockSpec((HGRN_NCH, HGRN_HPB, HEAD_DIM, HEAD_DIM), lambda h, i: (i, h, 0, 0))],
        out_shape=[jax.ShapeDtypeStruct((T, A_WIDTH), F32), jax.ShapeDtypeStruct((nC, A_HEADS, HEAD_DIM, HEAD_DIM), F32)],
        scratch_shapes=[pltpu.VMEM((HGRN_HPB, HEAD_DIM, HEAD_DIM), F32)],
        compiler_params=_cp("parallel", "arbitrary"), name=name)(proj, proj, proj, lb)


def _hgrn2_bwd(proj, lb, st_all, do, *, name):
    T = proj.shape[0]
    nT = T // HGRN_TB

    def body(q_ref, z_ref, v_ref, lb_ref, st_ref, do_ref, dq_ref, dz_ref, dv_ref, dlb_ref, dstate):
        @pl.when(pl.program_id(1) == 0)
        def _():
            dstate[...] = jnp.zeros_like(dstate)
            dlb_ref[...] = jnp.zeros_like(dlb_ref)

        row = lax.broadcasted_iota(jnp.int32, (CHUNK, CHUNK), 0)
        col = lax.broadcasted_iota(jnp.int32, (CHUNK, CHUNK), 1)
        causal = row >= col
        tril01 = causal.astype(BF16)
        triu01 = (row <= col).astype(BF16)

        def chunk(cc, carry):
            c = HGRN_NCH - 1 - cc
            rows = pl.ds(pl.multiple_of(c * CHUNK, CHUNK), CHUNK)
            for hh in range(HGRN_HPB):
                sl = slice(hh * HEAD_DIM, (hh + 1) * HEAD_DIM)
                lbv = lb_ref[:, sl]
                q = q_ref[rows, sl]
                v = v_ref[rows, sl].astype(BF16)
                sig, f, kk, sq, eb, emb, eo, dec = _hgrn_chunk_fwd(q, z_ref[rows, sl], lbv, tril01)
                qi32 = q * sq * eb
                ki32 = kk * emb
                ko32 = kk * eo
                qi, ki, ko = qi32.astype(BF16), ki32.astype(BF16), ko32.astype(BF16)
                att = jnp.where(causal, _dot_nt(qi, ki), 0.0).astype(BF16)
                dout = do_ref[rows, sl].astype(BF16)
                st = st_ref[c, hh]
                dst = dstate[hh]
                dst16 = dst.astype(BF16)
                datt = jnp.where(causal, _dot_nt(dout, v), 0.0).astype(BF16)
                dqi = _dot(datt, ki) + _dot(dout, st)
                dki = _dot_tn(datt, qi)
                dv_ref[rows, sl] = (_dot_tn(att, dout) + _dot_nt(ko, dst16)).astype(BF16)
                dko = _dot(v, dst16)
                ddec = jnp.sum(dst * st, axis=0, keepdims=True)
                dstate[hh] = dst * dec + _dot_tn(dout, qi)
                dkk = dki * emb + dko * eo
                db = dqi * qi32 - dki * ki32 - dko * ko32
                dbend = jnp.sum(dko * ko32, axis=0, keepdims=True) + ddec * dec
                dlogf = _dot3(triu01, db) + dbend
                df = dlogf / f - dkk
                dz_ref[rows, sl] = (df * (1.0 - lbv) * sig * (1.0 - sig)).astype(BF16)
                dlb_ref[:, sl] += jnp.sum(df * (1.0 - sig), axis=0, keepdims=True)
                dq_ref[rows, sl] = (dqi * eb * (sq * (1.0 + q * (1.0 - sq)))).astype(BF16)
            return carry

        lax.fori_loop(0, HGRN_NCH, chunk, 0, unroll=HGRN_UNROLL)

    W = HGRN_HPB * HEAD_DIM
    nG = A_HEADS // HGRN_HPB
    hb = lambda off: pl.BlockSpec((HGRN_TB, W), lambda h, i: (nT - 1 - i, off + h))
    hlb = pl.BlockSpec((1, W), lambda h, i: (0, h))
    o16 = jax.ShapeDtypeStruct((T, A_WIDTH), BF16)
    return pl.pallas_call(
        body, grid=(nG, nT),
        in_specs=[hb(0), hb(nG), hb(2 * nG), hlb,
                  pl.BlockSpec((HGRN_NCH, HGRN_HPB, HEAD_DIM, HEAD_DIM), lambda h, i: (nT - 1 - i, h, 0, 0)), hb(0)],
        out_specs=[hb(0), hb(0), hb(0), hlb],
        out_shape=[o16, o16, o16, jax.ShapeDtypeStruct((1, A_WIDTH), F32)],
        scratch_shapes=[pltpu.VMEM((HGRN_HPB, HEAD_DIM, HEAD_DIM), F32)],
        compiler_params=_cp("parallel", "arbitrary"), name=name)(proj, proj, proj, lb, st_all, do)


def _head_rms(x):
    r = lax.rsqrt(jnp.mean(x * x, axis=-1, keepdims=True) + EPS)
    return x * r, r


def _head_rms_bwd(dxhat, xhat, r):
    return r * (dxhat - xhat * jnp.mean(dxhat * xhat, axis=-1, keepdims=True))


def _a_post_fwd(o, proj, onorm, *, tt, name):
    T = o.shape[0]

    def body(o_ref, g_ref, w_ref, y_ref):
        for h in range(A_HEADS):
            sl = slice(h * HEAD_DIM, (h + 1) * HEAD_DIM)
            xhat, _ = _head_rms(o_ref[:, sl])
            g = g_ref[:, sl]
            y_ref[:, sl] = xhat * w_ref[:, sl] * (g * _sigmoid(g))

    blk = lambda c: pl.BlockSpec((tt, A_WIDTH), lambda i: (i, c))
    return pl.pallas_call(
        body, grid=(T // tt,), in_specs=[blk(0), blk(3), _full((1, A_WIDTH))], out_specs=blk(0),
        out_shape=jax.ShapeDtypeStruct((T, A_WIDTH), F32),
        compiler_params=_cp("parallel"), name=name)(o, proj, onorm)


def _a_post_bwd(o, proj, onorm, dmix, *, tt, name, dep=None):
    T = o.shape[0]

    def kernel_body(o_ref, g_ref, w_ref, dy_ref, do_ref, dg_ref, dw_ref):
        @pl.when(pl.program_id(0) == 0)
        def _():
            dw_ref[...] = jnp.zeros_like(dw_ref)

        for h in range(A_HEADS):
            sl = slice(h * HEAD_DIM, (h + 1) * HEAD_DIM)
            xhat, r = _head_rms(o_ref[:, sl])
            g = g_ref[:, sl]
            s = _sigmoid(g)
            dy = dy_ref[:, sl]
            w = w_ref[:, sl]
            dg_ref[:, sl] = (dy * xhat * w * (s * (1.0 + g * (1.0 - s)))).astype(BF16)
            dyn = dy * (g * s)
            dw_ref[:, sl] += jnp.sum(dyn * xhat, axis=0, keepdims=True)
            do_ref[:, sl] = _head_rms_bwd(dyn * w, xhat, r)

    blk = lambda c: pl.BlockSpec((tt, A_WIDTH), lambda i: (i, c))
    body, dep_specs, dep_args = _dep(kernel_body, 4, dep)
    return pl.pallas_call(
        body, grid=(T // tt,), in_specs=[blk(0), blk(3), _full((1, A_WIDTH)), blk(0)] + dep_specs,
        out_specs=[blk(0), blk(0), _full((1, A_WIDTH))],
        out_shape=[jax.ShapeDtypeStruct((T, A_WIDTH), F32), jax.ShapeDtypeStruct((T, A_WIDTH), BF16),
                   jax.ShapeDtypeStruct((1, A_WIDTH), F32)],
        compiler_params=_cp("arbitrary"), name=name)(o, proj, onorm, dmix, *dep_args)


def _mem_head_masks(n):
    lane = lax.broadcasted_iota(jnp.int32, (n, MEM_WIDTH), 1)
    return [(lane >= m * MEM_HEAD_DIM) & (lane < (m + 1) * MEM_HEAD_DIM) for m in range(MEM_HEADS)]


def _mem_head_rms(x, masks):
    x2 = x * x
    r = jnp.zeros_like(x)
    for mk in masks:
        ms = jnp.sum(jnp.where(mk, x2, 0.0), axis=-1, keepdims=True) * (1.0 / MEM_HEAD_DIM)
        r = jnp.where(mk, lax.rsqrt(ms + EPS), r)
    return x * r, r


def _mem_head_rms_bwd(dxhat, xhat, r, masks):
    t = dxhat * xhat
    m = jnp.zeros_like(t)
    for mk in masks:
        m = jnp.where(mk, jnp.sum(jnp.where(mk, t, 0.0), axis=-1, keepdims=True) * (1.0 / MEM_HEAD_DIM), m)
    return r * (dxhat - xhat * m)


MEM_SCALE = MEM_HEAD_DIM ** -0.5


def _mem_attn_fwd(proj, qcol, mkv, qn_w, kn_w, *, tt, name):
    T = proj.shape[0]

    def body(q_ref, k_ref, v_ref, qw_ref, kw_ref, o_ref):
        qmasks = _mem_head_masks(tt)
        kmasks = _mem_head_masks(MEM_TOKENS)
        qhat, _ = _mem_head_rms(q_ref[...], qmasks)
        qn = qhat * qw_ref[...]
        khat, _ = _mem_head_rms(k_ref[...], kmasks)
        kn = (khat * kw_ref[...]).astype(BF16)
        v = v_ref[...].astype(BF16)
        out = jnp.zeros((tt, MEM_WIDTH), F32)
        for m in range(MEM_HEADS):
            s = _dot_nt(jnp.where(qmasks[m], qn, 0.0), kn) * MEM_SCALE
            s = s - jnp.max(s, axis=-1, keepdims=True)
            p = jnp.exp(s)
            p = p / jnp.sum(p, axis=-1, keepdims=True)
            out = jnp.where(qmasks[m], _dot(p, v), out)
        o_ref[...] = out

    return pl.pallas_call(
        body, grid=(T // tt,),
        in_specs=[pl.BlockSpec((tt, MEM_WIDTH), lambda i: (i, qcol)), pl.BlockSpec((MEM_TOKENS, MEM_WIDTH), lambda i: (0, 0)),
                  pl.BlockSpec((MEM_TOKENS, MEM_WIDTH), lambda i: (0, 1)), _full((1, MEM_WIDTH)), _full((1, MEM_WIDTH))],
        out_specs=pl.BlockSpec((tt, MEM_WIDTH), lambda i: (i, 0)),
        out_shape=jax.ShapeDtypeStruct((T, MEM_WIDTH), F32),
        compiler_params=_cp("parallel"), name=name)(proj, mkv, mkv, qn_w, kn_w)


def _mem_attn_bwd(proj, qcol, mkv, qn_w, kn_w, dmix, *, tt, name):
    T = proj.shape[0]
    nsteps = T // tt
    ocol = (dmix.shape[1] - MEM_WIDTH) // MEM_WIDTH

    def body(q_ref, k_ref, v_ref, qw_ref, kw_ref, do_ref, dq_ref, dkv_ref, dqw_ref, dkw_ref, dk_acc, dv_acc):
        step = pl.program_id(0)

        @pl.when(step == 0)
        def _():
            dk_acc[...] = jnp.zeros_like(dk_acc)
            dv_acc[...] = jnp.zeros_like(dv_acc)
            dqw_ref[...] = jnp.zeros_like(dqw_ref)

        qmasks = _mem_head_masks(tt)
        kmasks = _mem_head_masks(MEM_TOKENS)
        qhat, qr = _mem_head_rms(q_ref[...], qmasks)
        qn = qhat * qw_ref[...]
        khat, kr = _mem_head_rms(k_ref[...], kmasks)
        kn = (khat * kw_ref[...]).astype(BF16)
        v = v_ref[...].astype(BF16)
        dout = do_ref[...]
        dqn = jnp.zeros((tt, MEM_WIDTH), F32)
        dkn = jnp.zeros((MEM_TOKENS, MEM_WIDTH), F32)
        dvv = jnp.zeros((MEM_TOKENS, MEM_WIDTH), F32)
        for m in range(MEM_HEADS):
            qm = jnp.where(qmasks[m], qn, 0.0).astype(BF16)
            s = _dot_nt(qm, kn) * MEM_SCALE
            s = s - jnp.max(s, axis=-1, keepdims=True)
            p = jnp.exp(s)
            p = p / jnp.sum(p, axis=-1, keepdims=True)
            dom = jnp.where(qmasks[m], dout, 0.0).astype(BF16)
            dp = _dot_nt(dom, v)
            ds = (p * (dp - jnp.sum(p * dp, axis=-1, keepdims=True)) * MEM_SCALE).astype(BF16)
            dqn = jnp.where(qmasks[m], _dot(ds, kn), dqn)
            dkn = jnp.where(kmasks[m], _dot_tn(ds, qm), dkn)
            dvv = jnp.where(kmasks[m], _dot_tn(p, dom), dvv)
        dqw_ref[...] += jnp.sum(dqn * qhat, axis=0, keepdims=True)
        dq_ref[...] = _mem_head_rms_bwd(dqn * qw_ref[...], qhat, qr, qmasks).astype(BF16)
        dk_acc[...] += dkn
        dv_acc[...] += dvv

        @pl.when(step == nsteps - 1)
        def _():
            dk = dk_acc[...]
            dkw_ref[...] = jnp.sum(dk * khat, axis=0, keepdims=True)
            dkv_ref[:, :MEM_WIDTH] = _mem_head_rms_bwd(dk * kw_ref[...], khat, kr, kmasks)
            dkv_ref[:, MEM_WIDTH:] = dv_acc[...]

    return pl.pallas_call(
        body, grid=(nsteps,),
        in_specs=[pl.BlockSpec((tt, MEM_WIDTH), lambda i: (i, qcol)), pl.BlockSpec((MEM_TOKENS, MEM_WIDTH), lambda i: (0, 0)),
                  pl.BlockSpec((MEM_TOKENS, MEM_WIDTH), lambda i: (0, 1)), _full((1, MEM_WIDTH)), _full((1, MEM_WIDTH)),
                  pl.BlockSpec((tt, MEM_WIDTH), lambda i: (i, ocol))],
        out_specs=[pl.BlockSpec((tt, MEM_WIDTH), lambda i: (i, 0)), _full((MEM_TOKENS, 2 * MEM_WIDTH)),
                   _full((1, MEM_WIDTH)), _full((1, MEM_WIDTH))],
        out_shape=[jax.ShapeDtypeStruct((T, MEM_WIDTH), BF16), jax.ShapeDtypeStruct((MEM_TOKENS, 2 * MEM_WIDTH), F32),
                   jax.ShapeDtypeStruct((1, MEM_WIDTH), F32), jax.ShapeDtypeStruct((1, MEM_WIDTH), F32)],
        scratch_shapes=[pltpu.VMEM((MEM_TOKENS, MEM_WIDTH), F32), pltpu.VMEM((MEM_TOKENS, MEM_WIDTH), F32)],
        compiler_params=_cp("arbitrary"), name=name)(proj, mkv, mkv, qn_w, kn_w, dmix)


HALF = HEAD_DIM // 2
ATT_SCALE = HEAD_DIM ** -0.5
NEG = -1e30


def _rope_tables(T):
    inv = np.float32(ROPE_THETA) ** (-np.arange(HALF, dtype=np.float32) / np.float32(HALF))
    ang = np.arange(T, dtype=np.float32)[:, None] * inv[None, :].astype(np.float32)
    cos, sin = np.cos(ang).astype(np.float32), np.sin(ang).astype(np.float32)
    return jnp.asarray(np.concatenate([cos, cos], axis=-1)), jnp.asarray(np.concatenate([-sin, sin], axis=-1))


def _rope(x, cosf, sinsg):
    return x * cosf + pltpu.roll(x, HALF, 1) * sinsg


def _rope_bwd(dy, cosf, sinsg):
    return dy * cosf + pltpu.roll(dy * sinsg, HALF, 1)


def _q_prep_bwd(proj, w_heads, cosf, sinsg, dqs, *, tt, name):
    T = proj.shape[0]
    W = N_GROUPS * B_WIDTH

    def body(x_ref, w_ref, c_ref, s_ref, d0, d1, d2, dx_ref, dw_ref):
        @pl.when(pl.program_id(0) == 0)
        def _():
            dw_ref[...] = jnp.zeros_like(dw_ref)

        c, s = c_ref[...], s_ref[...]
        for gi, d_ref in enumerate((d0, d1, d2)):
            for h in range(B_HEADS):
                sl = slice((gi * B_HEADS + h) * HEAD_DIM, (gi * B_HEADS + h + 1) * HEAD_DIM)
                xhat, r = _head_rms(x_ref[:, sl])
                dyn = _rope_bwd(d_ref[:, h * HEAD_DIM:(h + 1) * HEAD_DIM], c, s)
                dw_ref[:, sl] += jnp.sum(dyn * xhat, axis=0, keepdims=True)
                dx_ref[:, sl] = _head_rms_bwd(dyn * w_ref[:, sl], xhat, r).astype(BF16)

    tbl = pl.BlockSpec((tt, HEAD_DIM), lambda i: (i, 0))
    dyb = pl.BlockSpec((tt, B_WIDTH), lambda i: (i, 0))
    return pl.pallas_call(
        body, grid=(T // tt,),
        in_specs=[pl.BlockSpec((tt, W), lambda i: (i, 0)), _full((1, W)), tbl, tbl, dyb, dyb, dyb],
        out_specs=[pl.BlockSpec((tt, W), lambda i: (i, 0)), _full((1, W))],
        out_shape=[jax.ShapeDtypeStruct((T, W), BF16), jax.ShapeDtypeStruct((1, W), F32)],
        compiler_params=_cp("arbitrary"), name=name)(proj, w_heads, cosf, sinsg, *dqs)


def _kv_prep_bwd(kv, w_heads, cosf, sinsg, dks, dvs, *, tt, name):
    T = kv.shape[0]

    def body(x_ref, w_ref, c_ref, s_ref, k0, k1, k2, v0, v1, v2, dx_ref, dw_ref):
        @pl.when(pl.program_id(0) == 0)
        def _():
            dw_ref[...] = jnp.zeros_like(dw_ref)

        c, s = c_ref[...], s_ref[...]
        for h in range(B_HEADS):
            sl = slice(h * HEAD_DIM, (h + 1) * HEAD_DIM)
            vs = slice(B_WIDTH + h * HEAD_DIM, B_WIDTH + (h + 1) * HEAD_DIM)
            xhat, r = _head_rms(x_ref[:, sl])
            dyn = _rope_bwd(k0[:, sl] + k1[:, sl] + k2[:, sl], c, s)
            dw_ref[:, sl] += jnp.sum(dyn * xhat, axis=0, keepdims=True)
            dx_ref[:, sl] = _head_rms_bwd(dyn * w_ref[:, sl], xhat, r).astype(BF16)
            dx_ref[:, vs] = (v0[:, sl] + v1[:, sl] + v2[:, sl]).astype(BF16)

    tbl = pl.BlockSpec((tt, HEAD_DIM), lambda i: (i, 0))
    dyb = pl.BlockSpec((tt, B_WIDTH), lambda i: (i, 0))
    return pl.pallas_call(
        body, grid=(T // tt,),
        in_specs=[dyb, _full((1, B_WIDTH)), tbl, tbl] + [dyb] * 6,
        out_specs=[pl.BlockSpec((tt, 2 * B_WIDTH), lambda i: (i, 0)), _full((1, B_WIDTH))],
        out_shape=[jax.ShapeDtypeStruct((T, 2 * B_WIDTH), BF16), jax.ShapeDtypeStruct((1, B_WIDTH), F32)],
        compiler_params=_cp("arbitrary"), name=name)(kv, w_heads, cosf, sinsg, *dks, *dvs)


def _band_masks(n_is_first=None):
    row = lax.broadcasted_iota(jnp.int32, (SPAN, SPAN), 0)
    col = lax.broadcasted_iota(jnp.int32, (SPAN, SPAN), 1)
    return row >= col, col >= row


def _dil_views(T, d):
    L = T // d
    return L, L // SPAN


def _dil_fwd(qr, kr, kv, gi, d, *, name):
    T = qr.shape[0]
    L, nb = _dil_views(T, d)

    def body(q_ref, kc_ref, kp_ref, vc_ref, vp_ref, o_ref, lse_ref):
        cur_ok, prev_band = _band_masks()
        prev_ok = prev_band & (pl.program_id(1) > 0)
        for h in range(B_HEADS):
            sl = slice(h * HEAD_DIM, (h + 1) * HEAD_DIM)
            q = q_ref[:, sl]
            sc = jnp.where(cur_ok, _dot_nt(q, kc_ref[:, sl]) * ATT_SCALE, NEG)
            sp = jnp.where(prev_ok, _dot_nt(q, kp_ref[:, sl]) * ATT_SCALE, NEG)
            m = jnp.maximum(jnp.max(sc, axis=-1, keepdims=True), jnp.max(sp, axis=-1, keepdims=True))
            pc = jnp.exp(sc - m)
            pp = jnp.exp(sp - m)
            l = jnp.sum(pc, axis=-1, keepdims=True) + jnp.sum(pp, axis=-1, keepdims=True)
            o_ref[:, sl] = (_dot(pc, vc_ref[:, sl]) + _dot(pp, vp_ref[:, sl])) / l
            lse_ref[:, sl] = jnp.broadcast_to(m + jnp.log(l), (SPAN, HEAD_DIM))

    blk = lambda f: pl.BlockSpec((SPAN, B_WIDTH), f)
    cur = lambda r, n: (n, r)
    prev = lambda r, n: (jnp.maximum(n - 1, 0), r)
    ov = jax.ShapeDtypeStruct((L, d * B_WIDTH), F32)
    o, lse = pl.pallas_call(
        body, grid=(d, nb),
        in_specs=[blk(lambda r, n: (n, r * N_GROUPS + gi)), blk(cur), blk(prev),
                  blk(lambda r, n: (n, 2 * r + 1)), blk(lambda r, n: (jnp.maximum(n - 1, 0), 2 * r + 1))],
        out_specs=[blk(cur), blk(cur)], out_shape=[ov, ov],
        compiler_params=_cp("parallel", "arbitrary"), name=name,
    )(qr.reshape(L, d * N_GROUPS * B_WIDTH), kr.reshape(L, d * B_WIDTH), kr.reshape(L, d * B_WIDTH),
      kv.reshape(L, d * 2 * B_WIDTH), kv.reshape(L, d * 2 * B_WIDTH))
    return o.reshape(T, B_WIDTH), lse.reshape(T, B_WIDTH)


def _dil_combine_fwd(os_, lses, *, tt, name):
    T = os_[0].shape[0]

    def body(o0, o1, o2, l0, l1, l2, y_ref, lse_ref):
        a, b, c = l0[...], l1[...], l2[...]
        m = jnp.maximum(jnp.maximum(a, b), c)
        wa, wb, wc = jnp.exp(a - m), jnp.exp(b - m), jnp.exp(c - m)
        den = wa + wb + wc
        y_ref[...] = (wa * o0[...] + wb * o1[...] + wc * o2[...]) / den
        lse_ref[...] = m + jnp.log(den)

    blk = pl.BlockSpec((tt, B_WIDTH), lambda i: (i, 0))
    sh = jax.ShapeDtypeStruct((T, B_WIDTH), F32)
    return pl.pallas_call(
        body, grid=(T // tt,), in_specs=[blk] * 6, out_specs=[blk, blk], out_shape=[sh, sh],
        compiler_params=_cp("parallel"), name=name)(*os_, *lses)


DILS_UNROLL = 4


def _dils_specs(gi, d, nblk):
    blk = lambda f: pl.BlockSpec((SPAN * d, HEAD_DIM), f)
    return {
        "q": blk(lambda h, n: (n, gi * B_HEADS + h)), "q_next": blk(lambda h, n: (jnp.minimum(n + 1, nblk - 1), gi * B_HEADS + h)),
        "cur": blk(lambda h, n: (n, h)), "prev": blk(lambda h, n: (jnp.maximum(n - 1, 0), h)),
        "next": blk(lambda h, n: (jnp.minimum(n + 1, nblk - 1), h)),
        "v": blk(lambda h, n: (n, B_HEADS + h)), "v_prev": blk(lambda h, n: (jnp.maximum(n - 1, 0), B_HEADS + h)),
    }


def _dils_fwd(qr, kr, kv, gi, d, *, name):
    T = qr.shape[0]
    nblk = T // (SPAN * d)
    sp = _dils_specs(gi, d, nblk)

    def body(q_ref, kc_ref, vc_ref, o_ref, lse_ref, k_before, v_before):
        @pl.when(pl.program_id(1) == 0)
        def _():
            k_before[...] = jnp.zeros_like(k_before)
            v_before[...] = jnp.zeros_like(v_before)

        cur_ok, prev_band = _band_masks()
        prev_ok = prev_band & (pl.program_id(1) > 0)

        def residue(r, carry):
            rows = pl.ds(r, SPAN, stride=d)
            q, kc, vc = q_ref[rows, :], kc_ref[rows, :].astype(BF16), vc_ref[rows, :].astype(BF16)
            sc = jnp.where(cur_ok, _dot_nt(q, kc) * ATT_SCALE, NEG)
            sp_ = jnp.where(prev_ok, _dot_nt(q, k_before[r]) * ATT_SCALE, NEG)
            m = jnp.maximum(jnp.max(sc, axis=-1, keepdims=True), jnp.max(sp_, axis=-1, keepdims=True))
            pc = jnp.exp(sc - m)
            pp = jnp.exp(sp_ - m)
            l = jnp.sum(pc, axis=-1, keepdims=True) + jnp.sum(pp, axis=-1, keepdims=True)
            o_ref[rows, :] = (_dot(pc, vc) + _dot(pp, v_before[r])) / l
            lse_ref[rows, :] = jnp.broadcast_to(m + jnp.log(l), (SPAN, HEAD_DIM))
            k_before[r] = kc
            v_before[r] = vc
            return carry

        lax.fori_loop(0, d, residue, 0, unroll=DILS_UNROLL)

    sh = jax.ShapeDtypeStruct((T, B_WIDTH), F32)
    return pl.pallas_call(
        body, grid=(B_HEADS, nblk), in_specs=[sp["q"], sp["cur"], sp["v"]],
        out_specs=[sp["cur"], sp["cur"]], out_shape=[sh, sh],
        scratch_shapes=[pltpu.VMEM((d, SPAN, HEAD_DIM), BF16), pltpu.VMEM((d, SPAN, HEAD_DIM), BF16)],
        compiler_params=_cp("parallel", "arbitrary"), name=name)(qr, kr, kv)


DIL_BWD_GROUP = {1: 4, 4: 1, 16: 1}


def _dil_bwd(qr, kr, kv, dmix, lse, dd, gi, d, *, name, dep=None):
    T = qr.shape[0]
    G = DIL_BWD_GROUP[d]
    band = SPAN * d
    tb = G * band
    nblk = T // tb
    n_units = T // SPAN

    keep = G == 1

    def kernel_body(q_ref, dy_ref, lse_ref, dd_ref, kc_ref, vc_ref, *rest):
        if keep:
            dq_ref, dk_ref, dv_ref, dk_acc, dv_acc, k_before, v_before = rest
        else:
            kp_ref, vp_ref, dq_ref, dk_ref, dv_ref, dk_acc, dv_acc = rest
        n = pl.program_id(1)

        @pl.when(n == 0)
        def _():
            dk_acc[...] = jnp.zeros_like(dk_acc)
            dv_acc[...] = jnp.zeros_like(dv_acc)
            if keep:
                k_before[...] = jnp.zeros_like(k_before)
                v_before[...] = jnp.zeros_like(v_before)

        cur_ok, prev_band = _band_masks()
        for j in range(G):
            def residue(r, carry, j=j):
                off = j * band + r
                rows = pl.ds(off, SPAN, stride=d)
                q, dy = q_ref[rows, :], dy_ref[rows, :]
                lse_h = jnp.max(lse_ref[rows, :], axis=-1, keepdims=True)
                dd_h = jnp.max(dd_ref[rows, :], axis=-1, keepdims=True)
                kc, vc = kc_ref[rows, :].astype(BF16), vc_ref[rows, :].astype(BF16)
                if j > 0:
                    before = pl.ds(off - band, SPAN, stride=d)
                    kp, vp = kc_ref[before, :], vc_ref[before, :]
                    prev_ok = prev_band
                elif keep:
                    kp, vp = k_before[r], v_before[r]
                    k_before[r] = kc
                    v_before[r] = vc
                    prev_ok = prev_band & (n > 0)
                else:
                    before = pl.ds((G - 1) * band + r, SPAN, stride=d)
                    kp, vp = kp_ref[before, :], vp_ref[before, :]
                    prev_ok = prev_band & (n > 0)
                pc = jnp.exp(jnp.where(cur_ok, _dot_nt(q, kc) * ATT_SCALE, NEG) - lse_h)
                pp = jnp.exp(jnp.where(prev_ok, _dot_nt(q, kp) * ATT_SCALE, NEG) - lse_h)
                dsc = pc * (_dot_nt(dy, vc) - dd_h) * ATT_SCALE
                dsp = pp * (_dot_nt(dy, vp) - dd_h) * ATT_SCALE
                dq_ref[rows, :] = _dot(dsc, kc) + _dot(dsp, kp)
                u = (n * G + j) * d + r
                here = pl.ds(pl.multiple_of(u * SPAN, SPAN), SPAN)
                dk_acc[here, :] += _dot_tn(dsc, q)
                dv_acc[here, :] += _dot_tn(pc, dy)
                there = pl.ds(pl.multiple_of(jnp.maximum(u - d, 0) * SPAN, SPAN), SPAN)
                dk_acc[there, :] += _dot_tn(dsp, q)
                dv_acc[there, :] += _dot_tn(pp, dy)
                return carry

            lax.fori_loop(0, d, residue, 0, unroll=min(d, DILS_UNROLL))

        @pl.when(n == nblk - 1)
        def _():
            def place(u, carry):
                rows = pl.ds((u // d) * band + u % d, SPAN, stride=d)
                src = pl.ds(pl.multiple_of(u * SPAN, SPAN), SPAN)
                dk_ref[rows, :] = dk_acc[src, :]
                dv_ref[rows, :] = dv_acc[src, :]
                return carry

            lax.fori_loop(0, n_units, place, 0)

    blk = lambda f: pl.BlockSpec((tb, HEAD_DIM), f)
    cur = lambda h, n: (n, h)
    prev = lambda h, n: (jnp.maximum(n - 1, 0), h)
    whole = pl.BlockSpec((T, HEAD_DIM), lambda h, n: (0, h))
    sh = jax.ShapeDtypeStruct((T, B_WIDTH), F32)
    v_cur = blk(lambda h, n: (n, B_HEADS + h))
    if keep:
        kv_specs, kv_args = [blk(cur), v_cur], [kr, kv]
        kept = [pltpu.VMEM((d, SPAN, HEAD_DIM), BF16), pltpu.VMEM((d, SPAN, HEAD_DIM), BF16)]
    else:
        kv_specs = [blk(cur), v_cur, blk(prev), blk(lambda h, n: (jnp.maximum(n - 1, 0), B_HEADS + h))]
        kv_args, kept = [kr, kv, kr, kv], []
    body, dep_specs, dep_args = _dep(kernel_body, 4 + len(kv_args), dep)
    return pl.pallas_call(
        body, grid=(B_HEADS, nblk),
        in_specs=[blk(lambda h, n: (n, gi * B_HEADS + h)), blk(cur), blk(cur), blk(cur)] + kv_specs + dep_specs,
        out_specs=[blk(cur), whole, whole], out_shape=[sh, sh, sh],
        scratch_shapes=[pltpu.VMEM((T, HEAD_DIM), F32), pltpu.VMEM((T, HEAD_DIM), F32)] + kept,
        compiler_params=_cp("parallel", "arbitrary"), name=name)(qr, dmix, lse, dd, *kv_args, *dep_args)


A_MQ_COL = 4 * A_WIDTH // MEM_WIDTH
B_MQ_COL = N_GROUPS * B_WIDTH // MEM_WIDTH


def _row(v):
    return v.reshape(1, -1).astype(F32)


def _local_step(x, mem, tgt, get_w, P, put_g, first_dep=None, forward_point=lambda i, value: value):
    T = x.shape[0]
    cosf, sinsg = _rope_tables(T)
    lb_soft = jax.nn.softmax(P["a_lb_logits"].astype(F32), axis=0)
    lb = lb_soft[0:1]
    qw_heads = jnp.repeat(P["b_qnorm"][0], B_HEADS, axis=0).reshape(1, -1)
    kw_heads = jnp.tile(_row(P["b_knorm"]), (1, B_HEADS))
    mqw = [jnp.tile(_row(P["mem_qnorm"][l]), (1, MEM_HEADS)) for l in range(2)]
    mkw = [jnp.tile(_row(P["mem_knorm"][l]), (1, MEM_HEADS)) for l in range(2)]
    nmix = [_row(P["norm_mix"][l]) for l in range(2)]
    nffn = [_row(P["norm_ffn"][l]) for l in range(2)]
    mnorm = [_row(P["mem_norm"][l]) for l in range(2)]
    kvn = _row(P["kv_norm"])
    onorm = _row(P["a_onorm"])
    W = {}

    def w_of(name, after=None):
        if name not in W:
            W[name] = get_w(name, after)
        return W[name]

    proj_a, xn0 = _rms_matmul(x, nmix[0], w_of("a_w_in"), tt=512, tn=1664, wt=True, name="proj_a", dep=first_dep)
    mkv0, mn0 = _rms_matmul(mem, mnorm[0], w_of("w_mem_kv0"), tt=MEM_TOKENS, tn=2 * MEM_WIDTH, wt=False, name="mem_kv0")
    o_raw, st = _hgrn2_fwd(proj_a, lb, name="hgrn2_fwd")
    o_raw = forward_point(0, o_raw)
    mm0 = _a_post_fwd(o_raw, proj_a, onorm, tt=512, name="a_post_fwd")
    mo0 = _mem_attn_fwd(proj_a, A_MQ_COL, mkv0, mqw[0], mkw[0], tt=1024, name="mem_attn_fwd0")
    hm0 = _mm_res(x, mm0, mo0, w_of("w_out0", mo0), tt=512, name="out_proj0")
    hm0 = forward_point(1, hm0)
    gu0, hn0 = _rms_matmul(hm0, nffn[0], w_of("w_gate_up0", hm0), tt=512, tn=1408, wt=True, out_dtype=BF16, name="gate_up0")
    h1 = _swiglu_down(hm0, gu0, w_of("w_down0", gu0), tt=512, name="down0")
    h1 = forward_point(2, h1)
    kv, hkn, kr = _rms_matmul(h1, kvn, w_of("w_kv", h1), tt=512, tn=768, wt=True, name="kv_proj",
                              rotate=(kw_heads, cosf, sinsg))

    proj_b, xn1, qr = _rms_matmul(h1, nmix[1], w_of("b_w_in", kr), tt=512, tn=1280, wt=True, name="proj_b",
                                  rotate=(qw_heads, cosf, sinsg))
    proj_b = forward_point(3, proj_b)
    mkv1, mn1 = _rms_matmul(mem, mnorm[1], w_of("w_mem_kv1", kr), tt=MEM_TOKENS, tn=2 * MEM_WIDTH, wt=False, name="mem_kv1")
    outs = [(_dil_fwd if d == 1 else _dils_fwd)(qr, kr, kv, gi, d, name=f"dil_fwd{gi}") for gi, d in enumerate(DILATIONS)]
    mm1, lse_tot = _dil_combine_fwd([o for o, _ in outs], [s for _, s in outs], tt=512, name="dil_combine")
    mo1 = _mem_attn_fwd(proj_b, B_MQ_COL, mkv1, mqw[1], mkw[1], tt=1024, name="mem_attn_fwd1")
    hm1 = _mm_res(h1, mm1, mo1, w_of("w_out1", mo1), tt=512, name="out_proj1")
    gu1, hn1 = _rms_matmul(hm1, nffn[1], w_of("w_gate_up1", hm1), tt=512, tn=1408, wt=True, out_dtype=BF16, name="gate_up1")
    dy, sq = _swiglu_down_loss(hm1, gu1, w_of("w_down1", gu1), tgt, tt=512, name="down1_loss")

    gP = {}
    zeros_mem = jnp.zeros((MEM_TOKENS, D_MODEL), F32)

    def ffn_bwd(l, dh, hm, gu, hn):
        dgu, g_wd = _swiglu_bwd(dh, gu, w_of(f"w_down{l}"), tt=256, name=f"swiglu_bwd{l}")
        g_wgu = _mm_tn(dgu, hn, tt=512, tka=1408, name=f"g_w_gate_up{l}")
        sent = put_g({f"w_down{l}": g_wd, f"w_gate_up{l}": g_wgu})
        dhm, g_nf = _rms_bwd_dx(hm, nffn[l], w_of(f"w_gate_up{l}"), dgu, dh, tt=512, wt=True, name=f"gate_up_bwd{l}", dep=sent)
        return dhm, g_nf

    def mix_bwd(l, dhm, mix_main, mix_mem, proj, qcol, mkv, mn):
        dmix, g_wout, *head_dots = _out_proj_bwd(dhm, mix_main, mix_mem, w_of(f"w_out{l}"), tt=512, name=f"out_proj_bwd{l}",
                                                 head_dots=l == 1)
        dmq, dmkv, dqw, dkw = _mem_attn_bwd(proj, qcol, mkv, mqw[l], mkw[l], dmix, tt=1024, name=f"mem_attn_bwd{l}")
        g_wmkv = _mm_tn(mn, dmkv, tt=MEM_TOKENS, tka=512, name=f"g_w_mem_kv{l}")
        sent = put_g({f"w_out{l}": g_wout, f"w_mem_kv{l}": g_wmkv})
        _, g_mn = _rms_bwd_dx(mem, mnorm[l], w_of(f"w_mem_kv{l}"), dmkv, zeros_mem, tt=MEM_TOKENS, wt=False, name=f"mem_kv_bwd{l}")
        fold = lambda v: v.reshape(MEM_HEADS, MEM_HEAD_DIM).sum(axis=0)
        return dmix, dmq, g_mn, fold(dqw), fold(dkw), sent, head_dots

    dhm1, g_nf1 = ffn_bwd(1, dy, hm1, gu1, hn1)
    dmix1, dmq1, g_mn1, g_mq1, g_mk1, sent, (dd,) = mix_bwd(1, dhm1, mm1, mo1, proj_b, B_MQ_COL, mkv1, mn1)
    dqs, dks, dvs = [], [], []
    for gi, d in enumerate(DILATIONS):
        dq_g, dk_g, dv_g = _dil_bwd(qr, kr, kv, dmix1, lse_tot, dd, gi, d, name=f"dil_bwd{gi}", dep=sent if gi == 0 else None)
        dqs.append(dq_g)
        dks.append(dk_g)
        dvs.append(dv_g)
    dq_raw, dqw = _q_prep_bwd(proj_b, qw_heads, cosf, sinsg, dqs, tt=512, name="q_prep_bwd")
    dkv, dkw = _kv_prep_bwd(kv, kw_heads, cosf, sinsg, dks, dvs, tt=512, name="kv_prep_bwd")
    dproj_b = [dq_raw, dmq1]
    g_wb = _mm_tn_pieces(dproj_b, xn1, tt=512, name="g_b_w_in")
    g_wkv = _mm_tn(dkv, hkn, tt=512, tka=768, name="g_w_kv")
    sent = put_g({"b_w_in": g_wb, "w_kv": g_wkv})
    dh1, g_nm1 = _rms_bwd_dx(h1, nmix[1], w_of("b_w_in"), dproj_b, dhm1, tt=512, wt=True, name="proj_b_bwd", dep=sent)
    dh1, g_kvn = _rms_bwd_dx(h1, kvn, w_of("w_kv"), dkv, dh1, tt=512, wt=True, name="kv_proj_bwd")

    dhm0, g_nf0 = ffn_bwd(0, dh1, hm0, gu0, hn0)
    dmix0, dmq0, g_mn0, g_mq0, g_mk0, sent, _ = mix_bwd(0, dhm0, mm0, mo0, proj_a, A_MQ_COL, mkv0, mn0)
    do_raw, dg, g_onorm = _a_post_bwd(o_raw, proj_a, onorm, dmix0, tt=512, name="a_post_bwd", dep=sent)
    dq, dz, dv, dlb = _hgrn2_bwd(proj_a, lb, st, do_raw, name="hgrn2_bwd")
    dproj_a = [dq, dz, dv, dg, dmq0]
    sent = put_g({"a_w_in": _mm_tn_pieces(dproj_a, xn0, tt=512, name="g_a_w_in")})
    gx, g_nm0 = _rms_bwd_dx(x, nmix[0], w_of("a_w_in"), dproj_a, dhm0, tt=512, wt=True, name="proj_a_bwd", dep=sent)

    dl0 = lb_soft[0:1] * lb_soft[1:2] * dlb
    gP["a_lb_logits"] = jnp.concatenate([dl0, -dl0], axis=0)
    gP["a_onorm"] = g_onorm
    gP["norm_mix"] = jnp.concatenate([g_nm0, g_nm1], axis=0)
    gP["norm_ffn"] = jnp.concatenate([g_nf0, g_nf1], axis=0)
    gP["b_qnorm"] = dqw.reshape(N_GROUPS, B_HEADS, HEAD_DIM).sum(axis=1)[None]
    gP["kv_norm"] = g_kvn.reshape(-1)
    gP["b_knorm"] = dkw.reshape(B_HEADS, HEAD_DIM).sum(axis=0)
    gP["mem_norm"] = jnp.concatenate([g_mn0, g_mn1], axis=0)
    gP["mem_qnorm"] = jnp.stack([g_mq0, g_mq1])
    gP["mem_knorm"] = jnp.stack([g_mk0, g_mk1])
    return sq, gx, gP


MESH_ID = pl.DeviceIdType.MESH
HBM_SPEC = pl.BlockSpec(memory_space=pltpu.HBM)


def _position():
    return lax.axis_index("x"), lax.axis_index("y"), lax.axis_index("c")


def _all_gather_direct(block, after, *, name):
    def body(x_ref, after_ref, out_ref, send_sems, recv_sems, local_sem):
        x, y, c = _position()
        me = 4 * x + 2 * y + c
        mine = pltpu.make_async_copy(x_ref, out_ref.at[me], local_sem)
        mine.start()
        copies = []
        for k in ALL_PEERS:
            cp = pltpu.make_async_remote_copy(
                src_ref=x_ref, dst_ref=out_ref.at[me], send_sem=send_sems.at[k - 1], recv_sem=recv_sems.at[k - 1],
                device_id=_peer(k, x, y, c), device_id_type=MESH_ID)
            cp.start()
            copies.append(cp)
        for cp in copies:
            cp.wait()
        mine.wait()

    return pl.pallas_call(
        body, out_shape=jax.ShapeDtypeStruct((N_DEV,) + block.shape, block.dtype),
        in_specs=[HBM_SPEC, pl.BlockSpec(memory_space=pl.ANY)], out_specs=HBM_SPEC,
        scratch_shapes=[pltpu.SemaphoreType.DMA((7,)), pltpu.SemaphoreType.DMA((7,)), pltpu.SemaphoreType.DMA],
        name=name)(block, after)


SEM_SPEC = pl.BlockSpec(memory_space=pltpu.SEMAPHORE)
ANY_SPEC = pl.BlockSpec(memory_space=pl.ANY)
DATAFLOW = pltpu.SideEffectType.DATAFLOW_SIDE_EFFECTING


def _peer(k, x, y, c):
    return (1 - x if (k >> 2) & 1 else x, 1 - y if (k >> 1) & 1 else y, 1 - c if k & 1 else c)


def _own_slot_filled(own_block):
    x, y, c = _position()
    zone = lax.empty((N_DEV,) + own_block.shape, own_block.dtype)
    return lax.dynamic_update_slice_in_dim(zone, own_block[None], 4 * x + 2 * y + c, axis=0)


ALL_PEERS = tuple(range(1, N_DEV))
SIBLING_AND_SAME_CORE = (1, 2, 4, 6)
SAME_CORE = (2, 4, 6)


def _split_start(srcs, scatter, after, *, name, relations=ALL_PEERS, carried=None):
    n = len(srcs)
    extra = ([] if after is None else [after]) + ([] if carried is None else [carried])
    n_carried = 0 if carried is None else 1
    x, y, c = _position()
    me = 4 * x + 2 * y + c
    lands = [_own_slot_filled(lax.dynamic_index_in_dim(s, me, 0, keepdims=False) if scatter else s) for s in srcs]

    def body(*refs):
        src_refs, land_refs = refs[:n], refs[n:2 * n]
        send_sems, recv_sems = refs[2 * n + len(extra)], refs[2 * n + len(extra) + 1]
        token = refs[2 * n + len(extra) + 2 + 2 * n]
        bx, by, bc = _position()
        bme = 4 * bx + 2 * by + bc
        for a in range(n):
            for k in relations:
                tx, ty, tc = _peer(k, bx, by, bc)
                src = src_refs[a].at[4 * tx + 2 * ty + tc] if scatter else src_refs[a]
                pltpu.make_async_remote_copy(
                    src_ref=src, dst_ref=land_refs[a].at[bme],
                    send_sem=send_sems.at[7 * a + k - 1], recv_sem=recv_sems.at[7 * a + k - 1],
                    device_id=(tx, ty, tc), device_id_type=MESH_ID).start()
        token[...] = jnp.zeros_like(token)

    hbm = lambda a: pltpu.HBM(a.shape, a.dtype)
    outs = pl.pallas_call(
        body, name=name,
        out_shape=(pltpu.SemaphoreType.DMA((7 * n,)), pltpu.SemaphoreType.DMA((7 * n,)),
                   *[hbm(s) for s in srcs], *[hbm(l) for l in lands], jax.ShapeDtypeStruct((8, 128), F32),
                   *([hbm(carried)] if n_carried else [])),
        in_specs=[HBM_SPEC] * (2 * n) + [ANY_SPEC] * len(extra),
        out_specs=(SEM_SPEC, SEM_SPEC, *[HBM_SPEC] * (2 * n), pl.BlockSpec(memory_space=pltpu.VMEM), *([ANY_SPEC] * n_carried)),
        input_output_aliases={**{i: 2 + i for i in range(2 * n)},
                              **({2 * n + len(extra) - 1: 2 * n + 3} if n_carried else {})},
        compiler_params=pltpu.CompilerParams(has_side_effects=DATAFLOW),
    )(*[pltpu.with_memory_space_constraint(s, pltpu.HBM) for s in srcs],
      *[pltpu.with_memory_space_constraint(l, pltpu.HBM) for l in lands], *extra)
    return {"n": n, "relations": relations, "send": outs[0], "recv": outs[1], "srcs": list(outs[2:2 + n]),
            "lands": list(outs[2 + n:2 + 2 * n]), "token": outs[2 * n + 2], "carried": outs[-1] if n_carried else None}


def _forward_start(lands, carried, *, name):
    n = len(lands)

    def body(*refs):
        land_refs = refs[:n]
        send_sems, recv_sems = refs[n + 1], refs[n + 2]
        bx, by, bc = _position()
        for a in range(n):
            for k in SAME_CORE:
                tx, ty, tc = _peer(k, bx, by, bc)
                block = land_refs[a].at[4 * tx + 2 * ty + tc]
                pltpu.make_async_remote_copy(
                    src_ref=block, dst_ref=block,
                    send_sem=send_sems.at[7 * a + k - 1], recv_sem=recv_sems.at[7 * a + k - 1],
                    device_id=(bx, by, 1 - bc), device_id_type=MESH_ID).start()

    hbm = lambda a: pltpu.HBM(a.shape, a.dtype)
    outs = pl.pallas_call(
        body, name=name,
        out_shape=(pltpu.SemaphoreType.DMA((7 * n,)), pltpu.SemaphoreType.DMA((7 * n,)),
                   *[hbm(l) for l in lands], hbm(carried)),
        in_specs=[HBM_SPEC] * n + [ANY_SPEC],
        out_specs=(SEM_SPEC, SEM_SPEC, *[HBM_SPEC] * n, ANY_SPEC),
        input_output_aliases={i: 2 + i for i in range(n + 1)},
        compiler_params=pltpu.CompilerParams(has_side_effects=DATAFLOW),
    )(*lands, carried)
    handle = {"n": n, "relations": SAME_CORE, "send": outs[0], "recv": outs[1], "srcs": [], "lands": list(outs[2:2 + n])}
    return handle, outs[-1]


def _split_wait(handle, after, *, name):
    n, ns = handle["n"], len(handle["srcs"])

    def body(*refs):
        land_refs = refs[ns:ns + n]
        send_sems, recv_sems = refs[ns + n], refs[ns + n + 1]
        bx, by, bc = _position()
        for a in range(n):
            for k in handle["relations"]:
                block = land_refs[a].at[0]
                cp = pltpu.make_async_remote_copy(
                    src_ref=block, dst_ref=block,
                    send_sem=send_sems.at[7 * a + k - 1], recv_sem=recv_sems.at[7 * a + k - 1],
                    device_id=_peer(k, bx, by, bc), device_id_type=MESH_ID)
                cp.wait_send()
                cp.wait_recv()

    hbm = lambda a: pltpu.HBM(a.shape, a.dtype)
    outs = pl.pallas_call(
        body, name=name,
        out_shape=(*[hbm(s) for s in handle["srcs"]], *[hbm(l) for l in handle["lands"]]),
        in_specs=[HBM_SPEC] * (ns + n) + [SEM_SPEC, SEM_SPEC, ANY_SPEC],
        out_specs=tuple([HBM_SPEC] * (ns + n)),
        input_output_aliases={i: i for i in range(ns + n)},
        compiler_params=pltpu.CompilerParams(has_side_effects=DATAFLOW),
    )(*handle["srcs"], *handle["lands"], handle["send"], handle["recv"], after)
    return list(outs[ns:])


def _sum_sources(parts, *, tr, name):
    n, R, C = parts.shape

    def body(p_ref, o_ref):
        acc = p_ref[0].astype(F32)
        for s in range(1, n):
            acc = acc + p_ref[s].astype(F32)
        o_ref[...] = acc

    return pl.pallas_call(
        body, grid=(R // tr,), in_specs=[pl.BlockSpec((n, tr, C), lambda i: (0, i, 0))],
        out_specs=pl.BlockSpec((tr, C), lambda i: (i, 0)),
        out_shape=jax.ShapeDtypeStruct((R, C), F32), compiler_params=_cp("parallel"), name=name)(parts)


def _adamw_math(g, w, m, v):
    c1 = 1.0 - ADAM_B1 ** ADAM_STEP
    c2 = 1.0 - ADAM_B2 ** ADAM_STEP
    nm = ADAM_B1 * m + (1.0 - ADAM_B1) * g
    nv = ADAM_B2 * v + (1.0 - ADAM_B2) * (g * g)
    return -ADAM_LR * ((nm / c1) / (jnp.sqrt(nv / c2) + ADAM_EPS) + ADAM_WD * w), nm, nv


def _reduce_adamw(received, w, m, v, *, tr, name):
    L, R, C = w.shape

    def body(*refs):
        p_refs = refs[:L]
        w_ref, m_ref, v_ref, g_ref, d_ref, nm_ref, nv_ref = refs[L:]
        for l in range(L):
            @pl.when(pl.program_id(0) == l)
            def _(p_ref=p_refs[l]):
                acc = p_ref[0].astype(F32)
                for s in range(1, N_DEV):
                    acc = acc + p_ref[s].astype(F32)
                g_ref[...] = acc
                d_ref[...], nm_ref[...], nv_ref[...] = _adamw_math(acc, w_ref[...], m_ref[...], v_ref[...])

    p_spec = pl.BlockSpec((N_DEV, tr, C), lambda l, i: (0, i, 0))
    blk = pl.BlockSpec((None, tr, C), lambda l, i: (l, i, 0))
    sh = jax.ShapeDtypeStruct((L, R, C), F32)
    return pl.pallas_call(
        body, grid=(L, R // tr), in_specs=[p_spec] * L + [blk] * 3, out_specs=[blk] * 4, out_shape=[sh] * 4,
        compiler_params=_cp("parallel", "parallel"), name=name)(*received, w, m, v)


def _adamw(g, w, m, v, *, tr, name):
    L, R, C = w.shape

    def body(g_ref, w_ref, m_ref, v_ref, d_ref, nm_ref, nv_ref):
        d_ref[...], nm_ref[...], nv_ref[...] = _adamw_math(g_ref[...], w_ref[...], m_ref[...], v_ref[...])

    blk = pl.BlockSpec((None, tr, C), lambda l, i: (l, i, 0))
    sh = jax.ShapeDtypeStruct((L, R, C), F32)
    return pl.pallas_call(
        body, grid=(L, R // tr), in_specs=[blk] * 4, out_specs=[blk] * 3, out_shape=[sh] * 3,
        compiler_params=_cp("parallel", "parallel"), name=name)(g, w, m, v)


UNITS = {
    "a_w_in": ("a_w_in", 0, True), "w_mem_kv0": ("w_mem_kv", 0, False), "w_out0": ("w_out", 0, False),
    "w_gate_up0": ("w_gate_up", 0, True), "w_down0": ("w_down", 0, False), "w_kv": ("w_kv", None, True),
    "b_w_in": ("b_w_in", 0, True), "w_mem_kv1": ("w_mem_kv", 1, False), "w_out1": ("w_out", 1, False),
    "w_gate_up1": ("w_gate_up", 1, True), "w_down1": ("w_down", 1, False),
}
BIG = ("a_w_in", "b_w_in", "w_kv", "w_mem_kv", "w_out", "w_gate_up", "w_down")
ADAMW_ROW_TILE = {"a_w_in": 208, "b_w_in": 160, "w_kv": 192, "w_mem_kv": 128, "w_out": 128, "w_gate_up": 352, "w_down": 352}


def _wire_block(weights, unit):
    name, layer, col = UNITS[unit]
    a = weights[name] if layer is None else weights[name][layer]
    return (a.T if col else a).astype(BF16)


SMALL_REPLICATED = ("norm_mix", "norm_ffn", "b_qnorm", "kv_norm", "b_knorm", "mem_norm", "mem_qnorm", "mem_knorm")
SMALL_SHARDED = ("a_lb_logits", "a_onorm")
SMALL_ORDER = SMALL_REPLICATED + SMALL_SHARDED
LANES = 128


def _prod(shape):
    n = 1
    for s in shape:
        n *= s
    return n


def _pack_flat(arrays, rows, cols, dtype):
    flat = jnp.concatenate([a.reshape(-1).astype(dtype) for a in arrays])
    return jnp.pad(flat, (0, rows * cols - flat.shape[0])).reshape(rows, cols)


def _unpack_flat(packed, shapes):
    flat = packed.reshape(-1)
    out, off = [], 0
    for s in shapes:
        out.append(flat[off:off + _prod(s)].reshape(s))
        off += _prod(s)
    return out


def kernel(x, mem, norm_mix, norm_ffn, a_w_in, a_lb_logits, a_onorm, b_w_in, b_qnorm, kv_norm, w_kv, b_knorm, mem_norm, w_mem_kv, mem_qnorm, mem_knorm, w_out, w_gate_up, w_down, loss_target, m_norm_mix, m_norm_ffn, m_a_w_in, m_a_lb_logits, m_a_onorm, m_b_w_in, m_b_qnorm, m_kv_norm, m_w_kv, m_b_knorm, m_mem_norm, m_w_mem_kv, m_mem_qnorm, m_mem_knorm, m_w_out, m_w_gate_up, m_w_down, v_norm_mix, v_norm_ffn, v_a_w_in, v_a_lb_logits, v_a_onorm, v_b_w_in, v_b_qnorm, v_kv_norm, v_w_kv, v_b_knorm, v_mem_norm, v_w_mem_kv, v_mem_qnorm, v_mem_knorm, v_w_out, v_w_gate_up, v_w_down):
    names = ("norm_mix", "norm_ffn", "a_w_in", "a_lb_logits", "a_onorm", "b_w_in", "b_qnorm", "kv_norm", "w_kv", "b_knorm",
             "mem_norm", "w_mem_kv", "mem_qnorm", "mem_knorm", "w_out", "w_gate_up", "w_down")
    w = dict(zip(names, (norm_mix, norm_ffn, a_w_in, a_lb_logits, a_onorm, b_w_in, b_qnorm, kv_norm, w_kv, b_knorm,
                         mem_norm, w_mem_kv, mem_qnorm, mem_knorm, w_out, w_gate_up, w_down)))
    m = dict(zip(names, (m_norm_mix, m_norm_ffn, m_a_w_in, m_a_lb_logits, m_a_onorm, m_b_w_in, m_b_qnorm, m_kv_norm, m_w_kv,
                         m_b_knorm, m_mem_norm, m_w_mem_kv, m_mem_qnorm, m_mem_knorm, m_w_out, m_w_gate_up, m_w_down)))
    v = dict(zip(names, (v_norm_mix, v_norm_ffn, v_a_w_in, v_a_lb_logits, v_a_onorm, v_b_w_in, v_b_qnorm, v_kv_norm, v_w_kv,
                         v_b_knorm, v_mem_norm, v_w_mem_kv, v_mem_qnorm, v_mem_knorm, v_w_out, v_w_gate_up, v_w_down)))

    first = ["a_w_in", "w_mem_kv0"]
    later = [["w_out0", "w_gate_up0"], ["w_down0", "w_kv"], ["b_w_in", "w_mem_kv1"], ["w_out1", "w_gate_up1", "w_down1"]]
    first_half, second_half = {}, {}

    def start_first_half(i, after, carried=None):
        first_half[i] = _split_start([_wire_block(w, u) for u in later[i]], False, after, name=f"gather{i}_start",
                                     relations=SIBLING_AND_SAME_CORE, carried=carried)
        return first_half[i]

    opening = _split_start([_wire_block(w, u) for u in first] + [_pack_flat([a_lb_logits, a_onorm], 8, LANES, F32)],
                           False, None, name="gather_first_start", relations=SIBLING_AND_SAME_CORE)
    token = start_first_half(0, opening["token"])["token"]
    token = start_first_half(1, token)["token"]
    opening, token = _forward_start(_split_wait(opening, token, name="gather_first_landed"), token, name="gather_first_forward")
    gathered = _split_wait(opening, token, name="gather_first_wait")
    full = {u: g.reshape(-1, g.shape[-1]) for u, g in zip(first, gathered)}
    small_in = gathered[-1].reshape(N_DEV, -1)
    P = {n: w[n] for n in SMALL_REPLICATED}
    P["a_lb_logits"] = small_in[:, :192].reshape(N_DEV, 2, 96).transpose(1, 0, 2).reshape(2, A_WIDTH)
    P["a_onorm"] = small_in[:, 192:288].reshape(1, A_WIDTH)

    def forward_point(i, value):
        landed = _split_wait(first_half[i], value, name=f"gather{i}_landed")
        second_half[i], value = _forward_start(landed, value, name=f"gather{i}_forward")
        if i + 2 < len(later):
            value = start_first_half(i + 2, None, carried=value)["carried"]
        return value

    def get_w(unit, after):
        if unit not in full:
            i = [unit in group for group in later].index(True)
            for u, land in zip(later[i], _split_wait(second_half[i], after, name=f"gather{i}_wait")):
                full[u] = land.reshape(-1, land.shape[-1])
        return full[unit]

    sent = []

    def put_g(group):
        units = list(group)
        handle = _split_start([group[u].reshape(N_DEV, -1, group[u].shape[-1]) for u in units], True, None,
                              name=f"scatter{len(sent)}_start")
        sent.append((units, handle))
        return handle["token"]

    sq, gx, gP = _local_step(x[0], mem[0], loss_target[0], get_w, P, put_g, forward_point=forward_point)
    loss_here = (0.5 * jnp.sum(sq) / D_MODEL).reshape(1)

    received = {}
    group_of = {u: i for i, (units, _) in enumerate(sent) for u in units}
    out = {"grad": {}, "delta": {}, "new_m": {}, "new_v": {}}
    newest = [gx]

    def update_big(n):
        shape = w[n].shape
        as3 = lambda a: a.reshape((-1,) + shape[-2:])
        mine = [u for u, (wn, _, _) in UNITS.items() if wn == n]
        for i in sorted({group_of[u] for u in mine}):
            if sent[i][0][0] not in received:
                received.update(zip(sent[i][0], _split_wait(sent[i][1], newest[0], name=f"scatter{i}_wait")))
        flip = (lambda a: jnp.swapaxes(a, 1, 2)) if UNITS[mine[0]][2] else (lambda a: a)
        res = _reduce_adamw([received[u] for u in mine], flip(as3(w[n])), flip(as3(m[n])), flip(as3(v[n])),
                            tr=ADAMW_ROW_TILE[n], name=f"adamw_{n}")
        newest[0] = res[1]
        for kind, r in zip(("grad", "delta", "new_m", "new_v"), res):
            out[kind][n] = flip(r).reshape(shape)

    for n in ("w_down", "w_gate_up", "w_out", "w_mem_kv", "b_w_in", "w_kv"):
        update_big(n)

    full_shapes = [(2, A_WIDTH) if n == "a_lb_logits" else (1, A_WIDTH) if n == "a_onorm" else w[n].shape for n in SMALL_ORDER]
    n_small = sum(_prod(s) for s in full_shapes) + 1
    rows_small = -(-n_small // (8 * LANES)) * 8
    g_all = _all_gather_direct(_pack_flat([gP[n] for n in SMALL_ORDER] + [loss_here], rows_small, LANES, F32),
                               newest[0], name="gather_small_grads")
    summed = _unpack_flat(_sum_sources(g_all, tr=rows_small, name="sum_small_grads"), full_shapes + [(1,)])
    g_small = dict(zip(SMALL_ORDER, summed))
    loss = summed[-1].reshape(())
    me = 4 * lax.axis_index("x") + 2 * lax.axis_index("y") + lax.axis_index("c")
    for n in SMALL_SHARDED:
        g_small[n] = lax.dynamic_slice_in_dim(g_small[n], me * 96, 96, axis=1)
    shapes = [w[n].shape for n in SMALL_ORDER]
    rows_upd = -(-sum(_prod(s) for s in shapes) // (8 * LANES)) * 8
    pk = lambda d: _pack_flat([d[n] for n in SMALL_ORDER], rows_upd, LANES, F32)
    res = _adamw(pk(g_small)[None], pk(w)[None], pk(m)[None], pk(v)[None], tr=rows_upd, name="adamw_small")
    out["grad"].update(g_small)
    for kind, packed in zip(("delta", "new_m", "new_v"), res):
        out[kind].update(zip(SMALL_ORDER, _unpack_flat(packed[0], shapes)))
    newest[0] = res[0]
    update_big("a_w_in")

    return (loss, gx[None], *[out["grad"][n] for n in names], *[out["delta"][n] for n in names],
            *[out["new_m"][n] for n in names], *[out["new_v"][n] for n in names])
```

```python
import functools

import jax
import jax.numpy as jnp
import numpy as np
from jax import lax
from jax.experimental import pallas as pl
from jax.experimental.pallas import tpu as pltpu

F32 = jnp.float32
BF16 = jnp.bfloat16

N_DEV = 8
D_MODEL = 1024
HEAD_DIM = 128
A_HEADS = 6
A_WIDTH = A_HEADS * HEAD_DIM
CHUNK = 64
B_HEADS = 6
B_WIDTH = B_HEADS * HEAD_DIM
DILATIONS = (1, 4, 16)
SPAN = 128
N_GROUPS = 3
ROPE_THETA = 10000.0
MEM_TOKENS = 256
MEM_HEADS = 4
MEM_HEAD_DIM = 64
MEM_WIDTH = MEM_HEADS * MEM_HEAD_DIM
FFN_HIDDEN = 2816
EPS = 1e-6

ADAM_LR = 0.001
ADAM_B1 = 0.9
ADAM_B2 = 0.999
ADAM_EPS = 1e-08
ADAM_WD = 0.01
ADAM_STEP = 10

V7X_VMEM_LIMIT_BYTES = 56 * 1024 * 1024

NT_DIMS = (((1,), (1,)), ((), ()))
TN_DIMS = (((0,), (0,)), ((), ()))


def _cp(*sem):
    return pltpu.CompilerParams(dimension_semantics=sem, vmem_limit_bytes=V7X_VMEM_LIMIT_BYTES)


def _dot(a, b):
    return jnp.dot(a.astype(BF16), b.astype(BF16), preferred_element_type=F32)


def _dot_nt(a, b):
    return lax.dot_general(a.astype(BF16), b.astype(BF16), NT_DIMS, preferred_element_type=F32)


def _dot_tn(a, b):
    return lax.dot_general(a.astype(BF16), b.astype(BF16), TN_DIMS, preferred_element_type=F32)


def _dot3(m01, x):
    hi = x.astype(BF16)
    r1 = x - hi.astype(F32)
    mid = r1.astype(BF16)
    lo = (r1 - mid.astype(F32)).astype(BF16)
    d = functools.partial(jnp.dot, preferred_element_type=F32)
    return d(m01, hi) + d(m01, mid) + d(m01, lo)


def _sigmoid(x):
    return 0.5 * jnp.tanh(0.5 * x) + 0.5


def _full(shape):
    return pl.BlockSpec(shape, lambda *_: (0,) * len(shape))


def _dep(body, n_in, dep):
    if dep is None:
        return body, [], []

    def with_dep(*refs):
        return body(*refs[:n_in], *refs[n_in + 1:])

    return with_dep, [pl.BlockSpec(memory_space=pl.ANY)], [dep]


def _rms_matmul(x, g, w, *, tt, tn, wt, name, out_dtype=F32, dep=None, rotate=None):
    T, K = x.shape
    N = w.shape[0] if wt else w.shape[1]
    n_rot = 0 if rotate is None else rotate[0].shape[1] // HEAD_DIM
    extra_in = [] if rotate is None else list(rotate)

    def kernel_body(x_ref, g_ref, w_ref, *rest):
        y_ref, xn_ref = rest[len(extra_in)], rest[len(extra_in) + 1]
        xf = x_ref[...]
        r = lax.rsqrt(jnp.mean(xf * xf, axis=-1, keepdims=True) + EPS)
        xn = (xf * r * g_ref[...]).astype(BF16)
        xn_ref[...] = xn
        for j in range(N // tn):
            cols = slice(j * tn, (j + 1) * tn)
            y = _dot_nt(xn, w_ref[cols, :]) if wt else _dot(xn, w_ref[:, cols])
            y_ref[:, cols] = y.astype(out_dtype)
            for h in range(j * tn // HEAD_DIM, min((j + 1) * tn // HEAD_DIM, n_rot)):
                gw_ref, c_ref, s_ref, yr_ref = rest[0], rest[1], rest[2], rest[len(extra_in) + 2]
                sl = slice(h * HEAD_DIM, (h + 1) * HEAD_DIM)
                xhat, _ = _head_rms(y[:, h * HEAD_DIM - j * tn:(h + 1) * HEAD_DIM - j * tn])
                yr_ref[:, sl] = _rope(xhat * gw_ref[:, sl], c_ref[...], s_ref[...])

    tbl = pl.BlockSpec((tt, HEAD_DIM), lambda i: (i, 0))
    rot_specs = [] if rotate is None else [_full((1, n_rot * HEAD_DIM)), tbl, tbl]
    body, dep_specs, dep_args = _dep(kernel_body, 3 + len(extra_in), dep)
    return pl.pallas_call(
        body, grid=(T // tt,),
        in_specs=[pl.BlockSpec((tt, K), lambda i: (i, 0)), _full((1, K)), _full(w.shape)] + rot_specs + dep_specs,
        out_specs=[pl.BlockSpec((tt, N), lambda i: (i, 0)), pl.BlockSpec((tt, K), lambda i: (i, 0))]
        + ([] if rotate is None else [pl.BlockSpec((tt, n_rot * HEAD_DIM), lambda i: (i, 0))]),
        out_shape=[jax.ShapeDtypeStruct((T, N), out_dtype), jax.ShapeDtypeStruct((T, K), BF16)]
        + ([] if rotate is None else [jax.ShapeDtypeStruct((T, n_rot * HEAD_DIM), F32)]),
        compiler_params=_cp("parallel"), name=name)(x, g, w, *extra_in, *dep_args)


def _mm_res(res, a1, a2, w, *, tt, name):
    T, K1 = a1.shape
    K2 = a2.shape[1]
    N = w.shape[1]

    def body(r_ref, a1_ref, a2_ref, w_ref, o_ref):
        o_ref[...] = r_ref[...] + _dot(a1_ref[...], w_ref[:K1, :]) + _dot(a2_ref[...], w_ref[K1:, :])

    return pl.pallas_call(
        body, grid=(T // tt,),
        in_specs=[pl.BlockSpec((tt, N), lambda i: (i, 0)), pl.BlockSpec((tt, K1), lambda i: (i, 0)),
                  pl.BlockSpec((tt, K2), lambda i: (i, 0)), _full((K1 + K2, N))],
        out_specs=pl.BlockSpec((tt, N), lambda i: (i, 0)),
        out_shape=jax.ShapeDtypeStruct((T, N), F32),
        compiler_params=_cp("parallel"), name=name)(res, a1, a2, w)


def _swiglu_down(h, gu, wd, *, tt, name):
    T, D = h.shape
    Fh = wd.shape[0]

    def body(h_ref, gt_ref, up_ref, w_ref, o_ref):
        gt = gt_ref[...].astype(F32)
        act = gt * _sigmoid(gt) * up_ref[...].astype(F32)
        o_ref[...] = h_ref[...] + _dot(act, w_ref[...])

    return pl.pallas_call(
        body, grid=(T // tt,),
        in_specs=[pl.BlockSpec((tt, D), lambda i: (i, 0)), pl.BlockSpec((tt, Fh), lambda i: (i, 0)),
                  pl.BlockSpec((tt, Fh), lambda i: (i, 1)), _full((Fh, D))],
        out_specs=pl.BlockSpec((tt, D), lambda i: (i, 0)),
        out_shape=jax.ShapeDtypeStruct((T, D), F32),
        compiler_params=_cp("parallel"), name=name)(h, gu, gu, wd)


def _swiglu_down_loss(h, gu, wd, tgt, *, tt, name):
    T, D = h.shape
    Fh = wd.shape[0]

    def body(h_ref, gt_ref, up_ref, w_ref, t_ref, dy_ref, acc_ref):
        @pl.when(pl.program_id(0) == 0)
        def _():
            acc_ref[...] = jnp.zeros_like(acc_ref)

        gt = gt_ref[...].astype(F32)
        act = gt * _sigmoid(gt) * up_ref[...].astype(F32)
        e = h_ref[...] + _dot(act, w_ref[...]) - t_ref[...]
        dy_ref[...] = e * (1.0 / D)
        acc_ref[...] += jnp.sum(e * e, axis=0, keepdims=True)

    row = pl.BlockSpec((tt, D), lambda i: (i, 0))
    return pl.pallas_call(
        body, grid=(T // tt,),
        in_specs=[row, pl.BlockSpec((tt, Fh), lambda i: (i, 0)), pl.BlockSpec((tt, Fh), lambda i: (i, 1)), _full((Fh, D)), row],
        out_specs=[row, _full((1, D))],
        out_shape=[jax.ShapeDtypeStruct((T, D), F32), jax.ShapeDtypeStruct((1, D), F32)],
        compiler_params=_cp("arbitrary"), name=name)(h, gu, gu, wd, tgt)


SWIGLU_COLS = 256


def _swiglu_bwd(dh, gu, wd, *, tt, name):
    T, D = dh.shape
    Fh = wd.shape[0]
    last = T // tt - 1

    def body(dh_ref, gt_ref, up_ref, w_ref, dgu_ref, gw_ref, acc):
        @pl.when(pl.program_id(0) == 0)
        def _():
            acc[...] = jnp.zeros_like(acc)

        dh16 = dh_ref[...].astype(BF16)
        for c0 in range(0, Fh, SWIGLU_COLS):
            cols = slice(c0, c0 + SWIGLU_COLS)
            gt = gt_ref[:, cols].astype(F32)
            up = up_ref[:, cols].astype(F32)
            s = _sigmoid(gt)
            silu = gt * s
            dact = _dot_nt(dh16, w_ref[cols, :])
            acc[cols, :] += _dot_tn((silu * up).astype(BF16), dh16)
            dgu_ref[:, cols] = (dact * up * (s * (1.0 + gt * (1.0 - s)))).astype(BF16)
            dgu_ref[:, Fh + c0:Fh + c0 + SWIGLU_COLS] = (dact * silu).astype(BF16)

        @pl.when(pl.program_id(0) == last)
        def _():
            gw_ref[...] = acc[...].astype(BF16)

    return pl.pallas_call(
        body, grid=(T // tt,),
        in_specs=[pl.BlockSpec((tt, D), lambda i: (i, 0)), pl.BlockSpec((tt, Fh), lambda i: (i, 0)),
                  pl.BlockSpec((tt, Fh), lambda i: (i, 1)), _full((Fh, D))],
        out_specs=[pl.BlockSpec((tt, 2 * Fh), lambda i: (i, 0)), _full((Fh, D))],
        out_shape=[jax.ShapeDtypeStruct((T, 2 * Fh), BF16), jax.ShapeDtypeStruct((Fh, D), BF16)],
        scratch_shapes=[pltpu.VMEM((Fh, D), F32)],
        compiler_params=_cp("arbitrary"), name=name)(dh, gu, gu, wd)


def _out_proj_bwd(dy, a1, a2, w, *, tt, name, head_dots=False):
    T, N = dy.shape
    K1, K2 = a1.shape[1], a2.shape[1]
    K = K1 + K2
    last = T // tt - 1

    def body(dy_ref, a1_ref, a2_ref, w_ref, da_ref, gw_ref, *rest):
        acc = rest[-1]

        @pl.when(pl.program_id(0) == 0)
        def _():
            acc[...] = jnp.zeros_like(acc)

        dy16 = dy_ref[...].astype(BF16)
        da = _dot_nt(dy16, w_ref[...])
        da_ref[...] = da
        acc[:K1, :] += _dot_tn(a1_ref[...], dy16)
        acc[K1:, :] += _dot_tn(a2_ref[...], dy16)
        if head_dots:
            for h in range(K1 // HEAD_DIM):
                sl = slice(h * HEAD_DIM, (h + 1) * HEAD_DIM)
                rest[0][:, sl] = jnp.broadcast_to(jnp.sum(da[:, sl] * a1_ref[:, sl], axis=-1, keepdims=True), (tt, HEAD_DIM))

        @pl.when(pl.program_id(0) == last)
        def _():
            gw_ref[...] = acc[...].astype(BF16)

    extra_specs = [pl.BlockSpec((tt, K1), lambda i: (i, 0))] if head_dots else []
    extra_shapes = [jax.ShapeDtypeStruct((T, K1), F32)] if head_dots else []
    return pl.pallas_call(
        body, grid=(T // tt,),
        in_specs=[pl.BlockSpec((tt, N), lambda i: (i, 0)), pl.BlockSpec((tt, K1), lambda i: (i, 0)),
                  pl.BlockSpec((tt, K2), lambda i: (i, 0)), _full((K, N))],
        out_specs=[pl.BlockSpec((tt, K), lambda i: (i, 0)), _full((K, N))] + extra_specs,
        out_shape=[jax.ShapeDtypeStruct((T, K), F32), jax.ShapeDtypeStruct((K, N), BF16)] + extra_shapes,
        scratch_shapes=[pltpu.VMEM((K, N), F32)],
        compiler_params=_cp("arbitrary"), name=name)(dy, a1, a2, w)


def _mm_tn(a, b, *, tt, tka, name):
    T, Ka = a.shape
    N = b.shape[1]
    last = T // tt - 1

    def body(a_ref, b_ref, o_ref, acc):
        @pl.when(pl.program_id(1) == 0)
        def _():
            acc[...] = jnp.zeros_like(acc)

        acc[...] += _dot_tn(a_ref[...], b_ref[...])

        @pl.when(pl.program_id(1) == last)
        def _():
            o_ref[...] = acc[...].astype(BF16)

    return pl.pallas_call(
        body, grid=(Ka // tka, T // tt),
        in_specs=[pl.BlockSpec((tt, tka), lambda j, t: (t, j)), pl.BlockSpec((tt, N), lambda j, t: (t, 0))],
        out_specs=pl.BlockSpec((tka, N), lambda j, t: (j, 0)),
        out_shape=jax.ShapeDtypeStruct((Ka, N), BF16),
        scratch_shapes=[pltpu.VMEM((tka, N), F32)],
        compiler_params=_cp("parallel", "arbitrary"), name=name)(a, b)


def _mm_tn_pieces(pieces, b, *, tt, name):
    n = len(pieces)
    T = b.shape[0]
    N = b.shape[1]
    widths = [p.shape[1] for p in pieces]
    Ka = sum(widths)
    last = T // tt - 1

    def body(*refs):
        p_refs = refs[:n]
        b_ref, o_ref, acc = refs[n:]

        @pl.when(pl.program_id(0) == 0)
        def _():
            acc[...] = jnp.zeros_like(acc)

        bv = b_ref[...].astype(BF16)
        off = 0
        for p_ref, wd in zip(p_refs, widths):
            acc[off:off + wd, :] += _dot_tn(p_ref[...], bv)
            off += wd

        @pl.when(pl.program_id(0) == last)
        def _():
            o_ref[...] = acc[...].astype(BF16)

    return pl.pallas_call(
        body, grid=(T // tt,),
        in_specs=[pl.BlockSpec((tt, wd), lambda t: (t, 0)) for wd in widths] + [pl.BlockSpec((tt, N), lambda t: (t, 0))],
        out_specs=_full((Ka, N)), out_shape=jax.ShapeDtypeStruct((Ka, N), BF16),
        scratch_shapes=[pltpu.VMEM((Ka, N), F32)],
        compiler_params=_cp("arbitrary"), name=name)(*pieces, b)


def _rms_bwd_dx(x, g, w, dy, dres, *, tt, wt, name, dep=None):
    pieces = list(dy) if isinstance(dy, (list, tuple)) else [dy]
    n = len(pieces)
    widths = [p.shape[1] for p in pieces]
    T, K = x.shape

    def kernel_body(x_ref, g_ref, w_ref, *rest):
        dy_refs = rest[:n]
        dres_ref, dx_ref, dg_ref = rest[n:]

        @pl.when(pl.program_id(0) == 0)
        def _():
            dg_ref[...] = jnp.zeros_like(dg_ref)

        if n == 1:
            dxn = (_dot if wt else _dot_nt)(dy_refs[0][...], w_ref[...])
        else:
            dxn, off = 0.0, 0
            for dy_ref, wd in zip(dy_refs, widths):
                dxn = dxn + _dot(dy_ref[...], w_ref[off:off + wd, :])
                off += wd
        xf = x_ref[...]
        r = lax.rsqrt(jnp.mean(xf * xf, axis=-1, keepdims=True) + EPS)
        xhat = xf * r
        dg_ref[...] += jnp.sum(dxn * xhat, axis=0, keepdims=True)
        dxhat = dxn * g_ref[...]
        dx_ref[...] = dres_ref[...] + r * (dxhat - xhat * jnp.mean(dxhat * xhat, axis=-1, keepdims=True))

    assert n == 1 or wt
    body, dep_specs, dep_args = _dep(kernel_body, 4 + n, dep)
    return pl.pallas_call(
        body, grid=(T // tt,),
        in_specs=[pl.BlockSpec((tt, K), lambda i: (i, 0)), _full((1, K)), _full(w.shape)]
        + [pl.BlockSpec((tt, wd), lambda i: (i, 0)) for wd in widths]
        + [pl.BlockSpec((tt, K), lambda i: (i, 0))] + dep_specs,
        out_specs=[pl.BlockSpec((tt, K), lambda i: (i, 0)), _full((1, K))],
        out_shape=[jax.ShapeDtypeStruct((T, K), F32), jax.ShapeDtypeStruct((1, K), F32)],
        compiler_params=_cp("arbitrary"), name=name)(x, g, w, *pieces, dres, *dep_args)


HGRN_TB = 512
HGRN_NCH = HGRN_TB // CHUNK
HGRN_UNROLL = 8
HGRN_HPB = 6


def _hgrn_chunk_fwd(q, z, lbv, tril01):
    sig = _sigmoid(z)
    f = lbv + (1.0 - lbv) * sig
    kk = 1.0 - f
    b = _dot3(tril01, jnp.log(f))
    bend = b[CHUNK - 1:CHUNK, :]
    sq = _sigmoid(q)
    eb = jnp.exp(b)
    emb = jnp.exp(-b)
    eo = jnp.exp(bend - b)
    dec = jnp.exp(bend)
    return sig, f, kk, sq, eb, emb, eo, dec


def _hgrn2_fwd(proj, lb, *, name):
    T = proj.shape[0]
    nT = T // HGRN_TB
    nC = T // CHUNK

    def body(q_ref, z_ref, v_ref, lb_ref, o_ref, st_ref, state):
        @pl.when(pl.program_id(1) == 0)
        def _():
            state[...] = jnp.zeros_like(state)

        row = lax.broadcasted_iota(jnp.int32, (CHUNK, CHUNK), 0)
        col = lax.broadcasted_iota(jnp.int32, (CHUNK, CHUNK), 1)
        causal = row >= col
        tril01 = causal.astype(BF16)

        def chunk(c, carry):
            rows = pl.ds(pl.multiple_of(c * CHUNK, CHUNK), CHUNK)
            for hh in range(HGRN_HPB):
                sl = slice(hh * HEAD_DIM, (hh + 1) * HEAD_DIM)
                q = q_ref[rows, sl]
                v = v_ref[rows, sl].astype(BF16)
                sig, f, kk, sq, eb, emb, eo, dec = _hgrn_chunk_fwd(q, z_ref[rows, sl], lb_ref[:, sl], tril01)
                qi = (q * sq * eb).astype(BF16)
                ki = (kk * emb).astype(BF16)
                ko = (kk * eo).astype(BF16)
                st = state[hh]
                att = jnp.where(causal, _dot_nt(qi, ki), 0.0)
                o_ref[rows, sl] = _dot(att, v) + _dot_nt(qi, st)
                st_ref[c, hh] = st
                state[hh] = st * dec + _dot_tn(v, ko)
            return carry

        lax.fori_loop(0, HGRN_NCH, chunk, 0, unroll=HGRN_UNROLL)

    W = HGRN_HPB * HEAD_DIM
    nG = A_HEADS // HGRN_HPB
    hb = lambda off: pl.BlockSpec((HGRN_TB, W), lambda h, i: (i, off + h))
    return pl.pallas_call(
        body, grid=(nG, nT),
        in_specs=[hb(0), hb(nG), hb(2 * nG), pl.BlockSpec((1, W), lambda h, i: (0, h))],
        out_specs=[hb(0), pl.BlockSpec((HGRN_NCH, HGRN_HPB, HEAD_DIM, HEAD_DIM), lambda h, i: (i, h, 0, 0))],
        out_shape=[jax.ShapeDtypeStruct((T, A_WIDTH), F32), jax.ShapeDtypeStruct((nC, A_HEADS, HEAD_DIM, HEAD_DIM), F32)],
        scratch_shapes=[pltpu.VMEM((HGRN_HPB, HEAD_DIM, HEAD_DIM), F32)],
        compiler_params=_cp("parallel", "arbitrary"), name=name)(proj, proj, proj, lb)


def _hgrn2_bwd(proj, lb, st_all, do, *, name):
    T = proj.shape[0]
    nT = T // HGRN_TB

    def body(q_ref, z_ref, v_ref, lb_ref, st_ref, do_ref, dq_ref, dz_ref, dv_ref, dlb_ref, dstate):
        @pl.when(pl.program_id(1) == 0)
        def _():
            dstate[...] = jnp.zeros_like(dstate)
            dlb_ref[...] = jnp.zeros_like(dlb_ref)

        row = lax.broadcasted_iota(jnp.int32, (CHUNK, CHUNK), 0)
        col = lax.broadcasted_iota(jnp.int32, (CHUNK, CHUNK), 1)
        causal = row >= col
        tril01 = causal.astype(BF16)
        triu01 = (row <= col).astype(BF16)

        def chunk(cc, carry):
            c = HGRN_NCH - 1 - cc
            rows = pl.ds(pl.multiple_of(c * CHUNK, CHUNK), CHUNK)
            for hh in range(HGRN_HPB):
                sl = slice(hh * HEAD_DIM, (hh + 1) * HEAD_DIM)
                lbv = lb_ref[:, sl]
                q = q_ref[rows, sl]
                v = v_ref[rows, sl].astype(BF16)
                sig, f, kk, sq, eb, emb, eo, dec = _hgrn_chunk_fwd(q, z_ref[rows, sl], lbv, tril01)
                qi32 = q * sq * eb
                ki32 = kk * emb
                ko32 = kk * eo
                qi, ki, ko = qi32.astype(BF16), ki32.astype(BF16), ko32.astype(BF16)
                att = jnp.where(causal, _dot_nt(qi, ki), 0.0).astype(BF16)
                dout = do_ref[rows, sl].astype(BF16)
                st = st_ref[c, hh]
                dst = dstate[hh]
                dst16 = dst.astype(BF16)
                datt = jnp.where(causal, _dot_nt(dout, v), 0.0).astype(BF16)
                dqi = _dot(datt, ki) + _dot(dout, st)
                dki = _dot_tn(datt, qi)
                dv_ref[rows, sl] = (_dot_tn(att, dout) + _dot_nt(ko, dst16)).astype(BF16)
                dko = _dot(v, dst16)
                ddec = jnp.sum(dst * st, axis=0, keepdims=True)
                dstate[hh] = dst * dec + _dot_tn(dout, qi)
                dkk = dki * emb + dko * eo
                db = dqi * qi32 - dki * ki32 - dko * ko32
                dbend = jnp.sum(dko * ko32, axis=0, keepdims=True) + ddec * dec
                dlogf = _dot3(triu01, db) + dbend
                df = dlogf / f - dkk
                dz_ref[rows, sl] = (df * (1.0 - lbv) * sig * (1.0 - sig)).astype(BF16)
                dlb_ref[:, sl] += jnp.sum(df * (1.0 - sig), axis=0, keepdims=True)
                dq_ref[rows, sl] = (dqi * eb * (sq * (1.0 + q * (1.0 - sq)))).astype(BF16)
            return carry

        lax.fori_loop(0, HGRN_NCH, chunk, 0, unroll=HGRN_UNROLL)

    W = HGRN_HPB * HEAD_DIM
    nG = A_HEADS // HGRN_HPB
    hb = lambda off: pl.BlockSpec((HGRN_TB, W), lambda h, i: (nT - 1 - i, off + h))
    hlb = pl.BlockSpec((1, W), lambda h, i: (0, h))
    o16 = jax.ShapeDtypeStruct((T, A_WIDTH), BF16)
    return pl.pallas_call(
        body, grid=(nG, nT),
        in_specs=[hb(0), hb(nG), hb(2 * nG), hlb,
                  pl.BlockSpec((HGRN_NCH, HGRN_HPB, HEAD_DIM, HEAD_DIM), lambda h, i: (nT - 1 - i, h, 0, 0)), hb(0)],
        out_specs=[hb(0), hb(0), hb(0), hlb],
        out_shape=[o16, o16, o16, jax.ShapeDtypeStruct((1, A_WIDTH), F32)],
        scratch_shapes=[pltpu.VMEM((HGRN_HPB, HEAD_DIM, HEAD_DIM), F32)],
        compiler_params=_cp("parallel", "arbitrary"), name=name)(proj, proj, proj, lb, st_all, do)


def _head_rms(x):
    r = lax.rsqrt(jnp.mean(x * x, axis=-1, keepdims=True) + EPS)
    return x * r, r


def _head_rms_bwd(dxhat, xhat, r):
    return r * (dxhat - xhat * jnp.mean(dxhat * xhat, axis=-1, keepdims=True))


def _a_post_fwd(o, proj, onorm, *, tt, name):
    T = o.shape[0]

    def body(o_ref, g_ref, w_ref, y_ref):
        for h in range(A_HEADS):
            sl = slice(h * HEAD_DIM, (h + 1) * HEAD_DIM)
            xhat, _ = _head_rms(o_ref[:, sl])
            g = g_ref[:, sl]
            y_ref[:, sl] = xhat * w_ref[:, sl] * (g * _sigmoid(g))

    blk = lambda c: pl.BlockSpec((tt, A_WIDTH), lambda i: (i, c))
    return pl.pallas_call(
        body, grid=(T // tt,), in_specs=[blk(0), blk(3), _full((1, A_WIDTH))], out_specs=blk(0),
        out_shape=jax.ShapeDtypeStruct((T, A_WIDTH), F32),
        compiler_params=_cp("parallel"), name=name)(o, proj, onorm)


def _a_post_bwd(o, proj, onorm, dmix, *, tt, name, dep=None):
    T = o.shape[0]

    def kernel_body(o_ref, g_ref, w_ref, dy_ref, do_ref, dg_ref, dw_ref):
        @pl.when(pl.program_id(0) == 0)
        def _():
            dw_ref[...] = jnp.zeros_like(dw_ref)

        for h in range(A_HEADS):
            sl = slice(h * HEAD_DIM, (h + 1) * HEAD_DIM)
            xhat, r = _head_rms(o_ref[:, sl])
            g = g_ref[:, sl]
            s = _sigmoid(g)
            dy = dy_ref[:, sl]
            w = w_ref[:, sl]
            dg_ref[:, sl] = (dy * xhat * w * (s * (1.0 + g * (1.0 - s)))).astype(BF16)
            dyn = dy * (g * s)
            dw_ref[:, sl] += jnp.sum(dyn * xhat, axis=0, keepdims=True)
            do_ref[:, sl] = _head_rms_bwd(dyn * w, xhat, r)

    blk = lambda c: pl.BlockSpec((tt, A_WIDTH), lambda i: (i, c))
    body, dep_specs, dep_args = _dep(kernel_body, 4, dep)
    return pl.pallas_call(
        body, grid=(T // tt,), in_specs=[blk(0), blk(3), _full((1, A_WIDTH)), blk(0)] + dep_specs,
        out_specs=[blk(0), blk(0), _full((1, A_WIDTH))],
        out_shape=[jax.ShapeDtypeStruct((T, A_WIDTH), F32), jax.ShapeDtypeStruct((T, A_WIDTH), BF16),
                   jax.ShapeDtypeStruct((1, A_WIDTH), F32)],
        compiler_params=_cp("arbitrary"), name=name)(o, proj, onorm, dmix, *dep_args)


def _mem_head_masks(n):
    lane = lax.broadcasted_iota(jnp.int32, (n, MEM_WIDTH), 1)
    return [(lane >= m * MEM_HEAD_DIM) & (lane < (m + 1) * MEM_HEAD_DIM) for m in range(MEM_HEADS)]


def _mem_head_rms(x, masks):
    x2 = x * x
    r = jnp.zeros_like(x)
    for mk in masks:
        ms = jnp.sum(jnp.where(mk, x2, 0.0), axis=-1, keepdims=True) * (1.0 / MEM_HEAD_DIM)
        r = jnp.where(mk, lax.rsqrt(ms + EPS), r)
    return x * r, r


def _mem_head_rms_bwd(dxhat, xhat, r, masks):
    t = dxhat * xhat
    m = jnp.zeros_like(t)
    for mk in masks:
        m = jnp.where(mk, jnp.sum(jnp.where(mk, t, 0.0), axis=-1, keepdims=True) * (1.0 / MEM_HEAD_DIM), m)
    return r * (dxhat - xhat * m)


MEM_SCALE = MEM_HEAD_DIM ** -0.5


def _mem_attn_fwd(proj, qcol, mkv, qn_w, kn_w, *, tt, name):
    T = proj.shape[0]

    def body(q_ref, k_ref, v_ref, qw_ref, kw_ref, o_ref):
        qmasks = _mem_head_masks(tt)
        kmasks = _mem_head_masks(MEM_TOKENS)
        qhat, _ = _mem_head_rms(q_ref[...], qmasks)
        qn = qhat * qw_ref[...]
        khat, _ = _mem_head_rms(k_ref[...], kmasks)
        kn = (khat * kw_ref[...]).astype(BF16)
        v = v_ref[...].astype(BF16)
        out = jnp.zeros((tt, MEM_WIDTH), F32)
        for m in range(MEM_HEADS):
            s = _dot_nt(jnp.where(qmasks[m], qn, 0.0), kn) * MEM_SCALE
            s = s - jnp.max(s, axis=-1, keepdims=True)
            p = jnp.exp(s)
            p = p / jnp.sum(p, axis=-1, keepdims=True)
            out = jnp.where(qmasks[m], _dot(p, v), out)
        o_ref[...] = out

    return pl.pallas_call(
        body, grid=(T // tt,),
        in_specs=[pl.BlockSpec((tt, MEM_WIDTH), lambda i: (i, qcol)), pl.BlockSpec((MEM_TOKENS, MEM_WIDTH), lambda i: (0, 0)),
                  pl.BlockSpec((MEM_TOKENS, MEM_WIDTH), lambda i: (0, 1)), _full((1, MEM_WIDTH)), _full((1, MEM_WIDTH))],
        out_specs=pl.BlockSpec((tt, MEM_WIDTH), lambda i: (i, 0)),
        out_shape=jax.ShapeDtypeStruct((T, MEM_WIDTH), F32),
        compiler_params=_cp("parallel"), name=name)(proj, mkv, mkv, qn_w, kn_w)


def _mem_attn_bwd(proj, qcol, mkv, qn_w, kn_w, dmix, *, tt, name):
    T = proj.shape[0]
    nsteps = T // tt
    ocol = (dmix.shape[1] - MEM_WIDTH) // MEM_WIDTH

    def body(q_ref, k_ref, v_ref, qw_ref, kw_ref, do_ref, dq_ref, dkv_ref, dqw_ref, dkw_ref, dk_acc, dv_acc):
        step = pl.program_id(0)

        @pl.when(step == 0)
        def _():
            dk_acc[...] = jnp.zeros_like(dk_acc)
            dv_acc[...] = jnp.zeros_like(dv_acc)
            dqw_ref[...] = jnp.zeros_like(dqw_ref)

        qmasks = _mem_head_masks(tt)
        kmasks = _mem_head_masks(MEM_TOKENS)
        qhat, qr = _mem_head_rms(q_ref[...], qmasks)
        qn = qhat * qw_ref[...]
        khat, kr = _mem_head_rms(k_ref[...], kmasks)
        kn = (khat * kw_ref[...]).astype(BF16)
        v = v_ref[...].astype(BF16)
        dout = do_ref[...]
        dqn = jnp.zeros((tt, MEM_WIDTH), F32)
        dkn = jnp.zeros((MEM_TOKENS, MEM_WIDTH), F32)
        dvv = jnp.zeros((MEM_TOKENS, MEM_WIDTH), F32)
        for m in range(MEM_HEADS):
            qm = jnp.where(qmasks[m], qn, 0.0).astype(BF16)
            s = _dot_nt(qm, kn) * MEM_SCALE
            s = s - jnp.max(s, axis=-1, keepdims=True)
            p = jnp.exp(s)
            p = p / jnp.sum(p, axis=-1, keepdims=True)
            dom = jnp.where(qmasks[m], dout, 0.0).astype(BF16)
            dp = _dot_nt(dom, v)
            ds = (p * (dp - jnp.sum(p * dp, axis=-1, keepdims=True)) * MEM_SCALE).astype(BF16)
            dqn = jnp.where(qmasks[m], _dot(ds, kn), dqn)
            dkn = jnp.where(kmasks[m], _dot_tn(ds, qm), dkn)
            dvv = jnp.where(kmasks[m], _dot_tn(p, dom), dvv)
        dqw_ref[...] += jnp.sum(dqn * qhat, axis=0, keepdims=True)
        dq_ref[...] = _mem_head_rms_bwd(dqn * qw_ref[...], qhat, qr, qmasks).astype(BF16)
        dk_acc[...] += dkn
        dv_acc[...] += dvv

        @pl.when(step == nsteps - 1)
        def _():
            dk = dk_acc[...]
            dkw_ref[...] = jnp.sum(dk * khat, axis=0, keepdims=True)
            dkv_ref[:, :MEM_WIDTH] = _mem_head_rms_bwd(dk * kw_ref[...], khat, kr, kmasks)
            dkv_ref[:, MEM_WIDTH:] = dv_acc[...]

    return pl.pallas_call(
        body, grid=(nsteps,),
        in_specs=[pl.BlockSpec((tt, MEM_WIDTH), lambda i: (i, qcol)), pl.BlockSpec((MEM_TOKENS, MEM_WIDTH), lambda i: (0, 0)),
                  pl.BlockSpec((MEM_TOKENS, MEM_WIDTH), lambda i: (0, 1)), _full((1, MEM_WIDTH)), _full((1, MEM_WIDTH)),
                  pl.BlockSpec((tt, MEM_WIDTH), lambda i: (i, ocol))],
        out_specs=[pl.BlockSpec((tt, MEM_WIDTH), lambda i: (i, 0)), _full((MEM_TOKENS, 2 * MEM_WIDTH)),
                   _full((1, MEM_WIDTH)), _full((1, MEM_WIDTH))],
        out_shape=[jax.ShapeDtypeStruct((T, MEM_WIDTH), BF16), jax.ShapeDtypeStruct((MEM_TOKENS, 2 * MEM_WIDTH), F32),
                   jax.ShapeDtypeStruct((1, MEM_WIDTH), F32), jax.ShapeDtypeStruct((1, MEM_WIDTH), F32)],
        scratch_shapes=[pltpu.VMEM((MEM_TOKENS, MEM_WIDTH), F32), pltpu.VMEM((MEM_TOKENS, MEM_WIDTH), F32)],
        compiler_params=_cp("arbitrary"), name=name)(proj, mkv, mkv, qn_w, kn_w, dmix)


HALF = HEAD_DIM // 2
ATT_SCALE = HEAD_DIM ** -0.5
NEG = -1e30


def _rope_tables(T):
    inv = np.float32(ROPE_THETA) ** (-np.arange(HALF, dtype=np.float32) / np.float32(HALF))
    ang = np.arange(T, dtype=np.float32)[:, None] * inv[None, :].astype(np.float32)
    cos, sin = np.cos(ang).astype(np.float32), np.sin(ang).astype(np.float32)
    return jnp.asarray(np.concatenate([cos, cos], axis=-1)), jnp.asarray(np.concatenate([-sin, sin], axis=-1))


def _rope(x, cosf, sinsg):
    return x * cosf + pltpu.roll(x, HALF, 1) * sinsg


def _rope_bwd(dy, cosf, sinsg):
    return dy * cosf + pltpu.roll(dy * sinsg, HALF, 1)


def _q_prep_bwd(proj, w_heads, cosf, sinsg, dqs, *, tt, name):
    T = proj.shape[0]
    W = N_GROUPS * B_WIDTH

    def body(x_ref, w_ref, c_ref, s_ref, d0, d1, d2, dx_ref, dw_ref):
        @pl.when(pl.program_id(0) == 0)
        def _():
            dw_ref[...] = jnp.zeros_like(dw_ref)

        c, s = c_ref[...], s_ref[...]
        for gi, d_ref in enumerate((d0, d1, d2)):
            for h in range(B_HEADS):
                sl = slice((gi * B_HEADS + h) * HEAD_DIM, (gi * B_HEADS + h + 1) * HEAD_DIM)
                xhat, r = _head_rms(x_ref[:, sl])
                dyn = _rope_bwd(d_ref[:, h * HEAD_DIM:(h + 1) * HEAD_DIM], c, s)
                dw_ref[:, sl] += jnp.sum(dyn * xhat, axis=0, keepdims=True)
                dx_ref[:, sl] = _head_rms_bwd(dyn * w_ref[:, sl], xhat, r).astype(BF16)

    tbl = pl.BlockSpec((tt, HEAD_DIM), lambda i: (i, 0))
    dyb = pl.BlockSpec((tt, B_WIDTH), lambda i: (i, 0))
    return pl.pallas_call(
        body, grid=(T // tt,),
        in_specs=[pl.BlockSpec((tt, W), lambda i: (i, 0)), _full((1, W)), tbl, tbl, dyb, dyb, dyb],
        out_specs=[pl.BlockSpec((tt, W), lambda i: (i, 0)), _full((1, W))],
        out_shape=[jax.ShapeDtypeStruct((T, W), BF16), jax.ShapeDtypeStruct((1, W), F32)],
        compiler_params=_cp("arbitrary"), name=name)(proj, w_heads, cosf, sinsg, *dqs)


def _kv_prep_bwd(kv, w_heads, cosf, sinsg, dks, dvs, *, tt, name):
    T = kv.shape[0]

    def body(x_ref, w_ref, c_ref, s_ref, k0, k1, k2, v0, v1, v2, dx_ref, dw_ref):
        @pl.when(pl.program_id(0) == 0)
        def _():
            dw_ref[...] = jnp.zeros_like(dw_ref)

        c, s = c_ref[...], s_ref[...]
        for h in range(B_HEADS):
            sl = slice(h * HEAD_DIM, (h + 1) * HEAD_DIM)
            vs = slice(B_WIDTH + h * HEAD_DIM, B_WIDTH + (h + 1) * HEAD_DIM)
            xhat, r = _head_rms(x_ref[:, sl])
            dyn = _rope_bwd(k0[:, sl] + k1[:, sl] + k2[:, sl], c, s)
            dw_ref[:, sl] += jnp.sum(dyn * xhat, axis=0, keepdims=True)
            dx_ref[:, sl] = _head_rms_bwd(dyn * w_ref[:, sl], xhat, r).astype(BF16)
            dx_ref[:, vs] = (v0[:, sl] + v1[:, sl] + v2[:, sl]).astype(BF16)

    tbl = pl.BlockSpec((tt, HEAD_DIM), lambda i: (i, 0))
    dyb = pl.BlockSpec((tt, B_WIDTH), lambda i: (i, 0))
    return pl.pallas_call(
        body, grid=(T // tt,),
        in_specs=[dyb, _full((1, B_WIDTH)), tbl, tbl] + [dyb] * 6,
        out_specs=[pl.BlockSpec((tt, 2 * B_WIDTH), lambda i: (i, 0)), _full((1, B_WIDTH))],
        out_shape=[jax.ShapeDtypeStruct((T, 2 * B_WIDTH), BF16), jax.ShapeDtypeStruct((1, B_WIDTH), F32)],
        compiler_params=_cp("arbitrary"), name=name)(kv, w_heads, cosf, sinsg, *dks, *dvs)


def _band_masks(n_is_first=None):
    row = lax.broadcasted_iota(jnp.int32, (SPAN, SPAN), 0)
    col = lax.broadcasted_iota(jnp.int32, (SPAN, SPAN), 1)
    return row >= col, col >= row


def _dil_views(T, d):
    L = T // d
    return L, L // SPAN


def _dil_fwd(qr, kr, kv, gi, d, *, name):
    T = qr.shape[0]
    L, nb = _dil_views(T, d)

    def body(q_ref, kc_ref, kp_ref, vc_ref, vp_ref, o_ref, lse_ref):
        cur_ok, prev_band = _band_masks()
        prev_ok = prev_band & (pl.program_id(1) > 0)
        for h in range(B_HEADS):
            sl = slice(h * HEAD_DIM, (h + 1) * HEAD_DIM)
            q = q_ref[:, sl]
            sc = jnp.where(cur_ok, _dot_nt(q, kc_ref[:, sl]) * ATT_SCALE, NEG)
            sp = jnp.where(prev_ok, _dot_nt(q, kp_ref[:, sl]) * ATT_SCALE, NEG)
            m = jnp.maximum(jnp.max(sc, axis=-1, keepdims=True), jnp.max(sp, axis=-1, keepdims=True))
            pc = jnp.exp(sc - m)
            pp = jnp.exp(sp - m)
            l = jnp.sum(pc, axis=-1, keepdims=True) + jnp.sum(pp, axis=-1, keepdims=True)
            o_ref[:, sl] = (_dot(pc, vc_ref[:, sl]) + _dot(pp, vp_ref[:, sl])) / l
            lse_ref[:, sl] = jnp.broadcast_to(m + jnp.log(l), (SPAN, HEAD_DIM))

    blk = lambda f: pl.BlockSpec((SPAN, B_WIDTH), f)
    cur = lambda r, n: (n, r)
    prev = lambda r, n: (jnp.maximum(n - 1, 0), r)
    ov = jax.ShapeDtypeStruct((L, d * B_WIDTH), F32)
    o, lse = pl.pallas_call(
        body, grid=(d, nb),
        in_specs=[blk(lambda r, n: (n, r * N_GROUPS + gi)), blk(cur), blk(prev),
                  blk(lambda r, n: (n, 2 * r + 1)), blk(lambda r, n: (jnp.maximum(n - 1, 0), 2 * r + 1))],
        out_specs=[blk(cur), blk(cur)], out_shape=[ov, ov],
        compiler_params=_cp("parallel", "arbitrary"), name=name,
    )(qr.reshape(L, d * N_GROUPS * B_WIDTH), kr.reshape(L, d * B_WIDTH), kr.reshape(L, d * B_WIDTH),
      kv.reshape(L, d * 2 * B_WIDTH), kv.reshape(L, d * 2 * B_WIDTH))
    return o.reshape(T, B_WIDTH), lse.reshape(T, B_WIDTH)


def _dil_combine_fwd(os_, lses, *, tt, name):
    T = os_[0].shape[0]

    def body(o0, o1, o2, l0, l1, l2, y_ref, lse_ref):
        a, b, c = l0[...], l1[...], l2[...]
        m = jnp.maximum(jnp.maximum(a, b), c)
        wa, wb, wc = jnp.exp(a - m), jnp.exp(b - m), jnp.exp(c - m)
        den = wa + wb + wc
        y_ref[...] = (wa * o0[...] + wb * o1[...] + wc * o2[...]) / den
        lse_ref[...] = m + jnp.log(den)

    blk = pl.BlockSpec((tt, B_WIDTH), lambda i: (i, 0))
    sh = jax.ShapeDtypeStruct((T, B_WIDTH), F32)
    return pl.pallas_call(
        body, grid=(T // tt,), in_specs=[blk] * 6, out_specs=[blk, blk], out_shape=[sh, sh],
        compiler_params=_cp("parallel"), name=name)(*os_, *lses)


DILS_UNROLL = 8


def _dils_specs(gi, d, nblk):
    blk = lambda f: pl.BlockSpec((SPAN * d, HEAD_DIM), f)
    return {
        "q": blk(lambda h, n: (n, gi * B_HEADS + h)), "q_next": blk(lambda h, n: (jnp.minimum(n + 1, nblk - 1), gi * B_HEADS + h)),
        "cur": blk(lambda h, n: (n, h)), "prev": blk(lambda h, n: (jnp.maximum(n - 1, 0), h)),
        "next": blk(lambda h, n: (jnp.minimum(n + 1, nblk - 1), h)),
        "v": blk(lambda h, n: (n, B_HEADS + h)), "v_prev": blk(lambda h, n: (jnp.maximum(n - 1, 0), B_HEADS + h)),
    }


def _dils_fwd(qr, kr, kv, gi, d, *, name):
    T = qr.shape[0]
    nblk = T // (SPAN * d)
    sp = _dils_specs(gi, d, nblk)

    def body(q_ref, kc_ref, vc_ref, o_ref, lse_ref, k_before, v_before):
        @pl.when(pl.program_id(1) == 0)
        def _():
            k_before[...] = jnp.zeros_like(k_before)
            v_before[...] = jnp.zeros_like(v_before)

        cur_ok, prev_band = _band_masks()
        prev_ok = prev_band & (pl.program_id(1) > 0)

        def residue(r, carry):
            rows = pl.ds(r, SPAN, stride=d)
            q, kc, vc = q_ref[rows, :], kc_ref[rows, :].astype(BF16), vc_ref[rows, :].astype(BF16)
            sc = jnp.where(cur_ok, _dot_nt(q, kc) * ATT_SCALE, NEG)
            sp_ = jnp.where(prev_ok, _dot_nt(q, k_before[r]) * ATT_SCALE, NEG)
            m = jnp.maximum(jnp.max(sc, axis=-1, keepdims=True), jnp.max(sp_, axis=-1, keepdims=True))
            pc = jnp.exp(sc - m)
            pp = jnp.exp(sp_ - m)
            l = jnp.sum(pc, axis=-1, keepdims=True) + jnp.sum(pp, axis=-1, keepdims=True)
            o_ref[rows, :] = (_dot(pc, vc) + _dot(pp, v_before[r])) / l
            lse_ref[rows, :] = jnp.broadcast_to(m + jnp.log(l), (SPAN, HEAD_DIM))
            k_before[r] = kc
            v_before[r] = vc
            return carry

        lax.fori_loop(0, d, residue, 0, unroll=min(d, DILS_UNROLL))

    sh = jax.ShapeDtypeStruct((T, B_WIDTH), F32)
    return pl.pallas_call(
        body, grid=(B_HEADS, nblk), in_specs=[sp["q"], sp["cur"], sp["v"]],
        out_specs=[sp["cur"], sp["cur"]], out_shape=[sh, sh],
        scratch_shapes=[pltpu.VMEM((d, SPAN, HEAD_DIM), BF16), pltpu.VMEM((d, SPAN, HEAD_DIM), BF16)],
        compiler_params=_cp("parallel", "arbitrary"), name=name)(qr, kr, kv)


DIL_BWD_GROUP = {1: 4, 4: 1, 16: 1}


def _dil_bwd(qr, kr, kv, dmix, lse, dd, gi, d, *, name, dep=None):
    T = qr.shape[0]
    G = DIL_BWD_GROUP[d]
    band = SPAN * d
    tb = G * band
    nblk = T // tb
    n_units = T // SPAN

    keep = G == 1

    def kernel_body(q_ref, dy_ref, lse_ref, dd_ref, kc_ref, vc_ref, *rest):
        if keep:
            dq_ref, dk_ref, dv_ref, dk_acc, dv_acc, k_before, v_before = rest
        else:
            kp_ref, vp_ref, dq_ref, dk_ref, dv_ref, dk_acc, dv_acc = rest
        n = pl.program_id(1)

        @pl.when(n == 0)
        def _():
            dk_acc[...] = jnp.zeros_like(dk_acc)
            dv_acc[...] = jnp.zeros_like(dv_acc)
            if keep:
                k_before[...] = jnp.zeros_like(k_before)
                v_before[...] = jnp.zeros_like(v_before)

        cur_ok, prev_band = _band_masks()
        for j in range(G):
            def residue(r, carry, j=j):
                off = j * band + r
                rows = pl.ds(off, SPAN, stride=d)
                q, dy = q_ref[rows, :], dy_ref[rows, :]
                lse_h = jnp.max(lse_ref[rows, :], axis=-1, keepdims=True)
                dd_h = jnp.max(dd_ref[rows, :], axis=-1, keepdims=True)
                kc, vc = kc_ref[rows, :].astype(BF16), vc_ref[rows, :].astype(BF16)
                if j > 0:
                    before = pl.ds(off - band, SPAN, stride=d)
                    kp, vp = kc_ref[before, :], vc_ref[before, :]
                    prev_ok = prev_band
                elif keep:
                    kp, vp = k_before[r], v_before[r]
                    k_before[r] = kc
                    v_before[r] = vc
                    prev_ok = prev_band & (n > 0)
                else:
                    before = pl.ds((G - 1) * band + r, SPAN, stride=d)
                    kp, vp = kp_ref[before, :], vp_ref[before, :]
                    prev_ok = prev_band & (n > 0)
                pc = jnp.exp(jnp.where(cur_ok, _dot_nt(q, kc) * ATT_SCALE, NEG) - lse_h)
                pp = jnp.exp(jnp.where(prev_ok, _dot_nt(q, kp) * ATT_SCALE, NEG) - lse_h)
                dsc = pc * (_dot_nt(dy, vc) - dd_h) * ATT_SCALE
                dsp = pp * (_dot_nt(dy, vp) - dd_h) * ATT_SCALE
                dq_ref[rows, :] = _dot(dsc, kc) + _dot(dsp, kp)
                u = (n * G + j) * d + r
                here = pl.ds(pl.multiple_of(u * SPAN, SPAN), SPAN)
                dk_acc[here, :] += _dot_tn(dsc, q)
                dv_acc[here, :] += _dot_tn(pc, dy)
                there = pl.ds(pl.multiple_of(jnp.maximum(u - d, 0) * SPAN, SPAN), SPAN)
                dk_acc[there, :] += _dot_tn(dsp, q)
                dv_acc[there, :] += _dot_tn(pp, dy)
                return carry

            lax.fori_loop(0, d, residue, 0, unroll=min(d, DILS_UNROLL))

        @pl.when(n == nblk - 1)
        def _():
            def place(u, carry):
                rows = pl.ds((u // d) * band + u % d, SPAN, stride=d)
                src = pl.ds(pl.multiple_of(u * SPAN, SPAN), SPAN)
                dk_ref[rows, :] = dk_acc[src, :]
                dv_ref[rows, :] = dv_acc[src, :]
                return carry

            lax.fori_loop(0, n_units, place, 0)

    blk = lambda f: pl.BlockSpec((tb, HEAD_DIM), f)
    cur = lambda h, n: (n, h)
    prev = lambda h, n: (jnp.maximum(n - 1, 0), h)
    whole = pl.BlockSpec((T, HEAD_DIM), lambda h, n: (0, h))
    sh = jax.ShapeDtypeStruct((T, B_WIDTH), F32)
    v_cur = blk(lambda h, n: (n, B_HEADS + h))
    if keep:
        kv_specs, kv_args = [blk(cur), v_cur], [kr, kv]
        kept = [pltpu.VMEM((d, SPAN, HEAD_DIM), BF16), pltpu.VMEM((d, SPAN, HEAD_DIM), BF16)]
    else:
        kv_specs = [blk(cur), v_cur, blk(prev), blk(lambda h, n: (jnp.maximum(n - 1, 0), B_HEADS + h))]
        kv_args, kept = [kr, kv, kr, kv], []
    body, dep_specs, dep_args = _dep(kernel_body, 4 + len(kv_args), dep)
    return pl.pallas_call(
        body, grid=(B_HEADS, nblk),
        in_specs=[blk(lambda h, n: (n, gi * B_HEADS + h)), blk(cur), blk(cur), blk(cur)] + kv_specs + dep_specs,
        out_specs=[blk(cur), whole, whole], out_shape=[sh, sh, sh],
        scratch_shapes=[pltpu.VMEM((T, HEAD_DIM), F32), pltpu.VMEM((T, HEAD_DIM), F32)] + kept,
        compiler_params=_cp("parallel", "arbitrary"), name=name)(qr, dmix, lse, dd, *kv_args, *dep_args)


A_MQ_COL = 4 * A_WIDTH // MEM_WIDTH
B_MQ_COL = N_GROUPS * B_WIDTH // MEM_WIDTH


def _row(v):
    return v.reshape(1, -1).astype(F32)


def _local_step(x, mem, tgt, get_w, P, put_g, first_dep=None, forward_point=lambda i, value: value):
    T = x.shape[0]
    cosf, sinsg = _rope_tables(T)
    lb_soft = jax.nn.softmax(P["a_lb_logits"].astype(F32), axis=0)
    lb = lb_soft[0:1]
    qw_heads = jnp.repeat(P["b_qnorm"][0], B_HEADS, axis=0).reshape(1, -1)
    kw_heads = jnp.tile(_row(P["b_knorm"]), (1, B_HEADS))
    mqw = [jnp.tile(_row(P["mem_qnorm"][l]), (1, MEM_HEADS)) for l in range(2)]
    mkw = [jnp.tile(_row(P["mem_knorm"][l]), (1, MEM_HEADS)) for l in range(2)]
    nmix = [_row(P["norm_mix"][l]) for l in range(2)]
    nffn = [_row(P["norm_ffn"][l]) for l in range(2)]
    mnorm = [_row(P["mem_norm"][l]) for l in range(2)]
    kvn = _row(P["kv_norm"])
    onorm = _row(P["a_onorm"])
    W = {}

    def w_of(name, after=None):
        if name not in W:
            W[name] = get_w(name, after)
        return W[name]

    proj_a, xn0 = _rms_matmul(x, nmix[0], w_of("a_w_in"), tt=512, tn=1664, wt=True, name="proj_a", dep=first_dep)
    mkv0, mn0 = _rms_matmul(mem, mnorm[0], w_of("w_mem_kv0"), tt=MEM_TOKENS, tn=2 * MEM_WIDTH, wt=False, name="mem_kv0")
    o_raw, st = _hgrn2_fwd(proj_a, lb, name="hgrn2_fwd")
    o_raw = forward_point(0, o_raw)
    mm0 = _a_post_fwd(o_raw, proj_a, onorm, tt=512, name="a_post_fwd")
    mo0 = _mem_attn_fwd(proj_a, A_MQ_COL, mkv0, mqw[0], mkw[0], tt=1024, name="mem_attn_fwd0")
    hm0 = _mm_res(x, mm0, mo0, w_of("w_out0", mo0), tt=512, name="out_proj0")
    hm0 = forward_point(1, hm0)
    gu0, hn0 = _rms_matmul(hm0, nffn[0], w_of("w_gate_up0", hm0), tt=512, tn=1408, wt=True, out_dtype=BF16, name="gate_up0")
    h1 = _swiglu_down(hm0, gu0, w_of("w_down0", gu0), tt=512, name="down0")
    h1 = forward_point(2, h1)
    kv, hkn, kr = _rms_matmul(h1, kvn, w_of("w_kv", h1), tt=512, tn=768, wt=True, name="kv_proj",
                              rotate=(kw_heads, cosf, sinsg))

    proj_b, xn1, qr = _rms_matmul(h1, nmix[1], w_of("b_w_in", kr), tt=512, tn=1280, wt=True, name="proj_b",
                                  rotate=(qw_heads, cosf, sinsg))
    proj_b = forward_point(3, proj_b)
    mkv1, mn1 = _rms_matmul(mem, mnorm[1], w_of("w_mem_kv1", kr), tt=MEM_TOKENS, tn=2 * MEM_WIDTH, wt=False, name="mem_kv1")
    outs = [(_dil_fwd if d == 1 else _dils_fwd)(qr, kr, kv, gi, d, name=f"dil_fwd{gi}") for gi, d in enumerate(DILATIONS)]
    mm1, lse_tot = _dil_combine_fwd([o for o, _ in outs], [s for _, s in outs], tt=512, name="dil_combine")
    mo1 = _mem_attn_fwd(proj_b, B_MQ_COL, mkv1, mqw[1], mkw[1], tt=1024, name="mem_attn_fwd1")
    hm1 = _mm_res(h1, mm1, mo1, w_of("w_out1", mo1), tt=512, name="out_proj1")
    gu1, hn1 = _rms_matmul(hm1, nffn[1], w_of("w_gate_up1", hm1), tt=512, tn=1408, wt=True, out_dtype=BF16, name="gate_up1")
    dy, sq = _swiglu_down_loss(hm1, gu1, w_of("w_down1", gu1), tgt, tt=512, name="down1_loss")

    gP = {}
    zeros_mem = jnp.zeros((MEM_TOKENS, D_MODEL), F32)

    def ffn_bwd(l, dh, hm, gu, hn):
        dgu, g_wd = _swiglu_bwd(dh, gu, w_of(f"w_down{l}"), tt=256, name=f"swiglu_bwd{l}")
        g_wgu = _mm_tn(dgu, hn, tt=512, tka=1408, name=f"g_w_gate_up{l}")
        sent = put_g({f"w_down{l}": g_wd, f"w_gate_up{l}": g_wgu})
        dhm, g_nf = _rms_bwd_dx(hm, nffn[l], w_of(f"w_gate_up{l}"), dgu, dh, tt=512, wt=True, name=f"gate_up_bwd{l}", dep=sent)
        return dhm, g_nf

    def mix_bwd(l, dhm, mix_main, mix_mem, proj, qcol, mkv, mn):
        dmix, g_wout, *head_dots = _out_proj_bwd(dhm, mix_main, mix_mem, w_of(f"w_out{l}"), tt=512, name=f"out_proj_bwd{l}",
                                                 head_dots=l == 1)
        dmq, dmkv, dqw, dkw = _mem_attn_bwd(proj, qcol, mkv, mqw[l], mkw[l], dmix, tt=1024, name=f"mem_attn_bwd{l}")
        g_wmkv = _mm_tn(mn, dmkv, tt=MEM_TOKENS, tka=512, name=f"g_w_mem_kv{l}")
        sent = put_g({f"w_out{l}": g_wout, f"w_mem_kv{l}": g_wmkv})
        _, g_mn = _rms_bwd_dx(mem, mnorm[l], w_of(f"w_mem_kv{l}"), dmkv, zeros_mem, tt=MEM_TOKENS, wt=False, name=f"mem_kv_bwd{l}")
        fold = lambda v: v.reshape(MEM_HEADS, MEM_HEAD_DIM).sum(axis=0)
        return dmix, dmq, g_mn, fold(dqw), fold(dkw), sent, head_dots

    dhm1, g_nf1 = ffn_bwd(1, dy, hm1, gu1, hn1)
    dmix1, dmq1, g_mn1, g_mq1, g_mk1, sent, (dd,) = mix_bwd(1, dhm1, mm1, mo1, proj_b, B_MQ_COL, mkv1, mn1)
    dqs, dks, dvs = [], [], []
    for gi, d in enumerate(DILATIONS):
        dq_g, dk_g, dv_g = _dil_bwd(qr, kr, kv, dmix1, lse_tot, dd, gi, d, name=f"dil_bwd{gi}", dep=sent if gi == 0 else None)
        dqs.append(dq_g)
        dks.append(dk_g)
        dvs.append(dv_g)
    dq_raw, dqw = _q_prep_bwd(proj_b, qw_heads, cosf, sinsg, dqs, tt=512, name="q_prep_bwd")
    dkv, dkw = _kv_prep_bwd(kv, kw_heads, cosf, sinsg, dks, dvs, tt=512, name="kv_prep_bwd")
    dproj_b = [dq_raw, dmq1]
    g_wb = _mm_tn_pieces(dproj_b, xn1, tt=512, name="g_b_w_in")
    g_wkv = _mm_tn(dkv, hkn, tt=512, tka=768, name="g_w_kv")
    sent = put_g({"b_w_in": g_wb, "w_kv": g_wkv})
    dh1, g_nm1 = _rms_bwd_dx(h1, nmix[1], w_of("b_w_in"), dproj_b, dhm1, tt=512, wt=True, name="proj_b_bwd", dep=sent)
    dh1, g_kvn = _rms_bwd_dx(h1, kvn, w_of("w_kv"), dkv, dh1, tt=512, wt=True, name="kv_proj_bwd")

    dhm0, g_nf0 = ffn_bwd(0, dh1, hm0, gu0, hn0)
    dmix0, dmq0, g_mn0, g_mq0, g_mk0, sent, _ = mix_bwd(0, dhm0, mm0, mo0, proj_a, A_MQ_COL, mkv0, mn0)
    do_raw, dg, g_onorm = _a_post_bwd(o_raw, proj_a, onorm, dmix0, tt=512, name="a_post_bwd", dep=sent)
    dq, dz, dv, dlb = _hgrn2_bwd(proj_a, lb, st, do_raw, name="hgrn2_bwd")
    dproj_a = [dq, dz, dv, dg, dmq0]
    sent = put_g({"a_w_in": _mm_tn_pieces(dproj_a, xn0, tt=512, name="g_a_w_in")})
    gx, g_nm0 = _rms_bwd_dx(x, nmix[0], w_of("a_w_in"), dproj_a, dhm0, tt=512, wt=True, name="proj_a_bwd", dep=sent)

    dl0 = lb_soft[0:1] * lb_soft[1:2] * dlb
    gP["a_lb_logits"] = jnp.concatenate([dl0, -dl0], axis=0)
    gP["a_onorm"] = g_onorm
    gP["norm_mix"] = jnp.concatenate([g_nm0, g_nm1], axis=0)
    gP["norm_ffn"] = jnp.concatenate([g_nf0, g_nf1], axis=0)
    gP["b_qnorm"] = dqw.reshape(N_GROUPS, B_HEADS, HEAD_DIM).sum(axis=1)[None]
    gP["kv_norm"] = g_kvn.reshape(-1)
    gP["b_knorm"] = dkw.reshape(B_HEADS, HEAD_DIM).sum(axis=0)
    gP["mem_norm"] = jnp.concatenate([g_mn0, g_mn1], axis=0)
    gP["mem_qnorm"] = jnp.stack([g_mq0, g_mq1])
    gP["mem_knorm"] = jnp.stack([g_mk0, g_mk1])
    return sq, gx, gP


MESH_ID = pl.DeviceIdType.MESH
HBM_SPEC = pl.BlockSpec(memory_space=pltpu.HBM)


def _position():
    return lax.axis_index("x"), lax.axis_index("y"), lax.axis_index("c")


def _all_gather_direct(block, after, *, name):
    def body(x_ref, after_ref, out_ref, send_sems, recv_sems, local_sem):
        x, y, c = _position()
        me = 4 * x + 2 * y + c
        mine = pltpu.make_async_copy(x_ref, out_ref.at[me], local_sem)
        mine.start()
        copies = []
        for k in ALL_PEERS:
            cp = pltpu.make_async_remote_copy(
                src_ref=x_ref, dst_ref=out_ref.at[me], send_sem=send_sems.at[k - 1], recv_sem=recv_sems.at[k - 1],
                device_id=_peer(k, x, y, c), device_id_type=MESH_ID)
            cp.start()
            copies.append(cp)
        for cp in copies:
            cp.wait()
        mine.wait()

    return pl.pallas_call(
        body, out_shape=jax.ShapeDtypeStruct((N_DEV,) + block.shape, block.dtype),
        in_specs=[HBM_SPEC, pl.BlockSpec(memory_space=pl.ANY)], out_specs=HBM_SPEC,
        scratch_shapes=[pltpu.SemaphoreType.DMA((7,)), pltpu.SemaphoreType.DMA((7,)), pltpu.SemaphoreType.DMA],
        name=name)(block, after)


SEM_SPEC = pl.BlockSpec(memory_space=pltpu.SEMAPHORE)
ANY_SPEC = pl.BlockSpec(memory_space=pl.ANY)
DATAFLOW = pltpu.SideEffectType.DATAFLOW_SIDE_EFFECTING


def _peer(k, x, y, c):
    return (1 - x if (k >> 2) & 1 else x, 1 - y if (k >> 1) & 1 else y, 1 - c if k & 1 else c)


def _own_slot_filled(own_block):
    x, y, c = _position()
    zone = lax.empty((N_DEV,) + own_block.shape, own_block.dtype)
    return lax.dynamic_update_slice_in_dim(zone, own_block[None], 4 * x + 2 * y + c, axis=0)


ALL_PEERS = tuple(range(1, N_DEV))
SIBLING_AND_SAME_CORE = (1, 2, 4, 6)
SAME_CORE = (2, 4, 6)


def _split_start(srcs, scatter, after, *, name, relations=ALL_PEERS, carried=None):
    n = len(srcs)
    extra = ([] if after is None else [after]) + ([] if carried is None else [carried])
    n_carried = 0 if carried is None else 1
    x, y, c = _position()
    me = 4 * x + 2 * y + c
    lands = [_own_slot_filled(lax.dynamic_index_in_dim(s, me, 0, keepdims=False) if scatter else s) for s in srcs]

    def body(*refs):
        src_refs, land_refs = refs[:n], refs[n:2 * n]
        send_sems, recv_sems = refs[2 * n + len(extra)], refs[2 * n + len(extra) + 1]
        token = refs[2 * n + len(extra) + 2 + 2 * n]
        bx, by, bc = _position()
        bme = 4 * bx + 2 * by + bc
        for a in range(n):
            for k in relations:
                tx, ty, tc = _peer(k, bx, by, bc)
                src = src_refs[a].at[4 * tx + 2 * ty + tc] if scatter else src_refs[a]
                pltpu.make_async_remote_copy(
                    src_ref=src, dst_ref=land_refs[a].at[bme],
                    send_sem=send_sems.at[7 * a + k - 1], recv_sem=recv_sems.at[7 * a + k - 1],
                    device_id=(tx, ty, tc), device_id_type=MESH_ID).start()
        token[...] = jnp.zeros_like(token)

    hbm = lambda a: pltpu.HBM(a.shape, a.dtype)
    outs = pl.pallas_call(
        body, name=name,
        out_shape=(pltpu.SemaphoreType.DMA((7 * n,)), pltpu.SemaphoreType.DMA((7 * n,)),
                   *[hbm(s) for s in srcs], *[hbm(l) for l in lands], jax.ShapeDtypeStruct((8, 128), F32),
                   *([hbm(carried)] if n_carried else [])),
        in_specs=[HBM_SPEC] * (2 * n) + [ANY_SPEC] * len(extra),
        out_specs=(SEM_SPEC, SEM_SPEC, *[HBM_SPEC] * (2 * n), pl.BlockSpec(memory_space=pltpu.VMEM), *([ANY_SPEC] * n_carried)),
        input_output_aliases={**{i: 2 + i for i in range(2 * n)},
                              **({2 * n + len(extra) - 1: 2 * n + 3} if n_carried else {})},
        compiler_params=pltpu.CompilerParams(has_side_effects=DATAFLOW),
    )(*[pltpu.with_memory_space_constraint(s, pltpu.HBM) for s in srcs],
      *[pltpu.with_memory_space_constraint(l, pltpu.HBM) for l in lands], *extra)
    return {"n": n, "relations": relations, "send": outs[0], "recv": outs[1], "srcs": list(outs[2:2 + n]),
            "lands": list(outs[2 + n:2 + 2 * n]), "token": outs[2 * n + 2], "carried": outs[-1] if n_carried else None}


def _forward_start(lands, carried, *, name):
    n = len(lands)

    def body(*refs):
        land_refs = refs[:n]
        send_sems, recv_sems = refs[n + 1], refs[n + 2]
        bx, by, bc = _position()
        for a in range(n):
            for k in SAME_CORE:
                tx, ty, tc = _peer(k, bx, by, bc)
                block = land_refs[a].at[4 * tx + 2 * ty + tc]
                pltpu.make_async_remote_copy(
                    src_ref=block, dst_ref=block,
                    send_sem=send_sems.at[7 * a + k - 1], recv_sem=recv_sems.at[7 * a + k - 1],
                    device_id=(bx, by, 1 - bc), device_id_type=MESH_ID).start()

    hbm = lambda a: pltpu.HBM(a.shape, a.dtype)
    outs = pl.pallas_call(
        body, name=name,
        out_shape=(pltpu.SemaphoreType.DMA((7 * n,)), pltpu.SemaphoreType.DMA((7 * n,)),
                   *[hbm(l) for l in lands], hbm(carried)),
        in_specs=[HBM_SPEC] * n + [ANY_SPEC],
        out_specs=(SEM_SPEC, SEM_SPEC, *[HBM_SPEC] * n, ANY_SPEC),
        input_output_aliases={i: 2 + i for i in range(n + 1)},
        compiler_params=pltpu.CompilerParams(has_side_effects=DATAFLOW),
    )(*lands, carried)
    handle = {"n": n, "relations": SAME_CORE, "send": outs[0], "recv": outs[1], "srcs": [], "lands": list(outs[2:2 + n])}
    return handle, outs[-1]


def _split_wait(handle, after, *, name):
    n, ns = handle["n"], len(handle["srcs"])

    def body(*refs):
        land_refs = refs[ns:ns + n]
        send_sems, recv_sems = refs[ns + n], refs[ns + n + 1]
        bx, by, bc = _position()
        for a in range(n):
            for k in handle["relations"]:
                block = land_refs[a].at[0]
                cp = pltpu.make_async_remote_copy(
                    src_ref=block, dst_ref=block,
                    send_sem=send_sems.at[7 * a + k - 1], recv_sem=recv_sems.at[7 * a + k - 1],
                    device_id=_peer(k, bx, by, bc), device_id_type=MESH_ID)
                cp.wait_send()
                cp.wait_recv()

    hbm = lambda a: pltpu.HBM(a.shape, a.dtype)
    outs = pl.pallas_call(
        body, name=name,
        out_shape=(*[hbm(s) for s in handle["srcs"]], *[hbm(l) for l in handle["lands"]]),
        in_specs=[HBM_SPEC] * (ns + n) + [SEM_SPEC, SEM_SPEC, ANY_SPEC],
        out_specs=tuple([HBM_SPEC] * (ns + n)),
        input_output_aliases={i: i for i in range(ns + n)},
        compiler_params=pltpu.CompilerParams(has_side_effects=DATAFLOW),
    )(*handle["srcs"], *handle["lands"], handle["send"], handle["recv"], after)
    return list(outs[ns:])


def _sum_sources(parts, *, tr, name):
    n, R, C = parts.shape

    def body(p_ref, o_ref):
        acc = p_ref[0].astype(F32)
        for s in range(1, n):
            acc = acc + p_ref[s].astype(F32)
        o_ref[...] = acc

    return pl.pallas_call(
        body, grid=(R // tr,), in_specs=[pl.BlockSpec((n, tr, C), lambda i: (0, i, 0))],
        out_specs=pl.BlockSpec((tr, C), lambda i: (i, 0)),
        out_shape=jax.ShapeDtypeStruct((R, C), F32), compiler_params=_cp("parallel"), name=name)(parts)


def _adamw_math(g, w, m, v):
    c1 = 1.0 - ADAM_B1 ** ADAM_STEP
    c2 = 1.0 - ADAM_B2 ** ADAM_STEP
    nm = ADAM_B1 * m + (1.0 - ADAM_B1) * g
    nv = ADAM_B2 * v + (1.0 - ADAM_B2) * (g * g)
    return -ADAM_LR * ((nm / c1) / (jnp.sqrt(nv / c2) + ADAM_EPS) + ADAM_WD * w), nm, nv


def _reduce_adamw(received, w, m, v, *, tr, name):
    L, R, C = w.shape

    def body(*refs):
        p_refs = refs[:L]
        w_ref, m_ref, v_ref, g_ref, d_ref, nm_ref, nv_ref = refs[L:]
        for l in range(L):
            @pl.when(pl.program_id(0) == l)
            def _(p_ref=p_refs[l]):
                acc = p_ref[0].astype(F32)
                for s in range(1, N_DEV):
                    acc = acc + p_ref[s].astype(F32)
                g_ref[...] = acc
                d_ref[...], nm_ref[...], nv_ref[...] = _adamw_math(acc, w_ref[...], m_ref[...], v_ref[...])

    p_spec = pl.BlockSpec((N_DEV, tr, C), lambda l, i: (0, i, 0))
    blk = pl.BlockSpec((None, tr, C), lambda l, i: (l, i, 0))
    sh = jax.ShapeDtypeStruct((L, R, C), F32)
    return pl.pallas_call(
        body, grid=(L, R // tr), in_specs=[p_spec] * L + [blk] * 3, out_specs=[blk] * 4, out_shape=[sh] * 4,
        compiler_params=_cp("parallel", "parallel"), name=name)(*received, w, m, v)


def _adamw(g, w, m, v, *, tr, name):
    L, R, C = w.shape

    def body(g_ref, w_ref, m_ref, v_ref, d_ref, nm_ref, nv_ref):
        d_ref[...], nm_ref[...], nv_ref[...] = _adamw_math(g_ref[...], w_ref[...], m_ref[...], v_ref[...])

    blk = pl.BlockSpec((None, tr, C), lambda l, i: (l, i, 0))
    sh = jax.ShapeDtypeStruct((L, R, C), F32)
    return pl.pallas_call(
        body, grid=(L, R // tr), in_specs=[blk] * 4, out_specs=[blk] * 3, out_shape=[sh] * 3,
        compiler_params=_cp("parallel", "parallel"), name=name)(g, w, m, v)


UNITS = {
    "a_w_in": ("a_w_in", 0, True), "w_mem_kv0": ("w_mem_kv", 0, False), "w_out0": ("w_out", 0, False),
    "w_gate_up0": ("w_gate_up", 0, True), "w_down0": ("w_down", 0, False), "w_kv": ("w_kv", None, True),
    "b_w_in": ("b_w_in", 0, True), "w_mem_kv1": ("w_mem_kv", 1, False), "w_out1": ("w_out", 1, False),
    "w_gate_up1": ("w_gate_up", 1, True), "w_down1": ("w_down", 1, False),
}
BIG = ("a_w_in", "b_w_in", "w_kv", "w_mem_kv", "w_out", "w_gate_up", "w_down")
ADAMW_ROW_TILE = {"a_w_in": 208, "b_w_in": 160, "w_kv": 192, "w_mem_kv": 128, "w_out": 128, "w_gate_up": 352, "w_down": 352}


def _wire_block(weights, unit):
    name, layer, col = UNITS[unit]
    a = weights[name] if layer is None else weights[name][layer]
    return (a.T if col else a).astype(BF16)


SMALL_REPLICATED = ("norm_mix", "norm_ffn", "b_qnorm", "kv_norm", "b_knorm", "mem_norm", "mem_qnorm", "mem_knorm")
SMALL_SHARDED = ("a_lb_logits", "a_onorm")
SMALL_ORDER = SMALL_REPLICATED + SMALL_SHARDED
LANES = 128


def _prod(shape):
    n = 1
    for s in shape:
        n *= s
    return n


def _pack_flat(arrays, rows, cols, dtype):
    flat = jnp.concatenate([a.reshape(-1).astype(dtype) for a in arrays])
    return jnp.pad(flat, (0, rows * cols - flat.shape[0])).reshape(rows, cols)


def _unpack_flat(packed, shapes):
    flat = packed.reshape(-1)
    out, off = [], 0
    for s in shapes:
        out.append(flat[off:off + _prod(s)].reshape(s))
        off += _prod(s)
    return out


def kernel(x, mem, norm_mix, norm_ffn, a_w_in, a_lb_logits, a_onorm, b_w_in, b_qnorm, kv_norm, w_kv, b_knorm, mem_norm, w_mem_kv, mem_qnorm, mem_knorm, w_out, w_gate_up, w_down, loss_target, m_norm_mix, m_norm_ffn, m_a_w_in, m_a_lb_logits, m_a_onorm, m_b_w_in, m_b_qnorm, m_kv_norm, m_w_kv, m_b_knorm, m_mem_norm, m_w_mem_kv, m_mem_qnorm, m_mem_knorm, m_w_out, m_w_gate_up, m_w_down, v_norm_mix, v_norm_ffn, v_a_w_in, v_a_lb_logits, v_a_onorm, v_b_w_in, v_b_qnorm, v_kv_norm, v_w_kv, v_b_knorm, v_mem_norm, v_w_mem_kv, v_mem_qnorm, v_mem_knorm, v_w_out, v_w_gate_up, v_w_down):
    names = ("norm_mix", "norm_ffn", "a_w_in", "a_lb_logits", "a_onorm", "b_w_in", "b_qnorm", "kv_norm", "w_kv", "b_knorm",
             "mem_norm", "w_mem_kv", "mem_qnorm", "mem_knorm", "w_out", "w_gate_up", "w_down")
    w = dict(zip(names, (norm_mix, norm_ffn, a_w_in, a_lb_logits, a_onorm, b_w_in, b_qnorm, kv_norm, w_kv, b_knorm,
                         mem_norm, w_mem_kv, mem_qnorm, mem_knorm, w_out, w_gate_up, w_down)))
    m = dict(zip(names, (m_norm_mix, m_norm_ffn, m_a_w_in, m_a_lb_logits, m_a_onorm, m_b_w_in, m_b_qnorm, m_kv_norm, m_w_kv,
                         m_b_knorm, m_mem_norm, m_w_mem_kv, m_mem_qnorm, m_mem_knorm, m_w_out, m_w_gate_up, m_w_down)))
    v = dict(zip(names, (v_norm_mix, v_norm_ffn, v_a_w_in, v_a_lb_logits, v_a_onorm, v_b_w_in, v_b_qnorm, v_kv_norm, v_w_kv,
                         v_b_knorm, v_mem_norm, v_w_mem_kv, v_mem_qnorm, v_mem_knorm, v_w_out, v_w_gate_up, v_w_down)))

    first = ["a_w_in", "w_mem_kv0"]
    later = [["w_out0", "w_gate_up0"], ["w_down0", "w_kv"], ["b_w_in", "w_mem_kv1"], ["w_out1", "w_gate_up1", "w_down1"]]
    first_half, second_half = {}, {}

    def start_first_half(i, after, carried=None):
        first_half[i] = _split_start([_wire_block(w, u) for u in later[i]], False, after, name=f"gather{i}_start",
                                     relations=SIBLING_AND_SAME_CORE, carried=carried)
        return first_half[i]

    opening = _split_start([_wire_block(w, u) for u in first] + [_pack_flat([a_lb_logits, a_onorm], 8, LANES, F32)],
                           False, None, name="gather_first_start", relations=SIBLING_AND_SAME_CORE)
    token = start_first_half(0, opening["token"])["token"]
    token = start_first_half(1, token)["token"]
    opening, token = _forward_start(_split_wait(opening, token, name="gather_first_landed"), token, name="gather_first_forward")
    gathered = _split_wait(opening, token, name="gather_first_wait")
    full = {u: g.reshape(-1, g.shape[-1]) for u, g in zip(first, gathered)}
    small_in = gathered[-1].reshape(N_DEV, -1)
    P = {n: w[n] for n in SMALL_REPLICATED}
    P["a_lb_logits"] = small_in[:, :192].reshape(N_DEV, 2, 96).transpose(1, 0, 2).reshape(2, A_WIDTH)
    P["a_onorm"] = small_in[:, 192:288].reshape(1, A_WIDTH)

    def forward_point(i, value):
        landed = _split_wait(first_half[i], value, name=f"gather{i}_landed")
        second_half[i], value = _forward_start(landed, value, name=f"gather{i}_forward")
        if i + 2 < len(later):
            value = start_first_half(i + 2, None, carried=value)["carried"]
        return value

    def get_w(unit, after):
        if unit not in full:
            i = [unit in group for group in later].index(True)
            for u, land in zip(later[i], _split_wait(second_half[i], after, name=f"gather{i}_wait")):
                full[u] = land.reshape(-1, land.shape[-1])
        return full[unit]

    sent = []

    def put_g(group):
        units = list(group)
        handle = _split_start([group[u].reshape(N_DEV, -1, group[u].shape[-1]) for u in units], True, None,
                              name=f"scatter{len(sent)}_start")
        sent.append((units, handle))
        return handle["token"]

    sq, gx, gP = _local_step(x[0], mem[0], loss_target[0], get_w, P, put_g, forward_point=forward_point)
    loss_here = (0.5 * jnp.sum(sq) / D_MODEL).reshape(1)

    received = {}
    group_of = {u: i for i, (units, _) in enumerate(sent) for u in units}
    out = {"grad": {}, "delta": {}, "new_m": {}, "new_v": {}}
    newest = [gx]

    def update_big(n):
        shape = w[n].shape
        as3 = lambda a: a.reshape((-1,) + shape[-2:])
        mine = [u for u, (wn, _, _) in UNITS.items() if wn == n]
        for i in sorted({group_of[u] for u in mine}):
            if sent[i][0][0] not in received:
                received.update(zip(sent[i][0], _split_wait(sent[i][1], newest[0], name=f"scatter{i}_wait")))
        flip = (lambda a: jnp.swapaxes(a, 1, 2)) if UNITS[mine[0]][2] else (lambda a: a)
        res = _reduce_adamw([received[u] for u in mine], flip(as3(w[n])), flip(as3(m[n])), flip(as3(v[n])),
                            tr=ADAMW_ROW_TILE[n], name=f"adamw_{n}")
        newest[0] = res[1]
        for kind, r in zip(("grad", "delta", "new_m", "new_v"), res):
            out[kind][n] = flip(r).reshape(shape)

    for n in ("w_down", "w_gate_up", "w_out", "w_mem_kv", "b_w_in", "w_kv"):
        update_big(n)

    full_shapes = [(2, A_WIDTH) if n == "a_lb_logits" else (1, A_WIDTH) if n == "a_onorm" else w[n].shape for n in SMALL_ORDER]
    n_small = sum(_prod(s) for s in full_shapes) + 1
    rows_small = -(-n_small // (8 * LANES)) * 8
    g_all = _all_gather_direct(_pack_flat([gP[n] for n in SMALL_ORDER] + [loss_here], rows_small, LANES, F32),
                               newest[0], name="gather_small_grads")
    summed = _unpack_flat(_sum_sources(g_all, tr=rows_small, name="sum_small_grads"), full_shapes + [(1,)])
    g_small = dict(zip(SMALL_ORDER, summed))
    loss = summed[-1].reshape(())
    me = 4 * lax.axis_index("x") + 2 * lax.axis_index("y") + lax.axis_index("c")
    for n in SMALL_SHARDED:
        g_small[n] = lax.dynamic_slice_in_dim(g_small[n], me * 96, 96, axis=1)
    shapes = [w[n].shape for n in SMALL_ORDER]
    rows_upd = -(-sum(_prod(s) for s in shapes) // (8 * LANES)) * 8
    pk = lambda d: _pack_flat([d[n] for n in SMALL_ORDER], rows_upd, LANES, F32)
    res = _adamw(pk(g_small)[None], pk(w)[None], pk(m)[None], pk(v)[None], tr=rows_upd, name="adamw_small")
    out["grad"].update(g_small)
    for kind, packed in zip(("delta", "new_m", "new_v"), res):
        out[kind].update(zip(SMALL_ORDER, _unpack_flat(packed[0], shapes)))
    newest[0] = res[0]
    update_big("a_w_in")

    return (loss, gx[None], *[out["grad"][n] for n in names], *[out["delta"][n] for n in names],
            *[out["new_m"][n] for n in names], *[out["new_v"][n] for n in names])
```

```python
import functools

import jax
import jax.numpy as jnp
import numpy as np
from jax import lax
from jax.experimental import pallas as pl
from jax.experimental.pallas import tpu as pltpu

F32 = jnp.float32
BF16 = jnp.bfloat16

N_DEV = 8
D_MODEL = 1024
HEAD_DIM = 128
A_HEADS = 6
A_WIDTH = A_HEADS * HEAD_DIM
CHUNK = 64
B_HEADS = 6
B_WIDTH = B_HEADS * HEAD_DIM
DILATIONS = (1, 4, 16)
SPAN = 128
N_GROUPS = 3
ROPE_THETA = 10000.0
MEM_TOKENS = 256
MEM_HEADS = 4
MEM_HEAD_DIM = 64
MEM_WIDTH = MEM_HEADS * MEM_HEAD_DIM
FFN_HIDDEN = 2816
EPS = 1e-6

ADAM_LR = 0.001
ADAM_B1 = 0.9
ADAM_B2 = 0.999
ADAM_EPS = 1e-08
ADAM_WD = 0.01
ADAM_STEP = 10

V7X_VMEM_LIMIT_BYTES = 56 * 1024 * 1024

NT_DIMS = (((1,), (1,)), ((), ()))
TN_DIMS = (((0,), (0,)), ((), ()))


def _cp(*sem):
    return pltpu.CompilerParams(dimension_semantics=sem, vmem_limit_bytes=V7X_VMEM_LIMIT_BYTES)


def _dot(a, b):
    return jnp.dot(a.astype(BF16), b.astype(BF16), preferred_element_type=F32)


def _dot_nt(a, b):
    return lax.dot_general(a.astype(BF16), b.astype(BF16), NT_DIMS, preferred_element_type=F32)


def _dot_tn(a, b):
    return lax.dot_general(a.astype(BF16), b.astype(BF16), TN_DIMS, preferred_element_type=F32)


def _dot3(m01, x):
    hi = x.astype(BF16)
    r1 = x - hi.astype(F32)
    mid = r1.astype(BF16)
    lo = (r1 - mid.astype(F32)).astype(BF16)
    d = functools.partial(jnp.dot, preferred_element_type=F32)
    return d(m01, hi) + d(m01, mid) + d(m01, lo)


def _sigmoid(x):
    return 0.5 * jnp.tanh(0.5 * x) + 0.5


def _full(shape):
    return pl.BlockSpec(shape, lambda *_: (0,) * len(shape))


def _dep(body, n_in, dep):
    if dep is None:
        return body, [], []

    def with_dep(*refs):
        return body(*refs[:n_in], *refs[n_in + 1:])

    return with_dep, [pl.BlockSpec(memory_space=pl.ANY)], [dep]


def _rms_matmul(x, g, w, *, tt, tn, wt, name, out_dtype=F32, dep=None, rotate=None):
    T, K = x.shape
    N = w.shape[0] if wt else w.shape[1]
    n_rot = 0 if rotate is None else rotate[0].shape[1] // HEAD_DIM
    extra_in = [] if rotate is None else list(rotate)

    def kernel_body(x_ref, g_ref, w_ref, *rest):
        y_ref, xn_ref = rest[len(extra_in)], rest[len(extra_in) + 1]
        xf = x_ref[...]
        r = lax.rsqrt(jnp.mean(xf * xf, axis=-1, keepdims=True) + EPS)
        xn = (xf * r * g_ref[...]).astype(BF16)
        xn_ref[...] = xn
        for j in range(N // tn):
            cols = slice(j * tn, (j + 1) * tn)
            y = _dot_nt(xn, w_ref[cols, :]) if wt else _dot(xn, w_ref[:, cols])
            y_ref[:, cols] = y.astype(out_dtype)
            for h in range(j * tn // HEAD_DIM, min((j + 1) * tn // HEAD_DIM, n_rot)):
                gw_ref, c_ref, s_ref, yr_ref = rest[0], rest[1], rest[2], rest[len(extra_in) + 2]
                sl = slice(h * HEAD_DIM, (h + 1) * HEAD_DIM)
                xhat, _ = _head_rms(y[:, h * HEAD_DIM - j * tn:(h + 1) * HEAD_DIM - j * tn])
                yr_ref[:, sl] = _rope(xhat * gw_ref[:, sl], c_ref[...], s_ref[...])

    tbl = pl.BlockSpec((tt, HEAD_DIM), lambda i: (i, 0))
    rot_specs = [] if rotate is None else [_full((1, n_rot * HEAD_DIM)), tbl, tbl]
    body, dep_specs, dep_args = _dep(kernel_body, 3 + len(extra_in), dep)
    return pl.pallas_call(
        body, grid=(T // tt,),
        in_specs=[pl.BlockSpec((tt, K), lambda i: (i, 0)), _full((1, K)), _full(w.shape)] + rot_specs + dep_specs,
        out_specs=[pl.BlockSpec((tt, N), lambda i: (i, 0)), pl.BlockSpec((tt, K), lambda i: (i, 0))]
        + ([] if rotate is None else [pl.BlockSpec((tt, n_rot * HEAD_DIM), lambda i: (i, 0))]),
        out_shape=[jax.ShapeDtypeStruct((T, N), out_dtype), jax.ShapeDtypeStruct((T, K), BF16)]
        + ([] if rotate is None else [jax.ShapeDtypeStruct((T, n_rot * HEAD_DIM), F32)]),
        compiler_params=_cp("parallel"), name=name)(x, g, w, *extra_in, *dep_args)


def _mm_res(res, a1, a2, w, *, tt, name):
    T, K1 = a1.shape
    K2 = a2.shape[1]
    N = w.shape[1]

    def body(r_ref, a1_ref, a2_ref, w_ref, o_ref):
        o_ref[...] = r_ref[...] + _dot(a1_ref[...], w_ref[:K1, :]) + _dot(a2_ref[...], w_ref[K1:, :])

    return pl.pallas_call(
        body, grid=(T // tt,),
        in_specs=[pl.BlockSpec((tt, N), lambda i: (i, 0)), pl.BlockSpec((tt, K1), lambda i: (i, 0)),
                  pl.BlockSpec((tt, K2), lambda i: (i, 0)), _full((K1 + K2, N))],
        out_specs=pl.BlockSpec((tt, N), lambda i: (i, 0)),
        out_shape=jax.ShapeDtypeStruct((T, N), F32),
        compiler_params=_cp("parallel"), name=name)(res, a1, a2, w)


def _swiglu_down(h, gu, wd, *, tt, name):
    T, D = h.shape
    Fh = wd.shape[0]

    def body(h_ref, gt_ref, up_ref, w_ref, o_ref):
        gt = gt_ref[...].astype(F32)
        act = gt * _sigmoid(gt) * up_ref[...].astype(F32)
        o_ref[...] = h_ref[...] + _dot(act, w_ref[...])

    return pl.pallas_call(
        body, grid=(T // tt,),
        in_specs=[pl.BlockSpec((tt, D), lambda i: (i, 0)), pl.BlockSpec((tt, Fh), lambda i: (i, 0)),
                  pl.BlockSpec((tt, Fh), lambda i: (i, 1)), _full((Fh, D))],
        out_specs=pl.BlockSpec((tt, D), lambda i: (i, 0)),
        out_shape=jax.ShapeDtypeStruct((T, D), F32),
        compiler_params=_cp("parallel"), name=name)(h, gu, gu, wd)


def _swiglu_down_loss(h, gu, wd, tgt, *, tt, name):
    T, D = h.shape
    Fh = wd.shape[0]

    def body(h_ref, gt_ref, up_ref, w_ref, t_ref, dy_ref, acc_ref):
        @pl.when(pl.program_id(0) == 0)
        def _():
            acc_ref[...] = jnp.zeros_like(acc_ref)

        gt = gt_ref[...].astype(F32)
        act = gt * _sigmoid(gt) * up_ref[...].astype(F32)
        e = h_ref[...] + _dot(act, w_ref[...]) - t_ref[...]
        dy_ref[...] = e * (1.0 / D)
        acc_ref[...] += jnp.sum(e * e, axis=0, keepdims=True)

    row = pl.BlockSpec((tt, D), lambda i: (i, 0))
    return pl.pallas_call(
        body, grid=(T // tt,),
        in_specs=[row, pl.BlockSpec((tt, Fh), lambda i: (i, 0)), pl.BlockSpec((tt, Fh), lambda i: (i, 1)), _full((Fh, D)), row],
        out_specs=[row, _full((1, D))],
        out_shape=[jax.ShapeDtypeStruct((T, D), F32), jax.ShapeDtypeStruct((1, D), F32)],
        compiler_params=_cp("arbitrary"), name=name)(h, gu, gu, wd, tgt)


SWIGLU_COLS = 256


def _swiglu_bwd(dh, gu, wd, *, tt, name):
    T, D = dh.shape
    Fh = wd.shape[0]
    last = T // tt - 1

    def body(dh_ref, gt_ref, up_ref, w_ref, dgu_ref, gw_ref, acc):
        @pl.when(pl.program_id(0) == 0)
        def _():
            acc[...] = jnp.zeros_like(acc)

        dh16 = dh_ref[...].astype(BF16)
        for c0 in range(0, Fh, SWIGLU_COLS):
            cols = slice(c0, c0 + SWIGLU_COLS)
            gt = gt_ref[:, cols].astype(F32)
            up = up_ref[:, cols].astype(F32)
            s = _sigmoid(gt)
            silu = gt * s
            dact = _dot_nt(dh16, w_ref[cols, :])
            acc[cols, :] += _dot_tn((silu * up).astype(BF16), dh16)
            dgu_ref[:, cols] = (dact * up * (s * (1.0 + gt * (1.0 - s)))).astype(BF16)
            dgu_ref[:, Fh + c0:Fh + c0 + SWIGLU_COLS] = (dact * silu).astype(BF16)

        @pl.when(pl.program_id(0) == last)
        def _():
            gw_ref[...] = acc[...].astype(BF16)

    return pl.pallas_call(
        body, grid=(T // tt,),
        in_specs=[pl.BlockSpec((tt, D), lambda i: (i, 0)), pl.BlockSpec((tt, Fh), lambda i: (i, 0)),
                  pl.BlockSpec((tt, Fh), lambda i: (i, 1)), _full((Fh, D))],
        out_specs=[pl.BlockSpec((tt, 2 * Fh), lambda i: (i, 0)), _full((Fh, D))],
        out_shape=[jax.ShapeDtypeStruct((T, 2 * Fh), BF16), jax.ShapeDtypeStruct((Fh, D), BF16)],
        scratch_shapes=[pltpu.VMEM((Fh, D), F32)],
        compiler_params=_cp("arbitrary"), name=name)(dh, gu, gu, wd)


def _out_proj_bwd(dy, a1, a2, w, *, tt, name, head_dots=False):
    T, N = dy.shape
    K1, K2 = a1.shape[1], a2.shape[1]
    K = K1 + K2
    last = T // tt - 1

    def body(dy_ref, a1_ref, a2_ref, w_ref, da_ref, gw_ref, *rest):
        acc = rest[-1]

        @pl.when(pl.program_id(0) == 0)
        def _():
            acc[...] = jnp.zeros_like(acc)

        dy16 = dy_ref[...].astype(BF16)
        da = _dot_nt(dy16, w_ref[...])
        da_ref[...] = da
        acc[:K1, :] += _dot_tn(a1_ref[...], dy16)
        acc[K1:, :] += _dot_tn(a2_ref[...], dy16)
        if head_dots:
            for h in range(K1 // HEAD_DIM):
                sl = slice(h * HEAD_DIM, (h + 1) * HEAD_DIM)
                rest[0][:, sl] = jnp.broadcast_to(jnp.sum(da[:, sl] * a1_ref[:, sl], axis=-1, keepdims=True), (tt, HEAD_DIM))

        @pl.when(pl.program_id(0) == last)
        def _():
            gw_ref[...] = acc[...].astype(BF16)

    extra_specs = [pl.BlockSpec((tt, K1), lambda i: (i, 0))] if head_dots else []
    extra_shapes = [jax.ShapeDtypeStruct((T, K1), F32)] if head_dots else []
    return pl.pallas_call(
        body, grid=(T // tt,),
        in_specs=[pl.BlockSpec((tt, N), lambda i: (i, 0)), pl.BlockSpec((tt, K1), lambda i: (i, 0)),
                  pl.BlockSpec((tt, K2), lambda i: (i, 0)), _full((K, N))],
        out_specs=[pl.BlockSpec((tt, K), lambda i: (i, 0)), _full((K, N))] + extra_specs,
        out_shape=[jax.ShapeDtypeStruct((T, K), F32), jax.ShapeDtypeStruct((K, N), BF16)] + extra_shapes,
        scratch_shapes=[pltpu.VMEM((K, N), F32)],
        compiler_params=_cp("arbitrary"), name=name)(dy, a1, a2, w)


def _mm_tn(a, b, *, tt, tka, name):
    T, Ka = a.shape
    N = b.shape[1]
    last = T // tt - 1

    def body(a_ref, b_ref, o_ref, acc):
        @pl.when(pl.program_id(1) == 0)
        def _():
            acc[...] = jnp.zeros_like(acc)

        acc[...] += _dot_tn(a_ref[...], b_ref[...])

        @pl.when(pl.program_id(1) == last)
        def _():
            o_ref[...] = acc[...].astype(BF16)

    return pl.pallas_call(
        body, grid=(Ka // tka, T // tt),
        in_specs=[pl.BlockSpec((tt, tka), lambda j, t: (t, j)), pl.BlockSpec((tt, N), lambda j, t: (t, 0))],
        out_specs=pl.BlockSpec((tka, N), lambda j, t: (j, 0)),
        out_shape=jax.ShapeDtypeStruct((Ka, N), BF16),
        scratch_shapes=[pltpu.VMEM((tka, N), F32)],
        compiler_params=_cp("parallel", "arbitrary"), name=name)(a, b)


def _mm_tn_pieces(pieces, b, *, tt, name):
    n = len(pieces)
    T = b.shape[0]
    N = b.shape[1]
    widths = [p.shape[1] for p in pieces]
    Ka = sum(widths)
    last = T // tt - 1

    def body(*refs):
        p_refs = refs[:n]
        b_ref, o_ref, acc = refs[n:]

        @pl.when(pl.program_id(0) == 0)
        def _():
            acc[...] = jnp.zeros_like(acc)

        bv = b_ref[...].astype(BF16)
        off = 0
        for p_ref, wd in zip(p_refs, widths):
            acc[off:off + wd, :] += _dot_tn(p_ref[...], bv)
            off += wd

        @pl.when(pl.program_id(0) == last)
        def _():
            o_ref[...] = acc[...].astype(BF16)

    return pl.pallas_call(
        body, grid=(T // tt,),
        in_specs=[pl.BlockSpec((tt, wd), lambda t: (t, 0)) for wd in widths] + [pl.BlockSpec((tt, N), lambda t: (t, 0))],
        out_specs=_full((Ka, N)), out_shape=jax.ShapeDtypeStruct((Ka, N), BF16),
        scratch_shapes=[pltpu.VMEM((Ka, N), F32)],
        compiler_params=_cp("arbitrary"), name=name)(*pieces, b)


def _rms_bwd_dx(x, g, w, dy, dres, *, tt, wt, name, dep=None):
    pieces = list(dy) if isinstance(dy, (list, tuple)) else [dy]
    n = len(pieces)
    widths = [p.shape[1] for p in pieces]
    T, K = x.shape

    def kernel_body(x_ref, g_ref, w_ref, *rest):
        dy_refs = rest[:n]
        dres_ref, dx_ref, dg_ref = rest[n:]

        @pl.when(pl.program_id(0) == 0)
        def _():
            dg_ref[...] = jnp.zeros_like(dg_ref)

        if n == 1:
            dxn = (_dot if wt else _dot_nt)(dy_refs[0][...], w_ref[...])
        else:
            dxn, off = 0.0, 0
            for dy_ref, wd in zip(dy_refs, widths):
                dxn = dxn + _dot(dy_ref[...], w_ref[off:off + wd, :])
                off += wd
        xf = x_ref[...]
        r = lax.rsqrt(jnp.mean(xf * xf, axis=-1, keepdims=True) + EPS)
        xhat = xf * r
        dg_ref[...] += jnp.sum(dxn * xhat, axis=0, keepdims=True)
        dxhat = dxn * g_ref[...]
        dx_ref[...] = dres_ref[...] + r * (dxhat - xhat * jnp.mean(dxhat * xhat, axis=-1, keepdims=True))

    assert n == 1 or wt
    body, dep_specs, dep_args = _dep(kernel_body, 4 + n, dep)
    return pl.pallas_call(
        body, grid=(T // tt,),
        in_specs=[pl.BlockSpec((tt, K), lambda i: (i, 0)), _full((1, K)), _full(w.shape)]
        + [pl.BlockSpec((tt, wd), lambda i: (i, 0)) for wd in widths]
        + [pl.BlockSpec((tt, K), lambda i: (i, 0))] + dep_specs,
        out_specs=[pl.BlockSpec((tt, K), lambda i: (i, 0)), _full((1, K))],
        out_shape=[jax.ShapeDtypeStruct((T, K), F32), jax.ShapeDtypeStruct((1, K), F32)],
        compiler_params=_cp("arbitrary"), name=name)(x, g, w, *pieces, dres, *dep_args)


HGRN_TB = 512
HGRN_NCH = HGRN_TB // CHUNK
HGRN_UNROLL = 8
HGRN_HPB = 6


def _hgrn_chunk_fwd(q, z, lbv, tril01):
    sig = _sigmoid(z)
    f = lbv + (1.0 - lbv) * sig
    kk = 1.0 - f
    b = _dot3(tril01, jnp.log(f))
    bend = b[CHUNK - 1:CHUNK, :]
    sq = _sigmoid(q)
    eb = jnp.exp(b)
    emb = jnp.exp(-b)
    eo = jnp.exp(bend - b)
    dec = jnp.exp(bend)
    return sig, f, kk, sq, eb, emb, eo, dec


def _hgrn2_fwd(proj, lb, *, name):
    T = proj.shape[0]
    nT = T // HGRN_TB
    nC = T // CHUNK

    def body(q_ref, z_ref, v_ref, lb_ref, o_ref, st_ref, state):
        @pl.when(pl.program_id(1) == 0)
        def _():
            state[...] = jnp.zeros_like(state)

        row = lax.broadcasted_iota(jnp.int32, (CHUNK, CHUNK), 0)
        col = lax.broadcasted_iota(jnp.int32, (CHUNK, CHUNK), 1)
        causal = row >= col
        tril01 = causal.astype(BF16)

        def chunk(c, carry):
            rows = pl.ds(pl.multiple_of(c * CHUNK, CHUNK), CHUNK)
            for hh in range(HGRN_HPB):
                sl = slice(hh * HEAD_DIM, (hh + 1) * HEAD_DIM)
                q = q_ref[rows, sl]
                v = v_ref[rows, sl].astype(BF16)
                sig, f, kk, sq, eb, emb, eo, dec = _hgrn_chunk_fwd(q, z_ref[rows, sl], lb_ref[:, sl], tril01)
                qi = (q * sq * eb).astype(BF16)
                ki = (kk * emb).astype(BF16)
                ko = (kk * eo).astype(BF16)
                st = state[hh]
                att = jnp.where(causal, _dot_nt(qi, ki), 0.0)
                o_ref[rows, sl] = _dot(att, v) + _dot_nt(qi, st)
                st_ref[c, hh] = st
                state[hh] = st * dec + _dot_tn(v, ko)
            return carry

        lax.fori_loop(0, HGRN_NCH, chunk, 0, unroll=HGRN_UNROLL)

    W = HGRN_HPB * HEAD_DIM
    nG = A_HEADS // HGRN_HPB
    hb = lambda off: pl.BlockSpec((HGRN_TB, W), lambda h, i: (i, off + h))
    return pl.pallas_call(
        body, grid=(nG, nT),
        in_specs=[hb(0), hb(nG), hb(2 * nG), pl.BlockSpec((1, W), lambda h, i: (0, h))],
        out_specs=[hb(0), pl.BlockSpec((HGRN_NCH, HGRN_HPB, HEAD_DIM, HEAD_DIM), lambda h, i: (i, h, 0, 0))],
        out_shape=[jax.ShapeDtypeStruct((T, A_WIDTH), F32), jax.ShapeDtypeStruct((nC, A_HEADS, HEAD_DIM, HEAD_DIM), F32)],
        scratch_shapes=[pltpu.VMEM((HGRN_HPB, HEAD_DIM, HEAD_DIM), F32)],
        compiler_params=_cp("parallel", "arbitrary"), name=name)(proj, proj, proj, lb)


def _hgrn2_bwd(proj, lb, st_all, do, *, name):
    T = proj.shape[0]
    nT = T // HGRN_TB

    def body(q_ref, z_ref, v_ref, lb_ref, st_ref, do_ref, dq_ref, dz_ref, dv_ref, dlb_ref, dstate):
        @pl.when(pl.program_id(1) == 0)
        def _():
            dstate[...] = jnp.zeros_like(dstate)
            dlb_ref[...] = jnp.zeros_like(dlb_ref)

        row = lax.broadcasted_iota(jnp.int32, (CHUNK, CHUNK), 0)
        col = lax.broadcasted_iota(jnp.int32, (CHUNK, CHUNK), 1)
        causal = row >= col
        tril01 = causal.astype(BF16)
        triu01 = (row <= col).astype(BF16)

        def chunk(cc, carry):
            c = HGRN_NCH - 1 - cc
            rows = pl.ds(pl.multiple_of(c * CHUNK, CHUNK), CHUNK)
            for hh in range(HGRN_HPB):
                sl = slice(hh * HEAD_DIM, (hh + 1) * HEAD_DIM)
                lbv = lb_ref[:, sl]
                q = q_ref[rows, sl]
                v = v_ref[rows, sl].astype(BF16)
                sig, f, kk, sq, eb, emb, eo, dec = _hgrn_chunk_fwd(q, z_ref[rows, sl], lbv, tril01)
                qi32 = q * sq * eb
                ki32 = kk * emb
                ko32 = kk * eo
                qi, ki, ko = qi32.astype(BF16), ki32.astype(BF16), ko32.astype(BF16)
                att = jnp.where(causal, _dot_nt(qi, ki), 0.0).astype(BF16)
                dout = do_ref[rows, sl].astype(BF16)
                st = st_ref[c, hh]
                dst = dstate[hh]
                dst16 = dst.astype(BF16)
                datt = jnp.where(causal, _dot_nt(dout, v), 0.0).astype(BF16)
                dqi = _dot(datt, ki) + _dot(dout, st)
                dki = _dot_tn(datt, qi)
                dv_ref[rows, sl] = (_dot_tn(att, dout) + _dot_nt(ko, dst16)).astype(BF16)
                dko = _dot(v, dst16)
                ddec = jnp.sum(dst * st, axis=0, keepdims=True)
                dstate[hh] = dst * dec + _dot_tn(dout, qi)
                dkk = dki * emb + dko * eo
                db = dqi * qi32 - dki * ki32 - dko * ko32
                dbend = jnp.sum(dko * ko32, axis=0, keepdims=True) + ddec * dec
                dlogf = _dot3(triu01, db) + dbend
                df = dlogf / f - dkk
                dz_ref[rows, sl] = (df * (1.0 - lbv) * sig * (1.0 - sig)).astype(BF16)
                dlb_ref[:, sl] += jnp.sum(df * (1.0 - sig), axis=0, keepdims=True)
                dq_ref[rows, sl] = (dqi * eb * (sq * (1.0 + q * (1.0 - sq)))).astype(BF16)
            return carry

        lax.fori_loop(0, HGRN_NCH, chunk, 0, unroll=HGRN_UNROLL)

    W = HGRN_HPB * HEAD_DIM
    nG = A_HEADS // HGRN_HPB
    hb = lambda off: pl.BlockSpec((HGRN_TB, W), lambda h, i: (nT - 1 - i, off + h))
    hlb = pl.BlockSpec((1, W), lambda h, i: (0, h))
    o16 = jax.ShapeDtypeStruct((T, A_WIDTH), BF16)
    return pl.pallas_call(
        body, grid=(nG, nT),
        in_specs=[hb(0), hb(nG), hb(2 * nG), hlb,
                  pl.BlockSpec((HGRN_NCH, HGRN_HPB, HEAD_DIM, HEAD_DIM), lambda h, i: (nT - 1 - i, h, 0, 0)), hb(0)],
        out_specs=[hb(0), hb(0), hb(0), hlb],
        out_shape=[o16, o16, o16, jax.ShapeDtypeStruct((1, A_WIDTH), F32)],
        scratch_shapes=[pltpu.VMEM((HGRN_HPB, HEAD_DIM, HEAD_DIM), F32)],
        compiler_params=_cp("parallel", "arbitrary"), name=name)(proj, proj, proj, lb, st_all, do)


def _head_rms(x):
    r = lax.rsqrt(jnp.mean(x * x, axis=-1, keepdims=True) + EPS)
    return x * r, r


def _head_rms_bwd(dxhat, xhat, r):
    return r * (dxhat - xhat * jnp.mean(dxhat * xhat, axis=-1, keepdims=True))


def _a_post_fwd(o, proj, onorm, *, tt, name):
    T = o.shape[0]

    def body(o_ref, g_ref, w_ref, y_ref):
        for h in range(A_HEADS):
            sl = slice(h * HEAD_DIM, (h + 1) * HEAD_DIM)
            xhat, _ = _head_rms(o_ref[:, sl])
            g = g_ref[:, sl]
            y_ref[:, sl] = xhat * w_ref[:, sl] * (g * _sigmoid(g))

    blk = lambda c: pl.BlockSpec((tt, A_WIDTH), lambda i: (i, c))
    return pl.pallas_call(
        body, grid=(T // tt,), in_specs=[blk(0), blk(3), _full((1, A_WIDTH))], out_specs=blk(0),
        out_shape=jax.ShapeDtypeStruct((T, A_WIDTH), F32),
        compiler_params=_cp("parallel"), name=name)(o, proj, onorm)


def _a_post_bwd(o, proj, onorm, dmix, *, tt, name, dep=None):
    T = o.shape[0]

    def kernel_body(o_ref, g_ref, w_ref, dy_ref, do_ref, dg_ref, dw_ref):
        @pl.when(pl.program_id(0) == 0)
        def _():
            dw_ref[...] = jnp.zeros_like(dw_ref)

        for h in range(A_HEADS):
            sl = slice(h * HEAD_DIM, (h + 1) * HEAD_DIM)
            xhat, r = _head_rms(o_ref[:, sl])
            g = g_ref[:, sl]
            s = _sigmoid(g)
            dy = dy_ref[:, sl]
            w = w_ref[:, sl]
            dg_ref[:, sl] = (dy * xhat * w * (s * (1.0 + g * (1.0 - s)))).astype(BF16)
            dyn = dy * (g * s)
            dw_ref[:, sl] += jnp.sum(dyn * xhat, axis=0, keepdims=True)
            do_ref[:, sl] = _head_rms_bwd(dyn * w, xhat, r)

    blk = lambda c: pl.BlockSpec((tt, A_WIDTH), lambda i: (i, c))
    body, dep_specs, dep_args = _dep(kernel_body, 4, dep)
    return pl.pallas_call(
        body, grid=(T // tt,), in_specs=[blk(0), blk(3), _full((1, A_WIDTH)), blk(0)] + dep_specs,
        out_specs=[blk(0), blk(0), _full((1, A_WIDTH))],
        out_shape=[jax.ShapeDtypeStruct((T, A_WIDTH), F32), jax.ShapeDtypeStruct((T, A_WIDTH), BF16),
                   jax.ShapeDtypeStruct((1, A_WIDTH), F32)],
        compiler_params=_cp("arbitrary"), name=name)(o, proj, onorm, dmix, *dep_args)


def _mem_head_masks(n):
    lane = lax.broadcasted_iota(jnp.int32, (n, MEM_WIDTH), 1)
    return [(lane >= m * MEM_HEAD_DIM) & (lane < (m + 1) * MEM_HEAD_DIM) for m in range(MEM_HEADS)]


def _mem_head_rms(x, masks):
    x2 = x * x
    r = jnp.zeros_like(x)
    for mk in masks:
        ms = jnp.sum(jnp.where(mk, x2, 0.0), axis=-1, keepdims=True) * (1.0 / MEM_HEAD_DIM)
        r = jnp.where(mk, lax.rsqrt(ms + EPS), r)
    return x * r, r


def _mem_head_rms_bwd(dxhat, xhat, r, masks):
    t = dxhat * xhat
    m = jnp.zeros_like(t)
    for mk in masks:
        m = jnp.where(mk, jnp.sum(jnp.where(mk, t, 0.0), axis=-1, keepdims=True) * (1.0 / MEM_HEAD_DIM), m)
    return r * (dxhat - xhat * m)


MEM_SCALE = MEM_HEAD_DIM ** -0.5


def _mem_attn_fwd(proj, qcol, mkv, qn_w, kn_w, *, tt, name):
    T = proj.shape[0]

    def body(q_ref, k_ref, v_ref, qw_ref, kw_ref, o_ref):
        qmasks = _mem_head_masks(tt)
        kmasks = _mem_head_masks(MEM_TOKENS)
        qhat, _ = _mem_head_rms(q_ref[...], qmasks)
        qn = qhat * qw_ref[...]
        khat, _ = _mem_head_rms(k_ref[...], kmasks)
        kn = (khat * kw_ref[...]).astype(BF16)
        v = v_ref[...].astype(BF16)
        out = jnp.zeros((tt, MEM_WIDTH), F32)
        for m in range(MEM_HEADS):
            s = _dot_nt(jnp.where(qmasks[m], qn, 0.0), kn) * MEM_SCALE
            s = s - jnp.max(s, axis=-1, keepdims=True)
            p = jnp.exp(s)
            p = p / jnp.sum(p, axis=-1, keepdims=True)
            out = jnp.where(qmasks[m], _dot(p, v), out)
        o_ref[...] = out

    return pl.pallas_call(
        body, grid=(T // tt,),
        in_specs=[pl.BlockSpec((tt, MEM_WIDTH), lambda i: (i, qcol)), pl.BlockSpec((MEM_TOKENS, MEM_WIDTH), lambda i: (0, 0)),
                  pl.BlockSpec((MEM_TOKENS, MEM_WIDTH), lambda i: (0, 1)), _full((1, MEM_WIDTH)), _full((1, MEM_WIDTH))],
        out_specs=pl.BlockSpec((tt, MEM_WIDTH), lambda i: (i, 0)),
        out_shape=jax.ShapeDtypeStruct((T, MEM_WIDTH), F32),
        compiler_params=_cp("parallel"), name=name)(proj, mkv, mkv, qn_w, kn_w)


def _mem_attn_bwd(proj, qcol, mkv, qn_w, kn_w, dmix, *, tt, name):
    T = proj.shape[0]
    nsteps = T // tt
    ocol = (dmix.shape[1] - MEM_WIDTH) // MEM_WIDTH

    def body(q_ref, k_ref, v_ref, qw_ref, kw_ref, do_ref, dq_ref, dkv_ref, dqw_ref, dkw_ref, dk_acc, dv_acc):
        step = pl.program_id(0)

        @pl.when(step == 0)
        def _():
            dk_acc[...] = jnp.zeros_like(dk_acc)
            dv_acc[...] = jnp.zeros_like(dv_acc)
            dqw_ref[...] = jnp.zeros_like(dqw_ref)

        qmasks = _mem_head_masks(tt)
        kmasks = _mem_head_masks(MEM_TOKENS)
        qhat, qr = _mem_head_rms(q_ref[...], qmasks)
        qn = qhat * qw_ref[...]
        khat, kr = _mem_head_rms(k_ref[...], kmasks)
        kn = (khat * kw_ref[...]).astype(BF16)
        v = v_ref[...].astype(BF16)
        dout = do_ref[...]
        dqn = jnp.zeros((tt, MEM_WIDTH), F32)
        dkn = jnp.zeros((MEM_TOKENS, MEM_WIDTH), F32)
        dvv = jnp.zeros((MEM_TOKENS, MEM_WIDTH), F32)
        for m in range(MEM_HEADS):
            qm = jnp.where(qmasks[m], qn, 0.0).astype(BF16)
            s = _dot_nt(qm, kn) * MEM_SCALE
            s = s - jnp.max(s, axis=-1, keepdims=True)
            p = jnp.exp(s)
            p = p / jnp.sum(p, axis=-1, keepdims=True)
            dom = jnp.where(qmasks[m], dout, 0.0).astype(BF16)
            dp = _dot_nt(dom, v)
            ds = (p * (dp - jnp.sum(p * dp, axis=-1, keepdims=True)) * MEM_SCALE).astype(BF16)
            dqn = jnp.where(qmasks[m], _dot(ds, kn), dqn)
            dkn = jnp.where(kmasks[m], _dot_tn(ds, qm), dkn)
            dvv = jnp.where(kmasks[m], _dot_tn(p, dom), dvv)
        dqw_ref[...] += jnp.sum(dqn * qhat, axis=0, keepdims=True)
        dq_ref[...] = _mem_head_rms_bwd(dqn * qw_ref[...], qhat, qr, qmasks).astype(BF16)
        dk_acc[...] += dkn
        dv_acc[...] += dvv

        @pl.when(step == nsteps - 1)
        def _():
            dk = dk_acc[...]
            dkw_ref[...] = jnp.sum(dk * khat, axis=0, keepdims=True)
            dkv_ref[:, :MEM_WIDTH] = _mem_head_rms_bwd(dk * kw_ref[...], khat, kr, kmasks)
            dkv_ref[:, MEM_WIDTH:] = dv_acc[...]

    return pl.pallas_call(
        body, grid=(nsteps,),
        in_specs=[pl.BlockSpec((tt, MEM_WIDTH), lambda i: (i, qcol)), pl.BlockSpec((MEM_TOKENS, MEM_WIDTH), lambda i: (0, 0)),
                  pl.BlockSpec((MEM_TOKENS, MEM_WIDTH), lambda i: (0, 1)), _full((1, MEM_WIDTH)), _full((1, MEM_WIDTH)),
                  pl.BlockSpec((tt, MEM_WIDTH), lambda i: (i, ocol))],
        out_specs=[pl.BlockSpec((tt, MEM_WIDTH), lambda i: (i, 0)), _full((MEM_TOKENS, 2 * MEM_WIDTH)),
                   _full((1, MEM_WIDTH)), _full((1, MEM_WIDTH))],
        out_shape=[jax.ShapeDtypeStruct((T, MEM_WIDTH), BF16), jax.ShapeDtypeStruct((MEM_TOKENS, 2 * MEM_WIDTH), F32),
                   jax.ShapeDtypeStruct((1, MEM_WIDTH), F32), jax.ShapeDtypeStruct((1, MEM_WIDTH), F32)],
        scratch_shapes=[pltpu.VMEM((MEM_TOKENS, MEM_WIDTH), F32), pltpu.VMEM((MEM_TOKENS, MEM_WIDTH), F32)],
        compiler_params=_cp("arbitrary"), name=name)(proj, mkv, mkv, qn_w, kn_w, dmix)


HALF = HEAD_DIM // 2
ATT_SCALE = HEAD_DIM ** -0.5
NEG = -1e30


def _rope_tables(T):
    inv = np.float32(ROPE_THETA) ** (-np.arange(HALF, dtype=np.float32) / np.float32(HALF))
    ang = np.arange(T, dtype=np.float32)[:, None] * inv[None, :].astype(np.float32)
    cos, sin = np.cos(ang).astype(np.float32), np.sin(ang).astype(np.float32)
    return jnp.asarray(np.concatenate([cos, cos], axis=-1)), jnp.asarray(np.concatenate([-sin, sin], axis=-1))


def _rope(x, cosf, sinsg):
    return x * cosf + pltpu.roll(x, HALF, 1) * sinsg


def _rope_bwd(dy, cosf, sinsg):
    return dy * cosf + pltpu.roll(dy * sinsg, HALF, 1)


def _q_prep_bwd(proj, w_heads, cosf, sinsg, dqs, *, tt, name):
    T = proj.shape[0]
    W = N_GROUPS * B_WIDTH

    def body(x_ref, w_ref, c_ref, s_ref, d0, d1, d2, dx_ref, dw_ref):
        @pl.when(pl.program_id(0) == 0)
        def _():
            dw_ref[...] = jnp.zeros_like(dw_ref)

        c, s = c_ref[...], s_ref[...]
        for gi, d_ref in enumerate((d0, d1, d2)):
            for h in range(B_HEADS):
                sl = slice((gi * B_HEADS + h) * HEAD_DIM, (gi * B_HEADS + h + 1) * HEAD_DIM)
                xhat, r = _head_rms(x_ref[:, sl])
                dyn = _rope_bwd(d_ref[:, h * HEAD_DIM:(h + 1) * HEAD_DIM], c, s)
                dw_ref[:, sl] += jnp.sum(dyn * xhat, axis=0, keepdims=True)
                dx_ref[:, sl] = _head_rms_bwd(dyn * w_ref[:, sl], xhat, r).astype(BF16)

    tbl = pl.BlockSpec((tt, HEAD_DIM), lambda i: (i, 0))
    dyb = pl.BlockSpec((tt, B_WIDTH), lambda i: (i, 0))
    return pl.pallas_call(
        body, grid=(T // tt,),
        in_specs=[pl.BlockSpec((tt, W), lambda i: (i, 0)), _full((1, W)), tbl, tbl, dyb, dyb, dyb],
        out_specs=[pl.BlockSpec((tt, W), lambda i: (i, 0)), _full((1, W))],
        out_shape=[jax.ShapeDtypeStruct((T, W), BF16), jax.ShapeDtypeStruct((1, W), F32)],
        compiler_params=_cp("arbitrary"), name=name)(proj, w_heads, cosf, sinsg, *dqs)


def _kv_prep_bwd(kv, w_heads, cosf, sinsg, dks, dvs, *, tt, name):
    T = kv.shape[0]

    def body(x_ref, w_ref, c_ref, s_ref, k0, k1, k2, v0, v1, v2, dx_ref, dw_ref):
        @pl.when(pl.program_id(0) == 0)
        def _():
            dw_ref[...] = jnp.zeros_like(dw_ref)

        c, s = c_ref[...], s_ref[...]
        for h in range(B_HEADS):
            sl = slice(h * HEAD_DIM, (h + 1) * HEAD_DIM)
            vs = slice(B_WIDTH + h * HEAD_DIM, B_WIDTH + (h + 1) * HEAD_DIM)
            xhat, r = _head_rms(x_ref[:, sl])
            dyn = _rope_bwd(k0[:, sl] + k1[:, sl] + k2[:, sl], c, s)
            dw_ref[:, sl] += jnp.sum(dyn * xhat, axis=0, keepdims=True)
            dx_ref[:, sl] = _head_rms_bwd(dyn * w_ref[:, sl], xhat, r).astype(BF16)
            dx_ref[:, vs] = (v0[:, sl] + v1[:, sl] + v2[:, sl]).astype(BF16)

    tbl = pl.BlockSpec((tt, HEAD_DIM), lambda i: (i, 0))
    dyb = pl.BlockSpec((tt, B_WIDTH), lambda i: (i, 0))
    return pl.pallas_call(
        body, grid=(T // tt,),
        in_specs=[dyb, _full((1, B_WIDTH)), tbl, tbl] + [dyb] * 6,
        out_specs=[pl.BlockSpec((tt, 2 * B_WIDTH), lambda i: (i, 0)), _full((1, B_WIDTH))],
        out_shape=[jax.ShapeDtypeStruct((T, 2 * B_WIDTH), BF16), jax.ShapeDtypeStruct((1, B_WIDTH), F32)],
        compiler_params=_cp("arbitrary"), name=name)(kv, w_heads, cosf, sinsg, *dks, *dvs)


def _band_masks(n_is_first=None):
    row = lax.broadcasted_iota(jnp.int32, (SPAN, SPAN), 0)
    col = lax.broadcasted_iota(jnp.int32, (SPAN, SPAN), 1)
    return row >= col, col >= row


def _dil_views(T, d):
    L = T // d
    return L, L // SPAN


def _dil_fwd(qr, kr, kv, gi, d, *, name):
    T = qr.shape[0]
    L, nb = _dil_views(T, d)

    def body(q_ref, kc_ref, kp_ref, vc_ref, vp_ref, o_ref, lse_ref):
        cur_ok, prev_band = _band_masks()
        prev_ok = prev_band & (pl.program_id(1) > 0)
        for h in range(B_HEADS):
            sl = slice(h * HEAD_DIM, (h + 1) * HEAD_DIM)
            q = q_ref[:, sl]
            sc = jnp.where(cur_ok, _dot_nt(q, kc_ref[:, sl]) * ATT_SCALE, NEG)
            sp = jnp.where(prev_ok, _dot_nt(q, kp_ref[:, sl]) * ATT_SCALE, NEG)
            m = jnp.maximum(jnp.max(sc, axis=-1, keepdims=True), jnp.max(sp, axis=-1, keepdims=True))
            pc = jnp.exp(sc - m)
            pp = jnp.exp(sp - m)
            l = jnp.sum(pc, axis=-1, keepdims=True) + jnp.sum(pp, axis=-1, keepdims=True)
            o_ref[:, sl] = (_dot(pc, vc_ref[:, sl]) + _dot(pp, vp_ref[:, sl])) / l
            lse_ref[:, sl] = jnp.broadcast_to(m + jnp.log(l), (SPAN, HEAD_DIM))

    blk = lambda f: pl.BlockSpec((SPAN, B_WIDTH), f)
    cur = lambda r, n: (n, r)
    prev = lambda r, n: (jnp.maximum(n - 1, 0), r)
    ov = jax.ShapeDtypeStruct((L, d * B_WIDTH), F32)
    o, lse = pl.pallas_call(
        body, grid=(d, nb),
        in_specs=[blk(lambda r, n: (n, r * N_GROUPS + gi)), blk(cur), blk(prev),
                  blk(lambda r, n: (n, 2 * r + 1)), blk(lambda r, n: (jnp.maximum(n - 1, 0), 2 * r + 1))],
        out_specs=[blk(cur), blk(cur)], out_shape=[ov, ov],
        compiler_params=_cp("parallel", "arbitrary"), name=name,
    )(qr.reshape(L, d * N_GROUPS * B_WIDTH), kr.reshape(L, d * B_WIDTH), kr.reshape(L, d * B_WIDTH),
      kv.reshape(L, d * 2 * B_WIDTH), kv.reshape(L, d * 2 * B_WIDTH))
    return o.reshape(T, B_WIDTH), lse.reshape(T, B_WIDTH)


def _dil_combine_fwd(os_, lses, *, tt, name):
    T = os_[0].shape[0]

    def body(o0, o1, o2, l0, l1, l2, y_ref, lse_ref):
        a, b, c = l0[...], l1[...], l2[...]
        m = jnp.maximum(jnp.maximum(a, b), c)
        wa, wb, wc = jnp.exp(a - m), jnp.exp(b - m), jnp.exp(c - m)
        den = wa + wb + wc
        y_ref[...] = (wa * o0[...] + wb * o1[...] + wc * o2[...]) / den
        lse_ref[...] = m + jnp.log(den)

    blk = pl.BlockSpec((tt, B_WIDTH), lambda i: (i, 0))
    sh = jax.ShapeDtypeStruct((T, B_WIDTH), F32)
    return pl.pallas_call(
        body, grid=(T // tt,), in_specs=[blk] * 6, out_specs=[blk, blk], out_shape=[sh, sh],
        compiler_params=_cp("parallel"), name=name)(*os_, *lses)


DILS_UNROLL = 8


def _dils_specs(gi, d, nblk):
    blk = lambda f: pl.BlockSpec((SPAN * d, HEAD_DIM), f)
    return {
        "q": blk(lambda h, n: (n, gi * B_HEADS + h)), "q_next": blk(lambda h, n: (jnp.minimum(n + 1, nblk - 1), gi * B_HEADS + h)),
        "cur": blk(lambda h, n: (n, h)), "prev": blk(lambda h, n: (jnp.maximum(n - 1, 0), h)),
        "next": blk(lambda h, n: (jnp.minimum(n + 1, nblk - 1), h)),
        "v": blk(lambda h, n: (n, B_HEADS + h)), "v_prev": blk(lambda h, n: (jnp.maximum(n - 1, 0), B_HEADS + h)),
    }


def _dils_fwd(qr, kr, kv, gi, d, *, name):
    T = qr.shape[0]
    nblk = T // (SPAN * d)
    sp = _dils_specs(gi, d, nblk)

    def body(q_ref, kc_ref, vc_ref, o_ref, lse_ref, k_before, v_before):
        @pl.when(pl.program_id(1) == 0)
        def _():
            k_before[...] = jnp.zeros_like(k_before)
            v_before[...] = jnp.zeros_like(v_before)

        cur_ok, prev_band = _band_masks()
        prev_ok = prev_band & (pl.program_id(1) > 0)

        def residue(r, carry):
            rows = pl.ds(r, SPAN, stride=d)
            q, kc, vc = q_ref[rows, :], kc_ref[rows, :].astype(BF16), vc_ref[rows, :].astype(BF16)
            sc = jnp.where(cur_ok, _dot_nt(q, kc) * ATT_SCALE, NEG)
            sp_ = jnp.where(prev_ok, _dot_nt(q, k_before[r]) * ATT_SCALE, NEG)
            m = jnp.maximum(jnp.max(sc, axis=-1, keepdims=True), jnp.max(sp_, axis=-1, keepdims=True))
            pc = jnp.exp(sc - m)
            pp = jnp.exp(sp_ - m)
            l = jnp.sum(pc, axis=-1, keepdims=True) + jnp.sum(pp, axis=-1, keepdims=True)
            o_ref[rows, :] = (_dot(pc, vc) + _dot(pp, v_before[r])) / l
            lse_ref[rows, :] = jnp.broadcast_to(m + jnp.log(l), (SPAN, HEAD_DIM))
            k_before[r] = kc
            v_before[r] = vc
            return carry

        lax.fori_loop(0, d, residue, 0, unroll=min(d, DILS_UNROLL))

    sh = jax.ShapeDtypeStruct((T, B_WIDTH), F32)
    return pl.pallas_call(
        body, grid=(B_HEADS, nblk), in_specs=[sp["q"], sp["cur"], sp["v"]],
        out_specs=[sp["cur"], sp["cur"]], out_shape=[sh, sh],
        scratch_shapes=[pltpu.VMEM((d, SPAN, HEAD_DIM), BF16), pltpu.VMEM((d, SPAN, HEAD_DIM), BF16)],
        compiler_params=_cp("parallel", "arbitrary"), name=name)(qr, kr, kv)


DIL_BWD_GROUP = {1: 8, 4: 1, 16: 1}


def _dil_bwd(qr, kr, kv, dmix, lse, dd, gi, d, *, name, dep=None):
    T = qr.shape[0]
    G = DIL_BWD_GROUP[d]
    band = SPAN * d
    tb = G * band
    nblk = T // tb
    n_units = T // SPAN

    keep = G == 1

    def kernel_body(q_ref, dy_ref, lse_ref, dd_ref, kc_ref, vc_ref, *rest):
        if keep:
            dq_ref, dk_ref, dv_ref, dk_acc, dv_acc, k_before, v_before = rest
        else:
            kp_ref, vp_ref, dq_ref, dk_ref, dv_ref, dk_acc, dv_acc = rest
        n = pl.program_id(1)

        @pl.when(n == 0)
        def _():
            dk_acc[...] = jnp.zeros_like(dk_acc)
            dv_acc[...] = jnp.zeros_like(dv_acc)
            if keep:
                k_before[...] = jnp.zeros_like(k_before)
                v_before[...] = jnp.zeros_like(v_before)

        cur_ok, prev_band = _band_masks()
        for j in range(G):
            def residue(r, carry, j=j):
                off = j * band + r
                rows = pl.ds(off, SPAN, stride=d)
                q, dy = q_ref[rows, :], dy_ref[rows, :]
                lse_h = jnp.max(lse_ref[rows, :], axis=-1, keepdims=True)
                dd_h = jnp.max(dd_ref[rows, :], axis=-1, keepdims=True)
                kc, vc = kc_ref[rows, :].astype(BF16), vc_ref[rows, :].astype(BF16)
                if j > 0:
                    before = pl.ds(off - band, SPAN, stride=d)
                    kp, vp = kc_ref[before, :], vc_ref[before, :]
                    prev_ok = prev_band
                elif keep:
                    kp, vp = k_before[r], v_before[r]
                    k_before[r] = kc
                    v_before[r] = vc
                    prev_ok = prev_band & (n > 0)
                else:
                    before = pl.ds((G - 1) * band + r, SPAN, stride=d)
                    kp, vp = kp_ref[before, :], vp_ref[before, :]
                    prev_ok = prev_band & (n > 0)
                pc = jnp.exp(jnp.where(cur_ok, _dot_nt(q, kc) * ATT_SCALE, NEG) - lse_h)
                pp = jnp.exp(jnp.where(prev_ok, _dot_nt(q, kp) * ATT_SCALE, NEG) - lse_h)
                dsc = pc * (_dot_nt(dy, vc) - dd_h) * ATT_SCALE
                dsp = pp * (_dot_nt(dy, vp) - dd_h) * ATT_SCALE
                dq_ref[rows, :] = _dot(dsc, kc) + _dot(dsp, kp)
                u = (n * G + j) * d + r
                here = pl.ds(pl.multiple_of(u * SPAN, SPAN), SPAN)
                dk_acc[here, :] += _dot_tn(dsc, q)
                dv_acc[here, :] += _dot_tn(pc, dy)
                there = pl.ds(pl.multiple_of(jnp.maximum(u - d, 0) * SPAN, SPAN), SPAN)
                dk_acc[there, :] += _dot_tn(dsp, q)
                dv_acc[there, :] += _dot_tn(pp, dy)
                return carry

            lax.fori_loop(0, d, residue, 0, unroll=min(d, DILS_UNROLL))

        @pl.when(n == nblk - 1)
        def _():
            def place(u, carry):
                rows = pl.ds((u // d) * band + u % d, SPAN, stride=d)
                src = pl.ds(pl.multiple_of(u * SPAN, SPAN), SPAN)
                dk_ref[rows, :] = dk_acc[src, :]
                dv_ref[rows, :] = dv_acc[src, :]
                return carry

            lax.fori_loop(0, n_units, place, 0)

    blk = lambda f: pl.BlockSpec((tb, HEAD_DIM), f)
    cur = lambda h, n: (n, h)
    prev = lambda h, n: (jnp.maximum(n - 1, 0), h)
    whole = pl.BlockSpec((T, HEAD_DIM), lambda h, n: (0, h))
    sh = jax.ShapeDtypeStruct((T, B_WIDTH), F32)
    v_cur = blk(lambda h, n: (n, B_HEADS + h))
    if keep:
        kv_specs, kv_args = [blk(cur), v_cur], [kr, kv]
        kept = [pltpu.VMEM((d, SPAN, HEAD_DIM), BF16), pltpu.VMEM((d, SPAN, HEAD_DIM), BF16)]
    else:
        kv_specs = [blk(cur), v_cur, blk(prev), blk(lambda h, n: (jnp.maximum(n - 1, 0), B_HEADS + h))]
        kv_args, kept = [kr, kv, kr, kv], []
    body, dep_specs, dep_args = _dep(kernel_body, 4 + len(kv_args), dep)
    return pl.pallas_call(
        body, grid=(B_HEADS, nblk),
        in_specs=[blk(lambda h, n: (n, gi * B_HEADS + h)), blk(cur), blk(cur), blk(cur)] + kv_specs + dep_specs,
        out_specs=[blk(cur), whole, whole], out_shape=[sh, sh, sh],
        scratch_shapes=[pltpu.VMEM((T, HEAD_DIM), F32), pltpu.VMEM((T, HEAD_DIM), F32)] + kept,
        compiler_params=_cp("parallel", "arbitrary"), name=name)(qr, dmix, lse, dd, *kv_args, *dep_args)


A_MQ_COL = 4 * A_WIDTH // MEM_WIDTH
B_MQ_COL = N_GROUPS * B_WIDTH // MEM_WIDTH


def _row(v):
    return v.reshape(1, -1).astype(F32)


def _local_step(x, mem, tgt, get_w, P, put_g, first_dep=None, forward_point=lambda i, value: value):
    T = x.shape[0]
    cosf, sinsg = _rope_tables(T)
    lb_soft = jax.nn.softmax(P["a_lb_logits"].astype(F32), axis=0)
    lb = lb_soft[0:1]
    qw_heads = jnp.repeat(P["b_qnorm"][0], B_HEADS, axis=0).reshape(1, -1)
    kw_heads = jnp.tile(_row(P["b_knorm"]), (1, B_HEADS))
    mqw = [jnp.tile(_row(P["mem_qnorm"][l]), (1, MEM_HEADS)) for l in range(2)]
    mkw = [jnp.tile(_row(P["mem_knorm"][l]), (1, MEM_HEADS)) for l in range(2)]
    nmix = [_row(P["norm_mix"][l]) for l in range(2)]
    nffn = [_row(P["norm_ffn"][l]) for l in range(2)]
    mnorm = [_row(P["mem_norm"][l]) for l in range(2)]
    kvn = _row(P["kv_norm"])
    onorm = _row(P["a_onorm"])
    W = {}

    def w_of(name, after=None):
        if name not in W:
            W[name] = get_w(name, after)
        return W[name]

    proj_a, xn0 = _rms_matmul(x, nmix[0], w_of("a_w_in"), tt=512, tn=1664, wt=True, name="proj_a", dep=first_dep)
    mkv0, mn0 = _rms_matmul(mem, mnorm[0], w_of("w_mem_kv0"), tt=MEM_TOKENS, tn=2 * MEM_WIDTH, wt=False, name="mem_kv0")
    o_raw, st = _hgrn2_fwd(proj_a, lb, name="hgrn2_fwd")
    o_raw = forward_point(0, o_raw)
    mm0 = _a_post_fwd(o_raw, proj_a, onorm, tt=512, name="a_post_fwd")
    mo0 = _mem_attn_fwd(proj_a, A_MQ_COL, mkv0, mqw[0], mkw[0], tt=1024, name="mem_attn_fwd0")
    hm0 = _mm_res(x, mm0, mo0, w_of("w_out0", mo0), tt=512, name="out_proj0")
    hm0 = forward_point(1, hm0)
    gu0, hn0 = _rms_matmul(hm0, nffn[0], w_of("w_gate_up0", hm0), tt=512, tn=1408, wt=True, out_dtype=BF16, name="gate_up0")
    h1 = _swiglu_down(hm0, gu0, w_of("w_down0", gu0), tt=512, name="down0")
    h1 = forward_point(2, h1)
    kv, hkn, kr = _rms_matmul(h1, kvn, w_of("w_kv", h1), tt=512, tn=768, wt=True, name="kv_proj",
                              rotate=(kw_heads, cosf, sinsg))

    proj_b, xn1, qr = _rms_matmul(h1, nmix[1], w_of("b_w_in", kr), tt=512, tn=1280, wt=True, name="proj_b",
                                  rotate=(qw_heads, cosf, sinsg))
    proj_b = forward_point(3, proj_b)
    mkv1, mn1 = _rms_matmul(mem, mnorm[1], w_of("w_mem_kv1", kr), tt=MEM_TOKENS, tn=2 * MEM_WIDTH, wt=False, name="mem_kv1")
    outs = [(_dil_fwd if d == 1 else _dils_fwd)(qr, kr, kv, gi, d, name=f"dil_fwd{gi}") for gi, d in enumerate(DILATIONS)]
    mm1, lse_tot = _dil_combine_fwd([o for o, _ in outs], [s for _, s in outs], tt=512, name="dil_combine")
    mo1 = _mem_attn_fwd(proj_b, B_MQ_COL, mkv1, mqw[1], mkw[1], tt=1024, name="mem_attn_fwd1")
    hm1 = _mm_res(h1, mm1, mo1, w_of("w_out1", mo1), tt=512, name="out_proj1")
    gu1, hn1 = _rms_matmul(hm1, nffn[1], w_of("w_gate_up1", hm1), tt=512, tn=1408, wt=True, out_dtype=BF16, name="gate_up1")
    dy, sq = _swiglu_down_loss(hm1, gu1, w_of("w_down1", gu1), tgt, tt=512, name="down1_loss")

    gP = {}
    zeros_mem = jnp.zeros((MEM_TOKENS, D_MODEL), F32)

    def ffn_bwd(l, dh, hm, gu, hn):
        dgu, g_wd = _swiglu_bwd(dh, gu, w_of(f"w_down{l}"), tt=256, name=f"swiglu_bwd{l}")
        g_wgu = _mm_tn(dgu, hn, tt=512, tka=1408, name=f"g_w_gate_up{l}")
        sent = put_g({f"w_down{l}": g_wd, f"w_gate_up{l}": g_wgu})
        dhm, g_nf = _rms_bwd_dx(hm, nffn[l], w_of(f"w_gate_up{l}"), dgu, dh, tt=512, wt=True, name=f"gate_up_bwd{l}", dep=sent)
        return dhm, g_nf

    def mix_bwd(l, dhm, mix_main, mix_mem, proj, qcol, mkv, mn):
        dmix, g_wout, *head_dots = _out_proj_bwd(dhm, mix_main, mix_mem, w_of(f"w_out{l}"), tt=512, name=f"out_proj_bwd{l}",
                                                 head_dots=l == 1)
        dmq, dmkv, dqw, dkw = _mem_attn_bwd(proj, qcol, mkv, mqw[l], mkw[l], dmix, tt=1024, name=f"mem_attn_bwd{l}")
        g_wmkv = _mm_tn(mn, dmkv, tt=MEM_TOKENS, tka=512, name=f"g_w_mem_kv{l}")
        sent = put_g({f"w_out{l}": g_wout, f"w_mem_kv{l}": g_wmkv})
        _, g_mn = _rms_bwd_dx(mem, mnorm[l], w_of(f"w_mem_kv{l}"), dmkv, zeros_mem, tt=MEM_TOKENS, wt=False, name=f"mem_kv_bwd{l}")
        fold = lambda v: v.reshape(MEM_HEADS, MEM_HEAD_DIM).sum(axis=0)
        return dmix, dmq, g_mn, fold(dqw), fold(dkw), sent, head_dots

    dhm1, g_nf1 = ffn_bwd(1, dy, hm1, gu1, hn1)
    dmix1, dmq1, g_mn1, g_mq1, g_mk1, sent, (dd,) = mix_bwd(1, dhm1, mm1, mo1, proj_b, B_MQ_COL, mkv1, mn1)
    dqs, dks, dvs = [], [], []
    for gi, d in enumerate(DILATIONS):
        dq_g, dk_g, dv_g = _dil_bwd(qr, kr, kv, dmix1, lse_tot, dd, gi, d, name=f"dil_bwd{gi}", dep=sent if gi == 0 else None)
        dqs.append(dq_g)
        dks.append(dk_g)
        dvs.append(dv_g)
    dq_raw, dqw = _q_prep_bwd(proj_b, qw_heads, cosf, sinsg, dqs, tt=512, name="q_prep_bwd")
    dkv, dkw = _kv_prep_bwd(kv, kw_heads, cosf, sinsg, dks, dvs, tt=512, name="kv_prep_bwd")
    dproj_b = [dq_raw, dmq1]
    g_wb = _mm_tn_pieces(dproj_b, xn1, tt=512, name="g_b_w_in")
    g_wkv = _mm_tn(dkv, hkn, tt=512, tka=768, name="g_w_kv")
    sent = put_g({"b_w_in": g_wb, "w_kv": g_wkv})
    dh1, g_nm1 = _rms_bwd_dx(h1, nmix[1], w_of("b_w_in"), dproj_b, dhm1, tt=512, wt=True, name="proj_b_bwd", dep=sent)
    dh1, g_kvn = _rms_bwd_dx(h1, kvn, w_of("w_kv"), dkv, dh1, tt=512, wt=True, name="kv_proj_bwd")

    dhm0, g_nf0 = ffn_bwd(0, dh1, hm0, gu0, hn0)
    dmix0, dmq0, g_mn0, g_mq0, g_mk0, sent, _ = mix_bwd(0, dhm0, mm0, mo0, proj_a, A_MQ_COL, mkv0, mn0)
    do_raw, dg, g_onorm = _a_post_bwd(o_raw, proj_a, onorm, dmix0, tt=512, name="a_post_bwd", dep=sent)
    dq, dz, dv, dlb = _hgrn2_bwd(proj_a, lb, st, do_raw, name="hgrn2_bwd")
    dproj_a = [dq, dz, dv, dg, dmq0]
    sent = put_g({"a_w_in": _mm_tn_pieces(dproj_a, xn0, tt=512, name="g_a_w_in")})
    gx, g_nm0 = _rms_bwd_dx(x, nmix[0], w_of("a_w_in"), dproj_a, dhm0, tt=512, wt=True, name="proj_a_bwd", dep=sent)

    dl0 = lb_soft[0:1] * lb_soft[1:2] * dlb
    gP["a_lb_logits"] = jnp.concatenate([dl0, -dl0], axis=0)
    gP["a_onorm"] = g_onorm
    gP["norm_mix"] = jnp.concatenate([g_nm0, g_nm1], axis=0)
    gP["norm_ffn"] = jnp.concatenate([g_nf0, g_nf1], axis=0)
    gP["b_qnorm"] = dqw.reshape(N_GROUPS, B_HEADS, HEAD_DIM).sum(axis=1)[None]
    gP["kv_norm"] = g_kvn.reshape(-1)
    gP["b_knorm"] = dkw.reshape(B_HEADS, HEAD_DIM).sum(axis=0)
    gP["mem_norm"] = jnp.concatenate([g_mn0, g_mn1], axis=0)
    gP["mem_qnorm"] = jnp.stack([g_mq0, g_mq1])
    gP["mem_knorm"] = jnp.stack([g_mk0, g_mk1])
    return sq, gx, gP


MESH_ID = pl.DeviceIdType.MESH
HBM_SPEC = pl.BlockSpec(memory_space=pltpu.HBM)


def _position():
    return lax.axis_index("x"), lax.axis_index("y"), lax.axis_index("c")


def _all_gather_direct(block, after, *, name):
    def body(x_ref, after_ref, out_ref, send_sems, recv_sems, local_sem):
        x, y, c = _position()
        me = 4 * x + 2 * y + c
        mine = pltpu.make_async_copy(x_ref, out_ref.at[me], local_sem)
        mine.start()
        copies = []
        for k in ALL_PEERS:
            cp = pltpu.make_async_remote_copy(
                src_ref=x_ref, dst_ref=out_ref.at[me], send_sem=send_sems.at[k - 1], recv_sem=recv_sems.at[k - 1],
                device_id=_peer(k, x, y, c), device_id_type=MESH_ID)
            cp.start()
            copies.append(cp)
        for cp in copies:
            cp.wait()
        mine.wait()

    return pl.pallas_call(
        body, out_shape=jax.ShapeDtypeStruct((N_DEV,) + block.shape, block.dtype),
        in_specs=[HBM_SPEC, pl.BlockSpec(memory_space=pl.ANY)], out_specs=HBM_SPEC,
        scratch_shapes=[pltpu.SemaphoreType.DMA((7,)), pltpu.SemaphoreType.DMA((7,)), pltpu.SemaphoreType.DMA],
        name=name)(block, after)


SEM_SPEC = pl.BlockSpec(memory_space=pltpu.SEMAPHORE)
ANY_SPEC = pl.BlockSpec(memory_space=pl.ANY)
DATAFLOW = pltpu.SideEffectType.DATAFLOW_SIDE_EFFECTING


def _peer(k, x, y, c):
    return (1 - x if (k >> 2) & 1 else x, 1 - y if (k >> 1) & 1 else y, 1 - c if k & 1 else c)


def _own_slot_filled(own_block):
    x, y, c = _position()
    zone = lax.empty((N_DEV,) + own_block.shape, own_block.dtype)
    return lax.dynamic_update_slice_in_dim(zone, own_block[None], 4 * x + 2 * y + c, axis=0)


ALL_PEERS = tuple(range(1, N_DEV))
SIBLING_AND_SAME_CORE = (1, 2, 4, 6)
SAME_CORE = (2, 4, 6)


def _split_start(srcs, scatter, after, *, name, relations=ALL_PEERS, carried=None):
    n = len(srcs)
    extra = ([] if after is None else [after]) + ([] if carried is None else [carried])
    n_carried = 0 if carried is None else 1
    x, y, c = _position()
    me = 4 * x + 2 * y + c
    lands = [_own_slot_filled(lax.dynamic_index_in_dim(s, me, 0, keepdims=False) if scatter else s) for s in srcs]

    def body(*refs):
        src_refs, land_refs = refs[:n], refs[n:2 * n]
        send_sems, recv_sems = refs[2 * n + len(extra)], refs[2 * n + len(extra) + 1]
        token = refs[2 * n + len(extra) + 2 + 2 * n]
        bx, by, bc = _position()
        bme = 4 * bx + 2 * by + bc
        for a in range(n):
            for k in relations:
                tx, ty, tc = _peer(k, bx, by, bc)
                src = src_refs[a].at[4 * tx + 2 * ty + tc] if scatter else src_refs[a]
                pltpu.make_async_remote_copy(
                    src_ref=src, dst_ref=land_refs[a].at[bme],
                    send_sem=send_sems.at[7 * a + k - 1], recv_sem=recv_sems.at[7 * a + k - 1],
                    device_id=(tx, ty, tc), device_id_type=MESH_ID).start()
        token[...] = jnp.zeros_like(token)

    hbm = lambda a: pltpu.HBM(a.shape, a.dtype)
    outs = pl.pallas_call(
        body, name=name,
        out_shape=(pltpu.SemaphoreType.DMA((7 * n,)), pltpu.SemaphoreType.DMA((7 * n,)),
                   *[hbm(s) for s in srcs], *[hbm(l) for l in lands], jax.ShapeDtypeStruct((8, 128), F32),
                   *([hbm(carried)] if n_carried else [])),
        in_specs=[HBM_SPEC] * (2 * n) + [ANY_SPEC] * len(extra),
        out_specs=(SEM_SPEC, SEM_SPEC, *[HBM_SPEC] * (2 * n), pl.BlockSpec(memory_space=pltpu.VMEM), *([ANY_SPEC] * n_carried)),
        input_output_aliases={**{i: 2 + i for i in range(2 * n)},
                              **({2 * n + len(extra) - 1: 2 * n + 3} if n_carried else {})},
        compiler_params=pltpu.CompilerParams(has_side_effects=DATAFLOW),
    )(*[pltpu.with_memory_space_constraint(s, pltpu.HBM) for s in srcs],
      *[pltpu.with_memory_space_constraint(l, pltpu.HBM) for l in lands], *extra)
    return {"n": n, "relations": relations, "send": outs[0], "recv": outs[1], "srcs": list(outs[2:2 + n]),
            "lands": list(outs[2 + n:2 + 2 * n]), "token": outs[2 * n + 2], "carried": outs[-1] if n_carried else None}


def _forward_start(lands, carried, *, name):
    n = len(lands)

    def body(*refs):
        land_refs = refs[:n]
        send_sems, recv_sems = refs[n + 1], refs[n + 2]
        bx, by, bc = _position()
        for a in range(n):
            for k in SAME_CORE:
                tx, ty, tc = _peer(k, bx, by, bc)
                block = land_refs[a].at[4 * tx + 2 * ty + tc]
                pltpu.make_async_remote_copy(
                    src_ref=block, dst_ref=block,
                    send_sem=send_sems.at[7 * a + k - 1], recv_sem=recv_sems.at[7 * a + k - 1],
                    device_id=(bx, by, 1 - bc), device_id_type=MESH_ID).start()

    hbm = lambda a: pltpu.HBM(a.shape, a.dtype)
    outs = pl.pallas_call(
        body, name=name,
        out_shape=(pltpu.SemaphoreType.DMA((7 * n,)), pltpu.SemaphoreType.DMA((7 * n,)),
                   *[hbm(l) for l in lands], hbm(carried)),
        in_specs=[HBM_SPEC] * n + [ANY_SPEC],
        out_specs=(SEM_SPEC, SEM_SPEC, *[HBM_SPEC] * n, ANY_SPEC),
        input_output_aliases={i: 2 + i for i in range(n + 1)},
        compiler_params=pltpu.CompilerParams(has_side_effects=DATAFLOW),
    )(*lands, carried)
    handle = {"n": n, "relations": SAME_CORE, "send": outs[0], "recv": outs[1], "srcs": [], "lands": list(outs[2:2 + n])}
    return handle, outs[-1]


def _split_wait(handle, after, *, name):
    n, ns = handle["n"], len(handle["srcs"])

    def body(*refs):
        land_refs = refs[ns:ns + n]
        send_sems, recv_sems = refs[ns + n], refs[ns + n + 1]
        bx, by, bc = _position()
        for a in range(n):
            for k in handle["relations"]:
                block = land_refs[a].at[0]
                cp = pltpu.make_async_remote_copy(
                    src_ref=block, dst_ref=block,
                    send_sem=send_sems.at[7 * a + k - 1], recv_sem=recv_sems.at[7 * a + k - 1],
                    device_id=_peer(k, bx, by, bc), device_id_type=MESH_ID)
                cp.wait_send()
                cp.wait_recv()

    hbm = lambda a: pltpu.HBM(a.shape, a.dtype)
    outs = pl.pallas_call(
        body, name=name,
        out_shape=(*[hbm(s) for s in handle["srcs"]], *[hbm(l) for l in handle["lands"]]),
        in_specs=[HBM_SPEC] * (ns + n) + [SEM_SPEC, SEM_SPEC, ANY_SPEC],
        out_specs=tuple([HBM_SPEC] * (ns + n)),
        input_output_aliases={i: i for i in range(ns + n)},
        compiler_params=pltpu.CompilerParams(has_side_effects=DATAFLOW),
    )(*handle["srcs"], *handle["lands"], handle["send"], handle["recv"], after)
    return list(outs[ns:])


def _sum_sources(parts, *, tr, name):
    n, R, C = parts.shape

    def body(p_ref, o_ref):
        acc = p_ref[0].astype(F32)
        for s in range(1, n):
            acc = acc + p_ref[s].astype(F32)
        o_ref[...] = acc

    return pl.pallas_call(
        body, grid=(R // tr,), in_specs=[pl.BlockSpec((n, tr, C), lambda i: (0, i, 0))],
        out_specs=pl.BlockSpec((tr, C), lambda i: (i, 0)),
        out_shape=jax.ShapeDtypeStruct((R, C), F32), compiler_params=_cp("parallel"), name=name)(parts)


def _adamw_math(g, w, m, v):
    c1 = 1.0 - ADAM_B1 ** ADAM_STEP
    c2 = 1.0 - ADAM_B2 ** ADAM_STEP
    nm = ADAM_B1 * m + (1.0 - ADAM_B1) * g
    nv = ADAM_B2 * v + (1.0 - ADAM_B2) * (g * g)
    return -ADAM_LR * ((nm / c1) / (jnp.sqrt(nv / c2) + ADAM_EPS) + ADAM_WD * w), nm, nv


ADAMW_STRIP = 16


def _reduce_adamw(received, w, m, v, *, tr, name):
    L, R, C = w.shape

    def body(*refs):
        p_refs = refs[:L]
        w_ref, m_ref, v_ref, g_ref, d_ref, nm_ref, nv_ref = refs[L:]
        for l in range(L):
            @pl.when(pl.program_id(0) == l)
            def _(p_ref=p_refs[l]):
                def strip(i, carry):
                    rows = pl.ds(pl.multiple_of(i * ADAMW_STRIP, ADAMW_STRIP), ADAMW_STRIP)
                    acc = p_ref[0, rows, :].astype(F32)
                    for s in range(1, N_DEV):
                        acc = acc + p_ref[s, rows, :].astype(F32)
                    g_ref[rows, :] = acc
                    d_ref[rows, :], nm_ref[rows, :], nv_ref[rows, :] = _adamw_math(acc, w_ref[rows, :], m_ref[rows, :], v_ref[rows, :])
                    return carry

                lax.fori_loop(0, tr // ADAMW_STRIP, strip, 0)

    p_spec = pl.BlockSpec((N_DEV, tr, C), lambda l, i: (0, i, 0))
    blk = pl.BlockSpec((None, tr, C), lambda l, i: (l, i, 0))
    sh = jax.ShapeDtypeStruct((L, R, C), F32)
    return pl.pallas_call(
        body, grid=(L, R // tr), in_specs=[p_spec] * L + [blk] * 3, out_specs=[blk] * 4, out_shape=[sh] * 4,
        compiler_params=_cp("parallel", "parallel"), name=name)(*received, w, m, v)


def _adamw(g, w, m, v, *, tr, name):
    L, R, C = w.shape

    def body(g_ref, w_ref, m_ref, v_ref, d_ref, nm_ref, nv_ref):
        d_ref[...], nm_ref[...], nv_ref[...] = _adamw_math(g_ref[...], w_ref[...], m_ref[...], v_ref[...])

    blk = pl.BlockSpec((None, tr, C), lambda l, i: (l, i, 0))
    sh = jax.ShapeDtypeStruct((L, R, C), F32)
    return pl.pallas_call(
        body, grid=(L, R // tr), in_specs=[blk] * 4, out_specs=[blk] * 3, out_shape=[sh] * 3,
        compiler_params=_cp("parallel", "parallel"), name=name)(g, w, m, v)


UNITS = {
    "a_w_in": ("a_w_in", 0, True), "w_mem_kv0": ("w_mem_kv", 0, False), "w_out0": ("w_out", 0, False),
    "w_gate_up0": ("w_gate_up", 0, True), "w_down0": ("w_down", 0, False), "w_kv": ("w_kv", None, True),
    "b_w_in": ("b_w_in", 0, True), "w_mem_kv1": ("w_mem_kv", 1, False), "w_out1": ("w_out", 1, False),
    "w_gate_up1": ("w_gate_up", 1, True), "w_down1": ("w_down", 1, False),
}
BIG = ("a_w_in", "b_w_in", "w_kv", "w_mem_kv", "w_out", "w_gate_up", "w_down")
ADAMW_ROW_TILE = {"a_w_in": 208, "b_w_in": 160, "w_kv": 192, "w_mem_kv": 128, "w_out": 128, "w_gate_up": 352, "w_down": 352}


def _wire_block(weights, unit):
    name, layer, col = UNITS[unit]
    a = weights[name] if layer is None else weights[name][layer]
    return (a.T if col else a).astype(BF16)


SMALL_REPLICATED = ("norm_mix", "norm_ffn", "b_qnorm", "kv_norm", "b_knorm", "mem_norm", "mem_qnorm", "mem_knorm")
SMALL_SHARDED = ("a_lb_logits", "a_onorm")
SMALL_ORDER = SMALL_REPLICATED + SMALL_SHARDED
LANES = 128


def _prod(shape):
    n = 1
    for s in shape:
        n *= s
    return n


def _pack_flat(arrays, rows, cols, dtype):
    flat = jnp.concatenate([a.reshape(-1).astype(dtype) for a in arrays])
    return jnp.pad(flat, (0, rows * cols - flat.shape[0])).reshape(rows, cols)


def _unpack_flat(packed, shapes):
    flat = packed.reshape(-1)
    out, off = [], 0
    for s in shapes:
        out.append(flat[off:off + _prod(s)].reshape(s))
        off += _prod(s)
    return out


def kernel(x, mem, norm_mix, norm_ffn, a_w_in, a_lb_logits, a_onorm, b_w_in, b_qnorm, kv_norm, w_kv, b_knorm, mem_norm, w_mem_kv, mem_qnorm, mem_knorm, w_out, w_gate_up, w_down, loss_target, m_norm_mix, m_norm_ffn, m_a_w_in, m_a_lb_logits, m_a_onorm, m_b_w_in, m_b_qnorm, m_kv_norm, m_w_kv, m_b_knorm, m_mem_norm, m_w_mem_kv, m_mem_qnorm, m_mem_knorm, m_w_out, m_w_gate_up, m_w_down, v_norm_mix, v_norm_ffn, v_a_w_in, v_a_lb_logits, v_a_onorm, v_b_w_in, v_b_qnorm, v_kv_norm, v_w_kv, v_b_knorm, v_mem_norm, v_w_mem_kv, v_mem_qnorm, v_mem_knorm, v_w_out, v_w_gate_up, v_w_down):
    names = ("norm_mix", "norm_ffn", "a_w_in", "a_lb_logits", "a_onorm", "b_w_in", "b_qnorm", "kv_norm", "w_kv", "b_knorm",
             "mem_norm", "w_mem_kv", "mem_qnorm", "mem_knorm", "w_out", "w_gate_up", "w_down")
    w = dict(zip(names, (norm_mix, norm_ffn, a_w_in, a_lb_logits, a_onorm, b_w_in, b_qnorm, kv_norm, w_kv, b_knorm,
                         mem_norm, w_mem_kv, mem_qnorm, mem_knorm, w_out, w_gate_up, w_down)))
    m = dict(zip(names, (m_norm_mix, m_norm_ffn, m_a_w_in, m_a_lb_logits, m_a_onorm, m_b_w_in, m_b_qnorm, m_kv_norm, m_w_kv,
                         m_b_knorm, m_mem_norm, m_w_mem_kv, m_mem_qnorm, m_mem_knorm, m_w_out, m_w_gate_up, m_w_down)))
    v = dict(zip(names, (v_norm_mix, v_norm_ffn, v_a_w_in, v_a_lb_logits, v_a_onorm, v_b_w_in, v_b_qnorm, v_kv_norm, v_w_kv,
                         v_b_knorm, v_mem_norm, v_w_mem_kv, v_mem_qnorm, v_mem_knorm, v_w_out, v_w_gate_up, v_w_down)))

    first = ["a_w_in", "w_mem_kv0"]
    later = [["w_out0", "w_gate_up0"], ["w_down0", "w_kv"], ["b_w_in", "w_mem_kv1"], ["w_out1", "w_gate_up1", "w_down1"]]
    first_half, second_half = {}, {}

    def start_first_half(i, after, carried=None):
        first_half[i] = _split_start([_wire_block(w, u) for u in later[i]], False, after, name=f"gather{i}_start",
                                     relations=SIBLING_AND_SAME_CORE, carried=carried)
        return first_half[i]

    opening = _split_start([_wire_block(w, u) for u in first] + [_pack_flat([a_lb_logits, a_onorm], 8, LANES, F32)],
                           False, None, name="gather_first_start", relations=SIBLING_AND_SAME_CORE)
    token = start_first_half(0, opening["token"])["token"]
    token = start_first_half(1, token)["token"]
    opening, token = _forward_start(_split_wait(opening, token, name="gather_first_landed"), token, name="gather_first_forward")
    gathered = _split_wait(opening, token, name="gather_first_wait")
    full = {u: g.reshape(-1, g.shape[-1]) for u, g in zip(first, gathered)}
    small_in = gathered[-1].reshape(N_DEV, -1)
    P = {n: w[n] for n in SMALL_REPLICATED}
    P["a_lb_logits"] = small_in[:, :192].reshape(N_DEV, 2, 96).transpose(1, 0, 2).reshape(2, A_WIDTH)
    P["a_onorm"] = small_in[:, 192:288].reshape(1, A_WIDTH)

    def forward_point(i, value):
        landed = _split_wait(first_half[i], value, name=f"gather{i}_landed")
        second_half[i], value = _forward_start(landed, value, name=f"gather{i}_forward")
        if i + 2 < len(later):
            value = start_first_half(i + 2, None, carried=value)["carried"]
        return value

    def get_w(unit, after):
        if unit not in full:
            i = [unit in group for group in later].index(True)
            for u, land in zip(later[i], _split_wait(second_half[i], after, name=f"gather{i}_wait")):
                full[u] = land.reshape(-1, land.shape[-1])
        return full[unit]

    sent = []

    def put_g(group):
        units = list(group)
        handle = _split_start([group[u].reshape(N_DEV, -1, group[u].shape[-1]) for u in units], True, None,
                              name=f"scatter{len(sent)}_start")
        sent.append((units, handle))
        return handle["token"]

    sq, gx, gP = _local_step(x[0], mem[0], loss_target[0], get_w, P, put_g, forward_point=forward_point)
    loss_here = (0.5 * jnp.sum(sq) / D_MODEL).reshape(1)

    received = {}
    group_of = {u: i for i, (units, _) in enumerate(sent) for u in units}
    out = {"grad": {}, "delta": {}, "new_m": {}, "new_v": {}}
    newest = [gx]

    def update_big(n):
        shape = w[n].shape
        as3 = lambda a: a.reshape((-1,) + shape[-2:])
        mine = [u for u, (wn, _, _) in UNITS.items() if wn == n]
        for i in sorted({group_of[u] for u in mine}):
            if sent[i][0][0] not in received:
                received.update(zip(sent[i][0], _split_wait(sent[i][1], newest[0], name=f"scatter{i}_wait")))
        flip = (lambda a: jnp.swapaxes(a, 1, 2)) if UNITS[mine[0]][2] else (lambda a: a)
        res = _reduce_adamw([received[u] for u in mine], flip(as3(w[n])), flip(as3(m[n])), flip(as3(v[n])),
                            tr=ADAMW_ROW_TILE[n], name=f"adamw_{n}")
        newest[0] = res[1]
        for kind, r in zip(("grad", "delta", "new_m", "new_v"), res):
            out[kind][n] = flip(r).reshape(shape)

    for n in ("w_down", "w_gate_up", "w_out", "w_mem_kv", "b_w_in", "w_kv"):
        update_big(n)

    full_shapes = [(2, A_WIDTH) if n == "a_lb_logits" else (1, A_WIDTH) if n == "a_onorm" else w[n].shape for n in SMALL_ORDER]
    n_small = sum(_prod(s) for s in full_shapes) + 1
    rows_small = -(-n_small // (8 * LANES)) * 8
    g_all = _all_gather_direct(_pack_flat([gP[n] for n in SMALL_ORDER] + [loss_here], rows_small, LANES, F32),
                               newest[0], name="gather_small_grads")
    summed = _unpack_flat(_sum_sources(g_all, tr=rows_small, name="sum_small_grads"), full_shapes + [(1,)])
    g_small = dict(zip(SMALL_ORDER, summed))
    loss = summed[-1].reshape(())
    me = 4 * lax.axis_index("x") + 2 * lax.axis_index("y") + lax.axis_index("c")
    for n in SMALL_SHARDED:
        g_small[n] = lax.dynamic_slice_in_dim(g_small[n], me * 96, 96, axis=1)
    shapes = [w[n].shape for n in SMALL_ORDER]
    rows_upd = -(-sum(_prod(s) for s in shapes) // (8 * LANES)) * 8
    pk = lambda d: _pack_flat([d[n] for n in SMALL_ORDER], rows_upd, LANES, F32)
    res = _adamw(pk(g_small)[None], pk(w)[None], pk(m)[None], pk(v)[None], tr=rows_upd, name="adamw_small")
    out["grad"].update(g_small)
    for kind, packed in zip(("delta", "new_m", "new_v"), res):
        out[kind].update(zip(SMALL_ORDER, _unpack_flat(packed[0], shapes)))
    newest[0] = res[0]
    update_big("a_w_in")

    return (loss, gx[None], *[out["grad"][n] for n in names], *[out["delta"][n] for n in names],
            *[out["new_m"][n] for n in names], *[out["new_v"][n] for n in names])
```

```python
import functools

import jax
import jax.numpy as jnp
import numpy as np
from jax import lax
from jax.experimental import pallas as pl
from jax.experimental.pallas import tpu as pltpu

F32 = jnp.float32
BF16 = jnp.bfloat16

N_DEV = 8
D_MODEL = 1024
HEAD_DIM = 128
A_HEADS = 6
A_WIDTH = A_HEADS * HEAD_DIM
CHUNK = 64
B_HEADS = 6
B_WIDTH = B_HEADS * HEAD_DIM
DILATIONS = (1, 4, 16)
SPAN = 128
N_GROUPS = 3
ROPE_THETA = 10000.0
MEM_TOKENS = 256
MEM_HEADS = 4
MEM_HEAD_DIM = 64
MEM_WIDTH = MEM_HEADS * MEM_HEAD_DIM
FFN_HIDDEN = 2816
EPS = 1e-6

ADAM_LR = 0.001
ADAM_B1 = 0.9
ADAM_B2 = 0.999
ADAM_EPS = 1e-08
ADAM_WD = 0.01
ADAM_STEP = 10

V7X_VMEM_LIMIT_BYTES = 56 * 1024 * 1024

NT_DIMS = (((1,), (1,)), ((), ()))
TN_DIMS = (((0,), (0,)), ((), ()))


def _cp(*sem):
    return pltpu.CompilerParams(dimension_semantics=sem, vmem_limit_bytes=V7X_VMEM_LIMIT_BYTES)


def _dot(a, b):
    return jnp.dot(a.astype(BF16), b.astype(BF16), preferred_element_type=F32)


def _dot_nt(a, b):
    return lax.dot_general(a.astype(BF16), b.astype(BF16), NT_DIMS, preferred_element_type=F32)


def _dot_tn(a, b):
    return lax.dot_general(a.astype(BF16), b.astype(BF16), TN_DIMS, preferred_element_type=F32)


def _dot3(m01, x):
    hi = x.astype(BF16)
    r1 = x - hi.astype(F32)
    mid = r1.astype(BF16)
    lo = (r1 - mid.astype(F32)).astype(BF16)
    d = functools.partial(jnp.dot, preferred_element_type=F32)
    return d(m01, hi) + d(m01, mid) + d(m01, lo)


def _sigmoid(x):
    return 0.5 * jnp.tanh(0.5 * x) + 0.5


def _full(shape):
    return pl.BlockSpec(shape, lambda *_: (0,) * len(shape))


def _dep(body, n_in, dep):
    if dep is None:
        return body, [], []

    def with_dep(*refs):
        return body(*refs[:n_in], *refs[n_in + 1:])

    return with_dep, [pl.BlockSpec(memory_space=pl.ANY)], [dep]


def _rms_matmul(x, g, w, *, tt, tn, wt, name, out_dtype=F32, dep=None, rotate=None):
    T, K = x.shape
    N = w.shape[0] if wt else w.shape[1]
    n_rot = 0 if rotate is None else rotate[0].shape[1] // HEAD_DIM
    extra_in = [] if rotate is None else list(rotate)

    def kernel_body(x_ref, g_ref, w_ref, *rest):
        y_ref, xn_ref = rest[len(extra_in)], rest[len(extra_in) + 1]
        xf = x_ref[...]
        r = lax.rsqrt(jnp.mean(xf * xf, axis=-1, keepdims=True) + EPS)
        xn = (xf * r * g_ref[...]).astype(BF16)
        xn_ref[...] = xn
        for j in range(N // tn):
            cols = slice(j * tn, (j + 1) * tn)
            y = _dot_nt(xn, w_ref[cols, :]) if wt else _dot(xn, w_ref[:, cols])
            y_ref[:, cols] = y.astype(out_dtype)
            for h in range(j * tn // HEAD_DIM, min((j + 1) * tn // HEAD_DIM, n_rot)):
                gw_ref, c_ref, s_ref, yr_ref = rest[0], rest[1], rest[2], rest[len(extra_in) + 2]
                sl = slice(h * HEAD_DIM, (h + 1) * HEAD_DIM)
                xhat, _ = _head_rms(y[:, h * HEAD_DIM - j * tn:(h + 1) * HEAD_DIM - j * tn])
                yr_ref[:, sl] = _rope(xhat * gw_ref[:, sl], c_ref[...], s_ref[...])

    tbl = pl.BlockSpec((tt, HEAD_DIM), lambda i: (i, 0))
    rot_specs = [] if rotate is None else [_full((1, n_rot * HEAD_DIM)), tbl, tbl]
    body, dep_specs, dep_args = _dep(kernel_body, 3 + len(extra_in), dep)
    return pl.pallas_call(
        body, grid=(T // tt,),
        in_specs=[pl.BlockSpec((tt, K), lambda i: (i, 0)), _full((1, K)), _full(w.shape)] + rot_specs + dep_specs,
        out_specs=[pl.BlockSpec((tt, N), lambda i: (i, 0)), pl.BlockSpec((tt, K), lambda i: (i, 0))]
        + ([] if rotate is None else [pl.BlockSpec((tt, n_rot * HEAD_DIM), lambda i: (i, 0))]),
        out_shape=[jax.ShapeDtypeStruct((T, N), out_dtype), jax.ShapeDtypeStruct((T, K), BF16)]
        + ([] if rotate is None else [jax.ShapeDtypeStruct((T, n_rot * HEAD_DIM), F32)]),
        compiler_params=_cp("parallel"), name=name)(x, g, w, *extra_in, *dep_args)


def _mm_res(res, a1, a2, w, *, tt, name):
    T, K1 = a1.shape
    K2 = a2.shape[1]
    N = w.shape[1]

    def body(r_ref, a1_ref, a2_ref, w_ref, o_ref):
        o_ref[...] = r_ref[...] + _dot(a1_ref[...], w_ref[:K1, :]) + _dot(a2_ref[...], w_ref[K1:, :])

    return pl.pallas_call(
        body, grid=(T // tt,),
        in_specs=[pl.BlockSpec((tt, N), lambda i: (i, 0)), pl.BlockSpec((tt, K1), lambda i: (i, 0)),
                  pl.BlockSpec((tt, K2), lambda i: (i, 0)), _full((K1 + K2, N))],
        out_specs=pl.BlockSpec((tt, N), lambda i: (i, 0)),
        out_shape=jax.ShapeDtypeStruct((T, N), F32),
        compiler_params=_cp("parallel"), name=name)(res, a1, a2, w)


def _swiglu_down(h, gu, wd, *, tt, name):
    T, D = h.shape
    Fh = wd.shape[0]

    def body(h_ref, gt_ref, up_ref, w_ref, o_ref):
        gt = gt_ref[...].astype(F32)
        act = gt * _sigmoid(gt) * up_ref[...].astype(F32)
        o_ref[...] = h_ref[...] + _dot(act, w_ref[...])

    return pl.pallas_call(
        body, grid=(T // tt,),
        in_specs=[pl.BlockSpec((tt, D), lambda i: (i, 0)), pl.BlockSpec((tt, Fh), lambda i: (i, 0)),
                  pl.BlockSpec((tt, Fh), lambda i: (i, 1)), _full((Fh, D))],
        out_specs=pl.BlockSpec((tt, D), lambda i: (i, 0)),
        out_shape=jax.ShapeDtypeStruct((T, D), F32),
        compiler_params=_cp("parallel"), name=name)(h, gu, gu, wd)


def _swiglu_down_loss(h, gu, wd, tgt, *, tt, name):
    T, D = h.shape
    Fh = wd.shape[0]

    def body(h_ref, gt_ref, up_ref, w_ref, t_ref, dy_ref, acc_ref):
        @pl.when(pl.program_id(0) == 0)
        def _():
            acc_ref[...] = jnp.zeros_like(acc_ref)

        gt = gt_ref[...].astype(F32)
        act = gt * _sigmoid(gt) * up_ref[...].astype(F32)
        e = h_ref[...] + _dot(act, w_ref[...]) - t_ref[...]
        dy_ref[...] = e * (1.0 / D)
        acc_ref[...] += jnp.sum(e * e, axis=0, keepdims=True)

    row = pl.BlockSpec((tt, D), lambda i: (i, 0))
    return pl.pallas_call(
        body, grid=(T // tt,),
        in_specs=[row, pl.BlockSpec((tt, Fh), lambda i: (i, 0)), pl.BlockSpec((tt, Fh), lambda i: (i, 1)), _full((Fh, D)), row],
        out_specs=[row, _full((1, D))],
        out_shape=[jax.ShapeDtypeStruct((T, D), F32), jax.ShapeDtypeStruct((1, D), F32)],
        compiler_params=_cp("arbitrary"), name=name)(h, gu, gu, wd, tgt)


SWIGLU_COLS = 256


def _swiglu_bwd(dh, gu, wd, *, tt, name):
    T, D = dh.shape
    Fh = wd.shape[0]
    last = T // tt - 1

    def body(dh_ref, gt_ref, up_ref, w_ref, dgu_ref, gw_ref, acc):
        @pl.when(pl.program_id(0) == 0)
        def _():
            acc[...] = jnp.zeros_like(acc)

        dh16 = dh_ref[...].astype(BF16)
        for c0 in range(0, Fh, SWIGLU_COLS):
            cols = slice(c0, c0 + SWIGLU_COLS)
            gt = gt_ref[:, cols].astype(F32)
            up = up_ref[:, cols].astype(F32)
            s = _sigmoid(gt)
            silu = gt * s
            dact = _dot_nt(dh16, w_ref[cols, :])
            acc[cols, :] += _dot_tn((silu * up).astype(BF16), dh16)
            dgu_ref[:, cols] = (dact * up * (s * (1.0 + gt * (1.0 - s)))).astype(BF16)
            dgu_ref[:, Fh + c0:Fh + c0 + SWIGLU_COLS] = (dact * silu).astype(BF16)

        @pl.when(pl.program_id(0) == last)
        def _():
            gw_ref[...] = acc[...].astype(BF16)

    return pl.pallas_call(
        body, grid=(T // tt,),
        in_specs=[pl.BlockSpec((tt, D), lambda i: (i, 0)), pl.BlockSpec((tt, Fh), lambda i: (i, 0)),
                  pl.BlockSpec((tt, Fh), lambda i: (i, 1)), _full((Fh, D))],
        out_specs=[pl.BlockSpec((tt, 2 * Fh), lambda i: (i, 0)), _full((Fh, D))],
        out_shape=[jax.ShapeDtypeStruct((T, 2 * Fh), BF16), jax.ShapeDtypeStruct((Fh, D), BF16)],
        scratch_shapes=[pltpu.VMEM((Fh, D), F32)],
        compiler_params=_cp("arbitrary"), name=name)(dh, gu, gu, wd)


def _out_proj_bwd(dy, a1, a2, w, *, tt, name, head_dots=False):
    T, N = dy.shape
    K1, K2 = a1.shape[1], a2.shape[1]
    K = K1 + K2
    last = T // tt - 1

    def body(dy_ref, a1_ref, a2_ref, w_ref, da_ref, gw_ref, *rest):
        acc = rest[-1]

        @pl.when(pl.program_id(0) == 0)
        def _():
            acc[...] = jnp.zeros_like(acc)

        dy16 = dy_ref[...].astype(BF16)
        da = _dot_nt(dy16, w_ref[...])
        da_ref[...] = da
        acc[:K1, :] += _dot_tn(a1_ref[...], dy16)
        acc[K1:, :] += _dot_tn(a2_ref[...], dy16)
        if head_dots:
            for h in range(K1 // HEAD_DIM):
                sl = slice(h * HEAD_DIM, (h + 1) * HEAD_DIM)
                rest[0][:, sl] = jnp.broadcast_to(jnp.sum(da[:, sl] * a1_ref[:, sl], axis=-1, keepdims=True), (tt, HEAD_DIM))

        @pl.when(pl.program_id(0) == last)
        def _():
            gw_ref[...] = acc[...].astype(BF16)

    extra_specs = [pl.BlockSpec((tt, K1), lambda i: (i, 0))] if head_dots else []
    extra_shapes = [jax.ShapeDtypeStruct((T, K1), F32)] if head_dots else []
    return pl.pallas_call(
        body, grid=(T // tt,),
        in_specs=[pl.BlockSpec((tt, N), lambda i: (i, 0)), pl.BlockSpec((tt, K1), lambda i: (i, 0)),
                  pl.BlockSpec((tt, K2), lambda i: (i, 0)), _full((K, N))],
        out_specs=[pl.BlockSpec((tt, K), lambda i: (i, 0)), _full((K, N))] + extra_specs,
        out_shape=[jax.ShapeDtypeStruct((T, K), F32), jax.ShapeDtypeStruct((K, N), BF16)] + extra_shapes,
        scratch_shapes=[pltpu.VMEM((K, N), F32)],
        compiler_params=_cp("arbitrary"), name=name)(dy, a1, a2, w)


def _mm_tn(a, b, *, tt, tka, name):
    T, Ka = a.shape
    N = b.shape[1]
    last = T // tt - 1

    def body(a_ref, b_ref, o_ref, acc):
        @pl.when(pl.program_id(1) == 0)
        def _():
            acc[...] = jnp.zeros_like(acc)

        acc[...] += _dot_tn(a_ref[...], b_ref[...])

        @pl.when(pl.program_id(1) == last)
        def _():
            o_ref[...] = acc[...].astype(BF16)

    return pl.pallas_call(
        body, grid=(Ka // tka, T // tt),
        in_specs=[pl.BlockSpec((tt, tka), lambda j, t: (t, j)), pl.BlockSpec((tt, N), lambda j, t: (t, 0))],
        out_specs=pl.BlockSpec((tka, N), lambda j, t: (j, 0)),
        out_shape=jax.ShapeDtypeStruct((Ka, N), BF16),
        scratch_shapes=[pltpu.VMEM((tka, N), F32)],
        compiler_params=_cp("parallel", "arbitrary"), name=name)(a, b)


def _mm_tn_pieces(pieces, b, *, tt, name):
    n = len(pieces)
    T = b.shape[0]
    N = b.shape[1]
    widths = [p.shape[1] for p in pieces]
    Ka = sum(widths)
    last = T // tt - 1

    def body(*refs):
        p_refs = refs[:n]
        b_ref, o_ref, acc = refs[n:]

        @pl.when(pl.program_id(0) == 0)
        def _():
            acc[...] = jnp.zeros_like(acc)

        bv = b_ref[...].astype(BF16)
        off = 0
        for p_ref, wd in zip(p_refs, widths):
            acc[off:off + wd, :] += _dot_tn(p_ref[...], bv)
            off += wd

        @pl.when(pl.program_id(0) == last)
        def _():
            o_ref[...] = acc[...].astype(BF16)

    return pl.pallas_call(
        body, grid=(T // tt,),
        in_specs=[pl.BlockSpec((tt, wd), lambda t: (t, 0)) for wd in widths] + [pl.BlockSpec((tt, N), lambda t: (t, 0))],
        out_specs=_full((Ka, N)), out_shape=jax.ShapeDtypeStruct((Ka, N), BF16),
        scratch_shapes=[pltpu.VMEM((Ka, N), F32)],
        compiler_params=_cp("arbitrary"), name=name)(*pieces, b)


def _rms_bwd_dx(x, g, w, dy, dres, *, tt, wt, name, dep=None):
    pieces = list(dy) if isinstance(dy, (list, tuple)) else [dy]
    n = len(pieces)
    widths = [p.shape[1] for p in pieces]
    T, K = x.shape

    def kernel_body(x_ref, g_ref, w_ref, *rest):
        dy_refs = rest[:n]
        dres_ref, dx_ref, dg_ref = rest[n:]

        @pl.when(pl.program_id(0) == 0)
        def _():
            dg_ref[...] = jnp.zeros_like(dg_ref)

        if n == 1:
            dxn = (_dot if wt else _dot_nt)(dy_refs[0][...], w_ref[...])
        else:
            dxn, off = 0.0, 0
            for dy_ref, wd in zip(dy_refs, widths):
                dxn = dxn + _dot(dy_ref[...], w_ref[off:off + wd, :])
                off += wd
        xf = x_ref[...]
        r = lax.rsqrt(jnp.mean(xf * xf, axis=-1, keepdims=True) + EPS)
        xhat = xf * r
        dg_ref[...] += jnp.sum(dxn * xhat, axis=0, keepdims=True)
        dxhat = dxn * g_ref[...]
        dx_ref[...] = dres_ref[...] + r * (dxhat - xhat * jnp.mean(dxhat * xhat, axis=-1, keepdims=True))

    assert n == 1 or wt
    body, dep_specs, dep_args = _dep(kernel_body, 4 + n, dep)
    return pl.pallas_call(
        body, grid=(T // tt,),
        in_specs=[pl.BlockSpec((tt, K), lambda i: (i, 0)), _full((1, K)), _full(w.shape)]
        + [pl.BlockSpec((tt, wd), lambda i: (i, 0)) for wd in widths]
        + [pl.BlockSpec((tt, K), lambda i: (i, 0))] + dep_specs,
        out_specs=[pl.BlockSpec((tt, K), lambda i: (i, 0)), _full((1, K))],
        out_shape=[jax.ShapeDtypeStruct((T, K), F32), jax.ShapeDtypeStruct((1, K), F32)],
        compiler_params=_cp("arbitrary"), name=name)(x, g, w, *pieces, dres, *dep_args)


HGRN_TB = 512
HGRN_NCH = HGRN_TB // CHUNK
HGRN_UNROLL = 8
HGRN_HPB = 6


def _hgrn_chunk_fwd(q, z, lbv, tril01):
    sig = _sigmoid(z)
    f = lbv + (1.0 - lbv) * sig
    kk = 1.0 - f
    b = _dot3(tril01, jnp.log(f))
    bend = b[CHUNK - 1:CHUNK, :]
    sq = _sigmoid(q)
    eb = jnp.exp(b)
    emb = jnp.exp(-b)
    eo = jnp.exp(bend - b)
    dec = jnp.exp(bend)
    return sig, f, kk, sq, eb, emb, eo, dec


def _hgrn2_fwd(proj, lb, *, name):
    T = proj.shape[0]
    nT = T // HGRN_TB
    nC = T // CHUNK

    def body(q_ref, z_ref, v_ref, lb_ref, o_ref, st_ref, state):
        @pl.when(pl.program_id(1) == 0)
        def _():
            state[...] = jnp.zeros_like(state)

        row = lax.broadcasted_iota(jnp.int32, (CHUNK, CHUNK), 0)
        col = lax.broadcasted_iota(jnp.int32, (CHUNK, CHUNK), 1)
        causal = row >= col
        tril01 = causal.astype(BF16)

        def chunk(c, carry):
            rows = pl.ds(pl.multiple_of(c * CHUNK, CHUNK), CHUNK)
            for hh in range(HGRN_HPB):
                sl = slice(hh * HEAD_DIM, (hh + 1) * HEAD_DIM)
                q = q_ref[rows, sl]
                v = v_ref[rows, sl].astype(BF16)
                sig, f, kk, sq, eb, emb, eo, dec = _hgrn_chunk_fwd(q, z_ref[rows, sl], lb_ref[:, sl], tril01)
                qi = (q * sq * eb).astype(BF16)
                ki = (kk * emb).astype(BF16)
                ko = (kk * eo).astype(BF16)
                st = state[hh]
                att = jnp.where(causal, _dot_nt(qi, ki), 0.0)
                o_ref[rows, sl] = _dot(att, v) + _dot_nt(qi, st)
                st_ref[c, hh] = st
                state[hh] = st * dec + _dot_tn(v, ko)
            return carry

        lax.fori_loop(0, HGRN_NCH, chunk, 0, unroll=HGRN_UNROLL)

    W = HGRN_HPB * HEAD_DIM
    nG = A_HEADS // HGRN_HPB
    hb = lambda off: pl.BlockSpec((HGRN_TB, W), lambda h, i: (i, off + h))
    return pl.pallas_call(
        body, grid=(nG, nT),
        in_specs=[hb(0), hb(nG), hb(2 * nG), pl.BlockSpec((1, W), lambda h, i: (0, h))],
        out_specs=[hb(0), pl.BlockSpec((HGRN_NCH, HGRN_HPB, HEAD_DIM, HEAD_DIM), lambda h, i: (i, h, 0, 0))],
        out_shape=[jax.ShapeDtypeStruct((T, A_WIDTH), F32), jax.ShapeDtypeStruct((nC, A_HEADS, HEAD_DIM, HEAD_DIM), F32)],
        scratch_shapes=[pltpu.VMEM((HGRN_HPB, HEAD_DIM, HEAD_DIM), F32)],
        compiler_params=_cp("parallel", "arbitrary"), name=name)(proj, proj, proj, lb)


def _hgrn2_bwd(proj, lb, st_all, do, *, name):
    T = proj.shape[0]
    nT = T // HGRN_TB

    def body(q_ref, z_ref, v_ref, lb_ref, st_ref, do_ref, dq_ref, dz_ref, dv_ref, dlb_ref, dstate):
        @pl.when(pl.program_id(1) == 0)
        def _():
            dstate[...] = jnp.zeros_like(dstate)
            dlb_ref[...] = jnp.zeros_like(dlb_ref)

        row = lax.broadcasted_iota(jnp.int32, (CHUNK, CHUNK), 0)
        col = lax.broadcasted_iota(jnp.int32, (CHUNK, CHUNK), 1)
        causal = row >= col
        tril01 = causal.astype(BF16)
        triu01 = (row <= col).astype(BF16)

        def chunk(cc, carry):
            c = HGRN_NCH - 1 - cc
            rows = pl.ds(pl.multiple_of(c * CHUNK, CHUNK), CHUNK)
            for hh in range(HGRN_HPB):
                sl = slice(hh * HEAD_DIM, (hh + 1) * HEAD_DIM)
                lbv = lb_ref[:, sl]
                q = q_ref[rows, sl]
                v = v_ref[rows, sl].astype(BF16)
                sig, f, kk, sq, eb, emb, eo, dec = _hgrn_chunk_fwd(q, z_ref[rows, sl], lbv, tril01)
                qi32 = q * sq * eb
                ki32 = kk * emb
                ko32 = kk * eo
                qi, ki, ko = qi32.astype(BF16), ki32.astype(BF16), ko32.astype(BF16)
                att = jnp.where(causal, _dot_nt(qi, ki), 0.0).astype(BF16)
                dout = do_ref[rows, sl].astype(BF16)
                st = st_ref[c, hh]
                dst = dstate[hh]
                dst16 = dst.astype(BF16)
                datt = jnp.where(causal, _dot_nt(dout, v), 0.0).astype(BF16)
                dqi = _dot(datt, ki) + _dot(dout, st)
                dki = _dot_tn(datt, qi)
                dv_ref[rows, sl] = (_dot_tn(att, dout) + _dot_nt(ko, dst16)).astype(BF16)
                dko = _dot(v, dst16)
                ddec = jnp.sum(dst * st, axis=0, keepdims=True)
                dstate[hh] = dst * dec + _dot_tn(dout, qi)
                dkk = dki * emb + dko * eo
                db = dqi * qi32 - dki * ki32 - dko * ko32
                dbend = jnp.sum(dko * ko32, axis=0, keepdims=True) + ddec * dec
                dlogf = _dot3(triu01, db) + dbend
                df = dlogf / f - dkk
                dz_ref[rows, sl] = (df * (1.0 - lbv) * sig * (1.0 - sig)).astype(BF16)
                dlb_ref[:, sl] += jnp.sum(df * (1.0 - sig), axis=0, keepdims=True)
                dq_ref[rows, sl] = (dqi * eb * (sq * (1.0 + q * (1.0 - sq)))).astype(BF16)
            return carry

        lax.fori_loop(0, HGRN_NCH, chunk, 0, unroll=HGRN_UNROLL)

    W = HGRN_HPB * HEAD_DIM
    nG = A_HEADS // HGRN_HPB
    hb = lambda off: pl.BlockSpec((HGRN_TB, W), lambda h, i: (nT - 1 - i, off + h))
    hlb = pl.BlockSpec((1, W), lambda h, i: (0, h))
    o16 = jax.ShapeDtypeStruct((T, A_WIDTH), BF16)
    return pl.pallas_call(
        body, grid=(nG, nT),
        in_specs=[hb(0), hb(nG), hb(2 * nG), hlb,
                  pl.BlockSpec((HGRN_NCH, HGRN_HPB, HEAD_DIM, HEAD_DIM), lambda h, i: (nT - 1 - i, h, 0, 0)), hb(0)],
        out_specs=[hb(0), hb(0), hb(0), hlb],
        out_shape=[o16, o16, o16, jax.ShapeDtypeStruct((1, A_WIDTH), F32)],
        scratch_shapes=[pltpu.VMEM((HGRN_HPB, HEAD_DIM, HEAD_DIM), F32)],
        compiler_params=_cp("parallel", "arbitrary"), name=name)(proj, proj, proj, lb, st_all, do)


def _head_rms(x):
    r = lax.rsqrt(jnp.mean(x * x, axis=-1, keepdims=True) + EPS)
    return x * r, r


def _head_rms_bwd(dxhat, xhat, r):
    return r * (dxhat - xhat * jnp.mean(dxhat * xhat, axis=-1, keepdims=True))


def _a_post_fwd(o, proj, onorm, *, tt, name):
    T = o.shape[0]

    def body(o_ref, g_ref, w_ref, y_ref):
        for h in range(A_HEADS):
            sl = slice(h * HEAD_DIM, (h + 1) * HEAD_DIM)
            xhat, _ = _head_rms(o_ref[:, sl])
            g = g_ref[:, sl]
            y_ref[:, sl] = xhat * w_ref[:, sl] * (g * _sigmoid(g))

    blk = lambda c: pl.BlockSpec((tt, A_WIDTH), lambda i: (i, c))
    return pl.pallas_call(
        body, grid=(T // tt,), in_specs=[blk(0), blk(3), _full((1, A_WIDTH))], out_specs=blk(0),
        out_shape=jax.ShapeDtypeStruct((T, A_WIDTH), F32),
        compiler_params=_cp("parallel"), name=name)(o, proj, onorm)


def _a_post_bwd(o, proj, onorm, dmix, *, tt, name, dep=None):
    T = o.shape[0]

    def kernel_body(o_ref, g_ref, w_ref, dy_ref, do_ref, dg_ref, dw_ref):
        @pl.when(pl.program_id(0) == 0)
        def _():
            dw_ref[...] = jnp.zeros_like(dw_ref)

        for h in range(A_HEADS):
            sl = slice(h * HEAD_DIM, (h + 1) * HEAD_DIM)
            xhat, r = _head_rms(o_ref[:, sl])
            g = g_ref[:, sl]
            s = _sigmoid(g)
            dy = dy_ref[:, sl]
            w = w_ref[:, sl]
            dg_ref[:, sl] = (dy * xhat * w * (s * (1.0 + g * (1.0 - s)))).astype(BF16)
            dyn = dy * (g * s)
            dw_ref[:, sl] += jnp.sum(dyn * xhat, axis=0, keepdims=True)
            do_ref[:, sl] = _head_rms_bwd(dyn * w, xhat, r)

    blk = lambda c: pl.BlockSpec((tt, A_WIDTH), lambda i: (i, c))
    body, dep_specs, dep_args = _dep(kernel_body, 4, dep)
    return pl.pallas_call(
        body, grid=(T // tt,), in_specs=[blk(0), blk(3), _full((1, A_WIDTH)), blk(0)] + dep_specs,
        out_specs=[blk(0), blk(0), _full((1, A_WIDTH))],
        out_shape=[jax.ShapeDtypeStruct((T, A_WIDTH), F32), jax.ShapeDtypeStruct((T, A_WIDTH), BF16),
                   jax.ShapeDtypeStruct((1, A_WIDTH), F32)],
        compiler_params=_cp("arbitrary"), name=name)(o, proj, onorm, dmix, *dep_args)


def _mem_head_masks(n):
    lane = lax.broadcasted_iota(jnp.int32, (n, MEM_WIDTH), 1)
    return [(lane >= m * MEM_HEAD_DIM) & (lane < (m + 1) * MEM_HEAD_DIM) for m in range(MEM_HEADS)]


def _mem_head_rms(x, masks):
    x2 = x * x
    r = jnp.zeros_like(x)
    for mk in masks:
        ms = jnp.sum(jnp.where(mk, x2, 0.0), axis=-1, keepdims=True) * (1.0 / MEM_HEAD_DIM)
        r = jnp.where(mk, lax.rsqrt(ms + EPS), r)
    return x * r, r


def _mem_head_rms_bwd(dxhat, xhat, r, masks):
    t = dxhat * xhat
    m = jnp.zeros_like(t)
    for mk in masks:
        m = jnp.where(mk, jnp.sum(jnp.where(mk, t, 0.0), axis=-1, keepdims=True) * (1.0 / MEM_HEAD_DIM), m)
    return r * (dxhat - xhat * m)


MEM_SCALE = MEM_HEAD_DIM ** -0.5


def _mem_attn_fwd(proj, qcol, mkv, qn_w, kn_w, *, tt, name):
    T = proj.shape[0]

    def body(q_ref, k_ref, v_ref, qw_ref, kw_ref, o_ref):
        qmasks = _mem_head_masks(tt)
        kmasks = _mem_head_masks(MEM_TOKENS)
        qhat, _ = _mem_head_rms(q_ref[...], qmasks)
        qn = qhat * qw_ref[...]
        khat, _ = _mem_head_rms(k_ref[...], kmasks)
        kn = (khat * kw_ref[...]).astype(BF16)
        v = v_ref[...].astype(BF16)
        out = jnp.zeros((tt, MEM_WIDTH), F32)
        for m in range(MEM_HEADS):
            s = _dot_nt(jnp.where(qmasks[m], qn, 0.0), kn) * MEM_SCALE
            s = s - jnp.max(s, axis=-1, keepdims=True)
            p = jnp.exp(s)
            p = p / jnp.sum(p, axis=-1, keepdims=True)
            out = jnp.where(qmasks[m], _dot(p, v), out)
        o_ref[...] = out

    return pl.pallas_call(
        body, grid=(T // tt,),
        in_specs=[pl.BlockSpec((tt, MEM_WIDTH), lambda i: (i, qcol)), pl.BlockSpec((MEM_TOKENS, MEM_WIDTH), lambda i: (0, 0)),
                  pl.BlockSpec((MEM_TOKENS, MEM_WIDTH), lambda i: (0, 1)), _full((1, MEM_WIDTH)), _full((1, MEM_WIDTH))],
        out_specs=pl.BlockSpec((tt, MEM_WIDTH), lambda i: (i, 0)),
        out_shape=jax.ShapeDtypeStruct((T, MEM_WIDTH), F32),
        compiler_params=_cp("parallel"), name=name)(proj, mkv, mkv, qn_w, kn_w)


def _mem_attn_bwd(proj, qcol, mkv, qn_w, kn_w, dmix, *, tt, name):
    T = proj.shape[0]
    nsteps = T // tt
    ocol = (dmix.shape[1] - MEM_WIDTH) // MEM_WIDTH

    def body(q_ref, k_ref, v_ref, qw_ref, kw_ref, do_ref, dq_ref, dkv_ref, dqw_ref, dkw_ref, dk_acc, dv_acc):
        step = pl.program_id(0)

        @pl.when(step == 0)
        def _():
            dk_acc[...] = jnp.zeros_like(dk_acc)
            dv_acc[...] = jnp.zeros_like(dv_acc)
            dqw_ref[...] = jnp.zeros_like(dqw_ref)

        qmasks = _mem_head_masks(tt)
        kmasks = _mem_head_masks(MEM_TOKENS)
        qhat, qr = _mem_head_rms(q_ref[...], qmasks)
        qn = qhat * qw_ref[...]
        khat, kr = _mem_head_rms(k_ref[...], kmasks)
        kn = (khat * kw_ref[...]).astype(BF16)
        v = v_ref[...].astype(BF16)
        dout = do_ref[...]
        dqn = jnp.zeros((tt, MEM_WIDTH), F32)
        dkn = jnp.zeros((MEM_TOKENS, MEM_WIDTH), F32)
        dvv = jnp.zeros((MEM_TOKENS, MEM_WIDTH), F32)
        for m in range(MEM_HEADS):
            qm = jnp.where(qmasks[m], qn, 0.0).astype(BF16)
            s = _dot_nt(qm, kn) * MEM_SCALE
            s = s - jnp.max(s, axis=-1, keepdims=True)
            p = jnp.exp(s)
            p = p / jnp.sum(p, axis=-1, keepdims=True)
            dom = jnp.where(qmasks[m], dout, 0.0).astype(BF16)
            dp = _dot_nt(dom, v)
            ds = (p * (dp - jnp.sum(p * dp, axis=-1, keepdims=True)) * MEM_SCALE).astype(BF16)
            dqn = jnp.where(qmasks[m], _dot(ds, kn), dqn)
            dkn = jnp.where(kmasks[m], _dot_tn(ds, qm), dkn)
            dvv = jnp.where(kmasks[m], _dot_tn(p, dom), dvv)
        dqw_ref[...] += jnp.sum(dqn * qhat, axis=0, keepdims=True)
        dq_ref[...] = _mem_head_rms_bwd(dqn * qw_ref[...], qhat, qr, qmasks).astype(BF16)
        dk_acc[...] += dkn
        dv_acc[...] += dvv

        @pl.when(step == nsteps - 1)
        def _():
            dk = dk_acc[...]
            dkw_ref[...] = jnp.sum(dk * khat, axis=0, keepdims=True)
            dkv_ref[:, :MEM_WIDTH] = _mem_head_rms_bwd(dk * kw_ref[...], khat, kr, kmasks)
            dkv_ref[:, MEM_WIDTH:] = dv_acc[...]

    return pl.pallas_call(
        body, grid=(nsteps,),
        in_specs=[pl.BlockSpec((tt, MEM_WIDTH), lambda i: (i, qcol)), pl.BlockSpec((MEM_TOKENS, MEM_WIDTH), lambda i: (0, 0)),
                  pl.BlockSpec((MEM_TOKENS, MEM_WIDTH), lambda i: (0, 1)), _full((1, MEM_WIDTH)), _full((1, MEM_WIDTH)),
                  pl.BlockSpec((tt, MEM_WIDTH), lambda i: (i, ocol))],
        out_specs=[pl.BlockSpec((tt, MEM_WIDTH), lambda i: (i, 0)), _full((MEM_TOKENS, 2 * MEM_WIDTH)),
                   _full((1, MEM_WIDTH)), _full((1, MEM_WIDTH))],
        out_shape=[jax.ShapeDtypeStruct((T, MEM_WIDTH), BF16), jax.ShapeDtypeStruct((MEM_TOKENS, 2 * MEM_WIDTH), F32),
                   jax.ShapeDtypeStruct((1, MEM_WIDTH), F32), jax.ShapeDtypeStruct((1, MEM_WIDTH), F32)],
        scratch_shapes=[pltpu.VMEM((MEM_TOKENS, MEM_WIDTH), F32), pltpu.VMEM((MEM_TOKENS, MEM_WIDTH), F32)],
        compiler_params=_cp("arbitrary"), name=name)(proj, mkv, mkv, qn_w, kn_w, dmix)


HALF = HEAD_DIM // 2
ATT_SCALE = HEAD_DIM ** -0.5
NEG = -1e30


def _rope_tables(T):
    inv = np.float32(ROPE_THETA) ** (-np.arange(HALF, dtype=np.float32) / np.float32(HALF))
    ang = np.arange(T, dtype=np.float32)[:, None] * inv[None, :].astype(np.float32)
    cos, sin = np.cos(ang).astype(np.float32), np.sin(ang).astype(np.float32)
    return jnp.asarray(np.concatenate([cos, cos], axis=-1)), jnp.asarray(np.concatenate([-sin, sin], axis=-1))


def _rope(x, cosf, sinsg):
    return x * cosf + pltpu.roll(x, HALF, 1) * sinsg


def _rope_bwd(dy, cosf, sinsg):
    return dy * cosf + pltpu.roll(dy * sinsg, HALF, 1)


def _q_prep_bwd(proj, w_heads, cosf, sinsg, dqs, *, tt, name):
    T = proj.shape[0]
    W = N_GROUPS * B_WIDTH

    def body(x_ref, w_ref, c_ref, s_ref, d0, d1, d2, dx_ref, dw_ref):
        @pl.when(pl.program_id(0) == 0)
        def _():
            dw_ref[...] = jnp.zeros_like(dw_ref)

        c, s = c_ref[...], s_ref[...]
        for gi, d_ref in enumerate((d0, d1, d2)):
            for h in range(B_HEADS):
                sl = slice((gi * B_HEADS + h) * HEAD_DIM, (gi * B_HEADS + h + 1) * HEAD_DIM)
                xhat, r = _head_rms(x_ref[:, sl])
                dyn = _rope_bwd(d_ref[:, h * HEAD_DIM:(h + 1) * HEAD_DIM], c, s)
                dw_ref[:, sl] += jnp.sum(dyn * xhat, axis=0, keepdims=True)
                dx_ref[:, sl] = _head_rms_bwd(dyn * w_ref[:, sl], xhat, r).astype(BF16)

    tbl = pl.BlockSpec((tt, HEAD_DIM), lambda i: (i, 0))
    dyb = pl.BlockSpec((tt, B_WIDTH), lambda i: (i, 0))
    return pl.pallas_call(
        body, grid=(T // tt,),
        in_specs=[pl.BlockSpec((tt, W), lambda i: (i, 0)), _full((1, W)), tbl, tbl, dyb, dyb, dyb],
        out_specs=[pl.BlockSpec((tt, W), lambda i: (i, 0)), _full((1, W))],
        out_shape=[jax.ShapeDtypeStruct((T, W), BF16), jax.ShapeDtypeStruct((1, W), F32)],
        compiler_params=_cp("arbitrary"), name=name)(proj, w_heads, cosf, sinsg, *dqs)


def _kv_prep_bwd(kv, w_heads, cosf, sinsg, dks, dvs, *, tt, name):
    T = kv.shape[0]

    def body(x_ref, w_ref, c_ref, s_ref, k0, k1, k2, v0, v1, v2, dx_ref, dw_ref):
        @pl.when(pl.program_id(0) == 0)
        def _():
            dw_ref[...] = jnp.zeros_like(dw_ref)

        c, s = c_ref[...], s_ref[...]
        for h in range(B_HEADS):
            sl = slice(h * HEAD_DIM, (h + 1) * HEAD_DIM)
            vs = slice(B_WIDTH + h * HEAD_DIM, B_WIDTH + (h + 1) * HEAD_DIM)
            xhat, r = _head_rms(x_ref[:, sl])
            dyn = _rope_bwd(k0[:, sl] + k1[:, sl] + k2[:, sl], c, s)
            dw_ref[:, sl] += jnp.sum(dyn * xhat, axis=0, keepdims=True)
            dx_ref[:, sl] = _head_rms_bwd(dyn * w_ref[:, sl], xhat, r).astype(BF16)
            dx_ref[:, vs] = (v0[:, sl] + v1[:, sl] + v2[:, sl]).astype(BF16)

    tbl = pl.BlockSpec((tt, HEAD_DIM), lambda i: (i, 0))
    dyb = pl.BlockSpec((tt, B_WIDTH), lambda i: (i, 0))
    return pl.pallas_call(
        body, grid=(T // tt,),
        in_specs=[dyb, _full((1, B_WIDTH)), tbl, tbl] + [dyb] * 6,
        out_specs=[pl.BlockSpec((tt, 2 * B_WIDTH), lambda i: (i, 0)), _full((1, B_WIDTH))],
        out_shape=[jax.ShapeDtypeStruct((T, 2 * B_WIDTH), BF16), jax.ShapeDtypeStruct((1, B_WIDTH), F32)],
        compiler_params=_cp("arbitrary"), name=name)(kv, w_heads, cosf, sinsg, *dks, *dvs)


def _band_masks(n_is_first=None):
    row = lax.broadcasted_iota(jnp.int32, (SPAN, SPAN), 0)
    col = lax.broadcasted_iota(jnp.int32, (SPAN, SPAN), 1)
    return row >= col, col >= row


def _dil_views(T, d):
    L = T // d
    return L, L // SPAN


def _dil_fwd(qr, kr, kv, gi, d, *, name):
    T = qr.shape[0]
    L, nb = _dil_views(T, d)

    def body(q_ref, kc_ref, kp_ref, vc_ref, vp_ref, o_ref, lse_ref):
        cur_ok, prev_band = _band_masks()
        prev_ok = prev_band & (pl.program_id(1) > 0)
        for h in range(B_HEADS):
            sl = slice(h * HEAD_DIM, (h + 1) * HEAD_DIM)
            q = q_ref[:, sl]
            sc = jnp.where(cur_ok, _dot_nt(q, kc_ref[:, sl]) * ATT_SCALE, NEG)
            sp = jnp.where(prev_ok, _dot_nt(q, kp_ref[:, sl]) * ATT_SCALE, NEG)
            m = jnp.maximum(jnp.max(sc, axis=-1, keepdims=True), jnp.max(sp, axis=-1, keepdims=True))
            pc = jnp.exp(sc - m)
            pp = jnp.exp(sp - m)
            l = jnp.sum(pc, axis=-1, keepdims=True) + jnp.sum(pp, axis=-1, keepdims=True)
            o_ref[:, sl] = (_dot(pc, vc_ref[:, sl]) + _dot(pp, vp_ref[:, sl])) / l
            lse_ref[:, sl] = jnp.broadcast_to(m + jnp.log(l), (SPAN, HEAD_DIM))

    blk = lambda f: pl.BlockSpec((SPAN, B_WIDTH), f)
    cur = lambda r, n: (n, r)
    prev = lambda r, n: (jnp.maximum(n - 1, 0), r)
    ov = jax.ShapeDtypeStruct((L, d * B_WIDTH), F32)
    o, lse = pl.pallas_call(
        body, grid=(d, nb),
        in_specs=[blk(lambda r, n: (n, r * N_GROUPS + gi)), blk(cur), blk(prev),
                  blk(lambda r, n: (n, 2 * r + 1)), blk(lambda r, n: (jnp.maximum(n - 1, 0), 2 * r + 1))],
        out_specs=[blk(cur), blk(cur)], out_shape=[ov, ov],
        compiler_params=_cp("parallel", "arbitrary"), name=name,
    )(qr.reshape(L, d * N_GROUPS * B_WIDTH), kr.reshape(L, d * B_WIDTH), kr.reshape(L, d * B_WIDTH),
      kv.reshape(L, d * 2 * B_WIDTH), kv.reshape(L, d * 2 * B_WIDTH))
    return o.reshape(T, B_WIDTH), lse.reshape(T, B_WIDTH)


def _dil_combine_fwd(os_, lses, *, tt, name):
    T = os_[0].shape[0]

    def body(o0, o1, o2, l0, l1, l2, y_ref, lse_ref):
        a, b, c = l0[...], l1[...], l2[...]
        m = jnp.maximum(jnp.maximum(a, b), c)
        wa, wb, wc = jnp.exp(a - m), jnp.exp(b - m), jnp.exp(c - m)
        den = wa + wb + wc
        y_ref[...] = (wa * o0[...] + wb * o1[...] + wc * o2[...]) / den
        lse_ref[...] = m + jnp.log(den)

    blk = pl.BlockSpec((tt, B_WIDTH), lambda i: (i, 0))
    sh = jax.ShapeDtypeStruct((T, B_WIDTH), F32)
    return pl.pallas_call(
        body, grid=(T // tt,), in_specs=[blk] * 6, out_specs=[blk, blk], out_shape=[sh, sh],
        compiler_params=_cp("parallel"), name=name)(*os_, *lses)


DILS_UNROLL = 8


def _dils_specs(gi, d, nblk):
    blk = lambda f: pl.BlockSpec((SPAN * d, HEAD_DIM), f)
    return {
        "q": blk(lambda h, n: (n, gi * B_HEADS + h)), "q_next": blk(lambda h, n: (jnp.minimum(n + 1, nblk - 1), gi * B_HEADS + h)),
        "cur": blk(lambda h, n: (n, h)), "prev": blk(lambda h, n: (jnp.maximum(n - 1, 0), h)),
        "next": blk(lambda h, n: (jnp.minimum(n + 1, nblk - 1), h)),
        "v": blk(lambda h, n: (n, B_HEADS + h)), "v_prev": blk(lambda h, n: (jnp.maximum(n - 1, 0), B_HEADS + h)),
    }


def _dils_fwd(qr, kr, kv, gi, d, *, name):
    T = qr.shape[0]
    nblk = T // (SPAN * d)
    sp = _dils_specs(gi, d, nblk)

    def body(q_ref, kc_ref, vc_ref, o_ref, lse_ref, k_before, v_before):
        @pl.when(pl.program_id(1) == 0)
        def _():
            k_before[...] = jnp.zeros_like(k_before)
            v_before[...] = jnp.zeros_like(v_before)

        cur_ok, prev_band = _band_masks()
        prev_ok = prev_band & (pl.program_id(1) > 0)

        def residue(r, carry):
            rows = pl.ds(r, SPAN, stride=d)
            q, kc, vc = q_ref[rows, :], kc_ref[rows, :].astype(BF16), vc_ref[rows, :].astype(BF16)
            sc = jnp.where(cur_ok, _dot_nt(q, kc) * ATT_SCALE, NEG)
            sp_ = jnp.where(prev_ok, _dot_nt(q, k_before[r]) * ATT_SCALE, NEG)
            m = jnp.maximum(jnp.max(sc, axis=-1, keepdims=True), jnp.max(sp_, axis=-1, keepdims=True))
            pc = jnp.exp(sc - m)
            pp = jnp.exp(sp_ - m)
            l = jnp.sum(pc, axis=-1, keepdims=True) + jnp.sum(pp, axis=-1, keepdims=True)
            o_ref[rows, :] = (_dot(pc, vc) + _dot(pp, v_before[r])) / l
            lse_ref[rows, :] = jnp.broadcast_to(m + jnp.log(l), (SPAN, HEAD_DIM))
            k_before[r] = kc
            v_before[r] = vc
            return carry

        lax.fori_loop(0, d, residue, 0, unroll=min(d, DILS_UNROLL))

    sh = jax.ShapeDtypeStruct((T, B_WIDTH), F32)
    return pl.pallas_call(
        body, grid=(B_HEADS, nblk), in_specs=[sp["q"], sp["cur"], sp["v"]],
        out_specs=[sp["cur"], sp["cur"]], out_shape=[sh, sh],
        scratch_shapes=[pltpu.VMEM((d, SPAN, HEAD_DIM), BF16), pltpu.VMEM((d, SPAN, HEAD_DIM), BF16)],
        compiler_params=_cp("parallel", "arbitrary"), name=name)(qr, kr, kv)


DIL_BWD_GROUP = {1: 16, 4: 1, 16: 1}


def _dil_bwd(qr, kr, kv, dmix, lse, dd, gi, d, *, name, dep=None):
    T = qr.shape[0]
    G = DIL_BWD_GROUP[d]
    band = SPAN * d
    tb = G * band
    nblk = T // tb
    n_units = T // SPAN

    keep = G == 1

    def kernel_body(q_ref, dy_ref, lse_ref, dd_ref, kc_ref, vc_ref, *rest):
        if keep:
            dq_ref, dk_ref, dv_ref, dk_acc, dv_acc, k_before, v_before = rest
        else:
            kp_ref, vp_ref, dq_ref, dk_ref, dv_ref, dk_acc, dv_acc = rest
        n = pl.program_id(1)

        @pl.when(n == 0)
        def _():
            dk_acc[...] = jnp.zeros_like(dk_acc)
            dv_acc[...] = jnp.zeros_like(dv_acc)
            if keep:
                k_before[...] = jnp.zeros_like(k_before)
                v_before[...] = jnp.zeros_like(v_before)

        cur_ok, prev_band = _band_masks()
        for j in range(G):
            def residue(r, carry, j=j):
                off = j * band + r
                rows = pl.ds(off, SPAN, stride=d)
                q, dy = q_ref[rows, :], dy_ref[rows, :]
                lse_h = jnp.max(lse_ref[rows, :], axis=-1, keepdims=True)
                dd_h = jnp.max(dd_ref[rows, :], axis=-1, keepdims=True)
                kc, vc = kc_ref[rows, :].astype(BF16), vc_ref[rows, :].astype(BF16)
                if j > 0:
                    before = pl.ds(off - band, SPAN, stride=d)
                    kp, vp = kc_ref[before, :], vc_ref[before, :]
                    prev_ok = prev_band
                elif keep:
                    kp, vp = k_before[r], v_before[r]
                    k_before[r] = kc
                    v_before[r] = vc
                    prev_ok = prev_band & (n > 0)
                else:
                    before = pl.ds((G - 1) * band + r, SPAN, stride=d)
                    kp, vp = kp_ref[before, :], vp_ref[before, :]
                    prev_ok = prev_band & (n > 0)
                pc = jnp.exp(jnp.where(cur_ok, _dot_nt(q, kc) * ATT_SCALE, NEG) - lse_h)
                pp = jnp.exp(jnp.where(prev_ok, _dot_nt(q, kp) * ATT_SCALE, NEG) - lse_h)
                dsc = pc * (_dot_nt(dy, vc) - dd_h) * ATT_SCALE
                dsp = pp * (_dot_nt(dy, vp) - dd_h) * ATT_SCALE
                dq_ref[rows, :] = _dot(dsc, kc) + _dot(dsp, kp)
                u = (n * G + j) * d + r
                here = pl.ds(pl.multiple_of(u * SPAN, SPAN), SPAN)
                dk_acc[here, :] += _dot_tn(dsc, q)
                dv_acc[here, :] += _dot_tn(pc, dy)
                there = pl.ds(pl.multiple_of(jnp.maximum(u - d, 0) * SPAN, SPAN), SPAN)
                dk_acc[there, :] += _dot_tn(dsp, q)
                dv_acc[there, :] += _dot_tn(pp, dy)
                return carry

            lax.fori_loop(0, d, residue, 0, unroll=min(d, DILS_UNROLL))

        @pl.when(n == nblk - 1)
        def _():
            def place(u, carry):
                rows = pl.ds((u // d) * band + u % d, SPAN, stride=d)
                src = pl.ds(pl.multiple_of(u * SPAN, SPAN), SPAN)
                dk_ref[rows, :] = dk_acc[src, :]
                dv_ref[rows, :] = dv_acc[src, :]
                return carry

            lax.fori_loop(0, n_units, place, 0)

    blk = lambda f: pl.BlockSpec((tb, HEAD_DIM), f)
    cur = lambda h, n: (n, h)
    prev = lambda h, n: (jnp.maximum(n - 1, 0), h)
    whole = pl.BlockSpec((T, HEAD_DIM), lambda h, n: (0, h))
    sh = jax.ShapeDtypeStruct((T, B_WIDTH), F32)
    v_cur = blk(lambda h, n: (n, B_HEADS + h))
    if keep:
        kv_specs, kv_args = [blk(cur), v_cur], [kr, kv]
        kept = [pltpu.VMEM((d, SPAN, HEAD_DIM), BF16), pltpu.VMEM((d, SPAN, HEAD_DIM), BF16)]
    else:
        kv_specs = [blk(cur), v_cur, blk(prev), blk(lambda h, n: (jnp.maximum(n - 1, 0), B_HEADS + h))]
        kv_args, kept = [kr, kv, kr, kv], []
    body, dep_specs, dep_args = _dep(kernel_body, 4 + len(kv_args), dep)
    return pl.pallas_call(
        body, grid=(B_HEADS, nblk),
        in_specs=[blk(lambda h, n: (n, gi * B_HEADS + h)), blk(cur), blk(cur), blk(cur)] + kv_specs + dep_specs,
        out_specs=[blk(cur), whole, whole], out_shape=[sh, sh, sh],
        scratch_shapes=[pltpu.VMEM((T, HEAD_DIM), F32), pltpu.VMEM((T, HEAD_DIM), F32)] + kept,
        compiler_params=_cp("parallel", "arbitrary"), name=name)(qr, dmix, lse, dd, *kv_args, *dep_args)


A_MQ_COL = 4 * A_WIDTH // MEM_WIDTH
B_MQ_COL = N_GROUPS * B_WIDTH // MEM_WIDTH


def _row(v):
    return v.reshape(1, -1).astype(F32)


def _local_step(x, mem, tgt, get_w, P, put_g, first_dep=None, forward_point=lambda i, value: value):
    T = x.shape[0]
    cosf, sinsg = _rope_tables(T)
    lb_soft = jax.nn.softmax(P["a_lb_logits"].astype(F32), axis=0)
    lb = lb_soft[0:1]
    qw_heads = jnp.repeat(P["b_qnorm"][0], B_HEADS, axis=0).reshape(1, -1)
    kw_heads = jnp.tile(_row(P["b_knorm"]), (1, B_HEADS))
    mqw = [jnp.tile(_row(P["mem_qnorm"][l]), (1, MEM_HEADS)) for l in range(2)]
    mkw = [jnp.tile(_row(P["mem_knorm"][l]), (1, MEM_HEADS)) for l in range(2)]
    nmix = [_row(P["norm_mix"][l]) for l in range(2)]
    nffn = [_row(P["norm_ffn"][l]) for l in range(2)]
    mnorm = [_row(P["mem_norm"][l]) for l in range(2)]
    kvn = _row(P["kv_norm"])
    onorm = _row(P["a_onorm"])
    W = {}

    def w_of(name, after=None):
        if name not in W:
            W[name] = get_w(name, after)
        return W[name]

    proj_a, xn0 = _rms_matmul(x, nmix[0], w_of("a_w_in"), tt=512, tn=1664, wt=True, name="proj_a", dep=first_dep)
    mkv0, mn0 = _rms_matmul(mem, mnorm[0], w_of("w_mem_kv0"), tt=MEM_TOKENS, tn=2 * MEM_WIDTH, wt=False, name="mem_kv0")
    o_raw, st = _hgrn2_fwd(proj_a, lb, name="hgrn2_fwd")
    o_raw = forward_point(0, o_raw)
    mm0 = _a_post_fwd(o_raw, proj_a, onorm, tt=512, name="a_post_fwd")
    mo0 = _mem_attn_fwd(proj_a, A_MQ_COL, mkv0, mqw[0], mkw[0], tt=1024, name="mem_attn_fwd0")
    hm0 = _mm_res(x, mm0, mo0, w_of("w_out0", mo0), tt=512, name="out_proj0")
    hm0 = forward_point(1, hm0)
    gu0, hn0 = _rms_matmul(hm0, nffn[0], w_of("w_gate_up0", hm0), tt=512, tn=1408, wt=True, out_dtype=BF16, name="gate_up0")
    h1 = _swiglu_down(hm0, gu0, w_of("w_down0", gu0), tt=512, name="down0")
    h1 = forward_point(2, h1)
    kv, hkn, kr = _rms_matmul(h1, kvn, w_of("w_kv", h1), tt=512, tn=768, wt=True, name="kv_proj",
                              rotate=(kw_heads, cosf, sinsg))

    proj_b, xn1, qr = _rms_matmul(h1, nmix[1], w_of("b_w_in", kr), tt=512, tn=1280, wt=True, name="proj_b",
                                  rotate=(qw_heads, cosf, sinsg))
    proj_b = forward_point(3, proj_b)
    mkv1, mn1 = _rms_matmul(mem, mnorm[1], w_of("w_mem_kv1", kr), tt=MEM_TOKENS, tn=2 * MEM_WIDTH, wt=False, name="mem_kv1")
    outs = [(_dil_fwd if d == 1 else _dils_fwd)(qr, kr, kv, gi, d, name=f"dil_fwd{gi}") for gi, d in enumerate(DILATIONS)]
    mm1, lse_tot = _dil_combine_fwd([o for o, _ in outs], [s for _, s in outs], tt=512, name="dil_combine")
    mo1 = _mem_attn_fwd(proj_b, B_MQ_COL, mkv1, mqw[1], mkw[1], tt=1024, name="mem_attn_fwd1")
    hm1 = _mm_res(h1, mm1, mo1, w_of("w_out1", mo1), tt=512, name="out_proj1")
    gu1, hn1 = _rms_matmul(hm1, nffn[1], w_of("w_gate_up1", hm1), tt=512, tn=1408, wt=True, out_dtype=BF16, name="gate_up1")
    dy, sq = _swiglu_down_loss(hm1, gu1, w_of("w_down1", gu1), tgt, tt=512, name="down1_loss")

    gP = {}
    zeros_mem = jnp.zeros((MEM_TOKENS, D_MODEL), F32)

    def ffn_bwd(l, dh, hm, gu, hn):
        dgu, g_wd = _swiglu_bwd(dh, gu, w_of(f"w_down{l}"), tt=256, name=f"swiglu_bwd{l}")
        g_wgu = _mm_tn(dgu, hn, tt=512, tka=1408, name=f"g_w_gate_up{l}")
        sent = put_g({f"w_down{l}": g_wd, f"w_gate_up{l}": g_wgu})
        dhm, g_nf = _rms_bwd_dx(hm, nffn[l], w_of(f"w_gate_up{l}"), dgu, dh, tt=512, wt=True, name=f"gate_up_bwd{l}", dep=sent)
        return dhm, g_nf

    def mix_bwd(l, dhm, mix_main, mix_mem, proj, qcol, mkv, mn):
        dmix, g_wout, *head_dots = _out_proj_bwd(dhm, mix_main, mix_mem, w_of(f"w_out{l}"), tt=512, name=f"out_proj_bwd{l}",
                                                 head_dots=l == 1)
        dmq, dmkv, dqw, dkw = _mem_attn_bwd(proj, qcol, mkv, mqw[l], mkw[l], dmix, tt=1024, name=f"mem_attn_bwd{l}")
        g_wmkv = _mm_tn(mn, dmkv, tt=MEM_TOKENS, tka=512, name=f"g_w_mem_kv{l}")
        sent = put_g({f"w_out{l}": g_wout, f"w_mem_kv{l}": g_wmkv})
        _, g_mn = _rms_bwd_dx(mem, mnorm[l], w_of(f"w_mem_kv{l}"), dmkv, zeros_mem, tt=MEM_TOKENS, wt=False, name=f"mem_kv_bwd{l}")
        fold = lambda v: v.reshape(MEM_HEADS, MEM_HEAD_DIM).sum(axis=0)
        return dmix, dmq, g_mn, fold(dqw), fold(dkw), sent, head_dots

    dhm1, g_nf1 = ffn_bwd(1, dy, hm1, gu1, hn1)
    dmix1, dmq1, g_mn1, g_mq1, g_mk1, sent, (dd,) = mix_bwd(1, dhm1, mm1, mo1, proj_b, B_MQ_COL, mkv1, mn1)
    dqs, dks, dvs = [], [], []
    for gi, d in enumerate(DILATIONS):
        dq_g, dk_g, dv_g = _dil_bwd(qr, kr, kv, dmix1, lse_tot, dd, gi, d, name=f"dil_bwd{gi}", dep=sent if gi == 0 else None)
        dqs.append(dq_g)
        dks.append(dk_g)
        dvs.append(dv_g)
    dq_raw, dqw = _q_prep_bwd(proj_b, qw_heads, cosf, sinsg, dqs, tt=512, name="q_prep_bwd")
    dkv, dkw = _kv_prep_bwd(kv, kw_heads, cosf, sinsg, dks, dvs, tt=512, name="kv_prep_bwd")
    dproj_b = [dq_raw, dmq1]
    g_wb = _mm_tn_pieces(dproj_b, xn1, tt=512, name="g_b_w_in")
    g_wkv = _mm_tn(dkv, hkn, tt=512, tka=768, name="g_w_kv")
    sent = put_g({"b_w_in": g_wb, "w_kv": g_wkv})
    dh1, g_nm1 = _rms_bwd_dx(h1, nmix[1], w_of("b_w_in"), dproj_b, dhm1, tt=512, wt=True, name="proj_b_bwd", dep=sent)
    dh1, g_kvn = _rms_bwd_dx(h1, kvn, w_of("w_kv"), dkv, dh1, tt=512, wt=True, name="kv_proj_bwd")

    dhm0, g_nf0 = ffn_bwd(0, dh1, hm0, gu0, hn0)
    dmix0, dmq0, g_mn0, g_mq0, g_mk0, sent, _ = mix_bwd(0, dhm0, mm0, mo0, proj_a, A_MQ_COL, mkv0, mn0)
    do_raw, dg, g_onorm = _a_post_bwd(o_raw, proj_a, onorm, dmix0, tt=512, name="a_post_bwd", dep=sent)
    dq, dz, dv, dlb = _hgrn2_bwd(proj_a, lb, st, do_raw, name="hgrn2_bwd")
    dproj_a = [dq, dz, dv, dg, dmq0]
    sent = put_g({"a_w_in": _mm_tn_pieces(dproj_a, xn0, tt=512, name="g_a_w_in")})
    gx, g_nm0 = _rms_bwd_dx(x, nmix[0], w_of("a_w_in"), dproj_a, dhm0, tt=512, wt=True, name="proj_a_bwd", dep=sent)

    dl0 = lb_soft[0:1] * lb_soft[1:2] * dlb
    gP["a_lb_logits"] = jnp.concatenate([dl0, -dl0], axis=0)
    gP["a_onorm"] = g_onorm
    gP["norm_mix"] = jnp.concatenate([g_nm0, g_nm1], axis=0)
    gP["norm_ffn"] = jnp.concatenate([g_nf0, g_nf1], axis=0)
    gP["b_qnorm"] = dqw.reshape(N_GROUPS, B_HEADS, HEAD_DIM).sum(axis=1)[None]
    gP["kv_norm"] = g_kvn.reshape(-1)
    gP["b_knorm"] = dkw.reshape(B_HEADS, HEAD_DIM).sum(axis=0)
    gP["mem_norm"] = jnp.concatenate([g_mn0, g_mn1], axis=0)
    gP["mem_qnorm"] = jnp.stack([g_mq0, g_mq1])
    gP["mem_knorm"] = jnp.stack([g_mk0, g_mk1])
    return sq, gx, gP


MESH_ID = pl.DeviceIdType.MESH
HBM_SPEC = pl.BlockSpec(memory_space=pltpu.HBM)


def _position():
    return lax.axis_index("x"), lax.axis_index("y"), lax.axis_index("c")


def _all_gather_direct(block, after, *, name):
    def body(x_ref, after_ref, out_ref, send_sems, recv_sems, local_sem):
        x, y, c = _position()
        me = 4 * x + 2 * y + c
        mine = pltpu.make_async_copy(x_ref, out_ref.at[me], local_sem)
        mine.start()
        copies = []
        for k in ALL_PEERS:
            cp = pltpu.make_async_remote_copy(
                src_ref=x_ref, dst_ref=out_ref.at[me], send_sem=send_sems.at[k - 1], recv_sem=recv_sems.at[k - 1],
                device_id=_peer(k, x, y, c), device_id_type=MESH_ID)
            cp.start()
            copies.append(cp)
        for cp in copies:
            cp.wait()
        mine.wait()

    return pl.pallas_call(
        body, out_shape=jax.ShapeDtypeStruct((N_DEV,) + block.shape, block.dtype),
        in_specs=[HBM_SPEC, pl.BlockSpec(memory_space=pl.ANY)], out_specs=HBM_SPEC,
        scratch_shapes=[pltpu.SemaphoreType.DMA((7,)), pltpu.SemaphoreType.DMA((7,)), pltpu.SemaphoreType.DMA],
        name=name)(block, after)


SEM_SPEC = pl.BlockSpec(memory_space=pltpu.SEMAPHORE)
ANY_SPEC = pl.BlockSpec(memory_space=pl.ANY)
DATAFLOW = pltpu.SideEffectType.DATAFLOW_SIDE_EFFECTING


def _peer(k, x, y, c):
    return (1 - x if (k >> 2) & 1 else x, 1 - y if (k >> 1) & 1 else y, 1 - c if k & 1 else c)


def _own_slot_filled(own_block):
    x, y, c = _position()
    zone = lax.empty((N_DEV,) + own_block.shape, own_block.dtype)
    return lax.dynamic_update_slice_in_dim(zone, own_block[None], 4 * x + 2 * y + c, axis=0)


ALL_PEERS = tuple(range(1, N_DEV))
SIBLING_AND_SAME_CORE = (1, 2, 4, 6)
SAME_CORE = (2, 4, 6)


def _split_start(srcs, scatter, after, *, name, relations=ALL_PEERS, carried=None):
    n = len(srcs)
    extra = ([] if after is None else [after]) + ([] if carried is None else [carried])
    n_carried = 0 if carried is None else 1
    x, y, c = _position()
    me = 4 * x + 2 * y + c
    lands = [_own_slot_filled(lax.dynamic_index_in_dim(s, me, 0, keepdims=False) if scatter else s) for s in srcs]

    def body(*refs):
        src_refs, land_refs = refs[:n], refs[n:2 * n]
        send_sems, recv_sems = refs[2 * n + len(extra)], refs[2 * n + len(extra) + 1]
        token = refs[2 * n + len(extra) + 2 + 2 * n]
        bx, by, bc = _position()
        bme = 4 * bx + 2 * by + bc
        for a in range(n):
            for k in relations:
                tx, ty, tc = _peer(k, bx, by, bc)
                src = src_refs[a].at[4 * tx + 2 * ty + tc] if scatter else src_refs[a]
                pltpu.make_async_remote_copy(
                    src_ref=src, dst_ref=land_refs[a].at[bme],
                    send_sem=send_sems.at[7 * a + k - 1], recv_sem=recv_sems.at[7 * a + k - 1],
                    device_id=(tx, ty, tc), device_id_type=MESH_ID).start()
        token[...] = jnp.zeros_like(token)

    hbm = lambda a: pltpu.HBM(a.shape, a.dtype)
    outs = pl.pallas_call(
        body, name=name,
        out_shape=(pltpu.SemaphoreType.DMA((7 * n,)), pltpu.SemaphoreType.DMA((7 * n,)),
                   *[hbm(s) for s in srcs], *[hbm(l) for l in lands], jax.ShapeDtypeStruct((8, 128), F32),
                   *([hbm(carried)] if n_carried else [])),
        in_specs=[HBM_SPEC] * (2 * n) + [ANY_SPEC] * len(extra),
        out_specs=(SEM_SPEC, SEM_SPEC, *[HBM_SPEC] * (2 * n), pl.BlockSpec(memory_space=pltpu.VMEM), *([ANY_SPEC] * n_carried)),
        input_output_aliases={**{i: 2 + i for i in range(2 * n)},
                              **({2 * n + len(extra) - 1: 2 * n + 3} if n_carried else {})},
        compiler_params=pltpu.CompilerParams(has_side_effects=DATAFLOW),
    )(*[pltpu.with_memory_space_constraint(s, pltpu.HBM) for s in srcs],
      *[pltpu.with_memory_space_constraint(l, pltpu.HBM) for l in lands], *extra)
    return {"n": n, "relations": relations, "send": outs[0], "recv": outs[1], "srcs": list(outs[2:2 + n]),
            "lands": list(outs[2 + n:2 + 2 * n]), "token": outs[2 * n + 2], "carried": outs[-1] if n_carried else None}


def _forward_start(lands, carried, *, name):
    n = len(lands)

    def body(*refs):
        land_refs = refs[:n]
        send_sems, recv_sems = refs[n + 1], refs[n + 2]
        bx, by, bc = _position()
        for a in range(n):
            for k in SAME_CORE:
                tx, ty, tc = _peer(k, bx, by, bc)
                block = land_refs[a].at[4 * tx + 2 * ty + tc]
                pltpu.make_async_remote_copy(
                    src_ref=block, dst_ref=block,
                    send_sem=send_sems.at[7 * a + k - 1], recv_sem=recv_sems.at[7 * a + k - 1],
                    device_id=(bx, by, 1 - bc), device_id_type=MESH_ID).start()

    hbm = lambda a: pltpu.HBM(a.shape, a.dtype)
    outs = pl.pallas_call(
        body, name=name,
        out_shape=(pltpu.SemaphoreType.DMA((7 * n,)), pltpu.SemaphoreType.DMA((7 * n,)),
                   *[hbm(l) for l in lands], hbm(carried)),
        in_specs=[HBM_SPEC] * n + [ANY_SPEC],
        out_specs=(SEM_SPEC, SEM_SPEC, *[HBM_SPEC] * n, ANY_SPEC),
        input_output_aliases={i: 2 + i for i in range(n + 1)},
        compiler_params=pltpu.CompilerParams(has_side_effects=DATAFLOW),
    )(*lands, carried)
    handle = {"n": n, "relations": SAME_CORE, "send": outs[0], "recv": outs[1], "srcs": [], "lands": list(outs[2:2 + n])}
    return handle, outs[-1]


def _split_wait(handle, after, *, name):
    n, ns = handle["n"], len(handle["srcs"])

    def body(*refs):
        land_refs = refs[ns:ns + n]
        send_sems, recv_sems = refs[ns + n], refs[ns + n + 1]
        bx, by, bc = _position()
        for a in range(n):
            for k in handle["relations"]:
                block = land_refs[a].at[0]
                cp = pltpu.make_async_remote_copy(
                    src_ref=block, dst_ref=block,
                    send_sem=send_sems.at[7 * a + k - 1], recv_sem=recv_sems.at[7 * a + k - 1],
                    device_id=_peer(k, bx, by, bc), device_id_type=MESH_ID)
                cp.wait_send()
                cp.wait_recv()

    hbm = lambda a: pltpu.HBM(a.shape, a.dtype)
    outs = pl.pallas_call(
        body, name=name,
        out_shape=(*[hbm(s) for s in handle["srcs"]], *[hbm(l) for l in handle["lands"]]),
        in_specs=[HBM_SPEC] * (ns + n) + [SEM_SPEC, SEM_SPEC, ANY_SPEC],
        out_specs=tuple([HBM_SPEC] * (ns + n)),
        input_output_aliases={i: i for i in range(ns + n)},
        compiler_params=pltpu.CompilerParams(has_side_effects=DATAFLOW),
    )(*handle["srcs"], *handle["lands"], handle["send"], handle["recv"], after)
    return list(outs[ns:])


def _sum_sources(parts, *, tr, name):
    n, R, C = parts.shape

    def body(p_ref, o_ref):
        acc = p_ref[0].astype(F32)
        for s in range(1, n):
            acc = acc + p_ref[s].astype(F32)
        o_ref[...] = acc

    return pl.pallas_call(
        body, grid=(R // tr,), in_specs=[pl.BlockSpec((n, tr, C), lambda i: (0, i, 0))],
        out_specs=pl.BlockSpec((tr, C), lambda i: (i, 0)),
        out_shape=jax.ShapeDtypeStruct((R, C), F32), compiler_params=_cp("parallel"), name=name)(parts)


def _adamw_math(g, w, m, v):
    c1 = 1.0 - ADAM_B1 ** ADAM_STEP
    c2 = 1.0 - ADAM_B2 ** ADAM_STEP
    nm = ADAM_B1 * m + (1.0 - ADAM_B1) * g
    nv = ADAM_B2 * v + (1.0 - ADAM_B2) * (g * g)
    return -ADAM_LR * ((nm / c1) / (jnp.sqrt(nv / c2) + ADAM_EPS) + ADAM_WD * w), nm, nv


ADAMW_STRIP = 16


def _reduce_adamw(received, w, m, v, *, tr, name):
    L, R, C = w.shape

    def body(*refs):
        p_refs = refs[:L]
        w_ref, m_ref, v_ref, g_ref, d_ref, nm_ref, nv_ref = refs[L:]
        for l in range(L):
            @pl.when(pl.program_id(0) == l)
            def _(p_ref=p_refs[l]):
                def strip(i, carry):
                    rows = pl.ds(pl.multiple_of(i * ADAMW_STRIP, ADAMW_STRIP), ADAMW_STRIP)
                    acc = p_ref[0, rows, :].astype(F32)
                    for s in range(1, N_DEV):
                        acc = acc + p_ref[s, rows, :].astype(F32)
                    g_ref[rows, :] = acc
                    d_ref[rows, :], nm_ref[rows, :], nv_ref[rows, :] = _adamw_math(acc, w_ref[rows, :], m_ref[rows, :], v_ref[rows, :])
                    return carry

                lax.fori_loop(0, tr // ADAMW_STRIP, strip, 0)

    p_spec = pl.BlockSpec((N_DEV, tr, C), lambda l, i: (0, i, 0))
    blk = pl.BlockSpec((None, tr, C), lambda l, i: (l, i, 0))
    sh = jax.ShapeDtypeStruct((L, R, C), F32)
    return pl.pallas_call(
        body, grid=(L, R // tr), in_specs=[p_spec] * L + [blk] * 3, out_specs=[blk] * 4, out_shape=[sh] * 4,
        compiler_params=_cp("parallel", "parallel"), name=name)(*received, w, m, v)


def _adamw(g, w, m, v, *, tr, name):
    L, R, C = w.shape

    def body(g_ref, w_ref, m_ref, v_ref, d_ref, nm_ref, nv_ref):
        d_ref[...], nm_ref[...], nv_ref[...] = _adamw_math(g_ref[...], w_ref[...], m_ref[...], v_ref[...])

    blk = pl.BlockSpec((None, tr, C), lambda l, i: (l, i, 0))
    sh = jax.ShapeDtypeStruct((L, R, C), F32)
    return pl.pallas_call(
        body, grid=(L, R // tr), in_specs=[blk] * 4, out_specs=[blk] * 3, out_shape=[sh] * 3,
        compiler_params=_cp("parallel", "parallel"), name=name)(g, w, m, v)


UNITS = {
    "a_w_in": ("a_w_in", 0, True), "w_mem_kv0": ("w_mem_kv", 0, False), "w_out0": ("w_out", 0, False),
    "w_gate_up0": ("w_gate_up", 0, True), "w_down0": ("w_down", 0, False), "w_kv": ("w_kv", None, True),
    "b_w_in": ("b_w_in", 0, True), "w_mem_kv1": ("w_mem_kv", 1, False), "w_out1": ("w_out", 1, False),
    "w_gate_up1": ("w_gate_up", 1, True), "w_down1": ("w_down", 1, False),
}
BIG = ("a_w_in", "b_w_in", "w_kv", "w_mem_kv", "w_out", "w_gate_up", "w_down")
ADAMW_ROW_TILE = {"a_w_in": 208, "b_w_in": 160, "w_kv": 192, "w_mem_kv": 128, "w_out": 128, "w_gate_up": 352, "w_down": 352}


def _wire_block(weights, unit):
    name, layer, col = UNITS[unit]
    a = weights[name] if layer is None else weights[name][layer]
    return (a.T if col else a).astype(BF16)


SMALL_REPLICATED = ("norm_mix", "norm_ffn", "b_qnorm", "kv_norm", "b_knorm", "mem_norm", "mem_qnorm", "mem_knorm")
SMALL_SHARDED = ("a_lb_logits", "a_onorm")
SMALL_ORDER = SMALL_REPLICATED + SMALL_SHARDED
LANES = 128


def _prod(shape):
    n = 1
    for s in shape:
        n *= s
    return n


def _pack_flat(arrays, rows, cols, dtype):
    flat = jnp.concatenate([a.reshape(-1).astype(dtype) for a in arrays])
    return jnp.pad(flat, (0, rows * cols - flat.shape[0])).reshape(rows, cols)


def _unpack_flat(packed, shapes):
    flat = packed.reshape(-1)
    out, off = [], 0
    for s in shapes:
        out.append(flat[off:off + _prod(s)].reshape(s))
        off += _prod(s)
    return out


def kernel(x, mem, norm_mix, norm_ffn, a_w_in, a_lb_logits, a_onorm, b_w_in, b_qnorm, kv_norm, w_kv, b_knorm, mem_norm, w_mem_kv, mem_qnorm, mem_knorm, w_out, w_gate_up, w_down, loss_target, m_norm_mix, m_norm_ffn, m_a_w_in, m_a_lb_logits, m_a_onorm, m_b_w_in, m_b_qnorm, m_kv_norm, m_w_kv, m_b_knorm, m_mem_norm, m_w_mem_kv, m_mem_qnorm, m_mem_knorm, m_w_out, m_w_gate_up, m_w_down, v_norm_mix, v_norm_ffn, v_a_w_in, v_a_lb_logits, v_a_onorm, v_b_w_in, v_b_qnorm, v_kv_norm, v_w_kv, v_b_knorm, v_mem_norm, v_w_mem_kv, v_mem_qnorm, v_mem_knorm, v_w_out, v_w_gate_up, v_w_down):
    names = ("norm_mix", "norm_ffn", "a_w_in", "a_lb_logits", "a_onorm", "b_w_in", "b_qnorm", "kv_norm", "w_kv", "b_knorm",
             "mem_norm", "w_mem_kv", "mem_qnorm", "mem_knorm", "w_out", "w_gate_up", "w_down")
    w = dict(zip(names, (norm_mix, norm_ffn, a_w_in, a_lb_logits, a_onorm, b_w_in, b_qnorm, kv_norm, w_kv, b_knorm,
                         mem_norm, w_mem_kv, mem_qnorm, mem_knorm, w_out, w_gate_up, w_down)))
    m = dict(zip(names, (m_norm_mix, m_norm_ffn, m_a_w_in, m_a_lb_logits, m_a_onorm, m_b_w_in, m_b_qnorm, m_kv_norm, m_w_kv,
                         m_b_knorm, m_mem_norm, m_w_mem_kv, m_mem_qnorm, m_mem_knorm, m_w_out, m_w_gate_up, m_w_down)))
    v = dict(zip(names, (v_norm_mix, v_norm_ffn, v_a_w_in, v_a_lb_logits, v_a_onorm, v_b_w_in, v_b_qnorm, v_kv_norm, v_w_kv,
                         v_b_knorm, v_mem_norm, v_w_mem_kv, v_mem_qnorm, v_mem_knorm, v_w_out, v_w_gate_up, v_w_down)))

    first = ["a_w_in", "w_mem_kv0"]
    later = [["w_out0", "w_gate_up0"], ["w_down0", "w_kv"], ["b_w_in", "w_mem_kv1"], ["w_out1", "w_gate_up1", "w_down1"]]
    first_half, second_half = {}, {}

    def start_first_half(i, after, carried=None):
        first_half[i] = _split_start([_wire_block(w, u) for u in later[i]], False, after, name=f"gather{i}_start",
                                     relations=SIBLING_AND_SAME_CORE, carried=carried)
        return first_half[i]

    opening = _split_start([_wire_block(w, u) for u in first] + [_pack_flat([a_lb_logits, a_onorm], 8, LANES, F32)],
                           False, None, name="gather_first_start", relations=SIBLING_AND_SAME_CORE)
    token = start_first_half(0, opening["token"])["token"]
    token = start_first_half(1, token)["token"]
    opening, token = _forward_start(_split_wait(opening, token, name="gather_first_landed"), token, name="gather_first_forward")
    gathered = _split_wait(opening, token, name="gather_first_wait")
    full = {u: g.reshape(-1, g.shape[-1]) for u, g in zip(first, gathered)}
    small_in = gathered[-1].reshape(N_DEV, -1)
    P = {n: w[n] for n in SMALL_REPLICATED}
    P["a_lb_logits"] = small_in[:, :192].reshape(N_DEV, 2, 96).transpose(1, 0, 2).reshape(2, A_WIDTH)
    P["a_onorm"] = small_in[:, 192:288].reshape(1, A_WIDTH)

    def forward_point(i, value):
        landed = _split_wait(first_half[i], value, name=f"gather{i}_landed")
        second_half[i], value = _forward_start(landed, value, name=f"gather{i}_forward")
        if i + 2 < len(later):
            value = start_first_half(i + 2, None, carried=value)["carried"]
        return value

    def get_w(unit, after):
        if unit not in full:
            i = [unit in group for group in later].index(True)
            for u, land in zip(later[i], _split_wait(second_half[i], after, name=f"gather{i}_wait")):
                full[u] = land.reshape(-1, land.shape[-1])
        return full[unit]

    sent = []

    def put_g(group):
        units = list(group)
        handle = _split_start([group[u].reshape(N_DEV, -1, group[u].shape[-1]) for u in units], True, None,
                              name=f"scatter{len(sent)}_start")
        sent.append((units, handle))
        return handle["token"]

    sq, gx, gP = _local_step(x[0], mem[0], loss_target[0], get_w, P, put_g, forward_point=forward_point)
    loss_here = (0.5 * jnp.sum(sq) / D_MODEL).reshape(1)

    received = {}
    group_of = {u: i for i, (units, _) in enumerate(sent) for u in units}
    out = {"grad": {}, "delta": {}, "new_m": {}, "new_v": {}}
    newest = [gx]

    def update_big(n):
        shape = w[n].shape
        as3 = lambda a: a.reshape((-1,) + shape[-2:])
        mine = [u for u, (wn, _, _) in UNITS.items() if wn == n]
        for i in sorted({group_of[u] for u in mine}):
            if sent[i][0][0] not in received:
                received.update(zip(sent[i][0], _split_wait(sent[i][1], newest[0], name=f"scatter{i}_wait")))
        flip = (lambda a: jnp.swapaxes(a, 1, 2)) if UNITS[mine[0]][2] else (lambda a: a)
        res = _reduce_adamw([received[u] for u in mine], flip(as3(w[n])), flip(as3(m[n])), flip(as3(v[n])),
                            tr=ADAMW_ROW_TILE[n], name=f"adamw_{n}")
        newest[0] = res[1]
        for kind, r in zip(("grad", "delta", "new_m", "new_v"), res):
            out[kind][n] = flip(r).reshape(shape)

    for n in ("w_down", "w_gate_up", "w_out", "w_mem_kv", "b_w_in", "w_kv"):
        update_big(n)

    full_shapes = [(2, A_WIDTH) if n == "a_lb_logits" else (1, A_WIDTH) if n == "a_onorm" else w[n].shape for n in SMALL_ORDER]
    n_small = sum(_prod(s) for s in full_shapes) + 1
    rows_small = -(-n_small // (8 * LANES)) * 8
    g_all = _all_gather_direct(_pack_flat([gP[n] for n in SMALL_ORDER] + [loss_here], rows_small, LANES, F32),
                               newest[0], name="gather_small_grads")
    summed = _unpack_flat(_sum_sources(g_all, tr=rows_small, name="sum_small_grads"), full_shapes + [(1,)])
    g_small = dict(zip(SMALL_ORDER, summed))
    loss = summed[-1].reshape(())
    me = 4 * lax.axis_index("x") + 2 * lax.axis_index("y") + lax.axis_index("c")
    for n in SMALL_SHARDED:
        g_small[n] = lax.dynamic_slice_in_dim(g_small[n], me * 96, 96, axis=1)
    shapes = [w[n].shape for n in SMALL_ORDER]
    rows_upd = -(-sum(_prod(s) for s in shapes) // (8 * LANES)) * 8
    pk = lambda d: _pack_flat([d[n] for n in SMALL_ORDER], rows_upd, LANES, F32)
    res = _adamw(pk(g_small)[None], pk(w)[None], pk(m)[None], pk(v)[None], tr=rows_upd, name="adamw_small")
    out["grad"].update(g_small)
    for kind, packed in zip(("delta", "new_m", "new_v"), res):
        out[kind].update(zip(SMALL_ORDER, _unpack_flat(packed[0], shapes)))
    newest[0] = res[0]
    update_big("a_w_in")

    return (loss, gx[None], *[out["grad"][n] for n in names], *[out["delta"][n] for n in names],
            *[out["new_m"][n] for n in names], *[out["new_v"][n] for n in names])
```

```python
import functools

import jax
import jax.numpy as jnp
import numpy as np
from jax import lax
from jax.experimental import pallas as pl
from jax.experimental.pallas import tpu as pltpu

F32 = jnp.float32
BF16 = jnp.bfloat16

N_DEV = 8
D_MODEL = 1024
HEAD_DIM = 128
A_HEADS = 6
A_WIDTH = A_HEADS * HEAD_DIM
CHUNK = 64
B_HEADS = 6
B_WIDTH = B_HEADS * HEAD_DIM
DILATIONS = (1, 4, 16)
SPAN = 128
N_GROUPS = 3
ROPE_THETA = 10000.0
MEM_TOKENS = 256
MEM_HEADS = 4
MEM_HEAD_DIM = 64
MEM_WIDTH = MEM_HEADS * MEM_HEAD_DIM
FFN_HIDDEN = 2816
EPS = 1e-6

ADAM_LR = 0.001
ADAM_B1 = 0.9
ADAM_B2 = 0.999
ADAM_EPS = 1e-08
ADAM_WD = 0.01
ADAM_STEP = 10

V7X_VMEM_LIMIT_BYTES = 56 * 1024 * 1024

NT_DIMS = (((1,), (1,)), ((), ()))
TN_DIMS = (((0,), (0,)), ((), ()))


def _cp(*sem):
    return pltpu.CompilerParams(dimension_semantics=sem, vmem_limit_bytes=V7X_VMEM_LIMIT_BYTES)


def _dot(a, b):
    return jnp.dot(a.astype(BF16), b.astype(BF16), preferred_element_type=F32)


def _dot_nt(a, b):
    return lax.dot_general(a.astype(BF16), b.astype(BF16), NT_DIMS, preferred_element_type=F32)


def _dot_tn(a, b):
    return lax.dot_general(a.astype(BF16), b.astype(BF16), TN_DIMS, preferred_element_type=F32)


def _dot3(m01, x):
    hi = x.astype(BF16)
    r1 = x - hi.astype(F32)
    mid = r1.astype(BF16)
    lo = (r1 - mid.astype(F32)).astype(BF16)
    d = functools.partial(jnp.dot, preferred_element_type=F32)
    return d(m01, hi) + d(m01, mid) + d(m01, lo)


def _sigmoid(x):
    return 0.5 * jnp.tanh(0.5 * x) + 0.5


def _full(shape):
    return pl.BlockSpec(shape, lambda *_: (0,) * len(shape))


def _dep(body, n_in, dep):
    if dep is None:
        return body, [], []

    def with_dep(*refs):
        return body(*refs[:n_in], *refs[n_in + 1:])

    return with_dep, [pl.BlockSpec(memory_space=pl.ANY)], [dep]


def _rms_matmul(x, g, w, *, tt, tn, wt, name, out_dtype=F32, dep=None, rotate=None):
    T, K = x.shape
    N = w.shape[0] if wt else w.shape[1]
    n_rot = 0 if rotate is None else rotate[0].shape[1] // HEAD_DIM
    extra_in = [] if rotate is None else list(rotate)

    def kernel_body(x_ref, g_ref, w_ref, *rest):
        y_ref, xn_ref = rest[len(extra_in)], rest[len(extra_in) + 1]
        xf = x_ref[...]
        r = lax.rsqrt(jnp.mean(xf * xf, axis=-1, keepdims=True) + EPS)
        xn = (xf * r * g_ref[...]).astype(BF16)
        xn_ref[...] = xn
        for j in range(N // tn):
            cols = slice(j * tn, (j + 1) * tn)
            y = _dot_nt(xn, w_ref[cols, :]) if wt else _dot(xn, w_ref[:, cols])
            y_ref[:, cols] = y.astype(out_dtype)
            for h in range(j * tn // HEAD_DIM, min((j + 1) * tn // HEAD_DIM, n_rot)):
                gw_ref, c_ref, s_ref, yr_ref = rest[0], rest[1], rest[2], rest[len(extra_in) + 2]
                sl = slice(h * HEAD_DIM, (h + 1) * HEAD_DIM)
                xhat, _ = _head_rms(y[:, h * HEAD_DIM - j * tn:(h + 1) * HEAD_DIM - j * tn])
                yr_ref[:, sl] = _rope(xhat * gw_ref[:, sl], c_ref[...], s_ref[...])

    tbl = pl.BlockSpec((tt, HEAD_DIM), lambda i: (i, 0))
    rot_specs = [] if rotate is None else [_full((1, n_rot * HEAD_DIM)), tbl, tbl]
    body, dep_specs, dep_args = _dep(kernel_body, 3 + len(extra_in), dep)
    return pl.pallas_call(
        body, grid=(T // tt,),
        in_specs=[pl.BlockSpec((tt, K), lambda i: (i, 0)), _full((1, K)), _full(w.shape)] + rot_specs + dep_specs,
        out_specs=[pl.BlockSpec((tt, N), lambda i: (i, 0)), pl.BlockSpec((tt, K), lambda i: (i, 0))]
        + ([] if rotate is None else [pl.BlockSpec((tt, n_rot * HEAD_DIM), lambda i: (i, 0))]),
        out_shape=[jax.ShapeDtypeStruct((T, N), out_dtype), jax.ShapeDtypeStruct((T, K), BF16)]
        + ([] if rotate is None else [jax.ShapeDtypeStruct((T, n_rot * HEAD_DIM), F32)]),
        compiler_params=_cp("parallel"), name=name)(x, g, w, *extra_in, *dep_args)


def _mm_res(res, a1, a2, w, *, tt, name):
    T, K1 = a1.shape
    K2 = a2.shape[1]
    N = w.shape[1]

    def body(r_ref, a1_ref, a2_ref, w_ref, o_ref):
        o_ref[...] = r_ref[...] + _dot(a1_ref[...], w_ref[:K1, :]) + _dot(a2_ref[...], w_ref[K1:, :])

    return pl.pallas_call(
        body, grid=(T // tt,),
        in_specs=[pl.BlockSpec((tt, N), lambda i: (i, 0)), pl.BlockSpec((tt, K1), lambda i: (i, 0)),
                  pl.BlockSpec((tt, K2), lambda i: (i, 0)), _full((K1 + K2, N))],
        out_specs=pl.BlockSpec((tt, N), lambda i: (i, 0)),
        out_shape=jax.ShapeDtypeStruct((T, N), F32),
        compiler_params=_cp("parallel"), name=name)(res, a1, a2, w)


def _swiglu_down(h, gu, wd, *, tt, name):
    T, D = h.shape
    Fh = wd.shape[0]

    def body(h_ref, gt_ref, up_ref, w_ref, o_ref):
        gt = gt_ref[...].astype(F32)
        act = gt * _sigmoid(gt) * up_ref[...].astype(F32)
        o_ref[...] = h_ref[...] + _dot(act, w_ref[...])

    return pl.pallas_call(
        body, grid=(T // tt,),
        in_specs=[pl.BlockSpec((tt, D), lambda i: (i, 0)), pl.BlockSpec((tt, Fh), lambda i: (i, 0)),
                  pl.BlockSpec((tt, Fh), lambda i: (i, 1)), _full((Fh, D))],
        out_specs=pl.BlockSpec((tt, D), lambda i: (i, 0)),
        out_shape=jax.ShapeDtypeStruct((T, D), F32),
        compiler_params=_cp("parallel"), name=name)(h, gu, gu, wd)


def _swiglu_down_loss(h, gu, wd, tgt, *, tt, name):
    T, D = h.shape
    Fh = wd.shape[0]

    def body(h_ref, gt_ref, up_ref, w_ref, t_ref, dy_ref, acc_ref):
        @pl.when(pl.program_id(0) == 0)
        def _():
            acc_ref[...] = jnp.zeros_like(acc_ref)

        gt = gt_ref[...].astype(F32)
        act = gt * _sigmoid(gt) * up_ref[...].astype(F32)
        e = h_ref[...] + _dot(act, w_ref[...]) - t_ref[...]
        dy_ref[...] = e * (1.0 / D)
        acc_ref[...] += jnp.sum(e * e, axis=0, keepdims=True)

    row = pl.BlockSpec((tt, D), lambda i: (i, 0))
    return pl.pallas_call(
        body, grid=(T // tt,),
        in_specs=[row, pl.BlockSpec((tt, Fh), lambda i: (i, 0)), pl.BlockSpec((tt, Fh), lambda i: (i, 1)), _full((Fh, D)), row],
        out_specs=[row, _full((1, D))],
        out_shape=[jax.ShapeDtypeStruct((T, D), F32), jax.ShapeDtypeStruct((1, D), F32)],
        compiler_params=_cp("arbitrary"), name=name)(h, gu, gu, wd, tgt)


SWIGLU_COLS = 256


def _swiglu_bwd(dh, gu, wd, *, tt, name):
    T, D = dh.shape
    Fh = wd.shape[0]
    last = T // tt - 1

    def body(dh_ref, gt_ref, up_ref, w_ref, dgu_ref, gw_ref, acc):
        @pl.when(pl.program_id(0) == 0)
        def _():
            acc[...] = jnp.zeros_like(acc)

        dh16 = dh_ref[...].astype(BF16)
        for c0 in range(0, Fh, SWIGLU_COLS):
            cols = slice(c0, c0 + SWIGLU_COLS)
            gt = gt_ref[:, cols].astype(F32)
            up = up_ref[:, cols].astype(F32)
            s = _sigmoid(gt)
            silu = gt * s
            dact = _dot_nt(dh16, w_ref[cols, :])
            acc[cols, :] += _dot_tn((silu * up).astype(BF16), dh16)
            dgu_ref[:, cols] = (dact * up * (s * (1.0 + gt * (1.0 - s)))).astype(BF16)
            dgu_ref[:, Fh + c0:Fh + c0 + SWIGLU_COLS] = (dact * silu).astype(BF16)

        @pl.when(pl.program_id(0) == last)
        def _():
            gw_ref[...] = acc[...].astype(BF16)

    return pl.pallas_call(
        body, grid=(T // tt,),
        in_specs=[pl.BlockSpec((tt, D), lambda i: (i, 0)), pl.BlockSpec((tt, Fh), lambda i: (i, 0)),
                  pl.BlockSpec((tt, Fh), lambda i: (i, 1)), _full((Fh, D))],
        out_specs=[pl.BlockSpec((tt, 2 * Fh), lambda i: (i, 0)), _full((Fh, D))],
        out_shape=[jax.ShapeDtypeStruct((T, 2 * Fh), BF16), jax.ShapeDtypeStruct((Fh, D), BF16)],
        scratch_shapes=[pltpu.VMEM((Fh, D), F32)],
        compiler_params=_cp("arbitrary"), name=name)(dh, gu, gu, wd)


def _out_proj_bwd(dy, a1, a2, w, *, tt, name, head_dots=False):
    T, N = dy.shape
    K1, K2 = a1.shape[1], a2.shape[1]
    K = K1 + K2
    last = T // tt - 1

    def body(dy_ref, a1_ref, a2_ref, w_ref, da_ref, gw_ref, *rest):
        acc = rest[-1]

        @pl.when(pl.program_id(0) == 0)
        def _():
            acc[...] = jnp.zeros_like(acc)

        dy16 = dy_ref[...].astype(BF16)
        da = _dot_nt(dy16, w_ref[...])
        da_ref[...] = da
        acc[:K1, :] += _dot_tn(a1_ref[...], dy16)
        acc[K1:, :] += _dot_tn(a2_ref[...], dy16)
        if head_dots:
            for h in range(K1 // HEAD_DIM):
                sl = slice(h * HEAD_DIM, (h + 1) * HEAD_DIM)
                rest[0][:, sl] = jnp.broadcast_to(jnp.sum(da[:, sl] * a1_ref[:, sl], axis=-1, keepdims=True), (tt, HEAD_DIM))

        @pl.when(pl.program_id(0) == last)
        def _():
            gw_ref[...] = acc[...].astype(BF16)

    extra_specs = [pl.BlockSpec((tt, K1), lambda i: (i, 0))] if head_dots else []
    extra_shapes = [jax.ShapeDtypeStruct((T, K1), F32)] if head_dots else []
    return pl.pallas_call(
        body, grid=(T // tt,),
        in_specs=[pl.BlockSpec((tt, N), lambda i: (i, 0)), pl.BlockSpec((tt, K1), lambda i: (i, 0)),
                  pl.BlockSpec((tt, K2), lambda i: (i, 0)), _full((K, N))],
        out_specs=[pl.BlockSpec((tt, K), lambda i: (i, 0)), _full((K, N))] + extra_specs,
        out_shape=[jax.ShapeDtypeStruct((T, K), F32), jax.ShapeDtypeStruct((K, N), BF16)] + extra_shapes,
        scratch_shapes=[pltpu.VMEM((K, N), F32)],
        compiler_params=_cp("arbitrary"), name=name)(dy, a1, a2, w)


def _mm_tn(a, b, *, tt, tka, name):
    T, Ka = a.shape
    N = b.shape[1]
    last = T // tt - 1

    def body(a_ref, b_ref, o_ref, acc):
        @pl.when(pl.program_id(1) == 0)
        def _():
            acc[...] = jnp.zeros_like(acc)

        acc[...] += _dot_tn(a_ref[...], b_ref[...])

        @pl.when(pl.program_id(1) == last)
        def _():
            o_ref[...] = acc[...].astype(BF16)

    return pl.pallas_call(
        body, grid=(Ka // tka, T // tt),
        in_specs=[pl.BlockSpec((tt, tka), lambda j, t: (t, j)), pl.BlockSpec((tt, N), lambda j, t: (t, 0))],
        out_specs=pl.BlockSpec((tka, N), lambda j, t: (j, 0)),
        out_shape=jax.ShapeDtypeStruct((Ka, N), BF16),
        scratch_shapes=[pltpu.VMEM((tka, N), F32)],
        compiler_params=_cp("parallel", "arbitrary"), name=name)(a, b)


def _mm_tn_pieces(pieces, b, *, tt, name):
    n = len(pieces)
    T = b.shape[0]
    N = b.shape[1]
    widths = [p.shape[1] for p in pieces]
    Ka = sum(widths)
    last = T // tt - 1

    def body(*refs):
        p_refs = refs[:n]
        b_ref, o_ref, acc = refs[n:]

        @pl.when(pl.program_id(0) == 0)
        def _():
            acc[...] = jnp.zeros_like(acc)

        bv = b_ref[...].astype(BF16)
        off = 0
        for p_ref, wd in zip(p_refs, widths):
            acc[off:off + wd, :] += _dot_tn(p_ref[...], bv)
            off += wd

        @pl.when(pl.program_id(0) == last)
        def _():
            o_ref[...] = acc[...].astype(BF16)

    return pl.pallas_call(
        body, grid=(T // tt,),
        in_specs=[pl.BlockSpec((tt, wd), lambda t: (t, 0)) for wd in widths] + [pl.BlockSpec((tt, N), lambda t: (t, 0))],
        out_specs=_full((Ka, N)), out_shape=jax.ShapeDtypeStruct((Ka, N), BF16),
        scratch_shapes=[pltpu.VMEM((Ka, N), F32)],
        compiler_params=_cp("arbitrary"), name=name)(*pieces, b)


def _rms_bwd_dx(x, g, w, dy, dres, *, tt, wt, name, dep=None):
    pieces = list(dy) if isinstance(dy, (list, tuple)) else [dy]
    n = len(pieces)
    widths = [p.shape[1] for p in pieces]
    T, K = x.shape

    def kernel_body(x_ref, g_ref, w_ref, *rest):
        dy_refs = rest[:n]
        dres_ref, dx_ref, dg_ref = rest[n:]

        @pl.when(pl.program_id(0) == 0)
        def _():
            dg_ref[...] = jnp.zeros_like(dg_ref)

        if n == 1:
            dxn = (_dot if wt else _dot_nt)(dy_refs[0][...], w_ref[...])
        else:
            dxn, off = 0.0, 0
            for dy_ref, wd in zip(dy_refs, widths):
                dxn = dxn + _dot(dy_ref[...], w_ref[off:off + wd, :])
                off += wd
        xf = x_ref[...]
        r = lax.rsqrt(jnp.mean(xf * xf, axis=-1, keepdims=True) + EPS)
        xhat = xf * r
        dg_ref[...] += jnp.sum(dxn * xhat, axis=0, keepdims=True)
        dxhat = dxn * g_ref[...]
        dx_ref[...] = dres_ref[...] + r * (dxhat - xhat * jnp.mean(dxhat * xhat, axis=-1, keepdims=True))

    assert n == 1 or wt
    body, dep_specs, dep_args = _dep(kernel_body, 4 + n, dep)
    return pl.pallas_call(
        body, grid=(T // tt,),
        in_specs=[pl.BlockSpec((tt, K), lambda i: (i, 0)), _full((1, K)), _full(w.shape)]
        + [pl.BlockSpec((tt, wd), lambda i: (i, 0)) for wd in widths]
        + [pl.BlockSpec((tt, K), lambda i: (i, 0))] + dep_specs,
        out_specs=[pl.BlockSpec((tt, K), lambda i: (i, 0)), _full((1, K))],
        out_shape=[jax.ShapeDtypeStruct((T, K), F32), jax.ShapeDtypeStruct((1, K), F32)],
        compiler_params=_cp("arbitrary"), name=name)(x, g, w, *pieces, dres, *dep_args)


HGRN_TB = 512
HGRN_NCH = HGRN_TB // CHUNK
HGRN_UNROLL = 8
HGRN_HPB = 6


def _hgrn_chunk_fwd(q, z, lbv, tril01):
    sig = _sigmoid(z)
    f = lbv + (1.0 - lbv) * sig
    kk = 1.0 - f
    b = _dot3(tril01, jnp.log(f))
    bend = b[CHUNK - 1:CHUNK, :]
    sq = _sigmoid(q)
    eb = jnp.exp(b)
    emb = jnp.exp(-b)
    eo = jnp.exp(bend - b)
    dec = jnp.exp(bend)
    return sig, f, kk, sq, eb, emb, eo, dec


def _hgrn2_fwd(proj, lb, *, name):
    T = proj.shape[0]
    nT = T // HGRN_TB
    nC = T // CHUNK

    def body(q_ref, z_ref, v_ref, lb_ref, o_ref, st_ref, state):
        @pl.when(pl.program_id(1) == 0)
        def _():
            state[...] = jnp.zeros_like(state)

        row = lax.broadcasted_iota(jnp.int32, (CHUNK, CHUNK), 0)
        col = lax.broadcasted_iota(jnp.int32, (CHUNK, CHUNK), 1)
        causal = row >= col
        tril01 = causal.astype(BF16)

        def chunk(c, carry):
            rows = pl.ds(pl.multiple_of(c * CHUNK, CHUNK), CHUNK)
            for hh in range(HGRN_HPB):
                sl = slice(hh * HEAD_DIM, (hh + 1) * HEAD_DIM)
                q = q_ref[rows, sl]
                v = v_ref[rows, sl].astype(BF16)
                sig, f, kk, sq, eb, emb, eo, dec = _hgrn_chunk_fwd(q, z_ref[rows, sl], lb_ref[:, sl], tril01)
                qi = (q * sq * eb).astype(BF16)
                ki = (kk * emb).astype(BF16)
                ko = (kk * eo).astype(BF16)
                st = state[hh]
                att = jnp.where(causal, _dot_nt(qi, ki), 0.0)
                o_ref[rows, sl] = _dot(att, v) + _dot_nt(qi, st)
                st_ref[c, hh] = st
                state[hh] = st * dec + _dot_tn(v, ko)
            return carry

        lax.fori_loop(0, HGRN_NCH, chunk, 0, unroll=HGRN_UNROLL)

    W = HGRN_HPB * HEAD_DIM
    nG = A_HEADS // HGRN_HPB
    hb = lambda off: pl.BlockSpec((HGRN_TB, W), lambda h, i: (i, off + h))
    return pl.pallas_call(
        body, grid=(nG, nT),
        in_specs=[hb(0), hb(nG), hb(2 * nG), pl.BlockSpec((1, W), lambda h, i: (0, h))],
        out_specs=[hb(0), pl.BlockSpec((HGRN_NCH, HGRN_HPB, HEAD_DIM, HEAD_DIM), lambda h, i: (i, h, 0, 0))],
        out_shape=[jax.ShapeDtypeStruct((T, A_WIDTH), F32), jax.ShapeDtypeStruct((nC, A_HEADS, HEAD_DIM, HEAD_DIM), F32)],
        scratch_shapes=[pltpu.VMEM((HGRN_HPB, HEAD_DIM, HEAD_DIM), F32)],
        compiler_params=_cp("parallel", "arbitrary"), name=name)(proj, proj, proj, lb)


def _hgrn2_bwd(proj, lb, st_all, do, *, name):
    T = proj.shape[0]
    nT = T // HGRN_TB

    def body(q_ref, z_ref, v_ref, lb_ref, st_ref, do_ref, dq_ref, dz_ref, dv_ref, dlb_ref, dstate):
        @pl.when(pl.program_id(1) == 0)
        def _():
            dstate[...] = jnp.zeros_like(dstate)
            dlb_ref[...] = jnp.zeros_like(dlb_ref)

        row = lax.broadcasted_iota(jnp.int32, (CHUNK, CHUNK), 0)
        col = lax.broadcasted_iota(jnp.int32, (CHUNK, CHUNK), 1)
        causal = row >= col
        tril01 = causal.astype(BF16)
        triu01 = (row <= col).astype(BF16)

        def chunk(cc, carry):
            c = HGRN_NCH - 1 - cc
            rows = pl.ds(pl.multiple_of(c * CHUNK, CHUNK), CHUNK)
            for hh in range(HGRN_HPB):
                sl = slice(hh * HEAD_DIM, (hh + 1) * HEAD_DIM)
                lbv = lb_ref[:, sl]
                q = q_ref[rows, sl]
                v = v_ref[rows, sl].astype(BF16)
                sig, f, kk, sq, eb, emb, eo, dec = _hgrn_chunk_fwd(q, z_ref[rows, sl], lbv, tril01)
                qi32 = q * sq * eb
                ki32 = kk * emb
                ko32 = kk * eo
                qi, ki, ko = qi32.astype(BF16), ki32.astype(BF16), ko32.astype(BF16)
                att = jnp.where(causal, _dot_nt(qi, ki), 0.0).astype(BF16)
                dout = do_ref[rows, sl].astype(BF16)
                st = st_ref[c, hh]
                dst = dstate[hh]
                dst16 = dst.astype(BF16)
                datt = jnp.where(causal, _dot_nt(dout, v), 0.0).astype(BF16)
                dqi = _dot(datt, ki) + _dot(dout, st)
                dki = _dot_tn(datt, qi)
                dv_ref[rows, sl] = (_dot_tn(att, dout) + _dot_nt(ko, dst16)).astype(BF16)
                dko = _dot(v, dst16)
                ddec = jnp.sum(dst * st, axis=0, keepdims=True)
                dstate[hh] = dst * dec + _dot_tn(dout, qi)
                dkk = dki * emb + dko * eo
                db = dqi * qi32 - dki * ki32 - dko * ko32
                dbend = jnp.sum(dko * ko32, axis=0, keepdims=True) + ddec * dec
                dlogf = _dot3(triu01, db) + dbend
                df = dlogf / f - dkk
                dz_ref[rows, sl] = (df * (1.0 - lbv) * sig * (1.0 - sig)).astype(BF16)
                dlb_ref[:, sl] += jnp.sum(df * (1.0 - sig), axis=0, keepdims=True)
                dq_ref[rows, sl] = (dqi * eb * (sq * (1.0 + q * (1.0 - sq)))).astype(BF16)
            return carry

        lax.fori_loop(0, HGRN_NCH, chunk, 0, unroll=HGRN_UNROLL)

    W = HGRN_HPB * HEAD_DIM
    nG = A_HEADS // HGRN_HPB
    hb = lambda off: pl.BlockSpec((HGRN_TB, W), lambda h, i: (nT - 1 - i, off + h))
    hlb = pl.BlockSpec((1, W), lambda h, i: (0, h))
    o16 = jax.ShapeDtypeStruct((T, A_WIDTH), BF16)
    return pl.pallas_call(
        body, grid=(nG, nT),
        in_specs=[hb(0), hb(nG), hb(2 * nG), hlb,
                  pl.BlockSpec((HGRN_NCH, HGRN_HPB, HEAD_DIM, HEAD_DIM), lambda h, i: (nT - 1 - i, h, 0, 0)), hb(0)],
        out_specs=[hb(0), hb(0), hb(0), hlb],
        out_shape=[o16, o16, o16, jax.ShapeDtypeStruct((1, A_WIDTH), F32)],
        scratch_shapes=[pltpu.VMEM((HGRN_HPB, HEAD_DIM, HEAD_DIM), F32)],
        compiler_params=_cp("parallel", "arbitrary"), name=name)(proj, proj, proj, lb, st_all, do)


def _head_rms(x):
    r = lax.rsqrt(jnp.mean(x * x, axis=-1, keepdims=True) + EPS)
    return x * r, r


def _head_rms_bwd(dxhat, xhat, r):
    return r * (dxhat - xhat * jnp.mean(dxhat * xhat, axis=-1, keepdims=True))


def _a_post_fwd(o, proj, onorm, *, tt, name):
    T = o.shape[0]

    def body(o_ref, g_ref, w_ref, y_ref):
        for h in range(A_HEADS):
            sl = slice(h * HEAD_DIM, (h + 1) * HEAD_DIM)
            xhat, _ = _head_rms(o_ref[:, sl])
            g = g_ref[:, sl]
            y_ref[:, sl] = xhat * w_ref[:, sl] * (g * _sigmoid(g))

    blk = lambda c: pl.BlockSpec((tt, A_WIDTH), lambda i: (i, c))
    return pl.pallas_call(
        body, grid=(T // tt,), in_specs=[blk(0), blk(3), _full((1, A_WIDTH))], out_specs=blk(0),
        out_shape=jax.ShapeDtypeStruct((T, A_WIDTH), F32),
        compiler_params=_cp("parallel"), name=name)(o, proj, onorm)


def _a_post_bwd(o, proj, onorm, dmix, *, tt, name, dep=None):
    T = o.shape[0]

    def kernel_body(o_ref, g_ref, w_ref, dy_ref, do_ref, dg_ref, dw_ref):
        @pl.when(pl.program_id(0) == 0)
        def _():
            dw_ref[...] = jnp.zeros_like(dw_ref)

        for h in range(A_HEADS):
            sl = slice(h * HEAD_DIM, (h + 1) * HEAD_DIM)
            xhat, r = _head_rms(o_ref[:, sl])
            g = g_ref[:, sl]
            s = _sigmoid(g)
            dy = dy_ref[:, sl]
            w = w_ref[:, sl]
            dg_ref[:, sl] = (dy * xhat * w * (s * (1.0 + g * (1.0 - s)))).astype(BF16)
            dyn = dy * (g * s)
            dw_ref[:, sl] += jnp.sum(dyn * xhat, axis=0, keepdims=True)
            do_ref[:, sl] = _head_rms_bwd(dyn * w, xhat, r)

    blk = lambda c: pl.BlockSpec((tt, A_WIDTH), lambda i: (i, c))
    body, dep_specs, dep_args = _dep(kernel_body, 4, dep)
    return pl.pallas_call(
        body, grid=(T // tt,), in_specs=[blk(0), blk(3), _full((1, A_WIDTH)), blk(0)] + dep_specs,
        out_specs=[blk(0), blk(0), _full((1, A_WIDTH))],
        out_shape=[jax.ShapeDtypeStruct((T, A_WIDTH), F32), jax.ShapeDtypeStruct((T, A_WIDTH), BF16),
                   jax.ShapeDtypeStruct((1, A_WIDTH), F32)],
        compiler_params=_cp("arbitrary"), name=name)(o, proj, onorm, dmix, *dep_args)


def _mem_head_masks(n):
    lane = lax.broadcasted_iota(jnp.int32, (n, MEM_WIDTH), 1)
    return [(lane >= m * MEM_HEAD_DIM) & (lane < (m + 1) * MEM_HEAD_DIM) for m in range(MEM_HEADS)]


def _mem_head_rms(x, masks):
    x2 = x * x
    r = jnp.zeros_like(x)
    for mk in masks:
        ms = jnp.sum(jnp.where(mk, x2, 0.0), axis=-1, keepdims=True) * (1.0 / MEM_HEAD_DIM)
        r = jnp.where(mk, lax.rsqrt(ms + EPS), r)
    return x * r, r


def _mem_head_rms_bwd(dxhat, xhat, r, masks):
    t = dxhat * xhat
    m = jnp.zeros_like(t)
    for mk in masks:
        m = jnp.where(mk, jnp.sum(jnp.where(mk, t, 0.0), axis=-1, keepdims=True) * (1.0 / MEM_HEAD_DIM), m)
    return r * (dxhat - xhat * m)


MEM_SCALE = MEM_HEAD_DIM ** -0.5


def _mem_attn_fwd(proj, qcol, mkv, qn_w, kn_w, *, tt, name):
    T = proj.shape[0]

    def body(q_ref, k_ref, v_ref, qw_ref, kw_ref, o_ref):
        qmasks = _mem_head_masks(tt)
        kmasks = _mem_head_masks(MEM_TOKENS)
        qhat, _ = _mem_head_rms(q_ref[...], qmasks)
        qn = qhat * qw_ref[...]
        khat, _ = _mem_head_rms(k_ref[...], kmasks)
        kn = (khat * kw_ref[...]).astype(BF16)
        v = v_ref[...].astype(BF16)
        out = jnp.zeros((tt, MEM_WIDTH), F32)
        for m in range(MEM_HEADS):
            s = _dot_nt(jnp.where(qmasks[m], qn, 0.0), kn) * MEM_SCALE
            s = s - jnp.max(s, axis=-1, keepdims=True)
            p = jnp.exp(s)
            p = p / jnp.sum(p, axis=-1, keepdims=True)
            out = jnp.where(qmasks[m], _dot(p, v), out)
        o_ref[...] = out

    return pl.pallas_call(
        body, grid=(T // tt,),
        in_specs=[pl.BlockSpec((tt, MEM_WIDTH), lambda i: (i, qcol)), pl.BlockSpec((MEM_TOKENS, MEM_WIDTH), lambda i: (0, 0)),
                  pl.BlockSpec((MEM_TOKENS, MEM_WIDTH), lambda i: (0, 1)), _full((1, MEM_WIDTH)), _full((1, MEM_WIDTH))],
        out_specs=pl.BlockSpec((tt, MEM_WIDTH), lambda i: (i, 0)),
        out_shape=jax.ShapeDtypeStruct((T, MEM_WIDTH), F32),
        compiler_params=_cp("parallel"), name=name)(proj, mkv, mkv, qn_w, kn_w)


def _mem_attn_bwd(proj, qcol, mkv, qn_w, kn_w, dmix, *, tt, name):
    T = proj.shape[0]
    nsteps = T // tt
    ocol = (dmix.shape[1] - MEM_WIDTH) // MEM_WIDTH

    def body(q_ref, k_ref, v_ref, qw_ref, kw_ref, do_ref, dq_ref, dkv_ref, dqw_ref, dkw_ref, dk_acc, dv_acc):
        step = pl.program_id(0)

        @pl.when(step == 0)
        def _():
            dk_acc[...] = jnp.zeros_like(dk_acc)
            dv_acc[...] = jnp.zeros_like(dv_acc)
            dqw_ref[...] = jnp.zeros_like(dqw_ref)

        qmasks = _mem_head_masks(tt)
        kmasks = _mem_head_masks(MEM_TOKENS)
        qhat, qr = _mem_head_rms(q_ref[...], qmasks)
        qn = qhat * qw_ref[...]
        khat, kr = _mem_head_rms(k_ref[...], kmasks)
        kn = (khat * kw_ref[...]).astype(BF16)
        v = v_ref[...].astype(BF16)
        dout = do_ref[...]
        dqn = jnp.zeros((tt, MEM_WIDTH), F32)
        dkn = jnp.zeros((MEM_TOKENS, MEM_WIDTH), F32)
        dvv = jnp.zeros((MEM_TOKENS, MEM_WIDTH), F32)
        for m in range(MEM_HEADS):
            qm = jnp.where(qmasks[m], qn, 0.0).astype(BF16)
            s = _dot_nt(qm, kn) * MEM_SCALE
            s = s - jnp.max(s, axis=-1, keepdims=True)
            p = jnp.exp(s)
            p = p / jnp.sum(p, axis=-1, keepdims=True)
            dom = jnp.where(qmasks[m], dout, 0.0).astype(BF16)
            dp = _dot_nt(dom, v)
            ds = (p * (dp - jnp.sum(p * dp, axis=-1, keepdims=True)) * MEM_SCALE).astype(BF16)
            dqn = jnp.where(qmasks[m], _dot(ds, kn), dqn)
            dkn = jnp.where(kmasks[m], _dot_tn(ds, qm), dkn)
            dvv = jnp.where(kmasks[m], _dot_tn(p, dom), dvv)
        dqw_ref[...] += jnp.sum(dqn * qhat, axis=0, keepdims=True)
        dq_ref[...] = _mem_head_rms_bwd(dqn * qw_ref[...], qhat, qr, qmasks).astype(BF16)
        dk_acc[...] += dkn
        dv_acc[...] += dvv

        @pl.when(step == nsteps - 1)
        def _():
            dk = dk_acc[...]
            dkw_ref[...] = jnp.sum(dk * khat, axis=0, keepdims=True)
            dkv_ref[:, :MEM_WIDTH] = _mem_head_rms_bwd(dk * kw_ref[...], khat, kr, kmasks)
            dkv_ref[:, MEM_WIDTH:] = dv_acc[...]

    return pl.pallas_call(
        body, grid=(nsteps,),
        in_specs=[pl.BlockSpec((tt, MEM_WIDTH), lambda i: (i, qcol)), pl.BlockSpec((MEM_TOKENS, MEM_WIDTH), lambda i: (0, 0)),
                  pl.BlockSpec((MEM_TOKENS, MEM_WIDTH), lambda i: (0, 1)), _full((1, MEM_WIDTH)), _full((1, MEM_WIDTH)),
                  pl.BlockSpec((tt, MEM_WIDTH), lambda i: (i, ocol))],
        out_specs=[pl.BlockSpec((tt, MEM_WIDTH), lambda i: (i, 0)), _full((MEM_TOKENS, 2 * MEM_WIDTH)),
                   _full((1, MEM_WIDTH)), _full((1, MEM_WIDTH))],
        out_shape=[jax.ShapeDtypeStruct((T, MEM_WIDTH), BF16), jax.ShapeDtypeStruct((MEM_TOKENS, 2 * MEM_WIDTH), F32),
                   jax.ShapeDtypeStruct((1, MEM_WIDTH), F32), jax.ShapeDtypeStruct((1, MEM_WIDTH), F32)],
        scratch_shapes=[pltpu.VMEM((MEM_TOKENS, MEM_WIDTH), F32), pltpu.VMEM((MEM_TOKENS, MEM_WIDTH), F32)],
        compiler_params=_cp("arbitrary"), name=name)(proj, mkv, mkv, qn_w, kn_w, dmix)


HALF = HEAD_DIM // 2
ATT_SCALE = HEAD_DIM ** -0.5
NEG = -1e30


def _rope_tables(T):
    inv = np.float32(ROPE_THETA) ** (-np.arange(HALF, dtype=np.float32) / np.float32(HALF))
    ang = np.arange(T, dtype=np.float32)[:, None] * inv[None, :].astype(np.float32)
    cos, sin = np.cos(ang).astype(np.float32), np.sin(ang).astype(np.float32)
    return jnp.asarray(np.concatenate([cos, cos], axis=-1)), jnp.asarray(np.concatenate([-sin, sin], axis=-1))


def _rope(x, cosf, sinsg):
    return x * cosf + pltpu.roll(x, HALF, 1) * sinsg


def _rope_bwd(dy, cosf, sinsg):
    return dy * cosf + pltpu.roll(dy * sinsg, HALF, 1)


def _q_prep_bwd(proj, w_heads, cosf, sinsg, dqs, *, tt, name):
    T = proj.shape[0]
    W = N_GROUPS * B_WIDTH

    def body(x_ref, w_ref, c_ref, s_ref, d0, d1, d2, dx_ref, dw_ref):
        @pl.when(pl.program_id(0) == 0)
        def _():
            dw_ref[...] = jnp.zeros_like(dw_ref)

        c, s = c_ref[...], s_ref[...]
        for gi, d_ref in enumerate((d0, d1, d2)):
            for h in range(B_HEADS):
                sl = slice((gi * B_HEADS + h) * HEAD_DIM, (gi * B_HEADS + h + 1) * HEAD_DIM)
                xhat, r = _head_rms(x_ref[:, sl])
                dyn = _rope_bwd(d_ref[:, h * HEAD_DIM:(h + 1) * HEAD_DIM], c, s)
                dw_ref[:, sl] += jnp.sum(dyn * xhat, axis=0, keepdims=True)
                dx_ref[:, sl] = _head_rms_bwd(dyn * w_ref[:, sl], xhat, r).astype(BF16)

    tbl = pl.BlockSpec((tt, HEAD_DIM), lambda i: (i, 0))
    dyb = pl.BlockSpec((tt, B_WIDTH), lambda i: (i, 0))
    return pl.pallas_call(
        body, grid=(T // tt,),
        in_specs=[pl.BlockSpec((tt, W), lambda i: (i, 0)), _full((1, W)), tbl, tbl, dyb, dyb, dyb],
        out_specs=[pl.BlockSpec((tt, W), lambda i: (i, 0)), _full((1, W))],
        out_shape=[jax.ShapeDtypeStruct((T, W), BF16), jax.ShapeDtypeStruct((1, W), F32)],
        compiler_params=_cp("arbitrary"), name=name)(proj, w_heads, cosf, sinsg, *dqs)


def _kv_prep_bwd(kv, w_heads, cosf, sinsg, dks, dvs, *, tt, name):
    T = kv.shape[0]

    def body(x_ref, w_ref, c_ref, s_ref, k0, k1, k2, v0, v1, v2, dx_ref, dw_ref):
        @pl.when(pl.program_id(0) == 0)
        def _():
            dw_ref[...] = jnp.zeros_like(dw_ref)

        c, s = c_ref[...], s_ref[...]
        for h in range(B_HEADS):
            sl = slice(h * HEAD_DIM, (h + 1) * HEAD_DIM)
            vs = slice(B_WIDTH + h * HEAD_DIM, B_WIDTH + (h + 1) * HEAD_DIM)
            xhat, r = _head_rms(x_ref[:, sl])
            dyn = _rope_bwd(k0[:, sl] + k1[:, sl] + k2[:, sl], c, s)
            dw_ref[:, sl] += jnp.sum(dyn * xhat, axis=0, keepdims=True)
            dx_ref[:, sl] = _head_rms_bwd(dyn * w_ref[:, sl], xhat, r).astype(BF16)
            dx_ref[:, vs] = (v0[:, sl] + v1[:, sl] + v2[:, sl]).astype(BF16)

    tbl = pl.BlockSpec((tt, HEAD_DIM), lambda i: (i, 0))
    dyb = pl.BlockSpec((tt, B_WIDTH), lambda i: (i, 0))
    return pl.pallas_call(
        body, grid=(T // tt,),
        in_specs=[dyb, _full((1, B_WIDTH)), tbl, tbl] + [dyb] * 6,
        out_specs=[pl.BlockSpec((tt, 2 * B_WIDTH), lambda i: (i, 0)), _full((1, B_WIDTH))],
        out_shape=[jax.ShapeDtypeStruct((T, 2 * B_WIDTH), BF16), jax.ShapeDtypeStruct((1, B_WIDTH), F32)],
        compiler_params=_cp("arbitrary"), name=name)(kv, w_heads, cosf, sinsg, *dks, *dvs)


def _band_masks(n_is_first=None):
    row = lax.broadcasted_iota(jnp.int32, (SPAN, SPAN), 0)
    col = lax.broadcasted_iota(jnp.int32, (SPAN, SPAN), 1)
    return row >= col, col >= row


def _dil_views(T, d):
    L = T // d
    return L, L // SPAN


def _dil_fwd(qr, kr, kv, gi, d, *, name):
    T = qr.shape[0]
    L, nb = _dil_views(T, d)

    def body(q_ref, kc_ref, kp_ref, vc_ref, vp_ref, o_ref, lse_ref):
        cur_ok, prev_band = _band_masks()
        prev_ok = prev_band & (pl.program_id(1) > 0)
        for h in range(B_HEADS):
            sl = slice(h * HEAD_DIM, (h + 1) * HEAD_DIM)
            q = q_ref[:, sl]
            sc = jnp.where(cur_ok, _dot_nt(q, kc_ref[:, sl]) * ATT_SCALE, NEG)
            sp = jnp.where(prev_ok, _dot_nt(q, kp_ref[:, sl]) * ATT_SCALE, NEG)
            m = jnp.maximum(jnp.max(sc, axis=-1, keepdims=True), jnp.max(sp, axis=-1, keepdims=True))
            pc = jnp.exp(sc - m)
            pp = jnp.exp(sp - m)
            l = jnp.sum(pc, axis=-1, keepdims=True) + jnp.sum(pp, axis=-1, keepdims=True)
            o_ref[:, sl] = (_dot(pc, vc_ref[:, sl]) + _dot(pp, vp_ref[:, sl])) / l
            lse_ref[:, sl] = jnp.broadcast_to(m + jnp.log(l), (SPAN, HEAD_DIM))

    blk = lambda f: pl.BlockSpec((SPAN, B_WIDTH), f)
    cur = lambda r, n: (n, r)
    prev = lambda r, n: (jnp.maximum(n - 1, 0), r)
    ov = jax.ShapeDtypeStruct((L, d * B_WIDTH), F32)
    o, lse = pl.pallas_call(
        body, grid=(d, nb),
        in_specs=[blk(lambda r, n: (n, r * N_GROUPS + gi)), blk(cur), blk(prev),
                  blk(lambda r, n: (n, 2 * r + 1)), blk(lambda r, n: (jnp.maximum(n - 1, 0), 2 * r + 1))],
        out_specs=[blk(cur), blk(cur)], out_shape=[ov, ov],
        compiler_params=_cp("parallel", "arbitrary"), name=name,
    )(qr.reshape(L, d * N_GROUPS * B_WIDTH), kr.reshape(L, d * B_WIDTH), kr.reshape(L, d * B_WIDTH),
      kv.reshape(L, d * 2 * B_WIDTH), kv.reshape(L, d * 2 * B_WIDTH))
    return o.reshape(T, B_WIDTH), lse.reshape(T, B_WIDTH)


def _combine_out_proj(res, os_, lses, mem_out, w, *, tt, name):
    T, N = res.shape
    K2 = mem_out.shape[1]

    def body(r_ref, o0, o1, o2, l0, l1, l2, a2_ref, w_ref, h_ref, y_ref, lse_ref):
        a, b, c = l0[...], l1[...], l2[...]
        m = jnp.maximum(jnp.maximum(a, b), c)
        wa, wb, wc = jnp.exp(a - m), jnp.exp(b - m), jnp.exp(c - m)
        den = wa + wb + wc
        y = (wa * o0[...] + wb * o1[...] + wc * o2[...]) / den
        y_ref[...] = y
        lse_ref[...] = m + jnp.log(den)
        h_ref[...] = r_ref[...] + _dot(y, w_ref[:B_WIDTH, :]) + _dot(a2_ref[...], w_ref[B_WIDTH:, :])

    blk = pl.BlockSpec((tt, B_WIDTH), lambda i: (i, 0))
    row = pl.BlockSpec((tt, N), lambda i: (i, 0))
    sh = jax.ShapeDtypeStruct((T, B_WIDTH), F32)
    return pl.pallas_call(
        body, grid=(T // tt,),
        in_specs=[row] + [blk] * 6 + [pl.BlockSpec((tt, K2), lambda i: (i, 0)), _full((B_WIDTH + K2, N))],
        out_specs=[row, blk, blk], out_shape=[jax.ShapeDtypeStruct((T, N), F32), sh, sh],
        compiler_params=_cp("parallel"), name=name)(res, *os_, *lses, mem_out, w)


DILS_UNROLL = 8


def _dils_specs(gi, d, nblk):
    blk = lambda f: pl.BlockSpec((SPAN * d, HEAD_DIM), f)
    return {
        "q": blk(lambda h, n: (n, gi * B_HEADS + h)), "q_next": blk(lambda h, n: (jnp.minimum(n + 1, nblk - 1), gi * B_HEADS + h)),
        "cur": blk(lambda h, n: (n, h)), "prev": blk(lambda h, n: (jnp.maximum(n - 1, 0), h)),
        "next": blk(lambda h, n: (jnp.minimum(n + 1, nblk - 1), h)),
        "v": blk(lambda h, n: (n, B_HEADS + h)), "v_prev": blk(lambda h, n: (jnp.maximum(n - 1, 0), B_HEADS + h)),
    }


def _dils_fwd(qr, kr, kv, gi, d, *, name):
    T = qr.shape[0]
    nblk = T // (SPAN * d)
    sp = _dils_specs(gi, d, nblk)

    def body(q_ref, kc_ref, vc_ref, o_ref, lse_ref, k_before, v_before):
        @pl.when(pl.program_id(1) == 0)
        def _():
            k_before[...] = jnp.zeros_like(k_before)
            v_before[...] = jnp.zeros_like(v_before)

        cur_ok, prev_band = _band_masks()
        prev_ok = prev_band & (pl.program_id(1) > 0)

        def residue(r, carry):
            rows = pl.ds(r, SPAN, stride=d)
            q, kc, vc = q_ref[rows, :], kc_ref[rows, :].astype(BF16), vc_ref[rows, :].astype(BF16)
            sc = jnp.where(cur_ok, _dot_nt(q, kc) * ATT_SCALE, NEG)
            sp_ = jnp.where(prev_ok, _dot_nt(q, k_before[r]) * ATT_SCALE, NEG)
            m = jnp.maximum(jnp.max(sc, axis=-1, keepdims=True), jnp.max(sp_, axis=-1, keepdims=True))
            pc = jnp.exp(sc - m)
            pp = jnp.exp(sp_ - m)
            l = jnp.sum(pc, axis=-1, keepdims=True) + jnp.sum(pp, axis=-1, keepdims=True)
            o_ref[rows, :] = (_dot(pc, vc) + _dot(pp, v_before[r])) / l
            lse_ref[rows, :] = jnp.broadcast_to(m + jnp.log(l), (SPAN, HEAD_DIM))
            k_before[r] = kc
            v_before[r] = vc
            return carry

        lax.fori_loop(0, d, residue, 0, unroll=min(d, DILS_UNROLL))

    sh = jax.ShapeDtypeStruct((T, B_WIDTH), F32)
    return pl.pallas_call(
        body, grid=(B_HEADS, nblk), in_specs=[sp["q"], sp["cur"], sp["v"]],
        out_specs=[sp["cur"], sp["cur"]], out_shape=[sh, sh],
        scratch_shapes=[pltpu.VMEM((d, SPAN, HEAD_DIM), BF16), pltpu.VMEM((d, SPAN, HEAD_DIM), BF16)],
        compiler_params=_cp("parallel", "arbitrary"), name=name)(qr, kr, kv)


DIL_BWD_GROUP = {1: 16, 4: 1, 16: 1}


def _dil_bwd(qr, kr, kv, dmix, lse, dd, gi, d, *, name, dep=None):
    T = qr.shape[0]
    G = DIL_BWD_GROUP[d]
    band = SPAN * d
    tb = G * band
    nblk = T // tb
    n_units = T // SPAN

    keep = G == 1

    def kernel_body(q_ref, dy_ref, lse_ref, dd_ref, kc_ref, vc_ref, *rest):
        if keep:
            dq_ref, dk_ref, dv_ref, dk_acc, dv_acc, k_before, v_before = rest
        else:
            kp_ref, vp_ref, dq_ref, dk_ref, dv_ref, dk_acc, dv_acc = rest
        n = pl.program_id(1)

        @pl.when(n == 0)
        def _():
            dk_acc[...] = jnp.zeros_like(dk_acc)
            dv_acc[...] = jnp.zeros_like(dv_acc)
            if keep:
                k_before[...] = jnp.zeros_like(k_before)
                v_before[...] = jnp.zeros_like(v_before)

        cur_ok, prev_band = _band_masks()
        for j in range(G):
            def residue(r, carry, j=j):
                off = j * band + r
                rows = pl.ds(off, SPAN, stride=d)
                q, dy = q_ref[rows, :], dy_ref[rows, :]
                lse_h = jnp.max(lse_ref[rows, :], axis=-1, keepdims=True)
                dd_h = jnp.max(dd_ref[rows, :], axis=-1, keepdims=True)
                kc, vc = kc_ref[rows, :].astype(BF16), vc_ref[rows, :].astype(BF16)
                if j > 0:
                    before = pl.ds(off - band, SPAN, stride=d)
                    kp, vp = kc_ref[before, :], vc_ref[before, :]
                    prev_ok = prev_band
                elif keep:
                    kp, vp = k_before[r], v_before[r]
                    k_before[r] = kc
                    v_before[r] = vc
                    prev_ok = prev_band & (n > 0)
                else:
                    before = pl.ds((G - 1) * band + r, SPAN, stride=d)
                    kp, vp = kp_ref[before, :], vp_ref[before, :]
                    prev_ok = prev_band & (n > 0)
                pc = jnp.exp(jnp.where(cur_ok, _dot_nt(q, kc) * ATT_SCALE, NEG) - lse_h)
                pp = jnp.exp(jnp.where(prev_ok, _dot_nt(q, kp) * ATT_SCALE, NEG) - lse_h)
                dsc = pc * (_dot_nt(dy, vc) - dd_h) * ATT_SCALE
                dsp = pp * (_dot_nt(dy, vp) - dd_h) * ATT_SCALE
                dq_ref[rows, :] = _dot(dsc, kc) + _dot(dsp, kp)
                u = (n * G + j) * d + r
                here = pl.ds(pl.multiple_of(u * SPAN, SPAN), SPAN)
                dk_acc[here, :] += _dot_tn(dsc, q)
                dv_acc[here, :] += _dot_tn(pc, dy)
                there = pl.ds(pl.multiple_of(jnp.maximum(u - d, 0) * SPAN, SPAN), SPAN)
                dk_acc[there, :] += _dot_tn(dsp, q)
                dv_acc[there, :] += _dot_tn(pp, dy)
                return carry

            lax.fori_loop(0, d, residue, 0, unroll=min(d, DILS_UNROLL))

        @pl.when(n == nblk - 1)
        def _():
            def place(u, carry):
                rows = pl.ds((u // d) * band + u % d, SPAN, stride=d)
                src = pl.ds(pl.multiple_of(u * SPAN, SPAN), SPAN)
                dk_ref[rows, :] = dk_acc[src, :]
                dv_ref[rows, :] = dv_acc[src, :]
                return carry

            lax.fori_loop(0, n_units, place, 0)

    blk = lambda f: pl.BlockSpec((tb, HEAD_DIM), f)
    cur = lambda h, n: (n, h)
    prev = lambda h, n: (jnp.maximum(n - 1, 0), h)
    whole = pl.BlockSpec((T, HEAD_DIM), lambda h, n: (0, h))
    sh = jax.ShapeDtypeStruct((T, B_WIDTH), F32)
    v_cur = blk(lambda h, n: (n, B_HEADS + h))
    if keep:
        kv_specs, kv_args = [blk(cur), v_cur], [kr, kv]
        kept = [pltpu.VMEM((d, SPAN, HEAD_DIM), BF16), pltpu.VMEM((d, SPAN, HEAD_DIM), BF16)]
    else:
        kv_specs = [blk(cur), v_cur, blk(prev), blk(lambda h, n: (jnp.maximum(n - 1, 0), B_HEADS + h))]
        kv_args, kept = [kr, kv, kr, kv], []
    body, dep_specs, dep_args = _dep(kernel_body, 4 + len(kv_args), dep)
    return pl.pallas_call(
        body, grid=(B_HEADS, nblk),
        in_specs=[blk(lambda h, n: (n, gi * B_HEADS + h)), blk(cur), blk(cur), blk(cur)] + kv_specs + dep_specs,
        out_specs=[blk(cur), whole, whole], out_shape=[sh, sh, sh],
        scratch_shapes=[pltpu.VMEM((T, HEAD_DIM), F32), pltpu.VMEM((T, HEAD_DIM), F32)] + kept,
        compiler_params=_cp("parallel", "arbitrary"), name=name)(qr, dmix, lse, dd, *kv_args, *dep_args)


A_MQ_COL = 4 * A_WIDTH // MEM_WIDTH
B_MQ_COL = N_GROUPS * B_WIDTH // MEM_WIDTH


def _row(v):
    return v.reshape(1, -1).astype(F32)


def _local_step(x, mem, tgt, get_w, P, put_g, first_dep=None, forward_point=lambda i, value: value):
    T = x.shape[0]
    cosf, sinsg = _rope_tables(T)
    lb_soft = jax.nn.softmax(P["a_lb_logits"].astype(F32), axis=0)
    lb = lb_soft[0:1]
    qw_heads = jnp.repeat(P["b_qnorm"][0], B_HEADS, axis=0).reshape(1, -1)
    kw_heads = jnp.tile(_row(P["b_knorm"]), (1, B_HEADS))
    mqw = [jnp.tile(_row(P["mem_qnorm"][l]), (1, MEM_HEADS)) for l in range(2)]
    mkw = [jnp.tile(_row(P["mem_knorm"][l]), (1, MEM_HEADS)) for l in range(2)]
    nmix = [_row(P["norm_mix"][l]) for l in range(2)]
    nffn = [_row(P["norm_ffn"][l]) for l in range(2)]
    mnorm = [_row(P["mem_norm"][l]) for l in range(2)]
    kvn = _row(P["kv_norm"])
    onorm = _row(P["a_onorm"])
    W = {}

    def w_of(name, after=None):
        if name not in W:
            W[name] = get_w(name, after)
        return W[name]

    proj_a, xn0 = _rms_matmul(x, nmix[0], w_of("a_w_in"), tt=512, tn=1664, wt=True, name="proj_a", dep=first_dep)
    mkv0, mn0 = _rms_matmul(mem, mnorm[0], w_of("w_mem_kv0"), tt=MEM_TOKENS, tn=2 * MEM_WIDTH, wt=False, name="mem_kv0")
    o_raw, st = _hgrn2_fwd(proj_a, lb, name="hgrn2_fwd")
    o_raw = forward_point(0, o_raw)
    mm0 = _a_post_fwd(o_raw, proj_a, onorm, tt=512, name="a_post_fwd")
    mo0 = _mem_attn_fwd(proj_a, A_MQ_COL, mkv0, mqw[0], mkw[0], tt=1024, name="mem_attn_fwd0")
    hm0 = _mm_res(x, mm0, mo0, w_of("w_out0", mo0), tt=512, name="out_proj0")
    hm0 = forward_point(1, hm0)
    gu0, hn0 = _rms_matmul(hm0, nffn[0], w_of("w_gate_up0", hm0), tt=512, tn=1408, wt=True, out_dtype=BF16, name="gate_up0")
    h1 = _swiglu_down(hm0, gu0, w_of("w_down0", gu0), tt=512, name="down0")
    h1 = forward_point(2, h1)
    kv, hkn, kr = _rms_matmul(h1, kvn, w_of("w_kv", h1), tt=512, tn=768, wt=True, name="kv_proj",
                              rotate=(kw_heads, cosf, sinsg))

    proj_b, xn1, qr = _rms_matmul(h1, nmix[1], w_of("b_w_in", kr), tt=512, tn=1280, wt=True, name="proj_b",
                                  rotate=(qw_heads, cosf, sinsg))
    proj_b = forward_point(3, proj_b)
    mkv1, mn1 = _rms_matmul(mem, mnorm[1], w_of("w_mem_kv1", kr), tt=MEM_TOKENS, tn=2 * MEM_WIDTH, wt=False, name="mem_kv1")
    outs = [(_dil_fwd if d == 1 else _dils_fwd)(qr, kr, kv, gi, d, name=f"dil_fwd{gi}") for gi, d in enumerate(DILATIONS)]
    mo1 = _mem_attn_fwd(proj_b, B_MQ_COL, mkv1, mqw[1], mkw[1], tt=1024, name="mem_attn_fwd1")
    hm1, mm1, lse_tot = _combine_out_proj(h1, [o for o, _ in outs], [s for _, s in outs], mo1, w_of("w_out1", mo1),
                                          tt=512, name="out_proj1")
    gu1, hn1 = _rms_matmul(hm1, nffn[1], w_of("w_gate_up1", hm1), tt=512, tn=1408, wt=True, out_dtype=BF16, name="gate_up1")
    dy, sq = _swiglu_down_loss(hm1, gu1, w_of("w_down1", gu1), tgt, tt=512, name="down1_loss")

    gP = {}
    zeros_mem = jnp.zeros((MEM_TOKENS, D_MODEL), F32)

    def ffn_bwd(l, dh, hm, gu, hn):
        dgu, g_wd = _swiglu_bwd(dh, gu, w_of(f"w_down{l}"), tt=256, name=f"swiglu_bwd{l}")
        g_wgu = _mm_tn(dgu, hn, tt=512, tka=1408, name=f"g_w_gate_up{l}")
        sent = put_g({f"w_down{l}": g_wd, f"w_gate_up{l}": g_wgu})
        dhm, g_nf = _rms_bwd_dx(hm, nffn[l], w_of(f"w_gate_up{l}"), dgu, dh, tt=512, wt=True, name=f"gate_up_bwd{l}", dep=sent)
        return dhm, g_nf

    def mix_bwd(l, dhm, mix_main, mix_mem, proj, qcol, mkv, mn):
        dmix, g_wout, *head_dots = _out_proj_bwd(dhm, mix_main, mix_mem, w_of(f"w_out{l}"), tt=512, name=f"out_proj_bwd{l}",
                                                 head_dots=l == 1)
        dmq, dmkv, dqw, dkw = _mem_attn_bwd(proj, qcol, mkv, mqw[l], mkw[l], dmix, tt=1024, name=f"mem_attn_bwd{l}")
        g_wmkv = _mm_tn(mn, dmkv, tt=MEM_TOKENS, tka=512, name=f"g_w_mem_kv{l}")
        sent = put_g({f"w_out{l}": g_wout, f"w_mem_kv{l}": g_wmkv})
        _, g_mn = _rms_bwd_dx(mem, mnorm[l], w_of(f"w_mem_kv{l}"), dmkv, zeros_mem, tt=MEM_TOKENS, wt=False, name=f"mem_kv_bwd{l}")
        fold = lambda v: v.reshape(MEM_HEADS, MEM_HEAD_DIM).sum(axis=0)
        return dmix, dmq, g_mn, fold(dqw), fold(dkw), sent, head_dots

    dhm1, g_nf1 = ffn_bwd(1, dy, hm1, gu1, hn1)
    dmix1, dmq1, g_mn1, g_mq1, g_mk1, sent, (dd,) = mix_bwd(1, dhm1, mm1, mo1, proj_b, B_MQ_COL, mkv1, mn1)
    dqs, dks, dvs = [], [], []
    for gi, d in enumerate(DILATIONS):
        dq_g, dk_g, dv_g = _dil_bwd(qr, kr, kv, dmix1, lse_tot, dd, gi, d, name=f"dil_bwd{gi}", dep=sent if gi == 0 else None)
        dqs.append(dq_g)
        dks.append(dk_g)
        dvs.append(dv_g)
    dq_raw, dqw = _q_prep_bwd(proj_b, qw_heads, cosf, sinsg, dqs, tt=512, name="q_prep_bwd")
    dkv, dkw = _kv_prep_bwd(kv, kw_heads, cosf, sinsg, dks, dvs, tt=512, name="kv_prep_bwd")
    dproj_b = [dq_raw, dmq1]
    g_wb = _mm_tn_pieces(dproj_b, xn1, tt=512, name="g_b_w_in")
    g_wkv = _mm_tn(dkv, hkn, tt=512, tka=768, name="g_w_kv")
    sent = put_g({"b_w_in": g_wb, "w_kv": g_wkv})
    dh1, g_nm1 = _rms_bwd_dx(h1, nmix[1], w_of("b_w_in"), dproj_b, dhm1, tt=512, wt=True, name="proj_b_bwd", dep=sent)
    dh1, g_kvn = _rms_bwd_dx(h1, kvn, w_of("w_kv"), dkv, dh1, tt=512, wt=True, name="kv_proj_bwd")

    dhm0, g_nf0 = ffn_bwd(0, dh1, hm0, gu0, hn0)
    dmix0, dmq0, g_mn0, g_mq0, g_mk0, sent, _ = mix_bwd(0, dhm0, mm0, mo0, proj_a, A_MQ_COL, mkv0, mn0)
    do_raw, dg, g_onorm = _a_post_bwd(o_raw, proj_a, onorm, dmix0, tt=512, name="a_post_bwd", dep=sent)
    dq, dz, dv, dlb = _hgrn2_bwd(proj_a, lb, st, do_raw, name="hgrn2_bwd")
    dproj_a = [dq, dz, dv, dg, dmq0]
    sent = put_g({"a_w_in": _mm_tn_pieces(dproj_a, xn0, tt=512, name="g_a_w_in")})
    gx, g_nm0 = _rms_bwd_dx(x, nmix[0], w_of("a_w_in"), dproj_a, dhm0, tt=512, wt=True, name="proj_a_bwd", dep=sent)

    dl0 = lb_soft[0:1] * lb_soft[1:2] * dlb
    gP["a_lb_logits"] = jnp.concatenate([dl0, -dl0], axis=0)
    gP["a_onorm"] = g_onorm
    gP["norm_mix"] = jnp.concatenate([g_nm0, g_nm1], axis=0)
    gP["norm_ffn"] = jnp.concatenate([g_nf0, g_nf1], axis=0)
    gP["b_qnorm"] = dqw.reshape(N_GROUPS, B_HEADS, HEAD_DIM).sum(axis=1)[None]
    gP["kv_norm"] = g_kvn.reshape(-1)
    gP["b_knorm"] = dkw.reshape(B_HEADS, HEAD_DIM).sum(axis=0)
    gP["mem_norm"] = jnp.concatenate([g_mn0, g_mn1], axis=0)
    gP["mem_qnorm"] = jnp.stack([g_mq0, g_mq1])
    gP["mem_knorm"] = jnp.stack([g_mk0, g_mk1])
    return sq, gx, gP


MESH_ID = pl.DeviceIdType.MESH
HBM_SPEC = pl.BlockSpec(memory_space=pltpu.HBM)


def _position():
    return lax.axis_index("x"), lax.axis_index("y"), lax.axis_index("c")


def _all_gather_direct(block, after, *, name):
    def body(x_ref, after_ref, out_ref, send_sems, recv_sems, local_sem):
        x, y, c = _position()
        me = 4 * x + 2 * y + c
        mine = pltpu.make_async_copy(x_ref, out_ref.at[me], local_sem)
        mine.start()
        copies = []
        for k in ALL_PEERS:
            cp = pltpu.make_async_remote_copy(
                src_ref=x_ref, dst_ref=out_ref.at[me], send_sem=send_sems.at[k - 1], recv_sem=recv_sems.at[k - 1],
                device_id=_peer(k, x, y, c), device_id_type=MESH_ID)
            cp.start()
            copies.append(cp)
        for cp in copies:
            cp.wait()
        mine.wait()

    return pl.pallas_call(
        body, out_shape=jax.ShapeDtypeStruct((N_DEV,) + block.shape, block.dtype),
        in_specs=[HBM_SPEC, pl.BlockSpec(memory_space=pl.ANY)], out_specs=HBM_SPEC,
        scratch_shapes=[pltpu.SemaphoreType.DMA((7,)), pltpu.SemaphoreType.DMA((7,)), pltpu.SemaphoreType.DMA],
        name=name)(block, after)


SEM_SPEC = pl.BlockSpec(memory_space=pltpu.SEMAPHORE)
ANY_SPEC = pl.BlockSpec(memory_space=pl.ANY)
DATAFLOW = pltpu.SideEffectType.DATAFLOW_SIDE_EFFECTING


def _peer(k, x, y, c):
    return (1 - x if (k >> 2) & 1 else x, 1 - y if (k >> 1) & 1 else y, 1 - c if k & 1 else c)


def _own_slot_filled(own_block):
    x, y, c = _position()
    zone = lax.empty((N_DEV,) + own_block.shape, own_block.dtype)
    return lax.dynamic_update_slice_in_dim(zone, own_block[None], 4 * x + 2 * y + c, axis=0)


ALL_PEERS = tuple(range(1, N_DEV))
SIBLING_AND_SAME_CORE = (1, 2, 4, 6)
SAME_CORE = (2, 4, 6)


def _split_start(srcs, scatter, after, *, name, relations=ALL_PEERS, carried=None):
    n = len(srcs)
    extra = ([] if after is None else [after]) + ([] if carried is None else [carried])
    n_carried = 0 if carried is None else 1
    x, y, c = _position()
    me = 4 * x + 2 * y + c
    lands = [_own_slot_filled(lax.dynamic_index_in_dim(s, me, 0, keepdims=False) if scatter else s) for s in srcs]

    def body(*refs):
        src_refs, land_refs = refs[:n], refs[n:2 * n]
        send_sems, recv_sems = refs[2 * n + len(extra)], refs[2 * n + len(extra) + 1]
        token = refs[2 * n + len(extra) + 2 + 2 * n]
        bx, by, bc = _position()
        bme = 4 * bx + 2 * by + bc
        for a in range(n):
            for k in relations:
                tx, ty, tc = _peer(k, bx, by, bc)
                src = src_refs[a].at[4 * tx + 2 * ty + tc] if scatter else src_refs[a]
                pltpu.make_async_remote_copy(
                    src_ref=src, dst_ref=land_refs[a].at[bme],
                    send_sem=send_sems.at[7 * a + k - 1], recv_sem=recv_sems.at[7 * a + k - 1],
                    device_id=(tx, ty, tc), device_id_type=MESH_ID).start()
        token[...] = jnp.zeros_like(token)

    hbm = lambda a: pltpu.HBM(a.shape, a.dtype)
    outs = pl.pallas_call(
        body, name=name,
        out_shape=(pltpu.SemaphoreType.DMA((7 * n,)), pltpu.SemaphoreType.DMA((7 * n,)),
                   *[hbm(s) for s in srcs], *[hbm(l) for l in lands], jax.ShapeDtypeStruct((8, 128), F32),
                   *([hbm(carried)] if n_carried else [])),
        in_specs=[HBM_SPEC] * (2 * n) + [ANY_SPEC] * len(extra),
        out_specs=(SEM_SPEC, SEM_SPEC, *[HBM_SPEC] * (2 * n), pl.BlockSpec(memory_space=pltpu.VMEM), *([ANY_SPEC] * n_carried)),
        input_output_aliases={**{i: 2 + i for i in range(2 * n)},
                              **({2 * n + len(extra) - 1: 2 * n + 3} if n_carried else {})},
        compiler_params=pltpu.CompilerParams(has_side_effects=DATAFLOW),
    )(*[pltpu.with_memory_space_constraint(s, pltpu.HBM) for s in srcs],
      *[pltpu.with_memory_space_constraint(l, pltpu.HBM) for l in lands], *extra)
    return {"n": n, "relations": relations, "send": outs[0], "recv": outs[1], "srcs": list(outs[2:2 + n]),
            "lands": list(outs[2 + n:2 + 2 * n]), "token": outs[2 * n + 2], "carried": outs[-1] if n_carried else None}


def _forward_start(lands, carried, *, name):
    n = len(lands)

    def body(*refs):
        land_refs = refs[:n]
        send_sems, recv_sems = refs[n + 1], refs[n + 2]
        bx, by, bc = _position()
        for a in range(n):
            for k in SAME_CORE:
                tx, ty, tc = _peer(k, bx, by, bc)
                block = land_refs[a].at[4 * tx + 2 * ty + tc]
                pltpu.make_async_remote_copy(
                    src_ref=block, dst_ref=block,
                    send_sem=send_sems.at[7 * a + k - 1], recv_sem=recv_sems.at[7 * a + k - 1],
                    device_id=(bx, by, 1 - bc), device_id_type=MESH_ID).start()

    hbm = lambda a: pltpu.HBM(a.shape, a.dtype)
    outs = pl.pallas_call(
        body, name=name,
        out_shape=(pltpu.SemaphoreType.DMA((7 * n,)), pltpu.SemaphoreType.DMA((7 * n,)),
                   *[hbm(l) for l in lands], hbm(carried)),
        in_specs=[HBM_SPEC] * n + [ANY_SPEC],
        out_specs=(SEM_SPEC, SEM_SPEC, *[HBM_SPEC] * n, ANY_SPEC),
        input_output_aliases={i: 2 + i for i in range(n + 1)},
        compiler_params=pltpu.CompilerParams(has_side_effects=DATAFLOW),
    )(*lands, carried)
    handle = {"n": n, "relations": SAME_CORE, "send": outs[0], "recv": outs[1], "srcs": [], "lands": list(outs[2:2 + n])}
    return handle, outs[-1]


def _split_wait(handle, after, *, name):
    n, ns = handle["n"], len(handle["srcs"])

    def body(*refs):
        land_refs = refs[ns:ns + n]
        send_sems, recv_sems = refs[ns + n], refs[ns + n + 1]
        bx, by, bc = _position()
        for a in range(n):
            for k in handle["relations"]:
                block = land_refs[a].at[0]
                cp = pltpu.make_async_remote_copy(
                    src_ref=block, dst_ref=block,
                    send_sem=send_sems.at[7 * a + k - 1], recv_sem=recv_sems.at[7 * a + k - 1],
                    device_id=_peer(k, bx, by, bc), device_id_type=MESH_ID)
                cp.wait_send()
                cp.wait_recv()

    hbm = lambda a: pltpu.HBM(a.shape, a.dtype)
    outs = pl.pallas_call(
        body, name=name,
        out_shape=(*[hbm(s) for s in handle["srcs"]], *[hbm(l) for l in handle["lands"]]),
        in_specs=[HBM_SPEC] * (ns + n) + [SEM_SPEC, SEM_SPEC, ANY_SPEC],
        out_specs=tuple([HBM_SPEC] * (ns + n)),
        input_output_aliases={i: i for i in range(ns + n)},
        compiler_params=pltpu.CompilerParams(has_side_effects=DATAFLOW),
    )(*handle["srcs"], *handle["lands"], handle["send"], handle["recv"], after)
    return list(outs[ns:])


def _sum_sources(parts, *, tr, name):
    n, R, C = parts.shape

    def body(p_ref, o_ref):
        acc = p_ref[0].astype(F32)
        for s in range(1, n):
            acc = acc + p_ref[s].astype(F32)
        o_ref[...] = acc

    return pl.pallas_call(
        body, grid=(R // tr,), in_specs=[pl.BlockSpec((n, tr, C), lambda i: (0, i, 0))],
        out_specs=pl.BlockSpec((tr, C), lambda i: (i, 0)),
        out_shape=jax.ShapeDtypeStruct((R, C), F32), compiler_params=_cp("parallel"), name=name)(parts)


def _adamw_math(g, w, m, v):
    c1 = 1.0 - ADAM_B1 ** ADAM_STEP
    c2 = 1.0 - ADAM_B2 ** ADAM_STEP
    nm = ADAM_B1 * m + (1.0 - ADAM_B1) * g
    nv = ADAM_B2 * v + (1.0 - ADAM_B2) * (g * g)
    return -ADAM_LR * ((nm / c1) / (jnp.sqrt(nv / c2) + ADAM_EPS) + ADAM_WD * w), nm, nv


ADAMW_STRIP = 16


def _reduce_adamw(received, w, m, v, *, tr, name):
    L, R, C = w.shape

    def body(*refs):
        p_refs = refs[:L]
        w_ref, m_ref, v_ref, g_ref, d_ref, nm_ref, nv_ref = refs[L:]
        for l in range(L):
            @pl.when(pl.program_id(0) == l)
            def _(p_ref=p_refs[l]):
                def strip(i, carry):
                    rows = pl.ds(pl.multiple_of(i * ADAMW_STRIP, ADAMW_STRIP), ADAMW_STRIP)
                    acc = p_ref[0, rows, :].astype(F32)
                    for s in range(1, N_DEV):
                        acc = acc + p_ref[s, rows, :].astype(F32)
                    g_ref[rows, :] = acc
                    d_ref[rows, :], nm_ref[rows, :], nv_ref[rows, :] = _adamw_math(acc, w_ref[rows, :], m_ref[rows, :], v_ref[rows, :])
                    return carry

                lax.fori_loop(0, tr // ADAMW_STRIP, strip, 0)

    p_spec = pl.BlockSpec((N_DEV, tr, C), lambda l, i: (0, i, 0))
    blk = pl.BlockSpec((None, tr, C), lambda l, i: (l, i, 0))
    sh = jax.ShapeDtypeStruct((L, R, C), F32)
    return pl.pallas_call(
        body, grid=(L, R // tr), in_specs=[p_spec] * L + [blk] * 3, out_specs=[blk] * 4, out_shape=[sh] * 4,
        compiler_params=_cp("parallel", "parallel"), name=name)(*received, w, m, v)


def _adamw(g, w, m, v, *, tr, name):
    L, R, C = w.shape

    def body(g_ref, w_ref, m_ref, v_ref, d_ref, nm_ref, nv_ref):
        d_ref[...], nm_ref[...], nv_ref[...] = _adamw_math(g_ref[...], w_ref[...], m_ref[...], v_ref[...])

    blk = pl.BlockSpec((None, tr, C), lambda l, i: (l, i, 0))
    sh = jax.ShapeDtypeStruct((L, R, C), F32)
    return pl.pallas_call(
        body, grid=(L, R // tr), in_specs=[blk] * 4, out_specs=[blk] * 3, out_shape=[sh] * 3,
        compiler_params=_cp("parallel", "parallel"), name=name)(g, w, m, v)


UNITS = {
    "a_w_in": ("a_w_in", 0, True), "w_mem_kv0": ("w_mem_kv", 0, False), "w_out0": ("w_out", 0, False),
    "w_gate_up0": ("w_gate_up", 0, True), "w_down0": ("w_down", 0, False), "w_kv": ("w_kv", None, True),
    "b_w_in": ("b_w_in", 0, True), "w_mem_kv1": ("w_mem_kv", 1, False), "w_out1": ("w_out", 1, False),
    "w_gate_up1": ("w_gate_up", 1, True), "w_down1": ("w_down", 1, False),
}
BIG = ("a_w_in", "b_w_in", "w_kv", "w_mem_kv", "w_out", "w_gate_up", "w_down")
ADAMW_ROW_TILE = {"a_w_in": 208, "b_w_in": 160, "w_kv": 192, "w_mem_kv": 128, "w_out": 128, "w_gate_up": 352, "w_down": 352}


def _wire_block(weights, unit):
    name, layer, col = UNITS[unit]
    a = weights[name] if layer is None else weights[name][layer]
    return (a.T if col else a).astype(BF16)


SMALL_REPLICATED = ("norm_mix", "norm_ffn", "b_qnorm", "kv_norm", "b_knorm", "mem_norm", "mem_qnorm", "mem_knorm")
SMALL_SHARDED = ("a_lb_logits", "a_onorm")
SMALL_ORDER = SMALL_REPLICATED + SMALL_SHARDED
LANES = 128


def _prod(shape):
    n = 1
    for s in shape:
        n *= s
    return n


def _pack_flat(arrays, rows, cols, dtype):
    flat = jnp.concatenate([a.reshape(-1).astype(dtype) for a in arrays])
    return jnp.pad(flat, (0, rows * cols - flat.shape[0])).reshape(rows, cols)


def _unpack_flat(packed, shapes):
    flat = packed.reshape(-1)
    out, off = [], 0
    for s in shapes:
        out.append(flat[off:off + _prod(s)].reshape(s))
        off += _prod(s)
    return out


def kernel(x, mem, norm_mix, norm_ffn, a_w_in, a_lb_logits, a_onorm, b_w_in, b_qnorm, kv_norm, w_kv, b_knorm, mem_norm, w_mem_kv, mem_qnorm, mem_knorm, w_out, w_gate_up, w_down, loss_target, m_norm_mix, m_norm_ffn, m_a_w_in, m_a_lb_logits, m_a_onorm, m_b_w_in, m_b_qnorm, m_kv_norm, m_w_kv, m_b_knorm, m_mem_norm, m_w_mem_kv, m_mem_qnorm, m_mem_knorm, m_w_out, m_w_gate_up, m_w_down, v_norm_mix, v_norm_ffn, v_a_w_in, v_a_lb_logits, v_a_onorm, v_b_w_in, v_b_qnorm, v_kv_norm, v_w_kv, v_b_knorm, v_mem_norm, v_w_mem_kv, v_mem_qnorm, v_mem_knorm, v_w_out, v_w_gate_up, v_w_down):
    names = ("norm_mix", "norm_ffn", "a_w_in", "a_lb_logits", "a_onorm", "b_w_in", "b_qnorm", "kv_norm", "w_kv", "b_knorm",
             "mem_norm", "w_mem_kv", "mem_qnorm", "mem_knorm", "w_out", "w_gate_up", "w_down")
    w = dict(zip(names, (norm_mix, norm_ffn, a_w_in, a_lb_logits, a_onorm, b_w_in, b_qnorm, kv_norm, w_kv, b_knorm,
                         mem_norm, w_mem_kv, mem_qnorm, mem_knorm, w_out, w_gate_up, w_down)))
    m = dict(zip(names, (m_norm_mix, m_norm_ffn, m_a_w_in, m_a_lb_logits, m_a_onorm, m_b_w_in, m_b_qnorm, m_kv_norm, m_w_kv,
                         m_b_knorm, m_mem_norm, m_w_mem_kv, m_mem_qnorm, m_mem_knorm, m_w_out, m_w_gate_up, m_w_down)))
    v = dict(zip(names, (v_norm_mix, v_norm_ffn, v_a_w_in, v_a_lb_logits, v_a_onorm, v_b_w_in, v_b_qnorm, v_kv_norm, v_w_kv,
                         v_b_knorm, v_mem_norm, v_w_mem_kv, v_mem_qnorm, v_mem_knorm, v_w_out, v_w_gate_up, v_w_down)))

    first = ["a_w_in", "w_mem_kv0"]
    later = [["w_out0", "w_gate_up0"], ["w_down0", "w_kv"], ["b_w_in", "w_mem_kv1"], ["w_out1", "w_gate_up1", "w_down1"]]
    first_half, second_half = {}, {}

    def start_first_half(i, after, carried=None):
        first_half[i] = _split_start([_wire_block(w, u) for u in later[i]], False, after, name=f"gather{i}_start",
                                     relations=SIBLING_AND_SAME_CORE, carried=carried)
        return first_half[i]

    opening = _split_start([_wire_block(w, u) for u in first] + [_pack_flat([a_lb_logits, a_onorm], 8, LANES, F32)],
                           False, None, name="gather_first_start", relations=SIBLING_AND_SAME_CORE)
    token = start_first_half(0, opening["token"])["token"]
    token = start_first_half(1, token)["token"]
    opening, token = _forward_start(_split_wait(opening, token, name="gather_first_landed"), token, name="gather_first_forward")
    gathered = _split_wait(opening, token, name="gather_first_wait")
    full = {u: g.reshape(-1, g.shape[-1]) for u, g in zip(first, gathered)}
    small_in = gathered[-1].reshape(N_DEV, -1)
    P = {n: w[n] for n in SMALL_REPLICATED}
    P["a_lb_logits"] = small_in[:, :192].reshape(N_DEV, 2, 96).transpose(1, 0, 2).reshape(2, A_WIDTH)
    P["a_onorm"] = small_in[:, 192:288].reshape(1, A_WIDTH)

    def forward_point(i, value):
        landed = _split_wait(first_half[i], value, name=f"gather{i}_landed")
        second_half[i], value = _forward_start(landed, value, name=f"gather{i}_forward")
        if i + 2 < len(later):
            value = start_first_half(i + 2, None, carried=value)["carried"]
        return value

    def get_w(unit, after):
        if unit not in full:
            i = [unit in group for group in later].index(True)
            for u, land in zip(later[i], _split_wait(second_half[i], after, name=f"gather{i}_wait")):
                full[u] = land.reshape(-1, land.shape[-1])
        return full[unit]

    sent = []

    def put_g(group):
        units = list(group)
        handle = _split_start([group[u].reshape(N_DEV, -1, group[u].shape[-1]) for u in units], True, None,
                              name=f"scatter{len(sent)}_start")
        sent.append((units, handle))
        return handle["token"]

    sq, gx, gP = _local_step(x[0], mem[0], loss_target[0], get_w, P, put_g, forward_point=forward_point)
    loss_here = (0.5 * jnp.sum(sq) / D_MODEL).reshape(1)

    received = {}
    group_of = {u: i for i, (units, _) in enumerate(sent) for u in units}
    out = {"grad": {}, "delta": {}, "new_m": {}, "new_v": {}}
    newest = [gx]

    def update_big(n):
        shape = w[n].shape
        as3 = lambda a: a.reshape((-1,) + shape[-2:])
        mine = [u for u, (wn, _, _) in UNITS.items() if wn == n]
        for i in sorted({group_of[u] for u in mine}):
            if sent[i][0][0] not in received:
                received.update(zip(sent[i][0], _split_wait(sent[i][1], newest[0], name=f"scatter{i}_wait")))
        flip = (lambda a: jnp.swapaxes(a, 1, 2)) if UNITS[mine[0]][2] else (lambda a: a)
        res = _reduce_adamw([received[u] for u in mine], flip(as3(w[n])), flip(as3(m[n])), flip(as3(v[n])),
                            tr=ADAMW_ROW_TILE[n], name=f"adamw_{n}")
        newest[0] = res[1]
        for kind, r in zip(("grad", "delta", "new_m", "new_v"), res):
            out[kind][n] = flip(r).reshape(shape)

    for n in ("w_down", "w_gate_up", "w_out", "w_mem_kv", "b_w_in", "w_kv"):
        update_big(n)

    full_shapes = [(2, A_WIDTH) if n == "a_lb_logits" else (1, A_WIDTH) if n == "a_onorm" else w[n].shape for n in SMALL_ORDER]
    n_small = sum(_prod(s) for s in full_shapes) + 1
    rows_small = -(-n_small // (8 * LANES)) * 8
    g_all = _all_gather_direct(_pack_flat([gP[n] for n in SMALL_ORDER] + [loss_here], rows_small, LANES, F32),
                               newest[0], name="gather_small_grads")
    summed = _unpack_flat(_sum_sources(g_all, tr=rows_small, name="sum_small_grads"), full_shapes + [(1,)])
    g_small = dict(zip(SMALL_ORDER, summed))
    loss = summed[-1].reshape(())
    me = 4 * lax.axis_index("x") + 2 * lax.axis_index("y") + lax.axis_index("c")
    for n in SMALL_SHARDED:
        g_small[n] = lax.dynamic_slice_in_dim(g_small[n], me * 96, 96, axis=1)
    shapes = [w[n].shape for n in SMALL_ORDER]
    rows_upd = -(-sum(_prod(s) for s in shapes) // (8 * LANES)) * 8
    pk = lambda d: _pack_flat([d[n] for n in SMALL_ORDER], rows_upd, LANES, F32)
    res = _adamw(pk(g_small)[None], pk(w)[None], pk(m)[None], pk(v)[None], tr=rows_upd, name="adamw_small")
    out["grad"].update(g_small)
    for kind, packed in zip(("delta", "new_m", "new_v"), res):
        out[kind].update(zip(SMALL_ORDER, _unpack_flat(packed[0], shapes)))
    newest[0] = res[0]
    update_big("a_w_in")

    return (loss, gx[None], *[out["grad"][n] for n in names], *[out["delta"][n] for n in names],
            *[out["new_m"][n] for n in names], *[out["new_v"][n] for n in names])
```

```python
import functools

import jax
import jax.numpy as jnp
import numpy as np
from jax import lax
from jax.experimental import pallas as pl
from jax.experimental.pallas import tpu as pltpu

F32 = jnp.float32
BF16 = jnp.bfloat16

N_DEV = 8
D_MODEL = 1024
HEAD_DIM = 128
A_HEADS = 6
A_WIDTH = A_HEADS * HEAD_DIM
CHUNK = 64
B_HEADS = 6
B_WIDTH = B_HEADS * HEAD_DIM
DILATIONS = (1, 4, 16)
SPAN = 128
N_GROUPS = 3
ROPE_THETA = 10000.0
MEM_TOKENS = 256
MEM_HEADS = 4
MEM_HEAD_DIM = 64
MEM_WIDTH = MEM_HEADS * MEM_HEAD_DIM
FFN_HIDDEN = 2816
EPS = 1e-6

ADAM_LR = 0.001
ADAM_B1 = 0.9
ADAM_B2 = 0.999
ADAM_EPS = 1e-08
ADAM_WD = 0.01
ADAM_STEP = 10

V7X_VMEM_LIMIT_BYTES = 56 * 1024 * 1024

NT_DIMS = (((1,), (1,)), ((), ()))
TN_DIMS = (((0,), (0,)), ((), ()))


def _cp(*sem):
    return pltpu.CompilerParams(dimension_semantics=sem, vmem_limit_bytes=V7X_VMEM_LIMIT_BYTES)


def _dot(a, b):
    return jnp.dot(a.astype(BF16), b.astype(BF16), preferred_element_type=F32)


def _dot_nt(a, b):
    return lax.dot_general(a.astype(BF16), b.astype(BF16), NT_DIMS, preferred_element_type=F32)


def _dot_tn(a, b):
    return lax.dot_general(a.astype(BF16), b.astype(BF16), TN_DIMS, preferred_element_type=F32)


def _dot3(m01, x):
    hi = x.astype(BF16)
    r1 = x - hi.astype(F32)
    mid = r1.astype(BF16)
    lo = (r1 - mid.astype(F32)).astype(BF16)
    d = functools.partial(jnp.dot, preferred_element_type=F32)
    return d(m01, hi) + d(m01, mid) + d(m01, lo)


def _sigmoid(x):
    return 0.5 * jnp.tanh(0.5 * x) + 0.5


def _full(shape):
    return pl.BlockSpec(shape, lambda *_: (0,) * len(shape))


def _dep(body, n_in, dep):
    if dep is None:
        return body, [], []

    def with_dep(*refs):
        return body(*refs[:n_in], *refs[n_in + 1:])

    return with_dep, [pl.BlockSpec(memory_space=pl.ANY)], [dep]


def _rms_matmul(x, g, w, *, tt, tn, wt, name, out_dtype=F32, dep=None, rotate=None):
    T, K = x.shape
    N = w.shape[0] if wt else w.shape[1]
    n_rot = 0 if rotate is None else rotate[0].shape[1] // HEAD_DIM
    extra_in = [] if rotate is None else list(rotate)

    def kernel_body(x_ref, g_ref, w_ref, *rest):
        y_ref, xn_ref = rest[len(extra_in)], rest[len(extra_in) + 1]
        xf = x_ref[...]
        r = lax.rsqrt(jnp.mean(xf * xf, axis=-1, keepdims=True) + EPS)
        xn = (xf * r * g_ref[...]).astype(BF16)
        xn_ref[...] = xn
        for j in range(N // tn):
            cols = slice(j * tn, (j + 1) * tn)
            y = _dot_nt(xn, w_ref[cols, :]) if wt else _dot(xn, w_ref[:, cols])
            y_ref[:, cols] = y.astype(out_dtype)
            for h in range(j * tn // HEAD_DIM, min((j + 1) * tn // HEAD_DIM, n_rot)):
                gw_ref, c_ref, s_ref, yr_ref = rest[0], rest[1], rest[2], rest[len(extra_in) + 2]
                sl = slice(h * HEAD_DIM, (h + 1) * HEAD_DIM)
                xhat, _ = _head_rms(y[:, h * HEAD_DIM - j * tn:(h + 1) * HEAD_DIM - j * tn])
                yr_ref[:, sl] = _rope(xhat * gw_ref[:, sl], c_ref[...], s_ref[...])

    tbl = pl.BlockSpec((tt, HEAD_DIM), lambda i: (i, 0))
    rot_specs = [] if rotate is None else [_full((1, n_rot * HEAD_DIM)), tbl, tbl]
    body, dep_specs, dep_args = _dep(kernel_body, 3 + len(extra_in), dep)
    return pl.pallas_call(
        body, grid=(T // tt,),
        in_specs=[pl.BlockSpec((tt, K), lambda i: (i, 0)), _full((1, K)), _full(w.shape)] + rot_specs + dep_specs,
        out_specs=[pl.BlockSpec((tt, N), lambda i: (i, 0)), pl.BlockSpec((tt, K), lambda i: (i, 0))]
        + ([] if rotate is None else [pl.BlockSpec((tt, n_rot * HEAD_DIM), lambda i: (i, 0))]),
        out_shape=[jax.ShapeDtypeStruct((T, N), out_dtype), jax.ShapeDtypeStruct((T, K), BF16)]
        + ([] if rotate is None else [jax.ShapeDtypeStruct((T, n_rot * HEAD_DIM), F32)]),
        compiler_params=_cp("parallel"), name=name)(x, g, w, *extra_in, *dep_args)


def _swiglu_down(h, gu, wd, *, tt, name):
    T, D = h.shape
    Fh = wd.shape[0]

    def body(h_ref, gt_ref, up_ref, w_ref, o_ref):
        gt = gt_ref[...].astype(F32)
        act = gt * _sigmoid(gt) * up_ref[...].astype(F32)
        o_ref[...] = h_ref[...] + _dot(act, w_ref[...])

    return pl.pallas_call(
        body, grid=(T // tt,),
        in_specs=[pl.BlockSpec((tt, D), lambda i: (i, 0)), pl.BlockSpec((tt, Fh), lambda i: (i, 0)),
                  pl.BlockSpec((tt, Fh), lambda i: (i, 1)), _full((Fh, D))],
        out_specs=pl.BlockSpec((tt, D), lambda i: (i, 0)),
        out_shape=jax.ShapeDtypeStruct((T, D), F32),
        compiler_params=_cp("parallel"), name=name)(h, gu, gu, wd)


def _swiglu_down_loss(h, gu, wd, tgt, *, tt, name):
    T, D = h.shape
    Fh = wd.shape[0]

    def body(h_ref, gt_ref, up_ref, w_ref, t_ref, dy_ref, acc_ref):
        @pl.when(pl.program_id(0) == 0)
        def _():
            acc_ref[...] = jnp.zeros_like(acc_ref)

        gt = gt_ref[...].astype(F32)
        act = gt * _sigmoid(gt) * up_ref[...].astype(F32)
        e = h_ref[...] + _dot(act, w_ref[...]) - t_ref[...]
        dy_ref[...] = e * (1.0 / D)
        acc_ref[...] += jnp.sum(e * e, axis=0, keepdims=True)

    row = pl.BlockSpec((tt, D), lambda i: (i, 0))
    return pl.pallas_call(
        body, grid=(T // tt,),
        in_specs=[row, pl.BlockSpec((tt, Fh), lambda i: (i, 0)), pl.BlockSpec((tt, Fh), lambda i: (i, 1)), _full((Fh, D)), row],
        out_specs=[row, _full((1, D))],
        out_shape=[jax.ShapeDtypeStruct((T, D), F32), jax.ShapeDtypeStruct((1, D), F32)],
        compiler_params=_cp("arbitrary"), name=name)(h, gu, gu, wd, tgt)


SWIGLU_COLS = 256


def _swiglu_bwd(dh, gu, wd, *, tt, name):
    T, D = dh.shape
    Fh = wd.shape[0]
    last = T // tt - 1

    def body(dh_ref, gt_ref, up_ref, w_ref, dgu_ref, gw_ref, acc):
        @pl.when(pl.program_id(0) == 0)
        def _():
            acc[...] = jnp.zeros_like(acc)

        dh16 = dh_ref[...].astype(BF16)
        for c0 in range(0, Fh, SWIGLU_COLS):
            cols = slice(c0, c0 + SWIGLU_COLS)
            gt = gt_ref[:, cols].astype(F32)
            up = up_ref[:, cols].astype(F32)
            s = _sigmoid(gt)
            silu = gt * s
            dact = _dot_nt(dh16, w_ref[cols, :])
            acc[cols, :] += _dot_tn((silu * up).astype(BF16), dh16)
            dgu_ref[:, cols] = (dact * up * (s * (1.0 + gt * (1.0 - s)))).astype(BF16)
            dgu_ref[:, Fh + c0:Fh + c0 + SWIGLU_COLS] = (dact * silu).astype(BF16)

        @pl.when(pl.program_id(0) == last)
        def _():
            gw_ref[...] = acc[...].astype(BF16)

    return pl.pallas_call(
        body, grid=(T // tt,),
        in_specs=[pl.BlockSpec((tt, D), lambda i: (i, 0)), pl.BlockSpec((tt, Fh), lambda i: (i, 0)),
                  pl.BlockSpec((tt, Fh), lambda i: (i, 1)), _full((Fh, D))],
        out_specs=[pl.BlockSpec((tt, 2 * Fh), lambda i: (i, 0)), _full((Fh, D))],
        out_shape=[jax.ShapeDtypeStruct((T, 2 * Fh), BF16), jax.ShapeDtypeStruct((Fh, D), BF16)],
        scratch_shapes=[pltpu.VMEM((Fh, D), F32)],
        compiler_params=_cp("arbitrary"), name=name)(dh, gu, gu, wd)


def _out_proj_bwd(dy, a1, a2, w, *, tt, name, head_dots=False):
    T, N = dy.shape
    K1, K2 = a1.shape[1], a2.shape[1]
    K = K1 + K2
    last = T // tt - 1

    def body(dy_ref, a1_ref, a2_ref, w_ref, da_ref, gw_ref, *rest):
        acc = rest[-1]

        @pl.when(pl.program_id(0) == 0)
        def _():
            acc[...] = jnp.zeros_like(acc)

        dy16 = dy_ref[...].astype(BF16)
        da = _dot_nt(dy16, w_ref[...])
        da_ref[...] = da
        acc[:K1, :] += _dot_tn(a1_ref[...], dy16)
        acc[K1:, :] += _dot_tn(a2_ref[...], dy16)
        if head_dots:
            for h in range(K1 // HEAD_DIM):
                sl = slice(h * HEAD_DIM, (h + 1) * HEAD_DIM)
                rest[0][:, sl] = jnp.broadcast_to(jnp.sum(da[:, sl] * a1_ref[:, sl], axis=-1, keepdims=True), (tt, HEAD_DIM))

        @pl.when(pl.program_id(0) == last)
        def _():
            gw_ref[...] = acc[...].astype(BF16)

    extra_specs = [pl.BlockSpec((tt, K1), lambda i: (i, 0))] if head_dots else []
    extra_shapes = [jax.ShapeDtypeStruct((T, K1), F32)] if head_dots else []
    return pl.pallas_call(
        body, grid=(T // tt,),
        in_specs=[pl.BlockSpec((tt, N), lambda i: (i, 0)), pl.BlockSpec((tt, K1), lambda i: (i, 0)),
                  pl.BlockSpec((tt, K2), lambda i: (i, 0)), _full((K, N))],
        out_specs=[pl.BlockSpec((tt, K), lambda i: (i, 0)), _full((K, N))] + extra_specs,
        out_shape=[jax.ShapeDtypeStruct((T, K), F32), jax.ShapeDtypeStruct((K, N), BF16)] + extra_shapes,
        scratch_shapes=[pltpu.VMEM((K, N), F32)],
        compiler_params=_cp("arbitrary"), name=name)(dy, a1, a2, w)


def _mm_tn(a, b, *, tt, tka, name):
    T, Ka = a.shape
    N = b.shape[1]
    last = T // tt - 1

    def body(a_ref, b_ref, o_ref, acc):
        @pl.when(pl.program_id(1) == 0)
        def _():
            acc[...] = jnp.zeros_like(acc)

        acc[...] += _dot_tn(a_ref[...], b_ref[...])

        @pl.when(pl.program_id(1) == last)
        def _():
            o_ref[...] = acc[...].astype(BF16)

    return pl.pallas_call(
        body, grid=(Ka // tka, T // tt),
        in_specs=[pl.BlockSpec((tt, tka), lambda j, t: (t, j)), pl.BlockSpec((tt, N), lambda j, t: (t, 0))],
        out_specs=pl.BlockSpec((tka, N), lambda j, t: (j, 0)),
        out_shape=jax.ShapeDtypeStruct((Ka, N), BF16),
        scratch_shapes=[pltpu.VMEM((tka, N), F32)],
        compiler_params=_cp("parallel", "arbitrary"), name=name)(a, b)


def _mm_tn_pieces(pieces, b, *, tt, name):
    n = len(pieces)
    T = b.shape[0]
    N = b.shape[1]
    widths = [p.shape[1] for p in pieces]
    Ka = sum(widths)
    last = T // tt - 1

    def body(*refs):
        p_refs = refs[:n]
        b_ref, o_ref, acc = refs[n:]

        @pl.when(pl.program_id(0) == 0)
        def _():
            acc[...] = jnp.zeros_like(acc)

        bv = b_ref[...].astype(BF16)
        off = 0
        for p_ref, wd in zip(p_refs, widths):
            acc[off:off + wd, :] += _dot_tn(p_ref[...], bv)
            off += wd

        @pl.when(pl.program_id(0) == last)
        def _():
            o_ref[...] = acc[...].astype(BF16)

    return pl.pallas_call(
        body, grid=(T // tt,),
        in_specs=[pl.BlockSpec((tt, wd), lambda t: (t, 0)) for wd in widths] + [pl.BlockSpec((tt, N), lambda t: (t, 0))],
        out_specs=_full((Ka, N)), out_shape=jax.ShapeDtypeStruct((Ka, N), BF16),
        scratch_shapes=[pltpu.VMEM((Ka, N), F32)],
        compiler_params=_cp("arbitrary"), name=name)(*pieces, b)


def _rms_bwd_dx(x, g, w, dy, dres, *, tt, wt, name, dep=None):
    pieces = list(dy) if isinstance(dy, (list, tuple)) else [dy]
    n = len(pieces)
    widths = [p.shape[1] for p in pieces]
    T, K = x.shape

    def kernel_body(x_ref, g_ref, w_ref, *rest):
        dy_refs = rest[:n]
        dres_ref, dx_ref, dg_ref = rest[n:]

        @pl.when(pl.program_id(0) == 0)
        def _():
            dg_ref[...] = jnp.zeros_like(dg_ref)

        if n == 1:
            dxn = (_dot if wt else _dot_nt)(dy_refs[0][...], w_ref[...])
        else:
            dxn, off = 0.0, 0
            for dy_ref, wd in zip(dy_refs, widths):
                dxn = dxn + _dot(dy_ref[...], w_ref[off:off + wd, :])
                off += wd
        xf = x_ref[...]
        r = lax.rsqrt(jnp.mean(xf * xf, axis=-1, keepdims=True) + EPS)
        xhat = xf * r
        dg_ref[...] += jnp.sum(dxn * xhat, axis=0, keepdims=True)
        dxhat = dxn * g_ref[...]
        dx_ref[...] = dres_ref[...] + r * (dxhat - xhat * jnp.mean(dxhat * xhat, axis=-1, keepdims=True))

    assert n == 1 or wt
    body, dep_specs, dep_args = _dep(kernel_body, 4 + n, dep)
    return pl.pallas_call(
        body, grid=(T // tt,),
        in_specs=[pl.BlockSpec((tt, K), lambda i: (i, 0)), _full((1, K)), _full(w.shape)]
        + [pl.BlockSpec((tt, wd), lambda i: (i, 0)) for wd in widths]
        + [pl.BlockSpec((tt, K), lambda i: (i, 0))] + dep_specs,
        out_specs=[pl.BlockSpec((tt, K), lambda i: (i, 0)), _full((1, K))],
        out_shape=[jax.ShapeDtypeStruct((T, K), F32), jax.ShapeDtypeStruct((1, K), F32)],
        compiler_params=_cp("arbitrary"), name=name)(x, g, w, *pieces, dres, *dep_args)


HGRN_TB = 512
HGRN_NCH = HGRN_TB // CHUNK
HGRN_UNROLL = 8
HGRN_HPB = 6


def _hgrn_chunk_fwd(q, z, lbv, tril01):
    sig = _sigmoid(z)
    f = lbv + (1.0 - lbv) * sig
    kk = 1.0 - f
    b = _dot3(tril01, jnp.log(f))
    bend = b[CHUNK - 1:CHUNK, :]
    sq = _sigmoid(q)
    eb = jnp.exp(b)
    emb = jnp.exp(-b)
    eo = jnp.exp(bend - b)
    dec = jnp.exp(bend)
    return sig, f, kk, sq, eb, emb, eo, dec


def _hgrn2_fwd(proj, lb, *, name):
    T = proj.shape[0]
    nT = T // HGRN_TB
    nC = T // CHUNK

    def body(q_ref, z_ref, v_ref, lb_ref, o_ref, st_ref, state):
        @pl.when(pl.program_id(1) == 0)
        def _():
            state[...] = jnp.zeros_like(state)

        row = lax.broadcasted_iota(jnp.int32, (CHUNK, CHUNK), 0)
        col = lax.broadcasted_iota(jnp.int32, (CHUNK, CHUNK), 1)
        causal = row >= col
        tril01 = causal.astype(BF16)

        def chunk(c, carry):
            rows = pl.ds(pl.multiple_of(c * CHUNK, CHUNK), CHUNK)
            for hh in range(HGRN_HPB):
                sl = slice(hh * HEAD_DIM, (hh + 1) * HEAD_DIM)
                q = q_ref[rows, sl]
                v = v_ref[rows, sl].astype(BF16)
                sig, f, kk, sq, eb, emb, eo, dec = _hgrn_chunk_fwd(q, z_ref[rows, sl], lb_ref[:, sl], tril01)
                qi = (q * sq * eb).astype(BF16)
                ki = (kk * emb).astype(BF16)
                ko = (kk * eo).astype(BF16)
                st = state[hh]
                att = jnp.where(causal, _dot_nt(qi, ki), 0.0)
                o_ref[rows, sl] = _dot(att, v) + _dot_nt(qi, st)
                st_ref[c, hh] = st
                state[hh] = st * dec + _dot_tn(v, ko)
            return carry

        lax.fori_loop(0, HGRN_NCH, chunk, 0, unroll=HGRN_UNROLL)

    W = HGRN_HPB * HEAD_DIM
    nG = A_HEADS // HGRN_HPB
    hb = lambda off: pl.BlockSpec((HGRN_TB, W), lambda h, i: (i, off + h))
    return pl.pallas_call(
        body, grid=(nG, nT),
        in_specs=[hb(0), hb(nG), hb(2 * nG), pl.BlockSpec((1, W), lambda h, i: (0, h))],
        out_specs=[hb(0), pl.BlockSpec((HGRN_NCH, HGRN_HPB, HEAD_DIM, HEAD_DIM), lambda h, i: (i, h, 0, 0))],
        out_shape=[jax.ShapeDtypeStruct((T, A_WIDTH), F32), jax.ShapeDtypeStruct((nC, A_HEADS, HEAD_DIM, HEAD_DIM), F32)],
        scratch_shapes=[pltpu.VMEM((HGRN_HPB, HEAD_DIM, HEAD_DIM), F32)],
        compiler_params=_cp("parallel", "arbitrary"), name=name)(proj, proj, proj, lb)


def _hgrn2_bwd(proj, lb, st_all, do, *, name):
    T = proj.shape[0]
    nT = T // HGRN_TB

    def body(q_ref, z_ref, v_ref, lb_ref, st_ref, do_ref, dq_ref, dz_ref, dv_ref, dlb_ref, dstate):
        @pl.when(pl.program_id(1) == 0)
        def _():
            dstate[...] = jnp.zeros_like(dstate)
            dlb_ref[...] = jnp.zeros_like(dlb_ref)

        row = lax.broadcasted_iota(jnp.int32, (CHUNK, CHUNK), 0)
        col = lax.broadcasted_iota(jnp.int32, (CHUNK, CHUNK), 1)
        causal = row >= col
        tril01 = causal.astype(BF16)
        triu01 = (row <= col).astype(BF16)

        def chunk(cc, carry):
            c = HGRN_NCH - 1 - cc
            rows = pl.ds(pl.multiple_of(c * CHUNK, CHUNK), CHUNK)
            for hh in range(HGRN_HPB):
                sl = slice(hh * HEAD_DIM, (hh + 1) * HEAD_DIM)
                lbv = lb_ref[:, sl]
                q = q_ref[rows, sl]
                v = v_ref[rows, sl].astype(BF16)
                sig, f, kk, sq, eb, emb, eo, dec = _hgrn_chunk_fwd(q, z_ref[rows, sl], lbv, tril01)
                qi32 = q * sq * eb
                ki32 = kk * emb
                ko32 = kk * eo
                qi, ki, ko = qi32.astype(BF16), ki32.astype(BF16), ko32.astype(BF16)
                att = jnp.where(causal, _dot_nt(qi, ki), 0.0).astype(BF16)
                dout = do_ref[rows, sl].astype(BF16)
                st = st_ref[c, hh]
                dst = dstate[hh]
                dst16 = dst.astype(BF16)
                datt = jnp.where(causal, _dot_nt(dout, v), 0.0).astype(BF16)
                dqi = _dot(datt, ki) + _dot(dout, st)
                dki = _dot_tn(datt, qi)
                dv_ref[rows, sl] = (_dot_tn(att, dout) + _dot_nt(ko, dst16)).astype(BF16)
                dko = _dot(v, dst16)
                ddec = jnp.sum(dst * st, axis=0, keepdims=True)
                dstate[hh] = dst * dec + _dot_tn(dout, qi)
                dkk = dki * emb + dko * eo
                db = dqi * qi32 - dki * ki32 - dko * ko32
                dbend = jnp.sum(dko * ko32, axis=0, keepdims=True) + ddec * dec
                dlogf = _dot3(triu01, db) + dbend
                df = dlogf / f - dkk
                dz_ref[rows, sl] = (df * (1.0 - lbv) * sig * (1.0 - sig)).astype(BF16)
                dlb_ref[:, sl] += jnp.sum(df * (1.0 - sig), axis=0, keepdims=True)
                dq_ref[rows, sl] = (dqi * eb * (sq * (1.0 + q * (1.0 - sq)))).astype(BF16)
            return carry

        lax.fori_loop(0, HGRN_NCH, chunk, 0, unroll=HGRN_UNROLL)

    W = HGRN_HPB * HEAD_DIM
    nG = A_HEADS // HGRN_HPB
    hb = lambda off: pl.BlockSpec((HGRN_TB, W), lambda h, i: (nT - 1 - i, off + h))
    hlb = pl.BlockSpec((1, W), lambda h, i: (0, h))
    o16 = jax.ShapeDtypeStruct((T, A_WIDTH), BF16)
    return pl.pallas_call(
        body, grid=(nG, nT),
        in_specs=[hb(0), hb(nG), hb(2 * nG), hlb,
                  pl.BlockSpec((HGRN_NCH, HGRN_HPB, HEAD_DIM, HEAD_DIM), lambda h, i: (nT - 1 - i, h, 0, 0)), hb(0)],
        out_specs=[hb(0), hb(0), hb(0), hlb],
        out_shape=[o16, o16, o16, jax.ShapeDtypeStruct((1, A_WIDTH), F32)],
        scratch_shapes=[pltpu.VMEM((HGRN_HPB, HEAD_DIM, HEAD_DIM), F32)],
        compiler_params=_cp("parallel", "arbitrary"), name=name)(proj, proj, proj, lb, st_all, do)


def _head_rms(x):
    r = lax.rsqrt(jnp.mean(x * x, axis=-1, keepdims=True) + EPS)
    return x * r, r


def _head_rms_bwd(dxhat, xhat, r):
    return r * (dxhat - xhat * jnp.mean(dxhat * xhat, axis=-1, keepdims=True))


def _a_post_out_proj(res, o, proj, onorm, mem_out, w, *, tt, name):
    T, N = res.shape
    K2 = mem_out.shape[1]

    def body(r_ref, o_ref, g_ref, gain_ref, a2_ref, w_ref, h_ref, y_ref):
        for h in range(A_HEADS):
            sl = slice(h * HEAD_DIM, (h + 1) * HEAD_DIM)
            xhat, _ = _head_rms(o_ref[:, sl])
            g = g_ref[:, sl]
            y_ref[:, sl] = xhat * gain_ref[:, sl] * (g * _sigmoid(g))
        h_ref[...] = r_ref[...] + _dot(y_ref[...], w_ref[:A_WIDTH, :]) + _dot(a2_ref[...], w_ref[A_WIDTH:, :])

    blk = lambda c: pl.BlockSpec((tt, A_WIDTH), lambda i: (i, c))
    row = pl.BlockSpec((tt, N), lambda i: (i, 0))
    return pl.pallas_call(
        body, grid=(T // tt,),
        in_specs=[row, blk(0), blk(3), _full((1, A_WIDTH)), pl.BlockSpec((tt, K2), lambda i: (i, 0)), _full((A_WIDTH + K2, N))],
        out_specs=[row, blk(0)],
        out_shape=[jax.ShapeDtypeStruct((T, N), F32), jax.ShapeDtypeStruct((T, A_WIDTH), F32)],
        compiler_params=_cp("parallel"), name=name)(res, o, proj, onorm, mem_out, w)


def _a_post_bwd(o, proj, onorm, dmix, *, tt, name, dep=None):
    T = o.shape[0]

    def kernel_body(o_ref, g_ref, w_ref, dy_ref, do_ref, dg_ref, dw_ref):
        @pl.when(pl.program_id(0) == 0)
        def _():
            dw_ref[...] = jnp.zeros_like(dw_ref)

        for h in range(A_HEADS):
            sl = slice(h * HEAD_DIM, (h + 1) * HEAD_DIM)
            xhat, r = _head_rms(o_ref[:, sl])
            g = g_ref[:, sl]
            s = _sigmoid(g)
            dy = dy_ref[:, sl]
            w = w_ref[:, sl]
            dg_ref[:, sl] = (dy * xhat * w * (s * (1.0 + g * (1.0 - s)))).astype(BF16)
            dyn = dy * (g * s)
            dw_ref[:, sl] += jnp.sum(dyn * xhat, axis=0, keepdims=True)
            do_ref[:, sl] = _head_rms_bwd(dyn * w, xhat, r)

    blk = lambda c: pl.BlockSpec((tt, A_WIDTH), lambda i: (i, c))
    body, dep_specs, dep_args = _dep(kernel_body, 4, dep)
    return pl.pallas_call(
        body, grid=(T // tt,), in_specs=[blk(0), blk(3), _full((1, A_WIDTH)), blk(0)] + dep_specs,
        out_specs=[blk(0), blk(0), _full((1, A_WIDTH))],
        out_shape=[jax.ShapeDtypeStruct((T, A_WIDTH), F32), jax.ShapeDtypeStruct((T, A_WIDTH), BF16),
                   jax.ShapeDtypeStruct((1, A_WIDTH), F32)],
        compiler_params=_cp("arbitrary"), name=name)(o, proj, onorm, dmix, *dep_args)


def _mem_head_masks(n):
    lane = lax.broadcasted_iota(jnp.int32, (n, MEM_WIDTH), 1)
    return [(lane >= m * MEM_HEAD_DIM) & (lane < (m + 1) * MEM_HEAD_DIM) for m in range(MEM_HEADS)]


def _mem_head_rms(x, masks):
    x2 = x * x
    r = jnp.zeros_like(x)
    for mk in masks:
        ms = jnp.sum(jnp.where(mk, x2, 0.0), axis=-1, keepdims=True) * (1.0 / MEM_HEAD_DIM)
        r = jnp.where(mk, lax.rsqrt(ms + EPS), r)
    return x * r, r


def _mem_head_rms_bwd(dxhat, xhat, r, masks):
    t = dxhat * xhat
    m = jnp.zeros_like(t)
    for mk in masks:
        m = jnp.where(mk, jnp.sum(jnp.where(mk, t, 0.0), axis=-1, keepdims=True) * (1.0 / MEM_HEAD_DIM), m)
    return r * (dxhat - xhat * m)


MEM_SCALE = MEM_HEAD_DIM ** -0.5


def _mem_attn_fwd(proj, qcol, mkv, qn_w, kn_w, *, tt, name):
    T = proj.shape[0]

    def body(q_ref, k_ref, v_ref, qw_ref, kw_ref, o_ref):
        qmasks = _mem_head_masks(tt)
        kmasks = _mem_head_masks(MEM_TOKENS)
        qhat, _ = _mem_head_rms(q_ref[...], qmasks)
        qn = qhat * qw_ref[...]
        khat, _ = _mem_head_rms(k_ref[...], kmasks)
        kn = (khat * kw_ref[...]).astype(BF16)
        v = v_ref[...].astype(BF16)
        out = jnp.zeros((tt, MEM_WIDTH), F32)
        for m in range(MEM_HEADS):
            s = _dot_nt(jnp.where(qmasks[m], qn, 0.0), kn) * MEM_SCALE
            s = s - jnp.max(s, axis=-1, keepdims=True)
            p = jnp.exp(s)
            p = p / jnp.sum(p, axis=-1, keepdims=True)
            out = jnp.where(qmasks[m], _dot(p, v), out)
        o_ref[...] = out

    return pl.pallas_call(
        body, grid=(T // tt,),
        in_specs=[pl.BlockSpec((tt, MEM_WIDTH), lambda i: (i, qcol)), pl.BlockSpec((MEM_TOKENS, MEM_WIDTH), lambda i: (0, 0)),
                  pl.BlockSpec((MEM_TOKENS, MEM_WIDTH), lambda i: (0, 1)), _full((1, MEM_WIDTH)), _full((1, MEM_WIDTH))],
        out_specs=pl.BlockSpec((tt, MEM_WIDTH), lambda i: (i, 0)),
        out_shape=jax.ShapeDtypeStruct((T, MEM_WIDTH), F32),
        compiler_params=_cp("parallel"), name=name)(proj, mkv, mkv, qn_w, kn_w)


def _mem_attn_bwd(proj, qcol, mkv, qn_w, kn_w, dmix, *, tt, name):
    T = proj.shape[0]
    nsteps = T // tt
    ocol = (dmix.shape[1] - MEM_WIDTH) // MEM_WIDTH

    def body(q_ref, k_ref, v_ref, qw_ref, kw_ref, do_ref, dq_ref, dkv_ref, dqw_ref, dkw_ref, dk_acc, dv_acc):
        step = pl.program_id(0)

        @pl.when(step == 0)
        def _():
            dk_acc[...] = jnp.zeros_like(dk_acc)
            dv_acc[...] = jnp.zeros_like(dv_acc)
            dqw_ref[...] = jnp.zeros_like(dqw_ref)

        qmasks = _mem_head_masks(tt)
        kmasks = _mem_head_masks(MEM_TOKENS)
        qhat, qr = _mem_head_rms(q_ref[...], qmasks)
        qn = qhat * qw_ref[...]
        khat, kr = _mem_head_rms(k_ref[...], kmasks)
        kn = (khat * kw_ref[...]).astype(BF16)
        v = v_ref[...].astype(BF16)
        dout = do_ref[...]
        dqn = jnp.zeros((tt, MEM_WIDTH), F32)
        dkn = jnp.zeros((MEM_TOKENS, MEM_WIDTH), F32)
        dvv = jnp.zeros((MEM_TOKENS, MEM_WIDTH), F32)
        for m in range(MEM_HEADS):
            qm = jnp.where(qmasks[m], qn, 0.0).astype(BF16)
            s = _dot_nt(qm, kn) * MEM_SCALE
            s = s - jnp.max(s, axis=-1, keepdims=True)
            p = jnp.exp(s)
            p = p / jnp.sum(p, axis=-1, keepdims=True)
            dom = jnp.where(qmasks[m], dout, 0.0).astype(BF16)
            dp = _dot_nt(dom, v)
            ds = (p * (dp - jnp.sum(p * dp, axis=-1, keepdims=True)) * MEM_SCALE).astype(BF16)
            dqn = jnp.where(qmasks[m], _dot(ds, kn), dqn)
            dkn = jnp.where(kmasks[m], _dot_tn(ds, qm), dkn)
            dvv = jnp.where(kmasks[m], _dot_tn(p, dom), dvv)
        dqw_ref[...] += jnp.sum(dqn * qhat, axis=0, keepdims=True)
        dq_ref[...] = _mem_head_rms_bwd(dqn * qw_ref[...], qhat, qr, qmasks).astype(BF16)
        dk_acc[...] += dkn
        dv_acc[...] += dvv

        @pl.when(step == nsteps - 1)
        def _():
            dk = dk_acc[...]
            dkw_ref[...] = jnp.sum(dk * khat, axis=0, keepdims=True)
            dkv_ref[:, :MEM_WIDTH] = _mem_head_rms_bwd(dk * kw_ref[...], khat, kr, kmasks)
            dkv_ref[:, MEM_WIDTH:] = dv_acc[...]

    return pl.pallas_call(
        body, grid=(nsteps,),
        in_specs=[pl.BlockSpec((tt, MEM_WIDTH), lambda i: (i, qcol)), pl.BlockSpec((MEM_TOKENS, MEM_WIDTH), lambda i: (0, 0)),
                  pl.BlockSpec((MEM_TOKENS, MEM_WIDTH), lambda i: (0, 1)), _full((1, MEM_WIDTH)), _full((1, MEM_WIDTH)),
                  pl.BlockSpec((tt, MEM_WIDTH), lambda i: (i, ocol))],
        out_specs=[pl.BlockSpec((tt, MEM_WIDTH), lambda i: (i, 0)), _full((MEM_TOKENS, 2 * MEM_WIDTH)),
                   _full((1, MEM_WIDTH)), _full((1, MEM_WIDTH))],
        out_shape=[jax.ShapeDtypeStruct((T, MEM_WIDTH), BF16), jax.ShapeDtypeStruct((MEM_TOKENS, 2 * MEM_WIDTH), F32),
                   jax.ShapeDtypeStruct((1, MEM_WIDTH), F32), jax.ShapeDtypeStruct((1, MEM_WIDTH), F32)],
        scratch_shapes=[pltpu.VMEM((MEM_TOKENS, MEM_WIDTH), F32), pltpu.VMEM((MEM_TOKENS, MEM_WIDTH), F32)],
        compiler_params=_cp("arbitrary"), name=name)(proj, mkv, mkv, qn_w, kn_w, dmix)


HALF = HEAD_DIM // 2
ATT_SCALE = HEAD_DIM ** -0.5
NEG = -1e30


def _rope_tables(T):
    inv = np.float32(ROPE_THETA) ** (-np.arange(HALF, dtype=np.float32) / np.float32(HALF))
    ang = np.arange(T, dtype=np.float32)[:, None] * inv[None, :].astype(np.float32)
    cos, sin = np.cos(ang).astype(np.float32), np.sin(ang).astype(np.float32)
    return jnp.asarray(np.concatenate([cos, cos], axis=-1)), jnp.asarray(np.concatenate([-sin, sin], axis=-1))


def _rope(x, cosf, sinsg):
    return x * cosf + pltpu.roll(x, HALF, 1) * sinsg


def _rope_bwd(dy, cosf, sinsg):
    return dy * cosf + pltpu.roll(dy * sinsg, HALF, 1)


def _q_prep_bwd(proj, w_heads, cosf, sinsg, dqs, *, tt, name):
    T = proj.shape[0]
    W = N_GROUPS * B_WIDTH

    def body(x_ref, w_ref, c_ref, s_ref, d0, d1, d2, dx_ref, dw_ref):
        @pl.when(pl.program_id(0) == 0)
        def _():
            dw_ref[...] = jnp.zeros_like(dw_ref)

        c, s = c_ref[...], s_ref[...]
        for gi, d_ref in enumerate((d0, d1, d2)):
            for h in range(B_HEADS):
                sl = slice((gi * B_HEADS + h) * HEAD_DIM, (gi * B_HEADS + h + 1) * HEAD_DIM)
                xhat, r = _head_rms(x_ref[:, sl])
                dyn = _rope_bwd(d_ref[:, h * HEAD_DIM:(h + 1) * HEAD_DIM], c, s)
                dw_ref[:, sl] += jnp.sum(dyn * xhat, axis=0, keepdims=True)
                dx_ref[:, sl] = _head_rms_bwd(dyn * w_ref[:, sl], xhat, r).astype(BF16)

    tbl = pl.BlockSpec((tt, HEAD_DIM), lambda i: (i, 0))
    dyb = pl.BlockSpec((tt, B_WIDTH), lambda i: (i, 0))
    return pl.pallas_call(
        body, grid=(T // tt,),
        in_specs=[pl.BlockSpec((tt, W), lambda i: (i, 0)), _full((1, W)), tbl, tbl, dyb, dyb, dyb],
        out_specs=[pl.BlockSpec((tt, W), lambda i: (i, 0)), _full((1, W))],
        out_shape=[jax.ShapeDtypeStruct((T, W), BF16), jax.ShapeDtypeStruct((1, W), F32)],
        compiler_params=_cp("arbitrary"), name=name)(proj, w_heads, cosf, sinsg, *dqs)


def _kv_prep_bwd(kv, w_heads, cosf, sinsg, dks, dvs, *, tt, name):
    T = kv.shape[0]

    def body(x_ref, w_ref, c_ref, s_ref, k0, k1, k2, v0, v1, v2, dx_ref, dw_ref):
        @pl.when(pl.program_id(0) == 0)
        def _():
            dw_ref[...] = jnp.zeros_like(dw_ref)

        c, s = c_ref[...], s_ref[...]
        for h in range(B_HEADS):
            sl = slice(h * HEAD_DIM, (h + 1) * HEAD_DIM)
            vs = slice(B_WIDTH + h * HEAD_DIM, B_WIDTH + (h + 1) * HEAD_DIM)
            xhat, r = _head_rms(x_ref[:, sl])
            dyn = _rope_bwd(k0[:, sl] + k1[:, sl] + k2[:, sl], c, s)
            dw_ref[:, sl] += jnp.sum(dyn * xhat, axis=0, keepdims=True)
            dx_ref[:, sl] = _head_rms_bwd(dyn * w_ref[:, sl], xhat, r).astype(BF16)
            dx_ref[:, vs] = (v0[:, sl] + v1[:, sl] + v2[:, sl]).astype(BF16)

    tbl = pl.BlockSpec((tt, HEAD_DIM), lambda i: (i, 0))
    dyb = pl.BlockSpec((tt, B_WIDTH), lambda i: (i, 0))
    return pl.pallas_call(
        body, grid=(T // tt,),
        in_specs=[dyb, _full((1, B_WIDTH)), tbl, tbl] + [dyb] * 6,
        out_specs=[pl.BlockSpec((tt, 2 * B_WIDTH), lambda i: (i, 0)), _full((1, B_WIDTH))],
        out_shape=[jax.ShapeDtypeStruct((T, 2 * B_WIDTH), BF16), jax.ShapeDtypeStruct((1, B_WIDTH), F32)],
        compiler_params=_cp("arbitrary"), name=name)(kv, w_heads, cosf, sinsg, *dks, *dvs)


def _band_masks(n_is_first=None):
    row = lax.broadcasted_iota(jnp.int32, (SPAN, SPAN), 0)
    col = lax.broadcasted_iota(jnp.int32, (SPAN, SPAN), 1)
    return row >= col, col >= row


def _dil_views(T, d):
    L = T // d
    return L, L // SPAN


def _dil_fwd(qr, kr, kv, gi, d, *, name):
    T = qr.shape[0]
    L, nb = _dil_views(T, d)

    def body(q_ref, kc_ref, kp_ref, vc_ref, vp_ref, o_ref, lse_ref):
        cur_ok, prev_band = _band_masks()
        prev_ok = prev_band & (pl.program_id(1) > 0)
        for h in range(B_HEADS):
            sl = slice(h * HEAD_DIM, (h + 1) * HEAD_DIM)
            q = q_ref[:, sl]
            sc = jnp.where(cur_ok, _dot_nt(q, kc_ref[:, sl]) * ATT_SCALE, NEG)
            sp = jnp.where(prev_ok, _dot_nt(q, kp_ref[:, sl]) * ATT_SCALE, NEG)
            m = jnp.maximum(jnp.max(sc, axis=-1, keepdims=True), jnp.max(sp, axis=-1, keepdims=True))
            pc = jnp.exp(sc - m)
            pp = jnp.exp(sp - m)
            l = jnp.sum(pc, axis=-1, keepdims=True) + jnp.sum(pp, axis=-1, keepdims=True)
            o_ref[:, sl] = (_dot(pc, vc_ref[:, sl]) + _dot(pp, vp_ref[:, sl])) / l
            lse_ref[:, sl] = jnp.broadcast_to(m + jnp.log(l), (SPAN, HEAD_DIM))

    blk = lambda f: pl.BlockSpec((SPAN, B_WIDTH), f)
    cur = lambda r, n: (n, r)
    prev = lambda r, n: (jnp.maximum(n - 1, 0), r)
    ov = jax.ShapeDtypeStruct((L, d * B_WIDTH), F32)
    o, lse = pl.pallas_call(
        body, grid=(d, nb),
        in_specs=[blk(lambda r, n: (n, r * N_GROUPS + gi)), blk(cur), blk(prev),
                  blk(lambda r, n: (n, 2 * r + 1)), blk(lambda r, n: (jnp.maximum(n - 1, 0), 2 * r + 1))],
        out_specs=[blk(cur), blk(cur)], out_shape=[ov, ov],
        compiler_params=_cp("parallel", "arbitrary"), name=name,
    )(qr.reshape(L, d * N_GROUPS * B_WIDTH), kr.reshape(L, d * B_WIDTH), kr.reshape(L, d * B_WIDTH),
      kv.reshape(L, d * 2 * B_WIDTH), kv.reshape(L, d * 2 * B_WIDTH))
    return o.reshape(T, B_WIDTH), lse.reshape(T, B_WIDTH)


def _combine_out_proj(res, os_, lses, mem_out, w, *, tt, name):
    T, N = res.shape
    K2 = mem_out.shape[1]

    def body(r_ref, o0, o1, o2, l0, l1, l2, a2_ref, w_ref, h_ref, y_ref, lse_ref):
        a, b, c = l0[...], l1[...], l2[...]
        m = jnp.maximum(jnp.maximum(a, b), c)
        wa, wb, wc = jnp.exp(a - m), jnp.exp(b - m), jnp.exp(c - m)
        den = wa + wb + wc
        y = (wa * o0[...] + wb * o1[...] + wc * o2[...]) / den
        y_ref[...] = y
        lse_ref[...] = m + jnp.log(den)
        h_ref[...] = r_ref[...] + _dot(y, w_ref[:B_WIDTH, :]) + _dot(a2_ref[...], w_ref[B_WIDTH:, :])

    blk = pl.BlockSpec((tt, B_WIDTH), lambda i: (i, 0))
    row = pl.BlockSpec((tt, N), lambda i: (i, 0))
    sh = jax.ShapeDtypeStruct((T, B_WIDTH), F32)
    return pl.pallas_call(
        body, grid=(T // tt,),
        in_specs=[row] + [blk] * 6 + [pl.BlockSpec((tt, K2), lambda i: (i, 0)), _full((B_WIDTH + K2, N))],
        out_specs=[row, blk, blk], out_shape=[jax.ShapeDtypeStruct((T, N), F32), sh, sh],
        compiler_params=_cp("parallel"), name=name)(res, *os_, *lses, mem_out, w)


DILS_UNROLL = 8


def _dils_specs(gi, d, nblk):
    blk = lambda f: pl.BlockSpec((SPAN * d, HEAD_DIM), f)
    return {
        "q": blk(lambda h, n: (n, gi * B_HEADS + h)), "q_next": blk(lambda h, n: (jnp.minimum(n + 1, nblk - 1), gi * B_HEADS + h)),
        "cur": blk(lambda h, n: (n, h)), "prev": blk(lambda h, n: (jnp.maximum(n - 1, 0), h)),
        "next": blk(lambda h, n: (jnp.minimum(n + 1, nblk - 1), h)),
        "v": blk(lambda h, n: (n, B_HEADS + h)), "v_prev": blk(lambda h, n: (jnp.maximum(n - 1, 0), B_HEADS + h)),
    }


def _dils_fwd(qr, kr, kv, gi, d, *, name):
    T = qr.shape[0]
    nblk = T // (SPAN * d)
    sp = _dils_specs(gi, d, nblk)

    def body(q_ref, kc_ref, vc_ref, o_ref, lse_ref, k_before, v_before):
        @pl.when(pl.program_id(1) == 0)
        def _():
            k_before[...] = jnp.zeros_like(k_before)
            v_before[...] = jnp.zeros_like(v_before)

        cur_ok, prev_band = _band_masks()
        prev_ok = prev_band & (pl.program_id(1) > 0)

        def residue(r, carry):
            rows = pl.ds(r, SPAN, stride=d)
            q, kc, vc = q_ref[rows, :], kc_ref[rows, :].astype(BF16), vc_ref[rows, :].astype(BF16)
            sc = jnp.where(cur_ok, _dot_nt(q, kc) * ATT_SCALE, NEG)
            sp_ = jnp.where(prev_ok, _dot_nt(q, k_before[r]) * ATT_SCALE, NEG)
            m = jnp.maximum(jnp.max(sc, axis=-1, keepdims=True), jnp.max(sp_, axis=-1, keepdims=True))
            pc = jnp.exp(sc - m)
            pp = jnp.exp(sp_ - m)
            l = jnp.sum(pc, axis=-1, keepdims=True) + jnp.sum(pp, axis=-1, keepdims=True)
            o_ref[rows, :] = (_dot(pc, vc) + _dot(pp, v_before[r])) / l
            lse_ref[rows, :] = jnp.broadcast_to(m + jnp.log(l), (SPAN, HEAD_DIM))
            k_before[r] = kc
            v_before[r] = vc
            return carry

        lax.fori_loop(0, d, residue, 0, unroll=min(d, DILS_UNROLL))

    sh = jax.ShapeDtypeStruct((T, B_WIDTH), F32)
    return pl.pallas_call(
        body, grid=(B_HEADS, nblk), in_specs=[sp["q"], sp["cur"], sp["v"]],
        out_specs=[sp["cur"], sp["cur"]], out_shape=[sh, sh],
        scratch_shapes=[pltpu.VMEM((d, SPAN, HEAD_DIM), BF16), pltpu.VMEM((d, SPAN, HEAD_DIM), BF16)],
        compiler_params=_cp("parallel", "arbitrary"), name=name)(qr, kr, kv)


DIL_BWD_GROUP = {1: 16, 4: 1, 16: 1}


def _dil_bwd(qr, kr, kv, dmix, lse, dd, gi, d, *, name, dep=None):
    T = qr.shape[0]
    G = DIL_BWD_GROUP[d]
    band = SPAN * d
    tb = G * band
    nblk = T // tb
    n_units = T // SPAN

    keep = G == 1

    def kernel_body(q_ref, dy_ref, lse_ref, dd_ref, kc_ref, vc_ref, *rest):
        if keep:
            dq_ref, dk_ref, dv_ref, dk_acc, dv_acc, k_before, v_before = rest
        else:
            kp_ref, vp_ref, dq_ref, dk_ref, dv_ref, dk_acc, dv_acc = rest
        n = pl.program_id(1)

        @pl.when(n == 0)
        def _():
            dk_acc[...] = jnp.zeros_like(dk_acc)
            dv_acc[...] = jnp.zeros_like(dv_acc)
            if keep:
                k_before[...] = jnp.zeros_like(k_before)
                v_before[...] = jnp.zeros_like(v_before)

        cur_ok, prev_band = _band_masks()
        for j in range(G):
            def residue(r, carry, j=j):
                off = j * band + r
                rows = pl.ds(off, SPAN, stride=d)
                q, dy = q_ref[rows, :], dy_ref[rows, :]
                lse_h = jnp.max(lse_ref[rows, :], axis=-1, keepdims=True)
                dd_h = jnp.max(dd_ref[rows, :], axis=-1, keepdims=True)
                kc, vc = kc_ref[rows, :].astype(BF16), vc_ref[rows, :].astype(BF16)
                if j > 0:
                    before = pl.ds(off - band, SPAN, stride=d)
                    kp, vp = kc_ref[before, :], vc_ref[before, :]
                    prev_ok = prev_band
                elif keep:
                    kp, vp = k_before[r], v_before[r]
                    k_before[r] = kc
                    v_before[r] = vc
                    prev_ok = prev_band & (n > 0)
                else:
                    before = pl.ds((G - 1) * band + r, SPAN, stride=d)
                    kp, vp = kp_ref[before, :], vp_ref[before, :]
                    prev_ok = prev_band & (n > 0)
                pc = jnp.exp(jnp.where(cur_ok, _dot_nt(q, kc) * ATT_SCALE, NEG) - lse_h)
                pp = jnp.exp(jnp.where(prev_ok, _dot_nt(q, kp) * ATT_SCALE, NEG) - lse_h)
                dsc = pc * (_dot_nt(dy, vc) - dd_h) * ATT_SCALE
                dsp = pp * (_dot_nt(dy, vp) - dd_h) * ATT_SCALE
                dq_ref[rows, :] = _dot(dsc, kc) + _dot(dsp, kp)
                u = (n * G + j) * d + r
                here = pl.ds(pl.multiple_of(u * SPAN, SPAN), SPAN)
                dk_acc[here, :] += _dot_tn(dsc, q)
                dv_acc[here, :] += _dot_tn(pc, dy)
                there = pl.ds(pl.multiple_of(jnp.maximum(u - d, 0) * SPAN, SPAN), SPAN)
                dk_acc[there, :] += _dot_tn(dsp, q)
                dv_acc[there, :] += _dot_tn(pp, dy)
                return carry

            lax.fori_loop(0, d, residue, 0, unroll=min(d, DILS_UNROLL))

        @pl.when(n == nblk - 1)
        def _():
            def place(u, carry):
                rows = pl.ds((u // d) * band + u % d, SPAN, stride=d)
                src = pl.ds(pl.multiple_of(u * SPAN, SPAN), SPAN)
                dk_ref[rows, :] = dk_acc[src, :]
                dv_ref[rows, :] = dv_acc[src, :]
                return carry

            lax.fori_loop(0, n_units, place, 0)

    blk = lambda f: pl.BlockSpec((tb, HEAD_DIM), f)
    cur = lambda h, n: (n, h)
    prev = lambda h, n: (jnp.maximum(n - 1, 0), h)
    whole = pl.BlockSpec((T, HEAD_DIM), lambda h, n: (0, h))
    sh = jax.ShapeDtypeStruct((T, B_WIDTH), F32)
    v_cur = blk(lambda h, n: (n, B_HEADS + h))
    if keep:
        kv_specs, kv_args = [blk(cur), v_cur], [kr, kv]
        kept = [pltpu.VMEM((d, SPAN, HEAD_DIM), BF16), pltpu.VMEM((d, SPAN, HEAD_DIM), BF16)]
    else:
        kv_specs = [blk(cur), v_cur, blk(prev), blk(lambda h, n: (jnp.maximum(n - 1, 0), B_HEADS + h))]
        kv_args, kept = [kr, kv, kr, kv], []
    body, dep_specs, dep_args = _dep(kernel_body, 4 + len(kv_args), dep)
    return pl.pallas_call(
        body, grid=(B_HEADS, nblk),
        in_specs=[blk(lambda h, n: (n, gi * B_HEADS + h)), blk(cur), blk(cur), blk(cur)] + kv_specs + dep_specs,
        out_specs=[blk(cur), whole, whole], out_shape=[sh, sh, sh],
        scratch_shapes=[pltpu.VMEM((T, HEAD_DIM), F32), pltpu.VMEM((T, HEAD_DIM), F32)] + kept,
        compiler_params=_cp("parallel", "arbitrary"), name=name)(qr, dmix, lse, dd, *kv_args, *dep_args)


A_MQ_COL = 4 * A_WIDTH // MEM_WIDTH
B_MQ_COL = N_GROUPS * B_WIDTH // MEM_WIDTH


def _row(v):
    return v.reshape(1, -1).astype(F32)


def _local_step(x, mem, tgt, get_w, P, put_g, first_dep=None, forward_point=lambda i, value: value):
    T = x.shape[0]
    cosf, sinsg = _rope_tables(T)
    lb_soft = jax.nn.softmax(P["a_lb_logits"].astype(F32), axis=0)
    lb = lb_soft[0:1]
    qw_heads = jnp.repeat(P["b_qnorm"][0], B_HEADS, axis=0).reshape(1, -1)
    kw_heads = jnp.tile(_row(P["b_knorm"]), (1, B_HEADS))
    mqw = [jnp.tile(_row(P["mem_qnorm"][l]), (1, MEM_HEADS)) for l in range(2)]
    mkw = [jnp.tile(_row(P["mem_knorm"][l]), (1, MEM_HEADS)) for l in range(2)]
    nmix = [_row(P["norm_mix"][l]) for l in range(2)]
    nffn = [_row(P["norm_ffn"][l]) for l in range(2)]
    mnorm = [_row(P["mem_norm"][l]) for l in range(2)]
    kvn = _row(P["kv_norm"])
    onorm = _row(P["a_onorm"])
    W = {}

    def w_of(name, after=None):
        if name not in W:
            W[name] = get_w(name, after)
        return W[name]

    proj_a, xn0 = _rms_matmul(x, nmix[0], w_of("a_w_in"), tt=512, tn=1664, wt=True, name="proj_a", dep=first_dep)
    mkv0, mn0 = _rms_matmul(mem, mnorm[0], w_of("w_mem_kv0"), tt=MEM_TOKENS, tn=2 * MEM_WIDTH, wt=False, name="mem_kv0")
    o_raw, st = _hgrn2_fwd(proj_a, lb, name="hgrn2_fwd")
    o_raw = forward_point(0, o_raw)
    mo0 = _mem_attn_fwd(proj_a, A_MQ_COL, mkv0, mqw[0], mkw[0], tt=1024, name="mem_attn_fwd0")
    hm0, mm0 = _a_post_out_proj(x, o_raw, proj_a, onorm, mo0, w_of("w_out0", mo0), tt=512, name="out_proj0")
    hm0 = forward_point(1, hm0)
    gu0, hn0 = _rms_matmul(hm0, nffn[0], w_of("w_gate_up0", hm0), tt=512, tn=1408, wt=True, out_dtype=BF16, name="gate_up0")
    h1 = _swiglu_down(hm0, gu0, w_of("w_down0", gu0), tt=512, name="down0")
    h1 = forward_point(2, h1)
    kv, hkn, kr = _rms_matmul(h1, kvn, w_of("w_kv", h1), tt=512, tn=768, wt=True, name="kv_proj",
                              rotate=(kw_heads, cosf, sinsg))

    proj_b, xn1, qr = _rms_matmul(h1, nmix[1], w_of("b_w_in", kr), tt=512, tn=1280, wt=True, name="proj_b",
                                  rotate=(qw_heads, cosf, sinsg))
    proj_b = forward_point(3, proj_b)
    mkv1, mn1 = _rms_matmul(mem, mnorm[1], w_of("w_mem_kv1", kr), tt=MEM_TOKENS, tn=2 * MEM_WIDTH, wt=False, name="mem_kv1")
    outs = [(_dil_fwd if d == 1 else _dils_fwd)(qr, kr, kv, gi, d, name=f"dil_fwd{gi}") for gi, d in enumerate(DILATIONS)]
    mo1 = _mem_attn_fwd(proj_b, B_MQ_COL, mkv1, mqw[1], mkw[1], tt=1024, name="mem_attn_fwd1")
    hm1, mm1, lse_tot = _combine_out_proj(h1, [o for o, _ in outs], [s for _, s in outs], mo1, w_of("w_out1", mo1),
                                          tt=512, name="out_proj1")
    gu1, hn1 = _rms_matmul(hm1, nffn[1], w_of("w_gate_up1", hm1), tt=512, tn=1408, wt=True, out_dtype=BF16, name="gate_up1")
    dy, sq = _swiglu_down_loss(hm1, gu1, w_of("w_down1", gu1), tgt, tt=512, name="down1_loss")

    gP = {}
    zeros_mem = jnp.zeros((MEM_TOKENS, D_MODEL), F32)

    def ffn_bwd(l, dh, hm, gu, hn):
        dgu, g_wd = _swiglu_bwd(dh, gu, w_of(f"w_down{l}"), tt=256, name=f"swiglu_bwd{l}")
        g_wgu = _mm_tn(dgu, hn, tt=512, tka=1408, name=f"g_w_gate_up{l}")
        sent = put_g({f"w_down{l}": g_wd, f"w_gate_up{l}": g_wgu})
        dhm, g_nf = _rms_bwd_dx(hm, nffn[l], w_of(f"w_gate_up{l}"), dgu, dh, tt=512, wt=True, name=f"gate_up_bwd{l}", dep=sent)
        return dhm, g_nf

    def mix_bwd(l, dhm, mix_main, mix_mem, proj, qcol, mkv, mn):
        dmix, g_wout, *head_dots = _out_proj_bwd(dhm, mix_main, mix_mem, w_of(f"w_out{l}"), tt=512, name=f"out_proj_bwd{l}",
                                                 head_dots=l == 1)
        dmq, dmkv, dqw, dkw = _mem_attn_bwd(proj, qcol, mkv, mqw[l], mkw[l], dmix, tt=1024, name=f"mem_attn_bwd{l}")
        g_wmkv = _mm_tn(mn, dmkv, tt=MEM_TOKENS, tka=512, name=f"g_w_mem_kv{l}")
        sent = put_g({f"w_out{l}": g_wout, f"w_mem_kv{l}": g_wmkv})
        _, g_mn = _rms_bwd_dx(mem, mnorm[l], w_of(f"w_mem_kv{l}"), dmkv, zeros_mem, tt=MEM_TOKENS, wt=False, name=f"mem_kv_bwd{l}")
        fold = lambda v: v.reshape(MEM_HEADS, MEM_HEAD_DIM).sum(axis=0)
        return dmix, dmq, g_mn, fold(dqw), fold(dkw), sent, head_dots

    dhm1, g_nf1 = ffn_bwd(1, dy, hm1, gu1, hn1)
    dmix1, dmq1, g_mn1, g_mq1, g_mk1, sent, (dd,) = mix_bwd(1, dhm1, mm1, mo1, proj_b, B_MQ_COL, mkv1, mn1)
    dqs, dks, dvs = [], [], []
    for gi, d in enumerate(DILATIONS):
        dq_g, dk_g, dv_g = _dil_bwd(qr, kr, kv, dmix1, lse_tot, dd, gi, d, name=f"dil_bwd{gi}", dep=sent if gi == 0 else None)
        dqs.append(dq_g)
        dks.append(dk_g)
        dvs.append(dv_g)
    dq_raw, dqw = _q_prep_bwd(proj_b, qw_heads, cosf, sinsg, dqs, tt=512, name="q_prep_bwd")
    dkv, dkw = _kv_prep_bwd(kv, kw_heads, cosf, sinsg, dks, dvs, tt=512, name="kv_prep_bwd")
    dproj_b = [dq_raw, dmq1]
    g_wb = _mm_tn_pieces(dproj_b, xn1, tt=512, name="g_b_w_in")
    g_wkv = _mm_tn(dkv, hkn, tt=512, tka=768, name="g_w_kv")
    sent = put_g({"b_w_in": g_wb, "w_kv": g_wkv})
    dh1, g_nm1 = _rms_bwd_dx(h1, nmix[1], w_of("b_w_in"), dproj_b, dhm1, tt=512, wt=True, name="proj_b_bwd", dep=sent)
    dh1, g_kvn = _rms_bwd_dx(h1, kvn, w_of("w_kv"), dkv, dh1, tt=512, wt=True, name="kv_proj_bwd")

    dhm0, g_nf0 = ffn_bwd(0, dh1, hm0, gu0, hn0)
    dmix0, dmq0, g_mn0, g_mq0, g_mk0, sent, _ = mix_bwd(0, dhm0, mm0, mo0, proj_a, A_MQ_COL, mkv0, mn0)
    do_raw, dg, g_onorm = _a_post_bwd(o_raw, proj_a, onorm, dmix0, tt=512, name="a_post_bwd", dep=sent)
    dq, dz, dv, dlb = _hgrn2_bwd(proj_a, lb, st, do_raw, name="hgrn2_bwd")
    dproj_a = [dq, dz, dv, dg, dmq0]
    sent = put_g({"a_w_in": _mm_tn_pieces(dproj_a, xn0, tt=512, name="g_a_w_in")})
    gx, g_nm0 = _rms_bwd_dx(x, nmix[0], w_of("a_w_in"), dproj_a, dhm0, tt=512, wt=True, name="proj_a_bwd", dep=sent)

    dl0 = lb_soft[0:1] * lb_soft[1:2] * dlb
    gP["a_lb_logits"] = jnp.concatenate([dl0, -dl0], axis=0)
    gP["a_onorm"] = g_onorm
    gP["norm_mix"] = jnp.concatenate([g_nm0, g_nm1], axis=0)
    gP["norm_ffn"] = jnp.concatenate([g_nf0, g_nf1], axis=0)
    gP["b_qnorm"] = dqw.reshape(N_GROUPS, B_HEADS, HEAD_DIM).sum(axis=1)[None]
    gP["kv_norm"] = g_kvn.reshape(-1)
    gP["b_knorm"] = dkw.reshape(B_HEADS, HEAD_DIM).sum(axis=0)
    gP["mem_norm"] = jnp.concatenate([g_mn0, g_mn1], axis=0)
    gP["mem_qnorm"] = jnp.stack([g_mq0, g_mq1])
    gP["mem_knorm"] = jnp.stack([g_mk0, g_mk1])
    return sq, gx, gP


MESH_ID = pl.DeviceIdType.MESH
HBM_SPEC = pl.BlockSpec(memory_space=pltpu.HBM)


def _position():
    return lax.axis_index("x"), lax.axis_index("y"), lax.axis_index("c")


def _all_gather_direct(block, after, *, name):
    def body(x_ref, after_ref, out_ref, send_sems, recv_sems, local_sem):
        x, y, c = _position()
        me = 4 * x + 2 * y + c
        mine = pltpu.make_async_copy(x_ref, out_ref.at[me], local_sem)
        mine.start()
        copies = []
        for k in ALL_PEERS:
            cp = pltpu.make_async_remote_copy(
                src_ref=x_ref, dst_ref=out_ref.at[me], send_sem=send_sems.at[k - 1], recv_sem=recv_sems.at[k - 1],
                device_id=_peer(k, x, y, c), device_id_type=MESH_ID)
            cp.start()
            copies.append(cp)
        for cp in copies:
            cp.wait()
        mine.wait()

    return pl.pallas_call(
        body, out_shape=jax.ShapeDtypeStruct((N_DEV,) + block.shape, block.dtype),
        in_specs=[HBM_SPEC, pl.BlockSpec(memory_space=pl.ANY)], out_specs=HBM_SPEC,
        scratch_shapes=[pltpu.SemaphoreType.DMA((7,)), pltpu.SemaphoreType.DMA((7,)), pltpu.SemaphoreType.DMA],
        name=name)(block, after)


SEM_SPEC = pl.BlockSpec(memory_space=pltpu.SEMAPHORE)
ANY_SPEC = pl.BlockSpec(memory_space=pl.ANY)
DATAFLOW = pltpu.SideEffectType.DATAFLOW_SIDE_EFFECTING


def _peer(k, x, y, c):
    return (1 - x if (k >> 2) & 1 else x, 1 - y if (k >> 1) & 1 else y, 1 - c if k & 1 else c)


def _own_slot_filled(own_block):
    x, y, c = _position()
    zone = lax.empty((N_DEV,) + own_block.shape, own_block.dtype)
    return lax.dynamic_update_slice_in_dim(zone, own_block[None], 4 * x + 2 * y + c, axis=0)


ALL_PEERS = tuple(range(1, N_DEV))
SIBLING_AND_SAME_CORE = (1, 2, 4, 6)
SAME_CORE = (2, 4, 6)


def _split_start(srcs, scatter, after, *, name, relations=ALL_PEERS, carried=None):
    n = len(srcs)
    extra = ([] if after is None else [after]) + ([] if carried is None else [carried])
    n_carried = 0 if carried is None else 1
    x, y, c = _position()
    me = 4 * x + 2 * y + c
    lands = [_own_slot_filled(lax.dynamic_index_in_dim(s, me, 0, keepdims=False) if scatter else s) for s in srcs]

    def body(*refs):
        src_refs, land_refs = refs[:n], refs[n:2 * n]
        send_sems, recv_sems = refs[2 * n + len(extra)], refs[2 * n + len(extra) + 1]
        token = refs[2 * n + len(extra) + 2 + 2 * n]
        bx, by, bc = _position()
        bme = 4 * bx + 2 * by + bc
        for a in range(n):
            for k in relations:
                tx, ty, tc = _peer(k, bx, by, bc)
                src = src_refs[a].at[4 * tx + 2 * ty + tc] if scatter else src_refs[a]
                pltpu.make_async_remote_copy(
                    src_ref=src, dst_ref=land_refs[a].at[bme],
                    send_sem=send_sems.at[7 * a + k - 1], recv_sem=recv_sems.at[7 * a + k - 1],
                    device_id=(tx, ty, tc), device_id_type=MESH_ID).start()
        token[...] = jnp.zeros_like(token)

    hbm = lambda a: pltpu.HBM(a.shape, a.dtype)
    outs = pl.pallas_call(
        body, name=name,
        out_shape=(pltpu.SemaphoreType.DMA((7 * n,)), pltpu.SemaphoreType.DMA((7 * n,)),
                   *[hbm(s) for s in srcs], *[hbm(l) for l in lands], jax.ShapeDtypeStruct((8, 128), F32),
                   *([hbm(carried)] if n_carried else [])),
        in_specs=[HBM_SPEC] * (2 * n) + [ANY_SPEC] * len(extra),
        out_specs=(SEM_SPEC, SEM_SPEC, *[HBM_SPEC] * (2 * n), pl.BlockSpec(memory_space=pltpu.VMEM), *([ANY_SPEC] * n_carried)),
        input_output_aliases={**{i: 2 + i for i in range(2 * n)},
                              **({2 * n + len(extra) - 1: 2 * n + 3} if n_carried else {})},
        compiler_params=pltpu.CompilerParams(has_side_effects=DATAFLOW),
    )(*[pltpu.with_memory_space_constraint(s, pltpu.HBM) for s in srcs],
      *[pltpu.with_memory_space_constraint(l, pltpu.HBM) for l in lands], *extra)
    return {"n": n, "relations": relations, "send": outs[0], "recv": outs[1], "srcs": list(outs[2:2 + n]),
            "lands": list(outs[2 + n:2 + 2 * n]), "token": outs[2 * n + 2], "carried": outs[-1] if n_carried else None}


def _forward_start(lands, carried, *, name):
    n = len(lands)

    def body(*refs):
        land_refs = refs[:n]
        send_sems, recv_sems = refs[n + 1], refs[n + 2]
        bx, by, bc = _position()
        for a in range(n):
            for k in SAME_CORE:
                tx, ty, tc = _peer(k, bx, by, bc)
                block = land_refs[a].at[4 * tx + 2 * ty + tc]
                pltpu.make_async_remote_copy(
                    src_ref=block, dst_ref=block,
                    send_sem=send_sems.at[7 * a + k - 1], recv_sem=recv_sems.at[7 * a + k - 1],
                    device_id=(bx, by, 1 - bc), device_id_type=MESH_ID).start()

    hbm = lambda a: pltpu.HBM(a.shape, a.dtype)
    outs = pl.pallas_call(
        body, name=name,
        out_shape=(pltpu.SemaphoreType.DMA((7 * n,)), pltpu.SemaphoreType.DMA((7 * n,)),
                   *[hbm(l) for l in lands], hbm(carried)),
        in_specs=[HBM_SPEC] * n + [ANY_SPEC],
        out_specs=(SEM_SPEC, SEM_SPEC, *[HBM_SPEC] * n, ANY_SPEC),
        input_output_aliases={i: 2 + i for i in range(n + 1)},
        compiler_params=pltpu.CompilerParams(has_side_effects=DATAFLOW),
    )(*lands, carried)
    handle = {"n": n, "relations": SAME_CORE, "send": outs[0], "recv": outs[1], "srcs": [], "lands": list(outs[2:2 + n])}
    return handle, outs[-1]


def _split_wait(handle, after, *, name):
    n, ns = handle["n"], len(handle["srcs"])

    def body(*refs):
        land_refs = refs[ns:ns + n]
        send_sems, recv_sems = refs[ns + n], refs[ns + n + 1]
        bx, by, bc = _position()
        for a in range(n):
            for k in handle["relations"]:
                block = land_refs[a].at[0]
                cp = pltpu.make_async_remote_copy(
                    src_ref=block, dst_ref=block,
                    send_sem=send_sems.at[7 * a + k - 1], recv_sem=recv_sems.at[7 * a + k - 1],
                    device_id=_peer(k, bx, by, bc), device_id_type=MESH_ID)
                cp.wait_send()
                cp.wait_recv()

    hbm = lambda a: pltpu.HBM(a.shape, a.dtype)
    outs = pl.pallas_call(
        body, name=name,
        out_shape=(*[hbm(s) for s in handle["srcs"]], *[hbm(l) for l in handle["lands"]]),
        in_specs=[HBM_SPEC] * (ns + n) + [SEM_SPEC, SEM_SPEC, ANY_SPEC],
        out_specs=tuple([HBM_SPEC] * (ns + n)),
        input_output_aliases={i: i for i in range(ns + n)},
        compiler_params=pltpu.CompilerParams(has_side_effects=DATAFLOW),
    )(*handle["srcs"], *handle["lands"], handle["send"], handle["recv"], after)
    return list(outs[ns:])


def _sum_sources(parts, *, tr, name):
    n, R, C = parts.shape

    def body(p_ref, o_ref):
        acc = p_ref[0].astype(F32)
        for s in range(1, n):
            acc = acc + p_ref[s].astype(F32)
        o_ref[...] = acc

    return pl.pallas_call(
        body, grid=(R // tr,), in_specs=[pl.BlockSpec((n, tr, C), lambda i: (0, i, 0))],
        out_specs=pl.BlockSpec((tr, C), lambda i: (i, 0)),
        out_shape=jax.ShapeDtypeStruct((R, C), F32), compiler_params=_cp("parallel"), name=name)(parts)


def _adamw_math(g, w, m, v):
    c1 = 1.0 - ADAM_B1 ** ADAM_STEP
    c2 = 1.0 - ADAM_B2 ** ADAM_STEP
    nm = ADAM_B1 * m + (1.0 - ADAM_B1) * g
    nv = ADAM_B2 * v + (1.0 - ADAM_B2) * (g * g)
    return -ADAM_LR * ((nm / c1) / (jnp.sqrt(nv / c2) + ADAM_EPS) + ADAM_WD * w), nm, nv


ADAMW_STRIP = 16


def _reduce_adamw(received, w, m, v, *, tr, name):
    L, R, C = w.shape

    def body(*refs):
        p_refs = refs[:L]
        w_ref, m_ref, v_ref, g_ref, d_ref, nm_ref, nv_ref = refs[L:]
        for l in range(L):
            @pl.when(pl.program_id(0) == l)
            def _(p_ref=p_refs[l]):
                def strip(i, carry):
                    rows = pl.ds(pl.multiple_of(i * ADAMW_STRIP, ADAMW_STRIP), ADAMW_STRIP)
                    acc = p_ref[0, rows, :].astype(F32)
                    for s in range(1, N_DEV):
                        acc = acc + p_ref[s, rows, :].astype(F32)
                    g_ref[rows, :] = acc
                    d_ref[rows, :], nm_ref[rows, :], nv_ref[rows, :] = _adamw_math(acc, w_ref[rows, :], m_ref[rows, :], v_ref[rows, :])
                    return carry

                lax.fori_loop(0, tr // ADAMW_STRIP, strip, 0)

    p_spec = pl.BlockSpec((N_DEV, tr, C), lambda l, i: (0, i, 0))
    blk = pl.BlockSpec((None, tr, C), lambda l, i: (l, i, 0))
    sh = jax.ShapeDtypeStruct((L, R, C), F32)
    return pl.pallas_call(
        body, grid=(L, R // tr), in_specs=[p_spec] * L + [blk] * 3, out_specs=[blk] * 4, out_shape=[sh] * 4,
        compiler_params=_cp("parallel", "parallel"), name=name)(*received, w, m, v)


def _adamw(g, w, m, v, *, tr, name):
    L, R, C = w.shape

    def body(g_ref, w_ref, m_ref, v_ref, d_ref, nm_ref, nv_ref):
        d_ref[...], nm_ref[...], nv_ref[...] = _adamw_math(g_ref[...], w_ref[...], m_ref[...], v_ref[...])

    blk = pl.BlockSpec((None, tr, C), lambda l, i: (l, i, 0))
    sh = jax.ShapeDtypeStruct((L, R, C), F32)
    return pl.pallas_call(
        body, grid=(L, R // tr), in_specs=[blk] * 4, out_specs=[blk] * 3, out_shape=[sh] * 3,
        compiler_params=_cp("parallel", "parallel"), name=name)(g, w, m, v)


UNITS = {
    "a_w_in": ("a_w_in", 0, True), "w_mem_kv0": ("w_mem_kv", 0, False), "w_out0": ("w_out", 0, False),
    "w_gate_up0": ("w_gate_up", 0, True), "w_down0": ("w_down", 0, False), "w_kv": ("w_kv", None, True),
    "b_w_in": ("b_w_in", 0, True), "w_mem_kv1": ("w_mem_kv", 1, False), "w_out1": ("w_out", 1, False),
    "w_gate_up1": ("w_gate_up", 1, True), "w_down1": ("w_down", 1, False),
}
BIG = ("a_w_in", "b_w_in", "w_kv", "w_mem_kv", "w_out", "w_gate_up", "w_down")
ADAMW_ROW_TILE = {"a_w_in": 208, "b_w_in": 160, "w_kv": 192, "w_mem_kv": 128, "w_out": 128, "w_gate_up": 352, "w_down": 352}


def _wire_block(weights, unit):
    name, layer, col = UNITS[unit]
    a = weights[name] if layer is None else weights[name][layer]
    return (a.T if col else a).astype(BF16)


SMALL_REPLICATED = ("norm_mix", "norm_ffn", "b_qnorm", "kv_norm", "b_knorm", "mem_norm", "mem_qnorm", "mem_knorm")
SMALL_SHARDED = ("a_lb_logits", "a_onorm")
SMALL_ORDER = SMALL_REPLICATED + SMALL_SHARDED
LANES = 128


def _prod(shape):
    n = 1
    for s in shape:
        n *= s
    return n


def _pack_flat(arrays, rows, cols, dtype):
    flat = jnp.concatenate([a.reshape(-1).astype(dtype) for a in arrays])
    return jnp.pad(flat, (0, rows * cols - flat.shape[0])).reshape(rows, cols)


def _unpack_flat(packed, shapes):
    flat = packed.reshape(-1)
    out, off = [], 0
    for s in shapes:
        out.append(flat[off:off + _prod(s)].reshape(s))
        off += _prod(s)
    return out


def kernel(x, mem, norm_mix, norm_ffn, a_w_in, a_lb_logits, a_onorm, b_w_in, b_qnorm, kv_norm, w_kv, b_knorm, mem_norm, w_mem_kv, mem_qnorm, mem_knorm, w_out, w_gate_up, w_down, loss_target, m_norm_mix, m_norm_ffn, m_a_w_in, m_a_lb_logits, m_a_onorm, m_b_w_in, m_b_qnorm, m_kv_norm, m_w_kv, m_b_knorm, m_mem_norm, m_w_mem_kv, m_mem_qnorm, m_mem_knorm, m_w_out, m_w_gate_up, m_w_down, v_norm_mix, v_norm_ffn, v_a_w_in, v_a_lb_logits, v_a_onorm, v_b_w_in, v_b_qnorm, v_kv_norm, v_w_kv, v_b_knorm, v_mem_norm, v_w_mem_kv, v_mem_qnorm, v_mem_knorm, v_w_out, v_w_gate_up, v_w_down):
    names = ("norm_mix", "norm_ffn", "a_w_in", "a_lb_logits", "a_onorm", "b_w_in", "b_qnorm", "kv_norm", "w_kv", "b_knorm",
             "mem_norm", "w_mem_kv", "mem_qnorm", "mem_knorm", "w_out", "w_gate_up", "w_down")
    w = dict(zip(names, (norm_mix, norm_ffn, a_w_in, a_lb_logits, a_onorm, b_w_in, b_qnorm, kv_norm, w_kv, b_knorm,
                         mem_norm, w_mem_kv, mem_qnorm, mem_knorm, w_out, w_gate_up, w_down)))
    m = dict(zip(names, (m_norm_mix, m_norm_ffn, m_a_w_in, m_a_lb_logits, m_a_onorm, m_b_w_in, m_b_qnorm, m_kv_norm, m_w_kv,
                         m_b_knorm, m_mem_norm, m_w_mem_kv, m_mem_qnorm, m_mem_knorm, m_w_out, m_w_gate_up, m_w_down)))
    v = dict(zip(names, (v_norm_mix, v_norm_ffn, v_a_w_in, v_a_lb_logits, v_a_onorm, v_b_w_in, v_b_qnorm, v_kv_norm, v_w_kv,
                         v_b_knorm, v_mem_norm, v_w_mem_kv, v_mem_qnorm, v_mem_knorm, v_w_out, v_w_gate_up, v_w_down)))

    first = ["a_w_in", "w_mem_kv0"]
    later = [["w_out0", "w_gate_up0"], ["w_down0", "w_kv"], ["b_w_in", "w_mem_kv1"], ["w_out1", "w_gate_up1", "w_down1"]]
    first_half, second_half = {}, {}

    def start_first_half(i, after, carried=None):
        first_half[i] = _split_start([_wire_block(w, u) for u in later[i]], False, after, name=f"gather{i}_start",
                                     relations=SIBLING_AND_SAME_CORE, carried=carried)
        return first_half[i]

    opening = _split_start([_wire_block(w, u) for u in first] + [_pack_flat([a_lb_logits, a_onorm], 8, LANES, F32)],
                           False, None, name="gather_first_start", relations=SIBLING_AND_SAME_CORE)
    token = start_first_half(0, opening["token"])["token"]
    token = start_first_half(1, token)["token"]
    opening, token = _forward_start(_split_wait(opening, token, name="gather_first_landed"), token, name="gather_first_forward")
    gathered = _split_wait(opening, token, name="gather_first_wait")
    full = {u: g.reshape(-1, g.shape[-1]) for u, g in zip(first, gathered)}
    small_in = gathered[-1].reshape(N_DEV, -1)
    P = {n: w[n] for n in SMALL_REPLICATED}
    P["a_lb_logits"] = small_in[:, :192].reshape(N_DEV, 2, 96).transpose(1, 0, 2).reshape(2, A_WIDTH)
    P["a_onorm"] = small_in[:, 192:288].reshape(1, A_WIDTH)

    def forward_point(i, value):
        landed = _split_wait(first_half[i], value, name=f"gather{i}_landed")
        second_half[i], value = _forward_start(landed, value, name=f"gather{i}_forward")
        if i + 2 < len(later):
            value = start_first_half(i + 2, None, carried=value)["carried"]
        return value

    def get_w(unit, after):
        if unit not in full:
            i = [unit in group for group in later].index(True)
            for u, land in zip(later[i], _split_wait(second_half[i], after, name=f"gather{i}_wait")):
                full[u] = land.reshape(-1, land.shape[-1])
        return full[unit]

    sent = []

    def put_g(group):
        units = list(group)
        handle = _split_start([group[u].reshape(N_DEV, -1, group[u].shape[-1]) for u in units], True, None,
                              name=f"scatter{len(sent)}_start")
        sent.append((units, handle))
        return handle["token"]

    sq, gx, gP = _local_step(x[0], mem[0], loss_target[0], get_w, P, put_g, forward_point=forward_point)
    loss_here = (0.5 * jnp.sum(sq) / D_MODEL).reshape(1)

    received = {}
    group_of = {u: i for i, (units, _) in enumerate(sent) for u in units}
    out = {"grad": {}, "delta": {}, "new_m": {}, "new_v": {}}
    newest = [gx]

    def update_big(n):
        shape = w[n].shape
        as3 = lambda a: a.reshape((-1,) + shape[-2:])
        mine = [u for u, (wn, _, _) in UNITS.items() if wn == n]
        for i in sorted({group_of[u] for u in mine}):
            if sent[i][0][0] not in received:
                received.update(zip(sent[i][0], _split_wait(sent[i][1], newest[0], name=f"scatter{i}_wait")))
        flip = (lambda a: jnp.swapaxes(a, 1, 2)) if UNITS[mine[0]][2] else (lambda a: a)
        res = _reduce_adamw([received[u] for u in mine], flip(as3(w[n])), flip(as3(m[n])), flip(as3(v[n])),
                            tr=ADAMW_ROW_TILE[n], name=f"adamw_{n}")
        newest[0] = res[1]
        for kind, r in zip(("grad", "delta", "new_m", "new_v"), res):
            out[kind][n] = flip(r).reshape(shape)

    for n in ("w_down", "w_gate_up", "w_out", "w_mem_kv", "b_w_in", "w_kv"):
        update_big(n)

    full_shapes = [(2, A_WIDTH) if n == "a_lb_logits" else (1, A_WIDTH) if n == "a_onorm" else w[n].shape for n in SMALL_ORDER]
    n_small = sum(_prod(s) for s in full_shapes) + 1
    rows_small = -(-n_small // (8 * LANES)) * 8
    g_all = _all_gather_direct(_pack_flat([gP[n] for n in SMALL_ORDER] + [loss_here], rows_small, LANES, F32),
                               newest[0], name="gather_small_grads")
    summed = _unpack_flat(_sum_sources(g_all, tr=rows_small, name="sum_small_grads"), full_shapes + [(1,)])
    g_small = dict(zip(SMALL_ORDER, summed))
    loss = summed[-1].reshape(())
    me = 4 * lax.axis_index("x") + 2 * lax.axis_index("y") + lax.axis_index("c")
    for n in SMALL_SHARDED:
        g_small[n] = lax.dynamic_slice_in_dim(g_small[n], me * 96, 96, axis=1)
    shapes = [w[n].shape for n in SMALL_ORDER]
    rows_upd = -(-sum(_prod(s) for s in shapes) // (8 * LANES)) * 8
    pk = lambda d: _pack_flat([d[n] for n in SMALL_ORDER], rows_upd, LANES, F32)
    res = _adamw(pk(g_small)[None], pk(w)[None], pk(m)[None], pk(v)[None], tr=rows_upd, name="adamw_small")
    out["grad"].update(g_small)
    for kind, packed in zip(("delta", "new_m", "new_v"), res):
        out[kind].update(zip(SMALL_ORDER, _unpack_flat(packed[0], shapes)))
    newest[0] = res[0]
    update_big("a_w_in")

    return (loss, gx[None], *[out["grad"][n] for n in names], *[out["delta"][n] for n in names],
            *[out["new_m"][n] for n in names], *[out["new_v"][n] for n in names])
```

```python
import functools

import jax
import jax.numpy as jnp
import numpy as np
from jax import lax
from jax.experimental import pallas as pl
from jax.experimental.pallas import tpu as pltpu

F32 = jnp.float32
BF16 = jnp.bfloat16

N_DEV = 8
D_MODEL = 1024
HEAD_DIM = 128
A_HEADS = 6
A_WIDTH = A_HEADS * HEAD_DIM
CHUNK = 64
B_HEADS = 6
B_WIDTH = B_HEADS * HEAD_DIM
DILATIONS = (1, 4, 16)
SPAN = 128
N_GROUPS = 3
ROPE_THETA = 10000.0
MEM_TOKENS = 256
MEM_HEADS = 4
MEM_HEAD_DIM = 64
MEM_WIDTH = MEM_HEADS * MEM_HEAD_DIM
FFN_HIDDEN = 2816
EPS = 1e-6

ADAM_LR = 0.001
ADAM_B1 = 0.9
ADAM_B2 = 0.999
ADAM_EPS = 1e-08
ADAM_WD = 0.01
ADAM_STEP = 10

V7X_VMEM_LIMIT_BYTES = 56 * 1024 * 1024

NT_DIMS = (((1,), (1,)), ((), ()))
TN_DIMS = (((0,), (0,)), ((), ()))


def _cp(*sem):
    return pltpu.CompilerParams(dimension_semantics=sem, vmem_limit_bytes=V7X_VMEM_LIMIT_BYTES)


def _dot(a, b):
    return jnp.dot(a.astype(BF16), b.astype(BF16), preferred_element_type=F32)


def _dot_nt(a, b):
    return lax.dot_general(a.astype(BF16), b.astype(BF16), NT_DIMS, preferred_element_type=F32)


def _dot_tn(a, b):
    return lax.dot_general(a.astype(BF16), b.astype(BF16), TN_DIMS, preferred_element_type=F32)


def _dot3(m01, x):
    hi = x.astype(BF16)
    r1 = x - hi.astype(F32)
    mid = r1.astype(BF16)
    lo = (r1 - mid.astype(F32)).astype(BF16)
    d = functools.partial(jnp.dot, preferred_element_type=F32)
    return d(m01, hi) + d(m01, mid) + d(m01, lo)


def _sigmoid(x):
    return 0.5 * jnp.tanh(0.5 * x) + 0.5


def _full(shape):
    return pl.BlockSpec(shape, lambda *_: (0,) * len(shape))


def _dep(body, n_in, dep):
    if dep is None:
        return body, [], []

    def with_dep(*refs):
        return body(*refs[:n_in], *refs[n_in + 1:])

    return with_dep, [pl.BlockSpec(memory_space=pl.ANY)], [dep]


def _rms_matmul(x, g, w, *, tt, tn, wt, name, out_dtype=F32, dep=None, rotate=None):
    T, K = x.shape
    N = w.shape[0] if wt else w.shape[1]
    n_rot = 0 if rotate is None else rotate[0].shape[1] // HEAD_DIM
    extra_in = [] if rotate is None else list(rotate)

    def kernel_body(x_ref, g_ref, w_ref, *rest):
        y_ref, xn_ref = rest[len(extra_in)], rest[len(extra_in) + 1]
        xf = x_ref[...]
        r = lax.rsqrt(jnp.mean(xf * xf, axis=-1, keepdims=True) + EPS)
        xn = (xf * r * g_ref[...]).astype(BF16)
        xn_ref[...] = xn
        for j in range(N // tn):
            cols = slice(j * tn, (j + 1) * tn)
            y = _dot_nt(xn, w_ref[cols, :]) if wt else _dot(xn, w_ref[:, cols])
            y_ref[:, cols] = y.astype(out_dtype)
            for h in range(j * tn // HEAD_DIM, min((j + 1) * tn // HEAD_DIM, n_rot)):
                gw_ref, c_ref, s_ref, yr_ref = rest[0], rest[1], rest[2], rest[len(extra_in) + 2]
                sl = slice(h * HEAD_DIM, (h + 1) * HEAD_DIM)
                xhat, _ = _head_rms(y[:, h * HEAD_DIM - j * tn:(h + 1) * HEAD_DIM - j * tn])
                yr_ref[:, sl] = _rope(xhat * gw_ref[:, sl], c_ref[...], s_ref[...])

    tbl = pl.BlockSpec((tt, HEAD_DIM), lambda i: (i, 0))
    rot_specs = [] if rotate is None else [_full((1, n_rot * HEAD_DIM)), tbl, tbl]
    body, dep_specs, dep_args = _dep(kernel_body, 3 + len(extra_in), dep)
    return pl.pallas_call(
        body, grid=(T // tt,),
        in_specs=[pl.BlockSpec((tt, K), lambda i: (i, 0)), _full((1, K)), _full(w.shape)] + rot_specs + dep_specs,
        out_specs=[pl.BlockSpec((tt, N), lambda i: (i, 0)), pl.BlockSpec((tt, K), lambda i: (i, 0))]
        + ([] if rotate is None else [pl.BlockSpec((tt, n_rot * HEAD_DIM), lambda i: (i, 0))]),
        out_shape=[jax.ShapeDtypeStruct((T, N), out_dtype), jax.ShapeDtypeStruct((T, K), BF16)]
        + ([] if rotate is None else [jax.ShapeDtypeStruct((T, n_rot * HEAD_DIM), F32)]),
        compiler_params=_cp("parallel"), name=name)(x, g, w, *extra_in, *dep_args)


def _swiglu_down(h, gu, wd, *, tt, name):
    T, D = h.shape
    Fh = wd.shape[0]

    def body(h_ref, gt_ref, up_ref, w_ref, o_ref):
        gt = gt_ref[...].astype(F32)
        act = gt * _sigmoid(gt) * up_ref[...].astype(F32)
        o_ref[...] = h_ref[...] + _dot(act, w_ref[...])

    return pl.pallas_call(
        body, grid=(T // tt,),
        in_specs=[pl.BlockSpec((tt, D), lambda i: (i, 0)), pl.BlockSpec((tt, Fh), lambda i: (i, 0)),
                  pl.BlockSpec((tt, Fh), lambda i: (i, 1)), _full((Fh, D))],
        out_specs=pl.BlockSpec((tt, D), lambda i: (i, 0)),
        out_shape=jax.ShapeDtypeStruct((T, D), F32),
        compiler_params=_cp("parallel"), name=name)(h, gu, gu, wd)


def _swiglu_down_loss(h, gu, wd, tgt, *, tt, name):
    T, D = h.shape
    Fh = wd.shape[0]

    def body(h_ref, gt_ref, up_ref, w_ref, t_ref, dy_ref, acc_ref):
        @pl.when(pl.program_id(0) == 0)
        def _():
            acc_ref[...] = jnp.zeros_like(acc_ref)

        gt = gt_ref[...].astype(F32)
        act = gt * _sigmoid(gt) * up_ref[...].astype(F32)
        e = h_ref[...] + _dot(act, w_ref[...]) - t_ref[...]
        dy_ref[...] = e * (1.0 / D)
        acc_ref[...] += jnp.sum(e * e, axis=0, keepdims=True)

    row = pl.BlockSpec((tt, D), lambda i: (i, 0))
    return pl.pallas_call(
        body, grid=(T // tt,),
        in_specs=[row, pl.BlockSpec((tt, Fh), lambda i: (i, 0)), pl.BlockSpec((tt, Fh), lambda i: (i, 1)), _full((Fh, D)), row],
        out_specs=[row, _full((1, D))],
        out_shape=[jax.ShapeDtypeStruct((T, D), F32), jax.ShapeDtypeStruct((1, D), F32)],
        compiler_params=_cp("arbitrary"), name=name)(h, gu, gu, wd, tgt)


SWIGLU_COLS = 256


def _swiglu_bwd(dh, gu, wd, *, tt, name):
    T, D = dh.shape
    Fh = wd.shape[0]
    last = T // tt - 1

    def body(dh_ref, gt_ref, up_ref, w_ref, dgu_ref, gw_ref, acc):
        @pl.when(pl.program_id(0) == 0)
        def _():
            acc[...] = jnp.zeros_like(acc)

        dh16 = dh_ref[...].astype(BF16)
        for c0 in range(0, Fh, SWIGLU_COLS):
            cols = slice(c0, c0 + SWIGLU_COLS)
            gt = gt_ref[:, cols].astype(F32)
            up = up_ref[:, cols].astype(F32)
            s = _sigmoid(gt)
            silu = gt * s
            dact = _dot_nt(dh16, w_ref[cols, :])
            acc[cols, :] += _dot_tn((silu * up).astype(BF16), dh16)
            dgu_ref[:, cols] = (dact * up * (s * (1.0 + gt * (1.0 - s)))).astype(BF16)
            dgu_ref[:, Fh + c0:Fh + c0 + SWIGLU_COLS] = (dact * silu).astype(BF16)

        @pl.when(pl.program_id(0) == last)
        def _():
            gw_ref[...] = acc[...].astype(BF16)

    return pl.pallas_call(
        body, grid=(T // tt,),
        in_specs=[pl.BlockSpec((tt, D), lambda i: (i, 0)), pl.BlockSpec((tt, Fh), lambda i: (i, 0)),
                  pl.BlockSpec((tt, Fh), lambda i: (i, 1)), _full((Fh, D))],
        out_specs=[pl.BlockSpec((tt, 2 * Fh), lambda i: (i, 0)), _full((Fh, D))],
        out_shape=[jax.ShapeDtypeStruct((T, 2 * Fh), BF16), jax.ShapeDtypeStruct((Fh, D), BF16)],
        scratch_shapes=[pltpu.VMEM((Fh, D), F32)],
        compiler_params=_cp("arbitrary"), name=name)(dh, gu, gu, wd)


def _out_proj_bwd(dy, a1, a2, w, *, tt, name, head_dots=False):
    T, N = dy.shape
    K1, K2 = a1.shape[1], a2.shape[1]
    K = K1 + K2
    last = T // tt - 1

    def body(dy_ref, a1_ref, a2_ref, w_ref, da_ref, gw_ref, *rest):
        acc = rest[-1]

        @pl.when(pl.program_id(0) == 0)
        def _():
            acc[...] = jnp.zeros_like(acc)

        dy16 = dy_ref[...].astype(BF16)
        da = _dot_nt(dy16, w_ref[...])
        da_ref[...] = da
        acc[:K1, :] += _dot_tn(a1_ref[...], dy16)
        acc[K1:, :] += _dot_tn(a2_ref[...], dy16)
        if head_dots:
            for h in range(K1 // HEAD_DIM):
                sl = slice(h * HEAD_DIM, (h + 1) * HEAD_DIM)
                rest[0][:, sl] = jnp.broadcast_to(jnp.sum(da[:, sl] * a1_ref[:, sl], axis=-1, keepdims=True), (tt, HEAD_DIM))

        @pl.when(pl.program_id(0) == last)
        def _():
            gw_ref[...] = acc[...].astype(BF16)

    extra_specs = [pl.BlockSpec((tt, K1), lambda i: (i, 0))] if head_dots else []
    extra_shapes = [jax.ShapeDtypeStruct((T, K1), F32)] if head_dots else []
    return pl.pallas_call(
        body, grid=(T // tt,),
        in_specs=[pl.BlockSpec((tt, N), lambda i: (i, 0)), pl.BlockSpec((tt, K1), lambda i: (i, 0)),
                  pl.BlockSpec((tt, K2), lambda i: (i, 0)), _full((K, N))],
        out_specs=[pl.BlockSpec((tt, K), lambda i: (i, 0)), _full((K, N))] + extra_specs,
        out_shape=[jax.ShapeDtypeStruct((T, K), F32), jax.ShapeDtypeStruct((K, N), BF16)] + extra_shapes,
        scratch_shapes=[pltpu.VMEM((K, N), F32)],
        compiler_params=_cp("arbitrary"), name=name)(dy, a1, a2, w)


def _mm_tn(a, b, *, tt, tka, name):
    T, Ka = a.shape
    N = b.shape[1]
    last = T // tt - 1

    def body(a_ref, b_ref, o_ref, acc):
        @pl.when(pl.program_id(1) == 0)
        def _():
            acc[...] = jnp.zeros_like(acc)

        acc[...] += _dot_tn(a_ref[...], b_ref[...])

        @pl.when(pl.program_id(1) == last)
        def _():
            o_ref[...] = acc[...].astype(BF16)

    return pl.pallas_call(
        body, grid=(Ka // tka, T // tt),
        in_specs=[pl.BlockSpec((tt, tka), lambda j, t: (t, j)), pl.BlockSpec((tt, N), lambda j, t: (t, 0))],
        out_specs=pl.BlockSpec((tka, N), lambda j, t: (j, 0)),
        out_shape=jax.ShapeDtypeStruct((Ka, N), BF16),
        scratch_shapes=[pltpu.VMEM((tka, N), F32)],
        compiler_params=_cp("parallel", "arbitrary"), name=name)(a, b)


def _mm_tn_pieces(pieces, b, *, tt, name):
    n = len(pieces)
    T = b.shape[0]
    N = b.shape[1]
    widths = [p.shape[1] for p in pieces]
    Ka = sum(widths)
    last = T // tt - 1

    def body(*refs):
        p_refs = refs[:n]
        b_ref, o_ref, acc = refs[n:]

        @pl.when(pl.program_id(0) == 0)
        def _():
            acc[...] = jnp.zeros_like(acc)

        bv = b_ref[...].astype(BF16)
        off = 0
        for p_ref, wd in zip(p_refs, widths):
            acc[off:off + wd, :] += _dot_tn(p_ref[...], bv)
            off += wd

        @pl.when(pl.program_id(0) == last)
        def _():
            o_ref[...] = acc[...].astype(BF16)

    return pl.pallas_call(
        body, grid=(T // tt,),
        in_specs=[pl.BlockSpec((tt, wd), lambda t: (t, 0)) for wd in widths] + [pl.BlockSpec((tt, N), lambda t: (t, 0))],
        out_specs=_full((Ka, N)), out_shape=jax.ShapeDtypeStruct((Ka, N), BF16),
        scratch_shapes=[pltpu.VMEM((Ka, N), F32)],
        compiler_params=_cp("arbitrary"), name=name)(*pieces, b)


def _rms_bwd_dx(x, g, w, dy, dres, *, tt, wt, name, dep=None):
    pieces = list(dy) if isinstance(dy, (list, tuple)) else [dy]
    n = len(pieces)
    widths = [p.shape[1] for p in pieces]
    T, K = x.shape

    def kernel_body(x_ref, g_ref, w_ref, *rest):
        dy_refs = rest[:n]
        dres_ref, dx_ref, dg_ref = rest[n:]

        @pl.when(pl.program_id(0) == 0)
        def _():
            dg_ref[...] = jnp.zeros_like(dg_ref)

        if n == 1:
            dxn = (_dot if wt else _dot_nt)(dy_refs[0][...], w_ref[...])
        else:
            dxn, off = 0.0, 0
            for dy_ref, wd in zip(dy_refs, widths):
                dxn = dxn + _dot(dy_ref[...], w_ref[off:off + wd, :])
                off += wd
        xf = x_ref[...]
        r = lax.rsqrt(jnp.mean(xf * xf, axis=-1, keepdims=True) + EPS)
        xhat = xf * r
        dg_ref[...] += jnp.sum(dxn * xhat, axis=0, keepdims=True)
        dxhat = dxn * g_ref[...]
        dx_ref[...] = dres_ref[...] + r * (dxhat - xhat * jnp.mean(dxhat * xhat, axis=-1, keepdims=True))

    assert n == 1 or wt
    body, dep_specs, dep_args = _dep(kernel_body, 4 + n, dep)
    return pl.pallas_call(
        body, grid=(T // tt,),
        in_specs=[pl.BlockSpec((tt, K), lambda i: (i, 0)), _full((1, K)), _full(w.shape)]
        + [pl.BlockSpec((tt, wd), lambda i: (i, 0)) for wd in widths]
        + [pl.BlockSpec((tt, K), lambda i: (i, 0))] + dep_specs,
        out_specs=[pl.BlockSpec((tt, K), lambda i: (i, 0)), _full((1, K))],
        out_shape=[jax.ShapeDtypeStruct((T, K), F32), jax.ShapeDtypeStruct((1, K), F32)],
        compiler_params=_cp("arbitrary"), name=name)(x, g, w, *pieces, dres, *dep_args)


HGRN_TB = 512
HGRN_NCH = HGRN_TB // CHUNK
HGRN_UNROLL = 8
HGRN_HPB = 6


def _hgrn_chunk_fwd(q, z, lbv, tril01):
    sig = _sigmoid(z)
    f = lbv + (1.0 - lbv) * sig
    kk = 1.0 - f
    b = _dot3(tril01, jnp.log(f))
    bend = b[CHUNK - 1:CHUNK, :]
    sq = _sigmoid(q)
    eb = jnp.exp(b)
    emb = jnp.exp(-b)
    eo = jnp.exp(bend - b)
    dec = jnp.exp(bend)
    return sig, f, kk, sq, eb, emb, eo, dec


def _hgrn2_fwd(proj, lb, *, name):
    T = proj.shape[0]
    nT = T // HGRN_TB
    nC = T // CHUNK

    def body(q_ref, z_ref, v_ref, lb_ref, o_ref, st_ref, state):
        @pl.when(pl.program_id(1) == 0)
        def _():
            state[...] = jnp.zeros_like(state)

        row = lax.broadcasted_iota(jnp.int32, (CHUNK, CHUNK), 0)
        col = lax.broadcasted_iota(jnp.int32, (CHUNK, CHUNK), 1)
        causal = row >= col
        tril01 = causal.astype(BF16)

        def chunk(c, carry):
            rows = pl.ds(pl.multiple_of(c * CHUNK, CHUNK), CHUNK)
            for hh in range(HGRN_HPB):
                sl = slice(hh * HEAD_DIM, (hh + 1) * HEAD_DIM)
                q = q_ref[rows, sl]
                v = v_ref[rows, sl].astype(BF16)
                sig, f, kk, sq, eb, emb, eo, dec = _hgrn_chunk_fwd(q, z_ref[rows, sl], lb_ref[:, sl], tril01)
                qi = (q * sq * eb).astype(BF16)
                ki = (kk * emb).astype(BF16)
                ko = (kk * eo).astype(BF16)
                st = state[hh]
                att = jnp.where(causal, _dot_nt(qi, ki), 0.0)
                o_ref[rows, sl] = _dot(att, v) + _dot_nt(qi, st)
                st_ref[c, hh] = st
                state[hh] = st * dec + _dot_tn(v, ko)
            return carry

        lax.fori_loop(0, HGRN_NCH, chunk, 0, unroll=HGRN_UNROLL)

    W = HGRN_HPB * HEAD_DIM
    nG = A_HEADS // HGRN_HPB
    hb = lambda off: pl.BlockSpec((HGRN_TB, W), lambda h, i: (i, off + h))
    return pl.pallas_call(
        body, grid=(nG, nT),
        in_specs=[hb(0), hb(nG), hb(2 * nG), pl.BlockSpec((1, W), lambda h, i: (0, h))],
        out_specs=[hb(0), pl.BlockSpec((HGRN_NCH, HGRN_HPB, HEAD_DIM, HEAD_DIM), lambda h, i: (i, h, 0, 0))],
        out_shape=[jax.ShapeDtypeStruct((T, A_WIDTH), F32), jax.ShapeDtypeStruct((nC, A_HEADS, HEAD_DIM, HEAD_DIM), F32)],
        scratch_shapes=[pltpu.VMEM((HGRN_HPB, HEAD_DIM, HEAD_DIM), F32)],
        compiler_params=_cp("parallel", "arbitrary"), name=name)(proj, proj, proj, lb)


def _hgrn2_bwd(proj, lb, st_all, do, *, name):
    T = proj.shape[0]
    nT = T // HGRN_TB

    def body(q_ref, z_ref, v_ref, lb_ref, st_ref, do_ref, dq_ref, dz_ref, dv_ref, dlb_ref, dstate):
        @pl.when(pl.program_id(1) == 0)
        def _():
            dstate[...] = jnp.zeros_like(dstate)
            dlb_ref[...] = jnp.zeros_like(dlb_ref)

        row = lax.broadcasted_iota(jnp.int32, (CHUNK, CHUNK), 0)
        col = lax.broadcasted_iota(jnp.int32, (CHUNK, CHUNK), 1)
        causal = row >= col
        tril01 = causal.astype(BF16)
        triu01 = (row <= col).astype(BF16)

        def chunk(cc, carry):
            c = HGRN_NCH - 1 - cc
            rows = pl.ds(pl.multiple_of(c * CHUNK, CHUNK), CHUNK)
            for hh in range(HGRN_HPB):
                sl = slice(hh * HEAD_DIM, (hh + 1) * HEAD_DIM)
                lbv = lb_ref[:, sl]
                q = q_ref[rows, sl]
                v = v_ref[rows, sl].astype(BF16)
                sig, f, kk, sq, eb, emb, eo, dec = _hgrn_chunk_fwd(q, z_ref[rows, sl], lbv, tril01)
                qi32 = q * sq * eb
                ki32 = kk * emb
                ko32 = kk * eo
                qi, ki, ko = qi32.astype(BF16), ki32.astype(BF16), ko32.astype(BF16)
                att = jnp.where(causal, _dot_nt(qi, ki), 0.0).astype(BF16)
                dout = do_ref[rows, sl].astype(BF16)
                st = st_ref[c, hh]
                dst = dstate[hh]
                dst16 = dst.astype(BF16)
                datt = jnp.where(causal, _dot_nt(dout, v), 0.0).astype(BF16)
                dqi = _dot(datt, ki) + _dot(dout, st)
                dki = _dot_tn(datt, qi)
                dv_ref[rows, sl] = (_dot_tn(att, dout) + _dot_nt(ko, dst16)).astype(BF16)
                dko = _dot(v, dst16)
                ddec = jnp.sum(dst * st, axis=0, keepdims=True)
                dstate[hh] = dst * dec + _dot_tn(dout, qi)
                dkk = dki * emb + dko * eo
                db = dqi * qi32 - dki * ki32 - dko * ko32
                dbend = jnp.sum(dko * ko32, axis=0, keepdims=True) + ddec * dec
                dlogf = _dot3(triu01, db) + dbend
                df = dlogf / f - dkk
                dz_ref[rows, sl] = (df * (1.0 - lbv) * sig * (1.0 - sig)).astype(BF16)
                dlb_ref[:, sl] += jnp.sum(df * (1.0 - sig), axis=0, keepdims=True)
                dq_ref[rows, sl] = (dqi * eb * (sq * (1.0 + q * (1.0 - sq)))).astype(BF16)
            return carry

        lax.fori_loop(0, HGRN_NCH, chunk, 0, unroll=HGRN_UNROLL)

    W = HGRN_HPB * HEAD_DIM
    nG = A_HEADS // HGRN_HPB
    hb = lambda off: pl.BlockSpec((HGRN_TB, W), lambda h, i: (nT - 1 - i, off + h))
    hlb = pl.BlockSpec((1, W), lambda h, i: (0, h))
    o16 = jax.ShapeDtypeStruct((T, A_WIDTH), BF16)
    return pl.pallas_call(
        body, grid=(nG, nT),
        in_specs=[hb(0), hb(nG), hb(2 * nG), hlb,
                  pl.BlockSpec((HGRN_NCH, HGRN_HPB, HEAD_DIM, HEAD_DIM), lambda h, i: (nT - 1 - i, h, 0, 0)), hb(0)],
        out_specs=[hb(0), hb(0), hb(0), hlb],
        out_shape=[o16, o16, o16, jax.ShapeDtypeStruct((1, A_WIDTH), F32)],
        scratch_shapes=[pltpu.VMEM((HGRN_HPB, HEAD_DIM, HEAD_DIM), F32)],
        compiler_params=_cp("parallel", "arbitrary"), name=name)(proj, proj, proj, lb, st_all, do)


def _head_rms(x):
    r = lax.rsqrt(jnp.mean(x * x, axis=-1, keepdims=True) + EPS)
    return x * r, r


def _head_rms_bwd(dxhat, xhat, r):
    return r * (dxhat - xhat * jnp.mean(dxhat * xhat, axis=-1, keepdims=True))


def _a_post_out_proj(res, o, proj, onorm, mem_out, w, *, tt, name):
    T, N = res.shape
    K2 = mem_out.shape[1]

    def body(r_ref, o_ref, g_ref, gain_ref, a2_ref, w_ref, h_ref, y_ref):
        for h in range(A_HEADS):
            sl = slice(h * HEAD_DIM, (h + 1) * HEAD_DIM)
            xhat, _ = _head_rms(o_ref[:, sl])
            g = g_ref[:, sl]
            y_ref[:, sl] = xhat * gain_ref[:, sl] * (g * _sigmoid(g))
        h_ref[...] = r_ref[...] + _dot(y_ref[...], w_ref[:A_WIDTH, :]) + _dot(a2_ref[...], w_ref[A_WIDTH:, :])

    blk = lambda c: pl.BlockSpec((tt, A_WIDTH), lambda i: (i, c))
    row = pl.BlockSpec((tt, N), lambda i: (i, 0))
    return pl.pallas_call(
        body, grid=(T // tt,),
        in_specs=[row, blk(0), blk(3), _full((1, A_WIDTH)), pl.BlockSpec((tt, K2), lambda i: (i, 0)), _full((A_WIDTH + K2, N))],
        out_specs=[row, blk(0)],
        out_shape=[jax.ShapeDtypeStruct((T, N), F32), jax.ShapeDtypeStruct((T, A_WIDTH), F32)],
        compiler_params=_cp("parallel"), name=name)(res, o, proj, onorm, mem_out, w)


def _a_post_bwd(o, proj, onorm, dmix, *, tt, name, dep=None):
    T = o.shape[0]

    def kernel_body(o_ref, g_ref, w_ref, dy_ref, do_ref, dg_ref, dw_ref):
        @pl.when(pl.program_id(0) == 0)
        def _():
            dw_ref[...] = jnp.zeros_like(dw_ref)

        for h in range(A_HEADS):
            sl = slice(h * HEAD_DIM, (h + 1) * HEAD_DIM)
            xhat, r = _head_rms(o_ref[:, sl])
            g = g_ref[:, sl]
            s = _sigmoid(g)
            dy = dy_ref[:, sl]
            w = w_ref[:, sl]
            dg_ref[:, sl] = (dy * xhat * w * (s * (1.0 + g * (1.0 - s)))).astype(BF16)
            dyn = dy * (g * s)
            dw_ref[:, sl] += jnp.sum(dyn * xhat, axis=0, keepdims=True)
            do_ref[:, sl] = _head_rms_bwd(dyn * w, xhat, r)

    blk = lambda c: pl.BlockSpec((tt, A_WIDTH), lambda i: (i, c))
    body, dep_specs, dep_args = _dep(kernel_body, 4, dep)
    return pl.pallas_call(
        body, grid=(T // tt,), in_specs=[blk(0), blk(3), _full((1, A_WIDTH)), blk(0)] + dep_specs,
        out_specs=[blk(0), blk(0), _full((1, A_WIDTH))],
        out_shape=[jax.ShapeDtypeStruct((T, A_WIDTH), F32), jax.ShapeDtypeStruct((T, A_WIDTH), BF16),
                   jax.ShapeDtypeStruct((1, A_WIDTH), F32)],
        compiler_params=_cp("arbitrary"), name=name)(o, proj, onorm, dmix, *dep_args)


def _mem_head_masks(n):
    lane = lax.broadcasted_iota(jnp.int32, (n, MEM_WIDTH), 1)
    return [(lane >= m * MEM_HEAD_DIM) & (lane < (m + 1) * MEM_HEAD_DIM) for m in range(MEM_HEADS)]


def _mem_head_rms(x, masks):
    x2 = x * x
    r = jnp.zeros_like(x)
    for mk in masks:
        ms = jnp.sum(jnp.where(mk, x2, 0.0), axis=-1, keepdims=True) * (1.0 / MEM_HEAD_DIM)
        r = jnp.where(mk, lax.rsqrt(ms + EPS), r)
    return x * r, r


def _mem_head_rms_bwd(dxhat, xhat, r, masks):
    t = dxhat * xhat
    m = jnp.zeros_like(t)
    for mk in masks:
        m = jnp.where(mk, jnp.sum(jnp.where(mk, t, 0.0), axis=-1, keepdims=True) * (1.0 / MEM_HEAD_DIM), m)
    return r * (dxhat - xhat * m)


MEM_SCALE = MEM_HEAD_DIM ** -0.5


def _mem_attn_fwd(proj, qcol, mkv, qn_w, kn_w, *, tt, name):
    T = proj.shape[0]

    def body(q_ref, k_ref, v_ref, qw_ref, kw_ref, o_ref):
        qmasks = _mem_head_masks(tt)
        kmasks = _mem_head_masks(MEM_TOKENS)
        qhat, _ = _mem_head_rms(q_ref[...], qmasks)
        qn = qhat * qw_ref[...]
        khat, _ = _mem_head_rms(k_ref[...], kmasks)
        kn = (khat * kw_ref[...]).astype(BF16)
        v = v_ref[...].astype(BF16)
        out = jnp.zeros((tt, MEM_WIDTH), F32)
        for m in range(MEM_HEADS):
            s = _dot_nt(jnp.where(qmasks[m], qn, 0.0), kn) * MEM_SCALE
            s = s - jnp.max(s, axis=-1, keepdims=True)
            p = jnp.exp(s)
            p = p / jnp.sum(p, axis=-1, keepdims=True)
            out = jnp.where(qmasks[m], _dot(p, v), out)
        o_ref[...] = out

    return pl.pallas_call(
        body, grid=(T // tt,),
        in_specs=[pl.BlockSpec((tt, MEM_WIDTH), lambda i: (i, qcol)), pl.BlockSpec((MEM_TOKENS, MEM_WIDTH), lambda i: (0, 0)),
                  pl.BlockSpec((MEM_TOKENS, MEM_WIDTH), lambda i: (0, 1)), _full((1, MEM_WIDTH)), _full((1, MEM_WIDTH))],
        out_specs=pl.BlockSpec((tt, MEM_WIDTH), lambda i: (i, 0)),
        out_shape=jax.ShapeDtypeStruct((T, MEM_WIDTH), F32),
        compiler_params=_cp("parallel"), name=name)(proj, mkv, mkv, qn_w, kn_w)


def _mem_attn_bwd(proj, qcol, mkv, qn_w, kn_w, dmix, *, tt, name):
    T = proj.shape[0]
    nsteps = T // tt
    ocol = (dmix.shape[1] - MEM_WIDTH) // MEM_WIDTH

    def body(q_ref, k_ref, v_ref, qw_ref, kw_ref, do_ref, dq_ref, dkv_ref, dqw_ref, dkw_ref, dk_acc, dv_acc):
        step = pl.program_id(0)

        @pl.when(step == 0)
        def _():
            dk_acc[...] = jnp.zeros_like(dk_acc)
            dv_acc[...] = jnp.zeros_like(dv_acc)
            dqw_ref[...] = jnp.zeros_like(dqw_ref)

        qmasks = _mem_head_masks(tt)
        kmasks = _mem_head_masks(MEM_TOKENS)
        qhat, qr = _mem_head_rms(q_ref[...], qmasks)
        qn = qhat * qw_ref[...]
        khat, kr = _mem_head_rms(k_ref[...], kmasks)
        kn = (khat * kw_ref[...]).astype(BF16)
        v = v_ref[...].astype(BF16)
        dout = do_ref[...]
        dqn = jnp.zeros((tt, MEM_WIDTH), F32)
        dkn = jnp.zeros((MEM_TOKENS, MEM_WIDTH), F32)
        dvv = jnp.zeros((MEM_TOKENS, MEM_WIDTH), F32)
        for m in range(MEM_HEADS):
            qm = jnp.where(qmasks[m], qn, 0.0).astype(BF16)
            s = _dot_nt(qm, kn) * MEM_SCALE
            s = s - jnp.max(s, axis=-1, keepdims=True)
            p = jnp.exp(s)
            p = p / jnp.sum(p, axis=-1, keepdims=True)
            dom = jnp.where(qmasks[m], dout, 0.0).astype(BF16)
            dp = _dot_nt(dom, v)
            ds = (p * (dp - jnp.sum(p * dp, axis=-1, keepdims=True)) * MEM_SCALE).astype(BF16)
            dqn = jnp.where(qmasks[m], _dot(ds, kn), dqn)
            dkn = jnp.where(kmasks[m], _dot_tn(ds, qm), dkn)
            dvv = jnp.where(kmasks[m], _dot_tn(p, dom), dvv)
        dqw_ref[...] += jnp.sum(dqn * qhat, axis=0, keepdims=True)
        dq_ref[...] = _mem_head_rms_bwd(dqn * qw_ref[...], qhat, qr, qmasks).astype(BF16)
        dk_acc[...] += dkn
        dv_acc[...] += dvv

        @pl.when(step == nsteps - 1)
        def _():
            dk = dk_acc[...]
            dkw_ref[...] = jnp.sum(dk * khat, axis=0, keepdims=True)
            dkv_ref[:, :MEM_WIDTH] = _mem_head_rms_bwd(dk * kw_ref[...], khat, kr, kmasks)
            dkv_ref[:, MEM_WIDTH:] = dv_acc[...]

    return pl.pallas_call(
        body, grid=(nsteps,),
        in_specs=[pl.BlockSpec((tt, MEM_WIDTH), lambda i: (i, qcol)), pl.BlockSpec((MEM_TOKENS, MEM_WIDTH), lambda i: (0, 0)),
                  pl.BlockSpec((MEM_TOKENS, MEM_WIDTH), lambda i: (0, 1)), _full((1, MEM_WIDTH)), _full((1, MEM_WIDTH)),
                  pl.BlockSpec((tt, MEM_WIDTH), lambda i: (i, ocol))],
        out_specs=[pl.BlockSpec((tt, MEM_WIDTH), lambda i: (i, 0)), _full((MEM_TOKENS, 2 * MEM_WIDTH)),
                   _full((1, MEM_WIDTH)), _full((1, MEM_WIDTH))],
        out_shape=[jax.ShapeDtypeStruct((T, MEM_WIDTH), BF16), jax.ShapeDtypeStruct((MEM_TOKENS, 2 * MEM_WIDTH), F32),
                   jax.ShapeDtypeStruct((1, MEM_WIDTH), F32), jax.ShapeDtypeStruct((1, MEM_WIDTH), F32)],
        scratch_shapes=[pltpu.VMEM((MEM_TOKENS, MEM_WIDTH), F32), pltpu.VMEM((MEM_TOKENS, MEM_WIDTH), F32)],
        compiler_params=_cp("arbitrary"), name=name)(proj, mkv, mkv, qn_w, kn_w, dmix)


HALF = HEAD_DIM // 2
ATT_SCALE = HEAD_DIM ** -0.5
NEG = -1e30


def _rope_tables(T):
    inv = np.float32(ROPE_THETA) ** (-np.arange(HALF, dtype=np.float32) / np.float32(HALF))
    ang = np.arange(T, dtype=np.float32)[:, None] * inv[None, :].astype(np.float32)
    cos, sin = np.cos(ang).astype(np.float32), np.sin(ang).astype(np.float32)
    return jnp.asarray(np.concatenate([cos, cos], axis=-1)), jnp.asarray(np.concatenate([-sin, sin], axis=-1))


def _rope(x, cosf, sinsg):
    return x * cosf + pltpu.roll(x, HALF, 1) * sinsg


def _rope_bwd(dy, cosf, sinsg):
    return dy * cosf + pltpu.roll(dy * sinsg, HALF, 1)


def _q_prep_bwd(proj, w_heads, cosf, sinsg, dqs, *, tt, name):
    T = proj.shape[0]
    W = N_GROUPS * B_WIDTH

    def body(x_ref, w_ref, c_ref, s_ref, d0, d1, d2, dx_ref, dw_ref):
        @pl.when(pl.program_id(0) == 0)
        def _():
            dw_ref[...] = jnp.zeros_like(dw_ref)

        c, s = c_ref[...], s_ref[...]
        for gi, d_ref in enumerate((d0, d1, d2)):
            for h in range(B_HEADS):
                sl = slice((gi * B_HEADS + h) * HEAD_DIM, (gi * B_HEADS + h + 1) * HEAD_DIM)
                xhat, r = _head_rms(x_ref[:, sl])
                dyn = _rope_bwd(d_ref[:, h * HEAD_DIM:(h + 1) * HEAD_DIM], c, s)
                dw_ref[:, sl] += jnp.sum(dyn * xhat, axis=0, keepdims=True)
                dx_ref[:, sl] = _head_rms_bwd(dyn * w_ref[:, sl], xhat, r).astype(BF16)

    tbl = pl.BlockSpec((tt, HEAD_DIM), lambda i: (i, 0))
    dyb = pl.BlockSpec((tt, B_WIDTH), lambda i: (i, 0))
    return pl.pallas_call(
        body, grid=(T // tt,),
        in_specs=[pl.BlockSpec((tt, W), lambda i: (i, 0)), _full((1, W)), tbl, tbl, dyb, dyb, dyb],
        out_specs=[pl.BlockSpec((tt, W), lambda i: (i, 0)), _full((1, W))],
        out_shape=[jax.ShapeDtypeStruct((T, W), BF16), jax.ShapeDtypeStruct((1, W), F32)],
        compiler_params=_cp("arbitrary"), name=name)(proj, w_heads, cosf, sinsg, *dqs)


def _kv_prep_bwd(kv, w_heads, cosf, sinsg, dks, dvs, *, tt, name):
    T = kv.shape[0]

    def body(x_ref, w_ref, c_ref, s_ref, k0, k1, k2, v0, v1, v2, dx_ref, dw_ref):
        @pl.when(pl.program_id(0) == 0)
        def _():
            dw_ref[...] = jnp.zeros_like(dw_ref)

        c, s = c_ref[...], s_ref[...]
        for h in range(B_HEADS):
            sl = slice(h * HEAD_DIM, (h + 1) * HEAD_DIM)
            vs = slice(B_WIDTH + h * HEAD_DIM, B_WIDTH + (h + 1) * HEAD_DIM)
            xhat, r = _head_rms(x_ref[:, sl])
            dyn = _rope_bwd(k0[:, sl] + k1[:, sl] + k2[:, sl], c, s)
            dw_ref[:, sl] += jnp.sum(dyn * xhat, axis=0, keepdims=True)
            dx_ref[:, sl] = _head_rms_bwd(dyn * w_ref[:, sl], xhat, r).astype(BF16)
            dx_ref[:, vs] = (v0[:, sl] + v1[:, sl] + v2[:, sl]).astype(BF16)

    tbl = pl.BlockSpec((tt, HEAD_DIM), lambda i: (i, 0))
    dyb = pl.BlockSpec((tt, B_WIDTH), lambda i: (i, 0))
    return pl.pallas_call(
        body, grid=(T // tt,),
        in_specs=[dyb, _full((1, B_WIDTH)), tbl, tbl] + [dyb] * 6,
        out_specs=[pl.BlockSpec((tt, 2 * B_WIDTH), lambda i: (i, 0)), _full((1, B_WIDTH))],
        out_shape=[jax.ShapeDtypeStruct((T, 2 * B_WIDTH), BF16), jax.ShapeDtypeStruct((1, B_WIDTH), F32)],
        compiler_params=_cp("arbitrary"), name=name)(kv, w_heads, cosf, sinsg, *dks, *dvs)


def _band_masks(n_is_first=None):
    row = lax.broadcasted_iota(jnp.int32, (SPAN, SPAN), 0)
    col = lax.broadcasted_iota(jnp.int32, (SPAN, SPAN), 1)
    return row >= col, col >= row


def _dil_views(T, d):
    L = T // d
    return L, L // SPAN


def _dil_fwd(qr, kr, kv, gi, d, *, name):
    T = qr.shape[0]
    L, nb = _dil_views(T, d)

    def body(q_ref, kc_ref, kp_ref, vc_ref, vp_ref, o_ref, lse_ref):
        cur_ok, prev_band = _band_masks()
        prev_ok = prev_band & (pl.program_id(1) > 0)
        for h in range(B_HEADS):
            sl = slice(h * HEAD_DIM, (h + 1) * HEAD_DIM)
            q = q_ref[:, sl]
            sc = jnp.where(cur_ok, _dot_nt(q, kc_ref[:, sl]) * ATT_SCALE, NEG)
            sp = jnp.where(prev_ok, _dot_nt(q, kp_ref[:, sl]) * ATT_SCALE, NEG)
            m = jnp.maximum(jnp.max(sc, axis=-1, keepdims=True), jnp.max(sp, axis=-1, keepdims=True))
            pc = jnp.exp(sc - m)
            pp = jnp.exp(sp - m)
            l = jnp.sum(pc, axis=-1, keepdims=True) + jnp.sum(pp, axis=-1, keepdims=True)
            o_ref[:, sl] = (_dot(pc, vc_ref[:, sl]) + _dot(pp, vp_ref[:, sl])) / l
            lse_ref[:, sl] = jnp.broadcast_to(m + jnp.log(l), (SPAN, HEAD_DIM))

    blk = lambda f: pl.BlockSpec((SPAN, B_WIDTH), f)
    cur = lambda r, n: (n, r)
    prev = lambda r, n: (jnp.maximum(n - 1, 0), r)
    ov = jax.ShapeDtypeStruct((L, d * B_WIDTH), F32)
    o, lse = pl.pallas_call(
        body, grid=(d, nb),
        in_specs=[blk(lambda r, n: (n, r * N_GROUPS + gi)), blk(cur), blk(prev),
                  blk(lambda r, n: (n, 2 * r + 1)), blk(lambda r, n: (jnp.maximum(n - 1, 0), 2 * r + 1))],
        out_specs=[blk(cur), blk(cur)], out_shape=[ov, ov],
        compiler_params=_cp("parallel", "arbitrary"), name=name,
    )(qr.reshape(L, d * N_GROUPS * B_WIDTH), kr.reshape(L, d * B_WIDTH), kr.reshape(L, d * B_WIDTH),
      kv.reshape(L, d * 2 * B_WIDTH), kv.reshape(L, d * 2 * B_WIDTH))
    return o.reshape(T, B_WIDTH), lse.reshape(T, B_WIDTH)


def _combine_out_proj(res, os_, lses, mem_out, w, *, tt, name):
    T, N = res.shape
    K2 = mem_out.shape[1]

    def body(r_ref, o0, o1, o2, l0, l1, l2, a2_ref, w_ref, h_ref, y_ref, lse_ref):
        a, b, c = l0[...], l1[...], l2[...]
        m = jnp.maximum(jnp.maximum(a, b), c)
        wa, wb, wc = jnp.exp(a - m), jnp.exp(b - m), jnp.exp(c - m)
        den = wa + wb + wc
        y = (wa * o0[...] + wb * o1[...] + wc * o2[...]) / den
        y_ref[...] = y
        lse_ref[...] = m + jnp.log(den)
        h_ref[...] = r_ref[...] + _dot(y, w_ref[:B_WIDTH, :]) + _dot(a2_ref[...], w_ref[B_WIDTH:, :])

    blk = pl.BlockSpec((tt, B_WIDTH), lambda i: (i, 0))
    row = pl.BlockSpec((tt, N), lambda i: (i, 0))
    sh = jax.ShapeDtypeStruct((T, B_WIDTH), F32)
    return pl.pallas_call(
        body, grid=(T // tt,),
        in_specs=[row] + [blk] * 6 + [pl.BlockSpec((tt, K2), lambda i: (i, 0)), _full((B_WIDTH + K2, N))],
        out_specs=[row, blk, blk], out_shape=[jax.ShapeDtypeStruct((T, N), F32), sh, sh],
        compiler_params=_cp("parallel"), name=name)(res, *os_, *lses, mem_out, w)


DILS_UNROLL = 8


def _dils_specs(gi, d, nblk):
    blk = lambda f: pl.BlockSpec((SPAN * d, HEAD_DIM), f)
    return {
        "q": blk(lambda h, n: (n, gi * B_HEADS + h)), "q_next": blk(lambda h, n: (jnp.minimum(n + 1, nblk - 1), gi * B_HEADS + h)),
        "cur": blk(lambda h, n: (n, h)), "prev": blk(lambda h, n: (jnp.maximum(n - 1, 0), h)),
        "next": blk(lambda h, n: (jnp.minimum(n + 1, nblk - 1), h)),
        "v": blk(lambda h, n: (n, B_HEADS + h)), "v_prev": blk(lambda h, n: (jnp.maximum(n - 1, 0), B_HEADS + h)),
    }


def _dils_fwd(qr, kr, kv, gi, d, *, name):
    T = qr.shape[0]
    nblk = T // (SPAN * d)
    sp = _dils_specs(gi, d, nblk)

    def body(q_ref, kc_ref, vc_ref, o_ref, lse_ref, k_before, v_before):
        @pl.when(pl.program_id(1) == 0)
        def _():
            k_before[...] = jnp.zeros_like(k_before)
            v_before[...] = jnp.zeros_like(v_before)

        cur_ok, prev_band = _band_masks()
        prev_ok = prev_band & (pl.program_id(1) > 0)

        def residue(r, carry):
            rows = pl.ds(r, SPAN, stride=d)
            q, kc, vc = q_ref[rows, :], kc_ref[rows, :].astype(BF16), vc_ref[rows, :].astype(BF16)
            sc = jnp.where(cur_ok, _dot_nt(q, kc) * ATT_SCALE, NEG)
            sp_ = jnp.where(prev_ok, _dot_nt(q, k_before[r]) * ATT_SCALE, NEG)
            m = jnp.maximum(jnp.max(sc, axis=-1, keepdims=True), jnp.max(sp_, axis=-1, keepdims=True))
            pc = jnp.exp(sc - m)
            pp = jnp.exp(sp_ - m)
            l = jnp.sum(pc, axis=-1, keepdims=True) + jnp.sum(pp, axis=-1, keepdims=True)
            o_ref[rows, :] = (_dot(pc, vc) + _dot(pp, v_before[r])) / l
            lse_ref[rows, :] = jnp.broadcast_to(m + jnp.log(l), (SPAN, HEAD_DIM))
            k_before[r] = kc
            v_before[r] = vc
            return carry

        lax.fori_loop(0, d, residue, 0, unroll=min(d, DILS_UNROLL))

    sh = jax.ShapeDtypeStruct((T, B_WIDTH), F32)
    return pl.pallas_call(
        body, grid=(B_HEADS, nblk), in_specs=[sp["q"], sp["cur"], sp["v"]],
        out_specs=[sp["cur"], sp["cur"]], out_shape=[sh, sh],
        scratch_shapes=[pltpu.VMEM((d, SPAN, HEAD_DIM), BF16), pltpu.VMEM((d, SPAN, HEAD_DIM), BF16)],
        compiler_params=_cp("parallel", "arbitrary"), name=name)(qr, kr, kv)


DIL_BWD_GROUP = {1: 16, 4: 1, 16: 1}


def _dil_bwd(qr, kr, kv, dmix, lse, dd, gi, d, *, name, dep=None):
    T = qr.shape[0]
    G = DIL_BWD_GROUP[d]
    band = SPAN * d
    tb = G * band
    nblk = T // tb
    n_units = T // SPAN

    keep = G == 1

    def kernel_body(q_ref, dy_ref, lse_ref, dd_ref, kc_ref, vc_ref, *rest):
        if keep:
            dq_ref, dk_ref, dv_ref, dk_acc, dv_acc, k_before, v_before = rest
        else:
            kp_ref, vp_ref, dq_ref, dk_ref, dv_ref, dk_acc, dv_acc = rest
        n = pl.program_id(1)

        @pl.when(n == 0)
        def _():
            dk_acc[...] = jnp.zeros_like(dk_acc)
            dv_acc[...] = jnp.zeros_like(dv_acc)
            if keep:
                k_before[...] = jnp.zeros_like(k_before)
                v_before[...] = jnp.zeros_like(v_before)

        cur_ok, prev_band = _band_masks()
        for j in range(G):
            def residue(r, carry, j=j):
                off = j * band + r
                rows = pl.ds(off, SPAN, stride=d)
                q, dy = q_ref[rows, :], dy_ref[rows, :]
                lse_h = jnp.max(lse_ref[rows, :], axis=-1, keepdims=True)
                dd_h = jnp.max(dd_ref[rows, :], axis=-1, keepdims=True)
                kc, vc = kc_ref[rows, :].astype(BF16), vc_ref[rows, :].astype(BF16)
                if j > 0:
                    before = pl.ds(off - band, SPAN, stride=d)
                    kp, vp = kc_ref[before, :], vc_ref[before, :]
                    prev_ok = prev_band
                elif keep:
                    kp, vp = k_before[r], v_before[r]
                    k_before[r] = kc
                    v_before[r] = vc
                    prev_ok = prev_band & (n > 0)
                else:
                    before = pl.ds((G - 1) * band + r, SPAN, stride=d)
                    kp, vp = kp_ref[before, :], vp_ref[before, :]
                    prev_ok = prev_band & (n > 0)
                pc = jnp.exp(jnp.where(cur_ok, _dot_nt(q, kc) * ATT_SCALE, NEG) - lse_h)
                pp = jnp.exp(jnp.where(prev_ok, _dot_nt(q, kp) * ATT_SCALE, NEG) - lse_h)
                dsc = pc * (_dot_nt(dy, vc) - dd_h) * ATT_SCALE
                dsp = pp * (_dot_nt(dy, vp) - dd_h) * ATT_SCALE
                dq_ref[rows, :] = _dot(dsc, kc) + _dot(dsp, kp)
                u = (n * G + j) * d + r
                here = pl.ds(pl.multiple_of(u * SPAN, SPAN), SPAN)
                dk_acc[here, :] += _dot_tn(dsc, q)
                dv_acc[here, :] += _dot_tn(pc, dy)
                there = pl.ds(pl.multiple_of(jnp.maximum(u - d, 0) * SPAN, SPAN), SPAN)
                dk_acc[there, :] += _dot_tn(dsp, q)
                dv_acc[there, :] += _dot_tn(pp, dy)
                return carry

            lax.fori_loop(0, d, residue, 0, unroll=min(d, DILS_UNROLL))

        @pl.when(n == nblk - 1)
        def _():
            def place(u, carry):
                rows = pl.ds((u // d) * band + u % d, SPAN, stride=d)
                src = pl.ds(pl.multiple_of(u * SPAN, SPAN), SPAN)
                dk_ref[rows, :] = dk_acc[src, :]
                dv_ref[rows, :] = dv_acc[src, :]
                return carry

            lax.fori_loop(0, n_units, place, 0)

    blk = lambda f: pl.BlockSpec((tb, HEAD_DIM), f)
    cur = lambda h, n: (n, h)
    prev = lambda h, n: (jnp.maximum(n - 1, 0), h)
    whole = pl.BlockSpec((T, HEAD_DIM), lambda h, n: (0, h))
    sh = jax.ShapeDtypeStruct((T, B_WIDTH), F32)
    v_cur = blk(lambda h, n: (n, B_HEADS + h))
    if keep:
        kv_specs, kv_args = [blk(cur), v_cur], [kr, kv]
        kept = [pltpu.VMEM((d, SPAN, HEAD_DIM), BF16), pltpu.VMEM((d, SPAN, HEAD_DIM), BF16)]
    else:
        kv_specs = [blk(cur), v_cur, blk(prev), blk(lambda h, n: (jnp.maximum(n - 1, 0), B_HEADS + h))]
        kv_args, kept = [kr, kv, kr, kv], []
    body, dep_specs, dep_args = _dep(kernel_body, 4 + len(kv_args), dep)
    return pl.pallas_call(
        body, grid=(B_HEADS, nblk),
        in_specs=[blk(lambda h, n: (n, gi * B_HEADS + h)), blk(cur), blk(cur), blk(cur)] + kv_specs + dep_specs,
        out_specs=[blk(cur), whole, whole], out_shape=[sh, sh, sh],
        scratch_shapes=[pltpu.VMEM((T, HEAD_DIM), F32), pltpu.VMEM((T, HEAD_DIM), F32)] + kept,
        compiler_params=_cp("parallel", "arbitrary"), name=name)(qr, dmix, lse, dd, *kv_args, *dep_args)


def _rms_bwd_dx_pair(x, g1, w1, dy1, g2, w2, dy2, dres, *, tt, name, dep=None):
    n = len(dy1)
    widths = [p.shape[1] for p in dy1]
    T, K = x.shape

    def kernel_body(x_ref, g1_ref, w1_ref, g2_ref, w2_ref, dy2_ref, dres_ref, *rest):
        dy1_refs = rest[:n]
        dx_ref, dg1_ref, dg2_ref = rest[n:]

        @pl.when(pl.program_id(0) == 0)
        def _():
            dg1_ref[...] = jnp.zeros_like(dg1_ref)
            dg2_ref[...] = jnp.zeros_like(dg2_ref)

        dxn1, off = 0.0, 0
        for dy_ref, wd in zip(dy1_refs, widths):
            dxn1 = dxn1 + _dot(dy_ref[...], w1_ref[off:off + wd, :])
            off += wd
        dxn2 = _dot(dy2_ref[...], w2_ref[...])
        xf = x_ref[...]
        r = lax.rsqrt(jnp.mean(xf * xf, axis=-1, keepdims=True) + EPS)
        xhat = xf * r
        dg1_ref[...] += jnp.sum(dxn1 * xhat, axis=0, keepdims=True)
        dg2_ref[...] += jnp.sum(dxn2 * xhat, axis=0, keepdims=True)
        dxhat = dxn1 * g1_ref[...] + dxn2 * g2_ref[...]
        dx_ref[...] = dres_ref[...] + r * (dxhat - xhat * jnp.mean(dxhat * xhat, axis=-1, keepdims=True))

    row = pl.BlockSpec((tt, K), lambda i: (i, 0))
    body, dep_specs, dep_args = _dep(kernel_body, 7 + n, dep)
    return pl.pallas_call(
        body, grid=(T // tt,),
        in_specs=[row, _full((1, K)), _full(w1.shape), _full((1, K)), _full(w2.shape),
                  pl.BlockSpec((tt, dy2.shape[1]), lambda i: (i, 0)), row]
        + [pl.BlockSpec((tt, wd), lambda i: (i, 0)) for wd in widths] + dep_specs,
        out_specs=[row, _full((1, K)), _full((1, K))],
        out_shape=[jax.ShapeDtypeStruct((T, K), F32), jax.ShapeDtypeStruct((1, K), F32), jax.ShapeDtypeStruct((1, K), F32)],
        compiler_params=_cp("arbitrary"), name=name)(x, g1, w1, g2, w2, dy2, dres, *dy1, *dep_args)


A_MQ_COL = 4 * A_WIDTH // MEM_WIDTH
B_MQ_COL = N_GROUPS * B_WIDTH // MEM_WIDTH


def _row(v):
    return v.reshape(1, -1).astype(F32)


def _local_step(x, mem, tgt, get_w, P, put_g, first_dep=None, forward_point=lambda i, value: value):
    T = x.shape[0]
    cosf, sinsg = _rope_tables(T)
    lb_soft = jax.nn.softmax(P["a_lb_logits"].astype(F32), axis=0)
    lb = lb_soft[0:1]
    qw_heads = jnp.repeat(P["b_qnorm"][0], B_HEADS, axis=0).reshape(1, -1)
    kw_heads = jnp.tile(_row(P["b_knorm"]), (1, B_HEADS))
    mqw = [jnp.tile(_row(P["mem_qnorm"][l]), (1, MEM_HEADS)) for l in range(2)]
    mkw = [jnp.tile(_row(P["mem_knorm"][l]), (1, MEM_HEADS)) for l in range(2)]
    nmix = [_row(P["norm_mix"][l]) for l in range(2)]
    nffn = [_row(P["norm_ffn"][l]) for l in range(2)]
    mnorm = [_row(P["mem_norm"][l]) for l in range(2)]
    kvn = _row(P["kv_norm"])
    onorm = _row(P["a_onorm"])
    W = {}

    def w_of(name, after=None):
        if name not in W:
            W[name] = get_w(name, after)
        return W[name]

    proj_a, xn0 = _rms_matmul(x, nmix[0], w_of("a_w_in"), tt=512, tn=1664, wt=True, name="proj_a", dep=first_dep)
    mkv0, mn0 = _rms_matmul(mem, mnorm[0], w_of("w_mem_kv0"), tt=MEM_TOKENS, tn=2 * MEM_WIDTH, wt=False, name="mem_kv0")
    o_raw, st = _hgrn2_fwd(proj_a, lb, name="hgrn2_fwd")
    o_raw = forward_point(0, o_raw)
    mo0 = _mem_attn_fwd(proj_a, A_MQ_COL, mkv0, mqw[0], mkw[0], tt=1024, name="mem_attn_fwd0")
    hm0, mm0 = _a_post_out_proj(x, o_raw, proj_a, onorm, mo0, w_of("w_out0", mo0), tt=512, name="out_proj0")
    hm0 = forward_point(1, hm0)
    gu0, hn0 = _rms_matmul(hm0, nffn[0], w_of("w_gate_up0", hm0), tt=512, tn=1408, wt=True, out_dtype=BF16, name="gate_up0")
    h1 = _swiglu_down(hm0, gu0, w_of("w_down0", gu0), tt=512, name="down0")
    h1 = forward_point(2, h1)
    kv, hkn, kr = _rms_matmul(h1, kvn, w_of("w_kv", h1), tt=512, tn=768, wt=True, name="kv_proj",
                              rotate=(kw_heads, cosf, sinsg))

    proj_b, xn1, qr = _rms_matmul(h1, nmix[1], w_of("b_w_in", kr), tt=512, tn=1280, wt=True, name="proj_b",
                                  rotate=(qw_heads, cosf, sinsg))
    proj_b = forward_point(3, proj_b)
    mkv1, mn1 = _rms_matmul(mem, mnorm[1], w_of("w_mem_kv1", kr), tt=MEM_TOKENS, tn=2 * MEM_WIDTH, wt=False, name="mem_kv1")
    outs = [(_dil_fwd if d == 1 else _dils_fwd)(qr, kr, kv, gi, d, name=f"dil_fwd{gi}") for gi, d in enumerate(DILATIONS)]
    mo1 = _mem_attn_fwd(proj_b, B_MQ_COL, mkv1, mqw[1], mkw[1], tt=1024, name="mem_attn_fwd1")
    hm1, mm1, lse_tot = _combine_out_proj(h1, [o for o, _ in outs], [s for _, s in outs], mo1, w_of("w_out1", mo1),
                                          tt=512, name="out_proj1")
    gu1, hn1 = _rms_matmul(hm1, nffn[1], w_of("w_gate_up1", hm1), tt=512, tn=1408, wt=True, out_dtype=BF16, name="gate_up1")
    dy, sq = _swiglu_down_loss(hm1, gu1, w_of("w_down1", gu1), tgt, tt=512, name="down1_loss")

    gP = {}
    zeros_mem = jnp.zeros((MEM_TOKENS, D_MODEL), F32)

    def ffn_bwd(l, dh, hm, gu, hn):
        dgu, g_wd = _swiglu_bwd(dh, gu, w_of(f"w_down{l}"), tt=256, name=f"swiglu_bwd{l}")
        g_wgu = _mm_tn(dgu, hn, tt=512, tka=1408, name=f"g_w_gate_up{l}")
        sent = put_g({f"w_down{l}": g_wd, f"w_gate_up{l}": g_wgu})
        dhm, g_nf = _rms_bwd_dx(hm, nffn[l], w_of(f"w_gate_up{l}"), dgu, dh, tt=512, wt=True, name=f"gate_up_bwd{l}", dep=sent)
        return dhm, g_nf

    def mix_bwd(l, dhm, mix_main, mix_mem, proj, qcol, mkv, mn):
        dmix, g_wout, *head_dots = _out_proj_bwd(dhm, mix_main, mix_mem, w_of(f"w_out{l}"), tt=512, name=f"out_proj_bwd{l}",
                                                 head_dots=l == 1)
        dmq, dmkv, dqw, dkw = _mem_attn_bwd(proj, qcol, mkv, mqw[l], mkw[l], dmix, tt=1024, name=f"mem_attn_bwd{l}")
        g_wmkv = _mm_tn(mn, dmkv, tt=MEM_TOKENS, tka=512, name=f"g_w_mem_kv{l}")
        sent = put_g({f"w_out{l}": g_wout, f"w_mem_kv{l}": g_wmkv})
        _, g_mn = _rms_bwd_dx(mem, mnorm[l], w_of(f"w_mem_kv{l}"), dmkv, zeros_mem, tt=MEM_TOKENS, wt=False, name=f"mem_kv_bwd{l}")
        fold = lambda v: v.reshape(MEM_HEADS, MEM_HEAD_DIM).sum(axis=0)
        return dmix, dmq, g_mn, fold(dqw), fold(dkw), sent, head_dots

    dhm1, g_nf1 = ffn_bwd(1, dy, hm1, gu1, hn1)
    dmix1, dmq1, g_mn1, g_mq1, g_mk1, sent, (dd,) = mix_bwd(1, dhm1, mm1, mo1, proj_b, B_MQ_COL, mkv1, mn1)
    dqs, dks, dvs = [], [], []
    for gi, d in enumerate(DILATIONS):
        dq_g, dk_g, dv_g = _dil_bwd(qr, kr, kv, dmix1, lse_tot, dd, gi, d, name=f"dil_bwd{gi}", dep=sent if gi == 0 else None)
        dqs.append(dq_g)
        dks.append(dk_g)
        dvs.append(dv_g)
    dq_raw, dqw = _q_prep_bwd(proj_b, qw_heads, cosf, sinsg, dqs, tt=512, name="q_prep_bwd")
    dkv, dkw = _kv_prep_bwd(kv, kw_heads, cosf, sinsg, dks, dvs, tt=512, name="kv_prep_bwd")
    dproj_b = [dq_raw, dmq1]
    g_wb = _mm_tn_pieces(dproj_b, xn1, tt=512, name="g_b_w_in")
    g_wkv = _mm_tn(dkv, hkn, tt=512, tka=768, name="g_w_kv")
    sent = put_g({"b_w_in": g_wb, "w_kv": g_wkv})
    dh1, g_nm1, g_kvn = _rms_bwd_dx_pair(h1, nmix[1], w_of("b_w_in"), dproj_b, kvn, w_of("w_kv"), dkv, dhm1,
                                         tt=512, name="proj_b_kv_bwd", dep=sent)

    dhm0, g_nf0 = ffn_bwd(0, dh1, hm0, gu0, hn0)
    dmix0, dmq0, g_mn0, g_mq0, g_mk0, sent, _ = mix_bwd(0, dhm0, mm0, mo0, proj_a, A_MQ_COL, mkv0, mn0)
    do_raw, dg, g_onorm = _a_post_bwd(o_raw, proj_a, onorm, dmix0, tt=512, name="a_post_bwd", dep=sent)
    dq, dz, dv, dlb = _hgrn2_bwd(proj_a, lb, st, do_raw, name="hgrn2_bwd")
    dproj_a = [dq, dz, dv, dg, dmq0]
    sent = put_g({"a_w_in": _mm_tn_pieces(dproj_a, xn0, tt=512, name="g_a_w_in")})
    gx, g_nm0 = _rms_bwd_dx(x, nmix[0], w_of("a_w_in"), dproj_a, dhm0, tt=512, wt=True, name="proj_a_bwd", dep=sent)

    dl0 = lb_soft[0:1] * lb_soft[1:2] * dlb
    gP["a_lb_logits"] = jnp.concatenate([dl0, -dl0], axis=0)
    gP["a_onorm"] = g_onorm
    gP["norm_mix"] = jnp.concatenate([g_nm0, g_nm1], axis=0)
    gP["norm_ffn"] = jnp.concatenate([g_nf0, g_nf1], axis=0)
    gP["b_qnorm"] = dqw.reshape(N_GROUPS, B_HEADS, HEAD_DIM).sum(axis=1)[None]
    gP["kv_norm"] = g_kvn.reshape(-1)
    gP["b_knorm"] = dkw.reshape(B_HEADS, HEAD_DIM).sum(axis=0)
    gP["mem_norm"] = jnp.concatenate([g_mn0, g_mn1], axis=0)
    gP["mem_qnorm"] = jnp.stack([g_mq0, g_mq1])
    gP["mem_knorm"] = jnp.stack([g_mk0, g_mk1])
    return sq, gx, gP


MESH_ID = pl.DeviceIdType.MESH
HBM_SPEC = pl.BlockSpec(memory_space=pltpu.HBM)


def _position():
    return lax.axis_index("x"), lax.axis_index("y"), lax.axis_index("c")


def _all_gather_direct(block, after, *, name):
    def body(x_ref, after_ref, out_ref, send_sems, recv_sems, local_sem):
        x, y, c = _position()
        me = 4 * x + 2 * y + c
        mine = pltpu.make_async_copy(x_ref, out_ref.at[me], local_sem)
        mine.start()
        copies = []
        for k in ALL_PEERS:
            cp = pltpu.make_async_remote_copy(
                src_ref=x_ref, dst_ref=out_ref.at[me], send_sem=send_sems.at[k - 1], recv_sem=recv_sems.at[k - 1],
                device_id=_peer(k, x, y, c), device_id_type=MESH_ID)
            cp.start()
            copies.append(cp)
        for cp in copies:
            cp.wait()
        mine.wait()

    return pl.pallas_call(
        body, out_shape=jax.ShapeDtypeStruct((N_DEV,) + block.shape, block.dtype),
        in_specs=[HBM_SPEC, pl.BlockSpec(memory_space=pl.ANY)], out_specs=HBM_SPEC,
        scratch_shapes=[pltpu.SemaphoreType.DMA((7,)), pltpu.SemaphoreType.DMA((7,)), pltpu.SemaphoreType.DMA],
        name=name)(block, after)


SEM_SPEC = pl.BlockSpec(memory_space=pltpu.SEMAPHORE)
ANY_SPEC = pl.BlockSpec(memory_space=pl.ANY)
DATAFLOW = pltpu.SideEffectType.DATAFLOW_SIDE_EFFECTING


def _peer(k, x, y, c):
    return (1 - x if (k >> 2) & 1 else x, 1 - y if (k >> 1) & 1 else y, 1 - c if k & 1 else c)


def _own_slot_filled(own_block):
    x, y, c = _position()
    zone = lax.empty((N_DEV,) + own_block.shape, own_block.dtype)
    return lax.dynamic_update_slice_in_dim(zone, own_block[None], 4 * x + 2 * y + c, axis=0)


ALL_PEERS = tuple(range(1, N_DEV))
SIBLING_AND_SAME_CORE = (1, 2, 4, 6)
SAME_CORE = (2, 4, 6)


def _split_start(srcs, scatter, after, *, name, relations=ALL_PEERS, carried=None):
    n = len(srcs)
    extra = ([] if after is None else [after]) + ([] if carried is None else [carried])
    n_carried = 0 if carried is None else 1
    x, y, c = _position()
    me = 4 * x + 2 * y + c
    lands = [_own_slot_filled(lax.dynamic_index_in_dim(s, me, 0, keepdims=False) if scatter else s) for s in srcs]

    def body(*refs):
        src_refs, land_refs = refs[:n], refs[n:2 * n]
        send_sems, recv_sems = refs[2 * n + len(extra)], refs[2 * n + len(extra) + 1]
        token = refs[2 * n + len(extra) + 2 + 2 * n]
        bx, by, bc = _position()
        bme = 4 * bx + 2 * by + bc
        for a in range(n):
            for k in relations:
                tx, ty, tc = _peer(k, bx, by, bc)
                src = src_refs[a].at[4 * tx + 2 * ty + tc] if scatter else src_refs[a]
                pltpu.make_async_remote_copy(
                    src_ref=src, dst_ref=land_refs[a].at[bme],
                    send_sem=send_sems.at[7 * a + k - 1], recv_sem=recv_sems.at[7 * a + k - 1],
                    device_id=(tx, ty, tc), device_id_type=MESH_ID).start()
        token[...] = jnp.zeros_like(token)

    hbm = lambda a: pltpu.HBM(a.shape, a.dtype)
    outs = pl.pallas_call(
        body, name=name,
        out_shape=(pltpu.SemaphoreType.DMA((7 * n,)), pltpu.SemaphoreType.DMA((7 * n,)),
                   *[hbm(s) for s in srcs], *[hbm(l) for l in lands], jax.ShapeDtypeStruct((8, 128), F32),
                   *([hbm(carried)] if n_carried else [])),
        in_specs=[HBM_SPEC] * (2 * n) + [ANY_SPEC] * len(extra),
        out_specs=(SEM_SPEC, SEM_SPEC, *[HBM_SPEC] * (2 * n), pl.BlockSpec(memory_space=pltpu.VMEM), *([ANY_SPEC] * n_carried)),
        input_output_aliases={**{i: 2 + i for i in range(2 * n)},
                              **({2 * n + len(extra) - 1: 2 * n + 3} if n_carried else {})},
        compiler_params=pltpu.CompilerParams(has_side_effects=DATAFLOW),
    )(*[pltpu.with_memory_space_constraint(s, pltpu.HBM) for s in srcs],
      *[pltpu.with_memory_space_constraint(l, pltpu.HBM) for l in lands], *extra)
    return {"n": n, "relations": relations, "send": outs[0], "recv": outs[1], "srcs": list(outs[2:2 + n]),
            "lands": list(outs[2 + n:2 + 2 * n]), "token": outs[2 * n + 2], "carried": outs[-1] if n_carried else None}


def _forward_start(lands, carried, *, name):
    n = len(lands)

    def body(*refs):
        land_refs = refs[:n]
        send_sems, recv_sems = refs[n + 1], refs[n + 2]
        bx, by, bc = _position()
        for a in range(n):
            for k in SAME_CORE:
                tx, ty, tc = _peer(k, bx, by, bc)
                block = land_refs[a].at[4 * tx + 2 * ty + tc]
                pltpu.make_async_remote_copy(
                    src_ref=block, dst_ref=block,
                    send_sem=send_sems.at[7 * a + k - 1], recv_sem=recv_sems.at[7 * a + k - 1],
                    device_id=(bx, by, 1 - bc), device_id_type=MESH_ID).start()

    hbm = lambda a: pltpu.HBM(a.shape, a.dtype)
    outs = pl.pallas_call(
        body, name=name,
        out_shape=(pltpu.SemaphoreType.DMA((7 * n,)), pltpu.SemaphoreType.DMA((7 * n,)),
                   *[hbm(l) for l in lands], hbm(carried)),
        in_specs=[HBM_SPEC] * n + [ANY_SPEC],
        out_specs=(SEM_SPEC, SEM_SPEC, *[HBM_SPEC] * n, ANY_SPEC),
        input_output_aliases={i: 2 + i for i in range(n + 1)},
        compiler_params=pltpu.CompilerParams(has_side_effects=DATAFLOW),
    )(*lands, carried)
    handle = {"n": n, "relations": SAME_CORE, "send": outs[0], "recv": outs[1], "srcs": [], "lands": list(outs[2:2 + n])}
    return handle, outs[-1]


def _split_wait(handle, after, *, name):
    n, ns = handle["n"], len(handle["srcs"])

    def body(*refs):
        land_refs = refs[ns:ns + n]
        send_sems, recv_sems = refs[ns + n], refs[ns + n + 1]
        bx, by, bc = _position()
        for a in range(n):
            for k in handle["relations"]:
                block = land_refs[a].at[0]
                cp = pltpu.make_async_remote_copy(
                    src_ref=block, dst_ref=block,
                    send_sem=send_sems.at[7 * a + k - 1], recv_sem=recv_sems.at[7 * a + k - 1],
                    device_id=_peer(k, bx, by, bc), device_id_type=MESH_ID)
                cp.wait_send()
                cp.wait_recv()

    hbm = lambda a: pltpu.HBM(a.shape, a.dtype)
    outs = pl.pallas_call(
        body, name=name,
        out_shape=(*[hbm(s) for s in handle["srcs"]], *[hbm(l) for l in handle["lands"]]),
        in_specs=[HBM_SPEC] * (ns + n) + [SEM_SPEC, SEM_SPEC, ANY_SPEC],
        out_specs=tuple([HBM_SPEC] * (ns + n)),
        input_output_aliases={i: i for i in range(ns + n)},
        compiler_params=pltpu.CompilerParams(has_side_effects=DATAFLOW),
    )(*handle["srcs"], *handle["lands"], handle["send"], handle["recv"], after)
    return list(outs[ns:])


def _sum_sources(parts, *, tr, name):
    n, R, C = parts.shape

    def body(p_ref, o_ref):
        acc = p_ref[0].astype(F32)
        for s in range(1, n):
            acc = acc + p_ref[s].astype(F32)
        o_ref[...] = acc

    return pl.pallas_call(
        body, grid=(R // tr,), in_specs=[pl.BlockSpec((n, tr, C), lambda i: (0, i, 0))],
        out_specs=pl.BlockSpec((tr, C), lambda i: (i, 0)),
        out_shape=jax.ShapeDtypeStruct((R, C), F32), compiler_params=_cp("parallel"), name=name)(parts)


def _adamw_math(g, w, m, v):
    c1 = 1.0 - ADAM_B1 ** ADAM_STEP
    c2 = 1.0 - ADAM_B2 ** ADAM_STEP
    nm = ADAM_B1 * m + (1.0 - ADAM_B1) * g
    nv = ADAM_B2 * v + (1.0 - ADAM_B2) * (g * g)
    return -ADAM_LR * ((nm / c1) / (jnp.sqrt(nv / c2) + ADAM_EPS) + ADAM_WD * w), nm, nv


ADAMW_STRIP = 16


def _reduce_adamw(received, w, m, v, *, tr, name):
    L, R, C = w.shape

    def body(*refs):
        p_refs = refs[:L]
        w_ref, m_ref, v_ref, g_ref, d_ref, nm_ref, nv_ref = refs[L:]
        for l in range(L):
            @pl.when(pl.program_id(0) == l)
            def _(p_ref=p_refs[l]):
                def strip(i, carry):
                    rows = pl.ds(pl.multiple_of(i * ADAMW_STRIP, ADAMW_STRIP), ADAMW_STRIP)
                    acc = p_ref[0, rows, :].astype(F32)
                    for s in range(1, N_DEV):
                        acc = acc + p_ref[s, rows, :].astype(F32)
                    g_ref[rows, :] = acc
                    d_ref[rows, :], nm_ref[rows, :], nv_ref[rows, :] = _adamw_math(acc, w_ref[rows, :], m_ref[rows, :], v_ref[rows, :])
                    return carry

                lax.fori_loop(0, tr // ADAMW_STRIP, strip, 0)

    p_spec = pl.BlockSpec((N_DEV, tr, C), lambda l, i: (0, i, 0))
    blk = pl.BlockSpec((None, tr, C), lambda l, i: (l, i, 0))
    sh = jax.ShapeDtypeStruct((L, R, C), F32)
    return pl.pallas_call(
        body, grid=(L, R // tr), in_specs=[p_spec] * L + [blk] * 3, out_specs=[blk] * 4, out_shape=[sh] * 4,
        compiler_params=_cp("parallel", "parallel"), name=name)(*received, w, m, v)


def _adamw(g, w, m, v, *, tr, name):
    L, R, C = w.shape

    def body(g_ref, w_ref, m_ref, v_ref, d_ref, nm_ref, nv_ref):
        d_ref[...], nm_ref[...], nv_ref[...] = _adamw_math(g_ref[...], w_ref[...], m_ref[...], v_ref[...])

    blk = pl.BlockSpec((None, tr, C), lambda l, i: (l, i, 0))
    sh = jax.ShapeDtypeStruct((L, R, C), F32)
    return pl.pallas_call(
        body, grid=(L, R // tr), in_specs=[blk] * 4, out_specs=[blk] * 3, out_shape=[sh] * 3,
        compiler_params=_cp("parallel", "parallel"), name=name)(g, w, m, v)


UNITS = {
    "a_w_in": ("a_w_in", 0, True), "w_mem_kv0": ("w_mem_kv", 0, False), "w_out0": ("w_out", 0, False),
    "w_gate_up0": ("w_gate_up", 0, True), "w_down0": ("w_down", 0, False), "w_kv": ("w_kv", None, True),
    "b_w_in": ("b_w_in", 0, True), "w_mem_kv1": ("w_mem_kv", 1, False), "w_out1": ("w_out", 1, False),
    "w_gate_up1": ("w_gate_up", 1, True), "w_down1": ("w_down", 1, False),
}
BIG = ("a_w_in", "b_w_in", "w_kv", "w_mem_kv", "w_out", "w_gate_up", "w_down")
ADAMW_ROW_TILE = {"a_w_in": 208, "b_w_in": 160, "w_kv": 192, "w_mem_kv": 128, "w_out": 128, "w_gate_up": 352, "w_down": 352}


def _wire_block(weights, unit):
    name, layer, col = UNITS[unit]
    a = weights[name] if layer is None else weights[name][layer]
    return (a.T if col else a).astype(BF16)


SMALL_REPLICATED = ("norm_mix", "norm_ffn", "b_qnorm", "kv_norm", "b_knorm", "mem_norm", "mem_qnorm", "mem_knorm")
SMALL_SHARDED = ("a_lb_logits", "a_onorm")
SMALL_ORDER = SMALL_REPLICATED + SMALL_SHARDED
LANES = 128


def _prod(shape):
    n = 1
    for s in shape:
        n *= s
    return n


def _pack_flat(arrays, rows, cols, dtype):
    flat = jnp.concatenate([a.reshape(-1).astype(dtype) for a in arrays])
    return jnp.pad(flat, (0, rows * cols - flat.shape[0])).reshape(rows, cols)


def _unpack_flat(packed, shapes):
    flat = packed.reshape(-1)
    out, off = [], 0
    for s in shapes:
        out.append(flat[off:off + _prod(s)].reshape(s))
        off += _prod(s)
    return out


def kernel(x, mem, norm_mix, norm_ffn, a_w_in, a_lb_logits, a_onorm, b_w_in, b_qnorm, kv_norm, w_kv, b_knorm, mem_norm, w_mem_kv, mem_qnorm, mem_knorm, w_out, w_gate_up, w_down, loss_target, m_norm_mix, m_norm_ffn, m_a_w_in, m_a_lb_logits, m_a_onorm, m_b_w_in, m_b_qnorm, m_kv_norm, m_w_kv, m_b_knorm, m_mem_norm, m_w_mem_kv, m_mem_qnorm, m_mem_knorm, m_w_out, m_w_gate_up, m_w_down, v_norm_mix, v_norm_ffn, v_a_w_in, v_a_lb_logits, v_a_onorm, v_b_w_in, v_b_qnorm, v_kv_norm, v_w_kv, v_b_knorm, v_mem_norm, v_w_mem_kv, v_mem_qnorm, v_mem_knorm, v_w_out, v_w_gate_up, v_w_down):
    names = ("norm_mix", "norm_ffn", "a_w_in", "a_lb_logits", "a_onorm", "b_w_in", "b_qnorm", "kv_norm", "w_kv", "b_knorm",
             "mem_norm", "w_mem_kv", "mem_qnorm", "mem_knorm", "w_out", "w_gate_up", "w_down")
    w = dict(zip(names, (norm_mix, norm_ffn, a_w_in, a_lb_logits, a_onorm, b_w_in, b_qnorm, kv_norm, w_kv, b_knorm,
                         mem_norm, w_mem_kv, mem_qnorm, mem_knorm, w_out, w_gate_up, w_down)))
    m = dict(zip(names, (m_norm_mix, m_norm_ffn, m_a_w_in, m_a_lb_logits, m_a_onorm, m_b_w_in, m_b_qnorm, m_kv_norm, m_w_kv,
                         m_b_knorm, m_mem_norm, m_w_mem_kv, m_mem_qnorm, m_mem_knorm, m_w_out, m_w_gate_up, m_w_down)))
    v = dict(zip(names, (v_norm_mix, v_norm_ffn, v_a_w_in, v_a_lb_logits, v_a_onorm, v_b_w_in, v_b_qnorm, v_kv_norm, v_w_kv,
                         v_b_knorm, v_mem_norm, v_w_mem_kv, v_mem_qnorm, v_mem_knorm, v_w_out, v_w_gate_up, v_w_down)))

    first = ["a_w_in", "w_mem_kv0"]
    later = [["w_out0", "w_gate_up0"], ["w_down0", "w_kv"], ["b_w_in", "w_mem_kv1"], ["w_out1", "w_gate_up1", "w_down1"]]
    first_half, second_half = {}, {}

    def start_first_half(i, after, carried=None):
        first_half[i] = _split_start([_wire_block(w, u) for u in later[i]], False, after, name=f"gather{i}_start",
                                     relations=SIBLING_AND_SAME_CORE, carried=carried)
        return first_half[i]

    opening = _split_start([_wire_block(w, u) for u in first] + [_pack_flat([a_lb_logits, a_onorm], 8, LANES, F32)],
                           False, None, name="gather_first_start", relations=SIBLING_AND_SAME_CORE)
    token = start_first_half(0, opening["token"])["token"]
    token = start_first_half(1, token)["token"]
    opening, token = _forward_start(_split_wait(opening, token, name="gather_first_landed"), token, name="gather_first_forward")
    gathered = _split_wait(opening, token, name="gather_first_wait")
    full = {u: g.reshape(-1, g.shape[-1]) for u, g in zip(first, gathered)}
    small_in = gathered[-1].reshape(N_DEV, -1)
    P = {n: w[n] for n in SMALL_REPLICATED}
    P["a_lb_logits"] = small_in[:, :192].reshape(N_DEV, 2, 96).transpose(1, 0, 2).reshape(2, A_WIDTH)
    P["a_onorm"] = small_in[:, 192:288].reshape(1, A_WIDTH)

    def forward_point(i, value):
        landed = _split_wait(first_half[i], value, name=f"gather{i}_landed")
        second_half[i], value = _forward_start(landed, value, name=f"gather{i}_forward")
        if i + 2 < len(later):
            value = start_first_half(i + 2, None, carried=value)["carried"]
        return value

    def get_w(unit, after):
        if unit not in full:
            i = [unit in group for group in later].index(True)
            for u, land in zip(later[i], _split_wait(second_half[i], after, name=f"gather{i}_wait")):
                full[u] = land.reshape(-1, land.shape[-1])
        return full[unit]

    sent = []

    def put_g(group):
        units = list(group)
        handle = _split_start([group[u].reshape(N_DEV, -1, group[u].shape[-1]) for u in units], True, None,
                              name=f"scatter{len(sent)}_start")
        sent.append((units, handle))
        return handle["token"]

    sq, gx, gP = _local_step(x[0], mem[0], loss_target[0], get_w, P, put_g, forward_point=forward_point)
    loss_here = (0.5 * jnp.sum(sq) / D_MODEL).reshape(1)

    received = {}
    group_of = {u: i for i, (units, _) in enumerate(sent) for u in units}
    out = {"grad": {}, "delta": {}, "new_m": {}, "new_v": {}}
    newest = [gx]

    def update_big(n):
        shape = w[n].shape
        as3 = lambda a: a.reshape((-1,) + shape[-2:])
        mine = [u for u, (wn, _, _) in UNITS.items() if wn == n]
        for i in sorted({group_of[u] for u in mine}):
            if sent[i][0][0] not in received:
                received.update(zip(sent[i][0], _split_wait(sent[i][1], newest[0], name=f"scatter{i}_wait")))
        flip = (lambda a: jnp.swapaxes(a, 1, 2)) if UNITS[mine[0]][2] else (lambda a: a)
        res = _reduce_adamw([received[u] for u in mine], flip(as3(w[n])), flip(as3(m[n])), flip(as3(v[n])),
                            tr=ADAMW_ROW_TILE[n], name=f"adamw_{n}")
        newest[0] = res[1]
        for kind, r in zip(("grad", "delta", "new_m", "new_v"), res):
            out[kind][n] = flip(r).reshape(shape)

    for n in ("w_down", "w_gate_up", "w_out", "w_mem_kv", "b_w_in", "w_kv"):
        update_big(n)

    full_shapes = [(2, A_WIDTH) if n == "a_lb_logits" else (1, A_WIDTH) if n == "a_onorm" else w[n].shape for n in SMALL_ORDER]
    n_small = sum(_prod(s) for s in full_shapes) + 1
    rows_small = -(-n_small // (8 * LANES)) * 8
    g_all = _all_gather_direct(_pack_flat([gP[n] for n in SMALL_ORDER] + [loss_here], rows_small, LANES, F32),
                               newest[0], name="gather_small_grads")
    summed = _unpack_flat(_sum_sources(g_all, tr=rows_small, name="sum_small_grads"), full_shapes + [(1,)])
    g_small = dict(zip(SMALL_ORDER, summed))
    loss = summed[-1].reshape(())
    me = 4 * lax.axis_index("x") + 2 * lax.axis_index("y") + lax.axis_index("c")
    for n in SMALL_SHARDED:
        g_small[n] = lax.dynamic_slice_in_dim(g_small[n], me * 96, 96, axis=1)
    shapes = [w[n].shape for n in SMALL_ORDER]
    rows_upd = -(-sum(_prod(s) for s in shapes) // (8 * LANES)) * 8
    pk = lambda d: _pack_flat([d[n] for n in SMALL_ORDER], rows_upd, LANES, F32)
    res = _adamw(pk(g_small)[None], pk(w)[None], pk(m)[None], pk(v)[None], tr=rows_upd, name="adamw_small")
    out["grad"].update(g_small)
    for kind, packed in zip(("delta", "new_m", "new_v"), res):
        out[kind].update(zip(SMALL_ORDER, _unpack_flat(packed[0], shapes)))
    newest[0] = res[0]
    update_big("a_w_in")

    return (loss, gx[None], *[out["grad"][n] for n in names], *[out["delta"][n] for n in names],
            *[out["new_m"][n] for n in names], *[out["new_v"][n] for n in names])
```
